```python
import jax, jax.numpy as jnp
from jax import lax
import numpy as np

D_MODEL = 1024
BATCH = 8
SEQ = 2048
DEPTH = 1

HEAD_DIM = 64
N_FOX_HEADS = 8
DIL_GROUPS = ((128, 1), (512, 4), (2048, 16))
N_DIL_HEADS_PER_GROUP = 4
N_DIL_HEADS = N_DIL_HEADS_PER_GROUP * len(DIL_GROUPS)
FOX_W = N_FOX_HEADS * HEAD_DIM
DIL_W = N_DIL_HEADS * HEAD_DIM
DIL_OUT_W = N_DIL_HEADS_PER_GROUP * HEAD_DIM
ROT_DIM = HEAD_DIM // 4
ROPE_THETA = 500000.0
D_FF = -(-8 * D_MODEL // (3 * 256)) * 256
Q_BLOCK = 128
EPS = 1e-6
NEG = -1e30
SPLIT_SIZES = (FOX_W, FOX_W, FOX_W, N_FOX_HEADS, DIL_W, DIL_W, DIL_W, D_MODEL, D_MODEL)
IN_COLS = sum(SPLIT_SIZES)

kernel_name = "hybrid_fox_dilated_adaln_block"


def rmsnorm(x, g):
    xf = x.astype(jnp.float32)
    y = xf * lax.rsqrt(jnp.mean(xf * xf, axis=-1, keepdims=True) + EPS)
    return (y * g.astype(jnp.float32)).astype(x.dtype)


def modulate(h, shift, scale):
    return h * (1 + scale[:, None, :]) + shift[:, None, :]


def partial_rope(t):
    S = t.shape[1]
    pos = jnp.arange(S, dtype=jnp.float32)
    inv_freq = ROPE_THETA ** (-jnp.arange(0, ROT_DIM, 2, dtype=jnp.float32) / ROT_DIM)
    ang = pos[:, None] * inv_freq[None, :]
    cos = jnp.cos(ang)[None, :, None, :]
    sin = jnp.sin(ang)[None, :, None, :]
    tf = t.astype(jnp.float32)
    x1 = tf[..., : ROT_DIM // 2]
    x2 = tf[..., ROT_DIM // 2: ROT_DIM]
    rot = jnp.concatenate([x1 * cos - x2 * sin, x2 * cos + x1 * sin], axis=-1)
    return jnp.concatenate([rot, tf[..., ROT_DIM:]], axis=-1).astype(t.dtype)


def forgetting_attention(q, k, v, f_logit):
    B, S, H, Dh = q.shape
    scale = Dh ** -0.5
    F = jnp.cumsum(jax.nn.log_sigmoid(f_logit.astype(jnp.float32)), axis=1)
    Ft = jnp.transpose(F, (0, 2, 1))
    outs = []
    for blk in range(S // Q_BLOCK):
        q0, q1 = blk * Q_BLOCK, (blk + 1) * Q_BLOCK
        logits = jnp.einsum('bqhd,bkhd->bhqk', q[:, q0:q1], k[:, :q1],
                            preferred_element_type=jnp.float32) * scale
        logits = logits + (Ft[:, :, q0:q1, None] - Ft[:, :, None, :q1])
        causal = jnp.arange(q0, q1)[:, None] >= jnp.arange(q1)[None, :]
        p = jax.nn.softmax(jnp.where(causal[None, None], logits, NEG), axis=-1)
        outs.append(jnp.einsum('bhqk,bkhd->bqhd', p.astype(v.dtype), v[:, :q1]))
    return jnp.concatenate(outs, axis=1)


def dilated_window_attention(q, k, v, dilation, span):
    B, S, H, Dh = q.shape
    L = S // dilation
    nb = -(-L // span)
    Lp = nb * span
    Z = B * dilation
    scale = Dh ** -0.5

    def to_sub(t):
        t = t.reshape(B, L, dilation, H, Dh).transpose(0, 2, 1, 3, 4).reshape(Z, L, H, Dh)
        t = jnp.pad(t, ((0, 0), (0, Lp - L), (0, 0), (0, 0)))
        return t.reshape(Z, nb, span, H, Dh)

    qb, kb, vb = to_sub(q), to_sub(k), to_sub(v)

    def band(t):
        prev = jnp.pad(t, ((0, 0), (1, 0), (0, 0), (0, 0), (0, 0)))[:, :-1]
        return jnp.concatenate([prev, t], axis=2)

    kband, vband = band(kb), band(vb)
    logits = jnp.einsum('znqhd,znkhd->znhqk', qb, kband,
                        preferred_element_type=jnp.float32) * scale
    qi = jnp.arange(span)[:, None] + span
    kj = jnp.arange(2 * span)[None, :]
    dist = qi - kj
    in_band = (dist >= 0) & (dist <= span)
    has_prev = (jnp.arange(nb)[:, None, None] > 0) | (kj >= span)[None]
    valid = in_band[None] & has_prev
    logits = jnp.where(valid[None, :, None], logits, NEG)
    m = jnp.max(logits, axis=-1, keepdims=True)
    p = jnp.exp(logits - m)
    s = jnp.sum(p, axis=-1)
    o = jnp.einsum('znhqk,znkhd->znqhd', p.astype(v.dtype), vband).astype(jnp.float32)
    o = o / jnp.transpose(s, (0, 1, 3, 2))[..., None]
    lse = jnp.transpose(m[..., 0] + jnp.log(s), (0, 1, 3, 2))

    def from_sub(t):
        rest = t.shape[3:]
        t = t.reshape((Z, Lp) + rest)[:, :L]
        t = t.reshape((B, dilation, L) + rest)
        t = jnp.swapaxes(t, 1, 2)
        return t.reshape((B, S) + rest)

    return from_sub(o), from_sub(lse)


def hybrid_mixer(h, w_in, b_fgate, w_br_a, w_br_b, w_out):
    B, S, _ = h.shape
    proj = jnp.einsum('bsd,de->bse', h, w_in)
    splits = [int(i) for i in np.cumsum(SPLIT_SIZES)[:-1]]
    qa, ka, va, fa, qb, kb, vb, ga, gb = jnp.split(proj, splits, axis=-1)

    qa = qa.reshape(B, S, N_FOX_HEADS, HEAD_DIM)
    ka = ka.reshape(B, S, N_FOX_HEADS, HEAD_DIM)
    va = va.reshape(B, S, N_FOX_HEADS, HEAD_DIM)
    ya = forgetting_attention(qa, ka, va, fa + b_fgate)
    ya = jnp.einsum('bse,ed->bsd', ya.reshape(B, S, FOX_W), w_br_a)

    qb = partial_rope(qb.reshape(B, S, N_DIL_HEADS, HEAD_DIM))
    kb = partial_rope(kb.reshape(B, S, N_DIL_HEADS, HEAD_DIM))
    vb = vb.reshape(B, S, N_DIL_HEADS, HEAD_DIM)
    outs, lses = [], []
    for g, (window, dilation) in enumerate(DIL_GROUPS):
        sl = slice(g * N_DIL_HEADS_PER_GROUP, (g + 1) * N_DIL_HEADS_PER_GROUP)
        o, lse = dilated_window_attention(qb[:, :, sl], kb[:, :, sl], vb[:, :, sl],
                                          dilation, window // dilation)
        outs.append(o)
        lses.append(lse)
    alpha = jax.nn.softmax(jnp.stack(lses, axis=0), axis=0)
    yb = jnp.sum(alpha[..., None] * jnp.stack(outs, axis=0), axis=0).astype(h.dtype)
    yb = jnp.einsum('bse,ed->bsd', yb.reshape(B, S, DIL_OUT_W), w_br_b)

    merged = jax.nn.sigmoid(ga) * ya + jax.nn.sigmoid(gb) * yb
    return jnp.einsum('bsd,de->bse', merged, w_out)


def swiglu(h, w_gate, w_up, w_down):
    a = jnp.einsum('bsd,df->bsf', h, w_gate)
    u = jnp.einsum('bsd,df->bsf', h, w_up)
    return jnp.einsum('bsf,fd->bsd', jax.nn.silu(a) * u, w_down)


def _fwd_setup_inputs(seed: int = 0) -> dict:
    key = jax.random.key(seed)
    ks = jax.random.split(key, 16)
    f32 = jnp.float32
    L, D = DEPTH, D_MODEL
    nrm = lambda k, shape, fan_in, s=1.0: (jax.random.normal(k, shape, f32) * (s * fan_in ** -0.5))
    return {
        "x": jax.random.normal(ks[0], (BATCH, SEQ, D), f32),
        "c": jax.random.normal(ks[1], (BATCH, D), f32),
        "w_ada": nrm(ks[2], (L, D, 6 * D), D, 0.5),
        "b_ada": 0.1 * jax.random.normal(ks[3], (L, 6 * D), f32),
        "g_mix": 1.0 + 0.02 * jax.random.normal(ks[4], (L, D), f32),
        "w_in": nrm(ks[5], (L, D, IN_COLS), D),
        "b_fgate": jax.random.uniform(ks[6], (L, N_FOX_HEADS), f32, 1.0, 4.0),
        "w_br_a": nrm(ks[7], (L, FOX_W, D), FOX_W),
        "w_br_b": nrm(ks[8], (L, DIL_OUT_W, D), DIL_OUT_W),
        "w_out": nrm(ks[9], (L, D, D), D),
        "g_ffn": 1.0 + 0.02 * jax.random.normal(ks[10], (L, D), f32),
        "w_ffn_gate": nrm(ks[11], (L, D, D_FF), D),
        "w_ffn_up": nrm(ks[12], (L, D, D_FF), D),
        "w_ffn_down": nrm(ks[13], (L, D_FF, D), D_FF),
        "g_final": 1.0 + 0.02 * jax.random.normal(ks[14], (D,), f32),
    }


def _fwd_reference(x, c, w_ada, b_ada, g_mix, w_in, b_fgate, w_br_a, w_br_b, w_out,
              g_ffn, w_ffn_gate, w_ffn_up, w_ffn_down, g_final):
    for l in range(DEPTH):
        mod = jnp.einsum('bd,de->be', jax.nn.silu(c), w_ada[l]) + b_ada[l]
        sh_m, sc_m, ga_m, sh_f, sc_f, ga_f = jnp.split(mod, 6, axis=-1)
        h = modulate(rmsnorm(x, g_mix[l]), sh_m, sc_m)
        x = x + ga_m[:, None, :] * hybrid_mixer(h, w_in[l], b_fgate[l], w_br_a[l], w_br_b[l], w_out[l])
        h = modulate(rmsnorm(x, g_ffn[l]), sh_f, sc_f)
        x = x + ga_f[:, None, :] * swiglu(h, w_ffn_gate[l], w_ffn_up[l], w_ffn_down[l])
    return rmsnorm(x, g_final)


import jax as _jax
import jax.numpy as _jnp

TWIN_FORMAT = 'train_step'
FWD_PARAMS = ['x', 'c', 'w_ada', 'b_ada', 'g_mix', 'w_in', 'b_fgate', 'w_br_a', 'w_br_b', 'w_out', 'g_ffn', 'w_ffn_gate', 'w_ffn_up', 'w_ffn_down', 'g_final']
TWIN_WEIGHTS = ['w_ada', 'b_ada', 'g_mix', 'w_in', 'b_fgate', 'w_br_a', 'w_br_b', 'w_out', 'g_ffn', 'w_ffn_gate', 'w_ffn_up', 'w_ffn_down', 'g_final']
TWIN_DIFF_INPUT = 'x'
TWIN_INPUTS = ['x', 'c', 'w_ada', 'b_ada', 'g_mix', 'w_in', 'b_fgate', 'w_br_a', 'w_br_b', 'w_out', 'g_ffn', 'w_ffn_gate', 'w_ffn_up', 'w_ffn_down', 'g_final', 'loss_target', 'm_w_ada', 'm_b_ada', 'm_g_mix', 'm_w_in', 'm_b_fgate', 'm_w_br_a', 'm_w_br_b', 'm_w_out', 'm_g_ffn', 'm_w_ffn_gate', 'm_w_ffn_up', 'm_w_ffn_down', 'm_g_final', 'v_w_ada', 'v_b_ada', 'v_g_mix', 'v_w_in', 'v_b_fgate', 'v_w_br_a', 'v_w_br_b', 'v_w_out', 'v_g_ffn', 'v_w_ffn_gate', 'v_w_ffn_up', 'v_w_ffn_down', 'v_g_final']
TWIN_OUTPUTS = ['loss', 'grad_x', 'grad_w_ada', 'grad_b_ada', 'grad_g_mix', 'grad_w_in', 'grad_b_fgate', 'grad_w_br_a', 'grad_w_br_b', 'grad_w_out', 'grad_g_ffn', 'grad_w_ffn_gate', 'grad_w_ffn_up', 'grad_w_ffn_down', 'grad_g_final', 'delta_w_ada', 'delta_b_ada', 'delta_g_mix', 'delta_w_in', 'delta_b_fgate', 'delta_w_br_a', 'delta_w_br_b', 'delta_w_out', 'delta_g_ffn', 'delta_w_ffn_gate', 'delta_w_ffn_up', 'delta_w_ffn_down', 'delta_g_final', 'new_m_w_ada', 'new_m_b_ada', 'new_m_g_mix', 'new_m_w_in', 'new_m_b_fgate', 'new_m_w_br_a', 'new_m_w_br_b', 'new_m_w_out', 'new_m_g_ffn', 'new_m_w_ffn_gate', 'new_m_w_ffn_up', 'new_m_w_ffn_down', 'new_m_g_final', 'new_v_w_ada', 'new_v_b_ada', 'new_v_g_mix', 'new_v_w_in', 'new_v_b_fgate', 'new_v_w_br_a', 'new_v_w_br_b', 'new_v_w_out', 'new_v_g_ffn', 'new_v_w_ffn_gate', 'new_v_w_ffn_up', 'new_v_w_ffn_down', 'new_v_g_final']
TWIN_LEAF_KINDS = {'loss': 'loss', 'grad_x': 'grad_x', 'grad_w_ada': 'grad_w', 'grad_b_ada': 'grad_w', 'grad_g_mix': 'grad_w', 'grad_w_in': 'grad_w', 'grad_b_fgate': 'grad_w', 'grad_w_br_a': 'grad_w', 'grad_w_br_b': 'grad_w', 'grad_w_out': 'grad_w', 'grad_g_ffn': 'grad_w', 'grad_w_ffn_gate': 'grad_w', 'grad_w_ffn_up': 'grad_w', 'grad_w_ffn_down': 'grad_w', 'grad_g_final': 'grad_w', 'delta_w_ada': 'delta_w', 'delta_b_ada': 'delta_w', 'delta_g_mix': 'delta_w', 'delta_w_in': 'delta_w', 'delta_b_fgate': 'delta_w', 'delta_w_br_a': 'delta_w', 'delta_w_br_b': 'delta_w', 'delta_w_out': 'delta_w', 'delta_g_ffn': 'delta_w', 'delta_w_ffn_gate': 'delta_w', 'delta_w_ffn_up': 'delta_w', 'delta_w_ffn_down': 'delta_w', 'delta_g_final': 'delta_w', 'new_m_w_ada': 'new_m', 'new_m_b_ada': 'new_m', 'new_m_g_mix': 'new_m', 'new_m_w_in': 'new_m', 'new_m_b_fgate': 'new_m', 'new_m_w_br_a': 'new_m', 'new_m_w_br_b': 'new_m', 'new_m_w_out': 'new_m', 'new_m_g_ffn': 'new_m', 'new_m_w_ffn_gate': 'new_m', 'new_m_w_ffn_up': 'new_m', 'new_m_w_ffn_down': 'new_m', 'new_m_g_final': 'new_m', 'new_v_w_ada': 'new_v', 'new_v_b_ada': 'new_v', 'new_v_g_mix': 'new_v', 'new_v_w_in': 'new_v', 'new_v_b_fgate': 'new_v', 'new_v_w_br_a': 'new_v', 'new_v_w_br_b': 'new_v', 'new_v_w_out': 'new_v', 'new_v_g_ffn': 'new_v', 'new_v_w_ffn_gate': 'new_v', 'new_v_w_ffn_up': 'new_v', 'new_v_w_ffn_down': 'new_v', 'new_v_g_final': 'new_v'}


def _forward(args):
    return _fwd_reference(*[args[k] for k in FWD_PARAMS])


def _output_shape():
    out = _jax.eval_shape(lambda: _forward(_fwd_setup_inputs(0)))
    return out.shape, out.dtype

N_MICROBATCH = 1
ADAM_LR = 0.001
ADAM_B1 = 0.9
ADAM_B2 = 0.999
ADAM_EPS = 1e-08
ADAM_WD = 0.01
ADAM_STEP = 10
PER_EXAMPLE_BATCH_AXIS = {'x': 0, 'c': 0, 'loss_target': 0}
SHARED_INPUTS = []
_WEIGHT_DTYPES = {'w_ada': _jnp.float32, 'b_ada': _jnp.float32, 'g_mix': _jnp.float32, 'w_in': _jnp.float32, 'b_fgate': _jnp.float32, 'w_br_a': _jnp.float32, 'w_br_b': _jnp.float32, 'w_out': _jnp.float32, 'g_ffn': _jnp.float32, 'w_ffn_gate': _jnp.float32, 'w_ffn_up': _jnp.float32, 'w_ffn_down': _jnp.float32, 'g_final': _jnp.float32}
MOMENT_SCALE = {'w_ada': 3.009510e-02, 'b_ada': 5.046313e-02, 'g_mix': 1.756636e-02, 'w_in': 8.355044e-03, 'b_fgate': 4.646622e-02, 'w_br_a': 1.144999e-02, 'w_br_b': 7.405753e-03, 'w_out': 1.364011e-02, 'g_ffn': 4.234383e-02, 'w_ffn_gate': 1.782633e-02, 'w_ffn_up': 1.724305e-02, 'w_ffn_down': 2.852840e-02, 'g_final': 1.602044e+01}


def _to_microbatches(a, axis):
    t = _jnp.moveaxis(a, axis, 0)
    t = t.reshape((N_MICROBATCH, t.shape[0] // N_MICROBATCH) + t.shape[1:])
    return _jnp.moveaxis(t, 1, axis + 1)


def setup_inputs(seed: int = 0) -> dict:
    inp = _fwd_setup_inputs(seed)
    key = _jax.random.fold_in(_jax.random.key(seed), 7919)
    shape, _ = _output_shape()
    out = dict(inp)
    out["loss_target"] = _jax.random.normal(_jax.random.fold_in(key, 0), shape, _jnp.float32)
    for i, name in enumerate(TWIN_WEIGHTS):
        w = inp[name].astype(_jnp.float32)
        if MOMENT_SCALE is None:
            s = _jnp.sqrt(_jnp.mean(_jnp.square(w)) + 1e-30)
        else:
            s = MOMENT_SCALE[name]
        km, kv = _jax.random.split(_jax.random.fold_in(key, i + 1))
        out[name] = w
        out["m_" + name] = s * _jax.random.normal(km, w.shape, _jnp.float32)
        out["v_" + name] = (s * s) * _jax.random.uniform(kv, w.shape, _jnp.float32, 0.5, 1.5)
    if N_MICROBATCH > 1:
        for name, axis in PER_EXAMPLE_BATCH_AXIS.items():
            out[name] = _to_microbatches(out[name], axis)
    return {'x': out['x'], 'c': out['c'], 'w_ada': out['w_ada'], 'b_ada': out['b_ada'], 'g_mix': out['g_mix'], 'w_in': out['w_in'], 'b_fgate': out['b_fgate'], 'w_br_a': out['w_br_a'], 'w_br_b': out['w_br_b'], 'w_out': out['w_out'], 'g_ffn': out['g_ffn'], 'w_ffn_gate': out['w_ffn_gate'], 'w_ffn_up': out['w_ffn_up'], 'w_ffn_down': out['w_ffn_down'], 'g_final': out['g_final'], 'loss_target': out['loss_target'], 'm_w_ada': out['m_w_ada'], 'm_b_ada': out['m_b_ada'], 'm_g_mix': out['m_g_mix'], 'm_w_in': out['m_w_in'], 'm_b_fgate': out['m_b_fgate'], 'm_w_br_a': out['m_w_br_a'], 'm_w_br_b': out['m_w_br_b'], 'm_w_out': out['m_w_out'], 'm_g_ffn': out['m_g_ffn'], 'm_w_ffn_gate': out['m_w_ffn_gate'], 'm_w_ffn_up': out['m_w_ffn_up'], 'm_w_ffn_down': out['m_w_ffn_down'], 'm_g_final': out['m_g_final'], 'v_w_ada': out['v_w_ada'], 'v_b_ada': out['v_b_ada'], 'v_g_mix': out['v_g_mix'], 'v_w_in': out['v_w_in'], 'v_b_fgate': out['v_b_fgate'], 'v_w_br_a': out['v_w_br_a'], 'v_w_br_b': out['v_w_br_b'], 'v_w_out': out['v_w_out'], 'v_g_ffn': out['v_g_ffn'], 'v_w_ffn_gate': out['v_w_ffn_gate'], 'v_w_ffn_up': out['v_w_ffn_up'], 'v_w_ffn_down': out['v_w_ffn_down'], 'v_g_final': out['v_g_final']}


def _loss(weights, diff, rest, loss_target):
    with _jax.named_scope("forward"):
        args = {**rest, TWIN_DIFF_INPUT: diff, **{k: w.astype(_WEIGHT_DTYPES[k]) for k, w in weights.items()}}
        y = _forward(args)
    with _jax.named_scope("loss_head"):
        err = _jnp.square(y.astype(_jnp.float32) - loss_target)
        return 0.5 * _jnp.sum(_jnp.mean(err, axis=-1)) if err.ndim else 0.5 * err


def _adamw(w, g, m, v):
    m = ADAM_B1 * m + (1.0 - ADAM_B1) * g
    v = ADAM_B2 * v + (1.0 - ADAM_B2) * _jnp.square(g)
    m_hat = m / (1.0 - ADAM_B1 ** ADAM_STEP)
    v_hat = v / (1.0 - ADAM_B2 ** ADAM_STEP)
    delta = -ADAM_LR * (m_hat / (_jnp.sqrt(v_hat) + ADAM_EPS) + ADAM_WD * w)
    return delta, m, v


def reference(x, c, w_ada, b_ada, g_mix, w_in, b_fgate, w_br_a, w_br_b, w_out, g_ffn, w_ffn_gate, w_ffn_up, w_ffn_down, g_final, loss_target, m_w_ada, m_b_ada, m_g_mix, m_w_in, m_b_fgate, m_w_br_a, m_w_br_b, m_w_out, m_g_ffn, m_w_ffn_gate, m_w_ffn_up, m_w_ffn_down, m_g_final, v_w_ada, v_b_ada, v_g_mix, v_w_in, v_b_fgate, v_w_br_a, v_w_br_b, v_w_out, v_g_ffn, v_w_ffn_gate, v_w_ffn_up, v_w_ffn_down, v_g_final):
    given = dict(x=x, c=c, w_ada=w_ada, b_ada=b_ada, g_mix=g_mix, w_in=w_in, b_fgate=b_fgate, w_br_a=w_br_a, w_br_b=w_br_b, w_out=w_out, g_ffn=g_ffn, w_ffn_gate=w_ffn_gate, w_ffn_up=w_ffn_up, w_ffn_down=w_ffn_down, g_final=g_final, loss_target=loss_target, m_w_ada=m_w_ada, m_b_ada=m_b_ada, m_g_mix=m_g_mix, m_w_in=m_w_in, m_b_fgate=m_b_fgate, m_w_br_a=m_w_br_a, m_w_br_b=m_w_br_b, m_w_out=m_w_out, m_g_ffn=m_g_ffn, m_w_ffn_gate=m_w_ffn_gate, m_w_ffn_up=m_w_ffn_up, m_w_ffn_down=m_w_ffn_down, m_g_final=m_g_final, v_w_ada=v_w_ada, v_b_ada=v_b_ada, v_g_mix=v_g_mix, v_w_in=v_w_in, v_b_fgate=v_b_fgate, v_w_br_a=v_w_br_a, v_w_br_b=v_w_br_b, v_w_out=v_w_out, v_g_ffn=v_g_ffn, v_w_ffn_gate=v_w_ffn_gate, v_w_ffn_up=v_w_ffn_up, v_w_ffn_down=v_w_ffn_down, v_g_final=v_g_final)
    weights = {n: given[n] for n in TWIN_WEIGHTS}
    shared = {n: given[n] for n in SHARED_INPUTS}
    per_example = {n: given[n] for n in ['x', 'c']}
    grad_fn = _jax.value_and_grad(_loss, argnums=(0, 1))

    def one_microbatch(ex, loss_target):
        ex = dict(ex)
        diff = ex.pop(TWIN_DIFF_INPUT)
        return grad_fn(weights, diff, {**shared, **ex}, loss_target)

    if N_MICROBATCH == 1:
        loss, (grad_w, grad_x) = one_microbatch(per_example, given["loss_target"])
    else:
        def body(carry, xs):
            loss_sum, grad_sum = carry
            l_k, (gw_k, gx_k) = one_microbatch(xs[0], xs[1])
            with _jax.named_scope("update"):
                return (loss_sum + l_k, _jax.tree.map(_jnp.add, grad_sum, gw_k)), gx_k

        init = (_jnp.zeros((), _jnp.float32), _jax.tree.map(_jnp.zeros_like, weights))
        (loss, grad_w), grad_x = _jax.lax.scan(body, init, (per_example, given["loss_target"]))
    with _jax.named_scope("update"):
        delta_w, new_m, new_v = {}, {}, {}
        for n in TWIN_WEIGHTS:
            delta_w[n], new_m[n], new_v[n] = _adamw(weights[n], grad_w[n], given["m_" + n], given["v_" + n])
    return (loss, grad_x, *[grad_w[n] for n in TWIN_WEIGHTS], *[delta_w[n] for n in TWIN_WEIGHTS],
            *[new_m[n] for n in TWIN_WEIGHTS], *[new_v[n] for n in TWIN_WEIGHTS])
```

```python
import functools

import jax
import jax.numpy as jnp
from jax import lax
from jax.experimental import pallas as pl
from jax.experimental.pallas import tpu as pltpu

f32 = jnp.float32
bf16 = jnp.bfloat16
SDS = jax.ShapeDtypeStruct
MESH = pl.DeviceIdType.MESH

N_DEV = 8
D = 1024
SEQ = 2048
HEAD_DIM = 64
N_FOX_HEADS = 8
FOX_W = 512
DIL_W = 768
DIL_OUT_W = 256
ROT_DIM = 16
ROPE_THETA = 500000.0
D_FF = 2816
IN_COLS = 5896
EPS = 1e-6
NEG = -1e30
ATT_SCALE = HEAD_DIM ** -0.5

ADAM_LR = 0.001
ADAM_B1 = 0.9
ADAM_B2 = 0.999
ADAM_EPS = 1e-08
ADAM_WD = 0.01
ADAM_STEP = 10

C_GA, C_GB, C_QB, C_KB, C_VB, C_QA, C_KA, C_VA, C_F = 0, 1024, 2304, 3072, 3840, 4608, 5120, 5632, 6144
PROJ_W = 6272
LANES = 128
VMEM_LIMIT = 52 * 1024 * 1024

PACK_ROWS = 2048
SMALL_ROWS = 16


def _params(sem=None):
    if sem is None:
        return pltpu.CompilerParams(vmem_limit_bytes=VMEM_LIMIT)
    return pltpu.CompilerParams(dimension_semantics=sem, vmem_limit_bytes=VMEM_LIMIT)


def _rowwise(fn, name, tiled, vecs, outs, reds=(), tile=256):
    nt, nv, no = len(tiled), len(vecs), len(outs)
    rows = tiled[0][0].shape[0]
    assert rows % tile == 0

    def body(*refs):
        tin = [r[...] for r in refs[:nt]]
        vin = [r[...] for r in refs[nt:nt + nv]]
        orefs = refs[nt + nv:nt + nv + no]
        rrefs = refs[nt + nv + no:]
        touts, routs = fn(tin, vin)
        for r, t in zip(orefs, touts, strict=True):
            r[...] = t.astype(r.dtype)
        if rrefs:
            @pl.when(pl.program_id(0) == 0)
            def _():
                for r in rrefs:
                    r[...] = jnp.zeros_like(r)
            for r, t in zip(rrefs, routs, strict=True):
                r[...] += t

    def col_map(cb):
        return lambda i: (i, cb)

    def whole_map(nd):
        return lambda i: (0,) * nd

    in_specs = [pl.BlockSpec((tile, w), col_map(cb)) for (_, w, cb) in tiled]
    in_specs += [pl.BlockSpec(v.shape, whole_map(v.ndim)) for v in vecs]
    out_specs = [pl.BlockSpec((tile, w), lambda i: (i, 0)) for (w, _) in outs]
    out_specs += [pl.BlockSpec((1, w), lambda i: (0, 0)) for w in reds]
    out_shape = [SDS((rows, w), dt) for (w, dt) in outs] + [SDS((1, w), f32) for w in reds]
    res = pl.pallas_call(
        body, grid=(rows // tile,), in_specs=in_specs, out_specs=out_specs, out_shape=out_shape, name=name,
        compiler_params=_params(("arbitrary",)),
    )(*[t[0] for t in tiled], *vecs)
    return res


def _matmul(a, b, *, ta=False, tb=False, out_dtype=f32, name, tm, tn, tk):
    m, k = (a.shape[1], a.shape[0]) if ta else a.shape
    n = b.shape[0] if tb else b.shape[1]
    assert (b.shape[1] if tb else b.shape[0]) == k
    assert m % tm == 0 and n % tn == 0 and k % tk == 0
    nk = k // tk
    dims = (((0 if ta else 1,), (1 if tb else 0,)), ((), ()))

    def body(a_ref, b_ref, o_ref, *acc):
        p = lax.dot_general(a_ref[...].astype(bf16), b_ref[...].astype(bf16), dims, preferred_element_type=f32)
        if nk == 1:
            o_ref[...] = p.astype(o_ref.dtype)
        else:
            acc_ref, = acc
            kk = pl.program_id(2)

            @pl.when(kk == 0)
            def _():
                acc_ref[...] = p

            @pl.when(kk > 0)
            def _():
                acc_ref[...] += p

            @pl.when(kk == nk - 1)
            def _():
                o_ref[...] = acc_ref[...].astype(o_ref.dtype)

    a_spec = pl.BlockSpec((tk, tm), lambda i, j, kk: (kk, i)) if ta else pl.BlockSpec((tm, tk), lambda i, j, kk: (i, kk))
    b_spec = pl.BlockSpec((tn, tk), lambda i, j, kk: (j, kk)) if tb else pl.BlockSpec((tk, tn), lambda i, j, kk: (kk, j))
    return pl.pallas_call(
        body, grid=(m // tm, n // tn, nk), in_specs=[a_spec, b_spec],
        out_specs=pl.BlockSpec((tm, tn), lambda i, j, kk: (i, j)),
        out_shape=SDS((m, n), out_dtype), name=name,
        scratch_shapes=[pltpu.VMEM((tm, tn), f32)] if nk > 1 else [],
        compiler_params=_params(("parallel", "parallel", "arbitrary")),
    )(a, b)


def _rms(x):
    r = lax.rsqrt(jnp.mean(x * x, axis=-1, keepdims=True) + EPS)
    return r, x * r


def _rms_bwd(r, xn, dxn):
    return r * (dxn - xn * jnp.mean(dxn * xn, axis=-1, keepdims=True))


def _colsum(t):
    return jnp.sum(t, axis=0, keepdims=True)


def _sigmoid(x):
    return 1.0 / (1.0 + jnp.exp(-x))


def _modulated_norm(x, g, shift, scale):
    _, xn = _rms(x)
    return (xn * g) * (1.0 + scale) + shift


def _pre1(x, modv, g_mix):
    def fn(t, v):
        (xt,), (mv, g) = t, v
        return [_modulated_norm(xt, g, mv[0:1], mv[1:2])], []
    return _rowwise(fn, "pre1", [(x, D, 0)], [modv, g_mix], [(D, bf16)])[0]


def _post1(x, mix, modv, g_ffn):
    def fn(t, v):
        (xt, mt), (mv, g) = t, v
        x1 = xt + mv[2:3] * mt
        return [x1, _modulated_norm(x1, g, mv[3:4], mv[4:5])], []
    return _rowwise(fn, "post1", [(x, D, 0), (mix, D, 0)], [modv, g_ffn], [(D, f32), (D, bf16)])


def _swiglu_fwd(au):
    def fn(t, v):
        a, u = t[0][:, :D_FF], t[0][:, D_FF:]
        return [a * _sigmoid(a) * u], []
    return _rowwise(fn, "swiglu_fwd", [(au, 2 * D_FF, 0)], [], [(D_FF, bf16)])[0]


def _swiglu_bwd(au, dact):
    def fn(t, v):
        a, u = t[0][:, :D_FF], t[0][:, D_FF:]
        sg = _sigmoid(a)
        da = t[1] * u * (sg * (1.0 + a * (1.0 - sg)))
        du = t[1] * (a * sg)
        return [jnp.concatenate([da, du], axis=1)], []
    return _rowwise(fn, "swiglu_bwd", [(au, 2 * D_FF, 0), (dact, D_FF, 0)], [], [(2 * D_FF, bf16)], tile=128)[0]


def _final(x1, ff, target, modv, g_final):
    def fn(t, v):
        (x1t, fft, tgt), (mv, g) = t, v
        x2 = x1t + mv[5:6] * fft
        r, xn = _rms(x2)
        err = xn * g - tgt
        dy = err * (1.0 / D)
        dx2 = _rms_bwd(r, xn, dy * g)
        return [dx2, dx2 * mv[5:6]], [_colsum(dy * xn), _colsum(dx2 * fft), _colsum(err * err) * (0.5 / D)]
    return _rowwise(fn, "final", [(x1, D, 0), (ff, D, 0), (target, D, 0)], [modv, g_final],
                    [(D, f32), (D, bf16)], [D, D, D])


def _mid_bwd(dh2, x1, dx2, mix, modv, g_ffn):
    def fn(t, v):
        (dh, x1t, dx2t, mt), (mv, g) = t, v
        r, xn = _rms(x1t)
        dn = dh * (1.0 + mv[4:5])
        dx1 = dx2t + _rms_bwd(r, xn, dn * g)
        return [dx1, dx1 * mv[2:3]], [_colsum(dh), _colsum(dh * (xn * g)), _colsum(dn * xn), _colsum(dx1 * mt)]
    return _rowwise(fn, "mid_bwd", [(dh2, D, 0), (x1, D, 0), (dx2, D, 0), (mix, D, 0)], [modv, g_ffn],
                    [(D, f32), (D, bf16)], [D, D, D, D])


def _first_bwd(dh1, x, dx1, modv, g_mix):
    def fn(t, v):
        (dh, xt, dx1t), (mv, g) = t, v
        r, xn = _rms(xt)
        dn = dh * (1.0 + mv[1:2])
        return [dx1t + _rms_bwd(r, xn, dn * g)], [_colsum(dh), _colsum(dh * (xn * g)), _colsum(dn * xn)]
    return _rowwise(fn, "first_bwd", [(dh1, D, 0), (x, D, 0), (dx1, D, 0)], [modv, g_mix], [(D, f32)], [D, D, D])


def _merge_fwd(ya, yb, proj):
    def fn(t, v):
        ya_t, yb_t, ga, gb = t
        return [_sigmoid(ga) * ya_t + _sigmoid(gb) * yb_t], []
    return _rowwise(fn, "merge_fwd", [(ya, D, 0), (yb, D, 0), (proj, D, C_GA // D), (proj, D, C_GB // D)], [],
                    [(D, bf16)])[0]


def _merge_bwd(dmerged, ya, yb, proj):
    def fn(t, v):
        dm, ya_t, yb_t, ga, gb = t
        sa, sb = _sigmoid(ga), _sigmoid(gb)
        return [dm * sa, dm * sb, dm * ya_t * (sa * (1.0 - sa)), dm * yb_t * (sb * (1.0 - sb))], []
    return _rowwise(fn, "merge_bwd",
                    [(dmerged, D, 0), (ya, D, 0), (yb, D, 0), (proj, D, C_GA // D), (proj, D, C_GB // D)], [],
                    [(D, bf16), (D, bf16), (D, bf16), (D, bf16)])


def _rope_tables():
    half = ROT_DIM // 2
    pos = jnp.arange(SEQ, dtype=f32)
    inv_freq = ROPE_THETA ** (-jnp.arange(0, ROT_DIM, 2, dtype=f32) / ROT_DIM)
    ang = pos[:, None] * inv_freq[None, :]
    cos, sin = jnp.cos(ang), jnp.sin(ang)
    pad = jnp.zeros((SEQ, HEAD_DIM - ROT_DIM), f32)
    zero = jnp.zeros((SEQ, half), f32)
    c_head = jnp.concatenate([cos, cos, pad + 1.0], axis=1)
    lo_head = jnp.concatenate([-sin, zero, pad], axis=1)
    hi_head = jnp.concatenate([zero, sin, pad], axis=1)
    reps = DIL_W // HEAD_DIM
    return tuple(jnp.tile(t, (1, reps)) for t in (c_head, lo_head, hi_head))


def _rope_fwd(proj, tables):
    half = ROT_DIM // 2

    def fn(t, v):
        q, k, c, lo, hi = t
        rot = lambda z: z * c + pltpu.roll(z, DIL_W - half, 1) * lo + pltpu.roll(z, half, 1) * hi
        return [rot(q), rot(k)], []
    return _rowwise(fn, "rope_fwd", [(proj, DIL_W, C_QB // DIL_W), (proj, DIL_W, C_KB // DIL_W)]
                    + [(tb, DIL_W, 0) for tb in tables], [], [(DIL_W, bf16), (DIL_W, bf16)])


def _rope_bwd(dq, dk, tables):
    half = ROT_DIM // 2

    def fn(t, v):
        dq_t, dk_t, c, lo, hi = t
        rot_t = lambda z: z * c + pltpu.roll(z * lo, half, 1) + pltpu.roll(z * hi, DIL_W - half, 1)
        return [rot_t(dq_t), rot_t(dk_t)], []
    return _rowwise(fn, "rope_bwd", [(dq, DIL_W, 0), (dk, DIL_W, 0)] + [(tb, DIL_W, 0) for tb in tables], [],
                    [(DIL_W, bf16), (DIL_W, bf16)])


def _head_bcast_sum(d):
    lane = lax.broadcasted_iota(jnp.int32, d.shape, 1)
    out = jnp.zeros_like(d)
    for h in range(d.shape[1] // HEAD_DIM):
        sel = (lane >= h * HEAD_DIM) & (lane < (h + 1) * HEAD_DIM)
        out = jnp.where(sel, jnp.sum(jnp.where(sel, d, 0.0), axis=1, keepdims=True), out)
    return out


def _dil_combine(o, lse):
    def fn(t, v):
        o0, o1, o2, l0, l1, l2 = t
        m = jnp.maximum(jnp.maximum(l0, l1), l2)
        w0, w1, w2 = jnp.exp(l0 - m), jnp.exp(l1 - m), jnp.exp(l2 - m)
        tot = w0 + w1 + w2
        return [(w0 * o0 + w1 * o1 + w2 * o2) / tot, m + jnp.log(tot)], []
    w = DIL_OUT_W
    return _rowwise(fn, "dil_combine", [(o, w, 0), (o, w, 1), (o, w, 2), (lse, w, 0), (lse, w, 1), (lse, w, 2)], [],
                    [(w, f32), (w, f32)])


def _dil_delta(dyb_h, yb_h):
    def fn(t, v):
        return [_head_bcast_sum(t[0] * t[1])], []
    return _rowwise(fn, "dil_delta", [(dyb_h, DIL_OUT_W, 0), (yb_h, DIL_OUT_W, 0)], [], [(DIL_OUT_W, f32)])[0]


def _adamw(w, g, m, v, name):
    shape = w.shape
    if w.ndim == 1:
        w, g, m, v = (t.reshape(1, -1) for t in (w, g, m, v))
    rows, cols = w.shape
    tile = 256 if rows % 256 == 0 and rows > 512 else rows

    def fn(t, _):
        wt, gt, mt, vt = t
        mn = ADAM_B1 * mt + (1.0 - ADAM_B1) * gt
        vn = ADAM_B2 * vt + (1.0 - ADAM_B2) * (gt * gt)
        m_hat = mn / (1.0 - ADAM_B1 ** ADAM_STEP)
        v_hat = vn / (1.0 - ADAM_B2 ** ADAM_STEP)
        return [-ADAM_LR * (m_hat / (jnp.sqrt(v_hat) + ADAM_EPS) + ADAM_WD * wt), mn, vn], []
    delta, mn, vn = _rowwise(fn, name, [(w, cols, 0), (g, cols, 0), (m, cols, 0), (v, cols, 0)], [],
                             [(cols, f32)] * 3, tile=tile)
    return delta.reshape(shape), mn.reshape(shape), vn.reshape(shape)


def _ada_fwd(c_all, w_shard, b_shard):
    def body(c_ref, w_ref, b_ref, o_ref):
        cv = c_ref[...]
        sc = (cv * _sigmoid(cv)).astype(bf16)
        o_ref[...] = jnp.dot(sc, w_ref[...].astype(bf16), preferred_element_type=f32) + b_ref[...]
    return pl.pallas_call(body, out_shape=SDS((N_DEV, w_shard.shape[1]), f32), name="ada_fwd",
                          compiler_params=_params())(c_all, w_shard, b_shard)


def _ada_bwd(c_all, dmod_cols):
    def body(c_ref, d_ref, o_ref):
        cv = c_ref[...]
        sc = cv * _sigmoid(cv)
        o_ref[...] = lax.dot_general(sc, d_ref[...], (((0,), (0,)), ((), ())), precision=lax.Precision.HIGHEST,
                                     preferred_element_type=f32)
    return pl.pallas_call(body, out_shape=SDS((D, dmod_cols.shape[1]), f32), name="ada_bwd",
                          compiler_params=_params())(c_all, dmod_cols)


def _small_reduce(gathered):
    def body(g_ref, o_ref, loss_ref):
        acc = g_ref[0]
        for d in range(1, N_DEV):
            acc = acc + g_ref[d]
        o_ref[...] = acc
        loss_ref[...] = jnp.zeros((1, LANES), f32) + jnp.sum(acc[10:11, :])
    return pl.pallas_call(body, out_shape=(SDS((SMALL_ROWS, D), f32), SDS((1, LANES), f32)), name="small_reduce",
                          compiler_params=_params())(gathered)


FOX_BLK = 256
CUM_BLK = 128


def _fox_gate_fwd(proj, b_pad):
    nblk = SEQ // CUM_BLK

    def body(f_ref, b_ref, col_ref, row_ref):
        r = lax.broadcasted_iota(jnp.int32, (CUM_BLK, CUM_BLK), 0)
        c = lax.broadcasted_iota(jnp.int32, (CUM_BLK, CUM_BLK), 1)
        tri = (r >= c).astype(f32)
        carry = jnp.zeros((1, LANES), f32)
        for blk in range(nblk):
            z = f_ref[blk * CUM_BLK:(blk + 1) * CUM_BLK, :] + b_ref[...]
            logf = jnp.minimum(z, 0.0) - jnp.log1p(jnp.exp(-jnp.abs(z)))
            cs = jnp.dot(tri, logf, precision=lax.Precision.HIGHEST, preferred_element_type=f32) + carry
            col_ref[blk * CUM_BLK:(blk + 1) * CUM_BLK, :] = cs
            carry = cs[CUM_BLK - 1:CUM_BLK, :]
        row_ref[...] = col_ref[...].T

    return pl.pallas_call(
        body, grid=(1,), in_specs=[pl.BlockSpec((SEQ, LANES), lambda i: (0, C_F // LANES)),
                                   pl.BlockSpec((1, LANES), lambda i: (0, 0))],
        out_specs=[pl.BlockSpec((SEQ, LANES), lambda i: (0, 0)), pl.BlockSpec((LANES, SEQ), lambda i: (0, 0))],
        out_shape=(SDS((SEQ, LANES), f32), SDS((LANES, SEQ), f32)), name="fox_gate_fwd",
        compiler_params=_params(("arbitrary",)),
    )(proj, b_pad)


def _fox_gate_bwd(dF_row, proj, b_pad):
    nblk = SEQ // CUM_BLK

    def body(d_ref, f_ref, b_ref, df_ref, db_ref, col_ref):
        r = lax.broadcasted_iota(jnp.int32, (CUM_BLK, CUM_BLK), 0)
        c = lax.broadcasted_iota(jnp.int32, (CUM_BLK, CUM_BLK), 1)
        tri = (r <= c).astype(f32)
        lane = lax.broadcasted_iota(jnp.int32, (CUM_BLK, LANES), 1)
        col_ref[...] = d_ref[...].T
        carry = jnp.zeros((1, LANES), f32)
        total = jnp.zeros((1, LANES), f32)
        for blk in reversed(range(nblk)):
            rows = slice(blk * CUM_BLK, (blk + 1) * CUM_BLK)
            cs = jnp.dot(tri, col_ref[rows, :], precision=lax.Precision.HIGHEST, preferred_element_type=f32) + carry
            carry = cs[0:1, :]
            z = f_ref[rows, :] + b_ref[...]
            df = jnp.where(lane < N_FOX_HEADS, cs * _sigmoid(-z), 0.0)
            df_ref[rows, :] = df.astype(df_ref.dtype)
            total = total + _colsum(df)
        db_ref[...] = total

    return pl.pallas_call(
        body, grid=(1,), in_specs=[pl.BlockSpec((LANES, SEQ), lambda i: (0, 0)),
                                   pl.BlockSpec((SEQ, LANES), lambda i: (0, C_F // LANES)),
                                   pl.BlockSpec((1, LANES), lambda i: (0, 0))],
        out_specs=[pl.BlockSpec((SEQ, LANES), lambda i: (0, 0)), pl.BlockSpec((1, LANES), lambda i: (0, 0))],
        out_shape=(SDS((SEQ, LANES), bf16), SDS((1, LANES), f32)), name="fox_gate_bwd",
        scratch_shapes=[pltpu.VMEM((SEQ, LANES), f32)],
        compiler_params=_params(("arbitrary",)),
    )(dF_row, proj, b_pad)


def _nt(a, b):
    return lax.dot_general(a, b, (((1,), (1,)), ((), ())), preferred_element_type=f32)


def _tn(a, b):
    return lax.dot_general(a, b, (((0,), (0,)), ((), ())), preferred_element_type=f32)


def _fox_fwd(proj, f_bcast, f_row):
    blk = FOX_BLK
    npair = FOX_W // LANES

    def body(q_ref, k_ref, v_ref, fb_ref, fr_ref, o_ref, lse_ref):
        i = pl.program_id(1)
        lane = lax.broadcasted_iota(jnp.int32, (blk, LANES), 1)
        rowi = i * blk + lax.broadcasted_iota(jnp.int32, (blk, blk), 0)
        coli = lax.broadcasted_iota(jnp.int32, (blk, blk), 1)
        q = q_ref[...]
        fb = fb_ref[0]
        out = jnp.zeros((blk, LANES), f32)
        lse = jnp.zeros((blk, LANES), f32)
        for h in range(2):
            hm = (lane < HEAD_DIM) if h == 0 else (lane >= HEAD_DIM)
            qh = jnp.where(hm, q, 0.0).astype(bf16)
            fq = fb[:, h * HEAD_DIM:h * HEAD_DIM + 1]

            def step(j, carry, qh=qh, fq=fq, h=h):
                m, l, acc = carry
                off = pl.multiple_of(j * blk, blk)
                kk = k_ref[pl.ds(off, blk), :].astype(bf16)
                vv = v_ref[pl.ds(off, blk), :].astype(bf16)
                s = _nt(qh, kk) * ATT_SCALE + (fq - fr_ref[0, h:h + 1, pl.ds(off, blk)])
                s = jnp.where(rowi >= coli + off, s, NEG)
                mn = jnp.maximum(m, jnp.max(s, axis=1, keepdims=True))
                p = jnp.exp(s - mn)
                a = jnp.exp(m - mn)
                return mn, a * l + jnp.sum(p, axis=1, keepdims=True), a * acc + jnp.dot(
                    p.astype(bf16), vv, preferred_element_type=f32)

            m, l, acc = lax.fori_loop(0, i + 1, step, (jnp.full((blk, 1), NEG, f32), jnp.zeros((blk, 1), f32),
                                                       jnp.zeros((blk, LANES), f32)))
            out = jnp.where(hm, acc / l, out)
            lse = jnp.where(hm, m + jnp.log(l), lse)
        o_ref[...] = out
        lse_ref[0] = lse

    qa, ka, va = C_QA // LANES, C_KA // LANES, C_VA // LANES
    return pl.pallas_call(
        body, grid=(npair, SEQ // blk),
        in_specs=[pl.BlockSpec((blk, LANES), lambda p, i: (i, qa + p)),
                  pl.BlockSpec((SEQ, LANES), lambda p, i: (0, ka + p)),
                  pl.BlockSpec((SEQ, LANES), lambda p, i: (0, va + p)),
                  pl.BlockSpec((1, blk, LANES), lambda p, i: (p, i, 0)),
                  pl.BlockSpec((1, 8, SEQ), lambda p, i: (p, 0, 0))],
        out_specs=[pl.BlockSpec((blk, LANES), lambda p, i: (i, p)),
                   pl.BlockSpec((1, blk, LANES), lambda p, i: (p, i, 0))],
        out_shape=(SDS((SEQ, FOX_W), f32), SDS((npair, SEQ, LANES), f32)), name="fox_fwd",
        compiler_params=_params(("parallel", "arbitrary")),
    )(proj, proj, proj, f_bcast, f_row)


def _fox_bwd(proj, do, o, lse, f_bcast, f_row):
    blk = FOX_BLK
    npair = FOX_W // LANES
    nblk = SEQ // blk

    def body(q_ref, k_ref, v_ref, do_ref, o_ref, lse_ref, fb_ref, fr_ref, dq_ref, dk_ref, dv_ref, df_ref,
             dq_acc, delta_ref, res_ref):
        lane_s = lax.broadcasted_iota(jnp.int32, (SEQ, LANES), 1)
        prod = do_ref[...] * o_ref[...]
        d_a = jnp.sum(jnp.where(lane_s < HEAD_DIM, prod, 0.0), axis=1, keepdims=True)
        d_b = jnp.sum(jnp.where(lane_s >= HEAD_DIM, prod, 0.0), axis=1, keepdims=True)
        delta_ref[...] = jnp.where(lane_s < HEAD_DIM, d_a, d_b)
        dq_acc[...] = jnp.zeros_like(dq_acc)
        res_ref[...] = jnp.zeros_like(res_ref)
        df_ref[...] = jnp.zeros_like(df_ref)
        lane = lax.broadcasted_iota(jnp.int32, (blk, LANES), 1)
        rowi = lax.broadcasted_iota(jnp.int32, (blk, blk), 0)
        coli = lax.broadcasted_iota(jnp.int32, (blk, blk), 1)

        def probs(qq, kh, fk, col, qoff, koff):
            s = _nt(qq, kh) * ATT_SCALE + (fb_ref[0, pl.ds(qoff, blk), col:col + 1] - fk)
            s = jnp.where(rowi + qoff >= coli + koff, s, NEG)
            return jnp.exp(s - lse_ref[0, pl.ds(qoff, blk), col:col + 1])

        def kv_step(kj, _):
            koff = pl.multiple_of(kj * blk, blk)
            kf = k_ref[pl.ds(koff, blk), :]
            vf = v_ref[pl.ds(koff, blk), :]
            dk_out = jnp.zeros((blk, LANES), f32)
            dv_out = jnp.zeros((blk, LANES), f32)
            for h in range(2):
                hm = (lane < HEAD_DIM) if h == 0 else (lane >= HEAD_DIM)
                kh = jnp.where(hm, kf, 0.0).astype(bf16)
                vh = jnp.where(hm, vf, 0.0).astype(bf16)
                fk = fr_ref[0, h:h + 1, pl.ds(koff, blk)]
                col = h * HEAD_DIM

                def q_step(qi, carry, kh=kh, vh=vh, fk=fk, col=col, hm=hm):
                    dk_h, dv_h, dcol = carry
                    qoff = pl.multiple_of(qi * blk, blk)
                    qq = q_ref[pl.ds(qoff, blk), :].astype(bf16)
                    dd = do_ref[pl.ds(qoff, blk), :].astype(bf16)
                    p = probs(qq, kh, fk, col, qoff, koff)
                    dl = p * (_nt(dd, vh) - delta_ref[pl.ds(qoff, blk), col:col + 1])
                    dlb = dl.astype(bf16)
                    dq_acc[pl.ds(qoff, blk), :] += jnp.dot(dlb, kh, preferred_element_type=f32) * ATT_SCALE
                    res_ref[pl.ds(qoff, blk), :] += jnp.where(hm, jnp.sum(dl, axis=1, keepdims=True), 0.0)
                    return dk_h + _tn(dlb, qq), dv_h + _tn(p.astype(bf16), dd), dcol + _colsum(dl)

                dk_h, dv_h, dcol = lax.fori_loop(kj, nblk, q_step, (jnp.zeros((blk, LANES), f32),
                                                                   jnp.zeros((blk, LANES), f32),
                                                                   jnp.zeros((1, blk), f32)))
                dk_out = jnp.where(hm, dk_h * ATT_SCALE, dk_out)
                dv_out = jnp.where(hm, dv_h, dv_out)
                df_ref[0, h:h + 1, pl.ds(koff, blk)] = -dcol
            dk_ref[pl.ds(koff, blk), :] = dk_out.astype(dk_ref.dtype)
            dv_ref[pl.ds(koff, blk), :] = dv_out.astype(dv_ref.dtype)
            return 0

        lax.fori_loop(0, nblk, kv_step, 0)
        dq_ref[...] = dq_acc[...].astype(dq_ref.dtype)

        def kv_fix(kj, _):
            koff = pl.multiple_of(kj * blk, blk)
            kf = k_ref[pl.ds(koff, blk), :]
            for h in range(2):
                hm = (lane < HEAD_DIM) if h == 0 else (lane >= HEAD_DIM)
                kh = jnp.where(hm, kf, 0.0).astype(bf16)
                fk = fr_ref[0, h:h + 1, pl.ds(koff, blk)]
                col = h * HEAD_DIM

                def q_fix(qi, corr, kh=kh, fk=fk, col=col):
                    qoff = pl.multiple_of(qi * blk, blk)
                    p = probs(q_ref[pl.ds(qoff, blk), :].astype(bf16), kh, fk, col, qoff, koff)
                    return corr + _colsum(p * res_ref[pl.ds(qoff, blk), col:col + 1])

                df_ref[0, h:h + 1, pl.ds(koff, blk)] += lax.fori_loop(kj, nblk, q_fix, jnp.zeros((1, blk), f32))
            return 0

        lax.fori_loop(0, nblk, kv_fix, 0)

    qa, ka, va = C_QA // LANES, C_KA // LANES, C_VA // LANES
    slab = lambda base: pl.BlockSpec((SEQ, LANES), lambda p: (0, base + p))
    per_pair = pl.BlockSpec((1, SEQ, LANES), lambda p: (p, 0, 0))
    rows = pl.BlockSpec((1, 8, SEQ), lambda p: (p, 0, 0))
    return pl.pallas_call(
        body, grid=(npair,),
        in_specs=[slab(qa), slab(ka), slab(va), slab(0), slab(0), per_pair, per_pair, rows],
        out_specs=[slab(0), slab(0), slab(0), rows],
        out_shape=(SDS((SEQ, FOX_W), bf16),) * 3 + (SDS((npair, 8, SEQ), f32),), name="fox_bwd",
        scratch_shapes=[pltpu.VMEM((SEQ, LANES), f32)] * 3,
        compiler_params=_params(("parallel",)),
    )(proj, proj, proj, do, o, lse, f_bcast, f_row)


DIL_BLK = 128
N_GROUPS = 3
DIL_PAIRS = DIL_OUT_W // LANES
DIL_NBLK = SEQ // DIL_BLK


def _blocks_per_seq(g):
    return jnp.where(g == 0, 16, jnp.where(g == 1, 4, 1))


def _dil_masks(has_prev):
    r = lax.broadcasted_iota(jnp.int32, (DIL_BLK, DIL_BLK), 0)
    c = lax.broadcasted_iota(jnp.int32, (DIL_BLK, DIL_BLK), 1)
    return r >= c, (c >= r) & has_prev


def _dil_specs():
    cur = pl.BlockSpec((1, DIL_BLK, LANES), lambda g, p, n: (g, n, p))
    prev = pl.BlockSpec((1, DIL_BLK, LANES), lambda g, p, n: (g, jnp.maximum(n - 1, 0), p))
    return cur, prev


def _dil_fwd(q, k, v):
    def body(q_ref, kc_ref, kp_ref, vc_ref, vp_ref, o_ref, lse_ref):
        g, n = pl.program_id(0), pl.program_id(2)
        has_prev = (n & (_blocks_per_seq(g) - 1)) > 0
        mc, mp = _dil_masks(has_prev)
        lane = lax.broadcasted_iota(jnp.int32, (DIL_BLK, LANES), 1)
        qv, kc, kp, vc, vp = q_ref[0], kc_ref[0], kp_ref[0], vc_ref[0], vp_ref[0]
        out = jnp.zeros((DIL_BLK, LANES), f32)
        lse = jnp.zeros((DIL_BLK, LANES), f32)
        for h in range(2):
            hm = (lane < HEAD_DIM) if h == 0 else (lane >= HEAD_DIM)
            qh = jnp.where(hm, qv, jnp.zeros_like(qv))
            sc = jnp.where(mc, _nt(qh, kc) * ATT_SCALE, NEG)
            sp = jnp.where(mp, _nt(qh, kp) * ATT_SCALE, NEG)
            m = jnp.maximum(jnp.max(sc, axis=1, keepdims=True), jnp.max(sp, axis=1, keepdims=True))
            pc, pp = jnp.exp(sc - m), jnp.exp(sp - m)
            l = jnp.sum(pc, axis=1, keepdims=True) + jnp.sum(pp, axis=1, keepdims=True)
            acc = jnp.dot(pc.astype(bf16), vc, preferred_element_type=f32) + jnp.dot(
                pp.astype(bf16), vp, preferred_element_type=f32)
            out = jnp.where(hm, acc / l, out)
            lse = jnp.where(hm, m + jnp.log(l), lse)
        o_ref[0] = out
        lse_ref[0] = lse

    cur, prev = _dil_specs()
    shape = SDS((N_GROUPS, SEQ, DIL_OUT_W), f32)
    return pl.pallas_call(
        body, grid=(N_GROUPS, DIL_PAIRS, DIL_NBLK), in_specs=[cur, cur, prev, cur, prev], out_specs=[cur, cur],
        out_shape=(shape, shape), name="dil_fwd", compiler_params=_params(("parallel", "parallel", "arbitrary")),
    )(q, k, k, v, v)


def _dil_bwd(q, k, v, do, lse, delta):
    def body(q_ref, kc_ref, kp_ref, vc_ref, vp_ref, do_ref, lse_ref, dl_ref, dq_ref, dk_ref, dv_ref):
        g, n = pl.program_id(0), pl.program_id(2)
        has_prev = (n & (_blocks_per_seq(g) - 1)) > 0
        mc, mp = _dil_masks(has_prev)
        lane = lax.broadcasted_iota(jnp.int32, (DIL_BLK, LANES), 1)
        qv, kc, kp, vc, vp = q_ref[0], kc_ref[0], kp_ref[0], vc_ref[0], vp_ref[0]
        dov = do_ref[0].astype(bf16)
        lsev, delv = lse_ref[0], dl_ref[0]
        dq = jnp.zeros((DIL_BLK, LANES), f32)
        dkc = jnp.zeros((DIL_BLK, LANES), f32)
        dkp = jnp.zeros((DIL_BLK, LANES), f32)
        dvc = jnp.zeros((DIL_BLK, LANES), f32)
        dvp = jnp.zeros((DIL_BLK, LANES), f32)
        for h in range(2):
            hm = (lane < HEAD_DIM) if h == 0 else (lane >= HEAD_DIM)
            col = h * HEAD_DIM
            qh = jnp.where(hm, qv, jnp.zeros_like(qv))
            doh = jnp.where(hm, dov, jnp.zeros_like(dov))
            lh, dh = lsev[:, col:col + 1], delv[:, col:col + 1]
            pc = jnp.exp(jnp.where(mc, _nt(qh, kc) * ATT_SCALE, NEG) - lh)
            pp = jnp.exp(jnp.where(mp, _nt(qh, kp) * ATT_SCALE, NEG) - lh)
            dlc = (pc * (_nt(doh, vc) - dh)).astype(bf16)
            dlp = (pp * (_nt(doh, vp) - dh)).astype(bf16)
            dq = jnp.where(hm, jnp.dot(dlc, kc, preferred_element_type=f32)
                           + jnp.dot(dlp, kp, preferred_element_type=f32), dq)
            dkc = jnp.where(hm, _tn(dlc, qh), dkc)
            dkp = jnp.where(hm, _tn(dlp, qh), dkp)
            dvc = jnp.where(hm, _tn(pc.astype(bf16), doh), dvc)
            dvp = jnp.where(hm, _tn(pp.astype(bf16), doh), dvp)
        dq_ref[0] = dq * ATT_SCALE
        off = pl.multiple_of(n * DIL_BLK, DIL_BLK)
        dk_ref[0, pl.ds(off, DIL_BLK), :] = dkc * ATT_SCALE
        dv_ref[0, pl.ds(off, DIL_BLK), :] = dvc

        @pl.when(has_prev)
        def _():
            poff = pl.multiple_of((n - 1) * DIL_BLK, DIL_BLK)
            dk_ref[0, pl.ds(poff, DIL_BLK), :] += dkp * ATT_SCALE
            dv_ref[0, pl.ds(poff, DIL_BLK), :] += dvp

    cur, prev = _dil_specs()
    whole = pl.BlockSpec((1, SEQ, LANES), lambda g, p, n: (g, 0, p))
    shape = SDS((N_GROUPS, SEQ, DIL_OUT_W), f32)
    return pl.pallas_call(
        body, grid=(N_GROUPS, DIL_PAIRS, DIL_NBLK), in_specs=[cur, cur, prev, cur, prev, cur, cur, cur],
        out_specs=[cur, whole, whole], out_shape=(shape, shape, shape), name="dil_bwd",
        compiler_params=_params(("parallel", "parallel", "arbitrary")),
    )(q, k, k, v, v, do, lse, delta)


_DILATIONS = (1, 4, 16)


def _to_classes(t):
    w = t.shape[1] // N_GROUPS
    out = []
    for g, d in enumerate(_DILATIONS):
        s = t[:, g * w:(g + 1) * w]
        out.append(s.reshape(SEQ // d, d, w).transpose(1, 0, 2).reshape(SEQ, w))
    return jnp.stack(out)


def _from_classes(t):
    w = t.shape[2]
    out = [t[g].reshape(d, SEQ // d, w).transpose(1, 0, 2).reshape(SEQ, w) for g, d in enumerate(_DILATIONS)]
    return jnp.concatenate(out, axis=1)


def _position():
    return lax.axis_index("x"), lax.axis_index("y"), lax.axis_index("c")


def _all_gather(block, name):
    def body(x_ref, out_ref, send_sems, recv_sems, local_sem):
        x, y, c = _position()
        me, sibling = (x, y, c), (x, y, 1 - c)
        chips = [(1 - x, y), (x, 1 - y), (1 - x, 1 - y)]

        def slot(px, py, pc):
            return out_ref.at[4 * px + 2 * py + pc]

        def copy(k, blk, to, src=None):
            return pltpu.make_async_remote_copy(
                src_ref=slot(*blk) if src is None else src, dst_ref=slot(*blk),
                send_sem=send_sems.at[k], recv_sem=recv_sems.at[k], device_id=to, device_id_type=MESH)

        mine = pltpu.make_async_copy(x_ref, slot(*me), local_sem)
        mine.start()
        first = [copy(0, me, sibling, src=x_ref)]
        first += [copy(1 + j, me, (*chip, c), src=x_ref) for j, chip in enumerate(chips)]
        for cp in first:
            cp.start()
        passed = [copy(4 + j, (*chip, c), sibling) for j, chip in enumerate(chips)]
        for j, chip in enumerate(chips):
            copy(1 + j, (*chip, c), me).wait_recv()
            passed[j].start()
        copy(0, sibling, me).wait_recv()
        for j, chip in enumerate(chips):
            copy(4 + j, (*chip, 1 - c), me).wait_recv()
        for cp in first + passed:
            cp.wait_send()
        mine.wait()

    return pl.pallas_call(
        body, out_shape=SDS((N_DEV,) + block.shape, block.dtype),
        in_specs=[pl.BlockSpec(memory_space=pl.ANY)], out_specs=pl.BlockSpec(memory_space=pl.ANY),
        scratch_shapes=[pltpu.SemaphoreType.DMA((7,)), pltpu.SemaphoreType.DMA((7,)), pltpu.SemaphoreType.DMA],
        name=name,
    )(block)


def _pair_exchange(g):
    def body(g_ref, r_ref, send_sems, recv_sems):
        x, y, c = _position()
        copies = []
        for k in range(4):
            cp = pltpu.make_async_remote_copy(
                src_ref=g_ref.at[2 * k + (1 - c)], dst_ref=r_ref.at[k], send_sem=send_sems.at[k],
                recv_sem=recv_sems.at[k], device_id=(x, y, 1 - c), device_id_type=MESH)
            cp.start()
            copies.append(cp)
        for cp in copies:
            cp.wait()

    return pl.pallas_call(
        body, out_shape=SDS((4,) + g.shape[1:], g.dtype),
        in_specs=[pl.BlockSpec(memory_space=pl.ANY)], out_specs=pl.BlockSpec(memory_space=pl.ANY),
        scratch_shapes=[pltpu.SemaphoreType.DMA((4,)), pltpu.SemaphoreType.DMA((4,))], name="pair_exchange",
    )(g)


def _chip_exchange(t):
    def body(t_ref, r_ref, send_sems, recv_sems):
        x, y, c = _position()
        chips = [(1 - x, y), (x, 1 - y), (1 - x, 1 - y)]
        copies = []
        for j, (px, py) in enumerate(chips):
            cp = pltpu.make_async_remote_copy(
                src_ref=t_ref.at[2 * px + py], dst_ref=r_ref.at[j], send_sem=send_sems.at[j],
                recv_sem=recv_sems.at[j], device_id=(px, py, c), device_id_type=MESH)
            cp.start()
            copies.append(cp)
        for cp in copies:
            cp.wait()

    return pl.pallas_call(
        body, out_shape=SDS((3,) + t.shape[1:], t.dtype),
        in_specs=[pl.BlockSpec(memory_space=pl.ANY)], out_specs=pl.BlockSpec(memory_space=pl.ANY),
        scratch_shapes=[pltpu.SemaphoreType.DMA((3,)), pltpu.SemaphoreType.DMA((3,))], name="chip_exchange",
    )(t)


RS_TILE = 256


def _pair_add(g, r1, core):
    def body(c_ref, g_ref, r_ref, o_ref):
        o_ref[...] = g_ref[...] + r_ref[...]

    blk = (1, RS_TILE, D)
    return pl.pallas_call(
        body, out_shape=SDS((4,) + g.shape[1:], f32), name="pair_add",
        grid_spec=pltpu.PrefetchScalarGridSpec(
            num_scalar_prefetch=1, grid=(4, g.shape[1] // RS_TILE),
            in_specs=[pl.BlockSpec(blk, lambda k, i, c_ref: (2 * k + c_ref[0], i, 0)),
                      pl.BlockSpec(blk, lambda k, i, c_ref: (k, i, 0))],
            out_specs=pl.BlockSpec(blk, lambda k, i, c_ref: (k, i, 0))),
        compiler_params=_params(("parallel", "arbitrary")),
    )(core, g, r1)


def _chip_add(t, r2, chip):
    def body(c_ref, t_ref, r_ref, o_ref):
        o_ref[...] = ((t_ref[0] + r_ref[0]) + r_ref[1]) + r_ref[2]

    return pl.pallas_call(
        body, out_shape=SDS(t.shape[1:], f32), name="chip_add",
        grid_spec=pltpu.PrefetchScalarGridSpec(
            num_scalar_prefetch=1, grid=(t.shape[1] // RS_TILE,),
            in_specs=[pl.BlockSpec((1, RS_TILE, D), lambda i, c_ref: (c_ref[0], i, 0)),
                      pl.BlockSpec((3, RS_TILE, D), lambda i, c_ref: (0, i, 0))],
            out_specs=pl.BlockSpec((RS_TILE, D), lambda i, c_ref: (i, 0))),
        compiler_params=_params(("arbitrary",)),
    )(chip, t, r2)


W_IN_SH = IN_COLS // N_DEV
W_BR_SH = D // N_DEV
W_FF_SH = D_FF // N_DEV
_PACKED = (("w_in", D, True, W_IN_SH), ("w_br_a", FOX_W, True, W_BR_SH), ("w_br_b", DIL_OUT_W, True, W_BR_SH),
           ("w_out", D, False, W_BR_SH), ("w_ffn_gate", D, True, W_FF_SH), ("w_ffn_up", D, True, W_FF_SH),
           ("w_ffn_down", D_FF, False, W_FF_SH))


def _pack_rows(name, rows, by_col, width):
    n = rows * width if by_col else width * D
    assert n % D == 0
    return n // D


def _pack(shards, dtype):
    parts = [shards[name].astype(dtype).reshape(-1, D) for name, *_ in _PACKED]
    used = sum(p.shape[0] for p in parts)
    parts.append(jnp.zeros((PACK_ROWS - used, D), dtype))
    return jnp.concatenate(parts, axis=0)


def _unpack_full(gathered):
    out, row = {}, 0
    for spec in _PACKED:
        name, rows, by_col, width = spec
        n = _pack_rows(*spec)
        part = gathered[:, row:row + n, :]
        if by_col:
            out[name] = part.reshape(N_DEV, rows, width).transpose(1, 0, 2).reshape(rows, N_DEV * width)
        else:
            out[name] = part.reshape(N_DEV * width, D)
        row += n
    return out


def _pack_by_device(grads):
    parts = []
    for name, rows, by_col, width in _PACKED:
        gw = grads[name]
        if by_col:
            parts.append(gw.reshape(rows, N_DEV, width).transpose(1, 0, 2).reshape(N_DEV, -1, D))
        else:
            parts.append(gw.reshape(N_DEV, -1, D))
    used = sum(p.shape[1] for p in parts)
    parts.append(jnp.zeros((N_DEV, PACK_ROWS - used, D), f32))
    return jnp.concatenate(parts, axis=1)


def _unpack_shard(packed):
    out, row = {}, 0
    for spec in _PACKED:
        name, rows, by_col, width = spec
        n = _pack_rows(*spec)
        out[name] = packed[row:row + n].reshape((rows, width) if by_col else (width, D))
        row += n
    return out


def _pad_w_in(w):
    qa, ka, va = w[:, 0:512], w[:, 512:1024], w[:, 1024:1536]
    fg = w[:, 1536:1544]
    qb, kb, vb = w[:, 1544:2312], w[:, 2312:3080], w[:, 3080:3848]
    ga, gb = w[:, 3848:4872], w[:, 4872:5896]
    z = lambda n: jnp.zeros((w.shape[0], n), w.dtype)
    return jnp.concatenate([ga, gb, z(C_QB - 2 * D), qb, kb, vb, qa, ka, va, fg, z(LANES - N_FOX_HEADS)], axis=1)


def _unpad_w_in(w):
    sl = lambda a, n: w[:, a:a + n]
    return jnp.concatenate([sl(C_QA, FOX_W), sl(C_KA, FOX_W), sl(C_VA, FOX_W), sl(C_F, N_FOX_HEADS),
                            sl(C_QB, DIL_W), sl(C_KB, DIL_W), sl(C_VB, DIL_W), sl(C_GA, D), sl(C_GB, D)], axis=1)


def kernel(x, c, w_ada, b_ada, g_mix, w_in, b_fgate, w_br_a, w_br_b, w_out, g_ffn, w_ffn_gate, w_ffn_up, w_ffn_down, g_final, loss_target, m_w_ada, m_b_ada, m_g_mix, m_w_in, m_b_fgate, m_w_br_a, m_w_br_b, m_w_out, m_g_ffn, m_w_ffn_gate, m_w_ffn_up, m_w_ffn_down, m_g_final, v_w_ada, v_b_ada, v_g_mix, v_w_in, v_b_fgate, v_w_br_a, v_w_br_b, v_w_out, v_g_ffn, v_w_ffn_gate, v_w_ffn_up, v_w_ffn_down, v_g_final):
    px, py, pc = _position()
    dev = 4 * px + 2 * py + pc
    x2d, tgt = x[0], loss_target[0]
    local = {"w_in": w_in[0], "w_br_a": w_br_a[0], "w_br_b": w_br_b[0], "w_out": w_out[0],
             "w_ffn_gate": w_ffn_gate[0], "w_ffn_up": w_ffn_up[0], "w_ffn_down": w_ffn_down[0]}

    c_all = _all_gather(c, "gather_c").reshape(N_DEV, D)
    ada_cols = w_ada.shape[2]
    b_shard = lax.dynamic_slice(b_ada, (0, dev * ada_cols), (1, ada_cols))
    mod_shard = _ada_fwd(c_all, w_ada[0], b_shard)
    mod_all = _all_gather(mod_shard, "gather_mod")
    modv = lax.dynamic_index_in_dim(mod_all, dev, axis=1, keepdims=False).reshape(6, D)

    full = _unpack_full(_all_gather(_pack(local, bf16), "gather_weights"))
    w_in_p = _pad_w_in(full["w_in"])
    w_gu = jnp.concatenate([full["w_ffn_gate"], full["w_ffn_up"]], axis=1)
    w_a, w_b, w_o, w_d = full["w_br_a"], full["w_br_b"], full["w_out"], full["w_ffn_down"]

    h1 = _pre1(x2d, modv, g_mix)
    proj = _matmul(h1, w_in_p, name="mm_proj", tm=SEQ, tn=896, tk=D)
    b_pad = jnp.pad(b_fgate, ((0, 0), (0, LANES - N_FOX_HEADS)))
    f_col, f_rowm = _fox_gate_fwd(proj, b_pad)
    npair = FOX_W // LANES
    f_heads = f_rowm[:N_FOX_HEADS].reshape(npair, 2, SEQ)
    f_bcast = jnp.repeat(f_heads, HEAD_DIM, axis=1).transpose(0, 2, 1)
    f_row = jnp.pad(f_heads, ((0, 0), (0, 6), (0, 0)))
    ya_h, lse_a = _fox_fwd(proj, f_bcast, f_row)
    ya = _matmul(ya_h, w_a, name="mm_br_a", tm=SEQ, tn=512, tk=FOX_W)

    tables = _rope_tables()
    qb_r, kb_r = _rope_fwd(proj, tables)
    vb = proj[:, C_VB:C_VB + DIL_W].astype(bf16)
    q_c, k_c, v_c = _to_classes(qb_r), _to_classes(kb_r), _to_classes(vb)
    o_c, lse_c = _dil_fwd(q_c, k_c, v_c)
    yb_h, lse_b = _dil_combine(_from_classes(o_c), _from_classes(lse_c))
    yb = _matmul(yb_h, w_b, name="mm_br_b", tm=SEQ, tn=512, tk=DIL_OUT_W)

    merged = _merge_fwd(ya, yb, proj)
    mix = _matmul(merged, w_o, name="mm_out", tm=SEQ, tn=512, tk=D)
    x1, h2 = _post1(x2d, mix, modv, g_ffn)
    au = _matmul(h2, w_gu, name="mm_ffn_in", tm=SEQ, tn=512, tk=D)
    act = _swiglu_fwd(au)
    ff = _matmul(act, w_d, name="mm_ffn_down", tm=SEQ, tn=512, tk=D_FF // 2)

    dx2, dff, dg_final, dga_f, loss_lanes = _final(x1, ff, tgt, modv, g_final.reshape(1, D))
    dact = _matmul(dff, w_d, tb=True, name="mm_d_act", tm=SEQ // 2, tn=D_FF // 2, tk=D)
    dau = _swiglu_bwd(au, dact)
    dh2 = _matmul(dau, w_gu, tb=True, name="mm_d_h2", tm=SEQ // 2, tn=D, tk=D_FF // 2)
    dx1, dmix, dsh_f, dsc_f, dg_ffn, dga_m = _mid_bwd(dh2, x1, dx2, mix, modv, g_ffn)
    dmerged = _matmul(dmix, w_o, tb=True, name="mm_d_merged", tm=SEQ, tn=512, tk=D)
    dya, dyb, dga, dgb = _merge_bwd(dmerged, ya, yb, proj)
    dya_h = _matmul(dya, w_a, tb=True, name="mm_d_ya", tm=SEQ, tn=FOX_W, tk=D)
    dyb_h = _matmul(dyb, w_b, tb=True, name="mm_d_yb", tm=SEQ, tn=DIL_OUT_W, tk=D)

    dqa, dka, dva, dF = _fox_bwd(proj, dya_h, ya_h, lse_a, f_bcast, f_row)
    dF_row = jnp.pad(dF[:, :2, :].reshape(N_FOX_HEADS, SEQ), ((0, LANES - N_FOX_HEADS), (0, 0)))
    df, db_fgate = _fox_gate_bwd(dF_row, proj, b_pad)

    delta_b = _dil_delta(dyb_h, yb_h)
    rep = lambda t: _to_classes(jnp.tile(t, (1, N_GROUPS)))
    dq_c, dk_c, dv_c = _dil_bwd(q_c, k_c, v_c, rep(dyb_h), rep(lse_b), rep(delta_b))
    dqb, dkb = _rope_bwd(_from_classes(dq_c), _from_classes(dk_c), tables)
    dvb = _from_classes(dv_c).astype(bf16)

    zpad = jnp.zeros((SEQ, C_QB - 2 * D), bf16)
    dproj = jnp.concatenate([dga, dgb, zpad, dqb, dkb, dvb, dqa, dka, dva, df], axis=1)
    dh1 = _matmul(dproj, w_in_p, tb=True, name="mm_d_h1", tm=SEQ // 2, tn=D, tk=896)
    grad_x, dsh_m, dsc_m, dg_mix = _first_bwd(dh1, x2d, dx1, modv, g_mix)

    grads = {
        "w_ffn_down": _matmul(act, dff, ta=True, name="mm_g_down", tm=D_FF // 2, tn=512, tk=SEQ),
        "w_out": _matmul(merged, dmix, ta=True, name="mm_g_out", tm=D, tn=512, tk=SEQ),
        "w_br_a": _matmul(ya_h, dya, ta=True, name="mm_g_br_a", tm=FOX_W, tn=512, tk=SEQ),
        "w_br_b": _matmul(yb_h, dyb, ta=True, name="mm_g_br_b", tm=DIL_OUT_W, tn=512, tk=SEQ),
        "w_in": _unpad_w_in(_matmul(h1, dproj, ta=True, name="mm_g_in", tm=D, tn=896, tk=SEQ)),
    }
    g_gu = _matmul(h2, dau, ta=True, name="mm_g_ffn_in", tm=D, tn=512, tk=SEQ)
    grads["w_ffn_gate"], grads["w_ffn_up"] = g_gu[:, :D_FF], g_gu[:, D_FF:]

    by_dev = _pack_by_device(grads)
    core = pc.astype(jnp.int32).reshape(1)
    chip = (2 * px + py).astype(jnp.int32).reshape(1)
    partial_sums = _pair_add(by_dev, _pair_exchange(by_dev), core)
    g_shard = _unpack_shard(_chip_add(partial_sums, _chip_exchange(partial_sums), chip))

    pad_lane = lambda t: jnp.pad(t, ((0, 0), (0, D - t.shape[1])))
    small = jnp.concatenate([dsh_m, dsc_m, dga_m, dsh_f, dsc_f, dga_f, dg_mix, dg_ffn, dg_final,
                             pad_lane(db_fgate), loss_lanes, jnp.zeros((SMALL_ROWS - 11, D), f32)], axis=0)
    small_all = _all_gather(small, "gather_small")
    small_sum, loss_row = _small_reduce(small_all)
    dmod_all = small_all[:, :6, :].reshape(N_DEV, 6 * D)
    g_w_ada = _ada_bwd(c_all, lax.dynamic_slice(dmod_all, (0, dev * ada_cols), (N_DEV, ada_cols)))

    loss = loss_row[0, 0]
    g = {
        "w_ada": g_w_ada[None], "b_ada": small_sum[0:6].reshape(1, 6 * D), "g_mix": small_sum[6:7],
        "w_in": g_shard["w_in"][None], "b_fgate": small_sum[9:10, :N_FOX_HEADS], "w_br_a": g_shard["w_br_a"][None],
        "w_br_b": g_shard["w_br_b"][None], "w_out": g_shard["w_out"][None], "g_ffn": small_sum[7:8],
        "w_ffn_gate": g_shard["w_ffn_gate"][None], "w_ffn_up": g_shard["w_ffn_up"][None],
        "w_ffn_down": g_shard["w_ffn_down"][None], "g_final": small_sum[8],
    }
    w = {"w_ada": w_ada, "b_ada": b_ada, "g_mix": g_mix, "w_in": w_in, "b_fgate": b_fgate, "w_br_a": w_br_a,
         "w_br_b": w_br_b, "w_out": w_out, "g_ffn": g_ffn, "w_ffn_gate": w_ffn_gate, "w_ffn_up": w_ffn_up,
         "w_ffn_down": w_ffn_down, "g_final": g_final}
    m = {"w_ada": m_w_ada, "b_ada": m_b_ada, "g_mix": m_g_mix, "w_in": m_w_in, "b_fgate": m_b_fgate,
         "w_br_a": m_w_br_a, "w_br_b": m_w_br_b, "w_out": m_w_out, "g_ffn": m_g_ffn, "w_ffn_gate": m_w_ffn_gate,
         "w_ffn_up": m_w_ffn_up, "w_ffn_down": m_w_ffn_down, "g_final": m_g_final}
    v = {"w_ada": v_w_ada, "b_ada": v_b_ada, "g_mix": v_g_mix, "w_in": v_w_in, "b_fgate": v_b_fgate,
         "w_br_a": v_w_br_a, "w_br_b": v_w_br_b, "w_out": v_w_out, "g_ffn": v_g_ffn, "w_ffn_gate": v_w_ffn_gate,
         "w_ffn_up": v_w_ffn_up, "w_ffn_down": v_w_ffn_down, "g_final": v_g_final}
    names = list(w)
    delta, new_m, new_v = {}, {}, {}
    for n in names:
        shape = w[n].shape
        two_d = (lambda t: t.reshape(shape[-2:])) if len(shape) == 3 else (lambda t: t)
        dl, mn, vn = _adamw(two_d(w[n]), two_d(g[n]), two_d(m[n]), two_d(v[n]), "adamw_" + n)
        delta[n], new_m[n], new_v[n] = dl.reshape(shape), mn.reshape(shape), vn.reshape(shape)

    return (loss, grad_x[None], *[g[n] for n in names], *[delta[n] for n in names],
            *[new_m[n] for n in names], *[new_v[n] for n in names])
```

```python
import functools

import jax
import jax.numpy as jnp
from jax import lax
from jax.experimental import pallas as pl
from jax.experimental.pallas import tpu as pltpu

f32 = jnp.float32
bf16 = jnp.bfloat16
SDS = jax.ShapeDtypeStruct
MESH = pl.DeviceIdType.MESH

N_DEV = 8
D = 1024
SEQ = 2048
HEAD_DIM = 64
N_FOX_HEADS = 8
FOX_W = 512
DIL_W = 768
DIL_OUT_W = 256
ROT_DIM = 16
ROPE_THETA = 500000.0
D_FF = 2816
IN_COLS = 5896
EPS = 1e-6
NEG = -1e30
ATT_SCALE = HEAD_DIM ** -0.5

ADAM_LR = 0.001
ADAM_B1 = 0.9
ADAM_B2 = 0.999
ADAM_EPS = 1e-08
ADAM_WD = 0.01
ADAM_STEP = 10

C_GA, C_GB, C_QB, C_KB, C_VB, C_QA, C_KA, C_VA, C_F = 0, 1024, 2304, 3072, 3840, 4608, 5120, 5632, 6144
PROJ_W = 6272
LANES = 128
VMEM_LIMIT = 52 * 1024 * 1024

W_IN_SH, W_IN_PAD = IN_COLS // N_DEV, 768
W_BR_SH = D // N_DEV
W_FF_SH, FF_PAD = D_FF // N_DEV, 384
FF_HID = N_DEV * FF_PAD
SMALL_ROWS = 16


def _params(sem=None):
    if sem is None:
        return pltpu.CompilerParams(vmem_limit_bytes=VMEM_LIMIT)
    return pltpu.CompilerParams(dimension_semantics=sem, vmem_limit_bytes=VMEM_LIMIT)


def _rowwise(fn, name, tiled, vecs, outs, reds=(), tile=256):
    nt, nv, no = len(tiled), len(vecs), len(outs)
    rows = tiled[0][0].shape[0]
    assert rows % tile == 0

    def body(*refs):
        tin = [r[...] for r in refs[:nt]]
        vin = [r[...] for r in refs[nt:nt + nv]]
        orefs = refs[nt + nv:nt + nv + no]
        rrefs = refs[nt + nv + no:]
        touts, routs = fn(tin, vin)
        for r, t in zip(orefs, touts, strict=True):
            r[...] = t.astype(r.dtype)
        if rrefs:
            @pl.when(pl.program_id(0) == 0)
            def _():
                for r in rrefs:
                    r[...] = jnp.zeros_like(r)
            for r, t in zip(rrefs, routs, strict=True):
                r[...] += t

    def col_map(cb):
        return lambda i: (i, cb)

    def whole_map(nd):
        return lambda i: (0,) * nd

    in_specs = [pl.BlockSpec((tile, w), col_map(cb)) for (_, w, cb) in tiled]
    in_specs += [pl.BlockSpec(v.shape, whole_map(v.ndim)) for v in vecs]
    out_specs = [pl.BlockSpec((tile, w), lambda i: (i, 0)) for (w, _) in outs]
    out_specs += [pl.BlockSpec((1, w), lambda i: (0, 0)) for w in reds]
    out_shape = [SDS((rows, w), dt) for (w, dt) in outs] + [SDS((1, w), f32) for w in reds]
    res = pl.pallas_call(
        body, grid=(rows // tile,), in_specs=in_specs, out_specs=out_specs, out_shape=out_shape, name=name,
        compiler_params=_params(("arbitrary",)),
    )(*[t[0] for t in tiled], *vecs)
    return res


def _matmul(a, b, *, ta=False, tb=False, out_dtype=f32, name, tm, tn, tk, by_shard=False):
    m, k = (a.shape[1], a.shape[0]) if ta else a.shape
    if by_shard and not ta:
        n, kb = (b.shape[1], N_DEV * b.shape[2]) if tb else (N_DEV * b.shape[2], b.shape[1])
        assert (tk if tb else tn) == b.shape[2]
    else:
        n, kb = (b.shape[0], b.shape[1]) if tb else (b.shape[1], b.shape[0])
    assert kb == k and m % tm == 0 and n % tn == 0 and k % tk == 0
    nk = k // tk
    dims = (((0 if ta else 1,), (1 if tb else 0,)), ((), ()))
    b_stacked = by_shard and not ta
    o_stacked = by_shard and ta

    def body(a_ref, b_ref, o_ref, *acc):
        bv = b_ref[0] if b_stacked else b_ref[...]
        p = lax.dot_general(a_ref[...].astype(bf16), bv.astype(bf16), dims, preferred_element_type=f32)

        def put(val):
            if o_stacked:
                o_ref[0] = val.astype(o_ref.dtype)
            else:
                o_ref[...] = val.astype(o_ref.dtype)

        if nk == 1:
            put(p)
        else:
            acc_ref, = acc
            kk = pl.program_id(2)

            @pl.when(kk == 0)
            def _():
                acc_ref[...] = p

            @pl.when(kk > 0)
            def _():
                acc_ref[...] += p

            @pl.when(kk == nk - 1)
            def _():
                put(acc_ref[...])

    a_spec = pl.BlockSpec((tk, tm), lambda i, j, kk: (kk, i)) if ta else pl.BlockSpec((tm, tk), lambda i, j, kk: (i, kk))
    if b_stacked and tb:
        b_spec = pl.BlockSpec((1, tn, tk), lambda i, j, kk: (kk, j, 0))
    elif b_stacked:
        b_spec = pl.BlockSpec((1, tk, tn), lambda i, j, kk: (j, kk, 0))
    elif tb:
        b_spec = pl.BlockSpec((tn, tk), lambda i, j, kk: (j, kk))
    else:
        b_spec = pl.BlockSpec((tk, tn), lambda i, j, kk: (kk, j))
    if o_stacked:
        assert tn == n // N_DEV
        out_spec = pl.BlockSpec((1, tm, tn), lambda i, j, kk: (j, i, 0))
        out_shape = SDS((N_DEV, m, tn), out_dtype)
    else:
        out_spec = pl.BlockSpec((tm, tn), lambda i, j, kk: (i, j))
        out_shape = SDS((m, n), out_dtype)
    return pl.pallas_call(
        body, grid=(m // tm, n // tn, nk), in_specs=[a_spec, b_spec], out_specs=out_spec, out_shape=out_shape,
        name=name, scratch_shapes=[pltpu.VMEM((tm, tn), f32)] if nk > 1 else [],
        compiler_params=_params(("parallel", "parallel", "arbitrary")),
    )(a, b)


def _rms(x):
    r = lax.rsqrt(jnp.mean(x * x, axis=-1, keepdims=True) + EPS)
    return r, x * r


def _rms_bwd(r, xn, dxn):
    return r * (dxn - xn * jnp.mean(dxn * xn, axis=-1, keepdims=True))


def _colsum(t):
    return jnp.sum(t, axis=0, keepdims=True)


def _sigmoid(x):
    return 1.0 / (1.0 + jnp.exp(-x))


def _modulated_norm(x, g, shift, scale):
    _, xn = _rms(x)
    return (xn * g) * (1.0 + scale) + shift


def _pre1(x, modv, g_mix):
    def fn(t, v):
        (xt,), (mv, g) = t, v
        return [_modulated_norm(xt, g, mv[0:1], mv[1:2])], []
    return _rowwise(fn, "pre1", [(x, D, 0)], [modv, g_mix], [(D, bf16)])[0]


def _post1(x, mix, modv, g_ffn):
    def fn(t, v):
        (xt, mt), (mv, g) = t, v
        x1 = xt + mv[2:3] * mt
        return [x1, _modulated_norm(x1, g, mv[3:4], mv[4:5])], []
    return _rowwise(fn, "post1", [(x, D, 0), (mix, D, 0)], [modv, g_ffn], [(D, f32), (D, bf16)])


def _gate_up(au, j):
    base = 2 * j * FF_PAD
    return au[:, base:base + FF_PAD], au[:, base + FF_PAD:base + 2 * FF_PAD]


def _swiglu_fwd(au):
    def fn(t, v):
        acts = []
        for j in range(N_DEV):
            a, u = _gate_up(t[0], j)
            acts.append(a * _sigmoid(a) * u)
        return [jnp.concatenate(acts, axis=1)], []
    return _rowwise(fn, "swiglu_fwd", [(au, 2 * FF_HID, 0)], [], [(FF_HID, bf16)])[0]


def _swiglu_bwd(au, dact):
    def fn(t, v):
        parts = []
        for j in range(N_DEV):
            a, u = _gate_up(t[0], j)
            d = t[1][:, j * FF_PAD:(j + 1) * FF_PAD]
            sg = _sigmoid(a)
            parts += [d * u * (sg * (1.0 + a * (1.0 - sg))), d * (a * sg)]
        return [jnp.concatenate(parts, axis=1)], []
    return _rowwise(fn, "swiglu_bwd", [(au, 2 * FF_HID, 0), (dact, FF_HID, 0)], [], [(2 * FF_HID, bf16)],
                    tile=128)[0]


def _final(x1, ff, target, modv, g_final):
    def fn(t, v):
        (x1t, fft, tgt), (mv, g) = t, v
        x2 = x1t + mv[5:6] * fft
        r, xn = _rms(x2)
        err = xn * g - tgt
        dy = err * (1.0 / D)
        dx2 = _rms_bwd(r, xn, dy * g)
        return [dx2, dx2 * mv[5:6]], [_colsum(dy * xn), _colsum(dx2 * fft), _colsum(err * err) * (0.5 / D)]
    return _rowwise(fn, "final", [(x1, D, 0), (ff, D, 0), (target, D, 0)], [modv, g_final],
                    [(D, f32), (D, bf16)], [D, D, D])


def _mid_bwd(dh2, x1, dx2, mix, modv, g_ffn):
    def fn(t, v):
        (dh, x1t, dx2t, mt), (mv, g) = t, v
        r, xn = _rms(x1t)
        dn = dh * (1.0 + mv[4:5])
        dx1 = dx2t + _rms_bwd(r, xn, dn * g)
        return [dx1, dx1 * mv[2:3]], [_colsum(dh), _colsum(dh * (xn * g)), _colsum(dn * xn), _colsum(dx1 * mt)]
    return _rowwise(fn, "mid_bwd", [(dh2, D, 0), (x1, D, 0), (dx2, D, 0), (mix, D, 0)], [modv, g_ffn],
                    [(D, f32), (D, bf16)], [D, D, D, D])


def _first_bwd(dh1, x, dx1, modv, g_mix):
    def fn(t, v):
        (dh, xt, dx1t), (mv, g) = t, v
        r, xn = _rms(xt)
        dn = dh * (1.0 + mv[1:2])
        return [dx1t + _rms_bwd(r, xn, dn * g)], [_colsum(dh), _colsum(dh * (xn * g)), _colsum(dn * xn)]
    return _rowwise(fn, "first_bwd", [(dh1, D, 0), (x, D, 0), (dx1, D, 0)], [modv, g_mix], [(D, f32)], [D, D, D])


def _merge_fwd(ya, yb, proj):
    def fn(t, v):
        ya_t, yb_t, ga, gb = t
        return [_sigmoid(ga) * ya_t + _sigmoid(gb) * yb_t], []
    return _rowwise(fn, "merge_fwd", [(ya, D, 0), (yb, D, 0), (proj, D, C_GA // D), (proj, D, C_GB // D)], [],
                    [(D, bf16)])[0]


def _merge_bwd(dmerged, ya, yb, proj):
    def fn(t, v):
        dm, ya_t, yb_t, ga, gb = t
        sa, sb = _sigmoid(ga), _sigmoid(gb)
        return [dm * sa, dm * sb, dm * ya_t * (sa * (1.0 - sa)), dm * yb_t * (sb * (1.0 - sb))], []
    return _rowwise(fn, "merge_bwd",
                    [(dmerged, D, 0), (ya, D, 0), (yb, D, 0), (proj, D, C_GA // D), (proj, D, C_GB // D)], [],
                    [(D, bf16), (D, bf16), (D, bf16), (D, bf16)])


def _rope_tables():
    half = ROT_DIM // 2
    pos = jnp.arange(SEQ, dtype=f32)
    inv_freq = ROPE_THETA ** (-jnp.arange(0, ROT_DIM, 2, dtype=f32) / ROT_DIM)
    ang = pos[:, None] * inv_freq[None, :]
    cos, sin = jnp.cos(ang), jnp.sin(ang)
    pad = jnp.zeros((SEQ, HEAD_DIM - ROT_DIM), f32)
    zero = jnp.zeros((SEQ, half), f32)
    c_head = jnp.concatenate([cos, cos, pad + 1.0], axis=1)
    lo_head = jnp.concatenate([-sin, zero, pad], axis=1)
    hi_head = jnp.concatenate([zero, sin, pad], axis=1)
    reps = DIL_W // HEAD_DIM
    return tuple(jnp.tile(t, (1, reps)) for t in (c_head, lo_head, hi_head))


def _rope_fwd(proj, tables):
    half = ROT_DIM // 2

    def fn(t, v):
        q, k, vv, c, lo, hi = t
        rot = lambda z: z * c + pltpu.roll(z, DIL_W - half, 1) * lo + pltpu.roll(z, half, 1) * hi
        return [rot(q), rot(k), vv], []
    return _rowwise(fn, "rope_fwd", [(proj, DIL_W, C_QB // DIL_W), (proj, DIL_W, C_KB // DIL_W),
                                     (proj, DIL_W, C_VB // DIL_W)] + [(tb, DIL_W, 0) for tb in tables], [],
                    [(DIL_W, bf16)] * 3)


def _rope_bwd(dq, dk, tables):
    half = ROT_DIM // 2

    def fn(t, v):
        dq_t, dk_t, c, lo, hi = t
        rot_t = lambda z: z * c + pltpu.roll(z * lo, half, 1) + pltpu.roll(z * hi, DIL_W - half, 1)
        return [rot_t(dq_t), rot_t(dk_t)], []
    return _rowwise(fn, "rope_bwd", [(dq, DIL_W, 0), (dk, DIL_W, 0)] + [(tb, DIL_W, 0) for tb in tables], [],
                    [(DIL_W, bf16), (DIL_W, bf16)])


def _head_bcast_sum(d):
    lane = lax.broadcasted_iota(jnp.int32, d.shape, 1)
    out = jnp.zeros_like(d)
    for h in range(d.shape[1] // HEAD_DIM):
        sel = (lane >= h * HEAD_DIM) & (lane < (h + 1) * HEAD_DIM)
        out = jnp.where(sel, jnp.sum(jnp.where(sel, d, 0.0), axis=1, keepdims=True), out)
    return out


def _dil_combine(o, lse):
    def fn(t, v):
        o0, o1, o2, l0, l1, l2 = t
        m = jnp.maximum(jnp.maximum(l0, l1), l2)
        w0, w1, w2 = jnp.exp(l0 - m), jnp.exp(l1 - m), jnp.exp(l2 - m)
        tot = w0 + w1 + w2
        return [(w0 * o0 + w1 * o1 + w2 * o2) / tot, m + jnp.log(tot)], []
    w = DIL_OUT_W
    return _rowwise(fn, "dil_combine", [(o, w, 0), (o, w, 1), (o, w, 2), (lse, w, 0), (lse, w, 1), (lse, w, 2)], [],
                    [(w, f32), (w, f32)])


def _dil_delta(dyb_h, yb_h):
    def fn(t, v):
        return [_head_bcast_sum(t[0] * t[1])], []
    return _rowwise(fn, "dil_delta", [(dyb_h, DIL_OUT_W, 0), (yb_h, DIL_OUT_W, 0)], [], [(DIL_OUT_W, f32)])[0]


def _adamw(w, g, m, v, name):
    shape = w.shape
    if w.ndim == 1:
        w, g, m, v = (t.reshape(1, -1) for t in (w, g, m, v))
    rows, cols = w.shape
    tile = 256 if rows % 256 == 0 and rows > 512 else rows

    def fn(t, _):
        wt, gt, mt, vt = t
        mn = ADAM_B1 * mt + (1.0 - ADAM_B1) * gt
        vn = ADAM_B2 * vt + (1.0 - ADAM_B2) * (gt * gt)
        m_hat = mn / (1.0 - ADAM_B1 ** ADAM_STEP)
        v_hat = vn / (1.0 - ADAM_B2 ** ADAM_STEP)
        return [-ADAM_LR * (m_hat / (jnp.sqrt(v_hat) + ADAM_EPS) + ADAM_WD * wt), mn, vn], []
    delta, mn, vn = _rowwise(fn, name, [(w, cols, 0), (g, cols, 0), (m, cols, 0), (v, cols, 0)], [],
                             [(cols, f32)] * 3, tile=tile)
    return delta.reshape(shape), mn.reshape(shape), vn.reshape(shape)


def _ada_fwd(c_all, w_shard, b_shard):
    def body(c_ref, w_ref, b_ref, o_ref):
        cv = c_ref[...]
        sc = (cv * _sigmoid(cv)).astype(bf16)
        o_ref[...] = jnp.dot(sc, w_ref[...].astype(bf16), preferred_element_type=f32) + b_ref[...]
    return pl.pallas_call(body, out_shape=SDS((N_DEV, w_shard.shape[1]), f32), name="ada_fwd",
                          compiler_params=_params())(c_all, w_shard, b_shard)


def _ada_bwd(c_all, dmod_cols):
    def body(c_ref, d_ref, o_ref):
        cv = c_ref[...]
        sc = cv * _sigmoid(cv)
        o_ref[...] = lax.dot_general(sc, d_ref[...], (((0,), (0,)), ((), ())), precision=lax.Precision.HIGHEST,
                                     preferred_element_type=f32)
    return pl.pallas_call(body, out_shape=SDS((D, dmod_cols.shape[1]), f32), name="ada_bwd",
                          compiler_params=_params())(c_all, dmod_cols)


def _small_reduce(gathered):
    def body(g_ref, o_ref, loss_ref):
        acc = g_ref[0]
        for d in range(1, N_DEV):
            acc = acc + g_ref[d]
        o_ref[...] = acc
        loss_ref[...] = jnp.zeros((1, LANES), f32) + jnp.sum(acc[10:11, :])
    return pl.pallas_call(body, out_shape=(SDS((SMALL_ROWS, D), f32), SDS((1, LANES), f32)), name="small_reduce",
                          compiler_params=_params())(gathered)


FOX_BLK = 256
CUM_BLK = 128


def _fox_gate_fwd(proj, b_pad):
    nblk = SEQ // CUM_BLK

    def body(f_ref, b_ref, col_ref, row_ref):
        r = lax.broadcasted_iota(jnp.int32, (CUM_BLK, CUM_BLK), 0)
        c = lax.broadcasted_iota(jnp.int32, (CUM_BLK, CUM_BLK), 1)
        tri = (r >= c).astype(f32)
        carry = jnp.zeros((1, LANES), f32)
        for blk in range(nblk):
            z = f_ref[blk * CUM_BLK:(blk + 1) * CUM_BLK, :] + b_ref[...]
            logf = jnp.minimum(z, 0.0) - jnp.log1p(jnp.exp(-jnp.abs(z)))
            cs = jnp.dot(tri, logf, precision=lax.Precision.HIGHEST, preferred_element_type=f32) + carry
            col_ref[blk * CUM_BLK:(blk + 1) * CUM_BLK, :] = cs
            carry = cs[CUM_BLK - 1:CUM_BLK, :]
        row_ref[...] = col_ref[...].T

    return pl.pallas_call(
        body, grid=(1,), in_specs=[pl.BlockSpec((SEQ, LANES), lambda i: (0, C_F // LANES)),
                                   pl.BlockSpec((1, LANES), lambda i: (0, 0))],
        out_specs=[pl.BlockSpec((SEQ, LANES), lambda i: (0, 0)), pl.BlockSpec((LANES, SEQ), lambda i: (0, 0))],
        out_shape=(SDS((SEQ, LANES), f32), SDS((LANES, SEQ), f32)), name="fox_gate_fwd",
        compiler_params=_params(("arbitrary",)),
    )(proj, b_pad)


def _fox_gate_bwd(dF_row, proj, b_pad):
    nblk = SEQ // CUM_BLK

    def body(d_ref, f_ref, b_ref, df_ref, db_ref, col_ref):
        r = lax.broadcasted_iota(jnp.int32, (CUM_BLK, CUM_BLK), 0)
        c = lax.broadcasted_iota(jnp.int32, (CUM_BLK, CUM_BLK), 1)
        tri = (r <= c).astype(f32)
        lane = lax.broadcasted_iota(jnp.int32, (CUM_BLK, LANES), 1)
        col_ref[...] = d_ref[...].T
        carry = jnp.zeros((1, LANES), f32)
        total = jnp.zeros((1, LANES), f32)
        for blk in reversed(range(nblk)):
            rows = slice(blk * CUM_BLK, (blk + 1) * CUM_BLK)
            cs = jnp.dot(tri, col_ref[rows, :], precision=lax.Precision.HIGHEST, preferred_element_type=f32) + carry
            carry = cs[0:1, :]
            z = f_ref[rows, :] + b_ref[...]
            df = jnp.where(lane < N_FOX_HEADS, cs * _sigmoid(-z), 0.0)
            df_ref[rows, :] = df.astype(df_ref.dtype)
            total = total + _colsum(df)
        db_ref[...] = total

    return pl.pallas_call(
        body, grid=(1,), in_specs=[pl.BlockSpec((LANES, SEQ), lambda i: (0, 0)),
                                   pl.BlockSpec((SEQ, LANES), lambda i: (0, C_F // LANES)),
                                   pl.BlockSpec((1, LANES), lambda i: (0, 0))],
        out_specs=[pl.BlockSpec((SEQ, LANES), lambda i: (0, 0)), pl.BlockSpec((1, LANES), lambda i: (0, 0))],
        out_shape=(SDS((SEQ, LANES), bf16), SDS((1, LANES), f32)), name="fox_gate_bwd",
        scratch_shapes=[pltpu.VMEM((SEQ, LANES), f32)],
        compiler_params=_params(("arbitrary",)),
    )(dF_row, proj, b_pad)


def _nt(a, b):
    return lax.dot_general(a, b, (((1,), (1,)), ((), ())), preferred_element_type=f32)


def _tn(a, b):
    return lax.dot_general(a, b, (((0,), (0,)), ((), ())), preferred_element_type=f32)


def _fox_fwd(proj, f_bcast, f_row):
    blk = FOX_BLK
    npair = FOX_W // LANES

    def body(q_ref, k_ref, v_ref, fb_ref, fr_ref, o_ref, lse_ref):
        i = pl.program_id(1)
        lane = lax.broadcasted_iota(jnp.int32, (blk, LANES), 1)
        rowi = i * blk + lax.broadcasted_iota(jnp.int32, (blk, blk), 0)
        coli = lax.broadcasted_iota(jnp.int32, (blk, blk), 1)
        q = q_ref[...]
        fb = fb_ref[0]
        out = jnp.zeros((blk, LANES), f32)
        lse = jnp.zeros((blk, LANES), f32)
        for h in range(2):
            hm = (lane < HEAD_DIM) if h == 0 else (lane >= HEAD_DIM)
            qh = jnp.where(hm, q, 0.0).astype(bf16)
            fq = fb[:, h * HEAD_DIM:h * HEAD_DIM + 1]

            def step(j, carry, qh=qh, fq=fq, h=h):
                m, l, acc = carry
                off = pl.multiple_of(j * blk, blk)
                kk = k_ref[pl.ds(off, blk), :].astype(bf16)
                vv = v_ref[pl.ds(off, blk), :].astype(bf16)
                s = _nt(qh, kk) * ATT_SCALE + (fq - fr_ref[0, h:h + 1, pl.ds(off, blk)])
                s = jnp.where(rowi >= coli + off, s, NEG)
                mn = jnp.maximum(m, jnp.max(s, axis=1, keepdims=True))
                p = jnp.exp(s - mn)
                a = jnp.exp(m - mn)
                return mn, a * l + jnp.sum(p, axis=1, keepdims=True), a * acc + jnp.dot(
                    p.astype(bf16), vv, preferred_element_type=f32)

            m, l, acc = lax.fori_loop(0, i + 1, step, (jnp.full((blk, 1), NEG, f32), jnp.zeros((blk, 1), f32),
                                                       jnp.zeros((blk, LANES), f32)))
            out = jnp.where(hm, acc / l, out)
            lse = jnp.where(hm, m + jnp.log(l), lse)
        o_ref[...] = out
        lse_ref[0] = lse

    qa, ka, va = C_QA // LANES, C_KA // LANES, C_VA // LANES
    return pl.pallas_call(
        body, grid=(npair, SEQ // blk),
        in_specs=[pl.BlockSpec((blk, LANES), lambda p, i: (i, qa + p)),
                  pl.BlockSpec((SEQ, LANES), lambda p, i: (0, ka + p)),
                  pl.BlockSpec((SEQ, LANES), lambda p, i: (0, va + p)),
                  pl.BlockSpec((1, blk, LANES), lambda p, i: (p, i, 0)),
                  pl.BlockSpec((1, 8, SEQ), lambda p, i: (p, 0, 0))],
        out_specs=[pl.BlockSpec((blk, LANES), lambda p, i: (i, p)),
                   pl.BlockSpec((1, blk, LANES), lambda p, i: (p, i, 0))],
        out_shape=(SDS((SEQ, FOX_W), f32), SDS((npair, SEQ, LANES), f32)), name="fox_fwd",
        compiler_params=_params(("parallel", "arbitrary")),
    )(proj, proj, proj, f_bcast, f_row)


def _fox_bwd(proj, do, o, lse, f_bcast, f_row):
    blk = FOX_BLK
    npair = FOX_W // LANES
    nblk = SEQ // blk

    def body(q_ref, k_ref, v_ref, do_ref, o_ref, lse_ref, fb_ref, fr_ref, dq_ref, dk_ref, dv_ref, df_ref,
             dq_acc, delta_ref, res_ref):
        lane_s = lax.broadcasted_iota(jnp.int32, (SEQ, LANES), 1)
        prod = do_ref[...] * o_ref[...]
        d_a = jnp.sum(jnp.where(lane_s < HEAD_DIM, prod, 0.0), axis=1, keepdims=True)
        d_b = jnp.sum(jnp.where(lane_s >= HEAD_DIM, prod, 0.0), axis=1, keepdims=True)
        delta_ref[...] = jnp.where(lane_s < HEAD_DIM, d_a, d_b)
        dq_acc[...] = jnp.zeros_like(dq_acc)
        res_ref[...] = jnp.zeros_like(res_ref)
        df_ref[...] = jnp.zeros_like(df_ref)
        lane = lax.broadcasted_iota(jnp.int32, (blk, LANES), 1)
        rowi = lax.broadcasted_iota(jnp.int32, (blk, blk), 0)
        coli = lax.broadcasted_iota(jnp.int32, (blk, blk), 1)

        def probs(qq, kh, fk, col, qoff, koff):
            s = _nt(qq, kh) * ATT_SCALE + (fb_ref[0, pl.ds(qoff, blk), col:col + 1] - fk)
            s = jnp.where(rowi + qoff >= coli + koff, s, NEG)
            return jnp.exp(s - lse_ref[0, pl.ds(qoff, blk), col:col + 1])

        def kv_step(kj, _):
            koff = pl.multiple_of(kj * blk, blk)
            kf = k_ref[pl.ds(koff, blk), :]
            vf = v_ref[pl.ds(koff, blk), :]
            dk_out = jnp.zeros((blk, LANES), f32)
            dv_out = jnp.zeros((blk, LANES), f32)
            for h in range(2):
                hm = (lane < HEAD_DIM) if h == 0 else (lane >= HEAD_DIM)
                kh = jnp.where(hm, kf, 0.0).astype(bf16)
                vh = jnp.where(hm, vf, 0.0).astype(bf16)
                fk = fr_ref[0, h:h + 1, pl.ds(koff, blk)]
                col = h * HEAD_DIM

                def q_step(qi, carry, kh=kh, vh=vh, fk=fk, col=col, hm=hm):
                    dk_h, dv_h, dcol = carry
                    qoff = pl.multiple_of(qi * blk, blk)
                    qq = q_ref[pl.ds(qoff, blk), :].astype(bf16)
                    dd = do_ref[pl.ds(qoff, blk), :].astype(bf16)
                    p = probs(qq, kh, fk, col, qoff, koff)
                    dl = p * (_nt(dd, vh) - delta_ref[pl.ds(qoff, blk), col:col + 1])
                    dlb = dl.astype(bf16)
                    dq_acc[pl.ds(qoff, blk), :] += jnp.dot(dlb, kh, preferred_element_type=f32) * ATT_SCALE
                    res_ref[pl.ds(qoff, blk), :] += jnp.where(hm, jnp.sum(dl, axis=1, keepdims=True), 0.0)
                    return dk_h + _tn(dlb, qq), dv_h + _tn(p.astype(bf16), dd), dcol + _colsum(dl)

                dk_h, dv_h, dcol = lax.fori_loop(kj, nblk, q_step, (jnp.zeros((blk, LANES), f32),
                                                                   jnp.zeros((blk, LANES), f32),
                                                                   jnp.zeros((1, blk), f32)))
                dk_out = jnp.where(hm, dk_h * ATT_SCALE, dk_out)
                dv_out = jnp.where(hm, dv_h, dv_out)
                df_ref[0, h:h + 1, pl.ds(koff, blk)] = -dcol
            dk_ref[pl.ds(koff, blk), :] = dk_out.astype(dk_ref.dtype)
            dv_ref[pl.ds(koff, blk), :] = dv_out.astype(dv_ref.dtype)
            return 0

        lax.fori_loop(0, nblk, kv_step, 0)
        dq_ref[...] = dq_acc[...].astype(dq_ref.dtype)

        def kv_fix(kj, _):
            koff = pl.multiple_of(kj * blk, blk)
            kf = k_ref[pl.ds(koff, blk), :]
            for h in range(2):
                hm = (lane < HEAD_DIM) if h == 0 else (lane >= HEAD_DIM)
                kh = jnp.where(hm, kf, 0.0).astype(bf16)
                fk = fr_ref[0, h:h + 1, pl.ds(koff, blk)]
                col = h * HEAD_DIM

                def q_fix(qi, corr, kh=kh, fk=fk, col=col):
                    qoff = pl.multiple_of(qi * blk, blk)
                    p = probs(q_ref[pl.ds(qoff, blk), :].astype(bf16), kh, fk, col, qoff, koff)
                    return corr + _colsum(p * res_ref[pl.ds(qoff, blk), col:col + 1])

                df_ref[0, h:h + 1, pl.ds(koff, blk)] += lax.fori_loop(kj, nblk, q_fix, jnp.zeros((1, blk), f32))
            return 0

        lax.fori_loop(0, nblk, kv_fix, 0)

    qa, ka, va = C_QA // LANES, C_KA // LANES, C_VA // LANES
    slab = lambda base: pl.BlockSpec((SEQ, LANES), lambda p: (0, base + p))
    per_pair = pl.BlockSpec((1, SEQ, LANES), lambda p: (p, 0, 0))
    rows = pl.BlockSpec((1, 8, SEQ), lambda p: (p, 0, 0))
    return pl.pallas_call(
        body, grid=(npair,),
        in_specs=[slab(qa), slab(ka), slab(va), slab(0), slab(0), per_pair, per_pair, rows],
        out_specs=[slab(0), slab(0), slab(0), rows],
        out_shape=(SDS((SEQ, FOX_W), bf16),) * 3 + (SDS((npair, 8, SEQ), f32),), name="fox_bwd",
        scratch_shapes=[pltpu.VMEM((SEQ, LANES), f32)] * 3,
        compiler_params=_params(("parallel",)),
    )(proj, proj, proj, do, o, lse, f_bcast, f_row)


DIL_BLK = 128
N_GROUPS = 3
DIL_PAIRS = DIL_OUT_W // LANES
DIL_NBLK = SEQ // DIL_BLK


def _blocks_per_seq(g):
    return jnp.where(g == 0, 16, jnp.where(g == 1, 4, 1))


def _dil_masks(has_prev):
    r = lax.broadcasted_iota(jnp.int32, (DIL_BLK, DIL_BLK), 0)
    c = lax.broadcasted_iota(jnp.int32, (DIL_BLK, DIL_BLK), 1)
    return r >= c, (c >= r) & has_prev


def _dil_specs():
    cur = pl.BlockSpec((1, DIL_BLK, LANES), lambda g, p, n: (g, n, p))
    prev = pl.BlockSpec((1, DIL_BLK, LANES), lambda g, p, n: (g, jnp.maximum(n - 1, 0), p))
    return cur, prev


def _dil_fwd(q, k, v):
    def body(q_ref, kc_ref, kp_ref, vc_ref, vp_ref, o_ref, lse_ref):
        g, n = pl.program_id(0), pl.program_id(2)
        has_prev = (n & (_blocks_per_seq(g) - 1)) > 0
        mc, mp = _dil_masks(has_prev)
        lane = lax.broadcasted_iota(jnp.int32, (DIL_BLK, LANES), 1)
        qv, kc, kp, vc, vp = q_ref[0], kc_ref[0], kp_ref[0], vc_ref[0], vp_ref[0]
        out = jnp.zeros((DIL_BLK, LANES), f32)
        lse = jnp.zeros((DIL_BLK, LANES), f32)
        for h in range(2):
            hm = (lane < HEAD_DIM) if h == 0 else (lane >= HEAD_DIM)
            qh = jnp.where(hm, qv, jnp.zeros_like(qv))
            sc = jnp.where(mc, _nt(qh, kc) * ATT_SCALE, NEG)
            sp = jnp.where(mp, _nt(qh, kp) * ATT_SCALE, NEG)
            m = jnp.maximum(jnp.max(sc, axis=1, keepdims=True), jnp.max(sp, axis=1, keepdims=True))
            pc, pp = jnp.exp(sc - m), jnp.exp(sp - m)
            l = jnp.sum(pc, axis=1, keepdims=True) + jnp.sum(pp, axis=1, keepdims=True)
            acc = jnp.dot(pc.astype(bf16), vc, preferred_element_type=f32) + jnp.dot(
                pp.astype(bf16), vp, preferred_element_type=f32)
            out = jnp.where(hm, acc / l, out)
            lse = jnp.where(hm, m + jnp.log(l), lse)
        o_ref[0] = out
        lse_ref[0] = lse

    cur, prev = _dil_specs()
    shape = SDS((N_GROUPS, SEQ, DIL_OUT_W), f32)
    return pl.pallas_call(
        body, grid=(N_GROUPS, DIL_PAIRS, DIL_NBLK), in_specs=[cur, cur, prev, cur, prev], out_specs=[cur, cur],
        out_shape=(shape, shape), name="dil_fwd", compiler_params=_params(("parallel", "parallel", "arbitrary")),
    )(q, k, k, v, v)


def _dil_bwd(q, k, v, do, lse, delta):
    def body(q_ref, kc_ref, kp_ref, vc_ref, vp_ref, do_ref, lse_ref, dl_ref, dq_ref, dk_ref, dv_ref):
        g, n = pl.program_id(0), pl.program_id(2)
        has_prev = (n & (_blocks_per_seq(g) - 1)) > 0
        mc, mp = _dil_masks(has_prev)
        lane = lax.broadcasted_iota(jnp.int32, (DIL_BLK, LANES), 1)
        qv, kc, kp, vc, vp = q_ref[0], kc_ref[0], kp_ref[0], vc_ref[0], vp_ref[0]
        dov = do_ref[0].astype(bf16)
        lsev, delv = lse_ref[0], dl_ref[0]
        dq = jnp.zeros((DIL_BLK, LANES), f32)
        dkc = jnp.zeros((DIL_BLK, LANES), f32)
        dkp = jnp.zeros((DIL_BLK, LANES), f32)
        dvc = jnp.zeros((DIL_BLK, LANES), f32)
        dvp = jnp.zeros((DIL_BLK, LANES), f32)
        for h in range(2):
            hm = (lane < HEAD_DIM) if h == 0 else (lane >= HEAD_DIM)
            col = h * HEAD_DIM
            qh = jnp.where(hm, qv, jnp.zeros_like(qv))
            doh = jnp.where(hm, dov, jnp.zeros_like(dov))
            lh, dh = lsev[:, col:col + 1], delv[:, col:col + 1]
            pc = jnp.exp(jnp.where(mc, _nt(qh, kc) * ATT_SCALE, NEG) - lh)
            pp = jnp.exp(jnp.where(mp, _nt(qh, kp) * ATT_SCALE, NEG) - lh)
            dlc = (pc * (_nt(doh, vc) - dh)).astype(bf16)
            dlp = (pp * (_nt(doh, vp) - dh)).astype(bf16)
            dq = jnp.where(hm, jnp.dot(dlc, kc, preferred_element_type=f32)
                           + jnp.dot(dlp, kp, preferred_element_type=f32), dq)
            dkc = jnp.where(hm, _tn(dlc, qh), dkc)
            dkp = jnp.where(hm, _tn(dlp, qh), dkp)
            dvc = jnp.where(hm, _tn(pc.astype(bf16), doh), dvc)
            dvp = jnp.where(hm, _tn(pp.astype(bf16), doh), dvp)
        dq_ref[0] = dq * ATT_SCALE
        off = pl.multiple_of(n * DIL_BLK, DIL_BLK)
        dk_ref[0, pl.ds(off, DIL_BLK), :] = dkc * ATT_SCALE
        dv_ref[0, pl.ds(off, DIL_BLK), :] = dvc

        @pl.when(has_prev)
        def _():
            poff = pl.multiple_of((n - 1) * DIL_BLK, DIL_BLK)
            dk_ref[0, pl.ds(poff, DIL_BLK), :] += dkp * ATT_SCALE
            dv_ref[0, pl.ds(poff, DIL_BLK), :] += dvp

    cur, prev = _dil_specs()
    whole = pl.BlockSpec((1, SEQ, LANES), lambda g, p, n: (g, 0, p))
    shape = SDS((N_GROUPS, SEQ, DIL_OUT_W), f32)
    return pl.pallas_call(
        body, grid=(N_GROUPS, DIL_PAIRS, DIL_NBLK), in_specs=[cur, cur, prev, cur, prev, cur, cur, cur],
        out_specs=[cur, whole, whole], out_shape=(shape, shape, shape), name="dil_bwd",
        compiler_params=_params(("parallel", "parallel", "arbitrary")),
    )(q, k, k, v, v, do, lse, delta)


_DILATIONS = (1, 4, 16)


def _to_classes(t):
    w = t.shape[1] // N_GROUPS
    out = []
    for g, d in enumerate(_DILATIONS):
        s = t[:, g * w:(g + 1) * w]
        out.append(s.reshape(SEQ // d, d, w).transpose(1, 0, 2).reshape(SEQ, w))
    return jnp.stack(out)


def _from_classes(t):
    w = t.shape[2]
    out = [t[g].reshape(d, SEQ // d, w).transpose(1, 0, 2).reshape(SEQ, w) for g, d in enumerate(_DILATIONS)]
    return jnp.concatenate(out, axis=1)


def _position():
    return lax.axis_index("x"), lax.axis_index("y"), lax.axis_index("c")


def _all_gather(block, name):
    def body(x_ref, out_ref, send_sems, recv_sems, local_sem):
        x, y, c = _position()
        me, sibling = (x, y, c), (x, y, 1 - c)
        chips = [(1 - x, y), (x, 1 - y), (1 - x, 1 - y)]

        def slot(px, py, pc):
            return out_ref.at[4 * px + 2 * py + pc]

        def copy(k, blk, to, src=None):
            return pltpu.make_async_remote_copy(
                src_ref=slot(*blk) if src is None else src, dst_ref=slot(*blk),
                send_sem=send_sems.at[k], recv_sem=recv_sems.at[k], device_id=to, device_id_type=MESH)

        mine = pltpu.make_async_copy(x_ref, slot(*me), local_sem)
        mine.start()
        first = [copy(0, me, sibling, src=x_ref)]
        first += [copy(1 + j, me, (*chip, c), src=x_ref) for j, chip in enumerate(chips)]
        for cp in first:
            cp.start()
        passed = [copy(4 + j, (*chip, c), sibling) for j, chip in enumerate(chips)]
        for j, chip in enumerate(chips):
            copy(1 + j, (*chip, c), me).wait_recv()
            passed[j].start()
        copy(0, sibling, me).wait_recv()
        for j, chip in enumerate(chips):
            copy(4 + j, (*chip, 1 - c), me).wait_recv()
        for cp in first + passed:
            cp.wait_send()
        mine.wait()

    return pl.pallas_call(
        body, out_shape=SDS((N_DEV,) + block.shape, block.dtype),
        in_specs=[pl.BlockSpec(memory_space=pl.ANY)], out_specs=pl.BlockSpec(memory_space=pl.ANY),
        scratch_shapes=[pltpu.SemaphoreType.DMA((7,)), pltpu.SemaphoreType.DMA((7,)), pltpu.SemaphoreType.DMA],
        name=name,
    )(block)


def _all_gather_many(blocks, name):
    n = len(blocks)

    def body(*refs):
        x_refs, out_refs = refs[:n], refs[n:2 * n]
        send_sems, recv_sems, local_sems = refs[2 * n:]
        x, y, c = _position()
        me, sibling = (x, y, c), (x, y, 1 - c)
        chips = [(1 - x, y), (x, 1 - y), (1 - x, 1 - y)]

        def slot(a, px, py, pc):
            return out_refs[a].at[4 * px + 2 * py + pc]

        def copy(a, k, blk, to, own=False):
            return pltpu.make_async_remote_copy(
                src_ref=x_refs[a] if own else slot(a, *blk), dst_ref=slot(a, *blk),
                send_sem=send_sems.at[a, k], recv_sem=recv_sems.at[a, k], device_id=to, device_id_type=MESH)

        mine = [pltpu.make_async_copy(x_refs[a], slot(a, *me), local_sems.at[a]) for a in range(n)]
        for cp in mine:
            cp.start()
        started = []
        for a in range(n):
            first = [copy(a, 0, me, sibling, own=True)]
            first += [copy(a, 1 + j, me, (*chip, c), own=True) for j, chip in enumerate(chips)]
            for cp in first:
                cp.start()
            started += first
        for a in range(n):
            for j, chip in enumerate(chips):
                copy(a, 1 + j, (*chip, c), me).wait_recv()
                passed = copy(a, 4 + j, (*chip, c), sibling)
                passed.start()
                started.append(passed)
        for a in range(n):
            copy(a, 0, sibling, me).wait_recv()
            for j, chip in enumerate(chips):
                copy(a, 4 + j, (*chip, 1 - c), me).wait_recv()
        for cp in started:
            cp.wait_send()
        for cp in mine:
            cp.wait()

    hbm = pl.BlockSpec(memory_space=pl.ANY)
    return pl.pallas_call(
        body, out_shape=[SDS((N_DEV,) + b.shape, b.dtype) for b in blocks],
        in_specs=[hbm] * n, out_specs=[hbm] * n,
        scratch_shapes=[pltpu.SemaphoreType.DMA((n, 7)), pltpu.SemaphoreType.DMA((n, 7)),
                        pltpu.SemaphoreType.DMA((n,))],
        name=name,
    )(*blocks)


def _pair_exchange(gs):
    n = len(gs)

    def body(*refs):
        g_refs, r_refs, send_sems, recv_sems = refs[:n], refs[n:2 * n], refs[2 * n], refs[2 * n + 1]
        x, y, c = _position()
        copies = []
        for a in range(n):
            for k in range(4):
                cp = pltpu.make_async_remote_copy(
                    src_ref=g_refs[a].at[2 * k + (1 - c)], dst_ref=r_refs[a].at[k], send_sem=send_sems.at[a, k],
                    recv_sem=recv_sems.at[a, k], device_id=(x, y, 1 - c), device_id_type=MESH)
                cp.start()
                copies.append(cp)
        for cp in copies:
            cp.wait()

    hbm = pl.BlockSpec(memory_space=pl.ANY)
    return pl.pallas_call(
        body, out_shape=[SDS((4,) + g.shape[1:], g.dtype) for g in gs], in_specs=[hbm] * n, out_specs=[hbm] * n,
        scratch_shapes=[pltpu.SemaphoreType.DMA((n, 4)), pltpu.SemaphoreType.DMA((n, 4))], name="pair_exchange",
    )(*gs)


def _chip_exchange(ts):
    n = len(ts)

    def body(*refs):
        t_refs, r_refs, send_sems, recv_sems = refs[:n], refs[n:2 * n], refs[2 * n], refs[2 * n + 1]
        x, y, c = _position()
        chips = [(1 - x, y), (x, 1 - y), (1 - x, 1 - y)]
        copies = []
        for a in range(n):
            for j, (px, py) in enumerate(chips):
                cp = pltpu.make_async_remote_copy(
                    src_ref=t_refs[a].at[2 * px + py], dst_ref=r_refs[a].at[j], send_sem=send_sems.at[a, j],
                    recv_sem=recv_sems.at[a, j], device_id=(px, py, c), device_id_type=MESH)
                cp.start()
                copies.append(cp)
        for cp in copies:
            cp.wait()

    hbm = pl.BlockSpec(memory_space=pl.ANY)
    return pl.pallas_call(
        body, out_shape=[SDS((3,) + t.shape[1:], t.dtype) for t in ts], in_specs=[hbm] * n, out_specs=[hbm] * n,
        scratch_shapes=[pltpu.SemaphoreType.DMA((n, 3)), pltpu.SemaphoreType.DMA((n, 3))], name="chip_exchange",
    )(*ts)


def _row_tile(rows):
    return 256 if rows % 256 == 0 and rows > 512 else rows


def _pair_add(g, r1, core, name):
    def body(c_ref, g_ref, r_ref, o_ref):
        o_ref[...] = (g_ref[...].astype(f32) + r_ref[...].astype(f32)).astype(o_ref.dtype)

    rows, cols = g.shape[1:]
    tile = _row_tile(rows)
    blk = (1, tile, cols)
    return pl.pallas_call(
        body, out_shape=SDS((4, rows, cols), g.dtype), name=name,
        grid_spec=pltpu.PrefetchScalarGridSpec(
            num_scalar_prefetch=1, grid=(4, rows // tile),
            in_specs=[pl.BlockSpec(blk, lambda k, i, c_ref: (2 * k + c_ref[0], i, 0)),
                      pl.BlockSpec(blk, lambda k, i, c_ref: (k, i, 0))],
            out_specs=pl.BlockSpec(blk, lambda k, i, c_ref: (k, i, 0))),
        compiler_params=_params(("parallel", "arbitrary")),
    )(core, g, r1)


def _chip_add(t, r2, chip, name):
    def body(c_ref, t_ref, r_ref, o_ref):
        o_ref[...] = ((t_ref[0].astype(f32) + r_ref[0].astype(f32)) + r_ref[1].astype(f32)) + r_ref[2].astype(f32)

    rows, cols = t.shape[1:]
    tile = _row_tile(rows)
    return pl.pallas_call(
        body, out_shape=SDS((rows, cols), f32), name=name,
        grid_spec=pltpu.PrefetchScalarGridSpec(
            num_scalar_prefetch=1, grid=(rows // tile,),
            in_specs=[pl.BlockSpec((1, tile, cols), lambda i, c_ref: (c_ref[0], i, 0)),
                      pl.BlockSpec((3, tile, cols), lambda i, c_ref: (0, i, 0))],
            out_specs=pl.BlockSpec((tile, cols), lambda i, c_ref: (i, 0))),
        compiler_params=_params(("arbitrary",)),
    )(chip, t, r2)


def _pad_to(t, axis, size):
    pads = [(0, 0)] * t.ndim
    pads[axis] = (0, size - t.shape[axis])
    return jnp.pad(t, pads)


def _shard_pad_cols(t):
    rows = t.shape[0]
    return _pad_to(t.reshape(rows, N_DEV, W_IN_SH), 2, W_IN_PAD).reshape(rows, N_DEV * W_IN_PAD)


def _pad_w_in(w):
    qa, ka, va = w[:, 0:512], w[:, 512:1024], w[:, 1024:1536]
    fg = w[:, 1536:1544]
    qb, kb, vb = w[:, 1544:2312], w[:, 2312:3080], w[:, 3080:3848]
    ga, gb = w[:, 3848:4872], w[:, 4872:5896]
    z = lambda n: jnp.zeros((w.shape[0], n), w.dtype)
    return jnp.concatenate([ga, gb, z(C_QB - 2 * D), qb, kb, vb, qa, ka, va, fg, z(LANES - N_FOX_HEADS)], axis=1)


def kernel(x, c, w_ada, b_ada, g_mix, w_in, b_fgate, w_br_a, w_br_b, w_out, g_ffn, w_ffn_gate, w_ffn_up, w_ffn_down, g_final, loss_target, m_w_ada, m_b_ada, m_g_mix, m_w_in, m_b_fgate, m_w_br_a, m_w_br_b, m_w_out, m_g_ffn, m_w_ffn_gate, m_w_ffn_up, m_w_ffn_down, m_g_final, v_w_ada, v_b_ada, v_g_mix, v_w_in, v_b_fgate, v_w_br_a, v_w_br_b, v_w_out, v_g_ffn, v_w_ffn_gate, v_w_ffn_up, v_w_ffn_down, v_g_final):
    px, py, pc = _position()
    dev = 4 * px + 2 * py + pc
    x2d, tgt = x[0], loss_target[0]

    c_all = _all_gather(c, "gather_c").reshape(N_DEV, D)
    ada_cols = w_ada.shape[2]
    b_shard = lax.dynamic_slice(b_ada, (0, dev * ada_cols), (1, ada_cols))
    mod_shard = _ada_fwd(c_all, w_ada[0], b_shard)
    mod_all = _all_gather(mod_shard, "gather_mod")
    modv = lax.dynamic_index_in_dim(mod_all, dev, axis=1, keepdims=False).reshape(6, D)

    gate_up = jnp.concatenate([_pad_to(w_ffn_gate[0], 1, FF_PAD), _pad_to(w_ffn_up[0], 1, FF_PAD)], axis=1)
    shards = [_pad_to(w_in[0], 1, W_IN_PAD), w_br_a[0], w_br_b[0], w_out[0], gate_up, _pad_to(w_ffn_down[0], 0, FF_PAD)]
    w_in_s, w_a_s, w_b_s, w_o_s, w_gu_s, w_d_s = _all_gather_many([t.astype(bf16) for t in shards], "gather_weights")
    w_o = w_o_s.reshape(D, D)
    w_d = w_d_s.reshape(FF_HID, D)
    w_in_p = _pad_w_in(w_in_s[:, :, :W_IN_SH].transpose(1, 0, 2).reshape(D, IN_COLS))

    h1 = _pre1(x2d, modv, g_mix)
    proj = _matmul(h1, w_in_p, name="mm_proj", tm=SEQ, tn=896, tk=D)
    b_pad = jnp.pad(b_fgate, ((0, 0), (0, LANES - N_FOX_HEADS)))
    f_col, f_rowm = _fox_gate_fwd(proj, b_pad)
    npair = FOX_W // LANES
    f_heads = f_rowm[:N_FOX_HEADS].reshape(npair, 2, SEQ)
    f_bcast = jnp.repeat(f_heads, HEAD_DIM, axis=1).transpose(0, 2, 1)
    f_row = jnp.pad(f_heads, ((0, 0), (0, 6), (0, 0)))
    ya_h, lse_a = _fox_fwd(proj, f_bcast, f_row)
    ya = _matmul(ya_h, w_a_s, by_shard=True, name="mm_br_a", tm=SEQ, tn=W_BR_SH, tk=FOX_W)

    tables = _rope_tables()
    qb_r, kb_r, vb = _rope_fwd(proj, tables)
    q_c, k_c, v_c = _to_classes(qb_r), _to_classes(kb_r), _to_classes(vb)
    o_c, lse_c = _dil_fwd(q_c, k_c, v_c)
    yb_h, lse_b = _dil_combine(_from_classes(o_c), _from_classes(lse_c))
    yb = _matmul(yb_h, w_b_s, by_shard=True, name="mm_br_b", tm=SEQ, tn=W_BR_SH, tk=DIL_OUT_W)

    merged = _merge_fwd(ya, yb, proj)
    mix = _matmul(merged, w_o, name="mm_out", tm=SEQ, tn=512, tk=D)
    x1, h2 = _post1(x2d, mix, modv, g_ffn)
    au = _matmul(h2, w_gu_s, by_shard=True, name="mm_ffn_in", tm=SEQ, tn=2 * FF_PAD, tk=D)
    act = _swiglu_fwd(au)
    ff = _matmul(act, w_d, name="mm_ffn_down", tm=SEQ, tn=512, tk=FF_HID // 2)

    dx2, dff, dg_final, dga_f, loss_lanes = _final(x1, ff, tgt, modv, g_final.reshape(1, D))
    dact = _matmul(dff, w_d, tb=True, name="mm_d_act", tm=SEQ // 2, tn=FF_HID // 2, tk=D)
    dau = _swiglu_bwd(au, dact)
    dh2 = _matmul(dau, w_gu_s, tb=True, by_shard=True, name="mm_d_h2", tm=SEQ // 2, tn=D, tk=2 * FF_PAD)
    dx1, dmix, dsh_f, dsc_f, dg_ffn, dga_m = _mid_bwd(dh2, x1, dx2, mix, modv, g_ffn)
    dmerged = _matmul(dmix, w_o, tb=True, name="mm_d_merged", tm=SEQ, tn=512, tk=D)
    dya, dyb, dga, dgb = _merge_bwd(dmerged, ya, yb, proj)
    dya_h = _matmul(dya, w_a_s, tb=True, by_shard=True, name="mm_d_ya", tm=SEQ, tn=FOX_W, tk=W_BR_SH)
    dyb_h = _matmul(dyb, w_b_s, tb=True, by_shard=True, name="mm_d_yb", tm=SEQ, tn=DIL_OUT_W, tk=W_BR_SH)

    dqa, dka, dva, dF = _fox_bwd(proj, dya_h, ya_h, lse_a, f_bcast, f_row)
    dF_row = jnp.pad(dF[:, :2, :].reshape(N_FOX_HEADS, SEQ), ((0, LANES - N_FOX_HEADS), (0, 0)))
    df, db_fgate = _fox_gate_bwd(dF_row, proj, b_pad)

    delta_b = _dil_delta(dyb_h, yb_h)
    rep = lambda t: _to_classes(jnp.tile(t, (1, N_GROUPS)))
    dq_c, dk_c, dv_c = _dil_bwd(q_c, k_c, v_c, rep(dyb_h), rep(lse_b), rep(delta_b))
    dqb, dkb = _rope_bwd(_from_classes(dq_c), _from_classes(dk_c), tables)
    dvb = _from_classes(dv_c).astype(bf16)

    dproj = _shard_pad_cols(jnp.concatenate([dqa, dka, dva, df[:, :N_FOX_HEADS], dqb, dkb, dvb, dga, dgb], axis=1))
    dh1 = _matmul(dproj, w_in_s, tb=True, by_shard=True, name="mm_d_h1", tm=SEQ // 2, tn=D, tk=W_IN_PAD)
    grad_x, dsh_m, dsc_m, dg_mix = _first_bwd(dh1, x2d, dx1, modv, g_mix)

    g_in = _matmul(h1, dproj, ta=True, by_shard=True, out_dtype=bf16, name="mm_g_in", tm=D, tn=W_IN_PAD, tk=SEQ)
    g_gu = _matmul(h2, dau, ta=True, by_shard=True, out_dtype=bf16, name="mm_g_ffn_in", tm=D, tn=2 * FF_PAD, tk=SEQ)
    g_d = _matmul(act, dff, ta=True, out_dtype=bf16, name="mm_g_down", tm=FF_HID // 2, tn=512, tk=SEQ)
    g_o = _matmul(merged, dmix, ta=True, out_dtype=bf16, name="mm_g_out", tm=D, tn=512, tk=SEQ)
    g_a = _matmul(ya_h, dya, ta=True, by_shard=True, out_dtype=bf16, name="mm_g_br_a", tm=FOX_W, tn=W_BR_SH, tk=SEQ)
    g_b = _matmul(yb_h, dyb, ta=True, by_shard=True, out_dtype=bf16, name="mm_g_br_b", tm=DIL_OUT_W, tn=W_BR_SH,
                  tk=SEQ)
    rows_a, rows_b = FOX_W * W_BR_SH // D, DIL_OUT_W * W_BR_SH // D
    g_small = jnp.concatenate([g_a.reshape(N_DEV, rows_a, D), g_b.reshape(N_DEV, rows_b, D),
                               g_o.reshape(N_DEV, W_BR_SH, D)], axis=1)
    by_dev = [g_in, g_gu, g_d.reshape(N_DEV, FF_PAD, D), g_small]

    core = pc.astype(jnp.int32).reshape(1)
    chip = (2 * px + py).astype(jnp.int32).reshape(1)
    tags = ["in", "gu", "down", "small"]
    from_pair = _pair_exchange(by_dev)
    partial_sums = [_pair_add(g, r, core, "pair_add_" + t) for g, r, t in zip(by_dev, from_pair, tags)]
    from_chips = _chip_exchange(partial_sums)
    s_in, s_gu, s_d, s_small = [_chip_add(p, r, chip, "chip_add_" + t)
                                for p, r, t in zip(partial_sums, from_chips, tags)]
    g_shard = {
        "w_in": s_in[:, :W_IN_SH], "w_ffn_gate": s_gu[:, :W_FF_SH], "w_ffn_up": s_gu[:, FF_PAD:FF_PAD + W_FF_SH],
        "w_ffn_down": s_d[:W_FF_SH], "w_br_a": s_small[:rows_a].reshape(FOX_W, W_BR_SH),
        "w_br_b": s_small[rows_a:rows_a + rows_b].reshape(DIL_OUT_W, W_BR_SH), "w_out": s_small[rows_a + rows_b:],
    }

    pad_lane = lambda t: jnp.pad(t, ((0, 0), (0, D - t.shape[1])))
    small = jnp.concatenate([dsh_m, dsc_m, dga_m, dsh_f, dsc_f, dga_f, dg_mix, dg_ffn, dg_final,
                             pad_lane(db_fgate), loss_lanes, jnp.zeros((SMALL_ROWS - 11, D), f32)], axis=0)
    small_all = _all_gather(small, "gather_small")
    small_sum, loss_row = _small_reduce(small_all)
    dmod_all = small_all[:, :6, :].reshape(N_DEV, 6 * D)
    g_w_ada = _ada_bwd(c_all, lax.dynamic_slice(dmod_all, (0, dev * ada_cols), (N_DEV, ada_cols)))

    loss = loss_row[0, 0]
    g = {
        "w_ada": g_w_ada[None], "b_ada": small_sum[0:6].reshape(1, 6 * D), "g_mix": small_sum[6:7],
        "w_in": g_shard["w_in"][None], "b_fgate": small_sum[9:10, :N_FOX_HEADS], "w_br_a": g_shard["w_br_a"][None],
        "w_br_b": g_shard["w_br_b"][None], "w_out": g_shard["w_out"][None], "g_ffn": small_sum[7:8],
        "w_ffn_gate": g_shard["w_ffn_gate"][None], "w_ffn_up": g_shard["w_ffn_up"][None],
        "w_ffn_down": g_shard["w_ffn_down"][None], "g_final": small_sum[8],
    }
    w = {"w_ada": w_ada, "b_ada": b_ada, "g_mix": g_mix, "w_in": w_in, "b_fgate": b_fgate, "w_br_a": w_br_a,
         "w_br_b": w_br_b, "w_out": w_out, "g_ffn": g_ffn, "w_ffn_gate": w_ffn_gate, "w_ffn_up": w_ffn_up,
         "w_ffn_down": w_ffn_down, "g_final": g_final}
    m = {"w_ada": m_w_ada, "b_ada": m_b_ada, "g_mix": m_g_mix, "w_in": m_w_in, "b_fgate": m_b_fgate,
         "w_br_a": m_w_br_a, "w_br_b": m_w_br_b, "w_out": m_w_out, "g_ffn": m_g_ffn, "w_ffn_gate": m_w_ffn_gate,
         "w_ffn_up": m_w_ffn_up, "w_ffn_down": m_w_ffn_down, "g_final": m_g_final}
    v = {"w_ada": v_w_ada, "b_ada": v_b_ada, "g_mix": v_g_mix, "w_in": v_w_in, "b_fgate": v_b_fgate,
         "w_br_a": v_w_br_a, "w_br_b": v_w_br_b, "w_out": v_w_out, "g_ffn": v_g_ffn, "w_ffn_gate": v_w_ffn_gate,
         "w_ffn_up": v_w_ffn_up, "w_ffn_down": v_w_ffn_down, "g_final": v_g_final}
    names = list(w)
    delta, new_m, new_v = {}, {}, {}
    for n in names:
        shape = w[n].shape
        two_d = (lambda t: t.reshape(shape[-2:])) if len(shape) == 3 else (lambda t: t)
        dl, mn, vn = _adamw(two_d(w[n]), two_d(g[n]), two_d(m[n]), two_d(v[n]), "adamw_" + n)
        delta[n], new_m[n], new_v[n] = dl.reshape(shape), mn.reshape(shape), vn.reshape(shape)

    return (loss, grad_x[None], *[g[n] for n in names], *[delta[n] for n in names],
            *[new_m[n] for n in names], *[new_v[n] for n in names])
```

```python
import functools

import jax
import jax.numpy as jnp
from jax import lax
from jax.experimental import pallas as pl
from jax.experimental.pallas import tpu as pltpu

f32 = jnp.float32
bf16 = jnp.bfloat16
SDS = jax.ShapeDtypeStruct
MESH = pl.DeviceIdType.MESH

N_DEV = 8
D = 1024
SEQ = 2048
HEAD_DIM = 64
N_FOX_HEADS = 8
FOX_W = 512
DIL_W = 768
DIL_OUT_W = 256
ROT_DIM = 16
ROPE_THETA = 500000.0
D_FF = 2816
IN_COLS = 5896
EPS = 1e-6
NEG = -1e30
ATT_SCALE = HEAD_DIM ** -0.5

ADAM_LR = 0.001
ADAM_B1 = 0.9
ADAM_B2 = 0.999
ADAM_EPS = 1e-08
ADAM_WD = 0.01
ADAM_STEP = 10

C_GA, C_GB, C_QB, C_KB, C_VB, C_QA, C_KA, C_VA, C_F = 0, 1024, 2304, 3072, 3840, 4608, 5120, 5632, 6144
PROJ_W = 6272
LANES = 128
VMEM_LIMIT = 52 * 1024 * 1024

W_IN_SH, W_IN_PAD = IN_COLS // N_DEV, 768
W_BR_SH = D // N_DEV
W_FF_SH, FF_PAD = D_FF // N_DEV, 384
FF_HID = N_DEV * FF_PAD
SMALL_ROWS = 16


def _params(sem=None):
    if sem is None:
        return pltpu.CompilerParams(vmem_limit_bytes=VMEM_LIMIT)
    return pltpu.CompilerParams(dimension_semantics=sem, vmem_limit_bytes=VMEM_LIMIT)


def _rowwise(fn, name, tiled, vecs, outs, reds=(), tile=256):
    nt, nv, no = len(tiled), len(vecs), len(outs)
    rows = tiled[0][0].shape[0]
    assert rows % tile == 0

    def body(*refs):
        tin = [r[...] for r in refs[:nt]]
        vin = [r[...] for r in refs[nt:nt + nv]]
        orefs = refs[nt + nv:nt + nv + no]
        rrefs = refs[nt + nv + no:]
        touts, routs = fn(tin, vin)
        for r, t in zip(orefs, touts, strict=True):
            r[...] = t.astype(r.dtype)
        if rrefs:
            @pl.when(pl.program_id(0) == 0)
            def _():
                for r in rrefs:
                    r[...] = jnp.zeros_like(r)
            for r, t in zip(rrefs, routs, strict=True):
                r[...] += t

    def col_map(cb):
        return lambda i: (i, cb)

    def whole_map(nd):
        return lambda i: (0,) * nd

    in_specs = [pl.BlockSpec((tile, w), col_map(cb)) for (_, w, cb) in tiled]
    in_specs += [pl.BlockSpec(v.shape, whole_map(v.ndim)) for v in vecs]
    out_specs = [pl.BlockSpec((tile, w), lambda i: (i, 0)) for (w, _) in outs]
    out_specs += [pl.BlockSpec((1, w), lambda i: (0, 0)) for w in reds]
    out_shape = [SDS((rows, w), dt) for (w, dt) in outs] + [SDS((1, w), f32) for w in reds]
    res = pl.pallas_call(
        body, grid=(rows // tile,), in_specs=in_specs, out_specs=out_specs, out_shape=out_shape, name=name,
        compiler_params=_params(("arbitrary",)),
    )(*[t[0] for t in tiled], *vecs)
    return res


def _matmul(a, b, *, ta=False, tb=False, out_dtype=f32, name, tm, tn, tk, by_shard=False):
    m, k = (a.shape[1], a.shape[0]) if ta else a.shape
    if by_shard and not ta:
        n, kb = (b.shape[1], N_DEV * b.shape[2]) if tb else (N_DEV * b.shape[2], b.shape[1])
        assert (tk if tb else tn) == b.shape[2]
    else:
        n, kb = (b.shape[0], b.shape[1]) if tb else (b.shape[1], b.shape[0])
    assert kb == k and m % tm == 0 and n % tn == 0 and k % tk == 0
    nk = k // tk
    dims = (((0 if ta else 1,), (1 if tb else 0,)), ((), ()))
    b_stacked = by_shard and not ta
    o_stacked = by_shard and ta

    def body(a_ref, b_ref, o_ref, *acc):
        bv = b_ref[0] if b_stacked else b_ref[...]
        p = lax.dot_general(a_ref[...].astype(bf16), bv.astype(bf16), dims, preferred_element_type=f32)

        def put(val):
            if o_stacked:
                o_ref[0] = val.astype(o_ref.dtype)
            else:
                o_ref[...] = val.astype(o_ref.dtype)

        if nk == 1:
            put(p)
        else:
            acc_ref, = acc
            kk = pl.program_id(2)

            @pl.when(kk == 0)
            def _():
                acc_ref[...] = p

            @pl.when(kk > 0)
            def _():
                acc_ref[...] += p

            @pl.when(kk == nk - 1)
            def _():
                put(acc_ref[...])

    a_spec = pl.BlockSpec((tk, tm), lambda i, j, kk: (kk, i)) if ta else pl.BlockSpec((tm, tk), lambda i, j, kk: (i, kk))
    if b_stacked and tb:
        b_spec = pl.BlockSpec((1, tn, tk), lambda i, j, kk: (kk, j, 0))
    elif b_stacked:
        b_spec = pl.BlockSpec((1, tk, tn), lambda i, j, kk: (j, kk, 0))
    elif tb:
        b_spec = pl.BlockSpec((tn, tk), lambda i, j, kk: (j, kk))
    else:
        b_spec = pl.BlockSpec((tk, tn), lambda i, j, kk: (kk, j))
    if o_stacked:
        assert tn == n // N_DEV
        out_spec = pl.BlockSpec((1, tm, tn), lambda i, j, kk: (j, i, 0))
        out_shape = SDS((N_DEV, m, tn), out_dtype)
    else:
        out_spec = pl.BlockSpec((tm, tn), lambda i, j, kk: (i, j))
        out_shape = SDS((m, n), out_dtype)
    return pl.pallas_call(
        body, grid=(m // tm, n // tn, nk), in_specs=[a_spec, b_spec], out_specs=out_spec, out_shape=out_shape,
        name=name, scratch_shapes=[pltpu.VMEM((tm, tn), f32)] if nk > 1 else [],
        compiler_params=_params(("parallel", "parallel", "arbitrary")),
    )(a, b)


def _rms(x):
    r = lax.rsqrt(jnp.mean(x * x, axis=-1, keepdims=True) + EPS)
    return r, x * r


def _rms_bwd(r, xn, dxn):
    return r * (dxn - xn * jnp.mean(dxn * xn, axis=-1, keepdims=True))


def _colsum(t):
    return jnp.sum(t, axis=0, keepdims=True)


def _sigmoid(x):
    return 1.0 / (1.0 + jnp.exp(-x))


def _modulated_norm(x, g, shift, scale):
    _, xn = _rms(x)
    return (xn * g) * (1.0 + scale) + shift


def _pre1(x, modv, g_mix):
    def fn(t, v):
        (xt,), (mv, g) = t, v
        return [_modulated_norm(xt, g, mv[0:1], mv[1:2])], []
    return _rowwise(fn, "pre1", [(x, D, 0)], [modv, g_mix], [(D, bf16)])[0]


def _post1(x, mix, modv, g_ffn):
    def fn(t, v):
        (xt, mt), (mv, g) = t, v
        x1 = xt + mv[2:3] * mt
        return [x1, _modulated_norm(x1, g, mv[3:4], mv[4:5])], []
    return _rowwise(fn, "post1", [(x, D, 0), (mix, D, 0)], [modv, g_ffn], [(D, f32), (D, bf16)])


def _gate_up(au, j):
    base = 2 * j * FF_PAD
    return au[:, base:base + FF_PAD], au[:, base + FF_PAD:base + 2 * FF_PAD]


def _swiglu_fwd(au):
    def fn(t, v):
        acts = []
        for j in range(N_DEV):
            a, u = _gate_up(t[0], j)
            acts.append(a * _sigmoid(a) * u)
        return [jnp.concatenate(acts, axis=1)], []
    return _rowwise(fn, "swiglu_fwd", [(au, 2 * FF_HID, 0)], [], [(FF_HID, bf16)])[0]


def _swiglu_bwd(au, dact):
    def fn(t, v):
        parts = []
        for j in range(N_DEV):
            a, u = _gate_up(t[0], j)
            d = t[1][:, j * FF_PAD:(j + 1) * FF_PAD]
            sg = _sigmoid(a)
            parts += [d * u * (sg * (1.0 + a * (1.0 - sg))), d * (a * sg)]
        return [jnp.concatenate(parts, axis=1)], []
    return _rowwise(fn, "swiglu_bwd", [(au, 2 * FF_HID, 0), (dact, FF_HID, 0)], [], [(2 * FF_HID, bf16)],
                    tile=128)[0]


def _final(x1, ff, target, modv, g_final):
    def fn(t, v):
        (x1t, fft, tgt), (mv, g) = t, v
        x2 = x1t + mv[5:6] * fft
        r, xn = _rms(x2)
        err = xn * g - tgt
        dy = err * (1.0 / D)
        dx2 = _rms_bwd(r, xn, dy * g)
        return [dx2, dx2 * mv[5:6]], [_colsum(dy * xn), _colsum(dx2 * fft), _colsum(err * err) * (0.5 / D)]
    return _rowwise(fn, "final", [(x1, D, 0), (ff, D, 0), (target, D, 0)], [modv, g_final],
                    [(D, f32), (D, bf16)], [D, D, D])


def _mid_bwd(dh2, x1, dx2, mix, modv, g_ffn):
    def fn(t, v):
        (dh, x1t, dx2t, mt), (mv, g) = t, v
        r, xn = _rms(x1t)
        dn = dh * (1.0 + mv[4:5])
        dx1 = dx2t + _rms_bwd(r, xn, dn * g)
        return [dx1, dx1 * mv[2:3]], [_colsum(dh), _colsum(dh * (xn * g)), _colsum(dn * xn), _colsum(dx1 * mt)]
    return _rowwise(fn, "mid_bwd", [(dh2, D, 0), (x1, D, 0), (dx2, D, 0), (mix, D, 0)], [modv, g_ffn],
                    [(D, f32), (D, bf16)], [D, D, D, D])


def _first_bwd(dh1, x, dx1, modv, g_mix):
    def fn(t, v):
        (dh, xt, dx1t), (mv, g) = t, v
        r, xn = _rms(xt)
        dn = dh * (1.0 + mv[1:2])
        return [dx1t + _rms_bwd(r, xn, dn * g)], [_colsum(dh), _colsum(dh * (xn * g)), _colsum(dn * xn)]
    return _rowwise(fn, "first_bwd", [(dh1, D, 0), (x, D, 0), (dx1, D, 0)], [modv, g_mix], [(D, f32)], [D, D, D])


def _merge_fwd(ya, yb, proj):
    def fn(t, v):
        ya_t, yb_t, ga, gb = t
        return [_sigmoid(ga) * ya_t + _sigmoid(gb) * yb_t], []
    return _rowwise(fn, "merge_fwd", [(ya, D, 0), (yb, D, 0), (proj, D, C_GA // D), (proj, D, C_GB // D)], [],
                    [(D, bf16)])[0]


def _merge_bwd(dmerged, ya, yb, proj):
    def fn(t, v):
        dm, ya_t, yb_t, ga, gb = t
        sa, sb = _sigmoid(ga), _sigmoid(gb)
        return [dm * sa, dm * sb, dm * ya_t * (sa * (1.0 - sa)), dm * yb_t * (sb * (1.0 - sb))], []
    return _rowwise(fn, "merge_bwd",
                    [(dmerged, D, 0), (ya, D, 0), (yb, D, 0), (proj, D, C_GA // D), (proj, D, C_GB // D)], [],
                    [(D, bf16), (D, bf16), (D, bf16), (D, bf16)])


def _rope_tables():
    half = ROT_DIM // 2
    pos = jnp.arange(SEQ, dtype=f32)
    inv_freq = ROPE_THETA ** (-jnp.arange(0, ROT_DIM, 2, dtype=f32) / ROT_DIM)
    ang = pos[:, None] * inv_freq[None, :]
    cos, sin = jnp.cos(ang), jnp.sin(ang)
    pad = jnp.zeros((SEQ, HEAD_DIM - ROT_DIM), f32)
    zero = jnp.zeros((SEQ, half), f32)
    c_head = jnp.concatenate([cos, cos, pad + 1.0], axis=1)
    lo_head = jnp.concatenate([-sin, zero, pad], axis=1)
    hi_head = jnp.concatenate([zero, sin, pad], axis=1)
    reps = DIL_W // HEAD_DIM
    return tuple(jnp.tile(t, (1, reps)) for t in (c_head, lo_head, hi_head))


def _rope_fwd(proj, tables):
    half = ROT_DIM // 2

    def fn(t, v):
        q, k, vv, c, lo, hi = t
        rot = lambda z: z * c + pltpu.roll(z, DIL_W - half, 1) * lo + pltpu.roll(z, half, 1) * hi
        return [rot(q) * ATT_SCALE, rot(k), vv], []
    return _rowwise(fn, "rope_fwd", [(proj, DIL_W, C_QB // DIL_W), (proj, DIL_W, C_KB // DIL_W),
                                     (proj, DIL_W, C_VB // DIL_W)] + [(tb, DIL_W, 0) for tb in tables], [],
                    [(DIL_W, bf16)] * 3)


def _rope_bwd(dq, dk, tables):
    half = ROT_DIM // 2

    def fn(t, v):
        dq_t, dk_t, c, lo, hi = t
        rot_t = lambda z: z * c + pltpu.roll(z * lo, half, 1) + pltpu.roll(z * hi, DIL_W - half, 1)
        return [rot_t(dq_t), rot_t(dk_t)], []
    return _rowwise(fn, "rope_bwd", [(dq, DIL_W, 0), (dk, DIL_W, 0)] + [(tb, DIL_W, 0) for tb in tables], [],
                    [(DIL_W, bf16), (DIL_W, bf16)])


def _head_bcast_sum(d):
    lane = lax.broadcasted_iota(jnp.int32, d.shape, 1)
    out = jnp.zeros_like(d)
    for h in range(d.shape[1] // HEAD_DIM):
        sel = (lane >= h * HEAD_DIM) & (lane < (h + 1) * HEAD_DIM)
        out = jnp.where(sel, jnp.sum(jnp.where(sel, d, 0.0), axis=1, keepdims=True), out)
    return out


def _dil_combine(o, lse):
    def fn(t, v):
        o0, o1, o2, l0, l1, l2 = t
        m = jnp.maximum(jnp.maximum(l0, l1), l2)
        w0, w1, w2 = jnp.exp(l0 - m), jnp.exp(l1 - m), jnp.exp(l2 - m)
        tot = w0 + w1 + w2
        return [(w0 * o0 + w1 * o1 + w2 * o2) / tot, m + jnp.log(tot)], []
    w = DIL_OUT_W
    return _rowwise(fn, "dil_combine", [(o, w, 0), (o, w, 1), (o, w, 2), (lse, w, 0), (lse, w, 1), (lse, w, 2)], [],
                    [(w, f32), (w, f32)])


def _dil_delta(dyb_h, yb_h):
    def fn(t, v):
        return [_head_bcast_sum(t[0] * t[1])], []
    return _rowwise(fn, "dil_delta", [(dyb_h, DIL_OUT_W, 0), (yb_h, DIL_OUT_W, 0)], [], [(DIL_OUT_W, f32)])[0]


def _adamw(w, g, m, v, name):
    shape = w.shape
    if w.ndim == 1:
        w, g, m, v = (t.reshape(1, -1) for t in (w, g, m, v))
    rows, cols = w.shape
    tile = 256 if rows % 256 == 0 and rows > 512 else rows

    def fn(t, _):
        wt, gt, mt, vt = t
        mn = ADAM_B1 * mt + (1.0 - ADAM_B1) * gt
        vn = ADAM_B2 * vt + (1.0 - ADAM_B2) * (gt * gt)
        m_hat = mn / (1.0 - ADAM_B1 ** ADAM_STEP)
        v_hat = vn / (1.0 - ADAM_B2 ** ADAM_STEP)
        return [-ADAM_LR * (m_hat / (jnp.sqrt(v_hat) + ADAM_EPS) + ADAM_WD * wt), mn, vn], []
    delta, mn, vn = _rowwise(fn, name, [(w, cols, 0), (g, cols, 0), (m, cols, 0), (v, cols, 0)], [],
                             [(cols, f32)] * 3, tile=tile)
    return delta.reshape(shape), mn.reshape(shape), vn.reshape(shape)


def _ada_fwd(c_all, w_shard, b_shard):
    def body(c_ref, w_ref, b_ref, o_ref):
        cv = c_ref[...]
        sc = (cv * _sigmoid(cv)).astype(bf16)
        o_ref[...] = jnp.dot(sc, w_ref[...].astype(bf16), preferred_element_type=f32) + b_ref[...]
    return pl.pallas_call(body, out_shape=SDS((N_DEV, w_shard.shape[1]), f32), name="ada_fwd",
                          compiler_params=_params())(c_all, w_shard, b_shard)


def _ada_bwd(c_all, dmod_cols):
    def body(c_ref, d_ref, o_ref):
        cv = c_ref[...]
        sc = cv * _sigmoid(cv)
        o_ref[...] = lax.dot_general(sc, d_ref[...], (((0,), (0,)), ((), ())), precision=lax.Precision.HIGHEST,
                                     preferred_element_type=f32)
    return pl.pallas_call(body, out_shape=SDS((D, dmod_cols.shape[1]), f32), name="ada_bwd",
                          compiler_params=_params())(c_all, dmod_cols)


def _small_reduce(gathered):
    def body(g_ref, o_ref, loss_ref):
        acc = g_ref[0]
        for d in range(1, N_DEV):
            acc = acc + g_ref[d]
        o_ref[...] = acc
        loss_ref[...] = jnp.zeros((1, LANES), f32) + jnp.sum(acc[10:11, :])
    return pl.pallas_call(body, out_shape=(SDS((SMALL_ROWS, D), f32), SDS((1, LANES), f32)), name="small_reduce",
                          compiler_params=_params())(gathered)


FOX_BLK = 512
CUM_BLK = 128


def _fold_lanes(t, op):
    out = t[:, :LANES]
    for j in range(1, t.shape[1] // LANES):
        out = op(out, t[:, j * LANES:(j + 1) * LANES])
    return out


def _fox_gate_fwd(proj, b_pad):
    nblk = SEQ // CUM_BLK

    def body(f_ref, b_ref, col_ref):
        r = lax.broadcasted_iota(jnp.int32, (CUM_BLK, CUM_BLK), 0)
        c = lax.broadcasted_iota(jnp.int32, (CUM_BLK, CUM_BLK), 1)
        tri = (r >= c).astype(f32)
        carry = jnp.zeros((1, LANES), f32)
        for blk in range(nblk):
            z = f_ref[blk * CUM_BLK:(blk + 1) * CUM_BLK, :] + b_ref[...]
            logf = jnp.minimum(z, 0.0) - jnp.log1p(jnp.exp(-jnp.abs(z)))
            cs = jnp.dot(tri, logf, precision=lax.Precision.HIGHEST, preferred_element_type=f32) + carry
            col_ref[blk * CUM_BLK:(blk + 1) * CUM_BLK, :] = cs
            carry = cs[CUM_BLK - 1:CUM_BLK, :]

    return pl.pallas_call(
        body, grid=(1,), in_specs=[pl.BlockSpec((SEQ, LANES), lambda i: (0, C_F // LANES)),
                                   pl.BlockSpec((1, LANES), lambda i: (0, 0))],
        out_specs=pl.BlockSpec((SEQ, LANES), lambda i: (0, 0)),
        out_shape=SDS((SEQ, LANES), f32), name="fox_gate_fwd",
        compiler_params=_params(("arbitrary",)),
    )(proj, b_pad)


def _fox_gate_bwd(dF_row, proj, b_pad):
    nblk = SEQ // CUM_BLK

    def body(d_ref, f_ref, b_ref, df_ref, db_ref, col_ref):
        r = lax.broadcasted_iota(jnp.int32, (CUM_BLK, CUM_BLK), 0)
        c = lax.broadcasted_iota(jnp.int32, (CUM_BLK, CUM_BLK), 1)
        tri = (r <= c).astype(f32)
        lane = lax.broadcasted_iota(jnp.int32, (CUM_BLK, LANES), 1)
        col_ref[...] = d_ref[...].T
        carry = jnp.zeros((1, LANES), f32)
        total = jnp.zeros((1, LANES), f32)
        for blk in reversed(range(nblk)):
            rows = slice(blk * CUM_BLK, (blk + 1) * CUM_BLK)
            cs = jnp.dot(tri, col_ref[rows, :], precision=lax.Precision.HIGHEST, preferred_element_type=f32) + carry
            carry = cs[0:1, :]
            z = f_ref[rows, :] + b_ref[...]
            df = jnp.where(lane < N_FOX_HEADS, cs * _sigmoid(-z), 0.0)
            df_ref[rows, :] = df.astype(df_ref.dtype)
            total = total + _colsum(df)
        db_ref[...] = total

    return pl.pallas_call(
        body, grid=(1,), in_specs=[pl.BlockSpec((LANES, SEQ), lambda i: (0, 0)),
                                   pl.BlockSpec((SEQ, LANES), lambda i: (0, C_F // LANES)),
                                   pl.BlockSpec((1, LANES), lambda i: (0, 0))],
        out_specs=[pl.BlockSpec((SEQ, LANES), lambda i: (0, 0)), pl.BlockSpec((1, LANES), lambda i: (0, 0))],
        out_shape=(SDS((SEQ, LANES), bf16), SDS((1, LANES), f32)), name="fox_gate_bwd",
        scratch_shapes=[pltpu.VMEM((SEQ, LANES), f32)],
        compiler_params=_params(("arbitrary",)),
    )(dF_row, proj, b_pad)


def _nt(a, b):
    return lax.dot_general(a, b, (((1,), (1,)), ((), ())), preferred_element_type=f32)


def _tn(a, b):
    return lax.dot_general(a, b, (((0,), (0,)), ((), ())), preferred_element_type=f32)


def _fox_prep(proj, f_col):
    def fn(t, v):
        q, k, vv, fc = t
        lane = lax.broadcasted_iota(jnp.int32, (q.shape[0], LANES), 1)
        qs, ks = [], []
        for h in range(N_FOX_HEADS):
            pair, pos = divmod(h, 2)
            own = (lane >= pos * HEAD_DIM) & (lane < (pos + 1) * HEAD_DIM)
            base = (1 - pos) * HEAD_DIM
            f = fc[:, h:h + 1]
            hi = f.astype(bf16).astype(f32)
            mid = (f - hi).astype(bf16).astype(f32)
            lo = (f - hi) - mid
            one = jnp.ones_like(f)
            qa = jnp.where(own, q[:, pair * LANES:(pair + 1) * LANES] * ATT_SCALE, 0.0)
            ka = k[:, pair * LANES:(pair + 1) * LANES]
            for idx, (qv, kv) in enumerate([(hi, one), (mid, one), (lo, one), (one, -hi), (one, -mid), (one, -lo)]):
                sel = lane == base + idx
                qa = jnp.where(sel, qv, qa)
                ka = jnp.where(sel, kv, ka)
            qs.append(qa)
            ks.append(ka)
        return [jnp.concatenate(qs, axis=1), jnp.concatenate(ks, axis=1), vv], []
    w = N_FOX_HEADS * LANES
    return _rowwise(fn, "fox_prep", [(proj, FOX_W, C_QA // FOX_W), (proj, FOX_W, C_KA // FOX_W),
                                     (proj, FOX_W, C_VA // FOX_W), (f_col, LANES, 0)], [],
                    [(w, bf16), (w, bf16), (FOX_W, bf16)])


def _fox_fwd(q_aug, k_aug, v):
    blk = FOX_BLK
    npair = FOX_W // LANES

    def body(q_ref, k_ref, v_ref, o_ref, lse_ref, s_scr):
        i = pl.program_id(1)
        tri = lax.broadcasted_iota(jnp.int32, (blk, blk), 0) >= lax.broadcasted_iota(jnp.int32, (blk, blk), 1)
        qh = [q_ref[:, h * LANES:(h + 1) * LANES] for h in range(2)]

        def logits(c, masked):
            off = pl.multiple_of(c * blk, blk)
            tops = []
            for h in range(2):
                s = _nt(qh[h], k_ref[pl.ds(off, blk), h * LANES:(h + 1) * LANES])
                if masked:
                    s = jnp.where(tri, s, NEG)
                s_scr[h, :, pl.ds(off, blk)] = s
                tops.append(_fold_lanes(s, jnp.maximum))
            return tops

        def pass_a(c, m):
            return tuple(jnp.maximum(a, b) for a, b in zip(m, logits(c, False)))

        m = lax.fori_loop(0, i, pass_a, tuple(jnp.full((blk, LANES), NEG, f32) for _ in range(2)))
        mx = [jnp.max(jnp.maximum(a, b), axis=1, keepdims=True) for a, b in zip(m, logits(i, True))]

        def pass_b(c, carry):
            off = pl.multiple_of(c * blk, blk)
            vv = v_ref[pl.ds(off, blk), :]
            new = []
            for h in range(2):
                l, acc = carry[h]
                p = jnp.exp(s_scr[h, :, pl.ds(off, blk)] - mx[h])
                new.append((l + _fold_lanes(p, jnp.add),
                            acc + jnp.dot(p.astype(bf16), vv, preferred_element_type=f32)))
            return tuple(new)

        zero = jnp.zeros((blk, LANES), f32)
        (l_a, acc_a), (l_b, acc_b) = lax.fori_loop(0, i + 1, pass_b, ((zero, zero), (zero, zero)))
        l_a = jnp.sum(l_a, axis=1, keepdims=True)
        l_b = jnp.sum(l_b, axis=1, keepdims=True)
        first = lax.broadcasted_iota(jnp.int32, (blk, LANES), 1) < HEAD_DIM
        o_ref[...] = jnp.where(first, acc_a / l_a, acc_b / l_b)
        lse_ref[0] = jnp.where(first, mx[0] + jnp.log(l_a), mx[1] + jnp.log(l_b))

    return pl.pallas_call(
        body, grid=(npair, SEQ // blk),
        in_specs=[pl.BlockSpec((blk, 2 * LANES), lambda p, i: (i, p)),
                  pl.BlockSpec((SEQ, 2 * LANES), lambda p, i: (0, p)),
                  pl.BlockSpec((SEQ, LANES), lambda p, i: (0, p))],
        out_specs=[pl.BlockSpec((blk, LANES), lambda p, i: (i, p)),
                   pl.BlockSpec((1, blk, LANES), lambda p, i: (p, i, 0))],
        out_shape=(SDS((SEQ, FOX_W), f32), SDS((npair, SEQ, LANES), f32)), name="fox_fwd",
        scratch_shapes=[pltpu.VMEM((2, blk, SEQ), f32)],
        compiler_params=_params(("parallel", "arbitrary")),
    )(q_aug, k_aug, v)


def _fox_bwd(q_aug, k_aug, v, do, o, lse):
    blk = FOX_BLK
    npair = FOX_W // LANES
    nblk = SEQ // blk

    def body(q_ref, k_ref, v_ref, do_ref, o_ref, lse_ref, dq_ref, dk_ref, dv_ref, df_ref,
             dq_acc, delta_ref, res_ref):
        lane_s = lax.broadcasted_iota(jnp.int32, (SEQ, LANES), 1)
        prod = do_ref[...] * o_ref[...]
        d_a = jnp.sum(jnp.where(lane_s < HEAD_DIM, prod, 0.0), axis=1, keepdims=True)
        d_b = jnp.sum(jnp.where(lane_s >= HEAD_DIM, prod, 0.0), axis=1, keepdims=True)
        delta_ref[...] = jnp.where(lane_s < HEAD_DIM, d_a, d_b)
        dq_acc[...] = jnp.zeros_like(dq_acc)
        res_ref[...] = jnp.zeros_like(res_ref)
        df_ref[...] = jnp.zeros_like(df_ref)
        lane = lax.broadcasted_iota(jnp.int32, (blk, LANES), 1)
        own = [lane < HEAD_DIM, lane >= HEAD_DIM]
        tri = lax.broadcasted_iota(jnp.int32, (blk, blk), 0) >= lax.broadcasted_iota(jnp.int32, (blk, blk), 1)

        def q_slab(qoff, h):
            return q_ref[pl.ds(qoff, blk), h * LANES:(h + 1) * LANES]

        def probs(qoff, h, k_h, masked):
            s = _nt(q_slab(qoff, h), k_h)
            if masked:
                s = jnp.where(tri, s, NEG)
            return jnp.exp(s - lse_ref[0, pl.ds(qoff, blk), h * HEAD_DIM:h * HEAD_DIM + 1])

        def k_slabs(koff):
            return [k_ref[pl.ds(koff, blk), h * LANES:(h + 1) * LANES] for h in range(2)]

        def kv_step(kj, _):
            koff = pl.multiple_of(kj * blk, blk)
            k_aug = k_slabs(koff)
            k_own = [jnp.where(own[h], k_aug[h], jnp.zeros_like(k_aug[h])) for h in range(2)]
            vv = v_ref[pl.ds(koff, blk), :]
            v_own = [jnp.where(own[h], vv, jnp.zeros_like(vv)) for h in range(2)]

            def q_tile(qi, carry, masked):
                qoff = pl.multiple_of(qi * blk, blk)
                dd = do_ref[pl.ds(qoff, blk), :].astype(bf16)
                new, dq_add = [], None
                for h in range(2):
                    dk_h, dv_h, dcol = carry[h]
                    p = probs(qoff, h, k_aug[h], masked)
                    dl = p * (_nt(dd, v_own[h]) - delta_ref[pl.ds(qoff, blk), h * HEAD_DIM:h * HEAD_DIM + 1])
                    dlb = dl.astype(bf16)
                    part = jnp.dot(dlb, k_own[h], preferred_element_type=f32)
                    dq_add = part if dq_add is None else dq_add + part
                    res_ref[h, pl.ds(qoff, blk), :] += _fold_lanes(dl, jnp.add)
                    new.append((dk_h + _tn(dlb, q_slab(qoff, h)), dv_h + _tn(p.astype(bf16), dd),
                                dcol + _colsum(dl)))
                dq_acc[pl.ds(qoff, blk), :] += dq_add * ATT_SCALE
                return tuple(new)

            zero = (jnp.zeros((blk, LANES), f32), jnp.zeros((blk, LANES), f32), jnp.zeros((1, blk), f32))
            carry = q_tile(kj, (zero, zero), True)
            (dk_a, dv_a, dcol_a), (dk_b, dv_b, dcol_b) = lax.fori_loop(
                kj + 1, nblk, lambda qi, cr: q_tile(qi, cr, False), carry)
            dk_ref[pl.ds(koff, blk), :] = jnp.where(own[0], dk_a, dk_b).astype(dk_ref.dtype)
            dv_ref[pl.ds(koff, blk), :] = jnp.where(own[0], dv_a, dv_b).astype(dv_ref.dtype)
            df_ref[0, 0:1, pl.ds(koff, blk)] = -dcol_a
            df_ref[0, 1:2, pl.ds(koff, blk)] = -dcol_b
            return 0

        lax.fori_loop(0, nblk, kv_step, 0)
        dq_ref[...] = dq_acc[...].astype(dq_ref.dtype)

        for h in range(2):
            res_ref[h] = jnp.zeros((SEQ, LANES), f32) + jnp.sum(res_ref[h], axis=1, keepdims=True)

        def kv_fix(kj, _):
            koff = pl.multiple_of(kj * blk, blk)
            k_aug = k_slabs(koff)

            def q_fix(qi, corr, masked):
                qoff = pl.multiple_of(qi * blk, blk)
                return tuple(corr[h] + _colsum(probs(qoff, h, k_aug[h], masked) * res_ref[h, pl.ds(qoff, blk), 0:1])
                             for h in range(2))

            zero = jnp.zeros((1, blk), f32)
            corr = lax.fori_loop(kj + 1, nblk, lambda qi, cr: q_fix(qi, cr, False), q_fix(kj, (zero, zero), True))
            df_ref[0, 0:1, pl.ds(koff, blk)] += corr[0]
            df_ref[0, 1:2, pl.ds(koff, blk)] += corr[1]
            return 0

        lax.fori_loop(0, nblk, kv_fix, 0)

    pair_aug = pl.BlockSpec((SEQ, 2 * LANES), lambda p: (0, p))
    slab = pl.BlockSpec((SEQ, LANES), lambda p: (0, p))
    per_pair = pl.BlockSpec((1, SEQ, LANES), lambda p: (p, 0, 0))
    rows = pl.BlockSpec((1, 8, SEQ), lambda p: (p, 0, 0))
    return pl.pallas_call(
        body, grid=(npair,),
        in_specs=[pair_aug, pair_aug, slab, slab, slab, per_pair],
        out_specs=[slab, slab, slab, rows],
        out_shape=(SDS((SEQ, FOX_W), bf16),) * 3 + (SDS((npair, 8, SEQ), f32),), name="fox_bwd",
        scratch_shapes=[pltpu.VMEM((SEQ, LANES), f32), pltpu.VMEM((SEQ, LANES), f32),
                        pltpu.VMEM((2, SEQ, LANES), f32)],
        compiler_params=_params(("parallel",)),
    )(q_aug, k_aug, v, do, o, lse)


DIL_BLK = 128
N_GROUPS = 3
DIL_PAIRS = DIL_OUT_W // LANES
DIL_NBLK = SEQ // DIL_BLK


def _blocks_per_seq(g):
    return jnp.where(g == 0, 16, jnp.where(g == 1, 4, 1))


def _dil_block(n, g):
    rows = slice(n * DIL_BLK, (n + 1) * DIL_BLK)
    if n == 0:
        r = lax.broadcasted_iota(jnp.int32, (DIL_BLK, DIL_BLK), 0)
        c = lax.broadcasted_iota(jnp.int32, (DIL_BLK, DIL_BLK), 1)
        return rows, rows, r >= c
    r = lax.broadcasted_iota(jnp.int32, (DIL_BLK, 2 * DIL_BLK), 0)
    c = lax.broadcasted_iota(jnp.int32, (DIL_BLK, 2 * DIL_BLK), 1)
    has_prev = (n & (_blocks_per_seq(g) - 1)) > 0
    mask = ((c < DIL_BLK) & (c >= r) & has_prev) | ((c >= DIL_BLK) & (c - DIL_BLK <= r))
    return rows, slice((n - 1) * DIL_BLK, (n + 1) * DIL_BLK), mask


def _dil_spec():
    return pl.BlockSpec((1, SEQ, LANES), lambda g, p: (g, 0, p))


def _dil_fwd(q, k, v):
    def body(q_ref, k_ref, v_ref, o_ref, lse_ref):
        g = pl.program_id(0)
        first = lax.broadcasted_iota(jnp.int32, (DIL_BLK, LANES), 1) < HEAD_DIM
        for n in range(DIL_NBLK):
            rows, krows, mask = _dil_block(n, g)
            qv, kk, vv = q_ref[0, rows, :], k_ref[0, krows, :], v_ref[0, krows, :]
            outs, lses = [], []
            for own in (first, ~first):
                s = jnp.where(mask, _nt(jnp.where(own, qv, jnp.zeros_like(qv)), kk), NEG)
                m = jnp.max(s, axis=1, keepdims=True)
                p = jnp.exp(s - m)
                l = jnp.sum(p, axis=1, keepdims=True)
                outs.append(jnp.dot(p.astype(bf16), vv, preferred_element_type=f32) / l)
                lses.append(m + jnp.log(l))
            o_ref[0, rows, :] = jnp.where(first, outs[0], outs[1])
            lse_ref[0, rows, :] = jnp.where(first, lses[0], lses[1])

    spec = _dil_spec()
    shape = SDS((N_GROUPS, SEQ, DIL_OUT_W), f32)
    return pl.pallas_call(
        body, grid=(N_GROUPS, DIL_PAIRS), in_specs=[spec] * 3, out_specs=[spec] * 2,
        out_shape=(shape, shape), name="dil_fwd", compiler_params=_params(("parallel", "parallel")),
    )(q, k, v)


def _dil_bwd(q, k, v, do, lse, delta):
    def body(q_ref, k_ref, v_ref, do_ref, lse_ref, dl_ref, dq_ref, dk_ref, dv_ref):
        g = pl.program_id(0)
        first = lax.broadcasted_iota(jnp.int32, (DIL_BLK, LANES), 1) < HEAD_DIM
        dk_ref[...] = jnp.zeros_like(dk_ref)
        dv_ref[...] = jnp.zeros_like(dv_ref)
        for n in range(DIL_NBLK):
            rows, krows, mask = _dil_block(n, g)
            qv, kk, vv = q_ref[0, rows, :], k_ref[0, krows, :], v_ref[0, krows, :]
            dov = do_ref[0, rows, :].astype(bf16)
            lsev, delv = lse_ref[0, rows, :], dl_ref[0, rows, :]
            dqs, dk_add, dv_add = [], None, None
            for h, own in enumerate((first, ~first)):
                col = h * HEAD_DIM
                qh = jnp.where(own, qv, jnp.zeros_like(qv))
                doh = jnp.where(own, dov, jnp.zeros_like(dov))
                p = jnp.exp(jnp.where(mask, _nt(qh, kk), NEG) - lsev[:, col:col + 1])
                dl = (p * (_nt(doh, vv) - delv[:, col:col + 1])).astype(bf16)
                dqs.append(jnp.dot(dl, kk, preferred_element_type=f32))
                dk_h, dv_h = _tn(dl, qh), _tn(p.astype(bf16), doh)
                dk_add = dk_h if dk_add is None else dk_add + dk_h
                dv_add = dv_h if dv_add is None else dv_add + dv_h
            dq_ref[0, rows, :] = jnp.where(first, dqs[0], dqs[1]) * ATT_SCALE
            dk_ref[0, krows, :] += dk_add
            dv_ref[0, krows, :] += dv_add

    spec = _dil_spec()
    shape = SDS((N_GROUPS, SEQ, DIL_OUT_W), f32)
    return pl.pallas_call(
        body, grid=(N_GROUPS, DIL_PAIRS), in_specs=[spec] * 6, out_specs=[spec] * 3,
        out_shape=(shape, shape, shape), name="dil_bwd", compiler_params=_params(("parallel", "parallel")),
    )(q, k, v, do, lse, delta)


_DILATIONS = (1, 4, 16)


def _to_classes(t):
    w = t.shape[1] // N_GROUPS
    out = []
    for g, d in enumerate(_DILATIONS):
        s = t[:, g * w:(g + 1) * w]
        out.append(s.reshape(SEQ // d, d, w).transpose(1, 0, 2).reshape(SEQ, w))
    return jnp.stack(out)


def _from_classes(t):
    w = t.shape[2]
    out = [t[g].reshape(d, SEQ // d, w).transpose(1, 0, 2).reshape(SEQ, w) for g, d in enumerate(_DILATIONS)]
    return jnp.concatenate(out, axis=1)


def _position():
    return lax.axis_index("x"), lax.axis_index("y"), lax.axis_index("c")


def _all_gather(block, name):
    def body(x_ref, out_ref, send_sems, recv_sems, local_sem):
        x, y, c = _position()
        me, sibling = (x, y, c), (x, y, 1 - c)
        chips = [(1 - x, y), (x, 1 - y), (1 - x, 1 - y)]

        def slot(px, py, pc):
            return out_ref.at[4 * px + 2 * py + pc]

        def copy(k, blk, to, src=None):
            return pltpu.make_async_remote_copy(
                src_ref=slot(*blk) if src is None else src, dst_ref=slot(*blk),
                send_sem=send_sems.at[k], recv_sem=recv_sems.at[k], device_id=to, device_id_type=MESH)

        mine = pltpu.make_async_copy(x_ref, slot(*me), local_sem)
        mine.start()
        first = [copy(0, me, sibling, src=x_ref)]
        first += [copy(1 + j, me, (*chip, c), src=x_ref) for j, chip in enumerate(chips)]
        for cp in first:
            cp.start()
        passed = [copy(4 + j, (*chip, c), sibling) for j, chip in enumerate(chips)]
        for j, chip in enumerate(chips):
            copy(1 + j, (*chip, c), me).wait_recv()
            passed[j].start()
        copy(0, sibling, me).wait_recv()
        for j, chip in enumerate(chips):
            copy(4 + j, (*chip, 1 - c), me).wait_recv()
        for cp in first + passed:
            cp.wait_send()
        mine.wait()

    return pl.pallas_call(
        body, out_shape=SDS((N_DEV,) + block.shape, block.dtype),
        in_specs=[pl.BlockSpec(memory_space=pl.ANY)], out_specs=pl.BlockSpec(memory_space=pl.ANY),
        scratch_shapes=[pltpu.SemaphoreType.DMA((7,)), pltpu.SemaphoreType.DMA((7,)), pltpu.SemaphoreType.DMA],
        name=name,
    )(block)


def _all_gather_many(blocks, name):
    n = len(blocks)

    def body(*refs):
        x_refs, out_refs = refs[:n], refs[n:2 * n]
        send_sems, recv_sems, local_sems = refs[2 * n:]
        x, y, c = _position()
        me, sibling = (x, y, c), (x, y, 1 - c)
        chips = [(1 - x, y), (x, 1 - y), (1 - x, 1 - y)]

        def slot(a, px, py, pc):
            return out_refs[a].at[4 * px + 2 * py + pc]

        def copy(a, k, blk, to, own=False):
            return pltpu.make_async_remote_copy(
                src_ref=x_refs[a] if own else slot(a, *blk), dst_ref=slot(a, *blk),
                send_sem=send_sems.at[a, k], recv_sem=recv_sems.at[a, k], device_id=to, device_id_type=MESH)

        mine = [pltpu.make_async_copy(x_refs[a], slot(a, *me), local_sems.at[a]) for a in range(n)]
        for cp in mine:
            cp.start()
        started = []
        for a in range(n):
            first = [copy(a, 0, me, sibling, own=True)]
            first += [copy(a, 1 + j, me, (*chip, c), own=True) for j, chip in enumerate(chips)]
            for cp in first:
                cp.start()
            started += first
        for a in range(n):
            for j, chip in enumerate(chips):
                copy(a, 1 + j, (*chip, c), me).wait_recv()
                passed = copy(a, 4 + j, (*chip, c), sibling)
                passed.start()
                started.append(passed)
        for a in range(n):
            copy(a, 0, sibling, me).wait_recv()
            for j, chip in enumerate(chips):
                copy(a, 4 + j, (*chip, 1 - c), me).wait_recv()
        for cp in started:
            cp.wait_send()
        for cp in mine:
            cp.wait()

    hbm = pl.BlockSpec(memory_space=pl.ANY)
    return pl.pallas_call(
        body, out_shape=[SDS((N_DEV,) + b.shape, b.dtype) for b in blocks],
        in_specs=[hbm] * n, out_specs=[hbm] * n,
        scratch_shapes=[pltpu.SemaphoreType.DMA((n, 7)), pltpu.SemaphoreType.DMA((n, 7)),
                        pltpu.SemaphoreType.DMA((n,))],
        name=name,
    )(*blocks)


def _pair_exchange(gs):
    n = len(gs)

    def body(*refs):
        g_refs, r_refs, send_sems, recv_sems = refs[:n], refs[n:2 * n], refs[2 * n], refs[2 * n + 1]
        x, y, c = _position()
        copies = []
        for a in range(n):
            for k in range(4):
                cp = pltpu.make_async_remote_copy(
                    src_ref=g_refs[a].at[2 * k + (1 - c)], dst_ref=r_refs[a].at[k], send_sem=send_sems.at[a, k],
                    recv_sem=recv_sems.at[a, k], device_id=(x, y, 1 - c), device_id_type=MESH)
                cp.start()
                copies.append(cp)
        for cp in copies:
            cp.wait()

    hbm = pl.BlockSpec(memory_space=pl.ANY)
    return pl.pallas_call(
        body, out_shape=[SDS((4,) + g.shape[1:], g.dtype) for g in gs], in_specs=[hbm] * n, out_specs=[hbm] * n,
        scratch_shapes=[pltpu.SemaphoreType.DMA((n, 4)), pltpu.SemaphoreType.DMA((n, 4))], name="pair_exchange",
    )(*gs)


def _chip_exchange(ts):
    n = len(ts)

    def body(*refs):
        t_refs, r_refs, send_sems, recv_sems = refs[:n], refs[n:2 * n], refs[2 * n], refs[2 * n + 1]
        x, y, c = _position()
        chips = [(1 - x, y), (x, 1 - y), (1 - x, 1 - y)]
        copies = []
        for a in range(n):
            for j, (px, py) in enumerate(chips):
                cp = pltpu.make_async_remote_copy(
                    src_ref=t_refs[a].at[2 * px + py], dst_ref=r_refs[a].at[j], send_sem=send_sems.at[a, j],
                    recv_sem=recv_sems.at[a, j], device_id=(px, py, c), device_id_type=MESH)
                cp.start()
                copies.append(cp)
        for cp in copies:
            cp.wait()

    hbm = pl.BlockSpec(memory_space=pl.ANY)
    return pl.pallas_call(
        body, out_shape=[SDS((3,) + t.shape[1:], t.dtype) for t in ts], in_specs=[hbm] * n, out_specs=[hbm] * n,
        scratch_shapes=[pltpu.SemaphoreType.DMA((n, 3)), pltpu.SemaphoreType.DMA((n, 3))], name="chip_exchange",
    )(*ts)


def _row_tile(rows):
    return 256 if rows % 256 == 0 and rows > 512 else rows


def _pair_add(g, r1, core, name):
    def body(c_ref, g_ref, r_ref, o_ref):
        o_ref[...] = (g_ref[...].astype(f32) + r_ref[...].astype(f32)).astype(o_ref.dtype)

    rows, cols = g.shape[1:]
    tile = _row_tile(rows)
    blk = (1, tile, cols)
    return pl.pallas_call(
        body, out_shape=SDS((4, rows, cols), g.dtype), name=name,
        grid_spec=pltpu.PrefetchScalarGridSpec(
            num_scalar_prefetch=1, grid=(4, rows // tile),
            in_specs=[pl.BlockSpec(blk, lambda k, i, c_ref: (2 * k + c_ref[0], i, 0)),
                      pl.BlockSpec(blk, lambda k, i, c_ref: (k, i, 0))],
            out_specs=pl.BlockSpec(blk, lambda k, i, c_ref: (k, i, 0))),
        compiler_params=_params(("parallel", "arbitrary")),
    )(core, g, r1)


def _chip_add(t, r2, chip, name):
    def body(c_ref, t_ref, r_ref, o_ref):
        o_ref[...] = ((t_ref[0].astype(f32) + r_ref[0].astype(f32)) + r_ref[1].astype(f32)) + r_ref[2].astype(f32)

    rows, cols = t.shape[1:]
    tile = _row_tile(rows)
    return pl.pallas_call(
        body, out_shape=SDS((rows, cols), f32), name=name,
        grid_spec=pltpu.PrefetchScalarGridSpec(
            num_scalar_prefetch=1, grid=(rows // tile,),
            in_specs=[pl.BlockSpec((1, tile, cols), lambda i, c_ref: (c_ref[0], i, 0)),
                      pl.BlockSpec((3, tile, cols), lambda i, c_ref: (0, i, 0))],
            out_specs=pl.BlockSpec((tile, cols), lambda i, c_ref: (i, 0))),
        compiler_params=_params(("arbitrary",)),
    )(chip, t, r2)


def _pad_to(t, axis, size):
    pads = [(0, 0)] * t.ndim
    pads[axis] = (0, size - t.shape[axis])
    return jnp.pad(t, pads)


def _shard_pad_cols(t):
    rows = t.shape[0]
    return _pad_to(t.reshape(rows, N_DEV, W_IN_SH), 2, W_IN_PAD).reshape(rows, N_DEV * W_IN_PAD)


def _pad_w_in(w):
    qa, ka, va = w[:, 0:512], w[:, 512:1024], w[:, 1024:1536]
    fg = w[:, 1536:1544]
    qb, kb, vb = w[:, 1544:2312], w[:, 2312:3080], w[:, 3080:3848]
    ga, gb = w[:, 3848:4872], w[:, 4872:5896]
    z = lambda n: jnp.zeros((w.shape[0], n), w.dtype)
    return jnp.concatenate([ga, gb, z(C_QB - 2 * D), qb, kb, vb, qa, ka, va, fg, z(LANES - N_FOX_HEADS)], axis=1)


def kernel(x, c, w_ada, b_ada, g_mix, w_in, b_fgate, w_br_a, w_br_b, w_out, g_ffn, w_ffn_gate, w_ffn_up, w_ffn_down, g_final, loss_target, m_w_ada, m_b_ada, m_g_mix, m_w_in, m_b_fgate, m_w_br_a, m_w_br_b, m_w_out, m_g_ffn, m_w_ffn_gate, m_w_ffn_up, m_w_ffn_down, m_g_final, v_w_ada, v_b_ada, v_g_mix, v_w_in, v_b_fgate, v_w_br_a, v_w_br_b, v_w_out, v_g_ffn, v_w_ffn_gate, v_w_ffn_up, v_w_ffn_down, v_g_final):
    px, py, pc = _position()
    dev = 4 * px + 2 * py + pc
    x2d, tgt = x[0], loss_target[0]

    c_all = _all_gather(c, "gather_c").reshape(N_DEV, D)
    ada_cols = w_ada.shape[2]
    b_shard = lax.dynamic_slice(b_ada, (0, dev * ada_cols), (1, ada_cols))
    mod_shard = _ada_fwd(c_all, w_ada[0], b_shard)
    mod_all = _all_gather(mod_shard, "gather_mod")
    modv = lax.dynamic_index_in_dim(mod_all, dev, axis=1, keepdims=False).reshape(6, D)

    gate_up = jnp.concatenate([_pad_to(w_ffn_gate[0], 1, FF_PAD), _pad_to(w_ffn_up[0], 1, FF_PAD)], axis=1)
    shards = [_pad_to(w_in[0], 1, W_IN_PAD), w_br_a[0], w_br_b[0], w_out[0], gate_up, _pad_to(w_ffn_down[0], 0, FF_PAD)]
    w_in_s, w_a_s, w_b_s, w_o_s, w_gu_s, w_d_s = _all_gather_many([t.astype(bf16) for t in shards], "gather_weights")
    w_o = w_o_s.reshape(D, D)
    w_d = w_d_s.reshape(FF_HID, D)
    w_in_p = _pad_w_in(w_in_s[:, :, :W_IN_SH].transpose(1, 0, 2).reshape(D, IN_COLS))

    h1 = _pre1(x2d, modv, g_mix)
    proj = _matmul(h1, w_in_p, name="mm_proj", tm=SEQ, tn=896, tk=D)
    b_pad = jnp.pad(b_fgate, ((0, 0), (0, LANES - N_FOX_HEADS)))
    q_aug, k_aug, va = _fox_prep(proj, _fox_gate_fwd(proj, b_pad))
    ya_h, lse_a = _fox_fwd(q_aug, k_aug, va)
    ya = _matmul(ya_h, w_a_s, by_shard=True, name="mm_br_a", tm=SEQ, tn=W_BR_SH, tk=FOX_W)

    tables = _rope_tables()
    qb_r, kb_r, vb = _rope_fwd(proj, tables)
    q_c, k_c, v_c = _to_classes(qb_r), _to_classes(kb_r), _to_classes(vb)
    o_c, lse_c = _dil_fwd(q_c, k_c, v_c)
    yb_h, lse_b = _dil_combine(_from_classes(o_c), _from_classes(lse_c))
    yb = _matmul(yb_h, w_b_s, by_shard=True, name="mm_br_b", tm=SEQ, tn=W_BR_SH, tk=DIL_OUT_W)

    merged = _merge_fwd(ya, yb, proj)
    mix = _matmul(merged, w_o, name="mm_out", tm=SEQ, tn=512, tk=D)
    x1, h2 = _post1(x2d, mix, modv, g_ffn)
    au = _matmul(h2, w_gu_s, by_shard=True, name="mm_ffn_in", tm=SEQ, tn=2 * FF_PAD, tk=D)
    act = _swiglu_fwd(au)
    ff = _matmul(act, w_d, name="mm_ffn_down", tm=SEQ, tn=512, tk=FF_HID // 2)

    dx2, dff, dg_final, dga_f, loss_lanes = _final(x1, ff, tgt, modv, g_final.reshape(1, D))
    dact = _matmul(dff, w_d, tb=True, name="mm_d_act", tm=SEQ // 2, tn=FF_HID // 2, tk=D)
    dau = _swiglu_bwd(au, dact)
    dh2 = _matmul(dau, w_gu_s, tb=True, by_shard=True, name="mm_d_h2", tm=SEQ // 2, tn=D, tk=2 * FF_PAD)
    dx1, dmix, dsh_f, dsc_f, dg_ffn, dga_m = _mid_bwd(dh2, x1, dx2, mix, modv, g_ffn)
    dmerged = _matmul(dmix, w_o, tb=True, name="mm_d_merged", tm=SEQ, tn=512, tk=D)
    dya, dyb, dga, dgb = _merge_bwd(dmerged, ya, yb, proj)
    dya_h = _matmul(dya, w_a_s, tb=True, by_shard=True, name="mm_d_ya", tm=SEQ, tn=FOX_W, tk=W_BR_SH)
    dyb_h = _matmul(dyb, w_b_s, tb=True, by_shard=True, name="mm_d_yb", tm=SEQ, tn=DIL_OUT_W, tk=W_BR_SH)

    dqa, dka, dva, dF = _fox_bwd(q_aug, k_aug, va, dya_h, ya_h, lse_a)
    dF_row = jnp.pad(dF[:, :2, :].reshape(N_FOX_HEADS, SEQ), ((0, LANES - N_FOX_HEADS), (0, 0)))
    df, db_fgate = _fox_gate_bwd(dF_row, proj, b_pad)

    delta_b = _dil_delta(dyb_h, yb_h)
    rep = lambda t: _to_classes(jnp.tile(t, (1, N_GROUPS)))
    dq_c, dk_c, dv_c = _dil_bwd(q_c, k_c, v_c, rep(dyb_h), rep(lse_b), rep(delta_b))
    dqb, dkb = _rope_bwd(_from_classes(dq_c), _from_classes(dk_c), tables)
    dvb = _from_classes(dv_c).astype(bf16)

    dproj = _shard_pad_cols(jnp.concatenate([dqa, dka, dva, df[:, :N_FOX_HEADS], dqb, dkb, dvb, dga, dgb], axis=1))
    dh1 = _matmul(dproj, w_in_s, tb=True, by_shard=True, name="mm_d_h1", tm=SEQ // 2, tn=D, tk=W_IN_PAD)
    grad_x, dsh_m, dsc_m, dg_mix = _first_bwd(dh1, x2d, dx1, modv, g_mix)

    g_in = _matmul(h1, dproj, ta=True, by_shard=True, out_dtype=bf16, name="mm_g_in", tm=D, tn=W_IN_PAD, tk=SEQ)
    g_gu = _matmul(h2, dau, ta=True, by_shard=True, out_dtype=bf16, name="mm_g_ffn_in", tm=D, tn=2 * FF_PAD, tk=SEQ)
    g_d = _matmul(act, dff, ta=True, out_dtype=bf16, name="mm_g_down", tm=FF_HID // 2, tn=512, tk=SEQ)
    g_o = _matmul(merged, dmix, ta=True, out_dtype=bf16, name="mm_g_out", tm=D, tn=512, tk=SEQ)
    g_a = _matmul(ya_h, dya, ta=True, by_shard=True, out_dtype=bf16, name="mm_g_br_a", tm=FOX_W, tn=W_BR_SH, tk=SEQ)
    g_b = _matmul(yb_h, dyb, ta=True, by_shard=True, out_dtype=bf16, name="mm_g_br_b", tm=DIL_OUT_W, tn=W_BR_SH,
                  tk=SEQ)
    rows_a, rows_b = FOX_W * W_BR_SH // D, DIL_OUT_W * W_BR_SH // D
    g_small = jnp.concatenate([g_a.reshape(N_DEV, rows_a, D), g_b.reshape(N_DEV, rows_b, D),
                               g_o.reshape(N_DEV, W_BR_SH, D)], axis=1)
    by_dev = [g_in, g_gu, g_d.reshape(N_DEV, FF_PAD, D), g_small]

    core = pc.astype(jnp.int32).reshape(1)
    chip = (2 * px + py).astype(jnp.int32).reshape(1)
    tags = ["in", "gu", "down", "small"]
    from_pair = _pair_exchange(by_dev)
    partial_sums = [_pair_add(g, r, core, "pair_add_" + t) for g, r, t in zip(by_dev, from_pair, tags)]
    from_chips = _chip_exchange(partial_sums)
    s_in, s_gu, s_d, s_small = [_chip_add(p, r, chip, "chip_add_" + t)
                                for p, r, t in zip(partial_sums, from_chips, tags)]
    g_shard = {
        "w_in": s_in[:, :W_IN_SH], "w_ffn_gate": s_gu[:, :W_FF_SH], "w_ffn_up": s_gu[:, FF_PAD:FF_PAD + W_FF_SH],
        "w_ffn_down": s_d[:W_FF_SH], "w_br_a": s_small[:rows_a].reshape(FOX_W, W_BR_SH),
        "w_br_b": s_small[rows_a:rows_a + rows_b].reshape(DIL_OUT_W, W_BR_SH), "w_out": s_small[rows_a + rows_b:],
    }

    pad_lane = lambda t: jnp.pad(t, ((0, 0), (0, D - t.shape[1])))
    small = jnp.concatenate([dsh_m, dsc_m, dga_m, dsh_f, dsc_f, dga_f, dg_mix, dg_ffn, dg_final,
                             pad_lane(db_fgate), loss_lanes, jnp.zeros((SMALL_ROWS - 11, D), f32)], axis=0)
    small_all = _all_gather(small, "gather_small")
    small_sum, loss_row = _small_reduce(small_all)
    dmod_all = small_all[:, :6, :].reshape(N_DEV, 6 * D)
    g_w_ada = _ada_bwd(c_all, lax.dynamic_slice(dmod_all, (0, dev * ada_cols), (N_DEV, ada_cols)))

    loss = loss_row[0, 0]
    g = {
        "w_ada": g_w_ada[None], "b_ada": small_sum[0:6].reshape(1, 6 * D), "g_mix": small_sum[6:7],
        "w_in": g_shard["w_in"][None], "b_fgate": small_sum[9:10, :N_FOX_HEADS], "w_br_a": g_shard["w_br_a"][None],
        "w_br_b": g_shard["w_br_b"][None], "w_out": g_shard["w_out"][None], "g_ffn": small_sum[7:8],
        "w_ffn_gate": g_shard["w_ffn_gate"][None], "w_ffn_up": g_shard["w_ffn_up"][None],
        "w_ffn_down": g_shard["w_ffn_down"][None], "g_final": small_sum[8],
    }
    w = {"w_ada": w_ada, "b_ada": b_ada, "g_mix": g_mix, "w_in": w_in, "b_fgate": b_fgate, "w_br_a": w_br_a,
         "w_br_b": w_br_b, "w_out": w_out, "g_ffn": g_ffn, "w_ffn_gate": w_ffn_gate, "w_ffn_up": w_ffn_up,
         "w_ffn_down": w_ffn_down, "g_final": g_final}
    m = {"w_ada": m_w_ada, "b_ada": m_b_ada, "g_mix": m_g_mix, "w_in": m_w_in, "b_fgate": m_b_fgate,
         "w_br_a": m_w_br_a, "w_br_b": m_w_br_b, "w_out": m_w_out, "g_ffn": m_g_ffn, "w_ffn_gate": m_w_ffn_gate,
         "w_ffn_up": m_w_ffn_up, "w_ffn_down": m_w_ffn_down, "g_final": m_g_final}
    v = {"w_ada": v_w_ada, "b_ada": v_b_ada, "g_mix": v_g_mix, "w_in": v_w_in, "b_fgate": v_b_fgate,
         "w_br_a": v_w_br_a, "w_br_b": v_w_br_b, "w_out": v_w_out, "g_ffn": v_g_ffn, "w_ffn_gate": v_w_ffn_gate,
         "w_ffn_up": v_w_ffn_up, "w_ffn_down": v_w_ffn_down, "g_final": v_g_final}
    names = list(w)
    delta, new_m, new_v = {}, {}, {}
    for n in names:
        shape = w[n].shape
        two_d = (lambda t: t.reshape(shape[-2:])) if len(shape) == 3 else (lambda t: t)
        dl, mn, vn = _adamw(two_d(w[n]), two_d(g[n]), two_d(m[n]), two_d(v[n]), "adamw_" + n)
        delta[n], new_m[n], new_v[n] = dl.reshape(shape), mn.reshape(shape), vn.reshape(shape)

    return (loss, grad_x[None], *[g[n] for n in names], *[delta[n] for n in names],
            *[new_m[n] for n in names], *[new_v[n] for n in names])
```

```python
import functools

import jax
import jax.numpy as jnp
from jax import lax
from jax.experimental import pallas as pl
from jax.experimental.pallas import tpu as pltpu

f32 = jnp.float32
bf16 = jnp.bfloat16
SDS = jax.ShapeDtypeStruct
MESH = pl.DeviceIdType.MESH

N_DEV = 8
D = 1024
SEQ = 2048
HEAD_DIM = 64
N_FOX_HEADS = 8
FOX_W = 512
DIL_W = 768
DIL_OUT_W = 256
ROT_DIM = 16
ROPE_THETA = 500000.0
D_FF = 2816
IN_COLS = 5896
EPS = 1e-6
NEG = -1e30
ATT_SCALE = HEAD_DIM ** -0.5

ADAM_LR = 0.001
ADAM_B1 = 0.9
ADAM_B2 = 0.999
ADAM_EPS = 1e-08
ADAM_WD = 0.01
ADAM_STEP = 10

C_GA, C_GB, C_QB, C_KB, C_VB, C_QA, C_KA, C_VA, C_F = 0, 1024, 2304, 3072, 3840, 4608, 5120, 5632, 6144
PROJ_W = 6272
LANES = 128
VMEM_LIMIT = 52 * 1024 * 1024

W_IN_SH, W_IN_PAD = IN_COLS // N_DEV, 768
W_BR_SH = D // N_DEV
W_FF_SH, FF_PAD = D_FF // N_DEV, 384
FF_HID = N_DEV * FF_PAD
SMALL_ROWS = 16


def _params(sem=None):
    if sem is None:
        return pltpu.CompilerParams(vmem_limit_bytes=VMEM_LIMIT)
    return pltpu.CompilerParams(dimension_semantics=sem, vmem_limit_bytes=VMEM_LIMIT)


def _rowwise(fn, name, tiled, vecs, outs, reds=(), tile=256):
    nt, nv, no = len(tiled), len(vecs), len(outs)
    rows = tiled[0][0].shape[0]
    assert rows % tile == 0

    def body(*refs):
        tin = [r[...] for r in refs[:nt]]
        vin = [r[...] for r in refs[nt:nt + nv]]
        orefs = refs[nt + nv:nt + nv + no]
        rrefs = refs[nt + nv + no:]
        touts, routs = fn(tin, vin)
        for r, t in zip(orefs, touts, strict=True):
            r[...] = t.astype(r.dtype)
        if rrefs:
            @pl.when(pl.program_id(0) == 0)
            def _():
                for r in rrefs:
                    r[...] = jnp.zeros_like(r)
            for r, t in zip(rrefs, routs, strict=True):
                r[...] += t

    def col_map(cb):
        return lambda i: (i, cb)

    def whole_map(nd):
        return lambda i: (0,) * nd

    in_specs = [pl.BlockSpec((tile, w), col_map(cb)) for (_, w, cb) in tiled]
    in_specs += [pl.BlockSpec(v.shape, whole_map(v.ndim)) for v in vecs]
    out_specs = [pl.BlockSpec((tile, w), lambda i: (i, 0)) for (w, _) in outs]
    out_specs += [pl.BlockSpec((1, w), lambda i: (0, 0)) for w in reds]
    out_shape = [SDS((rows, w), dt) for (w, dt) in outs] + [SDS((1, w), f32) for w in reds]
    res = pl.pallas_call(
        body, grid=(rows // tile,), in_specs=in_specs, out_specs=out_specs, out_shape=out_shape, name=name,
        compiler_params=_params(("arbitrary",)),
    )(*[t[0] for t in tiled], *vecs)
    return res


def _matmul(a, b, *, ta=False, tb=False, out_dtype=f32, name, tm, tn, tk, by_shard=False, after=None):
    m, k = (a.shape[1], a.shape[0]) if ta else a.shape
    if by_shard and not ta:
        n, kb = (b.shape[1], N_DEV * b.shape[2]) if tb else (N_DEV * b.shape[2], b.shape[1])
        assert (tk if tb else tn) == b.shape[2]
    else:
        n, kb = (b.shape[0], b.shape[1]) if tb else (b.shape[1], b.shape[0])
    assert kb == k and m % tm == 0 and n % tn == 0 and k % tk == 0
    nk = k // tk
    dims = (((0 if ta else 1,), (1 if tb else 0,)), ((), ()))
    b_stacked = by_shard and not ta
    o_stacked = by_shard and ta

    def body(a_ref, b_ref, *rest):
        o_ref, *acc = rest[1:] if after is not None else rest
        bv = b_ref[0] if b_stacked else b_ref[...]
        p = lax.dot_general(a_ref[...].astype(bf16), bv.astype(bf16), dims, preferred_element_type=f32)

        def put(val):
            if o_stacked:
                o_ref[0] = val.astype(o_ref.dtype)
            else:
                o_ref[...] = val.astype(o_ref.dtype)

        if nk == 1:
            put(p)
        else:
            acc_ref, = acc
            kk = pl.program_id(2)

            @pl.when(kk == 0)
            def _():
                acc_ref[...] = p

            @pl.when(kk > 0)
            def _():
                acc_ref[...] += p

            @pl.when(kk == nk - 1)
            def _():
                put(acc_ref[...])

    a_spec = pl.BlockSpec((tk, tm), lambda i, j, kk: (kk, i)) if ta else pl.BlockSpec((tm, tk), lambda i, j, kk: (i, kk))
    if b_stacked and tb:
        b_spec = pl.BlockSpec((1, tn, tk), lambda i, j, kk: (kk, j, 0))
    elif b_stacked:
        b_spec = pl.BlockSpec((1, tk, tn), lambda i, j, kk: (j, kk, 0))
    elif tb:
        b_spec = pl.BlockSpec((tn, tk), lambda i, j, kk: (j, kk))
    else:
        b_spec = pl.BlockSpec((tk, tn), lambda i, j, kk: (kk, j))
    if o_stacked:
        assert tn == n // N_DEV
        out_spec = pl.BlockSpec((1, tm, tn), lambda i, j, kk: (j, i, 0))
        out_shape = SDS((N_DEV, m, tn), out_dtype)
    else:
        out_spec = pl.BlockSpec((tm, tn), lambda i, j, kk: (i, j))
        out_shape = SDS((m, n), out_dtype)
    extra_specs, extra = ([pl.BlockSpec(memory_space=pl.ANY)], [after]) if after is not None else ([], [])
    return pl.pallas_call(
        body, grid=(m // tm, n // tn, nk), in_specs=[a_spec, b_spec] + extra_specs, out_specs=out_spec,
        out_shape=out_shape, name=name, scratch_shapes=[pltpu.VMEM((tm, tn), f32)] if nk > 1 else [],
        compiler_params=_params(("parallel", "parallel", "arbitrary")),
    )(a, b, *extra)


def _rms(x):
    r = lax.rsqrt(jnp.mean(x * x, axis=-1, keepdims=True) + EPS)
    return r, x * r


def _rms_bwd(r, xn, dxn):
    return r * (dxn - xn * jnp.mean(dxn * xn, axis=-1, keepdims=True))


def _colsum(t):
    return jnp.sum(t, axis=0, keepdims=True)


def _sigmoid(x):
    return 1.0 / (1.0 + jnp.exp(-x))


def _modulated_norm(x, g, shift, scale):
    _, xn = _rms(x)
    return (xn * g) * (1.0 + scale) + shift


def _pre1(x, modv, g_mix):
    def fn(t, v):
        (xt,), (mv, g) = t, v
        return [_modulated_norm(xt, g, mv[0:1], mv[1:2])], []
    return _rowwise(fn, "pre1", [(x, D, 0)], [modv, g_mix], [(D, bf16)])[0]


def _post1(x, mix, modv, g_ffn):
    def fn(t, v):
        (xt, mt), (mv, g) = t, v
        x1 = xt + mv[2:3] * mt
        return [x1, _modulated_norm(x1, g, mv[3:4], mv[4:5])], []
    return _rowwise(fn, "post1", [(x, D, 0), (mix, D, 0)], [modv, g_ffn], [(D, f32), (D, bf16)])


def _gate_up(au, j):
    base = 2 * j * FF_PAD
    return au[:, base:base + FF_PAD], au[:, base + FF_PAD:base + 2 * FF_PAD]


def _swiglu_fwd(au):
    def fn(t, v):
        acts = []
        for j in range(N_DEV):
            a, u = _gate_up(t[0], j)
            acts.append(a * _sigmoid(a) * u)
        return [jnp.concatenate(acts, axis=1)], []
    return _rowwise(fn, "swiglu_fwd", [(au, 2 * FF_HID, 0)], [], [(FF_HID, bf16)])[0]


def _swiglu_bwd(au, dact):
    def fn(t, v):
        parts = []
        for j in range(N_DEV):
            a, u = _gate_up(t[0], j)
            d = t[1][:, j * FF_PAD:(j + 1) * FF_PAD]
            sg = _sigmoid(a)
            parts += [d * u * (sg * (1.0 + a * (1.0 - sg))), d * (a * sg)]
        return [jnp.concatenate(parts, axis=1)], []
    return _rowwise(fn, "swiglu_bwd", [(au, 2 * FF_HID, 0), (dact, FF_HID, 0)], [], [(2 * FF_HID, bf16)],
                    tile=128)[0]


def _final(x1, ff, target, modv, g_final):
    def fn(t, v):
        (x1t, fft, tgt), (mv, g) = t, v
        x2 = x1t + mv[5:6] * fft
        r, xn = _rms(x2)
        err = xn * g - tgt
        dy = err * (1.0 / D)
        dx2 = _rms_bwd(r, xn, dy * g)
        return [dx2, dx2 * mv[5:6]], [_colsum(dy * xn), _colsum(dx2 * fft), _colsum(err * err) * (0.5 / D)]
    return _rowwise(fn, "final", [(x1, D, 0), (ff, D, 0), (target, D, 0)], [modv, g_final],
                    [(D, f32), (D, bf16)], [D, D, D])


def _mid_bwd(dh2, x1, dx2, mix, modv, g_ffn):
    def fn(t, v):
        (dh, x1t, dx2t, mt), (mv, g) = t, v
        r, xn = _rms(x1t)
        dn = dh * (1.0 + mv[4:5])
        dx1 = dx2t + _rms_bwd(r, xn, dn * g)
        return [dx1, dx1 * mv[2:3]], [_colsum(dh), _colsum(dh * (xn * g)), _colsum(dn * xn), _colsum(dx1 * mt)]
    return _rowwise(fn, "mid_bwd", [(dh2, D, 0), (x1, D, 0), (dx2, D, 0), (mix, D, 0)], [modv, g_ffn],
                    [(D, f32), (D, bf16)], [D, D, D, D])


def _first_bwd(dh1, x, dx1, modv, g_mix):
    def fn(t, v):
        (dh, xt, dx1t), (mv, g) = t, v
        r, xn = _rms(xt)
        dn = dh * (1.0 + mv[1:2])
        return [dx1t + _rms_bwd(r, xn, dn * g)], [_colsum(dh), _colsum(dh * (xn * g)), _colsum(dn * xn)]
    return _rowwise(fn, "first_bwd", [(dh1, D, 0), (x, D, 0), (dx1, D, 0)], [modv, g_mix], [(D, f32)], [D, D, D])


def _merge_fwd(ya, yb, proj):
    def fn(t, v):
        ya_t, yb_t, ga, gb = t
        return [_sigmoid(ga) * ya_t + _sigmoid(gb) * yb_t], []
    return _rowwise(fn, "merge_fwd", [(ya, D, 0), (yb, D, 0), (proj, D, C_GA // D), (proj, D, C_GB // D)], [],
                    [(D, bf16)])[0]


def _merge_bwd(dmerged, ya, yb, proj):
    def fn(t, v):
        dm, ya_t, yb_t, ga, gb = t
        sa, sb = _sigmoid(ga), _sigmoid(gb)
        return [dm * sa, dm * sb, dm * ya_t * (sa * (1.0 - sa)), dm * yb_t * (sb * (1.0 - sb))], []
    return _rowwise(fn, "merge_bwd",
                    [(dmerged, D, 0), (ya, D, 0), (yb, D, 0), (proj, D, C_GA // D), (proj, D, C_GB // D)], [],
                    [(D, bf16), (D, bf16), (D, bf16), (D, bf16)])


def _rope_tables():
    half = ROT_DIM // 2
    pos = jnp.arange(SEQ, dtype=f32)
    inv_freq = ROPE_THETA ** (-jnp.arange(0, ROT_DIM, 2, dtype=f32) / ROT_DIM)
    ang = pos[:, None] * inv_freq[None, :]
    cos, sin = jnp.cos(ang), jnp.sin(ang)
    pad = jnp.zeros((SEQ, HEAD_DIM - ROT_DIM), f32)
    zero = jnp.zeros((SEQ, half), f32)
    c_head = jnp.concatenate([cos, cos, pad + 1.0], axis=1)
    lo_head = jnp.concatenate([-sin, zero, pad], axis=1)
    hi_head = jnp.concatenate([zero, sin, pad], axis=1)
    reps = DIL_W // HEAD_DIM
    return tuple(jnp.tile(t, (1, reps)) for t in (c_head, lo_head, hi_head))


def _rope_fwd(proj, tables):
    half = ROT_DIM // 2

    def fn(t, v):
        q, k, vv, c, lo, hi = t
        rot = lambda z: z * c + pltpu.roll(z, DIL_W - half, 1) * lo + pltpu.roll(z, half, 1) * hi
        return [rot(q) * ATT_SCALE, rot(k), vv], []
    return _rowwise(fn, "rope_fwd", [(proj, DIL_W, C_QB // DIL_W), (proj, DIL_W, C_KB // DIL_W),
                                     (proj, DIL_W, C_VB // DIL_W)] + [(tb, DIL_W, 0) for tb in tables], [],
                    [(DIL_W, bf16)] * 3)


def _rope_bwd(dq, dk, tables):
    half = ROT_DIM // 2

    def fn(t, v):
        dq_t, dk_t, c, lo, hi = t
        rot_t = lambda z: z * c + pltpu.roll(z * lo, half, 1) + pltpu.roll(z * hi, DIL_W - half, 1)
        return [rot_t(dq_t), rot_t(dk_t)], []
    return _rowwise(fn, "rope_bwd", [(dq, DIL_W, 0), (dk, DIL_W, 0)] + [(tb, DIL_W, 0) for tb in tables], [],
                    [(DIL_W, bf16), (DIL_W, bf16)])


def _head_bcast_sum(d):
    lane = lax.broadcasted_iota(jnp.int32, d.shape, 1)
    out = jnp.zeros_like(d)
    for h in range(d.shape[1] // HEAD_DIM):
        sel = (lane >= h * HEAD_DIM) & (lane < (h + 1) * HEAD_DIM)
        out = jnp.where(sel, jnp.sum(jnp.where(sel, d, 0.0), axis=1, keepdims=True), out)
    return out


def _dil_combine(o, lse):
    def fn(t, v):
        o0, o1, o2, l0, l1, l2 = t
        m = jnp.maximum(jnp.maximum(l0, l1), l2)
        w0, w1, w2 = jnp.exp(l0 - m), jnp.exp(l1 - m), jnp.exp(l2 - m)
        tot = w0 + w1 + w2
        return [(w0 * o0 + w1 * o1 + w2 * o2) / tot, m + jnp.log(tot)], []
    w = DIL_OUT_W
    return _rowwise(fn, "dil_combine", [(o, w, 0), (o, w, 1), (o, w, 2), (lse, w, 0), (lse, w, 1), (lse, w, 2)], [],
                    [(w, f32), (w, f32)])


def _dil_delta(dyb_h, yb_h):
    def fn(t, v):
        return [_head_bcast_sum(t[0] * t[1])], []
    return _rowwise(fn, "dil_delta", [(dyb_h, DIL_OUT_W, 0), (yb_h, DIL_OUT_W, 0)], [], [(DIL_OUT_W, f32)])[0]


def _adamw(w, g, m, v, name):
    shape = w.shape
    if w.ndim == 1:
        w, g, m, v = (t.reshape(1, -1) for t in (w, g, m, v))
    rows, cols = w.shape
    tile = 256 if rows % 256 == 0 and rows > 512 else rows

    def fn(t, _):
        wt, gt, mt, vt = t
        mn = ADAM_B1 * mt + (1.0 - ADAM_B1) * gt
        vn = ADAM_B2 * vt + (1.0 - ADAM_B2) * (gt * gt)
        m_hat = mn / (1.0 - ADAM_B1 ** ADAM_STEP)
        v_hat = vn / (1.0 - ADAM_B2 ** ADAM_STEP)
        return [-ADAM_LR * (m_hat / (jnp.sqrt(v_hat) + ADAM_EPS) + ADAM_WD * wt), mn, vn], []
    delta, mn, vn = _rowwise(fn, name, [(w, cols, 0), (g, cols, 0), (m, cols, 0), (v, cols, 0)], [],
                             [(cols, f32)] * 3, tile=tile)
    return delta.reshape(shape), mn.reshape(shape), vn.reshape(shape)


def _ada_fwd(c_all, w_shard, b_shard):
    def body(c_ref, w_ref, b_ref, o_ref):
        cv = c_ref[...]
        sc = (cv * _sigmoid(cv)).astype(bf16)
        o_ref[...] = jnp.dot(sc, w_ref[...].astype(bf16), preferred_element_type=f32) + b_ref[...]
    return pl.pallas_call(body, out_shape=SDS((N_DEV, w_shard.shape[1]), f32), name="ada_fwd",
                          compiler_params=_params())(c_all, w_shard, b_shard)


def _ada_bwd(c_all, dmod_cols):
    def body(c_ref, d_ref, o_ref):
        cv = c_ref[...]
        sc = cv * _sigmoid(cv)
        o_ref[...] = lax.dot_general(sc, d_ref[...], (((0,), (0,)), ((), ())), precision=lax.Precision.HIGHEST,
                                     preferred_element_type=f32)
    return pl.pallas_call(body, out_shape=SDS((D, dmod_cols.shape[1]), f32), name="ada_bwd",
                          compiler_params=_params())(c_all, dmod_cols)


def _small_reduce(gathered):
    def body(g_ref, o_ref, loss_ref):
        acc = g_ref[0]
        for d in range(1, N_DEV):
            acc = acc + g_ref[d]
        o_ref[...] = acc
        loss_ref[...] = jnp.zeros((1, LANES), f32) + jnp.sum(acc[10:11, :])
    return pl.pallas_call(body, out_shape=(SDS((SMALL_ROWS, D), f32), SDS((1, LANES), f32)), name="small_reduce",
                          compiler_params=_params())(gathered)


FOX_BLK = 512
CUM_BLK = 128


def _fold_lanes(t, op):
    out = t[:, :LANES]
    for j in range(1, t.shape[1] // LANES):
        out = op(out, t[:, j * LANES:(j + 1) * LANES])
    return out


def _fox_gate_fwd(proj, b_pad):
    nblk = SEQ // CUM_BLK

    def body(f_ref, b_ref, col_ref):
        r = lax.broadcasted_iota(jnp.int32, (CUM_BLK, CUM_BLK), 0)
        c = lax.broadcasted_iota(jnp.int32, (CUM_BLK, CUM_BLK), 1)
        tri = (r >= c).astype(f32)
        carry = jnp.zeros((1, LANES), f32)
        for blk in range(nblk):
            z = f_ref[blk * CUM_BLK:(blk + 1) * CUM_BLK, :] + b_ref[...]
            logf = jnp.minimum(z, 0.0) - jnp.log1p(jnp.exp(-jnp.abs(z)))
            cs = jnp.dot(tri, logf, precision=lax.Precision.HIGHEST, preferred_element_type=f32) + carry
            col_ref[blk * CUM_BLK:(blk + 1) * CUM_BLK, :] = cs
            carry = cs[CUM_BLK - 1:CUM_BLK, :]

    return pl.pallas_call(
        body, grid=(1,), in_specs=[pl.BlockSpec((SEQ, LANES), lambda i: (0, C_F // LANES)),
                                   pl.BlockSpec((1, LANES), lambda i: (0, 0))],
        out_specs=pl.BlockSpec((SEQ, LANES), lambda i: (0, 0)),
        out_shape=SDS((SEQ, LANES), f32), name="fox_gate_fwd",
        compiler_params=_params(("arbitrary",)),
    )(proj, b_pad)


def _fox_gate_bwd(dF_row, proj, b_pad):
    nblk = SEQ // CUM_BLK

    def body(d_ref, f_ref, b_ref, df_ref, db_ref, col_ref):
        r = lax.broadcasted_iota(jnp.int32, (CUM_BLK, CUM_BLK), 0)
        c = lax.broadcasted_iota(jnp.int32, (CUM_BLK, CUM_BLK), 1)
        tri = (r <= c).astype(f32)
        lane = lax.broadcasted_iota(jnp.int32, (CUM_BLK, LANES), 1)
        col_ref[...] = d_ref[...].T
        carry = jnp.zeros((1, LANES), f32)
        total = jnp.zeros((1, LANES), f32)
        for blk in reversed(range(nblk)):
            rows = slice(blk * CUM_BLK, (blk + 1) * CUM_BLK)
            cs = jnp.dot(tri, col_ref[rows, :], precision=lax.Precision.HIGHEST, preferred_element_type=f32) + carry
            carry = cs[0:1, :]
            z = f_ref[rows, :] + b_ref[...]
            df = jnp.where(lane < N_FOX_HEADS, cs * _sigmoid(-z), 0.0)
            df_ref[rows, :] = df.astype(df_ref.dtype)
            total = total + _colsum(df)
        db_ref[...] = total

    return pl.pallas_call(
        body, grid=(1,), in_specs=[pl.BlockSpec((LANES, SEQ), lambda i: (0, 0)),
                                   pl.BlockSpec((SEQ, LANES), lambda i: (0, C_F // LANES)),
                                   pl.BlockSpec((1, LANES), lambda i: (0, 0))],
        out_specs=[pl.BlockSpec((SEQ, LANES), lambda i: (0, 0)), pl.BlockSpec((1, LANES), lambda i: (0, 0))],
        out_shape=(SDS((SEQ, LANES), bf16), SDS((1, LANES), f32)), name="fox_gate_bwd",
        scratch_shapes=[pltpu.VMEM((SEQ, LANES), f32)],
        compiler_params=_params(("arbitrary",)),
    )(dF_row, proj, b_pad)


def _nt(a, b):
    return lax.dot_general(a, b, (((1,), (1,)), ((), ())), preferred_element_type=f32)


def _tn(a, b):
    return lax.dot_general(a, b, (((0,), (0,)), ((), ())), preferred_element_type=f32)


def _fox_prep(proj, f_col):
    def fn(t, v):
        q, k, vv, fc = t
        lane = lax.broadcasted_iota(jnp.int32, (q.shape[0], LANES), 1)
        qs, ks = [], []
        for h in range(N_FOX_HEADS):
            pair, pos = divmod(h, 2)
            own = (lane >= pos * HEAD_DIM) & (lane < (pos + 1) * HEAD_DIM)
            base = (1 - pos) * HEAD_DIM
            f = fc[:, h:h + 1]
            hi = f.astype(bf16).astype(f32)
            mid = (f - hi).astype(bf16).astype(f32)
            lo = (f - hi) - mid
            one = jnp.ones_like(f)
            qa = jnp.where(own, q[:, pair * LANES:(pair + 1) * LANES] * ATT_SCALE, 0.0)
            ka = k[:, pair * LANES:(pair + 1) * LANES]
            for idx, (qv, kv) in enumerate([(hi, one), (mid, one), (lo, one), (one, -hi), (one, -mid), (one, -lo)]):
                sel = lane == base + idx
                qa = jnp.where(sel, qv, qa)
                ka = jnp.where(sel, kv, ka)
            qs.append(qa)
            ks.append(ka)
        return [jnp.concatenate(qs, axis=1), jnp.concatenate(ks, axis=1), vv], []
    w = N_FOX_HEADS * LANES
    return _rowwise(fn, "fox_prep", [(proj, FOX_W, C_QA // FOX_W), (proj, FOX_W, C_KA // FOX_W),
                                     (proj, FOX_W, C_VA // FOX_W), (f_col, LANES, 0)], [],
                    [(w, bf16), (w, bf16), (FOX_W, bf16)])


def _fox_fwd(q_aug, k_aug, v):
    blk = FOX_BLK
    npair = FOX_W // LANES

    def body(q_ref, k_ref, v_ref, o_ref, lse_ref, s_scr):
        i = pl.program_id(1)
        tri = lax.broadcasted_iota(jnp.int32, (blk, blk), 0) >= lax.broadcasted_iota(jnp.int32, (blk, blk), 1)
        qh = [q_ref[:, h * LANES:(h + 1) * LANES] for h in range(2)]

        def logits(c, masked):
            off = pl.multiple_of(c * blk, blk)
            tops = []
            for h in range(2):
                s = _nt(qh[h], k_ref[pl.ds(off, blk), h * LANES:(h + 1) * LANES])
                if masked:
                    s = jnp.where(tri, s, NEG)
                s_scr[h, :, pl.ds(off, blk)] = s
                tops.append(_fold_lanes(s, jnp.maximum))
            return tops

        def pass_a(c, m):
            return tuple(jnp.maximum(a, b) for a, b in zip(m, logits(c, False)))

        m = lax.fori_loop(0, i, pass_a, tuple(jnp.full((blk, LANES), NEG, f32) for _ in range(2)))
        mx = [jnp.max(jnp.maximum(a, b), axis=1, keepdims=True) for a, b in zip(m, logits(i, True))]

        def pass_b(c, carry):
            off = pl.multiple_of(c * blk, blk)
            vv = v_ref[pl.ds(off, blk), :]
            new = []
            for h in range(2):
                l, acc = carry[h]
                p = jnp.exp(s_scr[h, :, pl.ds(off, blk)] - mx[h])
                new.append((l + _fold_lanes(p, jnp.add),
                            acc + jnp.dot(p.astype(bf16), vv, preferred_element_type=f32)))
            return tuple(new)

        zero = jnp.zeros((blk, LANES), f32)
        (l_a, acc_a), (l_b, acc_b) = lax.fori_loop(0, i + 1, pass_b, ((zero, zero), (zero, zero)))
        l_a = jnp.sum(l_a, axis=1, keepdims=True)
        l_b = jnp.sum(l_b, axis=1, keepdims=True)
        first = lax.broadcasted_iota(jnp.int32, (blk, LANES), 1) < HEAD_DIM
        o_ref[...] = jnp.where(first, acc_a / l_a, acc_b / l_b)
        lse_ref[0] = jnp.where(first, mx[0] + jnp.log(l_a), mx[1] + jnp.log(l_b))

    return pl.pallas_call(
        body, grid=(npair, SEQ // blk),
        in_specs=[pl.BlockSpec((blk, 2 * LANES), lambda p, i: (i, p)),
                  pl.BlockSpec((SEQ, 2 * LANES), lambda p, i: (0, p)),
                  pl.BlockSpec((SEQ, LANES), lambda p, i: (0, p))],
        out_specs=[pl.BlockSpec((blk, LANES), lambda p, i: (i, p)),
                   pl.BlockSpec((1, blk, LANES), lambda p, i: (p, i, 0))],
        out_shape=(SDS((SEQ, FOX_W), f32), SDS((npair, SEQ, LANES), f32)), name="fox_fwd",
        scratch_shapes=[pltpu.VMEM((2, blk, SEQ), f32)],
        compiler_params=_params(("parallel", "arbitrary")),
    )(q_aug, k_aug, v)


def _fox_bwd(q_aug, k_aug, v, do, o, lse):
    blk = FOX_BLK
    npair = FOX_W // LANES
    nblk = SEQ // blk

    def body(q_ref, k_ref, v_ref, do_ref, o_ref, lse_ref, dq_ref, dk_ref, dv_ref, df_ref,
             dq_acc, delta_ref, res_ref):
        lane_s = lax.broadcasted_iota(jnp.int32, (SEQ, LANES), 1)
        prod = do_ref[...] * o_ref[...]
        d_a = jnp.sum(jnp.where(lane_s < HEAD_DIM, prod, 0.0), axis=1, keepdims=True)
        d_b = jnp.sum(jnp.where(lane_s >= HEAD_DIM, prod, 0.0), axis=1, keepdims=True)
        delta_ref[...] = jnp.where(lane_s < HEAD_DIM, d_a, d_b)
        dq_acc[...] = jnp.zeros_like(dq_acc)
        res_ref[...] = jnp.zeros_like(res_ref)
        df_ref[...] = jnp.zeros_like(df_ref)
        lane = lax.broadcasted_iota(jnp.int32, (blk, LANES), 1)
        own = [lane < HEAD_DIM, lane >= HEAD_DIM]
        tri = lax.broadcasted_iota(jnp.int32, (blk, blk), 0) >= lax.broadcasted_iota(jnp.int32, (blk, blk), 1)

        def q_slab(qoff, h):
            return q_ref[pl.ds(qoff, blk), h * LANES:(h + 1) * LANES]

        def probs(qoff, h, k_h, masked):
            s = _nt(q_slab(qoff, h), k_h)
            if masked:
                s = jnp.where(tri, s, NEG)
            return jnp.exp(s - lse_ref[0, pl.ds(qoff, blk), h * HEAD_DIM:h * HEAD_DIM + 1])

        def k_slabs(koff):
            return [k_ref[pl.ds(koff, blk), h * LANES:(h + 1) * LANES] for h in range(2)]

        def kv_step(kj, _):
            koff = pl.multiple_of(kj * blk, blk)
            k_aug = k_slabs(koff)
            k_own = [jnp.where(own[h], k_aug[h], jnp.zeros_like(k_aug[h])) for h in range(2)]
            vv = v_ref[pl.ds(koff, blk), :]
            v_own = [jnp.where(own[h], vv, jnp.zeros_like(vv)) for h in range(2)]

            def q_tile(qi, carry, masked):
                qoff = pl.multiple_of(qi * blk, blk)
                dd = do_ref[pl.ds(qoff, blk), :].astype(bf16)
                new, dq_add = [], None
                for h in range(2):
                    dk_h, dv_h, dcol = carry[h]
                    p = probs(qoff, h, k_aug[h], masked)
                    dl = p * (_nt(dd, v_own[h]) - delta_ref[pl.ds(qoff, blk), h * HEAD_DIM:h * HEAD_DIM + 1])
                    dlb = dl.astype(bf16)
                    part = jnp.dot(dlb, k_own[h], preferred_element_type=f32)
                    dq_add = part if dq_add is None else dq_add + part
                    res_ref[h, pl.ds(qoff, blk), :] += _fold_lanes(dl, jnp.add)
                    new.append((dk_h + _tn(dlb, q_slab(qoff, h)), dv_h + _tn(p.astype(bf16), dd),
                                dcol + _colsum(dl)))
                dq_acc[pl.ds(qoff, blk), :] += dq_add * ATT_SCALE
                return tuple(new)

            zero = (jnp.zeros((blk, LANES), f32), jnp.zeros((blk, LANES), f32), jnp.zeros((1, blk), f32))
            carry = q_tile(kj, (zero, zero), True)
            (dk_a, dv_a, dcol_a), (dk_b, dv_b, dcol_b) = lax.fori_loop(
                kj + 1, nblk, lambda qi, cr: q_tile(qi, cr, False), carry)
            dk_ref[pl.ds(koff, blk), :] = jnp.where(own[0], dk_a, dk_b).astype(dk_ref.dtype)
            dv_ref[pl.ds(koff, blk), :] = jnp.where(own[0], dv_a, dv_b).astype(dv_ref.dtype)
            df_ref[0, 0:1, pl.ds(koff, blk)] = -dcol_a
            df_ref[0, 1:2, pl.ds(koff, blk)] = -dcol_b
            return 0

        lax.fori_loop(0, nblk, kv_step, 0)
        dq_ref[...] = dq_acc[...].astype(dq_ref.dtype)

        for h in range(2):
            res_ref[h] = jnp.zeros((SEQ, LANES), f32) + jnp.sum(res_ref[h], axis=1, keepdims=True)

        def kv_fix(kj, _):
            koff = pl.multiple_of(kj * blk, blk)
            k_aug = k_slabs(koff)

            def q_fix(qi, corr, masked):
                qoff = pl.multiple_of(qi * blk, blk)
                return tuple(corr[h] + _colsum(probs(qoff, h, k_aug[h], masked) * res_ref[h, pl.ds(qoff, blk), 0:1])
                             for h in range(2))

            zero = jnp.zeros((1, blk), f32)
            corr = lax.fori_loop(kj + 1, nblk, lambda qi, cr: q_fix(qi, cr, False), q_fix(kj, (zero, zero), True))
            df_ref[0, 0:1, pl.ds(koff, blk)] += corr[0]
            df_ref[0, 1:2, pl.ds(koff, blk)] += corr[1]
            return 0

        lax.fori_loop(0, nblk, kv_fix, 0)

    pair_aug = pl.BlockSpec((SEQ, 2 * LANES), lambda p: (0, p))
    slab = pl.BlockSpec((SEQ, LANES), lambda p: (0, p))
    per_pair = pl.BlockSpec((1, SEQ, LANES), lambda p: (p, 0, 0))
    rows = pl.BlockSpec((1, 8, SEQ), lambda p: (p, 0, 0))
    return pl.pallas_call(
        body, grid=(npair,),
        in_specs=[pair_aug, pair_aug, slab, slab, slab, per_pair],
        out_specs=[slab, slab, slab, rows],
        out_shape=(SDS((SEQ, FOX_W), bf16),) * 3 + (SDS((npair, 8, SEQ), f32),), name="fox_bwd",
        scratch_shapes=[pltpu.VMEM((SEQ, LANES), f32), pltpu.VMEM((SEQ, LANES), f32),
                        pltpu.VMEM((2, SEQ, LANES), f32)],
        compiler_params=_params(("parallel",)),
    )(q_aug, k_aug, v, do, o, lse)


DIL_BLK = 128
N_GROUPS = 3
DIL_PAIRS = DIL_OUT_W // LANES
DIL_NBLK = SEQ // DIL_BLK


def _blocks_per_seq(g):
    return jnp.where(g == 0, 16, jnp.where(g == 1, 4, 1))


def _dil_block(n, g):
    rows = slice(n * DIL_BLK, (n + 1) * DIL_BLK)
    if n == 0:
        r = lax.broadcasted_iota(jnp.int32, (DIL_BLK, DIL_BLK), 0)
        c = lax.broadcasted_iota(jnp.int32, (DIL_BLK, DIL_BLK), 1)
        return rows, rows, r >= c
    r = lax.broadcasted_iota(jnp.int32, (DIL_BLK, 2 * DIL_BLK), 0)
    c = lax.broadcasted_iota(jnp.int32, (DIL_BLK, 2 * DIL_BLK), 1)
    has_prev = (n & (_blocks_per_seq(g) - 1)) > 0
    mask = ((c < DIL_BLK) & (c >= r) & has_prev) | ((c >= DIL_BLK) & (c - DIL_BLK <= r))
    return rows, slice((n - 1) * DIL_BLK, (n + 1) * DIL_BLK), mask


def _dil_spec():
    return pl.BlockSpec((1, SEQ, LANES), lambda g, p: (g, 0, p))


def _dil_fwd(q, k, v):
    def body(q_ref, k_ref, v_ref, o_ref, lse_ref):
        g = pl.program_id(0)
        first = lax.broadcasted_iota(jnp.int32, (DIL_BLK, LANES), 1) < HEAD_DIM
        for n in range(DIL_NBLK):
            rows, krows, mask = _dil_block(n, g)
            qv, kk, vv = q_ref[0, rows, :], k_ref[0, krows, :], v_ref[0, krows, :]
            outs, lses = [], []
            for own in (first, ~first):
                s = jnp.where(mask, _nt(jnp.where(own, qv, jnp.zeros_like(qv)), kk), NEG)
                m = jnp.max(s, axis=1, keepdims=True)
                p = jnp.exp(s - m)
                l = jnp.sum(p, axis=1, keepdims=True)
                outs.append(jnp.dot(p.astype(bf16), vv, preferred_element_type=f32) / l)
                lses.append(m + jnp.log(l))
            o_ref[0, rows, :] = jnp.where(first, outs[0], outs[1])
            lse_ref[0, rows, :] = jnp.where(first, lses[0], lses[1])

    spec = _dil_spec()
    shape = SDS((N_GROUPS, SEQ, DIL_OUT_W), f32)
    return pl.pallas_call(
        body, grid=(N_GROUPS, DIL_PAIRS), in_specs=[spec] * 3, out_specs=[spec] * 2,
        out_shape=(shape, shape), name="dil_fwd", compiler_params=_params(("parallel", "parallel")),
    )(q, k, v)


def _dil_bwd(q, k, v, do, lse, delta):
    def body(q_ref, k_ref, v_ref, do_ref, lse_ref, dl_ref, dq_ref, dk_ref, dv_ref):
        g = pl.program_id(0)
        first = lax.broadcasted_iota(jnp.int32, (DIL_BLK, LANES), 1) < HEAD_DIM
        dk_ref[...] = jnp.zeros_like(dk_ref)
        dv_ref[...] = jnp.zeros_like(dv_ref)
        for n in range(DIL_NBLK):
            rows, krows, mask = _dil_block(n, g)
            qv, kk, vv = q_ref[0, rows, :], k_ref[0, krows, :], v_ref[0, krows, :]
            dov = do_ref[0, rows, :].astype(bf16)
            lsev, delv = lse_ref[0, rows, :], dl_ref[0, rows, :]
            dqs, dk_add, dv_add = [], None, None
            for h, own in enumerate((first, ~first)):
                col = h * HEAD_DIM
                qh = jnp.where(own, qv, jnp.zeros_like(qv))
                doh = jnp.where(own, dov, jnp.zeros_like(dov))
                p = jnp.exp(jnp.where(mask, _nt(qh, kk), NEG) - lsev[:, col:col + 1])
                dl = (p * (_nt(doh, vv) - delv[:, col:col + 1])).astype(bf16)
                dqs.append(jnp.dot(dl, kk, preferred_element_type=f32))
                dk_h, dv_h = _tn(dl, qh), _tn(p.astype(bf16), doh)
                dk_add = dk_h if dk_add is None else dk_add + dk_h
                dv_add = dv_h if dv_add is None else dv_add + dv_h
            dq_ref[0, rows, :] = jnp.where(first, dqs[0], dqs[1]) * ATT_SCALE
            dk_ref[0, krows, :] += dk_add
            dv_ref[0, krows, :] += dv_add

    spec = _dil_spec()
    shape = SDS((N_GROUPS, SEQ, DIL_OUT_W), f32)
    return pl.pallas_call(
        body, grid=(N_GROUPS, DIL_PAIRS), in_specs=[spec] * 6, out_specs=[spec] * 3,
        out_shape=(shape, shape, shape), name="dil_bwd", compiler_params=_params(("parallel", "parallel")),
    )(q, k, v, do, lse, delta)


_DILATIONS = (1, 4, 16)


def _to_classes(t):
    w = t.shape[1] // N_GROUPS
    out = []
    for g, d in enumerate(_DILATIONS):
        s = t[:, g * w:(g + 1) * w]
        out.append(s.reshape(SEQ // d, d, w).transpose(1, 0, 2).reshape(SEQ, w))
    return jnp.stack(out)


def _from_classes(t):
    w = t.shape[2]
    out = [t[g].reshape(d, SEQ // d, w).transpose(1, 0, 2).reshape(SEQ, w) for g, d in enumerate(_DILATIONS)]
    return jnp.concatenate(out, axis=1)


def _position():
    return lax.axis_index("x"), lax.axis_index("y"), lax.axis_index("c")


def _all_gather(block, name):
    def body(x_ref, out_ref, send_sems, recv_sems, local_sem):
        x, y, c = _position()
        me, sibling = (x, y, c), (x, y, 1 - c)
        chips = [(1 - x, y), (x, 1 - y), (1 - x, 1 - y)]

        def slot(px, py, pc):
            return out_ref.at[4 * px + 2 * py + pc]

        def copy(k, blk, to, src=None):
            return pltpu.make_async_remote_copy(
                src_ref=slot(*blk) if src is None else src, dst_ref=slot(*blk),
                send_sem=send_sems.at[k], recv_sem=recv_sems.at[k], device_id=to, device_id_type=MESH)

        mine = pltpu.make_async_copy(x_ref, slot(*me), local_sem)
        mine.start()
        first = [copy(0, me, sibling, src=x_ref)]
        first += [copy(1 + j, me, (*chip, c), src=x_ref) for j, chip in enumerate(chips)]
        for cp in first:
            cp.start()
        passed = [copy(4 + j, (*chip, c), sibling) for j, chip in enumerate(chips)]
        for j, chip in enumerate(chips):
            copy(1 + j, (*chip, c), me).wait_recv()
            passed[j].start()
        copy(0, sibling, me).wait_recv()
        for j, chip in enumerate(chips):
            copy(4 + j, (*chip, 1 - c), me).wait_recv()
        for cp in first + passed:
            cp.wait_send()
        mine.wait()

    return pl.pallas_call(
        body, out_shape=SDS((N_DEV,) + block.shape, block.dtype),
        in_specs=[pl.BlockSpec(memory_space=pl.ANY)], out_specs=pl.BlockSpec(memory_space=pl.ANY),
        scratch_shapes=[pltpu.SemaphoreType.DMA((7,)), pltpu.SemaphoreType.DMA((7,)), pltpu.SemaphoreType.DMA],
        name=name,
    )(block)


def _all_gather_many(blocks, name):
    n = len(blocks)

    def body(*refs):
        x_refs, out_refs = refs[:n], refs[n:2 * n]
        send_sems, recv_sems, local_sems = refs[2 * n:]
        x, y, c = _position()
        me, sibling = (x, y, c), (x, y, 1 - c)
        chips = [(1 - x, y), (x, 1 - y), (1 - x, 1 - y)]

        def slot(a, px, py, pc):
            return out_refs[a].at[4 * px + 2 * py + pc]

        def copy(a, k, blk, to, own=False):
            return pltpu.make_async_remote_copy(
                src_ref=x_refs[a] if own else slot(a, *blk), dst_ref=slot(a, *blk),
                send_sem=send_sems.at[a, k], recv_sem=recv_sems.at[a, k], device_id=to, device_id_type=MESH)

        mine = [pltpu.make_async_copy(x_refs[a], slot(a, *me), local_sems.at[a]) for a in range(n)]
        for cp in mine:
            cp.start()
        started = []
        for a in range(n):
            first = [copy(a, 0, me, sibling, own=True)]
            first += [copy(a, 1 + j, me, (*chip, c), own=True) for j, chip in enumerate(chips)]
            for cp in first:
                cp.start()
            started += first
        for a in range(n):
            for j, chip in enumerate(chips):
                copy(a, 1 + j, (*chip, c), me).wait_recv()
                passed = copy(a, 4 + j, (*chip, c), sibling)
                passed.start()
                started.append(passed)
        for a in range(n):
            copy(a, 0, sibling, me).wait_recv()
            for j, chip in enumerate(chips):
                copy(a, 4 + j, (*chip, 1 - c), me).wait_recv()
        for cp in started:
            cp.wait_send()
        for cp in mine:
            cp.wait()

    hbm = pl.BlockSpec(memory_space=pl.ANY)
    return pl.pallas_call(
        body, out_shape=[SDS((N_DEV,) + b.shape, b.dtype) for b in blocks],
        in_specs=[hbm] * n, out_specs=[hbm] * n,
        scratch_shapes=[pltpu.SemaphoreType.DMA((n, 7)), pltpu.SemaphoreType.DMA((n, 7)),
                        pltpu.SemaphoreType.DMA((n,))],
        name=name,
    )(*blocks)


def _pair_exchange(gs, name):
    n = len(gs)

    def body(*refs):
        g_refs, r_refs, send_sems, recv_sems = refs[:n], refs[n:2 * n], refs[2 * n], refs[2 * n + 1]
        x, y, c = _position()
        copies = []
        for a in range(n):
            for k in range(4):
                cp = pltpu.make_async_remote_copy(
                    src_ref=g_refs[a].at[2 * k + (1 - c)], dst_ref=r_refs[a].at[k], send_sem=send_sems.at[a, k],
                    recv_sem=recv_sems.at[a, k], device_id=(x, y, 1 - c), device_id_type=MESH)
                cp.start()
                copies.append(cp)
        for cp in copies:
            cp.wait()

    hbm = pl.BlockSpec(memory_space=pl.ANY)
    return pl.pallas_call(
        body, out_shape=[SDS((4,) + g.shape[1:], g.dtype) for g in gs], in_specs=[hbm] * n, out_specs=[hbm] * n,
        scratch_shapes=[pltpu.SemaphoreType.DMA((n, 4)), pltpu.SemaphoreType.DMA((n, 4))], name=name,
    )(*gs)


HBM_SPEC = pl.BlockSpec(memory_space=pltpu.HBM)
SEM_SPEC = pl.BlockSpec(memory_space=pltpu.SEMAPHORE)
SPLIT_COPY = pltpu.CompilerParams(has_side_effects=pltpu.SideEffectType.DATAFLOW_SIDE_EFFECTING)


def _in_hbm(t):
    return pltpu.with_memory_space_constraint(t, pltpu.HBM)


def _chip_copies(t_refs, land_refs, send_sems, recv_sems):
    x, y, c = _position()
    chips = [(1 - x, y), (x, 1 - y), (1 - x, 1 - y)]
    return [pltpu.make_async_remote_copy(
        src_ref=t.at[2 * px + py], dst_ref=land.at[j], send_sem=send_sems.at[3 * a + j],
        recv_sem=recv_sems.at[3 * a + j], device_id=(px, py, c), device_id_type=MESH)
        for a, (t, land) in enumerate(zip(t_refs, land_refs, strict=True)) for j, (px, py) in enumerate(chips)]


def _chip_exchange_start(ts, name):
    n = len(ts)
    lands = [_in_hbm(lax.empty((3,) + t.shape[1:], t.dtype)) for t in ts]

    def body(*refs):
        send_sems, recv_sems = refs[2 * n], refs[2 * n + 1]
        for cp in _chip_copies(refs[:n], refs[n:2 * n], send_sems, recv_sems):
            cp.start()
        refs[-1][...] = jnp.zeros_like(refs[-1])

    sems = pltpu.SemaphoreType.DMA((3 * n,))
    res = pl.pallas_call(
        body, name=name, in_specs=[HBM_SPEC] * (2 * n),
        out_shape=(sems, sems, *[pltpu.HBM(t.shape, t.dtype) for t in (*ts, *lands)], SDS((8, LANES), f32)),
        out_specs=(SEM_SPEC, SEM_SPEC, *[HBM_SPEC] * (2 * n), pl.BlockSpec(memory_space=pltpu.VMEM)),
        input_output_aliases={i: 2 + i for i in range(2 * n)}, compiler_params=SPLIT_COPY,
    )(*[_in_hbm(t) for t in ts], *lands)
    return res[:-1], res[-1]


def _chip_exchange_wait(state, after, name):
    send_sems, recv_sems, *arrays = state
    n = len(arrays) // 2

    def body(*refs):
        for cp in _chip_copies(refs[:n], refs[n:2 * n], refs[2 * n], refs[2 * n + 1]):
            cp.wait_send()
            cp.wait_recv()

    res = pl.pallas_call(
        body, name=name, in_specs=[HBM_SPEC] * (2 * n) + [SEM_SPEC, SEM_SPEC, pl.BlockSpec(memory_space=pl.ANY)],
        out_shape=[pltpu.HBM(t.shape, t.dtype) for t in arrays], out_specs=[HBM_SPEC] * (2 * n),
        input_output_aliases={i: i for i in range(2 * n)}, compiler_params=SPLIT_COPY,
    )(*arrays, send_sems, recv_sems, after)
    return res[n:]


def _row_tile(rows):
    return 256 if rows % 256 == 0 and rows > 512 else rows


def _pair_add(g, r1, core, name):
    def body(c_ref, g_ref, r_ref, o_ref):
        o_ref[...] = (g_ref[...].astype(f32) + r_ref[...].astype(f32)).astype(o_ref.dtype)

    rows, cols = g.shape[1:]
    tile = _row_tile(rows)
    blk = (1, tile, cols)
    return pl.pallas_call(
        body, out_shape=SDS((4, rows, cols), g.dtype), name=name,
        grid_spec=pltpu.PrefetchScalarGridSpec(
            num_scalar_prefetch=1, grid=(4, rows // tile),
            in_specs=[pl.BlockSpec(blk, lambda k, i, c_ref: (2 * k + c_ref[0], i, 0)),
                      pl.BlockSpec(blk, lambda k, i, c_ref: (k, i, 0))],
            out_specs=pl.BlockSpec(blk, lambda k, i, c_ref: (k, i, 0))),
        compiler_params=_params(("parallel", "arbitrary")),
    )(core, g, r1)


def _chip_add(t, r2, chip, name):
    def body(c_ref, t_ref, r_ref, o_ref):
        o_ref[...] = ((t_ref[0].astype(f32) + r_ref[0].astype(f32)) + r_ref[1].astype(f32)) + r_ref[2].astype(f32)

    rows, cols = t.shape[1:]
    tile = _row_tile(rows)
    return pl.pallas_call(
        body, out_shape=SDS((rows, cols), f32), name=name,
        grid_spec=pltpu.PrefetchScalarGridSpec(
            num_scalar_prefetch=1, grid=(rows // tile,),
            in_specs=[pl.BlockSpec((1, tile, cols), lambda i, c_ref: (c_ref[0], i, 0)),
                      pl.BlockSpec((3, tile, cols), lambda i, c_ref: (0, i, 0))],
            out_specs=pl.BlockSpec((tile, cols), lambda i, c_ref: (i, 0))),
        compiler_params=_params(("arbitrary",)),
    )(chip, t, r2)


def _pad_to(t, axis, size):
    pads = [(0, 0)] * t.ndim
    pads[axis] = (0, size - t.shape[axis])
    return jnp.pad(t, pads)


def _shard_pad_cols(t):
    rows = t.shape[0]
    return _pad_to(t.reshape(rows, N_DEV, W_IN_SH), 2, W_IN_PAD).reshape(rows, N_DEV * W_IN_PAD)


def _pad_w_in(w):
    qa, ka, va = w[:, 0:512], w[:, 512:1024], w[:, 1024:1536]
    fg = w[:, 1536:1544]
    qb, kb, vb = w[:, 1544:2312], w[:, 2312:3080], w[:, 3080:3848]
    ga, gb = w[:, 3848:4872], w[:, 4872:5896]
    z = lambda n: jnp.zeros((w.shape[0], n), w.dtype)
    return jnp.concatenate([ga, gb, z(C_QB - 2 * D), qb, kb, vb, qa, ka, va, fg, z(LANES - N_FOX_HEADS)], axis=1)


def kernel(x, c, w_ada, b_ada, g_mix, w_in, b_fgate, w_br_a, w_br_b, w_out, g_ffn, w_ffn_gate, w_ffn_up, w_ffn_down, g_final, loss_target, m_w_ada, m_b_ada, m_g_mix, m_w_in, m_b_fgate, m_w_br_a, m_w_br_b, m_w_out, m_g_ffn, m_w_ffn_gate, m_w_ffn_up, m_w_ffn_down, m_g_final, v_w_ada, v_b_ada, v_g_mix, v_w_in, v_b_fgate, v_w_br_a, v_w_br_b, v_w_out, v_g_ffn, v_w_ffn_gate, v_w_ffn_up, v_w_ffn_down, v_g_final):
    px, py, pc = _position()
    dev = 4 * px + 2 * py + pc
    x2d, tgt = x[0], loss_target[0]

    c_all = _all_gather(c, "gather_c").reshape(N_DEV, D)
    ada_cols = w_ada.shape[2]
    b_shard = lax.dynamic_slice(b_ada, (0, dev * ada_cols), (1, ada_cols))
    mod_shard = _ada_fwd(c_all, w_ada[0], b_shard)
    mod_all = _all_gather(mod_shard, "gather_mod")
    modv = lax.dynamic_index_in_dim(mod_all, dev, axis=1, keepdims=False).reshape(6, D)

    gate_up = jnp.concatenate([_pad_to(w_ffn_gate[0], 1, FF_PAD), _pad_to(w_ffn_up[0], 1, FF_PAD)], axis=1)
    shards = [_pad_to(w_in[0], 1, W_IN_PAD), w_br_a[0], w_br_b[0], w_out[0], gate_up, _pad_to(w_ffn_down[0], 0, FF_PAD)]
    w_in_s, w_a_s, w_b_s, w_o_s, w_gu_s, w_d_s = _all_gather_many([t.astype(bf16) for t in shards], "gather_weights")
    w_o = w_o_s.reshape(D, D)
    w_d = w_d_s.reshape(FF_HID, D)
    w_in_p = _pad_w_in(w_in_s[:, :, :W_IN_SH].transpose(1, 0, 2).reshape(D, IN_COLS))

    h1 = _pre1(x2d, modv, g_mix)
    proj = _matmul(h1, w_in_p, name="mm_proj", tm=SEQ, tn=896, tk=D)
    b_pad = jnp.pad(b_fgate, ((0, 0), (0, LANES - N_FOX_HEADS)))
    q_aug, k_aug, va = _fox_prep(proj, _fox_gate_fwd(proj, b_pad))
    ya_h, lse_a = _fox_fwd(q_aug, k_aug, va)
    ya = _matmul(ya_h, w_a_s, by_shard=True, name="mm_br_a", tm=SEQ, tn=W_BR_SH, tk=FOX_W)

    tables = _rope_tables()
    qb_r, kb_r, vb = _rope_fwd(proj, tables)
    q_c, k_c, v_c = _to_classes(qb_r), _to_classes(kb_r), _to_classes(vb)
    o_c, lse_c = _dil_fwd(q_c, k_c, v_c)
    yb_h, lse_b = _dil_combine(_from_classes(o_c), _from_classes(lse_c))
    yb = _matmul(yb_h, w_b_s, by_shard=True, name="mm_br_b", tm=SEQ, tn=W_BR_SH, tk=DIL_OUT_W)

    merged = _merge_fwd(ya, yb, proj)
    mix = _matmul(merged, w_o, name="mm_out", tm=SEQ, tn=512, tk=D)
    x1, h2 = _post1(x2d, mix, modv, g_ffn)
    au = _matmul(h2, w_gu_s, by_shard=True, name="mm_ffn_in", tm=SEQ, tn=2 * FF_PAD, tk=D)
    act = _swiglu_fwd(au)
    ff = _matmul(act, w_d, name="mm_ffn_down", tm=SEQ, tn=512, tk=FF_HID // 2)

    dx2, dff, dg_final, dga_f, loss_lanes = _final(x1, ff, tgt, modv, g_final.reshape(1, D))
    dact = _matmul(dff, w_d, tb=True, name="mm_d_act", tm=SEQ // 2, tn=FF_HID // 2, tk=D)
    dau = _swiglu_bwd(au, dact)

    core = pc.astype(jnp.int32).reshape(1)
    chip = (2 * px + py).astype(jnp.int32).reshape(1)

    def to_chips(by_dev, tags, name):
        from_pair = _pair_exchange(by_dev, "pair_exchange_" + name)
        sums = [_pair_add(g, r, core, "pair_add_" + t) for g, r, t in zip(by_dev, from_pair, tags)]
        state, token = _chip_exchange_start(sums, "chip_exchange_start_" + name)
        return sums, state, token

    def from_chips(sums, state, after, tags, name):
        got = _chip_exchange_wait(state, after, "chip_exchange_wait_" + name)
        return [_chip_add(p, r, chip, "chip_add_" + t) for p, r, t in zip(sums, got, tags)]

    g_gu = _matmul(h2, dau, ta=True, by_shard=True, out_dtype=bf16, name="mm_g_ffn_in", tm=D, tn=2 * FF_PAD, tk=SEQ)
    g_d = _matmul(act, dff, ta=True, out_dtype=bf16, name="mm_g_down", tm=FF_HID // 2, tn=512, tk=SEQ)
    ffn_tags = ["gu", "down"]
    ffn_sums, ffn_state, ffn_token = to_chips([g_gu, g_d.reshape(N_DEV, FF_PAD, D)], ffn_tags, "ffn")

    dh2 = _matmul(dau, w_gu_s, tb=True, by_shard=True, name="mm_d_h2", tm=SEQ // 2, tn=D, tk=2 * FF_PAD,
                  after=ffn_token)
    dx1, dmix, dsh_f, dsc_f, dg_ffn, dga_m = _mid_bwd(dh2, x1, dx2, mix, modv, g_ffn)
    dmerged = _matmul(dmix, w_o, tb=True, name="mm_d_merged", tm=SEQ, tn=512, tk=D)
    dya, dyb, dga, dgb = _merge_bwd(dmerged, ya, yb, proj)
    dya_h = _matmul(dya, w_a_s, tb=True, by_shard=True, name="mm_d_ya", tm=SEQ, tn=FOX_W, tk=W_BR_SH)
    dyb_h = _matmul(dyb, w_b_s, tb=True, by_shard=True, name="mm_d_yb", tm=SEQ, tn=DIL_OUT_W, tk=W_BR_SH)

    dqa, dka, dva, dF = _fox_bwd(q_aug, k_aug, va, dya_h, ya_h, lse_a)
    dF_row = jnp.pad(dF[:, :2, :].reshape(N_FOX_HEADS, SEQ), ((0, LANES - N_FOX_HEADS), (0, 0)))
    df, db_fgate = _fox_gate_bwd(dF_row, proj, b_pad)

    delta_b = _dil_delta(dyb_h, yb_h)
    rep = lambda t: _to_classes(jnp.tile(t, (1, N_GROUPS)))
    dq_c, dk_c, dv_c = _dil_bwd(q_c, k_c, v_c, rep(dyb_h), rep(lse_b), rep(delta_b))
    dqb, dkb = _rope_bwd(_from_classes(dq_c), _from_classes(dk_c), tables)
    dvb = _from_classes(dv_c).astype(bf16)

    dproj = _shard_pad_cols(jnp.concatenate([dqa, dka, dva, df[:, :N_FOX_HEADS], dqb, dkb, dvb, dga, dgb], axis=1))
    g_in = _matmul(h1, dproj, ta=True, by_shard=True, out_dtype=bf16, name="mm_g_in", tm=D, tn=W_IN_PAD, tk=SEQ)
    g_o = _matmul(merged, dmix, ta=True, out_dtype=bf16, name="mm_g_out", tm=D, tn=512, tk=SEQ)
    g_a = _matmul(ya_h, dya, ta=True, by_shard=True, out_dtype=bf16, name="mm_g_br_a", tm=FOX_W, tn=W_BR_SH, tk=SEQ)
    g_b = _matmul(yb_h, dyb, ta=True, by_shard=True, out_dtype=bf16, name="mm_g_br_b", tm=DIL_OUT_W, tn=W_BR_SH,
                  tk=SEQ)
    rows_a, rows_b = FOX_W * W_BR_SH // D, DIL_OUT_W * W_BR_SH // D
    g_small = jnp.concatenate([g_a.reshape(N_DEV, rows_a, D), g_b.reshape(N_DEV, rows_b, D),
                               g_o.reshape(N_DEV, W_BR_SH, D)], axis=1)
    mix_tags = ["in", "small"]
    mix_sums, mix_state, mix_token = to_chips([g_in, g_small], mix_tags, "mixer")

    dh1 = _matmul(dproj, w_in_s, tb=True, by_shard=True, name="mm_d_h1", tm=SEQ // 2, tn=D, tk=W_IN_PAD,
                  after=mix_token)
    grad_x, dsh_m, dsc_m, dg_mix = _first_bwd(dh1, x2d, dx1, modv, g_mix)

    pad_lane = lambda t: jnp.pad(t, ((0, 0), (0, D - t.shape[1])))
    small = jnp.concatenate([dsh_m, dsc_m, dga_m, dsh_f, dsc_f, dga_f, dg_mix, dg_ffn, dg_final,
                             pad_lane(db_fgate), loss_lanes, jnp.zeros((SMALL_ROWS - 11, D), f32)], axis=0)
    small_all = _all_gather(small, "gather_small")
    small_sum, loss_row = _small_reduce(small_all)
    dmod_all = small_all[:, :6, :].reshape(N_DEV, 6 * D)
    g_w_ada = _ada_bwd(c_all, lax.dynamic_slice(dmod_all, (0, dev * ada_cols), (N_DEV, ada_cols)))

    s_gu, s_d = from_chips(ffn_sums, ffn_state, small_sum, ffn_tags, "ffn")
    s_in, s_small = from_chips(mix_sums, mix_state, g_w_ada, mix_tags, "mixer")
    g_shard = {
        "w_in": s_in[:, :W_IN_SH], "w_ffn_gate": s_gu[:, :W_FF_SH], "w_ffn_up": s_gu[:, FF_PAD:FF_PAD + W_FF_SH],
        "w_ffn_down": s_d[:W_FF_SH], "w_br_a": s_small[:rows_a].reshape(FOX_W, W_BR_SH),
        "w_br_b": s_small[rows_a:rows_a + rows_b].reshape(DIL_OUT_W, W_BR_SH), "w_out": s_small[rows_a + rows_b:],
    }

    loss = loss_row[0, 0]
    g = {
        "w_ada": g_w_ada[None], "b_ada": small_sum[0:6].reshape(1, 6 * D), "g_mix": small_sum[6:7],
        "w_in": g_shard["w_in"][None], "b_fgate": small_sum[9:10, :N_FOX_HEADS], "w_br_a": g_shard["w_br_a"][None],
        "w_br_b": g_shard["w_br_b"][None], "w_out": g_shard["w_out"][None], "g_ffn": small_sum[7:8],
        "w_ffn_gate": g_shard["w_ffn_gate"][None], "w_ffn_up": g_shard["w_ffn_up"][None],
        "w_ffn_down": g_shard["w_ffn_down"][None], "g_final": small_sum[8],
    }
    w = {"w_ada": w_ada, "b_ada": b_ada, "g_mix": g_mix, "w_in": w_in, "b_fgate": b_fgate, "w_br_a": w_br_a,
         "w_br_b": w_br_b, "w_out": w_out, "g_ffn": g_ffn, "w_ffn_gate": w_ffn_gate, "w_ffn_up": w_ffn_up,
         "w_ffn_down": w_ffn_down, "g_final": g_final}
    m = {"w_ada": m_w_ada, "b_ada": m_b_ada, "g_mix": m_g_mix, "w_in": m_w_in, "b_fgate": m_b_fgate,
         "w_br_a": m_w_br_a, "w_br_b": m_w_br_b, "w_out": m_w_out, "g_ffn": m_g_ffn, "w_ffn_gate": m_w_ffn_gate,
         "w_ffn_up": m_w_ffn_up, "w_ffn_down": m_w_ffn_down, "g_final": m_g_final}
    v = {"w_ada": v_w_ada, "b_ada": v_b_ada, "g_mix": v_g_mix, "w_in": v_w_in, "b_fgate": v_b_fgate,
         "w_br_a": v_w_br_a, "w_br_b": v_w_br_b, "w_out": v_w_out, "g_ffn": v_g_ffn, "w_ffn_gate": v_w_ffn_gate,
         "w_ffn_up": v_w_ffn_up, "w_ffn_down": v_w_ffn_down, "g_final": v_g_final}
    names = list(w)
    delta, new_m, new_v = {}, {}, {}
    for n in names:
        shape = w[n].shape
        two_d = (lambda t: t.reshape(shape[-2:])) if len(shape) == 3 else (lambda t: t)
        dl, mn, vn = _adamw(two_d(w[n]), two_d(g[n]), two_d(m[n]), two_d(v[n]), "adamw_" + n)
        delta[n], new_m[n], new_v[n] = dl.reshape(shape), mn.reshape(shape), vn.reshape(shape)

    return (loss, grad_x[None], *[g[n] for n in names], *[delta[n] for n in names],
            *[new_m[n] for n in names], *[new_v[n] for n in names])
```

```python
import functools

import jax
import jax.numpy as jnp
from jax import lax
from jax.experimental import pallas as pl
from jax.experimental.pallas import tpu as pltpu

f32 = jnp.float32
bf16 = jnp.bfloat16
SDS = jax.ShapeDtypeStruct
MESH = pl.DeviceIdType.MESH

N_DEV = 8
D = 1024
SEQ = 2048
HEAD_DIM = 64
N_FOX_HEADS = 8
FOX_W = 512
DIL_W = 768
DIL_OUT_W = 256
ROT_DIM = 16
ROPE_THETA = 500000.0
D_FF = 2816
IN_COLS = 5896
EPS = 1e-6
NEG = -1e30
ATT_SCALE = HEAD_DIM ** -0.5

ADAM_LR = 0.001
ADAM_B1 = 0.9
ADAM_B2 = 0.999
ADAM_EPS = 1e-08
ADAM_WD = 0.01
ADAM_STEP = 10

C_GA, C_GB, C_QB, C_KB, C_VB, C_QA, C_KA, C_VA, C_F = 0, 1024, 2304, 3072, 3840, 4608, 5120, 5632, 6144
PROJ_W = 6272
LANES = 128
VMEM_LIMIT = 52 * 1024 * 1024

W_IN_SH, W_IN_PAD = IN_COLS // N_DEV, 768
W_BR_SH = D // N_DEV
W_FF_SH, FF_PAD = D_FF // N_DEV, 384
FF_HID = N_DEV * FF_PAD
SMALL_ROWS = 16


def _params(sem=None):
    if sem is None:
        return pltpu.CompilerParams(vmem_limit_bytes=VMEM_LIMIT)
    return pltpu.CompilerParams(dimension_semantics=sem, vmem_limit_bytes=VMEM_LIMIT)


def _rowwise(fn, name, tiled, vecs, outs, reds=(), tile=256):
    nt, nv, no = len(tiled), len(vecs), len(outs)
    rows = tiled[0][0].shape[0]
    assert rows % tile == 0

    def body(*refs):
        tin = [r[...] for r in refs[:nt]]
        vin = [r[...] for r in refs[nt:nt + nv]]
        orefs = refs[nt + nv:nt + nv + no]
        rrefs = refs[nt + nv + no:]
        touts, routs = fn(tin, vin)
        for r, t in zip(orefs, touts, strict=True):
            r[...] = t.astype(r.dtype)
        if rrefs:
            @pl.when(pl.program_id(0) == 0)
            def _():
                for r in rrefs:
                    r[...] = jnp.zeros_like(r)
            for r, t in zip(rrefs, routs, strict=True):
                r[...] += t

    def col_map(cb):
        return lambda i: (i, cb)

    def whole_map(nd):
        return lambda i: (0,) * nd

    in_specs = [pl.BlockSpec((tile, w), col_map(cb)) for (_, w, cb) in tiled]
    in_specs += [pl.BlockSpec(v.shape, whole_map(v.ndim)) for v in vecs]
    out_specs = [pl.BlockSpec((tile, w), lambda i: (i, 0)) for (w, _) in outs]
    out_specs += [pl.BlockSpec((1, w), lambda i: (0, 0)) for w in reds]
    out_shape = [SDS((rows, w), dt) for (w, dt) in outs] + [SDS((1, w), f32) for w in reds]
    res = pl.pallas_call(
        body, grid=(rows // tile,), in_specs=in_specs, out_specs=out_specs, out_shape=out_shape, name=name,
        compiler_params=_params(("arbitrary",)),
    )(*[t[0] for t in tiled], *vecs)
    return res


def _matmul(a, b, *, ta=False, tb=False, out_dtype=f32, name, tm, tn, tk, by_shard=False, after=None):
    m, k = (a.shape[1], a.shape[0]) if ta else a.shape
    if by_shard and not ta:
        n, kb = (b.shape[1], N_DEV * b.shape[2]) if tb else (N_DEV * b.shape[2], b.shape[1])
        assert (tk if tb else tn) == b.shape[2]
    else:
        n, kb = (b.shape[0], b.shape[1]) if tb else (b.shape[1], b.shape[0])
    assert kb == k and m % tm == 0 and n % tn == 0 and k % tk == 0
    nk = k // tk
    dims = (((0 if ta else 1,), (1 if tb else 0,)), ((), ()))
    b_stacked = by_shard and not ta
    o_stacked = by_shard and ta

    def body(a_ref, b_ref, *rest):
        o_ref, *acc = rest[1:] if after is not None else rest
        bv = b_ref[0] if b_stacked else b_ref[...]
        p = lax.dot_general(a_ref[...].astype(bf16), bv.astype(bf16), dims, preferred_element_type=f32)

        def put(val):
            if o_stacked:
                o_ref[0] = val.astype(o_ref.dtype)
            else:
                o_ref[...] = val.astype(o_ref.dtype)

        if nk == 1:
            put(p)
        else:
            acc_ref, = acc
            kk = pl.program_id(2)

            @pl.when(kk == 0)
            def _():
                acc_ref[...] = p

            @pl.when(kk > 0)
            def _():
                acc_ref[...] += p

            @pl.when(kk == nk - 1)
            def _():
                put(acc_ref[...])

    a_spec = pl.BlockSpec((tk, tm), lambda i, j, kk: (kk, i)) if ta else pl.BlockSpec((tm, tk), lambda i, j, kk: (i, kk))
    if b_stacked and tb:
        b_spec = pl.BlockSpec((1, tn, tk), lambda i, j, kk: (kk, j, 0))
    elif b_stacked:
        b_spec = pl.BlockSpec((1, tk, tn), lambda i, j, kk: (j, kk, 0))
    elif tb:
        b_spec = pl.BlockSpec((tn, tk), lambda i, j, kk: (j, kk))
    else:
        b_spec = pl.BlockSpec((tk, tn), lambda i, j, kk: (kk, j))
    if o_stacked:
        assert tn == n // N_DEV
        out_spec = pl.BlockSpec((1, tm, tn), lambda i, j, kk: (j, i, 0))
        out_shape = SDS((N_DEV, m, tn), out_dtype)
    else:
        out_spec = pl.BlockSpec((tm, tn), lambda i, j, kk: (i, j))
        out_shape = SDS((m, n), out_dtype)
    extra_specs, extra = ([pl.BlockSpec(memory_space=pl.ANY)], [after]) if after is not None else ([], [])
    return pl.pallas_call(
        body, grid=(m // tm, n // tn, nk), in_specs=[a_spec, b_spec] + extra_specs, out_specs=out_spec,
        out_shape=out_shape, name=name, scratch_shapes=[pltpu.VMEM((tm, tn), f32)] if nk > 1 else [],
        compiler_params=_params(("parallel", "parallel", "arbitrary")),
    )(a, b, *extra)


def _rms(x):
    r = lax.rsqrt(jnp.mean(x * x, axis=-1, keepdims=True) + EPS)
    return r, x * r


def _rms_bwd(r, xn, dxn):
    return r * (dxn - xn * jnp.mean(dxn * xn, axis=-1, keepdims=True))


def _colsum(t):
    return jnp.sum(t, axis=0, keepdims=True)


def _sigmoid(x):
    return 1.0 / (1.0 + jnp.exp(-x))


def _modulated_norm(x, g, shift, scale):
    _, xn = _rms(x)
    return (xn * g) * (1.0 + scale) + shift


def _pre1(x, modv, g_mix):
    def fn(t, v):
        (xt,), (mv, g) = t, v
        return [_modulated_norm(xt, g, mv[0:1], mv[1:2])], []
    return _rowwise(fn, "pre1", [(x, D, 0)], [modv, g_mix], [(D, bf16)])[0]


def _post1(x, mix, modv, g_ffn):
    def fn(t, v):
        (xt, mt), (mv, g) = t, v
        x1 = xt + mv[2:3] * mt
        return [x1, _modulated_norm(x1, g, mv[3:4], mv[4:5])], []
    return _rowwise(fn, "post1", [(x, D, 0), (mix, D, 0)], [modv, g_ffn], [(D, f32), (D, bf16)])


def _gate_up(au, j):
    base = 2 * j * FF_PAD
    return au[:, base:base + FF_PAD], au[:, base + FF_PAD:base + 2 * FF_PAD]


def _swiglu_fwd(au):
    def fn(t, v):
        acts = []
        for j in range(N_DEV):
            a, u = _gate_up(t[0], j)
            acts.append(a * _sigmoid(a) * u)
        return [jnp.concatenate(acts, axis=1)], []
    return _rowwise(fn, "swiglu_fwd", [(au, 2 * FF_HID, 0)], [], [(FF_HID, bf16)])[0]


def _swiglu_bwd(au, dact):
    def fn(t, v):
        parts = []
        for j in range(N_DEV):
            a, u = _gate_up(t[0], j)
            d = t[1][:, j * FF_PAD:(j + 1) * FF_PAD]
            sg = _sigmoid(a)
            parts += [d * u * (sg * (1.0 + a * (1.0 - sg))), d * (a * sg)]
        return [jnp.concatenate(parts, axis=1)], []
    return _rowwise(fn, "swiglu_bwd", [(au, 2 * FF_HID, 0), (dact, FF_HID, 0)], [], [(2 * FF_HID, bf16)],
                    tile=128)[0]


def _final(x1, ff, target, modv, g_final):
    def fn(t, v):
        (x1t, fft, tgt), (mv, g) = t, v
        x2 = x1t + mv[5:6] * fft
        r, xn = _rms(x2)
        err = xn * g - tgt
        dy = err * (1.0 / D)
        dx2 = _rms_bwd(r, xn, dy * g)
        return [dx2, dx2 * mv[5:6]], [_colsum(dy * xn), _colsum(dx2 * fft), _colsum(err * err) * (0.5 / D)]
    return _rowwise(fn, "final", [(x1, D, 0), (ff, D, 0), (target, D, 0)], [modv, g_final],
                    [(D, f32), (D, bf16)], [D, D, D])


def _mid_bwd(dh2, x1, dx2, mix, modv, g_ffn):
    def fn(t, v):
        (dh, x1t, dx2t, mt), (mv, g) = t, v
        r, xn = _rms(x1t)
        dn = dh * (1.0 + mv[4:5])
        dx1 = dx2t + _rms_bwd(r, xn, dn * g)
        return [dx1, dx1 * mv[2:3]], [_colsum(dh), _colsum(dh * (xn * g)), _colsum(dn * xn), _colsum(dx1 * mt)]
    return _rowwise(fn, "mid_bwd", [(dh2, D, 0), (x1, D, 0), (dx2, D, 0), (mix, D, 0)], [modv, g_ffn],
                    [(D, f32), (D, bf16)], [D, D, D, D])


def _first_bwd(dh1, x, dx1, modv, g_mix):
    def fn(t, v):
        (dh, xt, dx1t), (mv, g) = t, v
        r, xn = _rms(xt)
        dn = dh * (1.0 + mv[1:2])
        return [dx1t + _rms_bwd(r, xn, dn * g)], [_colsum(dh), _colsum(dh * (xn * g)), _colsum(dn * xn)]
    return _rowwise(fn, "first_bwd", [(dh1, D, 0), (x, D, 0), (dx1, D, 0)], [modv, g_mix], [(D, f32)], [D, D, D])


def _merge_fwd(ya, yb, proj):
    def fn(t, v):
        ya_t, yb_t, ga, gb = t
        return [_sigmoid(ga) * ya_t + _sigmoid(gb) * yb_t], []
    return _rowwise(fn, "merge_fwd", [(ya, D, 0), (yb, D, 0), (proj, D, C_GA // D), (proj, D, C_GB // D)], [],
                    [(D, bf16)])[0]


def _merge_bwd(dmerged, ya, yb, proj):
    def fn(t, v):
        dm, ya_t, yb_t, ga, gb = t
        sa, sb = _sigmoid(ga), _sigmoid(gb)
        return [dm * sa, dm * sb, dm * ya_t * (sa * (1.0 - sa)), dm * yb_t * (sb * (1.0 - sb))], []
    return _rowwise(fn, "merge_bwd",
                    [(dmerged, D, 0), (ya, D, 0), (yb, D, 0), (proj, D, C_GA // D), (proj, D, C_GB // D)], [],
                    [(D, bf16), (D, bf16), (D, bf16), (D, bf16)])


def _rope_tables():
    half = ROT_DIM // 2
    pos = jnp.arange(SEQ, dtype=f32)
    inv_freq = ROPE_THETA ** (-jnp.arange(0, ROT_DIM, 2, dtype=f32) / ROT_DIM)
    ang = pos[:, None] * inv_freq[None, :]
    cos, sin = jnp.cos(ang), jnp.sin(ang)
    pad = jnp.zeros((SEQ, HEAD_DIM - ROT_DIM), f32)
    zero = jnp.zeros((SEQ, half), f32)
    c_head = jnp.concatenate([cos, cos, pad + 1.0], axis=1)
    lo_head = jnp.concatenate([-sin, zero, pad], axis=1)
    hi_head = jnp.concatenate([zero, sin, pad], axis=1)
    reps = DIL_W // HEAD_DIM
    return tuple(jnp.tile(t, (1, reps)) for t in (c_head, lo_head, hi_head))


def _rope_fwd(proj, tables):
    half = ROT_DIM // 2

    def fn(t, v):
        q, k, vv, c, lo, hi = t
        rot = lambda z: z * c + pltpu.roll(z, DIL_W - half, 1) * lo + pltpu.roll(z, half, 1) * hi
        return [rot(q) * ATT_SCALE, rot(k), vv], []
    return _rowwise(fn, "rope_fwd", [(proj, DIL_W, C_QB // DIL_W), (proj, DIL_W, C_KB // DIL_W),
                                     (proj, DIL_W, C_VB // DIL_W)] + [(tb, DIL_W, 0) for tb in tables], [],
                    [(DIL_W, bf16)] * 3)


def _rope_bwd(dq, dk, tables):
    half = ROT_DIM // 2

    def fn(t, v):
        dq_t, dk_t, c, lo, hi = t
        rot_t = lambda z: z * c + pltpu.roll(z * lo, half, 1) + pltpu.roll(z * hi, DIL_W - half, 1)
        return [rot_t(dq_t), rot_t(dk_t)], []
    return _rowwise(fn, "rope_bwd", [(dq, DIL_W, 0), (dk, DIL_W, 0)] + [(tb, DIL_W, 0) for tb in tables], [],
                    [(DIL_W, bf16), (DIL_W, bf16)])


def _head_bcast_sum(d):
    lane = lax.broadcasted_iota(jnp.int32, d.shape, 1)
    out = jnp.zeros_like(d)
    for h in range(d.shape[1] // HEAD_DIM):
        sel = (lane >= h * HEAD_DIM) & (lane < (h + 1) * HEAD_DIM)
        out = jnp.where(sel, jnp.sum(jnp.where(sel, d, 0.0), axis=1, keepdims=True), out)
    return out


def _dil_combine(o, lse):
    def fn(t, v):
        o0, o1, o2, l0, l1, l2 = t
        m = jnp.maximum(jnp.maximum(l0, l1), l2)
        w0, w1, w2 = jnp.exp(l0 - m), jnp.exp(l1 - m), jnp.exp(l2 - m)
        tot = w0 + w1 + w2
        return [(w0 * o0 + w1 * o1 + w2 * o2) / tot, m + jnp.log(tot)], []
    w = DIL_OUT_W
    return _rowwise(fn, "dil_combine", [(o, w, 0), (o, w, 1), (o, w, 2), (lse, w, 0), (lse, w, 1), (lse, w, 2)], [],
                    [(w, f32), (w, f32)])


def _dil_delta(dyb_h, yb_h):
    def fn(t, v):
        return [_head_bcast_sum(t[0] * t[1])], []
    return _rowwise(fn, "dil_delta", [(dyb_h, DIL_OUT_W, 0), (yb_h, DIL_OUT_W, 0)], [], [(DIL_OUT_W, f32)])[0]


def _adamw(w, g, m, v, name):
    shape = w.shape
    if w.ndim == 1:
        w, g, m, v = (t.reshape(1, -1) for t in (w, g, m, v))
    rows, cols = w.shape
    tile = 256 if rows % 256 == 0 and rows > 512 else rows

    def fn(t, _):
        wt, gt, mt, vt = t
        mn = ADAM_B1 * mt + (1.0 - ADAM_B1) * gt
        vn = ADAM_B2 * vt + (1.0 - ADAM_B2) * (gt * gt)
        m_hat = mn / (1.0 - ADAM_B1 ** ADAM_STEP)
        v_hat = vn / (1.0 - ADAM_B2 ** ADAM_STEP)
        return [-ADAM_LR * (m_hat / (jnp.sqrt(v_hat) + ADAM_EPS) + ADAM_WD * wt), mn, vn], []
    delta, mn, vn = _rowwise(fn, name, [(w, cols, 0), (g, cols, 0), (m, cols, 0), (v, cols, 0)], [],
                             [(cols, f32)] * 3, tile=tile)
    return delta.reshape(shape), mn.reshape(shape), vn.reshape(shape)


def _ada_fwd(c_all, w_shard, b_shard):
    def body(c_ref, w_ref, b_ref, o_ref):
        cv = c_ref[...]
        sc = (cv * _sigmoid(cv)).astype(bf16)
        o_ref[...] = jnp.dot(sc, w_ref[...].astype(bf16), preferred_element_type=f32) + b_ref[...]
    return pl.pallas_call(body, out_shape=SDS((N_DEV, w_shard.shape[1]), f32), name="ada_fwd",
                          compiler_params=_params())(c_all, w_shard, b_shard)


def _ada_bwd(c_all, dmod_cols):
    def body(c_ref, d_ref, o_ref):
        cv = c_ref[...]
        sc = cv * _sigmoid(cv)
        o_ref[...] = lax.dot_general(sc, d_ref[...], (((0,), (0,)), ((), ())), precision=lax.Precision.HIGHEST,
                                     preferred_element_type=f32)
    return pl.pallas_call(body, out_shape=SDS((D, dmod_cols.shape[1]), f32), name="ada_bwd",
                          compiler_params=_params())(c_all, dmod_cols)


def _small_reduce(gathered):
    def body(g_ref, o_ref, loss_ref):
        acc = g_ref[0]
        for d in range(1, N_DEV):
            acc = acc + g_ref[d]
        o_ref[...] = acc
        loss_ref[...] = jnp.zeros((1, LANES), f32) + jnp.sum(acc[10:11, :])
    return pl.pallas_call(body, out_shape=(SDS((SMALL_ROWS, D), f32), SDS((1, LANES), f32)), name="small_reduce",
                          compiler_params=_params())(gathered)


FOX_BLK = 512
CUM_BLK = 128


def _fold_lanes(t, op):
    out = t[:, :LANES]
    for j in range(1, t.shape[1] // LANES):
        out = op(out, t[:, j * LANES:(j + 1) * LANES])
    return out


def _fox_gate_fwd(proj, b_pad):
    nblk = SEQ // CUM_BLK

    def body(f_ref, b_ref, col_ref):
        r = lax.broadcasted_iota(jnp.int32, (CUM_BLK, CUM_BLK), 0)
        c = lax.broadcasted_iota(jnp.int32, (CUM_BLK, CUM_BLK), 1)
        tri = (r >= c).astype(f32)
        carry = jnp.zeros((1, LANES), f32)
        for blk in range(nblk):
            z = f_ref[blk * CUM_BLK:(blk + 1) * CUM_BLK, :] + b_ref[...]
            logf = jnp.minimum(z, 0.0) - jnp.log1p(jnp.exp(-jnp.abs(z)))
            cs = jnp.dot(tri, logf, precision=lax.Precision.HIGHEST, preferred_element_type=f32) + carry
            col_ref[blk * CUM_BLK:(blk + 1) * CUM_BLK, :] = cs
            carry = cs[CUM_BLK - 1:CUM_BLK, :]

    return pl.pallas_call(
        body, grid=(1,), in_specs=[pl.BlockSpec((SEQ, LANES), lambda i: (0, C_F // LANES)),
                                   pl.BlockSpec((1, LANES), lambda i: (0, 0))],
        out_specs=pl.BlockSpec((SEQ, LANES), lambda i: (0, 0)),
        out_shape=SDS((SEQ, LANES), f32), name="fox_gate_fwd",
        compiler_params=_params(("arbitrary",)),
    )(proj, b_pad)


def _fox_gate_bwd(dF_row, proj, b_pad):
    nblk = SEQ // CUM_BLK

    def body(d_ref, f_ref, b_ref, df_ref, db_ref, col_ref):
        r = lax.broadcasted_iota(jnp.int32, (CUM_BLK, CUM_BLK), 0)
        c = lax.broadcasted_iota(jnp.int32, (CUM_BLK, CUM_BLK), 1)
        tri = (r <= c).astype(f32)
        lane = lax.broadcasted_iota(jnp.int32, (CUM_BLK, LANES), 1)
        col_ref[...] = d_ref[...].T
        carry = jnp.zeros((1, LANES), f32)
        total = jnp.zeros((1, LANES), f32)
        for blk in reversed(range(nblk)):
            rows = slice(blk * CUM_BLK, (blk + 1) * CUM_BLK)
            cs = jnp.dot(tri, col_ref[rows, :], precision=lax.Precision.HIGHEST, preferred_element_type=f32) + carry
            carry = cs[0:1, :]
            z = f_ref[rows, :] + b_ref[...]
            df = jnp.where(lane < N_FOX_HEADS, cs * _sigmoid(-z), 0.0)
            df_ref[rows, :] = df.astype(df_ref.dtype)
            total = total + _colsum(df)
        db_ref[...] = total

    return pl.pallas_call(
        body, grid=(1,), in_specs=[pl.BlockSpec((LANES, SEQ), lambda i: (0, 0)),
                                   pl.BlockSpec((SEQ, LANES), lambda i: (0, C_F // LANES)),
                                   pl.BlockSpec((1, LANES), lambda i: (0, 0))],
        out_specs=[pl.BlockSpec((SEQ, LANES), lambda i: (0, 0)), pl.BlockSpec((1, LANES), lambda i: (0, 0))],
        out_shape=(SDS((SEQ, LANES), bf16), SDS((1, LANES), f32)), name="fox_gate_bwd",
        scratch_shapes=[pltpu.VMEM((SEQ, LANES), f32)],
        compiler_params=_params(("arbitrary",)),
    )(dF_row, proj, b_pad)


def _nt(a, b):
    return lax.dot_general(a, b, (((1,), (1,)), ((), ())), preferred_element_type=f32)


def _tn(a, b):
    return lax.dot_general(a, b, (((0,), (0,)), ((), ())), preferred_element_type=f32)


def _fox_prep(proj, f_col):
    def fn(t, v):
        q, k, vv, fc = t
        lane = lax.broadcasted_iota(jnp.int32, (q.shape[0], LANES), 1)
        qs, ks = [], []
        for h in range(N_FOX_HEADS):
            pair, pos = divmod(h, 2)
            own = (lane >= pos * HEAD_DIM) & (lane < (pos + 1) * HEAD_DIM)
            base = (1 - pos) * HEAD_DIM
            f = fc[:, h:h + 1]
            hi = f.astype(bf16).astype(f32)
            mid = (f - hi).astype(bf16).astype(f32)
            lo = (f - hi) - mid
            one = jnp.ones_like(f)
            qa = jnp.where(own, q[:, pair * LANES:(pair + 1) * LANES] * ATT_SCALE, 0.0)
            ka = k[:, pair * LANES:(pair + 1) * LANES]
            for idx, (qv, kv) in enumerate([(hi, one), (mid, one), (lo, one), (one, -hi), (one, -mid), (one, -lo)]):
                sel = lane == base + idx
                qa = jnp.where(sel, qv, qa)
                ka = jnp.where(sel, kv, ka)
            qs.append(qa)
            ks.append(ka)
        return [jnp.concatenate(qs, axis=1), jnp.concatenate(ks, axis=1), vv], []
    w = N_FOX_HEADS * LANES
    return _rowwise(fn, "fox_prep", [(proj, FOX_W, C_QA // FOX_W), (proj, FOX_W, C_KA // FOX_W),
                                     (proj, FOX_W, C_VA // FOX_W), (f_col, LANES, 0)], [],
                    [(w, bf16), (w, bf16), (FOX_W, bf16)])


def _fox_fwd(q_aug, k_aug, v):
    blk = FOX_BLK
    npair = FOX_W // LANES

    def body(q_ref, k_ref, v_ref, o_ref, lse_ref, s_scr):
        i = pl.program_id(1)
        tri = lax.broadcasted_iota(jnp.int32, (blk, blk), 0) >= lax.broadcasted_iota(jnp.int32, (blk, blk), 1)
        qh = [q_ref[:, h * LANES:(h + 1) * LANES] for h in range(2)]

        def logits(c, masked):
            off = pl.multiple_of(c * blk, blk)
            tops = []
            for h in range(2):
                s = _nt(qh[h], k_ref[pl.ds(off, blk), h * LANES:(h + 1) * LANES])
                if masked:
                    s = jnp.where(tri, s, NEG)
                s_scr[h, :, pl.ds(off, blk)] = s
                tops.append(_fold_lanes(s, jnp.maximum))
            return tops

        def pass_a(c, m):
            return tuple(jnp.maximum(a, b) for a, b in zip(m, logits(c, False)))

        m = lax.fori_loop(0, i, pass_a, tuple(jnp.full((blk, LANES), NEG, f32) for _ in range(2)))
        mx = [jnp.max(jnp.maximum(a, b), axis=1, keepdims=True) for a, b in zip(m, logits(i, True))]

        def pass_b(c, carry):
            off = pl.multiple_of(c * blk, blk)
            vv = v_ref[pl.ds(off, blk), :]
            new = []
            for h in range(2):
                l, acc = carry[h]
                p = jnp.exp(s_scr[h, :, pl.ds(off, blk)] - mx[h])
                new.append((l + _fold_lanes(p, jnp.add),
                            acc + jnp.dot(p.astype(bf16), vv, preferred_element_type=f32)))
            return tuple(new)

        zero = jnp.zeros((blk, LANES), f32)
        (l_a, acc_a), (l_b, acc_b) = lax.fori_loop(0, i + 1, pass_b, ((zero, zero), (zero, zero)))
        l_a = jnp.sum(l_a, axis=1, keepdims=True)
        l_b = jnp.sum(l_b, axis=1, keepdims=True)
        first = lax.broadcasted_iota(jnp.int32, (blk, LANES), 1) < HEAD_DIM
        o_ref[...] = jnp.where(first, acc_a / l_a, acc_b / l_b)
        lse_ref[0] = jnp.where(first, mx[0] + jnp.log(l_a), mx[1] + jnp.log(l_b))

    return pl.pallas_call(
        body, grid=(npair, SEQ // blk),
        in_specs=[pl.BlockSpec((blk, 2 * LANES), lambda p, i: (i, p)),
                  pl.BlockSpec((SEQ, 2 * LANES), lambda p, i: (0, p)),
                  pl.BlockSpec((SEQ, LANES), lambda p, i: (0, p))],
        out_specs=[pl.BlockSpec((blk, LANES), lambda p, i: (i, p)),
                   pl.BlockSpec((1, blk, LANES), lambda p, i: (p, i, 0))],
        out_shape=(SDS((SEQ, FOX_W), f32), SDS((npair, SEQ, LANES), f32)), name="fox_fwd",
        scratch_shapes=[pltpu.VMEM((2, blk, SEQ), f32)],
        compiler_params=_params(("parallel", "arbitrary")),
    )(q_aug, k_aug, v)


def _fox_bwd(q_aug, k_aug, v, do, o, lse):
    blk = FOX_BLK
    npair = FOX_W // LANES
    nblk = SEQ // blk

    def body(q_ref, k_ref, v_ref, do_ref, o_ref, lse_ref, dq_ref, dk_ref, dv_ref, df_ref,
             dq_acc, delta_ref, res_ref):
        lane_s = lax.broadcasted_iota(jnp.int32, (SEQ, LANES), 1)
        prod = do_ref[...] * o_ref[...]
        d_a = jnp.sum(jnp.where(lane_s < HEAD_DIM, prod, 0.0), axis=1, keepdims=True)
        d_b = jnp.sum(jnp.where(lane_s >= HEAD_DIM, prod, 0.0), axis=1, keepdims=True)
        delta_ref[...] = jnp.where(lane_s < HEAD_DIM, d_a, d_b)
        dq_acc[...] = jnp.zeros_like(dq_acc)
        res_ref[...] = jnp.zeros_like(res_ref)
        df_ref[...] = jnp.zeros_like(df_ref)
        lane = lax.broadcasted_iota(jnp.int32, (blk, LANES), 1)
        own = [lane < HEAD_DIM, lane >= HEAD_DIM]
        tri = lax.broadcasted_iota(jnp.int32, (blk, blk), 0) >= lax.broadcasted_iota(jnp.int32, (blk, blk), 1)

        def q_slab(qoff, h):
            return q_ref[pl.ds(qoff, blk), h * LANES:(h + 1) * LANES]

        def probs(qoff, h, k_h, masked):
            s = _nt(q_slab(qoff, h), k_h)
            if masked:
                s = jnp.where(tri, s, NEG)
            return jnp.exp(s - lse_ref[0, pl.ds(qoff, blk), h * HEAD_DIM:h * HEAD_DIM + 1])

        def k_slabs(koff):
            return [k_ref[pl.ds(koff, blk), h * LANES:(h + 1) * LANES] for h in range(2)]

        def kv_step(kj, _):
            koff = pl.multiple_of(kj * blk, blk)
            k_aug = k_slabs(koff)
            k_own = [jnp.where(own[h], k_aug[h], jnp.zeros_like(k_aug[h])) for h in range(2)]
            vv = v_ref[pl.ds(koff, blk), :]
            v_own = [jnp.where(own[h], vv, jnp.zeros_like(vv)) for h in range(2)]

            def q_tile(qi, carry, masked):
                qoff = pl.multiple_of(qi * blk, blk)
                dd = do_ref[pl.ds(qoff, blk), :].astype(bf16)
                new, dq_add = [], None
                for h in range(2):
                    dk_h, dv_h, dcol = carry[h]
                    p = probs(qoff, h, k_aug[h], masked)
                    dl = p * (_nt(dd, v_own[h]) - delta_ref[pl.ds(qoff, blk), h * HEAD_DIM:h * HEAD_DIM + 1])
                    dlb = dl.astype(bf16)
                    part = jnp.dot(dlb, k_own[h], preferred_element_type=f32)
                    dq_add = part if dq_add is None else dq_add + part
                    res_ref[h, pl.ds(qoff, blk), :] += _fold_lanes(dl, jnp.add)
                    new.append((dk_h + _tn(dlb, q_slab(qoff, h)), dv_h + _tn(p.astype(bf16), dd),
                                dcol + _colsum(dl)))
                dq_acc[pl.ds(qoff, blk), :] += dq_add * ATT_SCALE
                return tuple(new)

            zero = (jnp.zeros((blk, LANES), f32), jnp.zeros((blk, LANES), f32), jnp.zeros((1, blk), f32))
            carry = q_tile(kj, (zero, zero), True)
            (dk_a, dv_a, dcol_a), (dk_b, dv_b, dcol_b) = lax.fori_loop(
                kj + 1, nblk, lambda qi, cr: q_tile(qi, cr, False), carry)
            dk_ref[pl.ds(koff, blk), :] = jnp.where(own[0], dk_a, dk_b).astype(dk_ref.dtype)
            dv_ref[pl.ds(koff, blk), :] = jnp.where(own[0], dv_a, dv_b).astype(dv_ref.dtype)
            df_ref[0, 0:1, pl.ds(koff, blk)] = -dcol_a
            df_ref[0, 1:2, pl.ds(koff, blk)] = -dcol_b
            return 0

        lax.fori_loop(0, nblk, kv_step, 0)
        dq_ref[...] = dq_acc[...].astype(dq_ref.dtype)

        for h in range(2):
            res_ref[h] = jnp.zeros((SEQ, LANES), f32) + jnp.sum(res_ref[h], axis=1, keepdims=True)

        def kv_fix(kj, _):
            koff = pl.multiple_of(kj * blk, blk)
            k_aug = k_slabs(koff)

            def q_fix(qi, corr, masked):
                qoff = pl.multiple_of(qi * blk, blk)
                return tuple(corr[h] + _colsum(probs(qoff, h, k_aug[h], masked) * res_ref[h, pl.ds(qoff, blk), 0:1])
                             for h in range(2))

            zero = jnp.zeros((1, blk), f32)
            corr = lax.fori_loop(kj + 1, nblk, lambda qi, cr: q_fix(qi, cr, False), q_fix(kj, (zero, zero), True))
            df_ref[0, 0:1, pl.ds(koff, blk)] += corr[0]
            df_ref[0, 1:2, pl.ds(koff, blk)] += corr[1]
            return 0

        lax.fori_loop(0, nblk, kv_fix, 0)

    pair_aug = pl.BlockSpec((SEQ, 2 * LANES), lambda p: (0, p))
    slab = pl.BlockSpec((SEQ, LANES), lambda p: (0, p))
    per_pair = pl.BlockSpec((1, SEQ, LANES), lambda p: (p, 0, 0))
    rows = pl.BlockSpec((1, 8, SEQ), lambda p: (p, 0, 0))
    return pl.pallas_call(
        body, grid=(npair,),
        in_specs=[pair_aug, pair_aug, slab, slab, slab, per_pair],
        out_specs=[slab, slab, slab, rows],
        out_shape=(SDS((SEQ, FOX_W), bf16),) * 3 + (SDS((npair, 8, SEQ), f32),), name="fox_bwd",
        scratch_shapes=[pltpu.VMEM((SEQ, LANES), f32), pltpu.VMEM((SEQ, LANES), f32),
                        pltpu.VMEM((2, SEQ, LANES), f32)],
        compiler_params=_params(("parallel",)),
    )(q_aug, k_aug, v, do, o, lse)


DIL_BLK = 128
N_GROUPS = 3
DIL_PAIRS = DIL_OUT_W // LANES
DIL_NBLK = SEQ // DIL_BLK


def _blocks_per_seq(g):
    return jnp.where(g == 0, 16, jnp.where(g == 1, 4, 1))


def _dil_block(n, g):
    rows = slice(n * DIL_BLK, (n + 1) * DIL_BLK)
    if n == 0:
        r = lax.broadcasted_iota(jnp.int32, (DIL_BLK, DIL_BLK), 0)
        c = lax.broadcasted_iota(jnp.int32, (DIL_BLK, DIL_BLK), 1)
        return rows, rows, r >= c
    r = lax.broadcasted_iota(jnp.int32, (DIL_BLK, 2 * DIL_BLK), 0)
    c = lax.broadcasted_iota(jnp.int32, (DIL_BLK, 2 * DIL_BLK), 1)
    has_prev = (n & (_blocks_per_seq(g) - 1)) > 0
    mask = ((c < DIL_BLK) & (c >= r) & has_prev) | ((c >= DIL_BLK) & (c - DIL_BLK <= r))
    return rows, slice((n - 1) * DIL_BLK, (n + 1) * DIL_BLK), mask


def _dil_spec():
    return pl.BlockSpec((1, SEQ, LANES), lambda g, p: (g, 0, p))


def _dil_fwd(q, k, v):
    def body(q_ref, k_ref, v_ref, o_ref, lse_ref):
        g = pl.program_id(0)
        first = lax.broadcasted_iota(jnp.int32, (DIL_BLK, LANES), 1) < HEAD_DIM
        for n in range(DIL_NBLK):
            rows, krows, mask = _dil_block(n, g)
            qv, kk, vv = q_ref[0, rows, :], k_ref[0, krows, :], v_ref[0, krows, :]
            outs, lses = [], []
            for own in (first, ~first):
                s = jnp.where(mask, _nt(jnp.where(own, qv, jnp.zeros_like(qv)), kk), NEG)
                m = jnp.max(s, axis=1, keepdims=True)
                p = jnp.exp(s - m)
                l = jnp.sum(p, axis=1, keepdims=True)
                outs.append(jnp.dot(p.astype(bf16), vv, preferred_element_type=f32) / l)
                lses.append(m + jnp.log(l))
            o_ref[0, rows, :] = jnp.where(first, outs[0], outs[1])
            lse_ref[0, rows, :] = jnp.where(first, lses[0], lses[1])

    spec = _dil_spec()
    shape = SDS((N_GROUPS, SEQ, DIL_OUT_W), f32)
    return pl.pallas_call(
        body, grid=(N_GROUPS, DIL_PAIRS), in_specs=[spec] * 3, out_specs=[spec] * 2,
        out_shape=(shape, shape), name="dil_fwd", compiler_params=_params(("parallel", "parallel")),
    )(q, k, v)


def _dil_bwd(q, k, v, do, lse, delta):
    def body(q_ref, k_ref, v_ref, do_ref, lse_ref, dl_ref, dq_ref, dk_ref, dv_ref):
        g = pl.program_id(0)
        first = lax.broadcasted_iota(jnp.int32, (DIL_BLK, LANES), 1) < HEAD_DIM
        dk_ref[...] = jnp.zeros_like(dk_ref)
        dv_ref[...] = jnp.zeros_like(dv_ref)
        for n in range(DIL_NBLK):
            rows, krows, mask = _dil_block(n, g)
            qv, kk, vv = q_ref[0, rows, :], k_ref[0, krows, :], v_ref[0, krows, :]
            dov = do_ref[0, rows, :].astype(bf16)
            lsev, delv = lse_ref[0, rows, :], dl_ref[0, rows, :]
            dqs, dk_add, dv_add = [], None, None
            for h, own in enumerate((first, ~first)):
                col = h * HEAD_DIM
                qh = jnp.where(own, qv, jnp.zeros_like(qv))
                doh = jnp.where(own, dov, jnp.zeros_like(dov))
                p = jnp.exp(jnp.where(mask, _nt(qh, kk), NEG) - lsev[:, col:col + 1])
                dl = (p * (_nt(doh, vv) - delv[:, col:col + 1])).astype(bf16)
                dqs.append(jnp.dot(dl, kk, preferred_element_type=f32))
                dk_h, dv_h = _tn(dl, qh), _tn(p.astype(bf16), doh)
                dk_add = dk_h if dk_add is None else dk_add + dk_h
                dv_add = dv_h if dv_add is None else dv_add + dv_h
            dq_ref[0, rows, :] = jnp.where(first, dqs[0], dqs[1]) * ATT_SCALE
            dk_ref[0, krows, :] += dk_add
            dv_ref[0, krows, :] += dv_add

    spec = _dil_spec()
    shape = SDS((N_GROUPS, SEQ, DIL_OUT_W), f32)
    return pl.pallas_call(
        body, grid=(N_GROUPS, DIL_PAIRS), in_specs=[spec] * 6, out_specs=[spec] * 3,
        out_shape=(shape, shape, shape), name="dil_bwd", compiler_params=_params(("parallel", "parallel")),
    )(q, k, v, do, lse, delta)


_DILATIONS = (1, 4, 16)


def _to_classes(t):
    w = t.shape[1] // N_GROUPS
    out = []
    for g, d in enumerate(_DILATIONS):
        s = t[:, g * w:(g + 1) * w]
        out.append(s.reshape(SEQ // d, d, w).transpose(1, 0, 2).reshape(SEQ, w))
    return jnp.stack(out)


def _from_classes(t):
    w = t.shape[2]
    out = [t[g].reshape(d, SEQ // d, w).transpose(1, 0, 2).reshape(SEQ, w) for g, d in enumerate(_DILATIONS)]
    return jnp.concatenate(out, axis=1)


def _position():
    return lax.axis_index("x"), lax.axis_index("y"), lax.axis_index("c")


def _all_gather(block, name):
    def body(x_ref, out_ref, send_sems, recv_sems, local_sem):
        x, y, c = _position()
        me, sibling = (x, y, c), (x, y, 1 - c)
        chips = [(1 - x, y), (x, 1 - y), (1 - x, 1 - y)]

        def slot(px, py, pc):
            return out_ref.at[4 * px + 2 * py + pc]

        def copy(k, blk, to, src=None):
            return pltpu.make_async_remote_copy(
                src_ref=slot(*blk) if src is None else src, dst_ref=slot(*blk),
                send_sem=send_sems.at[k], recv_sem=recv_sems.at[k], device_id=to, device_id_type=MESH)

        mine = pltpu.make_async_copy(x_ref, slot(*me), local_sem)
        mine.start()
        first = [copy(0, me, sibling, src=x_ref)]
        first += [copy(1 + j, me, (*chip, c), src=x_ref) for j, chip in enumerate(chips)]
        for cp in first:
            cp.start()
        passed = [copy(4 + j, (*chip, c), sibling) for j, chip in enumerate(chips)]
        for j, chip in enumerate(chips):
            copy(1 + j, (*chip, c), me).wait_recv()
            passed[j].start()
        copy(0, sibling, me).wait_recv()
        for j, chip in enumerate(chips):
            copy(4 + j, (*chip, 1 - c), me).wait_recv()
        for cp in first + passed:
            cp.wait_send()
        mine.wait()

    return pl.pallas_call(
        body, out_shape=SDS((N_DEV,) + block.shape, block.dtype),
        in_specs=[pl.BlockSpec(memory_space=pl.ANY)], out_specs=pl.BlockSpec(memory_space=pl.ANY),
        scratch_shapes=[pltpu.SemaphoreType.DMA((7,)), pltpu.SemaphoreType.DMA((7,)), pltpu.SemaphoreType.DMA],
        name=name,
    )(block)


def _all_gather_many(blocks, name):
    n = len(blocks)

    def body(*refs):
        x_refs, out_refs = refs[:n], refs[n:2 * n]
        send_sems, recv_sems, local_sems = refs[2 * n:]
        x, y, c = _position()
        me, sibling = (x, y, c), (x, y, 1 - c)
        chips = [(1 - x, y), (x, 1 - y), (1 - x, 1 - y)]

        def slot(a, px, py, pc):
            return out_refs[a].at[4 * px + 2 * py + pc]

        def copy(a, k, blk, to, own=False):
            return pltpu.make_async_remote_copy(
                src_ref=x_refs[a] if own else slot(a, *blk), dst_ref=slot(a, *blk),
                send_sem=send_sems.at[a, k], recv_sem=recv_sems.at[a, k], device_id=to, device_id_type=MESH)

        mine = [pltpu.make_async_copy(x_refs[a], slot(a, *me), local_sems.at[a]) for a in range(n)]
        for cp in mine:
            cp.start()
        started = []
        for a in range(n):
            first = [copy(a, 0, me, sibling, own=True)]
            first += [copy(a, 1 + j, me, (*chip, c), own=True) for j, chip in enumerate(chips)]
            for cp in first:
                cp.start()
            started += first
        for a in range(n):
            for j, chip in enumerate(chips):
                copy(a, 1 + j, (*chip, c), me).wait_recv()
                passed = copy(a, 4 + j, (*chip, c), sibling)
                passed.start()
                started.append(passed)
        for a in range(n):
            copy(a, 0, sibling, me).wait_recv()
            for j, chip in enumerate(chips):
                copy(a, 4 + j, (*chip, 1 - c), me).wait_recv()
        for cp in started:
            cp.wait_send()
        for cp in mine:
            cp.wait()

    hbm = pl.BlockSpec(memory_space=pl.ANY)
    return pl.pallas_call(
        body, out_shape=[SDS((N_DEV,) + b.shape, b.dtype) for b in blocks],
        in_specs=[hbm] * n, out_specs=[hbm] * n,
        scratch_shapes=[pltpu.SemaphoreType.DMA((n, 7)), pltpu.SemaphoreType.DMA((n, 7)),
                        pltpu.SemaphoreType.DMA((n,))],
        name=name,
    )(*blocks)


def _pair_exchange(gs, name):
    n = len(gs)

    def body(*refs):
        g_refs, r_refs, send_sems, recv_sems = refs[:n], refs[n:2 * n], refs[2 * n], refs[2 * n + 1]
        x, y, c = _position()
        copies = []
        for a in range(n):
            for k in range(4):
                cp = pltpu.make_async_remote_copy(
                    src_ref=g_refs[a].at[2 * k + (1 - c)], dst_ref=r_refs[a].at[k], send_sem=send_sems.at[a, k],
                    recv_sem=recv_sems.at[a, k], device_id=(x, y, 1 - c), device_id_type=MESH)
                cp.start()
                copies.append(cp)
        for cp in copies:
            cp.wait()

    hbm = pl.BlockSpec(memory_space=pl.ANY)
    return pl.pallas_call(
        body, out_shape=[SDS((4,) + g.shape[1:], g.dtype) for g in gs], in_specs=[hbm] * n, out_specs=[hbm] * n,
        scratch_shapes=[pltpu.SemaphoreType.DMA((n, 4)), pltpu.SemaphoreType.DMA((n, 4))], name=name,
    )(*gs)


HBM_SPEC = pl.BlockSpec(memory_space=pltpu.HBM)
SEM_SPEC = pl.BlockSpec(memory_space=pltpu.SEMAPHORE)
SPLIT_COPY = pltpu.CompilerParams(has_side_effects=pltpu.SideEffectType.DATAFLOW_SIDE_EFFECTING)


def _in_hbm(t):
    return pltpu.with_memory_space_constraint(t, pltpu.HBM)


def _chip_copies(t_refs, land_refs, send_sems, recv_sems):
    x, y, c = _position()
    chips = [(1 - x, y), (x, 1 - y), (1 - x, 1 - y)]
    return [pltpu.make_async_remote_copy(
        src_ref=t.at[2 * px + py], dst_ref=land.at[j], send_sem=send_sems.at[3 * a + j],
        recv_sem=recv_sems.at[3 * a + j], device_id=(px, py, c), device_id_type=MESH)
        for a, (t, land) in enumerate(zip(t_refs, land_refs, strict=True)) for j, (px, py) in enumerate(chips)]


def _chip_exchange_start(ts, name):
    n = len(ts)
    lands = [_in_hbm(lax.empty((3,) + t.shape[1:], t.dtype)) for t in ts]

    def body(*refs):
        send_sems, recv_sems = refs[2 * n], refs[2 * n + 1]
        for cp in _chip_copies(refs[:n], refs[n:2 * n], send_sems, recv_sems):
            cp.start()
        refs[-1][...] = jnp.zeros_like(refs[-1])

    sems = pltpu.SemaphoreType.DMA((3 * n,))
    res = pl.pallas_call(
        body, name=name, in_specs=[HBM_SPEC] * (2 * n),
        out_shape=(sems, sems, *[pltpu.HBM(t.shape, t.dtype) for t in (*ts, *lands)], SDS((8, LANES), f32)),
        out_specs=(SEM_SPEC, SEM_SPEC, *[HBM_SPEC] * (2 * n), pl.BlockSpec(memory_space=pltpu.VMEM)),
        input_output_aliases={i: 2 + i for i in range(2 * n)}, compiler_params=SPLIT_COPY,
    )(*[_in_hbm(t) for t in ts], *lands)
    return res[:-1], res[-1]


def _chip_exchange_wait(state, after, name):
    send_sems, recv_sems, *arrays = state
    n = len(arrays) // 2

    def body(*refs):
        for cp in _chip_copies(refs[:n], refs[n:2 * n], refs[2 * n], refs[2 * n + 1]):
            cp.wait_send()
            cp.wait_recv()

    res = pl.pallas_call(
        body, name=name, in_specs=[HBM_SPEC] * (2 * n) + [SEM_SPEC, SEM_SPEC, pl.BlockSpec(memory_space=pl.ANY)],
        out_shape=[pltpu.HBM(t.shape, t.dtype) for t in arrays], out_specs=[HBM_SPEC] * (2 * n),
        input_output_aliases={i: i for i in range(2 * n)}, compiler_params=SPLIT_COPY,
    )(*arrays, send_sems, recv_sems, after)
    return res[n:]


def _row_tile(rows):
    return 256 if rows % 256 == 0 and rows > 512 else rows


def _pair_add(g, r1, core, name):
    def body(c_ref, g_ref, r_ref, o_ref):
        o_ref[...] = (g_ref[...].astype(f32) + r_ref[...].astype(f32)).astype(o_ref.dtype)

    rows, cols = g.shape[1:]
    tile = _row_tile(rows)
    blk = (1, tile, cols)
    return pl.pallas_call(
        body, out_shape=SDS((4, rows, cols), g.dtype), name=name,
        grid_spec=pltpu.PrefetchScalarGridSpec(
            num_scalar_prefetch=1, grid=(4, rows // tile),
            in_specs=[pl.BlockSpec(blk, lambda k, i, c_ref: (2 * k + c_ref[0], i, 0)),
                      pl.BlockSpec(blk, lambda k, i, c_ref: (k, i, 0))],
            out_specs=pl.BlockSpec(blk, lambda k, i, c_ref: (k, i, 0))),
        compiler_params=_params(("parallel", "arbitrary")),
    )(core, g, r1)


def _chip_add(t, r2, chip, name):
    def body(c_ref, t_ref, r_ref, o_ref):
        o_ref[...] = ((t_ref[0].astype(f32) + r_ref[0].astype(f32)) + r_ref[1].astype(f32)) + r_ref[2].astype(f32)

    rows, cols = t.shape[1:]
    tile = _row_tile(rows)
    return pl.pallas_call(
        body, out_shape=SDS((rows, cols), f32), name=name,
        grid_spec=pltpu.PrefetchScalarGridSpec(
            num_scalar_prefetch=1, grid=(rows // tile,),
            in_specs=[pl.BlockSpec((1, tile, cols), lambda i, c_ref: (c_ref[0], i, 0)),
                      pl.BlockSpec((3, tile, cols), lambda i, c_ref: (0, i, 0))],
            out_specs=pl.BlockSpec((tile, cols), lambda i, c_ref: (i, 0))),
        compiler_params=_params(("arbitrary",)),
    )(chip, t, r2)


def _pad_to(t, axis, size):
    pads = [(0, 0)] * t.ndim
    pads[axis] = (0, size - t.shape[axis])
    return jnp.pad(t, pads)


_REF_COLS = {"qa": (0, FOX_W), "ka": (FOX_W, FOX_W), "va": (2 * FOX_W, FOX_W), "f": (3 * FOX_W, N_FOX_HEADS)}
_REF_COLS.update({n: (3 * FOX_W + N_FOX_HEADS + i * DIL_W, DIL_W) for i, n in enumerate(("qb", "kb", "vb"))})
_REF_COLS.update({n: (3 * FOX_W + N_FOX_HEADS + 3 * DIL_W + i * D, D) for i, n in enumerate(("ga", "gb"))})
_REF_ORDER = ("qa", "ka", "va", "f", "qb", "kb", "vb", "ga", "gb")


def _shard_pad_cols(pieces):
    first = pieces[_REF_ORDER[0]]
    pad = jnp.zeros((first.shape[0], W_IN_PAD - W_IN_SH), first.dtype)
    parts, names, used = [], list(_REF_ORDER), 0
    for _ in range(N_DEV):
        need = W_IN_SH
        while need:
            take = min(need, _REF_COLS[names[0]][1] - used)
            parts.append(pieces[names[0]][:, used:used + take])
            need, used = need - take, used + take
            if used == _REF_COLS[names[0]][1]:
                names, used = names[1:], 0
        parts.append(pad)
    return jnp.concatenate(parts, axis=1)


def _slab_w_in(stack):
    def cols(name):
        lo, width = _REF_COLS[name]
        hi, out = lo + width, []
        while lo < hi:
            j, off = divmod(lo, W_IN_SH)
            n = min(hi - lo, W_IN_SH - off)
            out.append(stack[j, :, off:off + n])
            lo += n
        return out
    z = lambda n: [jnp.zeros((stack.shape[1], n), stack.dtype)]
    parts = (cols("ga") + cols("gb") + z(C_QB - 2 * D) + cols("qb") + cols("kb") + cols("vb") + cols("qa")
             + cols("ka") + cols("va") + cols("f") + z(LANES - N_FOX_HEADS))
    return jnp.concatenate(parts, axis=1)


def kernel(x, c, w_ada, b_ada, g_mix, w_in, b_fgate, w_br_a, w_br_b, w_out, g_ffn, w_ffn_gate, w_ffn_up, w_ffn_down, g_final, loss_target, m_w_ada, m_b_ada, m_g_mix, m_w_in, m_b_fgate, m_w_br_a, m_w_br_b, m_w_out, m_g_ffn, m_w_ffn_gate, m_w_ffn_up, m_w_ffn_down, m_g_final, v_w_ada, v_b_ada, v_g_mix, v_w_in, v_b_fgate, v_w_br_a, v_w_br_b, v_w_out, v_g_ffn, v_w_ffn_gate, v_w_ffn_up, v_w_ffn_down, v_g_final):
    px, py, pc = _position()
    dev = 4 * px + 2 * py + pc
    x2d, tgt = x[0], loss_target[0]

    c_all = _all_gather(c, "gather_c").reshape(N_DEV, D)
    ada_cols = w_ada.shape[2]
    b_shard = lax.dynamic_slice(b_ada, (0, dev * ada_cols), (1, ada_cols))
    mod_shard = _ada_fwd(c_all, w_ada[0], b_shard)
    mod_all = _all_gather(mod_shard, "gather_mod")
    modv = lax.dynamic_index_in_dim(mod_all, dev, axis=1, keepdims=False).reshape(6, D)

    gate_up = jnp.concatenate([_pad_to(w_ffn_gate[0], 1, FF_PAD), _pad_to(w_ffn_up[0], 1, FF_PAD)], axis=1)
    shards = [_pad_to(w_in[0], 1, W_IN_PAD), w_br_a[0], w_br_b[0], w_out[0], gate_up, _pad_to(w_ffn_down[0], 0, FF_PAD)]
    w_in_s, w_a_s, w_b_s, w_o_s, w_gu_s, w_d_s = _all_gather_many([t.astype(bf16) for t in shards], "gather_weights")
    w_o = w_o_s.reshape(D, D)
    w_d = w_d_s.reshape(FF_HID, D)
    w_in_p = _slab_w_in(w_in_s)

    h1 = _pre1(x2d, modv, g_mix)
    proj = _matmul(h1, w_in_p, name="mm_proj", tm=SEQ, tn=896, tk=D)
    b_pad = jnp.pad(b_fgate, ((0, 0), (0, LANES - N_FOX_HEADS)))
    q_aug, k_aug, va = _fox_prep(proj, _fox_gate_fwd(proj, b_pad))
    ya_h, lse_a = _fox_fwd(q_aug, k_aug, va)
    ya = _matmul(ya_h, w_a_s, by_shard=True, name="mm_br_a", tm=SEQ, tn=W_BR_SH, tk=FOX_W)

    tables = _rope_tables()
    qb_r, kb_r, vb = _rope_fwd(proj, tables)
    q_c, k_c, v_c = _to_classes(qb_r), _to_classes(kb_r), _to_classes(vb)
    o_c, lse_c = _dil_fwd(q_c, k_c, v_c)
    yb_h, lse_b = _dil_combine(_from_classes(o_c), _from_classes(lse_c))
    yb = _matmul(yb_h, w_b_s, by_shard=True, name="mm_br_b", tm=SEQ, tn=W_BR_SH, tk=DIL_OUT_W)

    merged = _merge_fwd(ya, yb, proj)
    mix = _matmul(merged, w_o, name="mm_out", tm=SEQ, tn=512, tk=D)
    x1, h2 = _post1(x2d, mix, modv, g_ffn)
    au = _matmul(h2, w_gu_s, by_shard=True, name="mm_ffn_in", tm=SEQ, tn=2 * FF_PAD, tk=D)
    act = _swiglu_fwd(au)
    ff = _matmul(act, w_d, name="mm_ffn_down", tm=SEQ, tn=512, tk=FF_HID // 2)

    dx2, dff, dg_final, dga_f, loss_lanes = _final(x1, ff, tgt, modv, g_final.reshape(1, D))
    dact = _matmul(dff, w_d, tb=True, name="mm_d_act", tm=SEQ // 2, tn=FF_HID // 2, tk=D)
    dau = _swiglu_bwd(au, dact)

    core = pc.astype(jnp.int32).reshape(1)
    chip = (2 * px + py).astype(jnp.int32).reshape(1)

    def to_chips(by_dev, tags, name):
        from_pair = _pair_exchange(by_dev, "pair_exchange_" + name)
        sums = [_pair_add(g, r, core, "pair_add_" + t) for g, r, t in zip(by_dev, from_pair, tags)]
        state, token = _chip_exchange_start(sums, "chip_exchange_start_" + name)
        return sums, state, token

    def from_chips(sums, state, after, tags, name):
        got = _chip_exchange_wait(state, after, "chip_exchange_wait_" + name)
        return [_chip_add(p, r, chip, "chip_add_" + t) for p, r, t in zip(sums, got, tags)]

    g_gu = _matmul(h2, dau, ta=True, by_shard=True, out_dtype=bf16, name="mm_g_ffn_in", tm=D, tn=2 * FF_PAD, tk=SEQ)
    g_d = _matmul(act, dff, ta=True, out_dtype=bf16, name="mm_g_down", tm=FF_HID // 2, tn=512, tk=SEQ)
    ffn_tags = ["gu", "down"]
    ffn_sums, ffn_state, ffn_token = to_chips([g_gu, g_d.reshape(N_DEV, FF_PAD, D)], ffn_tags, "ffn")

    dh2 = _matmul(dau, w_gu_s, tb=True, by_shard=True, name="mm_d_h2", tm=SEQ // 2, tn=D, tk=2 * FF_PAD,
                  after=ffn_token)
    dx1, dmix, dsh_f, dsc_f, dg_ffn, dga_m = _mid_bwd(dh2, x1, dx2, mix, modv, g_ffn)
    dmerged = _matmul(dmix, w_o, tb=True, name="mm_d_merged", tm=SEQ, tn=512, tk=D)
    dya, dyb, dga, dgb = _merge_bwd(dmerged, ya, yb, proj)
    dya_h = _matmul(dya, w_a_s, tb=True, by_shard=True, name="mm_d_ya", tm=SEQ, tn=FOX_W, tk=W_BR_SH)
    dyb_h = _matmul(dyb, w_b_s, tb=True, by_shard=True, name="mm_d_yb", tm=SEQ, tn=DIL_OUT_W, tk=W_BR_SH)

    dqa, dka, dva, dF = _fox_bwd(q_aug, k_aug, va, dya_h, ya_h, lse_a)
    dF_row = jnp.pad(dF[:, :2, :].reshape(N_FOX_HEADS, SEQ), ((0, LANES - N_FOX_HEADS), (0, 0)))
    df, db_fgate = _fox_gate_bwd(dF_row, proj, b_pad)

    delta_b = _dil_delta(dyb_h, yb_h)
    rep = lambda t: _to_classes(jnp.tile(t, (1, N_GROUPS)))
    dq_c, dk_c, dv_c = _dil_bwd(q_c, k_c, v_c, rep(dyb_h), rep(lse_b), rep(delta_b))
    dqb, dkb = _rope_bwd(_from_classes(dq_c), _from_classes(dk_c), tables)
    dvb = _from_classes(dv_c).astype(bf16)

    dproj = _shard_pad_cols({"qa": dqa, "ka": dka, "va": dva, "f": df[:, :N_FOX_HEADS], "qb": dqb, "kb": dkb,
                             "vb": dvb, "ga": dga, "gb": dgb})
    g_in = _matmul(h1, dproj, ta=True, by_shard=True, out_dtype=bf16, name="mm_g_in", tm=D, tn=W_IN_PAD, tk=SEQ)
    g_o = _matmul(merged, dmix, ta=True, out_dtype=bf16, name="mm_g_out", tm=D, tn=512, tk=SEQ)
    g_a = _matmul(ya_h, dya, ta=True, by_shard=True, out_dtype=bf16, name="mm_g_br_a", tm=FOX_W, tn=W_BR_SH, tk=SEQ)
    g_b = _matmul(yb_h, dyb, ta=True, by_shard=True, out_dtype=bf16, name="mm_g_br_b", tm=DIL_OUT_W, tn=W_BR_SH,
                  tk=SEQ)
    rows_a, rows_b = FOX_W * W_BR_SH // D, DIL_OUT_W * W_BR_SH // D
    g_small = jnp.concatenate([g_a.reshape(N_DEV, rows_a, D), g_b.reshape(N_DEV, rows_b, D),
                               g_o.reshape(N_DEV, W_BR_SH, D)], axis=1)
    mix_tags = ["in", "small"]
    mix_sums, mix_state, mix_token = to_chips([g_in, g_small], mix_tags, "mixer")

    dh1 = _matmul(dproj, w_in_s, tb=True, by_shard=True, name="mm_d_h1", tm=SEQ // 2, tn=D, tk=W_IN_PAD,
                  after=mix_token)
    grad_x, dsh_m, dsc_m, dg_mix = _first_bwd(dh1, x2d, dx1, modv, g_mix)

    pad_lane = lambda t: jnp.pad(t, ((0, 0), (0, D - t.shape[1])))
    small = jnp.concatenate([dsh_m, dsc_m, dga_m, dsh_f, dsc_f, dga_f, dg_mix, dg_ffn, dg_final,
                             pad_lane(db_fgate), loss_lanes, jnp.zeros((SMALL_ROWS - 11, D), f32)], axis=0)
    small_all = _all_gather(small, "gather_small")
    small_sum, loss_row = _small_reduce(small_all)
    dmod_all = small_all[:, :6, :].reshape(N_DEV, 6 * D)
    g_w_ada = _ada_bwd(c_all, lax.dynamic_slice(dmod_all, (0, dev * ada_cols), (N_DEV, ada_cols)))

    s_gu, s_d = from_chips(ffn_sums, ffn_state, small_sum, ffn_tags, "ffn")
    s_in, s_small = from_chips(mix_sums, mix_state, g_w_ada, mix_tags, "mixer")
    g_shard = {
        "w_in": s_in[:, :W_IN_SH], "w_ffn_gate": s_gu[:, :W_FF_SH], "w_ffn_up": s_gu[:, FF_PAD:FF_PAD + W_FF_SH],
        "w_ffn_down": s_d[:W_FF_SH], "w_br_a": s_small[:rows_a].reshape(FOX_W, W_BR_SH),
        "w_br_b": s_small[rows_a:rows_a + rows_b].reshape(DIL_OUT_W, W_BR_SH), "w_out": s_small[rows_a + rows_b:],
    }

    loss = loss_row[0, 0]
    g = {
        "w_ada": g_w_ada[None], "b_ada": small_sum[0:6].reshape(1, 6 * D), "g_mix": small_sum[6:7],
        "w_in": g_shard["w_in"][None], "b_fgate": small_sum[9:10, :N_FOX_HEADS], "w_br_a": g_shard["w_br_a"][None],
        "w_br_b": g_shard["w_br_b"][None], "w_out": g_shard["w_out"][None], "g_ffn": small_sum[7:8],
        "w_ffn_gate": g_shard["w_ffn_gate"][None], "w_ffn_up": g_shard["w_ffn_up"][None],
        "w_ffn_down": g_shard["w_ffn_down"][None], "g_final": small_sum[8],
    }
    w = {"w_ada": w_ada, "b_ada": b_ada, "g_mix": g_mix, "w_in": w_in, "b_fgate": b_fgate, "w_br_a": w_br_a,
         "w_br_b": w_br_b, "w_out": w_out, "g_ffn": g_ffn, "w_ffn_gate": w_ffn_gate, "w_ffn_up": w_ffn_up,
         "w_ffn_down": w_ffn_down, "g_final": g_final}
    m = {"w_ada": m_w_ada, "b_ada": m_b_ada, "g_mix": m_g_mix, "w_in": m_w_in, "b_fgate": m_b_fgate,
         "w_br_a": m_w_br_a, "w_br_b": m_w_br_b, "w_out": m_w_out, "g_ffn": m_g_ffn, "w_ffn_gate": m_w_ffn_gate,
         "w_ffn_up": m_w_ffn_up, "w_ffn_down": m_w_ffn_down, "g_final": m_g_final}
    v = {"w_ada": v_w_ada, "b_ada": v_b_ada, "g_mix": v_g_mix, "w_in": v_w_in, "b_fgate": v_b_fgate,
         "w_br_a": v_w_br_a, "w_br_b": v_w_br_b, "w_out": v_w_out, "g_ffn": v_g_ffn, "w_ffn_gate": v_w_ffn_gate,
         "w_ffn_up": v_w_ffn_up, "w_ffn_down": v_w_ffn_down, "g_final": v_g_final}
    names = list(w)
    delta, new_m, new_v = {}, {}, {}
    for n in names:
        shape = w[n].shape
        two_d = (lambda t: t.reshape(shape[-2:])) if len(shape) == 3 else (lambda t: t)
        dl, mn, vn = _adamw(two_d(w[n]), two_d(g[n]), two_d(m[n]), two_d(v[n]), "adamw_" + n)
        delta[n], new_m[n], new_v[n] = dl.reshape(shape), mn.reshape(shape), vn.reshape(shape)

    return (loss, grad_x[None], *[g[n] for n in names], *[delta[n] for n in names],
            *[new_m[n] for n in names], *[new_v[n] for n in names])
```

```python
import functools

import jax
import jax.numpy as jnp
from jax import lax
from jax.experimental import pallas as pl
from jax.experimental.pallas import tpu as pltpu

f32 = jnp.float32
bf16 = jnp.bfloat16
SDS = jax.ShapeDtypeStruct
MESH = pl.DeviceIdType.MESH

N_DEV = 8
D = 1024
SEQ = 2048
HEAD_DIM = 64
N_FOX_HEADS = 8
FOX_W = 512
DIL_W = 768
DIL_OUT_W = 256
ROT_DIM = 16
ROPE_THETA = 500000.0
D_FF = 2816
IN_COLS = 5896
EPS = 1e-6
NEG = -1e30
ATT_SCALE = HEAD_DIM ** -0.5

ADAM_LR = 0.001
ADAM_B1 = 0.9
ADAM_B2 = 0.999
ADAM_EPS = 1e-08
ADAM_WD = 0.01
ADAM_STEP = 10

C_GA, C_GB, C_QB, C_KB, C_VB, C_QA, C_KA, C_VA, C_F = 0, 1024, 2304, 3072, 3840, 4608, 5120, 5632, 6144
PROJ_W = 6272
LANES = 128
VMEM_LIMIT = 52 * 1024 * 1024

W_IN_SH, W_IN_PAD = IN_COLS // N_DEV, 768
W_BR_SH = D // N_DEV
W_FF_SH, FF_PAD = D_FF // N_DEV, 384
FF_HID = N_DEV * FF_PAD
SMALL_ROWS = 16


def _params(sem=None):
    if sem is None:
        return pltpu.CompilerParams(vmem_limit_bytes=VMEM_LIMIT)
    return pltpu.CompilerParams(dimension_semantics=sem, vmem_limit_bytes=VMEM_LIMIT)


def _rowwise(fn, name, tiled, vecs, outs, reds=(), tile=256):
    nt, nv, no = len(tiled), len(vecs), len(outs)
    rows = tiled[0][0].shape[0]
    assert rows % tile == 0

    def body(*refs):
        tin = [r[...] for r in refs[:nt]]
        vin = [r[...] for r in refs[nt:nt + nv]]
        orefs = refs[nt + nv:nt + nv + no]
        rrefs = refs[nt + nv + no:]
        touts, routs = fn(tin, vin)
        for r, t in zip(orefs, touts, strict=True):
            r[...] = t.astype(r.dtype)
        if rrefs:
            @pl.when(pl.program_id(0) == 0)
            def _():
                for r in rrefs:
                    r[...] = jnp.zeros_like(r)
            for r, t in zip(rrefs, routs, strict=True):
                r[...] += t

    def col_map(cb):
        return lambda i: (i, cb)

    def whole_map(nd):
        return lambda i: (0,) * nd

    in_specs = [pl.BlockSpec((tile, w), col_map(cb)) for (_, w, cb) in tiled]
    in_specs += [pl.BlockSpec(v.shape, whole_map(v.ndim)) for v in vecs]
    out_specs = [pl.BlockSpec((tile, w), lambda i: (i, 0)) for (w, _) in outs]
    out_specs += [pl.BlockSpec((1, w), lambda i: (0, 0)) for w in reds]
    out_shape = [SDS((rows, w), dt) for (w, dt) in outs] + [SDS((1, w), f32) for w in reds]
    res = pl.pallas_call(
        body, grid=(rows // tile,), in_specs=in_specs, out_specs=out_specs, out_shape=out_shape, name=name,
        compiler_params=_params(("arbitrary",)),
    )(*[t[0] for t in tiled], *vecs)
    return res


def _matmul(a, b, *, ta=False, tb=False, out_dtype=f32, name, tm, tn, tk, by_shard=False, after=None):
    m, k = (a.shape[1], a.shape[0]) if ta else a.shape
    if by_shard and not ta:
        n, kb = (b.shape[1], N_DEV * b.shape[2]) if tb else (N_DEV * b.shape[2], b.shape[1])
        assert (tk if tb else tn) == b.shape[2]
    else:
        n, kb = (b.shape[0], b.shape[1]) if tb else (b.shape[1], b.shape[0])
    assert kb == k and m % tm == 0 and n % tn == 0 and k % tk == 0
    nk = k // tk
    dims = (((0 if ta else 1,), (1 if tb else 0,)), ((), ()))
    b_stacked = by_shard and not ta
    o_stacked = by_shard and ta

    def body(a_ref, b_ref, *rest):
        o_ref, *acc = rest[1:] if after is not None else rest
        bv = b_ref[0] if b_stacked else b_ref[...]
        p = lax.dot_general(a_ref[...].astype(bf16), bv.astype(bf16), dims, preferred_element_type=f32)

        def put(val):
            if o_stacked:
                o_ref[0] = val.astype(o_ref.dtype)
            else:
                o_ref[...] = val.astype(o_ref.dtype)

        if nk == 1:
            put(p)
        else:
            acc_ref, = acc
            kk = pl.program_id(2)

            @pl.when(kk == 0)
            def _():
                acc_ref[...] = p

            @pl.when(kk > 0)
            def _():
                acc_ref[...] += p

            @pl.when(kk == nk - 1)
            def _():
                put(acc_ref[...])

    a_spec = pl.BlockSpec((tk, tm), lambda i, j, kk: (kk, i)) if ta else pl.BlockSpec((tm, tk), lambda i, j, kk: (i, kk))
    if b_stacked and tb:
        b_spec = pl.BlockSpec((1, tn, tk), lambda i, j, kk: (kk, j, 0))
    elif b_stacked:
        b_spec = pl.BlockSpec((1, tk, tn), lambda i, j, kk: (j, kk, 0))
    elif tb:
        b_spec = pl.BlockSpec((tn, tk), lambda i, j, kk: (j, kk))
    else:
        b_spec = pl.BlockSpec((tk, tn), lambda i, j, kk: (kk, j))
    if o_stacked:
        assert tn == n // N_DEV
        out_spec = pl.BlockSpec((1, tm, tn), lambda i, j, kk: (j, i, 0))
        out_shape = SDS((N_DEV, m, tn), out_dtype)
    else:
        out_spec = pl.BlockSpec((tm, tn), lambda i, j, kk: (i, j))
        out_shape = SDS((m, n), out_dtype)
    extra_specs, extra = ([pl.BlockSpec(memory_space=pl.ANY)], [after]) if after is not None else ([], [])
    return pl.pallas_call(
        body, grid=(m // tm, n // tn, nk), in_specs=[a_spec, b_spec] + extra_specs, out_specs=out_spec,
        out_shape=out_shape, name=name, scratch_shapes=[pltpu.VMEM((tm, tn), f32)] if nk > 1 else [],
        compiler_params=_params(("parallel", "parallel", "arbitrary")),
    )(a, b, *extra)


def _rms(x):
    r = lax.rsqrt(jnp.mean(x * x, axis=-1, keepdims=True) + EPS)
    return r, x * r


def _rms_bwd(r, xn, dxn):
    return r * (dxn - xn * jnp.mean(dxn * xn, axis=-1, keepdims=True))


def _colsum(t):
    return jnp.sum(t, axis=0, keepdims=True)


def _sigmoid(x):
    return 1.0 / (1.0 + jnp.exp(-x))


def _modulated_norm(x, g, shift, scale):
    _, xn = _rms(x)
    return (xn * g) * (1.0 + scale) + shift


def _pre1(x, modv, g_mix):
    def fn(t, v):
        (xt,), (mv, g) = t, v
        return [_modulated_norm(xt, g, mv[0:1], mv[1:2])], []
    return _rowwise(fn, "pre1", [(x, D, 0)], [modv, g_mix], [(D, bf16)])[0]


def _post1(x, mix, modv, g_ffn):
    def fn(t, v):
        (xt, mt), (mv, g) = t, v
        x1 = xt + mv[2:3] * mt
        return [x1, _modulated_norm(x1, g, mv[3:4], mv[4:5])], []
    return _rowwise(fn, "post1", [(x, D, 0), (mix, D, 0)], [modv, g_ffn], [(D, f32), (D, bf16)])


def _gate_up(au, j):
    base = 2 * j * FF_PAD
    return au[:, base:base + FF_PAD], au[:, base + FF_PAD:base + 2 * FF_PAD]


def _swiglu_fwd(au):
    def fn(t, v):
        acts = []
        for j in range(N_DEV):
            a, u = _gate_up(t[0], j)
            acts.append(a * _sigmoid(a) * u)
        return [jnp.concatenate(acts, axis=1)], []
    return _rowwise(fn, "swiglu_fwd", [(au, 2 * FF_HID, 0)], [], [(FF_HID, bf16)])[0]


def _swiglu_bwd(au, dact):
    def fn(t, v):
        parts = []
        for j in range(N_DEV):
            a, u = _gate_up(t[0], j)
            d = t[1][:, j * FF_PAD:(j + 1) * FF_PAD]
            sg = _sigmoid(a)
            parts += [d * u * (sg * (1.0 + a * (1.0 - sg))), d * (a * sg)]
        return [jnp.concatenate(parts, axis=1)], []
    return _rowwise(fn, "swiglu_bwd", [(au, 2 * FF_HID, 0), (dact, FF_HID, 0)], [], [(2 * FF_HID, bf16)],
                    tile=128)[0]


def _final(x1, ff, target, modv, g_final):
    def fn(t, v):
        (x1t, fft, tgt), (mv, g) = t, v
        x2 = x1t + mv[5:6] * fft
        r, xn = _rms(x2)
        err = xn * g - tgt
        dy = err * (1.0 / D)
        dx2 = _rms_bwd(r, xn, dy * g)
        return [dx2, dx2 * mv[5:6]], [_colsum(dy * xn), _colsum(dx2 * fft), _colsum(err * err) * (0.5 / D)]
    return _rowwise(fn, "final", [(x1, D, 0), (ff, D, 0), (target, D, 0)], [modv, g_final],
                    [(D, f32), (D, bf16)], [D, D, D])


def _mid_bwd(dh2, x1, dx2, mix, modv, g_ffn):
    def fn(t, v):
        (dh, x1t, dx2t, mt), (mv, g) = t, v
        r, xn = _rms(x1t)
        dn = dh * (1.0 + mv[4:5])
        dx1 = dx2t + _rms_bwd(r, xn, dn * g)
        return [dx1, dx1 * mv[2:3]], [_colsum(dh), _colsum(dh * (xn * g)), _colsum(dn * xn), _colsum(dx1 * mt)]
    return _rowwise(fn, "mid_bwd", [(dh2, D, 0), (x1, D, 0), (dx2, D, 0), (mix, D, 0)], [modv, g_ffn],
                    [(D, f32), (D, bf16)], [D, D, D, D])


def _first_bwd(dh1, x, dx1, modv, g_mix):
    def fn(t, v):
        (dh, xt, dx1t), (mv, g) = t, v
        r, xn = _rms(xt)
        dn = dh * (1.0 + mv[1:2])
        return [dx1t + _rms_bwd(r, xn, dn * g)], [_colsum(dh), _colsum(dh * (xn * g)), _colsum(dn * xn)]
    return _rowwise(fn, "first_bwd", [(dh1, D, 0), (x, D, 0), (dx1, D, 0)], [modv, g_mix], [(D, f32)], [D, D, D])


def _merge_fwd(ya, yb, proj):
    def fn(t, v):
        ya_t, yb_t, ga, gb = t
        return [_sigmoid(ga) * ya_t + _sigmoid(gb) * yb_t], []
    return _rowwise(fn, "merge_fwd", [(ya, D, 0), (yb, D, 0), (proj, D, C_GA // D), (proj, D, C_GB // D)], [],
                    [(D, bf16)])[0]


def _merge_bwd(dmerged, ya, yb, proj):
    def fn(t, v):
        dm, ya_t, yb_t, ga, gb = t
        sa, sb = _sigmoid(ga), _sigmoid(gb)
        return [dm * sa, dm * sb, dm * ya_t * (sa * (1.0 - sa)), dm * yb_t * (sb * (1.0 - sb))], []
    return _rowwise(fn, "merge_bwd",
                    [(dmerged, D, 0), (ya, D, 0), (yb, D, 0), (proj, D, C_GA // D), (proj, D, C_GB // D)], [],
                    [(D, bf16), (D, bf16), (D, bf16), (D, bf16)])


def _rope_tables():
    half = ROT_DIM // 2
    pos = jnp.arange(SEQ, dtype=f32)
    inv_freq = ROPE_THETA ** (-jnp.arange(0, ROT_DIM, 2, dtype=f32) / ROT_DIM)
    ang = pos[:, None] * inv_freq[None, :]
    cos, sin = jnp.cos(ang), jnp.sin(ang)
    pad = jnp.zeros((SEQ, HEAD_DIM - ROT_DIM), f32)
    zero = jnp.zeros((SEQ, half), f32)
    c_head = jnp.concatenate([cos, cos, pad + 1.0], axis=1)
    lo_head = jnp.concatenate([-sin, zero, pad], axis=1)
    hi_head = jnp.concatenate([zero, sin, pad], axis=1)
    reps = DIL_W // HEAD_DIM
    return tuple(jnp.tile(t, (1, reps)) for t in (c_head, lo_head, hi_head))


def _rope_fwd(proj, tables):
    half = ROT_DIM // 2

    def fn(t, v):
        q, k, vv, c, lo, hi = t
        rot = lambda z: z * c + pltpu.roll(z, DIL_W - half, 1) * lo + pltpu.roll(z, half, 1) * hi
        return [rot(q) * ATT_SCALE, rot(k), vv], []
    return _rowwise(fn, "rope_fwd", [(proj, DIL_W, C_QB // DIL_W), (proj, DIL_W, C_KB // DIL_W),
                                     (proj, DIL_W, C_VB // DIL_W)] + [(tb, DIL_W, 0) for tb in tables], [],
                    [(DIL_W, bf16)] * 3)


def _rope_bwd(dq, dk, tables):
    half = ROT_DIM // 2

    def fn(t, v):
        dq_t, dk_t, c, lo, hi = t
        rot_t = lambda z: z * c + pltpu.roll(z * lo, half, 1) + pltpu.roll(z * hi, DIL_W - half, 1)
        return [rot_t(dq_t), rot_t(dk_t)], []
    return _rowwise(fn, "rope_bwd", [(dq, DIL_W, 0), (dk, DIL_W, 0)] + [(tb, DIL_W, 0) for tb in tables], [],
                    [(DIL_W, bf16), (DIL_W, bf16)])


def _head_bcast_sum(d):
    lane = lax.broadcasted_iota(jnp.int32, d.shape, 1)
    out = jnp.zeros_like(d)
    for h in range(d.shape[1] // HEAD_DIM):
        sel = (lane >= h * HEAD_DIM) & (lane < (h + 1) * HEAD_DIM)
        out = jnp.where(sel, jnp.sum(jnp.where(sel, d, 0.0), axis=1, keepdims=True), out)
    return out


def _dil_combine(o, lse):
    def fn(t, v):
        o0, o1, o2, l0, l1, l2 = t
        m = jnp.maximum(jnp.maximum(l0, l1), l2)
        w0, w1, w2 = jnp.exp(l0 - m), jnp.exp(l1 - m), jnp.exp(l2 - m)
        tot = w0 + w1 + w2
        return [(w0 * o0 + w1 * o1 + w2 * o2) / tot, m + jnp.log(tot)], []
    w = DIL_OUT_W
    return _rowwise(fn, "dil_combine", [(o, w, 0), (o, w, 1), (o, w, 2), (lse, w, 0), (lse, w, 1), (lse, w, 2)], [],
                    [(w, f32), (w, f32)])


def _dil_delta(dyb_h, yb_h):
    def fn(t, v):
        return [_head_bcast_sum(t[0] * t[1])], []
    return _rowwise(fn, "dil_delta", [(dyb_h, DIL_OUT_W, 0), (yb_h, DIL_OUT_W, 0)], [], [(DIL_OUT_W, f32)])[0]


def _adamw(w, g, m, v, name):
    shape = w.shape
    if w.ndim == 1:
        w, g, m, v = (t.reshape(1, -1) for t in (w, g, m, v))
    rows, cols = w.shape
    tile = 256 if rows % 256 == 0 and rows > 512 else rows

    def fn(t, _):
        wt, gt, mt, vt = t
        mn = ADAM_B1 * mt + (1.0 - ADAM_B1) * gt
        vn = ADAM_B2 * vt + (1.0 - ADAM_B2) * (gt * gt)
        m_hat = mn / (1.0 - ADAM_B1 ** ADAM_STEP)
        v_hat = vn / (1.0 - ADAM_B2 ** ADAM_STEP)
        return [-ADAM_LR * (m_hat / (jnp.sqrt(v_hat) + ADAM_EPS) + ADAM_WD * wt), mn, vn], []
    delta, mn, vn = _rowwise(fn, name, [(w, cols, 0), (g, cols, 0), (m, cols, 0), (v, cols, 0)], [],
                             [(cols, f32)] * 3, tile=tile)
    return delta.reshape(shape), mn.reshape(shape), vn.reshape(shape)


def _ada_fwd(c_all, w_shard, b_shard):
    def body(c_ref, w_ref, b_ref, o_ref):
        cv = c_ref[...]
        sc = (cv * _sigmoid(cv)).astype(bf16)
        o_ref[...] = jnp.dot(sc, w_ref[...].astype(bf16), preferred_element_type=f32) + b_ref[...]
    return pl.pallas_call(body, out_shape=SDS((N_DEV, w_shard.shape[1]), f32), name="ada_fwd",
                          compiler_params=_params())(c_all, w_shard, b_shard)


def _ada_bwd(c_all, dmod_cols):
    def body(c_ref, d_ref, o_ref):
        cv = c_ref[...]
        sc = cv * _sigmoid(cv)
        o_ref[...] = lax.dot_general(sc, d_ref[...], (((0,), (0,)), ((), ())), precision=lax.Precision.HIGHEST,
                                     preferred_element_type=f32)
    return pl.pallas_call(body, out_shape=SDS((D, dmod_cols.shape[1]), f32), name="ada_bwd",
                          compiler_params=_params())(c_all, dmod_cols)


def _small_reduce(gathered):
    def body(g_ref, o_ref, loss_ref):
        acc = g_ref[0]
        for d in range(1, N_DEV):
            acc = acc + g_ref[d]
        o_ref[...] = acc
        loss_ref[...] = jnp.zeros((1, LANES), f32) + jnp.sum(acc[10:11, :])
    return pl.pallas_call(body, out_shape=(SDS((SMALL_ROWS, D), f32), SDS((1, LANES), f32)), name="small_reduce",
                          compiler_params=_params())(gathered)


FOX_BLK = 512
CUM_BLK = 128


def _fold_lanes(t, op):
    out = t[:, :LANES]
    for j in range(1, t.shape[1] // LANES):
        out = op(out, t[:, j * LANES:(j + 1) * LANES])
    return out


def _fox_gate_fwd(proj, b_pad):
    nblk = SEQ // CUM_BLK

    def body(f_ref, b_ref, col_ref):
        r = lax.broadcasted_iota(jnp.int32, (CUM_BLK, CUM_BLK), 0)
        c = lax.broadcasted_iota(jnp.int32, (CUM_BLK, CUM_BLK), 1)
        tri = (r >= c).astype(f32)
        carry = jnp.zeros((1, LANES), f32)
        for blk in range(nblk):
            z = f_ref[blk * CUM_BLK:(blk + 1) * CUM_BLK, :] + b_ref[...]
            logf = jnp.minimum(z, 0.0) - jnp.log1p(jnp.exp(-jnp.abs(z)))
            cs = jnp.dot(tri, logf, precision=lax.Precision.HIGHEST, preferred_element_type=f32) + carry
            col_ref[blk * CUM_BLK:(blk + 1) * CUM_BLK, :] = cs
            carry = cs[CUM_BLK - 1:CUM_BLK, :]

    return pl.pallas_call(
        body, grid=(1,), in_specs=[pl.BlockSpec((SEQ, LANES), lambda i: (0, C_F // LANES)),
                                   pl.BlockSpec((1, LANES), lambda i: (0, 0))],
        out_specs=pl.BlockSpec((SEQ, LANES), lambda i: (0, 0)),
        out_shape=SDS((SEQ, LANES), f32), name="fox_gate_fwd",
        compiler_params=_params(("arbitrary",)),
    )(proj, b_pad)


def _fox_gate_bwd(dF_row, proj, b_pad):
    nblk = SEQ // CUM_BLK

    def body(d_ref, f_ref, b_ref, df_ref, db_ref, col_ref):
        r = lax.broadcasted_iota(jnp.int32, (CUM_BLK, CUM_BLK), 0)
        c = lax.broadcasted_iota(jnp.int32, (CUM_BLK, CUM_BLK), 1)
        tri = (r <= c).astype(f32)
        lane = lax.broadcasted_iota(jnp.int32, (CUM_BLK, LANES), 1)
        col_ref[...] = d_ref[...].T
        carry = jnp.zeros((1, LANES), f32)
        total = jnp.zeros((1, LANES), f32)
        for blk in reversed(range(nblk)):
            rows = slice(blk * CUM_BLK, (blk + 1) * CUM_BLK)
            cs = jnp.dot(tri, col_ref[rows, :], precision=lax.Precision.HIGHEST, preferred_element_type=f32) + carry
            carry = cs[0:1, :]
            z = f_ref[rows, :] + b_ref[...]
            df = jnp.where(lane < N_FOX_HEADS, cs * _sigmoid(-z), 0.0)
            df_ref[rows, :] = df.astype(df_ref.dtype)
            total = total + _colsum(df)
        db_ref[...] = total

    return pl.pallas_call(
        body, grid=(1,), in_specs=[pl.BlockSpec((LANES, SEQ), lambda i: (0, 0)),
                                   pl.BlockSpec((SEQ, LANES), lambda i: (0, C_F // LANES)),
                                   pl.BlockSpec((1, LANES), lambda i: (0, 0))],
        out_specs=[pl.BlockSpec((SEQ, LANES), lambda i: (0, 0)), pl.BlockSpec((1, LANES), lambda i: (0, 0))],
        out_shape=(SDS((SEQ, LANES), bf16), SDS((1, LANES), f32)), name="fox_gate_bwd",
        scratch_shapes=[pltpu.VMEM((SEQ, LANES), f32)],
        compiler_params=_params(("arbitrary",)),
    )(dF_row, proj, b_pad)


def _nt(a, b):
    return lax.dot_general(a, b, (((1,), (1,)), ((), ())), preferred_element_type=f32)


def _tn(a, b):
    return lax.dot_general(a, b, (((0,), (0,)), ((), ())), preferred_element_type=f32)


def _fox_prep(proj, f_col):
    def fn(t, v):
        q, k, vv, fc = t
        lane = lax.broadcasted_iota(jnp.int32, (q.shape[0], LANES), 1)
        qs, ks = [], []
        for h in range(N_FOX_HEADS):
            pair, pos = divmod(h, 2)
            own = (lane >= pos * HEAD_DIM) & (lane < (pos + 1) * HEAD_DIM)
            base = (1 - pos) * HEAD_DIM
            f = fc[:, h:h + 1]
            hi = f.astype(bf16).astype(f32)
            mid = (f - hi).astype(bf16).astype(f32)
            lo = (f - hi) - mid
            one = jnp.ones_like(f)
            qa = jnp.where(own, q[:, pair * LANES:(pair + 1) * LANES] * ATT_SCALE, 0.0)
            ka = k[:, pair * LANES:(pair + 1) * LANES]
            for idx, (qv, kv) in enumerate([(hi, one), (mid, one), (lo, one), (one, -hi), (one, -mid), (one, -lo)]):
                sel = lane == base + idx
                qa = jnp.where(sel, qv, qa)
                ka = jnp.where(sel, kv, ka)
            qs.append(qa)
            ks.append(ka)
        return [jnp.concatenate(qs, axis=1), jnp.concatenate(ks, axis=1), vv], []
    w = N_FOX_HEADS * LANES
    return _rowwise(fn, "fox_prep", [(proj, FOX_W, C_QA // FOX_W), (proj, FOX_W, C_KA // FOX_W),
                                     (proj, FOX_W, C_VA // FOX_W), (f_col, LANES, 0)], [],
                    [(w, bf16), (w, bf16), (FOX_W, bf16)])


def _fox_fwd(q_aug, k_aug, v):
    blk = FOX_BLK
    npair = FOX_W // LANES

    def body(q_ref, k_ref, v_ref, o_ref, lse_ref, s_scr):
        i = pl.program_id(1)
        tri = lax.broadcasted_iota(jnp.int32, (blk, blk), 0) >= lax.broadcasted_iota(jnp.int32, (blk, blk), 1)
        qh = [q_ref[:, h * LANES:(h + 1) * LANES] for h in range(2)]

        def logits(c, masked):
            off = pl.multiple_of(c * blk, blk)
            tops = []
            for h in range(2):
                s = _nt(qh[h], k_ref[pl.ds(off, blk), h * LANES:(h + 1) * LANES])
                if masked:
                    s = jnp.where(tri, s, NEG)
                s_scr[h, :, pl.ds(off, blk)] = s
                tops.append(_fold_lanes(s, jnp.maximum))
            return tops

        def pass_a(c, m):
            return tuple(jnp.maximum(a, b) for a, b in zip(m, logits(c, False)))

        m = lax.fori_loop(0, i, pass_a, tuple(jnp.full((blk, LANES), NEG, f32) for _ in range(2)))
        mx = [jnp.max(jnp.maximum(a, b), axis=1, keepdims=True) for a, b in zip(m, logits(i, True))]

        def pass_b(c, carry):
            off = pl.multiple_of(c * blk, blk)
            vv = v_ref[pl.ds(off, blk), :]
            new = []
            for h in range(2):
                l, acc = carry[h]
                p = jnp.exp(s_scr[h, :, pl.ds(off, blk)] - mx[h])
                new.append((l + _fold_lanes(p, jnp.add),
                            acc + jnp.dot(p.astype(bf16), vv, preferred_element_type=f32)))
            return tuple(new)

        zero = jnp.zeros((blk, LANES), f32)
        (l_a, acc_a), (l_b, acc_b) = lax.fori_loop(0, i + 1, pass_b, ((zero, zero), (zero, zero)))
        l_a = jnp.sum(l_a, axis=1, keepdims=True)
        l_b = jnp.sum(l_b, axis=1, keepdims=True)
        first = lax.broadcasted_iota(jnp.int32, (blk, LANES), 1) < HEAD_DIM
        o_ref[...] = jnp.where(first, acc_a / l_a, acc_b / l_b)
        lse_ref[0] = jnp.where(first, mx[0] + jnp.log(l_a), mx[1] + jnp.log(l_b))

    return pl.pallas_call(
        body, grid=(npair, SEQ // blk),
        in_specs=[pl.BlockSpec((blk, 2 * LANES), lambda p, i: (i, p)),
                  pl.BlockSpec((SEQ, 2 * LANES), lambda p, i: (0, p)),
                  pl.BlockSpec((SEQ, LANES), lambda p, i: (0, p))],
        out_specs=[pl.BlockSpec((blk, LANES), lambda p, i: (i, p)),
                   pl.BlockSpec((1, blk, LANES), lambda p, i: (p, i, 0))],
        out_shape=(SDS((SEQ, FOX_W), f32), SDS((npair, SEQ, LANES), f32)), name="fox_fwd",
        scratch_shapes=[pltpu.VMEM((2, blk, SEQ), f32)],
        compiler_params=_params(("parallel", "arbitrary")),
    )(q_aug, k_aug, v)


def _fox_bwd(q_aug, k_aug, v, do, o, lse):
    blk = FOX_BLK
    npair = FOX_W // LANES
    nblk = SEQ // blk

    def body(q_ref, k_ref, v_ref, do_ref, o_ref, lse_ref, dq_ref, dk_ref, dv_ref, df_ref,
             dq_acc, delta_ref, res_ref):
        lane_s = lax.broadcasted_iota(jnp.int32, (SEQ, LANES), 1)
        prod = do_ref[...] * o_ref[...]
        d_a = jnp.sum(jnp.where(lane_s < HEAD_DIM, prod, 0.0), axis=1, keepdims=True)
        d_b = jnp.sum(jnp.where(lane_s >= HEAD_DIM, prod, 0.0), axis=1, keepdims=True)
        delta_ref[...] = jnp.where(lane_s < HEAD_DIM, d_a, d_b)
        dq_acc[...] = jnp.zeros_like(dq_acc)
        res_ref[...] = jnp.zeros_like(res_ref)
        df_ref[...] = jnp.zeros_like(df_ref)
        lane = lax.broadcasted_iota(jnp.int32, (blk, LANES), 1)
        own = [lane < HEAD_DIM, lane >= HEAD_DIM]
        tri = lax.broadcasted_iota(jnp.int32, (blk, blk), 0) >= lax.broadcasted_iota(jnp.int32, (blk, blk), 1)

        def q_slab(qoff, h):
            return q_ref[pl.ds(qoff, blk), h * LANES:(h + 1) * LANES]

        def probs(qoff, h, k_h, masked):
            s = _nt(q_slab(qoff, h), k_h)
            if masked:
                s = jnp.where(tri, s, NEG)
            return jnp.exp(s - lse_ref[0, pl.ds(qoff, blk), h * HEAD_DIM:h * HEAD_DIM + 1])

        def k_slabs(koff):
            return [k_ref[pl.ds(koff, blk), h * LANES:(h + 1) * LANES] for h in range(2)]

        def kv_step(kj, _):
            koff = pl.multiple_of(kj * blk, blk)
            k_aug = k_slabs(koff)
            k_own = [jnp.where(own[h], k_aug[h], jnp.zeros_like(k_aug[h])) for h in range(2)]
            vv = v_ref[pl.ds(koff, blk), :]
            v_own = [jnp.where(own[h], vv, jnp.zeros_like(vv)) for h in range(2)]

            def q_tile(qi, carry, masked):
                qoff = pl.multiple_of(qi * blk, blk)
                dd = do_ref[pl.ds(qoff, blk), :].astype(bf16)
                new, dq_add = [], None
                for h in range(2):
                    dk_h, dv_h, dcol = carry[h]
                    p = probs(qoff, h, k_aug[h], masked)
                    dl = p * (_nt(dd, v_own[h]) - delta_ref[pl.ds(qoff, blk), h * HEAD_DIM:h * HEAD_DIM + 1])
                    dlb = dl.astype(bf16)
                    part = jnp.dot(dlb, k_own[h], preferred_element_type=f32)
                    dq_add = part if dq_add is None else dq_add + part
                    res_ref[h, pl.ds(qoff, blk), :] += _fold_lanes(dl, jnp.add)
                    new.append((dk_h + _tn(dlb, q_slab(qoff, h)), dv_h + _tn(p.astype(bf16), dd),
                                dcol + _colsum(dl)))
                dq_acc[pl.ds(qoff, blk), :] += dq_add * ATT_SCALE
                return tuple(new)

            zero = (jnp.zeros((blk, LANES), f32), jnp.zeros((blk, LANES), f32), jnp.zeros((1, blk), f32))
            carry = q_tile(kj, (zero, zero), True)
            (dk_a, dv_a, dcol_a), (dk_b, dv_b, dcol_b) = lax.fori_loop(
                kj + 1, nblk, lambda qi, cr: q_tile(qi, cr, False), carry)
            dk_ref[pl.ds(koff, blk), :] = jnp.where(own[0], dk_a, dk_b).astype(dk_ref.dtype)
            dv_ref[pl.ds(koff, blk), :] = jnp.where(own[0], dv_a, dv_b).astype(dv_ref.dtype)
            df_ref[0, 0:1, pl.ds(koff, blk)] = -dcol_a
            df_ref[0, 1:2, pl.ds(koff, blk)] = -dcol_b
            return 0

        lax.fori_loop(0, nblk, kv_step, 0)
        dq_ref[...] = dq_acc[...].astype(dq_ref.dtype)

        for h in range(2):
            res_ref[h] = jnp.zeros((SEQ, LANES), f32) + jnp.sum(res_ref[h], axis=1, keepdims=True)

        def kv_fix(kj, _):
            koff = pl.multiple_of(kj * blk, blk)
            k_aug = k_slabs(koff)

            def q_fix(qi, corr, masked):
                qoff = pl.multiple_of(qi * blk, blk)
                return tuple(corr[h] + _colsum(probs(qoff, h, k_aug[h], masked) * res_ref[h, pl.ds(qoff, blk), 0:1])
                             for h in range(2))

            zero = jnp.zeros((1, blk), f32)
            corr = lax.fori_loop(kj + 1, nblk, lambda qi, cr: q_fix(qi, cr, False), q_fix(kj, (zero, zero), True))
            df_ref[0, 0:1, pl.ds(koff, blk)] += corr[0]
            df_ref[0, 1:2, pl.ds(koff, blk)] += corr[1]
            return 0

        lax.fori_loop(0, nblk, kv_fix, 0)

    pair_aug = pl.BlockSpec((SEQ, 2 * LANES), lambda p: (0, p))
    slab = pl.BlockSpec((SEQ, LANES), lambda p: (0, p))
    per_pair = pl.BlockSpec((1, SEQ, LANES), lambda p: (p, 0, 0))
    rows = pl.BlockSpec((1, 8, SEQ), lambda p: (p, 0, 0))
    return pl.pallas_call(
        body, grid=(npair,),
        in_specs=[pair_aug, pair_aug, slab, slab, slab, per_pair],
        out_specs=[slab, slab, slab, rows],
        out_shape=(SDS((SEQ, FOX_W), bf16),) * 3 + (SDS((npair, 8, SEQ), f32),), name="fox_bwd",
        scratch_shapes=[pltpu.VMEM((SEQ, LANES), f32), pltpu.VMEM((SEQ, LANES), f32),
                        pltpu.VMEM((2, SEQ, LANES), f32)],
        compiler_params=_params(("parallel",)),
    )(q_aug, k_aug, v, do, o, lse)


DIL_BLK = 128
N_GROUPS = 3
DIL_PAIRS = DIL_OUT_W // LANES
DIL_NBLK = SEQ // DIL_BLK


def _blocks_per_seq(g):
    return jnp.where(g == 0, 16, jnp.where(g == 1, 4, 1))


def _dil_block(n, g):
    rows = slice(n * DIL_BLK, (n + 1) * DIL_BLK)
    if n == 0:
        r = lax.broadcasted_iota(jnp.int32, (DIL_BLK, DIL_BLK), 0)
        c = lax.broadcasted_iota(jnp.int32, (DIL_BLK, DIL_BLK), 1)
        return rows, rows, r >= c
    r = lax.broadcasted_iota(jnp.int32, (DIL_BLK, 2 * DIL_BLK), 0)
    c = lax.broadcasted_iota(jnp.int32, (DIL_BLK, 2 * DIL_BLK), 1)
    has_prev = (n & (_blocks_per_seq(g) - 1)) > 0
    mask = ((c < DIL_BLK) & (c >= r) & has_prev) | ((c >= DIL_BLK) & (c - DIL_BLK <= r))
    return rows, slice((n - 1) * DIL_BLK, (n + 1) * DIL_BLK), mask


def _dil_spec():
    return pl.BlockSpec((1, SEQ, LANES), lambda g, p: (g, 0, p))


def _dil_fwd(q, k, v):
    def body(q_ref, k_ref, v_ref, o_ref, lse_ref):
        g = pl.program_id(0)
        first = lax.broadcasted_iota(jnp.int32, (DIL_BLK, LANES), 1) < HEAD_DIM
        for n in range(DIL_NBLK):
            rows, krows, mask = _dil_block(n, g)
            qv, kk, vv = q_ref[0, rows, :], k_ref[0, krows, :], v_ref[0, krows, :]
            outs, lses = [], []
            for own in (first, ~first):
                s = jnp.where(mask, _nt(jnp.where(own, qv, jnp.zeros_like(qv)), kk), NEG)
                m = jnp.max(s, axis=1, keepdims=True)
                p = jnp.exp(s - m)
                l = jnp.sum(p, axis=1, keepdims=True)
                outs.append(jnp.dot(p.astype(bf16), vv, preferred_element_type=f32) / l)
                lses.append(m + jnp.log(l))
            o_ref[0, rows, :] = jnp.where(first, outs[0], outs[1])
            lse_ref[0, rows, :] = jnp.where(first, lses[0], lses[1])

    spec = _dil_spec()
    shape = SDS((N_GROUPS, SEQ, DIL_OUT_W), f32)
    return pl.pallas_call(
        body, grid=(N_GROUPS, DIL_PAIRS), in_specs=[spec] * 3, out_specs=[spec] * 2,
        out_shape=(shape, shape), name="dil_fwd", compiler_params=_params(("parallel", "parallel")),
    )(q, k, v)


def _dil_bwd(q, k, v, do, lse, delta):
    def body(q_ref, k_ref, v_ref, do_ref, lse_ref, dl_ref, dq_ref, dk_ref, dv_ref):
        g = pl.program_id(0)
        first = lax.broadcasted_iota(jnp.int32, (DIL_BLK, LANES), 1) < HEAD_DIM
        dk_ref[...] = jnp.zeros_like(dk_ref)
        dv_ref[...] = jnp.zeros_like(dv_ref)
        for n in range(DIL_NBLK):
            rows, krows, mask = _dil_block(n, g)
            qv, kk, vv = q_ref[0, rows, :], k_ref[0, krows, :], v_ref[0, krows, :]
            dov = do_ref[0, rows, :].astype(bf16)
            lsev, delv = lse_ref[0, rows, :], dl_ref[0, rows, :]
            dqs, dk_add, dv_add = [], None, None
            for h, own in enumerate((first, ~first)):
                col = h * HEAD_DIM
                qh = jnp.where(own, qv, jnp.zeros_like(qv))
                doh = jnp.where(own, dov, jnp.zeros_like(dov))
                p = jnp.exp(jnp.where(mask, _nt(qh, kk), NEG) - lsev[:, col:col + 1])
                dl = (p * (_nt(doh, vv) - delv[:, col:col + 1])).astype(bf16)
                dqs.append(jnp.dot(dl, kk, preferred_element_type=f32))
                dk_h, dv_h = _tn(dl, qh), _tn(p.astype(bf16), doh)
                dk_add = dk_h if dk_add is None else dk_add + dk_h
                dv_add = dv_h if dv_add is None else dv_add + dv_h
            dq_ref[0, rows, :] = jnp.where(first, dqs[0], dqs[1]) * ATT_SCALE
            dk_ref[0, krows, :] += dk_add
            dv_ref[0, krows, :] += dv_add

    spec = _dil_spec()
    shape = SDS((N_GROUPS, SEQ, DIL_OUT_W), f32)
    return pl.pallas_call(
        body, grid=(N_GROUPS, DIL_PAIRS), in_specs=[spec] * 6, out_specs=[spec] * 3,
        out_shape=(shape, shape, shape), name="dil_bwd", compiler_params=_params(("parallel", "parallel")),
    )(q, k, v, do, lse, delta)


_DILATIONS = (1, 4, 16)


def _to_classes(t):
    w = t.shape[1] // N_GROUPS
    out = []
    for g, d in enumerate(_DILATIONS):
        s = t[:, g * w:(g + 1) * w]
        out.append(s.reshape(SEQ // d, d, w).transpose(1, 0, 2).reshape(SEQ, w))
    return jnp.stack(out)


def _from_classes(t):
    w = t.shape[2]
    out = [t[g].reshape(d, SEQ // d, w).transpose(1, 0, 2).reshape(SEQ, w) for g, d in enumerate(_DILATIONS)]
    return jnp.concatenate(out, axis=1)


def _position():
    return lax.axis_index("x"), lax.axis_index("y"), lax.axis_index("c")


def _all_gather(block, name):
    def body(x_ref, out_ref, send_sems, recv_sems, local_sem):
        x, y, c = _position()
        me, sibling = (x, y, c), (x, y, 1 - c)
        chips = [(1 - x, y), (x, 1 - y), (1 - x, 1 - y)]

        def slot(px, py, pc):
            return out_ref.at[4 * px + 2 * py + pc]

        def copy(k, blk, to, src=None):
            return pltpu.make_async_remote_copy(
                src_ref=slot(*blk) if src is None else src, dst_ref=slot(*blk),
                send_sem=send_sems.at[k], recv_sem=recv_sems.at[k], device_id=to, device_id_type=MESH)

        mine = pltpu.make_async_copy(x_ref, slot(*me), local_sem)
        mine.start()
        first = [copy(0, me, sibling, src=x_ref)]
        first += [copy(1 + j, me, (*chip, c), src=x_ref) for j, chip in enumerate(chips)]
        for cp in first:
            cp.start()
        passed = [copy(4 + j, (*chip, c), sibling) for j, chip in enumerate(chips)]
        for j, chip in enumerate(chips):
            copy(1 + j, (*chip, c), me).wait_recv()
            passed[j].start()
        copy(0, sibling, me).wait_recv()
        for j, chip in enumerate(chips):
            copy(4 + j, (*chip, 1 - c), me).wait_recv()
        for cp in first + passed:
            cp.wait_send()
        mine.wait()

    return pl.pallas_call(
        body, out_shape=SDS((N_DEV,) + block.shape, block.dtype),
        in_specs=[pl.BlockSpec(memory_space=pl.ANY)], out_specs=pl.BlockSpec(memory_space=pl.ANY),
        scratch_shapes=[pltpu.SemaphoreType.DMA((7,)), pltpu.SemaphoreType.DMA((7,)), pltpu.SemaphoreType.DMA],
        name=name,
    )(block)


def _all_gather_many(blocks, name):
    n = len(blocks)

    def body(*refs):
        x_refs, out_refs = refs[:n], refs[n:2 * n]
        send_sems, recv_sems, local_sems = refs[2 * n:]
        x, y, c = _position()
        me, sibling = (x, y, c), (x, y, 1 - c)
        chips = [(1 - x, y), (x, 1 - y), (1 - x, 1 - y)]

        def slot(a, px, py, pc):
            return out_refs[a].at[4 * px + 2 * py + pc]

        def copy(a, k, blk, to, own=False):
            return pltpu.make_async_remote_copy(
                src_ref=x_refs[a] if own else slot(a, *blk), dst_ref=slot(a, *blk),
                send_sem=send_sems.at[a, k], recv_sem=recv_sems.at[a, k], device_id=to, device_id_type=MESH)

        mine = [pltpu.make_async_copy(x_refs[a], slot(a, *me), local_sems.at[a]) for a in range(n)]
        for cp in mine:
            cp.start()
        started = []
        for a in range(n):
            first = [copy(a, 0, me, sibling, own=True)]
            first += [copy(a, 1 + j, me, (*chip, c), own=True) for j, chip in enumerate(chips)]
            for cp in first:
                cp.start()
            started += first
        for a in range(n):
            for j, chip in enumerate(chips):
                copy(a, 1 + j, (*chip, c), me).wait_recv()
                passed = copy(a, 4 + j, (*chip, c), sibling)
                passed.start()
                started.append(passed)
        for a in range(n):
            copy(a, 0, sibling, me).wait_recv()
            for j, chip in enumerate(chips):
                copy(a, 4 + j, (*chip, 1 - c), me).wait_recv()
        for cp in started:
            cp.wait_send()
        for cp in mine:
            cp.wait()

    hbm = pl.BlockSpec(memory_space=pl.ANY)
    return pl.pallas_call(
        body, out_shape=[SDS((N_DEV,) + b.shape, b.dtype) for b in blocks],
        in_specs=[hbm] * n, out_specs=[hbm] * n,
        scratch_shapes=[pltpu.SemaphoreType.DMA((n, 7)), pltpu.SemaphoreType.DMA((n, 7)),
                        pltpu.SemaphoreType.DMA((n,))],
        name=name,
    )(*blocks)


def _pair_exchange(gs, name):
    n = len(gs)

    def body(*refs):
        g_refs, r_refs, send_sems, recv_sems = refs[:n], refs[n:2 * n], refs[2 * n], refs[2 * n + 1]
        x, y, c = _position()
        copies = []
        for a in range(n):
            for k in range(4):
                cp = pltpu.make_async_remote_copy(
                    src_ref=g_refs[a].at[2 * k + (1 - c)], dst_ref=r_refs[a].at[k], send_sem=send_sems.at[a, k],
                    recv_sem=recv_sems.at[a, k], device_id=(x, y, 1 - c), device_id_type=MESH)
                cp.start()
                copies.append(cp)
        for cp in copies:
            cp.wait()

    hbm = pl.BlockSpec(memory_space=pl.ANY)
    return pl.pallas_call(
        body, out_shape=[SDS((4,) + g.shape[1:], g.dtype) for g in gs], in_specs=[hbm] * n, out_specs=[hbm] * n,
        scratch_shapes=[pltpu.SemaphoreType.DMA((n, 4)), pltpu.SemaphoreType.DMA((n, 4))], name=name,
    )(*gs)


HBM_SPEC = pl.BlockSpec(memory_space=pltpu.HBM)
SEM_SPEC = pl.BlockSpec(memory_space=pltpu.SEMAPHORE)
SPLIT_COPY = pltpu.CompilerParams(has_side_effects=pltpu.SideEffectType.DATAFLOW_SIDE_EFFECTING)


def _in_hbm(t):
    return pltpu.with_memory_space_constraint(t, pltpu.HBM)


def _chip_copies(t_refs, land_refs, send_sems, recv_sems):
    x, y, c = _position()
    chips = [(1 - x, y), (x, 1 - y), (1 - x, 1 - y)]
    return [pltpu.make_async_remote_copy(
        src_ref=t.at[2 * px + py], dst_ref=land.at[j], send_sem=send_sems.at[3 * a + j],
        recv_sem=recv_sems.at[3 * a + j], device_id=(px, py, c), device_id_type=MESH)
        for a, (t, land) in enumerate(zip(t_refs, land_refs, strict=True)) for j, (px, py) in enumerate(chips)]


def _chip_exchange_start(ts, name):
    n = len(ts)
    lands = [_in_hbm(lax.empty((3,) + t.shape[1:], t.dtype)) for t in ts]

    def body(*refs):
        send_sems, recv_sems = refs[2 * n], refs[2 * n + 1]
        for cp in _chip_copies(refs[:n], refs[n:2 * n], send_sems, recv_sems):
            cp.start()
        refs[-1][...] = jnp.zeros_like(refs[-1])

    sems = pltpu.SemaphoreType.DMA((3 * n,))
    res = pl.pallas_call(
        body, name=name, in_specs=[HBM_SPEC] * (2 * n),
        out_shape=(sems, sems, *[pltpu.HBM(t.shape, t.dtype) for t in (*ts, *lands)], SDS((8, LANES), f32)),
        out_specs=(SEM_SPEC, SEM_SPEC, *[HBM_SPEC] * (2 * n), pl.BlockSpec(memory_space=pltpu.VMEM)),
        input_output_aliases={i: 2 + i for i in range(2 * n)}, compiler_params=SPLIT_COPY,
    )(*[_in_hbm(t) for t in ts], *lands)
    return res[:-1], res[-1]


def _chip_exchange_wait(state, after, name):
    send_sems, recv_sems, *arrays = state
    n = len(arrays) // 2

    def body(*refs):
        for cp in _chip_copies(refs[:n], refs[n:2 * n], refs[2 * n], refs[2 * n + 1]):
            cp.wait_send()
            cp.wait_recv()

    res = pl.pallas_call(
        body, name=name, in_specs=[HBM_SPEC] * (2 * n) + [SEM_SPEC, SEM_SPEC, pl.BlockSpec(memory_space=pl.ANY)],
        out_shape=[pltpu.HBM(t.shape, t.dtype) for t in arrays], out_specs=[HBM_SPEC] * (2 * n),
        input_output_aliases={i: i for i in range(2 * n)}, compiler_params=SPLIT_COPY,
    )(*arrays, send_sems, recv_sems, after)
    return res[n:]


def _gather_copies(x_refs, out_refs, send_sems, recv_sems):
    x, y, c = _position()
    peers = [(x, y, 1 - c), (1 - x, y, c), (x, 1 - y, c), (1 - x, 1 - y, c)]
    sends, arrivals = [], []
    for a, (x_ref, out_ref) in enumerate(zip(x_refs, out_refs, strict=True)):
        for k, (px, py, pc) in enumerate(peers):
            sems = dict(send_sem=send_sems.at[4 * a + k], recv_sem=recv_sems.at[4 * a + k],
                        device_id=(px, py, pc), device_id_type=MESH)
            sends.append(pltpu.make_async_remote_copy(src_ref=x_ref, dst_ref=out_ref.at[4 * x + 2 * y + c], **sems))
            arrivals.append(pltpu.make_async_remote_copy(src_ref=x_ref, dst_ref=out_ref.at[4 * px + 2 * py + pc],
                                                         **sems))
    return sends, arrivals


def _gather_start(blocks, after, name):
    n = len(blocks)
    outs = [_in_hbm(lax.empty((N_DEV,) + b.shape, b.dtype)) for b in blocks]

    def body(*refs):
        sends, _ = _gather_copies(refs[:n], refs[n:2 * n], refs[2 * n + 1], refs[2 * n + 2])
        for cp in sends:
            cp.start()
        refs[-1][...] = jnp.zeros_like(refs[-1])

    sems = pltpu.SemaphoreType.DMA((4 * n,))
    res = pl.pallas_call(
        body, name=name, in_specs=[HBM_SPEC] * (2 * n) + [pl.BlockSpec(memory_space=pl.ANY)],
        out_shape=(sems, sems, *[pltpu.HBM(t.shape, t.dtype) for t in (*blocks, *outs)], SDS((8, LANES), f32)),
        out_specs=(SEM_SPEC, SEM_SPEC, *[HBM_SPEC] * (2 * n), pl.BlockSpec(memory_space=pltpu.VMEM)),
        input_output_aliases={i: 2 + i for i in range(2 * n)}, compiler_params=SPLIT_COPY,
    )(*[_in_hbm(b) for b in blocks], *outs, after)
    return res[:-1], res[-1]


def _gather_wait(state, after, name):
    send_sems, recv_sems, *arrays = state
    n = len(arrays) // 2

    def body(*refs):
        sends, arrivals = _gather_copies(refs[:n], refs[n:2 * n], refs[2 * n], refs[2 * n + 1])
        for cp in sends:
            cp.wait_send()
        for cp in arrivals:
            cp.wait_recv()

    res = pl.pallas_call(
        body, name=name, in_specs=[HBM_SPEC] * (2 * n) + [SEM_SPEC, SEM_SPEC, pl.BlockSpec(memory_space=pl.ANY)],
        out_shape=[pltpu.HBM(t.shape, t.dtype) for t in arrays], out_specs=[HBM_SPEC] * (2 * n),
        input_output_aliases={i: i for i in range(2 * n)}, compiler_params=SPLIT_COPY,
    )(*arrays, send_sems, recv_sems, after)
    return res[:n], res[n:]


def _gather_finish(blocks, partial, name):
    n = len(blocks)

    def body(*refs):
        x_refs, in_refs, out_refs = refs[:n], refs[n:2 * n], refs[2 * n:3 * n]
        send_sems, recv_sems, local_sems = refs[3 * n:]
        x, y, c = _position()
        chips = [(1 - x, y), (x, 1 - y), (1 - x, 1 - y)]
        copies, mine = [], []
        for a in range(n):
            own = pltpu.make_async_copy(x_refs[a], out_refs[a].at[4 * x + 2 * y + c], local_sems.at[a])
            own.start()
            mine.append(own)
            for j, (px, py) in enumerate(chips):
                cp = pltpu.make_async_remote_copy(
                    src_ref=in_refs[a].at[4 * px + 2 * py + c], dst_ref=out_refs[a].at[4 * px + 2 * py + c],
                    send_sem=send_sems.at[a, j], recv_sem=recv_sems.at[a, j], device_id=(x, y, 1 - c),
                    device_id_type=MESH)
                cp.start()
                copies.append(cp)
        for a in range(n):
            for j, (px, py) in enumerate(chips):
                pltpu.make_async_remote_copy(
                    src_ref=in_refs[a].at[4 * px + 2 * py + (1 - c)], dst_ref=out_refs[a].at[4 * px + 2 * py + (1 - c)],
                    send_sem=send_sems.at[a, j], recv_sem=recv_sems.at[a, j], device_id=(x, y, 1 - c),
                    device_id_type=MESH).wait_recv()
        for cp in copies:
            cp.wait_send()
        for cp in mine:
            cp.wait()

    hbm = pl.BlockSpec(memory_space=pl.ANY)
    return pl.pallas_call(
        body, out_shape=[SDS(p.shape, p.dtype) for p in partial], in_specs=[hbm] * (2 * n), out_specs=[hbm] * n,
        input_output_aliases={n + a: a for a in range(n)},
        scratch_shapes=[pltpu.SemaphoreType.DMA((n, 3)), pltpu.SemaphoreType.DMA((n, 3)),
                        pltpu.SemaphoreType.DMA((n,))],
        name=name,
    )(*blocks, *partial)


def _row_tile(rows):
    return 256 if rows % 256 == 0 and rows > 512 else rows


def _pair_add(g, r1, core, name):
    def body(c_ref, g_ref, r_ref, o_ref):
        o_ref[...] = (g_ref[...].astype(f32) + r_ref[...].astype(f32)).astype(o_ref.dtype)

    rows, cols = g.shape[1:]
    tile = _row_tile(rows)
    blk = (1, tile, cols)
    return pl.pallas_call(
        body, out_shape=SDS((4, rows, cols), g.dtype), name=name,
        grid_spec=pltpu.PrefetchScalarGridSpec(
            num_scalar_prefetch=1, grid=(4, rows // tile),
            in_specs=[pl.BlockSpec(blk, lambda k, i, c_ref: (2 * k + c_ref[0], i, 0)),
                      pl.BlockSpec(blk, lambda k, i, c_ref: (k, i, 0))],
            out_specs=pl.BlockSpec(blk, lambda k, i, c_ref: (k, i, 0))),
        compiler_params=_params(("parallel", "arbitrary")),
    )(core, g, r1)


def _chip_add(t, r2, chip, name):
    def body(c_ref, t_ref, r_ref, o_ref):
        o_ref[...] = ((t_ref[0].astype(f32) + r_ref[0].astype(f32)) + r_ref[1].astype(f32)) + r_ref[2].astype(f32)

    rows, cols = t.shape[1:]
    tile = _row_tile(rows)
    return pl.pallas_call(
        body, out_shape=SDS((rows, cols), f32), name=name,
        grid_spec=pltpu.PrefetchScalarGridSpec(
            num_scalar_prefetch=1, grid=(rows // tile,),
            in_specs=[pl.BlockSpec((1, tile, cols), lambda i, c_ref: (c_ref[0], i, 0)),
                      pl.BlockSpec((3, tile, cols), lambda i, c_ref: (0, i, 0))],
            out_specs=pl.BlockSpec((tile, cols), lambda i, c_ref: (i, 0))),
        compiler_params=_params(("arbitrary",)),
    )(chip, t, r2)


def _pad_to(t, axis, size):
    pads = [(0, 0)] * t.ndim
    pads[axis] = (0, size - t.shape[axis])
    return jnp.pad(t, pads)


_REF_COLS = {"qa": (0, FOX_W), "ka": (FOX_W, FOX_W), "va": (2 * FOX_W, FOX_W), "f": (3 * FOX_W, N_FOX_HEADS)}
_REF_COLS.update({n: (3 * FOX_W + N_FOX_HEADS + i * DIL_W, DIL_W) for i, n in enumerate(("qb", "kb", "vb"))})
_REF_COLS.update({n: (3 * FOX_W + N_FOX_HEADS + 3 * DIL_W + i * D, D) for i, n in enumerate(("ga", "gb"))})
_REF_ORDER = ("qa", "ka", "va", "f", "qb", "kb", "vb", "ga", "gb")


def _shard_pad_cols(pieces):
    first = pieces[_REF_ORDER[0]]
    pad = jnp.zeros((first.shape[0], W_IN_PAD - W_IN_SH), first.dtype)
    parts, names, used = [], list(_REF_ORDER), 0
    for _ in range(N_DEV):
        need = W_IN_SH
        while need:
            take = min(need, _REF_COLS[names[0]][1] - used)
            parts.append(pieces[names[0]][:, used:used + take])
            need, used = need - take, used + take
            if used == _REF_COLS[names[0]][1]:
                names, used = names[1:], 0
        parts.append(pad)
    return jnp.concatenate(parts, axis=1)


def _slab_w_in(stack):
    def cols(name):
        lo, width = _REF_COLS[name]
        hi, out = lo + width, []
        while lo < hi:
            j, off = divmod(lo, W_IN_SH)
            n = min(hi - lo, W_IN_SH - off)
            out.append(stack[j, :, off:off + n])
            lo += n
        return out
    z = lambda n: [jnp.zeros((stack.shape[1], n), stack.dtype)]
    parts = (cols("ga") + cols("gb") + z(C_QB - 2 * D) + cols("qb") + cols("kb") + cols("vb") + cols("qa")
             + cols("ka") + cols("va") + cols("f") + z(LANES - N_FOX_HEADS))
    return jnp.concatenate(parts, axis=1)


def kernel(x, c, w_ada, b_ada, g_mix, w_in, b_fgate, w_br_a, w_br_b, w_out, g_ffn, w_ffn_gate, w_ffn_up, w_ffn_down, g_final, loss_target, m_w_ada, m_b_ada, m_g_mix, m_w_in, m_b_fgate, m_w_br_a, m_w_br_b, m_w_out, m_g_ffn, m_w_ffn_gate, m_w_ffn_up, m_w_ffn_down, m_g_final, v_w_ada, v_b_ada, v_g_mix, v_w_in, v_b_fgate, v_w_br_a, v_w_br_b, v_w_out, v_g_ffn, v_w_ffn_gate, v_w_ffn_up, v_w_ffn_down, v_g_final):
    px, py, pc = _position()
    dev = 4 * px + 2 * py + pc
    x2d, tgt = x[0], loss_target[0]

    c_all = _all_gather(c, "gather_c").reshape(N_DEV, D)
    ada_cols = w_ada.shape[2]
    b_shard = lax.dynamic_slice(b_ada, (0, dev * ada_cols), (1, ada_cols))
    mod_shard = _ada_fwd(c_all, w_ada[0], b_shard)
    mod_all = _all_gather(mod_shard, "gather_mod")
    modv = lax.dynamic_index_in_dim(mod_all, dev, axis=1, keepdims=False).reshape(6, D)

    gate_up = jnp.concatenate([_pad_to(w_ffn_gate[0], 1, FF_PAD), _pad_to(w_ffn_up[0], 1, FF_PAD)], axis=1)
    w_in_s, = _all_gather_many([_pad_to(w_in[0], 1, W_IN_PAD).astype(bf16)], "gather_w_in")
    later = [w_br_a[0], w_br_b[0], w_out[0], gate_up, _pad_to(w_ffn_down[0], 0, FF_PAD)]
    later_state, later_token = _gather_start([t.astype(bf16) for t in later], w_in_s, "gather_rest_start")
    w_in_p = _slab_w_in(w_in_s)

    h1 = _pre1(x2d, modv, g_mix)
    proj = _matmul(h1, w_in_p, name="mm_proj", tm=SEQ, tn=896, tk=D, after=later_token)
    b_pad = jnp.pad(b_fgate, ((0, 0), (0, LANES - N_FOX_HEADS)))
    q_aug, k_aug, va = _fox_prep(proj, _fox_gate_fwd(proj, b_pad))
    ya_h, lse_a = _fox_fwd(q_aug, k_aug, va)

    tables = _rope_tables()
    qb_r, kb_r, vb = _rope_fwd(proj, tables)
    q_c, k_c, v_c = _to_classes(qb_r), _to_classes(kb_r), _to_classes(vb)
    o_c, lse_c = _dil_fwd(q_c, k_c, v_c)
    yb_h, lse_b = _dil_combine(_from_classes(o_c), _from_classes(lse_c))

    w_a_s, w_b_s, w_o_s, w_gu_s, w_d_s = _gather_finish(*_gather_wait(later_state, yb_h, "gather_rest_wait"),
                                                        "gather_rest_finish")
    w_o = w_o_s.reshape(D, D)
    w_d = w_d_s.reshape(FF_HID, D)
    ya = _matmul(ya_h, w_a_s, by_shard=True, name="mm_br_a", tm=SEQ, tn=W_BR_SH, tk=FOX_W)
    yb = _matmul(yb_h, w_b_s, by_shard=True, name="mm_br_b", tm=SEQ, tn=W_BR_SH, tk=DIL_OUT_W)

    merged = _merge_fwd(ya, yb, proj)
    mix = _matmul(merged, w_o, name="mm_out", tm=SEQ, tn=512, tk=D)
    x1, h2 = _post1(x2d, mix, modv, g_ffn)
    au = _matmul(h2, w_gu_s, by_shard=True, name="mm_ffn_in", tm=SEQ, tn=2 * FF_PAD, tk=D)
    act = _swiglu_fwd(au)
    ff = _matmul(act, w_d, name="mm_ffn_down", tm=SEQ, tn=512, tk=FF_HID // 2)

    dx2, dff, dg_final, dga_f, loss_lanes = _final(x1, ff, tgt, modv, g_final.reshape(1, D))
    dact = _matmul(dff, w_d, tb=True, name="mm_d_act", tm=SEQ // 2, tn=FF_HID // 2, tk=D)
    dau = _swiglu_bwd(au, dact)

    core = pc.astype(jnp.int32).reshape(1)
    chip = (2 * px + py).astype(jnp.int32).reshape(1)

    def to_chips(by_dev, tags, name):
        from_pair = _pair_exchange(by_dev, "pair_exchange_" + name)
        sums = [_pair_add(g, r, core, "pair_add_" + t) for g, r, t in zip(by_dev, from_pair, tags)]
        state, token = _chip_exchange_start(sums, "chip_exchange_start_" + name)
        return sums, state, token

    def from_chips(sums, state, after, tags, name):
        got = _chip_exchange_wait(state, after, "chip_exchange_wait_" + name)
        return [_chip_add(p, r, chip, "chip_add_" + t) for p, r, t in zip(sums, got, tags)]

    g_gu = _matmul(h2, dau, ta=True, by_shard=True, out_dtype=bf16, name="mm_g_ffn_in", tm=D, tn=2 * FF_PAD, tk=SEQ)
    g_d = _matmul(act, dff, ta=True, out_dtype=bf16, name="mm_g_down", tm=FF_HID // 2, tn=512, tk=SEQ)
    ffn_tags = ["gu", "down"]
    ffn_sums, ffn_state, ffn_token = to_chips([g_gu, g_d.reshape(N_DEV, FF_PAD, D)], ffn_tags, "ffn")

    dh2 = _matmul(dau, w_gu_s, tb=True, by_shard=True, name="mm_d_h2", tm=SEQ // 2, tn=D, tk=2 * FF_PAD,
                  after=ffn_token)
    dx1, dmix, dsh_f, dsc_f, dg_ffn, dga_m = _mid_bwd(dh2, x1, dx2, mix, modv, g_ffn)
    dmerged = _matmul(dmix, w_o, tb=True, name="mm_d_merged", tm=SEQ, tn=512, tk=D)
    dya, dyb, dga, dgb = _merge_bwd(dmerged, ya, yb, proj)
    dya_h = _matmul(dya, w_a_s, tb=True, by_shard=True, name="mm_d_ya", tm=SEQ, tn=FOX_W, tk=W_BR_SH)
    dyb_h = _matmul(dyb, w_b_s, tb=True, by_shard=True, name="mm_d_yb", tm=SEQ, tn=DIL_OUT_W, tk=W_BR_SH)

    dqa, dka, dva, dF = _fox_bwd(q_aug, k_aug, va, dya_h, ya_h, lse_a)
    dF_row = jnp.pad(dF[:, :2, :].reshape(N_FOX_HEADS, SEQ), ((0, LANES - N_FOX_HEADS), (0, 0)))
    df, db_fgate = _fox_gate_bwd(dF_row, proj, b_pad)

    delta_b = _dil_delta(dyb_h, yb_h)
    rep = lambda t: _to_classes(jnp.tile(t, (1, N_GROUPS)))
    dq_c, dk_c, dv_c = _dil_bwd(q_c, k_c, v_c, rep(dyb_h), rep(lse_b), rep(delta_b))
    dqb, dkb = _rope_bwd(_from_classes(dq_c), _from_classes(dk_c), tables)
    dvb = _from_classes(dv_c).astype(bf16)

    dproj = _shard_pad_cols({"qa": dqa, "ka": dka, "va": dva, "f": df[:, :N_FOX_HEADS], "qb": dqb, "kb": dkb,
                             "vb": dvb, "ga": dga, "gb": dgb})
    g_in = _matmul(h1, dproj, ta=True, by_shard=True, out_dtype=bf16, name="mm_g_in", tm=D, tn=W_IN_PAD, tk=SEQ)
    g_o = _matmul(merged, dmix, ta=True, out_dtype=bf16, name="mm_g_out", tm=D, tn=512, tk=SEQ)
    g_a = _matmul(ya_h, dya, ta=True, by_shard=True, out_dtype=bf16, name="mm_g_br_a", tm=FOX_W, tn=W_BR_SH, tk=SEQ)
    g_b = _matmul(yb_h, dyb, ta=True, by_shard=True, out_dtype=bf16, name="mm_g_br_b", tm=DIL_OUT_W, tn=W_BR_SH,
                  tk=SEQ)
    rows_a, rows_b = FOX_W * W_BR_SH // D, DIL_OUT_W * W_BR_SH // D
    g_small = jnp.concatenate([g_a.reshape(N_DEV, rows_a, D), g_b.reshape(N_DEV, rows_b, D),
                               g_o.reshape(N_DEV, W_BR_SH, D)], axis=1)
    mix_tags = ["in", "small"]
    mix_sums, mix_state, mix_token = to_chips([g_in, g_small], mix_tags, "mixer")

    dh1 = _matmul(dproj, w_in_s, tb=True, by_shard=True, name="mm_d_h1", tm=SEQ // 2, tn=D, tk=W_IN_PAD,
                  after=mix_token)
    grad_x, dsh_m, dsc_m, dg_mix = _first_bwd(dh1, x2d, dx1, modv, g_mix)

    pad_lane = lambda t: jnp.pad(t, ((0, 0), (0, D - t.shape[1])))
    small = jnp.concatenate([dsh_m, dsc_m, dga_m, dsh_f, dsc_f, dga_f, dg_mix, dg_ffn, dg_final,
                             pad_lane(db_fgate), loss_lanes, jnp.zeros((SMALL_ROWS - 11, D), f32)], axis=0)
    small_all = _all_gather(small, "gather_small")
    small_sum, loss_row = _small_reduce(small_all)
    dmod_all = small_all[:, :6, :].reshape(N_DEV, 6 * D)
    g_w_ada = _ada_bwd(c_all, lax.dynamic_slice(dmod_all, (0, dev * ada_cols), (N_DEV, ada_cols)))

    s_gu, s_d = from_chips(ffn_sums, ffn_state, small_sum, ffn_tags, "ffn")
    s_in, s_small = from_chips(mix_sums, mix_state, g_w_ada, mix_tags, "mixer")
    g_shard = {
        "w_in": s_in[:, :W_IN_SH], "w_ffn_gate": s_gu[:, :W_FF_SH], "w_ffn_up": s_gu[:, FF_PAD:FF_PAD + W_FF_SH],
        "w_ffn_down": s_d[:W_FF_SH], "w_br_a": s_small[:rows_a].reshape(FOX_W, W_BR_SH),
        "w_br_b": s_small[rows_a:rows_a + rows_b].reshape(DIL_OUT_W, W_BR_SH), "w_out": s_small[rows_a + rows_b:],
    }

    loss = loss_row[0, 0]
    g = {
        "w_ada": g_w_ada[None], "b_ada": small_sum[0:6].reshape(1, 6 * D), "g_mix": small_sum[6:7],
        "w_in": g_shard["w_in"][None], "b_fgate": small_sum[9:10, :N_FOX_HEADS], "w_br_a": g_shard["w_br_a"][None],
        "w_br_b": g_shard["w_br_b"][None], "w_out": g_shard["w_out"][None], "g_ffn": small_sum[7:8],
        "w_ffn_gate": g_shard["w_ffn_gate"][None], "w_ffn_up": g_shard["w_ffn_up"][None],
        "w_ffn_down": g_shard["w_ffn_down"][None], "g_final": small_sum[8],
    }
    w = {"w_ada": w_ada, "b_ada": b_ada, "g_mix": g_mix, "w_in": w_in, "b_fgate": b_fgate, "w_br_a": w_br_a,
         "w_br_b": w_br_b, "w_out": w_out, "g_ffn": g_ffn, "w_ffn_gate": w_ffn_gate, "w_ffn_up": w_ffn_up,
         "w_ffn_down": w_ffn_down, "g_final": g_final}
    m = {"w_ada": m_w_ada, "b_ada": m_b_ada, "g_mix": m_g_mix, "w_in": m_w_in, "b_fgate": m_b_fgate,
         "w_br_a": m_w_br_a, "w_br_b": m_w_br_b, "w_out": m_w_out, "g_ffn": m_g_ffn, "w_ffn_gate": m_w_ffn_gate,
         "w_ffn_up": m_w_ffn_up, "w_ffn_down": m_w_ffn_down, "g_final": m_g_final}
    v = {"w_ada": v_w_ada, "b_ada": v_b_ada, "g_mix": v_g_mix, "w_in": v_w_in, "b_fgate": v_b_fgate,
         "w_br_a": v_w_br_a, "w_br_b": v_w_br_b, "w_out": v_w_out, "g_ffn": v_g_ffn, "w_ffn_gate": v_w_ffn_gate,
         "w_ffn_up": v_w_ffn_up, "w_ffn_down": v_w_ffn_down, "g_final": v_g_final}
    names = list(w)
    delta, new_m, new_v = {}, {}, {}
    for n in names:
        shape = w[n].shape
        two_d = (lambda t: t.reshape(shape[-2:])) if len(shape) == 3 else (lambda t: t)
        dl, mn, vn = _adamw(two_d(w[n]), two_d(g[n]), two_d(m[n]), two_d(v[n]), "adamw_" + n)
        delta[n], new_m[n], new_v[n] = dl.reshape(shape), mn.reshape(shape), vn.reshape(shape)

    return (loss, grad_x[None], *[g[n] for n in names], *[delta[n] for n in names],
            *[new_m[n] for n in names], *[new_v[n] for n in names])
```

```python
import functools

import jax
import jax.numpy as jnp
from jax import lax
from jax.experimental import pallas as pl
from jax.experimental.pallas import tpu as pltpu

f32 = jnp.float32
bf16 = jnp.bfloat16
SDS = jax.ShapeDtypeStruct
MESH = pl.DeviceIdType.MESH

N_DEV = 8
D = 1024
SEQ = 2048
HEAD_DIM = 64
N_FOX_HEADS = 8
FOX_W = 512
DIL_W = 768
DIL_OUT_W = 256
ROT_DIM = 16
ROPE_THETA = 500000.0
D_FF = 2816
IN_COLS = 5896
EPS = 1e-6
NEG = -1e30
ATT_SCALE = HEAD_DIM ** -0.5

ADAM_LR = 0.001
ADAM_B1 = 0.9
ADAM_B2 = 0.999
ADAM_EPS = 1e-08
ADAM_WD = 0.01
ADAM_STEP = 10

C_GA, C_GB, C_QB, C_KB, C_VB, C_QA, C_KA, C_VA, C_F = 0, 1024, 2304, 3072, 3840, 4608, 5120, 5632, 6144
PROJ_W = 6272
LANES = 128
VMEM_LIMIT = 52 * 1024 * 1024

W_IN_SH, W_IN_PAD = IN_COLS // N_DEV, 768
W_BR_SH = D // N_DEV
W_FF_SH, FF_PAD = D_FF // N_DEV, 384
FF_HID = N_DEV * FF_PAD
SMALL_ROWS = 16


def _params(sem=None):
    if sem is None:
        return pltpu.CompilerParams(vmem_limit_bytes=VMEM_LIMIT)
    return pltpu.CompilerParams(dimension_semantics=sem, vmem_limit_bytes=VMEM_LIMIT)


def _rowwise(fn, name, tiled, vecs, outs, reds=(), tile=256):
    nt, nv, no = len(tiled), len(vecs), len(outs)
    rows = tiled[0][0].shape[0]
    assert rows % tile == 0

    def body(*refs):
        tin = [r[...] for r in refs[:nt]]
        vin = [r[...] for r in refs[nt:nt + nv]]
        orefs = refs[nt + nv:nt + nv + no]
        rrefs = refs[nt + nv + no:]
        touts, routs = fn(tin, vin)
        for r, t in zip(orefs, touts, strict=True):
            r[...] = t.astype(r.dtype)
        if rrefs:
            @pl.when(pl.program_id(0) == 0)
            def _():
                for r in rrefs:
                    r[...] = jnp.zeros_like(r)
            for r, t in zip(rrefs, routs, strict=True):
                r[...] += t

    def col_map(cb):
        return lambda i: (i, cb)

    def whole_map(nd):
        return lambda i: (0,) * nd

    in_specs = [pl.BlockSpec((tile, w), col_map(cb)) for (_, w, cb) in tiled]
    in_specs += [pl.BlockSpec(v.shape, whole_map(v.ndim)) for v in vecs]
    out_specs = [pl.BlockSpec((tile, w), lambda i: (i, 0)) for (w, _) in outs]
    out_specs += [pl.BlockSpec((1, w), lambda i: (0, 0)) for w in reds]
    out_shape = [SDS((rows, w), dt) for (w, dt) in outs] + [SDS((1, w), f32) for w in reds]
    res = pl.pallas_call(
        body, grid=(rows // tile,), in_specs=in_specs, out_specs=out_specs, out_shape=out_shape, name=name,
        compiler_params=_params(("arbitrary",)),
    )(*[t[0] for t in tiled], *vecs)
    return res


def _matmul(a, b, *, ta=False, tb=False, out_dtype=f32, name, tm, tn, tk, by_shard=False, after=None):
    m, k = (a.shape[1], a.shape[0]) if ta else a.shape
    if by_shard and not ta:
        n, kb = (b.shape[1], N_DEV * b.shape[2]) if tb else (N_DEV * b.shape[2], b.shape[1])
        assert (tk if tb else tn) == b.shape[2]
    else:
        n, kb = (b.shape[0], b.shape[1]) if tb else (b.shape[1], b.shape[0])
    assert kb == k and m % tm == 0 and n % tn == 0 and k % tk == 0
    nk = k // tk
    dims = (((0 if ta else 1,), (1 if tb else 0,)), ((), ()))
    b_stacked = by_shard and not ta
    o_stacked = by_shard and ta

    def body(a_ref, b_ref, *rest):
        o_ref, *acc = rest[1:] if after is not None else rest
        bv = b_ref[0] if b_stacked else b_ref[...]
        p = lax.dot_general(a_ref[...].astype(bf16), bv.astype(bf16), dims, preferred_element_type=f32)

        def put(val):
            if o_stacked:
                o_ref[0] = val.astype(o_ref.dtype)
            else:
                o_ref[...] = val.astype(o_ref.dtype)

        if nk == 1:
            put(p)
        else:
            acc_ref, = acc
            kk = pl.program_id(2)

            @pl.when(kk == 0)
            def _():
                acc_ref[...] = p

            @pl.when(kk > 0)
            def _():
                acc_ref[...] += p

            @pl.when(kk == nk - 1)
            def _():
                put(acc_ref[...])

    a_spec = pl.BlockSpec((tk, tm), lambda i, j, kk: (kk, i)) if ta else pl.BlockSpec((tm, tk), lambda i, j, kk: (i, kk))
    if b_stacked and tb:
        b_spec = pl.BlockSpec((1, tn, tk), lambda i, j, kk: (kk, j, 0))
    elif b_stacked:
        b_spec = pl.BlockSpec((1, tk, tn), lambda i, j, kk: (j, kk, 0))
    elif tb:
        b_spec = pl.BlockSpec((tn, tk), lambda i, j, kk: (j, kk))
    else:
        b_spec = pl.BlockSpec((tk, tn), lambda i, j, kk: (kk, j))
    if o_stacked:
        assert tn == n // N_DEV
        out_spec = pl.BlockSpec((1, tm, tn), lambda i, j, kk: (j, i, 0))
        out_shape = SDS((N_DEV, m, tn), out_dtype)
    else:
        out_spec = pl.BlockSpec((tm, tn), lambda i, j, kk: (i, j))
        out_shape = SDS((m, n), out_dtype)
    extra_specs, extra = ([pl.BlockSpec(memory_space=pl.ANY)], [after]) if after is not None else ([], [])
    return pl.pallas_call(
        body, grid=(m // tm, n // tn, nk), in_specs=[a_spec, b_spec] + extra_specs, out_specs=out_spec,
        out_shape=out_shape, name=name, scratch_shapes=[pltpu.VMEM((tm, tn), f32)] if nk > 1 else [],
        compiler_params=_params(("parallel", "parallel", "arbitrary")),
    )(a, b, *extra)


def _rms(x):
    r = lax.rsqrt(jnp.mean(x * x, axis=-1, keepdims=True) + EPS)
    return r, x * r


def _rms_bwd(r, xn, dxn):
    return r * (dxn - xn * jnp.mean(dxn * xn, axis=-1, keepdims=True))


def _colsum(t):
    return jnp.sum(t, axis=0, keepdims=True)


def _sigmoid(x):
    return 1.0 / (1.0 + jnp.exp(-x))


def _modulated_norm(x, g, shift, scale):
    _, xn = _rms(x)
    return (xn * g) * (1.0 + scale) + shift


def _pre1(x, modv, g_mix):
    def fn(t, v):
        (xt,), (mv, g) = t, v
        return [_modulated_norm(xt, g, mv[0:1], mv[1:2])], []
    return _rowwise(fn, "pre1", [(x, D, 0)], [modv, g_mix], [(D, bf16)])[0]


def _post1(x, mix, modv, g_ffn):
    def fn(t, v):
        (xt, mt), (mv, g) = t, v
        x1 = xt + mv[2:3] * mt
        return [x1, _modulated_norm(x1, g, mv[3:4], mv[4:5])], []
    return _rowwise(fn, "post1", [(x, D, 0), (mix, D, 0)], [modv, g_ffn], [(D, f32), (D, bf16)])


def _gate_up(au, j):
    base = 2 * j * FF_PAD
    return au[:, base:base + FF_PAD], au[:, base + FF_PAD:base + 2 * FF_PAD]


def _swiglu_fwd(au):
    def fn(t, v):
        acts = []
        for j in range(N_DEV):
            a, u = _gate_up(t[0], j)
            acts.append(a * _sigmoid(a) * u)
        return [jnp.concatenate(acts, axis=1)], []
    return _rowwise(fn, "swiglu_fwd", [(au, 2 * FF_HID, 0)], [], [(FF_HID, bf16)])[0]


def _swiglu_bwd(au, dact):
    def fn(t, v):
        parts = []
        for j in range(N_DEV):
            a, u = _gate_up(t[0], j)
            d = t[1][:, j * FF_PAD:(j + 1) * FF_PAD]
            sg = _sigmoid(a)
            parts += [d * u * (sg * (1.0 + a * (1.0 - sg))), d * (a * sg)]
        return [jnp.concatenate(parts, axis=1)], []
    return _rowwise(fn, "swiglu_bwd", [(au, 2 * FF_HID, 0), (dact, FF_HID, 0)], [], [(2 * FF_HID, bf16)],
                    tile=128)[0]


def _final(x1, ff, target, modv, g_final):
    def fn(t, v):
        (x1t, fft, tgt), (mv, g) = t, v
        x2 = x1t + mv[5:6] * fft
        r, xn = _rms(x2)
        err = xn * g - tgt
        dy = err * (1.0 / D)
        dx2 = _rms_bwd(r, xn, dy * g)
        return [dx2, dx2 * mv[5:6]], [_colsum(dy * xn), _colsum(dx2 * fft), _colsum(err * err) * (0.5 / D)]
    return _rowwise(fn, "final", [(x1, D, 0), (ff, D, 0), (target, D, 0)], [modv, g_final],
                    [(D, f32), (D, bf16)], [D, D, D])


def _mid_bwd(dh2, x1, dx2, mix, modv, g_ffn):
    def fn(t, v):
        (dh, x1t, dx2t, mt), (mv, g) = t, v
        r, xn = _rms(x1t)
        dn = dh * (1.0 + mv[4:5])
        dx1 = dx2t + _rms_bwd(r, xn, dn * g)
        return [dx1, dx1 * mv[2:3]], [_colsum(dh), _colsum(dh * (xn * g)), _colsum(dn * xn), _colsum(dx1 * mt)]
    return _rowwise(fn, "mid_bwd", [(dh2, D, 0), (x1, D, 0), (dx2, D, 0), (mix, D, 0)], [modv, g_ffn],
                    [(D, f32), (D, bf16)], [D, D, D, D])


def _first_bwd(dh1, x, dx1, modv, g_mix):
    def fn(t, v):
        (dh, xt, dx1t), (mv, g) = t, v
        r, xn = _rms(xt)
        dn = dh * (1.0 + mv[1:2])
        return [dx1t + _rms_bwd(r, xn, dn * g)], [_colsum(dh), _colsum(dh * (xn * g)), _colsum(dn * xn)]
    return _rowwise(fn, "first_bwd", [(dh1, D, 0), (x, D, 0), (dx1, D, 0)], [modv, g_mix], [(D, f32)], [D, D, D])


def _merge_fwd(ya, yb, proj):
    def fn(t, v):
        ya_t, yb_t, ga, gb = t
        return [_sigmoid(ga) * ya_t + _sigmoid(gb) * yb_t], []
    return _rowwise(fn, "merge_fwd", [(ya, D, 0), (yb, D, 0), (proj, D, C_GA // D), (proj, D, C_GB // D)], [],
                    [(D, bf16)])[0]


def _merge_bwd(dmerged, ya, yb, proj):
    def fn(t, v):
        dm, ya_t, yb_t, ga, gb = t
        sa, sb = _sigmoid(ga), _sigmoid(gb)
        return [dm * sa, dm * sb, dm * ya_t * (sa * (1.0 - sa)), dm * yb_t * (sb * (1.0 - sb))], []
    return _rowwise(fn, "merge_bwd",
                    [(dmerged, D, 0), (ya, D, 0), (yb, D, 0), (proj, D, C_GA // D), (proj, D, C_GB // D)], [],
                    [(D, bf16), (D, bf16), (D, bf16), (D, bf16)])


def _rope_tables():
    half = ROT_DIM // 2
    pos = jnp.arange(SEQ, dtype=f32)
    inv_freq = ROPE_THETA ** (-jnp.arange(0, ROT_DIM, 2, dtype=f32) / ROT_DIM)
    ang = pos[:, None] * inv_freq[None, :]
    cos, sin = jnp.cos(ang), jnp.sin(ang)
    pad = jnp.zeros((SEQ, HEAD_DIM - ROT_DIM), f32)
    zero = jnp.zeros((SEQ, half), f32)
    c_head = jnp.concatenate([cos, cos, pad + 1.0], axis=1)
    lo_head = jnp.concatenate([-sin, zero, pad], axis=1)
    hi_head = jnp.concatenate([zero, sin, pad], axis=1)
    return tuple(jnp.concatenate([t, t], axis=1) for t in (c_head, lo_head, hi_head))


def _over_heads(tables):
    return [jnp.tile(t, (1, DIL_W // LANES)) for t in tables]


def _rope_fwd(proj, tables):
    half = ROT_DIM // 2

    def fn(t, v):
        q, k, vv = t[:3]
        c, lo, hi = _over_heads(t[3:])
        rot = lambda z: z * c + pltpu.roll(z, DIL_W - half, 1) * lo + pltpu.roll(z, half, 1) * hi
        return [rot(q) * ATT_SCALE, rot(k), vv], []
    return _rowwise(fn, "rope_fwd", [(proj, DIL_W, C_QB // DIL_W), (proj, DIL_W, C_KB // DIL_W),
                                     (proj, DIL_W, C_VB // DIL_W)] + [(tb, LANES, 0) for tb in tables], [],
                    [(DIL_W, bf16)] * 3)


def _rope_bwd(dq, dk, tables):
    half = ROT_DIM // 2

    def fn(t, v):
        dq_t, dk_t = t[:2]
        c, lo, hi = _over_heads(t[2:])
        rot_t = lambda z: z * c + pltpu.roll(z * lo, half, 1) + pltpu.roll(z * hi, DIL_W - half, 1)
        return [rot_t(dq_t), rot_t(dk_t)], []
    return _rowwise(fn, "rope_bwd", [(dq, DIL_W, 0), (dk, DIL_W, 0)] + [(tb, LANES, 0) for tb in tables], [],
                    [(DIL_W, bf16), (DIL_W, bf16)])


def _head_bcast_sum(d):
    lane = lax.broadcasted_iota(jnp.int32, d.shape, 1)
    out = jnp.zeros_like(d)
    for h in range(d.shape[1] // HEAD_DIM):
        sel = (lane >= h * HEAD_DIM) & (lane < (h + 1) * HEAD_DIM)
        out = jnp.where(sel, jnp.sum(jnp.where(sel, d, 0.0), axis=1, keepdims=True), out)
    return out


def _dil_combine(o, lse):
    def fn(t, v):
        o0, o1, o2, l0, l1, l2 = t
        m = jnp.maximum(jnp.maximum(l0, l1), l2)
        w0, w1, w2 = jnp.exp(l0 - m), jnp.exp(l1 - m), jnp.exp(l2 - m)
        tot = w0 + w1 + w2
        return [(w0 * o0 + w1 * o1 + w2 * o2) / tot, m + jnp.log(tot)], []
    w = DIL_OUT_W
    return _rowwise(fn, "dil_combine", [(o, w, 0), (o, w, 1), (o, w, 2), (lse, w, 0), (lse, w, 1), (lse, w, 2)], [],
                    [(w, f32), (w, f32)])


def _dil_delta(dyb_h, yb_h):
    def fn(t, v):
        return [_head_bcast_sum(t[0] * t[1])], []
    return _rowwise(fn, "dil_delta", [(dyb_h, DIL_OUT_W, 0), (yb_h, DIL_OUT_W, 0)], [], [(DIL_OUT_W, f32)])[0]


def _adamw(w, g, m, v, name):
    shape = w.shape
    if w.ndim == 1:
        w, g, m, v = (t.reshape(1, -1) for t in (w, g, m, v))
    rows, cols = w.shape
    tile = 256 if rows % 256 == 0 and rows > 512 else rows

    def fn(t, _):
        wt, gt, mt, vt = t
        mn = ADAM_B1 * mt + (1.0 - ADAM_B1) * gt
        vn = ADAM_B2 * vt + (1.0 - ADAM_B2) * (gt * gt)
        m_hat = mn / (1.0 - ADAM_B1 ** ADAM_STEP)
        v_hat = vn / (1.0 - ADAM_B2 ** ADAM_STEP)
        return [-ADAM_LR * (m_hat / (jnp.sqrt(v_hat) + ADAM_EPS) + ADAM_WD * wt), mn, vn], []
    delta, mn, vn = _rowwise(fn, name, [(w, cols, 0), (g, cols, 0), (m, cols, 0), (v, cols, 0)], [],
                             [(cols, f32)] * 3, tile=tile)
    return delta.reshape(shape), mn.reshape(shape), vn.reshape(shape)


def _ada_fwd(c_all, w_shard, b_shard):
    def body(c_ref, w_ref, b_ref, o_ref):
        cv = c_ref[...]
        sc = (cv * _sigmoid(cv)).astype(bf16)
        o_ref[...] = jnp.dot(sc, w_ref[...].astype(bf16), preferred_element_type=f32) + b_ref[...]
    return pl.pallas_call(body, out_shape=SDS((N_DEV, w_shard.shape[1]), f32), name="ada_fwd",
                          compiler_params=_params())(c_all, w_shard, b_shard)


def _ada_bwd(c_all, dmod_cols):
    def body(c_ref, d_ref, o_ref):
        cv = c_ref[...]
        sc = cv * _sigmoid(cv)
        o_ref[...] = lax.dot_general(sc, d_ref[...], (((0,), (0,)), ((), ())), precision=lax.Precision.HIGHEST,
                                     preferred_element_type=f32)
    return pl.pallas_call(body, out_shape=SDS((D, dmod_cols.shape[1]), f32), name="ada_bwd",
                          compiler_params=_params())(c_all, dmod_cols)


def _small_reduce(gathered):
    def body(g_ref, o_ref, loss_ref):
        acc = g_ref[0]
        for d in range(1, N_DEV):
            acc = acc + g_ref[d]
        o_ref[...] = acc
        loss_ref[...] = jnp.zeros((1, LANES), f32) + jnp.sum(acc[10:11, :])
    return pl.pallas_call(body, out_shape=(SDS((SMALL_ROWS, D), f32), SDS((1, LANES), f32)), name="small_reduce",
                          compiler_params=_params())(gathered)


FOX_BLK = 512
CUM_BLK = 128


def _fold_lanes(t, op):
    out = t[:, :LANES]
    for j in range(1, t.shape[1] // LANES):
        out = op(out, t[:, j * LANES:(j + 1) * LANES])
    return out


def _fox_gate_fwd(proj, b_pad):
    nblk = SEQ // CUM_BLK

    def body(f_ref, b_ref, col_ref):
        r = lax.broadcasted_iota(jnp.int32, (CUM_BLK, CUM_BLK), 0)
        c = lax.broadcasted_iota(jnp.int32, (CUM_BLK, CUM_BLK), 1)
        tri = (r >= c).astype(f32)
        carry = jnp.zeros((1, LANES), f32)
        for blk in range(nblk):
            z = f_ref[blk * CUM_BLK:(blk + 1) * CUM_BLK, :] + b_ref[...]
            logf = jnp.minimum(z, 0.0) - jnp.log1p(jnp.exp(-jnp.abs(z)))
            cs = jnp.dot(tri, logf, precision=lax.Precision.HIGHEST, preferred_element_type=f32) + carry
            col_ref[blk * CUM_BLK:(blk + 1) * CUM_BLK, :] = cs
            carry = cs[CUM_BLK - 1:CUM_BLK, :]

    return pl.pallas_call(
        body, grid=(1,), in_specs=[pl.BlockSpec((SEQ, LANES), lambda i: (0, C_F // LANES)),
                                   pl.BlockSpec((1, LANES), lambda i: (0, 0))],
        out_specs=pl.BlockSpec((SEQ, LANES), lambda i: (0, 0)),
        out_shape=SDS((SEQ, LANES), f32), name="fox_gate_fwd",
        compiler_params=_params(("arbitrary",)),
    )(proj, b_pad)


def _fox_gate_bwd(dF_row, proj, b_pad):
    nblk = SEQ // CUM_BLK

    def body(d_ref, f_ref, b_ref, df_ref, db_ref, col_ref):
        r = lax.broadcasted_iota(jnp.int32, (CUM_BLK, CUM_BLK), 0)
        c = lax.broadcasted_iota(jnp.int32, (CUM_BLK, CUM_BLK), 1)
        tri = (r <= c).astype(f32)
        lane = lax.broadcasted_iota(jnp.int32, (CUM_BLK, LANES), 1)
        col_ref[...] = d_ref[...].T
        carry = jnp.zeros((1, LANES), f32)
        total = jnp.zeros((1, LANES), f32)
        for blk in reversed(range(nblk)):
            rows = slice(blk * CUM_BLK, (blk + 1) * CUM_BLK)
            cs = jnp.dot(tri, col_ref[rows, :], precision=lax.Precision.HIGHEST, preferred_element_type=f32) + carry
            carry = cs[0:1, :]
            z = f_ref[rows, :] + b_ref[...]
            df = jnp.where(lane < N_FOX_HEADS, cs * _sigmoid(-z), 0.0)
            df_ref[rows, :] = df.astype(df_ref.dtype)
            total = total + _colsum(df)
        db_ref[...] = total

    return pl.pallas_call(
        body, grid=(1,), in_specs=[pl.BlockSpec((LANES, SEQ), lambda i: (0, 0)),
                                   pl.BlockSpec((SEQ, LANES), lambda i: (0, C_F // LANES)),
                                   pl.BlockSpec((1, LANES), lambda i: (0, 0))],
        out_specs=[pl.BlockSpec((SEQ, LANES), lambda i: (0, 0)), pl.BlockSpec((1, LANES), lambda i: (0, 0))],
        out_shape=(SDS((SEQ, LANES), bf16), SDS((1, LANES), f32)), name="fox_gate_bwd",
        scratch_shapes=[pltpu.VMEM((SEQ, LANES), f32)],
        compiler_params=_params(("arbitrary",)),
    )(dF_row, proj, b_pad)


def _nt(a, b):
    return lax.dot_general(a, b, (((1,), (1,)), ((), ())), preferred_element_type=f32)


def _tn(a, b):
    return lax.dot_general(a, b, (((0,), (0,)), ((), ())), preferred_element_type=f32)


def _fox_prep(proj, f_col):
    def fn(t, v):
        q, k, vv, fc = t
        lane = lax.broadcasted_iota(jnp.int32, (q.shape[0], LANES), 1)
        qs, ks = [], []
        for h in range(N_FOX_HEADS):
            pair, pos = divmod(h, 2)
            own = (lane >= pos * HEAD_DIM) & (lane < (pos + 1) * HEAD_DIM)
            base = (1 - pos) * HEAD_DIM
            f = fc[:, h:h + 1]
            hi = f.astype(bf16).astype(f32)
            mid = (f - hi).astype(bf16).astype(f32)
            lo = (f - hi) - mid
            one = jnp.ones_like(f)
            qa = jnp.where(own, q[:, pair * LANES:(pair + 1) * LANES] * ATT_SCALE, 0.0)
            ka = k[:, pair * LANES:(pair + 1) * LANES]
            for idx, (qv, kv) in enumerate([(hi, one), (mid, one), (lo, one), (one, -hi), (one, -mid), (one, -lo)]):
                sel = lane == base + idx
                qa = jnp.where(sel, qv, qa)
                ka = jnp.where(sel, kv, ka)
            qs.append(qa)
            ks.append(ka)
        return [jnp.concatenate(qs, axis=1), jnp.concatenate(ks, axis=1), vv], []
    w = N_FOX_HEADS * LANES
    return _rowwise(fn, "fox_prep", [(proj, FOX_W, C_QA // FOX_W), (proj, FOX_W, C_KA // FOX_W),
                                     (proj, FOX_W, C_VA // FOX_W), (f_col, LANES, 0)], [],
                    [(w, bf16), (w, bf16), (FOX_W, bf16)])


def _fox_fwd(q_aug, k_aug, v):
    blk = FOX_BLK
    npair = FOX_W // LANES

    def body(q_ref, k_ref, v_ref, o_ref, lse_ref, s_scr):
        i = pl.program_id(1)
        tri = lax.broadcasted_iota(jnp.int32, (blk, blk), 0) >= lax.broadcasted_iota(jnp.int32, (blk, blk), 1)
        qh = [q_ref[:, h * LANES:(h + 1) * LANES] for h in range(2)]

        def logits(c, masked):
            off = pl.multiple_of(c * blk, blk)
            tops = []
            for h in range(2):
                s = _nt(qh[h], k_ref[pl.ds(off, blk), h * LANES:(h + 1) * LANES])
                if masked:
                    s = jnp.where(tri, s, NEG)
                s_scr[h, :, pl.ds(off, blk)] = s
                tops.append(_fold_lanes(s, jnp.maximum))
            return tops

        def pass_a(c, m):
            return tuple(jnp.maximum(a, b) for a, b in zip(m, logits(c, False)))

        m = lax.fori_loop(0, i, pass_a, tuple(jnp.full((blk, LANES), NEG, f32) for _ in range(2)))
        mx = [jnp.max(jnp.maximum(a, b), axis=1, keepdims=True) for a, b in zip(m, logits(i, True))]

        def pass_b(c, carry):
            off = pl.multiple_of(c * blk, blk)
            vv = v_ref[pl.ds(off, blk), :]
            new = []
            for h in range(2):
                l, acc = carry[h]
                p = jnp.exp(s_scr[h, :, pl.ds(off, blk)] - mx[h])
                new.append((l + _fold_lanes(p, jnp.add),
                            acc + jnp.dot(p.astype(bf16), vv, preferred_element_type=f32)))
            return tuple(new)

        zero = jnp.zeros((blk, LANES), f32)
        (l_a, acc_a), (l_b, acc_b) = lax.fori_loop(0, i + 1, pass_b, ((zero, zero), (zero, zero)))
        l_a = jnp.sum(l_a, axis=1, keepdims=True)
        l_b = jnp.sum(l_b, axis=1, keepdims=True)
        first = lax.broadcasted_iota(jnp.int32, (blk, LANES), 1) < HEAD_DIM
        o_ref[...] = jnp.where(first, acc_a / l_a, acc_b / l_b)
        lse_ref[0] = jnp.where(first, mx[0] + jnp.log(l_a), mx[1] + jnp.log(l_b))

    return pl.pallas_call(
        body, grid=(npair, SEQ // blk),
        in_specs=[pl.BlockSpec((blk, 2 * LANES), lambda p, i: (i, p)),
                  pl.BlockSpec((SEQ, 2 * LANES), lambda p, i: (0, p)),
                  pl.BlockSpec((SEQ, LANES), lambda p, i: (0, p))],
        out_specs=[pl.BlockSpec((blk, LANES), lambda p, i: (i, p)),
                   pl.BlockSpec((1, blk, LANES), lambda p, i: (p, i, 0))],
        out_shape=(SDS((SEQ, FOX_W), f32), SDS((npair, SEQ, LANES), f32)), name="fox_fwd",
        scratch_shapes=[pltpu.VMEM((2, blk, SEQ), f32)],
        compiler_params=_params(("parallel", "arbitrary")),
    )(q_aug, k_aug, v)


def _fox_bwd(q_aug, k_aug, v, do, o, lse):
    blk = FOX_BLK
    npair = FOX_W // LANES
    nblk = SEQ // blk

    def body(q_ref, k_ref, v_ref, do_ref, o_ref, lse_ref, dq_ref, dk_ref, dv_ref, df_ref,
             dq_acc, delta_ref, res_ref):
        lane_s = lax.broadcasted_iota(jnp.int32, (SEQ, LANES), 1)
        prod = do_ref[...] * o_ref[...]
        d_a = jnp.sum(jnp.where(lane_s < HEAD_DIM, prod, 0.0), axis=1, keepdims=True)
        d_b = jnp.sum(jnp.where(lane_s >= HEAD_DIM, prod, 0.0), axis=1, keepdims=True)
        delta_ref[...] = jnp.where(lane_s < HEAD_DIM, d_a, d_b)
        dq_acc[...] = jnp.zeros_like(dq_acc)
        res_ref[...] = jnp.zeros_like(res_ref)
        df_ref[...] = jnp.zeros_like(df_ref)
        lane = lax.broadcasted_iota(jnp.int32, (blk, LANES), 1)
        own = [lane < HEAD_DIM, lane >= HEAD_DIM]
        tri = lax.broadcasted_iota(jnp.int32, (blk, blk), 0) >= lax.broadcasted_iota(jnp.int32, (blk, blk), 1)

        def q_slab(qoff, h):
            return q_ref[pl.ds(qoff, blk), h * LANES:(h + 1) * LANES]

        def probs(qoff, h, k_h, masked):
            s = _nt(q_slab(qoff, h), k_h)
            if masked:
                s = jnp.where(tri, s, NEG)
            return jnp.exp(s - lse_ref[0, pl.ds(qoff, blk), h * HEAD_DIM:h * HEAD_DIM + 1])

        def k_slabs(koff):
            return [k_ref[pl.ds(koff, blk), h * LANES:(h + 1) * LANES] for h in range(2)]

        def kv_step(kj, _):
            koff = pl.multiple_of(kj * blk, blk)
            k_aug = k_slabs(koff)
            k_own = [jnp.where(own[h], k_aug[h], jnp.zeros_like(k_aug[h])) for h in range(2)]
            vv = v_ref[pl.ds(koff, blk), :]
            v_own = [jnp.where(own[h], vv, jnp.zeros_like(vv)) for h in range(2)]

            def q_tile(qi, carry, masked):
                qoff = pl.multiple_of(qi * blk, blk)
                dd = do_ref[pl.ds(qoff, blk), :].astype(bf16)
                new, dq_add = [], None
                for h in range(2):
                    dk_h, dv_h, dcol = carry[h]
                    p = probs(qoff, h, k_aug[h], masked)
                    dl = p * (_nt(dd, v_own[h]) - delta_ref[pl.ds(qoff, blk), h * HEAD_DIM:h * HEAD_DIM + 1])
                    dlb = dl.astype(bf16)
                    part = jnp.dot(dlb, k_own[h], preferred_element_type=f32)
                    dq_add = part if dq_add is None else dq_add + part
                    res_ref[h, pl.ds(qoff, blk), :] += _fold_lanes(dl, jnp.add)
                    new.append((dk_h + _tn(dlb, q_slab(qoff, h)), dv_h + _tn(p.astype(bf16), dd),
                                dcol + _colsum(dl)))
                dq_acc[pl.ds(qoff, blk), :] += dq_add * ATT_SCALE
                return tuple(new)

            zero = (jnp.zeros((blk, LANES), f32), jnp.zeros((blk, LANES), f32), jnp.zeros((1, blk), f32))
            carry = q_tile(kj, (zero, zero), True)
            (dk_a, dv_a, dcol_a), (dk_b, dv_b, dcol_b) = lax.fori_loop(
                kj + 1, nblk, lambda qi, cr: q_tile(qi, cr, False), carry)
            dk_ref[pl.ds(koff, blk), :] = jnp.where(own[0], dk_a, dk_b).astype(dk_ref.dtype)
            dv_ref[pl.ds(koff, blk), :] = jnp.where(own[0], dv_a, dv_b).astype(dv_ref.dtype)
            df_ref[0, 0:1, pl.ds(koff, blk)] = -dcol_a
            df_ref[0, 1:2, pl.ds(koff, blk)] = -dcol_b
            return 0

        lax.fori_loop(0, nblk, kv_step, 0)
        dq_ref[...] = dq_acc[...].astype(dq_ref.dtype)

        for h in range(2):
            res_ref[h] = jnp.zeros((SEQ, LANES), f32) + jnp.sum(res_ref[h], axis=1, keepdims=True)

        def kv_fix(kj, _):
            koff = pl.multiple_of(kj * blk, blk)
            k_aug = k_slabs(koff)

            def q_fix(qi, corr, masked):
                qoff = pl.multiple_of(qi * blk, blk)
                return tuple(corr[h] + _colsum(probs(qoff, h, k_aug[h], masked) * res_ref[h, pl.ds(qoff, blk), 0:1])
                             for h in range(2))

            zero = jnp.zeros((1, blk), f32)
            corr = lax.fori_loop(kj + 1, nblk, lambda qi, cr: q_fix(qi, cr, False), q_fix(kj, (zero, zero), True))
            df_ref[0, 0:1, pl.ds(koff, blk)] += corr[0]
            df_ref[0, 1:2, pl.ds(koff, blk)] += corr[1]
            return 0

        lax.fori_loop(0, nblk, kv_fix, 0)

    pair_aug = pl.BlockSpec((SEQ, 2 * LANES), lambda p: (0, p))
    slab = pl.BlockSpec((SEQ, LANES), lambda p: (0, p))
    per_pair = pl.BlockSpec((1, SEQ, LANES), lambda p: (p, 0, 0))
    rows = pl.BlockSpec((1, 8, SEQ), lambda p: (p, 0, 0))
    return pl.pallas_call(
        body, grid=(npair,),
        in_specs=[pair_aug, pair_aug, slab, slab, slab, per_pair],
        out_specs=[slab, slab, slab, rows],
        out_shape=(SDS((SEQ, FOX_W), bf16),) * 3 + (SDS((npair, 8, SEQ), f32),), name="fox_bwd",
        scratch_shapes=[pltpu.VMEM((SEQ, LANES), f32), pltpu.VMEM((SEQ, LANES), f32),
                        pltpu.VMEM((2, SEQ, LANES), f32)],
        compiler_params=_params(("parallel",)),
    )(q_aug, k_aug, v, do, o, lse)


DIL_BLK = 128
N_GROUPS = 3
DIL_PAIRS = DIL_OUT_W // LANES
DIL_NBLK = SEQ // DIL_BLK


def _blocks_per_seq(g):
    return jnp.where(g == 0, 16, jnp.where(g == 1, 4, 1))


def _dil_block(n, g):
    rows = slice(n * DIL_BLK, (n + 1) * DIL_BLK)
    if n == 0:
        r = lax.broadcasted_iota(jnp.int32, (DIL_BLK, DIL_BLK), 0)
        c = lax.broadcasted_iota(jnp.int32, (DIL_BLK, DIL_BLK), 1)
        return rows, rows, r >= c
    r = lax.broadcasted_iota(jnp.int32, (DIL_BLK, 2 * DIL_BLK), 0)
    c = lax.broadcasted_iota(jnp.int32, (DIL_BLK, 2 * DIL_BLK), 1)
    has_prev = (n & (_blocks_per_seq(g) - 1)) > 0
    mask = ((c < DIL_BLK) & (c >= r) & has_prev) | ((c >= DIL_BLK) & (c - DIL_BLK <= r))
    return rows, slice((n - 1) * DIL_BLK, (n + 1) * DIL_BLK), mask


def _dil_spec():
    return pl.BlockSpec((1, SEQ, LANES), lambda g, p: (g, 0, p))


def _dil_fwd(q, k, v):
    def body(q_ref, k_ref, v_ref, o_ref, lse_ref):
        g = pl.program_id(0)
        first = lax.broadcasted_iota(jnp.int32, (DIL_BLK, LANES), 1) < HEAD_DIM
        for n in range(DIL_NBLK):
            rows, krows, mask = _dil_block(n, g)
            qv, kk, vv = q_ref[0, rows, :], k_ref[0, krows, :], v_ref[0, krows, :]
            outs, lses = [], []
            for own in (first, ~first):
                s = jnp.where(mask, _nt(jnp.where(own, qv, jnp.zeros_like(qv)), kk), NEG)
                m = jnp.max(s, axis=1, keepdims=True)
                p = jnp.exp(s - m)
                l = jnp.sum(p, axis=1, keepdims=True)
                outs.append(jnp.dot(p.astype(bf16), vv, preferred_element_type=f32) / l)
                lses.append(m + jnp.log(l))
            o_ref[0, rows, :] = jnp.where(first, outs[0], outs[1])
            lse_ref[0, rows, :] = jnp.where(first, lses[0], lses[1])

    spec = _dil_spec()
    shape = SDS((N_GROUPS, SEQ, DIL_OUT_W), f32)
    return pl.pallas_call(
        body, grid=(N_GROUPS, DIL_PAIRS), in_specs=[spec] * 3, out_specs=[spec] * 2,
        out_shape=(shape, shape), name="dil_fwd", compiler_params=_params(("parallel", "parallel")),
    )(q, k, v)


def _dil_bwd(q, k, v, do, lse, delta):
    def body(q_ref, k_ref, v_ref, do_ref, lse_ref, dl_ref, dq_ref, dk_ref, dv_ref):
        g = pl.program_id(0)
        first = lax.broadcasted_iota(jnp.int32, (DIL_BLK, LANES), 1) < HEAD_DIM
        dk_ref[...] = jnp.zeros_like(dk_ref)
        dv_ref[...] = jnp.zeros_like(dv_ref)
        for n in range(DIL_NBLK):
            rows, krows, mask = _dil_block(n, g)
            qv, kk, vv = q_ref[0, rows, :], k_ref[0, krows, :], v_ref[0, krows, :]
            dov = do_ref[0, rows, :].astype(bf16)
            lsev, delv = lse_ref[0, rows, :], dl_ref[0, rows, :]
            dqs, dk_add, dv_add = [], None, None
            for h, own in enumerate((first, ~first)):
                col = h * HEAD_DIM
                qh = jnp.where(own, qv, jnp.zeros_like(qv))
                doh = jnp.where(own, dov, jnp.zeros_like(dov))
                p = jnp.exp(jnp.where(mask, _nt(qh, kk), NEG) - lsev[:, col:col + 1])
                dl = (p * (_nt(doh, vv) - delv[:, col:col + 1])).astype(bf16)
                dqs.append(jnp.dot(dl, kk, preferred_element_type=f32))
                dk_h, dv_h = _tn(dl, qh), _tn(p.astype(bf16), doh)
                dk_add = dk_h if dk_add is None else dk_add + dk_h
                dv_add = dv_h if dv_add is None else dv_add + dv_h
            dq_ref[0, rows, :] = jnp.where(first, dqs[0], dqs[1]) * ATT_SCALE
            dk_ref[0, krows, :] += dk_add
            dv_ref[0, krows, :] += dv_add

    spec = _dil_spec()
    shape = SDS((N_GROUPS, SEQ, DIL_OUT_W), f32)
    return pl.pallas_call(
        body, grid=(N_GROUPS, DIL_PAIRS), in_specs=[spec] * 6, out_specs=[spec] * 3,
        out_shape=(shape, shape, shape), name="dil_bwd", compiler_params=_params(("parallel", "parallel")),
    )(q, k, v, do, lse, delta)


_DILATIONS = (1, 4, 16)


def _to_classes(t):
    w = t.shape[1] // N_GROUPS
    out = []
    for g, d in enumerate(_DILATIONS):
        s = t[:, g * w:(g + 1) * w]
        out.append(s.reshape(SEQ // d, d, w).transpose(1, 0, 2).reshape(SEQ, w))
    return jnp.stack(out)


def _from_classes(t):
    w = t.shape[2]
    out = [t[g].reshape(d, SEQ // d, w).transpose(1, 0, 2).reshape(SEQ, w) for g, d in enumerate(_DILATIONS)]
    return jnp.concatenate(out, axis=1)


def _position():
    return lax.axis_index("x"), lax.axis_index("y"), lax.axis_index("c")


def _all_gather(block, name):
    def body(x_ref, out_ref, send_sems, recv_sems, local_sem):
        x, y, c = _position()
        me, sibling = (x, y, c), (x, y, 1 - c)
        chips = [(1 - x, y), (x, 1 - y), (1 - x, 1 - y)]

        def slot(px, py, pc):
            return out_ref.at[4 * px + 2 * py + pc]

        def copy(k, blk, to, src=None):
            return pltpu.make_async_remote_copy(
                src_ref=slot(*blk) if src is None else src, dst_ref=slot(*blk),
                send_sem=send_sems.at[k], recv_sem=recv_sems.at[k], device_id=to, device_id_type=MESH)

        mine = pltpu.make_async_copy(x_ref, slot(*me), local_sem)
        mine.start()
        first = [copy(0, me, sibling, src=x_ref)]
        first += [copy(1 + j, me, (*chip, c), src=x_ref) for j, chip in enumerate(chips)]
        for cp in first:
            cp.start()
        passed = [copy(4 + j, (*chip, c), sibling) for j, chip in enumerate(chips)]
        for j, chip in enumerate(chips):
            copy(1 + j, (*chip, c), me).wait_recv()
            passed[j].start()
        copy(0, sibling, me).wait_recv()
        for j, chip in enumerate(chips):
            copy(4 + j, (*chip, 1 - c), me).wait_recv()
        for cp in first + passed:
            cp.wait_send()
        mine.wait()

    return pl.pallas_call(
        body, out_shape=SDS((N_DEV,) + block.shape, block.dtype),
        in_specs=[pl.BlockSpec(memory_space=pl.ANY)], out_specs=pl.BlockSpec(memory_space=pl.ANY),
        scratch_shapes=[pltpu.SemaphoreType.DMA((7,)), pltpu.SemaphoreType.DMA((7,)), pltpu.SemaphoreType.DMA],
        name=name,
    )(block)


def _all_gather_many(blocks, name):
    n = len(blocks)

    def body(*refs):
        x_refs, out_refs = refs[:n], refs[n:2 * n]
        send_sems, recv_sems, local_sems = refs[2 * n:]
        x, y, c = _position()
        me, sibling = (x, y, c), (x, y, 1 - c)
        chips = [(1 - x, y), (x, 1 - y), (1 - x, 1 - y)]

        def slot(a, px, py, pc):
            return out_refs[a].at[4 * px + 2 * py + pc]

        def copy(a, k, blk, to, own=False):
            return pltpu.make_async_remote_copy(
                src_ref=x_refs[a] if own else slot(a, *blk), dst_ref=slot(a, *blk),
                send_sem=send_sems.at[a, k], recv_sem=recv_sems.at[a, k], device_id=to, device_id_type=MESH)

        mine = [pltpu.make_async_copy(x_refs[a], slot(a, *me), local_sems.at[a]) for a in range(n)]
        for cp in mine:
            cp.start()
        started = []
        for a in range(n):
            first = [copy(a, 0, me, sibling, own=True)]
            first += [copy(a, 1 + j, me, (*chip, c), own=True) for j, chip in enumerate(chips)]
            for cp in first:
                cp.start()
            started += first
        for a in range(n):
            for j, chip in enumerate(chips):
                copy(a, 1 + j, (*chip, c), me).wait_recv()
                passed = copy(a, 4 + j, (*chip, c), sibling)
                passed.start()
                started.append(passed)
        for a in range(n):
            copy(a, 0, sibling, me).wait_recv()
            for j, chip in enumerate(chips):
                copy(a, 4 + j, (*chip, 1 - c), me).wait_recv()
        for cp in started:
            cp.wait_send()
        for cp in mine:
            cp.wait()

    hbm = pl.BlockSpec(memory_space=pl.ANY)
    return pl.pallas_call(
        body, out_shape=[SDS((N_DEV,) + b.shape, b.dtype) for b in blocks],
        in_specs=[hbm] * n, out_specs=[hbm] * n,
        scratch_shapes=[pltpu.SemaphoreType.DMA((n, 7)), pltpu.SemaphoreType.DMA((n, 7)),
                        pltpu.SemaphoreType.DMA((n,))],
        name=name,
    )(*blocks)


def _pair_exchange(gs, name):
    n = len(gs)

    def body(*refs):
        g_refs, r_refs, send_sems, recv_sems = refs[:n], refs[n:2 * n], refs[2 * n], refs[2 * n + 1]
        x, y, c = _position()
        copies = []
        for a in range(n):
            for k in range(4):
                cp = pltpu.make_async_remote_copy(
                    src_ref=g_refs[a].at[2 * k + (1 - c)], dst_ref=r_refs[a].at[k], send_sem=send_sems.at[a, k],
                    recv_sem=recv_sems.at[a, k], device_id=(x, y, 1 - c), device_id_type=MESH)
                cp.start()
                copies.append(cp)
        for cp in copies:
            cp.wait()

    hbm = pl.BlockSpec(memory_space=pl.ANY)
    return pl.pallas_call(
        body, out_shape=[SDS((4,) + g.shape[1:], g.dtype) for g in gs], in_specs=[hbm] * n, out_specs=[hbm] * n,
        scratch_shapes=[pltpu.SemaphoreType.DMA((n, 4)), pltpu.SemaphoreType.DMA((n, 4))], name=name,
    )(*gs)


HBM_SPEC = pl.BlockSpec(memory_space=pltpu.HBM)
SEM_SPEC = pl.BlockSpec(memory_space=pltpu.SEMAPHORE)
SPLIT_COPY = pltpu.CompilerParams(has_side_effects=pltpu.SideEffectType.DATAFLOW_SIDE_EFFECTING)


def _in_hbm(t):
    return pltpu.with_memory_space_constraint(t, pltpu.HBM)


def _chip_copies(t_refs, land_refs, send_sems, recv_sems):
    x, y, c = _position()
    chips = [(1 - x, y), (x, 1 - y), (1 - x, 1 - y)]
    return [pltpu.make_async_remote_copy(
        src_ref=t.at[2 * px + py], dst_ref=land.at[j], send_sem=send_sems.at[3 * a + j],
        recv_sem=recv_sems.at[3 * a + j], device_id=(px, py, c), device_id_type=MESH)
        for a, (t, land) in enumerate(zip(t_refs, land_refs, strict=True)) for j, (px, py) in enumerate(chips)]


def _chip_exchange_start(ts, name):
    n = len(ts)
    lands = [_in_hbm(lax.empty((3,) + t.shape[1:], t.dtype)) for t in ts]

    def body(*refs):
        send_sems, recv_sems = refs[2 * n], refs[2 * n + 1]
        for cp in _chip_copies(refs[:n], refs[n:2 * n], send_sems, recv_sems):
            cp.start()
        refs[-1][...] = jnp.zeros_like(refs[-1])

    sems = pltpu.SemaphoreType.DMA((3 * n,))
    res = pl.pallas_call(
        body, name=name, in_specs=[HBM_SPEC] * (2 * n),
        out_shape=(sems, sems, *[pltpu.HBM(t.shape, t.dtype) for t in (*ts, *lands)], SDS((8, LANES), f32)),
        out_specs=(SEM_SPEC, SEM_SPEC, *[HBM_SPEC] * (2 * n), pl.BlockSpec(memory_space=pltpu.VMEM)),
        input_output_aliases={i: 2 + i for i in range(2 * n)}, compiler_params=SPLIT_COPY,
    )(*[_in_hbm(t) for t in ts], *lands)
    return res[:-1], res[-1]


def _chip_exchange_wait(state, after, name):
    send_sems, recv_sems, *arrays = state
    n = len(arrays) // 2

    def body(*refs):
        for cp in _chip_copies(refs[:n], refs[n:2 * n], refs[2 * n], refs[2 * n + 1]):
            cp.wait_send()
            cp.wait_recv()

    res = pl.pallas_call(
        body, name=name, in_specs=[HBM_SPEC] * (2 * n) + [SEM_SPEC, SEM_SPEC, pl.BlockSpec(memory_space=pl.ANY)],
        out_shape=[pltpu.HBM(t.shape, t.dtype) for t in arrays], out_specs=[HBM_SPEC] * (2 * n),
        input_output_aliases={i: i for i in range(2 * n)}, compiler_params=SPLIT_COPY,
    )(*arrays, send_sems, recv_sems, after)
    return res[n:]


def _gather_copies(x_refs, out_refs, send_sems, recv_sems):
    x, y, c = _position()
    peers = [(x, y, 1 - c), (1 - x, y, c), (x, 1 - y, c), (1 - x, 1 - y, c)]
    sends, arrivals = [], []
    for a, (x_ref, out_ref) in enumerate(zip(x_refs, out_refs, strict=True)):
        for k, (px, py, pc) in enumerate(peers):
            sems = dict(send_sem=send_sems.at[4 * a + k], recv_sem=recv_sems.at[4 * a + k],
                        device_id=(px, py, pc), device_id_type=MESH)
            sends.append(pltpu.make_async_remote_copy(src_ref=x_ref, dst_ref=out_ref.at[4 * x + 2 * y + c], **sems))
            arrivals.append(pltpu.make_async_remote_copy(src_ref=x_ref, dst_ref=out_ref.at[4 * px + 2 * py + pc],
                                                         **sems))
    return sends, arrivals


def _gather_start(blocks, after, name):
    n = len(blocks)
    outs = [_in_hbm(lax.empty((N_DEV,) + b.shape, b.dtype)) for b in blocks]

    def body(*refs):
        sends, _ = _gather_copies(refs[:n], refs[n:2 * n], refs[2 * n + 1], refs[2 * n + 2])
        for cp in sends:
            cp.start()
        refs[-1][...] = jnp.zeros_like(refs[-1])

    sems = pltpu.SemaphoreType.DMA((4 * n,))
    res = pl.pallas_call(
        body, name=name, in_specs=[HBM_SPEC] * (2 * n) + [pl.BlockSpec(memory_space=pl.ANY)],
        out_shape=(sems, sems, *[pltpu.HBM(t.shape, t.dtype) for t in (*blocks, *outs)], SDS((8, LANES), f32)),
        out_specs=(SEM_SPEC, SEM_SPEC, *[HBM_SPEC] * (2 * n), pl.BlockSpec(memory_space=pltpu.VMEM)),
        input_output_aliases={i: 2 + i for i in range(2 * n)}, compiler_params=SPLIT_COPY,
    )(*[_in_hbm(b) for b in blocks], *outs, after)
    return res[:-1], res[-1]


def _gather_wait(state, after, name):
    send_sems, recv_sems, *arrays = state
    n = len(arrays) // 2

    def body(*refs):
        sends, arrivals = _gather_copies(refs[:n], refs[n:2 * n], refs[2 * n], refs[2 * n + 1])
        for cp in sends:
            cp.wait_send()
        for cp in arrivals:
            cp.wait_recv()

    res = pl.pallas_call(
        body, name=name, in_specs=[HBM_SPEC] * (2 * n) + [SEM_SPEC, SEM_SPEC, pl.BlockSpec(memory_space=pl.ANY)],
        out_shape=[pltpu.HBM(t.shape, t.dtype) for t in arrays], out_specs=[HBM_SPEC] * (2 * n),
        input_output_aliases={i: i for i in range(2 * n)}, compiler_params=SPLIT_COPY,
    )(*arrays, send_sems, recv_sems, after)
    return res[:n], res[n:]


def _gather_finish(partial, name):
    n = len(partial)

    def body(*refs):
        in_refs, out_refs = refs[:n], refs[n:2 * n]
        send_sems, recv_sems = refs[2 * n:]
        x, y, c = _position()
        chips = [(1 - x, y), (x, 1 - y), (1 - x, 1 - y)]
        copies = []
        for a in range(n):
            for j, (px, py) in enumerate(chips):
                cp = pltpu.make_async_remote_copy(
                    src_ref=in_refs[a].at[4 * px + 2 * py + c], dst_ref=out_refs[a].at[4 * px + 2 * py + c],
                    send_sem=send_sems.at[a, j], recv_sem=recv_sems.at[a, j], device_id=(x, y, 1 - c),
                    device_id_type=MESH)
                cp.start()
                copies.append(cp)
        for a in range(n):
            for j, (px, py) in enumerate(chips):
                pltpu.make_async_remote_copy(
                    src_ref=in_refs[a].at[4 * px + 2 * py + (1 - c)], dst_ref=out_refs[a].at[4 * px + 2 * py + (1 - c)],
                    send_sem=send_sems.at[a, j], recv_sem=recv_sems.at[a, j], device_id=(x, y, 1 - c),
                    device_id_type=MESH).wait_recv()
        for cp in copies:
            cp.wait_send()

    hbm = pl.BlockSpec(memory_space=pl.ANY)
    return pl.pallas_call(
        body, out_shape=[SDS(p.shape, p.dtype) for p in partial], in_specs=[hbm] * n, out_specs=[hbm] * n,
        input_output_aliases={a: a for a in range(n)},
        scratch_shapes=[pltpu.SemaphoreType.DMA((n, 3)), pltpu.SemaphoreType.DMA((n, 3))],
        name=name,
    )(*partial)


def _row_tile(rows):
    return 256 if rows % 256 == 0 and rows > 512 else rows


def _pair_add(g, r1, core, name):
    def body(c_ref, g_ref, r_ref, o_ref):
        o_ref[...] = (g_ref[...].astype(f32) + r_ref[...].astype(f32)).astype(o_ref.dtype)

    rows, cols = g.shape[1:]
    tile = _row_tile(rows)
    blk = (1, tile, cols)
    return pl.pallas_call(
        body, out_shape=SDS((4, rows, cols), g.dtype), name=name,
        grid_spec=pltpu.PrefetchScalarGridSpec(
            num_scalar_prefetch=1, grid=(4, rows // tile),
            in_specs=[pl.BlockSpec(blk, lambda k, i, c_ref: (2 * k + c_ref[0], i, 0)),
                      pl.BlockSpec(blk, lambda k, i, c_ref: (k, i, 0))],
            out_specs=pl.BlockSpec(blk, lambda k, i, c_ref: (k, i, 0))),
        compiler_params=_params(("parallel", "arbitrary")),
    )(core, g, r1)


def _chip_add(t, r2, chip, name):
    def body(c_ref, t_ref, r_ref, o_ref):
        o_ref[...] = ((t_ref[0].astype(f32) + r_ref[0].astype(f32)) + r_ref[1].astype(f32)) + r_ref[2].astype(f32)

    rows, cols = t.shape[1:]
    tile = _row_tile(rows)
    return pl.pallas_call(
        body, out_shape=SDS((rows, cols), f32), name=name,
        grid_spec=pltpu.PrefetchScalarGridSpec(
            num_scalar_prefetch=1, grid=(rows // tile,),
            in_specs=[pl.BlockSpec((1, tile, cols), lambda i, c_ref: (c_ref[0], i, 0)),
                      pl.BlockSpec((3, tile, cols), lambda i, c_ref: (0, i, 0))],
            out_specs=pl.BlockSpec((tile, cols), lambda i, c_ref: (i, 0))),
        compiler_params=_params(("arbitrary",)),
    )(chip, t, r2)


def _pad_to(t, axis, size):
    pads = [(0, 0)] * t.ndim
    pads[axis] = (0, size - t.shape[axis])
    return jnp.pad(t, pads)


_REF_COLS = {"qa": (0, FOX_W), "ka": (FOX_W, FOX_W), "va": (2 * FOX_W, FOX_W), "f": (3 * FOX_W, N_FOX_HEADS)}
_REF_COLS.update({n: (3 * FOX_W + N_FOX_HEADS + i * DIL_W, DIL_W) for i, n in enumerate(("qb", "kb", "vb"))})
_REF_COLS.update({n: (3 * FOX_W + N_FOX_HEADS + 3 * DIL_W + i * D, D) for i, n in enumerate(("ga", "gb"))})
_REF_ORDER = ("qa", "ka", "va", "f", "qb", "kb", "vb", "ga", "gb")


def _shard_pad_cols(pieces):
    first = pieces[_REF_ORDER[0]]
    pad = jnp.zeros((first.shape[0], W_IN_PAD - W_IN_SH), first.dtype)
    parts, names, used = [], list(_REF_ORDER), 0
    for _ in range(N_DEV):
        need = W_IN_SH
        while need:
            take = min(need, _REF_COLS[names[0]][1] - used)
            parts.append(pieces[names[0]][:, used:used + take])
            need, used = need - take, used + take
            if used == _REF_COLS[names[0]][1]:
                names, used = names[1:], 0
        parts.append(pad)
    return jnp.concatenate(parts, axis=1)


def _slab_w_in(stack):
    def cols(name):
        lo, width = _REF_COLS[name]
        hi, out = lo + width, []
        while lo < hi:
            j, off = divmod(lo, W_IN_SH)
            n = min(hi - lo, W_IN_SH - off)
            out.append(stack[j, :, off:off + n])
            lo += n
        return out
    z = lambda n: [jnp.zeros((stack.shape[1], n), stack.dtype)]
    parts = (cols("ga") + cols("gb") + z(C_QB - 2 * D) + cols("qb") + cols("kb") + cols("vb") + cols("qa")
             + cols("ka") + cols("va") + cols("f") + z(LANES - N_FOX_HEADS))
    return jnp.concatenate(parts, axis=1)


def kernel(x, c, w_ada, b_ada, g_mix, w_in, b_fgate, w_br_a, w_br_b, w_out, g_ffn, w_ffn_gate, w_ffn_up, w_ffn_down, g_final, loss_target, m_w_ada, m_b_ada, m_g_mix, m_w_in, m_b_fgate, m_w_br_a, m_w_br_b, m_w_out, m_g_ffn, m_w_ffn_gate, m_w_ffn_up, m_w_ffn_down, m_g_final, v_w_ada, v_b_ada, v_g_mix, v_w_in, v_b_fgate, v_w_br_a, v_w_br_b, v_w_out, v_g_ffn, v_w_ffn_gate, v_w_ffn_up, v_w_ffn_down, v_g_final):
    px, py, pc = _position()
    dev = 4 * px + 2 * py + pc
    x2d, tgt = x[0], loss_target[0]

    c_all = _all_gather(c, "gather_c").reshape(N_DEV, D)
    ada_cols = w_ada.shape[2]
    b_shard = lax.dynamic_slice(b_ada, (0, dev * ada_cols), (1, ada_cols))
    mod_shard = _ada_fwd(c_all, w_ada[0], b_shard)
    mod_all = _all_gather(mod_shard, "gather_mod")
    modv = lax.dynamic_index_in_dim(mod_all, dev, axis=1, keepdims=False).reshape(6, D)

    gate_up = jnp.concatenate([_pad_to(w_ffn_gate[0], 1, FF_PAD), _pad_to(w_ffn_up[0], 1, FF_PAD)], axis=1)
    w_in_s, = _all_gather_many([_pad_to(w_in[0], 1, W_IN_PAD).astype(bf16)], "gather_w_in")
    later = [w_br_a[0], w_br_b[0], w_out[0], gate_up, _pad_to(w_ffn_down[0], 0, FF_PAD)]
    later_state, later_token = _gather_start([t.astype(bf16) for t in later], w_in_s, "gather_rest_start")
    w_in_p = _slab_w_in(w_in_s)

    h1 = _pre1(x2d, modv, g_mix)
    proj = _matmul(h1, w_in_p, name="mm_proj", tm=SEQ, tn=896, tk=D, after=later_token)
    b_pad = jnp.pad(b_fgate, ((0, 0), (0, LANES - N_FOX_HEADS)))
    q_aug, k_aug, va = _fox_prep(proj, _fox_gate_fwd(proj, b_pad))
    ya_h, lse_a = _fox_fwd(q_aug, k_aug, va)

    tables = _rope_tables()
    qb_r, kb_r, vb = _rope_fwd(proj, tables)
    q_c, k_c, v_c = _to_classes(qb_r), _to_classes(kb_r), _to_classes(vb)
    o_c, lse_c = _dil_fwd(q_c, k_c, v_c)
    yb_h, lse_b = _dil_combine(_from_classes(o_c), _from_classes(lse_c))

    mine, arrived = _gather_wait(later_state, yb_h, "gather_rest_wait")
    w_a_s, w_b_s, w_o_s, w_gu_s, w_d_s = [
        lax.dynamic_update_slice(stack, block[None], (dev, 0, 0))
        for stack, block in zip(_gather_finish(arrived, "gather_rest_finish"), mine, strict=True)]
    w_o = w_o_s.reshape(D, D)
    w_d = w_d_s.reshape(FF_HID, D)
    ya = _matmul(ya_h, w_a_s, by_shard=True, name="mm_br_a", tm=SEQ, tn=W_BR_SH, tk=FOX_W)
    yb = _matmul(yb_h, w_b_s, by_shard=True, name="mm_br_b", tm=SEQ, tn=W_BR_SH, tk=DIL_OUT_W)

    merged = _merge_fwd(ya, yb, proj)
    mix = _matmul(merged, w_o, name="mm_out", tm=SEQ, tn=512, tk=D)
    x1, h2 = _post1(x2d, mix, modv, g_ffn)
    au = _matmul(h2, w_gu_s, by_shard=True, name="mm_ffn_in", tm=SEQ, tn=2 * FF_PAD, tk=D)
    act = _swiglu_fwd(au)
    ff = _matmul(act, w_d, name="mm_ffn_down", tm=SEQ, tn=512, tk=FF_HID // 2)

    dx2, dff, dg_final, dga_f, loss_lanes = _final(x1, ff, tgt, modv, g_final.reshape(1, D))
    dact = _matmul(dff, w_d, tb=True, name="mm_d_act", tm=SEQ // 2, tn=FF_HID // 2, tk=D)
    dau = _swiglu_bwd(au, dact)

    core = pc.astype(jnp.int32).reshape(1)
    chip = (2 * px + py).astype(jnp.int32).reshape(1)

    def to_chips(by_dev, tags, name):
        from_pair = _pair_exchange(by_dev, "pair_exchange_" + name)
        sums = [_pair_add(g, r, core, "pair_add_" + t) for g, r, t in zip(by_dev, from_pair, tags)]
        state, token = _chip_exchange_start(sums, "chip_exchange_start_" + name)
        return sums, state, token

    def from_chips(sums, state, after, tags, name):
        got = _chip_exchange_wait(state, after, "chip_exchange_wait_" + name)
        return [_chip_add(p, r, chip, "chip_add_" + t) for p, r, t in zip(sums, got, tags)]

    g_gu = _matmul(h2, dau, ta=True, by_shard=True, out_dtype=bf16, name="mm_g_ffn_in", tm=D, tn=2 * FF_PAD, tk=SEQ)
    g_d = _matmul(act, dff, ta=True, out_dtype=bf16, name="mm_g_down", tm=FF_HID // 2, tn=512, tk=SEQ)
    ffn_tags = ["gu", "down"]
    ffn_sums, ffn_state, ffn_token = to_chips([g_gu, g_d.reshape(N_DEV, FF_PAD, D)], ffn_tags, "ffn")

    dh2 = _matmul(dau, w_gu_s, tb=True, by_shard=True, name="mm_d_h2", tm=SEQ // 2, tn=D, tk=2 * FF_PAD,
                  after=ffn_token)
    dx1, dmix, dsh_f, dsc_f, dg_ffn, dga_m = _mid_bwd(dh2, x1, dx2, mix, modv, g_ffn)
    dmerged = _matmul(dmix, w_o, tb=True, name="mm_d_merged", tm=SEQ, tn=512, tk=D)
    dya, dyb, dga, dgb = _merge_bwd(dmerged, ya, yb, proj)
    dya_h = _matmul(dya, w_a_s, tb=True, by_shard=True, name="mm_d_ya", tm=SEQ, tn=FOX_W, tk=W_BR_SH)
    dyb_h = _matmul(dyb, w_b_s, tb=True, by_shard=True, name="mm_d_yb", tm=SEQ, tn=DIL_OUT_W, tk=W_BR_SH)

    dqa, dka, dva, dF = _fox_bwd(q_aug, k_aug, va, dya_h, ya_h, lse_a)
    dF_row = jnp.pad(dF[:, :2, :].reshape(N_FOX_HEADS, SEQ), ((0, LANES - N_FOX_HEADS), (0, 0)))
    df, db_fgate = _fox_gate_bwd(dF_row, proj, b_pad)

    delta_b = _dil_delta(dyb_h, yb_h)
    rep = lambda t: _to_classes(jnp.tile(t, (1, N_GROUPS)))
    dq_c, dk_c, dv_c = _dil_bwd(q_c, k_c, v_c, rep(dyb_h), rep(lse_b), rep(delta_b))
    dqb, dkb = _rope_bwd(_from_classes(dq_c), _from_classes(dk_c), tables)
    dvb = _from_classes(dv_c).astype(bf16)

    dproj = _shard_pad_cols({"qa": dqa, "ka": dka, "va": dva, "f": df[:, :N_FOX_HEADS], "qb": dqb, "kb": dkb,
                             "vb": dvb, "ga": dga, "gb": dgb})
    g_in = _matmul(h1, dproj, ta=True, by_shard=True, out_dtype=bf16, name="mm_g_in", tm=D, tn=W_IN_PAD, tk=SEQ)
    g_o = _matmul(merged, dmix, ta=True, out_dtype=bf16, name="mm_g_out", tm=D, tn=512, tk=SEQ)
    g_a = _matmul(ya_h, dya, ta=True, by_shard=True, out_dtype=bf16, name="mm_g_br_a", tm=FOX_W, tn=W_BR_SH, tk=SEQ)
    g_b = _matmul(yb_h, dyb, ta=True, by_shard=True, out_dtype=bf16, name="mm_g_br_b", tm=DIL_OUT_W, tn=W_BR_SH,
                  tk=SEQ)
    rows_a, rows_b = FOX_W * W_BR_SH // D, DIL_OUT_W * W_BR_SH // D
    g_small = jnp.concatenate([g_a.reshape(N_DEV, rows_a, D), g_b.reshape(N_DEV, rows_b, D),
                               g_o.reshape(N_DEV, W_BR_SH, D)], axis=1)
    mix_tags = ["in", "small"]
    mix_sums, mix_state, mix_token = to_chips([g_in, g_small], mix_tags, "mixer")

    dh1 = _matmul(dproj, w_in_s, tb=True, by_shard=True, name="mm_d_h1", tm=SEQ // 2, tn=D, tk=W_IN_PAD,
                  after=mix_token)
    grad_x, dsh_m, dsc_m, dg_mix = _first_bwd(dh1, x2d, dx1, modv, g_mix)

    pad_lane = lambda t: jnp.pad(t, ((0, 0), (0, D - t.shape[1])))
    small = jnp.concatenate([dsh_m, dsc_m, dga_m, dsh_f, dsc_f, dga_f, dg_mix, dg_ffn, dg_final,
                             pad_lane(db_fgate), loss_lanes, jnp.zeros((SMALL_ROWS - 11, D), f32)], axis=0)
    small_all = _all_gather(small, "gather_small")
    small_sum, loss_row = _small_reduce(small_all)
    dmod_all = small_all[:, :6, :].reshape(N_DEV, 6 * D)
    g_w_ada = _ada_bwd(c_all, lax.dynamic_slice(dmod_all, (0, dev * ada_cols), (N_DEV, ada_cols)))

    s_gu, s_d = from_chips(ffn_sums, ffn_state, small_sum, ffn_tags, "ffn")
    s_in, s_small = from_chips(mix_sums, mix_state, g_w_ada, mix_tags, "mixer")
    g_shard = {
        "w_in": s_in[:, :W_IN_SH], "w_ffn_gate": s_gu[:, :W_FF_SH], "w_ffn_up": s_gu[:, FF_PAD:FF_PAD + W_FF_SH],
        "w_ffn_down": s_d[:W_FF_SH], "w_br_a": s_small[:rows_a].reshape(FOX_W, W_BR_SH),
        "w_br_b": s_small[rows_a:rows_a + rows_b].reshape(DIL_OUT_W, W_BR_SH), "w_out": s_small[rows_a + rows_b:],
    }

    loss = loss_row[0, 0]
    g = {
        "w_ada": g_w_ada[None], "b_ada": small_sum[0:6].reshape(1, 6 * D), "g_mix": small_sum[6:7],
        "w_in": g_shard["w_in"][None], "b_fgate": small_sum[9:10, :N_FOX_HEADS], "w_br_a": g_shard["w_br_a"][None],
        "w_br_b": g_shard["w_br_b"][None], "w_out": g_shard["w_out"][None], "g_ffn": small_sum[7:8],
        "w_ffn_gate": g_shard["w_ffn_gate"][None], "w_ffn_up": g_shard["w_ffn_up"][None],
        "w_ffn_down": g_shard["w_ffn_down"][None], "g_final": small_sum[8],
    }
    w = {"w_ada": w_ada, "b_ada": b_ada, "g_mix": g_mix, "w_in": w_in, "b_fgate": b_fgate, "w_br_a": w_br_a,
         "w_br_b": w_br_b, "w_out": w_out, "g_ffn": g_ffn, "w_ffn_gate": w_ffn_gate, "w_ffn_up": w_ffn_up,
         "w_ffn_down": w_ffn_down, "g_final": g_final}
    m = {"w_ada": m_w_ada, "b_ada": m_b_ada, "g_mix": m_g_mix, "w_in": m_w_in, "b_fgate": m_b_fgate,
         "w_br_a": m_w_br_a, "w_br_b": m_w_br_b, "w_out": m_w_out, "g_ffn": m_g_ffn, "w_ffn_gate": m_w_ffn_gate,
         "w_ffn_up": m_w_ffn_up, "w_ffn_down": m_w_ffn_down, "g_final": m_g_final}
    v = {"w_ada": v_w_ada, "b_ada": v_b_ada, "g_mix": v_g_mix, "w_in": v_w_in, "b_fgate": v_b_fgate,
         "w_br_a": v_w_br_a, "w_br_b": v_w_br_b, "w_out": v_w_out, "g_ffn": v_g_ffn, "w_ffn_gate": v_w_ffn_gate,
         "w_ffn_up": v_w_ffn_up, "w_ffn_down": v_w_ffn_down, "g_final": v_g_final}
    names = list(w)
    delta, new_m, new_v = {}, {}, {}
    for n in names:
        shape = w[n].shape
        two_d = (lambda t: t.reshape(shape[-2:])) if len(shape) == 3 else (lambda t: t)
        dl, mn, vn = _adamw(two_d(w[n]), two_d(g[n]), two_d(m[n]), two_d(v[n]), "adamw_" + n)
        delta[n], new_m[n], new_v[n] = dl.reshape(shape), mn.reshape(shape), vn.reshape(shape)

    return (loss, grad_x[None], *[g[n] for n in names], *[delta[n] for n in names],
            *[new_m[n] for n in names], *[new_v[n] for n in names])
```

```python
import functools

import jax
import jax.numpy as jnp
from jax import lax
from jax.experimental import pallas as pl
from jax.experimental.pallas import tpu as pltpu

f32 = jnp.float32
bf16 = jnp.bfloat16
SDS = jax.ShapeDtypeStruct
MESH = pl.DeviceIdType.MESH

N_DEV = 8
D = 1024
SEQ = 2048
HEAD_DIM = 64
N_FOX_HEADS = 8
FOX_W = 512
DIL_W = 768
DIL_OUT_W = 256
ROT_DIM = 16
ROPE_THETA = 500000.0
D_FF = 2816
IN_COLS = 5896
EPS = 1e-6
NEG = -1e30
ATT_SCALE = HEAD_DIM ** -0.5

ADAM_LR = 0.001
ADAM_B1 = 0.9
ADAM_B2 = 0.999
ADAM_EPS = 1e-08
ADAM_WD = 0.01
ADAM_STEP = 10

C_GA, C_GB, C_QB, C_KB, C_VB, C_QA, C_KA, C_VA, C_F = 0, 1024, 2304, 3072, 3840, 4608, 5120, 5632, 6144
PROJ_W = 6272
LANES = 128
VMEM_LIMIT = 52 * 1024 * 1024

W_IN_SH, W_IN_PAD = IN_COLS // N_DEV, 768
W_BR_SH = D // N_DEV
W_FF_SH, FF_PAD = D_FF // N_DEV, 384
FF_HID = N_DEV * FF_PAD
SMALL_ROWS = 16


def _params(sem=None):
    if sem is None:
        return pltpu.CompilerParams(vmem_limit_bytes=VMEM_LIMIT)
    return pltpu.CompilerParams(dimension_semantics=sem, vmem_limit_bytes=VMEM_LIMIT)


def _rowwise(fn, name, tiled, vecs, outs, reds=(), tile=256):
    nt, nv, no = len(tiled), len(vecs), len(outs)
    rows = tiled[0][0].shape[0]
    assert rows % tile == 0

    def body(*refs):
        tin = [r[...] for r in refs[:nt]]
        vin = [r[...] for r in refs[nt:nt + nv]]
        orefs = refs[nt + nv:nt + nv + no]
        rrefs = refs[nt + nv + no:]
        touts, routs = fn(tin, vin)
        for r, t in zip(orefs, touts, strict=True):
            r[...] = t.astype(r.dtype)
        if rrefs:
            @pl.when(pl.program_id(0) == 0)
            def _():
                for r in rrefs:
                    r[...] = jnp.zeros_like(r)
            for r, t in zip(rrefs, routs, strict=True):
                r[...] += t

    def col_map(cb):
        return lambda i: (i, cb)

    def whole_map(nd):
        return lambda i: (0,) * nd

    in_specs = [pl.BlockSpec((tile, w), col_map(cb)) for (_, w, cb) in tiled]
    in_specs += [pl.BlockSpec(v.shape, whole_map(v.ndim)) for v in vecs]
    out_specs = [pl.BlockSpec((tile, w), lambda i: (i, 0)) for (w, _) in outs]
    out_specs += [pl.BlockSpec((1, w), lambda i: (0, 0)) for w in reds]
    out_shape = [SDS((rows, w), dt) for (w, dt) in outs] + [SDS((1, w), f32) for w in reds]
    res = pl.pallas_call(
        body, grid=(rows // tile,), in_specs=in_specs, out_specs=out_specs, out_shape=out_shape, name=name,
        compiler_params=_params(("arbitrary",)),
    )(*[t[0] for t in tiled], *vecs)
    return res


def _matmul(a, b, *, ta=False, tb=False, out_dtype=f32, name, tm, tn, tk, by_shard=False, after=None):
    m, k = (a.shape[1], a.shape[0]) if ta else a.shape
    if by_shard and not ta:
        n, kb = (b.shape[1], N_DEV * b.shape[2]) if tb else (N_DEV * b.shape[2], b.shape[1])
        assert (tk if tb else tn) == b.shape[2]
    else:
        n, kb = (b.shape[0], b.shape[1]) if tb else (b.shape[1], b.shape[0])
    assert kb == k and m % tm == 0 and n % tn == 0 and k % tk == 0
    nk = k // tk
    dims = (((0 if ta else 1,), (1 if tb else 0,)), ((), ()))
    b_stacked = by_shard and not ta
    o_stacked = by_shard and ta

    def body(a_ref, b_ref, *rest):
        o_ref, *acc = rest[1:] if after is not None else rest
        bv = b_ref[0] if b_stacked else b_ref[...]
        p = lax.dot_general(a_ref[...].astype(bf16), bv.astype(bf16), dims, preferred_element_type=f32)

        def put(val):
            if o_stacked:
                o_ref[0] = val.astype(o_ref.dtype)
            else:
                o_ref[...] = val.astype(o_ref.dtype)

        if nk == 1:
            put(p)
        else:
            acc_ref, = acc
            kk = pl.program_id(2)

            @pl.when(kk == 0)
            def _():
                acc_ref[...] = p

            @pl.when(kk > 0)
            def _():
                acc_ref[...] += p

            @pl.when(kk == nk - 1)
            def _():
                put(acc_ref[...])

    a_spec = pl.BlockSpec((tk, tm), lambda i, j, kk: (kk, i)) if ta else pl.BlockSpec((tm, tk), lambda i, j, kk: (i, kk))
    if b_stacked and tb:
        b_spec = pl.BlockSpec((1, tn, tk), lambda i, j, kk: (kk, j, 0))
    elif b_stacked:
        b_spec = pl.BlockSpec((1, tk, tn), lambda i, j, kk: (j, kk, 0))
    elif tb:
        b_spec = pl.BlockSpec((tn, tk), lambda i, j, kk: (j, kk))
    else:
        b_spec = pl.BlockSpec((tk, tn), lambda i, j, kk: (kk, j))
    if o_stacked:
        assert tn == n // N_DEV
        out_spec = pl.BlockSpec((1, tm, tn), lambda i, j, kk: (j, i, 0))
        out_shape = SDS((N_DEV, m, tn), out_dtype)
    else:
        out_spec = pl.BlockSpec((tm, tn), lambda i, j, kk: (i, j))
        out_shape = SDS((m, n), out_dtype)
    extra_specs, extra = ([pl.BlockSpec(memory_space=pl.ANY)], [after]) if after is not None else ([], [])
    return pl.pallas_call(
        body, grid=(m // tm, n // tn, nk), in_specs=[a_spec, b_spec] + extra_specs, out_specs=out_spec,
        out_shape=out_shape, name=name, scratch_shapes=[pltpu.VMEM((tm, tn), f32)] if nk > 1 else [],
        compiler_params=_params(("parallel", "parallel", "arbitrary")),
    )(a, b, *extra)


def _rms(x):
    r = lax.rsqrt(jnp.mean(x * x, axis=-1, keepdims=True) + EPS)
    return r, x * r


def _rms_bwd(r, xn, dxn):
    return r * (dxn - xn * jnp.mean(dxn * xn, axis=-1, keepdims=True))


def _colsum(t):
    return jnp.sum(t, axis=0, keepdims=True)


def _sigmoid(x):
    return 1.0 / (1.0 + jnp.exp(-x))


def _modulated_norm(x, g, shift, scale):
    _, xn = _rms(x)
    return (xn * g) * (1.0 + scale) + shift


def _pre1(x, modv, g_mix):
    def fn(t, v):
        (xt,), (mv, g) = t, v
        return [_modulated_norm(xt, g, mv[0:1], mv[1:2])], []
    return _rowwise(fn, "pre1", [(x, D, 0)], [modv, g_mix], [(D, bf16)])[0]


def _post1(x, mix, modv, g_ffn):
    def fn(t, v):
        (xt, mt), (mv, g) = t, v
        x1 = xt + mv[2:3] * mt
        return [x1, _modulated_norm(x1, g, mv[3:4], mv[4:5])], []
    return _rowwise(fn, "post1", [(x, D, 0), (mix, D, 0)], [modv, g_ffn], [(D, f32), (D, bf16)])


def _gate_up(au, j):
    base = 2 * j * FF_PAD
    return au[:, base:base + FF_PAD], au[:, base + FF_PAD:base + 2 * FF_PAD]


def _swiglu_fwd(au):
    def fn(t, v):
        acts = []
        for j in range(N_DEV):
            a, u = _gate_up(t[0], j)
            acts.append(a * _sigmoid(a) * u)
        return [jnp.concatenate(acts, axis=1)], []
    return _rowwise(fn, "swiglu_fwd", [(au, 2 * FF_HID, 0)], [], [(FF_HID, bf16)])[0]


def _swiglu_bwd(au, dact):
    def fn(t, v):
        parts = []
        for j in range(N_DEV):
            a, u = _gate_up(t[0], j)
            d = t[1][:, j * FF_PAD:(j + 1) * FF_PAD]
            sg = _sigmoid(a)
            parts += [d * u * (sg * (1.0 + a * (1.0 - sg))), d * (a * sg)]
        return [jnp.concatenate(parts, axis=1)], []
    return _rowwise(fn, "swiglu_bwd", [(au, 2 * FF_HID, 0), (dact, FF_HID, 0)], [], [(2 * FF_HID, bf16)],
                    tile=128)[0]


def _final(x1, ff, target, modv, g_final):
    def fn(t, v):
        (x1t, fft, tgt), (mv, g) = t, v
        x2 = x1t + mv[5:6] * fft
        r, xn = _rms(x2)
        err = xn * g - tgt
        dy = err * (1.0 / D)
        dx2 = _rms_bwd(r, xn, dy * g)
        return [dx2, dx2 * mv[5:6]], [_colsum(dy * xn), _colsum(dx2 * fft), _colsum(err * err) * (0.5 / D)]
    return _rowwise(fn, "final", [(x1, D, 0), (ff, D, 0), (target, D, 0)], [modv, g_final],
                    [(D, f32), (D, bf16)], [D, D, D])


def _mid_bwd(dh2, x1, dx2, mix, modv, g_ffn):
    def fn(t, v):
        (dh, x1t, dx2t, mt), (mv, g) = t, v
        r, xn = _rms(x1t)
        dn = dh * (1.0 + mv[4:5])
        dx1 = dx2t + _rms_bwd(r, xn, dn * g)
        return [dx1, dx1 * mv[2:3]], [_colsum(dh), _colsum(dh * (xn * g)), _colsum(dn * xn), _colsum(dx1 * mt)]
    return _rowwise(fn, "mid_bwd", [(dh2, D, 0), (x1, D, 0), (dx2, D, 0), (mix, D, 0)], [modv, g_ffn],
                    [(D, f32), (D, bf16)], [D, D, D, D])


def _first_bwd(dh1, x, dx1, modv, g_mix):
    def fn(t, v):
        (dh, xt, dx1t), (mv, g) = t, v
        r, xn = _rms(xt)
        dn = dh * (1.0 + mv[1:2])
        return [dx1t + _rms_bwd(r, xn, dn * g)], [_colsum(dh), _colsum(dh * (xn * g)), _colsum(dn * xn)]
    return _rowwise(fn, "first_bwd", [(dh1, D, 0), (x, D, 0), (dx1, D, 0)], [modv, g_mix], [(D, f32)], [D, D, D])


def _merge_fwd(ya, yb, proj):
    def fn(t, v):
        ya_t, yb_t, ga, gb = t
        return [_sigmoid(ga) * ya_t + _sigmoid(gb) * yb_t], []
    return _rowwise(fn, "merge_fwd", [(ya, D, 0), (yb, D, 0), (proj, D, C_GA // D), (proj, D, C_GB // D)], [],
                    [(D, bf16)])[0]


def _merge_bwd(dmerged, ya, yb, proj):
    def fn(t, v):
        dm, ya_t, yb_t, ga, gb = t
        sa, sb = _sigmoid(ga), _sigmoid(gb)
        return [dm * sa, dm * sb, dm * ya_t * (sa * (1.0 - sa)), dm * yb_t * (sb * (1.0 - sb))], []
    return _rowwise(fn, "merge_bwd",
                    [(dmerged, D, 0), (ya, D, 0), (yb, D, 0), (proj, D, C_GA // D), (proj, D, C_GB // D)], [],
                    [(D, bf16), (D, bf16), (D, bf16), (D, bf16)])


def _rope_tables():
    half = ROT_DIM // 2
    pos = jnp.arange(SEQ, dtype=f32)
    inv_freq = ROPE_THETA ** (-jnp.arange(0, ROT_DIM, 2, dtype=f32) / ROT_DIM)
    ang = pos[:, None] * inv_freq[None, :]
    cos, sin = jnp.cos(ang), jnp.sin(ang)
    pad = jnp.zeros((SEQ, HEAD_DIM - ROT_DIM), f32)
    zero = jnp.zeros((SEQ, half), f32)
    c_head = jnp.concatenate([cos, cos, pad + 1.0], axis=1)
    lo_head = jnp.concatenate([-sin, zero, pad], axis=1)
    hi_head = jnp.concatenate([zero, sin, pad], axis=1)
    return tuple(jnp.concatenate([t, t], axis=1) for t in (c_head, lo_head, hi_head))


def _over_heads(tables):
    return [jnp.tile(t, (1, DIL_W // LANES)) for t in tables]


def _rope_fwd(proj, tables):
    half = ROT_DIM // 2

    def fn(t, v):
        q, k, vv = t[:3]
        c, lo, hi = _over_heads(t[3:])
        rot = lambda z: z * c + pltpu.roll(z, DIL_W - half, 1) * lo + pltpu.roll(z, half, 1) * hi
        return [rot(q) * ATT_SCALE, rot(k), vv], []
    return _rowwise(fn, "rope_fwd", [(proj, DIL_W, C_QB // DIL_W), (proj, DIL_W, C_KB // DIL_W),
                                     (proj, DIL_W, C_VB // DIL_W)] + [(tb, LANES, 0) for tb in tables], [],
                    [(DIL_W, bf16)] * 3)


def _rope_bwd(dq, dk, tables):
    half = ROT_DIM // 2

    def fn(t, v):
        dq_t, dk_t = t[:2]
        c, lo, hi = _over_heads(t[2:])
        rot_t = lambda z: z * c + pltpu.roll(z * lo, half, 1) + pltpu.roll(z * hi, DIL_W - half, 1)
        return [rot_t(dq_t), rot_t(dk_t)], []
    return _rowwise(fn, "rope_bwd", [(dq, DIL_W, 0), (dk, DIL_W, 0)] + [(tb, LANES, 0) for tb in tables], [],
                    [(DIL_W, bf16), (DIL_W, bf16)])


def _head_bcast_sum(d):
    lane = lax.broadcasted_iota(jnp.int32, d.shape, 1)
    out = jnp.zeros_like(d)
    for h in range(d.shape[1] // HEAD_DIM):
        sel = (lane >= h * HEAD_DIM) & (lane < (h + 1) * HEAD_DIM)
        out = jnp.where(sel, jnp.sum(jnp.where(sel, d, 0.0), axis=1, keepdims=True), out)
    return out


def _dil_combine(o, lse):
    def fn(t, v):
        o0, o1, o2, l0, l1, l2 = t
        m = jnp.maximum(jnp.maximum(l0, l1), l2)
        w0, w1, w2 = jnp.exp(l0 - m), jnp.exp(l1 - m), jnp.exp(l2 - m)
        tot = w0 + w1 + w2
        return [(w0 * o0 + w1 * o1 + w2 * o2) / tot, m + jnp.log(tot)], []
    w = DIL_OUT_W
    return _rowwise(fn, "dil_combine", [(o, w, 0), (o, w, 1), (o, w, 2), (lse, w, 0), (lse, w, 1), (lse, w, 2)], [],
                    [(w, f32), (w, f32)])


def _dil_delta(dyb_h, yb_h):
    def fn(t, v):
        return [_head_bcast_sum(t[0] * t[1])], []
    return _rowwise(fn, "dil_delta", [(dyb_h, DIL_OUT_W, 0), (yb_h, DIL_OUT_W, 0)], [], [(DIL_OUT_W, f32)])[0]


def _adamw(w, g, m, v, name):
    shape = w.shape
    if w.ndim == 1:
        w, g, m, v = (t.reshape(1, -1) for t in (w, g, m, v))
    rows, cols = w.shape
    tile = 256 if rows % 256 == 0 and rows > 512 else rows

    def fn(t, _):
        wt, gt, mt, vt = t
        mn = ADAM_B1 * mt + (1.0 - ADAM_B1) * gt
        vn = ADAM_B2 * vt + (1.0 - ADAM_B2) * (gt * gt)
        m_hat = mn / (1.0 - ADAM_B1 ** ADAM_STEP)
        v_hat = vn / (1.0 - ADAM_B2 ** ADAM_STEP)
        return [-ADAM_LR * (m_hat / (jnp.sqrt(v_hat) + ADAM_EPS) + ADAM_WD * wt), mn, vn], []
    delta, mn, vn = _rowwise(fn, name, [(w, cols, 0), (g, cols, 0), (m, cols, 0), (v, cols, 0)], [],
                             [(cols, f32)] * 3, tile=tile)
    return delta.reshape(shape), mn.reshape(shape), vn.reshape(shape)


def _ada_fwd(c_all, w_shard, b_shard):
    def body(c_ref, w_ref, b_ref, o_ref):
        cv = c_ref[...]
        sc = (cv * _sigmoid(cv)).astype(bf16)
        o_ref[...] = jnp.dot(sc, w_ref[...].astype(bf16), preferred_element_type=f32) + b_ref[...]
    return pl.pallas_call(body, out_shape=SDS((N_DEV, w_shard.shape[1]), f32), name="ada_fwd",
                          compiler_params=_params())(c_all, w_shard, b_shard)


def _ada_bwd(c_all, dmod_cols):
    def body(c_ref, d_ref, o_ref):
        cv = c_ref[...]
        sc = cv * _sigmoid(cv)
        o_ref[...] = lax.dot_general(sc, d_ref[...], (((0,), (0,)), ((), ())), precision=lax.Precision.HIGHEST,
                                     preferred_element_type=f32)
    return pl.pallas_call(body, out_shape=SDS((D, dmod_cols.shape[1]), f32), name="ada_bwd",
                          compiler_params=_params())(c_all, dmod_cols)


def _small_reduce(gathered, after):
    def body(g_ref, after_ref, o_ref, loss_ref):
        acc = g_ref[0]
        for d in range(1, N_DEV):
            acc = acc + g_ref[d]
        o_ref[...] = acc
        loss_ref[...] = jnp.zeros((1, LANES), f32) + jnp.sum(acc[10:11, :])
    return pl.pallas_call(body, out_shape=(SDS((SMALL_ROWS, D), f32), SDS((1, LANES), f32)), name="small_reduce",
                          in_specs=[pl.BlockSpec(memory_space=pltpu.VMEM), pl.BlockSpec(memory_space=pl.ANY)],
                          compiler_params=_params())(gathered, after)


FOX_BLK = 512
CUM_BLK = 128


def _fold_lanes(t, op):
    out = t[:, :LANES]
    for j in range(1, t.shape[1] // LANES):
        out = op(out, t[:, j * LANES:(j + 1) * LANES])
    return out


def _fox_gate_fwd(proj, b_pad):
    nblk = SEQ // CUM_BLK

    def body(f_ref, b_ref, col_ref):
        r = lax.broadcasted_iota(jnp.int32, (CUM_BLK, CUM_BLK), 0)
        c = lax.broadcasted_iota(jnp.int32, (CUM_BLK, CUM_BLK), 1)
        tri = (r >= c).astype(f32)
        carry = jnp.zeros((1, LANES), f32)
        for blk in range(nblk):
            z = f_ref[blk * CUM_BLK:(blk + 1) * CUM_BLK, :] + b_ref[...]
            logf = jnp.minimum(z, 0.0) - jnp.log1p(jnp.exp(-jnp.abs(z)))
            cs = jnp.dot(tri, logf, precision=lax.Precision.HIGHEST, preferred_element_type=f32) + carry
            col_ref[blk * CUM_BLK:(blk + 1) * CUM_BLK, :] = cs
            carry = cs[CUM_BLK - 1:CUM_BLK, :]

    return pl.pallas_call(
        body, grid=(1,), in_specs=[pl.BlockSpec((SEQ, LANES), lambda i: (0, C_F // LANES)),
                                   pl.BlockSpec((1, LANES), lambda i: (0, 0))],
        out_specs=pl.BlockSpec((SEQ, LANES), lambda i: (0, 0)),
        out_shape=SDS((SEQ, LANES), f32), name="fox_gate_fwd",
        compiler_params=_params(("arbitrary",)),
    )(proj, b_pad)


def _fox_gate_bwd(dF_row, proj, b_pad):
    nblk = SEQ // CUM_BLK

    def body(d_ref, f_ref, b_ref, df_ref, db_ref, col_ref):
        r = lax.broadcasted_iota(jnp.int32, (CUM_BLK, CUM_BLK), 0)
        c = lax.broadcasted_iota(jnp.int32, (CUM_BLK, CUM_BLK), 1)
        tri = (r <= c).astype(f32)
        lane = lax.broadcasted_iota(jnp.int32, (CUM_BLK, LANES), 1)
        col_ref[...] = d_ref[...].T
        carry = jnp.zeros((1, LANES), f32)
        total = jnp.zeros((1, LANES), f32)
        for blk in reversed(range(nblk)):
            rows = slice(blk * CUM_BLK, (blk + 1) * CUM_BLK)
            cs = jnp.dot(tri, col_ref[rows, :], precision=lax.Precision.HIGHEST, preferred_element_type=f32) + carry
            carry = cs[0:1, :]
            z = f_ref[rows, :] + b_ref[...]
            df = jnp.where(lane < N_FOX_HEADS, cs * _sigmoid(-z), 0.0)
            df_ref[rows, :] = df.astype(df_ref.dtype)
            total = total + _colsum(df)
        db_ref[...] = total

    return pl.pallas_call(
        body, grid=(1,), in_specs=[pl.BlockSpec((LANES, SEQ), lambda i: (0, 0)),
                                   pl.BlockSpec((SEQ, LANES), lambda i: (0, C_F // LANES)),
                                   pl.BlockSpec((1, LANES), lambda i: (0, 0))],
        out_specs=[pl.BlockSpec((SEQ, LANES), lambda i: (0, 0)), pl.BlockSpec((1, LANES), lambda i: (0, 0))],
        out_shape=(SDS((SEQ, LANES), bf16), SDS((1, LANES), f32)), name="fox_gate_bwd",
        scratch_shapes=[pltpu.VMEM((SEQ, LANES), f32)],
        compiler_params=_params(("arbitrary",)),
    )(dF_row, proj, b_pad)


def _nt(a, b):
    return lax.dot_general(a, b, (((1,), (1,)), ((), ())), preferred_element_type=f32)


def _tn(a, b):
    return lax.dot_general(a, b, (((0,), (0,)), ((), ())), preferred_element_type=f32)


def _fox_prep(proj, f_col):
    def fn(t, v):
        q, k, vv, fc = t
        lane = lax.broadcasted_iota(jnp.int32, (q.shape[0], LANES), 1)
        qs, ks = [], []
        for h in range(N_FOX_HEADS):
            pair, pos = divmod(h, 2)
            own = (lane >= pos * HEAD_DIM) & (lane < (pos + 1) * HEAD_DIM)
            base = (1 - pos) * HEAD_DIM
            f = fc[:, h:h + 1]
            hi = f.astype(bf16).astype(f32)
            mid = (f - hi).astype(bf16).astype(f32)
            lo = (f - hi) - mid
            one = jnp.ones_like(f)
            qa = jnp.where(own, q[:, pair * LANES:(pair + 1) * LANES] * ATT_SCALE, 0.0)
            ka = k[:, pair * LANES:(pair + 1) * LANES]
            for idx, (qv, kv) in enumerate([(hi, one), (mid, one), (lo, one), (one, -hi), (one, -mid), (one, -lo)]):
                sel = lane == base + idx
                qa = jnp.where(sel, qv, qa)
                ka = jnp.where(sel, kv, ka)
            qs.append(qa)
            ks.append(ka)
        return [jnp.concatenate(qs, axis=1), jnp.concatenate(ks, axis=1), vv], []
    w = N_FOX_HEADS * LANES
    return _rowwise(fn, "fox_prep", [(proj, FOX_W, C_QA // FOX_W), (proj, FOX_W, C_KA // FOX_W),
                                     (proj, FOX_W, C_VA // FOX_W), (f_col, LANES, 0)], [],
                    [(w, bf16), (w, bf16), (FOX_W, bf16)])


def _fox_fwd(q_aug, k_aug, v):
    blk = FOX_BLK
    npair = FOX_W // LANES

    def body(q_ref, k_ref, v_ref, o_ref, lse_ref, s_scr):
        i = pl.program_id(1)
        tri = lax.broadcasted_iota(jnp.int32, (blk, blk), 0) >= lax.broadcasted_iota(jnp.int32, (blk, blk), 1)
        qh = [q_ref[:, h * LANES:(h + 1) * LANES] for h in range(2)]

        def logits(c, masked):
            off = pl.multiple_of(c * blk, blk)
            tops = []
            for h in range(2):
                s = _nt(qh[h], k_ref[pl.ds(off, blk), h * LANES:(h + 1) * LANES])
                if masked:
                    s = jnp.where(tri, s, NEG)
                s_scr[h, :, pl.ds(off, blk)] = s
                tops.append(_fold_lanes(s, jnp.maximum))
            return tops

        def pass_a(c, m):
            return tuple(jnp.maximum(a, b) for a, b in zip(m, logits(c, False)))

        m = lax.fori_loop(0, i, pass_a, tuple(jnp.full((blk, LANES), NEG, f32) for _ in range(2)))
        mx = [jnp.max(jnp.maximum(a, b), axis=1, keepdims=True) for a, b in zip(m, logits(i, True))]

        def pass_b(c, carry):
            off = pl.multiple_of(c * blk, blk)
            vv = v_ref[pl.ds(off, blk), :]
            new = []
            for h in range(2):
                l, acc = carry[h]
                p = jnp.exp(s_scr[h, :, pl.ds(off, blk)] - mx[h])
                new.append((l + _fold_lanes(p, jnp.add),
                            acc + jnp.dot(p.astype(bf16), vv, preferred_element_type=f32)))
            return tuple(new)

        zero = jnp.zeros((blk, LANES), f32)
        (l_a, acc_a), (l_b, acc_b) = lax.fori_loop(0, i + 1, pass_b, ((zero, zero), (zero, zero)))
        l_a = jnp.sum(l_a, axis=1, keepdims=True)
        l_b = jnp.sum(l_b, axis=1, keepdims=True)
        first = lax.broadcasted_iota(jnp.int32, (blk, LANES), 1) < HEAD_DIM
        o_ref[...] = jnp.where(first, acc_a / l_a, acc_b / l_b)
        lse_ref[0] = jnp.where(first, mx[0] + jnp.log(l_a), mx[1] + jnp.log(l_b))

    return pl.pallas_call(
        body, grid=(npair, SEQ // blk),
        in_specs=[pl.BlockSpec((blk, 2 * LANES), lambda p, i: (i, p)),
                  pl.BlockSpec((SEQ, 2 * LANES), lambda p, i: (0, p)),
                  pl.BlockSpec((SEQ, LANES), lambda p, i: (0, p))],
        out_specs=[pl.BlockSpec((blk, LANES), lambda p, i: (i, p)),
                   pl.BlockSpec((1, blk, LANES), lambda p, i: (p, i, 0))],
        out_shape=(SDS((SEQ, FOX_W), f32), SDS((npair, SEQ, LANES), f32)), name="fox_fwd",
        scratch_shapes=[pltpu.VMEM((2, blk, SEQ), f32)],
        compiler_params=_params(("parallel", "arbitrary")),
    )(q_aug, k_aug, v)


def _fox_bwd(q_aug, k_aug, v, do, o, lse):
    blk = FOX_BLK
    npair = FOX_W // LANES
    nblk = SEQ // blk

    def body(q_ref, k_ref, v_ref, do_ref, o_ref, lse_ref, dq_ref, dk_ref, dv_ref, df_ref,
             dq_acc, delta_ref, res_ref):
        lane_s = lax.broadcasted_iota(jnp.int32, (SEQ, LANES), 1)
        prod = do_ref[...] * o_ref[...]
        d_a = jnp.sum(jnp.where(lane_s < HEAD_DIM, prod, 0.0), axis=1, keepdims=True)
        d_b = jnp.sum(jnp.where(lane_s >= HEAD_DIM, prod, 0.0), axis=1, keepdims=True)
        delta_ref[...] = jnp.where(lane_s < HEAD_DIM, d_a, d_b)
        dq_acc[...] = jnp.zeros_like(dq_acc)
        res_ref[...] = jnp.zeros_like(res_ref)
        df_ref[...] = jnp.zeros_like(df_ref)
        lane = lax.broadcasted_iota(jnp.int32, (blk, LANES), 1)
        own = [lane < HEAD_DIM, lane >= HEAD_DIM]
        tri = lax.broadcasted_iota(jnp.int32, (blk, blk), 0) >= lax.broadcasted_iota(jnp.int32, (blk, blk), 1)

        def q_slab(qoff, h):
            return q_ref[pl.ds(qoff, blk), h * LANES:(h + 1) * LANES]

        def probs(qoff, h, k_h, masked):
            s = _nt(q_slab(qoff, h), k_h)
            if masked:
                s = jnp.where(tri, s, NEG)
            return jnp.exp(s - lse_ref[0, pl.ds(qoff, blk), h * HEAD_DIM:h * HEAD_DIM + 1])

        def k_slabs(koff):
            return [k_ref[pl.ds(koff, blk), h * LANES:(h + 1) * LANES] for h in range(2)]

        def kv_step(kj, _):
            koff = pl.multiple_of(kj * blk, blk)
            k_aug = k_slabs(koff)
            k_own = [jnp.where(own[h], k_aug[h], jnp.zeros_like(k_aug[h])) for h in range(2)]
            vv = v_ref[pl.ds(koff, blk), :]
            v_own = [jnp.where(own[h], vv, jnp.zeros_like(vv)) for h in range(2)]

            def q_tile(qi, carry, masked):
                qoff = pl.multiple_of(qi * blk, blk)
                dd = do_ref[pl.ds(qoff, blk), :].astype(bf16)
                new, dq_add = [], None
                for h in range(2):
                    dk_h, dv_h, dcol = carry[h]
                    p = probs(qoff, h, k_aug[h], masked)
                    dl = p * (_nt(dd, v_own[h]) - delta_ref[pl.ds(qoff, blk), h * HEAD_DIM:h * HEAD_DIM + 1])
                    dlb = dl.astype(bf16)
                    part = jnp.dot(dlb, k_own[h], preferred_element_type=f32)
                    dq_add = part if dq_add is None else dq_add + part
                    res_ref[h, pl.ds(qoff, blk), :] += _fold_lanes(dl, jnp.add)
                    new.append((dk_h + _tn(dlb, q_slab(qoff, h)), dv_h + _tn(p.astype(bf16), dd),
                                dcol + _colsum(dl)))
                dq_acc[pl.ds(qoff, blk), :] += dq_add * ATT_SCALE
                return tuple(new)

            zero = (jnp.zeros((blk, LANES), f32), jnp.zeros((blk, LANES), f32), jnp.zeros((1, blk), f32))
            carry = q_tile(kj, (zero, zero), True)
            (dk_a, dv_a, dcol_a), (dk_b, dv_b, dcol_b) = lax.fori_loop(
                kj + 1, nblk, lambda qi, cr: q_tile(qi, cr, False), carry)
            dk_ref[pl.ds(koff, blk), :] = jnp.where(own[0], dk_a, dk_b).astype(dk_ref.dtype)
            dv_ref[pl.ds(koff, blk), :] = jnp.where(own[0], dv_a, dv_b).astype(dv_ref.dtype)
            df_ref[0, 0:1, pl.ds(koff, blk)] = -dcol_a
            df_ref[0, 1:2, pl.ds(koff, blk)] = -dcol_b
            return 0

        lax.fori_loop(0, nblk, kv_step, 0)
        dq_ref[...] = dq_acc[...].astype(dq_ref.dtype)

        for h in range(2):
            res_ref[h] = jnp.zeros((SEQ, LANES), f32) + jnp.sum(res_ref[h], axis=1, keepdims=True)

        def kv_fix(kj, _):
            koff = pl.multiple_of(kj * blk, blk)
            k_aug = k_slabs(koff)

            def q_fix(qi, corr, masked):
                qoff = pl.multiple_of(qi * blk, blk)
                return tuple(corr[h] + _colsum(probs(qoff, h, k_aug[h], masked) * res_ref[h, pl.ds(qoff, blk), 0:1])
                             for h in range(2))

            zero = jnp.zeros((1, blk), f32)
            corr = lax.fori_loop(kj + 1, nblk, lambda qi, cr: q_fix(qi, cr, False), q_fix(kj, (zero, zero), True))
            df_ref[0, 0:1, pl.ds(koff, blk)] += corr[0]
            df_ref[0, 1:2, pl.ds(koff, blk)] += corr[1]
            return 0

        lax.fori_loop(0, nblk, kv_fix, 0)

    pair_aug = pl.BlockSpec((SEQ, 2 * LANES), lambda p: (0, p))
    slab = pl.BlockSpec((SEQ, LANES), lambda p: (0, p))
    per_pair = pl.BlockSpec((1, SEQ, LANES), lambda p: (p, 0, 0))
    rows = pl.BlockSpec((1, 8, SEQ), lambda p: (p, 0, 0))
    return pl.pallas_call(
        body, grid=(npair,),
        in_specs=[pair_aug, pair_aug, slab, slab, slab, per_pair],
        out_specs=[slab, slab, slab, rows],
        out_shape=(SDS((SEQ, FOX_W), bf16),) * 3 + (SDS((npair, 8, SEQ), f32),), name="fox_bwd",
        scratch_shapes=[pltpu.VMEM((SEQ, LANES), f32), pltpu.VMEM((SEQ, LANES), f32),
                        pltpu.VMEM((2, SEQ, LANES), f32)],
        compiler_params=_params(("parallel",)),
    )(q_aug, k_aug, v, do, o, lse)


DIL_BLK = 128
N_GROUPS = 3
DIL_PAIRS = DIL_OUT_W // LANES
DIL_NBLK = SEQ // DIL_BLK


def _blocks_per_seq(g):
    return jnp.where(g == 0, 16, jnp.where(g == 1, 4, 1))


def _dil_block(n, g):
    rows = slice(n * DIL_BLK, (n + 1) * DIL_BLK)
    if n == 0:
        r = lax.broadcasted_iota(jnp.int32, (DIL_BLK, DIL_BLK), 0)
        c = lax.broadcasted_iota(jnp.int32, (DIL_BLK, DIL_BLK), 1)
        return rows, rows, r >= c
    r = lax.broadcasted_iota(jnp.int32, (DIL_BLK, 2 * DIL_BLK), 0)
    c = lax.broadcasted_iota(jnp.int32, (DIL_BLK, 2 * DIL_BLK), 1)
    has_prev = (n & (_blocks_per_seq(g) - 1)) > 0
    mask = ((c < DIL_BLK) & (c >= r) & has_prev) | ((c >= DIL_BLK) & (c - DIL_BLK <= r))
    return rows, slice((n - 1) * DIL_BLK, (n + 1) * DIL_BLK), mask


def _dil_spec():
    return pl.BlockSpec((1, SEQ, LANES), lambda g, p: (g, 0, p))


def _dil_fwd(q, k, v):
    def body(q_ref, k_ref, v_ref, o_ref, lse_ref):
        g = pl.program_id(0)
        first = lax.broadcasted_iota(jnp.int32, (DIL_BLK, LANES), 1) < HEAD_DIM
        for n in range(DIL_NBLK):
            rows, krows, mask = _dil_block(n, g)
            qv, kk, vv = q_ref[0, rows, :], k_ref[0, krows, :], v_ref[0, krows, :]
            outs, lses = [], []
            for own in (first, ~first):
                s = jnp.where(mask, _nt(jnp.where(own, qv, jnp.zeros_like(qv)), kk), NEG)
                m = jnp.max(s, axis=1, keepdims=True)
                p = jnp.exp(s - m)
                l = jnp.sum(p, axis=1, keepdims=True)
                outs.append(jnp.dot(p.astype(bf16), vv, preferred_element_type=f32) / l)
                lses.append(m + jnp.log(l))
            o_ref[0, rows, :] = jnp.where(first, outs[0], outs[1])
            lse_ref[0, rows, :] = jnp.where(first, lses[0], lses[1])

    spec = _dil_spec()
    shape = SDS((N_GROUPS, SEQ, DIL_OUT_W), f32)
    return pl.pallas_call(
        body, grid=(N_GROUPS, DIL_PAIRS), in_specs=[spec] * 3, out_specs=[spec] * 2,
        out_shape=(shape, shape), name="dil_fwd", compiler_params=_params(("parallel", "parallel")),
    )(q, k, v)


def _dil_bwd(q, k, v, do, lse, delta):
    def body(q_ref, k_ref, v_ref, do_ref, lse_ref, dl_ref, dq_ref, dk_ref, dv_ref):
        g = pl.program_id(0)
        first = lax.broadcasted_iota(jnp.int32, (DIL_BLK, LANES), 1) < HEAD_DIM
        dk_ref[...] = jnp.zeros_like(dk_ref)
        dv_ref[...] = jnp.zeros_like(dv_ref)
        for n in range(DIL_NBLK):
            rows, krows, mask = _dil_block(n, g)
            qv, kk, vv = q_ref[0, rows, :], k_ref[0, krows, :], v_ref[0, krows, :]
            dov = do_ref[0, rows, :].astype(bf16)
            lsev, delv = lse_ref[0, rows, :], dl_ref[0, rows, :]
            dqs, dk_add, dv_add = [], None, None
            for h, own in enumerate((first, ~first)):
                col = h * HEAD_DIM
                qh = jnp.where(own, qv, jnp.zeros_like(qv))
                doh = jnp.where(own, dov, jnp.zeros_like(dov))
                p = jnp.exp(jnp.where(mask, _nt(qh, kk), NEG) - lsev[:, col:col + 1])
                dl = (p * (_nt(doh, vv) - delv[:, col:col + 1])).astype(bf16)
                dqs.append(jnp.dot(dl, kk, preferred_element_type=f32))
                dk_h, dv_h = _tn(dl, qh), _tn(p.astype(bf16), doh)
                dk_add = dk_h if dk_add is None else dk_add + dk_h
                dv_add = dv_h if dv_add is None else dv_add + dv_h
            dq_ref[0, rows, :] = jnp.where(first, dqs[0], dqs[1]) * ATT_SCALE
            dk_ref[0, krows, :] += dk_add
            dv_ref[0, krows, :] += dv_add

    spec = _dil_spec()
    shape = SDS((N_GROUPS, SEQ, DIL_OUT_W), f32)
    return pl.pallas_call(
        body, grid=(N_GROUPS, DIL_PAIRS), in_specs=[spec] * 6, out_specs=[spec] * 3,
        out_shape=(shape, shape, shape), name="dil_bwd", compiler_params=_params(("parallel", "parallel")),
    )(q, k, v, do, lse, delta)


_DILATIONS = (1, 4, 16)


def _to_classes(t):
    w = t.shape[1] // N_GROUPS
    out = []
    for g, d in enumerate(_DILATIONS):
        s = t[:, g * w:(g + 1) * w]
        out.append(s.reshape(SEQ // d, d, w).transpose(1, 0, 2).reshape(SEQ, w))
    return jnp.stack(out)


def _from_classes(t):
    w = t.shape[2]
    out = [t[g].reshape(d, SEQ // d, w).transpose(1, 0, 2).reshape(SEQ, w) for g, d in enumerate(_DILATIONS)]
    return jnp.concatenate(out, axis=1)


def _position():
    return lax.axis_index("x"), lax.axis_index("y"), lax.axis_index("c")


def _all_gather(block, name):
    def body(x_ref, out_ref, send_sems, recv_sems, local_sem):
        x, y, c = _position()
        me, sibling = (x, y, c), (x, y, 1 - c)
        chips = [(1 - x, y), (x, 1 - y), (1 - x, 1 - y)]

        def slot(px, py, pc):
            return out_ref.at[4 * px + 2 * py + pc]

        def copy(k, blk, to, src=None):
            return pltpu.make_async_remote_copy(
                src_ref=slot(*blk) if src is None else src, dst_ref=slot(*blk),
                send_sem=send_sems.at[k], recv_sem=recv_sems.at[k], device_id=to, device_id_type=MESH)

        mine = pltpu.make_async_copy(x_ref, slot(*me), local_sem)
        mine.start()
        first = [copy(0, me, sibling, src=x_ref)]
        first += [copy(1 + j, me, (*chip, c), src=x_ref) for j, chip in enumerate(chips)]
        for cp in first:
            cp.start()
        passed = [copy(4 + j, (*chip, c), sibling) for j, chip in enumerate(chips)]
        for j, chip in enumerate(chips):
            copy(1 + j, (*chip, c), me).wait_recv()
            passed[j].start()
        copy(0, sibling, me).wait_recv()
        for j, chip in enumerate(chips):
            copy(4 + j, (*chip, 1 - c), me).wait_recv()
        for cp in first + passed:
            cp.wait_send()
        mine.wait()

    return pl.pallas_call(
        body, out_shape=SDS((N_DEV,) + block.shape, block.dtype),
        in_specs=[pl.BlockSpec(memory_space=pl.ANY)], out_specs=pl.BlockSpec(memory_space=pl.ANY),
        scratch_shapes=[pltpu.SemaphoreType.DMA((7,)), pltpu.SemaphoreType.DMA((7,)), pltpu.SemaphoreType.DMA],
        name=name,
    )(block)


def _all_gather_many(blocks, name):
    n = len(blocks)

    def body(*refs):
        x_refs, out_refs = refs[:n], refs[n:2 * n]
        send_sems, recv_sems, local_sems = refs[2 * n:]
        x, y, c = _position()
        me, sibling = (x, y, c), (x, y, 1 - c)
        chips = [(1 - x, y), (x, 1 - y), (1 - x, 1 - y)]

        def slot(a, px, py, pc):
            return out_refs[a].at[4 * px + 2 * py + pc]

        def copy(a, k, blk, to, own=False):
            return pltpu.make_async_remote_copy(
                src_ref=x_refs[a] if own else slot(a, *blk), dst_ref=slot(a, *blk),
                send_sem=send_sems.at[a, k], recv_sem=recv_sems.at[a, k], device_id=to, device_id_type=MESH)

        mine = [pltpu.make_async_copy(x_refs[a], slot(a, *me), local_sems.at[a]) for a in range(n)]
        for cp in mine:
            cp.start()
        started = []
        for a in range(n):
            first = [copy(a, 0, me, sibling, own=True)]
            first += [copy(a, 1 + j, me, (*chip, c), own=True) for j, chip in enumerate(chips)]
            for cp in first:
                cp.start()
            started += first
        for a in range(n):
            for j, chip in enumerate(chips):
                copy(a, 1 + j, (*chip, c), me).wait_recv()
                passed = copy(a, 4 + j, (*chip, c), sibling)
                passed.start()
                started.append(passed)
        for a in range(n):
            copy(a, 0, sibling, me).wait_recv()
            for j, chip in enumerate(chips):
                copy(a, 4 + j, (*chip, 1 - c), me).wait_recv()
        for cp in started:
            cp.wait_send()
        for cp in mine:
            cp.wait()

    hbm = pl.BlockSpec(memory_space=pl.ANY)
    return pl.pallas_call(
        body, out_shape=[SDS((N_DEV,) + b.shape, b.dtype) for b in blocks],
        in_specs=[hbm] * n, out_specs=[hbm] * n,
        scratch_shapes=[pltpu.SemaphoreType.DMA((n, 7)), pltpu.SemaphoreType.DMA((n, 7)),
                        pltpu.SemaphoreType.DMA((n,))],
        name=name,
    )(*blocks)


def _pair_exchange(gs, name):
    n = len(gs)

    def body(*refs):
        g_refs, r_refs, send_sems, recv_sems = refs[:n], refs[n:2 * n], refs[2 * n], refs[2 * n + 1]
        x, y, c = _position()
        copies = []
        for a in range(n):
            for k in range(4):
                cp = pltpu.make_async_remote_copy(
                    src_ref=g_refs[a].at[2 * k + (1 - c)], dst_ref=r_refs[a].at[k], send_sem=send_sems.at[a, k],
                    recv_sem=recv_sems.at[a, k], device_id=(x, y, 1 - c), device_id_type=MESH)
                cp.start()
                copies.append(cp)
        for cp in copies:
            cp.wait()

    hbm = pl.BlockSpec(memory_space=pl.ANY)
    return pl.pallas_call(
        body, out_shape=[SDS((4,) + g.shape[1:], g.dtype) for g in gs], in_specs=[hbm] * n, out_specs=[hbm] * n,
        scratch_shapes=[pltpu.SemaphoreType.DMA((n, 4)), pltpu.SemaphoreType.DMA((n, 4))], name=name,
    )(*gs)


HBM_SPEC = pl.BlockSpec(memory_space=pltpu.HBM)
SEM_SPEC = pl.BlockSpec(memory_space=pltpu.SEMAPHORE)
SPLIT_COPY = pltpu.CompilerParams(has_side_effects=pltpu.SideEffectType.DATAFLOW_SIDE_EFFECTING)


def _in_hbm(t):
    return pltpu.with_memory_space_constraint(t, pltpu.HBM)


def _chip_copies(t_refs, land_refs, send_sems, recv_sems):
    x, y, c = _position()
    chips = [(1 - x, y), (x, 1 - y), (1 - x, 1 - y)]
    return [pltpu.make_async_remote_copy(
        src_ref=t.at[2 * px + py], dst_ref=land.at[j], send_sem=send_sems.at[3 * a + j],
        recv_sem=recv_sems.at[3 * a + j], device_id=(px, py, c), device_id_type=MESH)
        for a, (t, land) in enumerate(zip(t_refs, land_refs, strict=True)) for j, (px, py) in enumerate(chips)]


def _chip_exchange_start(ts, name):
    n = len(ts)
    lands = [_in_hbm(lax.empty((3,) + t.shape[1:], t.dtype)) for t in ts]

    def body(*refs):
        send_sems, recv_sems = refs[2 * n], refs[2 * n + 1]
        for cp in _chip_copies(refs[:n], refs[n:2 * n], send_sems, recv_sems):
            cp.start()
        refs[-1][...] = jnp.zeros_like(refs[-1])

    sems = pltpu.SemaphoreType.DMA((3 * n,))
    res = pl.pallas_call(
        body, name=name, in_specs=[HBM_SPEC] * (2 * n),
        out_shape=(sems, sems, *[pltpu.HBM(t.shape, t.dtype) for t in (*ts, *lands)], SDS((8, LANES), f32)),
        out_specs=(SEM_SPEC, SEM_SPEC, *[HBM_SPEC] * (2 * n), pl.BlockSpec(memory_space=pltpu.VMEM)),
        input_output_aliases={i: 2 + i for i in range(2 * n)}, compiler_params=SPLIT_COPY,
    )(*[_in_hbm(t) for t in ts], *lands)
    return res[:-1], res[-1]


def _chip_exchange_wait(state, after, name):
    send_sems, recv_sems, *arrays = state
    n = len(arrays) // 2

    def body(*refs):
        for cp in _chip_copies(refs[:n], refs[n:2 * n], refs[2 * n], refs[2 * n + 1]):
            cp.wait_send()
            cp.wait_recv()

    res = pl.pallas_call(
        body, name=name, in_specs=[HBM_SPEC] * (2 * n) + [SEM_SPEC, SEM_SPEC, pl.BlockSpec(memory_space=pl.ANY)],
        out_shape=[pltpu.HBM(t.shape, t.dtype) for t in arrays], out_specs=[HBM_SPEC] * (2 * n),
        input_output_aliases={i: i for i in range(2 * n)}, compiler_params=SPLIT_COPY,
    )(*arrays, send_sems, recv_sems, after)
    return res[:n], res[n:]


def _gather_copies(x_refs, out_refs, send_sems, recv_sems):
    x, y, c = _position()
    peers = [(x, y, 1 - c), (1 - x, y, c), (x, 1 - y, c), (1 - x, 1 - y, c)]
    sends, arrivals = [], []
    for a, (x_ref, out_ref) in enumerate(zip(x_refs, out_refs, strict=True)):
        for k, (px, py, pc) in enumerate(peers):
            sems = dict(send_sem=send_sems.at[4 * a + k], recv_sem=recv_sems.at[4 * a + k],
                        device_id=(px, py, pc), device_id_type=MESH)
            sends.append(pltpu.make_async_remote_copy(src_ref=x_ref, dst_ref=out_ref.at[4 * x + 2 * y + c], **sems))
            arrivals.append(pltpu.make_async_remote_copy(src_ref=x_ref, dst_ref=out_ref.at[4 * px + 2 * py + pc],
                                                         **sems))
    return sends, arrivals


def _gather_start(blocks, after, name):
    n = len(blocks)
    outs = [_in_hbm(lax.empty((N_DEV,) + b.shape, b.dtype)) for b in blocks]

    def body(*refs):
        sends, _ = _gather_copies(refs[:n], refs[n:2 * n], refs[2 * n + 1], refs[2 * n + 2])
        for cp in sends:
            cp.start()
        refs[-1][...] = jnp.zeros_like(refs[-1])

    sems = pltpu.SemaphoreType.DMA((4 * n,))
    res = pl.pallas_call(
        body, name=name, in_specs=[HBM_SPEC] * (2 * n) + [pl.BlockSpec(memory_space=pl.ANY)],
        out_shape=(sems, sems, *[pltpu.HBM(t.shape, t.dtype) for t in (*blocks, *outs)], SDS((8, LANES), f32)),
        out_specs=(SEM_SPEC, SEM_SPEC, *[HBM_SPEC] * (2 * n), pl.BlockSpec(memory_space=pltpu.VMEM)),
        input_output_aliases={i: 2 + i for i in range(2 * n)}, compiler_params=SPLIT_COPY,
    )(*[_in_hbm(b) for b in blocks], *outs, after)
    return res[:-1], res[-1]


def _gather_wait(state, after, name):
    send_sems, recv_sems, *arrays = state
    n = len(arrays) // 2

    def body(*refs):
        sends, arrivals = _gather_copies(refs[:n], refs[n:2 * n], refs[2 * n], refs[2 * n + 1])
        for cp in sends:
            cp.wait_send()
        for cp in arrivals:
            cp.wait_recv()

    res = pl.pallas_call(
        body, name=name, in_specs=[HBM_SPEC] * (2 * n) + [SEM_SPEC, SEM_SPEC, pl.BlockSpec(memory_space=pl.ANY)],
        out_shape=[pltpu.HBM(t.shape, t.dtype) for t in arrays], out_specs=[HBM_SPEC] * (2 * n),
        input_output_aliases={i: i for i in range(2 * n)}, compiler_params=SPLIT_COPY,
    )(*arrays, send_sems, recv_sems, after)
    return res[:n], res[n:]


def _gather_finish(partial, name):
    n = len(partial)

    def body(*refs):
        in_refs, out_refs = refs[:n], refs[n:2 * n]
        send_sems, recv_sems = refs[2 * n:]
        x, y, c = _position()
        chips = [(1 - x, y), (x, 1 - y), (1 - x, 1 - y)]
        copies = []
        for a in range(n):
            for j, (px, py) in enumerate(chips):
                cp = pltpu.make_async_remote_copy(
                    src_ref=in_refs[a].at[4 * px + 2 * py + c], dst_ref=out_refs[a].at[4 * px + 2 * py + c],
                    send_sem=send_sems.at[a, j], recv_sem=recv_sems.at[a, j], device_id=(x, y, 1 - c),
                    device_id_type=MESH)
                cp.start()
                copies.append(cp)
        for a in range(n):
            for j, (px, py) in enumerate(chips):
                pltpu.make_async_remote_copy(
                    src_ref=in_refs[a].at[4 * px + 2 * py + (1 - c)], dst_ref=out_refs[a].at[4 * px + 2 * py + (1 - c)],
                    send_sem=send_sems.at[a, j], recv_sem=recv_sems.at[a, j], device_id=(x, y, 1 - c),
                    device_id_type=MESH).wait_recv()
        for cp in copies:
            cp.wait_send()

    hbm = pl.BlockSpec(memory_space=pl.ANY)
    return pl.pallas_call(
        body, out_shape=[SDS(p.shape, p.dtype) for p in partial], in_specs=[hbm] * n, out_specs=[hbm] * n,
        input_output_aliases={a: a for a in range(n)},
        scratch_shapes=[pltpu.SemaphoreType.DMA((n, 3)), pltpu.SemaphoreType.DMA((n, 3))],
        name=name,
    )(*partial)


def _row_tile(rows):
    return 256 if rows % 256 == 0 and rows > 512 else rows


def _pair_add(g, r1, core, name):
    def body(c_ref, g_ref, r_ref, o_ref):
        o_ref[...] = (g_ref[...].astype(f32) + r_ref[...].astype(f32)).astype(o_ref.dtype)

    rows, cols = g.shape[1:]
    tile = _row_tile(rows)
    blk = (1, tile, cols)
    return pl.pallas_call(
        body, out_shape=SDS((4, rows, cols), g.dtype), name=name,
        grid_spec=pltpu.PrefetchScalarGridSpec(
            num_scalar_prefetch=1, grid=(4, rows // tile),
            in_specs=[pl.BlockSpec(blk, lambda k, i, c_ref: (2 * k + c_ref[0], i, 0)),
                      pl.BlockSpec(blk, lambda k, i, c_ref: (k, i, 0))],
            out_specs=pl.BlockSpec(blk, lambda k, i, c_ref: (k, i, 0))),
        compiler_params=_params(("parallel", "arbitrary")),
    )(core, g, r1)


def _chip_add(t, r2, chip, name):
    def body(c_ref, t_ref, r_ref, o_ref):
        o_ref[...] = ((t_ref[0].astype(f32) + r_ref[0].astype(f32)) + r_ref[1].astype(f32)) + r_ref[2].astype(f32)

    rows, cols = t.shape[1:]
    tile = _row_tile(rows)
    return pl.pallas_call(
        body, out_shape=SDS((rows, cols), f32), name=name,
        grid_spec=pltpu.PrefetchScalarGridSpec(
            num_scalar_prefetch=1, grid=(rows // tile,),
            in_specs=[pl.BlockSpec((1, tile, cols), lambda i, c_ref: (c_ref[0], i, 0)),
                      pl.BlockSpec((3, tile, cols), lambda i, c_ref: (0, i, 0))],
            out_specs=pl.BlockSpec((tile, cols), lambda i, c_ref: (i, 0))),
        compiler_params=_params(("arbitrary",)),
    )(chip, t, r2)


def _pad_to(t, axis, size):
    pads = [(0, 0)] * t.ndim
    pads[axis] = (0, size - t.shape[axis])
    return jnp.pad(t, pads)


_REF_COLS = {"qa": (0, FOX_W), "ka": (FOX_W, FOX_W), "va": (2 * FOX_W, FOX_W), "f": (3 * FOX_W, N_FOX_HEADS)}
_REF_COLS.update({n: (3 * FOX_W + N_FOX_HEADS + i * DIL_W, DIL_W) for i, n in enumerate(("qb", "kb", "vb"))})
_REF_COLS.update({n: (3 * FOX_W + N_FOX_HEADS + 3 * DIL_W + i * D, D) for i, n in enumerate(("ga", "gb"))})
_REF_ORDER = ("qa", "ka", "va", "f", "qb", "kb", "vb", "ga", "gb")


def _shard_pad_cols(pieces):
    first = pieces[_REF_ORDER[0]]
    pad = jnp.zeros((first.shape[0], W_IN_PAD - W_IN_SH), first.dtype)
    parts, names, used = [], list(_REF_ORDER), 0
    for _ in range(N_DEV):
        need = W_IN_SH
        while need:
            take = min(need, _REF_COLS[names[0]][1] - used)
            parts.append(pieces[names[0]][:, used:used + take])
            need, used = need - take, used + take
            if used == _REF_COLS[names[0]][1]:
                names, used = names[1:], 0
        parts.append(pad)
    return jnp.concatenate(parts, axis=1)


def _slab_w_in(stack):
    def cols(name):
        lo, width = _REF_COLS[name]
        hi, out = lo + width, []
        while lo < hi:
            j, off = divmod(lo, W_IN_SH)
            n = min(hi - lo, W_IN_SH - off)
            out.append(stack[j, :, off:off + n])
            lo += n
        return out
    z = lambda n: [jnp.zeros((stack.shape[1], n), stack.dtype)]
    parts = (cols("ga") + cols("gb") + z(C_QB - 2 * D) + cols("qb") + cols("kb") + cols("vb") + cols("qa")
             + cols("ka") + cols("va") + cols("f") + z(LANES - N_FOX_HEADS))
    return jnp.concatenate(parts, axis=1)


def kernel(x, c, w_ada, b_ada, g_mix, w_in, b_fgate, w_br_a, w_br_b, w_out, g_ffn, w_ffn_gate, w_ffn_up, w_ffn_down, g_final, loss_target, m_w_ada, m_b_ada, m_g_mix, m_w_in, m_b_fgate, m_w_br_a, m_w_br_b, m_w_out, m_g_ffn, m_w_ffn_gate, m_w_ffn_up, m_w_ffn_down, m_g_final, v_w_ada, v_b_ada, v_g_mix, v_w_in, v_b_fgate, v_w_br_a, v_w_br_b, v_w_out, v_g_ffn, v_w_ffn_gate, v_w_ffn_up, v_w_ffn_down, v_g_final):
    px, py, pc = _position()
    dev = 4 * px + 2 * py + pc
    x2d, tgt = x[0], loss_target[0]

    c_all = _all_gather(c, "gather_c").reshape(N_DEV, D)
    ada_cols = w_ada.shape[2]
    b_shard = lax.dynamic_slice(b_ada, (0, dev * ada_cols), (1, ada_cols))
    mod_shard = _ada_fwd(c_all, w_ada[0], b_shard)
    mod_all = _all_gather(mod_shard, "gather_mod")
    modv = lax.dynamic_index_in_dim(mod_all, dev, axis=1, keepdims=False).reshape(6, D)

    gate_up = jnp.concatenate([_pad_to(w_ffn_gate[0], 1, FF_PAD), _pad_to(w_ffn_up[0], 1, FF_PAD)], axis=1)
    w_in_s, = _all_gather_many([_pad_to(w_in[0], 1, W_IN_PAD).astype(bf16)], "gather_w_in")
    later = [w_br_a[0], w_br_b[0], w_out[0], gate_up, _pad_to(w_ffn_down[0], 0, FF_PAD)]
    later_state, later_token = _gather_start([t.astype(bf16) for t in later], w_in_s, "gather_rest_start")
    w_in_p = _slab_w_in(w_in_s)

    h1 = _pre1(x2d, modv, g_mix)
    proj = _matmul(h1, w_in_p, name="mm_proj", tm=SEQ, tn=896, tk=D, after=later_token)
    b_pad = jnp.pad(b_fgate, ((0, 0), (0, LANES - N_FOX_HEADS)))
    q_aug, k_aug, va = _fox_prep(proj, _fox_gate_fwd(proj, b_pad))
    ya_h, lse_a = _fox_fwd(q_aug, k_aug, va)

    tables = _rope_tables()
    qb_r, kb_r, vb = _rope_fwd(proj, tables)
    q_c, k_c, v_c = _to_classes(qb_r), _to_classes(kb_r), _to_classes(vb)
    o_c, lse_c = _dil_fwd(q_c, k_c, v_c)
    yb_h, lse_b = _dil_combine(_from_classes(o_c), _from_classes(lse_c))

    mine, arrived = _gather_wait(later_state, yb_h, "gather_rest_wait")
    w_a_s, w_b_s, w_o_s, w_gu_s, w_d_s = [
        lax.dynamic_update_slice(stack, block[None], (dev, 0, 0))
        for stack, block in zip(_gather_finish(arrived, "gather_rest_finish"), mine, strict=True)]
    w_o = w_o_s.reshape(D, D)
    w_d = w_d_s.reshape(FF_HID, D)
    ya = _matmul(ya_h, w_a_s, by_shard=True, name="mm_br_a", tm=SEQ, tn=W_BR_SH, tk=FOX_W)
    yb = _matmul(yb_h, w_b_s, by_shard=True, name="mm_br_b", tm=SEQ, tn=W_BR_SH, tk=DIL_OUT_W)

    merged = _merge_fwd(ya, yb, proj)
    mix = _matmul(merged, w_o, name="mm_out", tm=SEQ, tn=512, tk=D)
    x1, h2 = _post1(x2d, mix, modv, g_ffn)
    au = _matmul(h2, w_gu_s, by_shard=True, name="mm_ffn_in", tm=SEQ, tn=2 * FF_PAD, tk=D)
    act = _swiglu_fwd(au)
    ff = _matmul(act, w_d, name="mm_ffn_down", tm=SEQ, tn=512, tk=FF_HID // 2)

    dx2, dff, dg_final, dga_f, loss_lanes = _final(x1, ff, tgt, modv, g_final.reshape(1, D))
    dact = _matmul(dff, w_d, tb=True, name="mm_d_act", tm=SEQ // 2, tn=FF_HID // 2, tk=D)
    dau = _swiglu_bwd(au, dact)

    core = pc.astype(jnp.int32).reshape(1)
    chip = (2 * px + py).astype(jnp.int32).reshape(1)

    def to_chips(by_dev, tags, name):
        from_pair = _pair_exchange(by_dev, "pair_exchange_" + name)
        sums = [_pair_add(g, r, core, "pair_add_" + t) for g, r, t in zip(by_dev, from_pair, tags)]
        return _chip_exchange_start(sums, "chip_exchange_start_" + name)

    def from_chips(state, after, tags, name):
        sums, got = _chip_exchange_wait(state, after, "chip_exchange_wait_" + name)
        return [_chip_add(p, r, chip, "chip_add_" + t) for p, r, t in zip(sums, got, tags)]

    g_gu = _matmul(h2, dau, ta=True, by_shard=True, out_dtype=bf16, name="mm_g_ffn_in", tm=D, tn=2 * FF_PAD, tk=SEQ)
    g_d = _matmul(act, dff, ta=True, out_dtype=bf16, name="mm_g_down", tm=FF_HID // 2, tn=512, tk=SEQ)
    ffn_tags = ["gu", "down"]
    ffn_state, ffn_token = to_chips([g_gu, g_d.reshape(N_DEV, FF_PAD, D)], ffn_tags, "ffn")

    dh2 = _matmul(dau, w_gu_s, tb=True, by_shard=True, name="mm_d_h2", tm=SEQ // 2, tn=D, tk=2 * FF_PAD,
                  after=ffn_token)
    dx1, dmix, dsh_f, dsc_f, dg_ffn, dga_m = _mid_bwd(dh2, x1, dx2, mix, modv, g_ffn)
    dmerged = _matmul(dmix, w_o, tb=True, name="mm_d_merged", tm=SEQ, tn=512, tk=D)
    dya, dyb, dga, dgb = _merge_bwd(dmerged, ya, yb, proj)
    dya_h = _matmul(dya, w_a_s, tb=True, by_shard=True, name="mm_d_ya", tm=SEQ, tn=FOX_W, tk=W_BR_SH)
    dyb_h = _matmul(dyb, w_b_s, tb=True, by_shard=True, name="mm_d_yb", tm=SEQ, tn=DIL_OUT_W, tk=W_BR_SH)

    dqa, dka, dva, dF = _fox_bwd(q_aug, k_aug, va, dya_h, ya_h, lse_a)
    dF_row = jnp.pad(dF[:, :2, :].reshape(N_FOX_HEADS, SEQ), ((0, LANES - N_FOX_HEADS), (0, 0)))
    df, db_fgate = _fox_gate_bwd(dF_row, proj, b_pad)

    delta_b = _dil_delta(dyb_h, yb_h)
    rep = lambda t: _to_classes(jnp.tile(t, (1, N_GROUPS)))
    dq_c, dk_c, dv_c = _dil_bwd(q_c, k_c, v_c, rep(dyb_h), rep(lse_b), rep(delta_b))
    dqb, dkb = _rope_bwd(_from_classes(dq_c), _from_classes(dk_c), tables)
    dvb = _from_classes(dv_c).astype(bf16)

    dproj = _shard_pad_cols({"qa": dqa, "ka": dka, "va": dva, "f": df[:, :N_FOX_HEADS], "qb": dqb, "kb": dkb,
                             "vb": dvb, "ga": dga, "gb": dgb})
    dh1 = _matmul(dproj, w_in_s, tb=True, by_shard=True, name="mm_d_h1", tm=SEQ // 2, tn=D, tk=W_IN_PAD)
    grad_x, dsh_m, dsc_m, dg_mix = _first_bwd(dh1, x2d, dx1, modv, g_mix)

    pad_lane = lambda t: jnp.pad(t, ((0, 0), (0, D - t.shape[1])))
    small = jnp.concatenate([dsh_m, dsc_m, dga_m, dsh_f, dsc_f, dga_f, dg_mix, dg_ffn, dg_final,
                             pad_lane(db_fgate), loss_lanes, jnp.zeros((SMALL_ROWS - 11, D), f32)], axis=0)
    small_all = _all_gather(small, "gather_small")

    g_in = _matmul(h1, dproj, ta=True, by_shard=True, out_dtype=bf16, name="mm_g_in", tm=D, tn=W_IN_PAD, tk=SEQ,
                   after=small_all)
    g_o = _matmul(merged, dmix, ta=True, out_dtype=bf16, name="mm_g_out", tm=D, tn=512, tk=SEQ)
    g_a = _matmul(ya_h, dya, ta=True, by_shard=True, out_dtype=bf16, name="mm_g_br_a", tm=FOX_W, tn=W_BR_SH, tk=SEQ)
    g_b = _matmul(yb_h, dyb, ta=True, by_shard=True, out_dtype=bf16, name="mm_g_br_b", tm=DIL_OUT_W, tn=W_BR_SH,
                  tk=SEQ)
    rows_a, rows_b = FOX_W * W_BR_SH // D, DIL_OUT_W * W_BR_SH // D
    g_small = jnp.concatenate([g_a.reshape(N_DEV, rows_a, D), g_b.reshape(N_DEV, rows_b, D),
                               g_o.reshape(N_DEV, W_BR_SH, D)], axis=1)
    mix_tags = ["in", "small"]
    mix_state, mix_token = to_chips([g_in, g_small], mix_tags, "mixer")

    small_sum, loss_row = _small_reduce(small_all, mix_token)
    dmod_all = small_all[:, :6, :].reshape(N_DEV, 6 * D)
    g_w_ada = _ada_bwd(c_all, lax.dynamic_slice(dmod_all, (0, dev * ada_cols), (N_DEV, ada_cols)))
    s_gu, s_d = from_chips(ffn_state, small_sum, ffn_tags, "ffn")

    loss = loss_row[0, 0]
    g = {
        "w_ada": g_w_ada[None], "b_ada": small_sum[0:6].reshape(1, 6 * D), "g_mix": small_sum[6:7],
        "b_fgate": small_sum[9:10, :N_FOX_HEADS], "g_ffn": small_sum[7:8], "w_ffn_gate": s_gu[None, :, :W_FF_SH],
        "w_ffn_up": s_gu[None, :, FF_PAD:FF_PAD + W_FF_SH], "w_ffn_down": s_d[None, :W_FF_SH],
        "g_final": small_sum[8],
    }
    w = {"w_ada": w_ada, "b_ada": b_ada, "g_mix": g_mix, "w_in": w_in, "b_fgate": b_fgate, "w_br_a": w_br_a,
         "w_br_b": w_br_b, "w_out": w_out, "g_ffn": g_ffn, "w_ffn_gate": w_ffn_gate, "w_ffn_up": w_ffn_up,
         "w_ffn_down": w_ffn_down, "g_final": g_final}
    m = {"w_ada": m_w_ada, "b_ada": m_b_ada, "g_mix": m_g_mix, "w_in": m_w_in, "b_fgate": m_b_fgate,
         "w_br_a": m_w_br_a, "w_br_b": m_w_br_b, "w_out": m_w_out, "g_ffn": m_g_ffn, "w_ffn_gate": m_w_ffn_gate,
         "w_ffn_up": m_w_ffn_up, "w_ffn_down": m_w_ffn_down, "g_final": m_g_final}
    v = {"w_ada": v_w_ada, "b_ada": v_b_ada, "g_mix": v_g_mix, "w_in": v_w_in, "b_fgate": v_b_fgate,
         "w_br_a": v_w_br_a, "w_br_b": v_w_br_b, "w_out": v_w_out, "g_ffn": v_g_ffn, "w_ffn_gate": v_w_ffn_gate,
         "w_ffn_up": v_w_ffn_up, "w_ffn_down": v_w_ffn_down, "g_final": v_g_final}
    names = list(w)
    delta, new_m, new_v = {}, {}, {}

    def update(n):
        shape = w[n].shape
        two_d = (lambda t: t.reshape(shape[-2:])) if len(shape) == 3 else (lambda t: t)
        dl, mn, vn = _adamw(two_d(w[n]), two_d(g[n]), two_d(m[n]), two_d(v[n]), "adamw_" + n)
        delta[n], new_m[n], new_v[n] = dl.reshape(shape), mn.reshape(shape), vn.reshape(shape)

    for n in list(g):
        update(n)
    s_in, s_small = from_chips(mix_state, delta["w_ffn_down"], mix_tags, "mixer")
    g.update({"w_in": s_in[None, :, :W_IN_SH], "w_br_a": s_small[:rows_a].reshape(1, FOX_W, W_BR_SH),
              "w_br_b": s_small[rows_a:rows_a + rows_b].reshape(1, DIL_OUT_W, W_BR_SH),
              "w_out": s_small[None, rows_a + rows_b:]})
    for n in ("w_in", "w_br_a", "w_br_b", "w_out"):
        update(n)

    return (loss, grad_x[None], *[g[n] for n in names], *[delta[n] for n in names],
            *[new_m[n] for n in names], *[new_v[n] for n in names])
```

```python
import functools

import jax
import jax.numpy as jnp
from jax import lax
from jax.experimental import pallas as pl
from jax.experimental.pallas import tpu as pltpu

f32 = jnp.float32
bf16 = jnp.bfloat16
SDS = jax.ShapeDtypeStruct
MESH = pl.DeviceIdType.MESH

N_DEV = 8
D = 1024
SEQ = 2048
HEAD_DIM = 64
N_FOX_HEADS = 8
FOX_W = 512
DIL_W = 768
DIL_OUT_W = 256
ROT_DIM = 16
ROPE_THETA = 500000.0
D_FF = 2816
IN_COLS = 5896
EPS = 1e-6
NEG = -1e30
ATT_SCALE = HEAD_DIM ** -0.5

ADAM_LR = 0.001
ADAM_B1 = 0.9
ADAM_B2 = 0.999
ADAM_EPS = 1e-08
ADAM_WD = 0.01
ADAM_STEP = 10

C_GA, C_GB, C_QB, C_KB, C_VB, C_QA, C_KA, C_VA, C_F = 0, 1024, 2304, 3072, 3840, 4608, 5120, 5632, 6144
PROJ_W = 6272
LANES = 128
VMEM_LIMIT = 52 * 1024 * 1024

W_IN_SH, W_IN_PAD = IN_COLS // N_DEV, 768
W_BR_SH = D // N_DEV
W_FF_SH, FF_PAD = D_FF // N_DEV, 384
FF_HID = N_DEV * FF_PAD
SMALL_ROWS = 16


def _params(sem=None):
    if sem is None:
        return pltpu.CompilerParams(vmem_limit_bytes=VMEM_LIMIT)
    return pltpu.CompilerParams(dimension_semantics=sem, vmem_limit_bytes=VMEM_LIMIT)


def _rowwise(fn, name, tiled, vecs, outs, reds=(), tile=256):
    nt, nv, no = len(tiled), len(vecs), len(outs)
    rows = tiled[0][0].shape[0]
    assert rows % tile == 0

    def body(*refs):
        tin = [r[...] for r in refs[:nt]]
        vin = [r[...] for r in refs[nt:nt + nv]]
        orefs = refs[nt + nv:nt + nv + no]
        rrefs = refs[nt + nv + no:]
        touts, routs = fn(tin, vin)
        for r, t in zip(orefs, touts, strict=True):
            r[...] = t.astype(r.dtype)
        if rrefs:
            @pl.when(pl.program_id(0) == 0)
            def _():
                for r in rrefs:
                    r[...] = jnp.zeros_like(r)
            for r, t in zip(rrefs, routs, strict=True):
                r[...] += t

    def col_map(cb):
        return lambda i: (i, cb)

    def whole_map(nd):
        return lambda i: (0,) * nd

    in_specs = [pl.BlockSpec((tile, w), col_map(cb)) for (_, w, cb) in tiled]
    in_specs += [pl.BlockSpec(v.shape, whole_map(v.ndim)) for v in vecs]
    out_specs = [pl.BlockSpec((tile, w), lambda i: (i, 0)) for (w, _) in outs]
    out_specs += [pl.BlockSpec((1, w), lambda i: (0, 0)) for w in reds]
    out_shape = [SDS((rows, w), dt) for (w, dt) in outs] + [SDS((1, w), f32) for w in reds]
    res = pl.pallas_call(
        body, grid=(rows // tile,), in_specs=in_specs, out_specs=out_specs, out_shape=out_shape, name=name,
        compiler_params=_params(("arbitrary",)),
    )(*[t[0] for t in tiled], *vecs)
    return res


def _matmul(a, b, *, ta=False, tb=False, out_dtype=f32, name, tm, tn, tk, by_shard=False, after=None):
    m, k = (a.shape[1], a.shape[0]) if ta else a.shape
    if by_shard and not ta:
        n, kb = (b.shape[1], N_DEV * b.shape[2]) if tb else (N_DEV * b.shape[2], b.shape[1])
        assert (tk if tb else tn) == b.shape[2]
    else:
        n, kb = (b.shape[0], b.shape[1]) if tb else (b.shape[1], b.shape[0])
    assert kb == k and m % tm == 0 and n % tn == 0 and k % tk == 0
    nk = k // tk
    dims = (((0 if ta else 1,), (1 if tb else 0,)), ((), ()))
    b_stacked = by_shard and not ta
    o_stacked = by_shard and ta

    def body(a_ref, b_ref, *rest):
        o_ref, *acc = rest[1:] if after is not None else rest
        bv = b_ref[0] if b_stacked else b_ref[...]
        p = lax.dot_general(a_ref[...].astype(bf16), bv.astype(bf16), dims, preferred_element_type=f32)

        def put(val):
            if o_stacked:
                o_ref[0] = val.astype(o_ref.dtype)
            else:
                o_ref[...] = val.astype(o_ref.dtype)

        if nk == 1:
            put(p)
        else:
            acc_ref, = acc
            kk = pl.program_id(2)

            @pl.when(kk == 0)
            def _():
                acc_ref[...] = p

            @pl.when(kk > 0)
            def _():
                acc_ref[...] += p

            @pl.when(kk == nk - 1)
            def _():
                put(acc_ref[...])

    a_spec = pl.BlockSpec((tk, tm), lambda i, j, kk: (kk, i)) if ta else pl.BlockSpec((tm, tk), lambda i, j, kk: (i, kk))
    if b_stacked and tb:
        b_spec = pl.BlockSpec((1, tn, tk), lambda i, j, kk: (kk, j, 0))
    elif b_stacked:
        b_spec = pl.BlockSpec((1, tk, tn), lambda i, j, kk: (j, kk, 0))
    elif tb:
        b_spec = pl.BlockSpec((tn, tk), lambda i, j, kk: (j, kk))
    else:
        b_spec = pl.BlockSpec((tk, tn), lambda i, j, kk: (kk, j))
    if o_stacked:
        assert tn == n // N_DEV
        out_spec = pl.BlockSpec((1, tm, tn), lambda i, j, kk: (j, i, 0))
        out_shape = SDS((N_DEV, m, tn), out_dtype)
    else:
        out_spec = pl.BlockSpec((tm, tn), lambda i, j, kk: (i, j))
        out_shape = SDS((m, n), out_dtype)
    extra_specs, extra = ([pl.BlockSpec(memory_space=pl.ANY)], [after]) if after is not None else ([], [])
    return pl.pallas_call(
        body, grid=(m // tm, n // tn, nk), in_specs=[a_spec, b_spec] + extra_specs, out_specs=out_spec,
        out_shape=out_shape, name=name, scratch_shapes=[pltpu.VMEM((tm, tn), f32)] if nk > 1 else [],
        compiler_params=_params(("parallel", "parallel", "arbitrary")),
    )(a, b, *extra)


def _rms(x):
    r = lax.rsqrt(jnp.mean(x * x, axis=-1, keepdims=True) + EPS)
    return r, x * r


def _rms_bwd(r, xn, dxn):
    return r * (dxn - xn * jnp.mean(dxn * xn, axis=-1, keepdims=True))


def _colsum(t):
    return jnp.sum(t, axis=0, keepdims=True)


def _sigmoid(x):
    return 1.0 / (1.0 + jnp.exp(-x))


def _modulated_norm(x, g, shift, scale):
    _, xn = _rms(x)
    return (xn * g) * (1.0 + scale) + shift


def _pre1(x, modv, g_mix):
    def fn(t, v):
        (xt,), (mv, g) = t, v
        return [_modulated_norm(xt, g, mv[0:1], mv[1:2])], []
    return _rowwise(fn, "pre1", [(x, D, 0)], [modv, g_mix], [(D, bf16)])[0]


def _post1(x, mix, modv, g_ffn):
    def fn(t, v):
        (xt, mt), (mv, g) = t, v
        x1 = xt + mv[2:3] * mt
        return [x1, _modulated_norm(x1, g, mv[3:4], mv[4:5])], []
    return _rowwise(fn, "post1", [(x, D, 0), (mix, D, 0)], [modv, g_ffn], [(D, f32), (D, bf16)])


def _gate_up(au, j):
    base = 2 * j * FF_PAD
    return au[:, base:base + FF_PAD], au[:, base + FF_PAD:base + 2 * FF_PAD]


def _swiglu_fwd(au):
    def fn(t, v):
        acts = []
        for j in range(N_DEV):
            a, u = _gate_up(t[0], j)
            acts.append(a * _sigmoid(a) * u)
        return [jnp.concatenate(acts, axis=1)], []
    return _rowwise(fn, "swiglu_fwd", [(au, 2 * FF_HID, 0)], [], [(FF_HID, bf16)])[0]


def _swiglu_bwd(au, dact):
    def fn(t, v):
        parts = []
        for j in range(N_DEV):
            a, u = _gate_up(t[0], j)
            d = t[1][:, j * FF_PAD:(j + 1) * FF_PAD]
            sg = _sigmoid(a)
            parts += [d * u * (sg * (1.0 + a * (1.0 - sg))), d * (a * sg)]
        return [jnp.concatenate(parts, axis=1)], []
    return _rowwise(fn, "swiglu_bwd", [(au, 2 * FF_HID, 0), (dact, FF_HID, 0)], [], [(2 * FF_HID, bf16)],
                    tile=128)[0]


def _final(x1, ff, target, modv, g_final):
    def fn(t, v):
        (x1t, fft, tgt), (mv, g) = t, v
        x2 = x1t + mv[5:6] * fft
        r, xn = _rms(x2)
        err = xn * g - tgt
        dy = err * (1.0 / D)
        dx2 = _rms_bwd(r, xn, dy * g)
        return [dx2, dx2 * mv[5:6]], [_colsum(dy * xn), _colsum(dx2 * fft), _colsum(err * err) * (0.5 / D)]
    return _rowwise(fn, "final", [(x1, D, 0), (ff, D, 0), (target, D, 0)], [modv, g_final],
                    [(D, f32), (D, bf16)], [D, D, D])


def _mid_bwd(dh2, x1, dx2, mix, modv, g_ffn):
    def fn(t, v):
        (dh, x1t, dx2t, mt), (mv, g) = t, v
        r, xn = _rms(x1t)
        dn = dh * (1.0 + mv[4:5])
        dx1 = dx2t + _rms_bwd(r, xn, dn * g)
        return [dx1, dx1 * mv[2:3]], [_colsum(dh), _colsum(dh * (xn * g)), _colsum(dn * xn), _colsum(dx1 * mt)]
    return _rowwise(fn, "mid_bwd", [(dh2, D, 0), (x1, D, 0), (dx2, D, 0), (mix, D, 0)], [modv, g_ffn],
                    [(D, f32), (D, bf16)], [D, D, D, D])


def _first_bwd(dh1, x, dx1, modv, g_mix):
    def fn(t, v):
        (dh, xt, dx1t), (mv, g) = t, v
        r, xn = _rms(xt)
        dn = dh * (1.0 + mv[1:2])
        return [dx1t + _rms_bwd(r, xn, dn * g)], [_colsum(dh), _colsum(dh * (xn * g)), _colsum(dn * xn)]
    return _rowwise(fn, "first_bwd", [(dh1, D, 0), (x, D, 0), (dx1, D, 0)], [modv, g_mix], [(D, f32)], [D, D, D])


def _merge_fwd(ya, yb, proj):
    def fn(t, v):
        ya_t, yb_t, ga, gb = t
        return [_sigmoid(ga) * ya_t + _sigmoid(gb) * yb_t], []
    return _rowwise(fn, "merge_fwd", [(ya, D, 0), (yb, D, 0), (proj, D, C_GA // D), (proj, D, C_GB // D)], [],
                    [(D, bf16)])[0]


def _merge_bwd(dmerged, ya, yb, proj):
    def fn(t, v):
        dm, ya_t, yb_t, ga, gb = t
        sa, sb = _sigmoid(ga), _sigmoid(gb)
        return [dm * sa, dm * sb, dm * ya_t * (sa * (1.0 - sa)), dm * yb_t * (sb * (1.0 - sb))], []
    return _rowwise(fn, "merge_bwd",
                    [(dmerged, D, 0), (ya, D, 0), (yb, D, 0), (proj, D, C_GA // D), (proj, D, C_GB // D)], [],
                    [(D, bf16), (D, bf16), (D, bf16), (D, bf16)])


def _rope_tables():
    half = ROT_DIM // 2
    pos = jnp.arange(SEQ, dtype=f32)
    inv_freq = ROPE_THETA ** (-jnp.arange(0, ROT_DIM, 2, dtype=f32) / ROT_DIM)
    ang = pos[:, None] * inv_freq[None, :]
    cos, sin = jnp.cos(ang), jnp.sin(ang)
    pad = jnp.zeros((SEQ, HEAD_DIM - ROT_DIM), f32)
    zero = jnp.zeros((SEQ, half), f32)
    c_head = jnp.concatenate([cos, cos, pad + 1.0], axis=1)
    lo_head = jnp.concatenate([-sin, zero, pad], axis=1)
    hi_head = jnp.concatenate([zero, sin, pad], axis=1)
    return tuple(jnp.concatenate([t, t], axis=1) for t in (c_head, lo_head, hi_head))


def _over_heads(tables):
    return [jnp.tile(t, (1, DIL_W // LANES)) for t in tables]


def _rope_fwd(proj, tables):
    half = ROT_DIM // 2

    def fn(t, v):
        q, k, vv = t[:3]
        c, lo, hi = _over_heads(t[3:])
        rot = lambda z: z * c + pltpu.roll(z, DIL_W - half, 1) * lo + pltpu.roll(z, half, 1) * hi
        return [rot(q) * ATT_SCALE, rot(k), vv], []
    return _rowwise(fn, "rope_fwd", [(proj, DIL_W, C_QB // DIL_W), (proj, DIL_W, C_KB // DIL_W),
                                     (proj, DIL_W, C_VB // DIL_W)] + [(tb, LANES, 0) for tb in tables], [],
                    [(DIL_W, f32)] * 3)


def _rope_bwd(dqs, dks, tables):
    half = ROT_DIM // 2

    def fn(t, v):
        dq_t, dk_t = jnp.concatenate(t[:N_GROUPS], axis=1), jnp.concatenate(t[N_GROUPS:2 * N_GROUPS], axis=1)
        c, lo, hi = _over_heads(t[2 * N_GROUPS:])
        rot_t = lambda z: z * c + pltpu.roll(z * lo, half, 1) + pltpu.roll(z * hi, DIL_W - half, 1)
        return [rot_t(dq_t), rot_t(dk_t)], []
    return _rowwise(fn, "rope_bwd", [(a, DIL_OUT_W, 0) for a in (*dqs, *dks)] + [(tb, LANES, 0) for tb in tables],
                    [], [(DIL_W, bf16), (DIL_W, bf16)])


def _head_bcast_sum(d):
    lane = lax.broadcasted_iota(jnp.int32, d.shape, 1)
    out = jnp.zeros_like(d)
    for h in range(d.shape[1] // HEAD_DIM):
        sel = (lane >= h * HEAD_DIM) & (lane < (h + 1) * HEAD_DIM)
        out = jnp.where(sel, jnp.sum(jnp.where(sel, d, 0.0), axis=1, keepdims=True), out)
    return out


def _dil_combine(outs, lses):
    def fn(t, v):
        o0, o1, o2, l0, l1, l2 = t
        m = jnp.maximum(jnp.maximum(l0, l1), l2)
        w0, w1, w2 = jnp.exp(l0 - m), jnp.exp(l1 - m), jnp.exp(l2 - m)
        tot = w0 + w1 + w2
        return [(w0 * o0 + w1 * o1 + w2 * o2) / tot, m + jnp.log(tot)], []
    w = DIL_OUT_W
    return _rowwise(fn, "dil_combine", [(t, w, 0) for t in (*outs, *lses)], [], [(w, f32), (w, f32)])


def _dil_delta(dyb_h, yb_h):
    def fn(t, v):
        return [_head_bcast_sum(t[0] * t[1])], []
    return _rowwise(fn, "dil_delta", [(dyb_h, DIL_OUT_W, 0), (yb_h, DIL_OUT_W, 0)], [], [(DIL_OUT_W, f32)])[0]


def _adamw(w, g, m, v, name):
    shape = w.shape
    if w.ndim == 1:
        w, g, m, v = (t.reshape(1, -1) for t in (w, g, m, v))
    rows, cols = w.shape
    tile = 256 if rows % 256 == 0 and rows > 512 else rows

    def fn(t, _):
        wt, gt, mt, vt = t
        mn = ADAM_B1 * mt + (1.0 - ADAM_B1) * gt
        vn = ADAM_B2 * vt + (1.0 - ADAM_B2) * (gt * gt)
        m_hat = mn / (1.0 - ADAM_B1 ** ADAM_STEP)
        v_hat = vn / (1.0 - ADAM_B2 ** ADAM_STEP)
        return [-ADAM_LR * (m_hat / (jnp.sqrt(v_hat) + ADAM_EPS) + ADAM_WD * wt), mn, vn], []
    delta, mn, vn = _rowwise(fn, name, [(w, cols, 0), (g, cols, 0), (m, cols, 0), (v, cols, 0)], [],
                             [(cols, f32)] * 3, tile=tile)
    return delta.reshape(shape), mn.reshape(shape), vn.reshape(shape)


def _ada_fwd(c_all, w_shard, b_shard):
    def body(c_ref, w_ref, b_ref, o_ref):
        cv = c_ref[...]
        sc = (cv * _sigmoid(cv)).astype(bf16)
        o_ref[...] = jnp.dot(sc, w_ref[...].astype(bf16), preferred_element_type=f32) + b_ref[...]
    return pl.pallas_call(body, out_shape=SDS((N_DEV, w_shard.shape[1]), f32), name="ada_fwd",
                          compiler_params=_params())(c_all, w_shard, b_shard)


def _ada_bwd(c_all, dmod_cols):
    def body(c_ref, d_ref, o_ref):
        cv = c_ref[...]
        sc = cv * _sigmoid(cv)
        o_ref[...] = lax.dot_general(sc, d_ref[...], (((0,), (0,)), ((), ())), precision=lax.Precision.HIGHEST,
                                     preferred_element_type=f32)
    return pl.pallas_call(body, out_shape=SDS((D, dmod_cols.shape[1]), f32), name="ada_bwd",
                          compiler_params=_params())(c_all, dmod_cols)


def _small_reduce(gathered, after):
    def body(g_ref, after_ref, o_ref, loss_ref):
        acc = g_ref[0]
        for d in range(1, N_DEV):
            acc = acc + g_ref[d]
        o_ref[...] = acc
        loss_ref[...] = jnp.zeros((1, LANES), f32) + jnp.sum(acc[10:11, :])
    return pl.pallas_call(body, out_shape=(SDS((SMALL_ROWS, D), f32), SDS((1, LANES), f32)), name="small_reduce",
                          in_specs=[pl.BlockSpec(memory_space=pltpu.VMEM), pl.BlockSpec(memory_space=pl.ANY)],
                          compiler_params=_params())(gathered, after)


FOX_BLK = 512
CUM_BLK = 128


def _fold_lanes(t, op):
    out = t[:, :LANES]
    for j in range(1, t.shape[1] // LANES):
        out = op(out, t[:, j * LANES:(j + 1) * LANES])
    return out


def _fox_gate_fwd(proj, b_pad):
    nblk = SEQ // CUM_BLK

    def body(f_ref, b_ref, col_ref):
        r = lax.broadcasted_iota(jnp.int32, (CUM_BLK, CUM_BLK), 0)
        c = lax.broadcasted_iota(jnp.int32, (CUM_BLK, CUM_BLK), 1)
        tri = (r >= c).astype(f32)
        carry = jnp.zeros((1, LANES), f32)
        for blk in range(nblk):
            z = f_ref[blk * CUM_BLK:(blk + 1) * CUM_BLK, :] + b_ref[...]
            logf = jnp.minimum(z, 0.0) - jnp.log1p(jnp.exp(-jnp.abs(z)))
            cs = jnp.dot(tri, logf, precision=lax.Precision.HIGHEST, preferred_element_type=f32) + carry
            col_ref[blk * CUM_BLK:(blk + 1) * CUM_BLK, :] = cs
            carry = cs[CUM_BLK - 1:CUM_BLK, :]

    return pl.pallas_call(
        body, grid=(1,), in_specs=[pl.BlockSpec((SEQ, LANES), lambda i: (0, C_F // LANES)),
                                   pl.BlockSpec((1, LANES), lambda i: (0, 0))],
        out_specs=pl.BlockSpec((SEQ, LANES), lambda i: (0, 0)),
        out_shape=SDS((SEQ, LANES), f32), name="fox_gate_fwd",
        compiler_params=_params(("arbitrary",)),
    )(proj, b_pad)


def _fox_gate_bwd(dF_row, proj, b_pad):
    nblk = SEQ // CUM_BLK

    def body(d_ref, f_ref, b_ref, df_ref, db_ref, col_ref):
        r = lax.broadcasted_iota(jnp.int32, (CUM_BLK, CUM_BLK), 0)
        c = lax.broadcasted_iota(jnp.int32, (CUM_BLK, CUM_BLK), 1)
        tri = (r <= c).astype(f32)
        lane = lax.broadcasted_iota(jnp.int32, (CUM_BLK, LANES), 1)
        col_ref[...] = d_ref[...].T
        carry = jnp.zeros((1, LANES), f32)
        total = jnp.zeros((1, LANES), f32)
        for blk in reversed(range(nblk)):
            rows = slice(blk * CUM_BLK, (blk + 1) * CUM_BLK)
            cs = jnp.dot(tri, col_ref[rows, :], precision=lax.Precision.HIGHEST, preferred_element_type=f32) + carry
            carry = cs[0:1, :]
            z = f_ref[rows, :] + b_ref[...]
            df = jnp.where(lane < N_FOX_HEADS, cs * _sigmoid(-z), 0.0)
            df_ref[rows, :] = df.astype(df_ref.dtype)
            total = total + _colsum(df)
        db_ref[...] = total

    return pl.pallas_call(
        body, grid=(1,), in_specs=[pl.BlockSpec((LANES, SEQ), lambda i: (0, 0)),
                                   pl.BlockSpec((SEQ, LANES), lambda i: (0, C_F // LANES)),
                                   pl.BlockSpec((1, LANES), lambda i: (0, 0))],
        out_specs=[pl.BlockSpec((SEQ, LANES), lambda i: (0, 0)), pl.BlockSpec((1, LANES), lambda i: (0, 0))],
        out_shape=(SDS((SEQ, LANES), bf16), SDS((1, LANES), f32)), name="fox_gate_bwd",
        scratch_shapes=[pltpu.VMEM((SEQ, LANES), f32)],
        compiler_params=_params(("arbitrary",)),
    )(dF_row, proj, b_pad)


def _nt(a, b):
    return lax.dot_general(a, b, (((1,), (1,)), ((), ())), preferred_element_type=f32)


def _tn(a, b):
    return lax.dot_general(a, b, (((0,), (0,)), ((), ())), preferred_element_type=f32)


def _fox_prep(proj, f_col):
    def fn(t, v):
        q, k, vv, fc = t
        lane = lax.broadcasted_iota(jnp.int32, (q.shape[0], LANES), 1)
        qs, ks = [], []
        for h in range(N_FOX_HEADS):
            pair, pos = divmod(h, 2)
            own = (lane >= pos * HEAD_DIM) & (lane < (pos + 1) * HEAD_DIM)
            base = (1 - pos) * HEAD_DIM
            f = fc[:, h:h + 1]
            hi = f.astype(bf16).astype(f32)
            mid = (f - hi).astype(bf16).astype(f32)
            lo = (f - hi) - mid
            one = jnp.ones_like(f)
            qa = jnp.where(own, q[:, pair * LANES:(pair + 1) * LANES] * ATT_SCALE, 0.0)
            ka = k[:, pair * LANES:(pair + 1) * LANES]
            for idx, (qv, kv) in enumerate([(hi, one), (mid, one), (lo, one), (one, -hi), (one, -mid), (one, -lo)]):
                sel = lane == base + idx
                qa = jnp.where(sel, qv, qa)
                ka = jnp.where(sel, kv, ka)
            qs.append(qa)
            ks.append(ka)
        return [jnp.concatenate(qs, axis=1), jnp.concatenate(ks, axis=1), vv], []
    w = N_FOX_HEADS * LANES
    return _rowwise(fn, "fox_prep", [(proj, FOX_W, C_QA // FOX_W), (proj, FOX_W, C_KA // FOX_W),
                                     (proj, FOX_W, C_VA // FOX_W), (f_col, LANES, 0)], [],
                    [(w, bf16), (w, bf16), (FOX_W, bf16)])


def _fox_fwd(q_aug, k_aug, v):
    blk = FOX_BLK
    npair = FOX_W // LANES

    def body(q_ref, k_ref, v_ref, o_ref, lse_ref, s_scr):
        i = pl.program_id(1)
        tri = lax.broadcasted_iota(jnp.int32, (blk, blk), 0) >= lax.broadcasted_iota(jnp.int32, (blk, blk), 1)
        qh = [q_ref[:, h * LANES:(h + 1) * LANES] for h in range(2)]

        def logits(c, masked):
            off = pl.multiple_of(c * blk, blk)
            tops = []
            for h in range(2):
                s = _nt(qh[h], k_ref[pl.ds(off, blk), h * LANES:(h + 1) * LANES])
                if masked:
                    s = jnp.where(tri, s, NEG)
                s_scr[h, :, pl.ds(off, blk)] = s
                tops.append(_fold_lanes(s, jnp.maximum))
            return tops

        def pass_a(c, m):
            return tuple(jnp.maximum(a, b) for a, b in zip(m, logits(c, False)))

        m = lax.fori_loop(0, i, pass_a, tuple(jnp.full((blk, LANES), NEG, f32) for _ in range(2)))
        mx = [jnp.max(jnp.maximum(a, b), axis=1, keepdims=True) for a, b in zip(m, logits(i, True))]

        def pass_b(c, carry):
            off = pl.multiple_of(c * blk, blk)
            vv = v_ref[pl.ds(off, blk), :]
            new = []
            for h in range(2):
                l, acc = carry[h]
                p = jnp.exp(s_scr[h, :, pl.ds(off, blk)] - mx[h])
                new.append((l + _fold_lanes(p, jnp.add),
                            acc + jnp.dot(p.astype(bf16), vv, preferred_element_type=f32)))
            return tuple(new)

        zero = jnp.zeros((blk, LANES), f32)
        (l_a, acc_a), (l_b, acc_b) = lax.fori_loop(0, i + 1, pass_b, ((zero, zero), (zero, zero)))
        l_a = jnp.sum(l_a, axis=1, keepdims=True)
        l_b = jnp.sum(l_b, axis=1, keepdims=True)
        first = lax.broadcasted_iota(jnp.int32, (blk, LANES), 1) < HEAD_DIM
        o_ref[...] = jnp.where(first, acc_a / l_a, acc_b / l_b)
        lse_ref[0] = jnp.where(first, mx[0] + jnp.log(l_a), mx[1] + jnp.log(l_b))

    return pl.pallas_call(
        body, grid=(npair, SEQ // blk),
        in_specs=[pl.BlockSpec((blk, 2 * LANES), lambda p, i: (i, p)),
                  pl.BlockSpec((SEQ, 2 * LANES), lambda p, i: (0, p)),
                  pl.BlockSpec((SEQ, LANES), lambda p, i: (0, p))],
        out_specs=[pl.BlockSpec((blk, LANES), lambda p, i: (i, p)),
                   pl.BlockSpec((1, blk, LANES), lambda p, i: (p, i, 0))],
        out_shape=(SDS((SEQ, FOX_W), f32), SDS((npair, SEQ, LANES), f32)), name="fox_fwd",
        scratch_shapes=[pltpu.VMEM((2, blk, SEQ), f32)],
        compiler_params=_params(("parallel", "arbitrary")),
    )(q_aug, k_aug, v)


def _fox_bwd(q_aug, k_aug, v, do, o, lse):
    blk = FOX_BLK
    npair = FOX_W // LANES
    nblk = SEQ // blk

    def body(q_ref, k_ref, v_ref, do_ref, o_ref, lse_ref, dq_ref, dk_ref, dv_ref, df_ref,
             dq_acc, delta_ref, res_ref):
        lane_s = lax.broadcasted_iota(jnp.int32, (SEQ, LANES), 1)
        prod = do_ref[...] * o_ref[...]
        d_a = jnp.sum(jnp.where(lane_s < HEAD_DIM, prod, 0.0), axis=1, keepdims=True)
        d_b = jnp.sum(jnp.where(lane_s >= HEAD_DIM, prod, 0.0), axis=1, keepdims=True)
        delta_ref[...] = jnp.where(lane_s < HEAD_DIM, d_a, d_b)
        dq_acc[...] = jnp.zeros_like(dq_acc)
        res_ref[...] = jnp.zeros_like(res_ref)
        df_ref[...] = jnp.zeros_like(df_ref)
        lane = lax.broadcasted_iota(jnp.int32, (blk, LANES), 1)
        own = [lane < HEAD_DIM, lane >= HEAD_DIM]
        tri = lax.broadcasted_iota(jnp.int32, (blk, blk), 0) >= lax.broadcasted_iota(jnp.int32, (blk, blk), 1)

        def q_slab(qoff, h):
            return q_ref[pl.ds(qoff, blk), h * LANES:(h + 1) * LANES]

        def probs(qoff, h, k_h, masked):
            s = _nt(q_slab(qoff, h), k_h)
            if masked:
                s = jnp.where(tri, s, NEG)
            return jnp.exp(s - lse_ref[0, pl.ds(qoff, blk), h * HEAD_DIM:h * HEAD_DIM + 1])

        def k_slabs(koff):
            return [k_ref[pl.ds(koff, blk), h * LANES:(h + 1) * LANES] for h in range(2)]

        def kv_step(kj, _):
            koff = pl.multiple_of(kj * blk, blk)
            k_aug = k_slabs(koff)
            k_own = [jnp.where(own[h], k_aug[h], jnp.zeros_like(k_aug[h])) for h in range(2)]
            vv = v_ref[pl.ds(koff, blk), :]
            v_own = [jnp.where(own[h], vv, jnp.zeros_like(vv)) for h in range(2)]

            def q_tile(qi, carry, masked):
                qoff = pl.multiple_of(qi * blk, blk)
                dd = do_ref[pl.ds(qoff, blk), :].astype(bf16)
                new, dq_add = [], None
                for h in range(2):
                    dk_h, dv_h, dcol = carry[h]
                    p = probs(qoff, h, k_aug[h], masked)
                    dl = p * (_nt(dd, v_own[h]) - delta_ref[pl.ds(qoff, blk), h * HEAD_DIM:h * HEAD_DIM + 1])
                    dlb = dl.astype(bf16)
                    part = jnp.dot(dlb, k_own[h], preferred_element_type=f32)
                    dq_add = part if dq_add is None else dq_add + part
                    res_ref[h, pl.ds(qoff, blk), :] += _fold_lanes(dl, jnp.add)
                    new.append((dk_h + _tn(dlb, q_slab(qoff, h)), dv_h + _tn(p.astype(bf16), dd),
                                dcol + _colsum(dl)))
                dq_acc[pl.ds(qoff, blk), :] += dq_add * ATT_SCALE
                return tuple(new)

            zero = (jnp.zeros((blk, LANES), f32), jnp.zeros((blk, LANES), f32), jnp.zeros((1, blk), f32))
            carry = q_tile(kj, (zero, zero), True)
            (dk_a, dv_a, dcol_a), (dk_b, dv_b, dcol_b) = lax.fori_loop(
                kj + 1, nblk, lambda qi, cr: q_tile(qi, cr, False), carry)
            dk_ref[pl.ds(koff, blk), :] = jnp.where(own[0], dk_a, dk_b).astype(dk_ref.dtype)
            dv_ref[pl.ds(koff, blk), :] = jnp.where(own[0], dv_a, dv_b).astype(dv_ref.dtype)
            df_ref[0, 0:1, pl.ds(koff, blk)] = -dcol_a
            df_ref[0, 1:2, pl.ds(koff, blk)] = -dcol_b
            return 0

        lax.fori_loop(0, nblk, kv_step, 0)
        dq_ref[...] = dq_acc[...].astype(dq_ref.dtype)

        for h in range(2):
            res_ref[h] = jnp.zeros((SEQ, LANES), f32) + jnp.sum(res_ref[h], axis=1, keepdims=True)

        def kv_fix(kj, _):
            koff = pl.multiple_of(kj * blk, blk)
            k_aug = k_slabs(koff)

            def q_fix(qi, corr, masked):
                qoff = pl.multiple_of(qi * blk, blk)
                return tuple(corr[h] + _colsum(probs(qoff, h, k_aug[h], masked) * res_ref[h, pl.ds(qoff, blk), 0:1])
                             for h in range(2))

            zero = jnp.zeros((1, blk), f32)
            corr = lax.fori_loop(kj + 1, nblk, lambda qi, cr: q_fix(qi, cr, False), q_fix(kj, (zero, zero), True))
            df_ref[0, 0:1, pl.ds(koff, blk)] += corr[0]
            df_ref[0, 1:2, pl.ds(koff, blk)] += corr[1]
            return 0

        lax.fori_loop(0, nblk, kv_fix, 0)

    pair_aug = pl.BlockSpec((SEQ, 2 * LANES), lambda p: (0, p))
    slab = pl.BlockSpec((SEQ, LANES), lambda p: (0, p))
    per_pair = pl.BlockSpec((1, SEQ, LANES), lambda p: (p, 0, 0))
    rows = pl.BlockSpec((1, 8, SEQ), lambda p: (p, 0, 0))
    return pl.pallas_call(
        body, grid=(npair,),
        in_specs=[pair_aug, pair_aug, slab, slab, slab, per_pair],
        out_specs=[slab, slab, slab, rows],
        out_shape=(SDS((SEQ, FOX_W), bf16),) * 3 + (SDS((npair, 8, SEQ), f32),), name="fox_bwd",
        scratch_shapes=[pltpu.VMEM((SEQ, LANES), f32), pltpu.VMEM((SEQ, LANES), f32),
                        pltpu.VMEM((2, SEQ, LANES), f32)],
        compiler_params=_params(("parallel",)),
    )(q_aug, k_aug, v, do, o, lse)


DIL_BLK = 128
DILATIONS = (1, 4, 16)
N_GROUPS = len(DILATIONS)
DIL_PAIRS = DIL_OUT_W // LANES


def _dil_blocks(d):
    r1 = lax.broadcasted_iota(jnp.int32, (DIL_BLK, DIL_BLK), 0)
    c1 = lax.broadcasted_iota(jnp.int32, (DIL_BLK, DIL_BLK), 1)
    r2 = lax.broadcasted_iota(jnp.int32, (DIL_BLK, 2 * DIL_BLK), 0)
    c2 = lax.broadcasted_iota(jnp.int32, (DIL_BLK, 2 * DIL_BLK), 1)
    band = ((c2 < DIL_BLK) & (c2 >= r2)) | ((c2 >= DIL_BLK) & (c2 - DIL_BLK <= r2))
    out = []
    for r in range(d):
        for b in range(SEQ // d // DIL_BLK):
            rows = pl.ds(r + d * DIL_BLK * b, DIL_BLK, stride=d)
            if b == 0:
                out.append((rows, rows, r1 >= c1))
            else:
                out.append((rows, pl.ds(r + d * DIL_BLK * (b - 1), 2 * DIL_BLK, stride=d), band))
    return out


def _dil_fwd(q, k, v, g):
    def body(q_ref, k_ref, v_ref, o_ref, lse_ref):
        first = lax.broadcasted_iota(jnp.int32, (DIL_BLK, LANES), 1) < HEAD_DIM
        for rows, krows, mask in _dil_blocks(DILATIONS[g]):
            qv, kk, vv = q_ref[rows, :].astype(bf16), k_ref[krows, :].astype(bf16), v_ref[krows, :].astype(bf16)
            outs, lses = [], []
            for own in (first, ~first):
                s = jnp.where(mask, _nt(jnp.where(own, qv, jnp.zeros_like(qv)), kk), NEG)
                m = jnp.max(s, axis=1, keepdims=True)
                p = jnp.exp(s - m)
                l = jnp.sum(p, axis=1, keepdims=True)
                outs.append(jnp.dot(p.astype(bf16), vv, preferred_element_type=f32) / l)
                lses.append(m + jnp.log(l))
            o_ref[rows, :] = jnp.where(first, outs[0], outs[1])
            lse_ref[rows, :] = jnp.where(first, lses[0], lses[1])

    grouped = pl.BlockSpec((SEQ, LANES), lambda p: (0, DIL_PAIRS * g + p))
    own = pl.BlockSpec((SEQ, LANES), lambda p: (0, p))
    shape = SDS((SEQ, DIL_OUT_W), f32)
    return pl.pallas_call(
        body, grid=(DIL_PAIRS,), in_specs=[grouped] * 3, out_specs=[own] * 2, out_shape=(shape, shape),
        name=f"dil_fwd_{DILATIONS[g]}", compiler_params=_params(("parallel",)),
    )(q, k, v)


def _dil_bwd(q, k, v, do, lse, delta, g):
    def body(q_ref, k_ref, v_ref, do_ref, lse_ref, dl_ref, dq_ref, dk_ref, dv_ref):
        first = lax.broadcasted_iota(jnp.int32, (DIL_BLK, LANES), 1) < HEAD_DIM
        dk_ref[...] = jnp.zeros_like(dk_ref)
        dv_ref[...] = jnp.zeros_like(dv_ref)
        for rows, krows, mask in _dil_blocks(DILATIONS[g]):
            qv, kk, vv = q_ref[rows, :].astype(bf16), k_ref[krows, :].astype(bf16), v_ref[krows, :].astype(bf16)
            dov = do_ref[rows, :].astype(bf16)
            lsev, delv = lse_ref[rows, :], dl_ref[rows, :]
            dqs, dk_add, dv_add = [], None, None
            for h, own in enumerate((first, ~first)):
                col = h * HEAD_DIM
                qh = jnp.where(own, qv, jnp.zeros_like(qv))
                doh = jnp.where(own, dov, jnp.zeros_like(dov))
                p = jnp.exp(jnp.where(mask, _nt(qh, kk), NEG) - lsev[:, col:col + 1])
                dl = (p * (_nt(doh, vv) - delv[:, col:col + 1])).astype(bf16)
                dqs.append(jnp.dot(dl, kk, preferred_element_type=f32))
                dk_h, dv_h = _tn(dl, qh), _tn(p.astype(bf16), doh)
                dk_add = dk_h if dk_add is None else dk_add + dk_h
                dv_add = dv_h if dv_add is None else dv_add + dv_h
            dq_ref[rows, :] = jnp.where(first, dqs[0], dqs[1]) * ATT_SCALE
            dk_ref[krows, :] += dk_add
            dv_ref[krows, :] += dv_add

    grouped = pl.BlockSpec((SEQ, LANES), lambda p: (0, DIL_PAIRS * g + p))
    own = pl.BlockSpec((SEQ, LANES), lambda p: (0, p))
    shape = SDS((SEQ, DIL_OUT_W), f32)
    return pl.pallas_call(
        body, grid=(DIL_PAIRS,), in_specs=[grouped] * 3 + [own] * 3, out_specs=[own] * 3,
        out_shape=(shape, shape, shape), name=f"dil_bwd_{DILATIONS[g]}", compiler_params=_params(("parallel",)),
    )(q, k, v, do, lse, delta)


def _position():
    return lax.axis_index("x"), lax.axis_index("y"), lax.axis_index("c")


def _all_gather(block, name):
    def body(x_ref, out_ref, send_sems, recv_sems, local_sem):
        x, y, c = _position()
        me, sibling = (x, y, c), (x, y, 1 - c)
        chips = [(1 - x, y), (x, 1 - y), (1 - x, 1 - y)]

        def slot(px, py, pc):
            return out_ref.at[4 * px + 2 * py + pc]

        def copy(k, blk, to, src=None):
            return pltpu.make_async_remote_copy(
                src_ref=slot(*blk) if src is None else src, dst_ref=slot(*blk),
                send_sem=send_sems.at[k], recv_sem=recv_sems.at[k], device_id=to, device_id_type=MESH)

        mine = pltpu.make_async_copy(x_ref, slot(*me), local_sem)
        mine.start()
        first = [copy(0, me, sibling, src=x_ref)]
        first += [copy(1 + j, me, (*chip, c), src=x_ref) for j, chip in enumerate(chips)]
        for cp in first:
            cp.start()
        passed = [copy(4 + j, (*chip, c), sibling) for j, chip in enumerate(chips)]
        for j, chip in enumerate(chips):
            copy(1 + j, (*chip, c), me).wait_recv()
            passed[j].start()
        copy(0, sibling, me).wait_recv()
        for j, chip in enumerate(chips):
            copy(4 + j, (*chip, 1 - c), me).wait_recv()
        for cp in first + passed:
            cp.wait_send()
        mine.wait()

    return pl.pallas_call(
        body, out_shape=SDS((N_DEV,) + block.shape, block.dtype),
        in_specs=[pl.BlockSpec(memory_space=pl.ANY)], out_specs=pl.BlockSpec(memory_space=pl.ANY),
        scratch_shapes=[pltpu.SemaphoreType.DMA((7,)), pltpu.SemaphoreType.DMA((7,)), pltpu.SemaphoreType.DMA],
        name=name,
    )(block)


def _all_gather_many(blocks, name):
    n = len(blocks)

    def body(*refs):
        x_refs, out_refs = refs[:n], refs[n:2 * n]
        send_sems, recv_sems, local_sems = refs[2 * n:]
        x, y, c = _position()
        me, sibling = (x, y, c), (x, y, 1 - c)
        chips = [(1 - x, y), (x, 1 - y), (1 - x, 1 - y)]

        def slot(a, px, py, pc):
            return out_refs[a].at[4 * px + 2 * py + pc]

        def copy(a, k, blk, to, own=False):
            return pltpu.make_async_remote_copy(
                src_ref=x_refs[a] if own else slot(a, *blk), dst_ref=slot(a, *blk),
                send_sem=send_sems.at[a, k], recv_sem=recv_sems.at[a, k], device_id=to, device_id_type=MESH)

        mine = [pltpu.make_async_copy(x_refs[a], slot(a, *me), local_sems.at[a]) for a in range(n)]
        for cp in mine:
            cp.start()
        started = []
        for a in range(n):
            first = [copy(a, 0, me, sibling, own=True)]
            first += [copy(a, 1 + j, me, (*chip, c), own=True) for j, chip in enumerate(chips)]
            for cp in first:
                cp.start()
            started += first
        for a in range(n):
            for j, chip in enumerate(chips):
                copy(a, 1 + j, (*chip, c), me).wait_recv()
                passed = copy(a, 4 + j, (*chip, c), sibling)
                passed.start()
                started.append(passed)
        for a in range(n):
            copy(a, 0, sibling, me).wait_recv()
            for j, chip in enumerate(chips):
                copy(a, 4 + j, (*chip, 1 - c), me).wait_recv()
        for cp in started:
            cp.wait_send()
        for cp in mine:
            cp.wait()

    hbm = pl.BlockSpec(memory_space=pl.ANY)
    return pl.pallas_call(
        body, out_shape=[SDS((N_DEV,) + b.shape, b.dtype) for b in blocks],
        in_specs=[hbm] * n, out_specs=[hbm] * n,
        scratch_shapes=[pltpu.SemaphoreType.DMA((n, 7)), pltpu.SemaphoreType.DMA((n, 7)),
                        pltpu.SemaphoreType.DMA((n,))],
        name=name,
    )(*blocks)


def _pair_exchange(gs, name):
    n = len(gs)

    def body(*refs):
        g_refs, r_refs, send_sems, recv_sems = refs[:n], refs[n:2 * n], refs[2 * n], refs[2 * n + 1]
        x, y, c = _position()
        copies = []
        for a in range(n):
            for k in range(4):
                cp = pltpu.make_async_remote_copy(
                    src_ref=g_refs[a].at[2 * k + (1 - c)], dst_ref=r_refs[a].at[k], send_sem=send_sems.at[a, k],
                    recv_sem=recv_sems.at[a, k], device_id=(x, y, 1 - c), device_id_type=MESH)
                cp.start()
                copies.append(cp)
        for cp in copies:
            cp.wait()

    hbm = pl.BlockSpec(memory_space=pl.ANY)
    return pl.pallas_call(
        body, out_shape=[SDS((4,) + g.shape[1:], g.dtype) for g in gs], in_specs=[hbm] * n, out_specs=[hbm] * n,
        scratch_shapes=[pltpu.SemaphoreType.DMA((n, 4)), pltpu.SemaphoreType.DMA((n, 4))], name=name,
    )(*gs)


HBM_SPEC = pl.BlockSpec(memory_space=pltpu.HBM)
SEM_SPEC = pl.BlockSpec(memory_space=pltpu.SEMAPHORE)
SPLIT_COPY = pltpu.CompilerParams(has_side_effects=pltpu.SideEffectType.DATAFLOW_SIDE_EFFECTING)


def _in_hbm(t):
    return pltpu.with_memory_space_constraint(t, pltpu.HBM)


def _chip_copies(t_refs, land_refs, send_sems, recv_sems):
    x, y, c = _position()
    chips = [(1 - x, y), (x, 1 - y), (1 - x, 1 - y)]
    return [pltpu.make_async_remote_copy(
        src_ref=t.at[2 * px + py], dst_ref=land.at[j], send_sem=send_sems.at[3 * a + j],
        recv_sem=recv_sems.at[3 * a + j], device_id=(px, py, c), device_id_type=MESH)
        for a, (t, land) in enumerate(zip(t_refs, land_refs, strict=True)) for j, (px, py) in enumerate(chips)]


def _chip_exchange_start(ts, name):
    n = len(ts)
    lands = [_in_hbm(lax.empty((3,) + t.shape[1:], t.dtype)) for t in ts]

    def body(*refs):
        send_sems, recv_sems = refs[2 * n], refs[2 * n + 1]
        for cp in _chip_copies(refs[:n], refs[n:2 * n], send_sems, recv_sems):
            cp.start()
        refs[-1][...] = jnp.zeros_like(refs[-1])

    sems = pltpu.SemaphoreType.DMA((3 * n,))
    res = pl.pallas_call(
        body, name=name, in_specs=[HBM_SPEC] * (2 * n),
        out_shape=(sems, sems, *[pltpu.HBM(t.shape, t.dtype) for t in (*ts, *lands)], SDS((8, LANES), f32)),
        out_specs=(SEM_SPEC, SEM_SPEC, *[HBM_SPEC] * (2 * n), pl.BlockSpec(memory_space=pltpu.VMEM)),
        input_output_aliases={i: 2 + i for i in range(2 * n)}, compiler_params=SPLIT_COPY,
    )(*[_in_hbm(t) for t in ts], *lands)
    return res[:-1], res[-1]


def _chip_exchange_wait(state, after, name):
    send_sems, recv_sems, *arrays = state
    n = len(arrays) // 2

    def body(*refs):
        for cp in _chip_copies(refs[:n], refs[n:2 * n], refs[2 * n], refs[2 * n + 1]):
            cp.wait_send()
            cp.wait_recv()

    res = pl.pallas_call(
        body, name=name, in_specs=[HBM_SPEC] * (2 * n) + [SEM_SPEC, SEM_SPEC, pl.BlockSpec(memory_space=pl.ANY)],
        out_shape=[pltpu.HBM(t.shape, t.dtype) for t in arrays], out_specs=[HBM_SPEC] * (2 * n),
        input_output_aliases={i: i for i in range(2 * n)}, compiler_params=SPLIT_COPY,
    )(*arrays, send_sems, recv_sems, after)
    return res[:n], res[n:]


def _gather_copies(x_refs, out_refs, send_sems, recv_sems):
    x, y, c = _position()
    peers = [(x, y, 1 - c), (1 - x, y, c), (x, 1 - y, c), (1 - x, 1 - y, c)]
    sends, arrivals = [], []
    for a, (x_ref, out_ref) in enumerate(zip(x_refs, out_refs, strict=True)):
        for k, (px, py, pc) in enumerate(peers):
            sems = dict(send_sem=send_sems.at[4 * a + k], recv_sem=recv_sems.at[4 * a + k],
                        device_id=(px, py, pc), device_id_type=MESH)
            sends.append(pltpu.make_async_remote_copy(src_ref=x_ref, dst_ref=out_ref.at[4 * x + 2 * y + c], **sems))
            arrivals.append(pltpu.make_async_remote_copy(src_ref=x_ref, dst_ref=out_ref.at[4 * px + 2 * py + pc],
                                                         **sems))
    return sends, arrivals


def _gather_start(blocks, after, name):
    n = len(blocks)
    outs = [_in_hbm(lax.empty((N_DEV,) + b.shape, b.dtype)) for b in blocks]

    def body(*refs):
        sends, _ = _gather_copies(refs[:n], refs[n:2 * n], refs[2 * n + 1], refs[2 * n + 2])
        for cp in sends:
            cp.start()
        refs[-1][...] = jnp.zeros_like(refs[-1])

    sems = pltpu.SemaphoreType.DMA((4 * n,))
    res = pl.pallas_call(
        body, name=name, in_specs=[HBM_SPEC] * (2 * n) + [pl.BlockSpec(memory_space=pl.ANY)],
        out_shape=(sems, sems, *[pltpu.HBM(t.shape, t.dtype) for t in (*blocks, *outs)], SDS((8, LANES), f32)),
        out_specs=(SEM_SPEC, SEM_SPEC, *[HBM_SPEC] * (2 * n), pl.BlockSpec(memory_space=pltpu.VMEM)),
        input_output_aliases={i: 2 + i for i in range(2 * n)}, compiler_params=SPLIT_COPY,
    )(*[_in_hbm(b) for b in blocks], *outs, after)
    return res[:-1], res[-1]


def _gather_wait(state, after, name):
    send_sems, recv_sems, *arrays = state
    n = len(arrays) // 2

    def body(*refs):
        sends, arrivals = _gather_copies(refs[:n], refs[n:2 * n], refs[2 * n], refs[2 * n + 1])
        for cp in sends:
            cp.wait_send()
        for cp in arrivals:
            cp.wait_recv()

    res = pl.pallas_call(
        body, name=name, in_specs=[HBM_SPEC] * (2 * n) + [SEM_SPEC, SEM_SPEC, pl.BlockSpec(memory_space=pl.ANY)],
        out_shape=[pltpu.HBM(t.shape, t.dtype) for t in arrays], out_specs=[HBM_SPEC] * (2 * n),
        input_output_aliases={i: i for i in range(2 * n)}, compiler_params=SPLIT_COPY,
    )(*arrays, send_sems, recv_sems, after)
    return res[:n], res[n:]


def _gather_finish(partial, name):
    n = len(partial)

    def body(*refs):
        in_refs, out_refs = refs[:n], refs[n:2 * n]
        send_sems, recv_sems = refs[2 * n:]
        x, y, c = _position()
        chips = [(1 - x, y), (x, 1 - y), (1 - x, 1 - y)]
        copies = []
        for a in range(n):
            for j, (px, py) in enumerate(chips):
                cp = pltpu.make_async_remote_copy(
                    src_ref=in_refs[a].at[4 * px + 2 * py + c], dst_ref=out_refs[a].at[4 * px + 2 * py + c],
                    send_sem=send_sems.at[a, j], recv_sem=recv_sems.at[a, j], device_id=(x, y, 1 - c),
                    device_id_type=MESH)
                cp.start()
                copies.append(cp)
        for a in range(n):
            for j, (px, py) in enumerate(chips):
                pltpu.make_async_remote_copy(
                    src_ref=in_refs[a].at[4 * px + 2 * py + (1 - c)], dst_ref=out_refs[a].at[4 * px + 2 * py + (1 - c)],
                    send_sem=send_sems.at[a, j], recv_sem=recv_sems.at[a, j], device_id=(x, y, 1 - c),
                    device_id_type=MESH).wait_recv()
        for cp in copies:
            cp.wait_send()

    hbm = pl.BlockSpec(memory_space=pl.ANY)
    return pl.pallas_call(
        body, out_shape=[SDS(p.shape, p.dtype) for p in partial], in_specs=[hbm] * n, out_specs=[hbm] * n,
        input_output_aliases={a: a for a in range(n)},
        scratch_shapes=[pltpu.SemaphoreType.DMA((n, 3)), pltpu.SemaphoreType.DMA((n, 3))],
        name=name,
    )(*partial)


def _row_tile(rows):
    return 256 if rows % 256 == 0 and rows > 512 else rows


def _pair_add(g, r1, core, name):
    def body(c_ref, g_ref, r_ref, o_ref):
        o_ref[...] = (g_ref[...].astype(f32) + r_ref[...].astype(f32)).astype(o_ref.dtype)

    rows, cols = g.shape[1:]
    tile = _row_tile(rows)
    blk = (1, tile, cols)
    return pl.pallas_call(
        body, out_shape=SDS((4, rows, cols), g.dtype), name=name,
        grid_spec=pltpu.PrefetchScalarGridSpec(
            num_scalar_prefetch=1, grid=(4, rows // tile),
            in_specs=[pl.BlockSpec(blk, lambda k, i, c_ref: (2 * k + c_ref[0], i, 0)),
                      pl.BlockSpec(blk, lambda k, i, c_ref: (k, i, 0))],
            out_specs=pl.BlockSpec(blk, lambda k, i, c_ref: (k, i, 0))),
        compiler_params=_params(("parallel", "arbitrary")),
    )(core, g, r1)


def _chip_add(t, r2, chip, name):
    def body(c_ref, t_ref, r_ref, o_ref):
        o_ref[...] = ((t_ref[0].astype(f32) + r_ref[0].astype(f32)) + r_ref[1].astype(f32)) + r_ref[2].astype(f32)

    rows, cols = t.shape[1:]
    tile = _row_tile(rows)
    return pl.pallas_call(
        body, out_shape=SDS((rows, cols), f32), name=name,
        grid_spec=pltpu.PrefetchScalarGridSpec(
            num_scalar_prefetch=1, grid=(rows // tile,),
            in_specs=[pl.BlockSpec((1, tile, cols), lambda i, c_ref: (c_ref[0], i, 0)),
                      pl.BlockSpec((3, tile, cols), lambda i, c_ref: (0, i, 0))],
            out_specs=pl.BlockSpec((tile, cols), lambda i, c_ref: (i, 0))),
        compiler_params=_params(("arbitrary",)),
    )(chip, t, r2)


def _pad_to(t, axis, size):
    pads = [(0, 0)] * t.ndim
    pads[axis] = (0, size - t.shape[axis])
    return jnp.pad(t, pads)


_REF_COLS = {"qa": (0, FOX_W), "ka": (FOX_W, FOX_W), "va": (2 * FOX_W, FOX_W), "f": (3 * FOX_W, N_FOX_HEADS)}
_REF_COLS.update({n: (3 * FOX_W + N_FOX_HEADS + i * DIL_W, DIL_W) for i, n in enumerate(("qb", "kb", "vb"))})
_REF_COLS.update({n: (3 * FOX_W + N_FOX_HEADS + 3 * DIL_W + i * D, D) for i, n in enumerate(("ga", "gb"))})
_REF_ORDER = ("qa", "ka", "va", "f", "qb", "kb", "vb", "ga", "gb")


def _shard_pad_cols(pieces):
    first = pieces[_REF_ORDER[0]]
    pad = jnp.zeros((first.shape[0], W_IN_PAD - W_IN_SH), first.dtype)
    parts, names, used = [], list(_REF_ORDER), 0
    for _ in range(N_DEV):
        need = W_IN_SH
        while need:
            take = min(need, _REF_COLS[names[0]][1] - used)
            parts.append(pieces[names[0]][:, used:used + take])
            need, used = need - take, used + take
            if used == _REF_COLS[names[0]][1]:
                names, used = names[1:], 0
        parts.append(pad)
    return jnp.concatenate(parts, axis=1)


def _slab_w_in(stack):
    def cols(name):
        lo, width = _REF_COLS[name]
        hi, out = lo + width, []
        while lo < hi:
            j, off = divmod(lo, W_IN_SH)
            n = min(hi - lo, W_IN_SH - off)
            out.append(stack[j, :, off:off + n])
            lo += n
        return out
    z = lambda n: [jnp.zeros((stack.shape[1], n), stack.dtype)]
    parts = (cols("ga") + cols("gb") + z(C_QB - 2 * D) + cols("qb") + cols("kb") + cols("vb") + cols("qa")
             + cols("ka") + cols("va") + cols("f") + z(LANES - N_FOX_HEADS))
    return jnp.concatenate(parts, axis=1)


def kernel(x, c, w_ada, b_ada, g_mix, w_in, b_fgate, w_br_a, w_br_b, w_out, g_ffn, w_ffn_gate, w_ffn_up, w_ffn_down, g_final, loss_target, m_w_ada, m_b_ada, m_g_mix, m_w_in, m_b_fgate, m_w_br_a, m_w_br_b, m_w_out, m_g_ffn, m_w_ffn_gate, m_w_ffn_up, m_w_ffn_down, m_g_final, v_w_ada, v_b_ada, v_g_mix, v_w_in, v_b_fgate, v_w_br_a, v_w_br_b, v_w_out, v_g_ffn, v_w_ffn_gate, v_w_ffn_up, v_w_ffn_down, v_g_final):
    px, py, pc = _position()
    dev = 4 * px + 2 * py + pc
    x2d, tgt = x[0], loss_target[0]

    c_all = _all_gather(c, "gather_c").reshape(N_DEV, D)
    ada_cols = w_ada.shape[2]
    b_shard = lax.dynamic_slice(b_ada, (0, dev * ada_cols), (1, ada_cols))
    mod_shard = _ada_fwd(c_all, w_ada[0], b_shard)
    mod_all = _all_gather(mod_shard, "gather_mod")
    modv = lax.dynamic_index_in_dim(mod_all, dev, axis=1, keepdims=False).reshape(6, D)

    gate_up = jnp.concatenate([_pad_to(w_ffn_gate[0], 1, FF_PAD), _pad_to(w_ffn_up[0], 1, FF_PAD)], axis=1)
    w_in_s, = _all_gather_many([_pad_to(w_in[0], 1, W_IN_PAD).astype(bf16)], "gather_w_in")
    later = [w_br_a[0], w_br_b[0], w_out[0], gate_up, _pad_to(w_ffn_down[0], 0, FF_PAD)]
    later_state, later_token = _gather_start([t.astype(bf16) for t in later], w_in_s, "gather_rest_start")
    w_in_p = _slab_w_in(w_in_s)

    h1 = _pre1(x2d, modv, g_mix)
    proj = _matmul(h1, w_in_p, name="mm_proj", tm=SEQ, tn=896, tk=D, after=later_token)
    b_pad = jnp.pad(b_fgate, ((0, 0), (0, LANES - N_FOX_HEADS)))
    q_aug, k_aug, va = _fox_prep(proj, _fox_gate_fwd(proj, b_pad))
    ya_h, lse_a = _fox_fwd(q_aug, k_aug, va)

    tables = _rope_tables()
    qb_r, kb_r, vb = _rope_fwd(proj, tables)
    by_group = [_dil_fwd(qb_r, kb_r, vb, grp) for grp in range(N_GROUPS)]
    yb_h, lse_b = _dil_combine([o for o, _ in by_group], [l for _, l in by_group])

    mine, arrived = _gather_wait(later_state, yb_h, "gather_rest_wait")
    w_a_s, w_b_s, w_o_s, w_gu_s, w_d_s = [
        lax.dynamic_update_slice(stack, block[None], (dev, 0, 0))
        for stack, block in zip(_gather_finish(arrived, "gather_rest_finish"), mine, strict=True)]
    w_o = w_o_s.reshape(D, D)
    w_d = w_d_s.reshape(FF_HID, D)
    ya = _matmul(ya_h, w_a_s, by_shard=True, name="mm_br_a", tm=SEQ, tn=W_BR_SH, tk=FOX_W)
    yb = _matmul(yb_h, w_b_s, by_shard=True, name="mm_br_b", tm=SEQ, tn=W_BR_SH, tk=DIL_OUT_W)

    merged = _merge_fwd(ya, yb, proj)
    mix = _matmul(merged, w_o, name="mm_out", tm=SEQ, tn=512, tk=D)
    x1, h2 = _post1(x2d, mix, modv, g_ffn)
    au = _matmul(h2, w_gu_s, by_shard=True, name="mm_ffn_in", tm=SEQ, tn=2 * FF_PAD, tk=D)
    act = _swiglu_fwd(au)
    ff = _matmul(act, w_d, name="mm_ffn_down", tm=SEQ, tn=512, tk=FF_HID // 2)

    dx2, dff, dg_final, dga_f, loss_lanes = _final(x1, ff, tgt, modv, g_final.reshape(1, D))
    dact = _matmul(dff, w_d, tb=True, name="mm_d_act", tm=SEQ // 2, tn=FF_HID // 2, tk=D)
    dau = _swiglu_bwd(au, dact)

    core = pc.astype(jnp.int32).reshape(1)
    chip = (2 * px + py).astype(jnp.int32).reshape(1)

    def to_chips(by_dev, tags, name):
        from_pair = _pair_exchange(by_dev, "pair_exchange_" + name)
        sums = [_pair_add(g, r, core, "pair_add_" + t) for g, r, t in zip(by_dev, from_pair, tags)]
        return _chip_exchange_start(sums, "chip_exchange_start_" + name)

    def from_chips(state, after, tags, name):
        sums, got = _chip_exchange_wait(state, after, "chip_exchange_wait_" + name)
        return [_chip_add(p, r, chip, "chip_add_" + t) for p, r, t in zip(sums, got, tags)]

    g_gu = _matmul(h2, dau, ta=True, by_shard=True, out_dtype=bf16, name="mm_g_ffn_in", tm=D, tn=2 * FF_PAD, tk=SEQ)
    g_d = _matmul(act, dff, ta=True, out_dtype=bf16, name="mm_g_down", tm=FF_HID // 2, tn=512, tk=SEQ)
    ffn_tags = ["gu", "down"]
    ffn_state, ffn_token = to_chips([g_gu, g_d.reshape(N_DEV, FF_PAD, D)], ffn_tags, "ffn")

    dh2 = _matmul(dau, w_gu_s, tb=True, by_shard=True, name="mm_d_h2", tm=SEQ // 2, tn=D, tk=2 * FF_PAD,
                  after=ffn_token)
    dx1, dmix, dsh_f, dsc_f, dg_ffn, dga_m = _mid_bwd(dh2, x1, dx2, mix, modv, g_ffn)
    dmerged = _matmul(dmix, w_o, tb=True, name="mm_d_merged", tm=SEQ, tn=512, tk=D)
    dya, dyb, dga, dgb = _merge_bwd(dmerged, ya, yb, proj)
    dya_h = _matmul(dya, w_a_s, tb=True, by_shard=True, name="mm_d_ya", tm=SEQ, tn=FOX_W, tk=W_BR_SH)
    dyb_h = _matmul(dyb, w_b_s, tb=True, by_shard=True, name="mm_d_yb", tm=SEQ, tn=DIL_OUT_W, tk=W_BR_SH)

    dqa, dka, dva, dF = _fox_bwd(q_aug, k_aug, va, dya_h, ya_h, lse_a)
    dF_row = jnp.pad(dF[:, :2, :].reshape(N_FOX_HEADS, SEQ), ((0, LANES - N_FOX_HEADS), (0, 0)))
    df, db_fgate = _fox_gate_bwd(dF_row, proj, b_pad)

    delta_b = _dil_delta(dyb_h, yb_h)
    dil_grads = [_dil_bwd(qb_r, kb_r, vb, dyb_h, lse_b, delta_b, grp) for grp in range(N_GROUPS)]
    dqb, dkb = _rope_bwd([t[0] for t in dil_grads], [t[1] for t in dil_grads], tables)
    dvb = jnp.concatenate([t[2] for t in dil_grads], axis=1).astype(bf16)

    dproj = _shard_pad_cols({"qa": dqa, "ka": dka, "va": dva, "f": df[:, :N_FOX_HEADS], "qb": dqb, "kb": dkb,
                             "vb": dvb, "ga": dga, "gb": dgb})
    dh1 = _matmul(dproj, w_in_s, tb=True, by_shard=True, name="mm_d_h1", tm=SEQ // 2, tn=D, tk=W_IN_PAD)
    grad_x, dsh_m, dsc_m, dg_mix = _first_bwd(dh1, x2d, dx1, modv, g_mix)

    pad_lane = lambda t: jnp.pad(t, ((0, 0), (0, D - t.shape[1])))
    small = jnp.concatenate([dsh_m, dsc_m, dga_m, dsh_f, dsc_f, dga_f, dg_mix, dg_ffn, dg_final,
                             pad_lane(db_fgate), loss_lanes, jnp.zeros((SMALL_ROWS - 11, D), f32)], axis=0)
    small_all = _all_gather(small, "gather_small")

    g_in = _matmul(h1, dproj, ta=True, by_shard=True, out_dtype=bf16, name="mm_g_in", tm=D, tn=W_IN_PAD, tk=SEQ,
                   after=small_all)
    g_o = _matmul(merged, dmix, ta=True, out_dtype=bf16, name="mm_g_out", tm=D, tn=512, tk=SEQ)
    g_a = _matmul(ya_h, dya, ta=True, by_shard=True, out_dtype=bf16, name="mm_g_br_a", tm=FOX_W, tn=W_BR_SH, tk=SEQ)
    g_b = _matmul(yb_h, dyb, ta=True, by_shard=True, out_dtype=bf16, name="mm_g_br_b", tm=DIL_OUT_W, tn=W_BR_SH,
                  tk=SEQ)
    rows_a, rows_b = FOX_W * W_BR_SH // D, DIL_OUT_W * W_BR_SH // D
    g_small = jnp.concatenate([g_a.reshape(N_DEV, rows_a, D), g_b.reshape(N_DEV, rows_b, D),
                               g_o.reshape(N_DEV, W_BR_SH, D)], axis=1)
    mix_tags = ["in", "small"]
    mix_state, mix_token = to_chips([g_in, g_small], mix_tags, "mixer")

    small_sum, loss_row = _small_reduce(small_all, mix_token)
    dmod_all = small_all[:, :6, :].reshape(N_DEV, 6 * D)
    g_w_ada = _ada_bwd(c_all, lax.dynamic_slice(dmod_all, (0, dev * ada_cols), (N_DEV, ada_cols)))
    s_gu, s_d = from_chips(ffn_state, small_sum, ffn_tags, "ffn")

    loss = loss_row[0, 0]
    g = {
        "w_ada": g_w_ada[None], "b_ada": small_sum[0:6].reshape(1, 6 * D), "g_mix": small_sum[6:7],
        "b_fgate": small_sum[9:10, :N_FOX_HEADS], "g_ffn": small_sum[7:8], "w_ffn_gate": s_gu[None, :, :W_FF_SH],
        "w_ffn_up": s_gu[None, :, FF_PAD:FF_PAD + W_FF_SH], "w_ffn_down": s_d[None, :W_FF_SH],
        "g_final": small_sum[8],
    }
    w = {"w_ada": w_ada, "b_ada": b_ada, "g_mix": g_mix, "w_in": w_in, "b_fgate": b_fgate, "w_br_a": w_br_a,
         "w_br_b": w_br_b, "w_out": w_out, "g_ffn": g_ffn, "w_ffn_gate": w_ffn_gate, "w_ffn_up": w_ffn_up,
         "w_ffn_down": w_ffn_down, "g_final": g_final}
    m = {"w_ada": m_w_ada, "b_ada": m_b_ada, "g_mix": m_g_mix, "w_in": m_w_in, "b_fgate": m_b_fgate,
         "w_br_a": m_w_br_a, "w_br_b": m_w_br_b, "w_out": m_w_out, "g_ffn": m_g_ffn, "w_ffn_gate": m_w_ffn_gate,
         "w_ffn_up": m_w_ffn_up, "w_ffn_down": m_w_ffn_down, "g_final": m_g_final}
    v = {"w_ada": v_w_ada, "b_ada": v_b_ada, "g_mix": v_g_mix, "w_in": v_w_in, "b_fgate": v_b_fgate,
         "w_br_a": v_w_br_a, "w_br_b": v_w_br_b, "w_out": v_w_out, "g_ffn": v_g_ffn, "w_ffn_gate": v_w_ffn_gate,
         "w_ffn_up": v_w_ffn_up, "w_ffn_down": v_w_ffn_down, "g_final": v_g_final}
    names = list(w)
    delta, new_m, new_v = {}, {}, {}

    def update(n):
        shape = w[n].shape
        two_d = (lambda t: t.reshape(shape[-2:])) if len(shape) == 3 else (lambda t: t)
        dl, mn, vn = _adamw(two_d(w[n]), two_d(g[n]), two_d(m[n]), two_d(v[n]), "adamw_" + n)
        delta[n], new_m[n], new_v[n] = dl.reshape(shape), mn.reshape(shape), vn.reshape(shape)

    for n in list(g):
        update(n)
    done = sum(delta[n].reshape(-1)[:N_FOX_HEADS] for n in g)
    s_in, s_small = from_chips(mix_state, done, mix_tags, "mixer")
    g.update({"w_in": s_in[None, :, :W_IN_SH], "w_br_a": s_small[:rows_a].reshape(1, FOX_W, W_BR_SH),
              "w_br_b": s_small[rows_a:rows_a + rows_b].reshape(1, DIL_OUT_W, W_BR_SH),
              "w_out": s_small[None, rows_a + rows_b:]})
    for n in ("w_in", "w_br_a", "w_br_b", "w_out"):
        update(n)

    return (loss, grad_x[None], *[g[n] for n in names], *[delta[n] for n in names],
            *[new_m[n] for n in names], *[new_v[n] for n in names])
```

```python
import functools

import jax
import jax.numpy as jnp
from jax import lax
from jax.experimental import pallas as pl
from jax.experimental.pallas import tpu as pltpu

f32 = jnp.float32
bf16 = jnp.bfloat16
SDS = jax.ShapeDtypeStruct
MESH = pl.DeviceIdType.MESH

N_DEV = 8
D = 1024
SEQ = 2048
HEAD_DIM = 64
N_FOX_HEADS = 8
FOX_W = 512
DIL_W = 768
DIL_OUT_W = 256
ROT_DIM = 16
ROPE_THETA = 500000.0
D_FF = 2816
IN_COLS = 5896
EPS = 1e-6
NEG = -1e30
ATT_SCALE = HEAD_DIM ** -0.5

ADAM_LR = 0.001
ADAM_B1 = 0.9
ADAM_B2 = 0.999
ADAM_EPS = 1e-08
ADAM_WD = 0.01
ADAM_STEP = 10

C_GA, C_GB, C_QB, C_KB, C_VB, C_QA, C_KA, C_VA, C_F = 0, 1024, 2304, 3072, 3840, 4608, 5120, 5632, 6144
PROJ_W = 6272
LANES = 128
VMEM_LIMIT = 52 * 1024 * 1024

W_IN_SH, W_IN_PAD = IN_COLS // N_DEV, 768
W_BR_SH = D // N_DEV
W_FF_SH, FF_PAD = D_FF // N_DEV, 384
FF_HID = N_DEV * FF_PAD
SMALL_ROWS = 16


def _params(sem=None):
    if sem is None:
        return pltpu.CompilerParams(vmem_limit_bytes=VMEM_LIMIT)
    return pltpu.CompilerParams(dimension_semantics=sem, vmem_limit_bytes=VMEM_LIMIT)


def _rowwise(fn, name, tiled, vecs, outs, reds=(), tile=256):
    nt, nv, no = len(tiled), len(vecs), len(outs)
    rows = tiled[0][0].shape[0]
    assert rows % tile == 0

    def body(*refs):
        tin = [r[...] for r in refs[:nt]]
        vin = [r[...] for r in refs[nt:nt + nv]]
        orefs = refs[nt + nv:nt + nv + no]
        rrefs = refs[nt + nv + no:]
        touts, routs = fn(tin, vin)
        for r, t in zip(orefs, touts, strict=True):
            r[...] = t.astype(r.dtype)
        if rrefs:
            @pl.when(pl.program_id(0) == 0)
            def _():
                for r in rrefs:
                    r[...] = jnp.zeros_like(r)
            for r, t in zip(rrefs, routs, strict=True):
                r[...] += t

    def col_map(cb):
        return lambda i: (i, cb)

    def whole_map(nd):
        return lambda i: (0,) * nd

    in_specs = [pl.BlockSpec((tile, w), col_map(cb)) for (_, w, cb) in tiled]
    in_specs += [pl.BlockSpec(v.shape, whole_map(v.ndim)) for v in vecs]
    out_specs = [pl.BlockSpec((tile, w), lambda i: (i, 0)) for (w, _) in outs]
    out_specs += [pl.BlockSpec((1, w), lambda i: (0, 0)) for w in reds]
    out_shape = [SDS((rows, w), dt) for (w, dt) in outs] + [SDS((1, w), f32) for w in reds]
    res = pl.pallas_call(
        body, grid=(rows // tile,), in_specs=in_specs, out_specs=out_specs, out_shape=out_shape, name=name,
        compiler_params=_params(("arbitrary",)),
    )(*[t[0] for t in tiled], *vecs)
    return res


def _matmul(a, b, *, ta=False, tb=False, out_dtype=f32, name, tm, tn, tk, by_shard=False, after=None):
    m, k = (a.shape[1], a.shape[0]) if ta else a.shape
    if by_shard and not ta:
        n, kb = (b.shape[1], N_DEV * b.shape[2]) if tb else (N_DEV * b.shape[2], b.shape[1])
        assert (tk if tb else tn) == b.shape[2]
    else:
        n, kb = (b.shape[0], b.shape[1]) if tb else (b.shape[1], b.shape[0])
    assert kb == k and m % tm == 0 and n % tn == 0 and k % tk == 0
    nk = k // tk
    dims = (((0 if ta else 1,), (1 if tb else 0,)), ((), ()))
    b_stacked = by_shard and not ta
    o_stacked = by_shard and ta

    def body(a_ref, b_ref, *rest):
        o_ref, *acc = rest[1:] if after is not None else rest
        bv = b_ref[0] if b_stacked else b_ref[...]
        p = lax.dot_general(a_ref[...].astype(bf16), bv.astype(bf16), dims, preferred_element_type=f32)

        def put(val):
            if o_stacked:
                o_ref[0] = val.astype(o_ref.dtype)
            else:
                o_ref[...] = val.astype(o_ref.dtype)

        if nk == 1:
            put(p)
        else:
            acc_ref, = acc
            kk = pl.program_id(2)

            @pl.when(kk == 0)
            def _():
                acc_ref[...] = p

            @pl.when(kk > 0)
            def _():
                acc_ref[...] += p

            @pl.when(kk == nk - 1)
            def _():
                put(acc_ref[...])

    a_spec = pl.BlockSpec((tk, tm), lambda i, j, kk: (kk, i)) if ta else pl.BlockSpec((tm, tk), lambda i, j, kk: (i, kk))
    if b_stacked and tb:
        b_spec = pl.BlockSpec((1, tn, tk), lambda i, j, kk: (kk, j, 0))
    elif b_stacked:
        b_spec = pl.BlockSpec((1, tk, tn), lambda i, j, kk: (j, kk, 0))
    elif tb:
        b_spec = pl.BlockSpec((tn, tk), lambda i, j, kk: (j, kk))
    else:
        b_spec = pl.BlockSpec((tk, tn), lambda i, j, kk: (kk, j))
    if o_stacked:
        assert tn == n // N_DEV
        out_spec = pl.BlockSpec((1, tm, tn), lambda i, j, kk: (j, i, 0))
        out_shape = SDS((N_DEV, m, tn), out_dtype)
    else:
        out_spec = pl.BlockSpec((tm, tn), lambda i, j, kk: (i, j))
        out_shape = SDS((m, n), out_dtype)
    extra_specs, extra = ([pl.BlockSpec(memory_space=pl.ANY)], [after]) if after is not None else ([], [])
    return pl.pallas_call(
        body, grid=(m // tm, n // tn, nk), in_specs=[a_spec, b_spec] + extra_specs, out_specs=out_spec,
        out_shape=out_shape, name=name, scratch_shapes=[pltpu.VMEM((tm, tn), f32)] if nk > 1 else [],
        compiler_params=_params(("parallel", "parallel", "arbitrary")),
    )(a, b, *extra)


def _matmul_stack(a, b, *, ta=False, tb=False, out_dtype=f32, name):
    def lanes(ref):
        return jnp.concatenate([ref[j] for j in range(N_DEV)], axis=1).astype(bf16)

    if ta:
        w = b.shape[1] // N_DEV

        def body(a_ref, b_ref, o_ref):
            p = _tn(a_ref[...].astype(bf16), b_ref[...].astype(bf16))
            for j in range(N_DEV):
                o_ref[j] = p[:, j * w:(j + 1) * w].astype(o_ref.dtype)

        return pl.pallas_call(body, out_shape=SDS((N_DEV, a.shape[1], w), out_dtype), name=name,
                              compiler_params=_params())(a, b)

    m, half = a.shape[0], a.shape[0] // 2
    n = b.shape[1] if tb else N_DEV * b.shape[2]

    def body(a_ref, b_ref, o_ref):
        av = a_ref[...].astype(bf16)
        o_ref[...] = (_nt(av, lanes(b_ref)) if tb else jnp.dot(av, lanes(b_ref), preferred_element_type=f32)
                      ).astype(o_ref.dtype)

    return pl.pallas_call(
        body, grid=(2,), in_specs=[pl.BlockSpec((half, a.shape[1]), lambda i: (i, 0)),
                                   pl.BlockSpec(b.shape, lambda i: (0, 0, 0))],
        out_specs=pl.BlockSpec((half, n), lambda i: (i, 0)), out_shape=SDS((m, n), out_dtype), name=name,
        compiler_params=_params(("parallel",)),
    )(a, b)


def _rms(x):
    r = lax.rsqrt(jnp.mean(x * x, axis=-1, keepdims=True) + EPS)
    return r, x * r


def _rms_bwd(r, xn, dxn):
    return r * (dxn - xn * jnp.mean(dxn * xn, axis=-1, keepdims=True))


def _colsum(t):
    return jnp.sum(t, axis=0, keepdims=True)


def _sigmoid(x):
    return 1.0 / (1.0 + jnp.exp(-x))


def _modulated_norm(x, g, shift, scale):
    _, xn = _rms(x)
    return (xn * g) * (1.0 + scale) + shift


def _pre1(x, modv, g_mix):
    def fn(t, v):
        (xt,), (mv, g) = t, v
        return [_modulated_norm(xt, g, mv[0:1], mv[1:2])], []
    return _rowwise(fn, "pre1", [(x, D, 0)], [modv, g_mix], [(D, bf16)])[0]


def _post1(x, mix, modv, g_ffn):
    def fn(t, v):
        (xt, mt), (mv, g) = t, v
        x1 = xt + mv[2:3] * mt
        return [x1, _modulated_norm(x1, g, mv[3:4], mv[4:5])], []
    return _rowwise(fn, "post1", [(x, D, 0), (mix, D, 0)], [modv, g_ffn], [(D, f32), (D, bf16)])


def _gate_up(au, j):
    base = 2 * j * FF_PAD
    return au[:, base:base + FF_PAD], au[:, base + FF_PAD:base + 2 * FF_PAD]


def _ffn_in(h, w_stack):
    def body(h_ref, w_ref, act_ref, au_ref):
        p = jnp.dot(h_ref[...], w_ref[0], preferred_element_type=f32)
        a, u = p[:, :FF_PAD], p[:, FF_PAD:]
        act_ref[...] = (a * _sigmoid(a) * u).astype(act_ref.dtype)
        au_ref[...] = p.astype(au_ref.dtype)

    return pl.pallas_call(
        body, grid=(N_DEV,),
        in_specs=[pl.BlockSpec((SEQ, D), lambda j: (0, 0)), pl.BlockSpec((1, D, 2 * FF_PAD), lambda j: (j, 0, 0))],
        out_specs=[pl.BlockSpec((SEQ, FF_PAD), lambda j: (0, j)), pl.BlockSpec((SEQ, 2 * FF_PAD), lambda j: (0, j))],
        out_shape=(SDS((SEQ, FF_HID), bf16), SDS((SEQ, 2 * FF_HID), bf16)), name="ffn_in",
        compiler_params=_params(("parallel",)),
    )(h, w_stack)


def _swiglu_bwd(au, dact):
    def fn(t, v):
        parts, au_t = [], t[0].astype(f32)
        for j in range(N_DEV):
            a, u = _gate_up(au_t, j)
            d = t[1][:, j * FF_PAD:(j + 1) * FF_PAD]
            sg = _sigmoid(a)
            parts += [d * u * (sg * (1.0 + a * (1.0 - sg))), d * (a * sg)]
        return [jnp.concatenate(parts, axis=1)], []
    return _rowwise(fn, "swiglu_bwd", [(au, 2 * FF_HID, 0), (dact, FF_HID, 0)], [], [(2 * FF_HID, bf16)],
                    tile=128)[0]


def _final(x1, ff, target, modv, g_final):
    def fn(t, v):
        (x1t, fft, tgt), (mv, g) = t, v
        x2 = x1t + mv[5:6] * fft
        r, xn = _rms(x2)
        err = xn * g - tgt
        dy = err * (1.0 / D)
        dx2 = _rms_bwd(r, xn, dy * g)
        return [dx2, dx2 * mv[5:6]], [_colsum(dy * xn), _colsum(dx2 * fft), _colsum(err * err) * (0.5 / D)]
    return _rowwise(fn, "final", [(x1, D, 0), (ff, D, 0), (target, D, 0)], [modv, g_final],
                    [(D, f32), (D, bf16)], [D, D, D])


def _mid_bwd(dh2, x1, dx2, mix, modv, g_ffn):
    def fn(t, v):
        (dh, x1t, dx2t, mt), (mv, g) = t, v
        r, xn = _rms(x1t)
        dn = dh * (1.0 + mv[4:5])
        dx1 = dx2t + _rms_bwd(r, xn, dn * g)
        return [dx1, dx1 * mv[2:3]], [_colsum(dh), _colsum(dh * (xn * g)), _colsum(dn * xn), _colsum(dx1 * mt)]
    return _rowwise(fn, "mid_bwd", [(dh2, D, 0), (x1, D, 0), (dx2, D, 0), (mix, D, 0)], [modv, g_ffn],
                    [(D, f32), (D, bf16)], [D, D, D, D])


def _first_bwd(dh1, x, dx1, modv, g_mix):
    def fn(t, v):
        (dh, xt, dx1t), (mv, g) = t, v
        r, xn = _rms(xt)
        dn = dh * (1.0 + mv[1:2])
        return [dx1t + _rms_bwd(r, xn, dn * g)], [_colsum(dh), _colsum(dh * (xn * g)), _colsum(dn * xn)]
    return _rowwise(fn, "first_bwd", [(dh1, D, 0), (x, D, 0), (dx1, D, 0)], [modv, g_mix], [(D, f32)], [D, D, D])


def _merge_fwd(ya, yb, proj):
    def fn(t, v):
        ya_t, yb_t, ga, gb = t
        return [_sigmoid(ga) * ya_t + _sigmoid(gb) * yb_t], []
    return _rowwise(fn, "merge_fwd", [(ya, D, 0), (yb, D, 0), (proj, D, C_GA // D), (proj, D, C_GB // D)], [],
                    [(D, bf16)])[0]


def _merge_bwd(dmerged, ya, yb, proj):
    def fn(t, v):
        dm, ya_t, yb_t, ga, gb = t
        sa, sb = _sigmoid(ga), _sigmoid(gb)
        return [dm * sa, dm * sb, dm * ya_t * (sa * (1.0 - sa)), dm * yb_t * (sb * (1.0 - sb))], []
    return _rowwise(fn, "merge_bwd",
                    [(dmerged, D, 0), (ya, D, 0), (yb, D, 0), (proj, D, C_GA // D), (proj, D, C_GB // D)], [],
                    [(D, bf16), (D, bf16), (D, bf16), (D, bf16)])


def _rope_tables():
    half = ROT_DIM // 2
    pos = jnp.arange(SEQ, dtype=f32)
    inv_freq = ROPE_THETA ** (-jnp.arange(0, ROT_DIM, 2, dtype=f32) / ROT_DIM)
    ang = pos[:, None] * inv_freq[None, :]
    cos, sin = jnp.cos(ang), jnp.sin(ang)
    pad = jnp.zeros((SEQ, HEAD_DIM - ROT_DIM), f32)
    zero = jnp.zeros((SEQ, half), f32)
    c_head = jnp.concatenate([cos, cos, pad + 1.0], axis=1)
    lo_head = jnp.concatenate([-sin, zero, pad], axis=1)
    hi_head = jnp.concatenate([zero, sin, pad], axis=1)
    return tuple(jnp.concatenate([t, t], axis=1) for t in (c_head, lo_head, hi_head))


def _over_heads(tables):
    return [jnp.tile(t, (1, DIL_W // LANES)) for t in tables]


def _rope_fwd(proj, tables):
    half = ROT_DIM // 2

    def fn(t, v):
        q, k, vv = t[:3]
        c, lo, hi = _over_heads(t[3:])
        rot = lambda z: z * c + pltpu.roll(z, DIL_W - half, 1) * lo + pltpu.roll(z, half, 1) * hi
        return [rot(q) * ATT_SCALE, rot(k), vv], []
    return _rowwise(fn, "rope_fwd", [(proj, DIL_W, C_QB // DIL_W), (proj, DIL_W, C_KB // DIL_W),
                                     (proj, DIL_W, C_VB // DIL_W)] + [(tb, LANES, 0) for tb in tables], [],
                    [(DIL_W, f32)] * 3)


def _rope_bwd(dqs, dks, tables):
    half = ROT_DIM // 2

    def fn(t, v):
        dq_t, dk_t = jnp.concatenate(t[:N_GROUPS], axis=1), jnp.concatenate(t[N_GROUPS:2 * N_GROUPS], axis=1)
        c, lo, hi = _over_heads(t[2 * N_GROUPS:])
        rot_t = lambda z: z * c + pltpu.roll(z * lo, half, 1) + pltpu.roll(z * hi, DIL_W - half, 1)
        return [rot_t(dq_t), rot_t(dk_t)], []
    return _rowwise(fn, "rope_bwd", [(a, DIL_OUT_W, 0) for a in (*dqs, *dks)] + [(tb, LANES, 0) for tb in tables],
                    [], [(DIL_W, bf16), (DIL_W, bf16)])


def _head_bcast_sum(d):
    lane = lax.broadcasted_iota(jnp.int32, d.shape, 1)
    out = jnp.zeros_like(d)
    for h in range(d.shape[1] // HEAD_DIM):
        sel = (lane >= h * HEAD_DIM) & (lane < (h + 1) * HEAD_DIM)
        out = jnp.where(sel, jnp.sum(jnp.where(sel, d, 0.0), axis=1, keepdims=True), out)
    return out


def _dil_combine(outs, lses):
    def fn(t, v):
        o0, o1, o2, l0, l1, l2 = t
        m = jnp.maximum(jnp.maximum(l0, l1), l2)
        w0, w1, w2 = jnp.exp(l0 - m), jnp.exp(l1 - m), jnp.exp(l2 - m)
        tot = w0 + w1 + w2
        return [(w0 * o0 + w1 * o1 + w2 * o2) / tot, m + jnp.log(tot)], []
    w = DIL_OUT_W
    return _rowwise(fn, "dil_combine", [(t, w, 0) for t in (*outs, *lses)], [], [(w, f32), (w, f32)])


def _dil_delta(dyb_h, yb_h):
    def fn(t, v):
        return [_head_bcast_sum(t[0] * t[1])], []
    return _rowwise(fn, "dil_delta", [(dyb_h, DIL_OUT_W, 0), (yb_h, DIL_OUT_W, 0)], [], [(DIL_OUT_W, f32)])[0]


def _adamw(w, g, m, v, name):
    shape = w.shape
    if w.ndim == 1:
        w, g, m, v = (t.reshape(1, -1) for t in (w, g, m, v))
    rows, cols = w.shape
    tile = 256 if rows % 256 == 0 and rows > 512 else rows

    def fn(t, _):
        wt, gt, mt, vt = t
        mn = ADAM_B1 * mt + (1.0 - ADAM_B1) * gt
        vn = ADAM_B2 * vt + (1.0 - ADAM_B2) * (gt * gt)
        m_hat = mn / (1.0 - ADAM_B1 ** ADAM_STEP)
        v_hat = vn / (1.0 - ADAM_B2 ** ADAM_STEP)
        return [-ADAM_LR * (m_hat / (jnp.sqrt(v_hat) + ADAM_EPS) + ADAM_WD * wt), mn, vn], []
    delta, mn, vn = _rowwise(fn, name, [(w, cols, 0), (g, cols, 0), (m, cols, 0), (v, cols, 0)], [],
                             [(cols, f32)] * 3, tile=tile)
    return delta.reshape(shape), mn.reshape(shape), vn.reshape(shape)


def _ada_fwd(c_all, w_shard, b_shard):
    def body(c_ref, w_ref, b_ref, o_ref):
        cv = c_ref[...]
        sc = (cv * _sigmoid(cv)).astype(bf16)
        o_ref[...] = jnp.dot(sc, w_ref[...].astype(bf16), preferred_element_type=f32) + b_ref[...]
    return pl.pallas_call(body, out_shape=SDS((N_DEV, w_shard.shape[1]), f32), name="ada_fwd",
                          compiler_params=_params())(c_all, w_shard, b_shard)


def _ada_bwd(c_all, dmod_cols):
    def body(c_ref, d_ref, o_ref):
        cv = c_ref[...]
        sc = cv * _sigmoid(cv)
        o_ref[...] = lax.dot_general(sc, d_ref[...], (((0,), (0,)), ((), ())), precision=lax.Precision.HIGHEST,
                                     preferred_element_type=f32)
    return pl.pallas_call(body, out_shape=SDS((D, dmod_cols.shape[1]), f32), name="ada_bwd",
                          compiler_params=_params())(c_all, dmod_cols)


def _small_reduce(gathered, after):
    def body(g_ref, after_ref, o_ref, loss_ref):
        acc = g_ref[0]
        for d in range(1, N_DEV):
            acc = acc + g_ref[d]
        o_ref[...] = acc
        loss_ref[...] = jnp.zeros((1, LANES), f32) + jnp.sum(acc[10:11, :])
    return pl.pallas_call(body, out_shape=(SDS((SMALL_ROWS, D), f32), SDS((1, LANES), f32)), name="small_reduce",
                          in_specs=[pl.BlockSpec(memory_space=pltpu.VMEM), pl.BlockSpec(memory_space=pl.ANY)],
                          compiler_params=_params())(gathered, after)


FOX_BLK = 512
CUM_BLK = 128


def _fold_lanes(t, op):
    out = t[:, :LANES]
    for j in range(1, t.shape[1] // LANES):
        out = op(out, t[:, j * LANES:(j + 1) * LANES])
    return out


def _fox_gate_fwd(proj, b_pad):
    nblk = SEQ // CUM_BLK

    def body(f_ref, b_ref, col_ref):
        r = lax.broadcasted_iota(jnp.int32, (CUM_BLK, CUM_BLK), 0)
        c = lax.broadcasted_iota(jnp.int32, (CUM_BLK, CUM_BLK), 1)
        tri = (r >= c).astype(f32)
        carry = jnp.zeros((1, LANES), f32)
        for blk in range(nblk):
            z = f_ref[blk * CUM_BLK:(blk + 1) * CUM_BLK, :] + b_ref[...]
            logf = jnp.minimum(z, 0.0) - jnp.log1p(jnp.exp(-jnp.abs(z)))
            cs = jnp.dot(tri, logf, precision=lax.Precision.HIGHEST, preferred_element_type=f32) + carry
            col_ref[blk * CUM_BLK:(blk + 1) * CUM_BLK, :] = cs
            carry = cs[CUM_BLK - 1:CUM_BLK, :]

    return pl.pallas_call(
        body, grid=(1,), in_specs=[pl.BlockSpec((SEQ, LANES), lambda i: (0, C_F // LANES)),
                                   pl.BlockSpec((1, LANES), lambda i: (0, 0))],
        out_specs=pl.BlockSpec((SEQ, LANES), lambda i: (0, 0)),
        out_shape=SDS((SEQ, LANES), f32), name="fox_gate_fwd",
        compiler_params=_params(("arbitrary",)),
    )(proj, b_pad)


def _fox_gate_bwd(dF_row, proj, b_pad):
    nblk = SEQ // CUM_BLK

    def body(d_ref, f_ref, b_ref, df_ref, db_ref, col_ref):
        r = lax.broadcasted_iota(jnp.int32, (CUM_BLK, CUM_BLK), 0)
        c = lax.broadcasted_iota(jnp.int32, (CUM_BLK, CUM_BLK), 1)
        tri = (r <= c).astype(f32)
        lane = lax.broadcasted_iota(jnp.int32, (CUM_BLK, LANES), 1)
        col_ref[...] = d_ref[...].T
        carry = jnp.zeros((1, LANES), f32)
        total = jnp.zeros((1, LANES), f32)
        for blk in reversed(range(nblk)):
            rows = slice(blk * CUM_BLK, (blk + 1) * CUM_BLK)
            cs = jnp.dot(tri, col_ref[rows, :], precision=lax.Precision.HIGHEST, preferred_element_type=f32) + carry
            carry = cs[0:1, :]
            z = f_ref[rows, :] + b_ref[...]
            df = jnp.where(lane < N_FOX_HEADS, cs * _sigmoid(-z), 0.0)
            df_ref[rows, :] = df.astype(df_ref.dtype)
            total = total + _colsum(df)
        db_ref[...] = total

    return pl.pallas_call(
        body, grid=(1,), in_specs=[pl.BlockSpec((LANES, SEQ), lambda i: (0, 0)),
                                   pl.BlockSpec((SEQ, LANES), lambda i: (0, C_F // LANES)),
                                   pl.BlockSpec((1, LANES), lambda i: (0, 0))],
        out_specs=[pl.BlockSpec((SEQ, LANES), lambda i: (0, 0)), pl.BlockSpec((1, LANES), lambda i: (0, 0))],
        out_shape=(SDS((SEQ, LANES), bf16), SDS((1, LANES), f32)), name="fox_gate_bwd",
        scratch_shapes=[pltpu.VMEM((SEQ, LANES), f32)],
        compiler_params=_params(("arbitrary",)),
    )(dF_row, proj, b_pad)


def _nt(a, b):
    return lax.dot_general(a, b, (((1,), (1,)), ((), ())), preferred_element_type=f32)


def _tn(a, b):
    return lax.dot_general(a, b, (((0,), (0,)), ((), ())), preferred_element_type=f32)


def _fox_prep(proj, f_col):
    def fn(t, v):
        q, k, vv, fc = t
        lane = lax.broadcasted_iota(jnp.int32, (q.shape[0], LANES), 1)
        qs, ks = [], []
        for h in range(N_FOX_HEADS):
            pair, pos = divmod(h, 2)
            own = (lane >= pos * HEAD_DIM) & (lane < (pos + 1) * HEAD_DIM)
            base = (1 - pos) * HEAD_DIM
            f = fc[:, h:h + 1]
            hi = f.astype(bf16).astype(f32)
            mid = (f - hi).astype(bf16).astype(f32)
            lo = (f - hi) - mid
            one = jnp.ones_like(f)
            qa = jnp.where(own, q[:, pair * LANES:(pair + 1) * LANES] * ATT_SCALE, 0.0)
            ka = k[:, pair * LANES:(pair + 1) * LANES]
            for idx, (qv, kv) in enumerate([(hi, one), (mid, one), (lo, one), (one, -hi), (one, -mid), (one, -lo)]):
                sel = lane == base + idx
                qa = jnp.where(sel, qv, qa)
                ka = jnp.where(sel, kv, ka)
            qs.append(qa)
            ks.append(ka)
        return [jnp.concatenate(qs, axis=1), jnp.concatenate(ks, axis=1), vv], []
    w = N_FOX_HEADS * LANES
    return _rowwise(fn, "fox_prep", [(proj, FOX_W, C_QA // FOX_W), (proj, FOX_W, C_KA // FOX_W),
                                     (proj, FOX_W, C_VA // FOX_W), (f_col, LANES, 0)], [],
                    [(w, bf16), (w, bf16), (FOX_W, bf16)])


def _fox_fwd(q_aug, k_aug, v):
    blk = FOX_BLK
    npair = FOX_W // LANES

    def body(q_ref, k_ref, v_ref, o_ref, lse_ref, s_scr):
        i = pl.program_id(1)
        tri = lax.broadcasted_iota(jnp.int32, (blk, blk), 0) >= lax.broadcasted_iota(jnp.int32, (blk, blk), 1)
        qh = [q_ref[:, h * LANES:(h + 1) * LANES] for h in range(2)]

        def logits(c, masked):
            off = pl.multiple_of(c * blk, blk)
            tops = []
            for h in range(2):
                s = _nt(qh[h], k_ref[pl.ds(off, blk), h * LANES:(h + 1) * LANES])
                if masked:
                    s = jnp.where(tri, s, NEG)
                s_scr[h, :, pl.ds(off, blk)] = s
                tops.append(_fold_lanes(s, jnp.maximum))
            return tops

        def pass_a(c, m):
            return tuple(jnp.maximum(a, b) for a, b in zip(m, logits(c, False)))

        m = lax.fori_loop(0, i, pass_a, tuple(jnp.full((blk, LANES), NEG, f32) for _ in range(2)))
        mx = [jnp.max(jnp.maximum(a, b), axis=1, keepdims=True) for a, b in zip(m, logits(i, True))]

        def pass_b(c, carry):
            off = pl.multiple_of(c * blk, blk)
            vv = v_ref[pl.ds(off, blk), :]
            new = []
            for h in range(2):
                l, acc = carry[h]
                p = jnp.exp(s_scr[h, :, pl.ds(off, blk)] - mx[h])
                new.append((l + _fold_lanes(p, jnp.add),
                            acc + jnp.dot(p.astype(bf16), vv, preferred_element_type=f32)))
            return tuple(new)

        zero = jnp.zeros((blk, LANES), f32)
        (l_a, acc_a), (l_b, acc_b) = lax.fori_loop(0, i + 1, pass_b, ((zero, zero), (zero, zero)))
        l_a = jnp.sum(l_a, axis=1, keepdims=True)
        l_b = jnp.sum(l_b, axis=1, keepdims=True)
        first = lax.broadcasted_iota(jnp.int32, (blk, LANES), 1) < HEAD_DIM
        o_ref[...] = jnp.where(first, acc_a / l_a, acc_b / l_b)
        lse_ref[0] = jnp.where(first, mx[0] + jnp.log(l_a), mx[1] + jnp.log(l_b))

    return pl.pallas_call(
        body, grid=(npair, SEQ // blk),
        in_specs=[pl.BlockSpec((blk, 2 * LANES), lambda p, i: (i, p)),
                  pl.BlockSpec((SEQ, 2 * LANES), lambda p, i: (0, p)),
                  pl.BlockSpec((SEQ, LANES), lambda p, i: (0, p))],
        out_specs=[pl.BlockSpec((blk, LANES), lambda p, i: (i, p)),
                   pl.BlockSpec((1, blk, LANES), lambda p, i: (p, i, 0))],
        out_shape=(SDS((SEQ, FOX_W), f32), SDS((npair, SEQ, LANES), f32)), name="fox_fwd",
        scratch_shapes=[pltpu.VMEM((2, blk, SEQ), f32)],
        compiler_params=_params(("parallel", "arbitrary")),
    )(q_aug, k_aug, v)


def _fox_bwd(q_aug, k_aug, v, do, o, lse):
    blk = FOX_BLK
    npair = FOX_W // LANES
    nblk = SEQ // blk

    def body(q_ref, k_ref, v_ref, do_ref, o_ref, lse_ref, dq_ref, dk_ref, dv_ref, df_ref,
             dq_acc, delta_ref, res_ref):
        lane_s = lax.broadcasted_iota(jnp.int32, (SEQ, LANES), 1)
        prod = do_ref[...] * o_ref[...]
        d_a = jnp.sum(jnp.where(lane_s < HEAD_DIM, prod, 0.0), axis=1, keepdims=True)
        d_b = jnp.sum(jnp.where(lane_s >= HEAD_DIM, prod, 0.0), axis=1, keepdims=True)
        delta_ref[...] = jnp.where(lane_s < HEAD_DIM, d_a, d_b)
        dq_acc[...] = jnp.zeros_like(dq_acc)
        res_ref[...] = jnp.zeros_like(res_ref)
        df_ref[...] = jnp.zeros_like(df_ref)
        lane = lax.broadcasted_iota(jnp.int32, (blk, LANES), 1)
        own = [lane < HEAD_DIM, lane >= HEAD_DIM]
        tri = lax.broadcasted_iota(jnp.int32, (blk, blk), 0) >= lax.broadcasted_iota(jnp.int32, (blk, blk), 1)

        def q_slab(qoff, h):
            return q_ref[pl.ds(qoff, blk), h * LANES:(h + 1) * LANES]

        def probs(qoff, h, k_h, masked):
            s = _nt(q_slab(qoff, h), k_h)
            if masked:
                s = jnp.where(tri, s, NEG)
            return jnp.exp(s - lse_ref[0, pl.ds(qoff, blk), h * HEAD_DIM:h * HEAD_DIM + 1])

        def k_slabs(koff):
            return [k_ref[pl.ds(koff, blk), h * LANES:(h + 1) * LANES] for h in range(2)]

        def kv_step(kj, _):
            koff = pl.multiple_of(kj * blk, blk)
            k_aug = k_slabs(koff)
            k_own = [jnp.where(own[h], k_aug[h], jnp.zeros_like(k_aug[h])) for h in range(2)]
            vv = v_ref[pl.ds(koff, blk), :]
            v_own = [jnp.where(own[h], vv, jnp.zeros_like(vv)) for h in range(2)]

            def q_tile(qi, carry, masked):
                qoff = pl.multiple_of(qi * blk, blk)
                dd = do_ref[pl.ds(qoff, blk), :].astype(bf16)
                new, dq_add = [], None
                for h in range(2):
                    dk_h, dv_h, dcol = carry[h]
                    p = probs(qoff, h, k_aug[h], masked)
                    dl = p * (_nt(dd, v_own[h]) - delta_ref[pl.ds(qoff, blk), h * HEAD_DIM:h * HEAD_DIM + 1])
                    dlb = dl.astype(bf16)
                    part = jnp.dot(dlb, k_own[h], preferred_element_type=f32)
                    dq_add = part if dq_add is None else dq_add + part
                    res_ref[h, pl.ds(qoff, blk), :] += _fold_lanes(dl, jnp.add)
                    new.append((dk_h + _tn(dlb, q_slab(qoff, h)), dv_h + _tn(p.astype(bf16), dd),
                                dcol + _colsum(dl)))
                dq_acc[pl.ds(qoff, blk), :] += dq_add * ATT_SCALE
                return tuple(new)

            zero = (jnp.zeros((blk, LANES), f32), jnp.zeros((blk, LANES), f32), jnp.zeros((1, blk), f32))
            carry = q_tile(kj, (zero, zero), True)
            (dk_a, dv_a, dcol_a), (dk_b, dv_b, dcol_b) = lax.fori_loop(
                kj + 1, nblk, lambda qi, cr: q_tile(qi, cr, False), carry)
            dk_ref[pl.ds(koff, blk), :] = jnp.where(own[0], dk_a, dk_b).astype(dk_ref.dtype)
            dv_ref[pl.ds(koff, blk), :] = jnp.where(own[0], dv_a, dv_b).astype(dv_ref.dtype)
            df_ref[0, 0:1, pl.ds(koff, blk)] = -dcol_a
            df_ref[0, 1:2, pl.ds(koff, blk)] = -dcol_b
            return 0

        lax.fori_loop(0, nblk, kv_step, 0)
        dq_ref[...] = dq_acc[...].astype(dq_ref.dtype)

        for h in range(2):
            res_ref[h] = jnp.zeros((SEQ, LANES), f32) + jnp.sum(res_ref[h], axis=1, keepdims=True)

        def kv_fix(kj, _):
            koff = pl.multiple_of(kj * blk, blk)
            k_aug = k_slabs(koff)

            def q_fix(qi, corr, masked):
                qoff = pl.multiple_of(qi * blk, blk)
                return tuple(corr[h] + _colsum(probs(qoff, h, k_aug[h], masked) * res_ref[h, pl.ds(qoff, blk), 0:1])
                             for h in range(2))

            zero = jnp.zeros((1, blk), f32)
            corr = lax.fori_loop(kj + 1, nblk, lambda qi, cr: q_fix(qi, cr, False), q_fix(kj, (zero, zero), True))
            df_ref[0, 0:1, pl.ds(koff, blk)] += corr[0]
            df_ref[0, 1:2, pl.ds(koff, blk)] += corr[1]
            return 0

        lax.fori_loop(0, nblk, kv_fix, 0)

    pair_aug = pl.BlockSpec((SEQ, 2 * LANES), lambda p: (0, p))
    slab = pl.BlockSpec((SEQ, LANES), lambda p: (0, p))
    per_pair = pl.BlockSpec((1, SEQ, LANES), lambda p: (p, 0, 0))
    rows = pl.BlockSpec((1, 8, SEQ), lambda p: (p, 0, 0))
    return pl.pallas_call(
        body, grid=(npair,),
        in_specs=[pair_aug, pair_aug, slab, slab, slab, per_pair],
        out_specs=[slab, slab, slab, rows],
        out_shape=(SDS((SEQ, FOX_W), bf16),) * 3 + (SDS((npair, 8, SEQ), f32),), name="fox_bwd",
        scratch_shapes=[pltpu.VMEM((SEQ, LANES), f32), pltpu.VMEM((SEQ, LANES), f32),
                        pltpu.VMEM((2, SEQ, LANES), f32)],
        compiler_params=_params(("parallel",)),
    )(q_aug, k_aug, v, do, o, lse)


DIL_BLK = 128
DILATIONS = (1, 4, 16)
N_GROUPS = len(DILATIONS)
DIL_PAIRS = DIL_OUT_W // LANES


def _dil_blocks(d):
    r1 = lax.broadcasted_iota(jnp.int32, (DIL_BLK, DIL_BLK), 0)
    c1 = lax.broadcasted_iota(jnp.int32, (DIL_BLK, DIL_BLK), 1)
    r2 = lax.broadcasted_iota(jnp.int32, (DIL_BLK, 2 * DIL_BLK), 0)
    c2 = lax.broadcasted_iota(jnp.int32, (DIL_BLK, 2 * DIL_BLK), 1)
    band = ((c2 < DIL_BLK) & (c2 >= r2)) | ((c2 >= DIL_BLK) & (c2 - DIL_BLK <= r2))
    out = []
    for r in range(d):
        for b in range(SEQ // d // DIL_BLK):
            rows = pl.ds(r + d * DIL_BLK * b, DIL_BLK, stride=d)
            if b == 0:
                out.append((rows, rows, r1 >= c1))
            else:
                out.append((rows, pl.ds(r + d * DIL_BLK * (b - 1), 2 * DIL_BLK, stride=d), band))
    return out


def _dil_fwd(q, k, v, g):
    def body(q_ref, k_ref, v_ref, o_ref, lse_ref):
        first = lax.broadcasted_iota(jnp.int32, (DIL_BLK, LANES), 1) < HEAD_DIM
        for rows, krows, mask in _dil_blocks(DILATIONS[g]):
            qv, kk, vv = q_ref[rows, :].astype(bf16), k_ref[krows, :].astype(bf16), v_ref[krows, :].astype(bf16)
            outs, lses = [], []
            for own in (first, ~first):
                s = jnp.where(mask, _nt(jnp.where(own, qv, jnp.zeros_like(qv)), kk), NEG)
                m = jnp.max(s, axis=1, keepdims=True)
                p = jnp.exp(s - m)
                l = jnp.sum(p, axis=1, keepdims=True)
                outs.append(jnp.dot(p.astype(bf16), vv, preferred_element_type=f32) / l)
                lses.append(m + jnp.log(l))
            o_ref[rows, :] = jnp.where(first, outs[0], outs[1])
            lse_ref[rows, :] = jnp.where(first, lses[0], lses[1])

    grouped = pl.BlockSpec((SEQ, LANES), lambda p: (0, DIL_PAIRS * g + p))
    own = pl.BlockSpec((SEQ, LANES), lambda p: (0, p))
    shape = SDS((SEQ, DIL_OUT_W), f32)
    return pl.pallas_call(
        body, grid=(DIL_PAIRS,), in_specs=[grouped] * 3, out_specs=[own] * 2, out_shape=(shape, shape),
        name=f"dil_fwd_{DILATIONS[g]}", compiler_params=_params(("parallel",)),
    )(q, k, v)


def _dil_bwd(q, k, v, do, lse, delta, g):
    def body(q_ref, k_ref, v_ref, do_ref, lse_ref, dl_ref, dq_ref, dk_ref, dv_ref):
        first = lax.broadcasted_iota(jnp.int32, (DIL_BLK, LANES), 1) < HEAD_DIM
        dk_ref[...] = jnp.zeros_like(dk_ref)
        dv_ref[...] = jnp.zeros_like(dv_ref)
        for rows, krows, mask in _dil_blocks(DILATIONS[g]):
            qv, kk, vv = q_ref[rows, :].astype(bf16), k_ref[krows, :].astype(bf16), v_ref[krows, :].astype(bf16)
            dov = do_ref[rows, :].astype(bf16)
            lsev, delv = lse_ref[rows, :], dl_ref[rows, :]
            dqs, dk_add, dv_add = [], None, None
            for h, own in enumerate((first, ~first)):
                col = h * HEAD_DIM
                qh = jnp.where(own, qv, jnp.zeros_like(qv))
                doh = jnp.where(own, dov, jnp.zeros_like(dov))
                p = jnp.exp(jnp.where(mask, _nt(qh, kk), NEG) - lsev[:, col:col + 1])
                dl = (p * (_nt(doh, vv) - delv[:, col:col + 1])).astype(bf16)
                dqs.append(jnp.dot(dl, kk, preferred_element_type=f32))
                dk_h, dv_h = _tn(dl, qh), _tn(p.astype(bf16), doh)
                dk_add = dk_h if dk_add is None else dk_add + dk_h
                dv_add = dv_h if dv_add is None else dv_add + dv_h
            dq_ref[rows, :] = jnp.where(first, dqs[0], dqs[1]) * ATT_SCALE
            dk_ref[krows, :] += dk_add
            dv_ref[krows, :] += dv_add

    grouped = pl.BlockSpec((SEQ, LANES), lambda p: (0, DIL_PAIRS * g + p))
    own = pl.BlockSpec((SEQ, LANES), lambda p: (0, p))
    shape = SDS((SEQ, DIL_OUT_W), f32)
    return pl.pallas_call(
        body, grid=(DIL_PAIRS,), in_specs=[grouped] * 3 + [own] * 3, out_specs=[own] * 3,
        out_shape=(shape, shape, shape), name=f"dil_bwd_{DILATIONS[g]}", compiler_params=_params(("parallel",)),
    )(q, k, v, do, lse, delta)


def _position():
    return lax.axis_index("x"), lax.axis_index("y"), lax.axis_index("c")


def _all_gather(block, name):
    def body(x_ref, out_ref, send_sems, recv_sems, local_sem):
        x, y, c = _position()
        me, sibling = (x, y, c), (x, y, 1 - c)
        chips = [(1 - x, y), (x, 1 - y), (1 - x, 1 - y)]

        def slot(px, py, pc):
            return out_ref.at[4 * px + 2 * py + pc]

        def copy(k, blk, to, src=None):
            return pltpu.make_async_remote_copy(
                src_ref=slot(*blk) if src is None else src, dst_ref=slot(*blk),
                send_sem=send_sems.at[k], recv_sem=recv_sems.at[k], device_id=to, device_id_type=MESH)

        mine = pltpu.make_async_copy(x_ref, slot(*me), local_sem)
        mine.start()
        first = [copy(0, me, sibling, src=x_ref)]
        first += [copy(1 + j, me, (*chip, c), src=x_ref) for j, chip in enumerate(chips)]
        for cp in first:
            cp.start()
        passed = [copy(4 + j, (*chip, c), sibling) for j, chip in enumerate(chips)]
        for j, chip in enumerate(chips):
            copy(1 + j, (*chip, c), me).wait_recv()
            passed[j].start()
        copy(0, sibling, me).wait_recv()
        for j, chip in enumerate(chips):
            copy(4 + j, (*chip, 1 - c), me).wait_recv()
        for cp in first + passed:
            cp.wait_send()
        mine.wait()

    return pl.pallas_call(
        body, out_shape=SDS((N_DEV,) + block.shape, block.dtype),
        in_specs=[pl.BlockSpec(memory_space=pl.ANY)], out_specs=pl.BlockSpec(memory_space=pl.ANY),
        scratch_shapes=[pltpu.SemaphoreType.DMA((7,)), pltpu.SemaphoreType.DMA((7,)), pltpu.SemaphoreType.DMA],
        name=name,
    )(block)


def _all_gather_many(blocks, name):
    n = len(blocks)

    def body(*refs):
        x_refs, out_refs = refs[:n], refs[n:2 * n]
        send_sems, recv_sems, local_sems = refs[2 * n:]
        x, y, c = _position()
        me, sibling = (x, y, c), (x, y, 1 - c)
        chips = [(1 - x, y), (x, 1 - y), (1 - x, 1 - y)]

        def slot(a, px, py, pc):
            return out_refs[a].at[4 * px + 2 * py + pc]

        def copy(a, k, blk, to, own=False):
            return pltpu.make_async_remote_copy(
                src_ref=x_refs[a] if own else slot(a, *blk), dst_ref=slot(a, *blk),
                send_sem=send_sems.at[a, k], recv_sem=recv_sems.at[a, k], device_id=to, device_id_type=MESH)

        mine = [pltpu.make_async_copy(x_refs[a], slot(a, *me), local_sems.at[a]) for a in range(n)]
        for cp in mine:
            cp.start()
        started = []
        for a in range(n):
            first = [copy(a, 0, me, sibling, own=True)]
            first += [copy(a, 1 + j, me, (*chip, c), own=True) for j, chip in enumerate(chips)]
            for cp in first:
                cp.start()
            started += first
        for a in range(n):
            for j, chip in enumerate(chips):
                copy(a, 1 + j, (*chip, c), me).wait_recv()
                passed = copy(a, 4 + j, (*chip, c), sibling)
                passed.start()
                started.append(passed)
        for a in range(n):
            copy(a, 0, sibling, me).wait_recv()
            for j, chip in enumerate(chips):
                copy(a, 4 + j, (*chip, 1 - c), me).wait_recv()
        for cp in started:
            cp.wait_send()
        for cp in mine:
            cp.wait()

    hbm = pl.BlockSpec(memory_space=pl.ANY)
    return pl.pallas_call(
        body, out_shape=[SDS((N_DEV,) + b.shape, b.dtype) for b in blocks],
        in_specs=[hbm] * n, out_specs=[hbm] * n,
        scratch_shapes=[pltpu.SemaphoreType.DMA((n, 7)), pltpu.SemaphoreType.DMA((n, 7)),
                        pltpu.SemaphoreType.DMA((n,))],
        name=name,
    )(*blocks)


HBM_SPEC = pl.BlockSpec(memory_space=pltpu.HBM)
SEM_SPEC = pl.BlockSpec(memory_space=pltpu.SEMAPHORE)
SPLIT_COPY = pltpu.CompilerParams(has_side_effects=pltpu.SideEffectType.DATAFLOW_SIDE_EFFECTING)


def _in_hbm(t):
    return pltpu.with_memory_space_constraint(t, pltpu.HBM)


def _pair_copies(g_refs, land_refs, send_sems, recv_sems):
    x, y, c = _position()
    return [pltpu.make_async_remote_copy(
        src_ref=g.at[2 * k + (1 - c)], dst_ref=land.at[k], send_sem=send_sems.at[4 * a + k],
        recv_sem=recv_sems.at[4 * a + k], device_id=(x, y, 1 - c), device_id_type=MESH)
        for a, (g, land) in enumerate(zip(g_refs, land_refs, strict=True)) for k in range(4)]


def _chip_copies(t_refs, land_refs, send_sems, recv_sems):
    x, y, c = _position()
    chips = [(1 - x, y), (x, 1 - y), (1 - x, 1 - y)]
    return [pltpu.make_async_remote_copy(
        src_ref=t.at[2 * px + py], dst_ref=land.at[j], send_sem=send_sems.at[3 * a + j],
        recv_sem=recv_sems.at[3 * a + j], device_id=(px, py, c), device_id_type=MESH)
        for a, (t, land) in enumerate(zip(t_refs, land_refs, strict=True)) for j, (px, py) in enumerate(chips)]


_ROUNDS = {"pair": (_pair_copies, 4), "chip": (_chip_copies, 3)}


def _exchange_start(kind, ts, name):
    copies, slots = _ROUNDS[kind]
    n = len(ts)
    lands = [_in_hbm(lax.empty((slots,) + t.shape[1:], t.dtype)) for t in ts]

    def body(*refs):
        for cp in copies(refs[:n], refs[n:2 * n], refs[2 * n], refs[2 * n + 1]):
            cp.start()
        refs[-1][...] = jnp.zeros_like(refs[-1])

    sems = pltpu.SemaphoreType.DMA((slots * n,))
    res = pl.pallas_call(
        body, name=name, in_specs=[HBM_SPEC] * (2 * n),
        out_shape=(sems, sems, *[pltpu.HBM(t.shape, t.dtype) for t in (*ts, *lands)], SDS((8, LANES), f32)),
        out_specs=(SEM_SPEC, SEM_SPEC, *[HBM_SPEC] * (2 * n), pl.BlockSpec(memory_space=pltpu.VMEM)),
        input_output_aliases={i: 2 + i for i in range(2 * n)}, compiler_params=SPLIT_COPY,
    )(*[_in_hbm(t) for t in ts], *lands)
    return res[:-1], res[-1]


def _exchange_wait(kind, state, after, name):
    copies, _ = _ROUNDS[kind]
    send_sems, recv_sems, *arrays = state
    n = len(arrays) // 2

    def body(*refs):
        for cp in copies(refs[:n], refs[n:2 * n], refs[2 * n], refs[2 * n + 1]):
            cp.wait_send()
            cp.wait_recv()

    res = pl.pallas_call(
        body, name=name, in_specs=[HBM_SPEC] * (2 * n) + [SEM_SPEC, SEM_SPEC, pl.BlockSpec(memory_space=pl.ANY)],
        out_shape=[pltpu.HBM(t.shape, t.dtype) for t in arrays], out_specs=[HBM_SPEC] * (2 * n),
        input_output_aliases={i: i for i in range(2 * n)}, compiler_params=SPLIT_COPY,
    )(*arrays, send_sems, recv_sems, after)
    return res[:n], res[n:]


def _gather_copies(x_refs, out_refs, send_sems, recv_sems):
    x, y, c = _position()
    peers = [(x, y, 1 - c), (1 - x, y, c), (x, 1 - y, c), (1 - x, 1 - y, c)]
    sends, arrivals = [], []
    for a, (x_ref, out_ref) in enumerate(zip(x_refs, out_refs, strict=True)):
        for k, (px, py, pc) in enumerate(peers):
            sems = dict(send_sem=send_sems.at[4 * a + k], recv_sem=recv_sems.at[4 * a + k],
                        device_id=(px, py, pc), device_id_type=MESH)
            sends.append(pltpu.make_async_remote_copy(src_ref=x_ref, dst_ref=out_ref.at[4 * x + 2 * y + c], **sems))
            arrivals.append(pltpu.make_async_remote_copy(src_ref=x_ref, dst_ref=out_ref.at[4 * px + 2 * py + pc],
                                                         **sems))
    return sends, arrivals


def _gather_start(blocks, after, name):
    n = len(blocks)
    outs = [_in_hbm(lax.empty((N_DEV,) + b.shape, b.dtype)) for b in blocks]

    def body(*refs):
        sends, _ = _gather_copies(refs[:n], refs[n:2 * n], refs[2 * n + 1], refs[2 * n + 2])
        for cp in sends:
            cp.start()
        refs[-1][...] = jnp.zeros_like(refs[-1])

    sems = pltpu.SemaphoreType.DMA((4 * n,))
    res = pl.pallas_call(
        body, name=name, in_specs=[HBM_SPEC] * (2 * n) + [pl.BlockSpec(memory_space=pl.ANY)],
        out_shape=(sems, sems, *[pltpu.HBM(t.shape, t.dtype) for t in (*blocks, *outs)], SDS((8, LANES), f32)),
        out_specs=(SEM_SPEC, SEM_SPEC, *[HBM_SPEC] * (2 * n), pl.BlockSpec(memory_space=pltpu.VMEM)),
        input_output_aliases={i: 2 + i for i in range(2 * n)}, compiler_params=SPLIT_COPY,
    )(*[_in_hbm(b) for b in blocks], *outs, after)
    return res[:-1], res[-1]


def _gather_wait(state, after, name):
    send_sems, recv_sems, *arrays = state
    n = len(arrays) // 2

    def body(*refs):
        sends, arrivals = _gather_copies(refs[:n], refs[n:2 * n], refs[2 * n], refs[2 * n + 1])
        for cp in sends:
            cp.wait_send()
        for cp in arrivals:
            cp.wait_recv()

    res = pl.pallas_call(
        body, name=name, in_specs=[HBM_SPEC] * (2 * n) + [SEM_SPEC, SEM_SPEC, pl.BlockSpec(memory_space=pl.ANY)],
        out_shape=[pltpu.HBM(t.shape, t.dtype) for t in arrays], out_specs=[HBM_SPEC] * (2 * n),
        input_output_aliases={i: i for i in range(2 * n)}, compiler_params=SPLIT_COPY,
    )(*arrays, send_sems, recv_sems, after)
    return res[:n], res[n:]


def _gather_finish(partial, name):
    n = len(partial)

    def body(*refs):
        in_refs, out_refs = refs[:n], refs[n:2 * n]
        send_sems, recv_sems = refs[2 * n:]
        x, y, c = _position()
        chips = [(1 - x, y), (x, 1 - y), (1 - x, 1 - y)]
        copies = []
        for a in range(n):
            for j, (px, py) in enumerate(chips):
                cp = pltpu.make_async_remote_copy(
                    src_ref=in_refs[a].at[4 * px + 2 * py + c], dst_ref=out_refs[a].at[4 * px + 2 * py + c],
                    send_sem=send_sems.at[a, j], recv_sem=recv_sems.at[a, j], device_id=(x, y, 1 - c),
                    device_id_type=MESH)
                cp.start()
                copies.append(cp)
        for a in range(n):
            for j, (px, py) in enumerate(chips):
                pltpu.make_async_remote_copy(
                    src_ref=in_refs[a].at[4 * px + 2 * py + (1 - c)], dst_ref=out_refs[a].at[4 * px + 2 * py + (1 - c)],
                    send_sem=send_sems.at[a, j], recv_sem=recv_sems.at[a, j], device_id=(x, y, 1 - c),
                    device_id_type=MESH).wait_recv()
        for cp in copies:
            cp.wait_send()

    hbm = pl.BlockSpec(memory_space=pl.ANY)
    return pl.pallas_call(
        body, out_shape=[SDS(p.shape, p.dtype) for p in partial], in_specs=[hbm] * n, out_specs=[hbm] * n,
        input_output_aliases={a: a for a in range(n)},
        scratch_shapes=[pltpu.SemaphoreType.DMA((n, 3)), pltpu.SemaphoreType.DMA((n, 3))],
        name=name,
    )(*partial)


def _row_tile(rows):
    return 256 if rows % 256 == 0 and rows > 512 else rows


def _pair_add(g, r1, core, name):
    def body(c_ref, g_ref, r_ref, o_ref):
        o_ref[...] = (g_ref[...].astype(f32) + r_ref[...].astype(f32)).astype(o_ref.dtype)

    rows, cols = g.shape[1:]
    tile = _row_tile(rows)
    blk = (1, tile, cols)
    return pl.pallas_call(
        body, out_shape=SDS((4, rows, cols), g.dtype), name=name,
        grid_spec=pltpu.PrefetchScalarGridSpec(
            num_scalar_prefetch=1, grid=(4, rows // tile),
            in_specs=[pl.BlockSpec(blk, lambda k, i, c_ref: (2 * k + c_ref[0], i, 0)),
                      pl.BlockSpec(blk, lambda k, i, c_ref: (k, i, 0))],
            out_specs=pl.BlockSpec(blk, lambda k, i, c_ref: (k, i, 0))),
        compiler_params=_params(("parallel", "arbitrary")),
    )(core, g, r1)


def _chip_add(t, r2, chip, name):
    def body(c_ref, t_ref, r_ref, o_ref):
        o_ref[...] = ((t_ref[0].astype(f32) + r_ref[0].astype(f32)) + r_ref[1].astype(f32)) + r_ref[2].astype(f32)

    rows, cols = t.shape[1:]
    tile = _row_tile(rows)
    return pl.pallas_call(
        body, out_shape=SDS((rows, cols), f32), name=name,
        grid_spec=pltpu.PrefetchScalarGridSpec(
            num_scalar_prefetch=1, grid=(rows // tile,),
            in_specs=[pl.BlockSpec((1, tile, cols), lambda i, c_ref: (c_ref[0], i, 0)),
                      pl.BlockSpec((3, tile, cols), lambda i, c_ref: (0, i, 0))],
            out_specs=pl.BlockSpec((tile, cols), lambda i, c_ref: (i, 0))),
        compiler_params=_params(("arbitrary",)),
    )(chip, t, r2)


def _pad_to(t, axis, size):
    pads = [(0, 0)] * t.ndim
    pads[axis] = (0, size - t.shape[axis])
    return jnp.pad(t, pads)


_REF_COLS = {"qa": (0, FOX_W), "ka": (FOX_W, FOX_W), "va": (2 * FOX_W, FOX_W), "f": (3 * FOX_W, N_FOX_HEADS)}
_REF_COLS.update({n: (3 * FOX_W + N_FOX_HEADS + i * DIL_W, DIL_W) for i, n in enumerate(("qb", "kb", "vb"))})
_REF_COLS.update({n: (3 * FOX_W + N_FOX_HEADS + 3 * DIL_W + i * D, D) for i, n in enumerate(("ga", "gb"))})
_REF_ORDER = ("qa", "ka", "va", "f", "qb", "kb", "vb", "ga", "gb")


def _shard_pad_cols(pieces):
    first = pieces[_REF_ORDER[0]]
    pad = jnp.zeros((first.shape[0], W_IN_PAD - W_IN_SH), first.dtype)
    parts, names, used = [], list(_REF_ORDER), 0
    for _ in range(N_DEV):
        need = W_IN_SH
        while need:
            take = min(need, _REF_COLS[names[0]][1] - used)
            parts.append(pieces[names[0]][:, used:used + take])
            need, used = need - take, used + take
            if used == _REF_COLS[names[0]][1]:
                names, used = names[1:], 0
        parts.append(pad)
    return jnp.concatenate(parts, axis=1)


def _slab_w_in(stack):
    def cols(name):
        lo, width = _REF_COLS[name]
        hi, out = lo + width, []
        while lo < hi:
            j, off = divmod(lo, W_IN_SH)
            n = min(hi - lo, W_IN_SH - off)
            out.append(stack[j, :, off:off + n])
            lo += n
        return out
    z = lambda n: [jnp.zeros((stack.shape[1], n), stack.dtype)]
    parts = (cols("ga") + cols("gb") + z(C_QB - 2 * D) + cols("qb") + cols("kb") + cols("vb") + cols("qa")
             + cols("ka") + cols("va") + cols("f") + z(LANES - N_FOX_HEADS))
    return jnp.concatenate(parts, axis=1)


def kernel(x, c, w_ada, b_ada, g_mix, w_in, b_fgate, w_br_a, w_br_b, w_out, g_ffn, w_ffn_gate, w_ffn_up, w_ffn_down, g_final, loss_target, m_w_ada, m_b_ada, m_g_mix, m_w_in, m_b_fgate, m_w_br_a, m_w_br_b, m_w_out, m_g_ffn, m_w_ffn_gate, m_w_ffn_up, m_w_ffn_down, m_g_final, v_w_ada, v_b_ada, v_g_mix, v_w_in, v_b_fgate, v_w_br_a, v_w_br_b, v_w_out, v_g_ffn, v_w_ffn_gate, v_w_ffn_up, v_w_ffn_down, v_g_final):
    px, py, pc = _position()
    dev = 4 * px + 2 * py + pc
    x2d, tgt = x[0], loss_target[0]

    c_all = _all_gather(c, "gather_c").reshape(N_DEV, D)
    ada_cols = w_ada.shape[2]
    b_shard = lax.dynamic_slice(b_ada, (0, dev * ada_cols), (1, ada_cols))
    mod_shard = _ada_fwd(c_all, w_ada[0], b_shard)
    mod_all = _all_gather(mod_shard, "gather_mod")
    modv = lax.dynamic_index_in_dim(mod_all, dev, axis=1, keepdims=False).reshape(6, D)

    gate_up = jnp.concatenate([_pad_to(w_ffn_gate[0], 1, FF_PAD), _pad_to(w_ffn_up[0], 1, FF_PAD)], axis=1)
    w_in_s, = _all_gather_many([_pad_to(w_in[0], 1, W_IN_PAD).astype(bf16)], "gather_w_in")
    later = [w_br_a[0], w_br_b[0], w_out[0], gate_up, _pad_to(w_ffn_down[0], 0, FF_PAD)]
    later_state, later_token = _gather_start([t.astype(bf16) for t in later], w_in_s, "gather_rest_start")
    w_in_p = _slab_w_in(w_in_s)

    h1 = _pre1(x2d, modv, g_mix)
    proj = _matmul(h1, w_in_p, name="mm_proj", tm=SEQ, tn=896, tk=D, after=later_token)
    b_pad = jnp.pad(b_fgate, ((0, 0), (0, LANES - N_FOX_HEADS)))
    q_aug, k_aug, va = _fox_prep(proj, _fox_gate_fwd(proj, b_pad))
    ya_h, lse_a = _fox_fwd(q_aug, k_aug, va)

    tables = _rope_tables()
    qb_r, kb_r, vb = _rope_fwd(proj, tables)
    by_group = [_dil_fwd(qb_r, kb_r, vb, grp) for grp in range(N_GROUPS)]
    yb_h, lse_b = _dil_combine([o for o, _ in by_group], [l for _, l in by_group])

    mine, arrived = _gather_wait(later_state, yb_h, "gather_rest_wait")
    w_a_s, w_b_s, w_o_s, w_gu_s, w_d_s = [
        lax.dynamic_update_slice(stack, block[None], (dev, 0, 0))
        for stack, block in zip(_gather_finish(arrived, "gather_rest_finish"), mine, strict=True)]
    w_o = w_o_s.reshape(D, D)
    w_d = w_d_s.reshape(FF_HID, D)
    ya = _matmul_stack(ya_h, w_a_s, name="mm_br_a")
    yb = _matmul_stack(yb_h, w_b_s, name="mm_br_b")

    merged = _merge_fwd(ya, yb, proj)
    mix = _matmul(merged, w_o, name="mm_out", tm=SEQ, tn=512, tk=D)
    x1, h2 = _post1(x2d, mix, modv, g_ffn)
    act, au = _ffn_in(h2, w_gu_s)
    ff = _matmul(act, w_d, name="mm_ffn_down", tm=SEQ, tn=512, tk=FF_HID // 2)

    dx2, dff, dg_final, dga_f, loss_lanes = _final(x1, ff, tgt, modv, g_final.reshape(1, D))
    dact = _matmul(dff, w_d, tb=True, name="mm_d_act", tm=SEQ // 2, tn=FF_HID // 2, tk=D)
    dau = _swiglu_bwd(au, dact)

    core = pc.astype(jnp.int32).reshape(1)
    chip = (2 * px + py).astype(jnp.int32).reshape(1)

    def pair_done(state, after, tags, name):
        mine, theirs = _exchange_wait("pair", state, after, "pair_wait_" + name)
        sums = [_pair_add(g, r, core, "pair_add_" + t) for g, r, t in zip(mine, theirs, tags)]
        return _exchange_start("chip", sums, "chip_start_" + name)

    def from_chips(state, after, tags, name):
        sums, got = _exchange_wait("chip", state, after, "chip_wait_" + name)
        return [_chip_add(p, r, chip, "chip_add_" + t) for p, r, t in zip(sums, got, tags)]

    g_gu = _matmul(h2, dau, ta=True, by_shard=True, out_dtype=bf16, name="mm_g_ffn_in", tm=D, tn=2 * FF_PAD, tk=SEQ)
    g_d = _matmul(act, dff, ta=True, out_dtype=bf16, name="mm_g_down", tm=FF_HID // 2, tn=512, tk=SEQ)
    ffn_tags = ["gu", "down"]
    ffn_pair, ffn_pair_token = _exchange_start("pair", [g_gu, g_d.reshape(N_DEV, FF_PAD, D)], "pair_start_ffn")

    dh2 = _matmul(dau, w_gu_s, tb=True, by_shard=True, name="mm_d_h2", tm=SEQ // 2, tn=D, tk=2 * FF_PAD,
                  after=ffn_pair_token)
    ffn_state, ffn_token = pair_done(ffn_pair, dh2, ffn_tags, "ffn")
    dx1, dmix, dsh_f, dsc_f, dg_ffn, dga_m = _mid_bwd(dh2, x1, dx2, mix, modv, g_ffn)
    dmerged = _matmul(dmix, w_o, tb=True, name="mm_d_merged", tm=SEQ, tn=512, tk=D, after=ffn_token)
    dya, dyb, dga, dgb = _merge_bwd(dmerged, ya, yb, proj)
    dya_h = _matmul_stack(dya, w_a_s, tb=True, name="mm_d_ya")
    dyb_h = _matmul_stack(dyb, w_b_s, tb=True, name="mm_d_yb")

    dqa, dka, dva, dF = _fox_bwd(q_aug, k_aug, va, dya_h, ya_h, lse_a)
    dF_row = jnp.pad(dF[:, :2, :].reshape(N_FOX_HEADS, SEQ), ((0, LANES - N_FOX_HEADS), (0, 0)))
    df, db_fgate = _fox_gate_bwd(dF_row, proj, b_pad)

    delta_b = _dil_delta(dyb_h, yb_h)
    dil_grads = [_dil_bwd(qb_r, kb_r, vb, dyb_h, lse_b, delta_b, grp) for grp in range(N_GROUPS)]
    dqb, dkb = _rope_bwd([t[0] for t in dil_grads], [t[1] for t in dil_grads], tables)
    dvb = jnp.concatenate([t[2] for t in dil_grads], axis=1).astype(bf16)

    dproj = _shard_pad_cols({"qa": dqa, "ka": dka, "va": dva, "f": df[:, :N_FOX_HEADS], "qb": dqb, "kb": dkb,
                             "vb": dvb, "ga": dga, "gb": dgb})
    g_in = _matmul(h1, dproj, ta=True, by_shard=True, out_dtype=bf16, name="mm_g_in", tm=D, tn=W_IN_PAD, tk=SEQ)
    g_o = _matmul(merged, dmix, ta=True, out_dtype=bf16, name="mm_g_out", tm=D, tn=512, tk=SEQ)
    g_a = _matmul_stack(ya_h, dya, ta=True, out_dtype=bf16, name="mm_g_br_a")
    g_b = _matmul_stack(yb_h, dyb, ta=True, out_dtype=bf16, name="mm_g_br_b")
    rows_a, rows_b = FOX_W * W_BR_SH // D, DIL_OUT_W * W_BR_SH // D
    g_small = jnp.concatenate([g_a.reshape(N_DEV, rows_a, D), g_b.reshape(N_DEV, rows_b, D),
                               g_o.reshape(N_DEV, W_BR_SH, D)], axis=1)
    mix_tags = ["in", "small"]
    mix_pair, mix_pair_token = _exchange_start("pair", [g_in, g_small], "pair_start_mixer")

    dh1 = _matmul(dproj, w_in_s, tb=True, by_shard=True, name="mm_d_h1", tm=SEQ // 2, tn=D, tk=W_IN_PAD,
                  after=mix_pair_token)
    grad_x, dsh_m, dsc_m, dg_mix = _first_bwd(dh1, x2d, dx1, modv, g_mix)

    pad_lane = lambda t: jnp.pad(t, ((0, 0), (0, D - t.shape[1])))
    small = jnp.concatenate([dsh_m, dsc_m, dga_m, dsh_f, dsc_f, dga_f, dg_mix, dg_ffn, dg_final,
                             pad_lane(db_fgate), loss_lanes, jnp.zeros((SMALL_ROWS - 11, D), f32)], axis=0)
    small_all = _all_gather(small, "gather_small")
    mix_state, mix_token = pair_done(mix_pair, small_all, mix_tags, "mixer")

    small_sum, loss_row = _small_reduce(small_all, mix_token)
    dmod_all = small_all[:, :6, :].reshape(N_DEV, 6 * D)
    g_w_ada = _ada_bwd(c_all, lax.dynamic_slice(dmod_all, (0, dev * ada_cols), (N_DEV, ada_cols)))
    s_gu, s_d = from_chips(ffn_state, small_sum, ffn_tags, "ffn")

    loss = loss_row[0, 0]
    g = {
        "w_ada": g_w_ada[None], "b_ada": small_sum[0:6].reshape(1, 6 * D), "g_mix": small_sum[6:7],
        "b_fgate": small_sum[9:10, :N_FOX_HEADS], "g_ffn": small_sum[7:8], "w_ffn_gate": s_gu[None, :, :W_FF_SH],
        "w_ffn_up": s_gu[None, :, FF_PAD:FF_PAD + W_FF_SH], "w_ffn_down": s_d[None, :W_FF_SH],
        "g_final": small_sum[8],
    }
    w = {"w_ada": w_ada, "b_ada": b_ada, "g_mix": g_mix, "w_in": w_in, "b_fgate": b_fgate, "w_br_a": w_br_a,
         "w_br_b": w_br_b, "w_out": w_out, "g_ffn": g_ffn, "w_ffn_gate": w_ffn_gate, "w_ffn_up": w_ffn_up,
         "w_ffn_down": w_ffn_down, "g_final": g_final}
    m = {"w_ada": m_w_ada, "b_ada": m_b_ada, "g_mix": m_g_mix, "w_in": m_w_in, "b_fgate": m_b_fgate,
         "w_br_a": m_w_br_a, "w_br_b": m_w_br_b, "w_out": m_w_out, "g_ffn": m_g_ffn, "w_ffn_gate": m_w_ffn_gate,
         "w_ffn_up": m_w_ffn_up, "w_ffn_down": m_w_ffn_down, "g_final": m_g_final}
    v = {"w_ada": v_w_ada, "b_ada": v_b_ada, "g_mix": v_g_mix, "w_in": v_w_in, "b_fgate": v_b_fgate,
         "w_br_a": v_w_br_a, "w_br_b": v_w_br_b, "w_out": v_w_out, "g_ffn": v_g_ffn, "w_ffn_gate": v_w_ffn_gate,
         "w_ffn_up": v_w_ffn_up, "w_ffn_down": v_w_ffn_down, "g_final": v_g_final}
    names = list(w)
    delta, new_m, new_v = {}, {}, {}

    def update(n):
        shape = w[n].shape
        two_d = (lambda t: t.reshape(shape[-2:])) if len(shape) == 3 else (lambda t: t)
        dl, mn, vn = _adamw(two_d(w[n]), two_d(g[n]), two_d(m[n]), two_d(v[n]), "adamw_" + n)
        delta[n], new_m[n], new_v[n] = dl.reshape(shape), mn.reshape(shape), vn.reshape(shape)

    for n in list(g):
        update(n)
    done = sum(delta[n].reshape(-1)[:N_FOX_HEADS] for n in g)
    s_in, s_small = from_chips(mix_state, done, mix_tags, "mixer")
    g.update({"w_in": s_in[None, :, :W_IN_SH], "w_br_a": s_small[:rows_a].reshape(1, FOX_W, W_BR_SH),
              "w_br_b": s_small[rows_a:rows_a + rows_b].reshape(1, DIL_OUT_W, W_BR_SH),
              "w_out": s_small[None, rows_a + rows_b:]})
    for n in ("w_in", "w_br_a", "w_br_b", "w_out"):
        update(n)

    return (loss, grad_x[None], *[g[n] for n in names], *[delta[n] for n in names],
            *[new_m[n] for n in names], *[new_v[n] for n in names])
```

```python
import functools

import jax
import jax.numpy as jnp
from jax import lax
from jax.experimental import pallas as pl
from jax.experimental.pallas import tpu as pltpu

f32 = jnp.float32
bf16 = jnp.bfloat16
SDS = jax.ShapeDtypeStruct
MESH = pl.DeviceIdType.MESH

N_DEV = 8
D = 1024
SEQ = 2048
HEAD_DIM = 64
N_FOX_HEADS = 8
FOX_W = 512
DIL_W = 768
DIL_OUT_W = 256
ROT_DIM = 16
ROPE_THETA = 500000.0
D_FF = 2816
IN_COLS = 5896
EPS = 1e-6
NEG = -1e30
ATT_SCALE = HEAD_DIM ** -0.5

ADAM_LR = 0.001
ADAM_B1 = 0.9
ADAM_B2 = 0.999
ADAM_EPS = 1e-08
ADAM_WD = 0.01
ADAM_STEP = 10

C_GA, C_GB, C_QB, C_KB, C_VB, C_QA, C_KA, C_VA, C_F = 0, 1024, 2304, 3072, 3840, 4608, 5120, 5632, 6144
PROJ_W = 6272
LANES = 128
VMEM_LIMIT = 52 * 1024 * 1024

W_IN_SH, W_IN_PAD = IN_COLS // N_DEV, 768
W_BR_SH = D // N_DEV
W_FF_SH, FF_PAD = D_FF // N_DEV, 384
FF_HID = N_DEV * FF_PAD
SMALL_ROWS = 16


def _params(sem=None):
    if sem is None:
        return pltpu.CompilerParams(vmem_limit_bytes=VMEM_LIMIT)
    return pltpu.CompilerParams(dimension_semantics=sem, vmem_limit_bytes=VMEM_LIMIT)


def _rowwise(fn, name, tiled, vecs, outs, reds=(), tile=256):
    nt, nv, no = len(tiled), len(vecs), len(outs)
    rows = tiled[0][0].shape[0]
    assert rows % tile == 0

    def body(*refs):
        tin = [r[...] for r in refs[:nt]]
        vin = [r[...] for r in refs[nt:nt + nv]]
        orefs = refs[nt + nv:nt + nv + no]
        rrefs = refs[nt + nv + no:]
        touts, routs = fn(tin, vin)
        for r, t in zip(orefs, touts, strict=True):
            r[...] = t.astype(r.dtype)
        if rrefs:
            @pl.when(pl.program_id(0) == 0)
            def _():
                for r in rrefs:
                    r[...] = jnp.zeros_like(r)
            for r, t in zip(rrefs, routs, strict=True):
                r[...] += t

    def col_map(cb):
        return lambda i: (i, cb)

    def whole_map(nd):
        return lambda i: (0,) * nd

    in_specs = [pl.BlockSpec((tile, w), col_map(cb)) for (_, w, cb) in tiled]
    in_specs += [pl.BlockSpec(v.shape, whole_map(v.ndim)) for v in vecs]
    out_specs = [pl.BlockSpec((tile, w), lambda i: (i, 0)) for (w, _) in outs]
    out_specs += [pl.BlockSpec((1, w), lambda i: (0, 0)) for w in reds]
    out_shape = [SDS((rows, w), dt) for (w, dt) in outs] + [SDS((1, w), f32) for w in reds]
    res = pl.pallas_call(
        body, grid=(rows // tile,), in_specs=in_specs, out_specs=out_specs, out_shape=out_shape, name=name,
        compiler_params=_params(("arbitrary",)),
    )(*[t[0] for t in tiled], *vecs)
    return res


def _matmul(a, b, *, ta=False, tb=False, out_dtype=f32, name, tm, tn, tk, by_shard=False, after=None):
    m, k = (a.shape[1], a.shape[0]) if ta else a.shape
    if by_shard and not ta:
        n, kb = (b.shape[1], N_DEV * b.shape[2]) if tb else (N_DEV * b.shape[2], b.shape[1])
        assert (tk if tb else tn) == b.shape[2]
    else:
        n, kb = (b.shape[0], b.shape[1]) if tb else (b.shape[1], b.shape[0])
    assert kb == k and m % tm == 0 and n % tn == 0 and k % tk == 0
    nk = k // tk
    dims = (((0 if ta else 1,), (1 if tb else 0,)), ((), ()))
    b_stacked = by_shard and not ta
    o_stacked = by_shard and ta

    def body(a_ref, b_ref, *rest):
        o_ref, *acc = rest[1:] if after is not None else rest
        bv = b_ref[0] if b_stacked else b_ref[...]
        p = lax.dot_general(a_ref[...].astype(bf16), bv.astype(bf16), dims, preferred_element_type=f32)

        def put(val):
            if o_stacked:
                o_ref[0] = val.astype(o_ref.dtype)
            else:
                o_ref[...] = val.astype(o_ref.dtype)

        if nk == 1:
            put(p)
        else:
            acc_ref, = acc
            kk = pl.program_id(2)

            @pl.when(kk == 0)
            def _():
                acc_ref[...] = p

            @pl.when(kk > 0)
            def _():
                acc_ref[...] += p

            @pl.when(kk == nk - 1)
            def _():
                put(acc_ref[...])

    a_spec = pl.BlockSpec((tk, tm), lambda i, j, kk: (kk, i)) if ta else pl.BlockSpec((tm, tk), lambda i, j, kk: (i, kk))
    if b_stacked and tb:
        b_spec = pl.BlockSpec((1, tn, tk), lambda i, j, kk: (kk, j, 0))
    elif b_stacked:
        b_spec = pl.BlockSpec((1, tk, tn), lambda i, j, kk: (j, kk, 0))
    elif tb:
        b_spec = pl.BlockSpec((tn, tk), lambda i, j, kk: (j, kk))
    else:
        b_spec = pl.BlockSpec((tk, tn), lambda i, j, kk: (kk, j))
    if o_stacked:
        assert tn == n // N_DEV
        out_spec = pl.BlockSpec((1, tm, tn), lambda i, j, kk: (j, i, 0))
        out_shape = SDS((N_DEV, m, tn), out_dtype)
    else:
        out_spec = pl.BlockSpec((tm, tn), lambda i, j, kk: (i, j))
        out_shape = SDS((m, n), out_dtype)
    extra_specs, extra = ([pl.BlockSpec(memory_space=pl.ANY)], [after]) if after is not None else ([], [])
    return pl.pallas_call(
        body, grid=(m // tm, n // tn, nk), in_specs=[a_spec, b_spec] + extra_specs, out_specs=out_spec,
        out_shape=out_shape, name=name, scratch_shapes=[pltpu.VMEM((tm, tn), f32)] if nk > 1 else [],
        compiler_params=_params(("parallel", "parallel", "arbitrary")),
    )(a, b, *extra)


def _matmul_stack(a, b, *, ta=False, tb=False, out_dtype=f32, name):
    def lanes(ref):
        return jnp.concatenate([ref[j] for j in range(N_DEV)], axis=1).astype(bf16)

    if ta:
        w = b.shape[1] // N_DEV

        def body(a_ref, b_ref, o_ref):
            p = _tn(a_ref[...].astype(bf16), b_ref[...].astype(bf16))
            for j in range(N_DEV):
                o_ref[j] = p[:, j * w:(j + 1) * w].astype(o_ref.dtype)

        return pl.pallas_call(body, out_shape=SDS((N_DEV, a.shape[1], w), out_dtype), name=name,
                              compiler_params=_params())(a, b)

    m, half = a.shape[0], a.shape[0] // 2
    n = b.shape[1] if tb else N_DEV * b.shape[2]

    def body(a_ref, b_ref, o_ref):
        av = a_ref[...].astype(bf16)
        o_ref[...] = (_nt(av, lanes(b_ref)) if tb else jnp.dot(av, lanes(b_ref), preferred_element_type=f32)
                      ).astype(o_ref.dtype)

    return pl.pallas_call(
        body, grid=(2,), in_specs=[pl.BlockSpec((half, a.shape[1]), lambda i: (i, 0)),
                                   pl.BlockSpec(b.shape, lambda i: (0, 0, 0))],
        out_specs=pl.BlockSpec((half, n), lambda i: (i, 0)), out_shape=SDS((m, n), out_dtype), name=name,
        compiler_params=_params(("parallel",)),
    )(a, b)


def _rms(x):
    r = lax.rsqrt(jnp.mean(x * x, axis=-1, keepdims=True) + EPS)
    return r, x * r


def _rms_bwd(r, xn, dxn):
    return r * (dxn - xn * jnp.mean(dxn * xn, axis=-1, keepdims=True))


def _colsum(t):
    return jnp.sum(t, axis=0, keepdims=True)


def _sigmoid(x):
    return 1.0 / (1.0 + jnp.exp(-x))


def _modulated_norm(x, g, shift, scale):
    _, xn = _rms(x)
    return (xn * g) * (1.0 + scale) + shift


def _pre1(x, modv, g_mix):
    def fn(t, v):
        (xt,), (mv, g) = t, v
        return [_modulated_norm(xt, g, mv[0:1], mv[1:2])], []
    return _rowwise(fn, "pre1", [(x, D, 0)], [modv, g_mix], [(D, bf16)])[0]


def _post1(x, mix, modv, g_ffn):
    def fn(t, v):
        (xt, mt), (mv, g) = t, v
        x1 = xt + mv[2:3] * mt
        return [x1, _modulated_norm(x1, g, mv[3:4], mv[4:5])], []
    return _rowwise(fn, "post1", [(x, D, 0), (mix, D, 0)], [modv, g_ffn], [(D, f32), (D, bf16)])


def _gate_up(au, j):
    base = 2 * j * FF_PAD
    return au[:, base:base + FF_PAD], au[:, base + FF_PAD:base + 2 * FF_PAD]


def _ffn_in(h, w_stack):
    def body(h_ref, w_ref, act_ref, au_ref):
        p = jnp.dot(h_ref[...], w_ref[0], preferred_element_type=f32)
        a, u = p[:, :FF_PAD], p[:, FF_PAD:]
        act_ref[...] = (a * _sigmoid(a) * u).astype(act_ref.dtype)
        au_ref[...] = p.astype(au_ref.dtype)

    return pl.pallas_call(
        body, grid=(N_DEV,),
        in_specs=[pl.BlockSpec((SEQ, D), lambda j: (0, 0)), pl.BlockSpec((1, D, 2 * FF_PAD), lambda j: (j, 0, 0))],
        out_specs=[pl.BlockSpec((SEQ, FF_PAD), lambda j: (0, j)), pl.BlockSpec((SEQ, 2 * FF_PAD), lambda j: (0, j))],
        out_shape=(SDS((SEQ, FF_HID), bf16), SDS((SEQ, 2 * FF_HID), bf16)), name="ffn_in",
        compiler_params=_params(("parallel",)),
    )(h, w_stack)


def _swiglu_bwd(au, dact):
    def fn(t, v):
        parts, au_t = [], t[0].astype(f32)
        for j in range(N_DEV):
            a, u = _gate_up(au_t, j)
            d = t[1][:, j * FF_PAD:(j + 1) * FF_PAD]
            sg = _sigmoid(a)
            parts += [d * u * (sg * (1.0 + a * (1.0 - sg))), d * (a * sg)]
        return [jnp.concatenate(parts, axis=1)], []
    return _rowwise(fn, "swiglu_bwd", [(au, 2 * FF_HID, 0), (dact, FF_HID, 0)], [], [(2 * FF_HID, bf16)],
                    tile=128)[0]


def _final(x1, ff, target, modv, g_final):
    def fn(t, v):
        (x1t, fft, tgt), (mv, g) = t, v
        x2 = x1t + mv[5:6] * fft
        r, xn = _rms(x2)
        err = xn * g - tgt
        dy = err * (1.0 / D)
        dx2 = _rms_bwd(r, xn, dy * g)
        return [dx2, dx2 * mv[5:6]], [_colsum(dy * xn), _colsum(dx2 * fft), _colsum(err * err) * (0.5 / D)]
    return _rowwise(fn, "final", [(x1, D, 0), (ff, D, 0), (target, D, 0)], [modv, g_final],
                    [(D, f32), (D, bf16)], [D, D, D])


def _mid_bwd(dh2, x1, dx2, mix, modv, g_ffn):
    def fn(t, v):
        (dh, x1t, dx2t, mt), (mv, g) = t, v
        r, xn = _rms(x1t)
        dn = dh * (1.0 + mv[4:5])
        dx1 = dx2t + _rms_bwd(r, xn, dn * g)
        return [dx1, dx1 * mv[2:3]], [_colsum(dh), _colsum(dh * (xn * g)), _colsum(dn * xn), _colsum(dx1 * mt)]
    return _rowwise(fn, "mid_bwd", [(dh2, D, 0), (x1, D, 0), (dx2, D, 0), (mix, D, 0)], [modv, g_ffn],
                    [(D, f32), (D, bf16)], [D, D, D, D])


def _first_bwd(dh1, x, dx1, modv, g_mix):
    def fn(t, v):
        (dh, xt, dx1t), (mv, g) = t, v
        r, xn = _rms(xt)
        dn = dh * (1.0 + mv[1:2])
        return [dx1t + _rms_bwd(r, xn, dn * g)], [_colsum(dh), _colsum(dh * (xn * g)), _colsum(dn * xn)]
    return _rowwise(fn, "first_bwd", [(dh1, D, 0), (x, D, 0), (dx1, D, 0)], [modv, g_mix], [(D, f32)], [D, D, D])


def _merge_fwd(ya, yb, proj):
    def fn(t, v):
        ya_t, yb_t, ga, gb = t
        return [_sigmoid(ga) * ya_t + _sigmoid(gb) * yb_t], []
    return _rowwise(fn, "merge_fwd", [(ya, D, 0), (yb, D, 0), (proj, D, C_GA // D), (proj, D, C_GB // D)], [],
                    [(D, bf16)])[0]


def _merge_bwd(dmerged, ya, yb, proj):
    def fn(t, v):
        dm, ya_t, yb_t, ga, gb = t
        sa, sb = _sigmoid(ga), _sigmoid(gb)
        return [dm * sa, dm * sb, dm * ya_t * (sa * (1.0 - sa)), dm * yb_t * (sb * (1.0 - sb))], []
    return _rowwise(fn, "merge_bwd",
                    [(dmerged, D, 0), (ya, D, 0), (yb, D, 0), (proj, D, C_GA // D), (proj, D, C_GB // D)], [],
                    [(D, bf16), (D, bf16), (D, bf16), (D, bf16)])


def _rope_tables():
    half = ROT_DIM // 2
    pos = jnp.arange(SEQ, dtype=f32)
    inv_freq = ROPE_THETA ** (-jnp.arange(0, ROT_DIM, 2, dtype=f32) / ROT_DIM)
    ang = pos[:, None] * inv_freq[None, :]
    cos, sin = jnp.cos(ang), jnp.sin(ang)
    pad = jnp.zeros((SEQ, HEAD_DIM - ROT_DIM), f32)
    zero = jnp.zeros((SEQ, half), f32)
    c_head = jnp.concatenate([cos, cos, pad + 1.0], axis=1)
    lo_head = jnp.concatenate([-sin, zero, pad], axis=1)
    hi_head = jnp.concatenate([zero, sin, pad], axis=1)
    return tuple(jnp.concatenate([t, t], axis=1) for t in (c_head, lo_head, hi_head))


def _over_heads(tables):
    return [jnp.tile(t, (1, DIL_W // LANES)) for t in tables]


def _rope_fwd(proj, tables):
    half = ROT_DIM // 2

    def fn(t, v):
        q, k, vv = t[:3]
        c, lo, hi = _over_heads(t[3:])
        rot = lambda z: z * c + pltpu.roll(z, DIL_W - half, 1) * lo + pltpu.roll(z, half, 1) * hi
        return [rot(q) * ATT_SCALE, rot(k), vv], []
    return _rowwise(fn, "rope_fwd", [(proj, DIL_W, C_QB // DIL_W), (proj, DIL_W, C_KB // DIL_W),
                                     (proj, DIL_W, C_VB // DIL_W)] + [(tb, LANES, 0) for tb in tables], [],
                    [(DIL_W, f32)] * 3)


def _rope_bwd(dqs, dks, tables):
    half = ROT_DIM // 2

    def fn(t, v):
        dq_t, dk_t = jnp.concatenate(t[:N_GROUPS], axis=1), jnp.concatenate(t[N_GROUPS:2 * N_GROUPS], axis=1)
        c, lo, hi = _over_heads(t[2 * N_GROUPS:])
        rot_t = lambda z: z * c + pltpu.roll(z * lo, half, 1) + pltpu.roll(z * hi, DIL_W - half, 1)
        return [rot_t(dq_t), rot_t(dk_t)], []
    return _rowwise(fn, "rope_bwd", [(a, DIL_OUT_W, 0) for a in (*dqs, *dks)] + [(tb, LANES, 0) for tb in tables],
                    [], [(DIL_W, bf16), (DIL_W, bf16)])


def _head_bcast_sum(d):
    lane = lax.broadcasted_iota(jnp.int32, d.shape, 1)
    out = jnp.zeros_like(d)
    for h in range(d.shape[1] // HEAD_DIM):
        sel = (lane >= h * HEAD_DIM) & (lane < (h + 1) * HEAD_DIM)
        out = jnp.where(sel, jnp.sum(jnp.where(sel, d, 0.0), axis=1, keepdims=True), out)
    return out


def _dil_combine(outs, lses):
    def fn(t, v):
        o0, o1, o2, l0, l1, l2 = t
        m = jnp.maximum(jnp.maximum(l0, l1), l2)
        w0, w1, w2 = jnp.exp(l0 - m), jnp.exp(l1 - m), jnp.exp(l2 - m)
        tot = w0 + w1 + w2
        return [(w0 * o0 + w1 * o1 + w2 * o2) / tot, m + jnp.log(tot)], []
    w = DIL_OUT_W
    return _rowwise(fn, "dil_combine", [(t, w, 0) for t in (*outs, *lses)], [], [(w, f32), (w, f32)])


def _dil_delta(dyb_h, yb_h):
    def fn(t, v):
        return [_head_bcast_sum(t[0] * t[1])], []
    return _rowwise(fn, "dil_delta", [(dyb_h, DIL_OUT_W, 0), (yb_h, DIL_OUT_W, 0)], [], [(DIL_OUT_W, f32)])[0]


def _adamw_math(wt, gt, mt, vt):
    mn = ADAM_B1 * mt + (1.0 - ADAM_B1) * gt
    vn = ADAM_B2 * vt + (1.0 - ADAM_B2) * (gt * gt)
    m_hat = mn / (1.0 - ADAM_B1 ** ADAM_STEP)
    v_hat = vn / (1.0 - ADAM_B2 ** ADAM_STEP)
    return -ADAM_LR * (m_hat / (jnp.sqrt(v_hat) + ADAM_EPS) + ADAM_WD * wt), mn, vn


def _adamw(w, g, m, v, name):
    shape = w.shape
    if w.ndim == 1:
        w, g, m, v = (t.reshape(1, -1) for t in (w, g, m, v))
    rows, cols = w.shape
    if rows % 8 and rows > 8:
        return _adamw_by_cols(w, g, m, v, name)
    tile = 256 if rows % 256 == 0 and rows > 512 else rows

    def fn(t, _):
        return list(_adamw_math(*t)), []
    delta, mn, vn = _rowwise(fn, name, [(w, cols, 0), (g, cols, 0), (m, cols, 0), (v, cols, 0)], [],
                             [(cols, f32)] * 3, tile=tile)
    return delta.reshape(shape), mn.reshape(shape), vn.reshape(shape)


def _adamw_by_cols(w, g, m, v, name, tile=256):
    rows, cols = w.shape

    def body(w_ref, g_ref, m_ref, v_ref, d_ref, mn_ref, vn_ref):
        d_ref[...], mn_ref[...], vn_ref[...] = _adamw_math(w_ref[...], g_ref[...], m_ref[...], v_ref[...])

    spec = pl.BlockSpec((rows, tile), lambda j: (0, j))
    return pl.pallas_call(body, grid=(cols // tile,), in_specs=[spec] * 4, out_specs=[spec] * 3,
                          out_shape=[SDS((rows, cols), f32)] * 3, name=name,
                          compiler_params=_params(("parallel",)))(w, g, m, v)


def _ada_fwd(c_all, w_shard, b_shard):
    def body(c_ref, w_ref, b_ref, o_ref):
        cv = c_ref[...]
        sc = (cv * _sigmoid(cv)).astype(bf16)
        o_ref[...] = jnp.dot(sc, w_ref[...].astype(bf16), preferred_element_type=f32) + b_ref[...]
    return pl.pallas_call(body, out_shape=SDS((N_DEV, w_shard.shape[1]), f32), name="ada_fwd",
                          compiler_params=_params())(c_all, w_shard, b_shard)


def _ada_bwd(c_all, dmod_cols):
    def body(c_ref, d_ref, o_ref):
        cv = c_ref[...]
        sc = cv * _sigmoid(cv)
        o_ref[...] = lax.dot_general(sc, d_ref[...], (((0,), (0,)), ((), ())), precision=lax.Precision.HIGHEST,
                                     preferred_element_type=f32)
    return pl.pallas_call(body, out_shape=SDS((D, dmod_cols.shape[1]), f32), name="ada_bwd",
                          compiler_params=_params())(c_all, dmod_cols)


def _small_reduce(gathered, after):
    def body(g_ref, after_ref, o_ref, loss_ref):
        acc = g_ref[0]
        for d in range(1, N_DEV):
            acc = acc + g_ref[d]
        o_ref[...] = acc
        loss_ref[...] = jnp.zeros((1, LANES), f32) + jnp.sum(acc[10:11, :])
    return pl.pallas_call(body, out_shape=(SDS((SMALL_ROWS, D), f32), SDS((1, LANES), f32)), name="small_reduce",
                          in_specs=[pl.BlockSpec(memory_space=pltpu.VMEM), pl.BlockSpec(memory_space=pl.ANY)],
                          compiler_params=_params())(gathered, after)


FOX_BLK = 512
CUM_BLK = 128


def _fold_lanes(t, op):
    out = t[:, :LANES]
    for j in range(1, t.shape[1] // LANES):
        out = op(out, t[:, j * LANES:(j + 1) * LANES])
    return out


def _fox_gate_fwd(proj, b_pad):
    nblk = SEQ // CUM_BLK

    def body(f_ref, b_ref, col_ref):
        r = lax.broadcasted_iota(jnp.int32, (CUM_BLK, CUM_BLK), 0)
        c = lax.broadcasted_iota(jnp.int32, (CUM_BLK, CUM_BLK), 1)
        tri = (r >= c).astype(f32)
        carry = jnp.zeros((1, LANES), f32)
        for blk in range(nblk):
            z = f_ref[blk * CUM_BLK:(blk + 1) * CUM_BLK, :] + b_ref[...]
            logf = jnp.minimum(z, 0.0) - jnp.log1p(jnp.exp(-jnp.abs(z)))
            cs = jnp.dot(tri, logf, precision=lax.Precision.HIGHEST, preferred_element_type=f32) + carry
            col_ref[blk * CUM_BLK:(blk + 1) * CUM_BLK, :] = cs
            carry = cs[CUM_BLK - 1:CUM_BLK, :]

    return pl.pallas_call(
        body, grid=(1,), in_specs=[pl.BlockSpec((SEQ, LANES), lambda i: (0, C_F // LANES)),
                                   pl.BlockSpec((1, LANES), lambda i: (0, 0))],
        out_specs=pl.BlockSpec((SEQ, LANES), lambda i: (0, 0)),
        out_shape=SDS((SEQ, LANES), f32), name="fox_gate_fwd",
        compiler_params=_params(("arbitrary",)),
    )(proj, b_pad)


def _fox_gate_bwd(dF_row, proj, b_pad):
    nblk = SEQ // CUM_BLK

    def body(d_ref, f_ref, b_ref, df_ref, db_ref, col_ref):
        r = lax.broadcasted_iota(jnp.int32, (CUM_BLK, CUM_BLK), 0)
        c = lax.broadcasted_iota(jnp.int32, (CUM_BLK, CUM_BLK), 1)
        tri = (r <= c).astype(f32)
        lane = lax.broadcasted_iota(jnp.int32, (CUM_BLK, LANES), 1)
        col_ref[...] = d_ref[...].T
        carry = jnp.zeros((1, LANES), f32)
        total = jnp.zeros((1, LANES), f32)
        for blk in reversed(range(nblk)):
            rows = slice(blk * CUM_BLK, (blk + 1) * CUM_BLK)
            cs = jnp.dot(tri, col_ref[rows, :], precision=lax.Precision.HIGHEST, preferred_element_type=f32) + carry
            carry = cs[0:1, :]
            z = f_ref[rows, :] + b_ref[...]
            df = jnp.where(lane < N_FOX_HEADS, cs * _sigmoid(-z), 0.0)
            df_ref[rows, :] = df.astype(df_ref.dtype)
            total = total + _colsum(df)
        db_ref[...] = total

    return pl.pallas_call(
        body, grid=(1,), in_specs=[pl.BlockSpec((LANES, SEQ), lambda i: (0, 0)),
                                   pl.BlockSpec((SEQ, LANES), lambda i: (0, C_F // LANES)),
                                   pl.BlockSpec((1, LANES), lambda i: (0, 0))],
        out_specs=[pl.BlockSpec((SEQ, LANES), lambda i: (0, 0)), pl.BlockSpec((1, LANES), lambda i: (0, 0))],
        out_shape=(SDS((SEQ, LANES), bf16), SDS((1, LANES), f32)), name="fox_gate_bwd",
        scratch_shapes=[pltpu.VMEM((SEQ, LANES), f32)],
        compiler_params=_params(("arbitrary",)),
    )(dF_row, proj, b_pad)


def _nt(a, b):
    return lax.dot_general(a, b, (((1,), (1,)), ((), ())), preferred_element_type=f32)


def _tn(a, b):
    return lax.dot_general(a, b, (((0,), (0,)), ((), ())), preferred_element_type=f32)


def _fox_prep(proj, f_col):
    def fn(t, v):
        q, k, vv, fc = t
        lane = lax.broadcasted_iota(jnp.int32, (q.shape[0], LANES), 1)
        qs, ks = [], []
        for h in range(N_FOX_HEADS):
            pair, pos = divmod(h, 2)
            own = (lane >= pos * HEAD_DIM) & (lane < (pos + 1) * HEAD_DIM)
            base = (1 - pos) * HEAD_DIM
            f = fc[:, h:h + 1]
            hi = f.astype(bf16).astype(f32)
            mid = (f - hi).astype(bf16).astype(f32)
            lo = (f - hi) - mid
            one = jnp.ones_like(f)
            qa = jnp.where(own, q[:, pair * LANES:(pair + 1) * LANES] * ATT_SCALE, 0.0)
            ka = k[:, pair * LANES:(pair + 1) * LANES]
            for idx, (qv, kv) in enumerate([(hi, one), (mid, one), (lo, one), (one, -hi), (one, -mid), (one, -lo)]):
                sel = lane == base + idx
                qa = jnp.where(sel, qv, qa)
                ka = jnp.where(sel, kv, ka)
            qs.append(qa)
            ks.append(ka)
        return [jnp.concatenate(qs, axis=1), jnp.concatenate(ks, axis=1), vv], []
    w = N_FOX_HEADS * LANES
    return _rowwise(fn, "fox_prep", [(proj, FOX_W, C_QA // FOX_W), (proj, FOX_W, C_KA // FOX_W),
                                     (proj, FOX_W, C_VA // FOX_W), (f_col, LANES, 0)], [],
                    [(w, bf16), (w, bf16), (FOX_W, bf16)])


def _fox_fwd(q_aug, k_aug, v):
    blk = FOX_BLK
    npair = FOX_W // LANES

    def body(q_ref, k_ref, v_ref, o_ref, lse_ref, s_scr):
        i = pl.program_id(1)
        tri = lax.broadcasted_iota(jnp.int32, (blk, blk), 0) >= lax.broadcasted_iota(jnp.int32, (blk, blk), 1)
        qh = [q_ref[:, h * LANES:(h + 1) * LANES] for h in range(2)]

        def logits(c, masked):
            off = pl.multiple_of(c * blk, blk)
            tops = []
            for h in range(2):
                s = _nt(qh[h], k_ref[pl.ds(off, blk), h * LANES:(h + 1) * LANES])
                if masked:
                    s = jnp.where(tri, s, NEG)
                s_scr[h, :, pl.ds(off, blk)] = s
                tops.append(_fold_lanes(s, jnp.maximum))
            return tops

        def pass_a(c, m):
            return tuple(jnp.maximum(a, b) for a, b in zip(m, logits(c, False)))

        m = lax.fori_loop(0, i, pass_a, tuple(jnp.full((blk, LANES), NEG, f32) for _ in range(2)))
        mx = [jnp.max(jnp.maximum(a, b), axis=1, keepdims=True) for a, b in zip(m, logits(i, True))]

        def pass_b(c, carry):
            off = pl.multiple_of(c * blk, blk)
            vv = v_ref[pl.ds(off, blk), :]
            new = []
            for h in range(2):
                l, acc = carry[h]
                p = jnp.exp(s_scr[h, :, pl.ds(off, blk)] - mx[h])
                new.append((l + _fold_lanes(p, jnp.add),
                            acc + jnp.dot(p.astype(bf16), vv, preferred_element_type=f32)))
            return tuple(new)

        zero = jnp.zeros((blk, LANES), f32)
        (l_a, acc_a), (l_b, acc_b) = lax.fori_loop(0, i + 1, pass_b, ((zero, zero), (zero, zero)))
        l_a = jnp.sum(l_a, axis=1, keepdims=True)
        l_b = jnp.sum(l_b, axis=1, keepdims=True)
        first = lax.broadcasted_iota(jnp.int32, (blk, LANES), 1) < HEAD_DIM
        o_ref[...] = jnp.where(first, acc_a / l_a, acc_b / l_b)
        lse_ref[0] = jnp.where(first, mx[0] + jnp.log(l_a), mx[1] + jnp.log(l_b))

    return pl.pallas_call(
        body, grid=(npair, SEQ // blk),
        in_specs=[pl.BlockSpec((blk, 2 * LANES), lambda p, i: (i, p)),
                  pl.BlockSpec((SEQ, 2 * LANES), lambda p, i: (0, p)),
                  pl.BlockSpec((SEQ, LANES), lambda p, i: (0, p))],
        out_specs=[pl.BlockSpec((blk, LANES), lambda p, i: (i, p)),
                   pl.BlockSpec((1, blk, LANES), lambda p, i: (p, i, 0))],
        out_shape=(SDS((SEQ, FOX_W), f32), SDS((npair, SEQ, LANES), f32)), name="fox_fwd",
        scratch_shapes=[pltpu.VMEM((2, blk, SEQ), f32)],
        compiler_params=_params(("parallel", "arbitrary")),
    )(q_aug, k_aug, v)


def _fox_bwd(q_aug, k_aug, v, do, o, lse):
    blk = FOX_BLK
    npair = FOX_W // LANES
    nblk = SEQ // blk

    def body(q_ref, k_ref, v_ref, do_ref, o_ref, lse_ref, dq_ref, dk_ref, dv_ref, df_ref,
             dq_acc, delta_ref, res_ref):
        lane_s = lax.broadcasted_iota(jnp.int32, (SEQ, LANES), 1)
        prod = do_ref[...] * o_ref[...]
        d_a = jnp.sum(jnp.where(lane_s < HEAD_DIM, prod, 0.0), axis=1, keepdims=True)
        d_b = jnp.sum(jnp.where(lane_s >= HEAD_DIM, prod, 0.0), axis=1, keepdims=True)
        delta_ref[...] = jnp.where(lane_s < HEAD_DIM, d_a, d_b)
        dq_acc[...] = jnp.zeros_like(dq_acc)
        res_ref[...] = jnp.zeros_like(res_ref)
        df_ref[...] = jnp.zeros_like(df_ref)
        lane = lax.broadcasted_iota(jnp.int32, (blk, LANES), 1)
        own = [lane < HEAD_DIM, lane >= HEAD_DIM]
        tri = lax.broadcasted_iota(jnp.int32, (blk, blk), 0) >= lax.broadcasted_iota(jnp.int32, (blk, blk), 1)

        def q_slab(qoff, h):
            return q_ref[pl.ds(qoff, blk), h * LANES:(h + 1) * LANES]

        def probs(qoff, h, k_h, masked):
            s = _nt(q_slab(qoff, h), k_h)
            if masked:
                s = jnp.where(tri, s, NEG)
            return jnp.exp(s - lse_ref[0, pl.ds(qoff, blk), h * HEAD_DIM:h * HEAD_DIM + 1])

        def k_slabs(koff):
            return [k_ref[pl.ds(koff, blk), h * LANES:(h + 1) * LANES] for h in range(2)]

        def kv_step(kj, _):
            koff = pl.multiple_of(kj * blk, blk)
            k_aug = k_slabs(koff)
            k_own = [jnp.where(own[h], k_aug[h], jnp.zeros_like(k_aug[h])) for h in range(2)]
            vv = v_ref[pl.ds(koff, blk), :]
            v_own = [jnp.where(own[h], vv, jnp.zeros_like(vv)) for h in range(2)]

            def q_tile(qi, carry, masked):
                qoff = pl.multiple_of(qi * blk, blk)
                dd = do_ref[pl.ds(qoff, blk), :].astype(bf16)
                new, dq_add = [], None
                for h in range(2):
                    dk_h, dv_h, dcol = carry[h]
                    p = probs(qoff, h, k_aug[h], masked)
                    dl = p * (_nt(dd, v_own[h]) - delta_ref[pl.ds(qoff, blk), h * HEAD_DIM:h * HEAD_DIM + 1])
                    dlb = dl.astype(bf16)
                    part = jnp.dot(dlb, k_own[h], preferred_element_type=f32)
                    dq_add = part if dq_add is None else dq_add + part
                    res_ref[h, pl.ds(qoff, blk), :] += _fold_lanes(dl, jnp.add)
                    new.append((dk_h + _tn(dlb, q_slab(qoff, h)), dv_h + _tn(p.astype(bf16), dd),
                                dcol + _colsum(dl)))
                dq_acc[pl.ds(qoff, blk), :] += dq_add * ATT_SCALE
                return tuple(new)

            zero = (jnp.zeros((blk, LANES), f32), jnp.zeros((blk, LANES), f32), jnp.zeros((1, blk), f32))
            carry = q_tile(kj, (zero, zero), True)
            (dk_a, dv_a, dcol_a), (dk_b, dv_b, dcol_b) = lax.fori_loop(
                kj + 1, nblk, lambda qi, cr: q_tile(qi, cr, False), carry)
            dk_ref[pl.ds(koff, blk), :] = jnp.where(own[0], dk_a, dk_b).astype(dk_ref.dtype)
            dv_ref[pl.ds(koff, blk), :] = jnp.where(own[0], dv_a, dv_b).astype(dv_ref.dtype)
            df_ref[0, 0:1, pl.ds(koff, blk)] = -dcol_a
            df_ref[0, 1:2, pl.ds(koff, blk)] = -dcol_b
            return 0

        lax.fori_loop(0, nblk, kv_step, 0)
        dq_ref[...] = dq_acc[...].astype(dq_ref.dtype)

        for h in range(2):
            res_ref[h] = jnp.zeros((SEQ, LANES), f32) + jnp.sum(res_ref[h], axis=1, keepdims=True)

        def kv_fix(kj, _):
            koff = pl.multiple_of(kj * blk, blk)
            k_aug = k_slabs(koff)

            def q_fix(qi, corr, masked):
                qoff = pl.multiple_of(qi * blk, blk)
                return tuple(corr[h] + _colsum(probs(qoff, h, k_aug[h], masked) * res_ref[h, pl.ds(qoff, blk), 0:1])
                             for h in range(2))

            zero = jnp.zeros((1, blk), f32)
            corr = lax.fori_loop(kj + 1, nblk, lambda qi, cr: q_fix(qi, cr, False), q_fix(kj, (zero, zero), True))
            df_ref[0, 0:1, pl.ds(koff, blk)] += corr[0]
            df_ref[0, 1:2, pl.ds(koff, blk)] += corr[1]
            return 0

        lax.fori_loop(0, nblk, kv_fix, 0)

    pair_aug = pl.BlockSpec((SEQ, 2 * LANES), lambda p: (0, p))
    slab = pl.BlockSpec((SEQ, LANES), lambda p: (0, p))
    per_pair = pl.BlockSpec((1, SEQ, LANES), lambda p: (p, 0, 0))
    rows = pl.BlockSpec((1, 8, SEQ), lambda p: (p, 0, 0))
    return pl.pallas_call(
        body, grid=(npair,),
        in_specs=[pair_aug, pair_aug, slab, slab, slab, per_pair],
        out_specs=[slab, slab, slab, rows],
        out_shape=(SDS((SEQ, FOX_W), bf16),) * 3 + (SDS((npair, 8, SEQ), f32),), name="fox_bwd",
        scratch_shapes=[pltpu.VMEM((SEQ, LANES), f32), pltpu.VMEM((SEQ, LANES), f32),
                        pltpu.VMEM((2, SEQ, LANES), f32)],
        compiler_params=_params(("parallel",)),
    )(q_aug, k_aug, v, do, o, lse)


DIL_BLK = 128
DILATIONS = (1, 4, 16)
N_GROUPS = len(DILATIONS)
DIL_PAIRS = DIL_OUT_W // LANES


def _dil_blocks(d):
    r1 = lax.broadcasted_iota(jnp.int32, (DIL_BLK, DIL_BLK), 0)
    c1 = lax.broadcasted_iota(jnp.int32, (DIL_BLK, DIL_BLK), 1)
    r2 = lax.broadcasted_iota(jnp.int32, (DIL_BLK, 2 * DIL_BLK), 0)
    c2 = lax.broadcasted_iota(jnp.int32, (DIL_BLK, 2 * DIL_BLK), 1)
    band = ((c2 < DIL_BLK) & (c2 >= r2)) | ((c2 >= DIL_BLK) & (c2 - DIL_BLK <= r2))
    out = []
    for r in range(d):
        for b in range(SEQ // d // DIL_BLK):
            rows = pl.ds(r + d * DIL_BLK * b, DIL_BLK, stride=d)
            if b == 0:
                out.append((rows, rows, r1 >= c1))
            else:
                out.append((rows, pl.ds(r + d * DIL_BLK * (b - 1), 2 * DIL_BLK, stride=d), band))
    return out


def _dil_fwd(q, k, v, g):
    def body(q_ref, k_ref, v_ref, o_ref, lse_ref):
        first = lax.broadcasted_iota(jnp.int32, (DIL_BLK, LANES), 1) < HEAD_DIM
        for rows, krows, mask in _dil_blocks(DILATIONS[g]):
            qv, kk, vv = q_ref[rows, :].astype(bf16), k_ref[krows, :].astype(bf16), v_ref[krows, :].astype(bf16)
            outs, lses = [], []
            for own in (first, ~first):
                s = jnp.where(mask, _nt(jnp.where(own, qv, jnp.zeros_like(qv)), kk), NEG)
                m = jnp.max(s, axis=1, keepdims=True)
                p = jnp.exp(s - m)
                l = jnp.sum(p, axis=1, keepdims=True)
                outs.append(jnp.dot(p.astype(bf16), vv, preferred_element_type=f32) / l)
                lses.append(m + jnp.log(l))
            o_ref[rows, :] = jnp.where(first, outs[0], outs[1])
            lse_ref[rows, :] = jnp.where(first, lses[0], lses[1])

    grouped = pl.BlockSpec((SEQ, LANES), lambda p: (0, DIL_PAIRS * g + p))
    own = pl.BlockSpec((SEQ, LANES), lambda p: (0, p))
    shape = SDS((SEQ, DIL_OUT_W), f32)
    return pl.pallas_call(
        body, grid=(DIL_PAIRS,), in_specs=[grouped] * 3, out_specs=[own] * 2, out_shape=(shape, shape),
        name=f"dil_fwd_{DILATIONS[g]}", compiler_params=_params(("parallel",)),
    )(q, k, v)


def _dil_bwd(q, k, v, do, lse, delta, g):
    def body(q_ref, k_ref, v_ref, do_ref, lse_ref, dl_ref, dq_ref, dk_ref, dv_ref):
        first = lax.broadcasted_iota(jnp.int32, (DIL_BLK, LANES), 1) < HEAD_DIM
        dk_ref[...] = jnp.zeros_like(dk_ref)
        dv_ref[...] = jnp.zeros_like(dv_ref)
        for rows, krows, mask in _dil_blocks(DILATIONS[g]):
            qv, kk, vv = q_ref[rows, :].astype(bf16), k_ref[krows, :].astype(bf16), v_ref[krows, :].astype(bf16)
            dov = do_ref[rows, :].astype(bf16)
            lsev, delv = lse_ref[rows, :], dl_ref[rows, :]
            dqs, dk_add, dv_add = [], None, None
            for h, own in enumerate((first, ~first)):
                col = h * HEAD_DIM
                qh = jnp.where(own, qv, jnp.zeros_like(qv))
                doh = jnp.where(own, dov, jnp.zeros_like(dov))
                p = jnp.exp(jnp.where(mask, _nt(qh, kk), NEG) - lsev[:, col:col + 1])
                dl = (p * (_nt(doh, vv) - delv[:, col:col + 1])).astype(bf16)
                dqs.append(jnp.dot(dl, kk, preferred_element_type=f32))
                dk_h, dv_h = _tn(dl, qh), _tn(p.astype(bf16), doh)
                dk_add = dk_h if dk_add is None else dk_add + dk_h
                dv_add = dv_h if dv_add is None else dv_add + dv_h
            dq_ref[rows, :] = jnp.where(first, dqs[0], dqs[1]) * ATT_SCALE
            dk_ref[krows, :] += dk_add
            dv_ref[krows, :] += dv_add

    grouped = pl.BlockSpec((SEQ, LANES), lambda p: (0, DIL_PAIRS * g + p))
    own = pl.BlockSpec((SEQ, LANES), lambda p: (0, p))
    shape = SDS((SEQ, DIL_OUT_W), f32)
    return pl.pallas_call(
        body, grid=(DIL_PAIRS,), in_specs=[grouped] * 3 + [own] * 3, out_specs=[own] * 3,
        out_shape=(shape, shape, shape), name=f"dil_bwd_{DILATIONS[g]}", compiler_params=_params(("parallel",)),
    )(q, k, v, do, lse, delta)


def _position():
    return lax.axis_index("x"), lax.axis_index("y"), lax.axis_index("c")


def _all_gather(block, name):
    def body(x_ref, out_ref, send_sems, recv_sems, local_sem):
        x, y, c = _position()
        me, sibling = (x, y, c), (x, y, 1 - c)
        chips = [(1 - x, y), (x, 1 - y), (1 - x, 1 - y)]

        def slot(px, py, pc):
            return out_ref.at[4 * px + 2 * py + pc]

        def copy(k, blk, to, src=None):
            return pltpu.make_async_remote_copy(
                src_ref=slot(*blk) if src is None else src, dst_ref=slot(*blk),
                send_sem=send_sems.at[k], recv_sem=recv_sems.at[k], device_id=to, device_id_type=MESH)

        mine = pltpu.make_async_copy(x_ref, slot(*me), local_sem)
        mine.start()
        first = [copy(0, me, sibling, src=x_ref)]
        first += [copy(1 + j, me, (*chip, c), src=x_ref) for j, chip in enumerate(chips)]
        for cp in first:
            cp.start()
        passed = [copy(4 + j, (*chip, c), sibling) for j, chip in enumerate(chips)]
        for j, chip in enumerate(chips):
            copy(1 + j, (*chip, c), me).wait_recv()
            passed[j].start()
        copy(0, sibling, me).wait_recv()
        for j, chip in enumerate(chips):
            copy(4 + j, (*chip, 1 - c), me).wait_recv()
        for cp in first + passed:
            cp.wait_send()
        mine.wait()

    return pl.pallas_call(
        body, out_shape=SDS((N_DEV,) + block.shape, block.dtype),
        in_specs=[pl.BlockSpec(memory_space=pl.ANY)], out_specs=pl.BlockSpec(memory_space=pl.ANY),
        scratch_shapes=[pltpu.SemaphoreType.DMA((7,)), pltpu.SemaphoreType.DMA((7,)), pltpu.SemaphoreType.DMA],
        name=name,
    )(block)


def _all_gather_many(blocks, name):
    n = len(blocks)

    def body(*refs):
        x_refs, out_refs = refs[:n], refs[n:2 * n]
        send_sems, recv_sems, local_sems = refs[2 * n:]
        x, y, c = _position()
        me, sibling = (x, y, c), (x, y, 1 - c)
        chips = [(1 - x, y), (x, 1 - y), (1 - x, 1 - y)]

        def slot(a, px, py, pc):
            return out_refs[a].at[4 * px + 2 * py + pc]

        def copy(a, k, blk, to, own=False):
            return pltpu.make_async_remote_copy(
                src_ref=x_refs[a] if own else slot(a, *blk), dst_ref=slot(a, *blk),
                send_sem=send_sems.at[a, k], recv_sem=recv_sems.at[a, k], device_id=to, device_id_type=MESH)

        mine = [pltpu.make_async_copy(x_refs[a], slot(a, *me), local_sems.at[a]) for a in range(n)]
        for cp in mine:
            cp.start()
        started = []
        for a in range(n):
            first = [copy(a, 0, me, sibling, own=True)]
            first += [copy(a, 1 + j, me, (*chip, c), own=True) for j, chip in enumerate(chips)]
            for cp in first:
                cp.start()
            started += first
        for a in range(n):
            for j, chip in enumerate(chips):
                copy(a, 1 + j, (*chip, c), me).wait_recv()
                passed = copy(a, 4 + j, (*chip, c), sibling)
                passed.start()
                started.append(passed)
        for a in range(n):
            copy(a, 0, sibling, me).wait_recv()
            for j, chip in enumerate(chips):
                copy(a, 4 + j, (*chip, 1 - c), me).wait_recv()
        for cp in started:
            cp.wait_send()
        for cp in mine:
            cp.wait()

    hbm = pl.BlockSpec(memory_space=pl.ANY)
    return pl.pallas_call(
        body, out_shape=[SDS((N_DEV,) + b.shape, b.dtype) for b in blocks],
        in_specs=[hbm] * n, out_specs=[hbm] * n,
        scratch_shapes=[pltpu.SemaphoreType.DMA((n, 7)), pltpu.SemaphoreType.DMA((n, 7)),
                        pltpu.SemaphoreType.DMA((n,))],
        name=name,
    )(*blocks)


HBM_SPEC = pl.BlockSpec(memory_space=pltpu.HBM)
SEM_SPEC = pl.BlockSpec(memory_space=pltpu.SEMAPHORE)
SPLIT_COPY = pltpu.CompilerParams(has_side_effects=pltpu.SideEffectType.DATAFLOW_SIDE_EFFECTING)


def _in_hbm(t):
    return pltpu.with_memory_space_constraint(t, pltpu.HBM)


def _pair_copies(g_refs, land_refs, send_sems, recv_sems):
    x, y, c = _position()
    return [pltpu.make_async_remote_copy(
        src_ref=g.at[2 * k + (1 - c)], dst_ref=land.at[k], send_sem=send_sems.at[4 * a + k],
        recv_sem=recv_sems.at[4 * a + k], device_id=(x, y, 1 - c), device_id_type=MESH)
        for a, (g, land) in enumerate(zip(g_refs, land_refs, strict=True)) for k in range(4)]


def _chip_copies(t_refs, land_refs, send_sems, recv_sems):
    x, y, c = _position()
    chips = [(1 - x, y), (x, 1 - y), (1 - x, 1 - y)]
    return [pltpu.make_async_remote_copy(
        src_ref=t.at[2 * px + py], dst_ref=land.at[j], send_sem=send_sems.at[3 * a + j],
        recv_sem=recv_sems.at[3 * a + j], device_id=(px, py, c), device_id_type=MESH)
        for a, (t, land) in enumerate(zip(t_refs, land_refs, strict=True)) for j, (px, py) in enumerate(chips)]


_ROUNDS = {"pair": (_pair_copies, 4), "chip": (_chip_copies, 3)}


def _exchange_start(kind, ts, name):
    copies, slots = _ROUNDS[kind]
    n = len(ts)
    lands = [_in_hbm(lax.empty((slots,) + t.shape[1:], t.dtype)) for t in ts]

    def body(*refs):
        for cp in copies(refs[:n], refs[n:2 * n], refs[2 * n], refs[2 * n + 1]):
            cp.start()
        refs[-1][...] = jnp.zeros_like(refs[-1])

    sems = pltpu.SemaphoreType.DMA((slots * n,))
    res = pl.pallas_call(
        body, name=name, in_specs=[HBM_SPEC] * (2 * n),
        out_shape=(sems, sems, *[pltpu.HBM(t.shape, t.dtype) for t in (*ts, *lands)], SDS((8, LANES), f32)),
        out_specs=(SEM_SPEC, SEM_SPEC, *[HBM_SPEC] * (2 * n), pl.BlockSpec(memory_space=pltpu.VMEM)),
        input_output_aliases={i: 2 + i for i in range(2 * n)}, compiler_params=SPLIT_COPY,
    )(*[_in_hbm(t) for t in ts], *lands)
    return res[:-1], res[-1]


def _exchange_wait(kind, state, after, name):
    copies, _ = _ROUNDS[kind]
    send_sems, recv_sems, *arrays = state
    n = len(arrays) // 2

    def body(*refs):
        for cp in copies(refs[:n], refs[n:2 * n], refs[2 * n], refs[2 * n + 1]):
            cp.wait_send()
            cp.wait_recv()

    res = pl.pallas_call(
        body, name=name, in_specs=[HBM_SPEC] * (2 * n) + [SEM_SPEC, SEM_SPEC, pl.BlockSpec(memory_space=pl.ANY)],
        out_shape=[pltpu.HBM(t.shape, t.dtype) for t in arrays], out_specs=[HBM_SPEC] * (2 * n),
        input_output_aliases={i: i for i in range(2 * n)}, compiler_params=SPLIT_COPY,
    )(*arrays, send_sems, recv_sems, after)
    return res[:n], res[n:]


def _gather_copies(x_refs, out_refs, send_sems, recv_sems):
    x, y, c = _position()
    peers = [(x, y, 1 - c), (1 - x, y, c), (x, 1 - y, c), (1 - x, 1 - y, c)]
    sends, arrivals = [], []
    for a, (x_ref, out_ref) in enumerate(zip(x_refs, out_refs, strict=True)):
        for k, (px, py, pc) in enumerate(peers):
            sems = dict(send_sem=send_sems.at[4 * a + k], recv_sem=recv_sems.at[4 * a + k],
                        device_id=(px, py, pc), device_id_type=MESH)
            sends.append(pltpu.make_async_remote_copy(src_ref=x_ref, dst_ref=out_ref.at[4 * x + 2 * y + c], **sems))
            arrivals.append(pltpu.make_async_remote_copy(src_ref=x_ref, dst_ref=out_ref.at[4 * px + 2 * py + pc],
                                                         **sems))
    return sends, arrivals


def _gather_start(blocks, after, name):
    n = len(blocks)
    outs = [_in_hbm(lax.empty((N_DEV,) + b.shape, b.dtype)) for b in blocks]

    def body(*refs):
        sends, _ = _gather_copies(refs[:n], refs[n:2 * n], refs[2 * n + 1], refs[2 * n + 2])
        for cp in sends:
            cp.start()
        refs[-1][...] = jnp.zeros_like(refs[-1])

    sems = pltpu.SemaphoreType.DMA((4 * n,))
    res = pl.pallas_call(
        body, name=name, in_specs=[HBM_SPEC] * (2 * n) + [pl.BlockSpec(memory_space=pl.ANY)],
        out_shape=(sems, sems, *[pltpu.HBM(t.shape, t.dtype) for t in (*blocks, *outs)], SDS((8, LANES), f32)),
        out_specs=(SEM_SPEC, SEM_SPEC, *[HBM_SPEC] * (2 * n), pl.BlockSpec(memory_space=pltpu.VMEM)),
        input_output_aliases={i: 2 + i for i in range(2 * n)}, compiler_params=SPLIT_COPY,
    )(*[_in_hbm(b) for b in blocks], *outs, after)
    return res[:-1], res[-1]


def _gather_wait(state, after, name):
    send_sems, recv_sems, *arrays = state
    n = len(arrays) // 2

    def body(*refs):
        sends, arrivals = _gather_copies(refs[:n], refs[n:2 * n], refs[2 * n], refs[2 * n + 1])
        for cp in sends:
            cp.wait_send()
        for cp in arrivals:
            cp.wait_recv()

    res = pl.pallas_call(
        body, name=name, in_specs=[HBM_SPEC] * (2 * n) + [SEM_SPEC, SEM_SPEC, pl.BlockSpec(memory_space=pl.ANY)],
        out_shape=[pltpu.HBM(t.shape, t.dtype) for t in arrays], out_specs=[HBM_SPEC] * (2 * n),
        input_output_aliases={i: i for i in range(2 * n)}, compiler_params=SPLIT_COPY,
    )(*arrays, send_sems, recv_sems, after)
    return res[:n], res[n:]


def _gather_finish(partial, name):
    n = len(partial)

    def body(*refs):
        in_refs, out_refs = refs[:n], refs[n:2 * n]
        send_sems, recv_sems = refs[2 * n:]
        x, y, c = _position()
        chips = [(1 - x, y), (x, 1 - y), (1 - x, 1 - y)]
        copies = []
        for a in range(n):
            for j, (px, py) in enumerate(chips):
                cp = pltpu.make_async_remote_copy(
                    src_ref=in_refs[a].at[4 * px + 2 * py + c], dst_ref=out_refs[a].at[4 * px + 2 * py + c],
                    send_sem=send_sems.at[a, j], recv_sem=recv_sems.at[a, j], device_id=(x, y, 1 - c),
                    device_id_type=MESH)
                cp.start()
                copies.append(cp)
        for a in range(n):
            for j, (px, py) in enumerate(chips):
                pltpu.make_async_remote_copy(
                    src_ref=in_refs[a].at[4 * px + 2 * py + (1 - c)], dst_ref=out_refs[a].at[4 * px + 2 * py + (1 - c)],
                    send_sem=send_sems.at[a, j], recv_sem=recv_sems.at[a, j], device_id=(x, y, 1 - c),
                    device_id_type=MESH).wait_recv()
        for cp in copies:
            cp.wait_send()

    hbm = pl.BlockSpec(memory_space=pl.ANY)
    return pl.pallas_call(
        body, out_shape=[SDS(p.shape, p.dtype) for p in partial], in_specs=[hbm] * n, out_specs=[hbm] * n,
        input_output_aliases={a: a for a in range(n)},
        scratch_shapes=[pltpu.SemaphoreType.DMA((n, 3)), pltpu.SemaphoreType.DMA((n, 3))],
        name=name,
    )(*partial)


def _row_tile(rows):
    return 512 if rows % 512 == 0 and rows > 512 else rows


def _pair_add(g, r1, core, name):
    def body(c_ref, g_ref, r_ref, o_ref):
        o_ref[...] = (g_ref[...].astype(f32) + r_ref[...].astype(f32)).astype(o_ref.dtype)

    rows, cols = g.shape[1:]
    tile = _row_tile(rows)
    blk = (1, tile, cols)
    return pl.pallas_call(
        body, out_shape=SDS((4, rows, cols), g.dtype), name=name,
        grid_spec=pltpu.PrefetchScalarGridSpec(
            num_scalar_prefetch=1, grid=(4, rows // tile),
            in_specs=[pl.BlockSpec(blk, lambda k, i, c_ref: (2 * k + c_ref[0], i, 0)),
                      pl.BlockSpec(blk, lambda k, i, c_ref: (k, i, 0))],
            out_specs=pl.BlockSpec(blk, lambda k, i, c_ref: (k, i, 0))),
        compiler_params=_params(("parallel", "arbitrary")),
    )(core, g, r1)


def _chip_add(t, r2, chip, name):
    def body(c_ref, t_ref, r_ref, o_ref):
        o_ref[...] = ((t_ref[0].astype(f32) + r_ref[0].astype(f32)) + r_ref[1].astype(f32)) + r_ref[2].astype(f32)

    rows, cols = t.shape[1:]
    tile = _row_tile(rows)
    return pl.pallas_call(
        body, out_shape=SDS((rows, cols), f32), name=name,
        grid_spec=pltpu.PrefetchScalarGridSpec(
            num_scalar_prefetch=1, grid=(rows // tile,),
            in_specs=[pl.BlockSpec((1, tile, cols), lambda i, c_ref: (c_ref[0], i, 0)),
                      pl.BlockSpec((3, tile, cols), lambda i, c_ref: (0, i, 0))],
            out_specs=pl.BlockSpec((tile, cols), lambda i, c_ref: (i, 0))),
        compiler_params=_params(("arbitrary",)),
    )(chip, t, r2)


def _pad_to(t, axis, size):
    pads = [(0, 0)] * t.ndim
    pads[axis] = (0, size - t.shape[axis])
    return jnp.pad(t, pads)


_REF_COLS = {"qa": (0, FOX_W), "ka": (FOX_W, FOX_W), "va": (2 * FOX_W, FOX_W), "f": (3 * FOX_W, N_FOX_HEADS)}
_REF_COLS.update({n: (3 * FOX_W + N_FOX_HEADS + i * DIL_W, DIL_W) for i, n in enumerate(("qb", "kb", "vb"))})
_REF_COLS.update({n: (3 * FOX_W + N_FOX_HEADS + 3 * DIL_W + i * D, D) for i, n in enumerate(("ga", "gb"))})
_REF_ORDER = ("qa", "ka", "va", "f", "qb", "kb", "vb", "ga", "gb")


def _shard_pad_cols(pieces):
    first = pieces[_REF_ORDER[0]]
    pad = jnp.zeros((first.shape[0], W_IN_PAD - W_IN_SH), first.dtype)
    parts, names, used = [], list(_REF_ORDER), 0
    for _ in range(N_DEV):
        need = W_IN_SH
        while need:
            take = min(need, _REF_COLS[names[0]][1] - used)
            parts.append(pieces[names[0]][:, used:used + take])
            need, used = need - take, used + take
            if used == _REF_COLS[names[0]][1]:
                names, used = names[1:], 0
        parts.append(pad)
    return jnp.concatenate(parts, axis=1)


def _slab_w_in(stack):
    def cols(name):
        lo, width = _REF_COLS[name]
        hi, out = lo + width, []
        while lo < hi:
            j, off = divmod(lo, W_IN_SH)
            n = min(hi - lo, W_IN_SH - off)
            out.append(stack[j, :, off:off + n])
            lo += n
        return out
    z = lambda n: [jnp.zeros((stack.shape[1], n), stack.dtype)]
    parts = (cols("ga") + cols("gb") + z(C_QB - 2 * D) + cols("qb") + cols("kb") + cols("vb") + cols("qa")
             + cols("ka") + cols("va") + cols("f") + z(LANES - N_FOX_HEADS))
    return jnp.concatenate(parts, axis=1)


def kernel(x, c, w_ada, b_ada, g_mix, w_in, b_fgate, w_br_a, w_br_b, w_out, g_ffn, w_ffn_gate, w_ffn_up, w_ffn_down, g_final, loss_target, m_w_ada, m_b_ada, m_g_mix, m_w_in, m_b_fgate, m_w_br_a, m_w_br_b, m_w_out, m_g_ffn, m_w_ffn_gate, m_w_ffn_up, m_w_ffn_down, m_g_final, v_w_ada, v_b_ada, v_g_mix, v_w_in, v_b_fgate, v_w_br_a, v_w_br_b, v_w_out, v_g_ffn, v_w_ffn_gate, v_w_ffn_up, v_w_ffn_down, v_g_final):
    px, py, pc = _position()
    dev = 4 * px + 2 * py + pc
    x2d, tgt = x[0], loss_target[0]

    c_all = _all_gather(c, "gather_c").reshape(N_DEV, D)
    ada_cols = w_ada.shape[2]
    b_shard = lax.dynamic_slice(b_ada, (0, dev * ada_cols), (1, ada_cols))
    mod_shard = _ada_fwd(c_all, w_ada[0], b_shard)
    mod_all = _all_gather(mod_shard, "gather_mod")
    modv = lax.dynamic_index_in_dim(mod_all, dev, axis=1, keepdims=False).reshape(6, D)

    gate_up = jnp.concatenate([_pad_to(w_ffn_gate[0], 1, FF_PAD), _pad_to(w_ffn_up[0], 1, FF_PAD)], axis=1)
    w_in_s, = _all_gather_many([_pad_to(w_in[0], 1, W_IN_PAD).astype(bf16)], "gather_w_in")
    later = [w_br_a[0], w_br_b[0], w_out[0], gate_up, _pad_to(w_ffn_down[0], 0, FF_PAD)]
    later_state, later_token = _gather_start([t.astype(bf16) for t in later], w_in_s, "gather_rest_start")
    w_in_p = _slab_w_in(w_in_s)

    h1 = _pre1(x2d, modv, g_mix)
    proj = _matmul(h1, w_in_p, name="mm_proj", tm=SEQ, tn=896, tk=D, after=later_token)
    b_pad = jnp.pad(b_fgate, ((0, 0), (0, LANES - N_FOX_HEADS)))
    q_aug, k_aug, va = _fox_prep(proj, _fox_gate_fwd(proj, b_pad))
    ya_h, lse_a = _fox_fwd(q_aug, k_aug, va)

    tables = _rope_tables()
    qb_r, kb_r, vb = _rope_fwd(proj, tables)
    by_group = [_dil_fwd(qb_r, kb_r, vb, grp) for grp in range(N_GROUPS)]
    yb_h, lse_b = _dil_combine([o for o, _ in by_group], [l for _, l in by_group])

    mine, arrived = _gather_wait(later_state, yb_h, "gather_rest_wait")
    w_a_s, w_b_s, w_o_s, w_gu_s, w_d_s = [
        lax.dynamic_update_slice(stack, block[None], (dev, 0, 0))
        for stack, block in zip(_gather_finish(arrived, "gather_rest_finish"), mine, strict=True)]
    w_o = w_o_s.reshape(D, D)
    w_d = w_d_s.reshape(FF_HID, D)
    ya = _matmul_stack(ya_h, w_a_s, name="mm_br_a")
    yb = _matmul_stack(yb_h, w_b_s, name="mm_br_b")

    merged = _merge_fwd(ya, yb, proj)
    mix = _matmul(merged, w_o, name="mm_out", tm=SEQ, tn=512, tk=D)
    x1, h2 = _post1(x2d, mix, modv, g_ffn)
    act, au = _ffn_in(h2, w_gu_s)
    ff = _matmul(act, w_d, name="mm_ffn_down", tm=SEQ, tn=512, tk=FF_HID // 2)

    dx2, dff, dg_final, dga_f, loss_lanes = _final(x1, ff, tgt, modv, g_final.reshape(1, D))
    dact = _matmul(dff, w_d, tb=True, name="mm_d_act", tm=SEQ // 2, tn=FF_HID // 2, tk=D)
    dau = _swiglu_bwd(au, dact)

    core = pc.astype(jnp.int32).reshape(1)
    chip = (2 * px + py).astype(jnp.int32).reshape(1)

    def pair_done(state, after, tags, name):
        mine, theirs = _exchange_wait("pair", state, after, "pair_wait_" + name)
        sums = [_pair_add(g, r, core, "pair_add_" + t) for g, r, t in zip(mine, theirs, tags)]
        return _exchange_start("chip", sums, "chip_start_" + name)

    def from_chips(state, after, tags, name):
        sums, got = _exchange_wait("chip", state, after, "chip_wait_" + name)
        return [_chip_add(p, r, chip, "chip_add_" + t) for p, r, t in zip(sums, got, tags)]

    g_gu = _matmul(h2, dau, ta=True, by_shard=True, out_dtype=bf16, name="mm_g_ffn_in", tm=D, tn=2 * FF_PAD, tk=SEQ)
    g_d = _matmul(act, dff, ta=True, out_dtype=bf16, name="mm_g_down", tm=FF_HID // 2, tn=512, tk=SEQ)
    ffn_tags = ["gu", "down"]
    ffn_pair, ffn_pair_token = _exchange_start("pair", [g_gu, g_d.reshape(N_DEV, FF_PAD, D)], "pair_start_ffn")

    dh2 = _matmul(dau, w_gu_s, tb=True, by_shard=True, name="mm_d_h2", tm=SEQ // 2, tn=D, tk=2 * FF_PAD,
                  after=ffn_pair_token)
    ffn_state, ffn_token = pair_done(ffn_pair, dh2, ffn_tags, "ffn")
    dx1, dmix, dsh_f, dsc_f, dg_ffn, dga_m = _mid_bwd(dh2, x1, dx2, mix, modv, g_ffn)
    dmerged = _matmul(dmix, w_o, tb=True, name="mm_d_merged", tm=SEQ, tn=512, tk=D, after=ffn_token)
    dya, dyb, dga, dgb = _merge_bwd(dmerged, ya, yb, proj)
    dya_h = _matmul_stack(dya, w_a_s, tb=True, name="mm_d_ya")
    dyb_h = _matmul_stack(dyb, w_b_s, tb=True, name="mm_d_yb")

    dqa, dka, dva, dF = _fox_bwd(q_aug, k_aug, va, dya_h, ya_h, lse_a)
    dF_row = jnp.pad(dF[:, :2, :].reshape(N_FOX_HEADS, SEQ), ((0, LANES - N_FOX_HEADS), (0, 0)))
    df, db_fgate = _fox_gate_bwd(dF_row, proj, b_pad)

    delta_b = _dil_delta(dyb_h, yb_h)
    dil_grads = [_dil_bwd(qb_r, kb_r, vb, dyb_h, lse_b, delta_b, grp) for grp in range(N_GROUPS)]
    dqb, dkb = _rope_bwd([t[0] for t in dil_grads], [t[1] for t in dil_grads], tables)
    dvb = jnp.concatenate([t[2] for t in dil_grads], axis=1).astype(bf16)

    dproj = _shard_pad_cols({"qa": dqa, "ka": dka, "va": dva, "f": df[:, :N_FOX_HEADS], "qb": dqb, "kb": dkb,
                             "vb": dvb, "ga": dga, "gb": dgb})
    g_in = _matmul(h1, dproj, ta=True, by_shard=True, out_dtype=bf16, name="mm_g_in", tm=D, tn=W_IN_PAD, tk=SEQ)
    g_o = _matmul(merged, dmix, ta=True, out_dtype=bf16, name="mm_g_out", tm=D, tn=512, tk=SEQ)
    g_a = _matmul_stack(ya_h, dya, ta=True, out_dtype=bf16, name="mm_g_br_a")
    g_b = _matmul_stack(yb_h, dyb, ta=True, out_dtype=bf16, name="mm_g_br_b")
    rows_a, rows_b = FOX_W * W_BR_SH // D, DIL_OUT_W * W_BR_SH // D
    g_small = jnp.concatenate([g_a.reshape(N_DEV, rows_a, D), g_b.reshape(N_DEV, rows_b, D),
                               g_o.reshape(N_DEV, W_BR_SH, D)], axis=1)
    mix_tags = ["in", "small"]
    mix_pair, mix_pair_token = _exchange_start("pair", [g_in, g_small], "pair_start_mixer")

    dh1 = _matmul(dproj, w_in_s, tb=True, by_shard=True, name="mm_d_h1", tm=SEQ // 2, tn=D, tk=W_IN_PAD,
                  after=mix_pair_token)
    grad_x, dsh_m, dsc_m, dg_mix = _first_bwd(dh1, x2d, dx1, modv, g_mix)

    pad_lane = lambda t: jnp.pad(t, ((0, 0), (0, D - t.shape[1])))
    small = jnp.concatenate([dsh_m, dsc_m, dga_m, dsh_f, dsc_f, dga_f, dg_mix, dg_ffn, dg_final,
                             pad_lane(db_fgate), loss_lanes, jnp.zeros((SMALL_ROWS - 11, D), f32)], axis=0)
    small_all = _all_gather(small, "gather_small")
    mix_state, mix_token = pair_done(mix_pair, small_all, mix_tags, "mixer")

    small_sum, loss_row = _small_reduce(small_all, mix_token)
    dmod_all = small_all[:, :6, :].reshape(N_DEV, 6 * D)
    g_w_ada = _ada_bwd(c_all, lax.dynamic_slice(dmod_all, (0, dev * ada_cols), (N_DEV, ada_cols)))
    s_gu, s_d = from_chips(ffn_state, small_sum, ffn_tags, "ffn")

    loss = loss_row[0, 0]
    g = {
        "w_ada": g_w_ada[None], "b_ada": small_sum[0:6].reshape(1, 6 * D), "g_mix": small_sum[6:7],
        "b_fgate": small_sum[9:10, :N_FOX_HEADS], "g_ffn": small_sum[7:8], "w_ffn_gate": s_gu[None, :, :W_FF_SH],
        "w_ffn_up": s_gu[None, :, FF_PAD:FF_PAD + W_FF_SH], "w_ffn_down": s_d[None, :W_FF_SH],
        "g_final": small_sum[8],
    }
    w = {"w_ada": w_ada, "b_ada": b_ada, "g_mix": g_mix, "w_in": w_in, "b_fgate": b_fgate, "w_br_a": w_br_a,
         "w_br_b": w_br_b, "w_out": w_out, "g_ffn": g_ffn, "w_ffn_gate": w_ffn_gate, "w_ffn_up": w_ffn_up,
         "w_ffn_down": w_ffn_down, "g_final": g_final}
    m = {"w_ada": m_w_ada, "b_ada": m_b_ada, "g_mix": m_g_mix, "w_in": m_w_in, "b_fgate": m_b_fgate,
         "w_br_a": m_w_br_a, "w_br_b": m_w_br_b, "w_out": m_w_out, "g_ffn": m_g_ffn, "w_ffn_gate": m_w_ffn_gate,
         "w_ffn_up": m_w_ffn_up, "w_ffn_down": m_w_ffn_down, "g_final": m_g_final}
    v = {"w_ada": v_w_ada, "b_ada": v_b_ada, "g_mix": v_g_mix, "w_in": v_w_in, "b_fgate": v_b_fgate,
         "w_br_a": v_w_br_a, "w_br_b": v_w_br_b, "w_out": v_w_out, "g_ffn": v_g_ffn, "w_ffn_gate": v_w_ffn_gate,
         "w_ffn_up": v_w_ffn_up, "w_ffn_down": v_w_ffn_down, "g_final": v_g_final}
    names = list(w)
    delta, new_m, new_v = {}, {}, {}

    transposed = ("w_in", "w_ffn_gate", "w_ffn_up")

    def update(n):
        shape = w[n].shape
        if n in transposed:
            g_t = g[n][0].T
            dl, mn, vn = _adamw(w[n][0].T, g_t, m[n][0].T, v[n][0].T, "adamw_" + n)
            g[n], delta[n], new_m[n], new_v[n] = g_t.T[None], dl.T[None], mn.T[None], vn.T[None]
            return
        two_d = (lambda t: t.reshape(shape[-2:])) if len(shape) == 3 else (lambda t: t)
        dl, mn, vn = _adamw(two_d(w[n]), two_d(g[n]), two_d(m[n]), two_d(v[n]), "adamw_" + n)
        delta[n], new_m[n], new_v[n] = dl.reshape(shape), mn.reshape(shape), vn.reshape(shape)

    for n in list(g):
        update(n)
    done = sum(delta[n].reshape(-1)[:N_FOX_HEADS] for n in g)
    s_in, s_small = from_chips(mix_state, done, mix_tags, "mixer")
    g.update({"w_in": s_in[None, :, :W_IN_SH], "w_br_a": s_small[:rows_a].reshape(1, FOX_W, W_BR_SH),
              "w_br_b": s_small[rows_a:rows_a + rows_b].reshape(1, DIL_OUT_W, W_BR_SH),
              "w_out": s_small[None, rows_a + rows_b:]})
    for n in ("w_in", "w_br_a", "w_br_b", "w_out"):
        update(n)

    return (loss, grad_x[None], *[g[n] for n in names], *[delta[n] for n in names],
            *[new_m[n] for n in names], *[new_v[n] for n in names])
```

```python
import functools

import jax
import jax.numpy as jnp
from jax import lax
from jax.experimental import pallas as pl
from jax.experimental.pallas import tpu as pltpu

f32 = jnp.float32
bf16 = jnp.bfloat16
SDS = jax.ShapeDtypeStruct
MESH = pl.DeviceIdType.MESH

N_DEV = 8
D = 1024
SEQ = 2048
HEAD_DIM = 64
N_FOX_HEADS = 8
FOX_W = 512
DIL_W = 768
DIL_OUT_W = 256
ROT_DIM = 16
ROPE_THETA = 500000.0
D_FF = 2816
IN_COLS = 5896
EPS = 1e-6
NEG = -1e30
ATT_SCALE = HEAD_DIM ** -0.5

ADAM_LR = 0.001
ADAM_B1 = 0.9
ADAM_B2 = 0.999
ADAM_EPS = 1e-08
ADAM_WD = 0.01
ADAM_STEP = 10

C_GA, C_GB, C_QB, C_KB, C_VB, C_QA, C_KA, C_VA, C_F = 0, 1024, 2304, 3072, 3840, 4608, 5120, 5632, 6144
PROJ_W = 6272
LANES = 128
VMEM_LIMIT = 52 * 1024 * 1024

W_IN_SH, W_IN_PAD = IN_COLS // N_DEV, 768
W_BR_SH = D // N_DEV
W_FF_SH, FF_PAD = D_FF // N_DEV, 384
FF_HID = N_DEV * FF_PAD
SMALL_ROWS = 16


def _params(sem=None):
    if sem is None:
        return pltpu.CompilerParams(vmem_limit_bytes=VMEM_LIMIT)
    return pltpu.CompilerParams(dimension_semantics=sem, vmem_limit_bytes=VMEM_LIMIT)


def _rowwise(fn, name, tiled, vecs, outs, reds=(), tile=256):
    nt, nv, no = len(tiled), len(vecs), len(outs)
    rows = tiled[0][0].shape[0]
    assert rows % tile == 0

    def body(*refs):
        tin = [r[...] for r in refs[:nt]]
        vin = [r[...] for r in refs[nt:nt + nv]]
        orefs = refs[nt + nv:nt + nv + no]
        rrefs = refs[nt + nv + no:]
        touts, routs = fn(tin, vin)
        for r, t in zip(orefs, touts, strict=True):
            r[...] = t.astype(r.dtype)
        if rrefs:
            @pl.when(pl.program_id(0) == 0)
            def _():
                for r in rrefs:
                    r[...] = jnp.zeros_like(r)
            for r, t in zip(rrefs, routs, strict=True):
                r[...] += t

    def col_map(cb):
        return lambda i: (i, cb)

    def whole_map(nd):
        return lambda i: (0,) * nd

    in_specs = [pl.BlockSpec((tile, w), col_map(cb)) for (_, w, cb) in tiled]
    in_specs += [pl.BlockSpec(v.shape, whole_map(v.ndim)) for v in vecs]
    out_specs = [pl.BlockSpec((tile, w), lambda i: (i, 0)) for (w, _) in outs]
    out_specs += [pl.BlockSpec((1, w), lambda i: (0, 0)) for w in reds]
    out_shape = [SDS((rows, w), dt) for (w, dt) in outs] + [SDS((1, w), f32) for w in reds]
    res = pl.pallas_call(
        body, grid=(rows // tile,), in_specs=in_specs, out_specs=out_specs, out_shape=out_shape, name=name,
        compiler_params=_params(("arbitrary",)),
    )(*[t[0] for t in tiled], *vecs)
    return res


def _matmul(a, b, *, ta=False, tb=False, out_dtype=f32, name, tm, tn, tk, by_shard=False, after=None):
    m, k = (a.shape[1], a.shape[0]) if ta else a.shape
    if by_shard and not ta:
        n, kb = (b.shape[1], N_DEV * b.shape[2]) if tb else (N_DEV * b.shape[2], b.shape[1])
        assert (tk if tb else tn) == b.shape[2]
    else:
        n, kb = (b.shape[0], b.shape[1]) if tb else (b.shape[1], b.shape[0])
    assert kb == k and m % tm == 0 and n % tn == 0 and k % tk == 0
    nk = k // tk
    dims = (((0 if ta else 1,), (1 if tb else 0,)), ((), ()))
    b_stacked = by_shard and not ta
    o_stacked = by_shard and ta

    def body(a_ref, b_ref, *rest):
        o_ref, *acc = rest[1:] if after is not None else rest
        bv = b_ref[0] if b_stacked else b_ref[...]
        p = lax.dot_general(a_ref[...].astype(bf16), bv.astype(bf16), dims, preferred_element_type=f32)

        def put(val):
            if o_stacked:
                o_ref[0] = val.astype(o_ref.dtype)
            else:
                o_ref[...] = val.astype(o_ref.dtype)

        if nk == 1:
            put(p)
        else:
            acc_ref, = acc
            kk = pl.program_id(2)

            @pl.when(kk == 0)
            def _():
                acc_ref[...] = p

            @pl.when(kk > 0)
            def _():
                acc_ref[...] += p

            @pl.when(kk == nk - 1)
            def _():
                put(acc_ref[...])

    a_spec = pl.BlockSpec((tk, tm), lambda i, j, kk: (kk, i)) if ta else pl.BlockSpec((tm, tk), lambda i, j, kk: (i, kk))
    if b_stacked and tb:
        b_spec = pl.BlockSpec((1, tn, tk), lambda i, j, kk: (kk, j, 0))
    elif b_stacked:
        b_spec = pl.BlockSpec((1, tk, tn), lambda i, j, kk: (j, kk, 0))
    elif tb:
        b_spec = pl.BlockSpec((tn, tk), lambda i, j, kk: (j, kk))
    else:
        b_spec = pl.BlockSpec((tk, tn), lambda i, j, kk: (kk, j))
    if o_stacked:
        assert tn == n // N_DEV
        out_spec = pl.BlockSpec((1, tm, tn), lambda i, j, kk: (j, i, 0))
        out_shape = SDS((N_DEV, m, tn), out_dtype)
    else:
        out_spec = pl.BlockSpec((tm, tn), lambda i, j, kk: (i, j))
        out_shape = SDS((m, n), out_dtype)
    extra_specs, extra = ([pl.BlockSpec(memory_space=pl.ANY)], [after]) if after is not None else ([], [])
    return pl.pallas_call(
        body, grid=(m // tm, n // tn, nk), in_specs=[a_spec, b_spec] + extra_specs, out_specs=out_spec,
        out_shape=out_shape, name=name, scratch_shapes=[pltpu.VMEM((tm, tn), f32)] if nk > 1 else [],
        compiler_params=_params(("parallel", "parallel", "arbitrary")),
    )(a, b, *extra)


def _matmul_stack(a, b, *, ta=False, tb=False, out_dtype=f32, name):
    def lanes(ref):
        return jnp.concatenate([ref[j] for j in range(N_DEV)], axis=1).astype(bf16)

    if ta:
        w = b.shape[1] // N_DEV

        def body(a_ref, b_ref, o_ref):
            p = _tn(a_ref[...].astype(bf16), b_ref[...].astype(bf16))
            for j in range(N_DEV):
                o_ref[j] = p[:, j * w:(j + 1) * w].astype(o_ref.dtype)

        return pl.pallas_call(body, out_shape=SDS((N_DEV, a.shape[1], w), out_dtype), name=name,
                              compiler_params=_params())(a, b)

    m, half = a.shape[0], a.shape[0] // 2
    n = b.shape[1] if tb else N_DEV * b.shape[2]

    def body(a_ref, b_ref, o_ref):
        av = a_ref[...].astype(bf16)
        o_ref[...] = (_nt(av, lanes(b_ref)) if tb else jnp.dot(av, lanes(b_ref), preferred_element_type=f32)
                      ).astype(o_ref.dtype)

    return pl.pallas_call(
        body, grid=(2,), in_specs=[pl.BlockSpec((half, a.shape[1]), lambda i: (i, 0)),
                                   pl.BlockSpec(b.shape, lambda i: (0, 0, 0))],
        out_specs=pl.BlockSpec((half, n), lambda i: (i, 0)), out_shape=SDS((m, n), out_dtype), name=name,
        compiler_params=_params(("parallel",)),
    )(a, b)


def _rms(x):
    r = lax.rsqrt(jnp.mean(x * x, axis=-1, keepdims=True) + EPS)
    return r, x * r


def _rms_bwd(r, xn, dxn):
    return r * (dxn - xn * jnp.mean(dxn * xn, axis=-1, keepdims=True))


def _colsum(t):
    return jnp.sum(t, axis=0, keepdims=True)


def _sigmoid(x):
    return 1.0 / (1.0 + jnp.exp(-x))


def _modulated_norm(x, g, shift, scale):
    _, xn = _rms(x)
    return (xn * g) * (1.0 + scale) + shift


def _pre1(x, modv, g_mix):
    def fn(t, v):
        (xt,), (mv, g) = t, v
        return [_modulated_norm(xt, g, mv[0:1], mv[1:2])], []
    return _rowwise(fn, "pre1", [(x, D, 0)], [modv, g_mix], [(D, bf16)])[0]


def _post1(x, mix, modv, g_ffn):
    def fn(t, v):
        (xt, mt), (mv, g) = t, v
        x1 = xt + mv[2:3] * mt
        return [x1, _modulated_norm(x1, g, mv[3:4], mv[4:5])], []
    return _rowwise(fn, "post1", [(x, D, 0), (mix, D, 0)], [modv, g_ffn], [(D, f32), (D, bf16)])


def _ffn_in(h, w_stack):
    def body(h_ref, w_ref, act_ref, au_ref):
        p = jnp.dot(h_ref[...], w_ref[0], preferred_element_type=f32)
        a, u = p[:, :FF_PAD], p[:, FF_PAD:]
        act_ref[...] = (a * _sigmoid(a) * u).astype(act_ref.dtype)
        au_ref[...] = p.astype(au_ref.dtype)

    return pl.pallas_call(
        body, grid=(N_DEV,),
        in_specs=[pl.BlockSpec((SEQ, D), lambda j: (0, 0)), pl.BlockSpec((1, D, 2 * FF_PAD), lambda j: (j, 0, 0))],
        out_specs=[pl.BlockSpec((SEQ, FF_PAD), lambda j: (0, j)), pl.BlockSpec((SEQ, 2 * FF_PAD), lambda j: (0, j))],
        out_shape=(SDS((SEQ, FF_HID), bf16), SDS((SEQ, 2 * FF_HID), bf16)), name="ffn_in",
        compiler_params=_params(("parallel",)),
    )(h, w_stack)


def _ffn_bwd_in(dff, w_down_stack, au):
    def body(d_ref, w_ref, au_ref, o_ref):
        dact = _nt(d_ref[...], w_ref[0])
        p = au_ref[...].astype(f32)
        a, u = p[:, :FF_PAD], p[:, FF_PAD:]
        sg = _sigmoid(a)
        o_ref[...] = jnp.concatenate([dact * u * (sg * (1.0 + a * (1.0 - sg))), dact * (a * sg)],
                                     axis=1).astype(o_ref.dtype)

    return pl.pallas_call(
        body, grid=(N_DEV,),
        in_specs=[pl.BlockSpec((SEQ, D), lambda j: (0, 0)), pl.BlockSpec((1, FF_PAD, D), lambda j: (j, 0, 0)),
                  pl.BlockSpec((SEQ, 2 * FF_PAD), lambda j: (0, j))],
        out_specs=pl.BlockSpec((SEQ, 2 * FF_PAD), lambda j: (0, j)),
        out_shape=SDS((SEQ, 2 * FF_HID), bf16), name="ffn_bwd_in", compiler_params=_params(("parallel",)),
    )(dff, w_down_stack, au)


def _final(x1, ff, target, modv, g_final):
    def fn(t, v):
        (x1t, fft, tgt), (mv, g) = t, v
        x2 = x1t + mv[5:6] * fft
        r, xn = _rms(x2)
        err = xn * g - tgt
        dy = err * (1.0 / D)
        dx2 = _rms_bwd(r, xn, dy * g)
        return [dx2, dx2 * mv[5:6]], [_colsum(dy * xn), _colsum(dx2 * fft), _colsum(err * err) * (0.5 / D)]
    return _rowwise(fn, "final", [(x1, D, 0), (ff, D, 0), (target, D, 0)], [modv, g_final],
                    [(D, f32), (D, bf16)], [D, D, D])


def _mid_bwd(dh2, x1, dx2, mix, modv, g_ffn):
    def fn(t, v):
        (dh, x1t, dx2t, mt), (mv, g) = t, v
        r, xn = _rms(x1t)
        dn = dh * (1.0 + mv[4:5])
        dx1 = dx2t + _rms_bwd(r, xn, dn * g)
        return [dx1, dx1 * mv[2:3]], [_colsum(dh), _colsum(dh * (xn * g)), _colsum(dn * xn), _colsum(dx1 * mt)]
    return _rowwise(fn, "mid_bwd", [(dh2, D, 0), (x1, D, 0), (dx2, D, 0), (mix, D, 0)], [modv, g_ffn],
                    [(D, f32), (D, bf16)], [D, D, D, D])


def _first_bwd(dh1, x, dx1, modv, g_mix):
    def fn(t, v):
        (dh, xt, dx1t), (mv, g) = t, v
        r, xn = _rms(xt)
        dn = dh * (1.0 + mv[1:2])
        return [dx1t + _rms_bwd(r, xn, dn * g)], [_colsum(dh), _colsum(dh * (xn * g)), _colsum(dn * xn)]
    return _rowwise(fn, "first_bwd", [(dh1, D, 0), (x, D, 0), (dx1, D, 0)], [modv, g_mix], [(D, f32)], [D, D, D])


def _merge_fwd(ya, yb, proj):
    def fn(t, v):
        ya_t, yb_t, ga, gb = t
        return [_sigmoid(ga) * ya_t + _sigmoid(gb) * yb_t], []
    return _rowwise(fn, "merge_fwd", [(ya, D, 0), (yb, D, 0), (proj, D, C_GA // D), (proj, D, C_GB // D)], [],
                    [(D, bf16)])[0]


def _merge_bwd(dmerged, ya, yb, proj):
    def fn(t, v):
        dm, ya_t, yb_t, ga, gb = t
        sa, sb = _sigmoid(ga), _sigmoid(gb)
        return [dm * sa, dm * sb, dm * ya_t * (sa * (1.0 - sa)), dm * yb_t * (sb * (1.0 - sb))], []
    return _rowwise(fn, "merge_bwd",
                    [(dmerged, D, 0), (ya, D, 0), (yb, D, 0), (proj, D, C_GA // D), (proj, D, C_GB // D)], [],
                    [(D, bf16), (D, bf16), (D, bf16), (D, bf16)])


def _rope_tables():
    half = ROT_DIM // 2
    pos = jnp.arange(SEQ, dtype=f32)
    inv_freq = ROPE_THETA ** (-jnp.arange(0, ROT_DIM, 2, dtype=f32) / ROT_DIM)
    ang = pos[:, None] * inv_freq[None, :]
    cos, sin = jnp.cos(ang), jnp.sin(ang)
    pad = jnp.zeros((SEQ, HEAD_DIM - ROT_DIM), f32)
    zero = jnp.zeros((SEQ, half), f32)
    c_head = jnp.concatenate([cos, cos, pad + 1.0], axis=1)
    lo_head = jnp.concatenate([-sin, zero, pad], axis=1)
    hi_head = jnp.concatenate([zero, sin, pad], axis=1)
    return tuple(jnp.concatenate([t, t], axis=1) for t in (c_head, lo_head, hi_head))


def _over_heads(tables):
    return [jnp.tile(t, (1, DIL_W // LANES)) for t in tables]


def _rope_fwd(proj, tables):
    half = ROT_DIM // 2

    def fn(t, v):
        q, k, vv = t[:3]
        c, lo, hi = _over_heads(t[3:])
        rot = lambda z: z * c + pltpu.roll(z, DIL_W - half, 1) * lo + pltpu.roll(z, half, 1) * hi
        return [rot(q) * ATT_SCALE, rot(k), vv], []
    return _rowwise(fn, "rope_fwd", [(proj, DIL_W, C_QB // DIL_W), (proj, DIL_W, C_KB // DIL_W),
                                     (proj, DIL_W, C_VB // DIL_W)] + [(tb, LANES, 0) for tb in tables], [],
                    [(DIL_W, f32)] * 3)


def _rope_bwd(dqs, dks, tables):
    half = ROT_DIM // 2

    def fn(t, v):
        dq_t, dk_t = jnp.concatenate(t[:N_GROUPS], axis=1), jnp.concatenate(t[N_GROUPS:2 * N_GROUPS], axis=1)
        c, lo, hi = _over_heads(t[2 * N_GROUPS:])
        rot_t = lambda z: z * c + pltpu.roll(z * lo, half, 1) + pltpu.roll(z * hi, DIL_W - half, 1)
        return [rot_t(dq_t), rot_t(dk_t)], []
    return _rowwise(fn, "rope_bwd", [(a, DIL_OUT_W, 0) for a in (*dqs, *dks)] + [(tb, LANES, 0) for tb in tables],
                    [], [(DIL_W, bf16), (DIL_W, bf16)])


def _head_bcast_sum(d):
    lane = lax.broadcasted_iota(jnp.int32, d.shape, 1)
    out = jnp.zeros_like(d)
    for h in range(d.shape[1] // HEAD_DIM):
        sel = (lane >= h * HEAD_DIM) & (lane < (h + 1) * HEAD_DIM)
        out = jnp.where(sel, jnp.sum(jnp.where(sel, d, 0.0), axis=1, keepdims=True), out)
    return out


def _dil_combine(outs, lses):
    def fn(t, v):
        o0, o1, o2, l0, l1, l2 = t
        m = jnp.maximum(jnp.maximum(l0, l1), l2)
        w0, w1, w2 = jnp.exp(l0 - m), jnp.exp(l1 - m), jnp.exp(l2 - m)
        tot = w0 + w1 + w2
        return [(w0 * o0 + w1 * o1 + w2 * o2) / tot, m + jnp.log(tot)], []
    w = DIL_OUT_W
    return _rowwise(fn, "dil_combine", [(t, w, 0) for t in (*outs, *lses)], [], [(w, f32), (w, f32)])


def _dil_delta(dyb_h, yb_h):
    def fn(t, v):
        return [_head_bcast_sum(t[0] * t[1])], []
    return _rowwise(fn, "dil_delta", [(dyb_h, DIL_OUT_W, 0), (yb_h, DIL_OUT_W, 0)], [], [(DIL_OUT_W, f32)])[0]


def _adamw_math(wt, gt, mt, vt):
    mn = ADAM_B1 * mt + (1.0 - ADAM_B1) * gt
    vn = ADAM_B2 * vt + (1.0 - ADAM_B2) * (gt * gt)
    m_hat = mn / (1.0 - ADAM_B1 ** ADAM_STEP)
    v_hat = vn / (1.0 - ADAM_B2 ** ADAM_STEP)
    return -ADAM_LR * (m_hat / (jnp.sqrt(v_hat) + ADAM_EPS) + ADAM_WD * wt), mn, vn


def _adamw(w, g, m, v, name):
    shape = w.shape
    if w.ndim == 1:
        w, g, m, v = (t.reshape(1, -1) for t in (w, g, m, v))
    rows, cols = w.shape
    if rows % 8 and rows > 8:
        return _adamw_by_cols(w, g, m, v, name)
    tile = 256 if rows % 256 == 0 and rows > 512 else rows

    def fn(t, _):
        return list(_adamw_math(*t)), []
    delta, mn, vn = _rowwise(fn, name, [(w, cols, 0), (g, cols, 0), (m, cols, 0), (v, cols, 0)], [],
                             [(cols, f32)] * 3, tile=tile)
    return delta.reshape(shape), mn.reshape(shape), vn.reshape(shape)


def _adamw_by_cols(w, g, m, v, name, tile=256):
    rows, cols = w.shape

    def body(w_ref, g_ref, m_ref, v_ref, d_ref, mn_ref, vn_ref):
        d_ref[...], mn_ref[...], vn_ref[...] = _adamw_math(w_ref[...], g_ref[...], m_ref[...], v_ref[...])

    spec = pl.BlockSpec((rows, tile), lambda j: (0, j))
    return pl.pallas_call(body, grid=(cols // tile,), in_specs=[spec] * 4, out_specs=[spec] * 3,
                          out_shape=[SDS((rows, cols), f32)] * 3, name=name,
                          compiler_params=_params(("parallel",)))(w, g, m, v)


def _ada_fwd(c_all, w_shard, b_shard):
    def body(c_ref, w_ref, b_ref, o_ref):
        cv = c_ref[...]
        sc = (cv * _sigmoid(cv)).astype(bf16)
        o_ref[...] = jnp.dot(sc, w_ref[...].astype(bf16), preferred_element_type=f32) + b_ref[...]
    return pl.pallas_call(body, out_shape=SDS((N_DEV, w_shard.shape[1]), f32), name="ada_fwd",
                          compiler_params=_params())(c_all, w_shard, b_shard)


def _ada_bwd(c_all, dmod_cols):
    def body(c_ref, d_ref, o_ref):
        cv = c_ref[...]
        sc = cv * _sigmoid(cv)
        o_ref[...] = lax.dot_general(sc, d_ref[...], (((0,), (0,)), ((), ())), precision=lax.Precision.HIGHEST,
                                     preferred_element_type=f32)
    return pl.pallas_call(body, out_shape=SDS((D, dmod_cols.shape[1]), f32), name="ada_bwd",
                          compiler_params=_params())(c_all, dmod_cols)


def _small_reduce(gathered, after):
    def body(g_ref, after_ref, o_ref, loss_ref):
        acc = g_ref[0]
        for d in range(1, N_DEV):
            acc = acc + g_ref[d]
        o_ref[...] = acc
        loss_ref[...] = jnp.zeros((1, LANES), f32) + jnp.sum(acc[10:11, :])
    return pl.pallas_call(body, out_shape=(SDS((SMALL_ROWS, D), f32), SDS((1, LANES), f32)), name="small_reduce",
                          in_specs=[pl.BlockSpec(memory_space=pltpu.VMEM), pl.BlockSpec(memory_space=pl.ANY)],
                          compiler_params=_params())(gathered, after)


FOX_BLK = 512
CUM_BLK = 128


def _fold_lanes(t, op):
    out = t[:, :LANES]
    for j in range(1, t.shape[1] // LANES):
        out = op(out, t[:, j * LANES:(j + 1) * LANES])
    return out


def _fox_gate_fwd(proj, b_pad):
    nblk = SEQ // CUM_BLK

    def body(f_ref, b_ref, col_ref):
        r = lax.broadcasted_iota(jnp.int32, (CUM_BLK, CUM_BLK), 0)
        c = lax.broadcasted_iota(jnp.int32, (CUM_BLK, CUM_BLK), 1)
        tri = (r >= c).astype(f32)
        carry = jnp.zeros((1, LANES), f32)
        for blk in range(nblk):
            z = f_ref[blk * CUM_BLK:(blk + 1) * CUM_BLK, :] + b_ref[...]
            logf = jnp.minimum(z, 0.0) - jnp.log1p(jnp.exp(-jnp.abs(z)))
            cs = jnp.dot(tri, logf, precision=lax.Precision.HIGHEST, preferred_element_type=f32) + carry
            col_ref[blk * CUM_BLK:(blk + 1) * CUM_BLK, :] = cs
            carry = cs[CUM_BLK - 1:CUM_BLK, :]

    return pl.pallas_call(
        body, grid=(1,), in_specs=[pl.BlockSpec((SEQ, LANES), lambda i: (0, C_F // LANES)),
                                   pl.BlockSpec((1, LANES), lambda i: (0, 0))],
        out_specs=pl.BlockSpec((SEQ, LANES), lambda i: (0, 0)),
        out_shape=SDS((SEQ, LANES), f32), name="fox_gate_fwd",
        compiler_params=_params(("arbitrary",)),
    )(proj, b_pad)


def _fox_gate_bwd(dF_row, proj, b_pad):
    nblk = SEQ // CUM_BLK

    def body(d_ref, f_ref, b_ref, df_ref, db_ref, col_ref):
        r = lax.broadcasted_iota(jnp.int32, (CUM_BLK, CUM_BLK), 0)
        c = lax.broadcasted_iota(jnp.int32, (CUM_BLK, CUM_BLK), 1)
        tri = (r <= c).astype(f32)
        lane = lax.broadcasted_iota(jnp.int32, (CUM_BLK, LANES), 1)
        col_ref[...] = d_ref[...].T
        carry = jnp.zeros((1, LANES), f32)
        total = jnp.zeros((1, LANES), f32)
        for blk in reversed(range(nblk)):
            rows = slice(blk * CUM_BLK, (blk + 1) * CUM_BLK)
            cs = jnp.dot(tri, col_ref[rows, :], precision=lax.Precision.HIGHEST, preferred_element_type=f32) + carry
            carry = cs[0:1, :]
            z = f_ref[rows, :] + b_ref[...]
            df = jnp.where(lane < N_FOX_HEADS, cs * _sigmoid(-z), 0.0)
            df_ref[rows, :] = df.astype(df_ref.dtype)
            total = total + _colsum(df)
        db_ref[...] = total

    return pl.pallas_call(
        body, grid=(1,), in_specs=[pl.BlockSpec((LANES, SEQ), lambda i: (0, 0)),
                                   pl.BlockSpec((SEQ, LANES), lambda i: (0, C_F // LANES)),
                                   pl.BlockSpec((1, LANES), lambda i: (0, 0))],
        out_specs=[pl.BlockSpec((SEQ, LANES), lambda i: (0, 0)), pl.BlockSpec((1, LANES), lambda i: (0, 0))],
        out_shape=(SDS((SEQ, LANES), bf16), SDS((1, LANES), f32)), name="fox_gate_bwd",
        scratch_shapes=[pltpu.VMEM((SEQ, LANES), f32)],
        compiler_params=_params(("arbitrary",)),
    )(dF_row, proj, b_pad)


def _nt(a, b):
    return lax.dot_general(a, b, (((1,), (1,)), ((), ())), preferred_element_type=f32)


def _tn(a, b):
    return lax.dot_general(a, b, (((0,), (0,)), ((), ())), preferred_element_type=f32)


def _fox_prep(proj, f_col):
    def fn(t, v):
        q, k, vv, fc = t
        lane = lax.broadcasted_iota(jnp.int32, (q.shape[0], LANES), 1)
        qs, ks = [], []
        for h in range(N_FOX_HEADS):
            pair, pos = divmod(h, 2)
            own = (lane >= pos * HEAD_DIM) & (lane < (pos + 1) * HEAD_DIM)
            base = (1 - pos) * HEAD_DIM
            f = fc[:, h:h + 1]
            hi = f.astype(bf16).astype(f32)
            mid = (f - hi).astype(bf16).astype(f32)
            lo = (f - hi) - mid
            one = jnp.ones_like(f)
            qa = jnp.where(own, q[:, pair * LANES:(pair + 1) * LANES] * ATT_SCALE, 0.0)
            ka = k[:, pair * LANES:(pair + 1) * LANES]
            for idx, (qv, kv) in enumerate([(hi, one), (mid, one), (lo, one), (one, -hi), (one, -mid), (one, -lo)]):
                sel = lane == base + idx
                qa = jnp.where(sel, qv, qa)
                ka = jnp.where(sel, kv, ka)
            qs.append(qa)
            ks.append(ka)
        return [jnp.concatenate(qs, axis=1), jnp.concatenate(ks, axis=1), vv], []
    w = N_FOX_HEADS * LANES
    return _rowwise(fn, "fox_prep", [(proj, FOX_W, C_QA // FOX_W), (proj, FOX_W, C_KA // FOX_W),
                                     (proj, FOX_W, C_VA // FOX_W), (f_col, LANES, 0)], [],
                    [(w, bf16), (w, bf16), (FOX_W, bf16)])


def _fox_fwd(q_aug, k_aug, v):
    blk = FOX_BLK
    npair = FOX_W // LANES

    def body(q_ref, k_ref, v_ref, o_ref, lse_ref, s_scr):
        i = pl.program_id(1)
        tri = lax.broadcasted_iota(jnp.int32, (blk, blk), 0) >= lax.broadcasted_iota(jnp.int32, (blk, blk), 1)
        qh = [q_ref[:, h * LANES:(h + 1) * LANES] for h in range(2)]

        def logits(c, masked):
            off = pl.multiple_of(c * blk, blk)
            tops = []
            for h in range(2):
                s = _nt(qh[h], k_ref[pl.ds(off, blk), h * LANES:(h + 1) * LANES])
                if masked:
                    s = jnp.where(tri, s, NEG)
                s_scr[h, :, pl.ds(off, blk)] = s
                tops.append(_fold_lanes(s, jnp.maximum))
            return tops

        def pass_a(c, m):
            return tuple(jnp.maximum(a, b) for a, b in zip(m, logits(c, False)))

        m = lax.fori_loop(0, i, pass_a, tuple(jnp.full((blk, LANES), NEG, f32) for _ in range(2)))
        mx = [jnp.max(jnp.maximum(a, b), axis=1, keepdims=True) for a, b in zip(m, logits(i, True))]

        def pass_b(c, carry):
            off = pl.multiple_of(c * blk, blk)
            vv = v_ref[pl.ds(off, blk), :]
            new = []
            for h in range(2):
                l, acc = carry[h]
                p = jnp.exp(s_scr[h, :, pl.ds(off, blk)] - mx[h])
                new.append((l + _fold_lanes(p, jnp.add),
                            acc + jnp.dot(p.astype(bf16), vv, preferred_element_type=f32)))
            return tuple(new)

        zero = jnp.zeros((blk, LANES), f32)
        (l_a, acc_a), (l_b, acc_b) = lax.fori_loop(0, i + 1, pass_b, ((zero, zero), (zero, zero)))
        l_a = jnp.sum(l_a, axis=1, keepdims=True)
        l_b = jnp.sum(l_b, axis=1, keepdims=True)
        first = lax.broadcasted_iota(jnp.int32, (blk, LANES), 1) < HEAD_DIM
        o_ref[...] = jnp.where(first, acc_a / l_a, acc_b / l_b)
        lse_ref[0] = jnp.where(first, mx[0] + jnp.log(l_a), mx[1] + jnp.log(l_b))

    return pl.pallas_call(
        body, grid=(npair, SEQ // blk),
        in_specs=[pl.BlockSpec((blk, 2 * LANES), lambda p, i: (i, p)),
                  pl.BlockSpec((SEQ, 2 * LANES), lambda p, i: (0, p)),
                  pl.BlockSpec((SEQ, LANES), lambda p, i: (0, p))],
        out_specs=[pl.BlockSpec((blk, LANES), lambda p, i: (i, p)),
                   pl.BlockSpec((1, blk, LANES), lambda p, i: (p, i, 0))],
        out_shape=(SDS((SEQ, FOX_W), f32), SDS((npair, SEQ, LANES), f32)), name="fox_fwd",
        scratch_shapes=[pltpu.VMEM((2, blk, SEQ), f32)],
        compiler_params=_params(("parallel", "arbitrary")),
    )(q_aug, k_aug, v)


def _fox_bwd(q_aug, k_aug, v, do, o, lse):
    blk = FOX_BLK
    npair = FOX_W // LANES
    nblk = SEQ // blk

    def body(q_ref, k_ref, v_ref, do_ref, o_ref, lse_ref, dq_ref, dk_ref, dv_ref, df_ref,
             dq_acc, delta_ref, res_ref):
        lane_s = lax.broadcasted_iota(jnp.int32, (SEQ, LANES), 1)
        prod = do_ref[...] * o_ref[...]
        d_a = jnp.sum(jnp.where(lane_s < HEAD_DIM, prod, 0.0), axis=1, keepdims=True)
        d_b = jnp.sum(jnp.where(lane_s >= HEAD_DIM, prod, 0.0), axis=1, keepdims=True)
        delta_ref[...] = jnp.where(lane_s < HEAD_DIM, d_a, d_b)
        dq_acc[...] = jnp.zeros_like(dq_acc)
        res_ref[...] = jnp.zeros_like(res_ref)
        df_ref[...] = jnp.zeros_like(df_ref)
        lane = lax.broadcasted_iota(jnp.int32, (blk, LANES), 1)
        own = [lane < HEAD_DIM, lane >= HEAD_DIM]
        tri = lax.broadcasted_iota(jnp.int32, (blk, blk), 0) >= lax.broadcasted_iota(jnp.int32, (blk, blk), 1)

        def q_slab(qoff, h):
            return q_ref[pl.ds(qoff, blk), h * LANES:(h + 1) * LANES]

        def probs(qoff, h, k_h, masked):
            s = _nt(q_slab(qoff, h), k_h)
            if masked:
                s = jnp.where(tri, s, NEG)
            return jnp.exp(s - lse_ref[0, pl.ds(qoff, blk), h * HEAD_DIM:h * HEAD_DIM + 1])

        def k_slabs(koff):
            return [k_ref[pl.ds(koff, blk), h * LANES:(h + 1) * LANES] for h in range(2)]

        def kv_step(kj, _):
            koff = pl.multiple_of(kj * blk, blk)
            k_aug = k_slabs(koff)
            k_own = [jnp.where(own[h], k_aug[h], jnp.zeros_like(k_aug[h])) for h in range(2)]
            vv = v_ref[pl.ds(koff, blk), :]
            v_own = [jnp.where(own[h], vv, jnp.zeros_like(vv)) for h in range(2)]

            def q_tile(qi, carry, masked):
                qoff = pl.multiple_of(qi * blk, blk)
                dd = do_ref[pl.ds(qoff, blk), :].astype(bf16)
                new, dq_add = [], None
                for h in range(2):
                    dk_h, dv_h, dcol = carry[h]
                    p = probs(qoff, h, k_aug[h], masked)
                    dl = p * (_nt(dd, v_own[h]) - delta_ref[pl.ds(qoff, blk), h * HEAD_DIM:h * HEAD_DIM + 1])
                    dlb = dl.astype(bf16)
                    part = jnp.dot(dlb, k_own[h], preferred_element_type=f32)
                    dq_add = part if dq_add is None else dq_add + part
                    res_ref[h, pl.ds(qoff, blk), :] += _fold_lanes(dl, jnp.add)
                    new.append((dk_h + _tn(dlb, q_slab(qoff, h)), dv_h + _tn(p.astype(bf16), dd),
                                dcol + _colsum(dl)))
                dq_acc[pl.ds(qoff, blk), :] += dq_add * ATT_SCALE
                return tuple(new)

            zero = (jnp.zeros((blk, LANES), f32), jnp.zeros((blk, LANES), f32), jnp.zeros((1, blk), f32))
            carry = q_tile(kj, (zero, zero), True)
            (dk_a, dv_a, dcol_a), (dk_b, dv_b, dcol_b) = lax.fori_loop(
                kj + 1, nblk, lambda qi, cr: q_tile(qi, cr, False), carry)
            dk_ref[pl.ds(koff, blk), :] = jnp.where(own[0], dk_a, dk_b).astype(dk_ref.dtype)
            dv_ref[pl.ds(koff, blk), :] = jnp.where(own[0], dv_a, dv_b).astype(dv_ref.dtype)
            df_ref[0, 0:1, pl.ds(koff, blk)] = -dcol_a
            df_ref[0, 1:2, pl.ds(koff, blk)] = -dcol_b
            return 0

        lax.fori_loop(0, nblk, kv_step, 0)
        dq_ref[...] = dq_acc[...].astype(dq_ref.dtype)

        for h in range(2):
            res_ref[h] = jnp.zeros((SEQ, LANES), f32) + jnp.sum(res_ref[h], axis=1, keepdims=True)

        def kv_fix(kj, _):
            koff = pl.multiple_of(kj * blk, blk)
            k_aug = k_slabs(koff)

            def q_fix(qi, corr, masked):
                qoff = pl.multiple_of(qi * blk, blk)
                return tuple(corr[h] + _colsum(probs(qoff, h, k_aug[h], masked) * res_ref[h, pl.ds(qoff, blk), 0:1])
                             for h in range(2))

            zero = jnp.zeros((1, blk), f32)
            corr = lax.fori_loop(kj + 1, nblk, lambda qi, cr: q_fix(qi, cr, False), q_fix(kj, (zero, zero), True))
            df_ref[0, 0:1, pl.ds(koff, blk)] += corr[0]
            df_ref[0, 1:2, pl.ds(koff, blk)] += corr[1]
            return 0

        lax.fori_loop(0, nblk, kv_fix, 0)

    pair_aug = pl.BlockSpec((SEQ, 2 * LANES), lambda p: (0, p))
    slab = pl.BlockSpec((SEQ, LANES), lambda p: (0, p))
    per_pair = pl.BlockSpec((1, SEQ, LANES), lambda p: (p, 0, 0))
    rows = pl.BlockSpec((1, 8, SEQ), lambda p: (p, 0, 0))
    return pl.pallas_call(
        body, grid=(npair,),
        in_specs=[pair_aug, pair_aug, slab, slab, slab, per_pair],
        out_specs=[slab, slab, slab, rows],
        out_shape=(SDS((SEQ, FOX_W), bf16),) * 3 + (SDS((npair, 8, SEQ), f32),), name="fox_bwd",
        scratch_shapes=[pltpu.VMEM((SEQ, LANES), f32), pltpu.VMEM((SEQ, LANES), f32),
                        pltpu.VMEM((2, SEQ, LANES), f32)],
        compiler_params=_params(("parallel",)),
    )(q_aug, k_aug, v, do, o, lse)


DIL_BLK = 128
DILATIONS = (1, 4, 16)
N_GROUPS = len(DILATIONS)
DIL_PAIRS = DIL_OUT_W // LANES


def _dil_blocks(d):
    r1 = lax.broadcasted_iota(jnp.int32, (DIL_BLK, DIL_BLK), 0)
    c1 = lax.broadcasted_iota(jnp.int32, (DIL_BLK, DIL_BLK), 1)
    r2 = lax.broadcasted_iota(jnp.int32, (DIL_BLK, 2 * DIL_BLK), 0)
    c2 = lax.broadcasted_iota(jnp.int32, (DIL_BLK, 2 * DIL_BLK), 1)
    band = ((c2 < DIL_BLK) & (c2 >= r2)) | ((c2 >= DIL_BLK) & (c2 - DIL_BLK <= r2))
    out = []
    for r in range(d):
        for b in range(SEQ // d // DIL_BLK):
            rows = pl.ds(r + d * DIL_BLK * b, DIL_BLK, stride=d)
            if b == 0:
                out.append((rows, rows, r1 >= c1))
            else:
                out.append((rows, pl.ds(r + d * DIL_BLK * (b - 1), 2 * DIL_BLK, stride=d), band))
    return out


def _dil_fwd(q, k, v, g):
    def body(q_ref, k_ref, v_ref, o_ref, lse_ref):
        first = lax.broadcasted_iota(jnp.int32, (DIL_BLK, LANES), 1) < HEAD_DIM
        for rows, krows, mask in _dil_blocks(DILATIONS[g]):
            qv, kk, vv = q_ref[rows, :].astype(bf16), k_ref[krows, :].astype(bf16), v_ref[krows, :].astype(bf16)
            outs, lses = [], []
            for own in (first, ~first):
                s = jnp.where(mask, _nt(jnp.where(own, qv, jnp.zeros_like(qv)), kk), NEG)
                m = jnp.max(s, axis=1, keepdims=True)
                p = jnp.exp(s - m)
                l = jnp.sum(p, axis=1, keepdims=True)
                outs.append(jnp.dot(p.astype(bf16), vv, preferred_element_type=f32) / l)
                lses.append(m + jnp.log(l))
            o_ref[rows, :] = jnp.where(first, outs[0], outs[1])
            lse_ref[rows, :] = jnp.where(first, lses[0], lses[1])

    grouped = pl.BlockSpec((SEQ, LANES), lambda p: (0, DIL_PAIRS * g + p))
    own = pl.BlockSpec((SEQ, LANES), lambda p: (0, p))
    shape = SDS((SEQ, DIL_OUT_W), f32)
    return pl.pallas_call(
        body, grid=(DIL_PAIRS,), in_specs=[grouped] * 3, out_specs=[own] * 2, out_shape=(shape, shape),
        name=f"dil_fwd_{DILATIONS[g]}", compiler_params=_params(("parallel",)),
    )(q, k, v)


def _dil_bwd(q, k, v, do, lse, delta, g):
    def body(q_ref, k_ref, v_ref, do_ref, lse_ref, dl_ref, dq_ref, dk_ref, dv_ref):
        first = lax.broadcasted_iota(jnp.int32, (DIL_BLK, LANES), 1) < HEAD_DIM
        dk_ref[...] = jnp.zeros_like(dk_ref)
        dv_ref[...] = jnp.zeros_like(dv_ref)
        for rows, krows, mask in _dil_blocks(DILATIONS[g]):
            qv, kk, vv = q_ref[rows, :].astype(bf16), k_ref[krows, :].astype(bf16), v_ref[krows, :].astype(bf16)
            dov = do_ref[rows, :].astype(bf16)
            lsev, delv = lse_ref[rows, :], dl_ref[rows, :]
            dqs, dk_add, dv_add = [], None, None
            for h, own in enumerate((first, ~first)):
                col = h * HEAD_DIM
                qh = jnp.where(own, qv, jnp.zeros_like(qv))
                doh = jnp.where(own, dov, jnp.zeros_like(dov))
                p = jnp.exp(jnp.where(mask, _nt(qh, kk), NEG) - lsev[:, col:col + 1])
                dl = (p * (_nt(doh, vv) - delv[:, col:col + 1])).astype(bf16)
                dqs.append(jnp.dot(dl, kk, preferred_element_type=f32))
                dk_h, dv_h = _tn(dl, qh), _tn(p.astype(bf16), doh)
                dk_add = dk_h if dk_add is None else dk_add + dk_h
                dv_add = dv_h if dv_add is None else dv_add + dv_h
            dq_ref[rows, :] = jnp.where(first, dqs[0], dqs[1]) * ATT_SCALE
            dk_ref[krows, :] += dk_add
            dv_ref[krows, :] += dv_add

    grouped = pl.BlockSpec((SEQ, LANES), lambda p: (0, DIL_PAIRS * g + p))
    own = pl.BlockSpec((SEQ, LANES), lambda p: (0, p))
    shape = SDS((SEQ, DIL_OUT_W), f32)
    return pl.pallas_call(
        body, grid=(DIL_PAIRS,), in_specs=[grouped] * 3 + [own] * 3, out_specs=[own] * 3,
        out_shape=(shape, shape, shape), name=f"dil_bwd_{DILATIONS[g]}", compiler_params=_params(("parallel",)),
    )(q, k, v, do, lse, delta)


def _position():
    return lax.axis_index("x"), lax.axis_index("y"), lax.axis_index("c")


def _all_gather(block, name):
    def body(x_ref, out_ref, send_sems, recv_sems, local_sem):
        x, y, c = _position()
        me, sibling = (x, y, c), (x, y, 1 - c)
        chips = [(1 - x, y), (x, 1 - y), (1 - x, 1 - y)]

        def slot(px, py, pc):
            return out_ref.at[4 * px + 2 * py + pc]

        def copy(k, blk, to, src=None):
            return pltpu.make_async_remote_copy(
                src_ref=slot(*blk) if src is None else src, dst_ref=slot(*blk),
                send_sem=send_sems.at[k], recv_sem=recv_sems.at[k], device_id=to, device_id_type=MESH)

        mine = pltpu.make_async_copy(x_ref, slot(*me), local_sem)
        mine.start()
        first = [copy(0, me, sibling, src=x_ref)]
        first += [copy(1 + j, me, (*chip, c), src=x_ref) for j, chip in enumerate(chips)]
        for cp in first:
            cp.start()
        passed = [copy(4 + j, (*chip, c), sibling) for j, chip in enumerate(chips)]
        for j, chip in enumerate(chips):
            copy(1 + j, (*chip, c), me).wait_recv()
            passed[j].start()
        copy(0, sibling, me).wait_recv()
        for j, chip in enumerate(chips):
            copy(4 + j, (*chip, 1 - c), me).wait_recv()
        for cp in first + passed:
            cp.wait_send()
        mine.wait()

    return pl.pallas_call(
        body, out_shape=SDS((N_DEV,) + block.shape, block.dtype),
        in_specs=[pl.BlockSpec(memory_space=pl.ANY)], out_specs=pl.BlockSpec(memory_space=pl.ANY),
        scratch_shapes=[pltpu.SemaphoreType.DMA((7,)), pltpu.SemaphoreType.DMA((7,)), pltpu.SemaphoreType.DMA],
        name=name,
    )(block)


def _all_gather_many(blocks, name):
    n = len(blocks)

    def body(*refs):
        x_refs, out_refs = refs[:n], refs[n:2 * n]
        send_sems, recv_sems, local_sems = refs[2 * n:]
        x, y, c = _position()
        me, sibling = (x, y, c), (x, y, 1 - c)
        chips = [(1 - x, y), (x, 1 - y), (1 - x, 1 - y)]

        def slot(a, px, py, pc):
            return out_refs[a].at[4 * px + 2 * py + pc]

        def copy(a, k, blk, to, own=False):
            return pltpu.make_async_remote_copy(
                src_ref=x_refs[a] if own else slot(a, *blk), dst_ref=slot(a, *blk),
                send_sem=send_sems.at[a, k], recv_sem=recv_sems.at[a, k], device_id=to, device_id_type=MESH)

        mine = [pltpu.make_async_copy(x_refs[a], slot(a, *me), local_sems.at[a]) for a in range(n)]
        for cp in mine:
            cp.start()
        started = []
        for a in range(n):
            first = [copy(a, 0, me, sibling, own=True)]
            first += [copy(a, 1 + j, me, (*chip, c), own=True) for j, chip in enumerate(chips)]
            for cp in first:
                cp.start()
            started += first
        for a in range(n):
            for j, chip in enumerate(chips):
                copy(a, 1 + j, (*chip, c), me).wait_recv()
                passed = copy(a, 4 + j, (*chip, c), sibling)
                passed.start()
                started.append(passed)
        for a in range(n):
            copy(a, 0, sibling, me).wait_recv()
            for j, chip in enumerate(chips):
                copy(a, 4 + j, (*chip, 1 - c), me).wait_recv()
        for cp in started:
            cp.wait_send()
        for cp in mine:
            cp.wait()

    hbm = pl.BlockSpec(memory_space=pl.ANY)
    return pl.pallas_call(
        body, out_shape=[SDS((N_DEV,) + b.shape, b.dtype) for b in blocks],
        in_specs=[hbm] * n, out_specs=[hbm] * n,
        scratch_shapes=[pltpu.SemaphoreType.DMA((n, 7)), pltpu.SemaphoreType.DMA((n, 7)),
                        pltpu.SemaphoreType.DMA((n,))],
        name=name,
    )(*blocks)


HBM_SPEC = pl.BlockSpec(memory_space=pltpu.HBM)
SEM_SPEC = pl.BlockSpec(memory_space=pltpu.SEMAPHORE)
SPLIT_COPY = pltpu.CompilerParams(has_side_effects=pltpu.SideEffectType.DATAFLOW_SIDE_EFFECTING)


def _in_hbm(t):
    return pltpu.with_memory_space_constraint(t, pltpu.HBM)


def _pair_copies(g_refs, land_refs, send_sems, recv_sems):
    x, y, c = _position()
    return [pltpu.make_async_remote_copy(
        src_ref=g.at[2 * k + (1 - c)], dst_ref=land.at[k], send_sem=send_sems.at[4 * a + k],
        recv_sem=recv_sems.at[4 * a + k], device_id=(x, y, 1 - c), device_id_type=MESH)
        for a, (g, land) in enumerate(zip(g_refs, land_refs, strict=True)) for k in range(4)]


def _chip_copies(t_refs, land_refs, send_sems, recv_sems):
    x, y, c = _position()
    chips = [(1 - x, y), (x, 1 - y), (1 - x, 1 - y)]
    return [pltpu.make_async_remote_copy(
        src_ref=t.at[2 * px + py], dst_ref=land.at[j], send_sem=send_sems.at[3 * a + j],
        recv_sem=recv_sems.at[3 * a + j], device_id=(px, py, c), device_id_type=MESH)
        for a, (t, land) in enumerate(zip(t_refs, land_refs, strict=True)) for j, (px, py) in enumerate(chips)]


_ROUNDS = {"pair": (_pair_copies, 4), "chip": (_chip_copies, 3)}


def _exchange_start(kind, ts, name):
    copies, slots = _ROUNDS[kind]
    n = len(ts)
    lands = [_in_hbm(lax.empty((slots,) + t.shape[1:], t.dtype)) for t in ts]

    def body(*refs):
        for cp in copies(refs[:n], refs[n:2 * n], refs[2 * n], refs[2 * n + 1]):
            cp.start()
        refs[-1][...] = jnp.zeros_like(refs[-1])

    sems = pltpu.SemaphoreType.DMA((slots * n,))
    res = pl.pallas_call(
        body, name=name, in_specs=[HBM_SPEC] * (2 * n),
        out_shape=(sems, sems, *[pltpu.HBM(t.shape, t.dtype) for t in (*ts, *lands)], SDS((8, LANES), f32)),
        out_specs=(SEM_SPEC, SEM_SPEC, *[HBM_SPEC] * (2 * n), pl.BlockSpec(memory_space=pltpu.VMEM)),
        input_output_aliases={i: 2 + i for i in range(2 * n)}, compiler_params=SPLIT_COPY,
    )(*[_in_hbm(t) for t in ts], *lands)
    return res[:-1], res[-1]


def _exchange_wait(kind, state, after, name):
    copies, _ = _ROUNDS[kind]
    send_sems, recv_sems, *arrays = state
    n = len(arrays) // 2

    def body(*refs):
        for cp in copies(refs[:n], refs[n:2 * n], refs[2 * n], refs[2 * n + 1]):
            cp.wait_send()
            cp.wait_recv()

    res = pl.pallas_call(
        body, name=name, in_specs=[HBM_SPEC] * (2 * n) + [SEM_SPEC, SEM_SPEC, pl.BlockSpec(memory_space=pl.ANY)],
        out_shape=[pltpu.HBM(t.shape, t.dtype) for t in arrays], out_specs=[HBM_SPEC] * (2 * n),
        input_output_aliases={i: i for i in range(2 * n)}, compiler_params=SPLIT_COPY,
    )(*arrays, send_sems, recv_sems, after)
    return res[:n], res[n:]


def _gather_copies(x_refs, out_refs, send_sems, recv_sems):
    x, y, c = _position()
    peers = [(x, y, 1 - c), (1 - x, y, c), (x, 1 - y, c), (1 - x, 1 - y, c)]
    sends, arrivals = [], []
    for a, (x_ref, out_ref) in enumerate(zip(x_refs, out_refs, strict=True)):
        for k, (px, py, pc) in enumerate(peers):
            sems = dict(send_sem=send_sems.at[4 * a + k], recv_sem=recv_sems.at[4 * a + k],
                        device_id=(px, py, pc), device_id_type=MESH)
            sends.append(pltpu.make_async_remote_copy(src_ref=x_ref, dst_ref=out_ref.at[4 * x + 2 * y + c], **sems))
            arrivals.append(pltpu.make_async_remote_copy(src_ref=x_ref, dst_ref=out_ref.at[4 * px + 2 * py + pc],
                                                         **sems))
    return sends, arrivals


def _gather_start(blocks, after, name):
    n = len(blocks)
    outs = [_in_hbm(lax.empty((N_DEV,) + b.shape, b.dtype)) for b in blocks]

    def body(*refs):
        sends, _ = _gather_copies(refs[:n], refs[n:2 * n], refs[2 * n + 1], refs[2 * n + 2])
        for cp in sends:
            cp.start()
        refs[-1][...] = jnp.zeros_like(refs[-1])

    sems = pltpu.SemaphoreType.DMA((4 * n,))
    res = pl.pallas_call(
        body, name=name, in_specs=[HBM_SPEC] * (2 * n) + [pl.BlockSpec(memory_space=pl.ANY)],
        out_shape=(sems, sems, *[pltpu.HBM(t.shape, t.dtype) for t in (*blocks, *outs)], SDS((8, LANES), f32)),
        out_specs=(SEM_SPEC, SEM_SPEC, *[HBM_SPEC] * (2 * n), pl.BlockSpec(memory_space=pltpu.VMEM)),
        input_output_aliases={i: 2 + i for i in range(2 * n)}, compiler_params=SPLIT_COPY,
    )(*[_in_hbm(b) for b in blocks], *outs, after)
    return res[:-1], res[-1]


def _gather_wait(state, after, name):
    send_sems, recv_sems, *arrays = state
    n = len(arrays) // 2

    def body(*refs):
        sends, arrivals = _gather_copies(refs[:n], refs[n:2 * n], refs[2 * n], refs[2 * n + 1])
        for cp in sends:
            cp.wait_send()
        for cp in arrivals:
            cp.wait_recv()

    res = pl.pallas_call(
        body, name=name, in_specs=[HBM_SPEC] * (2 * n) + [SEM_SPEC, SEM_SPEC, pl.BlockSpec(memory_space=pl.ANY)],
        out_shape=[pltpu.HBM(t.shape, t.dtype) for t in arrays], out_specs=[HBM_SPEC] * (2 * n),
        input_output_aliases={i: i for i in range(2 * n)}, compiler_params=SPLIT_COPY,
    )(*arrays, send_sems, recv_sems, after)
    return res[:n], res[n:]


def _gather_finish(partial, name):
    n = len(partial)

    def body(*refs):
        in_refs, out_refs = refs[:n], refs[n:2 * n]
        send_sems, recv_sems = refs[2 * n:]
        x, y, c = _position()
        chips = [(1 - x, y), (x, 1 - y), (1 - x, 1 - y)]
        copies = []
        for a in range(n):
            for j, (px, py) in enumerate(chips):
                cp = pltpu.make_async_remote_copy(
                    src_ref=in_refs[a].at[4 * px + 2 * py + c], dst_ref=out_refs[a].at[4 * px + 2 * py + c],
                    send_sem=send_sems.at[a, j], recv_sem=recv_sems.at[a, j], device_id=(x, y, 1 - c),
                    device_id_type=MESH)
                cp.start()
                copies.append(cp)
        for a in range(n):
            for j, (px, py) in enumerate(chips):
                pltpu.make_async_remote_copy(
                    src_ref=in_refs[a].at[4 * px + 2 * py + (1 - c)], dst_ref=out_refs[a].at[4 * px + 2 * py + (1 - c)],
                    send_sem=send_sems.at[a, j], recv_sem=recv_sems.at[a, j], device_id=(x, y, 1 - c),
                    device_id_type=MESH).wait_recv()
        for cp in copies:
            cp.wait_send()

    hbm = pl.BlockSpec(memory_space=pl.ANY)
    return pl.pallas_call(
        body, out_shape=[SDS(p.shape, p.dtype) for p in partial], in_specs=[hbm] * n, out_specs=[hbm] * n,
        input_output_aliases={a: a for a in range(n)},
        scratch_shapes=[pltpu.SemaphoreType.DMA((n, 3)), pltpu.SemaphoreType.DMA((n, 3))],
        name=name,
    )(*partial)


def _row_tile(rows):
    return 512 if rows % 512 == 0 and rows > 512 else rows


def _pair_add(g, r1, core, name):
    def body(c_ref, g_ref, r_ref, o_ref):
        o_ref[...] = (g_ref[...].astype(f32) + r_ref[...].astype(f32)).astype(o_ref.dtype)

    rows, cols = g.shape[1:]
    tile = _row_tile(rows)
    blk = (1, tile, cols)
    return pl.pallas_call(
        body, out_shape=SDS((4, rows, cols), g.dtype), name=name,
        grid_spec=pltpu.PrefetchScalarGridSpec(
            num_scalar_prefetch=1, grid=(4, rows // tile),
            in_specs=[pl.BlockSpec(blk, lambda k, i, c_ref: (2 * k + c_ref[0], i, 0)),
                      pl.BlockSpec(blk, lambda k, i, c_ref: (k, i, 0))],
            out_specs=pl.BlockSpec(blk, lambda k, i, c_ref: (k, i, 0))),
        compiler_params=_params(("parallel", "arbitrary")),
    )(core, g, r1)


def _chip_add(t, r2, chip, name):
    def body(c_ref, t_ref, r_ref, o_ref):
        o_ref[...] = ((t_ref[0].astype(f32) + r_ref[0].astype(f32)) + r_ref[1].astype(f32)) + r_ref[2].astype(f32)

    rows, cols = t.shape[1:]
    tile = _row_tile(rows)
    return pl.pallas_call(
        body, out_shape=SDS((rows, cols), f32), name=name,
        grid_spec=pltpu.PrefetchScalarGridSpec(
            num_scalar_prefetch=1, grid=(rows // tile,),
            in_specs=[pl.BlockSpec((1, tile, cols), lambda i, c_ref: (c_ref[0], i, 0)),
                      pl.BlockSpec((3, tile, cols), lambda i, c_ref: (0, i, 0))],
            out_specs=pl.BlockSpec((tile, cols), lambda i, c_ref: (i, 0))),
        compiler_params=_params(("arbitrary",)),
    )(chip, t, r2)


def _pad_to(t, axis, size):
    pads = [(0, 0)] * t.ndim
    pads[axis] = (0, size - t.shape[axis])
    return jnp.pad(t, pads)


_REF_COLS = {"qa": (0, FOX_W), "ka": (FOX_W, FOX_W), "va": (2 * FOX_W, FOX_W), "f": (3 * FOX_W, N_FOX_HEADS)}
_REF_COLS.update({n: (3 * FOX_W + N_FOX_HEADS + i * DIL_W, DIL_W) for i, n in enumerate(("qb", "kb", "vb"))})
_REF_COLS.update({n: (3 * FOX_W + N_FOX_HEADS + 3 * DIL_W + i * D, D) for i, n in enumerate(("ga", "gb"))})
_REF_ORDER = ("qa", "ka", "va", "f", "qb", "kb", "vb", "ga", "gb")


def _shard_pad_cols(pieces):
    first = pieces[_REF_ORDER[0]]
    pad = jnp.zeros((first.shape[0], W_IN_PAD - W_IN_SH), first.dtype)
    parts, names, used = [], list(_REF_ORDER), 0
    for _ in range(N_DEV):
        need = W_IN_SH
        while need:
            take = min(need, _REF_COLS[names[0]][1] - used)
            parts.append(pieces[names[0]][:, used:used + take])
            need, used = need - take, used + take
            if used == _REF_COLS[names[0]][1]:
                names, used = names[1:], 0
        parts.append(pad)
    return jnp.concatenate(parts, axis=1)


def _slab_w_in(stack):
    def cols(name):
        lo, width = _REF_COLS[name]
        hi, out = lo + width, []
        while lo < hi:
            j, off = divmod(lo, W_IN_SH)
            n = min(hi - lo, W_IN_SH - off)
            out.append(stack[j, :, off:off + n])
            lo += n
        return out
    z = lambda n: [jnp.zeros((stack.shape[1], n), stack.dtype)]
    parts = (cols("ga") + cols("gb") + z(C_QB - 2 * D) + cols("qb") + cols("kb") + cols("vb") + cols("qa")
             + cols("ka") + cols("va") + cols("f") + z(LANES - N_FOX_HEADS))
    return jnp.concatenate(parts, axis=1)


def kernel(x, c, w_ada, b_ada, g_mix, w_in, b_fgate, w_br_a, w_br_b, w_out, g_ffn, w_ffn_gate, w_ffn_up, w_ffn_down, g_final, loss_target, m_w_ada, m_b_ada, m_g_mix, m_w_in, m_b_fgate, m_w_br_a, m_w_br_b, m_w_out, m_g_ffn, m_w_ffn_gate, m_w_ffn_up, m_w_ffn_down, m_g_final, v_w_ada, v_b_ada, v_g_mix, v_w_in, v_b_fgate, v_w_br_a, v_w_br_b, v_w_out, v_g_ffn, v_w_ffn_gate, v_w_ffn_up, v_w_ffn_down, v_g_final):
    px, py, pc = _position()
    dev = 4 * px + 2 * py + pc
    x2d, tgt = x[0], loss_target[0]

    c_all = _all_gather(c, "gather_c").reshape(N_DEV, D)
    ada_cols = w_ada.shape[2]
    b_shard = lax.dynamic_slice(b_ada, (0, dev * ada_cols), (1, ada_cols))
    mod_shard = _ada_fwd(c_all, w_ada[0], b_shard)
    mod_all = _all_gather(mod_shard, "gather_mod")
    modv = lax.dynamic_index_in_dim(mod_all, dev, axis=1, keepdims=False).reshape(6, D)

    gate_up = jnp.concatenate([_pad_to(w_ffn_gate[0], 1, FF_PAD), _pad_to(w_ffn_up[0], 1, FF_PAD)], axis=1)
    w_in_s, = _all_gather_many([_pad_to(w_in[0], 1, W_IN_PAD).astype(bf16)], "gather_w_in")
    later = [w_br_a[0], w_br_b[0], w_out[0], gate_up, _pad_to(w_ffn_down[0], 0, FF_PAD)]
    later_state, later_token = _gather_start([t.astype(bf16) for t in later], w_in_s, "gather_rest_start")
    w_in_p = _slab_w_in(w_in_s)

    h1 = _pre1(x2d, modv, g_mix)
    proj = _matmul(h1, w_in_p, name="mm_proj", tm=SEQ, tn=896, tk=D, after=later_token)
    b_pad = jnp.pad(b_fgate, ((0, 0), (0, LANES - N_FOX_HEADS)))
    q_aug, k_aug, va = _fox_prep(proj, _fox_gate_fwd(proj, b_pad))
    ya_h, lse_a = _fox_fwd(q_aug, k_aug, va)

    tables = _rope_tables()
    qb_r, kb_r, vb = _rope_fwd(proj, tables)
    by_group = [_dil_fwd(qb_r, kb_r, vb, grp) for grp in range(N_GROUPS)]
    yb_h, lse_b = _dil_combine([o for o, _ in by_group], [l for _, l in by_group])

    mine, arrived = _gather_wait(later_state, yb_h, "gather_rest_wait")
    w_a_s, w_b_s, w_o_s, w_gu_s, w_d_s = [
        lax.dynamic_update_slice(stack, block[None], (dev, 0, 0))
        for stack, block in zip(_gather_finish(arrived, "gather_rest_finish"), mine, strict=True)]
    w_o = w_o_s.reshape(D, D)
    w_d = w_d_s.reshape(FF_HID, D)
    ya = _matmul_stack(ya_h, w_a_s, name="mm_br_a")
    yb = _matmul_stack(yb_h, w_b_s, name="mm_br_b")

    merged = _merge_fwd(ya, yb, proj)
    mix = _matmul(merged, w_o, name="mm_out", tm=SEQ, tn=512, tk=D)
    x1, h2 = _post1(x2d, mix, modv, g_ffn)
    act, au = _ffn_in(h2, w_gu_s)
    ff = _matmul(act, w_d, name="mm_ffn_down", tm=SEQ, tn=512, tk=FF_HID // 2)

    dx2, dff, dg_final, dga_f, loss_lanes = _final(x1, ff, tgt, modv, g_final.reshape(1, D))
    dau = _ffn_bwd_in(dff, w_d_s, au)

    core = pc.astype(jnp.int32).reshape(1)
    chip = (2 * px + py).astype(jnp.int32).reshape(1)

    def pair_done(state, after, tags, name):
        mine, theirs = _exchange_wait("pair", state, after, "pair_wait_" + name)
        sums = [_pair_add(g, r, core, "pair_add_" + t) for g, r, t in zip(mine, theirs, tags)]
        return _exchange_start("chip", sums, "chip_start_" + name)

    def from_chips(state, after, tags, name):
        sums, got = _exchange_wait("chip", state, after, "chip_wait_" + name)
        return [_chip_add(p, r, chip, "chip_add_" + t) for p, r, t in zip(sums, got, tags)]

    g_gu = _matmul(h2, dau, ta=True, by_shard=True, out_dtype=bf16, name="mm_g_ffn_in", tm=D, tn=2 * FF_PAD, tk=SEQ)
    g_d = _matmul(act, dff, ta=True, out_dtype=bf16, name="mm_g_down", tm=FF_HID // 2, tn=512, tk=SEQ)
    ffn_tags = ["gu", "down"]
    ffn_pair, ffn_pair_token = _exchange_start("pair", [g_gu, g_d.reshape(N_DEV, FF_PAD, D)], "pair_start_ffn")

    dh2 = _matmul(dau, w_gu_s, tb=True, by_shard=True, name="mm_d_h2", tm=SEQ, tn=D, tk=2 * FF_PAD,
                  after=ffn_pair_token)
    ffn_state, ffn_token = pair_done(ffn_pair, dh2, ffn_tags, "ffn")
    dx1, dmix, dsh_f, dsc_f, dg_ffn, dga_m = _mid_bwd(dh2, x1, dx2, mix, modv, g_ffn)
    dmerged = _matmul(dmix, w_o, tb=True, name="mm_d_merged", tm=SEQ, tn=512, tk=D, after=ffn_token)
    dya, dyb, dga, dgb = _merge_bwd(dmerged, ya, yb, proj)
    dya_h = _matmul_stack(dya, w_a_s, tb=True, name="mm_d_ya")
    dyb_h = _matmul_stack(dyb, w_b_s, tb=True, name="mm_d_yb")

    dqa, dka, dva, dF = _fox_bwd(q_aug, k_aug, va, dya_h, ya_h, lse_a)
    dF_row = jnp.pad(dF[:, :2, :].reshape(N_FOX_HEADS, SEQ), ((0, LANES - N_FOX_HEADS), (0, 0)))
    df, db_fgate = _fox_gate_bwd(dF_row, proj, b_pad)

    delta_b = _dil_delta(dyb_h, yb_h)
    dil_grads = [_dil_bwd(qb_r, kb_r, vb, dyb_h, lse_b, delta_b, grp) for grp in range(N_GROUPS)]
    dqb, dkb = _rope_bwd([t[0] for t in dil_grads], [t[1] for t in dil_grads], tables)
    dvb = jnp.concatenate([t[2] for t in dil_grads], axis=1).astype(bf16)

    dproj = _shard_pad_cols({"qa": dqa, "ka": dka, "va": dva, "f": df[:, :N_FOX_HEADS], "qb": dqb, "kb": dkb,
                             "vb": dvb, "ga": dga, "gb": dgb})
    g_in = _matmul(h1, dproj, ta=True, by_shard=True, out_dtype=bf16, name="mm_g_in", tm=D, tn=W_IN_PAD, tk=SEQ)
    g_o = _matmul(merged, dmix, ta=True, out_dtype=bf16, name="mm_g_out", tm=D, tn=512, tk=SEQ)
    g_a = _matmul_stack(ya_h, dya, ta=True, out_dtype=bf16, name="mm_g_br_a")
    g_b = _matmul_stack(yb_h, dyb, ta=True, out_dtype=bf16, name="mm_g_br_b")
    rows_a, rows_b = FOX_W * W_BR_SH // D, DIL_OUT_W * W_BR_SH // D
    g_small = jnp.concatenate([g_a.reshape(N_DEV, rows_a, D), g_b.reshape(N_DEV, rows_b, D),
                               g_o.reshape(N_DEV, W_BR_SH, D)], axis=1)
    mix_tags = ["in", "small"]
    mix_pair, mix_pair_token = _exchange_start("pair", [g_in, g_small], "pair_start_mixer")

    dh1 = _matmul(dproj, w_in_s, tb=True, by_shard=True, name="mm_d_h1", tm=SEQ, tn=D, tk=W_IN_PAD,
                  after=mix_pair_token)
    grad_x, dsh_m, dsc_m, dg_mix = _first_bwd(dh1, x2d, dx1, modv, g_mix)

    pad_lane = lambda t: jnp.pad(t, ((0, 0), (0, D - t.shape[1])))
    small = jnp.concatenate([dsh_m, dsc_m, dga_m, dsh_f, dsc_f, dga_f, dg_mix, dg_ffn, dg_final,
                             pad_lane(db_fgate), loss_lanes, jnp.zeros((SMALL_ROWS - 11, D), f32)], axis=0)
    small_all = _all_gather(small, "gather_small")
    mix_state, mix_token = pair_done(mix_pair, small_all, mix_tags, "mixer")

    small_sum, loss_row = _small_reduce(small_all, mix_token)
    dmod_all = small_all[:, :6, :].reshape(N_DEV, 6 * D)
    g_w_ada = _ada_bwd(c_all, lax.dynamic_slice(dmod_all, (0, dev * ada_cols), (N_DEV, ada_cols)))
    s_gu, s_d = from_chips(ffn_state, small_sum, ffn_tags, "ffn")

    loss = loss_row[0, 0]
    g = {
        "w_ada": g_w_ada[None], "b_ada": small_sum[0:6].reshape(1, 6 * D), "g_mix": small_sum[6:7],
        "b_fgate": small_sum[9:10, :N_FOX_HEADS], "g_ffn": small_sum[7:8], "w_ffn_gate": s_gu[None, :, :W_FF_SH],
        "w_ffn_up": s_gu[None, :, FF_PAD:FF_PAD + W_FF_SH], "w_ffn_down": s_d[None, :W_FF_SH],
        "g_final": small_sum[8],
    }
    w = {"w_ada": w_ada, "b_ada": b_ada, "g_mix": g_mix, "w_in": w_in, "b_fgate": b_fgate, "w_br_a": w_br_a,
         "w_br_b": w_br_b, "w_out": w_out, "g_ffn": g_ffn, "w_ffn_gate": w_ffn_gate, "w_ffn_up": w_ffn_up,
         "w_ffn_down": w_ffn_down, "g_final": g_final}
    m = {"w_ada": m_w_ada, "b_ada": m_b_ada, "g_mix": m_g_mix, "w_in": m_w_in, "b_fgate": m_b_fgate,
         "w_br_a": m_w_br_a, "w_br_b": m_w_br_b, "w_out": m_w_out, "g_ffn": m_g_ffn, "w_ffn_gate": m_w_ffn_gate,
         "w_ffn_up": m_w_ffn_up, "w_ffn_down": m_w_ffn_down, "g_final": m_g_final}
    v = {"w_ada": v_w_ada, "b_ada": v_b_ada, "g_mix": v_g_mix, "w_in": v_w_in, "b_fgate": v_b_fgate,
         "w_br_a": v_w_br_a, "w_br_b": v_w_br_b, "w_out": v_w_out, "g_ffn": v_g_ffn, "w_ffn_gate": v_w_ffn_gate,
         "w_ffn_up": v_w_ffn_up, "w_ffn_down": v_w_ffn_down, "g_final": v_g_final}
    names = list(w)
    delta, new_m, new_v = {}, {}, {}

    transposed = ("w_in", "w_ffn_gate", "w_ffn_up")

    def update(n):
        shape = w[n].shape
        if n in transposed:
            g_t = g[n][0].T
            dl, mn, vn = _adamw(w[n][0].T, g_t, m[n][0].T, v[n][0].T, "adamw_" + n)
            g[n], delta[n], new_m[n], new_v[n] = g_t.T[None], dl.T[None], mn.T[None], vn.T[None]
            return
        two_d = (lambda t: t.reshape(shape[-2:])) if len(shape) == 3 else (lambda t: t)
        dl, mn, vn = _adamw(two_d(w[n]), two_d(g[n]), two_d(m[n]), two_d(v[n]), "adamw_" + n)
        delta[n], new_m[n], new_v[n] = dl.reshape(shape), mn.reshape(shape), vn.reshape(shape)

    for n in list(g):
        update(n)
    done = sum(delta[n].reshape(-1)[:N_FOX_HEADS] for n in g)
    s_in, s_small = from_chips(mix_state, done, mix_tags, "mixer")
    g.update({"w_in": s_in[None, :, :W_IN_SH], "w_br_a": s_small[:rows_a].reshape(1, FOX_W, W_BR_SH),
              "w_br_b": s_small[rows_a:rows_a + rows_b].reshape(1, DIL_OUT_W, W_BR_SH),
              "w_out": s_small[None, rows_a + rows_b:]})
    for n in ("w_in", "w_br_a", "w_br_b", "w_out"):
        update(n)

    return (loss, grad_x[None], *[g[n] for n in names], *[delta[n] for n in names],
            *[new_m[n] for n in names], *[new_v[n] for n in names])
```

```python
import functools

import jax
import jax.numpy as jnp
from jax import lax
from jax.experimental import pallas as pl
from jax.experimental.pallas import tpu as pltpu

f32 = jnp.float32
bf16 = jnp.bfloat16
SDS = jax.ShapeDtypeStruct
MESH = pl.DeviceIdType.MESH

N_DEV = 8
D = 1024
SEQ = 2048
HEAD_DIM = 64
N_FOX_HEADS = 8
FOX_W = 512
DIL_W = 768
DIL_OUT_W = 256
ROT_DIM = 16
ROPE_THETA = 500000.0
D_FF = 2816
IN_COLS = 5896
EPS = 1e-6
NEG = -1e30
ATT_SCALE = HEAD_DIM ** -0.5

ADAM_LR = 0.001
ADAM_B1 = 0.9
ADAM_B2 = 0.999
ADAM_EPS = 1e-08
ADAM_WD = 0.01
ADAM_STEP = 10

C_GA, C_GB, C_QB, C_KB, C_VB, C_QA, C_KA, C_VA, C_F = 0, 1024, 2304, 3072, 3840, 4608, 5120, 5632, 6144
PROJ_W = 6272
LANES = 128
VMEM_LIMIT = 52 * 1024 * 1024

W_IN_SH, W_IN_PAD = IN_COLS // N_DEV, 768
W_BR_SH = D // N_DEV
W_FF_SH, FF_PAD = D_FF // N_DEV, 384
FF_HID = N_DEV * FF_PAD
SMALL_ROWS = 16


def _params(sem=None):
    if sem is None:
        return pltpu.CompilerParams(vmem_limit_bytes=VMEM_LIMIT)
    return pltpu.CompilerParams(dimension_semantics=sem, vmem_limit_bytes=VMEM_LIMIT)


def _rowwise(fn, name, tiled, vecs, outs, reds=(), tile=256):
    nt, nv, no = len(tiled), len(vecs), len(outs)
    rows = tiled[0][0].shape[0]
    assert rows % tile == 0

    def body(*refs):
        tin = [r[...] for r in refs[:nt]]
        vin = [r[...] for r in refs[nt:nt + nv]]
        orefs = refs[nt + nv:nt + nv + no]
        rrefs = refs[nt + nv + no:]
        touts, routs = fn(tin, vin)
        for r, t in zip(orefs, touts, strict=True):
            r[...] = t.astype(r.dtype)
        if rrefs:
            @pl.when(pl.program_id(0) == 0)
            def _():
                for r in rrefs:
                    r[...] = jnp.zeros_like(r)
            for r, t in zip(rrefs, routs, strict=True):
                r[...] += t

    def col_map(cb):
        return lambda i: (i, cb)

    def whole_map(nd):
        return lambda i: (0,) * nd

    in_specs = [pl.BlockSpec((tile, w), col_map(cb)) for (_, w, cb) in tiled]
    in_specs += [pl.BlockSpec(v.shape, whole_map(v.ndim)) for v in vecs]
    out_specs = [pl.BlockSpec((tile, w), lambda i: (i, 0)) for (w, _) in outs]
    out_specs += [pl.BlockSpec((1, w), lambda i: (0, 0)) for w in reds]
    out_shape = [SDS((rows, w), dt) for (w, dt) in outs] + [SDS((1, w), f32) for w in reds]
    res = pl.pallas_call(
        body, grid=(rows // tile,), in_specs=in_specs, out_specs=out_specs, out_shape=out_shape, name=name,
        compiler_params=_params(("arbitrary",)),
    )(*[t[0] for t in tiled], *vecs)
    return res


def _matmul(a, b, *, ta=False, tb=False, out_dtype=f32, name, tm, tn, tk, by_shard=False, after=None):
    m, k = (a.shape[1], a.shape[0]) if ta else a.shape
    if by_shard and not ta:
        n, kb = (b.shape[1], N_DEV * b.shape[2]) if tb else (N_DEV * b.shape[2], b.shape[1])
        assert (tk if tb else tn) == b.shape[2]
    else:
        n, kb = (b.shape[0], b.shape[1]) if tb else (b.shape[1], b.shape[0])
    assert kb == k and m % tm == 0 and n % tn == 0 and k % tk == 0
    nk = k // tk
    dims = (((0 if ta else 1,), (1 if tb else 0,)), ((), ()))
    b_stacked = by_shard and not ta
    o_stacked = by_shard and ta

    def body(a_ref, b_ref, *rest):
        o_ref, *acc = rest[1:] if after is not None else rest
        bv = b_ref[0] if b_stacked else b_ref[...]
        p = lax.dot_general(a_ref[...].astype(bf16), bv.astype(bf16), dims, preferred_element_type=f32)

        def put(val):
            if o_stacked:
                o_ref[0] = val.astype(o_ref.dtype)
            else:
                o_ref[...] = val.astype(o_ref.dtype)

        if nk == 1:
            put(p)
        else:
            acc_ref, = acc
            kk = pl.program_id(2)

            @pl.when(kk == 0)
            def _():
                acc_ref[...] = p

            @pl.when(kk > 0)
            def _():
                acc_ref[...] += p

            @pl.when(kk == nk - 1)
            def _():
                put(acc_ref[...])

    a_spec = pl.BlockSpec((tk, tm), lambda i, j, kk: (kk, i)) if ta else pl.BlockSpec((tm, tk), lambda i, j, kk: (i, kk))
    if b_stacked and tb:
        b_spec = pl.BlockSpec((1, tn, tk), lambda i, j, kk: (kk, j, 0))
    elif b_stacked:
        b_spec = pl.BlockSpec((1, tk, tn), lambda i, j, kk: (j, kk, 0))
    elif tb:
        b_spec = pl.BlockSpec((tn, tk), lambda i, j, kk: (j, kk))
    else:
        b_spec = pl.BlockSpec((tk, tn), lambda i, j, kk: (kk, j))
    if o_stacked:
        assert tn == n // N_DEV
        out_spec = pl.BlockSpec((1, tm, tn), lambda i, j, kk: (j, i, 0))
        out_shape = SDS((N_DEV, m, tn), out_dtype)
    else:
        out_spec = pl.BlockSpec((tm, tn), lambda i, j, kk: (i, j))
        out_shape = SDS((m, n), out_dtype)
    extra_specs, extra = ([pl.BlockSpec(memory_space=pl.ANY)], [after]) if after is not None else ([], [])
    return pl.pallas_call(
        body, grid=(m // tm, n // tn, nk), in_specs=[a_spec, b_spec] + extra_specs, out_specs=out_spec,
        out_shape=out_shape, name=name, scratch_shapes=[pltpu.VMEM((tm, tn), f32)] if nk > 1 else [],
        compiler_params=_params(("parallel", "parallel", "arbitrary")),
    )(a, b, *extra)


def _matmul_stack(a, b, *, ta=False, tb=False, out_dtype=f32, name):
    def lanes(ref):
        return jnp.concatenate([ref[j] for j in range(N_DEV)], axis=1).astype(bf16)

    if ta:
        w = b.shape[1] // N_DEV

        def body(a_ref, b_ref, o_ref):
            p = _tn(a_ref[...].astype(bf16), b_ref[...].astype(bf16))
            for j in range(N_DEV):
                o_ref[j] = p[:, j * w:(j + 1) * w].astype(o_ref.dtype)

        return pl.pallas_call(body, out_shape=SDS((N_DEV, a.shape[1], w), out_dtype), name=name,
                              compiler_params=_params())(a, b)

    m, half = a.shape[0], a.shape[0] // 2
    n = b.shape[1] if tb else N_DEV * b.shape[2]

    def body(a_ref, b_ref, o_ref):
        av = a_ref[...].astype(bf16)
        o_ref[...] = (_nt(av, lanes(b_ref)) if tb else jnp.dot(av, lanes(b_ref), preferred_element_type=f32)
                      ).astype(o_ref.dtype)

    return pl.pallas_call(
        body, grid=(2,), in_specs=[pl.BlockSpec((half, a.shape[1]), lambda i: (i, 0)),
                                   pl.BlockSpec(b.shape, lambda i: (0, 0, 0))],
        out_specs=pl.BlockSpec((half, n), lambda i: (i, 0)), out_shape=SDS((m, n), out_dtype), name=name,
        compiler_params=_params(("parallel",)),
    )(a, b)


def _matmul_nt_shards(a, b, after, *, name, tm=512, tn=512):
    m, n, w = a.shape[0], b.shape[1], b.shape[2]
    assert a.shape[1] == N_DEV * w and m % tm == 0 and n % tn == 0

    def body(a_ref, b_ref, after_ref, o_ref):
        acc = _nt(a_ref[:, 0:w], b_ref[0])
        for j in range(1, N_DEV):
            acc = acc + _nt(a_ref[:, j * w:(j + 1) * w], b_ref[j])
        o_ref[...] = acc

    return pl.pallas_call(
        body, grid=(n // tn, m // tm),
        in_specs=[pl.BlockSpec((tm, N_DEV * w), lambda j, i: (i, 0)), pl.BlockSpec((N_DEV, tn, w), lambda j, i: (0, j, 0)),
                  pl.BlockSpec(memory_space=pl.ANY)],
        out_specs=pl.BlockSpec((tm, tn), lambda j, i: (i, j)), out_shape=SDS((m, n), f32), name=name,
        compiler_params=_params(("parallel", "parallel")),
    )(a, b, after)


def _rms(x):
    r = lax.rsqrt(jnp.mean(x * x, axis=-1, keepdims=True) + EPS)
    return r, x * r


def _rms_bwd(r, xn, dxn):
    return r * (dxn - xn * jnp.mean(dxn * xn, axis=-1, keepdims=True))


def _colsum(t):
    return jnp.sum(t, axis=0, keepdims=True)


def _sigmoid(x):
    return 1.0 / (1.0 + jnp.exp(-x))


def _modulated_norm(x, g, shift, scale):
    _, xn = _rms(x)
    return (xn * g) * (1.0 + scale) + shift


def _pre1(x, modv, g_mix):
    def fn(t, v):
        (xt,), (mv, g) = t, v
        return [_modulated_norm(xt, g, mv[0:1], mv[1:2])], []
    return _rowwise(fn, "pre1", [(x, D, 0)], [modv, g_mix], [(D, bf16)])[0]


def _post1(x, mix, modv, g_ffn):
    def fn(t, v):
        (xt, mt), (mv, g) = t, v
        x1 = xt + mv[2:3] * mt
        return [x1, _modulated_norm(x1, g, mv[3:4], mv[4:5])], []
    return _rowwise(fn, "post1", [(x, D, 0), (mix, D, 0)], [modv, g_ffn], [(D, f32), (D, bf16)])


def _ffn_in(h, w_stack):
    def body(h_ref, w_ref, act_ref, au_ref):
        p = jnp.dot(h_ref[...], w_ref[0], preferred_element_type=f32)
        a, u = p[:, :FF_PAD], p[:, FF_PAD:]
        act_ref[...] = (a * _sigmoid(a) * u).astype(act_ref.dtype)
        au_ref[...] = p.astype(au_ref.dtype)

    return pl.pallas_call(
        body, grid=(N_DEV,),
        in_specs=[pl.BlockSpec((SEQ, D), lambda j: (0, 0)), pl.BlockSpec((1, D, 2 * FF_PAD), lambda j: (j, 0, 0))],
        out_specs=[pl.BlockSpec((SEQ, FF_PAD), lambda j: (0, j)), pl.BlockSpec((SEQ, 2 * FF_PAD), lambda j: (0, j))],
        out_shape=(SDS((SEQ, FF_HID), bf16), SDS((SEQ, 2 * FF_HID), bf16)), name="ffn_in",
        compiler_params=_params(("parallel",)),
    )(h, w_stack)


def _ffn_bwd_in(dff, w_down_stack, au):
    def body(d_ref, w_ref, au_ref, o_ref):
        dact = _nt(d_ref[...], w_ref[0])
        p = au_ref[...].astype(f32)
        a, u = p[:, :FF_PAD], p[:, FF_PAD:]
        sg = _sigmoid(a)
        o_ref[...] = jnp.concatenate([dact * u * (sg * (1.0 + a * (1.0 - sg))), dact * (a * sg)],
                                     axis=1).astype(o_ref.dtype)

    return pl.pallas_call(
        body, grid=(N_DEV,),
        in_specs=[pl.BlockSpec((SEQ, D), lambda j: (0, 0)), pl.BlockSpec((1, FF_PAD, D), lambda j: (j, 0, 0)),
                  pl.BlockSpec((SEQ, 2 * FF_PAD), lambda j: (0, j))],
        out_specs=pl.BlockSpec((SEQ, 2 * FF_PAD), lambda j: (0, j)),
        out_shape=SDS((SEQ, 2 * FF_HID), bf16), name="ffn_bwd_in", compiler_params=_params(("parallel",)),
    )(dff, w_down_stack, au)


def _final(x1, ff, target, modv, g_final):
    def fn(t, v):
        (x1t, fft, tgt), (mv, g) = t, v
        x2 = x1t + mv[5:6] * fft
        r, xn = _rms(x2)
        err = xn * g - tgt
        dy = err * (1.0 / D)
        dx2 = _rms_bwd(r, xn, dy * g)
        return [dx2, dx2 * mv[5:6]], [_colsum(dy * xn), _colsum(dx2 * fft), _colsum(err * err) * (0.5 / D)]
    return _rowwise(fn, "final", [(x1, D, 0), (ff, D, 0), (target, D, 0)], [modv, g_final],
                    [(D, f32), (D, bf16)], [D, D, D])


def _mid_bwd(dh2, x1, dx2, mix, modv, g_ffn):
    def fn(t, v):
        (dh, x1t, dx2t, mt), (mv, g) = t, v
        r, xn = _rms(x1t)
        dn = dh * (1.0 + mv[4:5])
        dx1 = dx2t + _rms_bwd(r, xn, dn * g)
        return [dx1, dx1 * mv[2:3]], [_colsum(dh), _colsum(dh * (xn * g)), _colsum(dn * xn), _colsum(dx1 * mt)]
    return _rowwise(fn, "mid_bwd", [(dh2, D, 0), (x1, D, 0), (dx2, D, 0), (mix, D, 0)], [modv, g_ffn],
                    [(D, f32), (D, bf16)], [D, D, D, D])


def _first_bwd(dh1, x, dx1, modv, g_mix):
    def fn(t, v):
        (dh, xt, dx1t), (mv, g) = t, v
        r, xn = _rms(xt)
        dn = dh * (1.0 + mv[1:2])
        return [dx1t + _rms_bwd(r, xn, dn * g)], [_colsum(dh), _colsum(dh * (xn * g)), _colsum(dn * xn)]
    return _rowwise(fn, "first_bwd", [(dh1, D, 0), (x, D, 0), (dx1, D, 0)], [modv, g_mix], [(D, f32)], [D, D, D])


def _merge_fwd(ya, yb, proj):
    def fn(t, v):
        ya_t, yb_t, ga, gb = t
        return [_sigmoid(ga) * ya_t + _sigmoid(gb) * yb_t], []
    return _rowwise(fn, "merge_fwd", [(ya, D, 0), (yb, D, 0), (proj, D, C_GA // D), (proj, D, C_GB // D)], [],
                    [(D, bf16)])[0]


def _merge_bwd(dmerged, ya, yb, proj):
    def fn(t, v):
        dm, ya_t, yb_t, ga, gb = t
        sa, sb = _sigmoid(ga), _sigmoid(gb)
        return [dm * sa, dm * sb, dm * ya_t * (sa * (1.0 - sa)), dm * yb_t * (sb * (1.0 - sb))], []
    return _rowwise(fn, "merge_bwd",
                    [(dmerged, D, 0), (ya, D, 0), (yb, D, 0), (proj, D, C_GA // D), (proj, D, C_GB // D)], [],
                    [(D, bf16), (D, bf16), (D, bf16), (D, bf16)])


def _rope_tables():
    half = ROT_DIM // 2
    pos = jnp.arange(SEQ, dtype=f32)
    inv_freq = ROPE_THETA ** (-jnp.arange(0, ROT_DIM, 2, dtype=f32) / ROT_DIM)
    ang = pos[:, None] * inv_freq[None, :]
    cos, sin = jnp.cos(ang), jnp.sin(ang)
    pad = jnp.zeros((SEQ, HEAD_DIM - ROT_DIM), f32)
    zero = jnp.zeros((SEQ, half), f32)
    c_head = jnp.concatenate([cos, cos, pad + 1.0], axis=1)
    lo_head = jnp.concatenate([-sin, zero, pad], axis=1)
    hi_head = jnp.concatenate([zero, sin, pad], axis=1)
    return tuple(jnp.concatenate([t, t], axis=1) for t in (c_head, lo_head, hi_head))


def _over_heads(tables):
    return [jnp.tile(t, (1, DIL_W // LANES)) for t in tables]


def _rope_fwd(proj, tables):
    half = ROT_DIM // 2

    def fn(t, v):
        q, k, vv = t[:3]
        c, lo, hi = _over_heads(t[3:])
        rot = lambda z: z * c + pltpu.roll(z, DIL_W - half, 1) * lo + pltpu.roll(z, half, 1) * hi
        return [rot(q) * ATT_SCALE, rot(k), vv], []
    return _rowwise(fn, "rope_fwd", [(proj, DIL_W, C_QB // DIL_W), (proj, DIL_W, C_KB // DIL_W),
                                     (proj, DIL_W, C_VB // DIL_W)] + [(tb, LANES, 0) for tb in tables], [],
                    [(DIL_W, f32)] * 3)


def _rope_bwd(dqs, dks, tables):
    half = ROT_DIM // 2

    def fn(t, v):
        dq_t, dk_t = jnp.concatenate(t[:N_GROUPS], axis=1), jnp.concatenate(t[N_GROUPS:2 * N_GROUPS], axis=1)
        c, lo, hi = _over_heads(t[2 * N_GROUPS:])
        rot_t = lambda z: z * c + pltpu.roll(z * lo, half, 1) + pltpu.roll(z * hi, DIL_W - half, 1)
        return [rot_t(dq_t), rot_t(dk_t)], []
    return _rowwise(fn, "rope_bwd", [(a, DIL_OUT_W, 0) for a in (*dqs, *dks)] + [(tb, LANES, 0) for tb in tables],
                    [], [(DIL_W, bf16), (DIL_W, bf16)])


def _head_bcast_sum(d):
    lane = lax.broadcasted_iota(jnp.int32, d.shape, 1)
    out = jnp.zeros_like(d)
    for h in range(d.shape[1] // HEAD_DIM):
        sel = (lane >= h * HEAD_DIM) & (lane < (h + 1) * HEAD_DIM)
        out = jnp.where(sel, jnp.sum(jnp.where(sel, d, 0.0), axis=1, keepdims=True), out)
    return out


def _dil_combine(outs, lses):
    def fn(t, v):
        o0, o1, o2, l0, l1, l2 = t
        m = jnp.maximum(jnp.maximum(l0, l1), l2)
        w0, w1, w2 = jnp.exp(l0 - m), jnp.exp(l1 - m), jnp.exp(l2 - m)
        tot = w0 + w1 + w2
        return [(w0 * o0 + w1 * o1 + w2 * o2) / tot, m + jnp.log(tot)], []
    w = DIL_OUT_W
    return _rowwise(fn, "dil_combine", [(t, w, 0) for t in (*outs, *lses)], [], [(w, f32), (w, f32)])


def _dil_delta(dyb_h, yb_h):
    def fn(t, v):
        return [_head_bcast_sum(t[0] * t[1])], []
    return _rowwise(fn, "dil_delta", [(dyb_h, DIL_OUT_W, 0), (yb_h, DIL_OUT_W, 0)], [], [(DIL_OUT_W, f32)])[0]


def _adamw_math(wt, gt, mt, vt):
    mn = ADAM_B1 * mt + (1.0 - ADAM_B1) * gt
    vn = ADAM_B2 * vt + (1.0 - ADAM_B2) * (gt * gt)
    m_hat = mn / (1.0 - ADAM_B1 ** ADAM_STEP)
    v_hat = vn / (1.0 - ADAM_B2 ** ADAM_STEP)
    return -ADAM_LR * (m_hat / (jnp.sqrt(v_hat) + ADAM_EPS) + ADAM_WD * wt), mn, vn


def _adamw(w, g, m, v, name):
    shape = w.shape
    if w.ndim == 1:
        w, g, m, v = (t.reshape(1, -1) for t in (w, g, m, v))
    rows, cols = w.shape
    if rows % 8 and rows > 8:
        return _adamw_by_cols(w, g, m, v, name)
    tile = 256 if rows % 256 == 0 and rows > 512 else rows

    def fn(t, _):
        return list(_adamw_math(*t)), []
    delta, mn, vn = _rowwise(fn, name, [(w, cols, 0), (g, cols, 0), (m, cols, 0), (v, cols, 0)], [],
                             [(cols, f32)] * 3, tile=tile)
    return delta.reshape(shape), mn.reshape(shape), vn.reshape(shape)


def _adamw_by_cols(w, g, m, v, name, tile=256):
    rows, cols = w.shape

    def body(w_ref, g_ref, m_ref, v_ref, d_ref, mn_ref, vn_ref):
        d_ref[...], mn_ref[...], vn_ref[...] = _adamw_math(w_ref[...], g_ref[...], m_ref[...], v_ref[...])

    spec = pl.BlockSpec((rows, tile), lambda j: (0, j))
    return pl.pallas_call(body, grid=(cols // tile,), in_specs=[spec] * 4, out_specs=[spec] * 3,
                          out_shape=[SDS((rows, cols), f32)] * 3, name=name,
                          compiler_params=_params(("parallel",)))(w, g, m, v)


def _ada_fwd(c_all, w_shard, b_shard):
    def body(c_ref, w_ref, b_ref, o_ref):
        cv = c_ref[...]
        sc = (cv * _sigmoid(cv)).astype(bf16)
        o_ref[...] = jnp.dot(sc, w_ref[...].astype(bf16), preferred_element_type=f32) + b_ref[...]
    return pl.pallas_call(body, out_shape=SDS((N_DEV, w_shard.shape[1]), f32), name="ada_fwd",
                          compiler_params=_params())(c_all, w_shard, b_shard)


def _ada_bwd(c_all, dmod_cols):
    def body(c_ref, d_ref, o_ref):
        cv = c_ref[...]
        sc = cv * _sigmoid(cv)
        o_ref[...] = lax.dot_general(sc, d_ref[...], (((0,), (0,)), ((), ())), precision=lax.Precision.HIGHEST,
                                     preferred_element_type=f32)
    return pl.pallas_call(body, out_shape=SDS((D, dmod_cols.shape[1]), f32), name="ada_bwd",
                          compiler_params=_params())(c_all, dmod_cols)


def _small_reduce(gathered, after):
    def body(g_ref, after_ref, o_ref, loss_ref):
        acc = g_ref[0]
        for d in range(1, N_DEV):
            acc = acc + g_ref[d]
        o_ref[...] = acc
        loss_ref[...] = jnp.zeros((1, LANES), f32) + jnp.sum(acc[10:11, :])
    return pl.pallas_call(body, out_shape=(SDS((SMALL_ROWS, D), f32), SDS((1, LANES), f32)), name="small_reduce",
                          in_specs=[pl.BlockSpec(memory_space=pltpu.VMEM), pl.BlockSpec(memory_space=pl.ANY)],
                          compiler_params=_params())(gathered, after)


FOX_BLK = 512
CUM_BLK = 128


def _fold_lanes(t, op):
    out = t[:, :LANES]
    for j in range(1, t.shape[1] // LANES):
        out = op(out, t[:, j * LANES:(j + 1) * LANES])
    return out


def _fox_gate_fwd(proj, b_pad):
    nblk = SEQ // CUM_BLK

    def body(f_ref, b_ref, col_ref):
        r = lax.broadcasted_iota(jnp.int32, (CUM_BLK, CUM_BLK), 0)
        c = lax.broadcasted_iota(jnp.int32, (CUM_BLK, CUM_BLK), 1)
        tri = (r >= c).astype(f32)
        carry = jnp.zeros((1, LANES), f32)
        for blk in range(nblk):
            z = f_ref[blk * CUM_BLK:(blk + 1) * CUM_BLK, :] + b_ref[...]
            logf = jnp.minimum(z, 0.0) - jnp.log1p(jnp.exp(-jnp.abs(z)))
            cs = jnp.dot(tri, logf, precision=lax.Precision.HIGHEST, preferred_element_type=f32) + carry
            col_ref[blk * CUM_BLK:(blk + 1) * CUM_BLK, :] = cs
            carry = cs[CUM_BLK - 1:CUM_BLK, :]

    return pl.pallas_call(
        body, grid=(1,), in_specs=[pl.BlockSpec((SEQ, LANES), lambda i: (0, C_F // LANES)),
                                   pl.BlockSpec((1, LANES), lambda i: (0, 0))],
        out_specs=pl.BlockSpec((SEQ, LANES), lambda i: (0, 0)),
        out_shape=SDS((SEQ, LANES), f32), name="fox_gate_fwd",
        compiler_params=_params(("arbitrary",)),
    )(proj, b_pad)


def _fox_gate_bwd(dF_row, proj, b_pad):
    nblk = SEQ // CUM_BLK

    def body(d_ref, f_ref, b_ref, df_ref, db_ref, col_ref):
        r = lax.broadcasted_iota(jnp.int32, (CUM_BLK, CUM_BLK), 0)
        c = lax.broadcasted_iota(jnp.int32, (CUM_BLK, CUM_BLK), 1)
        tri = (r <= c).astype(f32)
        lane = lax.broadcasted_iota(jnp.int32, (CUM_BLK, LANES), 1)
        col_ref[...] = d_ref[...].T
        carry = jnp.zeros((1, LANES), f32)
        total = jnp.zeros((1, LANES), f32)
        for blk in reversed(range(nblk)):
            rows = slice(blk * CUM_BLK, (blk + 1) * CUM_BLK)
            cs = jnp.dot(tri, col_ref[rows, :], precision=lax.Precision.HIGHEST, preferred_element_type=f32) + carry
            carry = cs[0:1, :]
            z = f_ref[rows, :] + b_ref[...]
            df = jnp.where(lane < N_FOX_HEADS, cs * _sigmoid(-z), 0.0)
            df_ref[rows, :] = df.astype(df_ref.dtype)
            total = total + _colsum(df)
        db_ref[...] = total

    return pl.pallas_call(
        body, grid=(1,), in_specs=[pl.BlockSpec((LANES, SEQ), lambda i: (0, 0)),
                                   pl.BlockSpec((SEQ, LANES), lambda i: (0, C_F // LANES)),
                                   pl.BlockSpec((1, LANES), lambda i: (0, 0))],
        out_specs=[pl.BlockSpec((SEQ, LANES), lambda i: (0, 0)), pl.BlockSpec((1, LANES), lambda i: (0, 0))],
        out_shape=(SDS((SEQ, LANES), bf16), SDS((1, LANES), f32)), name="fox_gate_bwd",
        scratch_shapes=[pltpu.VMEM((SEQ, LANES), f32)],
        compiler_params=_params(("arbitrary",)),
    )(dF_row, proj, b_pad)


def _nt(a, b):
    return lax.dot_general(a, b, (((1,), (1,)), ((), ())), preferred_element_type=f32)


def _tn(a, b):
    return lax.dot_general(a, b, (((0,), (0,)), ((), ())), preferred_element_type=f32)


def _fox_prep(proj, f_col):
    def fn(t, v):
        q, k, vv, fc = t
        lane = lax.broadcasted_iota(jnp.int32, (q.shape[0], LANES), 1)
        qs, ks = [], []
        for h in range(N_FOX_HEADS):
            pair, pos = divmod(h, 2)
            own = (lane >= pos * HEAD_DIM) & (lane < (pos + 1) * HEAD_DIM)
            base = (1 - pos) * HEAD_DIM
            f = fc[:, h:h + 1]
            hi = f.astype(bf16).astype(f32)
            mid = (f - hi).astype(bf16).astype(f32)
            lo = (f - hi) - mid
            one = jnp.ones_like(f)
            qa = jnp.where(own, q[:, pair * LANES:(pair + 1) * LANES] * ATT_SCALE, 0.0)
            ka = k[:, pair * LANES:(pair + 1) * LANES]
            for idx, (qv, kv) in enumerate([(hi, one), (mid, one), (lo, one), (one, -hi), (one, -mid), (one, -lo)]):
                sel = lane == base + idx
                qa = jnp.where(sel, qv, qa)
                ka = jnp.where(sel, kv, ka)
            qs.append(qa)
            ks.append(ka)
        return [jnp.concatenate(qs, axis=1), jnp.concatenate(ks, axis=1), vv], []
    w = N_FOX_HEADS * LANES
    return _rowwise(fn, "fox_prep", [(proj, FOX_W, C_QA // FOX_W), (proj, FOX_W, C_KA // FOX_W),
                                     (proj, FOX_W, C_VA // FOX_W), (f_col, LANES, 0)], [],
                    [(w, bf16), (w, bf16), (FOX_W, bf16)])


def _fox_fwd(q_aug, k_aug, v):
    blk = FOX_BLK
    npair = FOX_W // LANES

    def body(q_ref, k_ref, v_ref, o_ref, lse_ref, s_scr):
        i = pl.program_id(1)
        tri = lax.broadcasted_iota(jnp.int32, (blk, blk), 0) >= lax.broadcasted_iota(jnp.int32, (blk, blk), 1)
        qh = [q_ref[:, h * LANES:(h + 1) * LANES] for h in range(2)]

        def logits(c, masked):
            off = pl.multiple_of(c * blk, blk)
            tops = []
            for h in range(2):
                s = _nt(qh[h], k_ref[pl.ds(off, blk), h * LANES:(h + 1) * LANES])
                if masked:
                    s = jnp.where(tri, s, NEG)
                s_scr[h, :, pl.ds(off, blk)] = s
                tops.append(_fold_lanes(s, jnp.maximum))
            return tops

        def pass_a(c, m):
            return tuple(jnp.maximum(a, b) for a, b in zip(m, logits(c, False)))

        m = lax.fori_loop(0, i, pass_a, tuple(jnp.full((blk, LANES), NEG, f32) for _ in range(2)))
        mx = [jnp.max(jnp.maximum(a, b), axis=1, keepdims=True) for a, b in zip(m, logits(i, True))]

        def pass_b(c, carry):
            off = pl.multiple_of(c * blk, blk)
            vv = v_ref[pl.ds(off, blk), :]
            new = []
            for h in range(2):
                l, acc = carry[h]
                p = jnp.exp(s_scr[h, :, pl.ds(off, blk)] - mx[h])
                new.append((l + _fold_lanes(p, jnp.add),
                            acc + jnp.dot(p.astype(bf16), vv, preferred_element_type=f32)))
            return tuple(new)

        zero = jnp.zeros((blk, LANES), f32)
        (l_a, acc_a), (l_b, acc_b) = lax.fori_loop(0, i + 1, pass_b, ((zero, zero), (zero, zero)))
        l_a = jnp.sum(l_a, axis=1, keepdims=True)
        l_b = jnp.sum(l_b, axis=1, keepdims=True)
        first = lax.broadcasted_iota(jnp.int32, (blk, LANES), 1) < HEAD_DIM
        o_ref[...] = jnp.where(first, acc_a / l_a, acc_b / l_b)
        lse_ref[0] = jnp.where(first, mx[0] + jnp.log(l_a), mx[1] + jnp.log(l_b))

    return pl.pallas_call(
        body, grid=(npair, SEQ // blk),
        in_specs=[pl.BlockSpec((blk, 2 * LANES), lambda p, i: (i, p)),
                  pl.BlockSpec((SEQ, 2 * LANES), lambda p, i: (0, p)),
                  pl.BlockSpec((SEQ, LANES), lambda p, i: (0, p))],
        out_specs=[pl.BlockSpec((blk, LANES), lambda p, i: (i, p)),
                   pl.BlockSpec((1, blk, LANES), lambda p, i: (p, i, 0))],
        out_shape=(SDS((SEQ, FOX_W), f32), SDS((npair, SEQ, LANES), f32)), name="fox_fwd",
        scratch_shapes=[pltpu.VMEM((2, blk, SEQ), f32)],
        compiler_params=_params(("parallel", "arbitrary")),
    )(q_aug, k_aug, v)


def _fox_bwd(q_aug, k_aug, v, do, o, lse):
    blk = FOX_BLK
    npair = FOX_W // LANES
    nblk = SEQ // blk

    def body(q_ref, k_ref, v_ref, do_ref, o_ref, lse_ref, dq_ref, dk_ref, dv_ref, df_ref,
             dq_acc, delta_ref, res_ref):
        lane_s = lax.broadcasted_iota(jnp.int32, (SEQ, LANES), 1)
        prod = do_ref[...] * o_ref[...]
        d_a = jnp.sum(jnp.where(lane_s < HEAD_DIM, prod, 0.0), axis=1, keepdims=True)
        d_b = jnp.sum(jnp.where(lane_s >= HEAD_DIM, prod, 0.0), axis=1, keepdims=True)
        delta_ref[...] = jnp.where(lane_s < HEAD_DIM, d_a, d_b)
        dq_acc[...] = jnp.zeros_like(dq_acc)
        res_ref[...] = jnp.zeros_like(res_ref)
        df_ref[...] = jnp.zeros_like(df_ref)
        lane = lax.broadcasted_iota(jnp.int32, (blk, LANES), 1)
        own = [lane < HEAD_DIM, lane >= HEAD_DIM]
        tri = lax.broadcasted_iota(jnp.int32, (blk, blk), 0) >= lax.broadcasted_iota(jnp.int32, (blk, blk), 1)

        def q_slab(qoff, h):
            return q_ref[pl.ds(qoff, blk), h * LANES:(h + 1) * LANES]

        def probs(qoff, h, k_h, masked):
            s = _nt(q_slab(qoff, h), k_h)
            if masked:
                s = jnp.where(tri, s, NEG)
            return jnp.exp(s - lse_ref[0, pl.ds(qoff, blk), h * HEAD_DIM:h * HEAD_DIM + 1])

        def k_slabs(koff):
            return [k_ref[pl.ds(koff, blk), h * LANES:(h + 1) * LANES] for h in range(2)]

        def kv_step(kj, _):
            koff = pl.multiple_of(kj * blk, blk)
            k_aug = k_slabs(koff)
            k_own = [jnp.where(own[h], k_aug[h], jnp.zeros_like(k_aug[h])) for h in range(2)]
            vv = v_ref[pl.ds(koff, blk), :]
            v_own = [jnp.where(own[h], vv, jnp.zeros_like(vv)) for h in range(2)]

            def q_tile(qi, carry, masked):
                qoff = pl.multiple_of(qi * blk, blk)
                dd = do_ref[pl.ds(qoff, blk), :].astype(bf16)
                new, dq_add = [], None
                for h in range(2):
                    dk_h, dv_h, dcol = carry[h]
                    p = probs(qoff, h, k_aug[h], masked)
                    dl = p * (_nt(dd, v_own[h]) - delta_ref[pl.ds(qoff, blk), h * HEAD_DIM:h * HEAD_DIM + 1])
                    dlb = dl.astype(bf16)
                    part = jnp.dot(dlb, k_own[h], preferred_element_type=f32)
                    dq_add = part if dq_add is None else dq_add + part
                    res_ref[h, pl.ds(qoff, blk), :] += _fold_lanes(dl, jnp.add)
                    new.append((dk_h + _tn(dlb, q_slab(qoff, h)), dv_h + _tn(p.astype(bf16), dd),
                                dcol + _colsum(dl)))
                dq_acc[pl.ds(qoff, blk), :] += dq_add * ATT_SCALE
                return tuple(new)

            zero = (jnp.zeros((blk, LANES), f32), jnp.zeros((blk, LANES), f32), jnp.zeros((1, blk), f32))
            carry = q_tile(kj, (zero, zero), True)
            (dk_a, dv_a, dcol_a), (dk_b, dv_b, dcol_b) = lax.fori_loop(
                kj + 1, nblk, lambda qi, cr: q_tile(qi, cr, False), carry)
            dk_ref[pl.ds(koff, blk), :] = jnp.where(own[0], dk_a, dk_b).astype(dk_ref.dtype)
            dv_ref[pl.ds(koff, blk), :] = jnp.where(own[0], dv_a, dv_b).astype(dv_ref.dtype)
            df_ref[0, 0:1, pl.ds(koff, blk)] = -dcol_a
            df_ref[0, 1:2, pl.ds(koff, blk)] = -dcol_b
            return 0

        lax.fori_loop(0, nblk, kv_step, 0)
        dq_ref[...] = dq_acc[...].astype(dq_ref.dtype)

        for h in range(2):
            res_ref[h] = jnp.zeros((SEQ, LANES), f32) + jnp.sum(res_ref[h], axis=1, keepdims=True)

        def kv_fix(kj, _):
            koff = pl.multiple_of(kj * blk, blk)
            k_aug = k_slabs(koff)

            def q_fix(qi, corr, masked):
                qoff = pl.multiple_of(qi * blk, blk)
                return tuple(corr[h] + _colsum(probs(qoff, h, k_aug[h], masked) * res_ref[h, pl.ds(qoff, blk), 0:1])
                             for h in range(2))

            zero = jnp.zeros((1, blk), f32)
            corr = lax.fori_loop(kj + 1, nblk, lambda qi, cr: q_fix(qi, cr, False), q_fix(kj, (zero, zero), True))
            df_ref[0, 0:1, pl.ds(koff, blk)] += corr[0]
            df_ref[0, 1:2, pl.ds(koff, blk)] += corr[1]
            return 0

        lax.fori_loop(0, nblk, kv_fix, 0)

    pair_aug = pl.BlockSpec((SEQ, 2 * LANES), lambda p: (0, p))
    slab = pl.BlockSpec((SEQ, LANES), lambda p: (0, p))
    per_pair = pl.BlockSpec((1, SEQ, LANES), lambda p: (p, 0, 0))
    rows = pl.BlockSpec((1, 8, SEQ), lambda p: (p, 0, 0))
    return pl.pallas_call(
        body, grid=(npair,),
        in_specs=[pair_aug, pair_aug, slab, slab, slab, per_pair],
        out_specs=[slab, slab, slab, rows],
        out_shape=(SDS((SEQ, FOX_W), bf16),) * 3 + (SDS((npair, 8, SEQ), f32),), name="fox_bwd",
        scratch_shapes=[pltpu.VMEM((SEQ, LANES), f32), pltpu.VMEM((SEQ, LANES), f32),
                        pltpu.VMEM((2, SEQ, LANES), f32)],
        compiler_params=_params(("parallel",)),
    )(q_aug, k_aug, v, do, o, lse)


DIL_BLK = 128
DILATIONS = (1, 4, 16)
N_GROUPS = len(DILATIONS)
DIL_PAIRS = DIL_OUT_W // LANES


def _dil_blocks(d):
    r1 = lax.broadcasted_iota(jnp.int32, (DIL_BLK, DIL_BLK), 0)
    c1 = lax.broadcasted_iota(jnp.int32, (DIL_BLK, DIL_BLK), 1)
    r2 = lax.broadcasted_iota(jnp.int32, (DIL_BLK, 2 * DIL_BLK), 0)
    c2 = lax.broadcasted_iota(jnp.int32, (DIL_BLK, 2 * DIL_BLK), 1)
    band = ((c2 < DIL_BLK) & (c2 >= r2)) | ((c2 >= DIL_BLK) & (c2 - DIL_BLK <= r2))
    out = []
    for r in range(d):
        for b in range(SEQ // d // DIL_BLK):
            rows = pl.ds(r + d * DIL_BLK * b, DIL_BLK, stride=d)
            if b == 0:
                out.append((rows, rows, r1 >= c1))
            else:
                out.append((rows, pl.ds(r + d * DIL_BLK * (b - 1), 2 * DIL_BLK, stride=d), band))
    return out


def _dil_fwd(q, k, v, g):
    def body(q_ref, k_ref, v_ref, o_ref, lse_ref):
        first = lax.broadcasted_iota(jnp.int32, (DIL_BLK, LANES), 1) < HEAD_DIM
        for rows, krows, mask in _dil_blocks(DILATIONS[g]):
            qv, kk, vv = q_ref[rows, :].astype(bf16), k_ref[krows, :].astype(bf16), v_ref[krows, :].astype(bf16)
            outs, lses = [], []
            for own in (first, ~first):
                s = jnp.where(mask, _nt(jnp.where(own, qv, jnp.zeros_like(qv)), kk), NEG)
                m = jnp.max(s, axis=1, keepdims=True)
                p = jnp.exp(s - m)
                l = jnp.sum(p, axis=1, keepdims=True)
                outs.append(jnp.dot(p.astype(bf16), vv, preferred_element_type=f32) / l)
                lses.append(m + jnp.log(l))
            o_ref[rows, :] = jnp.where(first, outs[0], outs[1])
            lse_ref[rows, :] = jnp.where(first, lses[0], lses[1])

    grouped = pl.BlockSpec((SEQ, LANES), lambda p: (0, DIL_PAIRS * g + p))
    own = pl.BlockSpec((SEQ, LANES), lambda p: (0, p))
    shape = SDS((SEQ, DIL_OUT_W), f32)
    return pl.pallas_call(
        body, grid=(DIL_PAIRS,), in_specs=[grouped] * 3, out_specs=[own] * 2, out_shape=(shape, shape),
        name=f"dil_fwd_{DILATIONS[g]}", compiler_params=_params(("parallel",)),
    )(q, k, v)


def _dil_bwd(q, k, v, do, lse, delta, g):
    def body(q_ref, k_ref, v_ref, do_ref, lse_ref, dl_ref, dq_ref, dk_ref, dv_ref):
        first = lax.broadcasted_iota(jnp.int32, (DIL_BLK, LANES), 1) < HEAD_DIM
        dk_ref[...] = jnp.zeros_like(dk_ref)
        dv_ref[...] = jnp.zeros_like(dv_ref)
        for rows, krows, mask in _dil_blocks(DILATIONS[g]):
            qv, kk, vv = q_ref[rows, :].astype(bf16), k_ref[krows, :].astype(bf16), v_ref[krows, :].astype(bf16)
            dov = do_ref[rows, :].astype(bf16)
            lsev, delv = lse_ref[rows, :], dl_ref[rows, :]
            dqs, dk_add, dv_add = [], None, None
            for h, own in enumerate((first, ~first)):
                col = h * HEAD_DIM
                qh = jnp.where(own, qv, jnp.zeros_like(qv))
                doh = jnp.where(own, dov, jnp.zeros_like(dov))
                p = jnp.exp(jnp.where(mask, _nt(qh, kk), NEG) - lsev[:, col:col + 1])
                dl = (p * (_nt(doh, vv) - delv[:, col:col + 1])).astype(bf16)
                dqs.append(jnp.dot(dl, kk, preferred_element_type=f32))
                dk_h, dv_h = _tn(dl, qh), _tn(p.astype(bf16), doh)
                dk_add = dk_h if dk_add is None else dk_add + dk_h
                dv_add = dv_h if dv_add is None else dv_add + dv_h
            dq_ref[rows, :] = jnp.where(first, dqs[0], dqs[1]) * ATT_SCALE
            dk_ref[krows, :] += dk_add
            dv_ref[krows, :] += dv_add

    grouped = pl.BlockSpec((SEQ, LANES), lambda p: (0, DIL_PAIRS * g + p))
    own = pl.BlockSpec((SEQ, LANES), lambda p: (0, p))
    shape = SDS((SEQ, DIL_OUT_W), f32)
    return pl.pallas_call(
        body, grid=(DIL_PAIRS,), in_specs=[grouped] * 3 + [own] * 3, out_specs=[own] * 3,
        out_shape=(shape, shape, shape), name=f"dil_bwd_{DILATIONS[g]}", compiler_params=_params(("parallel",)),
    )(q, k, v, do, lse, delta)


def _position():
    return lax.axis_index("x"), lax.axis_index("y"), lax.axis_index("c")


def _all_gather(block, name):
    def body(x_ref, out_ref, send_sems, recv_sems, local_sem):
        x, y, c = _position()
        me, sibling = (x, y, c), (x, y, 1 - c)
        chips = [(1 - x, y), (x, 1 - y), (1 - x, 1 - y)]

        def slot(px, py, pc):
            return out_ref.at[4 * px + 2 * py + pc]

        def copy(k, blk, to, src=None):
            return pltpu.make_async_remote_copy(
                src_ref=slot(*blk) if src is None else src, dst_ref=slot(*blk),
                send_sem=send_sems.at[k], recv_sem=recv_sems.at[k], device_id=to, device_id_type=MESH)

        mine = pltpu.make_async_copy(x_ref, slot(*me), local_sem)
        mine.start()
        first = [copy(0, me, sibling, src=x_ref)]
        first += [copy(1 + j, me, (*chip, c), src=x_ref) for j, chip in enumerate(chips)]
        for cp in first:
            cp.start()
        passed = [copy(4 + j, (*chip, c), sibling) for j, chip in enumerate(chips)]
        for j, chip in enumerate(chips):
            copy(1 + j, (*chip, c), me).wait_recv()
            passed[j].start()
        copy(0, sibling, me).wait_recv()
        for j, chip in enumerate(chips):
            copy(4 + j, (*chip, 1 - c), me).wait_recv()
        for cp in first + passed:
            cp.wait_send()
        mine.wait()

    return pl.pallas_call(
        body, out_shape=SDS((N_DEV,) + block.shape, block.dtype),
        in_specs=[pl.BlockSpec(memory_space=pl.ANY)], out_specs=pl.BlockSpec(memory_space=pl.ANY),
        scratch_shapes=[pltpu.SemaphoreType.DMA((7,)), pltpu.SemaphoreType.DMA((7,)), pltpu.SemaphoreType.DMA],
        name=name,
    )(block)


def _all_gather_many(blocks, name):
    n = len(blocks)

    def body(*refs):
        x_refs, out_refs = refs[:n], refs[n:2 * n]
        send_sems, recv_sems, local_sems = refs[2 * n:]
        x, y, c = _position()
        me, sibling = (x, y, c), (x, y, 1 - c)
        chips = [(1 - x, y), (x, 1 - y), (1 - x, 1 - y)]

        def slot(a, px, py, pc):
            return out_refs[a].at[4 * px + 2 * py + pc]

        def copy(a, k, blk, to, own=False):
            return pltpu.make_async_remote_copy(
                src_ref=x_refs[a] if own else slot(a, *blk), dst_ref=slot(a, *blk),
                send_sem=send_sems.at[a, k], recv_sem=recv_sems.at[a, k], device_id=to, device_id_type=MESH)

        mine = [pltpu.make_async_copy(x_refs[a], slot(a, *me), local_sems.at[a]) for a in range(n)]
        for cp in mine:
            cp.start()
        started = []
        for a in range(n):
            first = [copy(a, 0, me, sibling, own=True)]
            first += [copy(a, 1 + j, me, (*chip, c), own=True) for j, chip in enumerate(chips)]
            for cp in first:
                cp.start()
            started += first
        for a in range(n):
            for j, chip in enumerate(chips):
                copy(a, 1 + j, (*chip, c), me).wait_recv()
                passed = copy(a, 4 + j, (*chip, c), sibling)
                passed.start()
                started.append(passed)
        for a in range(n):
            copy(a, 0, sibling, me).wait_recv()
            for j, chip in enumerate(chips):
                copy(a, 4 + j, (*chip, 1 - c), me).wait_recv()
        for cp in started:
            cp.wait_send()
        for cp in mine:
            cp.wait()

    hbm = pl.BlockSpec(memory_space=pl.ANY)
    return pl.pallas_call(
        body, out_shape=[SDS((N_DEV,) + b.shape, b.dtype) for b in blocks],
        in_specs=[hbm] * n, out_specs=[hbm] * n,
        scratch_shapes=[pltpu.SemaphoreType.DMA((n, 7)), pltpu.SemaphoreType.DMA((n, 7)),
                        pltpu.SemaphoreType.DMA((n,))],
        name=name,
    )(*blocks)


HBM_SPEC = pl.BlockSpec(memory_space=pltpu.HBM)
SEM_SPEC = pl.BlockSpec(memory_space=pltpu.SEMAPHORE)
SPLIT_COPY = pltpu.CompilerParams(has_side_effects=pltpu.SideEffectType.DATAFLOW_SIDE_EFFECTING)


def _in_hbm(t):
    return pltpu.with_memory_space_constraint(t, pltpu.HBM)


def _pair_copies(g_refs, land_refs, send_sems, recv_sems):
    x, y, c = _position()
    return [pltpu.make_async_remote_copy(
        src_ref=g.at[2 * k + (1 - c)], dst_ref=land.at[k], send_sem=send_sems.at[4 * a + k],
        recv_sem=recv_sems.at[4 * a + k], device_id=(x, y, 1 - c), device_id_type=MESH)
        for a, (g, land) in enumerate(zip(g_refs, land_refs, strict=True)) for k in range(4)]


def _chip_copies(t_refs, land_refs, send_sems, recv_sems):
    x, y, c = _position()
    chips = [(1 - x, y), (x, 1 - y), (1 - x, 1 - y)]
    return [pltpu.make_async_remote_copy(
        src_ref=t.at[2 * px + py], dst_ref=land.at[j], send_sem=send_sems.at[3 * a + j],
        recv_sem=recv_sems.at[3 * a + j], device_id=(px, py, c), device_id_type=MESH)
        for a, (t, land) in enumerate(zip(t_refs, land_refs, strict=True)) for j, (px, py) in enumerate(chips)]


_ROUNDS = {"pair": (_pair_copies, 4), "chip": (_chip_copies, 3)}


def _exchange_start(kind, ts, name):
    copies, slots = _ROUNDS[kind]
    n = len(ts)
    lands = [_in_hbm(lax.empty((slots,) + t.shape[1:], t.dtype)) for t in ts]

    def body(*refs):
        for cp in copies(refs[:n], refs[n:2 * n], refs[2 * n], refs[2 * n + 1]):
            cp.start()
        refs[-1][...] = jnp.zeros_like(refs[-1])

    sems = pltpu.SemaphoreType.DMA((slots * n,))
    res = pl.pallas_call(
        body, name=name, in_specs=[HBM_SPEC] * (2 * n),
        out_shape=(sems, sems, *[pltpu.HBM(t.shape, t.dtype) for t in (*ts, *lands)], SDS((8, LANES), f32)),
        out_specs=(SEM_SPEC, SEM_SPEC, *[HBM_SPEC] * (2 * n), pl.BlockSpec(memory_space=pltpu.VMEM)),
        input_output_aliases={i: 2 + i for i in range(2 * n)}, compiler_params=SPLIT_COPY,
    )(*[_in_hbm(t) for t in ts], *lands)
    return res[:-1], res[-1]


def _exchange_wait(kind, state, after, name):
    copies, _ = _ROUNDS[kind]
    send_sems, recv_sems, *arrays = state
    n = len(arrays) // 2

    def body(*refs):
        for cp in copies(refs[:n], refs[n:2 * n], refs[2 * n], refs[2 * n + 1]):
            cp.wait_send()
            cp.wait_recv()

    res = pl.pallas_call(
        body, name=name, in_specs=[HBM_SPEC] * (2 * n) + [SEM_SPEC, SEM_SPEC, pl.BlockSpec(memory_space=pl.ANY)],
        out_shape=[pltpu.HBM(t.shape, t.dtype) for t in arrays], out_specs=[HBM_SPEC] * (2 * n),
        input_output_aliases={i: i for i in range(2 * n)}, compiler_params=SPLIT_COPY,
    )(*arrays, send_sems, recv_sems, after)
    return res[:n], res[n:]


def _gather_copies(x_refs, out_refs, send_sems, recv_sems):
    x, y, c = _position()
    peers = [(x, y, 1 - c), (1 - x, y, c), (x, 1 - y, c), (1 - x, 1 - y, c)]
    sends, arrivals = [], []
    for a, (x_ref, out_ref) in enumerate(zip(x_refs, out_refs, strict=True)):
        for k, (px, py, pc) in enumerate(peers):
            sems = dict(send_sem=send_sems.at[4 * a + k], recv_sem=recv_sems.at[4 * a + k],
                        device_id=(px, py, pc), device_id_type=MESH)
            sends.append(pltpu.make_async_remote_copy(src_ref=x_ref, dst_ref=out_ref.at[4 * x + 2 * y + c], **sems))
            arrivals.append(pltpu.make_async_remote_copy(src_ref=x_ref, dst_ref=out_ref.at[4 * px + 2 * py + pc],
                                                         **sems))
    return sends, arrivals


def _gather_start(blocks, after, name):
    n = len(blocks)
    outs = [_in_hbm(lax.empty((N_DEV,) + b.shape, b.dtype)) for b in blocks]

    def body(*refs):
        sends, _ = _gather_copies(refs[:n], refs[n:2 * n], refs[2 * n + 1], refs[2 * n + 2])
        for cp in sends:
            cp.start()
        refs[-1][...] = jnp.zeros_like(refs[-1])

    sems = pltpu.SemaphoreType.DMA((4 * n,))
    res = pl.pallas_call(
        body, name=name, in_specs=[HBM_SPEC] * (2 * n) + [pl.BlockSpec(memory_space=pl.ANY)],
        out_shape=(sems, sems, *[pltpu.HBM(t.shape, t.dtype) for t in (*blocks, *outs)], SDS((8, LANES), f32)),
        out_specs=(SEM_SPEC, SEM_SPEC, *[HBM_SPEC] * (2 * n), pl.BlockSpec(memory_space=pltpu.VMEM)),
        input_output_aliases={i: 2 + i for i in range(2 * n)}, compiler_params=SPLIT_COPY,
    )(*[_in_hbm(b) for b in blocks], *outs, after)
    return res[:-1], res[-1]


def _gather_wait(state, after, name):
    send_sems, recv_sems, *arrays = state
    n = len(arrays) // 2

    def body(*refs):
        sends, arrivals = _gather_copies(refs[:n], refs[n:2 * n], refs[2 * n], refs[2 * n + 1])
        for cp in sends:
            cp.wait_send()
        for cp in arrivals:
            cp.wait_recv()

    res = pl.pallas_call(
        body, name=name, in_specs=[HBM_SPEC] * (2 * n) + [SEM_SPEC, SEM_SPEC, pl.BlockSpec(memory_space=pl.ANY)],
        out_shape=[pltpu.HBM(t.shape, t.dtype) for t in arrays], out_specs=[HBM_SPEC] * (2 * n),
        input_output_aliases={i: i for i in range(2 * n)}, compiler_params=SPLIT_COPY,
    )(*arrays, send_sems, recv_sems, after)
    return res[:n], res[n:]


def _gather_finish(partial, name):
    n = len(partial)

    def body(*refs):
        in_refs, out_refs = refs[:n], refs[n:2 * n]
        send_sems, recv_sems = refs[2 * n:]
        x, y, c = _position()
        chips = [(1 - x, y), (x, 1 - y), (1 - x, 1 - y)]
        copies = []
        for a in range(n):
            for j, (px, py) in enumerate(chips):
                cp = pltpu.make_async_remote_copy(
                    src_ref=in_refs[a].at[4 * px + 2 * py + c], dst_ref=out_refs[a].at[4 * px + 2 * py + c],
                    send_sem=send_sems.at[a, j], recv_sem=recv_sems.at[a, j], device_id=(x, y, 1 - c),
                    device_id_type=MESH)
                cp.start()
                copies.append(cp)
        for a in range(n):
            for j, (px, py) in enumerate(chips):
                pltpu.make_async_remote_copy(
                    src_ref=in_refs[a].at[4 * px + 2 * py + (1 - c)], dst_ref=out_refs[a].at[4 * px + 2 * py + (1 - c)],
                    send_sem=send_sems.at[a, j], recv_sem=recv_sems.at[a, j], device_id=(x, y, 1 - c),
                    device_id_type=MESH).wait_recv()
        for cp in copies:
            cp.wait_send()

    hbm = pl.BlockSpec(memory_space=pl.ANY)
    return pl.pallas_call(
        body, out_shape=[SDS(p.shape, p.dtype) for p in partial], in_specs=[hbm] * n, out_specs=[hbm] * n,
        input_output_aliases={a: a for a in range(n)},
        scratch_shapes=[pltpu.SemaphoreType.DMA((n, 3)), pltpu.SemaphoreType.DMA((n, 3))],
        name=name,
    )(*partial)


def _row_tile(rows):
    return 512 if rows % 512 == 0 and rows > 512 else rows


def _pair_add(g, r1, core, name):
    def body(c_ref, g_ref, r_ref, o_ref):
        o_ref[...] = (g_ref[...].astype(f32) + r_ref[...].astype(f32)).astype(o_ref.dtype)

    rows, cols = g.shape[1:]
    tile = _row_tile(rows)
    blk = (1, tile, cols)
    return pl.pallas_call(
        body, out_shape=SDS((4, rows, cols), g.dtype), name=name,
        grid_spec=pltpu.PrefetchScalarGridSpec(
            num_scalar_prefetch=1, grid=(4, rows // tile),
            in_specs=[pl.BlockSpec(blk, lambda k, i, c_ref: (2 * k + c_ref[0], i, 0)),
                      pl.BlockSpec(blk, lambda k, i, c_ref: (k, i, 0))],
            out_specs=pl.BlockSpec(blk, lambda k, i, c_ref: (k, i, 0))),
        compiler_params=_params(("parallel", "arbitrary")),
    )(core, g, r1)


def _chip_add(t, r2, chip, name):
    def body(c_ref, t_ref, r_ref, o_ref):
        o_ref[...] = ((t_ref[0].astype(f32) + r_ref[0].astype(f32)) + r_ref[1].astype(f32)) + r_ref[2].astype(f32)

    rows, cols = t.shape[1:]
    tile = _row_tile(rows)
    return pl.pallas_call(
        body, out_shape=SDS((rows, cols), f32), name=name,
        grid_spec=pltpu.PrefetchScalarGridSpec(
            num_scalar_prefetch=1, grid=(rows // tile,),
            in_specs=[pl.BlockSpec((1, tile, cols), lambda i, c_ref: (c_ref[0], i, 0)),
                      pl.BlockSpec((3, tile, cols), lambda i, c_ref: (0, i, 0))],
            out_specs=pl.BlockSpec((tile, cols), lambda i, c_ref: (i, 0))),
        compiler_params=_params(("arbitrary",)),
    )(chip, t, r2)


def _pad_to(t, axis, size):
    pads = [(0, 0)] * t.ndim
    pads[axis] = (0, size - t.shape[axis])
    return jnp.pad(t, pads)


_REF_COLS = {"qa": (0, FOX_W), "ka": (FOX_W, FOX_W), "va": (2 * FOX_W, FOX_W), "f": (3 * FOX_W, N_FOX_HEADS)}
_REF_COLS.update({n: (3 * FOX_W + N_FOX_HEADS + i * DIL_W, DIL_W) for i, n in enumerate(("qb", "kb", "vb"))})
_REF_COLS.update({n: (3 * FOX_W + N_FOX_HEADS + 3 * DIL_W + i * D, D) for i, n in enumerate(("ga", "gb"))})
_REF_ORDER = ("qa", "ka", "va", "f", "qb", "kb", "vb", "ga", "gb")


def _shard_pad_cols(pieces):
    first = pieces[_REF_ORDER[0]]
    pad = jnp.zeros((first.shape[0], W_IN_PAD - W_IN_SH), first.dtype)
    parts, names, used = [], list(_REF_ORDER), 0
    for _ in range(N_DEV):
        need = W_IN_SH
        while need:
            take = min(need, _REF_COLS[names[0]][1] - used)
            parts.append(pieces[names[0]][:, used:used + take])
            need, used = need - take, used + take
            if used == _REF_COLS[names[0]][1]:
                names, used = names[1:], 0
        parts.append(pad)
    return jnp.concatenate(parts, axis=1)


def _slab_w_in(stack):
    def cols(name):
        lo, width = _REF_COLS[name]
        hi, out = lo + width, []
        while lo < hi:
            j, off = divmod(lo, W_IN_SH)
            n = min(hi - lo, W_IN_SH - off)
            out.append(stack[j, :, off:off + n])
            lo += n
        return out
    z = lambda n: [jnp.zeros((stack.shape[1], n), stack.dtype)]
    parts = (cols("ga") + cols("gb") + z(C_QB - 2 * D) + cols("qb") + cols("kb") + cols("vb") + cols("qa")
             + cols("ka") + cols("va") + cols("f") + z(LANES - N_FOX_HEADS))
    return jnp.concatenate(parts, axis=1)


def kernel(x, c, w_ada, b_ada, g_mix, w_in, b_fgate, w_br_a, w_br_b, w_out, g_ffn, w_ffn_gate, w_ffn_up, w_ffn_down, g_final, loss_target, m_w_ada, m_b_ada, m_g_mix, m_w_in, m_b_fgate, m_w_br_a, m_w_br_b, m_w_out, m_g_ffn, m_w_ffn_gate, m_w_ffn_up, m_w_ffn_down, m_g_final, v_w_ada, v_b_ada, v_g_mix, v_w_in, v_b_fgate, v_w_br_a, v_w_br_b, v_w_out, v_g_ffn, v_w_ffn_gate, v_w_ffn_up, v_w_ffn_down, v_g_final):
    px, py, pc = _position()
    dev = 4 * px + 2 * py + pc
    x2d, tgt = x[0], loss_target[0]

    c_all = _all_gather(c, "gather_c").reshape(N_DEV, D)
    ada_cols = w_ada.shape[2]
    b_shard = lax.dynamic_slice(b_ada, (0, dev * ada_cols), (1, ada_cols))
    mod_shard = _ada_fwd(c_all, w_ada[0], b_shard)
    mod_all = _all_gather(mod_shard, "gather_mod")
    modv = lax.dynamic_index_in_dim(mod_all, dev, axis=1, keepdims=False).reshape(6, D)

    gate_up = jnp.concatenate([_pad_to(w_ffn_gate[0], 1, FF_PAD), _pad_to(w_ffn_up[0], 1, FF_PAD)], axis=1)
    w_in_s, = _all_gather_many([_pad_to(w_in[0], 1, W_IN_PAD).astype(bf16)], "gather_w_in")
    later = [w_br_a[0], w_br_b[0], w_out[0], gate_up, _pad_to(w_ffn_down[0], 0, FF_PAD)]
    later_state, later_token = _gather_start([t.astype(bf16) for t in later], w_in_s, "gather_rest_start")
    w_in_p = _slab_w_in(w_in_s)

    h1 = _pre1(x2d, modv, g_mix)
    proj = _matmul(h1, w_in_p, name="mm_proj", tm=SEQ, tn=896, tk=D, after=later_token)
    b_pad = jnp.pad(b_fgate, ((0, 0), (0, LANES - N_FOX_HEADS)))
    q_aug, k_aug, va = _fox_prep(proj, _fox_gate_fwd(proj, b_pad))
    ya_h, lse_a = _fox_fwd(q_aug, k_aug, va)

    tables = _rope_tables()
    qb_r, kb_r, vb = _rope_fwd(proj, tables)
    by_group = [_dil_fwd(qb_r, kb_r, vb, grp) for grp in range(N_GROUPS)]
    yb_h, lse_b = _dil_combine([o for o, _ in by_group], [l for _, l in by_group])

    mine, arrived = _gather_wait(later_state, yb_h, "gather_rest_wait")
    w_a_s, w_b_s, w_o_s, w_gu_s, w_d_s = [
        lax.dynamic_update_slice(stack, block[None], (dev, 0, 0))
        for stack, block in zip(_gather_finish(arrived, "gather_rest_finish"), mine, strict=True)]
    w_o = w_o_s.reshape(D, D)
    w_d = w_d_s.reshape(FF_HID, D)
    ya = _matmul_stack(ya_h, w_a_s, name="mm_br_a")
    yb = _matmul_stack(yb_h, w_b_s, name="mm_br_b")

    merged = _merge_fwd(ya, yb, proj)
    mix = _matmul(merged, w_o, name="mm_out", tm=SEQ, tn=512, tk=D)
    x1, h2 = _post1(x2d, mix, modv, g_ffn)
    act, au = _ffn_in(h2, w_gu_s)
    ff = _matmul(act, w_d, name="mm_ffn_down", tm=SEQ // 2, tn=512, tk=FF_HID)

    dx2, dff, dg_final, dga_f, loss_lanes = _final(x1, ff, tgt, modv, g_final.reshape(1, D))
    dau = _ffn_bwd_in(dff, w_d_s, au)

    core = pc.astype(jnp.int32).reshape(1)
    chip = (2 * px + py).astype(jnp.int32).reshape(1)

    def pair_done(state, after, tags, name):
        mine, theirs = _exchange_wait("pair", state, after, "pair_wait_" + name)
        sums = [_pair_add(g, r, core, "pair_add_" + t) for g, r, t in zip(mine, theirs, tags)]
        return _exchange_start("chip", sums, "chip_start_" + name)

    def from_chips(state, after, tags, name):
        sums, got = _exchange_wait("chip", state, after, "chip_wait_" + name)
        return [_chip_add(p, r, chip, "chip_add_" + t) for p, r, t in zip(sums, got, tags)]

    g_gu = _matmul(h2, dau, ta=True, by_shard=True, out_dtype=bf16, name="mm_g_ffn_in", tm=D, tn=2 * FF_PAD, tk=SEQ)
    g_d = _matmul(act, dff, ta=True, out_dtype=bf16, name="mm_g_down", tm=FF_HID // 2, tn=512, tk=SEQ)
    ffn_tags = ["gu", "down"]
    ffn_pair, ffn_pair_token = _exchange_start("pair", [g_gu, g_d.reshape(N_DEV, FF_PAD, D)], "pair_start_ffn")

    dh2 = _matmul_nt_shards(dau, w_gu_s, ffn_pair_token, name="mm_d_h2")
    ffn_state, ffn_token = pair_done(ffn_pair, dh2, ffn_tags, "ffn")
    dx1, dmix, dsh_f, dsc_f, dg_ffn, dga_m = _mid_bwd(dh2, x1, dx2, mix, modv, g_ffn)
    dmerged = _matmul(dmix, w_o, tb=True, name="mm_d_merged", tm=SEQ, tn=512, tk=D, after=ffn_token)
    dya, dyb, dga, dgb = _merge_bwd(dmerged, ya, yb, proj)
    dya_h = _matmul_stack(dya, w_a_s, tb=True, name="mm_d_ya")
    dyb_h = _matmul_stack(dyb, w_b_s, tb=True, name="mm_d_yb")

    dqa, dka, dva, dF = _fox_bwd(q_aug, k_aug, va, dya_h, ya_h, lse_a)
    dF_row = jnp.pad(dF[:, :2, :].reshape(N_FOX_HEADS, SEQ), ((0, LANES - N_FOX_HEADS), (0, 0)))
    df, db_fgate = _fox_gate_bwd(dF_row, proj, b_pad)

    delta_b = _dil_delta(dyb_h, yb_h)
    dil_grads = [_dil_bwd(qb_r, kb_r, vb, dyb_h, lse_b, delta_b, grp) for grp in range(N_GROUPS)]
    dqb, dkb = _rope_bwd([t[0] for t in dil_grads], [t[1] for t in dil_grads], tables)
    dvb = jnp.concatenate([t[2] for t in dil_grads], axis=1).astype(bf16)

    dproj = _shard_pad_cols({"qa": dqa, "ka": dka, "va": dva, "f": df[:, :N_FOX_HEADS], "qb": dqb, "kb": dkb,
                             "vb": dvb, "ga": dga, "gb": dgb})
    g_in = _matmul(h1, dproj, ta=True, by_shard=True, out_dtype=bf16, name="mm_g_in", tm=D, tn=W_IN_PAD, tk=SEQ)
    g_o = _matmul(merged, dmix, ta=True, out_dtype=bf16, name="mm_g_out", tm=D, tn=512, tk=SEQ)
    g_a = _matmul_stack(ya_h, dya, ta=True, out_dtype=bf16, name="mm_g_br_a")
    g_b = _matmul_stack(yb_h, dyb, ta=True, out_dtype=bf16, name="mm_g_br_b")
    rows_a, rows_b = FOX_W * W_BR_SH // D, DIL_OUT_W * W_BR_SH // D
    g_small = jnp.concatenate([g_a.reshape(N_DEV, rows_a, D), g_b.reshape(N_DEV, rows_b, D),
                               g_o.reshape(N_DEV, W_BR_SH, D)], axis=1)
    mix_tags = ["in", "small"]
    mix_pair, mix_pair_token = _exchange_start("pair", [g_in, g_small], "pair_start_mixer")

    dh1 = _matmul_nt_shards(dproj, w_in_s, mix_pair_token, name="mm_d_h1")
    grad_x, dsh_m, dsc_m, dg_mix = _first_bwd(dh1, x2d, dx1, modv, g_mix)

    pad_lane = lambda t: jnp.pad(t, ((0, 0), (0, D - t.shape[1])))
    small = jnp.concatenate([dsh_m, dsc_m, dga_m, dsh_f, dsc_f, dga_f, dg_mix, dg_ffn, dg_final,
                             pad_lane(db_fgate), loss_lanes, jnp.zeros((SMALL_ROWS - 11, D), f32)], axis=0)
    small_all = _all_gather(small, "gather_small")
    mix_state, mix_token = pair_done(mix_pair, small_all, mix_tags, "mixer")

    small_sum, loss_row = _small_reduce(small_all, mix_token)
    dmod_all = small_all[:, :6, :].reshape(N_DEV, 6 * D)
    g_w_ada = _ada_bwd(c_all, lax.dynamic_slice(dmod_all, (0, dev * ada_cols), (N_DEV, ada_cols)))
    s_gu, s_d = from_chips(ffn_state, small_sum, ffn_tags, "ffn")

    loss = loss_row[0, 0]
    g = {
        "w_ada": g_w_ada[None], "b_ada": small_sum[0:6].reshape(1, 6 * D), "g_mix": small_sum[6:7],
        "b_fgate": small_sum[9:10, :N_FOX_HEADS], "g_ffn": small_sum[7:8], "w_ffn_gate": s_gu[None, :, :W_FF_SH],
        "w_ffn_up": s_gu[None, :, FF_PAD:FF_PAD + W_FF_SH], "w_ffn_down": s_d[None, :W_FF_SH],
        "g_final": small_sum[8],
    }
    w = {"w_ada": w_ada, "b_ada": b_ada, "g_mix": g_mix, "w_in": w_in, "b_fgate": b_fgate, "w_br_a": w_br_a,
         "w_br_b": w_br_b, "w_out": w_out, "g_ffn": g_ffn, "w_ffn_gate": w_ffn_gate, "w_ffn_up": w_ffn_up,
         "w_ffn_down": w_ffn_down, "g_final": g_final}
    m = {"w_ada": m_w_ada, "b_ada": m_b_ada, "g_mix": m_g_mix, "w_in": m_w_in, "b_fgate": m_b_fgate,
         "w_br_a": m_w_br_a, "w_br_b": m_w_br_b, "w_out": m_w_out, "g_ffn": m_g_ffn, "w_ffn_gate": m_w_ffn_gate,
         "w_ffn_up": m_w_ffn_up, "w_ffn_down": m_w_ffn_down, "g_final": m_g_final}
    v = {"w_ada": v_w_ada, "b_ada": v_b_ada, "g_mix": v_g_mix, "w_in": v_w_in, "b_fgate": v_b_fgate,
         "w_br_a": v_w_br_a, "w_br_b": v_w_br_b, "w_out": v_w_out, "g_ffn": v_g_ffn, "w_ffn_gate": v_w_ffn_gate,
         "w_ffn_up": v_w_ffn_up, "w_ffn_down": v_w_ffn_down, "g_final": v_g_final}
    names = list(w)
    delta, new_m, new_v = {}, {}, {}

    transposed = ("w_in", "w_ffn_gate", "w_ffn_up")

    def update(n):
        shape = w[n].shape
        if n in transposed:
            g_t = g[n][0].T
            dl, mn, vn = _adamw(w[n][0].T, g_t, m[n][0].T, v[n][0].T, "adamw_" + n)
            g[n], delta[n], new_m[n], new_v[n] = g_t.T[None], dl.T[None], mn.T[None], vn.T[None]
            return
        two_d = (lambda t: t.reshape(shape[-2:])) if len(shape) == 3 else (lambda t: t)
        dl, mn, vn = _adamw(two_d(w[n]), two_d(g[n]), two_d(m[n]), two_d(v[n]), "adamw_" + n)
        delta[n], new_m[n], new_v[n] = dl.reshape(shape), mn.reshape(shape), vn.reshape(shape)

    for n in list(g):
        update(n)
    done = sum(delta[n].reshape(-1)[:N_FOX_HEADS] for n in g)
    s_in, s_small = from_chips(mix_state, done, mix_tags, "mixer")
    g.update({"w_in": s_in[None, :, :W_IN_SH], "w_br_a": s_small[:rows_a].reshape(1, FOX_W, W_BR_SH),
              "w_br_b": s_small[rows_a:rows_a + rows_b].reshape(1, DIL_OUT_W, W_BR_SH),
              "w_out": s_small[None, rows_a + rows_b:]})
    for n in ("w_in", "w_br_a", "w_br_b", "w_out"):
        update(n)

    return (loss, grad_x[None], *[g[n] for n in names], *[delta[n] for n in names],
            *[new_m[n] for n in names], *[new_v[n] for n in names])
```

```python
import functools

import jax
import jax.numpy as jnp
from jax import lax
from jax.experimental import pallas as pl
from jax.experimental.pallas import tpu as pltpu

f32 = jnp.float32
bf16 = jnp.bfloat16
SDS = jax.ShapeDtypeStruct
MESH = pl.DeviceIdType.MESH

N_DEV = 8
D = 1024
SEQ = 2048
HEAD_DIM = 64
N_FOX_HEADS = 8
FOX_W = 512
DIL_W = 768
DIL_OUT_W = 256
ROT_DIM = 16
ROPE_THETA = 500000.0
D_FF = 2816
IN_COLS = 5896
EPS = 1e-6
NEG = -1e30
ATT_SCALE = HEAD_DIM ** -0.5

ADAM_LR = 0.001
ADAM_B1 = 0.9
ADAM_B2 = 0.999
ADAM_EPS = 1e-08
ADAM_WD = 0.01
ADAM_STEP = 10

C_GA, C_GB, C_QB, C_KB, C_VB, C_QA, C_KA, C_VA, C_F = 0, 1024, 2304, 3072, 3840, 4608, 5120, 5632, 6144
PROJ_W = 6272
LANES = 128
VMEM_LIMIT = 52 * 1024 * 1024

W_IN_SH, W_IN_PAD = IN_COLS // N_DEV, 768
W_BR_SH = D // N_DEV
W_FF_SH, FF_PAD = D_FF // N_DEV, 384
FF_HID = N_DEV * FF_PAD
SMALL_ROWS = 16


def _params(sem=None):
    if sem is None:
        return pltpu.CompilerParams(vmem_limit_bytes=VMEM_LIMIT)
    return pltpu.CompilerParams(dimension_semantics=sem, vmem_limit_bytes=VMEM_LIMIT)


def _rowwise(fn, name, tiled, vecs, outs, reds=(), tile=256):
    nt, nv, no = len(tiled), len(vecs), len(outs)
    rows = tiled[0][0].shape[0]
    assert rows % tile == 0

    def body(*refs):
        tin = [r[...] for r in refs[:nt]]
        vin = [r[...] for r in refs[nt:nt + nv]]
        orefs = refs[nt + nv:nt + nv + no]
        rrefs = refs[nt + nv + no:]
        touts, routs = fn(tin, vin)
        for r, t in zip(orefs, touts, strict=True):
            r[...] = t.astype(r.dtype)
        if rrefs:
            @pl.when(pl.program_id(0) == 0)
            def _():
                for r in rrefs:
                    r[...] = jnp.zeros_like(r)
            for r, t in zip(rrefs, routs, strict=True):
                r[...] += t

    def col_map(cb):
        return lambda i: (i, cb)

    def whole_map(nd):
        return lambda i: (0,) * nd

    in_specs = [pl.BlockSpec((tile, w), col_map(cb)) for (_, w, cb) in tiled]
    in_specs += [pl.BlockSpec(v.shape, whole_map(v.ndim)) for v in vecs]
    out_specs = [pl.BlockSpec((tile, w), lambda i: (i, 0)) for (w, _) in outs]
    out_specs += [pl.BlockSpec((1, w), lambda i: (0, 0)) for w in reds]
    out_shape = [SDS((rows, w), dt) for (w, dt) in outs] + [SDS((1, w), f32) for w in reds]
    res = pl.pallas_call(
        body, grid=(rows // tile,), in_specs=in_specs, out_specs=out_specs, out_shape=out_shape, name=name,
        compiler_params=_params(("arbitrary",)),
    )(*[t[0] for t in tiled], *vecs)
    return res


def _matmul(a, b, *, ta=False, tb=False, out_dtype=f32, name, tm, tn, tk, by_shard=False, after=None):
    m, k = (a.shape[1], a.shape[0]) if ta else a.shape
    if by_shard and not ta:
        n, kb = (b.shape[1], N_DEV * b.shape[2]) if tb else (N_DEV * b.shape[2], b.shape[1])
        assert (tk if tb else tn) == b.shape[2]
    else:
        n, kb = (b.shape[0], b.shape[1]) if tb else (b.shape[1], b.shape[0])
    assert kb == k and m % tm == 0 and n % tn == 0 and k % tk == 0
    nk = k // tk
    dims = (((0 if ta else 1,), (1 if tb else 0,)), ((), ()))
    b_stacked = by_shard and not ta
    o_stacked = by_shard and ta

    def body(a_ref, b_ref, *rest):
        o_ref, *acc = rest[1:] if after is not None else rest
        bv = b_ref[0] if b_stacked else b_ref[...]
        p = lax.dot_general(a_ref[...].astype(bf16), bv.astype(bf16), dims, preferred_element_type=f32)

        def put(val):
            if o_stacked:
                o_ref[0] = val.astype(o_ref.dtype)
            else:
                o_ref[...] = val.astype(o_ref.dtype)

        if nk == 1:
            put(p)
        else:
            acc_ref, = acc
            kk = pl.program_id(2)

            @pl.when(kk == 0)
            def _():
                acc_ref[...] = p

            @pl.when(kk > 0)
            def _():
                acc_ref[...] += p

            @pl.when(kk == nk - 1)
            def _():
                put(acc_ref[...])

    a_spec = pl.BlockSpec((tk, tm), lambda i, j, kk: (kk, i)) if ta else pl.BlockSpec((tm, tk), lambda i, j, kk: (i, kk))
    if b_stacked and tb:
        b_spec = pl.BlockSpec((1, tn, tk), lambda i, j, kk: (kk, j, 0))
    elif b_stacked:
        b_spec = pl.BlockSpec((1, tk, tn), lambda i, j, kk: (j, kk, 0))
    elif tb:
        b_spec = pl.BlockSpec((tn, tk), lambda i, j, kk: (j, kk))
    else:
        b_spec = pl.BlockSpec((tk, tn), lambda i, j, kk: (kk, j))
    if o_stacked:
        assert tn == n // N_DEV
        out_spec = pl.BlockSpec((1, tm, tn), lambda i, j, kk: (j, i, 0))
        out_shape = SDS((N_DEV, m, tn), out_dtype)
    else:
        out_spec = pl.BlockSpec((tm, tn), lambda i, j, kk: (i, j))
        out_shape = SDS((m, n), out_dtype)
    extra_specs, extra = ([pl.BlockSpec(memory_space=pl.ANY)], [after]) if after is not None else ([], [])
    return pl.pallas_call(
        body, grid=(m // tm, n // tn, nk), in_specs=[a_spec, b_spec] + extra_specs, out_specs=out_spec,
        out_shape=out_shape, name=name, scratch_shapes=[pltpu.VMEM((tm, tn), f32)] if nk > 1 else [],
        compiler_params=_params(("parallel", "parallel", "arbitrary")),
    )(a, b, *extra)


def _matmul_stack(a, b, *, ta=False, tb=False, out_dtype=f32, name):
    def lanes(ref):
        return jnp.concatenate([ref[j] for j in range(N_DEV)], axis=1).astype(bf16)

    if ta:
        w = b.shape[1] // N_DEV

        def body(a_ref, b_ref, o_ref):
            p = _tn(a_ref[...].astype(bf16), b_ref[...].astype(bf16))
            for j in range(N_DEV):
                o_ref[j] = p[:, j * w:(j + 1) * w].astype(o_ref.dtype)

        return pl.pallas_call(body, out_shape=SDS((N_DEV, a.shape[1], w), out_dtype), name=name,
                              compiler_params=_params())(a, b)

    m, half = a.shape[0], a.shape[0] // 2
    n = b.shape[1] if tb else N_DEV * b.shape[2]

    def body(a_ref, b_ref, o_ref):
        av = a_ref[...].astype(bf16)
        o_ref[...] = (_nt(av, lanes(b_ref)) if tb else jnp.dot(av, lanes(b_ref), preferred_element_type=f32)
                      ).astype(o_ref.dtype)

    return pl.pallas_call(
        body, grid=(2,), in_specs=[pl.BlockSpec((half, a.shape[1]), lambda i: (i, 0)),
                                   pl.BlockSpec(b.shape, lambda i: (0, 0, 0))],
        out_specs=pl.BlockSpec((half, n), lambda i: (i, 0)), out_shape=SDS((m, n), out_dtype), name=name,
        compiler_params=_params(("parallel",)),
    )(a, b)


def _matmul_nt_shards(a, b, after, *, name, tm=512, tn=1024):
    m, n, w = a.shape[0], b.shape[1], b.shape[2]
    assert a.shape[1] == N_DEV * w and m % tm == 0 and n % tn == 0

    def body(a_ref, b_ref, after_ref, o_ref):
        acc = _nt(a_ref[:, 0:w], b_ref[0])
        for j in range(1, N_DEV):
            acc = acc + _nt(a_ref[:, j * w:(j + 1) * w], b_ref[j])
        o_ref[...] = acc

    return pl.pallas_call(
        body, grid=(n // tn, m // tm),
        in_specs=[pl.BlockSpec((tm, N_DEV * w), lambda j, i: (i, 0)), pl.BlockSpec((N_DEV, tn, w), lambda j, i: (0, j, 0)),
                  pl.BlockSpec(memory_space=pl.ANY)],
        out_specs=pl.BlockSpec((tm, tn), lambda j, i: (i, j)), out_shape=SDS((m, n), f32), name=name,
        compiler_params=_params(("parallel", "parallel")),
    )(a, b, after)


def _rms(x):
    r = lax.rsqrt(jnp.mean(x * x, axis=-1, keepdims=True) + EPS)
    return r, x * r


def _rms_bwd(r, xn, dxn):
    return r * (dxn - xn * jnp.mean(dxn * xn, axis=-1, keepdims=True))


def _colsum(t):
    return jnp.sum(t, axis=0, keepdims=True)


def _sigmoid(x):
    return 1.0 / (1.0 + jnp.exp(-x))


def _modulated_norm(x, g, shift, scale):
    _, xn = _rms(x)
    return (xn * g) * (1.0 + scale) + shift


def _pre1(x, modv, g_mix):
    def fn(t, v):
        (xt,), (mv, g) = t, v
        return [_modulated_norm(xt, g, mv[0:1], mv[1:2])], []
    return _rowwise(fn, "pre1", [(x, D, 0)], [modv, g_mix], [(D, bf16)])[0]


def _post1(x, mix, modv, g_ffn):
    def fn(t, v):
        (xt, mt), (mv, g) = t, v
        x1 = xt + mv[2:3] * mt
        return [x1, _modulated_norm(x1, g, mv[3:4], mv[4:5])], []
    return _rowwise(fn, "post1", [(x, D, 0), (mix, D, 0)], [modv, g_ffn], [(D, f32), (D, bf16)])


def _ffn_in(h, w_stack):
    def body(h_ref, w_ref, act_ref, au_ref):
        p = jnp.dot(h_ref[...], w_ref[0], preferred_element_type=f32)
        a, u = p[:, :FF_PAD], p[:, FF_PAD:]
        act_ref[...] = (a * _sigmoid(a) * u).astype(act_ref.dtype)
        au_ref[...] = p.astype(au_ref.dtype)

    return pl.pallas_call(
        body, grid=(N_DEV,),
        in_specs=[pl.BlockSpec((SEQ, D), lambda j: (0, 0)), pl.BlockSpec((1, D, 2 * FF_PAD), lambda j: (j, 0, 0))],
        out_specs=[pl.BlockSpec((SEQ, FF_PAD), lambda j: (0, j)), pl.BlockSpec((SEQ, 2 * FF_PAD), lambda j: (0, j))],
        out_shape=(SDS((SEQ, FF_HID), bf16), SDS((SEQ, 2 * FF_HID), bf16)), name="ffn_in",
        compiler_params=_params(("parallel",)),
    )(h, w_stack)


def _ffn_bwd_in(dff, w_down_stack, au):
    def body(d_ref, w_ref, au_ref, o_ref):
        dact = _nt(d_ref[...], w_ref[0])
        p = au_ref[...].astype(f32)
        a, u = p[:, :FF_PAD], p[:, FF_PAD:]
        sg = _sigmoid(a)
        o_ref[...] = jnp.concatenate([dact * u * (sg * (1.0 + a * (1.0 - sg))), dact * (a * sg)],
                                     axis=1).astype(o_ref.dtype)

    return pl.pallas_call(
        body, grid=(N_DEV,),
        in_specs=[pl.BlockSpec((SEQ, D), lambda j: (0, 0)), pl.BlockSpec((1, FF_PAD, D), lambda j: (j, 0, 0)),
                  pl.BlockSpec((SEQ, 2 * FF_PAD), lambda j: (0, j))],
        out_specs=pl.BlockSpec((SEQ, 2 * FF_PAD), lambda j: (0, j)),
        out_shape=SDS((SEQ, 2 * FF_HID), bf16), name="ffn_bwd_in", compiler_params=_params(("parallel",)),
    )(dff, w_down_stack, au)


def _final(x1, ff, target, modv, g_final):
    def fn(t, v):
        (x1t, fft, tgt), (mv, g) = t, v
        x2 = x1t + mv[5:6] * fft
        r, xn = _rms(x2)
        err = xn * g - tgt
        dy = err * (1.0 / D)
        dx2 = _rms_bwd(r, xn, dy * g)
        return [dx2, dx2 * mv[5:6]], [_colsum(dy * xn), _colsum(dx2 * fft), _colsum(err * err) * (0.5 / D)]
    return _rowwise(fn, "final", [(x1, D, 0), (ff, D, 0), (target, D, 0)], [modv, g_final],
                    [(D, f32), (D, bf16)], [D, D, D])


def _mid_bwd(dh2, x1, dx2, mix, modv, g_ffn):
    def fn(t, v):
        (dh, x1t, dx2t, mt), (mv, g) = t, v
        r, xn = _rms(x1t)
        dn = dh * (1.0 + mv[4:5])
        dx1 = dx2t + _rms_bwd(r, xn, dn * g)
        return [dx1, dx1 * mv[2:3]], [_colsum(dh), _colsum(dh * (xn * g)), _colsum(dn * xn), _colsum(dx1 * mt)]
    return _rowwise(fn, "mid_bwd", [(dh2, D, 0), (x1, D, 0), (dx2, D, 0), (mix, D, 0)], [modv, g_ffn],
                    [(D, f32), (D, bf16)], [D, D, D, D])


def _first_bwd(dh1, x, dx1, modv, g_mix):
    def fn(t, v):
        (dh, xt, dx1t), (mv, g) = t, v
        r, xn = _rms(xt)
        dn = dh * (1.0 + mv[1:2])
        return [dx1t + _rms_bwd(r, xn, dn * g)], [_colsum(dh), _colsum(dh * (xn * g)), _colsum(dn * xn)]
    return _rowwise(fn, "first_bwd", [(dh1, D, 0), (x, D, 0), (dx1, D, 0)], [modv, g_mix], [(D, f32)], [D, D, D])


def _merge_fwd(ya, yb, proj):
    def fn(t, v):
        ya_t, yb_t, ga, gb = t
        return [_sigmoid(ga) * ya_t + _sigmoid(gb) * yb_t], []
    return _rowwise(fn, "merge_fwd", [(ya, D, 0), (yb, D, 0), (proj, D, C_GA // D), (proj, D, C_GB // D)], [],
                    [(D, bf16)])[0]


def _merge_bwd(dmerged, ya, yb, proj):
    def fn(t, v):
        dm, ya_t, yb_t, ga, gb = t
        sa, sb = _sigmoid(ga), _sigmoid(gb)
        return [dm * sa, dm * sb, dm * ya_t * (sa * (1.0 - sa)), dm * yb_t * (sb * (1.0 - sb))], []
    return _rowwise(fn, "merge_bwd",
                    [(dmerged, D, 0), (ya, D, 0), (yb, D, 0), (proj, D, C_GA // D), (proj, D, C_GB // D)], [],
                    [(D, bf16), (D, bf16), (D, bf16), (D, bf16)])


def _rope_tables():
    half = ROT_DIM // 2
    pos = jnp.arange(SEQ, dtype=f32)
    inv_freq = ROPE_THETA ** (-jnp.arange(0, ROT_DIM, 2, dtype=f32) / ROT_DIM)
    ang = pos[:, None] * inv_freq[None, :]
    cos, sin = jnp.cos(ang), jnp.sin(ang)
    pad = jnp.zeros((SEQ, HEAD_DIM - ROT_DIM), f32)
    zero = jnp.zeros((SEQ, half), f32)
    c_head = jnp.concatenate([cos, cos, pad + 1.0], axis=1)
    lo_head = jnp.concatenate([-sin, zero, pad], axis=1)
    hi_head = jnp.concatenate([zero, sin, pad], axis=1)
    return tuple(jnp.concatenate([t, t], axis=1) for t in (c_head, lo_head, hi_head))


def _over_heads(tables):
    return [jnp.tile(t, (1, DIL_W // LANES)) for t in tables]


def _rope_fwd(proj, tables):
    half = ROT_DIM // 2

    def fn(t, v):
        q, k, vv = t[:3]
        c, lo, hi = _over_heads(t[3:])
        rot = lambda z: z * c + pltpu.roll(z, DIL_W - half, 1) * lo + pltpu.roll(z, half, 1) * hi
        return [rot(q) * ATT_SCALE, rot(k), vv], []
    return _rowwise(fn, "rope_fwd", [(proj, DIL_W, C_QB // DIL_W), (proj, DIL_W, C_KB // DIL_W),
                                     (proj, DIL_W, C_VB // DIL_W)] + [(tb, LANES, 0) for tb in tables], [],
                    [(DIL_W, f32)] * 3)


def _rope_bwd(dqs, dks, tables):
    half = ROT_DIM // 2

    def fn(t, v):
        dq_t, dk_t = jnp.concatenate(t[:N_GROUPS], axis=1), jnp.concatenate(t[N_GROUPS:2 * N_GROUPS], axis=1)
        c, lo, hi = _over_heads(t[2 * N_GROUPS:])
        rot_t = lambda z: z * c + pltpu.roll(z * lo, half, 1) + pltpu.roll(z * hi, DIL_W - half, 1)
        return [rot_t(dq_t), rot_t(dk_t)], []
    return _rowwise(fn, "rope_bwd", [(a, DIL_OUT_W, 0) for a in (*dqs, *dks)] + [(tb, LANES, 0) for tb in tables],
                    [], [(DIL_W, bf16), (DIL_W, bf16)])


def _head_bcast_sum(d):
    lane = lax.broadcasted_iota(jnp.int32, d.shape, 1)
    out = jnp.zeros_like(d)
    for h in range(d.shape[1] // HEAD_DIM):
        sel = (lane >= h * HEAD_DIM) & (lane < (h + 1) * HEAD_DIM)
        out = jnp.where(sel, jnp.sum(jnp.where(sel, d, 0.0), axis=1, keepdims=True), out)
    return out


def _dil_combine(outs, lses):
    def fn(t, v):
        o0, o1, o2, l0, l1, l2 = t
        m = jnp.maximum(jnp.maximum(l0, l1), l2)
        w0, w1, w2 = jnp.exp(l0 - m), jnp.exp(l1 - m), jnp.exp(l2 - m)
        tot = w0 + w1 + w2
        return [(w0 * o0 + w1 * o1 + w2 * o2) / tot, m + jnp.log(tot)], []
    w = DIL_OUT_W
    return _rowwise(fn, "dil_combine", [(t, w, 0) for t in (*outs, *lses)], [], [(w, f32), (w, f32)])


def _dil_delta(dyb_h, yb_h):
    def fn(t, v):
        return [_head_bcast_sum(t[0] * t[1])], []
    return _rowwise(fn, "dil_delta", [(dyb_h, DIL_OUT_W, 0), (yb_h, DIL_OUT_W, 0)], [], [(DIL_OUT_W, f32)])[0]


def _adamw_math(wt, gt, mt, vt):
    mn = ADAM_B1 * mt + (1.0 - ADAM_B1) * gt
    vn = ADAM_B2 * vt + (1.0 - ADAM_B2) * (gt * gt)
    m_hat = mn / (1.0 - ADAM_B1 ** ADAM_STEP)
    v_hat = vn / (1.0 - ADAM_B2 ** ADAM_STEP)
    return -ADAM_LR * (m_hat / (jnp.sqrt(v_hat) + ADAM_EPS) + ADAM_WD * wt), mn, vn


def _adamw(w, g, m, v, name):
    shape = w.shape
    if w.ndim == 1:
        w, g, m, v = (t.reshape(1, -1) for t in (w, g, m, v))
    rows, cols = w.shape
    if rows % 8 and rows > 8:
        return _adamw_by_cols(w, g, m, v, name)
    tile = 256 if rows % 256 == 0 and rows > 512 else rows

    def fn(t, _):
        return list(_adamw_math(*t)), []
    delta, mn, vn = _rowwise(fn, name, [(w, cols, 0), (g, cols, 0), (m, cols, 0), (v, cols, 0)], [],
                             [(cols, f32)] * 3, tile=tile)
    return delta.reshape(shape), mn.reshape(shape), vn.reshape(shape)


def _adamw_by_cols(w, g, m, v, name, tile=256):
    rows, cols = w.shape

    def body(w_ref, g_ref, m_ref, v_ref, d_ref, mn_ref, vn_ref):
        d_ref[...], mn_ref[...], vn_ref[...] = _adamw_math(w_ref[...], g_ref[...], m_ref[...], v_ref[...])

    spec = pl.BlockSpec((rows, tile), lambda j: (0, j))
    return pl.pallas_call(body, grid=(cols // tile,), in_specs=[spec] * 4, out_specs=[spec] * 3,
                          out_shape=[SDS((rows, cols), f32)] * 3, name=name,
                          compiler_params=_params(("parallel",)))(w, g, m, v)


def _ada_fwd(c_all, w_shard, b_shard):
    def body(c_ref, w_ref, b_ref, o_ref):
        cv = c_ref[...]
        sc = (cv * _sigmoid(cv)).astype(bf16)
        o_ref[...] = jnp.dot(sc, w_ref[...].astype(bf16), preferred_element_type=f32) + b_ref[...]
    return pl.pallas_call(body, out_shape=SDS((N_DEV, w_shard.shape[1]), f32), name="ada_fwd",
                          compiler_params=_params())(c_all, w_shard, b_shard)


def _ada_bwd(c_all, dmod_cols):
    def body(c_ref, d_ref, o_ref):
        cv = c_ref[...]
        sc = cv * _sigmoid(cv)
        o_ref[...] = lax.dot_general(sc, d_ref[...], (((0,), (0,)), ((), ())), precision=lax.Precision.HIGHEST,
                                     preferred_element_type=f32)
    return pl.pallas_call(body, out_shape=SDS((D, dmod_cols.shape[1]), f32), name="ada_bwd",
                          compiler_params=_params())(c_all, dmod_cols)


def _small_reduce(gathered, after):
    def body(g_ref, after_ref, o_ref, loss_ref):
        acc = g_ref[0]
        for d in range(1, N_DEV):
            acc = acc + g_ref[d]
        o_ref[...] = acc
        loss_ref[...] = jnp.zeros((1, LANES), f32) + jnp.sum(acc[10:11, :])
    return pl.pallas_call(body, out_shape=(SDS((SMALL_ROWS, D), f32), SDS((1, LANES), f32)), name="small_reduce",
                          in_specs=[pl.BlockSpec(memory_space=pltpu.VMEM), pl.BlockSpec(memory_space=pl.ANY)],
                          compiler_params=_params())(gathered, after)


FOX_BLK = 512
CUM_BLK = 128


def _fold_lanes(t, op):
    out = t[:, :LANES]
    for j in range(1, t.shape[1] // LANES):
        out = op(out, t[:, j * LANES:(j + 1) * LANES])
    return out


def _fox_gate_fwd(proj, b_pad):
    nblk = SEQ // CUM_BLK

    def body(f_ref, b_ref, col_ref):
        r = lax.broadcasted_iota(jnp.int32, (CUM_BLK, CUM_BLK), 0)
        c = lax.broadcasted_iota(jnp.int32, (CUM_BLK, CUM_BLK), 1)
        tri = (r >= c).astype(f32)
        carry = jnp.zeros((1, LANES), f32)
        for blk in range(nblk):
            z = f_ref[blk * CUM_BLK:(blk + 1) * CUM_BLK, :] + b_ref[...]
            logf = jnp.minimum(z, 0.0) - jnp.log1p(jnp.exp(-jnp.abs(z)))
            cs = jnp.dot(tri, logf, precision=lax.Precision.HIGHEST, preferred_element_type=f32) + carry
            col_ref[blk * CUM_BLK:(blk + 1) * CUM_BLK, :] = cs
            carry = cs[CUM_BLK - 1:CUM_BLK, :]

    return pl.pallas_call(
        body, grid=(1,), in_specs=[pl.BlockSpec((SEQ, LANES), lambda i: (0, C_F // LANES)),
                                   pl.BlockSpec((1, LANES), lambda i: (0, 0))],
        out_specs=pl.BlockSpec((SEQ, LANES), lambda i: (0, 0)),
        out_shape=SDS((SEQ, LANES), f32), name="fox_gate_fwd",
        compiler_params=_params(("arbitrary",)),
    )(proj, b_pad)


def _fox_gate_bwd(dF_row, proj, b_pad):
    nblk = SEQ // CUM_BLK

    def body(d_ref, f_ref, b_ref, df_ref, db_ref, col_ref):
        r = lax.broadcasted_iota(jnp.int32, (CUM_BLK, CUM_BLK), 0)
        c = lax.broadcasted_iota(jnp.int32, (CUM_BLK, CUM_BLK), 1)
        tri = (r <= c).astype(f32)
        lane = lax.broadcasted_iota(jnp.int32, (CUM_BLK, LANES), 1)
        col_ref[...] = d_ref[...].T
        carry = jnp.zeros((1, LANES), f32)
        total = jnp.zeros((1, LANES), f32)
        for blk in reversed(range(nblk)):
            rows = slice(blk * CUM_BLK, (blk + 1) * CUM_BLK)
            cs = jnp.dot(tri, col_ref[rows, :], precision=lax.Precision.HIGHEST, preferred_element_type=f32) + carry
            carry = cs[0:1, :]
            z = f_ref[rows, :] + b_ref[...]
            df = jnp.where(lane < N_FOX_HEADS, cs * _sigmoid(-z), 0.0)
            df_ref[rows, :] = df.astype(df_ref.dtype)
            total = total + _colsum(df)
        db_ref[...] = total

    return pl.pallas_call(
        body, grid=(1,), in_specs=[pl.BlockSpec((LANES, SEQ), lambda i: (0, 0)),
                                   pl.BlockSpec((SEQ, LANES), lambda i: (0, C_F // LANES)),
                                   pl.BlockSpec((1, LANES), lambda i: (0, 0))],
        out_specs=[pl.BlockSpec((SEQ, LANES), lambda i: (0, 0)), pl.BlockSpec((1, LANES), lambda i: (0, 0))],
        out_shape=(SDS((SEQ, LANES), bf16), SDS((1, LANES), f32)), name="fox_gate_bwd",
        scratch_shapes=[pltpu.VMEM((SEQ, LANES), f32)],
        compiler_params=_params(("arbitrary",)),
    )(dF_row, proj, b_pad)


def _nt(a, b):
    return lax.dot_general(a, b, (((1,), (1,)), ((), ())), preferred_element_type=f32)


def _tn(a, b):
    return lax.dot_general(a, b, (((0,), (0,)), ((), ())), preferred_element_type=f32)


def _fox_prep(proj, f_col):
    def fn(t, v):
        q, k, vv, fc = t
        lane = lax.broadcasted_iota(jnp.int32, (q.shape[0], LANES), 1)
        qs, ks = [], []
        for h in range(N_FOX_HEADS):
            pair, pos = divmod(h, 2)
            own = (lane >= pos * HEAD_DIM) & (lane < (pos + 1) * HEAD_DIM)
            base = (1 - pos) * HEAD_DIM
            f = fc[:, h:h + 1]
            hi = f.astype(bf16).astype(f32)
            mid = (f - hi).astype(bf16).astype(f32)
            lo = (f - hi) - mid
            one = jnp.ones_like(f)
            qa = jnp.where(own, q[:, pair * LANES:(pair + 1) * LANES] * ATT_SCALE, 0.0)
            ka = k[:, pair * LANES:(pair + 1) * LANES]
            for idx, (qv, kv) in enumerate([(hi, one), (mid, one), (lo, one), (one, -hi), (one, -mid), (one, -lo)]):
                sel = lane == base + idx
                qa = jnp.where(sel, qv, qa)
                ka = jnp.where(sel, kv, ka)
            qs.append(qa)
            ks.append(ka)
        return [jnp.concatenate(qs, axis=1), jnp.concatenate(ks, axis=1), vv], []
    w = N_FOX_HEADS * LANES
    return _rowwise(fn, "fox_prep", [(proj, FOX_W, C_QA // FOX_W), (proj, FOX_W, C_KA // FOX_W),
                                     (proj, FOX_W, C_VA // FOX_W), (f_col, LANES, 0)], [],
                    [(w, bf16), (w, bf16), (FOX_W, bf16)])


def _fox_fwd(q_aug, k_aug, v):
    blk = FOX_BLK
    npair = FOX_W // LANES

    def body(q_ref, k_ref, v_ref, o_ref, lse_ref, s_scr):
        i = pl.program_id(1)
        tri = lax.broadcasted_iota(jnp.int32, (blk, blk), 0) >= lax.broadcasted_iota(jnp.int32, (blk, blk), 1)
        qh = [q_ref[:, h * LANES:(h + 1) * LANES] for h in range(2)]

        def logits(c, masked):
            off = pl.multiple_of(c * blk, blk)
            tops = []
            for h in range(2):
                s = _nt(qh[h], k_ref[pl.ds(off, blk), h * LANES:(h + 1) * LANES])
                if masked:
                    s = jnp.where(tri, s, NEG)
                s_scr[h, :, pl.ds(off, blk)] = s
                tops.append(_fold_lanes(s, jnp.maximum))
            return tops

        def pass_a(c, m):
            return tuple(jnp.maximum(a, b) for a, b in zip(m, logits(c, False)))

        m = lax.fori_loop(0, i, pass_a, tuple(jnp.full((blk, LANES), NEG, f32) for _ in range(2)))
        mx = [jnp.max(jnp.maximum(a, b), axis=1, keepdims=True) for a, b in zip(m, logits(i, True))]

        def pass_b(c, carry):
            off = pl.multiple_of(c * blk, blk)
            vv = v_ref[pl.ds(off, blk), :]
            new = []
            for h in range(2):
                l, acc = carry[h]
                p = jnp.exp(s_scr[h, :, pl.ds(off, blk)] - mx[h])
                new.append((l + _fold_lanes(p, jnp.add),
                            acc + jnp.dot(p.astype(bf16), vv, preferred_element_type=f32)))
            return tuple(new)

        zero = jnp.zeros((blk, LANES), f32)
        (l_a, acc_a), (l_b, acc_b) = lax.fori_loop(0, i + 1, pass_b, ((zero, zero), (zero, zero)))
        l_a = jnp.sum(l_a, axis=1, keepdims=True)
        l_b = jnp.sum(l_b, axis=1, keepdims=True)
        first = lax.broadcasted_iota(jnp.int32, (blk, LANES), 1) < HEAD_DIM
        o_ref[...] = jnp.where(first, acc_a / l_a, acc_b / l_b)
        lse_ref[0] = jnp.where(first, mx[0] + jnp.log(l_a), mx[1] + jnp.log(l_b))

    return pl.pallas_call(
        body, grid=(npair, SEQ // blk),
        in_specs=[pl.BlockSpec((blk, 2 * LANES), lambda p, i: (i, p)),
                  pl.BlockSpec((SEQ, 2 * LANES), lambda p, i: (0, p)),
                  pl.BlockSpec((SEQ, LANES), lambda p, i: (0, p))],
        out_specs=[pl.BlockSpec((blk, LANES), lambda p, i: (i, p)),
                   pl.BlockSpec((1, blk, LANES), lambda p, i: (p, i, 0))],
        out_shape=(SDS((SEQ, FOX_W), f32), SDS((npair, SEQ, LANES), f32)), name="fox_fwd",
        scratch_shapes=[pltpu.VMEM((2, blk, SEQ), f32)],
        compiler_params=_params(("parallel", "arbitrary")),
    )(q_aug, k_aug, v)


def _fox_bwd(q_aug, k_aug, v, do, o, lse):
    blk = FOX_BLK
    npair = FOX_W // LANES
    nblk = SEQ // blk

    def body(q_ref, k_ref, v_ref, do_ref, o_ref, lse_ref, dq_ref, dk_ref, dv_ref, df_ref,
             dq_acc, delta_ref, res_ref):
        lane_s = lax.broadcasted_iota(jnp.int32, (SEQ, LANES), 1)
        prod = do_ref[...] * o_ref[...]
        d_a = jnp.sum(jnp.where(lane_s < HEAD_DIM, prod, 0.0), axis=1, keepdims=True)
        d_b = jnp.sum(jnp.where(lane_s >= HEAD_DIM, prod, 0.0), axis=1, keepdims=True)
        delta_ref[...] = jnp.where(lane_s < HEAD_DIM, d_a, d_b)
        dq_acc[...] = jnp.zeros_like(dq_acc)
        res_ref[...] = jnp.zeros_like(res_ref)
        df_ref[...] = jnp.zeros_like(df_ref)
        lane = lax.broadcasted_iota(jnp.int32, (blk, LANES), 1)
        own = [lane < HEAD_DIM, lane >= HEAD_DIM]
        tri = lax.broadcasted_iota(jnp.int32, (blk, blk), 0) >= lax.broadcasted_iota(jnp.int32, (blk, blk), 1)

        def q_slab(qoff, h):
            return q_ref[pl.ds(qoff, blk), h * LANES:(h + 1) * LANES]

        def probs(qoff, h, k_h, masked):
            s = _nt(q_slab(qoff, h), k_h)
            if masked:
                s = jnp.where(tri, s, NEG)
            return jnp.exp(s - lse_ref[0, pl.ds(qoff, blk), h * HEAD_DIM:h * HEAD_DIM + 1])

        def k_slabs(koff):
            return [k_ref[pl.ds(koff, blk), h * LANES:(h + 1) * LANES] for h in range(2)]

        def kv_step(kj, _):
            koff = pl.multiple_of(kj * blk, blk)
            k_aug = k_slabs(koff)
            k_own = [jnp.where(own[h], k_aug[h], jnp.zeros_like(k_aug[h])) for h in range(2)]
            vv = v_ref[pl.ds(koff, blk), :]
            v_own = [jnp.where(own[h], vv, jnp.zeros_like(vv)) for h in range(2)]

            def q_tile(qi, carry, masked):
                qoff = pl.multiple_of(qi * blk, blk)
                dd = do_ref[pl.ds(qoff, blk), :].astype(bf16)
                new, dq_add = [], None
                for h in range(2):
                    dk_h, dv_h, dcol = carry[h]
                    p = probs(qoff, h, k_aug[h], masked)
                    dl = p * (_nt(dd, v_own[h]) - delta_ref[pl.ds(qoff, blk), h * HEAD_DIM:h * HEAD_DIM + 1])
                    dlb = dl.astype(bf16)
                    part = jnp.dot(dlb, k_own[h], preferred_element_type=f32)
                    dq_add = part if dq_add is None else dq_add + part
                    res_ref[h, pl.ds(qoff, blk), :] += _fold_lanes(dl, jnp.add)
                    new.append((dk_h + _tn(dlb, q_slab(qoff, h)), dv_h + _tn(p.astype(bf16), dd),
                                dcol + _colsum(dl)))
                dq_acc[pl.ds(qoff, blk), :] += dq_add * ATT_SCALE
                return tuple(new)

            zero = (jnp.zeros((blk, LANES), f32), jnp.zeros((blk, LANES), f32), jnp.zeros((1, blk), f32))
            carry = q_tile(kj, (zero, zero), True)
            (dk_a, dv_a, dcol_a), (dk_b, dv_b, dcol_b) = lax.fori_loop(
                kj + 1, nblk, lambda qi, cr: q_tile(qi, cr, False), carry)
            dk_ref[pl.ds(koff, blk), :] = jnp.where(own[0], dk_a, dk_b).astype(dk_ref.dtype)
            dv_ref[pl.ds(koff, blk), :] = jnp.where(own[0], dv_a, dv_b).astype(dv_ref.dtype)
            df_ref[0, 0:1, pl.ds(koff, blk)] = -dcol_a
            df_ref[0, 1:2, pl.ds(koff, blk)] = -dcol_b
            return 0

        lax.fori_loop(0, nblk, kv_step, 0)
        dq_ref[...] = dq_acc[...].astype(dq_ref.dtype)

        for h in range(2):
            res_ref[h] = jnp.zeros((SEQ, LANES), f32) + jnp.sum(res_ref[h], axis=1, keepdims=True)

        def kv_fix(kj, _):
            koff = pl.multiple_of(kj * blk, blk)
            k_aug = k_slabs(koff)

            def q_fix(qi, corr, masked):
                qoff = pl.multiple_of(qi * blk, blk)
                return tuple(corr[h] + _colsum(probs(qoff, h, k_aug[h], masked) * res_ref[h, pl.ds(qoff, blk), 0:1])
                             for h in range(2))

            zero = jnp.zeros((1, blk), f32)
            corr = lax.fori_loop(kj + 1, nblk, lambda qi, cr: q_fix(qi, cr, False), q_fix(kj, (zero, zero), True))
            df_ref[0, 0:1, pl.ds(koff, blk)] += corr[0]
            df_ref[0, 1:2, pl.ds(koff, blk)] += corr[1]
            return 0

        lax.fori_loop(0, nblk, kv_fix, 0)

    pair_aug = pl.BlockSpec((SEQ, 2 * LANES), lambda p: (0, p))
    slab = pl.BlockSpec((SEQ, LANES), lambda p: (0, p))
    per_pair = pl.BlockSpec((1, SEQ, LANES), lambda p: (p, 0, 0))
    rows = pl.BlockSpec((1, 8, SEQ), lambda p: (p, 0, 0))
    return pl.pallas_call(
        body, grid=(npair,),
        in_specs=[pair_aug, pair_aug, slab, slab, slab, per_pair],
        out_specs=[slab, slab, slab, rows],
        out_shape=(SDS((SEQ, FOX_W), bf16),) * 3 + (SDS((npair, 8, SEQ), f32),), name="fox_bwd",
        scratch_shapes=[pltpu.VMEM((SEQ, LANES), f32), pltpu.VMEM((SEQ, LANES), f32),
                        pltpu.VMEM((2, SEQ, LANES), f32)],
        compiler_params=_params(("parallel",)),
    )(q_aug, k_aug, v, do, o, lse)


DIL_BLK = 128
DILATIONS = (1, 4, 16)
N_GROUPS = len(DILATIONS)
DIL_PAIRS = DIL_OUT_W // LANES


def _dil_blocks(d):
    r1 = lax.broadcasted_iota(jnp.int32, (DIL_BLK, DIL_BLK), 0)
    c1 = lax.broadcasted_iota(jnp.int32, (DIL_BLK, DIL_BLK), 1)
    r2 = lax.broadcasted_iota(jnp.int32, (DIL_BLK, 2 * DIL_BLK), 0)
    c2 = lax.broadcasted_iota(jnp.int32, (DIL_BLK, 2 * DIL_BLK), 1)
    band = ((c2 < DIL_BLK) & (c2 >= r2)) | ((c2 >= DIL_BLK) & (c2 - DIL_BLK <= r2))
    out = []
    for r in range(d):
        for b in range(SEQ // d // DIL_BLK):
            rows = pl.ds(r + d * DIL_BLK * b, DIL_BLK, stride=d)
            if b == 0:
                out.append((rows, rows, r1 >= c1))
            else:
                out.append((rows, pl.ds(r + d * DIL_BLK * (b - 1), 2 * DIL_BLK, stride=d), band))
    return out


def _dil_fwd(q, k, v, g):
    def body(q_ref, k_ref, v_ref, o_ref, lse_ref):
        first = lax.broadcasted_iota(jnp.int32, (DIL_BLK, LANES), 1) < HEAD_DIM
        for rows, krows, mask in _dil_blocks(DILATIONS[g]):
            qv, kk, vv = q_ref[rows, :].astype(bf16), k_ref[krows, :].astype(bf16), v_ref[krows, :].astype(bf16)
            outs, lses = [], []
            for own in (first, ~first):
                s = jnp.where(mask, _nt(jnp.where(own, qv, jnp.zeros_like(qv)), kk), NEG)
                m = jnp.max(s, axis=1, keepdims=True)
                p = jnp.exp(s - m)
                l = jnp.sum(p, axis=1, keepdims=True)
                outs.append(jnp.dot(p.astype(bf16), vv, preferred_element_type=f32) / l)
                lses.append(m + jnp.log(l))
            o_ref[rows, :] = jnp.where(first, outs[0], outs[1])
            lse_ref[rows, :] = jnp.where(first, lses[0], lses[1])

    grouped = pl.BlockSpec((SEQ, LANES), lambda p: (0, DIL_PAIRS * g + p))
    own = pl.BlockSpec((SEQ, LANES), lambda p: (0, p))
    shape = SDS((SEQ, DIL_OUT_W), f32)
    return pl.pallas_call(
        body, grid=(DIL_PAIRS,), in_specs=[grouped] * 3, out_specs=[own] * 2, out_shape=(shape, shape),
        name=f"dil_fwd_{DILATIONS[g]}", compiler_params=_params(("parallel",)),
    )(q, k, v)


def _dil_bwd(q, k, v, do, lse, delta, g):
    def body(q_ref, k_ref, v_ref, do_ref, lse_ref, dl_ref, dq_ref, dk_ref, dv_ref):
        first = lax.broadcasted_iota(jnp.int32, (DIL_BLK, LANES), 1) < HEAD_DIM
        dk_ref[...] = jnp.zeros_like(dk_ref)
        dv_ref[...] = jnp.zeros_like(dv_ref)
        for rows, krows, mask in _dil_blocks(DILATIONS[g]):
            qv, kk, vv = q_ref[rows, :].astype(bf16), k_ref[krows, :].astype(bf16), v_ref[krows, :].astype(bf16)
            dov = do_ref[rows, :].astype(bf16)
            lsev, delv = lse_ref[rows, :], dl_ref[rows, :]
            dqs, dk_add, dv_add = [], None, None
            for h, own in enumerate((first, ~first)):
                col = h * HEAD_DIM
                qh = jnp.where(own, qv, jnp.zeros_like(qv))
                doh = jnp.where(own, dov, jnp.zeros_like(dov))
                p = jnp.exp(jnp.where(mask, _nt(qh, kk), NEG) - lsev[:, col:col + 1])
                dl = (p * (_nt(doh, vv) - delv[:, col:col + 1])).astype(bf16)
                dqs.append(jnp.dot(dl, kk, preferred_element_type=f32))
                dk_h, dv_h = _tn(dl, qh), _tn(p.astype(bf16), doh)
                dk_add = dk_h if dk_add is None else dk_add + dk_h
                dv_add = dv_h if dv_add is None else dv_add + dv_h
            dq_ref[rows, :] = jnp.where(first, dqs[0], dqs[1]) * ATT_SCALE
            dk_ref[krows, :] += dk_add
            dv_ref[krows, :] += dv_add

    grouped = pl.BlockSpec((SEQ, LANES), lambda p: (0, DIL_PAIRS * g + p))
    own = pl.BlockSpec((SEQ, LANES), lambda p: (0, p))
    shape = SDS((SEQ, DIL_OUT_W), f32)
    return pl.pallas_call(
        body, grid=(DIL_PAIRS,), in_specs=[grouped] * 3 + [own] * 3, out_specs=[own] * 3,
        out_shape=(shape, shape, shape), name=f"dil_bwd_{DILATIONS[g]}", compiler_params=_params(("parallel",)),
    )(q, k, v, do, lse, delta)


def _position():
    return lax.axis_index("x"), lax.axis_index("y"), lax.axis_index("c")


def _all_gather(block, name):
    def body(x_ref, out_ref, send_sems, recv_sems, local_sem):
        x, y, c = _position()
        me, sibling = (x, y, c), (x, y, 1 - c)
        chips = [(1 - x, y), (x, 1 - y), (1 - x, 1 - y)]

        def slot(px, py, pc):
            return out_ref.at[4 * px + 2 * py + pc]

        def copy(k, blk, to, src=None):
            return pltpu.make_async_remote_copy(
                src_ref=slot(*blk) if src is None else src, dst_ref=slot(*blk),
                send_sem=send_sems.at[k], recv_sem=recv_sems.at[k], device_id=to, device_id_type=MESH)

        mine = pltpu.make_async_copy(x_ref, slot(*me), local_sem)
        mine.start()
        first = [copy(0, me, sibling, src=x_ref)]
        first += [copy(1 + j, me, (*chip, c), src=x_ref) for j, chip in enumerate(chips)]
        for cp in first:
            cp.start()
        passed = [copy(4 + j, (*chip, c), sibling) for j, chip in enumerate(chips)]
        for j, chip in enumerate(chips):
            copy(1 + j, (*chip, c), me).wait_recv()
            passed[j].start()
        copy(0, sibling, me).wait_recv()
        for j, chip in enumerate(chips):
            copy(4 + j, (*chip, 1 - c), me).wait_recv()
        for cp in first + passed:
            cp.wait_send()
        mine.wait()

    return pl.pallas_call(
        body, out_shape=SDS((N_DEV,) + block.shape, block.dtype),
        in_specs=[pl.BlockSpec(memory_space=pl.ANY)], out_specs=pl.BlockSpec(memory_space=pl.ANY),
        scratch_shapes=[pltpu.SemaphoreType.DMA((7,)), pltpu.SemaphoreType.DMA((7,)), pltpu.SemaphoreType.DMA],
        name=name,
    )(block)


def _all_gather_many(blocks, name):
    n = len(blocks)

    def body(*refs):
        x_refs, out_refs = refs[:n], refs[n:2 * n]
        send_sems, recv_sems, local_sems = refs[2 * n:]
        x, y, c = _position()
        me, sibling = (x, y, c), (x, y, 1 - c)
        chips = [(1 - x, y), (x, 1 - y), (1 - x, 1 - y)]

        def slot(a, px, py, pc):
            return out_refs[a].at[4 * px + 2 * py + pc]

        def copy(a, k, blk, to, own=False):
            return pltpu.make_async_remote_copy(
                src_ref=x_refs[a] if own else slot(a, *blk), dst_ref=slot(a, *blk),
                send_sem=send_sems.at[a, k], recv_sem=recv_sems.at[a, k], device_id=to, device_id_type=MESH)

        mine = [pltpu.make_async_copy(x_refs[a], slot(a, *me), local_sems.at[a]) for a in range(n)]
        for cp in mine:
            cp.start()
        started = []
        for a in range(n):
            first = [copy(a, 0, me, sibling, own=True)]
            first += [copy(a, 1 + j, me, (*chip, c), own=True) for j, chip in enumerate(chips)]
            for cp in first:
                cp.start()
            started += first
        for a in range(n):
            for j, chip in enumerate(chips):
                copy(a, 1 + j, (*chip, c), me).wait_recv()
                passed = copy(a, 4 + j, (*chip, c), sibling)
                passed.start()
                started.append(passed)
        for a in range(n):
            copy(a, 0, sibling, me).wait_recv()
            for j, chip in enumerate(chips):
                copy(a, 4 + j, (*chip, 1 - c), me).wait_recv()
        for cp in started:
            cp.wait_send()
        for cp in mine:
            cp.wait()

    hbm = pl.BlockSpec(memory_space=pl.ANY)
    return pl.pallas_call(
        body, out_shape=[SDS((N_DEV,) + b.shape, b.dtype) for b in blocks],
        in_specs=[hbm] * n, out_specs=[hbm] * n,
        scratch_shapes=[pltpu.SemaphoreType.DMA((n, 7)), pltpu.SemaphoreType.DMA((n, 7)),
                        pltpu.SemaphoreType.DMA((n,))],
        name=name,
    )(*blocks)


HBM_SPEC = pl.BlockSpec(memory_space=pltpu.HBM)
SEM_SPEC = pl.BlockSpec(memory_space=pltpu.SEMAPHORE)
SPLIT_COPY = pltpu.CompilerParams(has_side_effects=pltpu.SideEffectType.DATAFLOW_SIDE_EFFECTING)


def _in_hbm(t):
    return pltpu.with_memory_space_constraint(t, pltpu.HBM)


def _pair_copies(g_refs, land_refs, send_sems, recv_sems):
    x, y, c = _position()
    return [pltpu.make_async_remote_copy(
        src_ref=g.at[2 * k + (1 - c)], dst_ref=land.at[k], send_sem=send_sems.at[4 * a + k],
        recv_sem=recv_sems.at[4 * a + k], device_id=(x, y, 1 - c), device_id_type=MESH)
        for a, (g, land) in enumerate(zip(g_refs, land_refs, strict=True)) for k in range(4)]


def _chip_copies(t_refs, land_refs, send_sems, recv_sems):
    x, y, c = _position()
    chips = [(1 - x, y), (x, 1 - y), (1 - x, 1 - y)]
    return [pltpu.make_async_remote_copy(
        src_ref=t.at[2 * px + py], dst_ref=land.at[j], send_sem=send_sems.at[3 * a + j],
        recv_sem=recv_sems.at[3 * a + j], device_id=(px, py, c), device_id_type=MESH)
        for a, (t, land) in enumerate(zip(t_refs, land_refs, strict=True)) for j, (px, py) in enumerate(chips)]


_ROUNDS = {"pair": (_pair_copies, 4), "chip": (_chip_copies, 3)}


def _exchange_start(kind, ts, name):
    copies, slots = _ROUNDS[kind]
    n = len(ts)
    lands = [_in_hbm(lax.empty((slots,) + t.shape[1:], t.dtype)) for t in ts]

    def body(*refs):
        for cp in copies(refs[:n], refs[n:2 * n], refs[2 * n], refs[2 * n + 1]):
            cp.start()
        refs[-1][...] = jnp.zeros_like(refs[-1])

    sems = pltpu.SemaphoreType.DMA((slots * n,))
    res = pl.pallas_call(
        body, name=name, in_specs=[HBM_SPEC] * (2 * n),
        out_shape=(sems, sems, *[pltpu.HBM(t.shape, t.dtype) for t in (*ts, *lands)], SDS((8, LANES), f32)),
        out_specs=(SEM_SPEC, SEM_SPEC, *[HBM_SPEC] * (2 * n), pl.BlockSpec(memory_space=pltpu.VMEM)),
        input_output_aliases={i: 2 + i for i in range(2 * n)}, compiler_params=SPLIT_COPY,
    )(*[_in_hbm(t) for t in ts], *lands)
    return res[:-1], res[-1]


def _exchange_wait(kind, state, after, name):
    copies, _ = _ROUNDS[kind]
    send_sems, recv_sems, *arrays = state
    n = len(arrays) // 2

    def body(*refs):
        for cp in copies(refs[:n], refs[n:2 * n], refs[2 * n], refs[2 * n + 1]):
            cp.wait_send()
            cp.wait_recv()

    res = pl.pallas_call(
        body, name=name, in_specs=[HBM_SPEC] * (2 * n) + [SEM_SPEC, SEM_SPEC, pl.BlockSpec(memory_space=pl.ANY)],
        out_shape=[pltpu.HBM(t.shape, t.dtype) for t in arrays], out_specs=[HBM_SPEC] * (2 * n),
        input_output_aliases={i: i for i in range(2 * n)}, compiler_params=SPLIT_COPY,
    )(*arrays, send_sems, recv_sems, after)
    return res[:n], res[n:]


def _gather_copies(x_refs, out_refs, send_sems, recv_sems):
    x, y, c = _position()
    peers = [(x, y, 1 - c), (1 - x, y, c), (x, 1 - y, c), (1 - x, 1 - y, c)]
    sends, arrivals = [], []
    for a, (x_ref, out_ref) in enumerate(zip(x_refs, out_refs, strict=True)):
        for k, (px, py, pc) in enumerate(peers):
            sems = dict(send_sem=send_sems.at[4 * a + k], recv_sem=recv_sems.at[4 * a + k],
                        device_id=(px, py, pc), device_id_type=MESH)
            sends.append(pltpu.make_async_remote_copy(src_ref=x_ref, dst_ref=out_ref.at[4 * x + 2 * y + c], **sems))
            arrivals.append(pltpu.make_async_remote_copy(src_ref=x_ref, dst_ref=out_ref.at[4 * px + 2 * py + pc],
                                                         **sems))
    return sends, arrivals


def _gather_start(blocks, after, name):
    n = len(blocks)
    outs = [_in_hbm(lax.empty((N_DEV,) + b.shape, b.dtype)) for b in blocks]

    def body(*refs):
        sends, _ = _gather_copies(refs[:n], refs[n:2 * n], refs[2 * n + 1], refs[2 * n + 2])
        for cp in sends:
            cp.start()
        refs[-1][...] = jnp.zeros_like(refs[-1])

    sems = pltpu.SemaphoreType.DMA((4 * n,))
    res = pl.pallas_call(
        body, name=name, in_specs=[HBM_SPEC] * (2 * n) + [pl.BlockSpec(memory_space=pl.ANY)],
        out_shape=(sems, sems, *[pltpu.HBM(t.shape, t.dtype) for t in (*blocks, *outs)], SDS((8, LANES), f32)),
        out_specs=(SEM_SPEC, SEM_SPEC, *[HBM_SPEC] * (2 * n), pl.BlockSpec(memory_space=pltpu.VMEM)),
        input_output_aliases={i: 2 + i for i in range(2 * n)}, compiler_params=SPLIT_COPY,
    )(*[_in_hbm(b) for b in blocks], *outs, after)
    return res[:-1], res[-1]


def _gather_wait(state, after, name):
    send_sems, recv_sems, *arrays = state
    n = len(arrays) // 2

    def body(*refs):
        sends, arrivals = _gather_copies(refs[:n], refs[n:2 * n], refs[2 * n], refs[2 * n + 1])
        for cp in sends:
            cp.wait_send()
        for cp in arrivals:
            cp.wait_recv()

    res = pl.pallas_call(
        body, name=name, in_specs=[HBM_SPEC] * (2 * n) + [SEM_SPEC, SEM_SPEC, pl.BlockSpec(memory_space=pl.ANY)],
        out_shape=[pltpu.HBM(t.shape, t.dtype) for t in arrays], out_specs=[HBM_SPEC] * (2 * n),
        input_output_aliases={i: i for i in range(2 * n)}, compiler_params=SPLIT_COPY,
    )(*arrays, send_sems, recv_sems, after)
    return res[:n], res[n:]


def _gather_finish(partial, name):
    n = len(partial)

    def body(*refs):
        in_refs, out_refs = refs[:n], refs[n:2 * n]
        send_sems, recv_sems = refs[2 * n:]
        x, y, c = _position()
        chips = [(1 - x, y), (x, 1 - y), (1 - x, 1 - y)]
        copies = []
        for a in range(n):
            for j, (px, py) in enumerate(chips):
                cp = pltpu.make_async_remote_copy(
                    src_ref=in_refs[a].at[4 * px + 2 * py + c], dst_ref=out_refs[a].at[4 * px + 2 * py + c],
                    send_sem=send_sems.at[a, j], recv_sem=recv_sems.at[a, j], device_id=(x, y, 1 - c),
                    device_id_type=MESH)
                cp.start()
                copies.append(cp)
        for a in range(n):
            for j, (px, py) in enumerate(chips):
                pltpu.make_async_remote_copy(
                    src_ref=in_refs[a].at[4 * px + 2 * py + (1 - c)], dst_ref=out_refs[a].at[4 * px + 2 * py + (1 - c)],
                    send_sem=send_sems.at[a, j], recv_sem=recv_sems.at[a, j], device_id=(x, y, 1 - c),
                    device_id_type=MESH).wait_recv()
        for cp in copies:
            cp.wait_send()

    hbm = pl.BlockSpec(memory_space=pl.ANY)
    return pl.pallas_call(
        body, out_shape=[SDS(p.shape, p.dtype) for p in partial], in_specs=[hbm] * n, out_specs=[hbm] * n,
        input_output_aliases={a: a for a in range(n)},
        scratch_shapes=[pltpu.SemaphoreType.DMA((n, 3)), pltpu.SemaphoreType.DMA((n, 3))],
        name=name,
    )(*partial)


def _row_tile(rows):
    return 512 if rows % 512 == 0 and rows > 512 else rows


def _pair_add(g, r1, core, name):
    def body(c_ref, g_ref, r_ref, o_ref):
        o_ref[...] = (g_ref[...].astype(f32) + r_ref[...].astype(f32)).astype(o_ref.dtype)

    rows, cols = g.shape[1:]
    tile = _row_tile(rows)
    blk = (1, tile, cols)
    return pl.pallas_call(
        body, out_shape=SDS((4, rows, cols), g.dtype), name=name,
        grid_spec=pltpu.PrefetchScalarGridSpec(
            num_scalar_prefetch=1, grid=(4, rows // tile),
            in_specs=[pl.BlockSpec(blk, lambda k, i, c_ref: (2 * k + c_ref[0], i, 0)),
                      pl.BlockSpec(blk, lambda k, i, c_ref: (k, i, 0))],
            out_specs=pl.BlockSpec(blk, lambda k, i, c_ref: (k, i, 0))),
        compiler_params=_params(("parallel", "arbitrary")),
    )(core, g, r1)


def _chip_add(t, r2, chip, name):
    def body(c_ref, t_ref, r_ref, o_ref):
        o_ref[...] = ((t_ref[0].astype(f32) + r_ref[0].astype(f32)) + r_ref[1].astype(f32)) + r_ref[2].astype(f32)

    rows, cols = t.shape[1:]
    tile = _row_tile(rows)
    return pl.pallas_call(
        body, out_shape=SDS((rows, cols), f32), name=name,
        grid_spec=pltpu.PrefetchScalarGridSpec(
            num_scalar_prefetch=1, grid=(rows // tile,),
            in_specs=[pl.BlockSpec((1, tile, cols), lambda i, c_ref: (c_ref[0], i, 0)),
                      pl.BlockSpec((3, tile, cols), lambda i, c_ref: (0, i, 0))],
            out_specs=pl.BlockSpec((tile, cols), lambda i, c_ref: (i, 0))),
        compiler_params=_params(("arbitrary",)),
    )(chip, t, r2)


def _pad_to(t, axis, size):
    pads = [(0, 0)] * t.ndim
    pads[axis] = (0, size - t.shape[axis])
    return jnp.pad(t, pads)


_REF_COLS = {"qa": (0, FOX_W), "ka": (FOX_W, FOX_W), "va": (2 * FOX_W, FOX_W), "f": (3 * FOX_W, N_FOX_HEADS)}
_REF_COLS.update({n: (3 * FOX_W + N_FOX_HEADS + i * DIL_W, DIL_W) for i, n in enumerate(("qb", "kb", "vb"))})
_REF_COLS.update({n: (3 * FOX_W + N_FOX_HEADS + 3 * DIL_W + i * D, D) for i, n in enumerate(("ga", "gb"))})
_REF_ORDER = ("qa", "ka", "va", "f", "qb", "kb", "vb", "ga", "gb")


def _place_cols(sources, src_of, out_cols, name, row_block=512):
    arrays = [s[0] if isinstance(s, tuple) else s for s in sources]
    widths = [a.shape[-1] for a in arrays]
    rows = arrays[0].shape[-2]
    plan = []
    for t in range(out_cols // LANES):
        segs, c, end = [], t * LANES, (t + 1) * LANES
        while c < end:
            s = src_of(c)
            if s is None:
                c += 1
                continue
            n = 1
            while c + n < end and src_of(c + n) == (s[0], s[1] + n):
                n += 1
            segs.append((s[0], s[1], c - t * LANES, n))
            c += n
        plan.append(segs)

    def body(*refs):
        o_ref = refs[-1]
        for t, segs in enumerate(plan):
            acc = None
            for si, c0, o0, n in segs:
                a0 = c0 // LANES * LANES
                wide = min(2 * LANES, widths[si] - a0)
                win = refs[si][0, :, a0:a0 + wide] if isinstance(sources[si], tuple) else refs[si][:, a0:a0 + wide]
                r = lax.broadcasted_iota(jnp.int32, (wide, LANES), 0)
                c = lax.broadcasted_iota(jnp.int32, (wide, LANES), 1)
                pick = ((r - (c0 - a0) == c - o0) & (c >= o0) & (c < o0 + n)).astype(bf16)
                part = jnp.dot(win.astype(bf16), pick, preferred_element_type=f32)
                acc = part if acc is None else acc + part
            tile = jnp.zeros((row_block, LANES), f32) if acc is None else acc
            o_ref[:, t * LANES:(t + 1) * LANES] = tile.astype(o_ref.dtype)

    def spec(s):
        if isinstance(s, tuple):
            j = s[1]
            return pl.BlockSpec((1, row_block, s[0].shape[-1]), lambda i: (j, i, 0))
        return pl.BlockSpec((row_block, s.shape[-1]), lambda i: (i, 0))

    return pl.pallas_call(
        body, grid=(rows // row_block,), in_specs=[spec(s) for s in sources],
        out_specs=pl.BlockSpec((row_block, out_cols), lambda i: (i, 0)), out_shape=SDS((rows, out_cols), bf16),
        name=name, compiler_params=_params(("parallel",)),
    )(*arrays)


def _ref_piece(r):
    for name in _REF_ORDER:
        lo, width = _REF_COLS[name]
        if lo <= r < lo + width:
            return name, r - lo
    raise ValueError(r)


def _shard_pad_cols(pieces):
    names = [n for n in _REF_ORDER if n != "vb"]
    sources = [pieces[n] for n in names] + list(pieces["vb"])

    def src_of(c):
        j, i = divmod(c, W_IN_PAD)
        if i >= W_IN_SH:
            return None
        name, col = _ref_piece(j * W_IN_SH + i)
        if name == "vb":
            return len(names) + col // DIL_OUT_W, col % DIL_OUT_W
        return names.index(name), col

    return _place_cols(sources, src_of, N_DEV * W_IN_PAD, "place_dproj")


_SLABS = {"ga": C_GA, "gb": C_GB, "qb": C_QB, "kb": C_KB, "vb": C_VB, "qa": C_QA, "ka": C_KA, "va": C_VA, "f": C_F}


def _slab_w_in(stack):
    def src_of(c):
        for name, start in _SLABS.items():
            lo, width = _REF_COLS[name]
            if start <= c < start + width:
                return divmod(lo + c - start, W_IN_SH)
        return None

    return _place_cols([(stack, j) for j in range(N_DEV)], src_of, PROJ_W, "place_w_in")


def kernel(x, c, w_ada, b_ada, g_mix, w_in, b_fgate, w_br_a, w_br_b, w_out, g_ffn, w_ffn_gate, w_ffn_up, w_ffn_down, g_final, loss_target, m_w_ada, m_b_ada, m_g_mix, m_w_in, m_b_fgate, m_w_br_a, m_w_br_b, m_w_out, m_g_ffn, m_w_ffn_gate, m_w_ffn_up, m_w_ffn_down, m_g_final, v_w_ada, v_b_ada, v_g_mix, v_w_in, v_b_fgate, v_w_br_a, v_w_br_b, v_w_out, v_g_ffn, v_w_ffn_gate, v_w_ffn_up, v_w_ffn_down, v_g_final):
    px, py, pc = _position()
    dev = 4 * px + 2 * py + pc
    x2d, tgt = x[0], loss_target[0]

    c_all = _all_gather(c, "gather_c").reshape(N_DEV, D)
    ada_cols = w_ada.shape[2]
    b_shard = lax.dynamic_slice(b_ada, (0, dev * ada_cols), (1, ada_cols))
    mod_shard = _ada_fwd(c_all, w_ada[0], b_shard)
    mod_all = _all_gather(mod_shard, "gather_mod")
    modv = lax.dynamic_index_in_dim(mod_all, dev, axis=1, keepdims=False).reshape(6, D)

    gate_up = jnp.concatenate([_pad_to(w_ffn_gate[0], 1, FF_PAD), _pad_to(w_ffn_up[0], 1, FF_PAD)], axis=1)
    w_in_s, = _all_gather_many([_pad_to(w_in[0], 1, W_IN_PAD).astype(bf16)], "gather_w_in")
    later = [w_br_a[0], w_br_b[0], w_out[0], gate_up, _pad_to(w_ffn_down[0], 0, FF_PAD)]
    later_state, later_token = _gather_start([t.astype(bf16) for t in later], w_in_s, "gather_rest_start")
    w_in_p = _slab_w_in(w_in_s)

    h1 = _pre1(x2d, modv, g_mix)
    proj = _matmul(h1, w_in_p, name="mm_proj", tm=SEQ, tn=896, tk=D, after=later_token)
    b_pad = jnp.pad(b_fgate, ((0, 0), (0, LANES - N_FOX_HEADS)))
    q_aug, k_aug, va = _fox_prep(proj, _fox_gate_fwd(proj, b_pad))
    ya_h, lse_a = _fox_fwd(q_aug, k_aug, va)

    tables = _rope_tables()
    qb_r, kb_r, vb = _rope_fwd(proj, tables)
    by_group = [_dil_fwd(qb_r, kb_r, vb, grp) for grp in range(N_GROUPS)]
    yb_h, lse_b = _dil_combine([o for o, _ in by_group], [l for _, l in by_group])

    mine, arrived = _gather_wait(later_state, yb_h, "gather_rest_wait")
    w_a_s, w_b_s, w_o_s, w_gu_s, w_d_s = [
        lax.dynamic_update_slice(stack, block[None], (dev, 0, 0))
        for stack, block in zip(_gather_finish(arrived, "gather_rest_finish"), mine, strict=True)]
    w_o = w_o_s.reshape(D, D)
    w_d = w_d_s.reshape(FF_HID, D)
    ya = _matmul_stack(ya_h, w_a_s, name="mm_br_a")
    yb = _matmul_stack(yb_h, w_b_s, name="mm_br_b")

    merged = _merge_fwd(ya, yb, proj)
    mix = _matmul(merged, w_o, name="mm_out", tm=SEQ, tn=512, tk=D)
    x1, h2 = _post1(x2d, mix, modv, g_ffn)
    act, au = _ffn_in(h2, w_gu_s)
    ff = _matmul(act, w_d, name="mm_ffn_down", tm=SEQ // 2, tn=512, tk=FF_HID)

    dx2, dff, dg_final, dga_f, loss_lanes = _final(x1, ff, tgt, modv, g_final.reshape(1, D))
    dau = _ffn_bwd_in(dff, w_d_s, au)

    core = pc.astype(jnp.int32).reshape(1)
    chip = (2 * px + py).astype(jnp.int32).reshape(1)

    def pair_done(state, after, tags, name):
        mine, theirs = _exchange_wait("pair", state, after, "pair_wait_" + name)
        sums = [_pair_add(g, r, core, "pair_add_" + t) for g, r, t in zip(mine, theirs, tags)]
        return _exchange_start("chip", sums, "chip_start_" + name)

    def from_chips(state, after, tags, name):
        sums, got = _exchange_wait("chip", state, after, "chip_wait_" + name)
        return [_chip_add(p, r, chip, "chip_add_" + t) for p, r, t in zip(sums, got, tags)]

    g_gu = _matmul(h2, dau, ta=True, by_shard=True, out_dtype=bf16, name="mm_g_ffn_in", tm=D, tn=2 * FF_PAD, tk=SEQ)
    g_d = _matmul(act, dff, ta=True, out_dtype=bf16, name="mm_g_down", tm=FF_HID // 2, tn=512, tk=SEQ)
    ffn_tags = ["gu", "down"]
    ffn_pair, ffn_pair_token = _exchange_start("pair", [g_gu, g_d.reshape(N_DEV, FF_PAD, D)], "pair_start_ffn")

    dh2 = _matmul_nt_shards(dau, w_gu_s, ffn_pair_token, name="mm_d_h2")
    ffn_state, ffn_token = pair_done(ffn_pair, dh2, ffn_tags, "ffn")
    dx1, dmix, dsh_f, dsc_f, dg_ffn, dga_m = _mid_bwd(dh2, x1, dx2, mix, modv, g_ffn)
    dmerged = _matmul(dmix, w_o, tb=True, name="mm_d_merged", tm=SEQ, tn=512, tk=D, after=ffn_token)
    dya, dyb, dga, dgb = _merge_bwd(dmerged, ya, yb, proj)
    dya_h = _matmul_stack(dya, w_a_s, tb=True, name="mm_d_ya")
    dyb_h = _matmul_stack(dyb, w_b_s, tb=True, name="mm_d_yb")

    dqa, dka, dva, dF = _fox_bwd(q_aug, k_aug, va, dya_h, ya_h, lse_a)
    dF_row = jnp.pad(dF[:, :2, :].reshape(N_FOX_HEADS, SEQ), ((0, LANES - N_FOX_HEADS), (0, 0)))
    df, db_fgate = _fox_gate_bwd(dF_row, proj, b_pad)

    delta_b = _dil_delta(dyb_h, yb_h)
    dil_grads = [_dil_bwd(qb_r, kb_r, vb, dyb_h, lse_b, delta_b, grp) for grp in range(N_GROUPS)]
    dqb, dkb = _rope_bwd([t[0] for t in dil_grads], [t[1] for t in dil_grads], tables)

    dproj = _shard_pad_cols({"qa": dqa, "ka": dka, "va": dva, "f": df, "qb": dqb, "kb": dkb,
                             "vb": [t[2] for t in dil_grads], "ga": dga, "gb": dgb})
    g_in = _matmul(h1, dproj, ta=True, by_shard=True, out_dtype=bf16, name="mm_g_in", tm=D, tn=W_IN_PAD, tk=SEQ)
    g_o = _matmul(merged, dmix, ta=True, out_dtype=bf16, name="mm_g_out", tm=D, tn=512, tk=SEQ)
    g_a = _matmul_stack(ya_h, dya, ta=True, out_dtype=bf16, name="mm_g_br_a")
    g_b = _matmul_stack(yb_h, dyb, ta=True, out_dtype=bf16, name="mm_g_br_b")
    rows_a, rows_b = FOX_W * W_BR_SH // D, DIL_OUT_W * W_BR_SH // D
    g_small = jnp.concatenate([g_a.reshape(N_DEV, rows_a, D), g_b.reshape(N_DEV, rows_b, D),
                               g_o.reshape(N_DEV, W_BR_SH, D)], axis=1)
    mix_tags = ["in", "small"]
    mix_pair, mix_pair_token = _exchange_start("pair", [g_in, g_small], "pair_start_mixer")

    dh1 = _matmul_nt_shards(dproj, w_in_s, mix_pair_token, name="mm_d_h1")
    grad_x, dsh_m, dsc_m, dg_mix = _first_bwd(dh1, x2d, dx1, modv, g_mix)

    pad_lane = lambda t: jnp.pad(t, ((0, 0), (0, D - t.shape[1])))
    small = jnp.concatenate([dsh_m, dsc_m, dga_m, dsh_f, dsc_f, dga_f, dg_mix, dg_ffn, dg_final,
                             pad_lane(db_fgate), loss_lanes, jnp.zeros((SMALL_ROWS - 11, D), f32)], axis=0)
    small_all = _all_gather(small, "gather_small")
    mix_state, mix_token = pair_done(mix_pair, small_all, mix_tags, "mixer")

    small_sum, loss_row = _small_reduce(small_all, mix_token)
    dmod_all = small_all[:, :6, :].reshape(N_DEV, 6 * D)
    g_w_ada = _ada_bwd(c_all, lax.dynamic_slice(dmod_all, (0, dev * ada_cols), (N_DEV, ada_cols)))
    s_gu, s_d = from_chips(ffn_state, small_sum, ffn_tags, "ffn")

    loss = loss_row[0, 0]
    g = {
        "w_ada": g_w_ada[None], "b_ada": small_sum[0:6].reshape(1, 6 * D), "g_mix": small_sum[6:7],
        "b_fgate": small_sum[9:10, :N_FOX_HEADS], "g_ffn": small_sum[7:8], "w_ffn_gate": s_gu[None, :, :W_FF_SH],
        "w_ffn_up": s_gu[None, :, FF_PAD:FF_PAD + W_FF_SH], "w_ffn_down": s_d[None, :W_FF_SH],
        "g_final": small_sum[8],
    }
    w = {"w_ada": w_ada, "b_ada": b_ada, "g_mix": g_mix, "w_in": w_in, "b_fgate": b_fgate, "w_br_a": w_br_a,
         "w_br_b": w_br_b, "w_out": w_out, "g_ffn": g_ffn, "w_ffn_gate": w_ffn_gate, "w_ffn_up": w_ffn_up,
         "w_ffn_down": w_ffn_down, "g_final": g_final}
    m = {"w_ada": m_w_ada, "b_ada": m_b_ada, "g_mix": m_g_mix, "w_in": m_w_in, "b_fgate": m_b_fgate,
         "w_br_a": m_w_br_a, "w_br_b": m_w_br_b, "w_out": m_w_out, "g_ffn": m_g_ffn, "w_ffn_gate": m_w_ffn_gate,
         "w_ffn_up": m_w_ffn_up, "w_ffn_down": m_w_ffn_down, "g_final": m_g_final}
    v = {"w_ada": v_w_ada, "b_ada": v_b_ada, "g_mix": v_g_mix, "w_in": v_w_in, "b_fgate": v_b_fgate,
         "w_br_a": v_w_br_a, "w_br_b": v_w_br_b, "w_out": v_w_out, "g_ffn": v_g_ffn, "w_ffn_gate": v_w_ffn_gate,
         "w_ffn_up": v_w_ffn_up, "w_ffn_down": v_w_ffn_down, "g_final": v_g_final}
    names = list(w)
    delta, new_m, new_v = {}, {}, {}

    transposed = ("w_in", "w_ffn_gate", "w_ffn_up")

    def update(n):
        shape = w[n].shape
        if n in transposed:
            g_t = g[n][0].T
            dl, mn, vn = _adamw(w[n][0].T, g_t, m[n][0].T, v[n][0].T, "adamw_" + n)
            g[n], delta[n], new_m[n], new_v[n] = g_t.T[None], dl.T[None], mn.T[None], vn.T[None]
            return
        two_d = (lambda t: t.reshape(shape[-2:])) if len(shape) == 3 else (lambda t: t)
        dl, mn, vn = _adamw(two_d(w[n]), two_d(g[n]), two_d(m[n]), two_d(v[n]), "adamw_" + n)
        delta[n], new_m[n], new_v[n] = dl.reshape(shape), mn.reshape(shape), vn.reshape(shape)

    for n in list(g):
        update(n)
    done = sum(delta[n].reshape(-1)[:N_FOX_HEADS] for n in g)
    s_in, s_small = from_chips(mix_state, done, mix_tags, "mixer")
    g.update({"w_in": s_in[None, :, :W_IN_SH], "w_br_a": s_small[:rows_a].reshape(1, FOX_W, W_BR_SH),
              "w_br_b": s_small[rows_a:rows_a + rows_b].reshape(1, DIL_OUT_W, W_BR_SH),
              "w_out": s_small[None, rows_a + rows_b:]})
    for n in ("w_in", "w_br_a", "w_br_b", "w_out"):
        update(n)

    return (loss, grad_x[None], *[g[n] for n in names], *[delta[n] for n in names],
            *[new_m[n] for n in names], *[new_v[n] for n in names])
```

```python
import functools

import jax
import jax.numpy as jnp
from jax import lax
from jax.experimental import pallas as pl
from jax.experimental.pallas import tpu as pltpu

f32 = jnp.float32
bf16 = jnp.bfloat16
SDS = jax.ShapeDtypeStruct
MESH = pl.DeviceIdType.MESH

N_DEV = 8
D = 1024
SEQ = 2048
HEAD_DIM = 64
N_FOX_HEADS = 8
FOX_W = 512
DIL_W = 768
DIL_OUT_W = 256
ROT_DIM = 16
ROPE_THETA = 500000.0
D_FF = 2816
IN_COLS = 5896
EPS = 1e-6
NEG = -1e30
ATT_SCALE = HEAD_DIM ** -0.5

ADAM_LR = 0.001
ADAM_B1 = 0.9
ADAM_B2 = 0.999
ADAM_EPS = 1e-08
ADAM_WD = 0.01
ADAM_STEP = 10

C_GA, C_GB, C_QB, C_KB, C_VB, C_QA, C_KA, C_VA, C_F = 0, 1024, 2304, 3072, 3840, 4608, 5120, 5632, 6144
PROJ_W = 6272
LANES = 128
VMEM_LIMIT = 52 * 1024 * 1024

W_IN_SH, W_IN_PAD = IN_COLS // N_DEV, 768
W_BR_SH = D // N_DEV
W_FF_SH, FF_PAD = D_FF // N_DEV, 384
FF_HID = N_DEV * FF_PAD
SMALL_ROWS = 16


def _params(sem=None):
    if sem is None:
        return pltpu.CompilerParams(vmem_limit_bytes=VMEM_LIMIT)
    return pltpu.CompilerParams(dimension_semantics=sem, vmem_limit_bytes=VMEM_LIMIT)


def _rowwise(fn, name, tiled, vecs, outs, reds=(), tile=256):
    nt, nv, no = len(tiled), len(vecs), len(outs)
    rows = tiled[0][0].shape[0]
    assert rows % tile == 0

    def body(*refs):
        tin = [r[...] for r in refs[:nt]]
        vin = [r[...] for r in refs[nt:nt + nv]]
        orefs = refs[nt + nv:nt + nv + no]
        rrefs = refs[nt + nv + no:]
        touts, routs = fn(tin, vin)
        for r, t in zip(orefs, touts, strict=True):
            r[...] = t.astype(r.dtype)
        if rrefs:
            @pl.when(pl.program_id(0) == 0)
            def _():
                for r in rrefs:
                    r[...] = jnp.zeros_like(r)
            for r, t in zip(rrefs, routs, strict=True):
                r[...] += t

    def col_map(cb):
        return lambda i: (i, cb)

    def whole_map(nd):
        return lambda i: (0,) * nd

    in_specs = [pl.BlockSpec((tile, w), col_map(cb)) for (_, w, cb) in tiled]
    in_specs += [pl.BlockSpec(v.shape, whole_map(v.ndim)) for v in vecs]
    out_specs = [pl.BlockSpec((tile, w), lambda i: (i, 0)) for (w, _) in outs]
    out_specs += [pl.BlockSpec((1, w), lambda i: (0, 0)) for w in reds]
    out_shape = [SDS((rows, w), dt) for (w, dt) in outs] + [SDS((1, w), f32) for w in reds]
    res = pl.pallas_call(
        body, grid=(rows // tile,), in_specs=in_specs, out_specs=out_specs, out_shape=out_shape, name=name,
        compiler_params=_params(("arbitrary",)),
    )(*[t[0] for t in tiled], *vecs)
    return res


def _matmul(a, b, *, ta=False, tb=False, out_dtype=f32, name, tm, tn, tk, by_shard=False, after=None):
    m, k = (a.shape[1], a.shape[0]) if ta else a.shape
    if by_shard and not ta:
        n, kb = (b.shape[1], N_DEV * b.shape[2]) if tb else (N_DEV * b.shape[2], b.shape[1])
        assert (tk if tb else tn) == b.shape[2]
    else:
        n, kb = (b.shape[0], b.shape[1]) if tb else (b.shape[1], b.shape[0])
    assert kb == k and m % tm == 0 and n % tn == 0 and k % tk == 0
    nk = k // tk
    dims = (((0 if ta else 1,), (1 if tb else 0,)), ((), ()))
    b_stacked = by_shard and not ta
    o_stacked = by_shard and ta

    def body(a_ref, b_ref, *rest):
        o_ref, *acc = rest[1:] if after is not None else rest
        bv = b_ref[0] if b_stacked else b_ref[...]
        p = lax.dot_general(a_ref[...].astype(bf16), bv.astype(bf16), dims, preferred_element_type=f32)

        def put(val):
            if o_stacked:
                o_ref[0] = val.astype(o_ref.dtype)
            else:
                o_ref[...] = val.astype(o_ref.dtype)

        if nk == 1:
            put(p)
        else:
            acc_ref, = acc
            kk = pl.program_id(2)

            @pl.when(kk == 0)
            def _():
                acc_ref[...] = p

            @pl.when(kk > 0)
            def _():
                acc_ref[...] += p

            @pl.when(kk == nk - 1)
            def _():
                put(acc_ref[...])

    a_spec = pl.BlockSpec((tk, tm), lambda i, j, kk: (kk, i)) if ta else pl.BlockSpec((tm, tk), lambda i, j, kk: (i, kk))
    if b_stacked and tb:
        b_spec = pl.BlockSpec((1, tn, tk), lambda i, j, kk: (kk, j, 0))
    elif b_stacked:
        b_spec = pl.BlockSpec((1, tk, tn), lambda i, j, kk: (j, kk, 0))
    elif tb:
        b_spec = pl.BlockSpec((tn, tk), lambda i, j, kk: (j, kk))
    else:
        b_spec = pl.BlockSpec((tk, tn), lambda i, j, kk: (kk, j))
    if o_stacked:
        assert tn == n // N_DEV
        out_spec = pl.BlockSpec((1, tm, tn), lambda i, j, kk: (j, i, 0))
        out_shape = SDS((N_DEV, m, tn), out_dtype)
    else:
        out_spec = pl.BlockSpec((tm, tn), lambda i, j, kk: (i, j))
        out_shape = SDS((m, n), out_dtype)
    extra_specs, extra = ([pl.BlockSpec(memory_space=pl.ANY)], [after]) if after is not None else ([], [])
    return pl.pallas_call(
        body, grid=(m // tm, n // tn, nk), in_specs=[a_spec, b_spec] + extra_specs, out_specs=out_spec,
        out_shape=out_shape, name=name, scratch_shapes=[pltpu.VMEM((tm, tn), f32)] if nk > 1 else [],
        compiler_params=_params(("parallel", "parallel", "arbitrary")),
    )(a, b, *extra)


def _matmul_stack(a, b, *, ta=False, tb=False, out_dtype=f32, name):
    def lanes(ref):
        return jnp.concatenate([ref[j] for j in range(N_DEV)], axis=1).astype(bf16)

    if ta:
        w = b.shape[1] // N_DEV

        def body(a_ref, b_ref, o_ref):
            p = _tn(a_ref[...].astype(bf16), b_ref[...].astype(bf16))
            for j in range(N_DEV):
                o_ref[j] = p[:, j * w:(j + 1) * w].astype(o_ref.dtype)

        return pl.pallas_call(body, out_shape=SDS((N_DEV, a.shape[1], w), out_dtype), name=name,
                              compiler_params=_params())(a, b)

    m, half = a.shape[0], a.shape[0] // 2
    n = b.shape[1] if tb else N_DEV * b.shape[2]

    def body(a_ref, b_ref, o_ref):
        av = a_ref[...].astype(bf16)
        o_ref[...] = (_nt(av, lanes(b_ref)) if tb else jnp.dot(av, lanes(b_ref), preferred_element_type=f32)
                      ).astype(o_ref.dtype)

    return pl.pallas_call(
        body, grid=(2,), in_specs=[pl.BlockSpec((half, a.shape[1]), lambda i: (i, 0)),
                                   pl.BlockSpec(b.shape, lambda i: (0, 0, 0))],
        out_specs=pl.BlockSpec((half, n), lambda i: (i, 0)), out_shape=SDS((m, n), out_dtype), name=name,
        compiler_params=_params(("parallel",)),
    )(a, b)


def _matmul_nt_shards(a, b, after, *, name, tm=512, tn=1024):
    m, n, w = a.shape[0], b.shape[1], b.shape[2]
    assert a.shape[1] == N_DEV * w and m % tm == 0 and n % tn == 0

    def body(a_ref, b_ref, after_ref, o_ref):
        acc = _nt(a_ref[:, 0:w], b_ref[0])
        for j in range(1, N_DEV):
            acc = acc + _nt(a_ref[:, j * w:(j + 1) * w], b_ref[j])
        o_ref[...] = acc

    return pl.pallas_call(
        body, grid=(n // tn, m // tm),
        in_specs=[pl.BlockSpec((tm, N_DEV * w), lambda j, i: (i, 0)), pl.BlockSpec((N_DEV, tn, w), lambda j, i: (0, j, 0)),
                  pl.BlockSpec(memory_space=pl.ANY)],
        out_specs=pl.BlockSpec((tm, tn), lambda j, i: (i, j)), out_shape=SDS((m, n), f32), name=name,
        compiler_params=_params(("parallel", "parallel")),
    )(a, b, after)


def _rms(x):
    r = lax.rsqrt(jnp.mean(x * x, axis=-1, keepdims=True) + EPS)
    return r, x * r


def _rms_bwd(r, xn, dxn):
    return r * (dxn - xn * jnp.mean(dxn * xn, axis=-1, keepdims=True))


def _colsum(t):
    return jnp.sum(t, axis=0, keepdims=True)


def _sigmoid(x):
    return 1.0 / (1.0 + jnp.exp(-x))


def _modulated_norm(x, g, shift, scale):
    _, xn = _rms(x)
    return (xn * g) * (1.0 + scale) + shift


def _pre1(x, modv, g_mix):
    def fn(t, v):
        (xt,), (mv, g) = t, v
        return [_modulated_norm(xt, g, mv[0:1], mv[1:2])], []
    return _rowwise(fn, "pre1", [(x, D, 0)], [modv, g_mix], [(D, bf16)])[0]


def _post1(x, mix, modv, g_ffn):
    def fn(t, v):
        (xt, mt), (mv, g) = t, v
        x1 = xt + mv[2:3] * mt
        return [x1, _modulated_norm(x1, g, mv[3:4], mv[4:5])], []
    return _rowwise(fn, "post1", [(x, D, 0), (mix, D, 0)], [modv, g_ffn], [(D, f32), (D, bf16)])


def _ffn_in(h, w_stack):
    def body(h_ref, w_ref, act_ref, au_ref):
        p = jnp.dot(h_ref[...], w_ref[0], preferred_element_type=f32)
        a, u = p[:, :FF_PAD], p[:, FF_PAD:]
        act_ref[...] = (a * _sigmoid(a) * u).astype(act_ref.dtype)
        au_ref[...] = p.astype(au_ref.dtype)

    return pl.pallas_call(
        body, grid=(N_DEV,),
        in_specs=[pl.BlockSpec((SEQ, D), lambda j: (0, 0)), pl.BlockSpec((1, D, 2 * FF_PAD), lambda j: (j, 0, 0))],
        out_specs=[pl.BlockSpec((SEQ, FF_PAD), lambda j: (0, j)), pl.BlockSpec((SEQ, 2 * FF_PAD), lambda j: (0, j))],
        out_shape=(SDS((SEQ, FF_HID), bf16), SDS((SEQ, 2 * FF_HID), bf16)), name="ffn_in",
        compiler_params=_params(("parallel",)),
    )(h, w_stack)


def _ffn_bwd_in(dff, w_down_stack, au):
    def body(d_ref, w_ref, au_ref, o_ref):
        dact = _nt(d_ref[...], w_ref[0])
        p = au_ref[...].astype(f32)
        a, u = p[:, :FF_PAD], p[:, FF_PAD:]
        sg = _sigmoid(a)
        o_ref[...] = jnp.concatenate([dact * u * (sg * (1.0 + a * (1.0 - sg))), dact * (a * sg)],
                                     axis=1).astype(o_ref.dtype)

    return pl.pallas_call(
        body, grid=(N_DEV,),
        in_specs=[pl.BlockSpec((SEQ, D), lambda j: (0, 0)), pl.BlockSpec((1, FF_PAD, D), lambda j: (j, 0, 0)),
                  pl.BlockSpec((SEQ, 2 * FF_PAD), lambda j: (0, j))],
        out_specs=pl.BlockSpec((SEQ, 2 * FF_PAD), lambda j: (0, j)),
        out_shape=SDS((SEQ, 2 * FF_HID), bf16), name="ffn_bwd_in", compiler_params=_params(("parallel",)),
    )(dff, w_down_stack, au)


def _final(x1, ff, target, modv, g_final):
    def fn(t, v):
        (x1t, fft, tgt), (mv, g) = t, v
        x2 = x1t + mv[5:6] * fft
        r, xn = _rms(x2)
        err = xn * g - tgt
        dy = err * (1.0 / D)
        dx2 = _rms_bwd(r, xn, dy * g)
        return [dx2, dx2 * mv[5:6]], [_colsum(dy * xn), _colsum(dx2 * fft), _colsum(err * err) * (0.5 / D)]
    return _rowwise(fn, "final", [(x1, D, 0), (ff, D, 0), (target, D, 0)], [modv, g_final],
                    [(D, f32), (D, bf16)], [D, D, D])


def _mid_bwd(dh2, x1, dx2, mix, modv, g_ffn):
    def fn(t, v):
        (dh, x1t, dx2t, mt), (mv, g) = t, v
        r, xn = _rms(x1t)
        dn = dh * (1.0 + mv[4:5])
        dx1 = dx2t + _rms_bwd(r, xn, dn * g)
        return [dx1, dx1 * mv[2:3]], [_colsum(dh), _colsum(dh * (xn * g)), _colsum(dn * xn), _colsum(dx1 * mt)]
    return _rowwise(fn, "mid_bwd", [(dh2, D, 0), (x1, D, 0), (dx2, D, 0), (mix, D, 0)], [modv, g_ffn],
                    [(D, f32), (D, bf16)], [D, D, D, D])


def _first_bwd(dh1, x, dx1, modv, g_mix):
    def fn(t, v):
        (dh, xt, dx1t), (mv, g) = t, v
        r, xn = _rms(xt)
        dn = dh * (1.0 + mv[1:2])
        return [dx1t + _rms_bwd(r, xn, dn * g)], [_colsum(dh), _colsum(dh * (xn * g)), _colsum(dn * xn)]
    return _rowwise(fn, "first_bwd", [(dh1, D, 0), (x, D, 0), (dx1, D, 0)], [modv, g_mix], [(D, f32)], [D, D, D])


def _merge_fwd(ya, yb, proj):
    def fn(t, v):
        ya_t, yb_t, ga, gb = t
        return [_sigmoid(ga) * ya_t + _sigmoid(gb) * yb_t], []
    return _rowwise(fn, "merge_fwd", [(ya, D, 0), (yb, D, 0), (proj, D, C_GA // D), (proj, D, C_GB // D)], [],
                    [(D, bf16)])[0]


def _merge_bwd(dmerged, ya, yb, proj):
    def fn(t, v):
        dm, ya_t, yb_t, ga, gb = t
        sa, sb = _sigmoid(ga), _sigmoid(gb)
        return [dm * sa, dm * sb, dm * ya_t * (sa * (1.0 - sa)), dm * yb_t * (sb * (1.0 - sb))], []
    return _rowwise(fn, "merge_bwd",
                    [(dmerged, D, 0), (ya, D, 0), (yb, D, 0), (proj, D, C_GA // D), (proj, D, C_GB // D)], [],
                    [(D, bf16), (D, bf16), (D, bf16), (D, bf16)])


def _rope_tables():
    half = ROT_DIM // 2
    pos = jnp.arange(SEQ, dtype=f32)
    inv_freq = ROPE_THETA ** (-jnp.arange(0, ROT_DIM, 2, dtype=f32) / ROT_DIM)
    ang = pos[:, None] * inv_freq[None, :]
    cos, sin = jnp.cos(ang), jnp.sin(ang)
    pad = jnp.zeros((SEQ, HEAD_DIM - ROT_DIM), f32)
    zero = jnp.zeros((SEQ, half), f32)
    c_head = jnp.concatenate([cos, cos, pad + 1.0], axis=1)
    lo_head = jnp.concatenate([-sin, zero, pad], axis=1)
    hi_head = jnp.concatenate([zero, sin, pad], axis=1)
    return tuple(jnp.concatenate([t, t], axis=1) for t in (c_head, lo_head, hi_head))


def _over_heads(tables):
    return [jnp.tile(t, (1, DIL_W // LANES)) for t in tables]


def _rope_fwd(proj, tables):
    half = ROT_DIM // 2

    def fn(t, v):
        q, k, vv = t[:3]
        c, lo, hi = _over_heads(t[3:])
        rot = lambda z: z * c + pltpu.roll(z, DIL_W - half, 1) * lo + pltpu.roll(z, half, 1) * hi
        return [rot(q) * ATT_SCALE, rot(k), vv], []
    return _rowwise(fn, "rope_fwd", [(proj, DIL_W, C_QB // DIL_W), (proj, DIL_W, C_KB // DIL_W),
                                     (proj, DIL_W, C_VB // DIL_W)] + [(tb, LANES, 0) for tb in tables], [],
                    [(DIL_W, f32)] * 3)


def _rope_bwd(dqs, dks, tables):
    half = ROT_DIM // 2

    def fn(t, v):
        dq_t, dk_t = jnp.concatenate(t[:N_GROUPS], axis=1), jnp.concatenate(t[N_GROUPS:2 * N_GROUPS], axis=1)
        c, lo, hi = _over_heads(t[2 * N_GROUPS:])
        rot_t = lambda z: z * c + pltpu.roll(z * lo, half, 1) + pltpu.roll(z * hi, DIL_W - half, 1)
        return [rot_t(dq_t), rot_t(dk_t)], []
    return _rowwise(fn, "rope_bwd", [(a, DIL_OUT_W, 0) for a in (*dqs, *dks)] + [(tb, LANES, 0) for tb in tables],
                    [], [(DIL_W, bf16), (DIL_W, bf16)])


def _head_bcast_sum(d):
    lane = lax.broadcasted_iota(jnp.int32, d.shape, 1)
    out = jnp.zeros_like(d)
    for h in range(d.shape[1] // HEAD_DIM):
        sel = (lane >= h * HEAD_DIM) & (lane < (h + 1) * HEAD_DIM)
        out = jnp.where(sel, jnp.sum(jnp.where(sel, d, 0.0), axis=1, keepdims=True), out)
    return out


def _dil_combine(outs, lses):
    def fn(t, v):
        o0, o1, o2, l0, l1, l2 = t
        m = jnp.maximum(jnp.maximum(l0, l1), l2)
        w0, w1, w2 = jnp.exp(l0 - m), jnp.exp(l1 - m), jnp.exp(l2 - m)
        tot = w0 + w1 + w2
        return [(w0 * o0 + w1 * o1 + w2 * o2) / tot, m + jnp.log(tot)], []
    w = DIL_OUT_W
    return _rowwise(fn, "dil_combine", [(t, w, 0) for t in (*outs, *lses)], [], [(w, f32), (w, f32)])


def _dil_delta(dyb_h, yb_h):
    def fn(t, v):
        return [_head_bcast_sum(t[0] * t[1])], []
    return _rowwise(fn, "dil_delta", [(dyb_h, DIL_OUT_W, 0), (yb_h, DIL_OUT_W, 0)], [], [(DIL_OUT_W, f32)])[0]


def _adamw_math(wt, gt, mt, vt):
    mn = ADAM_B1 * mt + (1.0 - ADAM_B1) * gt
    vn = ADAM_B2 * vt + (1.0 - ADAM_B2) * (gt * gt)
    m_hat = mn / (1.0 - ADAM_B1 ** ADAM_STEP)
    v_hat = vn / (1.0 - ADAM_B2 ** ADAM_STEP)
    return -ADAM_LR * (m_hat / (jnp.sqrt(v_hat) + ADAM_EPS) + ADAM_WD * wt), mn, vn


def _adamw(w, g, m, v, name):
    shape = w.shape
    if w.ndim == 1:
        w, g, m, v = (t.reshape(1, -1) for t in (w, g, m, v))
    rows, cols = w.shape
    if rows % 8 and rows > 8:
        return _adamw_by_cols(w, g, m, v, name)
    tile = 256 if rows % 256 == 0 and rows > 512 else rows

    def fn(t, _):
        return list(_adamw_math(*t)), []
    delta, mn, vn = _rowwise(fn, name, [(w, cols, 0), (g, cols, 0), (m, cols, 0), (v, cols, 0)], [],
                             [(cols, f32)] * 3, tile=tile)
    return delta.reshape(shape), mn.reshape(shape), vn.reshape(shape)


def _adamw_by_cols(w, g, m, v, name, tile=256):
    rows, cols = w.shape

    def body(w_ref, g_ref, m_ref, v_ref, d_ref, mn_ref, vn_ref):
        d_ref[...], mn_ref[...], vn_ref[...] = _adamw_math(w_ref[...], g_ref[...], m_ref[...], v_ref[...])

    spec = pl.BlockSpec((rows, tile), lambda j: (0, j))
    return pl.pallas_call(body, grid=(cols // tile,), in_specs=[spec] * 4, out_specs=[spec] * 3,
                          out_shape=[SDS((rows, cols), f32)] * 3, name=name,
                          compiler_params=_params(("parallel",)))(w, g, m, v)


def _ada_fwd(c_all, w_shard, b_shard):
    def body(c_ref, w_ref, b_ref, o_ref):
        cv = c_ref[...]
        sc = (cv * _sigmoid(cv)).astype(bf16)
        o_ref[...] = jnp.dot(sc, w_ref[...].astype(bf16), preferred_element_type=f32) + b_ref[...]
    return pl.pallas_call(body, out_shape=SDS((N_DEV, w_shard.shape[1]), f32), name="ada_fwd",
                          compiler_params=_params())(c_all, w_shard, b_shard)


def _ada_bwd(c_all, dmod_cols):
    def body(c_ref, d_ref, o_ref):
        cv = c_ref[...]
        sc = cv * _sigmoid(cv)
        o_ref[...] = lax.dot_general(sc, d_ref[...], (((0,), (0,)), ((), ())), precision=lax.Precision.HIGHEST,
                                     preferred_element_type=f32)
    return pl.pallas_call(body, out_shape=SDS((D, dmod_cols.shape[1]), f32), name="ada_bwd",
                          compiler_params=_params())(c_all, dmod_cols)


def _small_reduce(gathered, after):
    def body(g_ref, after_ref, o_ref, loss_ref):
        acc = g_ref[0]
        for d in range(1, N_DEV):
            acc = acc + g_ref[d]
        o_ref[...] = acc
        loss_ref[...] = jnp.zeros((1, LANES), f32) + jnp.sum(acc[10:11, :])
    return pl.pallas_call(body, out_shape=(SDS((SMALL_ROWS, D), f32), SDS((1, LANES), f32)), name="small_reduce",
                          in_specs=[pl.BlockSpec(memory_space=pltpu.VMEM), pl.BlockSpec(memory_space=pl.ANY)],
                          compiler_params=_params())(gathered, after)


FOX_BLK = 512
CUM_BLK = 128


def _fold_lanes(t, op):
    out = t[:, :LANES]
    for j in range(1, t.shape[1] // LANES):
        out = op(out, t[:, j * LANES:(j + 1) * LANES])
    return out


def _fox_gate_fwd(proj, b_pad):
    nblk = SEQ // CUM_BLK

    def body(f_ref, b_ref, col_ref):
        r = lax.broadcasted_iota(jnp.int32, (CUM_BLK, CUM_BLK), 0)
        c = lax.broadcasted_iota(jnp.int32, (CUM_BLK, CUM_BLK), 1)
        tri = (r >= c).astype(f32)
        carry = jnp.zeros((1, LANES), f32)
        for blk in range(nblk):
            z = f_ref[blk * CUM_BLK:(blk + 1) * CUM_BLK, :] + b_ref[...]
            logf = jnp.minimum(z, 0.0) - jnp.log1p(jnp.exp(-jnp.abs(z)))
            cs = jnp.dot(tri, logf, precision=lax.Precision.HIGHEST, preferred_element_type=f32) + carry
            col_ref[blk * CUM_BLK:(blk + 1) * CUM_BLK, :] = cs
            carry = cs[CUM_BLK - 1:CUM_BLK, :]

    return pl.pallas_call(
        body, grid=(1,), in_specs=[pl.BlockSpec((SEQ, LANES), lambda i: (0, C_F // LANES)),
                                   pl.BlockSpec((1, LANES), lambda i: (0, 0))],
        out_specs=pl.BlockSpec((SEQ, LANES), lambda i: (0, 0)),
        out_shape=SDS((SEQ, LANES), f32), name="fox_gate_fwd",
        compiler_params=_params(("arbitrary",)),
    )(proj, b_pad)


def _fox_gate_bwd(dF_row, proj, b_pad):
    nblk = SEQ // CUM_BLK

    def body(d_ref, f_ref, b_ref, df_ref, db_ref, col_ref):
        r = lax.broadcasted_iota(jnp.int32, (CUM_BLK, CUM_BLK), 0)
        c = lax.broadcasted_iota(jnp.int32, (CUM_BLK, CUM_BLK), 1)
        tri = (r <= c).astype(f32)
        lane = lax.broadcasted_iota(jnp.int32, (CUM_BLK, LANES), 1)
        col_ref[...] = d_ref[...].T
        carry = jnp.zeros((1, LANES), f32)
        total = jnp.zeros((1, LANES), f32)
        for blk in reversed(range(nblk)):
            rows = slice(blk * CUM_BLK, (blk + 1) * CUM_BLK)
            cs = jnp.dot(tri, col_ref[rows, :], precision=lax.Precision.HIGHEST, preferred_element_type=f32) + carry
            carry = cs[0:1, :]
            z = f_ref[rows, :] + b_ref[...]
            df = jnp.where(lane < N_FOX_HEADS, cs * _sigmoid(-z), 0.0)
            df_ref[rows, :] = df.astype(df_ref.dtype)
            total = total + _colsum(df)
        db_ref[...] = total

    return pl.pallas_call(
        body, grid=(1,), in_specs=[pl.BlockSpec((LANES, SEQ), lambda i: (0, 0)),
                                   pl.BlockSpec((SEQ, LANES), lambda i: (0, C_F // LANES)),
                                   pl.BlockSpec((1, LANES), lambda i: (0, 0))],
        out_specs=[pl.BlockSpec((SEQ, LANES), lambda i: (0, 0)), pl.BlockSpec((1, LANES), lambda i: (0, 0))],
        out_shape=(SDS((SEQ, LANES), bf16), SDS((1, LANES), f32)), name="fox_gate_bwd",
        scratch_shapes=[pltpu.VMEM((SEQ, LANES), f32)],
        compiler_params=_params(("arbitrary",)),
    )(dF_row, proj, b_pad)


def _nt(a, b):
    return lax.dot_general(a, b, (((1,), (1,)), ((), ())), preferred_element_type=f32)


def _tn(a, b):
    return lax.dot_general(a, b, (((0,), (0,)), ((), ())), preferred_element_type=f32)


def _fox_prep(proj, f_col):
    def fn(t, v):
        q, k, vv, fc = t
        lane = lax.broadcasted_iota(jnp.int32, (q.shape[0], LANES), 1)
        qs, ks = [], []
        for h in range(N_FOX_HEADS):
            pair, pos = divmod(h, 2)
            own = (lane >= pos * HEAD_DIM) & (lane < (pos + 1) * HEAD_DIM)
            base = (1 - pos) * HEAD_DIM
            f = fc[:, h:h + 1]
            hi = f.astype(bf16).astype(f32)
            mid = (f - hi).astype(bf16).astype(f32)
            lo = (f - hi) - mid
            one = jnp.ones_like(f)
            qa = jnp.where(own, q[:, pair * LANES:(pair + 1) * LANES] * ATT_SCALE, 0.0)
            ka = k[:, pair * LANES:(pair + 1) * LANES]
            for idx, (qv, kv) in enumerate([(hi, one), (mid, one), (lo, one), (one, -hi), (one, -mid), (one, -lo)]):
                sel = lane == base + idx
                qa = jnp.where(sel, qv, qa)
                ka = jnp.where(sel, kv, ka)
            qs.append(qa)
            ks.append(ka)
        return [jnp.concatenate(qs, axis=1), jnp.concatenate(ks, axis=1), vv], []
    w = N_FOX_HEADS * LANES
    return _rowwise(fn, "fox_prep", [(proj, FOX_W, C_QA // FOX_W), (proj, FOX_W, C_KA // FOX_W),
                                     (proj, FOX_W, C_VA // FOX_W), (f_col, LANES, 0)], [],
                    [(w, bf16), (w, bf16), (FOX_W, bf16)])


def _fox_fwd(q_aug, k_aug, v):
    blk = FOX_BLK
    npair = FOX_W // LANES

    def body(q_ref, k_ref, v_ref, o_ref, lse_ref, s_scr):
        i = pl.program_id(1)
        tri = lax.broadcasted_iota(jnp.int32, (blk, blk), 0) >= lax.broadcasted_iota(jnp.int32, (blk, blk), 1)
        qh = [q_ref[:, h * LANES:(h + 1) * LANES] for h in range(2)]

        def logits(c, masked):
            off = pl.multiple_of(c * blk, blk)
            tops = []
            for h in range(2):
                s = _nt(qh[h], k_ref[pl.ds(off, blk), h * LANES:(h + 1) * LANES])
                if masked:
                    s = jnp.where(tri, s, NEG)
                s_scr[h, :, pl.ds(off, blk)] = s
                tops.append(_fold_lanes(s, jnp.maximum))
            return tops

        def pass_a(c, m):
            return tuple(jnp.maximum(a, b) for a, b in zip(m, logits(c, False)))

        m = lax.fori_loop(0, i, pass_a, tuple(jnp.full((blk, LANES), NEG, f32) for _ in range(2)))
        mx = [jnp.max(jnp.maximum(a, b), axis=1, keepdims=True) for a, b in zip(m, logits(i, True))]

        def pass_b(c, carry):
            off = pl.multiple_of(c * blk, blk)
            vv = v_ref[pl.ds(off, blk), :]
            new = []
            for h in range(2):
                l, acc = carry[h]
                p = jnp.exp(s_scr[h, :, pl.ds(off, blk)] - mx[h])
                hi = p.astype(bf16)
                lo = (p - hi.astype(f32)).astype(bf16)
                new.append((l + _fold_lanes(p, jnp.add),
                            acc + jnp.dot(hi, vv, preferred_element_type=f32)
                            + jnp.dot(lo, vv, preferred_element_type=f32)))
            return tuple(new)

        zero = jnp.zeros((blk, LANES), f32)
        (l_a, acc_a), (l_b, acc_b) = lax.fori_loop(0, i + 1, pass_b, ((zero, zero), (zero, zero)))
        l_a = jnp.sum(l_a, axis=1, keepdims=True)
        l_b = jnp.sum(l_b, axis=1, keepdims=True)
        first = lax.broadcasted_iota(jnp.int32, (blk, LANES), 1) < HEAD_DIM
        o_ref[...] = jnp.where(first, acc_a / l_a, acc_b / l_b)
        lse_ref[0] = jnp.where(first, mx[0] + jnp.log(l_a), mx[1] + jnp.log(l_b))

    return pl.pallas_call(
        body, grid=(npair, SEQ // blk),
        in_specs=[pl.BlockSpec((blk, 2 * LANES), lambda p, i: (i, p)),
                  pl.BlockSpec((SEQ, 2 * LANES), lambda p, i: (0, p)),
                  pl.BlockSpec((SEQ, LANES), lambda p, i: (0, p))],
        out_specs=[pl.BlockSpec((blk, LANES), lambda p, i: (i, p)),
                   pl.BlockSpec((1, blk, LANES), lambda p, i: (p, i, 0))],
        out_shape=(SDS((SEQ, FOX_W), f32), SDS((npair, SEQ, LANES), f32)), name="fox_fwd",
        scratch_shapes=[pltpu.VMEM((2, blk, SEQ), f32)],
        compiler_params=_params(("parallel", "arbitrary")),
    )(q_aug, k_aug, v)


def _fox_bwd(q_aug, k_aug, v, do, o, lse):
    blk = FOX_BLK
    npair = FOX_W // LANES
    nblk = SEQ // blk

    def body(q_ref, k_ref, v_ref, do_ref, o_ref, lse_ref, dq_ref, dk_ref, dv_ref, df_ref, dq_acc, delta_ref):
        lane_s = lax.broadcasted_iota(jnp.int32, (SEQ, LANES), 1)
        prod = do_ref[...].astype(bf16).astype(f32) * o_ref[...]
        d_a = jnp.sum(jnp.where(lane_s < HEAD_DIM, prod, 0.0), axis=1, keepdims=True)
        d_b = jnp.sum(jnp.where(lane_s >= HEAD_DIM, prod, 0.0), axis=1, keepdims=True)
        delta_ref[...] = jnp.where(lane_s < HEAD_DIM, d_a, d_b)
        dq_acc[...] = jnp.zeros_like(dq_acc)
        df_ref[...] = jnp.zeros_like(df_ref)
        lane = lax.broadcasted_iota(jnp.int32, (blk, LANES), 1)
        own = [lane < HEAD_DIM, lane >= HEAD_DIM]
        tri = lax.broadcasted_iota(jnp.int32, (blk, blk), 0) >= lax.broadcasted_iota(jnp.int32, (blk, blk), 1)

        def q_slab(qoff, h):
            return q_ref[pl.ds(qoff, blk), h * LANES:(h + 1) * LANES]

        def probs(qoff, h, k_h, masked):
            s = _nt(q_slab(qoff, h), k_h)
            if masked:
                s = jnp.where(tri, s, NEG)
            return jnp.exp(s - lse_ref[0, pl.ds(qoff, blk), h * HEAD_DIM:h * HEAD_DIM + 1])

        def k_slabs(koff):
            return [k_ref[pl.ds(koff, blk), h * LANES:(h + 1) * LANES] for h in range(2)]

        def kv_step(kj, _):
            koff = pl.multiple_of(kj * blk, blk)
            k_aug = k_slabs(koff)
            k_own = [jnp.where(own[h], k_aug[h], jnp.zeros_like(k_aug[h])) for h in range(2)]
            vv = v_ref[pl.ds(koff, blk), :]
            v_own = [jnp.where(own[h], vv, jnp.zeros_like(vv)) for h in range(2)]

            def q_tile(qi, carry, masked):
                qoff = pl.multiple_of(qi * blk, blk)
                dd = do_ref[pl.ds(qoff, blk), :].astype(bf16)
                new, dq_add = [], None
                for h in range(2):
                    dk_h, dv_h, dcol = carry[h]
                    p = probs(qoff, h, k_aug[h], masked)
                    dl = p * (_nt(dd, v_own[h]) - delta_ref[pl.ds(qoff, blk), h * HEAD_DIM:h * HEAD_DIM + 1])
                    dlb = dl.astype(bf16)
                    part = jnp.dot(dlb, k_own[h], preferred_element_type=f32)
                    dq_add = part if dq_add is None else dq_add + part
                    new.append((dk_h + _tn(dlb, q_slab(qoff, h)), dv_h + _tn(p.astype(bf16), dd),
                                dcol + _colsum(dl)))
                dq_acc[pl.ds(qoff, blk), :] += dq_add * ATT_SCALE
                return tuple(new)

            zero = (jnp.zeros((blk, LANES), f32), jnp.zeros((blk, LANES), f32), jnp.zeros((1, blk), f32))
            carry = q_tile(kj, (zero, zero), True)
            (dk_a, dv_a, dcol_a), (dk_b, dv_b, dcol_b) = lax.fori_loop(
                kj + 1, nblk, lambda qi, cr: q_tile(qi, cr, False), carry)
            dk_ref[pl.ds(koff, blk), :] = jnp.where(own[0], dk_a, dk_b).astype(dk_ref.dtype)
            dv_ref[pl.ds(koff, blk), :] = jnp.where(own[0], dv_a, dv_b).astype(dv_ref.dtype)
            df_ref[0, 0:1, pl.ds(koff, blk)] = -dcol_a
            df_ref[0, 1:2, pl.ds(koff, blk)] = -dcol_b
            return 0

        lax.fori_loop(0, nblk, kv_step, 0)
        dq_ref[...] = dq_acc[...].astype(dq_ref.dtype)

    pair_aug = pl.BlockSpec((SEQ, 2 * LANES), lambda p: (0, p))
    slab = pl.BlockSpec((SEQ, LANES), lambda p: (0, p))
    per_pair = pl.BlockSpec((1, SEQ, LANES), lambda p: (p, 0, 0))
    rows = pl.BlockSpec((1, 8, SEQ), lambda p: (p, 0, 0))
    return pl.pallas_call(
        body, grid=(npair,),
        in_specs=[pair_aug, pair_aug, slab, slab, slab, per_pair],
        out_specs=[slab, slab, slab, rows],
        out_shape=(SDS((SEQ, FOX_W), bf16),) * 3 + (SDS((npair, 8, SEQ), f32),), name="fox_bwd",
        scratch_shapes=[pltpu.VMEM((SEQ, LANES), f32), pltpu.VMEM((SEQ, LANES), f32)],
        compiler_params=_params(("parallel",)),
    )(q_aug, k_aug, v, do, o, lse)


DIL_BLK = 128
DILATIONS = (1, 4, 16)
N_GROUPS = len(DILATIONS)
DIL_PAIRS = DIL_OUT_W // LANES


def _dil_blocks(d):
    r1 = lax.broadcasted_iota(jnp.int32, (DIL_BLK, DIL_BLK), 0)
    c1 = lax.broadcasted_iota(jnp.int32, (DIL_BLK, DIL_BLK), 1)
    r2 = lax.broadcasted_iota(jnp.int32, (DIL_BLK, 2 * DIL_BLK), 0)
    c2 = lax.broadcasted_iota(jnp.int32, (DIL_BLK, 2 * DIL_BLK), 1)
    band = ((c2 < DIL_BLK) & (c2 >= r2)) | ((c2 >= DIL_BLK) & (c2 - DIL_BLK <= r2))
    out = []
    for r in range(d):
        for b in range(SEQ // d // DIL_BLK):
            rows = pl.ds(r + d * DIL_BLK * b, DIL_BLK, stride=d)
            if b == 0:
                out.append((rows, rows, r1 >= c1))
            else:
                out.append((rows, pl.ds(r + d * DIL_BLK * (b - 1), 2 * DIL_BLK, stride=d), band))
    return out


def _dil_fwd(q, k, v, g):
    def body(q_ref, k_ref, v_ref, o_ref, lse_ref):
        first = lax.broadcasted_iota(jnp.int32, (DIL_BLK, LANES), 1) < HEAD_DIM
        for rows, krows, mask in _dil_blocks(DILATIONS[g]):
            qv, kk, vv = q_ref[rows, :].astype(bf16), k_ref[krows, :].astype(bf16), v_ref[krows, :].astype(bf16)
            outs, lses = [], []
            for own in (first, ~first):
                s = jnp.where(mask, _nt(jnp.where(own, qv, jnp.zeros_like(qv)), kk), NEG)
                m = jnp.max(s, axis=1, keepdims=True)
                p = jnp.exp(s - m)
                l = jnp.sum(p, axis=1, keepdims=True)
                outs.append(jnp.dot(p.astype(bf16), vv, preferred_element_type=f32) / l)
                lses.append(m + jnp.log(l))
            o_ref[rows, :] = jnp.where(first, outs[0], outs[1])
            lse_ref[rows, :] = jnp.where(first, lses[0], lses[1])

    grouped = pl.BlockSpec((SEQ, LANES), lambda p: (0, DIL_PAIRS * g + p))
    own = pl.BlockSpec((SEQ, LANES), lambda p: (0, p))
    shape = SDS((SEQ, DIL_OUT_W), f32)
    return pl.pallas_call(
        body, grid=(DIL_PAIRS,), in_specs=[grouped] * 3, out_specs=[own] * 2, out_shape=(shape, shape),
        name=f"dil_fwd_{DILATIONS[g]}", compiler_params=_params(("parallel",)),
    )(q, k, v)


def _dil_bwd(q, k, v, do, lse, delta, g):
    def body(q_ref, k_ref, v_ref, do_ref, lse_ref, dl_ref, dq_ref, dk_ref, dv_ref):
        first = lax.broadcasted_iota(jnp.int32, (DIL_BLK, LANES), 1) < HEAD_DIM
        dk_ref[...] = jnp.zeros_like(dk_ref)
        dv_ref[...] = jnp.zeros_like(dv_ref)
        for rows, krows, mask in _dil_blocks(DILATIONS[g]):
            qv, kk, vv = q_ref[rows, :].astype(bf16), k_ref[krows, :].astype(bf16), v_ref[krows, :].astype(bf16)
            dov = do_ref[rows, :].astype(bf16)
            lsev, delv = lse_ref[rows, :], dl_ref[rows, :]
            dqs, dk_add, dv_add = [], None, None
            for h, own in enumerate((first, ~first)):
                col = h * HEAD_DIM
                qh = jnp.where(own, qv, jnp.zeros_like(qv))
                doh = jnp.where(own, dov, jnp.zeros_like(dov))
                p = jnp.exp(jnp.where(mask, _nt(qh, kk), NEG) - lsev[:, col:col + 1])
                dl = (p * (_nt(doh, vv) - delv[:, col:col + 1])).astype(bf16)
                dqs.append(jnp.dot(dl, kk, preferred_element_type=f32))
                dk_h, dv_h = _tn(dl, qh), _tn(p.astype(bf16), doh)
                dk_add = dk_h if dk_add is None else dk_add + dk_h
                dv_add = dv_h if dv_add is None else dv_add + dv_h
            dq_ref[rows, :] = jnp.where(first, dqs[0], dqs[1]) * ATT_SCALE
            dk_ref[krows, :] += dk_add
            dv_ref[krows, :] += dv_add

    grouped = pl.BlockSpec((SEQ, LANES), lambda p: (0, DIL_PAIRS * g + p))
    own = pl.BlockSpec((SEQ, LANES), lambda p: (0, p))
    shape = SDS((SEQ, DIL_OUT_W), f32)
    return pl.pallas_call(
        body, grid=(DIL_PAIRS,), in_specs=[grouped] * 3 + [own] * 3, out_specs=[own] * 3,
        out_shape=(shape, shape, shape), name=f"dil_bwd_{DILATIONS[g]}", compiler_params=_params(("parallel",)),
    )(q, k, v, do, lse, delta)


def _position():
    return lax.axis_index("x"), lax.axis_index("y"), lax.axis_index("c")


def _all_gather(block, name):
    def body(x_ref, out_ref, send_sems, recv_sems, local_sem):
        x, y, c = _position()
        me, sibling = (x, y, c), (x, y, 1 - c)
        chips = [(1 - x, y), (x, 1 - y), (1 - x, 1 - y)]

        def slot(px, py, pc):
            return out_ref.at[4 * px + 2 * py + pc]

        def copy(k, blk, to, src=None):
            return pltpu.make_async_remote_copy(
                src_ref=slot(*blk) if src is None else src, dst_ref=slot(*blk),
                send_sem=send_sems.at[k], recv_sem=recv_sems.at[k], device_id=to, device_id_type=MESH)

        mine = pltpu.make_async_copy(x_ref, slot(*me), local_sem)
        mine.start()
        first = [copy(0, me, sibling, src=x_ref)]
        first += [copy(1 + j, me, (*chip, c), src=x_ref) for j, chip in enumerate(chips)]
        for cp in first:
            cp.start()
        passed = [copy(4 + j, (*chip, c), sibling) for j, chip in enumerate(chips)]
        for j, chip in enumerate(chips):
            copy(1 + j, (*chip, c), me).wait_recv()
            passed[j].start()
        copy(0, sibling, me).wait_recv()
        for j, chip in enumerate(chips):
            copy(4 + j, (*chip, 1 - c), me).wait_recv()
        for cp in first + passed:
            cp.wait_send()
        mine.wait()

    return pl.pallas_call(
        body, out_shape=SDS((N_DEV,) + block.shape, block.dtype),
        in_specs=[pl.BlockSpec(memory_space=pl.ANY)], out_specs=pl.BlockSpec(memory_space=pl.ANY),
        scratch_shapes=[pltpu.SemaphoreType.DMA((7,)), pltpu.SemaphoreType.DMA((7,)), pltpu.SemaphoreType.DMA],
        name=name,
    )(block)


def _all_gather_many(blocks, name):
    n = len(blocks)

    def body(*refs):
        x_refs, out_refs = refs[:n], refs[n:2 * n]
        send_sems, recv_sems, local_sems = refs[2 * n:]
        x, y, c = _position()
        me, sibling = (x, y, c), (x, y, 1 - c)
        chips = [(1 - x, y), (x, 1 - y), (1 - x, 1 - y)]

        def slot(a, px, py, pc):
            return out_refs[a].at[4 * px + 2 * py + pc]

        def copy(a, k, blk, to, own=False):
            return pltpu.make_async_remote_copy(
                src_ref=x_refs[a] if own else slot(a, *blk), dst_ref=slot(a, *blk),
                send_sem=send_sems.at[a, k], recv_sem=recv_sems.at[a, k], device_id=to, device_id_type=MESH)

        mine = [pltpu.make_async_copy(x_refs[a], slot(a, *me), local_sems.at[a]) for a in range(n)]
        for cp in mine:
            cp.start()
        started = []
        for a in range(n):
            first = [copy(a, 0, me, sibling, own=True)]
            first += [copy(a, 1 + j, me, (*chip, c), own=True) for j, chip in enumerate(chips)]
            for cp in first:
                cp.start()
            started += first
        for a in range(n):
            for j, chip in enumerate(chips):
                copy(a, 1 + j, (*chip, c), me).wait_recv()
                passed = copy(a, 4 + j, (*chip, c), sibling)
                passed.start()
                started.append(passed)
        for a in range(n):
            copy(a, 0, sibling, me).wait_recv()
            for j, chip in enumerate(chips):
                copy(a, 4 + j, (*chip, 1 - c), me).wait_recv()
        for cp in started:
            cp.wait_send()
        for cp in mine:
            cp.wait()

    hbm = pl.BlockSpec(memory_space=pl.ANY)
    return pl.pallas_call(
        body, out_shape=[SDS((N_DEV,) + b.shape, b.dtype) for b in blocks],
        in_specs=[hbm] * n, out_specs=[hbm] * n,
        scratch_shapes=[pltpu.SemaphoreType.DMA((n, 7)), pltpu.SemaphoreType.DMA((n, 7)),
                        pltpu.SemaphoreType.DMA((n,))],
        name=name,
    )(*blocks)


HBM_SPEC = pl.BlockSpec(memory_space=pltpu.HBM)
SEM_SPEC = pl.BlockSpec(memory_space=pltpu.SEMAPHORE)
SPLIT_COPY = pltpu.CompilerParams(has_side_effects=pltpu.SideEffectType.DATAFLOW_SIDE_EFFECTING)


def _in_hbm(t):
    return pltpu.with_memory_space_constraint(t, pltpu.HBM)


def _pair_copies(g_refs, land_refs, send_sems, recv_sems):
    x, y, c = _position()
    return [pltpu.make_async_remote_copy(
        src_ref=g.at[2 * k + (1 - c)], dst_ref=land.at[k], send_sem=send_sems.at[4 * a + k],
        recv_sem=recv_sems.at[4 * a + k], device_id=(x, y, 1 - c), device_id_type=MESH)
        for a, (g, land) in enumerate(zip(g_refs, land_refs, strict=True)) for k in range(4)]


def _chip_copies(t_refs, land_refs, send_sems, recv_sems):
    x, y, c = _position()
    chips = [(1 - x, y), (x, 1 - y), (1 - x, 1 - y)]
    return [pltpu.make_async_remote_copy(
        src_ref=t.at[2 * px + py], dst_ref=land.at[j], send_sem=send_sems.at[3 * a + j],
        recv_sem=recv_sems.at[3 * a + j], device_id=(px, py, c), device_id_type=MESH)
        for a, (t, land) in enumerate(zip(t_refs, land_refs, strict=True)) for j, (px, py) in enumerate(chips)]


_ROUNDS = {"pair": (_pair_copies, 4), "chip": (_chip_copies, 3)}


def _exchange_start(kind, ts, name):
    copies, slots = _ROUNDS[kind]
    n = len(ts)
    lands = [_in_hbm(lax.empty((slots,) + t.shape[1:], t.dtype)) for t in ts]

    def body(*refs):
        for cp in copies(refs[:n], refs[n:2 * n], refs[2 * n], refs[2 * n + 1]):
            cp.start()
        refs[-1][...] = jnp.zeros_like(refs[-1])

    sems = pltpu.SemaphoreType.DMA((slots * n,))
    res = pl.pallas_call(
        body, name=name, in_specs=[HBM_SPEC] * (2 * n),
        out_shape=(sems, sems, *[pltpu.HBM(t.shape, t.dtype) for t in (*ts, *lands)], SDS((8, LANES), f32)),
        out_specs=(SEM_SPEC, SEM_SPEC, *[HBM_SPEC] * (2 * n), pl.BlockSpec(memory_space=pltpu.VMEM)),
        input_output_aliases={i: 2 + i for i in range(2 * n)}, compiler_params=SPLIT_COPY,
    )(*[_in_hbm(t) for t in ts], *lands)
    return res[:-1], res[-1]


def _exchange_wait(kind, state, after, name):
    copies, _ = _ROUNDS[kind]
    send_sems, recv_sems, *arrays = state
    n = len(arrays) // 2

    def body(*refs):
        for cp in copies(refs[:n], refs[n:2 * n], refs[2 * n], refs[2 * n + 1]):
            cp.wait_send()
            cp.wait_recv()

    res = pl.pallas_call(
        body, name=name, in_specs=[HBM_SPEC] * (2 * n) + [SEM_SPEC, SEM_SPEC, pl.BlockSpec(memory_space=pl.ANY)],
        out_shape=[pltpu.HBM(t.shape, t.dtype) for t in arrays], out_specs=[HBM_SPEC] * (2 * n),
        input_output_aliases={i: i for i in range(2 * n)}, compiler_params=SPLIT_COPY,
    )(*arrays, send_sems, recv_sems, after)
    return res[:n], res[n:]


def _gather_copies(x_refs, out_refs, send_sems, recv_sems):
    x, y, c = _position()
    peers = [(x, y, 1 - c), (1 - x, y, c), (x, 1 - y, c), (1 - x, 1 - y, c)]
    sends, arrivals = [], []
    for a, (x_ref, out_ref) in enumerate(zip(x_refs, out_refs, strict=True)):
        for k, (px, py, pc) in enumerate(peers):
            sems = dict(send_sem=send_sems.at[4 * a + k], recv_sem=recv_sems.at[4 * a + k],
                        device_id=(px, py, pc), device_id_type=MESH)
            sends.append(pltpu.make_async_remote_copy(src_ref=x_ref, dst_ref=out_ref.at[4 * x + 2 * y + c], **sems))
            arrivals.append(pltpu.make_async_remote_copy(src_ref=x_ref, dst_ref=out_ref.at[4 * px + 2 * py + pc],
                                                         **sems))
    return sends, arrivals


def _gather_start(blocks, after, name):
    n = len(blocks)
    outs = [_in_hbm(lax.empty((N_DEV,) + b.shape, b.dtype)) for b in blocks]

    def body(*refs):
        sends, _ = _gather_copies(refs[:n], refs[n:2 * n], refs[2 * n + 1], refs[2 * n + 2])
        for cp in sends:
            cp.start()
        refs[-1][...] = jnp.zeros_like(refs[-1])

    sems = pltpu.SemaphoreType.DMA((4 * n,))
    res = pl.pallas_call(
        body, name=name, in_specs=[HBM_SPEC] * (2 * n) + [pl.BlockSpec(memory_space=pl.ANY)],
        out_shape=(sems, sems, *[pltpu.HBM(t.shape, t.dtype) for t in (*blocks, *outs)], SDS((8, LANES), f32)),
        out_specs=(SEM_SPEC, SEM_SPEC, *[HBM_SPEC] * (2 * n), pl.BlockSpec(memory_space=pltpu.VMEM)),
        input_output_aliases={i: 2 + i for i in range(2 * n)}, compiler_params=SPLIT_COPY,
    )(*[_in_hbm(b) for b in blocks], *outs, after)
    return res[:-1], res[-1]


def _gather_wait(state, after, name):
    send_sems, recv_sems, *arrays = state
    n = len(arrays) // 2

    def body(*refs):
        sends, arrivals = _gather_copies(refs[:n], refs[n:2 * n], refs[2 * n], refs[2 * n + 1])
        for cp in sends:
            cp.wait_send()
        for cp in arrivals:
            cp.wait_recv()

    res = pl.pallas_call(
        body, name=name, in_specs=[HBM_SPEC] * (2 * n) + [SEM_SPEC, SEM_SPEC, pl.BlockSpec(memory_space=pl.ANY)],
        out_shape=[pltpu.HBM(t.shape, t.dtype) for t in arrays], out_specs=[HBM_SPEC] * (2 * n),
        input_output_aliases={i: i for i in range(2 * n)}, compiler_params=SPLIT_COPY,
    )(*arrays, send_sems, recv_sems, after)
    return res[:n], res[n:]


def _gather_finish(partial, name):
    n = len(partial)

    def body(*refs):
        in_refs, out_refs = refs[:n], refs[n:2 * n]
        send_sems, recv_sems = refs[2 * n:]
        x, y, c = _position()
        chips = [(1 - x, y), (x, 1 - y), (1 - x, 1 - y)]
        copies = []
        for a in range(n):
            for j, (px, py) in enumerate(chips):
                cp = pltpu.make_async_remote_copy(
                    src_ref=in_refs[a].at[4 * px + 2 * py + c], dst_ref=out_refs[a].at[4 * px + 2 * py + c],
                    send_sem=send_sems.at[a, j], recv_sem=recv_sems.at[a, j], device_id=(x, y, 1 - c),
                    device_id_type=MESH)
                cp.start()
                copies.append(cp)
        for a in range(n):
            for j, (px, py) in enumerate(chips):
                pltpu.make_async_remote_copy(
                    src_ref=in_refs[a].at[4 * px + 2 * py + (1 - c)], dst_ref=out_refs[a].at[4 * px + 2 * py + (1 - c)],
                    send_sem=send_sems.at[a, j], recv_sem=recv_sems.at[a, j], device_id=(x, y, 1 - c),
                    device_id_type=MESH).wait_recv()
        for cp in copies:
            cp.wait_send()

    hbm = pl.BlockSpec(memory_space=pl.ANY)
    return pl.pallas_call(
        body, out_shape=[SDS(p.shape, p.dtype) for p in partial], in_specs=[hbm] * n, out_specs=[hbm] * n,
        input_output_aliases={a: a for a in range(n)},
        scratch_shapes=[pltpu.SemaphoreType.DMA((n, 3)), pltpu.SemaphoreType.DMA((n, 3))],
        name=name,
    )(*partial)


def _row_tile(rows):
    return 512 if rows % 512 == 0 and rows > 512 else rows


def _pair_add(g, r1, core, name):
    def body(c_ref, g_ref, r_ref, o_ref):
        o_ref[...] = (g_ref[...].astype(f32) + r_ref[...].astype(f32)).astype(o_ref.dtype)

    rows, cols = g.shape[1:]
    tile = _row_tile(rows)
    blk = (1, tile, cols)
    return pl.pallas_call(
        body, out_shape=SDS((4, rows, cols), g.dtype), name=name,
        grid_spec=pltpu.PrefetchScalarGridSpec(
            num_scalar_prefetch=1, grid=(4, rows // tile),
            in_specs=[pl.BlockSpec(blk, lambda k, i, c_ref: (2 * k + c_ref[0], i, 0)),
                      pl.BlockSpec(blk, lambda k, i, c_ref: (k, i, 0))],
            out_specs=pl.BlockSpec(blk, lambda k, i, c_ref: (k, i, 0))),
        compiler_params=_params(("parallel", "arbitrary")),
    )(core, g, r1)


def _chip_add(t, r2, chip, name):
    def body(c_ref, t_ref, r_ref, o_ref):
        o_ref[...] = ((t_ref[0].astype(f32) + r_ref[0].astype(f32)) + r_ref[1].astype(f32)) + r_ref[2].astype(f32)

    rows, cols = t.shape[1:]
    tile = _row_tile(rows)
    return pl.pallas_call(
        body, out_shape=SDS((rows, cols), f32), name=name,
        grid_spec=pltpu.PrefetchScalarGridSpec(
            num_scalar_prefetch=1, grid=(rows // tile,),
            in_specs=[pl.BlockSpec((1, tile, cols), lambda i, c_ref: (c_ref[0], i, 0)),
                      pl.BlockSpec((3, tile, cols), lambda i, c_ref: (0, i, 0))],
            out_specs=pl.BlockSpec((tile, cols), lambda i, c_ref: (i, 0))),
        compiler_params=_params(("arbitrary",)),
    )(chip, t, r2)


def _pad_to(t, axis, size):
    pads = [(0, 0)] * t.ndim
    pads[axis] = (0, size - t.shape[axis])
    return jnp.pad(t, pads)


_REF_COLS = {"qa": (0, FOX_W), "ka": (FOX_W, FOX_W), "va": (2 * FOX_W, FOX_W), "f": (3 * FOX_W, N_FOX_HEADS)}
_REF_COLS.update({n: (3 * FOX_W + N_FOX_HEADS + i * DIL_W, DIL_W) for i, n in enumerate(("qb", "kb", "vb"))})
_REF_COLS.update({n: (3 * FOX_W + N_FOX_HEADS + 3 * DIL_W + i * D, D) for i, n in enumerate(("ga", "gb"))})
_REF_ORDER = ("qa", "ka", "va", "f", "qb", "kb", "vb", "ga", "gb")


def _place_cols(sources, src_of, out_cols, name, row_block=512):
    arrays = [s[0] if isinstance(s, tuple) else s for s in sources]
    widths = [a.shape[-1] for a in arrays]
    rows = arrays[0].shape[-2]
    plan = []
    for t in range(out_cols // LANES):
        segs, c, end = [], t * LANES, (t + 1) * LANES
        while c < end:
            s = src_of(c)
            if s is None:
                c += 1
                continue
            n = 1
            while c + n < end and src_of(c + n) == (s[0], s[1] + n):
                n += 1
            segs.append((s[0], s[1], c - t * LANES, n))
            c += n
        plan.append(segs)

    def body(*refs):
        o_ref = refs[-1]
        for t, segs in enumerate(plan):
            acc = None
            for si, c0, o0, n in segs:
                a0 = c0 // LANES * LANES
                wide = min(2 * LANES, widths[si] - a0)
                win = refs[si][0, :, a0:a0 + wide] if isinstance(sources[si], tuple) else refs[si][:, a0:a0 + wide]
                r = lax.broadcasted_iota(jnp.int32, (wide, LANES), 0)
                c = lax.broadcasted_iota(jnp.int32, (wide, LANES), 1)
                pick = ((r - (c0 - a0) == c - o0) & (c >= o0) & (c < o0 + n)).astype(bf16)
                part = jnp.dot(win.astype(bf16), pick, preferred_element_type=f32)
                acc = part if acc is None else acc + part
            tile = jnp.zeros((row_block, LANES), f32) if acc is None else acc
            o_ref[:, t * LANES:(t + 1) * LANES] = tile.astype(o_ref.dtype)

    def spec(s):
        if isinstance(s, tuple):
            j = s[1]
            return pl.BlockSpec((1, row_block, s[0].shape[-1]), lambda i: (j, i, 0))
        return pl.BlockSpec((row_block, s.shape[-1]), lambda i: (i, 0))

    return pl.pallas_call(
        body, grid=(rows // row_block,), in_specs=[spec(s) for s in sources],
        out_specs=pl.BlockSpec((row_block, out_cols), lambda i: (i, 0)), out_shape=SDS((rows, out_cols), bf16),
        name=name, compiler_params=_params(("parallel",)),
    )(*arrays)


def _ref_piece(r):
    for name in _REF_ORDER:
        lo, width = _REF_COLS[name]
        if lo <= r < lo + width:
            return name, r - lo
    raise ValueError(r)


def _shard_pad_cols(pieces):
    names = [n for n in _REF_ORDER if n != "vb"]
    sources = [pieces[n] for n in names] + list(pieces["vb"])

    def src_of(c):
        j, i = divmod(c, W_IN_PAD)
        if i >= W_IN_SH:
            return None
        name, col = _ref_piece(j * W_IN_SH + i)
        if name == "vb":
            return len(names) + col // DIL_OUT_W, col % DIL_OUT_W
        return names.index(name), col

    return _place_cols(sources, src_of, N_DEV * W_IN_PAD, "place_dproj")


_SLABS = {"ga": C_GA, "gb": C_GB, "qb": C_QB, "kb": C_KB, "vb": C_VB, "qa": C_QA, "ka": C_KA, "va": C_VA, "f": C_F}


def _slab_w_in(stack):
    def src_of(c):
        for name, start in _SLABS.items():
            lo, width = _REF_COLS[name]
            if start <= c < start + width:
                return divmod(lo + c - start, W_IN_SH)
        return None

    return _place_cols([(stack, j) for j in range(N_DEV)], src_of, PROJ_W, "place_w_in")


def kernel(x, c, w_ada, b_ada, g_mix, w_in, b_fgate, w_br_a, w_br_b, w_out, g_ffn, w_ffn_gate, w_ffn_up, w_ffn_down, g_final, loss_target, m_w_ada, m_b_ada, m_g_mix, m_w_in, m_b_fgate, m_w_br_a, m_w_br_b, m_w_out, m_g_ffn, m_w_ffn_gate, m_w_ffn_up, m_w_ffn_down, m_g_final, v_w_ada, v_b_ada, v_g_mix, v_w_in, v_b_fgate, v_w_br_a, v_w_br_b, v_w_out, v_g_ffn, v_w_ffn_gate, v_w_ffn_up, v_w_ffn_down, v_g_final):
    px, py, pc = _position()
    dev = 4 * px + 2 * py + pc
    x2d, tgt = x[0], loss_target[0]

    c_all = _all_gather(c, "gather_c").reshape(N_DEV, D)
    ada_cols = w_ada.shape[2]
    b_shard = lax.dynamic_slice(b_ada, (0, dev * ada_cols), (1, ada_cols))
    mod_shard = _ada_fwd(c_all, w_ada[0], b_shard)
    mod_all = _all_gather(mod_shard, "gather_mod")
    modv = lax.dynamic_index_in_dim(mod_all, dev, axis=1, keepdims=False).reshape(6, D)

    gate_up = jnp.concatenate([_pad_to(w_ffn_gate[0], 1, FF_PAD), _pad_to(w_ffn_up[0], 1, FF_PAD)], axis=1)
    w_in_s, = _all_gather_many([_pad_to(w_in[0], 1, W_IN_PAD).astype(bf16)], "gather_w_in")
    later = [w_br_a[0], w_br_b[0], w_out[0], gate_up, _pad_to(w_ffn_down[0], 0, FF_PAD)]
    later_state, later_token = _gather_start([t.astype(bf16) for t in later], w_in_s, "gather_rest_start")
    w_in_p = _slab_w_in(w_in_s)

    h1 = _pre1(x2d, modv, g_mix)
    proj = _matmul(h1, w_in_p, name="mm_proj", tm=SEQ, tn=896, tk=D, after=later_token)
    b_pad = jnp.pad(b_fgate, ((0, 0), (0, LANES - N_FOX_HEADS)))
    q_aug, k_aug, va = _fox_prep(proj, _fox_gate_fwd(proj, b_pad))
    ya_h, lse_a = _fox_fwd(q_aug, k_aug, va)

    tables = _rope_tables()
    qb_r, kb_r, vb = _rope_fwd(proj, tables)
    by_group = [_dil_fwd(qb_r, kb_r, vb, grp) for grp in range(N_GROUPS)]
    yb_h, lse_b = _dil_combine([o for o, _ in by_group], [l for _, l in by_group])

    mine, arrived = _gather_wait(later_state, yb_h, "gather_rest_wait")
    w_a_s, w_b_s, w_o_s, w_gu_s, w_d_s = [
        lax.dynamic_update_slice(stack, block[None], (dev, 0, 0))
        for stack, block in zip(_gather_finish(arrived, "gather_rest_finish"), mine, strict=True)]
    w_o = w_o_s.reshape(D, D)
    w_d = w_d_s.reshape(FF_HID, D)
    ya = _matmul_stack(ya_h, w_a_s, name="mm_br_a")
    yb = _matmul_stack(yb_h, w_b_s, name="mm_br_b")

    merged = _merge_fwd(ya, yb, proj)
    mix = _matmul(merged, w_o, name="mm_out", tm=SEQ, tn=512, tk=D)
    x1, h2 = _post1(x2d, mix, modv, g_ffn)
    act, au = _ffn_in(h2, w_gu_s)
    ff = _matmul(act, w_d, name="mm_ffn_down", tm=SEQ // 2, tn=512, tk=FF_HID)

    dx2, dff, dg_final, dga_f, loss_lanes = _final(x1, ff, tgt, modv, g_final.reshape(1, D))
    dau = _ffn_bwd_in(dff, w_d_s, au)

    core = pc.astype(jnp.int32).reshape(1)
    chip = (2 * px + py).astype(jnp.int32).reshape(1)

    def pair_done(state, after, tags, name):
        mine, theirs = _exchange_wait("pair", state, after, "pair_wait_" + name)
        sums = [_pair_add(g, r, core, "pair_add_" + t) for g, r, t in zip(mine, theirs, tags)]
        return _exchange_start("chip", sums, "chip_start_" + name)

    def from_chips(state, after, tags, name):
        sums, got = _exchange_wait("chip", state, after, "chip_wait_" + name)
        return [_chip_add(p, r, chip, "chip_add_" + t) for p, r, t in zip(sums, got, tags)]

    g_gu = _matmul(h2, dau, ta=True, by_shard=True, out_dtype=bf16, name="mm_g_ffn_in", tm=D, tn=2 * FF_PAD, tk=SEQ)
    g_d = _matmul(act, dff, ta=True, out_dtype=bf16, name="mm_g_down", tm=FF_HID // 2, tn=512, tk=SEQ)
    ffn_tags = ["gu", "down"]
    ffn_pair, ffn_pair_token = _exchange_start("pair", [g_gu, g_d.reshape(N_DEV, FF_PAD, D)], "pair_start_ffn")

    dh2 = _matmul_nt_shards(dau, w_gu_s, ffn_pair_token, name="mm_d_h2")
    ffn_state, ffn_token = pair_done(ffn_pair, dh2, ffn_tags, "ffn")
    dx1, dmix, dsh_f, dsc_f, dg_ffn, dga_m = _mid_bwd(dh2, x1, dx2, mix, modv, g_ffn)
    dmerged = _matmul(dmix, w_o, tb=True, name="mm_d_merged", tm=SEQ, tn=512, tk=D, after=ffn_token)
    dya, dyb, dga, dgb = _merge_bwd(dmerged, ya, yb, proj)
    dya_h = _matmul_stack(dya, w_a_s, tb=True, name="mm_d_ya")
    dyb_h = _matmul_stack(dyb, w_b_s, tb=True, name="mm_d_yb")

    dqa, dka, dva, dF = _fox_bwd(q_aug, k_aug, va, dya_h, ya_h, lse_a)
    dF_row = jnp.pad(dF[:, :2, :].reshape(N_FOX_HEADS, SEQ), ((0, LANES - N_FOX_HEADS), (0, 0)))
    df, db_fgate = _fox_gate_bwd(dF_row, proj, b_pad)

    delta_b = _dil_delta(dyb_h, yb_h)
    dil_grads = [_dil_bwd(qb_r, kb_r, vb, dyb_h, lse_b, delta_b, grp) for grp in range(N_GROUPS)]
    dqb, dkb = _rope_bwd([t[0] for t in dil_grads], [t[1] for t in dil_grads], tables)

    dproj = _shard_pad_cols({"qa": dqa, "ka": dka, "va": dva, "f": df, "qb": dqb, "kb": dkb,
                             "vb": [t[2] for t in dil_grads], "ga": dga, "gb": dgb})
    g_in = _matmul(h1, dproj, ta=True, by_shard=True, out_dtype=bf16, name="mm_g_in", tm=D, tn=W_IN_PAD, tk=SEQ)
    g_o = _matmul(merged, dmix, ta=True, out_dtype=bf16, name="mm_g_out", tm=D, tn=512, tk=SEQ)
    g_a = _matmul_stack(ya_h, dya, ta=True, out_dtype=bf16, name="mm_g_br_a")
    g_b = _matmul_stack(yb_h, dyb, ta=True, out_dtype=bf16, name="mm_g_br_b")
    rows_a, rows_b = FOX_W * W_BR_SH // D, DIL_OUT_W * W_BR_SH // D
    g_small = jnp.concatenate([g_a.reshape(N_DEV, rows_a, D), g_b.reshape(N_DEV, rows_b, D),
                               g_o.reshape(N_DEV, W_BR_SH, D)], axis=1)
    mix_tags = ["in", "small"]
    mix_pair, mix_pair_token = _exchange_start("pair", [g_in, g_small], "pair_start_mixer")

    dh1 = _matmul_nt_shards(dproj, w_in_s, mix_pair_token, name="mm_d_h1")
    grad_x, dsh_m, dsc_m, dg_mix = _first_bwd(dh1, x2d, dx1, modv, g_mix)

    pad_lane = lambda t: jnp.pad(t, ((0, 0), (0, D - t.shape[1])))
    small = jnp.concatenate([dsh_m, dsc_m, dga_m, dsh_f, dsc_f, dga_f, dg_mix, dg_ffn, dg_final,
                             pad_lane(db_fgate), loss_lanes, jnp.zeros((SMALL_ROWS - 11, D), f32)], axis=0)
    small_all = _all_gather(small, "gather_small")
    mix_state, mix_token = pair_done(mix_pair, small_all, mix_tags, "mixer")

    small_sum, loss_row = _small_reduce(small_all, mix_token)
    dmod_all = small_all[:, :6, :].reshape(N_DEV, 6 * D)
    g_w_ada = _ada_bwd(c_all, lax.dynamic_slice(dmod_all, (0, dev * ada_cols), (N_DEV, ada_cols)))
    s_gu, s_d = from_chips(ffn_state, small_sum, ffn_tags, "ffn")

    loss = loss_row[0, 0]
    g = {
        "w_ada": g_w_ada[None], "b_ada": small_sum[0:6].reshape(1, 6 * D), "g_mix": small_sum[6:7],
        "b_fgate": small_sum[9:10, :N_FOX_HEADS], "g_ffn": small_sum[7:8], "w_ffn_gate": s_gu[None, :, :W_FF_SH],
        "w_ffn_up": s_gu[None, :, FF_PAD:FF_PAD + W_FF_SH], "w_ffn_down": s_d[None, :W_FF_SH],
        "g_final": small_sum[8],
    }
    w = {"w_ada": w_ada, "b_ada": b_ada, "g_mix": g_mix, "w_in": w_in, "b_fgate": b_fgate, "w_br_a": w_br_a,
         "w_br_b": w_br_b, "w_out": w_out, "g_ffn": g_ffn, "w_ffn_gate": w_ffn_gate, "w_ffn_up": w_ffn_up,
         "w_ffn_down": w_ffn_down, "g_final": g_final}
    m = {"w_ada": m_w_ada, "b_ada": m_b_ada, "g_mix": m_g_mix, "w_in": m_w_in, "b_fgate": m_b_fgate,
         "w_br_a": m_w_br_a, "w_br_b": m_w_br_b, "w_out": m_w_out, "g_ffn": m_g_ffn, "w_ffn_gate": m_w_ffn_gate,
         "w_ffn_up": m_w_ffn_up, "w_ffn_down": m_w_ffn_down, "g_final": m_g_final}
    v = {"w_ada": v_w_ada, "b_ada": v_b_ada, "g_mix": v_g_mix, "w_in": v_w_in, "b_fgate": v_b_fgate,
         "w_br_a": v_w_br_a, "w_br_b": v_w_br_b, "w_out": v_w_out, "g_ffn": v_g_ffn, "w_ffn_gate": v_w_ffn_gate,
         "w_ffn_up": v_w_ffn_up, "w_ffn_down": v_w_ffn_down, "g_final": v_g_final}
    names = list(w)
    delta, new_m, new_v = {}, {}, {}

    transposed = ("w_in", "w_ffn_gate", "w_ffn_up")

    def update(n):
        shape = w[n].shape
        if n in transposed:
            g_t = g[n][0].T
            dl, mn, vn = _adamw(w[n][0].T, g_t, m[n][0].T, v[n][0].T, "adamw_" + n)
            g[n], delta[n], new_m[n], new_v[n] = g_t.T[None], dl.T[None], mn.T[None], vn.T[None]
            return
        two_d = (lambda t: t.reshape(shape[-2:])) if len(shape) == 3 else (lambda t: t)
        dl, mn, vn = _adamw(two_d(w[n]), two_d(g[n]), two_d(m[n]), two_d(v[n]), "adamw_" + n)
        delta[n], new_m[n], new_v[n] = dl.reshape(shape), mn.reshape(shape), vn.reshape(shape)

    for n in list(g):
        update(n)
    done = sum(delta[n].reshape(-1)[:N_FOX_HEADS] for n in g)
    s_in, s_small = from_chips(mix_state, done, mix_tags, "mixer")
    g.update({"w_in": s_in[None, :, :W_IN_SH], "w_br_a": s_small[:rows_a].reshape(1, FOX_W, W_BR_SH),
              "w_br_b": s_small[rows_a:rows_a + rows_b].reshape(1, DIL_OUT_W, W_BR_SH),
              "w_out": s_small[None, rows_a + rows_b:]})
    for n in ("w_in", "w_br_a", "w_br_b", "w_out"):
        update(n)

    return (loss, grad_x[None], *[g[n] for n in names], *[delta[n] for n in names],
            *[new_m[n] for n in names], *[new_v[n] for n in names])
```

```python
import functools

import jax
import jax.numpy as jnp
from jax import lax
from jax.experimental import pallas as pl
from jax.experimental.pallas import tpu as pltpu

f32 = jnp.float32
bf16 = jnp.bfloat16
SDS = jax.ShapeDtypeStruct
MESH = pl.DeviceIdType.MESH

N_DEV = 8
D = 1024
SEQ = 2048
HEAD_DIM = 64
N_FOX_HEADS = 8
FOX_W = 512
DIL_W = 768
DIL_OUT_W = 256
ROT_DIM = 16
ROPE_THETA = 500000.0
D_FF = 2816
IN_COLS = 5896
EPS = 1e-6
NEG = -1e30
ATT_SCALE = HEAD_DIM ** -0.5

ADAM_LR = 0.001
ADAM_B1 = 0.9
ADAM_B2 = 0.999
ADAM_EPS = 1e-08
ADAM_WD = 0.01
ADAM_STEP = 10

C_GA, C_GB, C_QB, C_KB, C_VB, C_QA, C_KA, C_VA, C_F = 0, 1024, 2304, 3072, 3840, 4608, 5120, 5632, 6144
PROJ_W = 6272
LANES = 128
VMEM_LIMIT = 52 * 1024 * 1024

W_IN_SH, W_IN_PAD = IN_COLS // N_DEV, 768
W_BR_SH = D // N_DEV
W_FF_SH, FF_PAD = D_FF // N_DEV, 384
FF_HID = N_DEV * FF_PAD
SMALL_ROWS = 16


def _params(sem=None):
    if sem is None:
        return pltpu.CompilerParams(vmem_limit_bytes=VMEM_LIMIT)
    return pltpu.CompilerParams(dimension_semantics=sem, vmem_limit_bytes=VMEM_LIMIT)


def _rowwise(fn, name, tiled, vecs, outs, reds=(), tile=256):
    nt, nv, no = len(tiled), len(vecs), len(outs)
    rows = tiled[0][0].shape[0]
    assert rows % tile == 0

    def body(*refs):
        tin = [r[...] for r in refs[:nt]]
        vin = [r[...] for r in refs[nt:nt + nv]]
        orefs = refs[nt + nv:nt + nv + no]
        rrefs = refs[nt + nv + no:]
        touts, routs = fn(tin, vin)
        for r, t in zip(orefs, touts, strict=True):
            r[...] = t.astype(r.dtype)
        if rrefs:
            @pl.when(pl.program_id(0) == 0)
            def _():
                for r in rrefs:
                    r[...] = jnp.zeros_like(r)
            for r, t in zip(rrefs, routs, strict=True):
                r[...] += t

    def col_map(cb):
        return lambda i: (i, cb)

    def whole_map(nd):
        return lambda i: (0,) * nd

    in_specs = [pl.BlockSpec((tile, w), col_map(cb)) for (_, w, cb) in tiled]
    in_specs += [pl.BlockSpec(v.shape, whole_map(v.ndim)) for v in vecs]
    out_specs = [pl.BlockSpec((tile, w), lambda i: (i, 0)) for (w, _) in outs]
    out_specs += [pl.BlockSpec((1, w), lambda i: (0, 0)) for w in reds]
    out_shape = [SDS((rows, w), dt) for (w, dt) in outs] + [SDS((1, w), f32) for w in reds]
    res = pl.pallas_call(
        body, grid=(rows // tile,), in_specs=in_specs, out_specs=out_specs, out_shape=out_shape, name=name,
        compiler_params=_params(("arbitrary",)),
    )(*[t[0] for t in tiled], *vecs)
    return res


def _matmul(a, b, *, ta=False, tb=False, out_dtype=f32, name, tm, tn, tk, by_shard=False, after=None):
    m, k = (a.shape[1], a.shape[0]) if ta else a.shape
    if by_shard and not ta:
        n, kb = (b.shape[1], N_DEV * b.shape[2]) if tb else (N_DEV * b.shape[2], b.shape[1])
        assert (tk if tb else tn) == b.shape[2]
    else:
        n, kb = (b.shape[0], b.shape[1]) if tb else (b.shape[1], b.shape[0])
    assert kb == k and m % tm == 0 and n % tn == 0 and k % tk == 0
    nk = k // tk
    dims = (((0 if ta else 1,), (1 if tb else 0,)), ((), ()))
    b_stacked = by_shard and not ta
    o_stacked = by_shard and ta

    def body(a_ref, b_ref, *rest):
        o_ref, *acc = rest[1:] if after is not None else rest
        bv = b_ref[0] if b_stacked else b_ref[...]
        p = lax.dot_general(a_ref[...].astype(bf16), bv.astype(bf16), dims, preferred_element_type=f32)

        def put(val):
            if o_stacked:
                o_ref[0] = val.astype(o_ref.dtype)
            else:
                o_ref[...] = val.astype(o_ref.dtype)

        if nk == 1:
            put(p)
        else:
            acc_ref, = acc
            kk = pl.program_id(2)

            @pl.when(kk == 0)
            def _():
                acc_ref[...] = p

            @pl.when(kk > 0)
            def _():
                acc_ref[...] += p

            @pl.when(kk == nk - 1)
            def _():
                put(acc_ref[...])

    a_spec = pl.BlockSpec((tk, tm), lambda i, j, kk: (kk, i)) if ta else pl.BlockSpec((tm, tk), lambda i, j, kk: (i, kk))
    if b_stacked and tb:
        b_spec = pl.BlockSpec((1, tn, tk), lambda i, j, kk: (kk, j, 0))
    elif b_stacked:
        b_spec = pl.BlockSpec((1, tk, tn), lambda i, j, kk: (j, kk, 0))
    elif tb:
        b_spec = pl.BlockSpec((tn, tk), lambda i, j, kk: (j, kk))
    else:
        b_spec = pl.BlockSpec((tk, tn), lambda i, j, kk: (kk, j))
    if o_stacked:
        assert tn == n // N_DEV
        out_spec = pl.BlockSpec((1, tm, tn), lambda i, j, kk: (j, i, 0))
        out_shape = SDS((N_DEV, m, tn), out_dtype)
    else:
        out_spec = pl.BlockSpec((tm, tn), lambda i, j, kk: (i, j))
        out_shape = SDS((m, n), out_dtype)
    extra_specs, extra = ([pl.BlockSpec(memory_space=pl.ANY)], [after]) if after is not None else ([], [])
    return pl.pallas_call(
        body, grid=(m // tm, n // tn, nk), in_specs=[a_spec, b_spec] + extra_specs, out_specs=out_spec,
        out_shape=out_shape, name=name, scratch_shapes=[pltpu.VMEM((tm, tn), f32)] if nk > 1 else [],
        compiler_params=_params(("parallel", "parallel", "arbitrary")),
    )(a, b, *extra)


def _matmul_stack(a, b, *, ta=False, tb=False, out_dtype=f32, name):
    def lanes(ref):
        return jnp.concatenate([ref[j] for j in range(N_DEV)], axis=1).astype(bf16)

    if ta:
        w = b.shape[1] // N_DEV

        def body(a_ref, b_ref, o_ref):
            p = _tn(a_ref[...].astype(bf16), b_ref[...].astype(bf16))
            for j in range(N_DEV):
                o_ref[j] = p[:, j * w:(j + 1) * w].astype(o_ref.dtype)

        return pl.pallas_call(body, out_shape=SDS((N_DEV, a.shape[1], w), out_dtype), name=name,
                              compiler_params=_params())(a, b)

    m, half = a.shape[0], a.shape[0] // 2
    n = b.shape[1] if tb else N_DEV * b.shape[2]

    def body(a_ref, b_ref, o_ref):
        av = a_ref[...].astype(bf16)
        o_ref[...] = (_nt(av, lanes(b_ref)) if tb else jnp.dot(av, lanes(b_ref), preferred_element_type=f32)
                      ).astype(o_ref.dtype)

    return pl.pallas_call(
        body, grid=(2,), in_specs=[pl.BlockSpec((half, a.shape[1]), lambda i: (i, 0)),
                                   pl.BlockSpec(b.shape, lambda i: (0, 0, 0))],
        out_specs=pl.BlockSpec((half, n), lambda i: (i, 0)), out_shape=SDS((m, n), out_dtype), name=name,
        compiler_params=_params(("parallel",)),
    )(a, b)


def _matmul_nt_shards(a, b, after, *, name, tm=512, tn=1024):
    m, n, w = a.shape[0], b.shape[1], b.shape[2]
    assert a.shape[1] == N_DEV * w and m % tm == 0 and n % tn == 0

    def body(a_ref, b_ref, after_ref, o_ref):
        acc = _nt(a_ref[:, 0:w], b_ref[0])
        for j in range(1, N_DEV):
            acc = acc + _nt(a_ref[:, j * w:(j + 1) * w], b_ref[j])
        o_ref[...] = acc

    return pl.pallas_call(
        body, grid=(n // tn, m // tm),
        in_specs=[pl.BlockSpec((tm, N_DEV * w), lambda j, i: (i, 0)), pl.BlockSpec((N_DEV, tn, w), lambda j, i: (0, j, 0)),
                  pl.BlockSpec(memory_space=pl.ANY)],
        out_specs=pl.BlockSpec((tm, tn), lambda j, i: (i, j)), out_shape=SDS((m, n), f32), name=name,
        compiler_params=_params(("parallel", "parallel")),
    )(a, b, after)


def _rms(x):
    r = lax.rsqrt(jnp.mean(x * x, axis=-1, keepdims=True) + EPS)
    return r, x * r


def _rms_bwd(r, xn, dxn):
    return r * (dxn - xn * jnp.mean(dxn * xn, axis=-1, keepdims=True))


def _colsum(t):
    return jnp.sum(t, axis=0, keepdims=True)


def _sigmoid(x):
    return 1.0 / (1.0 + jnp.exp(-x))


def _modulated_norm(x, g, shift, scale):
    _, xn = _rms(x)
    return (xn * g) * (1.0 + scale) + shift


def _pre1(x, modv, g_mix):
    def fn(t, v):
        (xt,), (mv, g) = t, v
        return [_modulated_norm(xt, g, mv[0:1], mv[1:2])], []
    return _rowwise(fn, "pre1", [(x, D, 0)], [modv, g_mix], [(D, bf16)])[0]


def _post1(x, mix, modv, g_ffn):
    def fn(t, v):
        (xt, mt), (mv, g) = t, v
        x1 = xt + mv[2:3] * mt
        return [x1, _modulated_norm(x1, g, mv[3:4], mv[4:5])], []
    return _rowwise(fn, "post1", [(x, D, 0), (mix, D, 0)], [modv, g_ffn], [(D, f32), (D, bf16)])


def _ffn_in(h, w_stack):
    def body(h_ref, w_ref, act_ref, au_ref):
        p = jnp.dot(h_ref[...], w_ref[0], preferred_element_type=f32)
        a, u = p[:, :FF_PAD], p[:, FF_PAD:]
        act_ref[...] = (a * _sigmoid(a) * u).astype(act_ref.dtype)
        au_ref[...] = p.astype(au_ref.dtype)

    return pl.pallas_call(
        body, grid=(N_DEV,),
        in_specs=[pl.BlockSpec((SEQ, D), lambda j: (0, 0)), pl.BlockSpec((1, D, 2 * FF_PAD), lambda j: (j, 0, 0))],
        out_specs=[pl.BlockSpec((SEQ, FF_PAD), lambda j: (0, j)), pl.BlockSpec((SEQ, 2 * FF_PAD), lambda j: (0, j))],
        out_shape=(SDS((SEQ, FF_HID), bf16), SDS((SEQ, 2 * FF_HID), bf16)), name="ffn_in",
        compiler_params=_params(("parallel",)),
    )(h, w_stack)


def _ffn_bwd_in(dff, w_down_stack, au):
    def body(d_ref, w_ref, au_ref, o_ref):
        dact = _nt(d_ref[...], w_ref[0])
        p = au_ref[...].astype(f32)
        a, u = p[:, :FF_PAD], p[:, FF_PAD:]
        sg = _sigmoid(a)
        o_ref[...] = jnp.concatenate([dact * u * (sg * (1.0 + a * (1.0 - sg))), dact * (a * sg)],
                                     axis=1).astype(o_ref.dtype)

    return pl.pallas_call(
        body, grid=(N_DEV,),
        in_specs=[pl.BlockSpec((SEQ, D), lambda j: (0, 0)), pl.BlockSpec((1, FF_PAD, D), lambda j: (j, 0, 0)),
                  pl.BlockSpec((SEQ, 2 * FF_PAD), lambda j: (0, j))],
        out_specs=pl.BlockSpec((SEQ, 2 * FF_PAD), lambda j: (0, j)),
        out_shape=SDS((SEQ, 2 * FF_HID), bf16), name="ffn_bwd_in", compiler_params=_params(("parallel",)),
    )(dff, w_down_stack, au)


def _final(x1, ff, target, modv, g_final):
    def fn(t, v):
        (x1t, fft, tgt), (mv, g) = t, v
        x2 = x1t + mv[5:6] * fft
        r, xn = _rms(x2)
        err = xn * g - tgt
        dy = err * (1.0 / D)
        dx2 = _rms_bwd(r, xn, dy * g)
        return [dx2, dx2 * mv[5:6]], [_colsum(dy * xn), _colsum(dx2 * fft), _colsum(err * err) * (0.5 / D)]
    return _rowwise(fn, "final", [(x1, D, 0), (ff, D, 0), (target, D, 0)], [modv, g_final],
                    [(D, f32), (D, bf16)], [D, D, D])


def _mid_bwd(dh2, x1, dx2, mix, modv, g_ffn):
    def fn(t, v):
        (dh, x1t, dx2t, mt), (mv, g) = t, v
        r, xn = _rms(x1t)
        dn = dh * (1.0 + mv[4:5])
        dx1 = dx2t + _rms_bwd(r, xn, dn * g)
        return [dx1, dx1 * mv[2:3]], [_colsum(dh), _colsum(dh * (xn * g)), _colsum(dn * xn), _colsum(dx1 * mt)]
    return _rowwise(fn, "mid_bwd", [(dh2, D, 0), (x1, D, 0), (dx2, D, 0), (mix, D, 0)], [modv, g_ffn],
                    [(D, f32), (D, bf16)], [D, D, D, D])


def _first_bwd(dh1, x, dx1, modv, g_mix):
    def fn(t, v):
        (dh, xt, dx1t), (mv, g) = t, v
        r, xn = _rms(xt)
        dn = dh * (1.0 + mv[1:2])
        return [dx1t + _rms_bwd(r, xn, dn * g)], [_colsum(dh), _colsum(dh * (xn * g)), _colsum(dn * xn)]
    return _rowwise(fn, "first_bwd", [(dh1, D, 0), (x, D, 0), (dx1, D, 0)], [modv, g_mix], [(D, f32)], [D, D, D])


def _merge_fwd(ya, yb, proj):
    def fn(t, v):
        ya_t, yb_t, ga, gb = t
        return [_sigmoid(ga) * ya_t + _sigmoid(gb) * yb_t], []
    return _rowwise(fn, "merge_fwd", [(ya, D, 0), (yb, D, 0), (proj, D, C_GA // D), (proj, D, C_GB // D)], [],
                    [(D, bf16)])[0]


def _merge_bwd(dmerged, ya, yb, proj):
    def fn(t, v):
        dm, ya_t, yb_t, ga, gb = t
        sa, sb = _sigmoid(ga), _sigmoid(gb)
        return [dm * sa, dm * sb, dm * ya_t * (sa * (1.0 - sa)), dm * yb_t * (sb * (1.0 - sb))], []
    return _rowwise(fn, "merge_bwd",
                    [(dmerged, D, 0), (ya, D, 0), (yb, D, 0), (proj, D, C_GA // D), (proj, D, C_GB // D)], [],
                    [(D, bf16), (D, bf16), (D, bf16), (D, bf16)])


def _rope_tables():
    half = ROT_DIM // 2
    pos = jnp.arange(SEQ, dtype=f32)
    inv_freq = ROPE_THETA ** (-jnp.arange(0, ROT_DIM, 2, dtype=f32) / ROT_DIM)
    ang = pos[:, None] * inv_freq[None, :]
    cos, sin = jnp.cos(ang), jnp.sin(ang)
    pad = jnp.zeros((SEQ, HEAD_DIM - ROT_DIM), f32)
    zero = jnp.zeros((SEQ, half), f32)
    c_head = jnp.concatenate([cos, cos, pad + 1.0], axis=1)
    lo_head = jnp.concatenate([-sin, zero, pad], axis=1)
    hi_head = jnp.concatenate([zero, sin, pad], axis=1)
    return tuple(jnp.concatenate([t, t], axis=1) for t in (c_head, lo_head, hi_head))


def _over_heads(tables):
    return [jnp.tile(t, (1, DIL_W // LANES)) for t in tables]


def _rope_fwd(proj, tables):
    half = ROT_DIM // 2

    def fn(t, v):
        q, k, vv = t[:3]
        c, lo, hi = _over_heads(t[3:])
        rot = lambda z: z * c + pltpu.roll(z, DIL_W - half, 1) * lo + pltpu.roll(z, half, 1) * hi
        return [rot(q) * ATT_SCALE, rot(k), vv], []
    return _rowwise(fn, "rope_fwd", [(proj, DIL_W, C_QB // DIL_W), (proj, DIL_W, C_KB // DIL_W),
                                     (proj, DIL_W, C_VB // DIL_W)] + [(tb, LANES, 0) for tb in tables], [],
                    [(DIL_W, f32)] * 3)


def _rope_bwd(dqs, dks, tables):
    half = ROT_DIM // 2

    def fn(t, v):
        dq_t, dk_t = jnp.concatenate(t[:N_GROUPS], axis=1), jnp.concatenate(t[N_GROUPS:2 * N_GROUPS], axis=1)
        c, lo, hi = _over_heads(t[2 * N_GROUPS:])
        rot_t = lambda z: z * c + pltpu.roll(z * lo, half, 1) + pltpu.roll(z * hi, DIL_W - half, 1)
        return [rot_t(dq_t), rot_t(dk_t)], []
    return _rowwise(fn, "rope_bwd", [(a, DIL_OUT_W, 0) for a in (*dqs, *dks)] + [(tb, LANES, 0) for tb in tables],
                    [], [(DIL_W, bf16), (DIL_W, bf16)])


def _head_bcast_sum(d):
    lane = lax.broadcasted_iota(jnp.int32, d.shape, 1)
    out = jnp.zeros_like(d)
    for h in range(d.shape[1] // HEAD_DIM):
        sel = (lane >= h * HEAD_DIM) & (lane < (h + 1) * HEAD_DIM)
        out = jnp.where(sel, jnp.sum(jnp.where(sel, d, 0.0), axis=1, keepdims=True), out)
    return out


def _dil_combine(outs, lses):
    def fn(t, v):
        o0, o1, o2, l0, l1, l2 = t
        m = jnp.maximum(jnp.maximum(l0, l1), l2)
        w0, w1, w2 = jnp.exp(l0 - m), jnp.exp(l1 - m), jnp.exp(l2 - m)
        tot = w0 + w1 + w2
        return [(w0 * o0 + w1 * o1 + w2 * o2) / tot, m + jnp.log(tot)], []
    w = DIL_OUT_W
    return _rowwise(fn, "dil_combine", [(t, w, 0) for t in (*outs, *lses)], [], [(w, f32), (w, f32)])


def _dil_delta(dyb_h, yb_h):
    def fn(t, v):
        return [_head_bcast_sum(t[0] * t[1])], []
    return _rowwise(fn, "dil_delta", [(dyb_h, DIL_OUT_W, 0), (yb_h, DIL_OUT_W, 0)], [], [(DIL_OUT_W, f32)])[0]


def _adamw_math(wt, gt, mt, vt):
    mn = ADAM_B1 * mt + (1.0 - ADAM_B1) * gt
    vn = ADAM_B2 * vt + (1.0 - ADAM_B2) * (gt * gt)
    m_hat = mn / (1.0 - ADAM_B1 ** ADAM_STEP)
    v_hat = vn / (1.0 - ADAM_B2 ** ADAM_STEP)
    return -ADAM_LR * (m_hat / (jnp.sqrt(v_hat) + ADAM_EPS) + ADAM_WD * wt), mn, vn


def _adamw(w, g, m, v, name):
    shape = w.shape
    if w.ndim == 1:
        w, g, m, v = (t.reshape(1, -1) for t in (w, g, m, v))
    rows, cols = w.shape
    if rows % 8 and rows > 8:
        return _adamw_by_cols(w, g, m, v, name)
    tile = 256 if rows % 256 == 0 and rows > 512 else rows

    def fn(t, _):
        return list(_adamw_math(*t)), []
    delta, mn, vn = _rowwise(fn, name, [(w, cols, 0), (g, cols, 0), (m, cols, 0), (v, cols, 0)], [],
                             [(cols, f32)] * 3, tile=tile)
    return delta.reshape(shape), mn.reshape(shape), vn.reshape(shape)


def _adamw_by_cols(w, g, m, v, name, tile=256):
    rows, cols = w.shape

    def body(w_ref, g_ref, m_ref, v_ref, d_ref, mn_ref, vn_ref):
        d_ref[...], mn_ref[...], vn_ref[...] = _adamw_math(w_ref[...], g_ref[...], m_ref[...], v_ref[...])

    spec = pl.BlockSpec((rows, tile), lambda j: (0, j))
    return pl.pallas_call(body, grid=(cols // tile,), in_specs=[spec] * 4, out_specs=[spec] * 3,
                          out_shape=[SDS((rows, cols), f32)] * 3, name=name,
                          compiler_params=_params(("parallel",)))(w, g, m, v)


def _ada_fwd(c_all, w_shard, b_shard):
    def body(c_ref, w_ref, b_ref, o_ref):
        cv = c_ref[...]
        sc = (cv * _sigmoid(cv)).astype(bf16)
        o_ref[...] = jnp.dot(sc, w_ref[...].astype(bf16), preferred_element_type=f32) + b_ref[...]
    return pl.pallas_call(body, out_shape=SDS((N_DEV, w_shard.shape[1]), f32), name="ada_fwd",
                          compiler_params=_params())(c_all, w_shard, b_shard)


def _ada_bwd(c_all, dmod_cols):
    def body(c_ref, d_ref, o_ref):
        cv = c_ref[...]
        sc = cv * _sigmoid(cv)
        o_ref[...] = lax.dot_general(sc, d_ref[...], (((0,), (0,)), ((), ())), precision=lax.Precision.HIGHEST,
                                     preferred_element_type=f32)
    return pl.pallas_call(body, out_shape=SDS((D, dmod_cols.shape[1]), f32), name="ada_bwd",
                          compiler_params=_params())(c_all, dmod_cols)


def _small_reduce(gathered, after):
    def body(g_ref, after_ref, o_ref, loss_ref):
        acc = g_ref[0]
        for d in range(1, N_DEV):
            acc = acc + g_ref[d]
        o_ref[...] = acc
        loss_ref[...] = jnp.zeros((1, LANES), f32) + jnp.sum(acc[10:11, :])
    return pl.pallas_call(body, out_shape=(SDS((SMALL_ROWS, D), f32), SDS((1, LANES), f32)), name="small_reduce",
                          in_specs=[pl.BlockSpec(memory_space=pltpu.VMEM), pl.BlockSpec(memory_space=pl.ANY)],
                          compiler_params=_params())(gathered, after)


FOX_BLK = 512
CUM_BLK = 128


def _fold_lanes(t, op):
    out = t[:, :LANES]
    for j in range(1, t.shape[1] // LANES):
        out = op(out, t[:, j * LANES:(j + 1) * LANES])
    return out


def _fox_gate_fwd(proj, b_pad):
    nblk = SEQ // CUM_BLK

    def body(f_ref, b_ref, col_ref):
        r = lax.broadcasted_iota(jnp.int32, (CUM_BLK, CUM_BLK), 0)
        c = lax.broadcasted_iota(jnp.int32, (CUM_BLK, CUM_BLK), 1)
        tri = (r >= c).astype(f32)
        carry = jnp.zeros((1, LANES), f32)
        for blk in range(nblk):
            z = f_ref[blk * CUM_BLK:(blk + 1) * CUM_BLK, :] + b_ref[...]
            logf = jnp.minimum(z, 0.0) - jnp.log1p(jnp.exp(-jnp.abs(z)))
            cs = jnp.dot(tri, logf, precision=lax.Precision.HIGHEST, preferred_element_type=f32) + carry
            col_ref[blk * CUM_BLK:(blk + 1) * CUM_BLK, :] = cs
            carry = cs[CUM_BLK - 1:CUM_BLK, :]

    return pl.pallas_call(
        body, grid=(1,), in_specs=[pl.BlockSpec((SEQ, LANES), lambda i: (0, C_F // LANES)),
                                   pl.BlockSpec((1, LANES), lambda i: (0, 0))],
        out_specs=pl.BlockSpec((SEQ, LANES), lambda i: (0, 0)),
        out_shape=SDS((SEQ, LANES), f32), name="fox_gate_fwd",
        compiler_params=_params(("arbitrary",)),
    )(proj, b_pad)


def _fox_gate_bwd(dF_row, proj, b_pad):
    nblk = SEQ // CUM_BLK

    def body(d_ref, f_ref, b_ref, df_ref, db_ref, col_ref):
        r = lax.broadcasted_iota(jnp.int32, (CUM_BLK, CUM_BLK), 0)
        c = lax.broadcasted_iota(jnp.int32, (CUM_BLK, CUM_BLK), 1)
        tri = (r <= c).astype(f32)
        lane = lax.broadcasted_iota(jnp.int32, (CUM_BLK, LANES), 1)
        col_ref[...] = d_ref[...].T
        carry = jnp.zeros((1, LANES), f32)
        total = jnp.zeros((1, LANES), f32)
        for blk in reversed(range(nblk)):
            rows = slice(blk * CUM_BLK, (blk + 1) * CUM_BLK)
            cs = jnp.dot(tri, col_ref[rows, :], precision=lax.Precision.HIGHEST, preferred_element_type=f32) + carry
            carry = cs[0:1, :]
            z = f_ref[rows, :] + b_ref[...]
            df = jnp.where(lane < N_FOX_HEADS, cs * _sigmoid(-z), 0.0)
            df_ref[rows, :] = df.astype(df_ref.dtype)
            total = total + _colsum(df)
        db_ref[...] = total

    return pl.pallas_call(
        body, grid=(1,), in_specs=[pl.BlockSpec((LANES, SEQ), lambda i: (0, 0)),
                                   pl.BlockSpec((SEQ, LANES), lambda i: (0, C_F // LANES)),
                                   pl.BlockSpec((1, LANES), lambda i: (0, 0))],
        out_specs=[pl.BlockSpec((SEQ, LANES), lambda i: (0, 0)), pl.BlockSpec((1, LANES), lambda i: (0, 0))],
        out_shape=(SDS((SEQ, LANES), bf16), SDS((1, LANES), f32)), name="fox_gate_bwd",
        scratch_shapes=[pltpu.VMEM((SEQ, LANES), f32)],
        compiler_params=_params(("arbitrary",)),
    )(dF_row, proj, b_pad)


def _nt(a, b):
    return lax.dot_general(a, b, (((1,), (1,)), ((), ())), preferred_element_type=f32)


def _tn(a, b):
    return lax.dot_general(a, b, (((0,), (0,)), ((), ())), preferred_element_type=f32)


def _fox_prep(proj, f_col):
    def fn(t, v):
        q, k, vv, fc = t
        lane = lax.broadcasted_iota(jnp.int32, (q.shape[0], LANES), 1)
        qs, ks = [], []
        for h in range(N_FOX_HEADS):
            pair, pos = divmod(h, 2)
            own = (lane >= pos * HEAD_DIM) & (lane < (pos + 1) * HEAD_DIM)
            base = (1 - pos) * HEAD_DIM
            f = fc[:, h:h + 1]
            hi = f.astype(bf16).astype(f32)
            mid = (f - hi).astype(bf16).astype(f32)
            lo = (f - hi) - mid
            one = jnp.ones_like(f)
            qa = jnp.where(own, q[:, pair * LANES:(pair + 1) * LANES] * ATT_SCALE, 0.0)
            ka = k[:, pair * LANES:(pair + 1) * LANES]
            for idx, (qv, kv) in enumerate([(hi, one), (mid, one), (lo, one), (one, -hi), (one, -mid), (one, -lo)]):
                sel = lane == base + idx
                qa = jnp.where(sel, qv, qa)
                ka = jnp.where(sel, kv, ka)
            qs.append(qa)
            ks.append(ka)
        return [jnp.concatenate(qs, axis=1), jnp.concatenate(ks, axis=1), vv], []
    w = N_FOX_HEADS * LANES
    return _rowwise(fn, "fox_prep", [(proj, FOX_W, C_QA // FOX_W), (proj, FOX_W, C_KA // FOX_W),
                                     (proj, FOX_W, C_VA // FOX_W), (f_col, LANES, 0)], [],
                    [(w, bf16), (w, bf16), (FOX_W, bf16)])


def _fox_fwd(q_aug, k_aug, v):
    blk = FOX_BLK
    npair = FOX_W // LANES

    def body(q_ref, k_ref, v_ref, o_ref, lse_ref, s_scr):
        i = pl.program_id(1)
        tri = lax.broadcasted_iota(jnp.int32, (blk, blk), 0) >= lax.broadcasted_iota(jnp.int32, (blk, blk), 1)
        qh = [q_ref[:, h * LANES:(h + 1) * LANES] for h in range(2)]

        def logits(c, masked):
            off = pl.multiple_of(c * blk, blk)
            tops = []
            for h in range(2):
                s = _nt(qh[h], k_ref[pl.ds(off, blk), h * LANES:(h + 1) * LANES])
                if masked:
                    s = jnp.where(tri, s, NEG)
                s_scr[h, :, pl.ds(off, blk)] = s
                tops.append(_fold_lanes(s, jnp.maximum))
            return tops

        def pass_a(c, m):
            return tuple(jnp.maximum(a, b) for a, b in zip(m, logits(c, False)))

        m = lax.fori_loop(0, i, pass_a, tuple(jnp.full((blk, LANES), NEG, f32) for _ in range(2)))
        mx = [jnp.max(jnp.maximum(a, b), axis=1, keepdims=True) for a, b in zip(m, logits(i, True))]

        def pass_b(c, carry):
            off = pl.multiple_of(c * blk, blk)
            vv = v_ref[pl.ds(off, blk), :]
            new = []
            for h in range(2):
                l, acc = carry[h]
                p = jnp.exp(s_scr[h, :, pl.ds(off, blk)] - mx[h])
                hi = p.astype(bf16)
                lo = (p - hi.astype(f32)).astype(bf16)
                new.append((l + _fold_lanes(p, jnp.add),
                            acc + jnp.dot(hi, vv, preferred_element_type=f32)
                            + jnp.dot(lo, vv, preferred_element_type=f32)))
            return tuple(new)

        zero = jnp.zeros((blk, LANES), f32)
        (l_a, acc_a), (l_b, acc_b) = lax.fori_loop(0, i + 1, pass_b, ((zero, zero), (zero, zero)))
        l_a = jnp.sum(l_a, axis=1, keepdims=True)
        l_b = jnp.sum(l_b, axis=1, keepdims=True)
        first = lax.broadcasted_iota(jnp.int32, (blk, LANES), 1) < HEAD_DIM
        o_ref[...] = jnp.where(first, acc_a / l_a, acc_b / l_b)
        lse_ref[0] = jnp.where(first, mx[0] + jnp.log(l_a), mx[1] + jnp.log(l_b))

    return pl.pallas_call(
        body, grid=(npair, SEQ // blk),
        in_specs=[pl.BlockSpec((blk, 2 * LANES), lambda p, i: (i, p)),
                  pl.BlockSpec((SEQ, 2 * LANES), lambda p, i: (0, p)),
                  pl.BlockSpec((SEQ, LANES), lambda p, i: (0, p))],
        out_specs=[pl.BlockSpec((blk, LANES), lambda p, i: (i, p)),
                   pl.BlockSpec((1, blk, LANES), lambda p, i: (p, i, 0))],
        out_shape=(SDS((SEQ, FOX_W), f32), SDS((npair, SEQ, LANES), f32)), name="fox_fwd",
        scratch_shapes=[pltpu.VMEM((2, blk, SEQ), f32)],
        compiler_params=_params(("parallel", "arbitrary")),
    )(q_aug, k_aug, v)


def _fox_bwd(q_aug, k_aug, v, do, o, lse, after):
    blk = FOX_BLK
    npair = FOX_W // LANES
    nblk = SEQ // blk

    def body(q_ref, k_ref, v_ref, do_ref, o_ref, lse_ref, after_ref, dq_ref, dk_ref, dv_ref, df_ref, dq_acc,
             delta_ref):
        lane_s = lax.broadcasted_iota(jnp.int32, (SEQ, LANES), 1)
        prod = do_ref[...].astype(bf16).astype(f32) * o_ref[...]
        d_a = jnp.sum(jnp.where(lane_s < HEAD_DIM, prod, 0.0), axis=1, keepdims=True)
        d_b = jnp.sum(jnp.where(lane_s >= HEAD_DIM, prod, 0.0), axis=1, keepdims=True)
        delta_ref[...] = jnp.where(lane_s < HEAD_DIM, d_a, d_b)
        dq_acc[...] = jnp.zeros_like(dq_acc)
        df_ref[...] = jnp.zeros_like(df_ref)
        lane = lax.broadcasted_iota(jnp.int32, (blk, LANES), 1)
        own = [lane < HEAD_DIM, lane >= HEAD_DIM]
        tri = lax.broadcasted_iota(jnp.int32, (blk, blk), 0) >= lax.broadcasted_iota(jnp.int32, (blk, blk), 1)

        def q_slab(qoff, h):
            return q_ref[pl.ds(qoff, blk), h * LANES:(h + 1) * LANES]

        def probs(qoff, h, k_h, masked):
            s = _nt(q_slab(qoff, h), k_h)
            if masked:
                s = jnp.where(tri, s, NEG)
            return jnp.exp(s - lse_ref[0, pl.ds(qoff, blk), h * HEAD_DIM:h * HEAD_DIM + 1])

        def k_slabs(koff):
            return [k_ref[pl.ds(koff, blk), h * LANES:(h + 1) * LANES] for h in range(2)]

        def kv_step(kj, _):
            koff = pl.multiple_of(kj * blk, blk)
            k_aug = k_slabs(koff)
            k_own = [jnp.where(own[h], k_aug[h], jnp.zeros_like(k_aug[h])) for h in range(2)]
            vv = v_ref[pl.ds(koff, blk), :]
            v_own = [jnp.where(own[h], vv, jnp.zeros_like(vv)) for h in range(2)]

            def q_tile(qi, carry, masked):
                qoff = pl.multiple_of(qi * blk, blk)
                dd = do_ref[pl.ds(qoff, blk), :].astype(bf16)
                new, dq_add = [], None
                for h in range(2):
                    dk_h, dv_h, dcol = carry[h]
                    p = probs(qoff, h, k_aug[h], masked)
                    dl = p * (_nt(dd, v_own[h]) - delta_ref[pl.ds(qoff, blk), h * HEAD_DIM:h * HEAD_DIM + 1])
                    dlb = dl.astype(bf16)
                    part = jnp.dot(dlb, k_own[h], preferred_element_type=f32)
                    dq_add = part if dq_add is None else dq_add + part
                    new.append((dk_h + _tn(dlb, q_slab(qoff, h)), dv_h + _tn(p.astype(bf16), dd),
                                dcol + _colsum(dl)))
                dq_acc[pl.ds(qoff, blk), :] += dq_add * ATT_SCALE
                return tuple(new)

            zero = (jnp.zeros((blk, LANES), f32), jnp.zeros((blk, LANES), f32), jnp.zeros((1, blk), f32))
            carry = q_tile(kj, (zero, zero), True)
            (dk_a, dv_a, dcol_a), (dk_b, dv_b, dcol_b) = lax.fori_loop(
                kj + 1, nblk, lambda qi, cr: q_tile(qi, cr, False), carry)
            dk_ref[pl.ds(koff, blk), :] = jnp.where(own[0], dk_a, dk_b).astype(dk_ref.dtype)
            dv_ref[pl.ds(koff, blk), :] = jnp.where(own[0], dv_a, dv_b).astype(dv_ref.dtype)
            df_ref[0, 0:1, pl.ds(koff, blk)] = -dcol_a
            df_ref[0, 1:2, pl.ds(koff, blk)] = -dcol_b
            return 0

        lax.fori_loop(0, nblk, kv_step, 0)
        dq_ref[...] = dq_acc[...].astype(dq_ref.dtype)

    pair_aug = pl.BlockSpec((SEQ, 2 * LANES), lambda p: (0, p))
    slab = pl.BlockSpec((SEQ, LANES), lambda p: (0, p))
    per_pair = pl.BlockSpec((1, SEQ, LANES), lambda p: (p, 0, 0))
    rows = pl.BlockSpec((1, 8, SEQ), lambda p: (p, 0, 0))
    return pl.pallas_call(
        body, grid=(npair,),
        in_specs=[pair_aug, pair_aug, slab, slab, slab, per_pair, pl.BlockSpec(memory_space=pl.ANY)],
        out_specs=[slab, slab, slab, rows],
        out_shape=(SDS((SEQ, FOX_W), bf16),) * 3 + (SDS((npair, 8, SEQ), f32),), name="fox_bwd",
        scratch_shapes=[pltpu.VMEM((SEQ, LANES), f32), pltpu.VMEM((SEQ, LANES), f32)],
        compiler_params=_params(("parallel",)),
    )(q_aug, k_aug, v, do, o, lse, after)


DIL_BLK = 128
DILATIONS = (1, 4, 16)
N_GROUPS = len(DILATIONS)
DIL_PAIRS = DIL_OUT_W // LANES


def _dil_blocks(d):
    r1 = lax.broadcasted_iota(jnp.int32, (DIL_BLK, DIL_BLK), 0)
    c1 = lax.broadcasted_iota(jnp.int32, (DIL_BLK, DIL_BLK), 1)
    r2 = lax.broadcasted_iota(jnp.int32, (DIL_BLK, 2 * DIL_BLK), 0)
    c2 = lax.broadcasted_iota(jnp.int32, (DIL_BLK, 2 * DIL_BLK), 1)
    band = ((c2 < DIL_BLK) & (c2 >= r2)) | ((c2 >= DIL_BLK) & (c2 - DIL_BLK <= r2))
    out = []
    for r in range(d):
        for b in range(SEQ // d // DIL_BLK):
            rows = pl.ds(r + d * DIL_BLK * b, DIL_BLK, stride=d)
            if b == 0:
                out.append((rows, rows, r1 >= c1))
            else:
                out.append((rows, pl.ds(r + d * DIL_BLK * (b - 1), 2 * DIL_BLK, stride=d), band))
    return out


def _dil_fwd(q, k, v, g):
    def body(q_ref, k_ref, v_ref, o_ref, lse_ref):
        first = lax.broadcasted_iota(jnp.int32, (DIL_BLK, LANES), 1) < HEAD_DIM
        for rows, krows, mask in _dil_blocks(DILATIONS[g]):
            qv, kk, vv = q_ref[rows, :].astype(bf16), k_ref[krows, :].astype(bf16), v_ref[krows, :].astype(bf16)
            outs, lses = [], []
            for own in (first, ~first):
                s = jnp.where(mask, _nt(jnp.where(own, qv, jnp.zeros_like(qv)), kk), NEG)
                m = jnp.max(s, axis=1, keepdims=True)
                p = jnp.exp(s - m)
                l = jnp.sum(p, axis=1, keepdims=True)
                outs.append(jnp.dot(p.astype(bf16), vv, preferred_element_type=f32) / l)
                lses.append(m + jnp.log(l))
            o_ref[rows, :] = jnp.where(first, outs[0], outs[1])
            lse_ref[rows, :] = jnp.where(first, lses[0], lses[1])

    grouped = pl.BlockSpec((SEQ, LANES), lambda p: (0, DIL_PAIRS * g + p))
    own = pl.BlockSpec((SEQ, LANES), lambda p: (0, p))
    shape = SDS((SEQ, DIL_OUT_W), f32)
    return pl.pallas_call(
        body, grid=(DIL_PAIRS,), in_specs=[grouped] * 3, out_specs=[own] * 2, out_shape=(shape, shape),
        name=f"dil_fwd_{DILATIONS[g]}", compiler_params=_params(("parallel",)),
    )(q, k, v)


def _dil_bwd(q, k, v, do, lse, delta, g):
    def body(q_ref, k_ref, v_ref, do_ref, lse_ref, dl_ref, dq_ref, dk_ref, dv_ref):
        first = lax.broadcasted_iota(jnp.int32, (DIL_BLK, LANES), 1) < HEAD_DIM
        dk_ref[...] = jnp.zeros_like(dk_ref)
        dv_ref[...] = jnp.zeros_like(dv_ref)
        for rows, krows, mask in _dil_blocks(DILATIONS[g]):
            qv, kk, vv = q_ref[rows, :].astype(bf16), k_ref[krows, :].astype(bf16), v_ref[krows, :].astype(bf16)
            dov = do_ref[rows, :].astype(bf16)
            lsev, delv = lse_ref[rows, :], dl_ref[rows, :]
            dqs, dk_add, dv_add = [], None, None
            for h, own in enumerate((first, ~first)):
                col = h * HEAD_DIM
                qh = jnp.where(own, qv, jnp.zeros_like(qv))
                doh = jnp.where(own, dov, jnp.zeros_like(dov))
                p = jnp.exp(jnp.where(mask, _nt(qh, kk), NEG) - lsev[:, col:col + 1])
                dl = (p * (_nt(doh, vv) - delv[:, col:col + 1])).astype(bf16)
                dqs.append(jnp.dot(dl, kk, preferred_element_type=f32))
                dk_h, dv_h = _tn(dl, qh), _tn(p.astype(bf16), doh)
                dk_add = dk_h if dk_add is None else dk_add + dk_h
                dv_add = dv_h if dv_add is None else dv_add + dv_h
            dq_ref[rows, :] = jnp.where(first, dqs[0], dqs[1]) * ATT_SCALE
            dk_ref[krows, :] += dk_add
            dv_ref[krows, :] += dv_add

    grouped = pl.BlockSpec((SEQ, LANES), lambda p: (0, DIL_PAIRS * g + p))
    own = pl.BlockSpec((SEQ, LANES), lambda p: (0, p))
    shape = SDS((SEQ, DIL_OUT_W), f32)
    return pl.pallas_call(
        body, grid=(DIL_PAIRS,), in_specs=[grouped] * 3 + [own] * 3, out_specs=[own] * 3,
        out_shape=(shape, shape, shape), name=f"dil_bwd_{DILATIONS[g]}", compiler_params=_params(("parallel",)),
    )(q, k, v, do, lse, delta)


def _position():
    return lax.axis_index("x"), lax.axis_index("y"), lax.axis_index("c")


def _all_gather(block, name):
    def body(x_ref, out_ref, send_sems, recv_sems, local_sem):
        x, y, c = _position()
        me, sibling = (x, y, c), (x, y, 1 - c)
        chips = [(1 - x, y), (x, 1 - y), (1 - x, 1 - y)]

        def slot(px, py, pc):
            return out_ref.at[4 * px + 2 * py + pc]

        def copy(k, blk, to, src=None):
            return pltpu.make_async_remote_copy(
                src_ref=slot(*blk) if src is None else src, dst_ref=slot(*blk),
                send_sem=send_sems.at[k], recv_sem=recv_sems.at[k], device_id=to, device_id_type=MESH)

        mine = pltpu.make_async_copy(x_ref, slot(*me), local_sem)
        mine.start()
        first = [copy(0, me, sibling, src=x_ref)]
        first += [copy(1 + j, me, (*chip, c), src=x_ref) for j, chip in enumerate(chips)]
        for cp in first:
            cp.start()
        passed = [copy(4 + j, (*chip, c), sibling) for j, chip in enumerate(chips)]
        for j, chip in enumerate(chips):
            copy(1 + j, (*chip, c), me).wait_recv()
            passed[j].start()
        copy(0, sibling, me).wait_recv()
        for j, chip in enumerate(chips):
            copy(4 + j, (*chip, 1 - c), me).wait_recv()
        for cp in first + passed:
            cp.wait_send()
        mine.wait()

    return pl.pallas_call(
        body, out_shape=SDS((N_DEV,) + block.shape, block.dtype),
        in_specs=[pl.BlockSpec(memory_space=pl.ANY)], out_specs=pl.BlockSpec(memory_space=pl.ANY),
        scratch_shapes=[pltpu.SemaphoreType.DMA((7,)), pltpu.SemaphoreType.DMA((7,)), pltpu.SemaphoreType.DMA],
        name=name,
    )(block)


def _all_gather_many(blocks, name):
    n = len(blocks)

    def body(*refs):
        x_refs, out_refs = refs[:n], refs[n:2 * n]
        send_sems, recv_sems, local_sems = refs[2 * n:]
        x, y, c = _position()
        me, sibling = (x, y, c), (x, y, 1 - c)
        chips = [(1 - x, y), (x, 1 - y), (1 - x, 1 - y)]

        def slot(a, px, py, pc):
            return out_refs[a].at[4 * px + 2 * py + pc]

        def copy(a, k, blk, to, own=False):
            return pltpu.make_async_remote_copy(
                src_ref=x_refs[a] if own else slot(a, *blk), dst_ref=slot(a, *blk),
                send_sem=send_sems.at[a, k], recv_sem=recv_sems.at[a, k], device_id=to, device_id_type=MESH)

        mine = [pltpu.make_async_copy(x_refs[a], slot(a, *me), local_sems.at[a]) for a in range(n)]
        for cp in mine:
            cp.start()
        started = []
        for a in range(n):
            first = [copy(a, 0, me, sibling, own=True)]
            first += [copy(a, 1 + j, me, (*chip, c), own=True) for j, chip in enumerate(chips)]
            for cp in first:
                cp.start()
            started += first
        for a in range(n):
            for j, chip in enumerate(chips):
                copy(a, 1 + j, (*chip, c), me).wait_recv()
                passed = copy(a, 4 + j, (*chip, c), sibling)
                passed.start()
                started.append(passed)
        for a in range(n):
            copy(a, 0, sibling, me).wait_recv()
            for j, chip in enumerate(chips):
                copy(a, 4 + j, (*chip, 1 - c), me).wait_recv()
        for cp in started:
            cp.wait_send()
        for cp in mine:
            cp.wait()

    hbm = pl.BlockSpec(memory_space=pl.ANY)
    return pl.pallas_call(
        body, out_shape=[SDS((N_DEV,) + b.shape, b.dtype) for b in blocks],
        in_specs=[hbm] * n, out_specs=[hbm] * n,
        scratch_shapes=[pltpu.SemaphoreType.DMA((n, 7)), pltpu.SemaphoreType.DMA((n, 7)),
                        pltpu.SemaphoreType.DMA((n,))],
        name=name,
    )(*blocks)


HBM_SPEC = pl.BlockSpec(memory_space=pltpu.HBM)
SEM_SPEC = pl.BlockSpec(memory_space=pltpu.SEMAPHORE)
SPLIT_COPY = pltpu.CompilerParams(has_side_effects=pltpu.SideEffectType.DATAFLOW_SIDE_EFFECTING)


def _in_hbm(t):
    return pltpu.with_memory_space_constraint(t, pltpu.HBM)


def _pair_copies(g_refs, land_refs, send_sems, recv_sems):
    x, y, c = _position()
    return [pltpu.make_async_remote_copy(
        src_ref=g.at[2 * k + (1 - c)], dst_ref=land.at[k], send_sem=send_sems.at[4 * a + k],
        recv_sem=recv_sems.at[4 * a + k], device_id=(x, y, 1 - c), device_id_type=MESH)
        for a, (g, land) in enumerate(zip(g_refs, land_refs, strict=True)) for k in range(4)]


def _chip_copies(t_refs, land_refs, send_sems, recv_sems):
    x, y, c = _position()
    chips = [(1 - x, y), (x, 1 - y), (1 - x, 1 - y)]
    return [pltpu.make_async_remote_copy(
        src_ref=t.at[2 * px + py], dst_ref=land.at[j], send_sem=send_sems.at[3 * a + j],
        recv_sem=recv_sems.at[3 * a + j], device_id=(px, py, c), device_id_type=MESH)
        for a, (t, land) in enumerate(zip(t_refs, land_refs, strict=True)) for j, (px, py) in enumerate(chips)]


_ROUNDS = {"pair": (_pair_copies, 4), "chip": (_chip_copies, 3)}


def _exchange_start(kind, ts, name):
    copies, slots = _ROUNDS[kind]
    n = len(ts)
    lands = [_in_hbm(lax.empty((slots,) + t.shape[1:], t.dtype)) for t in ts]

    def body(*refs):
        for cp in copies(refs[:n], refs[n:2 * n], refs[2 * n], refs[2 * n + 1]):
            cp.start()
        refs[-1][...] = jnp.zeros_like(refs[-1])

    sems = pltpu.SemaphoreType.DMA((slots * n,))
    res = pl.pallas_call(
        body, name=name, in_specs=[HBM_SPEC] * (2 * n),
        out_shape=(sems, sems, *[pltpu.HBM(t.shape, t.dtype) for t in (*ts, *lands)], SDS((8, LANES), f32)),
        out_specs=(SEM_SPEC, SEM_SPEC, *[HBM_SPEC] * (2 * n), pl.BlockSpec(memory_space=pltpu.VMEM)),
        input_output_aliases={i: 2 + i for i in range(2 * n)}, compiler_params=SPLIT_COPY,
    )(*[_in_hbm(t) for t in ts], *lands)
    return res[:-1], res[-1]


def _exchange_wait(kind, state, after, name):
    copies, _ = _ROUNDS[kind]
    send_sems, recv_sems, *arrays = state
    n = len(arrays) // 2

    def body(*refs):
        for cp in copies(refs[:n], refs[n:2 * n], refs[2 * n], refs[2 * n + 1]):
            cp.wait_send()
            cp.wait_recv()

    res = pl.pallas_call(
        body, name=name, in_specs=[HBM_SPEC] * (2 * n) + [SEM_SPEC, SEM_SPEC, pl.BlockSpec(memory_space=pl.ANY)],
        out_shape=[pltpu.HBM(t.shape, t.dtype) for t in arrays], out_specs=[HBM_SPEC] * (2 * n),
        input_output_aliases={i: i for i in range(2 * n)}, compiler_params=SPLIT_COPY,
    )(*arrays, send_sems, recv_sems, after)
    return res[:n], res[n:]


def _gather_copies(x_refs, out_refs, send_sems, recv_sems):
    x, y, c = _position()
    peers = [(x, y, 1 - c), (1 - x, y, c), (x, 1 - y, c), (1 - x, 1 - y, c)]
    sends, arrivals = [], []
    for a, (x_ref, out_ref) in enumerate(zip(x_refs, out_refs, strict=True)):
        for k, (px, py, pc) in enumerate(peers):
            sems = dict(send_sem=send_sems.at[4 * a + k], recv_sem=recv_sems.at[4 * a + k],
                        device_id=(px, py, pc), device_id_type=MESH)
            sends.append(pltpu.make_async_remote_copy(src_ref=x_ref, dst_ref=out_ref.at[4 * x + 2 * y + c], **sems))
            arrivals.append(pltpu.make_async_remote_copy(src_ref=x_ref, dst_ref=out_ref.at[4 * px + 2 * py + pc],
                                                         **sems))
    return sends, arrivals


def _gather_start(blocks, after, name):
    n = len(blocks)
    outs = [_in_hbm(lax.empty((N_DEV,) + b.shape, b.dtype)) for b in blocks]

    def body(*refs):
        sends, _ = _gather_copies(refs[:n], refs[n:2 * n], refs[2 * n + 1], refs[2 * n + 2])
        for cp in sends:
            cp.start()
        refs[-1][...] = jnp.zeros_like(refs[-1])

    sems = pltpu.SemaphoreType.DMA((4 * n,))
    res = pl.pallas_call(
        body, name=name, in_specs=[HBM_SPEC] * (2 * n) + [pl.BlockSpec(memory_space=pl.ANY)],
        out_shape=(sems, sems, *[pltpu.HBM(t.shape, t.dtype) for t in (*blocks, *outs)], SDS((8, LANES), f32)),
        out_specs=(SEM_SPEC, SEM_SPEC, *[HBM_SPEC] * (2 * n), pl.BlockSpec(memory_space=pltpu.VMEM)),
        input_output_aliases={i: 2 + i for i in range(2 * n)}, compiler_params=SPLIT_COPY,
    )(*[_in_hbm(b) for b in blocks], *outs, after)
    return res[:-1], res[-1]


def _gather_wait(state, after, name):
    send_sems, recv_sems, *arrays = state
    n = len(arrays) // 2

    def body(*refs):
        sends, arrivals = _gather_copies(refs[:n], refs[n:2 * n], refs[2 * n], refs[2 * n + 1])
        for cp in sends:
            cp.wait_send()
        for cp in arrivals:
            cp.wait_recv()

    res = pl.pallas_call(
        body, name=name, in_specs=[HBM_SPEC] * (2 * n) + [SEM_SPEC, SEM_SPEC, pl.BlockSpec(memory_space=pl.ANY)],
        out_shape=[pltpu.HBM(t.shape, t.dtype) for t in arrays], out_specs=[HBM_SPEC] * (2 * n),
        input_output_aliases={i: i for i in range(2 * n)}, compiler_params=SPLIT_COPY,
    )(*arrays, send_sems, recv_sems, after)
    return res[:n], res[n:]


def _gather_finish(partial, name):
    n = len(partial)

    def body(*refs):
        in_refs, out_refs = refs[:n], refs[n:2 * n]
        send_sems, recv_sems = refs[2 * n:]
        x, y, c = _position()
        chips = [(1 - x, y), (x, 1 - y), (1 - x, 1 - y)]
        copies = []
        for a in range(n):
            for j, (px, py) in enumerate(chips):
                cp = pltpu.make_async_remote_copy(
                    src_ref=in_refs[a].at[4 * px + 2 * py + c], dst_ref=out_refs[a].at[4 * px + 2 * py + c],
                    send_sem=send_sems.at[a, j], recv_sem=recv_sems.at[a, j], device_id=(x, y, 1 - c),
                    device_id_type=MESH)
                cp.start()
                copies.append(cp)
        for a in range(n):
            for j, (px, py) in enumerate(chips):
                pltpu.make_async_remote_copy(
                    src_ref=in_refs[a].at[4 * px + 2 * py + (1 - c)], dst_ref=out_refs[a].at[4 * px + 2 * py + (1 - c)],
                    send_sem=send_sems.at[a, j], recv_sem=recv_sems.at[a, j], device_id=(x, y, 1 - c),
                    device_id_type=MESH).wait_recv()
        for cp in copies:
            cp.wait_send()

    hbm = pl.BlockSpec(memory_space=pl.ANY)
    return pl.pallas_call(
        body, out_shape=[SDS(p.shape, p.dtype) for p in partial], in_specs=[hbm] * n, out_specs=[hbm] * n,
        input_output_aliases={a: a for a in range(n)},
        scratch_shapes=[pltpu.SemaphoreType.DMA((n, 3)), pltpu.SemaphoreType.DMA((n, 3))],
        name=name,
    )(*partial)


def _row_tile(rows):
    return 512 if rows % 512 == 0 and rows > 512 else rows


def _pair_add(g, r1, core, name):
    def body(c_ref, g_ref, r_ref, o_ref):
        o_ref[...] = (g_ref[...].astype(f32) + r_ref[...].astype(f32)).astype(o_ref.dtype)

    rows, cols = g.shape[1:]
    tile = _row_tile(rows)
    blk = (1, tile, cols)
    return pl.pallas_call(
        body, out_shape=SDS((4, rows, cols), g.dtype), name=name,
        grid_spec=pltpu.PrefetchScalarGridSpec(
            num_scalar_prefetch=1, grid=(4, rows // tile),
            in_specs=[pl.BlockSpec(blk, lambda k, i, c_ref: (2 * k + c_ref[0], i, 0)),
                      pl.BlockSpec(blk, lambda k, i, c_ref: (k, i, 0))],
            out_specs=pl.BlockSpec(blk, lambda k, i, c_ref: (k, i, 0))),
        compiler_params=_params(("parallel", "arbitrary")),
    )(core, g, r1)


def _chip_add(t, r2, chip, name):
    def body(c_ref, t_ref, r_ref, o_ref):
        o_ref[...] = ((t_ref[0].astype(f32) + r_ref[0].astype(f32)) + r_ref[1].astype(f32)) + r_ref[2].astype(f32)

    rows, cols = t.shape[1:]
    tile = _row_tile(rows)
    return pl.pallas_call(
        body, out_shape=SDS((rows, cols), f32), name=name,
        grid_spec=pltpu.PrefetchScalarGridSpec(
            num_scalar_prefetch=1, grid=(rows // tile,),
            in_specs=[pl.BlockSpec((1, tile, cols), lambda i, c_ref: (c_ref[0], i, 0)),
                      pl.BlockSpec((3, tile, cols), lambda i, c_ref: (0, i, 0))],
            out_specs=pl.BlockSpec((tile, cols), lambda i, c_ref: (i, 0))),
        compiler_params=_params(("arbitrary",)),
    )(chip, t, r2)


def _pad_to(t, axis, size):
    pads = [(0, 0)] * t.ndim
    pads[axis] = (0, size - t.shape[axis])
    return jnp.pad(t, pads)


_REF_COLS = {"qa": (0, FOX_W), "ka": (FOX_W, FOX_W), "va": (2 * FOX_W, FOX_W), "f": (3 * FOX_W, N_FOX_HEADS)}
_REF_COLS.update({n: (3 * FOX_W + N_FOX_HEADS + i * DIL_W, DIL_W) for i, n in enumerate(("qb", "kb", "vb"))})
_REF_COLS.update({n: (3 * FOX_W + N_FOX_HEADS + 3 * DIL_W + i * D, D) for i, n in enumerate(("ga", "gb"))})
_REF_ORDER = ("qa", "ka", "va", "f", "qb", "kb", "vb", "ga", "gb")


def _place_cols(sources, src_of, out_cols, name, row_block=512):
    arrays = [s[0] if isinstance(s, tuple) else s for s in sources]
    widths = [a.shape[-1] for a in arrays]
    rows = arrays[0].shape[-2]
    plan = []
    for t in range(out_cols // LANES):
        segs, c, end = [], t * LANES, (t + 1) * LANES
        while c < end:
            s = src_of(c)
            if s is None:
                c += 1
                continue
            n = 1
            while c + n < end and src_of(c + n) == (s[0], s[1] + n):
                n += 1
            segs.append((s[0], s[1], c - t * LANES, n))
            c += n
        plan.append(segs)

    def body(*refs):
        o_ref = refs[-1]
        for t, segs in enumerate(plan):
            acc = None
            for si, c0, o0, n in segs:
                a0 = c0 // LANES * LANES
                wide = min(2 * LANES, widths[si] - a0)
                win = refs[si][0, :, a0:a0 + wide] if isinstance(sources[si], tuple) else refs[si][:, a0:a0 + wide]
                r = lax.broadcasted_iota(jnp.int32, (wide, LANES), 0)
                c = lax.broadcasted_iota(jnp.int32, (wide, LANES), 1)
                pick = ((r - (c0 - a0) == c - o0) & (c >= o0) & (c < o0 + n)).astype(bf16)
                part = jnp.dot(win.astype(bf16), pick, preferred_element_type=f32)
                acc = part if acc is None else acc + part
            tile = jnp.zeros((row_block, LANES), f32) if acc is None else acc
            o_ref[:, t * LANES:(t + 1) * LANES] = tile.astype(o_ref.dtype)

    def spec(s):
        if isinstance(s, tuple):
            j = s[1]
            return pl.BlockSpec((1, row_block, s[0].shape[-1]), lambda i: (j, i, 0))
        return pl.BlockSpec((row_block, s.shape[-1]), lambda i: (i, 0))

    return pl.pallas_call(
        body, grid=(rows // row_block,), in_specs=[spec(s) for s in sources],
        out_specs=pl.BlockSpec((row_block, out_cols), lambda i: (i, 0)), out_shape=SDS((rows, out_cols), bf16),
        name=name, compiler_params=_params(("parallel",)),
    )(*arrays)


def _ref_piece(r):
    for name in _REF_ORDER:
        lo, width = _REF_COLS[name]
        if lo <= r < lo + width:
            return name, r - lo
    raise ValueError(r)


def _shard_pad_cols(pieces):
    names = [n for n in _REF_ORDER if n != "vb"]
    sources = [pieces[n] for n in names] + list(pieces["vb"])

    def src_of(c):
        j, i = divmod(c, W_IN_PAD)
        if i >= W_IN_SH:
            return None
        name, col = _ref_piece(j * W_IN_SH + i)
        if name == "vb":
            return len(names) + col // DIL_OUT_W, col % DIL_OUT_W
        return names.index(name), col

    return _place_cols(sources, src_of, N_DEV * W_IN_PAD, "place_dproj")


_SLABS = {"ga": C_GA, "gb": C_GB, "qb": C_QB, "kb": C_KB, "vb": C_VB, "qa": C_QA, "ka": C_KA, "va": C_VA, "f": C_F}


def _slab_w_in(stack):
    def src_of(c):
        for name, start in _SLABS.items():
            lo, width = _REF_COLS[name]
            if start <= c < start + width:
                return divmod(lo + c - start, W_IN_SH)
        return None

    return _place_cols([(stack, j) for j in range(N_DEV)], src_of, PROJ_W, "place_w_in")


def kernel(x, c, w_ada, b_ada, g_mix, w_in, b_fgate, w_br_a, w_br_b, w_out, g_ffn, w_ffn_gate, w_ffn_up, w_ffn_down, g_final, loss_target, m_w_ada, m_b_ada, m_g_mix, m_w_in, m_b_fgate, m_w_br_a, m_w_br_b, m_w_out, m_g_ffn, m_w_ffn_gate, m_w_ffn_up, m_w_ffn_down, m_g_final, v_w_ada, v_b_ada, v_g_mix, v_w_in, v_b_fgate, v_w_br_a, v_w_br_b, v_w_out, v_g_ffn, v_w_ffn_gate, v_w_ffn_up, v_w_ffn_down, v_g_final):
    px, py, pc = _position()
    dev = 4 * px + 2 * py + pc
    x2d, tgt = x[0], loss_target[0]

    c_all = _all_gather(c, "gather_c").reshape(N_DEV, D)
    ada_cols = w_ada.shape[2]
    b_shard = lax.dynamic_slice(b_ada, (0, dev * ada_cols), (1, ada_cols))
    mod_shard = _ada_fwd(c_all, w_ada[0], b_shard)
    mod_all = _all_gather(mod_shard, "gather_mod")
    modv = lax.dynamic_index_in_dim(mod_all, dev, axis=1, keepdims=False).reshape(6, D)

    gate_up = jnp.concatenate([_pad_to(w_ffn_gate[0], 1, FF_PAD), _pad_to(w_ffn_up[0], 1, FF_PAD)], axis=1)
    w_in_s, = _all_gather_many([_pad_to(w_in[0], 1, W_IN_PAD).astype(bf16)], "gather_w_in")
    later = [w_br_a[0], w_br_b[0], w_out[0], gate_up, _pad_to(w_ffn_down[0], 0, FF_PAD)]
    later_state, later_token = _gather_start([t.astype(bf16) for t in later], w_in_s, "gather_rest_start")
    w_in_p = _slab_w_in(w_in_s)

    h1 = _pre1(x2d, modv, g_mix)
    proj = _matmul(h1, w_in_p, name="mm_proj", tm=SEQ, tn=896, tk=D, after=later_token)
    b_pad = jnp.pad(b_fgate, ((0, 0), (0, LANES - N_FOX_HEADS)))
    q_aug, k_aug, va = _fox_prep(proj, _fox_gate_fwd(proj, b_pad))
    ya_h, lse_a = _fox_fwd(q_aug, k_aug, va)

    tables = _rope_tables()
    qb_r, kb_r, vb = _rope_fwd(proj, tables)
    by_group = [_dil_fwd(qb_r, kb_r, vb, grp) for grp in range(N_GROUPS)]
    yb_h, lse_b = _dil_combine([o for o, _ in by_group], [l for _, l in by_group])

    mine, arrived = _gather_wait(later_state, yb_h, "gather_rest_wait")
    w_a_s, w_b_s, w_o_s, w_gu_s, w_d_s = [
        lax.dynamic_update_slice(stack, block[None], (dev, 0, 0))
        for stack, block in zip(_gather_finish(arrived, "gather_rest_finish"), mine, strict=True)]
    w_o = w_o_s.reshape(D, D)
    w_d = w_d_s.reshape(FF_HID, D)
    ya = _matmul_stack(ya_h, w_a_s, name="mm_br_a")
    yb = _matmul_stack(yb_h, w_b_s, name="mm_br_b")

    merged = _merge_fwd(ya, yb, proj)
    mix = _matmul(merged, w_o, name="mm_out", tm=SEQ, tn=512, tk=D)
    x1, h2 = _post1(x2d, mix, modv, g_ffn)
    act, au = _ffn_in(h2, w_gu_s)
    ff = _matmul(act, w_d, name="mm_ffn_down", tm=SEQ // 2, tn=512, tk=FF_HID)

    dx2, dff, dg_final, dga_f, loss_lanes = _final(x1, ff, tgt, modv, g_final.reshape(1, D))
    dau = _ffn_bwd_in(dff, w_d_s, au)

    core = pc.astype(jnp.int32).reshape(1)
    chip = (2 * px + py).astype(jnp.int32).reshape(1)

    def pair_done(state, after, tags, name):
        mine, theirs = _exchange_wait("pair", state, after, "pair_wait_" + name)
        sums = [_pair_add(g, r, core, "pair_add_" + t) for g, r, t in zip(mine, theirs, tags)]
        return _exchange_start("chip", sums, "chip_start_" + name)

    def from_chips(state, after, tags, name):
        sums, got = _exchange_wait("chip", state, after, "chip_wait_" + name)
        return [_chip_add(p, r, chip, "chip_add_" + t) for p, r, t in zip(sums, got, tags)]

    g_gu = _matmul(h2, dau, ta=True, by_shard=True, out_dtype=bf16, name="mm_g_ffn_in", tm=D, tn=2 * FF_PAD, tk=SEQ)
    g_d = _matmul(act, dff, ta=True, out_dtype=bf16, name="mm_g_down", tm=FF_HID // 2, tn=512, tk=SEQ)
    ffn_tags = ["gu", "down"]
    ffn_pair, ffn_pair_token = _exchange_start("pair", [g_gu, g_d.reshape(N_DEV, FF_PAD, D)], "pair_start_ffn")

    dh2 = _matmul_nt_shards(dau, w_gu_s, ffn_pair_token, name="mm_d_h2")
    ffn_state, ffn_token = pair_done(ffn_pair, dh2, ffn_tags, "ffn")
    dx1, dmix, dsh_f, dsc_f, dg_ffn, dga_m = _mid_bwd(dh2, x1, dx2, mix, modv, g_ffn)
    dmerged = _matmul(dmix, w_o, tb=True, name="mm_d_merged", tm=SEQ, tn=512, tk=D, after=ffn_token)
    dya, dyb, dga, dgb = _merge_bwd(dmerged, ya, yb, proj)
    dya_h = _matmul_stack(dya, w_a_s, tb=True, name="mm_d_ya")
    dyb_h = _matmul_stack(dyb, w_b_s, tb=True, name="mm_d_yb")

    g_o = _matmul(merged, dmix, ta=True, out_dtype=bf16, name="mm_g_out", tm=D, tn=512, tk=SEQ)
    g_a = _matmul_stack(ya_h, dya, ta=True, out_dtype=bf16, name="mm_g_br_a")
    g_b = _matmul_stack(yb_h, dyb, ta=True, out_dtype=bf16, name="mm_g_br_b")
    rows_a, rows_b = FOX_W * W_BR_SH // D, DIL_OUT_W * W_BR_SH // D
    g_small = jnp.concatenate([g_a.reshape(N_DEV, rows_a, D), g_b.reshape(N_DEV, rows_b, D),
                               g_o.reshape(N_DEV, W_BR_SH, D)], axis=1)
    small_pair, small_pair_token = _exchange_start("pair", [g_small], "pair_start_small")

    dqa, dka, dva, dF = _fox_bwd(q_aug, k_aug, va, dya_h, ya_h, lse_a, small_pair_token)
    dF_row = jnp.pad(dF[:, :2, :].reshape(N_FOX_HEADS, SEQ), ((0, LANES - N_FOX_HEADS), (0, 0)))
    df, db_fgate = _fox_gate_bwd(dF_row, proj, b_pad)
    small_state, _ = pair_done(small_pair, df, ["small"], "small")

    delta_b = _dil_delta(dyb_h, yb_h)
    dil_grads = [_dil_bwd(qb_r, kb_r, vb, dyb_h, lse_b, delta_b, grp) for grp in range(N_GROUPS)]
    dqb, dkb = _rope_bwd([t[0] for t in dil_grads], [t[1] for t in dil_grads], tables)

    dproj = _shard_pad_cols({"qa": dqa, "ka": dka, "va": dva, "f": df, "qb": dqb, "kb": dkb,
                             "vb": [t[2] for t in dil_grads], "ga": dga, "gb": dgb})
    g_in = _matmul(h1, dproj, ta=True, by_shard=True, out_dtype=bf16, name="mm_g_in", tm=D, tn=W_IN_PAD, tk=SEQ)
    mix_tags = ["in"]
    mix_pair, mix_pair_token = _exchange_start("pair", [g_in], "pair_start_mixer")

    dh1 = _matmul_nt_shards(dproj, w_in_s, mix_pair_token, name="mm_d_h1")
    grad_x, dsh_m, dsc_m, dg_mix = _first_bwd(dh1, x2d, dx1, modv, g_mix)

    pad_lane = lambda t: jnp.pad(t, ((0, 0), (0, D - t.shape[1])))
    small = jnp.concatenate([dsh_m, dsc_m, dga_m, dsh_f, dsc_f, dga_f, dg_mix, dg_ffn, dg_final,
                             pad_lane(db_fgate), loss_lanes, jnp.zeros((SMALL_ROWS - 11, D), f32)], axis=0)
    small_all = _all_gather(small, "gather_small")
    mix_state, mix_token = pair_done(mix_pair, small_all, mix_tags, "mixer")

    small_sum, loss_row = _small_reduce(small_all, mix_token)
    dmod_all = small_all[:, :6, :].reshape(N_DEV, 6 * D)
    g_w_ada = _ada_bwd(c_all, lax.dynamic_slice(dmod_all, (0, dev * ada_cols), (N_DEV, ada_cols)))
    s_gu, s_d = from_chips(ffn_state, small_sum, ffn_tags, "ffn")
    s_small, = from_chips(small_state, small_sum, ["small"], "small")

    loss = loss_row[0, 0]
    g = {
        "w_ada": g_w_ada[None], "b_ada": small_sum[0:6].reshape(1, 6 * D), "g_mix": small_sum[6:7],
        "b_fgate": small_sum[9:10, :N_FOX_HEADS], "g_ffn": small_sum[7:8], "w_ffn_gate": s_gu[None, :, :W_FF_SH],
        "w_ffn_up": s_gu[None, :, FF_PAD:FF_PAD + W_FF_SH], "w_ffn_down": s_d[None, :W_FF_SH],
        "g_final": small_sum[8], "w_br_a": s_small[:rows_a].reshape(1, FOX_W, W_BR_SH),
        "w_br_b": s_small[rows_a:rows_a + rows_b].reshape(1, DIL_OUT_W, W_BR_SH), "w_out": s_small[None, rows_a + rows_b:],
    }
    w = {"w_ada": w_ada, "b_ada": b_ada, "g_mix": g_mix, "w_in": w_in, "b_fgate": b_fgate, "w_br_a": w_br_a,
         "w_br_b": w_br_b, "w_out": w_out, "g_ffn": g_ffn, "w_ffn_gate": w_ffn_gate, "w_ffn_up": w_ffn_up,
         "w_ffn_down": w_ffn_down, "g_final": g_final}
    m = {"w_ada": m_w_ada, "b_ada": m_b_ada, "g_mix": m_g_mix, "w_in": m_w_in, "b_fgate": m_b_fgate,
         "w_br_a": m_w_br_a, "w_br_b": m_w_br_b, "w_out": m_w_out, "g_ffn": m_g_ffn, "w_ffn_gate": m_w_ffn_gate,
         "w_ffn_up": m_w_ffn_up, "w_ffn_down": m_w_ffn_down, "g_final": m_g_final}
    v = {"w_ada": v_w_ada, "b_ada": v_b_ada, "g_mix": v_g_mix, "w_in": v_w_in, "b_fgate": v_b_fgate,
         "w_br_a": v_w_br_a, "w_br_b": v_w_br_b, "w_out": v_w_out, "g_ffn": v_g_ffn, "w_ffn_gate": v_w_ffn_gate,
         "w_ffn_up": v_w_ffn_up, "w_ffn_down": v_w_ffn_down, "g_final": v_g_final}
    names = list(w)
    delta, new_m, new_v = {}, {}, {}

    transposed = ("w_in", "w_ffn_gate", "w_ffn_up")

    def update(n):
        shape = w[n].shape
        if n in transposed:
            g_t = g[n][0].T
            dl, mn, vn = _adamw(w[n][0].T, g_t, m[n][0].T, v[n][0].T, "adamw_" + n)
            g[n], delta[n], new_m[n], new_v[n] = g_t.T[None], dl.T[None], mn.T[None], vn.T[None]
            return
        two_d = (lambda t: t.reshape(shape[-2:])) if len(shape) == 3 else (lambda t: t)
        dl, mn, vn = _adamw(two_d(w[n]), two_d(g[n]), two_d(m[n]), two_d(v[n]), "adamw_" + n)
        delta[n], new_m[n], new_v[n] = dl.reshape(shape), mn.reshape(shape), vn.reshape(shape)

    for n in list(g):
        update(n)
    done = sum(delta[n].reshape(-1)[:N_FOX_HEADS] for n in g)
    s_in, = from_chips(mix_state, done, mix_tags, "mixer")
    g["w_in"] = s_in[None, :, :W_IN_SH]
    update("w_in")

    return (loss, grad_x[None], *[g[n] for n in names], *[delta[n] for n in names],
            *[new_m[n] for n in names], *[new_v[n] for n in names])
```

```python
import functools

import jax
import jax.numpy as jnp
from jax import lax
from jax.experimental import pallas as pl
from jax.experimental.pallas import tpu as pltpu

f32 = jnp.float32
bf16 = jnp.bfloat16
SDS = jax.ShapeDtypeStruct
MESH = pl.DeviceIdType.MESH

N_DEV = 8
D = 1024
SEQ = 2048
HEAD_DIM = 64
N_FOX_HEADS = 8
FOX_W = 512
DIL_W = 768
DIL_OUT_W = 256
ROT_DIM = 16
ROPE_THETA = 500000.0
D_FF = 2816
IN_COLS = 5896
EPS = 1e-6
NEG = -1e30
ATT_SCALE = HEAD_DIM ** -0.5

ADAM_LR = 0.001
ADAM_B1 = 0.9
ADAM_B2 = 0.999
ADAM_EPS = 1e-08
ADAM_WD = 0.01
ADAM_STEP = 10

C_GA, C_GB, C_QB, C_KB, C_VB, C_QA, C_KA, C_VA, C_F = 0, 1024, 2304, 3072, 3840, 4608, 5120, 5632, 6144
PROJ_W = 6272
LANES = 128
VMEM_LIMIT = 52 * 1024 * 1024

W_IN_SH, W_IN_PAD = IN_COLS // N_DEV, 768
W_BR_SH = D // N_DEV
W_FF_SH, FF_PAD = D_FF // N_DEV, 384
FF_HID = N_DEV * FF_PAD
SMALL_ROWS = 16


def _params(sem=None):
    if sem is None:
        return pltpu.CompilerParams(vmem_limit_bytes=VMEM_LIMIT)
    return pltpu.CompilerParams(dimension_semantics=sem, vmem_limit_bytes=VMEM_LIMIT)


def _rowwise(fn, name, tiled, vecs, outs, reds=(), tile=256):
    nt, nv, no = len(tiled), len(vecs), len(outs)
    rows = tiled[0][0].shape[0]
    assert rows % tile == 0

    def body(*refs):
        tin = [r[...] for r in refs[:nt]]
        vin = [r[...] for r in refs[nt:nt + nv]]
        orefs = refs[nt + nv:nt + nv + no]
        rrefs = refs[nt + nv + no:]
        touts, routs = fn(tin, vin)
        for r, t in zip(orefs, touts, strict=True):
            r[...] = t.astype(r.dtype)
        if rrefs:
            @pl.when(pl.program_id(0) == 0)
            def _():
                for r in rrefs:
                    r[...] = jnp.zeros_like(r)
            for r, t in zip(rrefs, routs, strict=True):
                r[...] += t

    def col_map(cb):
        return lambda i: (i, cb)

    def whole_map(nd):
        return lambda i: (0,) * nd

    in_specs = [pl.BlockSpec((tile, w), col_map(cb)) for (_, w, cb) in tiled]
    in_specs += [pl.BlockSpec(v.shape, whole_map(v.ndim)) for v in vecs]
    out_specs = [pl.BlockSpec((tile, w), lambda i: (i, 0)) for (w, _) in outs]
    out_specs += [pl.BlockSpec((1, w), lambda i: (0, 0)) for w in reds]
    out_shape = [SDS((rows, w), dt) for (w, dt) in outs] + [SDS((1, w), f32) for w in reds]
    res = pl.pallas_call(
        body, grid=(rows // tile,), in_specs=in_specs, out_specs=out_specs, out_shape=out_shape, name=name,
        compiler_params=_params(("arbitrary",)),
    )(*[t[0] for t in tiled], *vecs)
    return res


def _matmul(a, b, *, ta=False, tb=False, out_dtype=f32, name, tm, tn, tk, by_shard=False, after=None):
    m, k = (a.shape[1], a.shape[0]) if ta else a.shape
    if by_shard and not ta:
        n, kb = (b.shape[1], N_DEV * b.shape[2]) if tb else (N_DEV * b.shape[2], b.shape[1])
        assert (tk if tb else tn) == b.shape[2]
    else:
        n, kb = (b.shape[0], b.shape[1]) if tb else (b.shape[1], b.shape[0])
    assert kb == k and m % tm == 0 and n % tn == 0 and k % tk == 0
    nk = k // tk
    dims = (((0 if ta else 1,), (1 if tb else 0,)), ((), ()))
    b_stacked = by_shard and not ta
    o_stacked = by_shard and ta

    def body(a_ref, b_ref, *rest):
        o_ref, *acc = rest[1:] if after is not None else rest
        bv = b_ref[0] if b_stacked else b_ref[...]
        p = lax.dot_general(a_ref[...].astype(bf16), bv.astype(bf16), dims, preferred_element_type=f32)

        def put(val):
            if o_stacked:
                o_ref[0] = val.astype(o_ref.dtype)
            else:
                o_ref[...] = val.astype(o_ref.dtype)

        if nk == 1:
            put(p)
        else:
            acc_ref, = acc
            kk = pl.program_id(2)

            @pl.when(kk == 0)
            def _():
                acc_ref[...] = p

            @pl.when(kk > 0)
            def _():
                acc_ref[...] += p

            @pl.when(kk == nk - 1)
            def _():
                put(acc_ref[...])

    a_spec = pl.BlockSpec((tk, tm), lambda i, j, kk: (kk, i)) if ta else pl.BlockSpec((tm, tk), lambda i, j, kk: (i, kk))
    if b_stacked and tb:
        b_spec = pl.BlockSpec((1, tn, tk), lambda i, j, kk: (kk, j, 0))
    elif b_stacked:
        b_spec = pl.BlockSpec((1, tk, tn), lambda i, j, kk: (j, kk, 0))
    elif tb:
        b_spec = pl.BlockSpec((tn, tk), lambda i, j, kk: (j, kk))
    else:
        b_spec = pl.BlockSpec((tk, tn), lambda i, j, kk: (kk, j))
    if o_stacked:
        assert tn == n // N_DEV
        out_spec = pl.BlockSpec((1, tm, tn), lambda i, j, kk: (j, i, 0))
        out_shape = SDS((N_DEV, m, tn), out_dtype)
    else:
        out_spec = pl.BlockSpec((tm, tn), lambda i, j, kk: (i, j))
        out_shape = SDS((m, n), out_dtype)
    extra_specs, extra = ([pl.BlockSpec(memory_space=pl.ANY)], [after]) if after is not None else ([], [])
    return pl.pallas_call(
        body, grid=(m // tm, n // tn, nk), in_specs=[a_spec, b_spec] + extra_specs, out_specs=out_spec,
        out_shape=out_shape, name=name, scratch_shapes=[pltpu.VMEM((tm, tn), f32)] if nk > 1 else [],
        compiler_params=_params(("parallel", "parallel", "arbitrary")),
    )(a, b, *extra)


def _matmul_stack(a, b, *, ta=False, tb=False, out_dtype=f32, name):
    def lanes(ref):
        return jnp.concatenate([ref[j] for j in range(N_DEV)], axis=1).astype(bf16)

    if ta:
        w = b.shape[1] // N_DEV

        def body(a_ref, b_ref, o_ref):
            p = _tn(a_ref[...].astype(bf16), b_ref[...].astype(bf16))
            for j in range(N_DEV):
                o_ref[j] = p[:, j * w:(j + 1) * w].astype(o_ref.dtype)

        return pl.pallas_call(body, out_shape=SDS((N_DEV, a.shape[1], w), out_dtype), name=name,
                              compiler_params=_params())(a, b)

    m, half = a.shape[0], a.shape[0] // 2
    n = b.shape[1] if tb else N_DEV * b.shape[2]

    def body(a_ref, b_ref, o_ref):
        av = a_ref[...].astype(bf16)
        o_ref[...] = (_nt(av, lanes(b_ref)) if tb else jnp.dot(av, lanes(b_ref), preferred_element_type=f32)
                      ).astype(o_ref.dtype)

    return pl.pallas_call(
        body, grid=(2,), in_specs=[pl.BlockSpec((half, a.shape[1]), lambda i: (i, 0)),
                                   pl.BlockSpec(b.shape, lambda i: (0, 0, 0))],
        out_specs=pl.BlockSpec((half, n), lambda i: (i, 0)), out_shape=SDS((m, n), out_dtype), name=name,
        compiler_params=_params(("parallel",)),
    )(a, b)


def _matmul_nt_shards(a, b, after, *, name, tm=512, tn=1024):
    m, n, w = a.shape[0], b.shape[1], b.shape[2]
    assert a.shape[1] == N_DEV * w and m % tm == 0 and n % tn == 0

    def body(a_ref, b_ref, after_ref, o_ref):
        acc = _nt(a_ref[:, 0:w], b_ref[0])
        for j in range(1, N_DEV):
            acc = acc + _nt(a_ref[:, j * w:(j + 1) * w], b_ref[j])
        o_ref[...] = acc

    return pl.pallas_call(
        body, grid=(n // tn, m // tm),
        in_specs=[pl.BlockSpec((tm, N_DEV * w), lambda j, i: (i, 0)), pl.BlockSpec((N_DEV, tn, w), lambda j, i: (0, j, 0)),
                  pl.BlockSpec(memory_space=pl.ANY)],
        out_specs=pl.BlockSpec((tm, tn), lambda j, i: (i, j)), out_shape=SDS((m, n), f32), name=name,
        compiler_params=_params(("parallel", "parallel")),
    )(a, b, after)


def _rms(x):
    r = lax.rsqrt(jnp.mean(x * x, axis=-1, keepdims=True) + EPS)
    return r, x * r


def _rms_bwd(r, xn, dxn):
    return r * (dxn - xn * jnp.mean(dxn * xn, axis=-1, keepdims=True))


def _colsum(t):
    return jnp.sum(t, axis=0, keepdims=True)


def _sigmoid(x):
    return 1.0 / (1.0 + jnp.exp(-x))


def _modulated_norm(x, g, shift, scale):
    _, xn = _rms(x)
    return (xn * g) * (1.0 + scale) + shift


def _pre1(x, modv, g_mix):
    def fn(t, v):
        (xt,), (mv, g) = t, v
        return [_modulated_norm(xt, g, mv[0:1], mv[1:2])], []
    return _rowwise(fn, "pre1", [(x, D, 0)], [modv, g_mix], [(D, bf16)])[0]


def _post1(x, mix, modv, g_ffn):
    def fn(t, v):
        (xt, mt), (mv, g) = t, v
        x1 = xt + mv[2:3] * mt
        return [x1, _modulated_norm(x1, g, mv[3:4], mv[4:5])], []
    return _rowwise(fn, "post1", [(x, D, 0), (mix, D, 0)], [modv, g_ffn], [(D, f32), (D, bf16)])


def _ffn_in(h, w_stack):
    def body(h_ref, w_ref, act_ref, au_ref):
        p = jnp.dot(h_ref[...], w_ref[0], preferred_element_type=f32)
        a, u = p[:, :FF_PAD], p[:, FF_PAD:]
        act_ref[...] = (a * _sigmoid(a) * u).astype(act_ref.dtype)
        au_ref[...] = p.astype(au_ref.dtype)

    return pl.pallas_call(
        body, grid=(N_DEV,),
        in_specs=[pl.BlockSpec((SEQ, D), lambda j: (0, 0)), pl.BlockSpec((1, D, 2 * FF_PAD), lambda j: (j, 0, 0))],
        out_specs=[pl.BlockSpec((SEQ, FF_PAD), lambda j: (0, j)), pl.BlockSpec((SEQ, 2 * FF_PAD), lambda j: (0, j))],
        out_shape=(SDS((SEQ, FF_HID), bf16), SDS((SEQ, 2 * FF_HID), bf16)), name="ffn_in",
        compiler_params=_params(("parallel",)),
    )(h, w_stack)


def _ffn_bwd_in(dff, w_down_stack, au):
    def body(d_ref, w_ref, au_ref, o_ref):
        dact = _nt(d_ref[...], w_ref[0])
        p = au_ref[...].astype(f32)
        a, u = p[:, :FF_PAD], p[:, FF_PAD:]
        sg = _sigmoid(a)
        o_ref[...] = jnp.concatenate([dact * u * (sg * (1.0 + a * (1.0 - sg))), dact * (a * sg)],
                                     axis=1).astype(o_ref.dtype)

    return pl.pallas_call(
        body, grid=(N_DEV,),
        in_specs=[pl.BlockSpec((SEQ, D), lambda j: (0, 0)), pl.BlockSpec((1, FF_PAD, D), lambda j: (j, 0, 0)),
                  pl.BlockSpec((SEQ, 2 * FF_PAD), lambda j: (0, j))],
        out_specs=pl.BlockSpec((SEQ, 2 * FF_PAD), lambda j: (0, j)),
        out_shape=SDS((SEQ, 2 * FF_HID), bf16), name="ffn_bwd_in", compiler_params=_params(("parallel",)),
    )(dff, w_down_stack, au)


def _final(x1, ff, target, modv, g_final):
    def fn(t, v):
        (x1t, fft, tgt), (mv, g) = t, v
        x2 = x1t + mv[5:6] * fft
        r, xn = _rms(x2)
        err = xn * g - tgt
        dy = err * (1.0 / D)
        dx2 = _rms_bwd(r, xn, dy * g)
        return [dx2, dx2 * mv[5:6]], [_colsum(dy * xn), _colsum(dx2 * fft), _colsum(err * err) * (0.5 / D)]
    return _rowwise(fn, "final", [(x1, D, 0), (ff, D, 0), (target, D, 0)], [modv, g_final],
                    [(D, f32), (D, bf16)], [D, D, D])


def _mid_bwd(dh2, x1, dx2, mix, modv, g_ffn):
    def fn(t, v):
        (dh, x1t, dx2t, mt), (mv, g) = t, v
        r, xn = _rms(x1t)
        dn = dh * (1.0 + mv[4:5])
        dx1 = dx2t + _rms_bwd(r, xn, dn * g)
        return [dx1, dx1 * mv[2:3]], [_colsum(dh), _colsum(dh * (xn * g)), _colsum(dn * xn), _colsum(dx1 * mt)]
    return _rowwise(fn, "mid_bwd", [(dh2, D, 0), (x1, D, 0), (dx2, D, 0), (mix, D, 0)], [modv, g_ffn],
                    [(D, f32), (D, bf16)], [D, D, D, D])


def _first_bwd(dh1, x, dx1, modv, g_mix):
    def fn(t, v):
        (dh, xt, dx1t), (mv, g) = t, v
        r, xn = _rms(xt)
        dn = dh * (1.0 + mv[1:2])
        return [dx1t + _rms_bwd(r, xn, dn * g)], [_colsum(dh), _colsum(dh * (xn * g)), _colsum(dn * xn)]
    return _rowwise(fn, "first_bwd", [(dh1, D, 0), (x, D, 0), (dx1, D, 0)], [modv, g_mix], [(D, f32)], [D, D, D])


def _merge_fwd(ya, yb, proj):
    def fn(t, v):
        ya_t, yb_t, ga, gb = t
        return [_sigmoid(ga) * ya_t + _sigmoid(gb) * yb_t], []
    return _rowwise(fn, "merge_fwd", [(ya, D, 0), (yb, D, 0), (proj, D, C_GA // D), (proj, D, C_GB // D)], [],
                    [(D, bf16)])[0]


def _merge_bwd(dmerged, ya, yb, proj):
    def fn(t, v):
        dm, ya_t, yb_t, ga, gb = t
        sa, sb = _sigmoid(ga), _sigmoid(gb)
        return [dm * sa, dm * sb, dm * ya_t * (sa * (1.0 - sa)), dm * yb_t * (sb * (1.0 - sb))], []
    return _rowwise(fn, "merge_bwd",
                    [(dmerged, D, 0), (ya, D, 0), (yb, D, 0), (proj, D, C_GA // D), (proj, D, C_GB // D)], [],
                    [(D, bf16), (D, bf16), (D, bf16), (D, bf16)])


def _rope_tables():
    half = ROT_DIM // 2
    pos = jnp.arange(SEQ, dtype=f32)
    inv_freq = ROPE_THETA ** (-jnp.arange(0, ROT_DIM, 2, dtype=f32) / ROT_DIM)
    ang = pos[:, None] * inv_freq[None, :]
    cos, sin = jnp.cos(ang), jnp.sin(ang)
    pad = jnp.zeros((SEQ, HEAD_DIM - ROT_DIM), f32)
    zero = jnp.zeros((SEQ, half), f32)
    c_head = jnp.concatenate([cos, cos, pad + 1.0], axis=1)
    lo_head = jnp.concatenate([-sin, zero, pad], axis=1)
    hi_head = jnp.concatenate([zero, sin, pad], axis=1)
    return tuple(jnp.concatenate([t, t], axis=1) for t in (c_head, lo_head, hi_head))


def _over_heads(tables):
    return [jnp.tile(t, (1, DIL_W // LANES)) for t in tables]


def _rope_fwd(proj, tables):
    half = ROT_DIM // 2

    def fn(t, v):
        q, k, vv = t[:3]
        c, lo, hi = _over_heads(t[3:])
        rot = lambda z: z * c + pltpu.roll(z, DIL_W - half, 1) * lo + pltpu.roll(z, half, 1) * hi
        return [rot(q) * ATT_SCALE, rot(k), vv], []
    return _rowwise(fn, "rope_fwd", [(proj, DIL_W, C_QB // DIL_W), (proj, DIL_W, C_KB // DIL_W),
                                     (proj, DIL_W, C_VB // DIL_W)] + [(tb, LANES, 0) for tb in tables], [],
                    [(DIL_W, f32)] * 3)


def _rope_bwd(dqs, dks, tables):
    half = ROT_DIM // 2

    def fn(t, v):
        dq_t, dk_t = jnp.concatenate(t[:N_GROUPS], axis=1), jnp.concatenate(t[N_GROUPS:2 * N_GROUPS], axis=1)
        c, lo, hi = _over_heads(t[2 * N_GROUPS:])
        rot_t = lambda z: z * c + pltpu.roll(z * lo, half, 1) + pltpu.roll(z * hi, DIL_W - half, 1)
        return [rot_t(dq_t), rot_t(dk_t)], []
    return _rowwise(fn, "rope_bwd", [(a, DIL_OUT_W, 0) for a in (*dqs, *dks)] + [(tb, LANES, 0) for tb in tables],
                    [], [(DIL_W, bf16), (DIL_W, bf16)])


def _head_bcast_sum(d):
    lane = lax.broadcasted_iota(jnp.int32, d.shape, 1)
    out = jnp.zeros_like(d)
    for h in range(d.shape[1] // HEAD_DIM):
        sel = (lane >= h * HEAD_DIM) & (lane < (h + 1) * HEAD_DIM)
        out = jnp.where(sel, jnp.sum(jnp.where(sel, d, 0.0), axis=1, keepdims=True), out)
    return out


def _dil_combine(outs, lses):
    def fn(t, v):
        o0, o1, o2, l0, l1, l2 = t
        m = jnp.maximum(jnp.maximum(l0, l1), l2)
        w0, w1, w2 = jnp.exp(l0 - m), jnp.exp(l1 - m), jnp.exp(l2 - m)
        tot = w0 + w1 + w2
        return [(w0 * o0 + w1 * o1 + w2 * o2) / tot, m + jnp.log(tot)], []
    w = DIL_OUT_W
    return _rowwise(fn, "dil_combine", [(t, w, 0) for t in (*outs, *lses)], [], [(w, f32), (w, f32)])


def _dil_delta(dyb_h, yb_h):
    def fn(t, v):
        return [_head_bcast_sum(t[0] * t[1])], []
    return _rowwise(fn, "dil_delta", [(dyb_h, DIL_OUT_W, 0), (yb_h, DIL_OUT_W, 0)], [], [(DIL_OUT_W, f32)])[0]


def _adamw_math(wt, gt, mt, vt):
    mn = ADAM_B1 * mt + (1.0 - ADAM_B1) * gt
    vn = ADAM_B2 * vt + (1.0 - ADAM_B2) * (gt * gt)
    m_hat = mn / (1.0 - ADAM_B1 ** ADAM_STEP)
    v_hat = vn / (1.0 - ADAM_B2 ** ADAM_STEP)
    return -ADAM_LR * (m_hat / (jnp.sqrt(v_hat) + ADAM_EPS) + ADAM_WD * wt), mn, vn


def _adamw(w, g, m, v, name):
    shape = w.shape
    if w.ndim == 1:
        w, g, m, v = (t.reshape(1, -1) for t in (w, g, m, v))
    rows, cols = w.shape
    if rows % 8 and rows > 8:
        return _adamw_by_cols(w, g, m, v, name)
    tile = 256 if rows % 256 == 0 and rows > 512 else rows

    def fn(t, _):
        return list(_adamw_math(*t)), []
    delta, mn, vn = _rowwise(fn, name, [(w, cols, 0), (g, cols, 0), (m, cols, 0), (v, cols, 0)], [],
                             [(cols, f32)] * 3, tile=tile)
    return delta.reshape(shape), mn.reshape(shape), vn.reshape(shape)


def _adamw_by_cols(w, g, m, v, name, tile=256):
    rows, cols = w.shape

    def body(w_ref, g_ref, m_ref, v_ref, d_ref, mn_ref, vn_ref):
        d_ref[...], mn_ref[...], vn_ref[...] = _adamw_math(w_ref[...], g_ref[...], m_ref[...], v_ref[...])

    spec = pl.BlockSpec((rows, tile), lambda j: (0, j))
    return pl.pallas_call(body, grid=(cols // tile,), in_specs=[spec] * 4, out_specs=[spec] * 3,
                          out_shape=[SDS((rows, cols), f32)] * 3, name=name,
                          compiler_params=_params(("parallel",)))(w, g, m, v)


def _ada_fwd(c_all, w_shard, b_shard):
    def body(c_ref, w_ref, b_ref, o_ref):
        cv = c_ref[...]
        sc = (cv * _sigmoid(cv)).astype(bf16)
        o_ref[...] = jnp.dot(sc, w_ref[...].astype(bf16), preferred_element_type=f32) + b_ref[...]
    return pl.pallas_call(body, out_shape=SDS((N_DEV, w_shard.shape[1]), f32), name="ada_fwd",
                          compiler_params=_params())(c_all, w_shard, b_shard)


def _ada_bwd(c_all, dmod_cols):
    def body(c_ref, d_ref, o_ref):
        cv = c_ref[...]
        sc = cv * _sigmoid(cv)
        o_ref[...] = lax.dot_general(sc, d_ref[...], (((0,), (0,)), ((), ())), precision=lax.Precision.HIGHEST,
                                     preferred_element_type=f32)
    return pl.pallas_call(body, out_shape=SDS((D, dmod_cols.shape[1]), f32), name="ada_bwd",
                          compiler_params=_params())(c_all, dmod_cols)


def _small_reduce(gathered, after):
    def body(g_ref, after_ref, o_ref, loss_ref):
        acc = g_ref[0]
        for d in range(1, N_DEV):
            acc = acc + g_ref[d]
        o_ref[...] = acc
        loss_ref[...] = jnp.zeros((1, LANES), f32) + jnp.sum(acc[10:11, :])
    return pl.pallas_call(body, out_shape=(SDS((SMALL_ROWS, D), f32), SDS((1, LANES), f32)), name="small_reduce",
                          in_specs=[pl.BlockSpec(memory_space=pltpu.VMEM), pl.BlockSpec(memory_space=pl.ANY)],
                          compiler_params=_params())(gathered, after)


FOX_BLK = 512
CUM_BLK = 128


def _fold_lanes(t, op):
    out = t[:, :LANES]
    for j in range(1, t.shape[1] // LANES):
        out = op(out, t[:, j * LANES:(j + 1) * LANES])
    return out


def _fox_gate_fwd(proj, b_pad):
    nblk = SEQ // CUM_BLK

    def body(f_ref, b_ref, col_ref):
        r = lax.broadcasted_iota(jnp.int32, (CUM_BLK, CUM_BLK), 0)
        c = lax.broadcasted_iota(jnp.int32, (CUM_BLK, CUM_BLK), 1)
        tri = (r >= c).astype(f32)
        carry = jnp.zeros((1, LANES), f32)
        for blk in range(nblk):
            z = f_ref[blk * CUM_BLK:(blk + 1) * CUM_BLK, :] + b_ref[...]
            logf = jnp.minimum(z, 0.0) - jnp.log1p(jnp.exp(-jnp.abs(z)))
            cs = jnp.dot(tri, logf, precision=lax.Precision.HIGHEST, preferred_element_type=f32) + carry
            col_ref[blk * CUM_BLK:(blk + 1) * CUM_BLK, :] = cs
            carry = cs[CUM_BLK - 1:CUM_BLK, :]

    return pl.pallas_call(
        body, grid=(1,), in_specs=[pl.BlockSpec((SEQ, LANES), lambda i: (0, C_F // LANES)),
                                   pl.BlockSpec((1, LANES), lambda i: (0, 0))],
        out_specs=pl.BlockSpec((SEQ, LANES), lambda i: (0, 0)),
        out_shape=SDS((SEQ, LANES), f32), name="fox_gate_fwd",
        compiler_params=_params(("arbitrary",)),
    )(proj, b_pad)


def _fox_gate_bwd(dF_row, proj, b_pad):
    nblk = SEQ // CUM_BLK

    def body(d_ref, f_ref, b_ref, df_ref, db_ref, col_ref):
        r = lax.broadcasted_iota(jnp.int32, (CUM_BLK, CUM_BLK), 0)
        c = lax.broadcasted_iota(jnp.int32, (CUM_BLK, CUM_BLK), 1)
        tri = (r <= c).astype(f32)
        lane = lax.broadcasted_iota(jnp.int32, (CUM_BLK, LANES), 1)
        col_ref[...] = d_ref[...].T
        carry = jnp.zeros((1, LANES), f32)
        total = jnp.zeros((1, LANES), f32)
        for blk in reversed(range(nblk)):
            rows = slice(blk * CUM_BLK, (blk + 1) * CUM_BLK)
            cs = jnp.dot(tri, col_ref[rows, :], precision=lax.Precision.HIGHEST, preferred_element_type=f32) + carry
            carry = cs[0:1, :]
            z = f_ref[rows, :] + b_ref[...]
            df = jnp.where(lane < N_FOX_HEADS, cs * _sigmoid(-z), 0.0)
            df_ref[rows, :] = df.astype(df_ref.dtype)
            total = total + _colsum(df)
        db_ref[...] = total

    return pl.pallas_call(
        body, grid=(1,), in_specs=[pl.BlockSpec((LANES, SEQ), lambda i: (0, 0)),
                                   pl.BlockSpec((SEQ, LANES), lambda i: (0, C_F // LANES)),
                                   pl.BlockSpec((1, LANES), lambda i: (0, 0))],
        out_specs=[pl.BlockSpec((SEQ, LANES), lambda i: (0, 0)), pl.BlockSpec((1, LANES), lambda i: (0, 0))],
        out_shape=(SDS((SEQ, LANES), bf16), SDS((1, LANES), f32)), name="fox_gate_bwd",
        scratch_shapes=[pltpu.VMEM((SEQ, LANES), f32)],
        compiler_params=_params(("arbitrary",)),
    )(dF_row, proj, b_pad)


def _nt(a, b):
    return lax.dot_general(a, b, (((1,), (1,)), ((), ())), preferred_element_type=f32)


def _tn(a, b):
    return lax.dot_general(a, b, (((0,), (0,)), ((), ())), preferred_element_type=f32)


def _fox_prep(proj, f_col):
    def fn(t, v):
        q, k, vv, fc = t
        lane = lax.broadcasted_iota(jnp.int32, (q.shape[0], LANES), 1)
        qs, ks = [], []
        for h in range(N_FOX_HEADS):
            pair, pos = divmod(h, 2)
            own = (lane >= pos * HEAD_DIM) & (lane < (pos + 1) * HEAD_DIM)
            base = (1 - pos) * HEAD_DIM
            f = fc[:, h:h + 1]
            hi = f.astype(bf16).astype(f32)
            mid = (f - hi).astype(bf16).astype(f32)
            lo = (f - hi) - mid
            one = jnp.ones_like(f)
            qa = jnp.where(own, q[:, pair * LANES:(pair + 1) * LANES] * ATT_SCALE, 0.0)
            ka = k[:, pair * LANES:(pair + 1) * LANES]
            for idx, (qv, kv) in enumerate([(hi, one), (mid, one), (lo, one), (one, -hi), (one, -mid), (one, -lo)]):
                sel = lane == base + idx
                qa = jnp.where(sel, qv, qa)
                ka = jnp.where(sel, kv, ka)
            qs.append(qa)
            ks.append(ka)
        return [jnp.concatenate(qs, axis=1), jnp.concatenate(ks, axis=1), vv], []
    w = N_FOX_HEADS * LANES
    return _rowwise(fn, "fox_prep", [(proj, FOX_W, C_QA // FOX_W), (proj, FOX_W, C_KA // FOX_W),
                                     (proj, FOX_W, C_VA // FOX_W), (f_col, LANES, 0)], [],
                    [(w, bf16), (w, bf16), (FOX_W, bf16)])


def _fox_fwd(q_aug, k_aug, v):
    blk = FOX_BLK
    npair = FOX_W // LANES

    def body(q_ref, k_ref, v_ref, o_ref, lse_ref, s_scr):
        i = pl.program_id(1)
        tri = lax.broadcasted_iota(jnp.int32, (blk, blk), 0) >= lax.broadcasted_iota(jnp.int32, (blk, blk), 1)
        qh = [q_ref[:, h * LANES:(h + 1) * LANES] for h in range(2)]

        def logits(c, masked):
            off = pl.multiple_of(c * blk, blk)
            tops = []
            for h in range(2):
                s = _nt(qh[h], k_ref[pl.ds(off, blk), h * LANES:(h + 1) * LANES])
                if masked:
                    s = jnp.where(tri, s, NEG)
                s_scr[h, :, pl.ds(off, blk)] = s
                tops.append(_fold_lanes(s, jnp.maximum))
            return tops

        def pass_a(c, m):
            return tuple(jnp.maximum(a, b) for a, b in zip(m, logits(c, False)))

        m = lax.fori_loop(0, i, pass_a, tuple(jnp.full((blk, LANES), NEG, f32) for _ in range(2)))
        mx = [jnp.max(jnp.maximum(a, b), axis=1, keepdims=True) for a, b in zip(m, logits(i, True))]

        def pass_b(c, carry):
            off = pl.multiple_of(c * blk, blk)
            vv = v_ref[pl.ds(off, blk), :]
            new = []
            for h in range(2):
                l, acc = carry[h]
                p = jnp.exp(s_scr[h, :, pl.ds(off, blk)] - mx[h])
                hi = p.astype(bf16)
                lo = (p - hi.astype(f32)).astype(bf16)
                new.append((l + _fold_lanes(p, jnp.add),
                            acc + jnp.dot(hi, vv, preferred_element_type=f32)
                            + jnp.dot(lo, vv, preferred_element_type=f32)))
            return tuple(new)

        zero = jnp.zeros((blk, LANES), f32)
        (l_a, acc_a), (l_b, acc_b) = lax.fori_loop(0, i + 1, pass_b, ((zero, zero), (zero, zero)))
        l_a = jnp.sum(l_a, axis=1, keepdims=True)
        l_b = jnp.sum(l_b, axis=1, keepdims=True)
        first = lax.broadcasted_iota(jnp.int32, (blk, LANES), 1) < HEAD_DIM
        o_ref[...] = jnp.where(first, acc_a / l_a, acc_b / l_b)
        lse_ref[0] = jnp.where(first, mx[0] + jnp.log(l_a), mx[1] + jnp.log(l_b))

    return pl.pallas_call(
        body, grid=(npair, SEQ // blk),
        in_specs=[pl.BlockSpec((blk, 2 * LANES), lambda p, i: (i, p)),
                  pl.BlockSpec((SEQ, 2 * LANES), lambda p, i: (0, p)),
                  pl.BlockSpec((SEQ, LANES), lambda p, i: (0, p))],
        out_specs=[pl.BlockSpec((blk, LANES), lambda p, i: (i, p)),
                   pl.BlockSpec((1, blk, LANES), lambda p, i: (p, i, 0))],
        out_shape=(SDS((SEQ, FOX_W), f32), SDS((npair, SEQ, LANES), f32)), name="fox_fwd",
        scratch_shapes=[pltpu.VMEM((2, blk, SEQ), f32)],
        compiler_params=_params(("parallel", "arbitrary")),
    )(q_aug, k_aug, v)


def _fox_bwd(q_aug, k_aug, v, do, o, lse, after):
    blk = FOX_BLK
    npair = FOX_W // LANES
    nblk = SEQ // blk

    def body(q_ref, k_ref, v_ref, do_ref, o_ref, lse_ref, after_ref, dq_ref, dk_ref, dv_ref, df_ref, dq_acc,
             delta_ref):
        lane_s = lax.broadcasted_iota(jnp.int32, (SEQ, LANES), 1)
        prod = do_ref[...].astype(bf16).astype(f32) * o_ref[...]
        d_a = jnp.sum(jnp.where(lane_s < HEAD_DIM, prod, 0.0), axis=1, keepdims=True)
        d_b = jnp.sum(jnp.where(lane_s >= HEAD_DIM, prod, 0.0), axis=1, keepdims=True)
        delta_ref[...] = jnp.where(lane_s < HEAD_DIM, d_a, d_b)
        dq_acc[...] = jnp.zeros_like(dq_acc)
        df_ref[...] = jnp.zeros_like(df_ref)
        lane = lax.broadcasted_iota(jnp.int32, (blk, LANES), 1)
        own = [lane < HEAD_DIM, lane >= HEAD_DIM]
        tri = lax.broadcasted_iota(jnp.int32, (blk, blk), 0) >= lax.broadcasted_iota(jnp.int32, (blk, blk), 1)

        def q_slab(qoff, h):
            return q_ref[pl.ds(qoff, blk), h * LANES:(h + 1) * LANES]

        def probs(qoff, h, k_h, masked):
            s = _nt(q_slab(qoff, h), k_h)
            if masked:
                s = jnp.where(tri, s, NEG)
            return jnp.exp(s - lse_ref[0, pl.ds(qoff, blk), h * HEAD_DIM:h * HEAD_DIM + 1])

        def k_slabs(koff):
            return [k_ref[pl.ds(koff, blk), h * LANES:(h + 1) * LANES] for h in range(2)]

        def kv_step(kj, _):
            koff = pl.multiple_of(kj * blk, blk)
            k_aug = k_slabs(koff)
            k_own = [jnp.where(own[h], k_aug[h], jnp.zeros_like(k_aug[h])) for h in range(2)]
            vv = v_ref[pl.ds(koff, blk), :]
            v_own = [jnp.where(own[h], vv, jnp.zeros_like(vv)) for h in range(2)]

            def q_tile(qi, carry, masked):
                qoff = pl.multiple_of(qi * blk, blk)
                dd = do_ref[pl.ds(qoff, blk), :].astype(bf16)
                new, dq_add = [], None
                for h in range(2):
                    dk_h, dv_h, dcol = carry[h]
                    p = probs(qoff, h, k_aug[h], masked)
                    dl = p * (_nt(dd, v_own[h]) - delta_ref[pl.ds(qoff, blk), h * HEAD_DIM:h * HEAD_DIM + 1])
                    dlb = dl.astype(bf16)
                    part = jnp.dot(dlb, k_own[h], preferred_element_type=f32)
                    dq_add = part if dq_add is None else dq_add + part
                    new.append((dk_h + _tn(dlb, q_slab(qoff, h)), dv_h + _tn(p.astype(bf16), dd),
                                dcol + _colsum(dl)))
                dq_acc[pl.ds(qoff, blk), :] += dq_add * ATT_SCALE
                return tuple(new)

            zero = (jnp.zeros((blk, LANES), f32), jnp.zeros((blk, LANES), f32), jnp.zeros((1, blk), f32))
            carry = q_tile(kj, (zero, zero), True)
            (dk_a, dv_a, dcol_a), (dk_b, dv_b, dcol_b) = lax.fori_loop(
                kj + 1, nblk, lambda qi, cr: q_tile(qi, cr, False), carry)
            dk_ref[pl.ds(koff, blk), :] = jnp.where(own[0], dk_a, dk_b).astype(dk_ref.dtype)
            dv_ref[pl.ds(koff, blk), :] = jnp.where(own[0], dv_a, dv_b).astype(dv_ref.dtype)
            df_ref[0, 0:1, pl.ds(koff, blk)] = -dcol_a
            df_ref[0, 1:2, pl.ds(koff, blk)] = -dcol_b
            return 0

        lax.fori_loop(0, nblk, kv_step, 0)
        dq_ref[...] = dq_acc[...].astype(dq_ref.dtype)

    pair_aug = pl.BlockSpec((SEQ, 2 * LANES), lambda p: (0, p))
    slab = pl.BlockSpec((SEQ, LANES), lambda p: (0, p))
    per_pair = pl.BlockSpec((1, SEQ, LANES), lambda p: (p, 0, 0))
    rows = pl.BlockSpec((1, 8, SEQ), lambda p: (p, 0, 0))
    return pl.pallas_call(
        body, grid=(npair,),
        in_specs=[pair_aug, pair_aug, slab, slab, slab, per_pair, pl.BlockSpec(memory_space=pl.ANY)],
        out_specs=[slab, slab, slab, rows],
        out_shape=(SDS((SEQ, FOX_W), bf16),) * 3 + (SDS((npair, 8, SEQ), f32),), name="fox_bwd",
        scratch_shapes=[pltpu.VMEM((SEQ, LANES), f32), pltpu.VMEM((SEQ, LANES), f32)],
        compiler_params=_params(("parallel",)),
    )(q_aug, k_aug, v, do, o, lse, after)


DIL_BLK = 128
DILATIONS = (1, 4, 16)
N_GROUPS = len(DILATIONS)
DIL_PAIRS = DIL_OUT_W // LANES


def _dil_blocks(d):
    r1 = lax.broadcasted_iota(jnp.int32, (DIL_BLK, DIL_BLK), 0)
    c1 = lax.broadcasted_iota(jnp.int32, (DIL_BLK, DIL_BLK), 1)
    r2 = lax.broadcasted_iota(jnp.int32, (DIL_BLK, 2 * DIL_BLK), 0)
    c2 = lax.broadcasted_iota(jnp.int32, (DIL_BLK, 2 * DIL_BLK), 1)
    band = ((c2 < DIL_BLK) & (c2 >= r2)) | ((c2 >= DIL_BLK) & (c2 - DIL_BLK <= r2))
    out = []
    for r in range(d):
        for b in range(SEQ // d // DIL_BLK):
            rows = pl.ds(r + d * DIL_BLK * b, DIL_BLK, stride=d)
            if b == 0:
                out.append((rows, rows, r1 >= c1))
            else:
                out.append((rows, pl.ds(r + d * DIL_BLK * (b - 1), 2 * DIL_BLK, stride=d), band))
    return out


def _dil_fwd(q, k, v, g):
    def body(q_ref, k_ref, v_ref, o_ref, lse_ref):
        first = lax.broadcasted_iota(jnp.int32, (DIL_BLK, LANES), 1) < HEAD_DIM
        for rows, krows, mask in _dil_blocks(DILATIONS[g]):
            qv, kk, vv = q_ref[rows, :].astype(bf16), k_ref[krows, :].astype(bf16), v_ref[krows, :].astype(bf16)
            outs, lses = [], []
            for own in (first, ~first):
                s = jnp.where(mask, _nt(jnp.where(own, qv, jnp.zeros_like(qv)), kk), NEG)
                m = jnp.max(s, axis=1, keepdims=True)
                p = jnp.exp(s - m)
                l = jnp.sum(p, axis=1, keepdims=True)
                outs.append(jnp.dot(p.astype(bf16), vv, preferred_element_type=f32) / l)
                lses.append(m + jnp.log(l))
            o_ref[rows, :] = jnp.where(first, outs[0], outs[1])
            lse_ref[rows, :] = jnp.where(first, lses[0], lses[1])

    grouped = pl.BlockSpec((SEQ, LANES), lambda p: (0, DIL_PAIRS * g + p))
    own = pl.BlockSpec((SEQ, LANES), lambda p: (0, p))
    shape = SDS((SEQ, DIL_OUT_W), f32)
    return pl.pallas_call(
        body, grid=(DIL_PAIRS,), in_specs=[grouped] * 3, out_specs=[own] * 2, out_shape=(shape, shape),
        name=f"dil_fwd_{DILATIONS[g]}", compiler_params=_params(("parallel",)),
    )(q, k, v)


def _dil_bwd(q, k, v, do, lse, delta, g):
    def body(q_ref, k_ref, v_ref, do_ref, lse_ref, dl_ref, dq_ref, dk_ref, dv_ref):
        first = lax.broadcasted_iota(jnp.int32, (DIL_BLK, LANES), 1) < HEAD_DIM
        dk_ref[...] = jnp.zeros_like(dk_ref)
        dv_ref[...] = jnp.zeros_like(dv_ref)
        for rows, krows, mask in _dil_blocks(DILATIONS[g]):
            qv, kk, vv = q_ref[rows, :].astype(bf16), k_ref[krows, :].astype(bf16), v_ref[krows, :].astype(bf16)
            dov = do_ref[rows, :].astype(bf16)
            lsev, delv = lse_ref[rows, :], dl_ref[rows, :]
            dqs, dk_add, dv_add = [], None, None
            for h, own in enumerate((first, ~first)):
                col = h * HEAD_DIM
                qh = jnp.where(own, qv, jnp.zeros_like(qv))
                doh = jnp.where(own, dov, jnp.zeros_like(dov))
                p = jnp.exp(jnp.where(mask, _nt(qh, kk), NEG) - lsev[:, col:col + 1])
                dl = (p * (_nt(doh, vv) - delv[:, col:col + 1])).astype(bf16)
                dqs.append(jnp.dot(dl, kk, preferred_element_type=f32))
                dk_h, dv_h = _tn(dl, qh), _tn(p.astype(bf16), doh)
                dk_add = dk_h if dk_add is None else dk_add + dk_h
                dv_add = dv_h if dv_add is None else dv_add + dv_h
            dq_ref[rows, :] = jnp.where(first, dqs[0], dqs[1]) * ATT_SCALE
            dk_ref[krows, :] += dk_add
            dv_ref[krows, :] += dv_add

    grouped = pl.BlockSpec((SEQ, LANES), lambda p: (0, DIL_PAIRS * g + p))
    own = pl.BlockSpec((SEQ, LANES), lambda p: (0, p))
    shape = SDS((SEQ, DIL_OUT_W), f32)
    return pl.pallas_call(
        body, grid=(DIL_PAIRS,), in_specs=[grouped] * 3 + [own] * 3, out_specs=[own] * 3,
        out_shape=(shape, shape, shape), name=f"dil_bwd_{DILATIONS[g]}", compiler_params=_params(("parallel",)),
    )(q, k, v, do, lse, delta)


def _position():
    return lax.axis_index("x"), lax.axis_index("y"), lax.axis_index("c")


def _all_gather(block, name):
    def body(x_ref, out_ref, send_sems, recv_sems, local_sem):
        x, y, c = _position()
        me, sibling = (x, y, c), (x, y, 1 - c)
        chips = [(1 - x, y), (x, 1 - y), (1 - x, 1 - y)]

        def slot(px, py, pc):
            return out_ref.at[4 * px + 2 * py + pc]

        def copy(k, blk, to, src=None):
            return pltpu.make_async_remote_copy(
                src_ref=slot(*blk) if src is None else src, dst_ref=slot(*blk),
                send_sem=send_sems.at[k], recv_sem=recv_sems.at[k], device_id=to, device_id_type=MESH)

        mine = pltpu.make_async_copy(x_ref, slot(*me), local_sem)
        mine.start()
        first = [copy(0, me, sibling, src=x_ref)]
        first += [copy(1 + j, me, (*chip, c), src=x_ref) for j, chip in enumerate(chips)]
        for cp in first:
            cp.start()
        passed = [copy(4 + j, (*chip, c), sibling) for j, chip in enumerate(chips)]
        for j, chip in enumerate(chips):
            copy(1 + j, (*chip, c), me).wait_recv()
            passed[j].start()
        copy(0, sibling, me).wait_recv()
        for j, chip in enumerate(chips):
            copy(4 + j, (*chip, 1 - c), me).wait_recv()
        for cp in first + passed:
            cp.wait_send()
        mine.wait()

    return pl.pallas_call(
        body, out_shape=SDS((N_DEV,) + block.shape, block.dtype),
        in_specs=[pl.BlockSpec(memory_space=pl.ANY)], out_specs=pl.BlockSpec(memory_space=pl.ANY),
        scratch_shapes=[pltpu.SemaphoreType.DMA((7,)), pltpu.SemaphoreType.DMA((7,)), pltpu.SemaphoreType.DMA],
        name=name,
    )(block)


def _all_gather_many(blocks, name):
    n = len(blocks)

    def body(*refs):
        x_refs, out_refs = refs[:n], refs[n:2 * n]
        send_sems, recv_sems, local_sems = refs[2 * n:]
        x, y, c = _position()
        me, sibling = (x, y, c), (x, y, 1 - c)
        chips = [(1 - x, y), (x, 1 - y), (1 - x, 1 - y)]

        def slot(a, px, py, pc):
            return out_refs[a].at[4 * px + 2 * py + pc]

        def copy(a, k, blk, to, own=False):
            return pltpu.make_async_remote_copy(
                src_ref=x_refs[a] if own else slot(a, *blk), dst_ref=slot(a, *blk),
                send_sem=send_sems.at[a, k], recv_sem=recv_sems.at[a, k], device_id=to, device_id_type=MESH)

        mine = [pltpu.make_async_copy(x_refs[a], slot(a, *me), local_sems.at[a]) for a in range(n)]
        for cp in mine:
            cp.start()
        started = []
        for a in range(n):
            first = [copy(a, 0, me, sibling, own=True)]
            first += [copy(a, 1 + j, me, (*chip, c), own=True) for j, chip in enumerate(chips)]
            for cp in first:
                cp.start()
            started += first
        for a in range(n):
            for j, chip in enumerate(chips):
                copy(a, 1 + j, (*chip, c), me).wait_recv()
                passed = copy(a, 4 + j, (*chip, c), sibling)
                passed.start()
                started.append(passed)
        for a in range(n):
            copy(a, 0, sibling, me).wait_recv()
            for j, chip in enumerate(chips):
                copy(a, 4 + j, (*chip, 1 - c), me).wait_recv()
        for cp in started:
            cp.wait_send()
        for cp in mine:
            cp.wait()

    hbm = pl.BlockSpec(memory_space=pl.ANY)
    return pl.pallas_call(
        body, out_shape=[SDS((N_DEV,) + b.shape, b.dtype) for b in blocks],
        in_specs=[hbm] * n, out_specs=[hbm] * n,
        scratch_shapes=[pltpu.SemaphoreType.DMA((n, 7)), pltpu.SemaphoreType.DMA((n, 7)),
                        pltpu.SemaphoreType.DMA((n,))],
        name=name,
    )(*blocks)


HBM_SPEC = pl.BlockSpec(memory_space=pltpu.HBM)
SEM_SPEC = pl.BlockSpec(memory_space=pltpu.SEMAPHORE)
SPLIT_COPY = pltpu.CompilerParams(has_side_effects=pltpu.SideEffectType.DATAFLOW_SIDE_EFFECTING)


def _in_hbm(t):
    return pltpu.with_memory_space_constraint(t, pltpu.HBM)


def _pair_copies(g_refs, land_refs, send_sems, recv_sems):
    x, y, c = _position()
    return [pltpu.make_async_remote_copy(
        src_ref=g.at[2 * k + (1 - c)], dst_ref=land.at[k], send_sem=send_sems.at[4 * a + k],
        recv_sem=recv_sems.at[4 * a + k], device_id=(x, y, 1 - c), device_id_type=MESH)
        for a, (g, land) in enumerate(zip(g_refs, land_refs, strict=True)) for k in range(4)]


def _chip_copies(t_refs, land_refs, send_sems, recv_sems):
    x, y, c = _position()
    chips = [(1 - x, y), (x, 1 - y), (1 - x, 1 - y)]
    return [pltpu.make_async_remote_copy(
        src_ref=t.at[2 * px + py], dst_ref=land.at[j], send_sem=send_sems.at[3 * a + j],
        recv_sem=recv_sems.at[3 * a + j], device_id=(px, py, c), device_id_type=MESH)
        for a, (t, land) in enumerate(zip(t_refs, land_refs, strict=True)) for j, (px, py) in enumerate(chips)]


_ROUNDS = {"pair": (_pair_copies, 4), "chip": (_chip_copies, 3)}


def _exchange_start(kind, ts, name):
    copies, slots = _ROUNDS[kind]
    n = len(ts)
    lands = [_in_hbm(lax.empty((slots,) + t.shape[1:], t.dtype)) for t in ts]

    def body(*refs):
        for cp in copies(refs[:n], refs[n:2 * n], refs[2 * n], refs[2 * n + 1]):
            cp.start()
        refs[-1][...] = jnp.zeros_like(refs[-1])

    sems = pltpu.SemaphoreType.DMA((slots * n,))
    res = pl.pallas_call(
        body, name=name, in_specs=[HBM_SPEC] * (2 * n),
        out_shape=(sems, sems, *[pltpu.HBM(t.shape, t.dtype) for t in (*ts, *lands)], SDS((8, LANES), f32)),
        out_specs=(SEM_SPEC, SEM_SPEC, *[HBM_SPEC] * (2 * n), pl.BlockSpec(memory_space=pltpu.VMEM)),
        input_output_aliases={i: 2 + i for i in range(2 * n)}, compiler_params=SPLIT_COPY,
    )(*[_in_hbm(t) for t in ts], *lands)
    return res[:-1], res[-1]


def _exchange_wait(kind, state, after, name):
    copies, _ = _ROUNDS[kind]
    send_sems, recv_sems, *arrays = state
    n = len(arrays) // 2

    def body(*refs):
        for cp in copies(refs[:n], refs[n:2 * n], refs[2 * n], refs[2 * n + 1]):
            cp.wait_send()
            cp.wait_recv()

    res = pl.pallas_call(
        body, name=name, in_specs=[HBM_SPEC] * (2 * n) + [SEM_SPEC, SEM_SPEC, pl.BlockSpec(memory_space=pl.ANY)],
        out_shape=[pltpu.HBM(t.shape, t.dtype) for t in arrays], out_specs=[HBM_SPEC] * (2 * n),
        input_output_aliases={i: i for i in range(2 * n)}, compiler_params=SPLIT_COPY,
    )(*arrays, send_sems, recv_sems, after)
    return res[:n], res[n:]


def _gather_copies(x_refs, out_refs, send_sems, recv_sems):
    x, y, c = _position()
    peers = [(x, y, 1 - c), (1 - x, y, c), (x, 1 - y, c), (1 - x, 1 - y, c)]
    sends, arrivals = [], []
    for a, (x_ref, out_ref) in enumerate(zip(x_refs, out_refs, strict=True)):
        for k, (px, py, pc) in enumerate(peers):
            sems = dict(send_sem=send_sems.at[4 * a + k], recv_sem=recv_sems.at[4 * a + k],
                        device_id=(px, py, pc), device_id_type=MESH)
            sends.append(pltpu.make_async_remote_copy(src_ref=x_ref, dst_ref=out_ref.at[4 * x + 2 * y + c], **sems))
            arrivals.append(pltpu.make_async_remote_copy(src_ref=x_ref, dst_ref=out_ref.at[4 * px + 2 * py + pc],
                                                         **sems))
    return sends, arrivals


def _gather_start(blocks, after, name):
    n = len(blocks)
    outs = [_in_hbm(lax.empty((N_DEV,) + b.shape, b.dtype)) for b in blocks]

    def body(*refs):
        sends, _ = _gather_copies(refs[:n], refs[n:2 * n], refs[2 * n + 1], refs[2 * n + 2])
        for cp in sends:
            cp.start()
        refs[-1][...] = jnp.zeros_like(refs[-1])

    sems = pltpu.SemaphoreType.DMA((4 * n,))
    res = pl.pallas_call(
        body, name=name, in_specs=[HBM_SPEC] * (2 * n) + [pl.BlockSpec(memory_space=pl.ANY)],
        out_shape=(sems, sems, *[pltpu.HBM(t.shape, t.dtype) for t in (*blocks, *outs)], SDS((8, LANES), f32)),
        out_specs=(SEM_SPEC, SEM_SPEC, *[HBM_SPEC] * (2 * n), pl.BlockSpec(memory_space=pltpu.VMEM)),
        input_output_aliases={i: 2 + i for i in range(2 * n)}, compiler_params=SPLIT_COPY,
    )(*[_in_hbm(b) for b in blocks], *outs, after)
    return res[:-1], res[-1]


def _gather_wait(state, after, name):
    send_sems, recv_sems, *arrays = state
    n = len(arrays) // 2

    def body(*refs):
        sends, arrivals = _gather_copies(refs[:n], refs[n:2 * n], refs[2 * n], refs[2 * n + 1])
        for cp in sends:
            cp.wait_send()
        for cp in arrivals:
            cp.wait_recv()

    res = pl.pallas_call(
        body, name=name, in_specs=[HBM_SPEC] * (2 * n) + [SEM_SPEC, SEM_SPEC, pl.BlockSpec(memory_space=pl.ANY)],
        out_shape=[pltpu.HBM(t.shape, t.dtype) for t in arrays], out_specs=[HBM_SPEC] * (2 * n),
        input_output_aliases={i: i for i in range(2 * n)}, compiler_params=SPLIT_COPY,
    )(*arrays, send_sems, recv_sems, after)
    return res[:n], res[n:]


def _gather_finish(partial, name):
    n = len(partial)

    def body(*refs):
        in_refs, out_refs = refs[:n], refs[n:2 * n]
        send_sems, recv_sems = refs[2 * n:]
        x, y, c = _position()
        chips = [(1 - x, y), (x, 1 - y), (1 - x, 1 - y)]
        copies = []
        for a in range(n):
            for j, (px, py) in enumerate(chips):
                cp = pltpu.make_async_remote_copy(
                    src_ref=in_refs[a].at[4 * px + 2 * py + c], dst_ref=out_refs[a].at[4 * px + 2 * py + c],
                    send_sem=send_sems.at[a, j], recv_sem=recv_sems.at[a, j], device_id=(x, y, 1 - c),
                    device_id_type=MESH)
                cp.start()
                copies.append(cp)
        for a in range(n):
            for j, (px, py) in enumerate(chips):
                pltpu.make_async_remote_copy(
                    src_ref=in_refs[a].at[4 * px + 2 * py + (1 - c)], dst_ref=out_refs[a].at[4 * px + 2 * py + (1 - c)],
                    send_sem=send_sems.at[a, j], recv_sem=recv_sems.at[a, j], device_id=(x, y, 1 - c),
                    device_id_type=MESH).wait_recv()
        for cp in copies:
            cp.wait_send()

    hbm = pl.BlockSpec(memory_space=pl.ANY)
    return pl.pallas_call(
        body, out_shape=[SDS(p.shape, p.dtype) for p in partial], in_specs=[hbm] * n, out_specs=[hbm] * n,
        input_output_aliases={a: a for a in range(n)},
        scratch_shapes=[pltpu.SemaphoreType.DMA((n, 3)), pltpu.SemaphoreType.DMA((n, 3))],
        name=name,
    )(*partial)


def _row_tile(rows):
    return 512 if rows % 512 == 0 and rows > 512 else rows


def _pair_add(g, r1, core, name):
    def body(c_ref, g_ref, r_ref, o_ref):
        o_ref[...] = (g_ref[...].astype(f32) + r_ref[...].astype(f32)).astype(o_ref.dtype)

    rows, cols = g.shape[1:]
    tile = _row_tile(rows)
    blk = (1, tile, cols)
    return pl.pallas_call(
        body, out_shape=SDS((4, rows, cols), g.dtype), name=name,
        grid_spec=pltpu.PrefetchScalarGridSpec(
            num_scalar_prefetch=1, grid=(4, rows // tile),
            in_specs=[pl.BlockSpec(blk, lambda k, i, c_ref: (2 * k + c_ref[0], i, 0)),
                      pl.BlockSpec(blk, lambda k, i, c_ref: (k, i, 0))],
            out_specs=pl.BlockSpec(blk, lambda k, i, c_ref: (k, i, 0))),
        compiler_params=_params(("parallel", "arbitrary")),
    )(core, g, r1)


def _chip_add(t, r2, chip, name):
    def body(c_ref, t_ref, r_ref, o_ref):
        o_ref[...] = ((t_ref[0].astype(f32) + r_ref[0].astype(f32)) + r_ref[1].astype(f32)) + r_ref[2].astype(f32)

    rows, cols = t.shape[1:]
    tile = _row_tile(rows)
    return pl.pallas_call(
        body, out_shape=SDS((rows, cols), f32), name=name,
        grid_spec=pltpu.PrefetchScalarGridSpec(
            num_scalar_prefetch=1, grid=(rows // tile,),
            in_specs=[pl.BlockSpec((1, tile, cols), lambda i, c_ref: (c_ref[0], i, 0)),
                      pl.BlockSpec((3, tile, cols), lambda i, c_ref: (0, i, 0))],
            out_specs=pl.BlockSpec((tile, cols), lambda i, c_ref: (i, 0))),
        compiler_params=_params(("arbitrary",)),
    )(chip, t, r2)


def _pad_to(t, axis, size):
    pads = [(0, 0)] * t.ndim
    pads[axis] = (0, size - t.shape[axis])
    return jnp.pad(t, pads)


_REF_COLS = {"qa": (0, FOX_W), "ka": (FOX_W, FOX_W), "va": (2 * FOX_W, FOX_W), "f": (3 * FOX_W, N_FOX_HEADS)}
_REF_COLS.update({n: (3 * FOX_W + N_FOX_HEADS + i * DIL_W, DIL_W) for i, n in enumerate(("qb", "kb", "vb"))})
_REF_COLS.update({n: (3 * FOX_W + N_FOX_HEADS + 3 * DIL_W + i * D, D) for i, n in enumerate(("ga", "gb"))})
_REF_ORDER = ("qa", "ka", "va", "f", "qb", "kb", "vb", "ga", "gb")


def _place_cols(sources, src_of, out_cols, name, row_block=512):
    arrays = [s[0] if isinstance(s, tuple) else s for s in sources]
    widths = [a.shape[-1] for a in arrays]
    rows = arrays[0].shape[-2]
    plan = []
    for t in range(out_cols // LANES):
        segs, c, end = [], t * LANES, (t + 1) * LANES
        while c < end:
            s = src_of(c)
            if s is None:
                c += 1
                continue
            n = 1
            while c + n < end and src_of(c + n) == (s[0], s[1] + n):
                n += 1
            segs.append((s[0], s[1], c - t * LANES, n))
            c += n
        plan.append(segs)

    def body(*refs):
        o_ref = refs[-1]
        for t, segs in enumerate(plan):
            acc = None
            for si, c0, o0, n in segs:
                a0 = c0 // LANES * LANES
                wide = min(2 * LANES, widths[si] - a0)
                win = refs[si][0, :, a0:a0 + wide] if isinstance(sources[si], tuple) else refs[si][:, a0:a0 + wide]
                r = lax.broadcasted_iota(jnp.int32, (wide, LANES), 0)
                c = lax.broadcasted_iota(jnp.int32, (wide, LANES), 1)
                pick = ((r - (c0 - a0) == c - o0) & (c >= o0) & (c < o0 + n)).astype(bf16)
                part = jnp.dot(win.astype(bf16), pick, preferred_element_type=f32)
                acc = part if acc is None else acc + part
            tile = jnp.zeros((row_block, LANES), f32) if acc is None else acc
            o_ref[:, t * LANES:(t + 1) * LANES] = tile.astype(o_ref.dtype)

    def spec(s):
        if isinstance(s, tuple):
            j = s[1]
            return pl.BlockSpec((1, row_block, s[0].shape[-1]), lambda i: (j, i, 0))
        return pl.BlockSpec((row_block, s.shape[-1]), lambda i: (i, 0))

    return pl.pallas_call(
        body, grid=(rows // row_block,), in_specs=[spec(s) for s in sources],
        out_specs=pl.BlockSpec((row_block, out_cols), lambda i: (i, 0)), out_shape=SDS((rows, out_cols), bf16),
        name=name, compiler_params=_params(("parallel",)),
    )(*arrays)


def _ref_piece(r):
    for name in _REF_ORDER:
        lo, width = _REF_COLS[name]
        if lo <= r < lo + width:
            return name, r - lo
    raise ValueError(r)


def _shard_pad_cols(pieces):
    names = [n for n in _REF_ORDER if n != "vb"]
    sources = [pieces[n] for n in names] + list(pieces["vb"])

    def src_of(c):
        j, i = divmod(c, W_IN_PAD)
        if i >= W_IN_SH:
            return None
        name, col = _ref_piece(j * W_IN_SH + i)
        if name == "vb":
            return len(names) + col // DIL_OUT_W, col % DIL_OUT_W
        return names.index(name), col

    return _place_cols(sources, src_of, N_DEV * W_IN_PAD, "place_dproj")


_SLABS = {"ga": C_GA, "gb": C_GB, "qb": C_QB, "kb": C_KB, "vb": C_VB, "qa": C_QA, "ka": C_KA, "va": C_VA, "f": C_F}


def _slab_w_in(stack):
    def src_of(c):
        for name, start in _SLABS.items():
            lo, width = _REF_COLS[name]
            if start <= c < start + width:
                return divmod(lo + c - start, W_IN_SH)
        return None

    return _place_cols([(stack, j) for j in range(N_DEV)], src_of, PROJ_W, "place_w_in")


def kernel(x, c, w_ada, b_ada, g_mix, w_in, b_fgate, w_br_a, w_br_b, w_out, g_ffn, w_ffn_gate, w_ffn_up, w_ffn_down, g_final, loss_target, m_w_ada, m_b_ada, m_g_mix, m_w_in, m_b_fgate, m_w_br_a, m_w_br_b, m_w_out, m_g_ffn, m_w_ffn_gate, m_w_ffn_up, m_w_ffn_down, m_g_final, v_w_ada, v_b_ada, v_g_mix, v_w_in, v_b_fgate, v_w_br_a, v_w_br_b, v_w_out, v_g_ffn, v_w_ffn_gate, v_w_ffn_up, v_w_ffn_down, v_g_final):
    px, py, pc = _position()
    dev = 4 * px + 2 * py + pc
    x2d, tgt = x[0], loss_target[0]

    c_all = _all_gather(c, "gather_c").reshape(N_DEV, D)
    ada_cols = w_ada.shape[2]
    b_shard = lax.dynamic_slice(b_ada, (0, dev * ada_cols), (1, ada_cols))
    mod_shard = _ada_fwd(c_all, w_ada[0], b_shard)
    mod_all = _all_gather(mod_shard, "gather_mod")
    modv = lax.dynamic_index_in_dim(mod_all, dev, axis=1, keepdims=False).reshape(6, D)

    gate_up = jnp.concatenate([_pad_to(w_ffn_gate[0], 1, FF_PAD), _pad_to(w_ffn_up[0], 1, FF_PAD)], axis=1)
    w_in_s, = _all_gather_many([_pad_to(w_in[0], 1, W_IN_PAD).astype(bf16)], "gather_w_in")
    later = [w_br_a[0], w_br_b[0], w_out[0], gate_up, _pad_to(w_ffn_down[0], 0, FF_PAD)]
    later_state, later_token = _gather_start([t.astype(bf16) for t in later], w_in_s, "gather_rest_start")
    w_in_p = _slab_w_in(w_in_s)

    h1 = _pre1(x2d, modv, g_mix)
    proj = _matmul(h1, w_in_p, name="mm_proj", tm=SEQ, tn=896, tk=D, after=later_token)
    b_pad = jnp.pad(b_fgate, ((0, 0), (0, LANES - N_FOX_HEADS)))
    q_aug, k_aug, va = _fox_prep(proj, _fox_gate_fwd(proj, b_pad))
    ya_h, lse_a = _fox_fwd(q_aug, k_aug, va)

    tables = _rope_tables()
    qb_r, kb_r, vb = _rope_fwd(proj, tables)
    by_group = [_dil_fwd(qb_r, kb_r, vb, grp) for grp in range(N_GROUPS)]
    yb_h, lse_b = _dil_combine([o for o, _ in by_group], [l for _, l in by_group])

    mine, arrived = _gather_wait(later_state, yb_h, "gather_rest_wait")
    w_a_s, w_b_s, w_o_s, w_gu_s, w_d_s = [
        lax.dynamic_update_slice(stack, block[None], (dev, 0, 0))
        for stack, block in zip(_gather_finish(arrived, "gather_rest_finish"), mine, strict=True)]
    w_o = w_o_s.reshape(D, D)
    w_d = w_d_s.reshape(FF_HID, D)
    ya = _matmul_stack(ya_h, w_a_s, name="mm_br_a")
    yb = _matmul_stack(yb_h, w_b_s, name="mm_br_b")

    merged = _merge_fwd(ya, yb, proj)
    mix = _matmul(merged, w_o, name="mm_out", tm=SEQ, tn=512, tk=D)
    x1, h2 = _post1(x2d, mix, modv, g_ffn)
    act, au = _ffn_in(h2, w_gu_s)
    ff = _matmul(act, w_d, name="mm_ffn_down", tm=SEQ // 2, tn=512, tk=FF_HID)

    dx2, dff, dg_final, dga_f, loss_lanes = _final(x1, ff, tgt, modv, g_final.reshape(1, D))
    dau = _ffn_bwd_in(dff, w_d_s, au)

    core = pc.astype(jnp.int32).reshape(1)
    chip = (2 * px + py).astype(jnp.int32).reshape(1)

    def pair_done(state, after, tags, name):
        mine, theirs = _exchange_wait("pair", state, after, "pair_wait_" + name)
        sums = [_pair_add(g, r, core, "pair_add_" + t) for g, r, t in zip(mine, theirs, tags)]
        return _exchange_start("chip", sums, "chip_start_" + name)

    def from_chips(state, after, tags, name):
        sums, got = _exchange_wait("chip", state, after, "chip_wait_" + name)
        return [_chip_add(p, r, chip, "chip_add_" + t) for p, r, t in zip(sums, got, tags)]

    g_gu = _matmul(h2, dau, ta=True, by_shard=True, out_dtype=bf16, name="mm_g_ffn_in", tm=D, tn=2 * FF_PAD, tk=SEQ)
    g_d = _matmul(act, dff, ta=True, out_dtype=bf16, name="mm_g_down", tm=FF_HID // 2, tn=512, tk=SEQ)
    ffn_tags = ["gu", "down"]
    ffn_pair, ffn_pair_token = _exchange_start("pair", [g_gu, g_d.reshape(N_DEV, FF_PAD, D)], "pair_start_ffn")

    dh2 = _matmul_nt_shards(dau, w_gu_s, ffn_pair_token, name="mm_d_h2")
    ffn_state, ffn_token = pair_done(ffn_pair, dh2, ffn_tags, "ffn")
    dx1, dmix, dsh_f, dsc_f, dg_ffn, dga_m = _mid_bwd(dh2, x1, dx2, mix, modv, g_ffn)
    dmerged = _matmul(dmix, w_o, tb=True, name="mm_d_merged", tm=SEQ, tn=512, tk=D, after=ffn_token)
    dya, dyb, dga, dgb = _merge_bwd(dmerged, ya, yb, proj)
    dya_h = _matmul_stack(dya, w_a_s, tb=True, name="mm_d_ya")
    dyb_h = _matmul_stack(dyb, w_b_s, tb=True, name="mm_d_yb")

    g_o = _matmul(merged, dmix, ta=True, out_dtype=bf16, name="mm_g_out", tm=D, tn=512, tk=SEQ)
    g_a = _matmul_stack(ya_h, dya, ta=True, out_dtype=bf16, name="mm_g_br_a")
    g_b = _matmul_stack(yb_h, dyb, ta=True, out_dtype=bf16, name="mm_g_br_b")
    rows_a, rows_b = FOX_W * W_BR_SH // D, DIL_OUT_W * W_BR_SH // D
    g_small = jnp.concatenate([g_a.reshape(N_DEV, rows_a, D), g_b.reshape(N_DEV, rows_b, D),
                               g_o.reshape(N_DEV, W_BR_SH, D)], axis=1)
    small_pair, small_pair_token = _exchange_start("pair", [g_small], "pair_start_small")

    dqa, dka, dva, dF = _fox_bwd(q_aug, k_aug, va, dya_h, ya_h, lse_a, small_pair_token)
    dF_row = jnp.pad(dF[:, :2, :].reshape(N_FOX_HEADS, SEQ), ((0, LANES - N_FOX_HEADS), (0, 0)))
    df, db_fgate = _fox_gate_bwd(dF_row, proj, b_pad)
    small_state, small_token = pair_done(small_pair, df, ["small"], "small")

    delta_b = _dil_delta(dyb_h, yb_h)
    dil_grads = [_dil_bwd(qb_r, kb_r, vb, dyb_h, lse_b, delta_b, grp) for grp in range(N_GROUPS)]
    dqb, dkb = _rope_bwd([t[0] for t in dil_grads], [t[1] for t in dil_grads], tables)

    dproj = _shard_pad_cols({"qa": dqa, "ka": dka, "va": dva, "f": df, "qb": dqb, "kb": dkb,
                             "vb": [t[2] for t in dil_grads], "ga": dga, "gb": dgb})
    g_in = _matmul(h1, dproj, ta=True, by_shard=True, out_dtype=bf16, name="mm_g_in", tm=D, tn=W_IN_PAD, tk=SEQ,
                   after=small_token)
    mix_tags = ["in"]
    mix_pair, mix_pair_token = _exchange_start("pair", [g_in], "pair_start_mixer")

    dh1 = _matmul_nt_shards(dproj, w_in_s, mix_pair_token, name="mm_d_h1")
    grad_x, dsh_m, dsc_m, dg_mix = _first_bwd(dh1, x2d, dx1, modv, g_mix)

    pad_lane = lambda t: jnp.pad(t, ((0, 0), (0, D - t.shape[1])))
    small = jnp.concatenate([dsh_m, dsc_m, dga_m, dsh_f, dsc_f, dga_f, dg_mix, dg_ffn, dg_final,
                             pad_lane(db_fgate), loss_lanes, jnp.zeros((SMALL_ROWS - 11, D), f32)], axis=0)
    small_all = _all_gather(small, "gather_small")
    mix_state, mix_token = pair_done(mix_pair, small_all, mix_tags, "mixer")

    small_sum, loss_row = _small_reduce(small_all, mix_token)
    dmod_all = small_all[:, :6, :].reshape(N_DEV, 6 * D)
    g_w_ada = _ada_bwd(c_all, lax.dynamic_slice(dmod_all, (0, dev * ada_cols), (N_DEV, ada_cols)))
    s_gu, s_d = from_chips(ffn_state, small_sum, ffn_tags, "ffn")
    s_small, = from_chips(small_state, small_sum, ["small"], "small")

    loss = loss_row[0, 0]
    g = {
        "w_ada": g_w_ada[None], "b_ada": small_sum[0:6].reshape(1, 6 * D), "g_mix": small_sum[6:7],
        "b_fgate": small_sum[9:10, :N_FOX_HEADS], "g_ffn": small_sum[7:8], "w_ffn_gate": s_gu[None, :, :W_FF_SH],
        "w_ffn_up": s_gu[None, :, FF_PAD:FF_PAD + W_FF_SH], "w_ffn_down": s_d[None, :W_FF_SH],
        "g_final": small_sum[8], "w_br_a": s_small[:rows_a].reshape(1, FOX_W, W_BR_SH),
        "w_br_b": s_small[rows_a:rows_a + rows_b].reshape(1, DIL_OUT_W, W_BR_SH), "w_out": s_small[None, rows_a + rows_b:],
    }
    w = {"w_ada": w_ada, "b_ada": b_ada, "g_mix": g_mix, "w_in": w_in, "b_fgate": b_fgate, "w_br_a": w_br_a,
         "w_br_b": w_br_b, "w_out": w_out, "g_ffn": g_ffn, "w_ffn_gate": w_ffn_gate, "w_ffn_up": w_ffn_up,
         "w_ffn_down": w_ffn_down, "g_final": g_final}
    m = {"w_ada": m_w_ada, "b_ada": m_b_ada, "g_mix": m_g_mix, "w_in": m_w_in, "b_fgate": m_b_fgate,
         "w_br_a": m_w_br_a, "w_br_b": m_w_br_b, "w_out": m_w_out, "g_ffn": m_g_ffn, "w_ffn_gate": m_w_ffn_gate,
         "w_ffn_up": m_w_ffn_up, "w_ffn_down": m_w_ffn_down, "g_final": m_g_final}
    v = {"w_ada": v_w_ada, "b_ada": v_b_ada, "g_mix": v_g_mix, "w_in": v_w_in, "b_fgate": v_b_fgate,
         "w_br_a": v_w_br_a, "w_br_b": v_w_br_b, "w_out": v_w_out, "g_ffn": v_g_ffn, "w_ffn_gate": v_w_ffn_gate,
         "w_ffn_up": v_w_ffn_up, "w_ffn_down": v_w_ffn_down, "g_final": v_g_final}
    names = list(w)
    delta, new_m, new_v = {}, {}, {}

    transposed = ("w_in", "w_ffn_gate", "w_ffn_up")

    def update(n):
        shape = w[n].shape
        if n in transposed:
            g_t = g[n][0].T
            dl, mn, vn = _adamw(w[n][0].T, g_t, m[n][0].T, v[n][0].T, "adamw_" + n)
            g[n], delta[n], new_m[n], new_v[n] = g_t.T[None], dl.T[None], mn.T[None], vn.T[None]
            return
        two_d = (lambda t: t.reshape(shape[-2:])) if len(shape) == 3 else (lambda t: t)
        dl, mn, vn = _adamw(two_d(w[n]), two_d(g[n]), two_d(m[n]), two_d(v[n]), "adamw_" + n)
        delta[n], new_m[n], new_v[n] = dl.reshape(shape), mn.reshape(shape), vn.reshape(shape)

    for n in list(g):
        update(n)
    done = sum(delta[n].reshape(-1)[:N_FOX_HEADS] for n in g)
    s_in, = from_chips(mix_state, done, mix_tags, "mixer")
    g["w_in"] = s_in[None, :, :W_IN_SH]
    update("w_in")

    return (loss, grad_x[None], *[g[n] for n in names], *[delta[n] for n in names],
            *[new_m[n] for n in names], *[new_v[n] for n in names])
```

```python
import functools

import jax
import jax.numpy as jnp
from jax import lax
from jax.experimental import pallas as pl
from jax.experimental.pallas import tpu as pltpu

f32 = jnp.float32
bf16 = jnp.bfloat16
SDS = jax.ShapeDtypeStruct
MESH = pl.DeviceIdType.MESH

N_DEV = 8
D = 1024
SEQ = 2048
HEAD_DIM = 64
N_FOX_HEADS = 8
FOX_W = 512
DIL_W = 768
DIL_OUT_W = 256
ROT_DIM = 16
ROPE_THETA = 500000.0
D_FF = 2816
IN_COLS = 5896
EPS = 1e-6
NEG = -1e30
ATT_SCALE = HEAD_DIM ** -0.5

ADAM_LR = 0.001
ADAM_B1 = 0.9
ADAM_B2 = 0.999
ADAM_EPS = 1e-08
ADAM_WD = 0.01
ADAM_STEP = 10

C_GA, C_GB, C_QB, C_KB, C_VB, C_QA, C_KA, C_VA, C_F = 0, 1024, 2304, 3072, 3840, 4608, 5120, 5632, 6144
PROJ_W = 6272
LANES = 128
VMEM_LIMIT = 52 * 1024 * 1024

W_IN_SH, W_IN_PAD = IN_COLS // N_DEV, 768
W_BR_SH = D // N_DEV
W_FF_SH, FF_PAD = D_FF // N_DEV, 384
FF_HID = N_DEV * FF_PAD
SMALL_ROWS = 16


def _params(sem=None):
    if sem is None:
        return pltpu.CompilerParams(vmem_limit_bytes=VMEM_LIMIT)
    return pltpu.CompilerParams(dimension_semantics=sem, vmem_limit_bytes=VMEM_LIMIT)


def _rowwise(fn, name, tiled, vecs, outs, reds=(), tile=256):
    nt, nv, no = len(tiled), len(vecs), len(outs)
    rows = tiled[0][0].shape[0]
    assert rows % tile == 0

    def body(*refs):
        tin = [r[...] for r in refs[:nt]]
        vin = [r[...] for r in refs[nt:nt + nv]]
        orefs = refs[nt + nv:nt + nv + no]
        rrefs = refs[nt + nv + no:]
        touts, routs = fn(tin, vin)
        for r, t in zip(orefs, touts, strict=True):
            r[...] = t.astype(r.dtype)
        if rrefs:
            @pl.when(pl.program_id(0) == 0)
            def _():
                for r in rrefs:
                    r[...] = jnp.zeros_like(r)
            for r, t in zip(rrefs, routs, strict=True):
                r[...] += t

    def col_map(cb):
        return lambda i: (i, cb)

    def whole_map(nd):
        return lambda i: (0,) * nd

    in_specs = [pl.BlockSpec((tile, w), col_map(cb)) for (_, w, cb) in tiled]
    in_specs += [pl.BlockSpec(v.shape, whole_map(v.ndim)) for v in vecs]
    out_specs = [pl.BlockSpec((tile, w), lambda i: (i, 0)) for (w, _) in outs]
    out_specs += [pl.BlockSpec((1, w), lambda i: (0, 0)) for w in reds]
    out_shape = [SDS((rows, w), dt) for (w, dt) in outs] + [SDS((1, w), f32) for w in reds]
    res = pl.pallas_call(
        body, grid=(rows // tile,), in_specs=in_specs, out_specs=out_specs, out_shape=out_shape, name=name,
        compiler_params=_params(("arbitrary",)),
    )(*[t[0] for t in tiled], *vecs)
    return res


def _matmul(a, b, *, ta=False, tb=False, out_dtype=f32, name, tm, tn, tk, by_shard=False, after=None):
    m, k = (a.shape[1], a.shape[0]) if ta else a.shape
    if by_shard and not ta:
        n, kb = (b.shape[1], N_DEV * b.shape[2]) if tb else (N_DEV * b.shape[2], b.shape[1])
        assert (tk if tb else tn) == b.shape[2]
    else:
        n, kb = (b.shape[0], b.shape[1]) if tb else (b.shape[1], b.shape[0])
    assert kb == k and m % tm == 0 and n % tn == 0 and k % tk == 0
    nk = k // tk
    dims = (((0 if ta else 1,), (1 if tb else 0,)), ((), ()))
    b_stacked = by_shard and not ta
    o_stacked = by_shard and ta

    def body(a_ref, b_ref, *rest):
        o_ref, *acc = rest[1:] if after is not None else rest
        bv = b_ref[0] if b_stacked else b_ref[...]
        p = lax.dot_general(a_ref[...].astype(bf16), bv.astype(bf16), dims, preferred_element_type=f32)

        def put(val):
            if o_stacked:
                o_ref[0] = val.astype(o_ref.dtype)
            else:
                o_ref[...] = val.astype(o_ref.dtype)

        if nk == 1:
            put(p)
        else:
            acc_ref, = acc
            kk = pl.program_id(2)

            @pl.when(kk == 0)
            def _():
                acc_ref[...] = p

            @pl.when(kk > 0)
            def _():
                acc_ref[...] += p

            @pl.when(kk == nk - 1)
            def _():
                put(acc_ref[...])

    a_spec = pl.BlockSpec((tk, tm), lambda i, j, kk: (kk, i)) if ta else pl.BlockSpec((tm, tk), lambda i, j, kk: (i, kk))
    if b_stacked and tb:
        b_spec = pl.BlockSpec((1, tn, tk), lambda i, j, kk: (kk, j, 0))
    elif b_stacked:
        b_spec = pl.BlockSpec((1, tk, tn), lambda i, j, kk: (j, kk, 0))
    elif tb:
        b_spec = pl.BlockSpec((tn, tk), lambda i, j, kk: (j, kk))
    else:
        b_spec = pl.BlockSpec((tk, tn), lambda i, j, kk: (kk, j))
    if o_stacked:
        assert tn == n // N_DEV
        out_spec = pl.BlockSpec((1, tm, tn), lambda i, j, kk: (j, i, 0))
        out_shape = SDS((N_DEV, m, tn), out_dtype)
    else:
        out_spec = pl.BlockSpec((tm, tn), lambda i, j, kk: (i, j))
        out_shape = SDS((m, n), out_dtype)
    extra_specs, extra = ([pl.BlockSpec(memory_space=pl.ANY)], [after]) if after is not None else ([], [])
    return pl.pallas_call(
        body, grid=(m // tm, n // tn, nk), in_specs=[a_spec, b_spec] + extra_specs, out_specs=out_spec,
        out_shape=out_shape, name=name, scratch_shapes=[pltpu.VMEM((tm, tn), f32)] if nk > 1 else [],
        compiler_params=_params(("parallel", "parallel", "arbitrary")),
    )(a, b, *extra)


def _matmul_stack(a, b, *, ta=False, tb=False, out_dtype=f32, name):
    def lanes(ref):
        return jnp.concatenate([ref[j] for j in range(N_DEV)], axis=1).astype(bf16)

    if ta:
        w = b.shape[1] // N_DEV

        def body(a_ref, b_ref, o_ref):
            p = _tn(a_ref[...].astype(bf16), b_ref[...].astype(bf16))
            for j in range(N_DEV):
                o_ref[j] = p[:, j * w:(j + 1) * w].astype(o_ref.dtype)

        return pl.pallas_call(body, out_shape=SDS((N_DEV, a.shape[1], w), out_dtype), name=name,
                              compiler_params=_params())(a, b)

    m, half = a.shape[0], a.shape[0] // 2
    n = b.shape[1] if tb else N_DEV * b.shape[2]

    def body(a_ref, b_ref, o_ref):
        av = a_ref[...].astype(bf16)
        o_ref[...] = (_nt(av, lanes(b_ref)) if tb else jnp.dot(av, lanes(b_ref), preferred_element_type=f32)
                      ).astype(o_ref.dtype)

    return pl.pallas_call(
        body, grid=(2,), in_specs=[pl.BlockSpec((half, a.shape[1]), lambda i: (i, 0)),
                                   pl.BlockSpec(b.shape, lambda i: (0, 0, 0))],
        out_specs=pl.BlockSpec((half, n), lambda i: (i, 0)), out_shape=SDS((m, n), out_dtype), name=name,
        compiler_params=_params(("parallel",)),
    )(a, b)


def _matmul_nt_shards(a, b, after, *, name, tm=512, tn=1024):
    m, n, w = a.shape[0], b.shape[1], b.shape[2]
    assert a.shape[1] == N_DEV * w and m % tm == 0 and n % tn == 0

    def body(a_ref, b_ref, after_ref, o_ref):
        acc = _nt(a_ref[:, 0:w], b_ref[0])
        for j in range(1, N_DEV):
            acc = acc + _nt(a_ref[:, j * w:(j + 1) * w], b_ref[j])
        o_ref[...] = acc

    return pl.pallas_call(
        body, grid=(n // tn, m // tm),
        in_specs=[pl.BlockSpec((tm, N_DEV * w), lambda j, i: (i, 0)), pl.BlockSpec((N_DEV, tn, w), lambda j, i: (0, j, 0)),
                  pl.BlockSpec(memory_space=pl.ANY)],
        out_specs=pl.BlockSpec((tm, tn), lambda j, i: (i, j)), out_shape=SDS((m, n), f32), name=name,
        compiler_params=_params(("parallel", "parallel")),
    )(a, b, after)


def _rms(x):
    r = lax.rsqrt(jnp.mean(x * x, axis=-1, keepdims=True) + EPS)
    return r, x * r


def _rms_bwd(r, xn, dxn):
    return r * (dxn - xn * jnp.mean(dxn * xn, axis=-1, keepdims=True))


def _colsum(t):
    return jnp.sum(t, axis=0, keepdims=True)


def _sigmoid(x):
    return 1.0 / (1.0 + jnp.exp(-x))


def _modulated_norm(x, g, shift, scale):
    _, xn = _rms(x)
    return (xn * g) * (1.0 + scale) + shift


def _pre1(x, modv, g_mix):
    def fn(t, v):
        (xt,), (mv, g) = t, v
        return [_modulated_norm(xt, g, mv[0:1], mv[1:2])], []
    return _rowwise(fn, "pre1", [(x, D, 0)], [modv, g_mix], [(D, bf16)])[0]


def _post1(x, mix, modv, g_ffn):
    def fn(t, v):
        (xt, mt), (mv, g) = t, v
        x1 = xt + mv[2:3] * mt
        return [x1, _modulated_norm(x1, g, mv[3:4], mv[4:5])], []
    return _rowwise(fn, "post1", [(x, D, 0), (mix, D, 0)], [modv, g_ffn], [(D, f32), (D, bf16)])


def _ffn_in(h, w_stack):
    def body(h_ref, w_ref, act_ref, au_ref):
        p = jnp.dot(h_ref[...], w_ref[0], preferred_element_type=f32)
        a, u = p[:, :FF_PAD], p[:, FF_PAD:]
        act_ref[...] = (a * _sigmoid(a) * u).astype(act_ref.dtype)
        au_ref[...] = p.astype(au_ref.dtype)

    return pl.pallas_call(
        body, grid=(N_DEV,),
        in_specs=[pl.BlockSpec((SEQ, D), lambda j: (0, 0)), pl.BlockSpec((1, D, 2 * FF_PAD), lambda j: (j, 0, 0))],
        out_specs=[pl.BlockSpec((SEQ, FF_PAD), lambda j: (0, j)), pl.BlockSpec((SEQ, 2 * FF_PAD), lambda j: (0, j))],
        out_shape=(SDS((SEQ, FF_HID), bf16), SDS((SEQ, 2 * FF_HID), bf16)), name="ffn_in",
        compiler_params=_params(("parallel",)),
    )(h, w_stack)


def _ffn_bwd_in(dff, w_down_stack, au):
    def body(d_ref, w_ref, au_ref, o_ref):
        dact = _nt(d_ref[...], w_ref[0])
        p = au_ref[...].astype(f32)
        a, u = p[:, :FF_PAD], p[:, FF_PAD:]
        sg = _sigmoid(a)
        o_ref[...] = jnp.concatenate([dact * u * (sg * (1.0 + a * (1.0 - sg))), dact * (a * sg)],
                                     axis=1).astype(o_ref.dtype)

    return pl.pallas_call(
        body, grid=(N_DEV,),
        in_specs=[pl.BlockSpec((SEQ, D), lambda j: (0, 0)), pl.BlockSpec((1, FF_PAD, D), lambda j: (j, 0, 0)),
                  pl.BlockSpec((SEQ, 2 * FF_PAD), lambda j: (0, j))],
        out_specs=pl.BlockSpec((SEQ, 2 * FF_PAD), lambda j: (0, j)),
        out_shape=SDS((SEQ, 2 * FF_HID), bf16), name="ffn_bwd_in", compiler_params=_params(("parallel",)),
    )(dff, w_down_stack, au)


def _final(x1, ff, target, modv, g_final):
    def fn(t, v):
        (x1t, fft, tgt), (mv, g) = t, v
        x2 = x1t + mv[5:6] * fft
        r, xn = _rms(x2)
        err = xn * g - tgt
        dy = err * (1.0 / D)
        dx2 = _rms_bwd(r, xn, dy * g)
        return [dx2, dx2 * mv[5:6]], [_colsum(dy * xn), _colsum(dx2 * fft), _colsum(err * err) * (0.5 / D)]
    return _rowwise(fn, "final", [(x1, D, 0), (ff, D, 0), (target, D, 0)], [modv, g_final],
                    [(D, f32), (D, bf16)], [D, D, D])


def _mid_bwd(dh2, x1, dx2, mix, modv, g_ffn):
    def fn(t, v):
        (dh, x1t, dx2t, mt), (mv, g) = t, v
        r, xn = _rms(x1t)
        dn = dh * (1.0 + mv[4:5])
        dx1 = dx2t + _rms_bwd(r, xn, dn * g)
        return [dx1, dx1 * mv[2:3]], [_colsum(dh), _colsum(dh * (xn * g)), _colsum(dn * xn), _colsum(dx1 * mt)]
    return _rowwise(fn, "mid_bwd", [(dh2, D, 0), (x1, D, 0), (dx2, D, 0), (mix, D, 0)], [modv, g_ffn],
                    [(D, f32), (D, bf16)], [D, D, D, D])


def _first_bwd(dh1, x, dx1, modv, g_mix):
    def fn(t, v):
        (dh, xt, dx1t), (mv, g) = t, v
        r, xn = _rms(xt)
        dn = dh * (1.0 + mv[1:2])
        return [dx1t + _rms_bwd(r, xn, dn * g)], [_colsum(dh), _colsum(dh * (xn * g)), _colsum(dn * xn)]
    return _rowwise(fn, "first_bwd", [(dh1, D, 0), (x, D, 0), (dx1, D, 0)], [modv, g_mix], [(D, f32)], [D, D, D])


def _merge_fwd(ya, yb, proj):
    def fn(t, v):
        ya_t, yb_t, ga, gb = t
        return [_sigmoid(ga) * ya_t + _sigmoid(gb) * yb_t], []
    return _rowwise(fn, "merge_fwd", [(ya, D, 0), (yb, D, 0), (proj, D, C_GA // D), (proj, D, C_GB // D)], [],
                    [(D, bf16)])[0]


def _merge_bwd(dmerged, ya, yb, proj):
    def fn(t, v):
        dm, ya_t, yb_t, ga, gb = t
        sa, sb = _sigmoid(ga), _sigmoid(gb)
        return [dm * sa, dm * sb, dm * ya_t * (sa * (1.0 - sa)), dm * yb_t * (sb * (1.0 - sb))], []
    return _rowwise(fn, "merge_bwd",
                    [(dmerged, D, 0), (ya, D, 0), (yb, D, 0), (proj, D, C_GA // D), (proj, D, C_GB // D)], [],
                    [(D, bf16), (D, bf16), (D, bf16), (D, bf16)])


def _rope_tables():
    half = ROT_DIM // 2
    pos = jnp.arange(SEQ, dtype=f32)
    inv_freq = ROPE_THETA ** (-jnp.arange(0, ROT_DIM, 2, dtype=f32) / ROT_DIM)
    ang = pos[:, None] * inv_freq[None, :]
    cos, sin = jnp.cos(ang), jnp.sin(ang)
    pad = jnp.zeros((SEQ, HEAD_DIM - ROT_DIM), f32)
    zero = jnp.zeros((SEQ, half), f32)
    c_head = jnp.concatenate([cos, cos, pad + 1.0], axis=1)
    lo_head = jnp.concatenate([-sin, zero, pad], axis=1)
    hi_head = jnp.concatenate([zero, sin, pad], axis=1)
    return tuple(jnp.concatenate([t, t], axis=1) for t in (c_head, lo_head, hi_head))


def _over_heads(tables):
    return [jnp.tile(t, (1, DIL_W // LANES)) for t in tables]


def _rope_fwd(proj, tables):
    half = ROT_DIM // 2

    def fn(t, v):
        q, k, vv = t[:3]
        c, lo, hi = _over_heads(t[3:])
        rot = lambda z: z * c + pltpu.roll(z, DIL_W - half, 1) * lo + pltpu.roll(z, half, 1) * hi
        return [rot(q) * ATT_SCALE, rot(k), vv], []
    return _rowwise(fn, "rope_fwd", [(proj, DIL_W, C_QB // DIL_W), (proj, DIL_W, C_KB // DIL_W),
                                     (proj, DIL_W, C_VB // DIL_W)] + [(tb, LANES, 0) for tb in tables], [],
                    [(DIL_W, f32)] * 3)


def _rope_bwd(dqs, dks, tables):
    half = ROT_DIM // 2

    def fn(t, v):
        dq_t, dk_t = jnp.concatenate(t[:N_GROUPS], axis=1), jnp.concatenate(t[N_GROUPS:2 * N_GROUPS], axis=1)
        c, lo, hi = _over_heads(t[2 * N_GROUPS:])
        rot_t = lambda z: z * c + pltpu.roll(z * lo, half, 1) + pltpu.roll(z * hi, DIL_W - half, 1)
        return [rot_t(dq_t), rot_t(dk_t)], []
    return _rowwise(fn, "rope_bwd", [(a, DIL_OUT_W, 0) for a in (*dqs, *dks)] + [(tb, LANES, 0) for tb in tables],
                    [], [(DIL_W, bf16), (DIL_W, bf16)])


def _head_bcast_sum(d):
    lane = lax.broadcasted_iota(jnp.int32, d.shape, 1)
    out = jnp.zeros_like(d)
    for h in range(d.shape[1] // HEAD_DIM):
        sel = (lane >= h * HEAD_DIM) & (lane < (h + 1) * HEAD_DIM)
        out = jnp.where(sel, jnp.sum(jnp.where(sel, d, 0.0), axis=1, keepdims=True), out)
    return out


def _dil_combine(outs, lses):
    def fn(t, v):
        o0, o1, o2, l0, l1, l2 = t
        m = jnp.maximum(jnp.maximum(l0, l1), l2)
        w0, w1, w2 = jnp.exp(l0 - m), jnp.exp(l1 - m), jnp.exp(l2 - m)
        tot = w0 + w1 + w2
        return [(w0 * o0 + w1 * o1 + w2 * o2) / tot, m + jnp.log(tot)], []
    w = DIL_OUT_W
    return _rowwise(fn, "dil_combine", [(t, w, 0) for t in (*outs, *lses)], [], [(w, f32), (w, f32)])


def _dil_delta(dyb_h, yb_h):
    def fn(t, v):
        return [_head_bcast_sum(t[0] * t[1])], []
    return _rowwise(fn, "dil_delta", [(dyb_h, DIL_OUT_W, 0), (yb_h, DIL_OUT_W, 0)], [], [(DIL_OUT_W, f32)])[0]


def _adamw_math(wt, gt, mt, vt):
    mn = ADAM_B1 * mt + (1.0 - ADAM_B1) * gt
    vn = ADAM_B2 * vt + (1.0 - ADAM_B2) * (gt * gt)
    m_hat = mn / (1.0 - ADAM_B1 ** ADAM_STEP)
    v_hat = vn / (1.0 - ADAM_B2 ** ADAM_STEP)
    return -ADAM_LR * (m_hat / (jnp.sqrt(v_hat) + ADAM_EPS) + ADAM_WD * wt), mn, vn


def _adamw(w, g, m, v, name):
    shape = w.shape
    if w.ndim == 1:
        w, g, m, v = (t.reshape(1, -1) for t in (w, g, m, v))
    rows, cols = w.shape
    if rows % 8 and rows > 8:
        return _adamw_by_cols(w, g, m, v, name)
    tile = 256 if rows % 256 == 0 and rows > 512 else rows

    def fn(t, _):
        return list(_adamw_math(*t)), []
    delta, mn, vn = _rowwise(fn, name, [(w, cols, 0), (g, cols, 0), (m, cols, 0), (v, cols, 0)], [],
                             [(cols, f32)] * 3, tile=tile)
    return delta.reshape(shape), mn.reshape(shape), vn.reshape(shape)


def _adamw_by_cols(w, g, m, v, name, tile=256):
    rows, cols = w.shape

    def body(w_ref, g_ref, m_ref, v_ref, d_ref, mn_ref, vn_ref):
        d_ref[...], mn_ref[...], vn_ref[...] = _adamw_math(w_ref[...], g_ref[...], m_ref[...], v_ref[...])

    spec = pl.BlockSpec((rows, tile), lambda j: (0, j))
    return pl.pallas_call(body, grid=(cols // tile,), in_specs=[spec] * 4, out_specs=[spec] * 3,
                          out_shape=[SDS((rows, cols), f32)] * 3, name=name,
                          compiler_params=_params(("parallel",)))(w, g, m, v)


def _ada_fwd(c_all, w_shard, b_shard):
    def body(c_ref, w_ref, b_ref, o_ref):
        cv = c_ref[...]
        sc = (cv * _sigmoid(cv)).astype(bf16)
        o_ref[...] = jnp.dot(sc, w_ref[...].astype(bf16), preferred_element_type=f32) + b_ref[...]
    return pl.pallas_call(body, out_shape=SDS((N_DEV, w_shard.shape[1]), f32), name="ada_fwd",
                          compiler_params=_params())(c_all, w_shard, b_shard)


def _ada_bwd(c_all, dmod_cols):
    def body(c_ref, d_ref, o_ref):
        cv = c_ref[...]
        sc = cv * _sigmoid(cv)
        o_ref[...] = lax.dot_general(sc, d_ref[...], (((0,), (0,)), ((), ())), precision=lax.Precision.HIGHEST,
                                     preferred_element_type=f32)
    return pl.pallas_call(body, out_shape=SDS((D, dmod_cols.shape[1]), f32), name="ada_bwd",
                          compiler_params=_params())(c_all, dmod_cols)


def _small_reduce(gathered, after):
    def body(g_ref, after_ref, o_ref, loss_ref):
        acc = g_ref[0]
        for d in range(1, N_DEV):
            acc = acc + g_ref[d]
        o_ref[...] = acc
        loss_ref[...] = jnp.zeros((1, LANES), f32) + jnp.sum(acc[10:11, :])
    return pl.pallas_call(body, out_shape=(SDS((SMALL_ROWS, D), f32), SDS((1, LANES), f32)), name="small_reduce",
                          in_specs=[pl.BlockSpec(memory_space=pltpu.VMEM), pl.BlockSpec(memory_space=pl.ANY)],
                          compiler_params=_params())(gathered, after)


FOX_BLK = 512
CUM_BLK = 128


def _fold_lanes(t, op):
    out = t[:, :LANES]
    for j in range(1, t.shape[1] // LANES):
        out = op(out, t[:, j * LANES:(j + 1) * LANES])
    return out


def _fox_gate_fwd(proj, b_pad):
    nblk = SEQ // CUM_BLK

    def body(f_ref, b_ref, col_ref):
        r = lax.broadcasted_iota(jnp.int32, (CUM_BLK, CUM_BLK), 0)
        c = lax.broadcasted_iota(jnp.int32, (CUM_BLK, CUM_BLK), 1)
        tri = (r >= c).astype(f32)
        carry = jnp.zeros((1, LANES), f32)
        for blk in range(nblk):
            z = f_ref[blk * CUM_BLK:(blk + 1) * CUM_BLK, :] + b_ref[...]
            logf = jnp.minimum(z, 0.0) - jnp.log1p(jnp.exp(-jnp.abs(z)))
            cs = jnp.dot(tri, logf, precision=lax.Precision.HIGHEST, preferred_element_type=f32) + carry
            col_ref[blk * CUM_BLK:(blk + 1) * CUM_BLK, :] = cs
            carry = cs[CUM_BLK - 1:CUM_BLK, :]

    return pl.pallas_call(
        body, grid=(1,), in_specs=[pl.BlockSpec((SEQ, LANES), lambda i: (0, C_F // LANES)),
                                   pl.BlockSpec((1, LANES), lambda i: (0, 0))],
        out_specs=pl.BlockSpec((SEQ, LANES), lambda i: (0, 0)),
        out_shape=SDS((SEQ, LANES), f32), name="fox_gate_fwd",
        compiler_params=_params(("arbitrary",)),
    )(proj, b_pad)


def _fox_gate_bwd(dF_row, proj, b_pad):
    nblk = SEQ // CUM_BLK

    def body(d_ref, f_ref, b_ref, df_ref, db_ref, col_ref):
        r = lax.broadcasted_iota(jnp.int32, (CUM_BLK, CUM_BLK), 0)
        c = lax.broadcasted_iota(jnp.int32, (CUM_BLK, CUM_BLK), 1)
        tri = (r <= c).astype(f32)
        lane = lax.broadcasted_iota(jnp.int32, (CUM_BLK, LANES), 1)
        col_ref[...] = d_ref[...].T
        carry = jnp.zeros((1, LANES), f32)
        total = jnp.zeros((1, LANES), f32)
        for blk in reversed(range(nblk)):
            rows = slice(blk * CUM_BLK, (blk + 1) * CUM_BLK)
            cs = jnp.dot(tri, col_ref[rows, :], precision=lax.Precision.HIGHEST, preferred_element_type=f32) + carry
            carry = cs[0:1, :]
            z = f_ref[rows, :] + b_ref[...]
            df = jnp.where(lane < N_FOX_HEADS, cs * _sigmoid(-z), 0.0)
            df_ref[rows, :] = df.astype(df_ref.dtype)
            total = total + _colsum(df)
        db_ref[...] = total

    return pl.pallas_call(
        body, grid=(1,), in_specs=[pl.BlockSpec((LANES, SEQ), lambda i: (0, 0)),
                                   pl.BlockSpec((SEQ, LANES), lambda i: (0, C_F // LANES)),
                                   pl.BlockSpec((1, LANES), lambda i: (0, 0))],
        out_specs=[pl.BlockSpec((SEQ, LANES), lambda i: (0, 0)), pl.BlockSpec((1, LANES), lambda i: (0, 0))],
        out_shape=(SDS((SEQ, LANES), bf16), SDS((1, LANES), f32)), name="fox_gate_bwd",
        scratch_shapes=[pltpu.VMEM((SEQ, LANES), f32)],
        compiler_params=_params(("arbitrary",)),
    )(dF_row, proj, b_pad)


def _nt(a, b):
    return lax.dot_general(a, b, (((1,), (1,)), ((), ())), preferred_element_type=f32)


def _tn(a, b):
    return lax.dot_general(a, b, (((0,), (0,)), ((), ())), preferred_element_type=f32)


def _fox_prep(proj, f_col):
    def fn(t, v):
        q, k, vv, fc = t
        lane = lax.broadcasted_iota(jnp.int32, (q.shape[0], LANES), 1)
        qs, ks = [], []
        for h in range(N_FOX_HEADS):
            pair, pos = divmod(h, 2)
            own = (lane >= pos * HEAD_DIM) & (lane < (pos + 1) * HEAD_DIM)
            base = (1 - pos) * HEAD_DIM
            f = fc[:, h:h + 1]
            hi = f.astype(bf16).astype(f32)
            mid = (f - hi).astype(bf16).astype(f32)
            lo = (f - hi) - mid
            one = jnp.ones_like(f)
            qa = jnp.where(own, q[:, pair * LANES:(pair + 1) * LANES] * ATT_SCALE, 0.0)
            ka = k[:, pair * LANES:(pair + 1) * LANES]
            for idx, (qv, kv) in enumerate([(hi, one), (mid, one), (lo, one), (one, -hi), (one, -mid), (one, -lo)]):
                sel = lane == base + idx
                qa = jnp.where(sel, qv, qa)
                ka = jnp.where(sel, kv, ka)
            qs.append(qa)
            ks.append(ka)
        return [jnp.concatenate(qs, axis=1), jnp.concatenate(ks, axis=1), vv], []
    w = N_FOX_HEADS * LANES
    return _rowwise(fn, "fox_prep", [(proj, FOX_W, C_QA // FOX_W), (proj, FOX_W, C_KA // FOX_W),
                                     (proj, FOX_W, C_VA // FOX_W), (f_col, LANES, 0)], [],
                    [(w, bf16), (w, bf16), (FOX_W, bf16)])


def _fox_fwd(q_aug, k_aug, v):
    blk = FOX_BLK
    npair = FOX_W // LANES

    def body(q_ref, k_ref, v_ref, o_ref, lse_ref, s_scr):
        i = pl.program_id(1)
        tri = lax.broadcasted_iota(jnp.int32, (blk, blk), 0) >= lax.broadcasted_iota(jnp.int32, (blk, blk), 1)
        qh = [q_ref[:, h * LANES:(h + 1) * LANES] for h in range(2)]

        def logits(c, masked):
            off = pl.multiple_of(c * blk, blk)
            tops = []
            for h in range(2):
                s = _nt(qh[h], k_ref[pl.ds(off, blk), h * LANES:(h + 1) * LANES])
                if masked:
                    s = jnp.where(tri, s, NEG)
                s_scr[h, :, pl.ds(off, blk)] = s
                tops.append(_fold_lanes(s, jnp.maximum))
            return tops

        def pass_a(c, m):
            return tuple(jnp.maximum(a, b) for a, b in zip(m, logits(c, False)))

        m = lax.fori_loop(0, i, pass_a, tuple(jnp.full((blk, LANES), NEG, f32) for _ in range(2)))
        mx = [jnp.max(jnp.maximum(a, b), axis=1, keepdims=True) for a, b in zip(m, logits(i, True))]

        def pass_b(c, carry):
            off = pl.multiple_of(c * blk, blk)
            vv = v_ref[pl.ds(off, blk), :]
            new = []
            for h in range(2):
                l, acc = carry[h]
                p = jnp.exp(s_scr[h, :, pl.ds(off, blk)] - mx[h])
                hi = p.astype(bf16)
                lo = (p - hi.astype(f32)).astype(bf16)
                new.append((l + _fold_lanes(p, jnp.add),
                            acc + jnp.dot(hi, vv, preferred_element_type=f32)
                            + jnp.dot(lo, vv, preferred_element_type=f32)))
            return tuple(new)

        zero = jnp.zeros((blk, LANES), f32)
        (l_a, acc_a), (l_b, acc_b) = lax.fori_loop(0, i + 1, pass_b, ((zero, zero), (zero, zero)))
        l_a = jnp.sum(l_a, axis=1, keepdims=True)
        l_b = jnp.sum(l_b, axis=1, keepdims=True)
        first = lax.broadcasted_iota(jnp.int32, (blk, LANES), 1) < HEAD_DIM
        o_ref[...] = jnp.where(first, acc_a / l_a, acc_b / l_b)
        lse_ref[0] = jnp.where(first, mx[0] + jnp.log(l_a), mx[1] + jnp.log(l_b))

    return pl.pallas_call(
        body, grid=(npair, SEQ // blk),
        in_specs=[pl.BlockSpec((blk, 2 * LANES), lambda p, i: (i, p)),
                  pl.BlockSpec((SEQ, 2 * LANES), lambda p, i: (0, p)),
                  pl.BlockSpec((SEQ, LANES), lambda p, i: (0, p))],
        out_specs=[pl.BlockSpec((blk, LANES), lambda p, i: (i, p)),
                   pl.BlockSpec((1, blk, LANES), lambda p, i: (p, i, 0))],
        out_shape=(SDS((SEQ, FOX_W), f32), SDS((npair, SEQ, LANES), f32)), name="fox_fwd",
        scratch_shapes=[pltpu.VMEM((2, blk, SEQ), f32)],
        compiler_params=_params(("parallel", "arbitrary")),
    )(q_aug, k_aug, v)


def _fox_bwd(q_aug, k_aug, v, do, o, lse, after):
    blk = FOX_BLK
    npair = FOX_W // LANES
    nblk = SEQ // blk

    def body(q_ref, k_ref, v_ref, do_ref, o_ref, lse_ref, after_ref, dq_ref, dk_ref, dv_ref, df_ref, dq_acc,
             delta_ref):
        lane_s = lax.broadcasted_iota(jnp.int32, (SEQ, LANES), 1)
        prod = do_ref[...].astype(bf16).astype(f32) * o_ref[...]
        d_a = jnp.sum(jnp.where(lane_s < HEAD_DIM, prod, 0.0), axis=1, keepdims=True)
        d_b = jnp.sum(jnp.where(lane_s >= HEAD_DIM, prod, 0.0), axis=1, keepdims=True)
        delta_ref[...] = jnp.where(lane_s < HEAD_DIM, d_a, d_b)
        dq_acc[...] = jnp.zeros_like(dq_acc)
        df_ref[...] = jnp.zeros_like(df_ref)
        lane = lax.broadcasted_iota(jnp.int32, (blk, LANES), 1)
        own = [lane < HEAD_DIM, lane >= HEAD_DIM]
        tri = lax.broadcasted_iota(jnp.int32, (blk, blk), 0) >= lax.broadcasted_iota(jnp.int32, (blk, blk), 1)

        def q_slab(qoff, h):
            return q_ref[pl.ds(qoff, blk), h * LANES:(h + 1) * LANES]

        def probs(qoff, h, k_h, masked):
            s = _nt(q_slab(qoff, h), k_h)
            if masked:
                s = jnp.where(tri, s, NEG)
            return jnp.exp(s - lse_ref[0, pl.ds(qoff, blk), h * HEAD_DIM:h * HEAD_DIM + 1])

        def k_slabs(koff):
            return [k_ref[pl.ds(koff, blk), h * LANES:(h + 1) * LANES] for h in range(2)]

        def kv_step(kj, _):
            koff = pl.multiple_of(kj * blk, blk)
            k_aug = k_slabs(koff)
            k_own = [jnp.where(own[h], k_aug[h], jnp.zeros_like(k_aug[h])) for h in range(2)]
            vv = v_ref[pl.ds(koff, blk), :]
            v_own = [jnp.where(own[h], vv, jnp.zeros_like(vv)) for h in range(2)]

            def q_tile(qi, carry, masked):
                qoff = pl.multiple_of(qi * blk, blk)
                dd = do_ref[pl.ds(qoff, blk), :].astype(bf16)
                new, dq_add = [], None
                for h in range(2):
                    dk_h, dv_h, dcol = carry[h]
                    p = probs(qoff, h, k_aug[h], masked)
                    dl = p * (_nt(dd, v_own[h]) - delta_ref[pl.ds(qoff, blk), h * HEAD_DIM:h * HEAD_DIM + 1])
                    dlb = dl.astype(bf16)
                    part = jnp.dot(dlb, k_own[h], preferred_element_type=f32)
                    dq_add = part if dq_add is None else dq_add + part
                    new.append((dk_h + _tn(dlb, q_slab(qoff, h)), dv_h + _tn(p.astype(bf16), dd),
                                dcol + _colsum(dl)))
                dq_acc[pl.ds(qoff, blk), :] += dq_add * ATT_SCALE
                return tuple(new)

            zero = (jnp.zeros((blk, LANES), f32), jnp.zeros((blk, LANES), f32), jnp.zeros((1, blk), f32))
            carry = q_tile(kj, (zero, zero), True)
            (dk_a, dv_a, dcol_a), (dk_b, dv_b, dcol_b) = lax.fori_loop(
                kj + 1, nblk, lambda qi, cr: q_tile(qi, cr, False), carry)
            dk_ref[pl.ds(koff, blk), :] = jnp.where(own[0], dk_a, dk_b).astype(dk_ref.dtype)
            dv_ref[pl.ds(koff, blk), :] = jnp.where(own[0], dv_a, dv_b).astype(dv_ref.dtype)
            df_ref[0, 0:1, pl.ds(koff, blk)] = -dcol_a
            df_ref[0, 1:2, pl.ds(koff, blk)] = -dcol_b
            return 0

        lax.fori_loop(0, nblk, kv_step, 0)
        dq_ref[...] = dq_acc[...].astype(dq_ref.dtype)

    pair_aug = pl.BlockSpec((SEQ, 2 * LANES), lambda p: (0, p))
    slab = pl.BlockSpec((SEQ, LANES), lambda p: (0, p))
    per_pair = pl.BlockSpec((1, SEQ, LANES), lambda p: (p, 0, 0))
    rows = pl.BlockSpec((1, 8, SEQ), lambda p: (p, 0, 0))
    return pl.pallas_call(
        body, grid=(npair,),
        in_specs=[pair_aug, pair_aug, slab, slab, slab, per_pair, pl.BlockSpec(memory_space=pl.ANY)],
        out_specs=[slab, slab, slab, rows],
        out_shape=(SDS((SEQ, FOX_W), bf16),) * 3 + (SDS((npair, 8, SEQ), f32),), name="fox_bwd",
        scratch_shapes=[pltpu.VMEM((SEQ, LANES), f32), pltpu.VMEM((SEQ, LANES), f32)],
        compiler_params=_params(("parallel",)),
    )(q_aug, k_aug, v, do, o, lse, after)


DIL_BLK = 128
DILATIONS = (1, 4, 16)
N_GROUPS = len(DILATIONS)
DIL_PAIRS = DIL_OUT_W // LANES


def _dil_blocks(d):
    r1 = lax.broadcasted_iota(jnp.int32, (2 * DIL_BLK, DIL_BLK), 0) & (DIL_BLK - 1)
    c1 = lax.broadcasted_iota(jnp.int32, (2 * DIL_BLK, DIL_BLK), 1)
    r2 = lax.broadcasted_iota(jnp.int32, (2 * DIL_BLK, 2 * DIL_BLK), 0) & (DIL_BLK - 1)
    c2 = lax.broadcasted_iota(jnp.int32, (2 * DIL_BLK, 2 * DIL_BLK), 1)
    band = ((c2 < DIL_BLK) & (c2 >= r2)) | ((c2 >= DIL_BLK) & (c2 - DIL_BLK <= r2))
    out = []
    for r in range(d):
        for b in range(SEQ // d // DIL_BLK):
            rows = pl.ds(r + d * DIL_BLK * b, DIL_BLK, stride=d)
            if b == 0:
                out.append((rows, rows, r1 >= c1))
            else:
                out.append((rows, pl.ds(r + d * DIL_BLK * (b - 1), 2 * DIL_BLK, stride=d), band))
    return out


def _stack_heads(t, first):
    zero = jnp.zeros_like(t)
    return jnp.concatenate([jnp.where(first, t, zero), jnp.where(first, zero, t)], axis=0)


def _dil_fwd(q, k, v, g):
    def body(q_ref, k_ref, v_ref, o_ref, lse_ref):
        first = lax.broadcasted_iota(jnp.int32, (DIL_BLK, LANES), 1) < HEAD_DIM
        for rows, krows, mask in _dil_blocks(DILATIONS[g]):
            qv, kk, vv = q_ref[rows, :].astype(bf16), k_ref[krows, :].astype(bf16), v_ref[krows, :].astype(bf16)
            s = jnp.where(mask, _nt(_stack_heads(qv, first), kk), NEG)
            m = jnp.max(s, axis=1, keepdims=True)
            p = jnp.exp(s - m)
            l = jnp.sum(p, axis=1, keepdims=True)
            out = jnp.dot(p.astype(bf16), vv, preferred_element_type=f32) / l
            lse = m + jnp.log(l)
            o_ref[rows, :] = jnp.where(first, out[:DIL_BLK], out[DIL_BLK:])
            lse_ref[rows, :] = jnp.where(first, lse[:DIL_BLK], lse[DIL_BLK:])

    grouped = pl.BlockSpec((SEQ, LANES), lambda p: (0, DIL_PAIRS * g + p))
    own = pl.BlockSpec((SEQ, LANES), lambda p: (0, p))
    shape = SDS((SEQ, DIL_OUT_W), f32)
    return pl.pallas_call(
        body, grid=(DIL_PAIRS,), in_specs=[grouped] * 3, out_specs=[own] * 2, out_shape=(shape, shape),
        name=f"dil_fwd_{DILATIONS[g]}", compiler_params=_params(("parallel",)),
    )(q, k, v)


def _dil_bwd(q, k, v, do, lse, delta, g):
    def body(q_ref, k_ref, v_ref, do_ref, lse_ref, dl_ref, dq_ref, dk_ref, dv_ref):
        first = lax.broadcasted_iota(jnp.int32, (DIL_BLK, LANES), 1) < HEAD_DIM
        dk_ref[...] = jnp.zeros_like(dk_ref)
        dv_ref[...] = jnp.zeros_like(dv_ref)
        for rows, krows, mask in _dil_blocks(DILATIONS[g]):
            qv, kk, vv = q_ref[rows, :].astype(bf16), k_ref[krows, :].astype(bf16), v_ref[krows, :].astype(bf16)
            lsev, delv = lse_ref[rows, :], dl_ref[rows, :]
            q2 = _stack_heads(qv, first)
            do2 = _stack_heads(do_ref[rows, :].astype(bf16), first)
            per_head = lambda t: jnp.concatenate([t[:, 0:1], t[:, HEAD_DIM:HEAD_DIM + 1]], axis=0)
            p = jnp.exp(jnp.where(mask, _nt(q2, kk), NEG) - per_head(lsev))
            dl = (p * (_nt(do2, vv) - per_head(delv))).astype(bf16)
            dq = jnp.dot(dl, kk, preferred_element_type=f32)
            dq_ref[rows, :] = jnp.where(first, dq[:DIL_BLK], dq[DIL_BLK:]) * ATT_SCALE
            dk_ref[krows, :] += _tn(dl, q2)
            dv_ref[krows, :] += _tn(p.astype(bf16), do2)

    grouped = pl.BlockSpec((SEQ, LANES), lambda p: (0, DIL_PAIRS * g + p))
    own = pl.BlockSpec((SEQ, LANES), lambda p: (0, p))
    shape = SDS((SEQ, DIL_OUT_W), f32)
    return pl.pallas_call(
        body, grid=(DIL_PAIRS,), in_specs=[grouped] * 3 + [own] * 3, out_specs=[own] * 3,
        out_shape=(shape, shape, shape), name=f"dil_bwd_{DILATIONS[g]}", compiler_params=_params(("parallel",)),
    )(q, k, v, do, lse, delta)


def _position():
    return lax.axis_index("x"), lax.axis_index("y"), lax.axis_index("c")


def _all_gather(block, name):
    def body(x_ref, out_ref, send_sems, recv_sems, local_sem):
        x, y, c = _position()
        me, sibling = (x, y, c), (x, y, 1 - c)
        chips = [(1 - x, y), (x, 1 - y), (1 - x, 1 - y)]

        def slot(px, py, pc):
            return out_ref.at[4 * px + 2 * py + pc]

        def copy(k, blk, to, src=None):
            return pltpu.make_async_remote_copy(
                src_ref=slot(*blk) if src is None else src, dst_ref=slot(*blk),
                send_sem=send_sems.at[k], recv_sem=recv_sems.at[k], device_id=to, device_id_type=MESH)

        mine = pltpu.make_async_copy(x_ref, slot(*me), local_sem)
        mine.start()
        first = [copy(0, me, sibling, src=x_ref)]
        first += [copy(1 + j, me, (*chip, c), src=x_ref) for j, chip in enumerate(chips)]
        for cp in first:
            cp.start()
        passed = [copy(4 + j, (*chip, c), sibling) for j, chip in enumerate(chips)]
        for j, chip in enumerate(chips):
            copy(1 + j, (*chip, c), me).wait_recv()
            passed[j].start()
        copy(0, sibling, me).wait_recv()
        for j, chip in enumerate(chips):
            copy(4 + j, (*chip, 1 - c), me).wait_recv()
        for cp in first + passed:
            cp.wait_send()
        mine.wait()

    return pl.pallas_call(
        body, out_shape=SDS((N_DEV,) + block.shape, block.dtype),
        in_specs=[pl.BlockSpec(memory_space=pl.ANY)], out_specs=pl.BlockSpec(memory_space=pl.ANY),
        scratch_shapes=[pltpu.SemaphoreType.DMA((7,)), pltpu.SemaphoreType.DMA((7,)), pltpu.SemaphoreType.DMA],
        name=name,
    )(block)


def _all_gather_many(blocks, name):
    n = len(blocks)

    def body(*refs):
        x_refs, out_refs = refs[:n], refs[n:2 * n]
        send_sems, recv_sems, local_sems = refs[2 * n:]
        x, y, c = _position()
        me, sibling = (x, y, c), (x, y, 1 - c)
        chips = [(1 - x, y), (x, 1 - y), (1 - x, 1 - y)]

        def slot(a, px, py, pc):
            return out_refs[a].at[4 * px + 2 * py + pc]

        def copy(a, k, blk, to, own=False):
            return pltpu.make_async_remote_copy(
                src_ref=x_refs[a] if own else slot(a, *blk), dst_ref=slot(a, *blk),
                send_sem=send_sems.at[a, k], recv_sem=recv_sems.at[a, k], device_id=to, device_id_type=MESH)

        mine = [pltpu.make_async_copy(x_refs[a], slot(a, *me), local_sems.at[a]) for a in range(n)]
        for cp in mine:
            cp.start()
        started = []
        for a in range(n):
            first = [copy(a, 0, me, sibling, own=True)]
            first += [copy(a, 1 + j, me, (*chip, c), own=True) for j, chip in enumerate(chips)]
            for cp in first:
                cp.start()
            started += first
        for a in range(n):
            for j, chip in enumerate(chips):
                copy(a, 1 + j, (*chip, c), me).wait_recv()
                passed = copy(a, 4 + j, (*chip, c), sibling)
                passed.start()
                started.append(passed)
        for a in range(n):
            copy(a, 0, sibling, me).wait_recv()
            for j, chip in enumerate(chips):
                copy(a, 4 + j, (*chip, 1 - c), me).wait_recv()
        for cp in started:
            cp.wait_send()
        for cp in mine:
            cp.wait()

    hbm = pl.BlockSpec(memory_space=pl.ANY)
    return pl.pallas_call(
        body, out_shape=[SDS((N_DEV,) + b.shape, b.dtype) for b in blocks],
        in_specs=[hbm] * n, out_specs=[hbm] * n,
        scratch_shapes=[pltpu.SemaphoreType.DMA((n, 7)), pltpu.SemaphoreType.DMA((n, 7)),
                        pltpu.SemaphoreType.DMA((n,))],
        name=name,
    )(*blocks)


HBM_SPEC = pl.BlockSpec(memory_space=pltpu.HBM)
SEM_SPEC = pl.BlockSpec(memory_space=pltpu.SEMAPHORE)
SPLIT_COPY = pltpu.CompilerParams(has_side_effects=pltpu.SideEffectType.DATAFLOW_SIDE_EFFECTING)


def _in_hbm(t):
    return pltpu.with_memory_space_constraint(t, pltpu.HBM)


def _pair_copies(g_refs, land_refs, send_sems, recv_sems):
    x, y, c = _position()
    return [pltpu.make_async_remote_copy(
        src_ref=g.at[2 * k + (1 - c)], dst_ref=land.at[k], send_sem=send_sems.at[4 * a + k],
        recv_sem=recv_sems.at[4 * a + k], device_id=(x, y, 1 - c), device_id_type=MESH)
        for a, (g, land) in enumerate(zip(g_refs, land_refs, strict=True)) for k in range(4)]


def _chip_copies(t_refs, land_refs, send_sems, recv_sems):
    x, y, c = _position()
    chips = [(1 - x, y), (x, 1 - y), (1 - x, 1 - y)]
    return [pltpu.make_async_remote_copy(
        src_ref=t.at[2 * px + py], dst_ref=land.at[j], send_sem=send_sems.at[3 * a + j],
        recv_sem=recv_sems.at[3 * a + j], device_id=(px, py, c), device_id_type=MESH)
        for a, (t, land) in enumerate(zip(t_refs, land_refs, strict=True)) for j, (px, py) in enumerate(chips)]


_ROUNDS = {"pair": (_pair_copies, 4), "chip": (_chip_copies, 3)}


def _exchange_start(kind, ts, name):
    copies, slots = _ROUNDS[kind]
    n = len(ts)
    lands = [_in_hbm(lax.empty((slots,) + t.shape[1:], t.dtype)) for t in ts]

    def body(*refs):
        for cp in copies(refs[:n], refs[n:2 * n], refs[2 * n], refs[2 * n + 1]):
            cp.start()
        refs[-1][...] = jnp.zeros_like(refs[-1])

    sems = pltpu.SemaphoreType.DMA((slots * n,))
    res = pl.pallas_call(
        body, name=name, in_specs=[HBM_SPEC] * (2 * n),
        out_shape=(sems, sems, *[pltpu.HBM(t.shape, t.dtype) for t in (*ts, *lands)], SDS((8, LANES), f32)),
        out_specs=(SEM_SPEC, SEM_SPEC, *[HBM_SPEC] * (2 * n), pl.BlockSpec(memory_space=pltpu.VMEM)),
        input_output_aliases={i: 2 + i for i in range(2 * n)}, compiler_params=SPLIT_COPY,
    )(*[_in_hbm(t) for t in ts], *lands)
    return res[:-1], res[-1]


def _exchange_wait(kind, state, after, name):
    copies, _ = _ROUNDS[kind]
    send_sems, recv_sems, *arrays = state
    n = len(arrays) // 2

    def body(*refs):
        for cp in copies(refs[:n], refs[n:2 * n], refs[2 * n], refs[2 * n + 1]):
            cp.wait_send()
            cp.wait_recv()

    res = pl.pallas_call(
        body, name=name, in_specs=[HBM_SPEC] * (2 * n) + [SEM_SPEC, SEM_SPEC, pl.BlockSpec(memory_space=pl.ANY)],
        out_shape=[pltpu.HBM(t.shape, t.dtype) for t in arrays], out_specs=[HBM_SPEC] * (2 * n),
        input_output_aliases={i: i for i in range(2 * n)}, compiler_params=SPLIT_COPY,
    )(*arrays, send_sems, recv_sems, after)
    return res[:n], res[n:]


def _gather_copies(x_refs, out_refs, send_sems, recv_sems):
    x, y, c = _position()
    peers = [(x, y, 1 - c), (1 - x, y, c), (x, 1 - y, c), (1 - x, 1 - y, c)]
    sends, arrivals = [], []
    for a, (x_ref, out_ref) in enumerate(zip(x_refs, out_refs, strict=True)):
        for k, (px, py, pc) in enumerate(peers):
            sems = dict(send_sem=send_sems.at[4 * a + k], recv_sem=recv_sems.at[4 * a + k],
                        device_id=(px, py, pc), device_id_type=MESH)
            sends.append(pltpu.make_async_remote_copy(src_ref=x_ref, dst_ref=out_ref.at[4 * x + 2 * y + c], **sems))
            arrivals.append(pltpu.make_async_remote_copy(src_ref=x_ref, dst_ref=out_ref.at[4 * px + 2 * py + pc],
                                                         **sems))
    return sends, arrivals


def _gather_start(blocks, after, name):
    n = len(blocks)
    outs = [_in_hbm(lax.empty((N_DEV,) + b.shape, b.dtype)) for b in blocks]

    def body(*refs):
        sends, _ = _gather_copies(refs[:n], refs[n:2 * n], refs[2 * n + 1], refs[2 * n + 2])
        for cp in sends:
            cp.start()
        refs[-1][...] = jnp.zeros_like(refs[-1])

    sems = pltpu.SemaphoreType.DMA((4 * n,))
    res = pl.pallas_call(
        body, name=name, in_specs=[HBM_SPEC] * (2 * n) + [pl.BlockSpec(memory_space=pl.ANY)],
        out_shape=(sems, sems, *[pltpu.HBM(t.shape, t.dtype) for t in (*blocks, *outs)], SDS((8, LANES), f32)),
        out_specs=(SEM_SPEC, SEM_SPEC, *[HBM_SPEC] * (2 * n), pl.BlockSpec(memory_space=pltpu.VMEM)),
        input_output_aliases={i: 2 + i for i in range(2 * n)}, compiler_params=SPLIT_COPY,
    )(*[_in_hbm(b) for b in blocks], *outs, after)
    return res[:-1], res[-1]


def _gather_wait(state, after, name):
    send_sems, recv_sems, *arrays = state
    n = len(arrays) // 2

    def body(*refs):
        sends, arrivals = _gather_copies(refs[:n], refs[n:2 * n], refs[2 * n], refs[2 * n + 1])
        for cp in sends:
            cp.wait_send()
        for cp in arrivals:
            cp.wait_recv()

    res = pl.pallas_call(
        body, name=name, in_specs=[HBM_SPEC] * (2 * n) + [SEM_SPEC, SEM_SPEC, pl.BlockSpec(memory_space=pl.ANY)],
        out_shape=[pltpu.HBM(t.shape, t.dtype) for t in arrays], out_specs=[HBM_SPEC] * (2 * n),
        input_output_aliases={i: i for i in range(2 * n)}, compiler_params=SPLIT_COPY,
    )(*arrays, send_sems, recv_sems, after)
    return res[:n], res[n:]


def _gather_finish(partial, name):
    n = len(partial)

    def body(*refs):
        in_refs, out_refs = refs[:n], refs[n:2 * n]
        send_sems, recv_sems = refs[2 * n:]
        x, y, c = _position()
        chips = [(1 - x, y), (x, 1 - y), (1 - x, 1 - y)]
        copies = []
        for a in range(n):
            for j, (px, py) in enumerate(chips):
                cp = pltpu.make_async_remote_copy(
                    src_ref=in_refs[a].at[4 * px + 2 * py + c], dst_ref=out_refs[a].at[4 * px + 2 * py + c],
                    send_sem=send_sems.at[a, j], recv_sem=recv_sems.at[a, j], device_id=(x, y, 1 - c),
                    device_id_type=MESH)
                cp.start()
                copies.append(cp)
        for a in range(n):
            for j, (px, py) in enumerate(chips):
                pltpu.make_async_remote_copy(
                    src_ref=in_refs[a].at[4 * px + 2 * py + (1 - c)], dst_ref=out_refs[a].at[4 * px + 2 * py + (1 - c)],
                    send_sem=send_sems.at[a, j], recv_sem=recv_sems.at[a, j], device_id=(x, y, 1 - c),
                    device_id_type=MESH).wait_recv()
        for cp in copies:
            cp.wait_send()

    hbm = pl.BlockSpec(memory_space=pl.ANY)
    return pl.pallas_call(
        body, out_shape=[SDS(p.shape, p.dtype) for p in partial], in_specs=[hbm] * n, out_specs=[hbm] * n,
        input_output_aliases={a: a for a in range(n)},
        scratch_shapes=[pltpu.SemaphoreType.DMA((n, 3)), pltpu.SemaphoreType.DMA((n, 3))],
        name=name,
    )(*partial)


def _row_tile(rows):
    return 512 if rows % 512 == 0 and rows > 512 else rows


def _pair_add(g, r1, core, name):
    def body(c_ref, g_ref, r_ref, o_ref):
        o_ref[...] = (g_ref[...].astype(f32) + r_ref[...].astype(f32)).astype(o_ref.dtype)

    rows, cols = g.shape[1:]
    tile = _row_tile(rows)
    blk = (1, tile, cols)
    return pl.pallas_call(
        body, out_shape=SDS((4, rows, cols), g.dtype), name=name,
        grid_spec=pltpu.PrefetchScalarGridSpec(
            num_scalar_prefetch=1, grid=(4, rows // tile),
            in_specs=[pl.BlockSpec(blk, lambda k, i, c_ref: (2 * k + c_ref[0], i, 0)),
                      pl.BlockSpec(blk, lambda k, i, c_ref: (k, i, 0))],
            out_specs=pl.BlockSpec(blk, lambda k, i, c_ref: (k, i, 0))),
        compiler_params=_params(("parallel", "arbitrary")),
    )(core, g, r1)


def _chip_add(t, r2, chip, name):
    def body(c_ref, t_ref, r_ref, o_ref):
        o_ref[...] = ((t_ref[0].astype(f32) + r_ref[0].astype(f32)) + r_ref[1].astype(f32)) + r_ref[2].astype(f32)

    rows, cols = t.shape[1:]
    tile = _row_tile(rows)
    return pl.pallas_call(
        body, out_shape=SDS((rows, cols), f32), name=name,
        grid_spec=pltpu.PrefetchScalarGridSpec(
            num_scalar_prefetch=1, grid=(rows // tile,),
            in_specs=[pl.BlockSpec((1, tile, cols), lambda i, c_ref: (c_ref[0], i, 0)),
                      pl.BlockSpec((3, tile, cols), lambda i, c_ref: (0, i, 0))],
            out_specs=pl.BlockSpec((tile, cols), lambda i, c_ref: (i, 0))),
        compiler_params=_params(("arbitrary",)),
    )(chip, t, r2)


def _pad_to(t, axis, size):
    pads = [(0, 0)] * t.ndim
    pads[axis] = (0, size - t.shape[axis])
    return jnp.pad(t, pads)


_REF_COLS = {"qa": (0, FOX_W), "ka": (FOX_W, FOX_W), "va": (2 * FOX_W, FOX_W), "f": (3 * FOX_W, N_FOX_HEADS)}
_REF_COLS.update({n: (3 * FOX_W + N_FOX_HEADS + i * DIL_W, DIL_W) for i, n in enumerate(("qb", "kb", "vb"))})
_REF_COLS.update({n: (3 * FOX_W + N_FOX_HEADS + 3 * DIL_W + i * D, D) for i, n in enumerate(("ga", "gb"))})
_REF_ORDER = ("qa", "ka", "va", "f", "qb", "kb", "vb", "ga", "gb")


def _place_cols(sources, src_of, out_cols, name, row_block=512):
    arrays = [s[0] if isinstance(s, tuple) else s for s in sources]
    widths = [a.shape[-1] for a in arrays]
    rows = arrays[0].shape[-2]
    plan = []
    for t in range(out_cols // LANES):
        segs, c, end = [], t * LANES, (t + 1) * LANES
        while c < end:
            s = src_of(c)
            if s is None:
                c += 1
                continue
            n = 1
            while c + n < end and src_of(c + n) == (s[0], s[1] + n):
                n += 1
            segs.append((s[0], s[1], c - t * LANES, n))
            c += n
        plan.append(segs)

    def body(*refs):
        o_ref = refs[-1]
        for t, segs in enumerate(plan):
            acc = None
            for si, c0, o0, n in segs:
                a0 = c0 // LANES * LANES
                wide = min(2 * LANES, widths[si] - a0)
                win = refs[si][0, :, a0:a0 + wide] if isinstance(sources[si], tuple) else refs[si][:, a0:a0 + wide]
                r = lax.broadcasted_iota(jnp.int32, (wide, LANES), 0)
                c = lax.broadcasted_iota(jnp.int32, (wide, LANES), 1)
                pick = ((r - (c0 - a0) == c - o0) & (c >= o0) & (c < o0 + n)).astype(bf16)
                part = jnp.dot(win.astype(bf16), pick, preferred_element_type=f32)
                acc = part if acc is None else acc + part
            tile = jnp.zeros((row_block, LANES), f32) if acc is None else acc
            o_ref[:, t * LANES:(t + 1) * LANES] = tile.astype(o_ref.dtype)

    def spec(s):
        if isinstance(s, tuple):
            j = s[1]
            return pl.BlockSpec((1, row_block, s[0].shape[-1]), lambda i: (j, i, 0))
        return pl.BlockSpec((row_block, s.shape[-1]), lambda i: (i, 0))

    return pl.pallas_call(
        body, grid=(rows // row_block,), in_specs=[spec(s) for s in sources],
        out_specs=pl.BlockSpec((row_block, out_cols), lambda i: (i, 0)), out_shape=SDS((rows, out_cols), bf16),
        name=name, compiler_params=_params(("parallel",)),
    )(*arrays)


def _ref_piece(r):
    for name in _REF_ORDER:
        lo, width = _REF_COLS[name]
        if lo <= r < lo + width:
            return name, r - lo
    raise ValueError(r)


def _shard_pad_cols(pieces):
    names = [n for n in _REF_ORDER if n != "vb"]
    sources = [pieces[n] for n in names] + list(pieces["vb"])

    def src_of(c):
        j, i = divmod(c, W_IN_PAD)
        if i >= W_IN_SH:
            return None
        name, col = _ref_piece(j * W_IN_SH + i)
        if name == "vb":
            return len(names) + col // DIL_OUT_W, col % DIL_OUT_W
        return names.index(name), col

    return _place_cols(sources, src_of, N_DEV * W_IN_PAD, "place_dproj")


_SLABS = {"ga": C_GA, "gb": C_GB, "qb": C_QB, "kb": C_KB, "vb": C_VB, "qa": C_QA, "ka": C_KA, "va": C_VA, "f": C_F}


def _slab_w_in(stack):
    def src_of(c):
        for name, start in _SLABS.items():
            lo, width = _REF_COLS[name]
            if start <= c < start + width:
                return divmod(lo + c - start, W_IN_SH)
        return None

    return _place_cols([(stack, j) for j in range(N_DEV)], src_of, PROJ_W, "place_w_in")


def kernel(x, c, w_ada, b_ada, g_mix, w_in, b_fgate, w_br_a, w_br_b, w_out, g_ffn, w_ffn_gate, w_ffn_up, w_ffn_down, g_final, loss_target, m_w_ada, m_b_ada, m_g_mix, m_w_in, m_b_fgate, m_w_br_a, m_w_br_b, m_w_out, m_g_ffn, m_w_ffn_gate, m_w_ffn_up, m_w_ffn_down, m_g_final, v_w_ada, v_b_ada, v_g_mix, v_w_in, v_b_fgate, v_w_br_a, v_w_br_b, v_w_out, v_g_ffn, v_w_ffn_gate, v_w_ffn_up, v_w_ffn_down, v_g_final):
    px, py, pc = _position()
    dev = 4 * px + 2 * py + pc
    x2d, tgt = x[0], loss_target[0]

    c_all = _all_gather(c, "gather_c").reshape(N_DEV, D)
    ada_cols = w_ada.shape[2]
    b_shard = lax.dynamic_slice(b_ada, (0, dev * ada_cols), (1, ada_cols))
    mod_shard = _ada_fwd(c_all, w_ada[0], b_shard)
    mod_all = _all_gather(mod_shard, "gather_mod")
    modv = lax.dynamic_index_in_dim(mod_all, dev, axis=1, keepdims=False).reshape(6, D)

    gate_up = jnp.concatenate([_pad_to(w_ffn_gate[0], 1, FF_PAD), _pad_to(w_ffn_up[0], 1, FF_PAD)], axis=1)
    w_in_s, = _all_gather_many([_pad_to(w_in[0], 1, W_IN_PAD).astype(bf16)], "gather_w_in")
    later = [w_br_a[0], w_br_b[0], w_out[0], gate_up, _pad_to(w_ffn_down[0], 0, FF_PAD)]
    later_state, later_token = _gather_start([t.astype(bf16) for t in later], w_in_s, "gather_rest_start")
    w_in_p = _slab_w_in(w_in_s)

    h1 = _pre1(x2d, modv, g_mix)
    proj = _matmul(h1, w_in_p, name="mm_proj", tm=SEQ, tn=896, tk=D, after=later_token)
    b_pad = jnp.pad(b_fgate, ((0, 0), (0, LANES - N_FOX_HEADS)))
    q_aug, k_aug, va = _fox_prep(proj, _fox_gate_fwd(proj, b_pad))
    ya_h, lse_a = _fox_fwd(q_aug, k_aug, va)

    tables = _rope_tables()
    qb_r, kb_r, vb = _rope_fwd(proj, tables)
    by_group = [_dil_fwd(qb_r, kb_r, vb, grp) for grp in range(N_GROUPS)]
    yb_h, lse_b = _dil_combine([o for o, _ in by_group], [l for _, l in by_group])

    mine, arrived = _gather_wait(later_state, yb_h, "gather_rest_wait")
    w_a_s, w_b_s, w_o_s, w_gu_s, w_d_s = [
        lax.dynamic_update_slice(stack, block[None], (dev, 0, 0))
        for stack, block in zip(_gather_finish(arrived, "gather_rest_finish"), mine, strict=True)]
    w_o = w_o_s.reshape(D, D)
    w_d = w_d_s.reshape(FF_HID, D)
    ya = _matmul_stack(ya_h, w_a_s, name="mm_br_a")
    yb = _matmul_stack(yb_h, w_b_s, name="mm_br_b")

    merged = _merge_fwd(ya, yb, proj)
    mix = _matmul(merged, w_o, name="mm_out", tm=SEQ, tn=512, tk=D)
    x1, h2 = _post1(x2d, mix, modv, g_ffn)
    act, au = _ffn_in(h2, w_gu_s)
    ff = _matmul(act, w_d, name="mm_ffn_down", tm=SEQ // 2, tn=512, tk=FF_HID)

    dx2, dff, dg_final, dga_f, loss_lanes = _final(x1, ff, tgt, modv, g_final.reshape(1, D))
    dau = _ffn_bwd_in(dff, w_d_s, au)

    core = pc.astype(jnp.int32).reshape(1)
    chip = (2 * px + py).astype(jnp.int32).reshape(1)

    def pair_done(state, after, tags, name):
        mine, theirs = _exchange_wait("pair", state, after, "pair_wait_" + name)
        sums = [_pair_add(g, r, core, "pair_add_" + t) for g, r, t in zip(mine, theirs, tags)]
        return _exchange_start("chip", sums, "chip_start_" + name)

    def from_chips(state, after, tags, name):
        sums, got = _exchange_wait("chip", state, after, "chip_wait_" + name)
        return [_chip_add(p, r, chip, "chip_add_" + t) for p, r, t in zip(sums, got, tags)]

    g_gu = _matmul(h2, dau, ta=True, by_shard=True, out_dtype=bf16, name="mm_g_ffn_in", tm=D, tn=2 * FF_PAD, tk=SEQ)
    g_d = _matmul(act, dff, ta=True, out_dtype=bf16, name="mm_g_down", tm=FF_HID // 2, tn=512, tk=SEQ)
    ffn_tags = ["gu", "down"]
    ffn_pair, ffn_pair_token = _exchange_start("pair", [g_gu, g_d.reshape(N_DEV, FF_PAD, D)], "pair_start_ffn")

    dh2 = _matmul_nt_shards(dau, w_gu_s, ffn_pair_token, name="mm_d_h2")
    ffn_state, ffn_token = pair_done(ffn_pair, dh2, ffn_tags, "ffn")
    dx1, dmix, dsh_f, dsc_f, dg_ffn, dga_m = _mid_bwd(dh2, x1, dx2, mix, modv, g_ffn)
    dmerged = _matmul(dmix, w_o, tb=True, name="mm_d_merged", tm=SEQ, tn=512, tk=D, after=ffn_token)
    dya, dyb, dga, dgb = _merge_bwd(dmerged, ya, yb, proj)
    dya_h = _matmul_stack(dya, w_a_s, tb=True, name="mm_d_ya")
    dyb_h = _matmul_stack(dyb, w_b_s, tb=True, name="mm_d_yb")

    g_o = _matmul(merged, dmix, ta=True, out_dtype=bf16, name="mm_g_out", tm=D, tn=512, tk=SEQ)
    g_a = _matmul_stack(ya_h, dya, ta=True, out_dtype=bf16, name="mm_g_br_a")
    g_b = _matmul_stack(yb_h, dyb, ta=True, out_dtype=bf16, name="mm_g_br_b")
    rows_a, rows_b = FOX_W * W_BR_SH // D, DIL_OUT_W * W_BR_SH // D
    g_small = jnp.concatenate([g_a.reshape(N_DEV, rows_a, D), g_b.reshape(N_DEV, rows_b, D),
                               g_o.reshape(N_DEV, W_BR_SH, D)], axis=1)
    small_pair, small_pair_token = _exchange_start("pair", [g_small], "pair_start_small")

    dqa, dka, dva, dF = _fox_bwd(q_aug, k_aug, va, dya_h, ya_h, lse_a, small_pair_token)
    dF_row = jnp.pad(dF[:, :2, :].reshape(N_FOX_HEADS, SEQ), ((0, LANES - N_FOX_HEADS), (0, 0)))
    df, db_fgate = _fox_gate_bwd(dF_row, proj, b_pad)
    small_state, small_token = pair_done(small_pair, df, ["small"], "small")

    delta_b = _dil_delta(dyb_h, yb_h)
    dil_grads = [_dil_bwd(qb_r, kb_r, vb, dyb_h, lse_b, delta_b, grp) for grp in range(N_GROUPS)]
    dqb, dkb = _rope_bwd([t[0] for t in dil_grads], [t[1] for t in dil_grads], tables)

    dproj = _shard_pad_cols({"qa": dqa, "ka": dka, "va": dva, "f": df, "qb": dqb, "kb": dkb,
                             "vb": [t[2] for t in dil_grads], "ga": dga, "gb": dgb})
    g_in = _matmul(h1, dproj, ta=True, by_shard=True, out_dtype=bf16, name="mm_g_in", tm=D, tn=W_IN_PAD, tk=SEQ,
                   after=small_token)
    mix_tags = ["in"]
    mix_pair, mix_pair_token = _exchange_start("pair", [g_in], "pair_start_mixer")

    dh1 = _matmul_nt_shards(dproj, w_in_s, mix_pair_token, name="mm_d_h1")
    grad_x, dsh_m, dsc_m, dg_mix = _first_bwd(dh1, x2d, dx1, modv, g_mix)

    pad_lane = lambda t: jnp.pad(t, ((0, 0), (0, D - t.shape[1])))
    small = jnp.concatenate([dsh_m, dsc_m, dga_m, dsh_f, dsc_f, dga_f, dg_mix, dg_ffn, dg_final,
                             pad_lane(db_fgate), loss_lanes, jnp.zeros((SMALL_ROWS - 11, D), f32)], axis=0)
    small_all = _all_gather(small, "gather_small")
    mix_state, mix_token = pair_done(mix_pair, small_all, mix_tags, "mixer")

    small_sum, loss_row = _small_reduce(small_all, mix_token)
    dmod_all = small_all[:, :6, :].reshape(N_DEV, 6 * D)
    g_w_ada = _ada_bwd(c_all, lax.dynamic_slice(dmod_all, (0, dev * ada_cols), (N_DEV, ada_cols)))
    s_gu, s_d = from_chips(ffn_state, small_sum, ffn_tags, "ffn")
    s_small, = from_chips(small_state, small_sum, ["small"], "small")

    loss = loss_row[0, 0]
    g = {
        "w_ada": g_w_ada[None], "b_ada": small_sum[0:6].reshape(1, 6 * D), "g_mix": small_sum[6:7],
        "b_fgate": small_sum[9:10, :N_FOX_HEADS], "g_ffn": small_sum[7:8], "w_ffn_gate": s_gu[None, :, :W_FF_SH],
        "w_ffn_up": s_gu[None, :, FF_PAD:FF_PAD + W_FF_SH], "w_ffn_down": s_d[None, :W_FF_SH],
        "g_final": small_sum[8], "w_br_a": s_small[:rows_a].reshape(1, FOX_W, W_BR_SH),
        "w_br_b": s_small[rows_a:rows_a + rows_b].reshape(1, DIL_OUT_W, W_BR_SH), "w_out": s_small[None, rows_a + rows_b:],
    }
    w = {"w_ada": w_ada, "b_ada": b_ada, "g_mix": g_mix, "w_in": w_in, "b_fgate": b_fgate, "w_br_a": w_br_a,
         "w_br_b": w_br_b, "w_out": w_out, "g_ffn": g_ffn, "w_ffn_gate": w_ffn_gate, "w_ffn_up": w_ffn_up,
         "w_ffn_down": w_ffn_down, "g_final": g_final}
    m = {"w_ada": m_w_ada, "b_ada": m_b_ada, "g_mix": m_g_mix, "w_in": m_w_in, "b_fgate": m_b_fgate,
         "w_br_a": m_w_br_a, "w_br_b": m_w_br_b, "w_out": m_w_out, "g_ffn": m_g_ffn, "w_ffn_gate": m_w_ffn_gate,
         "w_ffn_up": m_w_ffn_up, "w_ffn_down": m_w_ffn_down, "g_final": m_g_final}
    v = {"w_ada": v_w_ada, "b_ada": v_b_ada, "g_mix": v_g_mix, "w_in": v_w_in, "b_fgate": v_b_fgate,
         "w_br_a": v_w_br_a, "w_br_b": v_w_br_b, "w_out": v_w_out, "g_ffn": v_g_ffn, "w_ffn_gate": v_w_ffn_gate,
         "w_ffn_up": v_w_ffn_up, "w_ffn_down": v_w_ffn_down, "g_final": v_g_final}
    names = list(w)
    delta, new_m, new_v = {}, {}, {}

    transposed = ("w_in", "w_ffn_gate", "w_ffn_up")

    def update(n):
        shape = w[n].shape
        if n in transposed:
            g_t = g[n][0].T
            dl, mn, vn = _adamw(w[n][0].T, g_t, m[n][0].T, v[n][0].T, "adamw_" + n)
            g[n], delta[n], new_m[n], new_v[n] = g_t.T[None], dl.T[None], mn.T[None], vn.T[None]
            return
        two_d = (lambda t: t.reshape(shape[-2:])) if len(shape) == 3 else (lambda t: t)
        dl, mn, vn = _adamw(two_d(w[n]), two_d(g[n]), two_d(m[n]), two_d(v[n]), "adamw_" + n)
        delta[n], new_m[n], new_v[n] = dl.reshape(shape), mn.reshape(shape), vn.reshape(shape)

    for n in list(g):
        update(n)
    done = sum(delta[n].reshape(-1)[:N_FOX_HEADS] for n in g)
    s_in, = from_chips(mix_state, done, mix_tags, "mixer")
    g["w_in"] = s_in[None, :, :W_IN_SH]
    update("w_in")

    return (loss, grad_x[None], *[g[n] for n in names], *[delta[n] for n in names],
            *[new_m[n] for n in names], *[new_v[n] for n in names])
```

```python
import functools

import jax
import jax.numpy as jnp
from jax import lax
from jax.experimental import pallas as pl
from jax.experimental.pallas import tpu as pltpu

f32 = jnp.float32
bf16 = jnp.bfloat16
SDS = jax.ShapeDtypeStruct
MESH = pl.DeviceIdType.MESH

N_DEV = 8
D = 1024
SEQ = 2048
HEAD_DIM = 64
N_FOX_HEADS = 8
FOX_W = 512
DIL_W = 768
DIL_OUT_W = 256
ROT_DIM = 16
ROPE_THETA = 500000.0
D_FF = 2816
IN_COLS = 5896
EPS = 1e-6
NEG = -1e30
ATT_SCALE = HEAD_DIM ** -0.5

ADAM_LR = 0.001
ADAM_B1 = 0.9
ADAM_B2 = 0.999
ADAM_EPS = 1e-08
ADAM_WD = 0.01
ADAM_STEP = 10

C_GA, C_GB, C_QB, C_KB, C_VB, C_QA, C_KA, C_VA, C_F = 0, 1024, 2304, 3072, 3840, 4608, 5120, 5632, 6144
PROJ_W = 6272
LANES = 128
VMEM_LIMIT = 52 * 1024 * 1024

W_IN_SH, W_IN_PAD = IN_COLS // N_DEV, 768
W_BR_SH = D // N_DEV
W_FF_SH, FF_PAD = D_FF // N_DEV, 384
FF_HID = N_DEV * FF_PAD
SMALL_ROWS = 16


def _params(sem=None):
    if sem is None:
        return pltpu.CompilerParams(vmem_limit_bytes=VMEM_LIMIT)
    return pltpu.CompilerParams(dimension_semantics=sem, vmem_limit_bytes=VMEM_LIMIT)


def _rowwise(fn, name, tiled, vecs, outs, reds=(), tile=256):
    nt, nv, no = len(tiled), len(vecs), len(outs)
    rows = tiled[0][0].shape[0]
    assert rows % tile == 0

    def body(*refs):
        tin = [r[...] for r in refs[:nt]]
        vin = [r[...] for r in refs[nt:nt + nv]]
        orefs = refs[nt + nv:nt + nv + no]
        rrefs = refs[nt + nv + no:]
        touts, routs = fn(tin, vin)
        for r, t in zip(orefs, touts, strict=True):
            r[...] = t.astype(r.dtype)
        if rrefs:
            @pl.when(pl.program_id(0) == 0)
            def _():
                for r in rrefs:
                    r[...] = jnp.zeros_like(r)
            for r, t in zip(rrefs, routs, strict=True):
                r[...] += t

    def col_map(cb):
        return lambda i: (i, cb)

    def whole_map(nd):
        return lambda i: (0,) * nd

    in_specs = [pl.BlockSpec((tile, w), col_map(cb)) for (_, w, cb) in tiled]
    in_specs += [pl.BlockSpec(v.shape, whole_map(v.ndim)) for v in vecs]
    out_specs = [pl.BlockSpec((tile, w), lambda i: (i, 0)) for (w, _) in outs]
    out_specs += [pl.BlockSpec((1, w), lambda i: (0, 0)) for w in reds]
    out_shape = [SDS((rows, w), dt) for (w, dt) in outs] + [SDS((1, w), f32) for w in reds]
    res = pl.pallas_call(
        body, grid=(rows // tile,), in_specs=in_specs, out_specs=out_specs, out_shape=out_shape, name=name,
        compiler_params=_params(("arbitrary",)),
    )(*[t[0] for t in tiled], *vecs)
    return res


def _matmul(a, b, *, ta=False, tb=False, out_dtype=f32, name, tm, tn, tk, by_shard=False, after=None):
    m, k = (a.shape[1], a.shape[0]) if ta else a.shape
    if by_shard and not ta:
        n, kb = (b.shape[1], N_DEV * b.shape[2]) if tb else (N_DEV * b.shape[2], b.shape[1])
        assert (tk if tb else tn) == b.shape[2]
    else:
        n, kb = (b.shape[0], b.shape[1]) if tb else (b.shape[1], b.shape[0])
    assert kb == k and m % tm == 0 and n % tn == 0 and k % tk == 0
    nk = k // tk
    dims = (((0 if ta else 1,), (1 if tb else 0,)), ((), ()))
    b_stacked = by_shard and not ta
    o_stacked = by_shard and ta

    def body(a_ref, b_ref, *rest):
        o_ref, *acc = rest[1:] if after is not None else rest
        bv = b_ref[0] if b_stacked else b_ref[...]
        p = lax.dot_general(a_ref[...].astype(bf16), bv.astype(bf16), dims, preferred_element_type=f32)

        def put(val):
            if o_stacked:
                o_ref[0] = val.astype(o_ref.dtype)
            else:
                o_ref[...] = val.astype(o_ref.dtype)

        if nk == 1:
            put(p)
        else:
            acc_ref, = acc
            kk = pl.program_id(2)

            @pl.when(kk == 0)
            def _():
                acc_ref[...] = p

            @pl.when(kk > 0)
            def _():
                acc_ref[...] += p

            @pl.when(kk == nk - 1)
            def _():
                put(acc_ref[...])

    a_spec = pl.BlockSpec((tk, tm), lambda i, j, kk: (kk, i)) if ta else pl.BlockSpec((tm, tk), lambda i, j, kk: (i, kk))
    if b_stacked and tb:
        b_spec = pl.BlockSpec((1, tn, tk), lambda i, j, kk: (kk, j, 0))
    elif b_stacked:
        b_spec = pl.BlockSpec((1, tk, tn), lambda i, j, kk: (j, kk, 0))
    elif tb:
        b_spec = pl.BlockSpec((tn, tk), lambda i, j, kk: (j, kk))
    else:
        b_spec = pl.BlockSpec((tk, tn), lambda i, j, kk: (kk, j))
    if o_stacked:
        assert tn == n // N_DEV
        out_spec = pl.BlockSpec((1, tm, tn), lambda i, j, kk: (j, i, 0))
        out_shape = SDS((N_DEV, m, tn), out_dtype)
    else:
        out_spec = pl.BlockSpec((tm, tn), lambda i, j, kk: (i, j))
        out_shape = SDS((m, n), out_dtype)
    extra_specs, extra = ([pl.BlockSpec(memory_space=pl.ANY)], [after]) if after is not None else ([], [])
    return pl.pallas_call(
        body, grid=(m // tm, n // tn, nk), in_specs=[a_spec, b_spec] + extra_specs, out_specs=out_spec,
        out_shape=out_shape, name=name, scratch_shapes=[pltpu.VMEM((tm, tn), f32)] if nk > 1 else [],
        compiler_params=_params(("parallel", "parallel", "arbitrary")),
    )(a, b, *extra)


def _matmul_stack(a, b, *, ta=False, tb=False, out_dtype=f32, name):
    def lanes(ref):
        return jnp.concatenate([ref[j] for j in range(N_DEV)], axis=1).astype(bf16)

    if ta:
        w = b.shape[1] // N_DEV

        def body(a_ref, b_ref, o_ref):
            p = _tn(a_ref[...].astype(bf16), b_ref[...].astype(bf16))
            for j in range(N_DEV):
                o_ref[j] = p[:, j * w:(j + 1) * w].astype(o_ref.dtype)

        return pl.pallas_call(body, out_shape=SDS((N_DEV, a.shape[1], w), out_dtype), name=name,
                              compiler_params=_params())(a, b)

    m, half = a.shape[0], a.shape[0] // 2
    n = b.shape[1] if tb else N_DEV * b.shape[2]

    def body(a_ref, b_ref, o_ref):
        av = a_ref[...].astype(bf16)
        o_ref[...] = (_nt(av, lanes(b_ref)) if tb else jnp.dot(av, lanes(b_ref), preferred_element_type=f32)
                      ).astype(o_ref.dtype)

    return pl.pallas_call(
        body, grid=(2,), in_specs=[pl.BlockSpec((half, a.shape[1]), lambda i: (i, 0)),
                                   pl.BlockSpec(b.shape, lambda i: (0, 0, 0))],
        out_specs=pl.BlockSpec((half, n), lambda i: (i, 0)), out_shape=SDS((m, n), out_dtype), name=name,
        compiler_params=_params(("parallel",)),
    )(a, b)


def _matmul_nt_shards(a, b, after, *, name, tm=512, tn=1024):
    m, n, w = a.shape[0], b.shape[1], b.shape[2]
    assert a.shape[1] == N_DEV * w and m % tm == 0 and n % tn == 0

    def body(a_ref, b_ref, after_ref, o_ref):
        acc = _nt(a_ref[:, 0:w], b_ref[0])
        for j in range(1, N_DEV):
            acc = acc + _nt(a_ref[:, j * w:(j + 1) * w], b_ref[j])
        o_ref[...] = acc

    return pl.pallas_call(
        body, grid=(n // tn, m // tm),
        in_specs=[pl.BlockSpec((tm, N_DEV * w), lambda j, i: (i, 0)), pl.BlockSpec((N_DEV, tn, w), lambda j, i: (0, j, 0)),
                  pl.BlockSpec(memory_space=pl.ANY)],
        out_specs=pl.BlockSpec((tm, tn), lambda j, i: (i, j)), out_shape=SDS((m, n), f32), name=name,
        compiler_params=_params(("parallel", "parallel")),
    )(a, b, after)


def _rms(x):
    r = lax.rsqrt(jnp.mean(x * x, axis=-1, keepdims=True) + EPS)
    return r, x * r


def _rms_bwd(r, xn, dxn):
    return r * (dxn - xn * jnp.mean(dxn * xn, axis=-1, keepdims=True))


def _colsum(t):
    return jnp.sum(t, axis=0, keepdims=True)


def _sigmoid(x):
    return 1.0 / (1.0 + jnp.exp(-x))


def _modulated_norm(x, g, shift, scale):
    _, xn = _rms(x)
    return (xn * g) * (1.0 + scale) + shift


def _pre1(x, modv, g_mix):
    def fn(t, v):
        (xt,), (mv, g) = t, v
        return [_modulated_norm(xt, g, mv[0:1], mv[1:2])], []
    return _rowwise(fn, "pre1", [(x, D, 0)], [modv, g_mix], [(D, bf16)])[0]


def _post1(x, mix, modv, g_ffn):
    def fn(t, v):
        (xt, mt), (mv, g) = t, v
        x1 = xt + mv[2:3] * mt
        return [x1, _modulated_norm(x1, g, mv[3:4], mv[4:5])], []
    return _rowwise(fn, "post1", [(x, D, 0), (mix, D, 0)], [modv, g_ffn], [(D, f32), (D, bf16)])


def _ffn_in(h, w_stack):
    def body(h_ref, w_ref, act_ref, au_ref):
        p = jnp.dot(h_ref[...], w_ref[0], preferred_element_type=f32)
        a, u = p[:, :FF_PAD], p[:, FF_PAD:]
        act_ref[...] = (a * _sigmoid(a) * u).astype(act_ref.dtype)
        au_ref[...] = p.astype(au_ref.dtype)

    return pl.pallas_call(
        body, grid=(N_DEV,),
        in_specs=[pl.BlockSpec((SEQ, D), lambda j: (0, 0)), pl.BlockSpec((1, D, 2 * FF_PAD), lambda j: (j, 0, 0))],
        out_specs=[pl.BlockSpec((SEQ, FF_PAD), lambda j: (0, j)), pl.BlockSpec((SEQ, 2 * FF_PAD), lambda j: (0, j))],
        out_shape=(SDS((SEQ, FF_HID), bf16), SDS((SEQ, 2 * FF_HID), bf16)), name="ffn_in",
        compiler_params=_params(("parallel",)),
    )(h, w_stack)


def _ffn_bwd_in(dff, w_down_stack, au):
    def body(d_ref, w_ref, au_ref, o_ref):
        dact = _nt(d_ref[...], w_ref[0])
        p = au_ref[...].astype(f32)
        a, u = p[:, :FF_PAD], p[:, FF_PAD:]
        sg = _sigmoid(a)
        o_ref[...] = jnp.concatenate([dact * u * (sg * (1.0 + a * (1.0 - sg))), dact * (a * sg)],
                                     axis=1).astype(o_ref.dtype)

    return pl.pallas_call(
        body, grid=(N_DEV,),
        in_specs=[pl.BlockSpec((SEQ, D), lambda j: (0, 0)), pl.BlockSpec((1, FF_PAD, D), lambda j: (j, 0, 0)),
                  pl.BlockSpec((SEQ, 2 * FF_PAD), lambda j: (0, j))],
        out_specs=pl.BlockSpec((SEQ, 2 * FF_PAD), lambda j: (0, j)),
        out_shape=SDS((SEQ, 2 * FF_HID), bf16), name="ffn_bwd_in", compiler_params=_params(("parallel",)),
    )(dff, w_down_stack, au)


def _final(x1, ff, target, modv, g_final):
    def fn(t, v):
        (x1t, fft, tgt), (mv, g) = t, v
        x2 = x1t + mv[5:6] * fft
        r, xn = _rms(x2)
        err = xn * g - tgt
        dy = err * (1.0 / D)
        dx2 = _rms_bwd(r, xn, dy * g)
        return [dx2, dx2 * mv[5:6]], [_colsum(dy * xn), _colsum(dx2 * fft), _colsum(err * err) * (0.5 / D)]
    return _rowwise(fn, "final", [(x1, D, 0), (ff, D, 0), (target, D, 0)], [modv, g_final],
                    [(D, f32), (D, bf16)], [D, D, D])


def _mid_bwd(dh2, x1, dx2, mix, modv, g_ffn):
    def fn(t, v):
        (dh, x1t, dx2t, mt), (mv, g) = t, v
        r, xn = _rms(x1t)
        dn = dh * (1.0 + mv[4:5])
        dx1 = dx2t + _rms_bwd(r, xn, dn * g)
        return [dx1, dx1 * mv[2:3]], [_colsum(dh), _colsum(dh * (xn * g)), _colsum(dn * xn), _colsum(dx1 * mt)]
    return _rowwise(fn, "mid_bwd", [(dh2, D, 0), (x1, D, 0), (dx2, D, 0), (mix, D, 0)], [modv, g_ffn],
                    [(D, f32), (D, bf16)], [D, D, D, D])


def _first_bwd(dh1, x, dx1, modv, g_mix):
    def fn(t, v):
        (dh, xt, dx1t), (mv, g) = t, v
        r, xn = _rms(xt)
        dn = dh * (1.0 + mv[1:2])
        return [dx1t + _rms_bwd(r, xn, dn * g)], [_colsum(dh), _colsum(dh * (xn * g)), _colsum(dn * xn)]
    return _rowwise(fn, "first_bwd", [(dh1, D, 0), (x, D, 0), (dx1, D, 0)], [modv, g_mix], [(D, f32)], [D, D, D])


def _merge_fwd(ya, yb, proj):
    def fn(t, v):
        ya_t, yb_t, ga, gb = t
        return [_sigmoid(ga) * ya_t + _sigmoid(gb) * yb_t], []
    return _rowwise(fn, "merge_fwd", [(ya, D, 0), (yb, D, 0), (proj, D, C_GA // D), (proj, D, C_GB // D)], [],
                    [(D, bf16)])[0]


def _merge_bwd(dmerged, ya, yb, proj):
    def fn(t, v):
        dm, ya_t, yb_t, ga, gb = t
        sa, sb = _sigmoid(ga), _sigmoid(gb)
        return [dm * sa, dm * sb, dm * ya_t * (sa * (1.0 - sa)), dm * yb_t * (sb * (1.0 - sb))], []
    return _rowwise(fn, "merge_bwd",
                    [(dmerged, D, 0), (ya, D, 0), (yb, D, 0), (proj, D, C_GA // D), (proj, D, C_GB // D)], [],
                    [(D, bf16), (D, bf16), (D, bf16), (D, bf16)])


def _rope_tables():
    half = ROT_DIM // 2
    pos = jnp.arange(SEQ, dtype=f32)
    inv_freq = ROPE_THETA ** (-jnp.arange(0, ROT_DIM, 2, dtype=f32) / ROT_DIM)
    ang = pos[:, None] * inv_freq[None, :]
    cos, sin = jnp.cos(ang), jnp.sin(ang)
    pad = jnp.zeros((SEQ, HEAD_DIM - ROT_DIM), f32)
    zero = jnp.zeros((SEQ, half), f32)
    c_head = jnp.concatenate([cos, cos, pad + 1.0], axis=1)
    lo_head = jnp.concatenate([-sin, zero, pad], axis=1)
    hi_head = jnp.concatenate([zero, sin, pad], axis=1)
    return tuple(jnp.concatenate([t, t], axis=1) for t in (c_head, lo_head, hi_head))


def _over_heads(tables):
    return [jnp.tile(t, (1, DIL_W // LANES)) for t in tables]


def _rope_fwd(proj, tables):
    half = ROT_DIM // 2

    def fn(t, v):
        q, k, vv = t[:3]
        c, lo, hi = _over_heads(t[3:])
        rot = lambda z: z * c + pltpu.roll(z, DIL_W - half, 1) * lo + pltpu.roll(z, half, 1) * hi
        return [rot(q) * ATT_SCALE, rot(k), vv], []
    return _rowwise(fn, "rope_fwd", [(proj, DIL_W, C_QB // DIL_W), (proj, DIL_W, C_KB // DIL_W),
                                     (proj, DIL_W, C_VB // DIL_W)] + [(tb, LANES, 0) for tb in tables], [],
                    [(DIL_W, f32)] * 3)


def _rope_bwd(dqs, dks, tables):
    half = ROT_DIM // 2

    def fn(t, v):
        dq_t, dk_t = jnp.concatenate(t[:N_GROUPS], axis=1), jnp.concatenate(t[N_GROUPS:2 * N_GROUPS], axis=1)
        c, lo, hi = _over_heads(t[2 * N_GROUPS:])
        rot_t = lambda z: z * c + pltpu.roll(z * lo, half, 1) + pltpu.roll(z * hi, DIL_W - half, 1)
        return [rot_t(dq_t), rot_t(dk_t)], []
    return _rowwise(fn, "rope_bwd", [(a, DIL_OUT_W, 0) for a in (*dqs, *dks)] + [(tb, LANES, 0) for tb in tables],
                    [], [(DIL_W, bf16), (DIL_W, bf16)])


def _head_bcast_sum(d):
    lane = lax.broadcasted_iota(jnp.int32, d.shape, 1)
    out = jnp.zeros_like(d)
    for h in range(d.shape[1] // HEAD_DIM):
        sel = (lane >= h * HEAD_DIM) & (lane < (h + 1) * HEAD_DIM)
        out = jnp.where(sel, jnp.sum(jnp.where(sel, d, 0.0), axis=1, keepdims=True), out)
    return out


def _dil_combine(outs, lses):
    def fn(t, v):
        o0, o1, o2, l0, l1, l2 = t
        m = jnp.maximum(jnp.maximum(l0, l1), l2)
        w0, w1, w2 = jnp.exp(l0 - m), jnp.exp(l1 - m), jnp.exp(l2 - m)
        tot = w0 + w1 + w2
        return [(w0 * o0 + w1 * o1 + w2 * o2) / tot, m + jnp.log(tot)], []
    w = DIL_OUT_W
    return _rowwise(fn, "dil_combine", [(t, w, 0) for t in (*outs, *lses)], [], [(w, f32), (w, f32)])


def _dil_delta(dyb_h, yb_h):
    def fn(t, v):
        return [_head_bcast_sum(t[0] * t[1])], []
    return _rowwise(fn, "dil_delta", [(dyb_h, DIL_OUT_W, 0), (yb_h, DIL_OUT_W, 0)], [], [(DIL_OUT_W, f32)])[0]


def _adamw_math(wt, gt, mt, vt):
    mn = ADAM_B1 * mt + (1.0 - ADAM_B1) * gt
    vn = ADAM_B2 * vt + (1.0 - ADAM_B2) * (gt * gt)
    m_hat = mn / (1.0 - ADAM_B1 ** ADAM_STEP)
    v_hat = vn / (1.0 - ADAM_B2 ** ADAM_STEP)
    return -ADAM_LR * (m_hat / (jnp.sqrt(v_hat) + ADAM_EPS) + ADAM_WD * wt), mn, vn


def _adamw(w, g, m, v, name):
    shape = w.shape
    if w.ndim == 1:
        w, g, m, v = (t.reshape(1, -1) for t in (w, g, m, v))
    rows, cols = w.shape
    if rows % 8 and rows > 8:
        return _adamw_by_cols(w, g, m, v, name)
    tile = 256 if rows % 256 == 0 and rows > 512 else rows

    def fn(t, _):
        return list(_adamw_math(*t)), []
    delta, mn, vn = _rowwise(fn, name, [(w, cols, 0), (g, cols, 0), (m, cols, 0), (v, cols, 0)], [],
                             [(cols, f32)] * 3, tile=tile)
    return delta.reshape(shape), mn.reshape(shape), vn.reshape(shape)


def _adamw_by_cols(w, g, m, v, name, tile=256):
    rows, cols = w.shape

    def body(w_ref, g_ref, m_ref, v_ref, d_ref, mn_ref, vn_ref):
        d_ref[...], mn_ref[...], vn_ref[...] = _adamw_math(w_ref[...], g_ref[...], m_ref[...], v_ref[...])

    spec = pl.BlockSpec((rows, tile), lambda j: (0, j))
    return pl.pallas_call(body, grid=(cols // tile,), in_specs=[spec] * 4, out_specs=[spec] * 3,
                          out_shape=[SDS((rows, cols), f32)] * 3, name=name,
                          compiler_params=_params(("parallel",)))(w, g, m, v)


def _ada_fwd(c_all, w_shard, b_shard):
    def body(c_ref, w_ref, b_ref, o_ref):
        cv = c_ref[...]
        sc = (cv * _sigmoid(cv)).astype(bf16)
        o_ref[...] = jnp.dot(sc, w_ref[...].astype(bf16), preferred_element_type=f32) + b_ref[...]
    return pl.pallas_call(body, out_shape=SDS((N_DEV, w_shard.shape[1]), f32), name="ada_fwd",
                          compiler_params=_params())(c_all, w_shard, b_shard)


def _ada_bwd(c_all, dmod_cols):
    def body(c_ref, d_ref, o_ref):
        cv = c_ref[...]
        sc = cv * _sigmoid(cv)
        o_ref[...] = lax.dot_general(sc, d_ref[...], (((0,), (0,)), ((), ())), precision=lax.Precision.HIGHEST,
                                     preferred_element_type=f32)
    return pl.pallas_call(body, out_shape=SDS((D, dmod_cols.shape[1]), f32), name="ada_bwd",
                          compiler_params=_params())(c_all, dmod_cols)


def _small_reduce(gathered, after):
    def body(g_ref, after_ref, o_ref, loss_ref):
        acc = g_ref[0]
        for d in range(1, N_DEV):
            acc = acc + g_ref[d]
        o_ref[...] = acc
        loss_ref[...] = jnp.zeros((1, LANES), f32) + jnp.sum(acc[10:11, :])
    return pl.pallas_call(body, out_shape=(SDS((SMALL_ROWS, D), f32), SDS((1, LANES), f32)), name="small_reduce",
                          in_specs=[pl.BlockSpec(memory_space=pltpu.VMEM), pl.BlockSpec(memory_space=pl.ANY)],
                          compiler_params=_params())(gathered, after)


FOX_BLK = 512
CUM_BLK = 128


def _fold_lanes(t, op):
    out = t[:, :LANES]
    for j in range(1, t.shape[1] // LANES):
        out = op(out, t[:, j * LANES:(j + 1) * LANES])
    return out


def _fox_gate_fwd(proj, b_pad):
    nblk = SEQ // CUM_BLK

    def body(f_ref, b_ref, col_ref):
        r = lax.broadcasted_iota(jnp.int32, (CUM_BLK, CUM_BLK), 0)
        c = lax.broadcasted_iota(jnp.int32, (CUM_BLK, CUM_BLK), 1)
        tri = (r >= c).astype(f32)
        carry = jnp.zeros((1, LANES), f32)
        for blk in range(nblk):
            z = f_ref[blk * CUM_BLK:(blk + 1) * CUM_BLK, :] + b_ref[...]
            logf = jnp.minimum(z, 0.0) - jnp.log1p(jnp.exp(-jnp.abs(z)))
            cs = jnp.dot(tri, logf, precision=lax.Precision.HIGHEST, preferred_element_type=f32) + carry
            col_ref[blk * CUM_BLK:(blk + 1) * CUM_BLK, :] = cs
            carry = cs[CUM_BLK - 1:CUM_BLK, :]

    return pl.pallas_call(
        body, grid=(1,), in_specs=[pl.BlockSpec((SEQ, LANES), lambda i: (0, C_F // LANES)),
                                   pl.BlockSpec((1, LANES), lambda i: (0, 0))],
        out_specs=pl.BlockSpec((SEQ, LANES), lambda i: (0, 0)),
        out_shape=SDS((SEQ, LANES), f32), name="fox_gate_fwd",
        compiler_params=_params(("arbitrary",)),
    )(proj, b_pad)


def _fox_gate_bwd(dF_row, proj, b_pad):
    nblk = SEQ // CUM_BLK

    def body(d_ref, f_ref, b_ref, df_ref, db_ref, col_ref):
        r = lax.broadcasted_iota(jnp.int32, (CUM_BLK, CUM_BLK), 0)
        c = lax.broadcasted_iota(jnp.int32, (CUM_BLK, CUM_BLK), 1)
        tri = (r <= c).astype(f32)
        lane = lax.broadcasted_iota(jnp.int32, (CUM_BLK, LANES), 1)
        col_ref[...] = d_ref[...].T
        carry = jnp.zeros((1, LANES), f32)
        total = jnp.zeros((1, LANES), f32)
        for blk in reversed(range(nblk)):
            rows = slice(blk * CUM_BLK, (blk + 1) * CUM_BLK)
            cs = jnp.dot(tri, col_ref[rows, :], precision=lax.Precision.HIGHEST, preferred_element_type=f32) + carry
            carry = cs[0:1, :]
            z = f_ref[rows, :] + b_ref[...]
            df = jnp.where(lane < N_FOX_HEADS, cs * _sigmoid(-z), 0.0)
            df_ref[rows, :] = df.astype(df_ref.dtype)
            total = total + _colsum(df)
        db_ref[...] = total

    return pl.pallas_call(
        body, grid=(1,), in_specs=[pl.BlockSpec((LANES, SEQ), lambda i: (0, 0)),
                                   pl.BlockSpec((SEQ, LANES), lambda i: (0, C_F // LANES)),
                                   pl.BlockSpec((1, LANES), lambda i: (0, 0))],
        out_specs=[pl.BlockSpec((SEQ, LANES), lambda i: (0, 0)), pl.BlockSpec((1, LANES), lambda i: (0, 0))],
        out_shape=(SDS((SEQ, LANES), bf16), SDS((1, LANES), f32)), name="fox_gate_bwd",
        scratch_shapes=[pltpu.VMEM((SEQ, LANES), f32)],
        compiler_params=_params(("arbitrary",)),
    )(dF_row, proj, b_pad)


def _nt(a, b):
    return lax.dot_general(a, b, (((1,), (1,)), ((), ())), preferred_element_type=f32)


def _tn(a, b):
    return lax.dot_general(a, b, (((0,), (0,)), ((), ())), preferred_element_type=f32)


def _fox_prep(proj, f_col):
    def fn(t, v):
        q, k, vv, fc = t
        lane = lax.broadcasted_iota(jnp.int32, (q.shape[0], LANES), 1)
        qs, ks = [], []
        for h in range(N_FOX_HEADS):
            pair, pos = divmod(h, 2)
            own = (lane >= pos * HEAD_DIM) & (lane < (pos + 1) * HEAD_DIM)
            base = (1 - pos) * HEAD_DIM
            f = fc[:, h:h + 1]
            hi = f.astype(bf16).astype(f32)
            mid = (f - hi).astype(bf16).astype(f32)
            lo = (f - hi) - mid
            one = jnp.ones_like(f)
            qa = jnp.where(own, q[:, pair * LANES:(pair + 1) * LANES] * ATT_SCALE, 0.0)
            ka = k[:, pair * LANES:(pair + 1) * LANES]
            for idx, (qv, kv) in enumerate([(hi, one), (mid, one), (lo, one), (one, -hi), (one, -mid), (one, -lo)]):
                sel = lane == base + idx
                qa = jnp.where(sel, qv, qa)
                ka = jnp.where(sel, kv, ka)
            qs.append(qa)
            ks.append(ka)
        return [jnp.concatenate(qs, axis=1), jnp.concatenate(ks, axis=1), vv], []
    w = N_FOX_HEADS * LANES
    return _rowwise(fn, "fox_prep", [(proj, FOX_W, C_QA // FOX_W), (proj, FOX_W, C_KA // FOX_W),
                                     (proj, FOX_W, C_VA // FOX_W), (f_col, LANES, 0)], [],
                    [(w, bf16), (w, bf16), (FOX_W, bf16)])


def _fox_fwd(q_aug, k_aug, v):
    blk = FOX_BLK
    npair = FOX_W // LANES

    def body(q_ref, k_ref, v_ref, o_ref, lse_ref, s_scr):
        i = pl.program_id(1)
        tri = lax.broadcasted_iota(jnp.int32, (blk, blk), 0) >= lax.broadcasted_iota(jnp.int32, (blk, blk), 1)
        qh = [q_ref[:, h * LANES:(h + 1) * LANES] for h in range(2)]

        def logits(c, masked):
            off = pl.multiple_of(c * blk, blk)
            tops = []
            for h in range(2):
                s = _nt(qh[h], k_ref[pl.ds(off, blk), h * LANES:(h + 1) * LANES])
                if masked:
                    s = jnp.where(tri, s, NEG)
                s_scr[h, :, pl.ds(off, blk)] = s
                tops.append(_fold_lanes(s, jnp.maximum))
            return tops

        def pass_a(c, m):
            return tuple(jnp.maximum(a, b) for a, b in zip(m, logits(c, False)))

        m = lax.fori_loop(0, i, pass_a, tuple(jnp.full((blk, LANES), NEG, f32) for _ in range(2)))
        mx = [jnp.max(jnp.maximum(a, b), axis=1, keepdims=True) for a, b in zip(m, logits(i, True))]

        def pass_b(c, carry):
            off = pl.multiple_of(c * blk, blk)
            vv = v_ref[pl.ds(off, blk), :]
            new = []
            for h in range(2):
                l, acc = carry[h]
                p = jnp.exp(s_scr[h, :, pl.ds(off, blk)] - mx[h])
                hi = p.astype(bf16)
                lo = (p - hi.astype(f32)).astype(bf16)
                new.append((l + _fold_lanes(p, jnp.add),
                            acc + jnp.dot(hi, vv, preferred_element_type=f32)
                            + jnp.dot(lo, vv, preferred_element_type=f32)))
            return tuple(new)

        zero = jnp.zeros((blk, LANES), f32)
        (l_a, acc_a), (l_b, acc_b) = lax.fori_loop(0, i + 1, pass_b, ((zero, zero), (zero, zero)))
        l_a = jnp.sum(l_a, axis=1, keepdims=True)
        l_b = jnp.sum(l_b, axis=1, keepdims=True)
        first = lax.broadcasted_iota(jnp.int32, (blk, LANES), 1) < HEAD_DIM
        o_ref[...] = jnp.where(first, acc_a / l_a, acc_b / l_b)
        lse_ref[0] = jnp.where(first, mx[0] + jnp.log(l_a), mx[1] + jnp.log(l_b))

    return pl.pallas_call(
        body, grid=(npair, SEQ // blk),
        in_specs=[pl.BlockSpec((blk, 2 * LANES), lambda p, i: (i, p)),
                  pl.BlockSpec((SEQ, 2 * LANES), lambda p, i: (0, p)),
                  pl.BlockSpec((SEQ, LANES), lambda p, i: (0, p))],
        out_specs=[pl.BlockSpec((blk, LANES), lambda p, i: (i, p)),
                   pl.BlockSpec((1, blk, LANES), lambda p, i: (p, i, 0))],
        out_shape=(SDS((SEQ, FOX_W), f32), SDS((npair, SEQ, LANES), f32)), name="fox_fwd",
        scratch_shapes=[pltpu.VMEM((2, blk, SEQ), f32)],
        compiler_params=_params(("parallel", "arbitrary")),
    )(q_aug, k_aug, v)


def _fox_bwd(q_aug, k_aug, v, do, o, lse, after):
    blk = FOX_BLK
    npair = FOX_W // LANES
    nblk = SEQ // blk

    def body(q_ref, k_ref, v_ref, do_ref, o_ref, lse_ref, after_ref, dq_ref, dk_ref, dv_ref, df_ref, dq_acc,
             delta_ref):
        lane_s = lax.broadcasted_iota(jnp.int32, (SEQ, LANES), 1)
        prod = do_ref[...].astype(bf16).astype(f32) * o_ref[...]
        d_a = jnp.sum(jnp.where(lane_s < HEAD_DIM, prod, 0.0), axis=1, keepdims=True)
        d_b = jnp.sum(jnp.where(lane_s >= HEAD_DIM, prod, 0.0), axis=1, keepdims=True)
        delta_ref[...] = jnp.where(lane_s < HEAD_DIM, d_a, d_b)
        dq_acc[...] = jnp.zeros_like(dq_acc)
        df_ref[...] = jnp.zeros_like(df_ref)
        lane = lax.broadcasted_iota(jnp.int32, (blk, LANES), 1)
        own = [lane < HEAD_DIM, lane >= HEAD_DIM]
        tri = lax.broadcasted_iota(jnp.int32, (blk, blk), 0) >= lax.broadcasted_iota(jnp.int32, (blk, blk), 1)

        def q_slab(qoff, h):
            return q_ref[pl.ds(qoff, blk), h * LANES:(h + 1) * LANES]

        def probs(qoff, h, k_h, masked):
            s = _nt(q_slab(qoff, h), k_h)
            if masked:
                s = jnp.where(tri, s, NEG)
            return jnp.exp(s - lse_ref[0, pl.ds(qoff, blk), h * HEAD_DIM:h * HEAD_DIM + 1])

        def k_slabs(koff):
            return [k_ref[pl.ds(koff, blk), h * LANES:(h + 1) * LANES] for h in range(2)]

        def kv_step(kj, _):
            koff = pl.multiple_of(kj * blk, blk)
            k_aug = k_slabs(koff)
            k_own = [jnp.where(own[h], k_aug[h], jnp.zeros_like(k_aug[h])) for h in range(2)]
            vv = v_ref[pl.ds(koff, blk), :]
            v_own = [jnp.where(own[h], vv, jnp.zeros_like(vv)) for h in range(2)]

            def q_tile(qi, carry, masked):
                qoff = pl.multiple_of(qi * blk, blk)
                dd = do_ref[pl.ds(qoff, blk), :].astype(bf16)
                new, dq_add = [], None
                for h in range(2):
                    dk_h, dv_h, dcol = carry[h]
                    p = probs(qoff, h, k_aug[h], masked)
                    dl = p * (_nt(dd, v_own[h]) - delta_ref[pl.ds(qoff, blk), h * HEAD_DIM:h * HEAD_DIM + 1])
                    dlb = dl.astype(bf16)
                    part = jnp.dot(dlb, k_own[h], preferred_element_type=f32)
                    dq_add = part if dq_add is None else dq_add + part
                    new.append((dk_h + _tn(dlb, q_slab(qoff, h)), dv_h + _tn(p.astype(bf16), dd),
                                dcol + _colsum(dl)))
                dq_acc[pl.ds(qoff, blk), :] += dq_add * ATT_SCALE
                return tuple(new)

            zero = (jnp.zeros((blk, LANES), f32), jnp.zeros((blk, LANES), f32), jnp.zeros((1, blk), f32))
            carry = q_tile(kj, (zero, zero), True)
            (dk_a, dv_a, dcol_a), (dk_b, dv_b, dcol_b) = lax.fori_loop(
                kj + 1, nblk, lambda qi, cr: q_tile(qi, cr, False), carry)
            dk_ref[pl.ds(koff, blk), :] = jnp.where(own[0], dk_a, dk_b).astype(dk_ref.dtype)
            dv_ref[pl.ds(koff, blk), :] = jnp.where(own[0], dv_a, dv_b).astype(dv_ref.dtype)
            df_ref[0, 0:1, pl.ds(koff, blk)] = -dcol_a
            df_ref[0, 1:2, pl.ds(koff, blk)] = -dcol_b
            return 0

        lax.fori_loop(0, nblk, kv_step, 0)
        dq_ref[...] = dq_acc[...].astype(dq_ref.dtype)

    pair_aug = pl.BlockSpec((SEQ, 2 * LANES), lambda p: (0, p))
    slab = pl.BlockSpec((SEQ, LANES), lambda p: (0, p))
    per_pair = pl.BlockSpec((1, SEQ, LANES), lambda p: (p, 0, 0))
    rows = pl.BlockSpec((1, 8, SEQ), lambda p: (p, 0, 0))
    return pl.pallas_call(
        body, grid=(npair,),
        in_specs=[pair_aug, pair_aug, slab, slab, slab, per_pair, pl.BlockSpec(memory_space=pl.ANY)],
        out_specs=[slab, slab, slab, rows],
        out_shape=(SDS((SEQ, FOX_W), bf16),) * 3 + (SDS((npair, 8, SEQ), f32),), name="fox_bwd",
        scratch_shapes=[pltpu.VMEM((SEQ, LANES), f32), pltpu.VMEM((SEQ, LANES), f32)],
        compiler_params=_params(("parallel",)),
    )(q_aug, k_aug, v, do, o, lse, after)


DIL_BLK = 128
DILATIONS = (1, 4, 16)
N_GROUPS = len(DILATIONS)
DIL_PAIRS = DIL_OUT_W // LANES


def _dil_blocks(d):
    r1 = lax.broadcasted_iota(jnp.int32, (2 * DIL_BLK, DIL_BLK), 0) & (DIL_BLK - 1)
    c1 = lax.broadcasted_iota(jnp.int32, (2 * DIL_BLK, DIL_BLK), 1)
    r2 = lax.broadcasted_iota(jnp.int32, (2 * DIL_BLK, 2 * DIL_BLK), 0) & (DIL_BLK - 1)
    c2 = lax.broadcasted_iota(jnp.int32, (2 * DIL_BLK, 2 * DIL_BLK), 1)
    band = ((c2 < DIL_BLK) & (c2 >= r2)) | ((c2 >= DIL_BLK) & (c2 - DIL_BLK <= r2))
    out = []
    for r in range(d):
        for b in range(SEQ // d // DIL_BLK):
            rows = pl.ds(r + d * DIL_BLK * b, DIL_BLK, stride=d)
            if b == 0:
                out.append((rows, rows, r1 >= c1))
            else:
                out.append((rows, pl.ds(r + d * DIL_BLK * (b - 1), 2 * DIL_BLK, stride=d), band))
    return out


def _stack_heads(t, first):
    zero = jnp.zeros_like(t)
    return jnp.concatenate([jnp.where(first, t, zero), jnp.where(first, zero, t)], axis=0)


def _dil_fwd(q, k, v, g):
    def body(q_ref, k_ref, v_ref, o_ref, lse_ref):
        first = lax.broadcasted_iota(jnp.int32, (DIL_BLK, LANES), 1) < HEAD_DIM
        for rows, krows, mask in _dil_blocks(DILATIONS[g]):
            qv, kk, vv = q_ref[rows, :].astype(bf16), k_ref[krows, :].astype(bf16), v_ref[krows, :].astype(bf16)
            s = jnp.where(mask, _nt(_stack_heads(qv, first), kk), NEG)
            m = jnp.max(s, axis=1, keepdims=True)
            p = jnp.exp(s - m)
            l = jnp.sum(p, axis=1, keepdims=True)
            out = jnp.dot(p.astype(bf16), vv, preferred_element_type=f32) / l
            lse = m + jnp.log(l)
            o_ref[rows, :] = jnp.where(first, out[:DIL_BLK], out[DIL_BLK:])
            lse_ref[rows, :] = jnp.where(first, lse[:DIL_BLK], lse[DIL_BLK:])

    grouped = pl.BlockSpec((SEQ, LANES), lambda p: (0, DIL_PAIRS * g + p))
    own = pl.BlockSpec((SEQ, LANES), lambda p: (0, p))
    shape = SDS((SEQ, DIL_OUT_W), f32)
    return pl.pallas_call(
        body, grid=(DIL_PAIRS,), in_specs=[grouped] * 3, out_specs=[own] * 2, out_shape=(shape, shape),
        name=f"dil_fwd_{DILATIONS[g]}", compiler_params=_params(("parallel",)),
    )(q, k, v)


def _dil_bwd(q, k, v, do, lse, delta, g):
    def body(q_ref, k_ref, v_ref, do_ref, lse_ref, dl_ref, dq_ref, dk_ref, dv_ref):
        first = lax.broadcasted_iota(jnp.int32, (DIL_BLK, LANES), 1) < HEAD_DIM
        dk_ref[...] = jnp.zeros_like(dk_ref)
        dv_ref[...] = jnp.zeros_like(dv_ref)
        for rows, krows, mask in _dil_blocks(DILATIONS[g]):
            qv, kk, vv = q_ref[rows, :].astype(bf16), k_ref[krows, :].astype(bf16), v_ref[krows, :].astype(bf16)
            lsev, delv = lse_ref[rows, :], dl_ref[rows, :]
            q2 = _stack_heads(qv, first)
            do2 = _stack_heads(do_ref[rows, :].astype(bf16), first)
            per_head = lambda t: jnp.concatenate([t[:, 0:1], t[:, HEAD_DIM:HEAD_DIM + 1]], axis=0)
            p = jnp.exp(jnp.where(mask, _nt(q2, kk), NEG) - per_head(lsev))
            dl = (p * (_nt(do2, vv) - per_head(delv))).astype(bf16)
            dq = jnp.dot(dl, kk, preferred_element_type=f32)
            dq_ref[rows, :] = jnp.where(first, dq[:DIL_BLK], dq[DIL_BLK:]) * ATT_SCALE
            dk_ref[krows, :] += _tn(dl, q2)
            dv_ref[krows, :] += _tn(p.astype(bf16), do2)

    grouped = pl.BlockSpec((SEQ, LANES), lambda p: (0, DIL_PAIRS * g + p))
    own = pl.BlockSpec((SEQ, LANES), lambda p: (0, p))
    shape = SDS((SEQ, DIL_OUT_W), f32)
    return pl.pallas_call(
        body, grid=(DIL_PAIRS,), in_specs=[grouped] * 3 + [own] * 3, out_specs=[own] * 3,
        out_shape=(shape, shape, shape), name=f"dil_bwd_{DILATIONS[g]}", compiler_params=_params(("parallel",)),
    )(q, k, v, do, lse, delta)


def _position():
    return lax.axis_index("x"), lax.axis_index("y"), lax.axis_index("c")


def _all_gather(block, name):
    def body(x_ref, out_ref, send_sems, recv_sems, local_sem):
        x, y, c = _position()
        me, sibling = (x, y, c), (x, y, 1 - c)
        chips = [(1 - x, y), (x, 1 - y), (1 - x, 1 - y)]

        def slot(px, py, pc):
            return out_ref.at[4 * px + 2 * py + pc]

        def copy(k, blk, to, src=None):
            return pltpu.make_async_remote_copy(
                src_ref=slot(*blk) if src is None else src, dst_ref=slot(*blk),
                send_sem=send_sems.at[k], recv_sem=recv_sems.at[k], device_id=to, device_id_type=MESH)

        mine = pltpu.make_async_copy(x_ref, slot(*me), local_sem)
        mine.start()
        first = [copy(0, me, sibling, src=x_ref)]
        first += [copy(1 + j, me, (*chip, c), src=x_ref) for j, chip in enumerate(chips)]
        for cp in first:
            cp.start()
        passed = [copy(4 + j, (*chip, c), sibling) for j, chip in enumerate(chips)]
        for j, chip in enumerate(chips):
            copy(1 + j, (*chip, c), me).wait_recv()
            passed[j].start()
        copy(0, sibling, me).wait_recv()
        for j, chip in enumerate(chips):
            copy(4 + j, (*chip, 1 - c), me).wait_recv()
        for cp in first + passed:
            cp.wait_send()
        mine.wait()

    return pl.pallas_call(
        body, out_shape=SDS((N_DEV,) + block.shape, block.dtype),
        in_specs=[pl.BlockSpec(memory_space=pl.ANY)], out_specs=pl.BlockSpec(memory_space=pl.ANY),
        scratch_shapes=[pltpu.SemaphoreType.DMA((7,)), pltpu.SemaphoreType.DMA((7,)), pltpu.SemaphoreType.DMA],
        name=name,
    )(block)


HBM_SPEC = pl.BlockSpec(memory_space=pltpu.HBM)
SEM_SPEC = pl.BlockSpec(memory_space=pltpu.SEMAPHORE)
SPLIT_COPY = pltpu.CompilerParams(has_side_effects=pltpu.SideEffectType.DATAFLOW_SIDE_EFFECTING)


def _in_hbm(t):
    return pltpu.with_memory_space_constraint(t, pltpu.HBM)


def _pair_copies(g_refs, land_refs, send_sems, recv_sems):
    x, y, c = _position()
    return [pltpu.make_async_remote_copy(
        src_ref=g.at[2 * k + (1 - c)], dst_ref=land.at[k], send_sem=send_sems.at[4 * a + k],
        recv_sem=recv_sems.at[4 * a + k], device_id=(x, y, 1 - c), device_id_type=MESH)
        for a, (g, land) in enumerate(zip(g_refs, land_refs, strict=True)) for k in range(4)]


def _chip_copies(t_refs, land_refs, send_sems, recv_sems):
    x, y, c = _position()
    chips = [(1 - x, y), (x, 1 - y), (1 - x, 1 - y)]
    return [pltpu.make_async_remote_copy(
        src_ref=t.at[2 * px + py], dst_ref=land.at[j], send_sem=send_sems.at[3 * a + j],
        recv_sem=recv_sems.at[3 * a + j], device_id=(px, py, c), device_id_type=MESH)
        for a, (t, land) in enumerate(zip(t_refs, land_refs, strict=True)) for j, (px, py) in enumerate(chips)]


_ROUNDS = {"pair": (_pair_copies, 4), "chip": (_chip_copies, 3)}


def _exchange_start(kind, ts, name):
    copies, slots = _ROUNDS[kind]
    n = len(ts)
    lands = [_in_hbm(lax.empty((slots,) + t.shape[1:], t.dtype)) for t in ts]

    def body(*refs):
        for cp in copies(refs[:n], refs[n:2 * n], refs[2 * n], refs[2 * n + 1]):
            cp.start()
        refs[-1][...] = jnp.zeros_like(refs[-1])

    sems = pltpu.SemaphoreType.DMA((slots * n,))
    res = pl.pallas_call(
        body, name=name, in_specs=[HBM_SPEC] * (2 * n),
        out_shape=(sems, sems, *[pltpu.HBM(t.shape, t.dtype) for t in (*ts, *lands)], SDS((8, LANES), f32)),
        out_specs=(SEM_SPEC, SEM_SPEC, *[HBM_SPEC] * (2 * n), pl.BlockSpec(memory_space=pltpu.VMEM)),
        input_output_aliases={i: 2 + i for i in range(2 * n)}, compiler_params=SPLIT_COPY,
    )(*[_in_hbm(t) for t in ts], *lands)
    return res[:-1], res[-1]


def _exchange_wait(kind, state, after, name):
    copies, _ = _ROUNDS[kind]
    send_sems, recv_sems, *arrays = state
    n = len(arrays) // 2

    def body(*refs):
        for cp in copies(refs[:n], refs[n:2 * n], refs[2 * n], refs[2 * n + 1]):
            cp.wait_send()
            cp.wait_recv()

    res = pl.pallas_call(
        body, name=name, in_specs=[HBM_SPEC] * (2 * n) + [SEM_SPEC, SEM_SPEC, pl.BlockSpec(memory_space=pl.ANY)],
        out_shape=[pltpu.HBM(t.shape, t.dtype) for t in arrays], out_specs=[HBM_SPEC] * (2 * n),
        input_output_aliases={i: i for i in range(2 * n)}, compiler_params=SPLIT_COPY,
    )(*arrays, send_sems, recv_sems, after)
    return res[:n], res[n:]


def _gather_copies(x_refs, out_refs, send_sems, recv_sems):
    x, y, c = _position()
    peers = [(x, y, 1 - c), (1 - x, y, c), (x, 1 - y, c), (1 - x, 1 - y, c)]
    sends, arrivals = [], []
    for a, (x_ref, out_ref) in enumerate(zip(x_refs, out_refs, strict=True)):
        for k, (px, py, pc) in enumerate(peers):
            sems = dict(send_sem=send_sems.at[4 * a + k], recv_sem=recv_sems.at[4 * a + k],
                        device_id=(px, py, pc), device_id_type=MESH)
            sends.append(pltpu.make_async_remote_copy(src_ref=x_ref, dst_ref=out_ref.at[4 * x + 2 * y + c], **sems))
            arrivals.append(pltpu.make_async_remote_copy(src_ref=x_ref, dst_ref=out_ref.at[4 * px + 2 * py + pc],
                                                         **sems))
    return sends, arrivals


def _gather_start(blocks, after, name):
    n = len(blocks)
    outs = [_in_hbm(lax.empty((N_DEV,) + b.shape, b.dtype)) for b in blocks]

    def body(*refs):
        sends, _ = _gather_copies(refs[:n], refs[n:2 * n], refs[2 * n + 1], refs[2 * n + 2])
        for cp in sends:
            cp.start()
        refs[-1][...] = jnp.zeros_like(refs[-1])

    sems = pltpu.SemaphoreType.DMA((4 * n,))
    res = pl.pallas_call(
        body, name=name, in_specs=[HBM_SPEC] * (2 * n) + [pl.BlockSpec(memory_space=pl.ANY)],
        out_shape=(sems, sems, *[pltpu.HBM(t.shape, t.dtype) for t in (*blocks, *outs)], SDS((8, LANES), f32)),
        out_specs=(SEM_SPEC, SEM_SPEC, *[HBM_SPEC] * (2 * n), pl.BlockSpec(memory_space=pltpu.VMEM)),
        input_output_aliases={i: 2 + i for i in range(2 * n)}, compiler_params=SPLIT_COPY,
    )(*[_in_hbm(b) for b in blocks], *outs, after)
    return res[:-1], res[-1]


def _gather_wait(state, after, name):
    send_sems, recv_sems, *arrays = state
    n = len(arrays) // 2

    def body(*refs):
        sends, arrivals = _gather_copies(refs[:n], refs[n:2 * n], refs[2 * n], refs[2 * n + 1])
        for cp in sends:
            cp.wait_send()
        for cp in arrivals:
            cp.wait_recv()

    res = pl.pallas_call(
        body, name=name, in_specs=[HBM_SPEC] * (2 * n) + [SEM_SPEC, SEM_SPEC, pl.BlockSpec(memory_space=pl.ANY)],
        out_shape=[pltpu.HBM(t.shape, t.dtype) for t in arrays], out_specs=[HBM_SPEC] * (2 * n),
        input_output_aliases={i: i for i in range(2 * n)}, compiler_params=SPLIT_COPY,
    )(*arrays, send_sems, recv_sems, after)
    return res[:n], res[n:]


def _gather_finish(partial, name):
    n = len(partial)

    def body(*refs):
        in_refs, out_refs = refs[:n], refs[n:2 * n]
        send_sems, recv_sems = refs[2 * n:]
        x, y, c = _position()
        chips = [(1 - x, y), (x, 1 - y), (1 - x, 1 - y)]
        copies = []
        for a in range(n):
            for j, (px, py) in enumerate(chips):
                cp = pltpu.make_async_remote_copy(
                    src_ref=in_refs[a].at[4 * px + 2 * py + c], dst_ref=out_refs[a].at[4 * px + 2 * py + c],
                    send_sem=send_sems.at[a, j], recv_sem=recv_sems.at[a, j], device_id=(x, y, 1 - c),
                    device_id_type=MESH)
                cp.start()
                copies.append(cp)
        for a in range(n):
            for j, (px, py) in enumerate(chips):
                pltpu.make_async_remote_copy(
                    src_ref=in_refs[a].at[4 * px + 2 * py + (1 - c)], dst_ref=out_refs[a].at[4 * px + 2 * py + (1 - c)],
                    send_sem=send_sems.at[a, j], recv_sem=recv_sems.at[a, j], device_id=(x, y, 1 - c),
                    device_id_type=MESH).wait_recv()
        for cp in copies:
            cp.wait_send()

    hbm = pl.BlockSpec(memory_space=pl.ANY)
    return pl.pallas_call(
        body, out_shape=[SDS(p.shape, p.dtype) for p in partial], in_specs=[hbm] * n, out_specs=[hbm] * n,
        input_output_aliases={a: a for a in range(n)},
        scratch_shapes=[pltpu.SemaphoreType.DMA((n, 3)), pltpu.SemaphoreType.DMA((n, 3))],
        name=name,
    )(*partial)


def _row_tile(rows):
    return 512 if rows % 512 == 0 and rows > 512 else rows


def _pair_add(g, r1, core, name):
    def body(c_ref, g_ref, r_ref, o_ref):
        o_ref[...] = (g_ref[...].astype(f32) + r_ref[...].astype(f32)).astype(o_ref.dtype)

    rows, cols = g.shape[1:]
    tile = _row_tile(rows)
    blk = (1, tile, cols)
    return pl.pallas_call(
        body, out_shape=SDS((4, rows, cols), g.dtype), name=name,
        grid_spec=pltpu.PrefetchScalarGridSpec(
            num_scalar_prefetch=1, grid=(4, rows // tile),
            in_specs=[pl.BlockSpec(blk, lambda k, i, c_ref: (2 * k + c_ref[0], i, 0)),
                      pl.BlockSpec(blk, lambda k, i, c_ref: (k, i, 0))],
            out_specs=pl.BlockSpec(blk, lambda k, i, c_ref: (k, i, 0))),
        compiler_params=_params(("parallel", "arbitrary")),
    )(core, g, r1)


def _chip_add(t, r2, chip, name):
    def body(c_ref, t_ref, r_ref, o_ref):
        o_ref[...] = ((t_ref[0].astype(f32) + r_ref[0].astype(f32)) + r_ref[1].astype(f32)) + r_ref[2].astype(f32)

    rows, cols = t.shape[1:]
    tile = _row_tile(rows)
    return pl.pallas_call(
        body, out_shape=SDS((rows, cols), f32), name=name,
        grid_spec=pltpu.PrefetchScalarGridSpec(
            num_scalar_prefetch=1, grid=(rows // tile,),
            in_specs=[pl.BlockSpec((1, tile, cols), lambda i, c_ref: (c_ref[0], i, 0)),
                      pl.BlockSpec((3, tile, cols), lambda i, c_ref: (0, i, 0))],
            out_specs=pl.BlockSpec((tile, cols), lambda i, c_ref: (i, 0))),
        compiler_params=_params(("arbitrary",)),
    )(chip, t, r2)


def _pad_to(t, axis, size):
    pads = [(0, 0)] * t.ndim
    pads[axis] = (0, size - t.shape[axis])
    return jnp.pad(t, pads)


_REF_COLS = {"qa": (0, FOX_W), "ka": (FOX_W, FOX_W), "va": (2 * FOX_W, FOX_W), "f": (3 * FOX_W, N_FOX_HEADS)}
_REF_COLS.update({n: (3 * FOX_W + N_FOX_HEADS + i * DIL_W, DIL_W) for i, n in enumerate(("qb", "kb", "vb"))})
_REF_COLS.update({n: (3 * FOX_W + N_FOX_HEADS + 3 * DIL_W + i * D, D) for i, n in enumerate(("ga", "gb"))})
_REF_ORDER = ("qa", "ka", "va", "f", "qb", "kb", "vb", "ga", "gb")


def _place_cols(sources, src_of, out_cols, name, row_block=512):
    arrays = [s[0] if isinstance(s, tuple) else s for s in sources]
    widths = [a.shape[-1] for a in arrays]
    rows = arrays[0].shape[-2]
    plan = []
    for t in range(out_cols // LANES):
        segs, c, end = [], t * LANES, (t + 1) * LANES
        while c < end:
            s = src_of(c)
            if s is None:
                c += 1
                continue
            n = 1
            while c + n < end and src_of(c + n) == (s[0], s[1] + n):
                n += 1
            segs.append((s[0], s[1], c - t * LANES, n))
            c += n
        plan.append(segs)

    def body(*refs):
        o_ref = refs[-1]
        for t, segs in enumerate(plan):
            acc = None
            for si, c0, o0, n in segs:
                a0 = c0 // LANES * LANES
                wide = min(2 * LANES, widths[si] - a0)
                win = refs[si][0, :, a0:a0 + wide] if isinstance(sources[si], tuple) else refs[si][:, a0:a0 + wide]
                r = lax.broadcasted_iota(jnp.int32, (wide, LANES), 0)
                c = lax.broadcasted_iota(jnp.int32, (wide, LANES), 1)
                pick = ((r - (c0 - a0) == c - o0) & (c >= o0) & (c < o0 + n)).astype(bf16)
                part = jnp.dot(win.astype(bf16), pick, preferred_element_type=f32)
                acc = part if acc is None else acc + part
            tile = jnp.zeros((row_block, LANES), f32) if acc is None else acc
            o_ref[:, t * LANES:(t + 1) * LANES] = tile.astype(o_ref.dtype)

    def spec(s):
        if isinstance(s, tuple):
            j = s[1]
            return pl.BlockSpec((1, row_block, s[0].shape[-1]), lambda i: (j, i, 0))
        return pl.BlockSpec((row_block, s.shape[-1]), lambda i: (i, 0))

    return pl.pallas_call(
        body, grid=(rows // row_block,), in_specs=[spec(s) for s in sources],
        out_specs=pl.BlockSpec((row_block, out_cols), lambda i: (i, 0)), out_shape=SDS((rows, out_cols), bf16),
        name=name, compiler_params=_params(("parallel",)),
    )(*arrays)


def _ref_piece(r):
    for name in _REF_ORDER:
        lo, width = _REF_COLS[name]
        if lo <= r < lo + width:
            return name, r - lo
    raise ValueError(r)


def _shard_pad_cols(pieces):
    names = [n for n in _REF_ORDER if n != "vb"]
    sources = [pieces[n] for n in names] + list(pieces["vb"])

    def src_of(c):
        j, i = divmod(c, W_IN_PAD)
        if i >= W_IN_SH:
            return None
        name, col = _ref_piece(j * W_IN_SH + i)
        if name == "vb":
            return len(names) + col // DIL_OUT_W, col % DIL_OUT_W
        return names.index(name), col

    return _place_cols(sources, src_of, N_DEV * W_IN_PAD, "place_dproj")


_SLABS = {"ga": C_GA, "gb": C_GB, "qb": C_QB, "kb": C_KB, "vb": C_VB, "qa": C_QA, "ka": C_KA, "va": C_VA, "f": C_F}


def _slab_w_in(stack):
    def src_of(c):
        for name, start in _SLABS.items():
            lo, width = _REF_COLS[name]
            if start <= c < start + width:
                return divmod(lo + c - start, W_IN_SH)
        return None

    return _place_cols([(stack, j) for j in range(N_DEV)], src_of, PROJ_W, "place_w_in")


def kernel(x, c, w_ada, b_ada, g_mix, w_in, b_fgate, w_br_a, w_br_b, w_out, g_ffn, w_ffn_gate, w_ffn_up, w_ffn_down, g_final, loss_target, m_w_ada, m_b_ada, m_g_mix, m_w_in, m_b_fgate, m_w_br_a, m_w_br_b, m_w_out, m_g_ffn, m_w_ffn_gate, m_w_ffn_up, m_w_ffn_down, m_g_final, v_w_ada, v_b_ada, v_g_mix, v_w_in, v_b_fgate, v_w_br_a, v_w_br_b, v_w_out, v_g_ffn, v_w_ffn_gate, v_w_ffn_up, v_w_ffn_down, v_g_final):
    px, py, pc = _position()
    dev = 4 * px + 2 * py + pc
    x2d, tgt = x[0], loss_target[0]

    w_state, w_token = _gather_start([_pad_to(w_in[0], 1, W_IN_PAD).astype(bf16)], c, "gather_w_in_start")
    c_all = _all_gather(c + w_token[:1, :1], "gather_c").reshape(N_DEV, D)
    ada_cols = w_ada.shape[2]
    b_shard = lax.dynamic_slice(b_ada, (0, dev * ada_cols), (1, ada_cols))
    mod_shard = _ada_fwd(c_all, w_ada[0], b_shard)
    mod_all = _all_gather(mod_shard, "gather_mod")
    modv = lax.dynamic_index_in_dim(mod_all, dev, axis=1, keepdims=False).reshape(6, D)
    h1 = _pre1(x2d, modv, g_mix)

    (w_mine,), w_arrived = _gather_wait(w_state, h1, "gather_w_in_wait")
    w_in_s = lax.dynamic_update_slice(_gather_finish(w_arrived, "gather_w_in_finish")[0], w_mine[None], (dev, 0, 0))
    gate_up = jnp.concatenate([_pad_to(w_ffn_gate[0], 1, FF_PAD), _pad_to(w_ffn_up[0], 1, FF_PAD)], axis=1)
    later = [w_br_a[0], w_br_b[0], w_out[0], gate_up, _pad_to(w_ffn_down[0], 0, FF_PAD)]
    later_state, later_token = _gather_start([t.astype(bf16) for t in later], w_in_s, "gather_rest_start")
    w_in_p = _slab_w_in(w_in_s)

    proj = _matmul(h1, w_in_p, name="mm_proj", tm=SEQ, tn=896, tk=D, after=later_token)
    b_pad = jnp.pad(b_fgate, ((0, 0), (0, LANES - N_FOX_HEADS)))
    q_aug, k_aug, va = _fox_prep(proj, _fox_gate_fwd(proj, b_pad))
    ya_h, lse_a = _fox_fwd(q_aug, k_aug, va)

    tables = _rope_tables()
    qb_r, kb_r, vb = _rope_fwd(proj, tables)
    by_group = [_dil_fwd(qb_r, kb_r, vb, grp) for grp in range(N_GROUPS)]
    yb_h, lse_b = _dil_combine([o for o, _ in by_group], [l for _, l in by_group])

    mine, arrived = _gather_wait(later_state, yb_h, "gather_rest_wait")
    w_a_s, w_b_s, w_o_s, w_gu_s, w_d_s = [
        lax.dynamic_update_slice(stack, block[None], (dev, 0, 0))
        for stack, block in zip(_gather_finish(arrived, "gather_rest_finish"), mine, strict=True)]
    w_o = w_o_s.reshape(D, D)
    w_d = w_d_s.reshape(FF_HID, D)
    ya = _matmul_stack(ya_h, w_a_s, name="mm_br_a")
    yb = _matmul_stack(yb_h, w_b_s, name="mm_br_b")

    merged = _merge_fwd(ya, yb, proj)
    mix = _matmul(merged, w_o, name="mm_out", tm=SEQ, tn=512, tk=D)
    x1, h2 = _post1(x2d, mix, modv, g_ffn)
    act, au = _ffn_in(h2, w_gu_s)
    ff = _matmul(act, w_d, name="mm_ffn_down", tm=SEQ // 2, tn=512, tk=FF_HID)

    dx2, dff, dg_final, dga_f, loss_lanes = _final(x1, ff, tgt, modv, g_final.reshape(1, D))
    dau = _ffn_bwd_in(dff, w_d_s, au)

    core = pc.astype(jnp.int32).reshape(1)
    chip = (2 * px + py).astype(jnp.int32).reshape(1)

    def pair_done(state, after, tags, name):
        mine, theirs = _exchange_wait("pair", state, after, "pair_wait_" + name)
        sums = [_pair_add(g, r, core, "pair_add_" + t) for g, r, t in zip(mine, theirs, tags)]
        return _exchange_start("chip", sums, "chip_start_" + name)

    def from_chips(state, after, tags, name):
        sums, got = _exchange_wait("chip", state, after, "chip_wait_" + name)
        return [_chip_add(p, r, chip, "chip_add_" + t) for p, r, t in zip(sums, got, tags)]

    g_gu = _matmul(h2, dau, ta=True, by_shard=True, out_dtype=bf16, name="mm_g_ffn_in", tm=D, tn=2 * FF_PAD, tk=SEQ)
    g_d = _matmul(act, dff, ta=True, out_dtype=bf16, name="mm_g_down", tm=FF_HID // 2, tn=512, tk=SEQ)
    ffn_tags = ["gu", "down"]
    ffn_pair, ffn_pair_token = _exchange_start("pair", [g_gu, g_d.reshape(N_DEV, FF_PAD, D)], "pair_start_ffn")

    dh2 = _matmul_nt_shards(dau, w_gu_s, ffn_pair_token, name="mm_d_h2")
    ffn_state, ffn_token = pair_done(ffn_pair, dh2, ffn_tags, "ffn")
    dx1, dmix, dsh_f, dsc_f, dg_ffn, dga_m = _mid_bwd(dh2, x1, dx2, mix, modv, g_ffn)
    dmerged = _matmul(dmix, w_o, tb=True, name="mm_d_merged", tm=SEQ, tn=512, tk=D, after=ffn_token)
    dya, dyb, dga, dgb = _merge_bwd(dmerged, ya, yb, proj)
    dya_h = _matmul_stack(dya, w_a_s, tb=True, name="mm_d_ya")
    dyb_h = _matmul_stack(dyb, w_b_s, tb=True, name="mm_d_yb")

    g_o = _matmul(merged, dmix, ta=True, out_dtype=bf16, name="mm_g_out", tm=D, tn=512, tk=SEQ)
    g_a = _matmul_stack(ya_h, dya, ta=True, out_dtype=bf16, name="mm_g_br_a")
    g_b = _matmul_stack(yb_h, dyb, ta=True, out_dtype=bf16, name="mm_g_br_b")
    rows_a, rows_b = FOX_W * W_BR_SH // D, DIL_OUT_W * W_BR_SH // D
    g_small = jnp.concatenate([g_a.reshape(N_DEV, rows_a, D), g_b.reshape(N_DEV, rows_b, D),
                               g_o.reshape(N_DEV, W_BR_SH, D)], axis=1)
    small_pair, small_pair_token = _exchange_start("pair", [g_small], "pair_start_small")

    dqa, dka, dva, dF = _fox_bwd(q_aug, k_aug, va, dya_h, ya_h, lse_a, small_pair_token)
    dF_row = jnp.pad(dF[:, :2, :].reshape(N_FOX_HEADS, SEQ), ((0, LANES - N_FOX_HEADS), (0, 0)))
    df, db_fgate = _fox_gate_bwd(dF_row, proj, b_pad)
    small_state, small_token = pair_done(small_pair, df, ["small"], "small")

    delta_b = _dil_delta(dyb_h, yb_h)
    dil_grads = [_dil_bwd(qb_r, kb_r, vb, dyb_h, lse_b, delta_b, grp) for grp in range(N_GROUPS)]
    dqb, dkb = _rope_bwd([t[0] for t in dil_grads], [t[1] for t in dil_grads], tables)

    dproj = _shard_pad_cols({"qa": dqa, "ka": dka, "va": dva, "f": df, "qb": dqb, "kb": dkb,
                             "vb": [t[2] for t in dil_grads], "ga": dga, "gb": dgb})
    g_in = _matmul(h1, dproj, ta=True, by_shard=True, out_dtype=bf16, name="mm_g_in", tm=D, tn=W_IN_PAD, tk=SEQ,
                   after=small_token)
    mix_tags = ["in"]
    mix_pair, mix_pair_token = _exchange_start("pair", [g_in], "pair_start_mixer")

    dh1 = _matmul_nt_shards(dproj, w_in_s, mix_pair_token, name="mm_d_h1")
    grad_x, dsh_m, dsc_m, dg_mix = _first_bwd(dh1, x2d, dx1, modv, g_mix)

    pad_lane = lambda t: jnp.pad(t, ((0, 0), (0, D - t.shape[1])))
    small = jnp.concatenate([dsh_m, dsc_m, dga_m, dsh_f, dsc_f, dga_f, dg_mix, dg_ffn, dg_final,
                             pad_lane(db_fgate), loss_lanes, jnp.zeros((SMALL_ROWS - 11, D), f32)], axis=0)
    small_all = _all_gather(small, "gather_small")
    mix_state, mix_token = pair_done(mix_pair, small_all, mix_tags, "mixer")

    small_sum, loss_row = _small_reduce(small_all, mix_token)
    dmod_all = small_all[:, :6, :].reshape(N_DEV, 6 * D)
    g_w_ada = _ada_bwd(c_all, lax.dynamic_slice(dmod_all, (0, dev * ada_cols), (N_DEV, ada_cols)))
    s_gu, s_d = from_chips(ffn_state, small_sum, ffn_tags, "ffn")
    s_small, = from_chips(small_state, small_sum, ["small"], "small")

    loss = loss_row[0, 0]
    g = {
        "w_ada": g_w_ada[None], "b_ada": small_sum[0:6].reshape(1, 6 * D), "g_mix": small_sum[6:7],
        "b_fgate": small_sum[9:10, :N_FOX_HEADS], "g_ffn": small_sum[7:8], "w_ffn_gate": s_gu[None, :, :W_FF_SH],
        "w_ffn_up": s_gu[None, :, FF_PAD:FF_PAD + W_FF_SH], "w_ffn_down": s_d[None, :W_FF_SH],
        "g_final": small_sum[8], "w_br_a": s_small[:rows_a].reshape(1, FOX_W, W_BR_SH),
        "w_br_b": s_small[rows_a:rows_a + rows_b].reshape(1, DIL_OUT_W, W_BR_SH), "w_out": s_small[None, rows_a + rows_b:],
    }
    w = {"w_ada": w_ada, "b_ada": b_ada, "g_mix": g_mix, "w_in": w_in, "b_fgate": b_fgate, "w_br_a": w_br_a,
         "w_br_b": w_br_b, "w_out": w_out, "g_ffn": g_ffn, "w_ffn_gate": w_ffn_gate, "w_ffn_up": w_ffn_up,
         "w_ffn_down": w_ffn_down, "g_final": g_final}
    m = {"w_ada": m_w_ada, "b_ada": m_b_ada, "g_mix": m_g_mix, "w_in": m_w_in, "b_fgate": m_b_fgate,
         "w_br_a": m_w_br_a, "w_br_b": m_w_br_b, "w_out": m_w_out, "g_ffn": m_g_ffn, "w_ffn_gate": m_w_ffn_gate,
         "w_ffn_up": m_w_ffn_up, "w_ffn_down": m_w_ffn_down, "g_final": m_g_final}
    v = {"w_ada": v_w_ada, "b_ada": v_b_ada, "g_mix": v_g_mix, "w_in": v_w_in, "b_fgate": v_b_fgate,
         "w_br_a": v_w_br_a, "w_br_b": v_w_br_b, "w_out": v_w_out, "g_ffn": v_g_ffn, "w_ffn_gate": v_w_ffn_gate,
         "w_ffn_up": v_w_ffn_up, "w_ffn_down": v_w_ffn_down, "g_final": v_g_final}
    names = list(w)
    delta, new_m, new_v = {}, {}, {}

    transposed = ("w_in", "w_ffn_gate", "w_ffn_up")

    def update(n):
        shape = w[n].shape
        if n in transposed:
            g_t = g[n][0].T
            dl, mn, vn = _adamw(w[n][0].T, g_t, m[n][0].T, v[n][0].T, "adamw_" + n)
            g[n], delta[n], new_m[n], new_v[n] = g_t.T[None], dl.T[None], mn.T[None], vn.T[None]
            return
        two_d = (lambda t: t.reshape(shape[-2:])) if len(shape) == 3 else (lambda t: t)
        dl, mn, vn = _adamw(two_d(w[n]), two_d(g[n]), two_d(m[n]), two_d(v[n]), "adamw_" + n)
        delta[n], new_m[n], new_v[n] = dl.reshape(shape), mn.reshape(shape), vn.reshape(shape)

    for n in list(g):
        update(n)
    done = sum(delta[n].reshape(-1)[:N_FOX_HEADS] for n in g)
    s_in, = from_chips(mix_state, done, mix_tags, "mixer")
    g["w_in"] = s_in[None, :, :W_IN_SH]
    update("w_in")

    return (loss, grad_x[None], *[g[n] for n in names], *[delta[n] for n in names],
            *[new_m[n] for n in names], *[new_v[n] for n in names])
```

```python
import functools

import jax
import jax.numpy as jnp
from jax import lax
from jax.experimental import pallas as pl
from jax.experimental.pallas import tpu as pltpu

f32 = jnp.float32
bf16 = jnp.bfloat16
SDS = jax.ShapeDtypeStruct
MESH = pl.DeviceIdType.MESH

N_DEV = 8
D = 1024
SEQ = 2048
HEAD_DIM = 64
N_FOX_HEADS = 8
FOX_W = 512
DIL_W = 768
DIL_OUT_W = 256
ROT_DIM = 16
ROPE_THETA = 500000.0
D_FF = 2816
IN_COLS = 5896
EPS = 1e-6
NEG = -1e30
ATT_SCALE = HEAD_DIM ** -0.5

ADAM_LR = 0.001
ADAM_B1 = 0.9
ADAM_B2 = 0.999
ADAM_EPS = 1e-08
ADAM_WD = 0.01
ADAM_STEP = 10

C_GA, C_GB, C_QB, C_KB, C_VB, C_QA, C_KA, C_VA, C_F = 0, 1024, 2304, 3072, 3840, 4608, 5120, 5632, 6144
PROJ_W = 6272
LANES = 128
VMEM_LIMIT = 52 * 1024 * 1024

W_IN_SH, W_IN_PAD = IN_COLS // N_DEV, 768
W_BR_SH = D // N_DEV
W_FF_SH, FF_PAD = D_FF // N_DEV, 384
FF_HID = N_DEV * FF_PAD
SMALL_ROWS = 16


def _params(sem=None):
    if sem is None:
        return pltpu.CompilerParams(vmem_limit_bytes=VMEM_LIMIT)
    return pltpu.CompilerParams(dimension_semantics=sem, vmem_limit_bytes=VMEM_LIMIT)


def _rowwise(fn, name, tiled, vecs, outs, reds=(), tile=256):
    nt, nv, no = len(tiled), len(vecs), len(outs)
    rows = tiled[0][0].shape[0]
    assert rows % tile == 0

    def body(*refs):
        tin = [r[...] for r in refs[:nt]]
        vin = [r[...] for r in refs[nt:nt + nv]]
        orefs = refs[nt + nv:nt + nv + no]
        rrefs = refs[nt + nv + no:]
        touts, routs = fn(tin, vin)
        for r, t in zip(orefs, touts, strict=True):
            r[...] = t.astype(r.dtype)
        if rrefs:
            @pl.when(pl.program_id(0) == 0)
            def _():
                for r in rrefs:
                    r[...] = jnp.zeros_like(r)
            for r, t in zip(rrefs, routs, strict=True):
                r[...] += t

    def col_map(cb):
        return lambda i: (i, cb)

    def whole_map(nd):
        return lambda i: (0,) * nd

    in_specs = [pl.BlockSpec((tile, w), col_map(cb)) for (_, w, cb) in tiled]
    in_specs += [pl.BlockSpec(v.shape, whole_map(v.ndim)) for v in vecs]
    out_specs = [pl.BlockSpec((tile, w), lambda i: (i, 0)) for (w, _) in outs]
    out_specs += [pl.BlockSpec((1, w), lambda i: (0, 0)) for w in reds]
    out_shape = [SDS((rows, w), dt) for (w, dt) in outs] + [SDS((1, w), f32) for w in reds]
    res = pl.pallas_call(
        body, grid=(rows // tile,), in_specs=in_specs, out_specs=out_specs, out_shape=out_shape, name=name,
        compiler_params=_params(("arbitrary",)),
    )(*[t[0] for t in tiled], *vecs)
    return res


def _matmul(a, b, *, ta=False, tb=False, out_dtype=f32, name, tm, tn, tk, by_shard=False, after=None):
    m, k = (a.shape[1], a.shape[0]) if ta else a.shape
    if by_shard and not ta:
        n, kb = (b.shape[1], N_DEV * b.shape[2]) if tb else (N_DEV * b.shape[2], b.shape[1])
        assert (tk if tb else tn) == b.shape[2]
    else:
        n, kb = (b.shape[0], b.shape[1]) if tb else (b.shape[1], b.shape[0])
    assert kb == k and m % tm == 0 and n % tn == 0 and k % tk == 0
    nk = k // tk
    dims = (((0 if ta else 1,), (1 if tb else 0,)), ((), ()))
    b_stacked = by_shard and not ta
    o_stacked = by_shard and ta

    def body(a_ref, b_ref, *rest):
        o_ref, *acc = rest[1:] if after is not None else rest
        bv = b_ref[0] if b_stacked else b_ref[...]
        p = lax.dot_general(a_ref[...].astype(bf16), bv.astype(bf16), dims, preferred_element_type=f32)

        def put(val):
            if o_stacked:
                o_ref[0] = val.astype(o_ref.dtype)
            else:
                o_ref[...] = val.astype(o_ref.dtype)

        if nk == 1:
            put(p)
        else:
            acc_ref, = acc
            kk = pl.program_id(2)

            @pl.when(kk == 0)
            def _():
                acc_ref[...] = p

            @pl.when(kk > 0)
            def _():
                acc_ref[...] += p

            @pl.when(kk == nk - 1)
            def _():
                put(acc_ref[...])

    a_spec = pl.BlockSpec((tk, tm), lambda i, j, kk: (kk, i)) if ta else pl.BlockSpec((tm, tk), lambda i, j, kk: (i, kk))
    if b_stacked and tb:
        b_spec = pl.BlockSpec((1, tn, tk), lambda i, j, kk: (kk, j, 0))
    elif b_stacked:
        b_spec = pl.BlockSpec((1, tk, tn), lambda i, j, kk: (j, kk, 0))
    elif tb:
        b_spec = pl.BlockSpec((tn, tk), lambda i, j, kk: (j, kk))
    else:
        b_spec = pl.BlockSpec((tk, tn), lambda i, j, kk: (kk, j))
    if o_stacked:
        assert tn == n // N_DEV
        out_spec = pl.BlockSpec((1, tm, tn), lambda i, j, kk: (j, i, 0))
        out_shape = SDS((N_DEV, m, tn), out_dtype)
    else:
        out_spec = pl.BlockSpec((tm, tn), lambda i, j, kk: (i, j))
        out_shape = SDS((m, n), out_dtype)
    extra_specs, extra = ([pl.BlockSpec(memory_space=pl.ANY)], [after]) if after is not None else ([], [])
    return pl.pallas_call(
        body, grid=(m // tm, n // tn, nk), in_specs=[a_spec, b_spec] + extra_specs, out_specs=out_spec,
        out_shape=out_shape, name=name, scratch_shapes=[pltpu.VMEM((tm, tn), f32)] if nk > 1 else [],
        compiler_params=_params(("parallel", "parallel", "arbitrary")),
    )(a, b, *extra)


def _matmul_stack(a, b, *, ta=False, tb=False, out_dtype=f32, name):
    def lanes(ref):
        return jnp.concatenate([ref[j] for j in range(N_DEV)], axis=1).astype(bf16)

    if ta:
        w = b.shape[1] // N_DEV

        def body(a_ref, b_ref, o_ref):
            p = _tn(a_ref[...].astype(bf16), b_ref[...].astype(bf16))
            for j in range(N_DEV):
                o_ref[j] = p[:, j * w:(j + 1) * w].astype(o_ref.dtype)

        return pl.pallas_call(body, out_shape=SDS((N_DEV, a.shape[1], w), out_dtype), name=name,
                              compiler_params=_params())(a, b)

    m, half = a.shape[0], a.shape[0] // 2
    n = b.shape[1] if tb else N_DEV * b.shape[2]

    def body(a_ref, b_ref, o_ref):
        av = a_ref[...].astype(bf16)
        o_ref[...] = (_nt(av, lanes(b_ref)) if tb else jnp.dot(av, lanes(b_ref), preferred_element_type=f32)
                      ).astype(o_ref.dtype)

    return pl.pallas_call(
        body, grid=(2,), in_specs=[pl.BlockSpec((half, a.shape[1]), lambda i: (i, 0)),
                                   pl.BlockSpec(b.shape, lambda i: (0, 0, 0))],
        out_specs=pl.BlockSpec((half, n), lambda i: (i, 0)), out_shape=SDS((m, n), out_dtype), name=name,
        compiler_params=_params(("parallel",)),
    )(a, b)


def _matmul_nt_shards(a, b, after, fn, tiled, vecs, outs, reds, *, name, tm=512):
    m, n, w = a.shape[0], b.shape[1], b.shape[2]
    assert a.shape[1] == N_DEV * w and m % tm == 0
    nt, nv, no = len(tiled), len(vecs), len(outs)

    def body(a_ref, b_ref, after_ref, *refs):
        acc = _nt(a_ref[:, 0:w], b_ref[0])
        for j in range(1, N_DEV):
            acc = acc + _nt(a_ref[:, j * w:(j + 1) * w], b_ref[j])
        orefs, rrefs = refs[nt + nv:nt + nv + no], refs[nt + nv + no:]
        touts, routs = fn([acc] + [r[...] for r in refs[:nt]], [r[...] for r in refs[nt:nt + nv]])
        for r, t in zip(orefs, touts, strict=True):
            r[...] = t.astype(r.dtype)

        @pl.when(pl.program_id(0) == 0)
        def _():
            for r in rrefs:
                r[...] = jnp.zeros_like(r)
        for r, t in zip(rrefs, routs, strict=True):
            r[...] += t

    def whole_map(nd):
        return lambda i: (0,) * nd

    rows = lambda width: pl.BlockSpec((tm, width), lambda i: (i, 0))
    return pl.pallas_call(
        body, grid=(m // tm,),
        in_specs=[rows(N_DEV * w), pl.BlockSpec((N_DEV, n, w), lambda i: (0, 0, 0), pipeline_mode=pl.Buffered(1)),
                  pl.BlockSpec(memory_space=pl.ANY)]
        + [rows(t.shape[1]) for t in tiled] + [pl.BlockSpec(v.shape, whole_map(v.ndim)) for v in vecs],
        out_specs=[rows(width) for width, _ in outs] + [pl.BlockSpec((1, width), lambda i: (0, 0)) for width in reds],
        out_shape=[SDS((m, width), dt) for width, dt in outs] + [SDS((1, width), f32) for width in reds], name=name,
        compiler_params=_params(("arbitrary",)),
    )(a, b, after, *tiled, *vecs)


def _rms(x):
    r = lax.rsqrt(jnp.mean(x * x, axis=-1, keepdims=True) + EPS)
    return r, x * r


def _rms_bwd(r, xn, dxn):
    return r * (dxn - xn * jnp.mean(dxn * xn, axis=-1, keepdims=True))


def _colsum(t):
    return jnp.sum(t, axis=0, keepdims=True)


def _sigmoid(x):
    return 1.0 / (1.0 + jnp.exp(-x))


def _modulated_norm(x, g, shift, scale):
    _, xn = _rms(x)
    return (xn * g) * (1.0 + scale) + shift


def _pre1(x, modv, g_mix):
    def fn(t, v):
        (xt,), (mv, g) = t, v
        return [_modulated_norm(xt, g, mv[0:1], mv[1:2])], []
    return _rowwise(fn, "pre1", [(x, D, 0)], [modv, g_mix], [(D, bf16)])[0]


def _post1(x, mix, modv, g_ffn):
    def fn(t, v):
        (xt, mt), (mv, g) = t, v
        x1 = xt + mv[2:3] * mt
        return [x1, _modulated_norm(x1, g, mv[3:4], mv[4:5])], []
    return _rowwise(fn, "post1", [(x, D, 0), (mix, D, 0)], [modv, g_ffn], [(D, f32), (D, bf16)])


def _ffn_in(h, w_stack):
    def body(h_ref, w_ref, act_ref, au_ref):
        p = jnp.dot(h_ref[...], w_ref[0], preferred_element_type=f32)
        a, u = p[:, :FF_PAD], p[:, FF_PAD:]
        act_ref[...] = (a * _sigmoid(a) * u).astype(act_ref.dtype)
        au_ref[...] = p.astype(au_ref.dtype)

    return pl.pallas_call(
        body, grid=(N_DEV,),
        in_specs=[pl.BlockSpec((SEQ, D), lambda j: (0, 0)), pl.BlockSpec((1, D, 2 * FF_PAD), lambda j: (j, 0, 0))],
        out_specs=[pl.BlockSpec((SEQ, FF_PAD), lambda j: (0, j)), pl.BlockSpec((SEQ, 2 * FF_PAD), lambda j: (0, j))],
        out_shape=(SDS((SEQ, FF_HID), bf16), SDS((SEQ, 2 * FF_HID), bf16)), name="ffn_in",
        compiler_params=_params(("parallel",)),
    )(h, w_stack)


def _ffn_bwd_in(dff, w_down_stack, au):
    def body(d_ref, w_ref, au_ref, o_ref):
        dact = _nt(d_ref[...], w_ref[0])
        p = au_ref[...].astype(f32)
        a, u = p[:, :FF_PAD], p[:, FF_PAD:]
        sg = _sigmoid(a)
        o_ref[...] = jnp.concatenate([dact * u * (sg * (1.0 + a * (1.0 - sg))), dact * (a * sg)],
                                     axis=1).astype(o_ref.dtype)

    return pl.pallas_call(
        body, grid=(N_DEV,),
        in_specs=[pl.BlockSpec((SEQ, D), lambda j: (0, 0)), pl.BlockSpec((1, FF_PAD, D), lambda j: (j, 0, 0)),
                  pl.BlockSpec((SEQ, 2 * FF_PAD), lambda j: (0, j))],
        out_specs=pl.BlockSpec((SEQ, 2 * FF_PAD), lambda j: (0, j)),
        out_shape=SDS((SEQ, 2 * FF_HID), bf16), name="ffn_bwd_in", compiler_params=_params(("parallel",)),
    )(dff, w_down_stack, au)


def _final(x1, ff, target, modv, g_final):
    def fn(t, v):
        (x1t, fft, tgt), (mv, g) = t, v
        x2 = x1t + mv[5:6] * fft
        r, xn = _rms(x2)
        err = xn * g - tgt
        dy = err * (1.0 / D)
        dx2 = _rms_bwd(r, xn, dy * g)
        return [dx2, dx2 * mv[5:6]], [_colsum(dy * xn), _colsum(dx2 * fft), _colsum(err * err) * (0.5 / D)]
    return _rowwise(fn, "final", [(x1, D, 0), (ff, D, 0), (target, D, 0)], [modv, g_final],
                    [(D, f32), (D, bf16)], [D, D, D])


def _mid_bwd(dau, w_stack, after, x1, dx2, mix, modv, g_ffn):
    def fn(t, v):
        (dh, x1t, dx2t, mt), (mv, g) = t, v
        r, xn = _rms(x1t)
        dn = dh * (1.0 + mv[4:5])
        dx1 = dx2t + _rms_bwd(r, xn, dn * g)
        return [dx1, dx1 * mv[2:3]], [_colsum(dh), _colsum(dh * (xn * g)), _colsum(dn * xn), _colsum(dx1 * mt)]
    return _matmul_nt_shards(dau, w_stack, after, fn, [x1, dx2, mix], [modv, g_ffn], [(D, f32), (D, bf16)],
                             [D, D, D, D], name="mid_bwd")


def _first_bwd(dproj, w_stack, after, x, dx1, modv, g_mix):
    def fn(t, v):
        (dh, xt, dx1t), (mv, g) = t, v
        r, xn = _rms(xt)
        dn = dh * (1.0 + mv[1:2])
        return [dx1t + _rms_bwd(r, xn, dn * g)], [_colsum(dh), _colsum(dh * (xn * g)), _colsum(dn * xn)]
    return _matmul_nt_shards(dproj, w_stack, after, fn, [x, dx1], [modv, g_mix], [(D, f32)], [D, D, D],
                             name="first_bwd")


def _merge_fwd(ya, yb, proj):
    def fn(t, v):
        ya_t, yb_t, ga, gb = t
        return [_sigmoid(ga) * ya_t + _sigmoid(gb) * yb_t], []
    return _rowwise(fn, "merge_fwd", [(ya, D, 0), (yb, D, 0), (proj, D, C_GA // D), (proj, D, C_GB // D)], [],
                    [(D, bf16)])[0]


def _merge_bwd(dmerged, ya, yb, proj):
    def fn(t, v):
        dm, ya_t, yb_t, ga, gb = t
        sa, sb = _sigmoid(ga), _sigmoid(gb)
        return [dm * sa, dm * sb, dm * ya_t * (sa * (1.0 - sa)), dm * yb_t * (sb * (1.0 - sb))], []
    return _rowwise(fn, "merge_bwd",
                    [(dmerged, D, 0), (ya, D, 0), (yb, D, 0), (proj, D, C_GA // D), (proj, D, C_GB // D)], [],
                    [(D, bf16), (D, bf16), (D, bf16), (D, bf16)])


def _rope_tables():
    half = ROT_DIM // 2
    pos = jnp.arange(SEQ, dtype=f32)
    inv_freq = ROPE_THETA ** (-jnp.arange(0, ROT_DIM, 2, dtype=f32) / ROT_DIM)
    ang = pos[:, None] * inv_freq[None, :]
    cos, sin = jnp.cos(ang), jnp.sin(ang)
    pad = jnp.zeros((SEQ, HEAD_DIM - ROT_DIM), f32)
    zero = jnp.zeros((SEQ, half), f32)
    c_head = jnp.concatenate([cos, cos, pad + 1.0], axis=1)
    lo_head = jnp.concatenate([-sin, zero, pad], axis=1)
    hi_head = jnp.concatenate([zero, sin, pad], axis=1)
    return tuple(jnp.concatenate([t, t], axis=1) for t in (c_head, lo_head, hi_head))


def _over_heads(tables):
    return [jnp.tile(t, (1, DIL_W // LANES)) for t in tables]


def _rope_fwd(proj, tables):
    half = ROT_DIM // 2

    def fn(t, v):
        q, k, vv = t[:3]
        c, lo, hi = _over_heads(t[3:])
        rot = lambda z: z * c + pltpu.roll(z, DIL_W - half, 1) * lo + pltpu.roll(z, half, 1) * hi
        return [rot(q) * ATT_SCALE, rot(k), vv], []
    return _rowwise(fn, "rope_fwd", [(proj, DIL_W, C_QB // DIL_W), (proj, DIL_W, C_KB // DIL_W),
                                     (proj, DIL_W, C_VB // DIL_W)] + [(tb, LANES, 0) for tb in tables], [],
                    [(DIL_W, f32)] * 3)


def _rope_bwd(dqs, dks, tables):
    half = ROT_DIM // 2

    def fn(t, v):
        dq_t, dk_t = jnp.concatenate(t[:N_GROUPS], axis=1), jnp.concatenate(t[N_GROUPS:2 * N_GROUPS], axis=1)
        c, lo, hi = _over_heads(t[2 * N_GROUPS:])
        rot_t = lambda z: z * c + pltpu.roll(z * lo, half, 1) + pltpu.roll(z * hi, DIL_W - half, 1)
        return [rot_t(dq_t), rot_t(dk_t)], []
    return _rowwise(fn, "rope_bwd", [(a, DIL_OUT_W, 0) for a in (*dqs, *dks)] + [(tb, LANES, 0) for tb in tables],
                    [], [(DIL_W, bf16), (DIL_W, bf16)])


def _head_bcast_sum(d):
    lane = lax.broadcasted_iota(jnp.int32, d.shape, 1)
    out = jnp.zeros_like(d)
    for h in range(d.shape[1] // HEAD_DIM):
        sel = (lane >= h * HEAD_DIM) & (lane < (h + 1) * HEAD_DIM)
        out = jnp.where(sel, jnp.sum(jnp.where(sel, d, 0.0), axis=1, keepdims=True), out)
    return out


def _dil_combine(outs, lses):
    def fn(t, v):
        o0, o1, o2, l0, l1, l2 = t
        m = jnp.maximum(jnp.maximum(l0, l1), l2)
        w0, w1, w2 = jnp.exp(l0 - m), jnp.exp(l1 - m), jnp.exp(l2 - m)
        tot = w0 + w1 + w2
        return [(w0 * o0 + w1 * o1 + w2 * o2) / tot, m + jnp.log(tot)], []
    w = DIL_OUT_W
    return _rowwise(fn, "dil_combine", [(t, w, 0) for t in (*outs, *lses)], [], [(w, f32), (w, f32)])


def _dil_delta(dyb_h, yb_h):
    def fn(t, v):
        return [_head_bcast_sum(t[0] * t[1])], []
    return _rowwise(fn, "dil_delta", [(dyb_h, DIL_OUT_W, 0), (yb_h, DIL_OUT_W, 0)], [], [(DIL_OUT_W, f32)])[0]


def _adamw_math(wt, gt, mt, vt):
    mn = ADAM_B1 * mt + (1.0 - ADAM_B1) * gt
    vn = ADAM_B2 * vt + (1.0 - ADAM_B2) * (gt * gt)
    m_hat = mn / (1.0 - ADAM_B1 ** ADAM_STEP)
    v_hat = vn / (1.0 - ADAM_B2 ** ADAM_STEP)
    return -ADAM_LR * (m_hat / (jnp.sqrt(v_hat) + ADAM_EPS) + ADAM_WD * wt), mn, vn


def _adamw(w, g, m, v, name):
    shape = w.shape
    if w.ndim == 1:
        w, g, m, v = (t.reshape(1, -1) for t in (w, g, m, v))
    rows, cols = w.shape
    if rows % 8 and rows > 8:
        return _adamw_by_cols(w, g, m, v, name)
    tile = 256 if rows % 256 == 0 and rows > 512 else rows

    def fn(t, _):
        return list(_adamw_math(*t)), []
    delta, mn, vn = _rowwise(fn, name, [(w, cols, 0), (g, cols, 0), (m, cols, 0), (v, cols, 0)], [],
                             [(cols, f32)] * 3, tile=tile)
    return delta.reshape(shape), mn.reshape(shape), vn.reshape(shape)


def _adamw_by_cols(w, g, m, v, name, tile=256):
    rows, cols = w.shape

    def body(w_ref, g_ref, m_ref, v_ref, d_ref, mn_ref, vn_ref):
        d_ref[...], mn_ref[...], vn_ref[...] = _adamw_math(w_ref[...], g_ref[...], m_ref[...], v_ref[...])

    spec = pl.BlockSpec((rows, tile), lambda j: (0, j))
    return pl.pallas_call(body, grid=(cols // tile,), in_specs=[spec] * 4, out_specs=[spec] * 3,
                          out_shape=[SDS((rows, cols), f32)] * 3, name=name,
                          compiler_params=_params(("parallel",)))(w, g, m, v)


def _ada_fwd(c_all, w_shard, b_shard):
    def body(c_ref, w_ref, b_ref, o_ref):
        cv = c_ref[...]
        sc = (cv * _sigmoid(cv)).astype(bf16)
        o_ref[...] = jnp.dot(sc, w_ref[...].astype(bf16), preferred_element_type=f32) + b_ref[...]
    return pl.pallas_call(body, out_shape=SDS((N_DEV, w_shard.shape[1]), f32), name="ada_fwd",
                          compiler_params=_params())(c_all, w_shard, b_shard)


def _ada_bwd(c_all, dmod_cols):
    def body(c_ref, d_ref, o_ref):
        cv = c_ref[...]
        sc = cv * _sigmoid(cv)
        o_ref[...] = lax.dot_general(sc, d_ref[...], (((0,), (0,)), ((), ())), precision=lax.Precision.HIGHEST,
                                     preferred_element_type=f32)
    return pl.pallas_call(body, out_shape=SDS((D, dmod_cols.shape[1]), f32), name="ada_bwd",
                          compiler_params=_params())(c_all, dmod_cols)


def _small_reduce(gathered, after):
    def body(g_ref, after_ref, o_ref, loss_ref):
        acc = g_ref[0]
        for d in range(1, N_DEV):
            acc = acc + g_ref[d]
        o_ref[...] = acc
        loss_ref[...] = jnp.zeros((1, LANES), f32) + jnp.sum(acc[10:11, :])
    return pl.pallas_call(body, out_shape=(SDS((SMALL_ROWS, D), f32), SDS((1, LANES), f32)), name="small_reduce",
                          in_specs=[pl.BlockSpec(memory_space=pltpu.VMEM), pl.BlockSpec(memory_space=pl.ANY)],
                          compiler_params=_params())(gathered, after)


FOX_BLK = 512
CUM_BLK = 128


def _fold_lanes(t, op):
    out = t[:, :LANES]
    for j in range(1, t.shape[1] // LANES):
        out = op(out, t[:, j * LANES:(j + 1) * LANES])
    return out


def _fox_gate_fwd(proj, b_pad):
    nblk = SEQ // CUM_BLK

    def body(f_ref, b_ref, col_ref):
        r = lax.broadcasted_iota(jnp.int32, (CUM_BLK, CUM_BLK), 0)
        c = lax.broadcasted_iota(jnp.int32, (CUM_BLK, CUM_BLK), 1)
        tri = (r >= c).astype(f32)
        carry = jnp.zeros((1, LANES), f32)
        for blk in range(nblk):
            z = f_ref[blk * CUM_BLK:(blk + 1) * CUM_BLK, :] + b_ref[...]
            logf = jnp.minimum(z, 0.0) - jnp.log1p(jnp.exp(-jnp.abs(z)))
            cs = jnp.dot(tri, logf, precision=lax.Precision.HIGHEST, preferred_element_type=f32) + carry
            col_ref[blk * CUM_BLK:(blk + 1) * CUM_BLK, :] = cs
            carry = cs[CUM_BLK - 1:CUM_BLK, :]

    return pl.pallas_call(
        body, grid=(1,), in_specs=[pl.BlockSpec((SEQ, LANES), lambda i: (0, C_F // LANES)),
                                   pl.BlockSpec((1, LANES), lambda i: (0, 0))],
        out_specs=pl.BlockSpec((SEQ, LANES), lambda i: (0, 0)),
        out_shape=SDS((SEQ, LANES), f32), name="fox_gate_fwd",
        compiler_params=_params(("arbitrary",)),
    )(proj, b_pad)


def _fox_gate_bwd(dF_row, proj, b_pad):
    nblk = SEQ // CUM_BLK

    def body(d_ref, f_ref, b_ref, df_ref, db_ref, col_ref):
        r = lax.broadcasted_iota(jnp.int32, (CUM_BLK, CUM_BLK), 0)
        c = lax.broadcasted_iota(jnp.int32, (CUM_BLK, CUM_BLK), 1)
        tri = (r <= c).astype(f32)
        lane = lax.broadcasted_iota(jnp.int32, (CUM_BLK, LANES), 1)
        col_ref[...] = d_ref[...].T
        carry = jnp.zeros((1, LANES), f32)
        total = jnp.zeros((1, LANES), f32)
        for blk in reversed(range(nblk)):
            rows = slice(blk * CUM_BLK, (blk + 1) * CUM_BLK)
            cs = jnp.dot(tri, col_ref[rows, :], precision=lax.Precision.HIGHEST, preferred_element_type=f32) + carry
            carry = cs[0:1, :]
            z = f_ref[rows, :] + b_ref[...]
            df = jnp.where(lane < N_FOX_HEADS, cs * _sigmoid(-z), 0.0)
            df_ref[rows, :] = df.astype(df_ref.dtype)
            total = total + _colsum(df)
        db_ref[...] = total

    return pl.pallas_call(
        body, grid=(1,), in_specs=[pl.BlockSpec((LANES, SEQ), lambda i: (0, 0)),
                                   pl.BlockSpec((SEQ, LANES), lambda i: (0, C_F // LANES)),
                                   pl.BlockSpec((1, LANES), lambda i: (0, 0))],
        out_specs=[pl.BlockSpec((SEQ, LANES), lambda i: (0, 0)), pl.BlockSpec((1, LANES), lambda i: (0, 0))],
        out_shape=(SDS((SEQ, LANES), bf16), SDS((1, LANES), f32)), name="fox_gate_bwd",
        scratch_shapes=[pltpu.VMEM((SEQ, LANES), f32)],
        compiler_params=_params(("arbitrary",)),
    )(dF_row, proj, b_pad)


def _nt(a, b):
    return lax.dot_general(a, b, (((1,), (1,)), ((), ())), preferred_element_type=f32)


def _tn(a, b):
    return lax.dot_general(a, b, (((0,), (0,)), ((), ())), preferred_element_type=f32)


def _fox_prep(proj, f_col):
    def fn(t, v):
        q, k, vv, fc = t
        lane = lax.broadcasted_iota(jnp.int32, (q.shape[0], LANES), 1)
        qs, ks = [], []
        for h in range(N_FOX_HEADS):
            pair, pos = divmod(h, 2)
            own = (lane >= pos * HEAD_DIM) & (lane < (pos + 1) * HEAD_DIM)
            base = (1 - pos) * HEAD_DIM
            f = fc[:, h:h + 1]
            hi = f.astype(bf16).astype(f32)
            mid = (f - hi).astype(bf16).astype(f32)
            lo = (f - hi) - mid
            one = jnp.ones_like(f)
            qa = jnp.where(own, q[:, pair * LANES:(pair + 1) * LANES] * ATT_SCALE, 0.0)
            ka = k[:, pair * LANES:(pair + 1) * LANES]
            for idx, (qv, kv) in enumerate([(hi, one), (mid, one), (lo, one), (one, -hi), (one, -mid), (one, -lo)]):
                sel = lane == base + idx
                qa = jnp.where(sel, qv, qa)
                ka = jnp.where(sel, kv, ka)
            qs.append(qa)
            ks.append(ka)
        return [jnp.concatenate(qs, axis=1), jnp.concatenate(ks, axis=1), vv], []
    w = N_FOX_HEADS * LANES
    return _rowwise(fn, "fox_prep", [(proj, FOX_W, C_QA // FOX_W), (proj, FOX_W, C_KA // FOX_W),
                                     (proj, FOX_W, C_VA // FOX_W), (f_col, LANES, 0)], [],
                    [(w, bf16), (w, bf16), (FOX_W, bf16)])


def _fox_fwd(q_aug, k_aug, v):
    blk = FOX_BLK
    npair = FOX_W // LANES

    def body(q_ref, k_ref, v_ref, o_ref, lse_ref, s_scr):
        i = pl.program_id(1)
        tri = lax.broadcasted_iota(jnp.int32, (blk, blk), 0) >= lax.broadcasted_iota(jnp.int32, (blk, blk), 1)
        qh = [q_ref[:, h * LANES:(h + 1) * LANES] for h in range(2)]

        def logits(c, masked):
            off = pl.multiple_of(c * blk, blk)
            tops = []
            for h in range(2):
                s = _nt(qh[h], k_ref[pl.ds(off, blk), h * LANES:(h + 1) * LANES])
                if masked:
                    s = jnp.where(tri, s, NEG)
                s_scr[h, :, pl.ds(off, blk)] = s
                tops.append(_fold_lanes(s, jnp.maximum))
            return tops

        def pass_a(c, m):
            return tuple(jnp.maximum(a, b) for a, b in zip(m, logits(c, False)))

        m = lax.fori_loop(0, i, pass_a, tuple(jnp.full((blk, LANES), NEG, f32) for _ in range(2)))
        mx = [jnp.max(jnp.maximum(a, b), axis=1, keepdims=True) for a, b in zip(m, logits(i, True))]

        def pass_b(c, carry):
            off = pl.multiple_of(c * blk, blk)
            vv = v_ref[pl.ds(off, blk), :]
            new = []
            for h in range(2):
                l, acc = carry[h]
                p = jnp.exp(s_scr[h, :, pl.ds(off, blk)] - mx[h])
                hi = p.astype(bf16)
                lo = (p - hi.astype(f32)).astype(bf16)
                new.append((l + _fold_lanes(p, jnp.add),
                            acc + jnp.dot(hi, vv, preferred_element_type=f32)
                            + jnp.dot(lo, vv, preferred_element_type=f32)))
            return tuple(new)

        zero = jnp.zeros((blk, LANES), f32)
        (l_a, acc_a), (l_b, acc_b) = lax.fori_loop(0, i + 1, pass_b, ((zero, zero), (zero, zero)))
        l_a = jnp.sum(l_a, axis=1, keepdims=True)
        l_b = jnp.sum(l_b, axis=1, keepdims=True)
        first = lax.broadcasted_iota(jnp.int32, (blk, LANES), 1) < HEAD_DIM
        o_ref[...] = jnp.where(first, acc_a / l_a, acc_b / l_b)
        lse_ref[0] = jnp.where(first, mx[0] + jnp.log(l_a), mx[1] + jnp.log(l_b))

    return pl.pallas_call(
        body, grid=(npair, SEQ // blk),
        in_specs=[pl.BlockSpec((blk, 2 * LANES), lambda p, i: (i, p)),
                  pl.BlockSpec((SEQ, 2 * LANES), lambda p, i: (0, p)),
                  pl.BlockSpec((SEQ, LANES), lambda p, i: (0, p))],
        out_specs=[pl.BlockSpec((blk, LANES), lambda p, i: (i, p)),
                   pl.BlockSpec((1, blk, LANES), lambda p, i: (p, i, 0))],
        out_shape=(SDS((SEQ, FOX_W), f32), SDS((npair, SEQ, LANES), f32)), name="fox_fwd",
        scratch_shapes=[pltpu.VMEM((2, blk, SEQ), f32)],
        compiler_params=_params(("parallel", "arbitrary")),
    )(q_aug, k_aug, v)


def _fox_bwd(q_aug, k_aug, v, do, o, lse, after):
    blk = FOX_BLK
    npair = FOX_W // LANES
    nblk = SEQ // blk

    def body(q_ref, k_ref, v_ref, do_ref, o_ref, lse_ref, after_ref, dq_ref, dk_ref, dv_ref, df_ref, dq_acc,
             delta_ref):
        lane_s = lax.broadcasted_iota(jnp.int32, (SEQ, LANES), 1)
        prod = do_ref[...].astype(bf16).astype(f32) * o_ref[...]
        d_a = jnp.sum(jnp.where(lane_s < HEAD_DIM, prod, 0.0), axis=1, keepdims=True)
        d_b = jnp.sum(jnp.where(lane_s >= HEAD_DIM, prod, 0.0), axis=1, keepdims=True)
        delta_ref[...] = jnp.where(lane_s < HEAD_DIM, d_a, d_b)
        dq_acc[...] = jnp.zeros_like(dq_acc)
        df_ref[...] = jnp.zeros_like(df_ref)
        lane = lax.broadcasted_iota(jnp.int32, (blk, LANES), 1)
        own = [lane < HEAD_DIM, lane >= HEAD_DIM]
        tri = lax.broadcasted_iota(jnp.int32, (blk, blk), 0) >= lax.broadcasted_iota(jnp.int32, (blk, blk), 1)

        def q_slab(qoff, h):
            return q_ref[pl.ds(qoff, blk), h * LANES:(h + 1) * LANES]

        def probs(qoff, h, k_h, masked):
            s = _nt(q_slab(qoff, h), k_h)
            if masked:
                s = jnp.where(tri, s, NEG)
            return jnp.exp(s - lse_ref[0, pl.ds(qoff, blk), h * HEAD_DIM:h * HEAD_DIM + 1])

        def k_slabs(koff):
            return [k_ref[pl.ds(koff, blk), h * LANES:(h + 1) * LANES] for h in range(2)]

        def kv_step(kj, _):
            koff = pl.multiple_of(kj * blk, blk)
            k_aug = k_slabs(koff)
            k_own = [jnp.where(own[h], k_aug[h], jnp.zeros_like(k_aug[h])) for h in range(2)]
            vv = v_ref[pl.ds(koff, blk), :]
            v_own = [jnp.where(own[h], vv, jnp.zeros_like(vv)) for h in range(2)]

            def q_tile(qi, carry, masked):
                qoff = pl.multiple_of(qi * blk, blk)
                dd = do_ref[pl.ds(qoff, blk), :].astype(bf16)
                new, dq_add = [], None
                for h in range(2):
                    dk_h, dv_h, dcol = carry[h]
                    p = probs(qoff, h, k_aug[h], masked)
                    dl = p * (_nt(dd, v_own[h]) - delta_ref[pl.ds(qoff, blk), h * HEAD_DIM:h * HEAD_DIM + 1])
                    dlb = dl.astype(bf16)
                    part = jnp.dot(dlb, k_own[h], preferred_element_type=f32)
                    dq_add = part if dq_add is None else dq_add + part
                    new.append((dk_h + _tn(dlb, q_slab(qoff, h)), dv_h + _tn(p.astype(bf16), dd),
                                dcol + _colsum(dl)))
                dq_acc[pl.ds(qoff, blk), :] += dq_add * ATT_SCALE
                return tuple(new)

            zero = (jnp.zeros((blk, LANES), f32), jnp.zeros((blk, LANES), f32), jnp.zeros((1, blk), f32))
            carry = q_tile(kj, (zero, zero), True)
            (dk_a, dv_a, dcol_a), (dk_b, dv_b, dcol_b) = lax.fori_loop(
                kj + 1, nblk, lambda qi, cr: q_tile(qi, cr, False), carry)
            dk_ref[pl.ds(koff, blk), :] = jnp.where(own[0], dk_a, dk_b).astype(dk_ref.dtype)
            dv_ref[pl.ds(koff, blk), :] = jnp.where(own[0], dv_a, dv_b).astype(dv_ref.dtype)
            df_ref[0, 0:1, pl.ds(koff, blk)] = -dcol_a
            df_ref[0, 1:2, pl.ds(koff, blk)] = -dcol_b
            return 0

        lax.fori_loop(0, nblk, kv_step, 0)
        dq_ref[...] = dq_acc[...].astype(dq_ref.dtype)

    pair_aug = pl.BlockSpec((SEQ, 2 * LANES), lambda p: (0, p))
    slab = pl.BlockSpec((SEQ, LANES), lambda p: (0, p))
    per_pair = pl.BlockSpec((1, SEQ, LANES), lambda p: (p, 0, 0))
    rows = pl.BlockSpec((1, 8, SEQ), lambda p: (p, 0, 0))
    return pl.pallas_call(
        body, grid=(npair,),
        in_specs=[pair_aug, pair_aug, slab, slab, slab, per_pair, pl.BlockSpec(memory_space=pl.ANY)],
        out_specs=[slab, slab, slab, rows],
        out_shape=(SDS((SEQ, FOX_W), bf16),) * 3 + (SDS((npair, 8, SEQ), f32),), name="fox_bwd",
        scratch_shapes=[pltpu.VMEM((SEQ, LANES), f32), pltpu.VMEM((SEQ, LANES), f32)],
        compiler_params=_params(("parallel",)),
    )(q_aug, k_aug, v, do, o, lse, after)


DIL_BLK = 128
DILATIONS = (1, 4, 16)
N_GROUPS = len(DILATIONS)
DIL_PAIRS = DIL_OUT_W // LANES


def _dil_blocks(d):
    r1 = lax.broadcasted_iota(jnp.int32, (2 * DIL_BLK, DIL_BLK), 0) & (DIL_BLK - 1)
    c1 = lax.broadcasted_iota(jnp.int32, (2 * DIL_BLK, DIL_BLK), 1)
    r2 = lax.broadcasted_iota(jnp.int32, (2 * DIL_BLK, 2 * DIL_BLK), 0) & (DIL_BLK - 1)
    c2 = lax.broadcasted_iota(jnp.int32, (2 * DIL_BLK, 2 * DIL_BLK), 1)
    band = ((c2 < DIL_BLK) & (c2 >= r2)) | ((c2 >= DIL_BLK) & (c2 - DIL_BLK <= r2))
    out = []
    for r in range(d):
        for b in range(SEQ // d // DIL_BLK):
            rows = pl.ds(r + d * DIL_BLK * b, DIL_BLK, stride=d)
            if b == 0:
                out.append((rows, rows, r1 >= c1))
            else:
                out.append((rows, pl.ds(r + d * DIL_BLK * (b - 1), 2 * DIL_BLK, stride=d), band))
    return out


def _stack_heads(t, first):
    zero = jnp.zeros_like(t)
    return jnp.concatenate([jnp.where(first, t, zero), jnp.where(first, zero, t)], axis=0)


def _dil_fwd(q, k, v, g):
    def body(q_ref, k_ref, v_ref, o_ref, lse_ref):
        first = lax.broadcasted_iota(jnp.int32, (DIL_BLK, LANES), 1) < HEAD_DIM
        for rows, krows, mask in _dil_blocks(DILATIONS[g]):
            qv, kk, vv = q_ref[rows, :].astype(bf16), k_ref[krows, :].astype(bf16), v_ref[krows, :].astype(bf16)
            s = jnp.where(mask, _nt(_stack_heads(qv, first), kk), NEG)
            m = jnp.max(s, axis=1, keepdims=True)
            p = jnp.exp(s - m)
            l = jnp.sum(p, axis=1, keepdims=True)
            out = jnp.dot(p.astype(bf16), vv, preferred_element_type=f32) / l
            lse = m + jnp.log(l)
            o_ref[rows, :] = jnp.where(first, out[:DIL_BLK], out[DIL_BLK:])
            lse_ref[rows, :] = jnp.where(first, lse[:DIL_BLK], lse[DIL_BLK:])

    grouped = pl.BlockSpec((SEQ, LANES), lambda p: (0, DIL_PAIRS * g + p))
    own = pl.BlockSpec((SEQ, LANES), lambda p: (0, p))
    shape = SDS((SEQ, DIL_OUT_W), f32)
    return pl.pallas_call(
        body, grid=(DIL_PAIRS,), in_specs=[grouped] * 3, out_specs=[own] * 2, out_shape=(shape, shape),
        name=f"dil_fwd_{DILATIONS[g]}", compiler_params=_params(("parallel",)),
    )(q, k, v)


def _dil_bwd(q, k, v, do, lse, delta, g):
    def body(q_ref, k_ref, v_ref, do_ref, lse_ref, dl_ref, dq_ref, dk_ref, dv_ref):
        first = lax.broadcasted_iota(jnp.int32, (DIL_BLK, LANES), 1) < HEAD_DIM
        dk_ref[...] = jnp.zeros_like(dk_ref)
        dv_ref[...] = jnp.zeros_like(dv_ref)
        for rows, krows, mask in _dil_blocks(DILATIONS[g]):
            qv, kk, vv = q_ref[rows, :].astype(bf16), k_ref[krows, :].astype(bf16), v_ref[krows, :].astype(bf16)
            lsev, delv = lse_ref[rows, :], dl_ref[rows, :]
            q2 = _stack_heads(qv, first)
            do2 = _stack_heads(do_ref[rows, :].astype(bf16), first)
            per_head = lambda t: jnp.concatenate([t[:, 0:1], t[:, HEAD_DIM:HEAD_DIM + 1]], axis=0)
            p = jnp.exp(jnp.where(mask, _nt(q2, kk), NEG) - per_head(lsev))
            dl = (p * (_nt(do2, vv) - per_head(delv))).astype(bf16)
            dq = jnp.dot(dl, kk, preferred_element_type=f32)
            dq_ref[rows, :] = jnp.where(first, dq[:DIL_BLK], dq[DIL_BLK:]) * ATT_SCALE
            dk_ref[krows, :] += _tn(dl, q2)
            dv_ref[krows, :] += _tn(p.astype(bf16), do2)

    grouped = pl.BlockSpec((SEQ, LANES), lambda p: (0, DIL_PAIRS * g + p))
    own = pl.BlockSpec((SEQ, LANES), lambda p: (0, p))
    shape = SDS((SEQ, DIL_OUT_W), f32)
    return pl.pallas_call(
        body, grid=(DIL_PAIRS,), in_specs=[grouped] * 3 + [own] * 3, out_specs=[own] * 3,
        out_shape=(shape, shape, shape), name=f"dil_bwd_{DILATIONS[g]}", compiler_params=_params(("parallel",)),
    )(q, k, v, do, lse, delta)


def _position():
    return lax.axis_index("x"), lax.axis_index("y"), lax.axis_index("c")


def _all_gather(block, name):
    def body(x_ref, out_ref, send_sems, recv_sems, local_sem):
        x, y, c = _position()
        me, sibling = (x, y, c), (x, y, 1 - c)
        chips = [(1 - x, y), (x, 1 - y), (1 - x, 1 - y)]

        def slot(px, py, pc):
            return out_ref.at[4 * px + 2 * py + pc]

        def copy(k, blk, to, src=None):
            return pltpu.make_async_remote_copy(
                src_ref=slot(*blk) if src is None else src, dst_ref=slot(*blk),
                send_sem=send_sems.at[k], recv_sem=recv_sems.at[k], device_id=to, device_id_type=MESH)

        mine = pltpu.make_async_copy(x_ref, slot(*me), local_sem)
        mine.start()
        first = [copy(0, me, sibling, src=x_ref)]
        first += [copy(1 + j, me, (*chip, c), src=x_ref) for j, chip in enumerate(chips)]
        for cp in first:
            cp.start()
        passed = [copy(4 + j, (*chip, c), sibling) for j, chip in enumerate(chips)]
        for j, chip in enumerate(chips):
            copy(1 + j, (*chip, c), me).wait_recv()
            passed[j].start()
        copy(0, sibling, me).wait_recv()
        for j, chip in enumerate(chips):
            copy(4 + j, (*chip, 1 - c), me).wait_recv()
        for cp in first + passed:
            cp.wait_send()
        mine.wait()

    return pl.pallas_call(
        body, out_shape=SDS((N_DEV,) + block.shape, block.dtype),
        in_specs=[pl.BlockSpec(memory_space=pl.ANY)], out_specs=pl.BlockSpec(memory_space=pl.ANY),
        scratch_shapes=[pltpu.SemaphoreType.DMA((7,)), pltpu.SemaphoreType.DMA((7,)), pltpu.SemaphoreType.DMA],
        name=name,
    )(block)


HBM_SPEC = pl.BlockSpec(memory_space=pltpu.HBM)
SEM_SPEC = pl.BlockSpec(memory_space=pltpu.SEMAPHORE)
SPLIT_COPY = pltpu.CompilerParams(has_side_effects=pltpu.SideEffectType.DATAFLOW_SIDE_EFFECTING)


def _in_hbm(t):
    return pltpu.with_memory_space_constraint(t, pltpu.HBM)


def _pair_copies(g_refs, land_refs, send_sems, recv_sems):
    x, y, c = _position()
    return [pltpu.make_async_remote_copy(
        src_ref=g.at[2 * k + (1 - c)], dst_ref=land.at[k], send_sem=send_sems.at[4 * a + k],
        recv_sem=recv_sems.at[4 * a + k], device_id=(x, y, 1 - c), device_id_type=MESH)
        for a, (g, land) in enumerate(zip(g_refs, land_refs, strict=True)) for k in range(4)]


def _chip_copies(t_refs, land_refs, send_sems, recv_sems):
    x, y, c = _position()
    chips = [(1 - x, y), (x, 1 - y), (1 - x, 1 - y)]
    return [pltpu.make_async_remote_copy(
        src_ref=t.at[2 * px + py], dst_ref=land.at[j], send_sem=send_sems.at[3 * a + j],
        recv_sem=recv_sems.at[3 * a + j], device_id=(px, py, c), device_id_type=MESH)
        for a, (t, land) in enumerate(zip(t_refs, land_refs, strict=True)) for j, (px, py) in enumerate(chips)]


_ROUNDS = {"pair": (_pair_copies, 4), "chip": (_chip_copies, 3)}


def _exchange_start(kind, ts, name):
    copies, slots = _ROUNDS[kind]
    n = len(ts)
    lands = [_in_hbm(lax.empty((slots,) + t.shape[1:], t.dtype)) for t in ts]

    def body(*refs):
        for cp in copies(refs[:n], refs[n:2 * n], refs[2 * n], refs[2 * n + 1]):
            cp.start()
        refs[-1][...] = jnp.zeros_like(refs[-1])

    sems = pltpu.SemaphoreType.DMA((slots * n,))
    res = pl.pallas_call(
        body, name=name, in_specs=[HBM_SPEC] * (2 * n),
        out_shape=(sems, sems, *[pltpu.HBM(t.shape, t.dtype) for t in (*ts, *lands)], SDS((8, LANES), f32)),
        out_specs=(SEM_SPEC, SEM_SPEC, *[HBM_SPEC] * (2 * n), pl.BlockSpec(memory_space=pltpu.VMEM)),
        input_output_aliases={i: 2 + i for i in range(2 * n)}, compiler_params=SPLIT_COPY,
    )(*[_in_hbm(t) for t in ts], *lands)
    return res[:-1], res[-1]


def _exchange_wait(kind, state, after, name):
    copies, _ = _ROUNDS[kind]
    send_sems, recv_sems, *arrays = state
    n = len(arrays) // 2

    def body(*refs):
        for cp in copies(refs[:n], refs[n:2 * n], refs[2 * n], refs[2 * n + 1]):
            cp.wait_send()
            cp.wait_recv()

    res = pl.pallas_call(
        body, name=name, in_specs=[HBM_SPEC] * (2 * n) + [SEM_SPEC, SEM_SPEC, pl.BlockSpec(memory_space=pl.ANY)],
        out_shape=[pltpu.HBM(t.shape, t.dtype) for t in arrays], out_specs=[HBM_SPEC] * (2 * n),
        input_output_aliases={i: i for i in range(2 * n)}, compiler_params=SPLIT_COPY,
    )(*arrays, send_sems, recv_sems, after)
    return res[:n], res[n:]


def _gather_copies(x_refs, out_refs, send_sems, recv_sems):
    x, y, c = _position()
    peers = [(x, y, 1 - c), (1 - x, y, c), (x, 1 - y, c), (1 - x, 1 - y, c)]
    sends, arrivals = [], []
    for a, (x_ref, out_ref) in enumerate(zip(x_refs, out_refs, strict=True)):
        for k, (px, py, pc) in enumerate(peers):
            sems = dict(send_sem=send_sems.at[4 * a + k], recv_sem=recv_sems.at[4 * a + k],
                        device_id=(px, py, pc), device_id_type=MESH)
            sends.append(pltpu.make_async_remote_copy(src_ref=x_ref, dst_ref=out_ref.at[4 * x + 2 * y + c], **sems))
            arrivals.append(pltpu.make_async_remote_copy(src_ref=x_ref, dst_ref=out_ref.at[4 * px + 2 * py + pc],
                                                         **sems))
    return sends, arrivals


def _gather_start(blocks, after, name):
    n = len(blocks)
    outs = [_in_hbm(lax.empty((N_DEV,) + b.shape, b.dtype)) for b in blocks]

    def body(*refs):
        sends, _ = _gather_copies(refs[:n], refs[n:2 * n], refs[2 * n + 1], refs[2 * n + 2])
        for cp in sends:
            cp.start()
        refs[-1][...] = jnp.zeros_like(refs[-1])

    sems = pltpu.SemaphoreType.DMA((4 * n,))
    res = pl.pallas_call(
        body, name=name, in_specs=[HBM_SPEC] * (2 * n) + [pl.BlockSpec(memory_space=pl.ANY)],
        out_shape=(sems, sems, *[pltpu.HBM(t.shape, t.dtype) for t in (*blocks, *outs)], SDS((8, LANES), f32)),
        out_specs=(SEM_SPEC, SEM_SPEC, *[HBM_SPEC] * (2 * n), pl.BlockSpec(memory_space=pltpu.VMEM)),
        input_output_aliases={i: 2 + i for i in range(2 * n)}, compiler_params=SPLIT_COPY,
    )(*[_in_hbm(b) for b in blocks], *outs, after)
    return res[:-1], res[-1]


def _gather_wait(state, after, name):
    send_sems, recv_sems, *arrays = state
    n = len(arrays) // 2

    def body(*refs):
        sends, arrivals = _gather_copies(refs[:n], refs[n:2 * n], refs[2 * n], refs[2 * n + 1])
        for cp in sends:
            cp.wait_send()
        for cp in arrivals:
            cp.wait_recv()

    res = pl.pallas_call(
        body, name=name, in_specs=[HBM_SPEC] * (2 * n) + [SEM_SPEC, SEM_SPEC, pl.BlockSpec(memory_space=pl.ANY)],
        out_shape=[pltpu.HBM(t.shape, t.dtype) for t in arrays], out_specs=[HBM_SPEC] * (2 * n),
        input_output_aliases={i: i for i in range(2 * n)}, compiler_params=SPLIT_COPY,
    )(*arrays, send_sems, recv_sems, after)
    return res[:n], res[n:]


def _gather_finish(partial, name):
    n = len(partial)

    def body(*refs):
        in_refs, out_refs = refs[:n], refs[n:2 * n]
        send_sems, recv_sems = refs[2 * n:]
        x, y, c = _position()
        chips = [(1 - x, y), (x, 1 - y), (1 - x, 1 - y)]
        copies = []
        for a in range(n):
            for j, (px, py) in enumerate(chips):
                cp = pltpu.make_async_remote_copy(
                    src_ref=in_refs[a].at[4 * px + 2 * py + c], dst_ref=out_refs[a].at[4 * px + 2 * py + c],
                    send_sem=send_sems.at[a, j], recv_sem=recv_sems.at[a, j], device_id=(x, y, 1 - c),
                    device_id_type=MESH)
                cp.start()
                copies.append(cp)
        for a in range(n):
            for j, (px, py) in enumerate(chips):
                pltpu.make_async_remote_copy(
                    src_ref=in_refs[a].at[4 * px + 2 * py + (1 - c)], dst_ref=out_refs[a].at[4 * px + 2 * py + (1 - c)],
                    send_sem=send_sems.at[a, j], recv_sem=recv_sems.at[a, j], device_id=(x, y, 1 - c),
                    device_id_type=MESH).wait_recv()
        for cp in copies:
            cp.wait_send()

    hbm = pl.BlockSpec(memory_space=pl.ANY)
    return pl.pallas_call(
        body, out_shape=[SDS(p.shape, p.dtype) for p in partial], in_specs=[hbm] * n, out_specs=[hbm] * n,
        input_output_aliases={a: a for a in range(n)},
        scratch_shapes=[pltpu.SemaphoreType.DMA((n, 3)), pltpu.SemaphoreType.DMA((n, 3))],
        name=name,
    )(*partial)


def _row_tile(rows):
    return 512 if rows % 512 == 0 and rows > 512 else rows


def _pair_add(g, r1, core, name):
    def body(c_ref, g_ref, r_ref, o_ref):
        o_ref[...] = (g_ref[...].astype(f32) + r_ref[...].astype(f32)).astype(o_ref.dtype)

    rows, cols = g.shape[1:]
    tile = _row_tile(rows)
    blk = (1, tile, cols)
    return pl.pallas_call(
        body, out_shape=SDS((4, rows, cols), g.dtype), name=name,
        grid_spec=pltpu.PrefetchScalarGridSpec(
            num_scalar_prefetch=1, grid=(4, rows // tile),
            in_specs=[pl.BlockSpec(blk, lambda k, i, c_ref: (2 * k + c_ref[0], i, 0)),
                      pl.BlockSpec(blk, lambda k, i, c_ref: (k, i, 0))],
            out_specs=pl.BlockSpec(blk, lambda k, i, c_ref: (k, i, 0))),
        compiler_params=_params(("parallel", "arbitrary")),
    )(core, g, r1)


def _chip_add(t, r2, chip, name):
    def body(c_ref, t_ref, r_ref, o_ref):
        o_ref[...] = ((t_ref[0].astype(f32) + r_ref[0].astype(f32)) + r_ref[1].astype(f32)) + r_ref[2].astype(f32)

    rows, cols = t.shape[1:]
    tile = _row_tile(rows)
    return pl.pallas_call(
        body, out_shape=SDS((rows, cols), f32), name=name,
        grid_spec=pltpu.PrefetchScalarGridSpec(
            num_scalar_prefetch=1, grid=(rows // tile,),
            in_specs=[pl.BlockSpec((1, tile, cols), lambda i, c_ref: (c_ref[0], i, 0)),
                      pl.BlockSpec((3, tile, cols), lambda i, c_ref: (0, i, 0))],
            out_specs=pl.BlockSpec((tile, cols), lambda i, c_ref: (i, 0))),
        compiler_params=_params(("arbitrary",)),
    )(chip, t, r2)


def _pad_to(t, axis, size):
    pads = [(0, 0)] * t.ndim
    pads[axis] = (0, size - t.shape[axis])
    return jnp.pad(t, pads)


_REF_COLS = {"qa": (0, FOX_W), "ka": (FOX_W, FOX_W), "va": (2 * FOX_W, FOX_W), "f": (3 * FOX_W, N_FOX_HEADS)}
_REF_COLS.update({n: (3 * FOX_W + N_FOX_HEADS + i * DIL_W, DIL_W) for i, n in enumerate(("qb", "kb", "vb"))})
_REF_COLS.update({n: (3 * FOX_W + N_FOX_HEADS + 3 * DIL_W + i * D, D) for i, n in enumerate(("ga", "gb"))})
_REF_ORDER = ("qa", "ka", "va", "f", "qb", "kb", "vb", "ga", "gb")


def _place_cols(sources, src_of, out_cols, name, row_block=512):
    arrays = [s[0] if isinstance(s, tuple) else s for s in sources]
    widths = [a.shape[-1] for a in arrays]
    rows = arrays[0].shape[-2]
    plan = []
    for t in range(out_cols // LANES):
        segs, c, end = [], t * LANES, (t + 1) * LANES
        while c < end:
            s = src_of(c)
            if s is None:
                c += 1
                continue
            n = 1
            while c + n < end and src_of(c + n) == (s[0], s[1] + n):
                n += 1
            segs.append((s[0], s[1], c - t * LANES, n))
            c += n
        plan.append(segs)

    def body(*refs):
        o_ref = refs[-1]
        for t, segs in enumerate(plan):
            acc = None
            for si, c0, o0, n in segs:
                a0 = c0 // LANES * LANES
                wide = min(2 * LANES, widths[si] - a0)
                win = refs[si][0, :, a0:a0 + wide] if isinstance(sources[si], tuple) else refs[si][:, a0:a0 + wide]
                r = lax.broadcasted_iota(jnp.int32, (wide, LANES), 0)
                c = lax.broadcasted_iota(jnp.int32, (wide, LANES), 1)
                pick = ((r - (c0 - a0) == c - o0) & (c >= o0) & (c < o0 + n)).astype(bf16)
                part = jnp.dot(win.astype(bf16), pick, preferred_element_type=f32)
                acc = part if acc is None else acc + part
            tile = jnp.zeros((row_block, LANES), f32) if acc is None else acc
            o_ref[:, t * LANES:(t + 1) * LANES] = tile.astype(o_ref.dtype)

    def spec(s):
        if isinstance(s, tuple):
            j = s[1]
            return pl.BlockSpec((1, row_block, s[0].shape[-1]), lambda i: (j, i, 0))
        return pl.BlockSpec((row_block, s.shape[-1]), lambda i: (i, 0))

    return pl.pallas_call(
        body, grid=(rows // row_block,), in_specs=[spec(s) for s in sources],
        out_specs=pl.BlockSpec((row_block, out_cols), lambda i: (i, 0)), out_shape=SDS((rows, out_cols), bf16),
        name=name, compiler_params=_params(("parallel",)),
    )(*arrays)


def _ref_piece(r):
    for name in _REF_ORDER:
        lo, width = _REF_COLS[name]
        if lo <= r < lo + width:
            return name, r - lo
    raise ValueError(r)


def _shard_pad_cols(pieces):
    names = [n for n in _REF_ORDER if n != "vb"]
    sources = [pieces[n] for n in names] + list(pieces["vb"])

    def src_of(c):
        j, i = divmod(c, W_IN_PAD)
        if i >= W_IN_SH:
            return None
        name, col = _ref_piece(j * W_IN_SH + i)
        if name == "vb":
            return len(names) + col // DIL_OUT_W, col % DIL_OUT_W
        return names.index(name), col

    return _place_cols(sources, src_of, N_DEV * W_IN_PAD, "place_dproj")


_SLABS = {"ga": C_GA, "gb": C_GB, "qb": C_QB, "kb": C_KB, "vb": C_VB, "qa": C_QA, "ka": C_KA, "va": C_VA, "f": C_F}


def _slab_w_in(stack):
    def src_of(c):
        for name, start in _SLABS.items():
            lo, width = _REF_COLS[name]
            if start <= c < start + width:
                return divmod(lo + c - start, W_IN_SH)
        return None

    return _place_cols([(stack, j) for j in range(N_DEV)], src_of, PROJ_W, "place_w_in")


def kernel(x, c, w_ada, b_ada, g_mix, w_in, b_fgate, w_br_a, w_br_b, w_out, g_ffn, w_ffn_gate, w_ffn_up, w_ffn_down, g_final, loss_target, m_w_ada, m_b_ada, m_g_mix, m_w_in, m_b_fgate, m_w_br_a, m_w_br_b, m_w_out, m_g_ffn, m_w_ffn_gate, m_w_ffn_up, m_w_ffn_down, m_g_final, v_w_ada, v_b_ada, v_g_mix, v_w_in, v_b_fgate, v_w_br_a, v_w_br_b, v_w_out, v_g_ffn, v_w_ffn_gate, v_w_ffn_up, v_w_ffn_down, v_g_final):
    px, py, pc = _position()
    dev = 4 * px + 2 * py + pc
    x2d, tgt = x[0], loss_target[0]

    c_all = _all_gather(c, "gather_c").reshape(N_DEV, D)
    ada_cols = w_ada.shape[2]
    b_shard = lax.dynamic_slice(b_ada, (0, dev * ada_cols), (1, ada_cols))
    mod_shard = _ada_fwd(c_all, w_ada[0], b_shard)
    mod_all = _all_gather(mod_shard, "gather_mod")
    modv = lax.dynamic_index_in_dim(mod_all, dev, axis=1, keepdims=False).reshape(6, D)
    h1 = _pre1(x2d, modv, g_mix)

    w_in_s = _all_gather(_pad_to(w_in[0], 1, W_IN_PAD).astype(bf16), "gather_w_in")
    gate_up = jnp.concatenate([_pad_to(w_ffn_gate[0], 1, FF_PAD), _pad_to(w_ffn_up[0], 1, FF_PAD)], axis=1)
    later = [w_br_a[0], w_br_b[0], w_out[0], gate_up, _pad_to(w_ffn_down[0], 0, FF_PAD)]
    later_state, later_token = _gather_start([t.astype(bf16) for t in later], w_in_s, "gather_rest_start")
    w_in_p = _slab_w_in(w_in_s)

    proj = _matmul(h1, w_in_p, name="mm_proj", tm=SEQ, tn=896, tk=D, after=later_token)
    b_pad = jnp.pad(b_fgate, ((0, 0), (0, LANES - N_FOX_HEADS)))
    q_aug, k_aug, va = _fox_prep(proj, _fox_gate_fwd(proj, b_pad))
    ya_h, lse_a = _fox_fwd(q_aug, k_aug, va)

    tables = _rope_tables()
    qb_r, kb_r, vb = _rope_fwd(proj, tables)
    by_group = [_dil_fwd(qb_r, kb_r, vb, grp) for grp in range(N_GROUPS)]
    yb_h, lse_b = _dil_combine([o for o, _ in by_group], [l for _, l in by_group])

    mine, arrived = _gather_wait(later_state, yb_h, "gather_rest_wait")
    w_a_s, w_b_s, w_o_s, w_gu_s, w_d_s = [
        lax.dynamic_update_slice(stack, block[None], (dev, 0, 0))
        for stack, block in zip(_gather_finish(arrived, "gather_rest_finish"), mine, strict=True)]
    w_o = w_o_s.reshape(D, D)
    w_d = w_d_s.reshape(FF_HID, D)
    ya = _matmul_stack(ya_h, w_a_s, name="mm_br_a")
    yb = _matmul_stack(yb_h, w_b_s, name="mm_br_b")

    merged = _merge_fwd(ya, yb, proj)
    mix = _matmul(merged, w_o, name="mm_out", tm=SEQ, tn=512, tk=D)
    x1, h2 = _post1(x2d, mix, modv, g_ffn)
    act, au = _ffn_in(h2, w_gu_s)
    ff = _matmul(act, w_d, name="mm_ffn_down", tm=SEQ // 2, tn=512, tk=FF_HID)

    dx2, dff, dg_final, dga_f, loss_lanes = _final(x1, ff, tgt, modv, g_final.reshape(1, D))
    dau = _ffn_bwd_in(dff, w_d_s, au)

    core = pc.astype(jnp.int32).reshape(1)
    chip = (2 * px + py).astype(jnp.int32).reshape(1)

    def pair_done(state, after, tags, name):
        mine, theirs = _exchange_wait("pair", state, after, "pair_wait_" + name)
        sums = [_pair_add(g, r, core, "pair_add_" + t) for g, r, t in zip(mine, theirs, tags)]
        return _exchange_start("chip", sums, "chip_start_" + name)

    def from_chips(state, after, tags, name):
        sums, got = _exchange_wait("chip", state, after, "chip_wait_" + name)
        return [_chip_add(p, r, chip, "chip_add_" + t) for p, r, t in zip(sums, got, tags)]

    g_gu = _matmul(h2, dau, ta=True, by_shard=True, out_dtype=bf16, name="mm_g_ffn_in", tm=D, tn=2 * FF_PAD, tk=SEQ)
    g_d = _matmul(act, dff, ta=True, out_dtype=bf16, name="mm_g_down", tm=FF_HID // 2, tn=512, tk=SEQ)
    ffn_tags = ["gu", "down"]
    ffn_pair, ffn_pair_token = _exchange_start("pair", [g_gu, g_d.reshape(N_DEV, FF_PAD, D)], "pair_start_ffn")

    dx1, dmix, dsh_f, dsc_f, dg_ffn, dga_m = _mid_bwd(dau, w_gu_s, ffn_pair_token, x1, dx2, mix, modv, g_ffn)
    ffn_state, ffn_token = pair_done(ffn_pair, dx1, ffn_tags, "ffn")
    dmerged = _matmul(dmix, w_o, tb=True, name="mm_d_merged", tm=SEQ, tn=512, tk=D, after=ffn_token)
    dya, dyb, dga, dgb = _merge_bwd(dmerged, ya, yb, proj)
    dya_h = _matmul_stack(dya, w_a_s, tb=True, name="mm_d_ya")
    dyb_h = _matmul_stack(dyb, w_b_s, tb=True, name="mm_d_yb")

    g_o = _matmul(merged, dmix, ta=True, out_dtype=bf16, name="mm_g_out", tm=D, tn=512, tk=SEQ)
    g_a = _matmul_stack(ya_h, dya, ta=True, out_dtype=bf16, name="mm_g_br_a")
    g_b = _matmul_stack(yb_h, dyb, ta=True, out_dtype=bf16, name="mm_g_br_b")
    rows_a, rows_b = FOX_W * W_BR_SH // D, DIL_OUT_W * W_BR_SH // D
    g_small = jnp.concatenate([g_a.reshape(N_DEV, rows_a, D), g_b.reshape(N_DEV, rows_b, D),
                               g_o.reshape(N_DEV, W_BR_SH, D)], axis=1)
    small_pair, small_pair_token = _exchange_start("pair", [g_small], "pair_start_small")

    dqa, dka, dva, dF = _fox_bwd(q_aug, k_aug, va, dya_h, ya_h, lse_a, small_pair_token)
    dF_row = jnp.pad(dF[:, :2, :].reshape(N_FOX_HEADS, SEQ), ((0, LANES - N_FOX_HEADS), (0, 0)))
    df, db_fgate = _fox_gate_bwd(dF_row, proj, b_pad)
    small_state, small_token = pair_done(small_pair, df, ["small"], "small")

    delta_b = _dil_delta(dyb_h, yb_h)
    dil_grads = [_dil_bwd(qb_r, kb_r, vb, dyb_h, lse_b, delta_b, grp) for grp in range(N_GROUPS)]
    dqb, dkb = _rope_bwd([t[0] for t in dil_grads], [t[1] for t in dil_grads], tables)

    dproj = _shard_pad_cols({"qa": dqa, "ka": dka, "va": dva, "f": df, "qb": dqb, "kb": dkb,
                             "vb": [t[2] for t in dil_grads], "ga": dga, "gb": dgb})
    g_in = _matmul(h1, dproj, ta=True, by_shard=True, out_dtype=bf16, name="mm_g_in", tm=D, tn=W_IN_PAD, tk=SEQ,
                   after=small_token)
    mix_tags = ["in"]
    mix_pair, mix_pair_token = _exchange_start("pair", [g_in], "pair_start_mixer")

    grad_x, dsh_m, dsc_m, dg_mix = _first_bwd(dproj, w_in_s, mix_pair_token, x2d, dx1, modv, g_mix)

    pad_lane = lambda t: jnp.pad(t, ((0, 0), (0, D - t.shape[1])))
    small = jnp.concatenate([dsh_m, dsc_m, dga_m, dsh_f, dsc_f, dga_f, dg_mix, dg_ffn, dg_final,
                             pad_lane(db_fgate), loss_lanes, jnp.zeros((SMALL_ROWS - 11, D), f32)], axis=0)
    small_all = _all_gather(small, "gather_small")
    mix_state, mix_token = pair_done(mix_pair, small_all, mix_tags, "mixer")

    small_sum, loss_row = _small_reduce(small_all, mix_token)
    dmod_all = small_all[:, :6, :].reshape(N_DEV, 6 * D)
    g_w_ada = _ada_bwd(c_all, lax.dynamic_slice(dmod_all, (0, dev * ada_cols), (N_DEV, ada_cols)))
    s_gu, s_d = from_chips(ffn_state, small_sum, ffn_tags, "ffn")
    s_small, = from_chips(small_state, small_sum, ["small"], "small")

    loss = loss_row[0, 0]
    g = {
        "w_ada": g_w_ada[None], "b_ada": small_sum[0:6].reshape(1, 6 * D), "g_mix": small_sum[6:7],
        "b_fgate": small_sum[9:10, :N_FOX_HEADS], "g_ffn": small_sum[7:8], "w_ffn_gate": s_gu[None, :, :W_FF_SH],
        "w_ffn_up": s_gu[None, :, FF_PAD:FF_PAD + W_FF_SH], "w_ffn_down": s_d[None, :W_FF_SH],
        "g_final": small_sum[8], "w_br_a": s_small[:rows_a].reshape(1, FOX_W, W_BR_SH),
        "w_br_b": s_small[rows_a:rows_a + rows_b].reshape(1, DIL_OUT_W, W_BR_SH), "w_out": s_small[None, rows_a + rows_b:],
    }
    w = {"w_ada": w_ada, "b_ada": b_ada, "g_mix": g_mix, "w_in": w_in, "b_fgate": b_fgate, "w_br_a": w_br_a,
         "w_br_b": w_br_b, "w_out": w_out, "g_ffn": g_ffn, "w_ffn_gate": w_ffn_gate, "w_ffn_up": w_ffn_up,
         "w_ffn_down": w_ffn_down, "g_final": g_final}
    m = {"w_ada": m_w_ada, "b_ada": m_b_ada, "g_mix": m_g_mix, "w_in": m_w_in, "b_fgate": m_b_fgate,
         "w_br_a": m_w_br_a, "w_br_b": m_w_br_b, "w_out": m_w_out, "g_ffn": m_g_ffn, "w_ffn_gate": m_w_ffn_gate,
         "w_ffn_up": m_w_ffn_up, "w_ffn_down": m_w_ffn_down, "g_final": m_g_final}
    v = {"w_ada": v_w_ada, "b_ada": v_b_ada, "g_mix": v_g_mix, "w_in": v_w_in, "b_fgate": v_b_fgate,
         "w_br_a": v_w_br_a, "w_br_b": v_w_br_b, "w_out": v_w_out, "g_ffn": v_g_ffn, "w_ffn_gate": v_w_ffn_gate,
         "w_ffn_up": v_w_ffn_up, "w_ffn_down": v_w_ffn_down, "g_final": v_g_final}
    names = list(w)
    delta, new_m, new_v = {}, {}, {}

    transposed = ("w_in", "w_ffn_gate", "w_ffn_up")

    def update(n):
        shape = w[n].shape
        if n in transposed:
            g_t = g[n][0].T
            dl, mn, vn = _adamw(w[n][0].T, g_t, m[n][0].T, v[n][0].T, "adamw_" + n)
            g[n], delta[n], new_m[n], new_v[n] = g_t.T[None], dl.T[None], mn.T[None], vn.T[None]
            return
        two_d = (lambda t: t.reshape(shape[-2:])) if len(shape) == 3 else (lambda t: t)
        dl, mn, vn = _adamw(two_d(w[n]), two_d(g[n]), two_d(m[n]), two_d(v[n]), "adamw_" + n)
        delta[n], new_m[n], new_v[n] = dl.reshape(shape), mn.reshape(shape), vn.reshape(shape)

    for n in list(g):
        update(n)
    done = sum(delta[n].reshape(-1)[:N_FOX_HEADS] for n in g)
    s_in, = from_chips(mix_state, done, mix_tags, "mixer")
    g["w_in"] = s_in[None, :, :W_IN_SH]
    update("w_in")

    return (loss, grad_x[None], *[g[n] for n in names], *[delta[n] for n in names],
            *[new_m[n] for n in names], *[new_v[n] for n in names])
```

```python
import functools

import jax
import jax.numpy as jnp
from jax import lax
from jax.experimental import pallas as pl
from jax.experimental.pallas import tpu as pltpu

f32 = jnp.float32
bf16 = jnp.bfloat16
SDS = jax.ShapeDtypeStruct
MESH = pl.DeviceIdType.MESH

N_DEV = 8
D = 1024
SEQ = 2048
HEAD_DIM = 64
N_FOX_HEADS = 8
FOX_W = 512
DIL_W = 768
DIL_OUT_W = 256
ROT_DIM = 16
ROPE_THETA = 500000.0
D_FF = 2816
IN_COLS = 5896
EPS = 1e-6
NEG = -1e30
ATT_SCALE = HEAD_DIM ** -0.5

ADAM_LR = 0.001
ADAM_B1 = 0.9
ADAM_B2 = 0.999
ADAM_EPS = 1e-08
ADAM_WD = 0.01
ADAM_STEP = 10

C_GA, C_GB, C_QB, C_KB, C_VB, C_QA, C_KA, C_VA, C_F = 0, 1024, 2304, 3072, 3840, 4608, 5120, 5632, 6144
PROJ_W = 6272
LANES = 128
VMEM_LIMIT = 52 * 1024 * 1024

W_IN_SH, W_IN_PAD = IN_COLS // N_DEV, 768
W_BR_SH = D // N_DEV
W_FF_SH, FF_PAD = D_FF // N_DEV, 384
FF_HID = N_DEV * FF_PAD
SMALL_ROWS = 16


def _params(sem=None):
    if sem is None:
        return pltpu.CompilerParams(vmem_limit_bytes=VMEM_LIMIT)
    return pltpu.CompilerParams(dimension_semantics=sem, vmem_limit_bytes=VMEM_LIMIT)


def _rowwise(fn, name, tiled, vecs, outs, reds=(), tile=256):
    nt, nv, no = len(tiled), len(vecs), len(outs)
    rows = tiled[0][0].shape[0]
    assert rows % tile == 0

    def body(*refs):
        tin = [r[...] for r in refs[:nt]]
        vin = [r[...] for r in refs[nt:nt + nv]]
        orefs = refs[nt + nv:nt + nv + no]
        rrefs = refs[nt + nv + no:]
        touts, routs = fn(tin, vin)
        for r, t in zip(orefs, touts, strict=True):
            r[...] = t.astype(r.dtype)
        if rrefs:
            @pl.when(pl.program_id(0) == 0)
            def _():
                for r in rrefs:
                    r[...] = jnp.zeros_like(r)
            for r, t in zip(rrefs, routs, strict=True):
                r[...] += t

    def col_map(cb):
        return lambda i: (i, cb)

    def whole_map(nd):
        return lambda i: (0,) * nd

    in_specs = [pl.BlockSpec((tile, w), col_map(cb)) for (_, w, cb) in tiled]
    in_specs += [pl.BlockSpec(v.shape, whole_map(v.ndim)) for v in vecs]
    out_specs = [pl.BlockSpec((tile, w), lambda i: (i, 0)) for (w, _) in outs]
    out_specs += [pl.BlockSpec((1, w), lambda i: (0, 0)) for w in reds]
    out_shape = [SDS((rows, w), dt) for (w, dt) in outs] + [SDS((1, w), f32) for w in reds]
    res = pl.pallas_call(
        body, grid=(rows // tile,), in_specs=in_specs, out_specs=out_specs, out_shape=out_shape, name=name,
        compiler_params=_params(("arbitrary",)),
    )(*[t[0] for t in tiled], *vecs)
    return res


def _matmul(a, b, *, ta=False, tb=False, out_dtype=f32, name, tm, tn, tk, by_shard=False, after=None):
    m, k = (a.shape[1], a.shape[0]) if ta else a.shape
    if by_shard and not ta:
        n, kb = (b.shape[1], N_DEV * b.shape[2]) if tb else (N_DEV * b.shape[2], b.shape[1])
        assert (tk if tb else tn) == b.shape[2]
    else:
        n, kb = (b.shape[0], b.shape[1]) if tb else (b.shape[1], b.shape[0])
    assert kb == k and m % tm == 0 and n % tn == 0 and k % tk == 0
    nk = k // tk
    dims = (((0 if ta else 1,), (1 if tb else 0,)), ((), ()))
    b_stacked = by_shard and not ta
    o_stacked = by_shard and ta

    def body(a_ref, b_ref, *rest):
        o_ref, *acc = rest[1:] if after is not None else rest
        bv = b_ref[0] if b_stacked else b_ref[...]
        p = lax.dot_general(a_ref[...].astype(bf16), bv.astype(bf16), dims, preferred_element_type=f32)

        def put(val):
            if o_stacked:
                o_ref[0] = val.astype(o_ref.dtype)
            else:
                o_ref[...] = val.astype(o_ref.dtype)

        if nk == 1:
            put(p)
        else:
            acc_ref, = acc
            kk = pl.program_id(2)

            @pl.when(kk == 0)
            def _():
                acc_ref[...] = p

            @pl.when(kk > 0)
            def _():
                acc_ref[...] += p

            @pl.when(kk == nk - 1)
            def _():
                put(acc_ref[...])

    a_spec = pl.BlockSpec((tk, tm), lambda i, j, kk: (kk, i)) if ta else pl.BlockSpec((tm, tk), lambda i, j, kk: (i, kk))
    if b_stacked and tb:
        b_spec = pl.BlockSpec((1, tn, tk), lambda i, j, kk: (kk, j, 0))
    elif b_stacked:
        b_spec = pl.BlockSpec((1, tk, tn), lambda i, j, kk: (j, kk, 0))
    elif tb:
        b_spec = pl.BlockSpec((tn, tk), lambda i, j, kk: (j, kk))
    else:
        b_spec = pl.BlockSpec((tk, tn), lambda i, j, kk: (kk, j))
    if o_stacked:
        assert tn == n // N_DEV
        out_spec = pl.BlockSpec((1, tm, tn), lambda i, j, kk: (j, i, 0))
        out_shape = SDS((N_DEV, m, tn), out_dtype)
    else:
        out_spec = pl.BlockSpec((tm, tn), lambda i, j, kk: (i, j))
        out_shape = SDS((m, n), out_dtype)
    extra_specs, extra = ([pl.BlockSpec(memory_space=pl.ANY)], [after]) if after is not None else ([], [])
    return pl.pallas_call(
        body, grid=(m // tm, n // tn, nk), in_specs=[a_spec, b_spec] + extra_specs, out_specs=out_spec,
        out_shape=out_shape, name=name, scratch_shapes=[pltpu.VMEM((tm, tn), f32)] if nk > 1 else [],
        compiler_params=_params(("parallel", "parallel", "arbitrary")),
    )(a, b, *extra)


def _matmul_stack(a, b, *, ta=False, tb=False, out_dtype=f32, name):
    def lanes(ref):
        return jnp.concatenate([ref[j] for j in range(N_DEV)], axis=1).astype(bf16)

    if ta:
        w = b.shape[1] // N_DEV

        def body(a_ref, b_ref, o_ref):
            p = _tn(a_ref[...].astype(bf16), b_ref[...].astype(bf16))
            for j in range(N_DEV):
                o_ref[j] = p[:, j * w:(j + 1) * w].astype(o_ref.dtype)

        return pl.pallas_call(body, out_shape=SDS((N_DEV, a.shape[1], w), out_dtype), name=name,
                              compiler_params=_params())(a, b)

    m, half = a.shape[0], a.shape[0] // 2
    n = b.shape[1] if tb else N_DEV * b.shape[2]

    def body(a_ref, b_ref, o_ref):
        av = a_ref[...].astype(bf16)
        o_ref[...] = (_nt(av, lanes(b_ref)) if tb else jnp.dot(av, lanes(b_ref), preferred_element_type=f32)
                      ).astype(o_ref.dtype)

    return pl.pallas_call(
        body, grid=(2,), in_specs=[pl.BlockSpec((half, a.shape[1]), lambda i: (i, 0)),
                                   pl.BlockSpec(b.shape, lambda i: (0, 0, 0))],
        out_specs=pl.BlockSpec((half, n), lambda i: (i, 0)), out_shape=SDS((m, n), out_dtype), name=name,
        compiler_params=_params(("parallel",)),
    )(a, b)


def _matmul_nt_shards(a, b, after, fn, tiled, vecs, outs, reds, *, name, tm=512):
    m, n, w = a.shape[0], b.shape[1], b.shape[2]
    assert a.shape[1] == N_DEV * w and m % tm == 0
    nt, nv, no = len(tiled), len(vecs), len(outs)

    def body(a_ref, b_ref, after_ref, *refs):
        acc = _nt(a_ref[:, 0:w], b_ref[0])
        for j in range(1, N_DEV):
            acc = acc + _nt(a_ref[:, j * w:(j + 1) * w], b_ref[j])
        orefs, rrefs = refs[nt + nv:nt + nv + no], refs[nt + nv + no:]
        touts, routs = fn([acc] + [r[...] for r in refs[:nt]], [r[...] for r in refs[nt:nt + nv]])
        for r, t in zip(orefs, touts, strict=True):
            r[...] = t.astype(r.dtype)

        @pl.when(pl.program_id(0) == 0)
        def _():
            for r in rrefs:
                r[...] = jnp.zeros_like(r)
        for r, t in zip(rrefs, routs, strict=True):
            r[...] += t

    def whole_map(nd):
        return lambda i: (0,) * nd

    rows = lambda width: pl.BlockSpec((tm, width), lambda i: (i, 0))
    return pl.pallas_call(
        body, grid=(m // tm,),
        in_specs=[rows(N_DEV * w), pl.BlockSpec((N_DEV, n, w), lambda i: (0, 0, 0), pipeline_mode=pl.Buffered(1)),
                  pl.BlockSpec(memory_space=pl.ANY)]
        + [rows(t.shape[1]) for t in tiled] + [pl.BlockSpec(v.shape, whole_map(v.ndim)) for v in vecs],
        out_specs=[rows(width) for width, _ in outs] + [pl.BlockSpec((1, width), lambda i: (0, 0)) for width in reds],
        out_shape=[SDS((m, width), dt) for width, dt in outs] + [SDS((1, width), f32) for width in reds], name=name,
        compiler_params=_params(("arbitrary",)),
    )(a, b, after, *tiled, *vecs)


def _rms(x):
    r = lax.rsqrt(jnp.mean(x * x, axis=-1, keepdims=True) + EPS)
    return r, x * r


def _rms_bwd(r, xn, dxn):
    return r * (dxn - xn * jnp.mean(dxn * xn, axis=-1, keepdims=True))


def _colsum(t):
    return jnp.sum(t, axis=0, keepdims=True)


def _sigmoid(x):
    return 1.0 / (1.0 + jnp.exp(-x))


def _modulated_norm(x, g, shift, scale):
    _, xn = _rms(x)
    return (xn * g) * (1.0 + scale) + shift


def _pre1(x, modv, g_mix):
    def fn(t, v):
        (xt,), (mv, g) = t, v
        return [_modulated_norm(xt, g, mv[0:1], mv[1:2])], []
    return _rowwise(fn, "pre1", [(x, D, 0)], [modv, g_mix], [(D, bf16)])[0]


def _post1(x, mix, modv, g_ffn):
    def fn(t, v):
        (xt, mt), (mv, g) = t, v
        x1 = xt + mv[2:3] * mt
        return [x1, _modulated_norm(x1, g, mv[3:4], mv[4:5])], []
    return _rowwise(fn, "post1", [(x, D, 0), (mix, D, 0)], [modv, g_ffn], [(D, f32), (D, bf16)])


def _ffn_in(h, w_stack):
    def body(h_ref, w_ref, act_ref, au_ref):
        p = jnp.dot(h_ref[...], w_ref[0], preferred_element_type=f32)
        a, u = p[:, :FF_PAD], p[:, FF_PAD:]
        act_ref[...] = (a * _sigmoid(a) * u).astype(act_ref.dtype)
        au_ref[...] = p.astype(au_ref.dtype)

    return pl.pallas_call(
        body, grid=(N_DEV,),
        in_specs=[pl.BlockSpec((SEQ, D), lambda j: (0, 0)), pl.BlockSpec((1, D, 2 * FF_PAD), lambda j: (j, 0, 0))],
        out_specs=[pl.BlockSpec((SEQ, FF_PAD), lambda j: (0, j)), pl.BlockSpec((SEQ, 2 * FF_PAD), lambda j: (0, j))],
        out_shape=(SDS((SEQ, FF_HID), bf16), SDS((SEQ, 2 * FF_HID), bf16)), name="ffn_in",
        compiler_params=_params(("parallel",)),
    )(h, w_stack)


def _ffn_bwd_in(dff, w_down_stack, au):
    def body(d_ref, w_ref, au_ref, o_ref):
        dact = _nt(d_ref[...], w_ref[0])
        p = au_ref[...].astype(f32)
        a, u = p[:, :FF_PAD], p[:, FF_PAD:]
        sg = _sigmoid(a)
        o_ref[...] = jnp.concatenate([dact * u * (sg * (1.0 + a * (1.0 - sg))), dact * (a * sg)],
                                     axis=1).astype(o_ref.dtype)

    return pl.pallas_call(
        body, grid=(N_DEV,),
        in_specs=[pl.BlockSpec((SEQ, D), lambda j: (0, 0)), pl.BlockSpec((1, FF_PAD, D), lambda j: (j, 0, 0)),
                  pl.BlockSpec((SEQ, 2 * FF_PAD), lambda j: (0, j))],
        out_specs=pl.BlockSpec((SEQ, 2 * FF_PAD), lambda j: (0, j)),
        out_shape=SDS((SEQ, 2 * FF_HID), bf16), name="ffn_bwd_in", compiler_params=_params(("parallel",)),
    )(dff, w_down_stack, au)


def _final(x1, ff, target, modv, g_final):
    def fn(t, v):
        (x1t, fft, tgt), (mv, g) = t, v
        x2 = x1t + mv[5:6] * fft
        r, xn = _rms(x2)
        err = xn * g - tgt
        dy = err * (1.0 / D)
        dx2 = _rms_bwd(r, xn, dy * g)
        return [dx2, dx2 * mv[5:6]], [_colsum(dy * xn), _colsum(dx2 * fft), _colsum(err * err) * (0.5 / D)]
    return _rowwise(fn, "final", [(x1, D, 0), (ff, D, 0), (target, D, 0)], [modv, g_final],
                    [(D, f32), (D, bf16)], [D, D, D])


def _mid_bwd(dau, w_stack, after, x1, dx2, mix, modv, g_ffn):
    def fn(t, v):
        (dh, x1t, dx2t, mt), (mv, g) = t, v
        r, xn = _rms(x1t)
        dn = dh * (1.0 + mv[4:5])
        dx1 = dx2t + _rms_bwd(r, xn, dn * g)
        return [dx1, dx1 * mv[2:3]], [_colsum(dh), _colsum(dh * (xn * g)), _colsum(dn * xn), _colsum(dx1 * mt)]
    return _matmul_nt_shards(dau, w_stack, after, fn, [x1, dx2, mix], [modv, g_ffn], [(D, f32), (D, bf16)],
                             [D, D, D, D], name="mid_bwd")


def _first_bwd(dproj, w_stack, after, x, dx1, modv, g_mix):
    def fn(t, v):
        (dh, xt, dx1t), (mv, g) = t, v
        r, xn = _rms(xt)
        dn = dh * (1.0 + mv[1:2])
        return [dx1t + _rms_bwd(r, xn, dn * g)], [_colsum(dh), _colsum(dh * (xn * g)), _colsum(dn * xn)]
    return _matmul_nt_shards(dproj, w_stack, after, fn, [x, dx1], [modv, g_mix], [(D, f32)], [D, D, D],
                             name="first_bwd")


def _merge_fwd(ya, yb, proj):
    def fn(t, v):
        ya_t, yb_t, ga, gb = t
        return [_sigmoid(ga) * ya_t + _sigmoid(gb) * yb_t], []
    return _rowwise(fn, "merge_fwd", [(ya, D, 0), (yb, D, 0), (proj, D, C_GA // D), (proj, D, C_GB // D)], [],
                    [(D, bf16)])[0]


def _merge_bwd(dmerged, ya, yb, proj):
    def fn(t, v):
        dm, ya_t, yb_t, ga, gb = t
        sa, sb = _sigmoid(ga), _sigmoid(gb)
        return [dm * sa, dm * sb, dm * ya_t * (sa * (1.0 - sa)), dm * yb_t * (sb * (1.0 - sb))], []
    return _rowwise(fn, "merge_bwd",
                    [(dmerged, D, 0), (ya, D, 0), (yb, D, 0), (proj, D, C_GA // D), (proj, D, C_GB // D)], [],
                    [(D, bf16), (D, bf16), (D, bf16), (D, bf16)])


def _rope_tables():
    half = ROT_DIM // 2
    pos = jnp.arange(SEQ, dtype=f32)
    inv_freq = ROPE_THETA ** (-jnp.arange(0, ROT_DIM, 2, dtype=f32) / ROT_DIM)
    ang = pos[:, None] * inv_freq[None, :]
    cos, sin = jnp.cos(ang), jnp.sin(ang)
    pad = jnp.zeros((SEQ, HEAD_DIM - ROT_DIM), f32)
    zero = jnp.zeros((SEQ, half), f32)
    c_head = jnp.concatenate([cos, cos, pad + 1.0], axis=1)
    lo_head = jnp.concatenate([-sin, zero, pad], axis=1)
    hi_head = jnp.concatenate([zero, sin, pad], axis=1)
    return tuple(jnp.concatenate([t, t], axis=1) for t in (c_head, lo_head, hi_head))


def _over_heads(tables):
    return [jnp.tile(t, (1, DIL_W // LANES)) for t in tables]


def _rope_fwd(proj, tables):
    half = ROT_DIM // 2

    def fn(t, v):
        q, k, vv = t[:3]
        c, lo, hi = _over_heads(t[3:])
        rot = lambda z: z * c + pltpu.roll(z, DIL_W - half, 1) * lo + pltpu.roll(z, half, 1) * hi
        return [rot(q) * ATT_SCALE, rot(k), vv], []
    return _rowwise(fn, "rope_fwd", [(proj, DIL_W, C_QB // DIL_W), (proj, DIL_W, C_KB // DIL_W),
                                     (proj, DIL_W, C_VB // DIL_W)] + [(tb, LANES, 0) for tb in tables], [],
                    [(DIL_W, f32)] * 3)


def _rope_bwd(dqs, dks, tables):
    half = ROT_DIM // 2

    def fn(t, v):
        dq_t, dk_t = jnp.concatenate(t[:N_GROUPS], axis=1), jnp.concatenate(t[N_GROUPS:2 * N_GROUPS], axis=1)
        c, lo, hi = _over_heads(t[2 * N_GROUPS:])
        rot_t = lambda z: z * c + pltpu.roll(z * lo, half, 1) + pltpu.roll(z * hi, DIL_W - half, 1)
        return [rot_t(dq_t), rot_t(dk_t)], []
    return _rowwise(fn, "rope_bwd", [(a, DIL_OUT_W, 0) for a in (*dqs, *dks)] + [(tb, LANES, 0) for tb in tables],
                    [], [(DIL_W, bf16), (DIL_W, bf16)])


def _head_bcast_sum(d):
    lane = lax.broadcasted_iota(jnp.int32, d.shape, 1)
    out = jnp.zeros_like(d)
    for h in range(d.shape[1] // HEAD_DIM):
        sel = (lane >= h * HEAD_DIM) & (lane < (h + 1) * HEAD_DIM)
        out = jnp.where(sel, jnp.sum(jnp.where(sel, d, 0.0), axis=1, keepdims=True), out)
    return out


def _dil_combine(outs, lses):
    def fn(t, v):
        o0, o1, o2, l0, l1, l2 = t
        m = jnp.maximum(jnp.maximum(l0, l1), l2)
        w0, w1, w2 = jnp.exp(l0 - m), jnp.exp(l1 - m), jnp.exp(l2 - m)
        tot = w0 + w1 + w2
        return [(w0 * o0 + w1 * o1 + w2 * o2) / tot, m + jnp.log(tot)], []
    w = DIL_OUT_W
    return _rowwise(fn, "dil_combine", [(t, w, 0) for t in (*outs, *lses)], [], [(w, f32), (w, f32)])


def _dil_delta(dyb_h, yb_h):
    def fn(t, v):
        return [_head_bcast_sum(t[0] * t[1])], []
    return _rowwise(fn, "dil_delta", [(dyb_h, DIL_OUT_W, 0), (yb_h, DIL_OUT_W, 0)], [], [(DIL_OUT_W, f32)])[0]


def _adamw_math(wt, gt, mt, vt):
    mn = ADAM_B1 * mt + (1.0 - ADAM_B1) * gt
    vn = ADAM_B2 * vt + (1.0 - ADAM_B2) * (gt * gt)
    m_hat = mn / (1.0 - ADAM_B1 ** ADAM_STEP)
    v_hat = vn / (1.0 - ADAM_B2 ** ADAM_STEP)
    return -ADAM_LR * (m_hat / (jnp.sqrt(v_hat) + ADAM_EPS) + ADAM_WD * wt), mn, vn


def _adamw(w, g, m, v, name):
    shape = w.shape
    if w.ndim == 1:
        w, g, m, v = (t.reshape(1, -1) for t in (w, g, m, v))
    rows, cols = w.shape
    if rows % 8 and rows > 8:
        return _adamw_by_cols(w, g, m, v, name)
    tile = 256 if rows % 256 == 0 and rows > 512 else rows

    def fn(t, _):
        return list(_adamw_math(*t)), []
    delta, mn, vn = _rowwise(fn, name, [(w, cols, 0), (g, cols, 0), (m, cols, 0), (v, cols, 0)], [],
                             [(cols, f32)] * 3, tile=tile)
    return delta.reshape(shape), mn.reshape(shape), vn.reshape(shape)


def _adamw_by_cols(w, g, m, v, name, tile=256):
    rows, cols = w.shape

    def body(w_ref, g_ref, m_ref, v_ref, d_ref, mn_ref, vn_ref):
        d_ref[...], mn_ref[...], vn_ref[...] = _adamw_math(w_ref[...], g_ref[...], m_ref[...], v_ref[...])

    spec = pl.BlockSpec((rows, tile), lambda j: (0, j))
    return pl.pallas_call(body, grid=(cols // tile,), in_specs=[spec] * 4, out_specs=[spec] * 3,
                          out_shape=[SDS((rows, cols), f32)] * 3, name=name,
                          compiler_params=_params(("parallel",)))(w, g, m, v)


def _ada_fwd(c_all, w_shard, b_shard):
    def body(c_ref, w_ref, b_ref, o_ref):
        cv = c_ref[...]
        sc = (cv * _sigmoid(cv)).astype(bf16)
        o_ref[...] = jnp.dot(sc, w_ref[...].astype(bf16), preferred_element_type=f32) + b_ref[...]
    return pl.pallas_call(body, out_shape=SDS((N_DEV, w_shard.shape[1]), f32), name="ada_fwd",
                          compiler_params=_params())(c_all, w_shard, b_shard)


def _ada_bwd(c_all, dmod_cols):
    def body(c_ref, d_ref, o_ref):
        cv = c_ref[...]
        sc = cv * _sigmoid(cv)
        o_ref[...] = lax.dot_general(sc, d_ref[...], (((0,), (0,)), ((), ())), precision=lax.Precision.HIGHEST,
                                     preferred_element_type=f32)
    return pl.pallas_call(body, out_shape=SDS((D, dmod_cols.shape[1]), f32), name="ada_bwd",
                          compiler_params=_params())(c_all, dmod_cols)


def _small_reduce(gathered, after):
    def body(g_ref, after_ref, o_ref, loss_ref):
        acc = g_ref[0]
        for d in range(1, N_DEV):
            acc = acc + g_ref[d]
        o_ref[...] = acc
        loss_ref[...] = jnp.zeros((1, LANES), f32) + jnp.sum(acc[10:11, :])
    return pl.pallas_call(body, out_shape=(SDS((SMALL_ROWS, D), f32), SDS((1, LANES), f32)), name="small_reduce",
                          in_specs=[pl.BlockSpec(memory_space=pltpu.VMEM), pl.BlockSpec(memory_space=pl.ANY)],
                          compiler_params=_params())(gathered, after)


FOX_BLK = 512
CUM_BLK = 128


def _fold_lanes(t, op):
    out = t[:, :LANES]
    for j in range(1, t.shape[1] // LANES):
        out = op(out, t[:, j * LANES:(j + 1) * LANES])
    return out


def _fox_gate_fwd(proj, b_pad):
    nblk = SEQ // CUM_BLK

    def body(f_ref, b_ref, col_ref):
        r = lax.broadcasted_iota(jnp.int32, (CUM_BLK, CUM_BLK), 0)
        c = lax.broadcasted_iota(jnp.int32, (CUM_BLK, CUM_BLK), 1)
        tri = (r >= c).astype(f32)
        carry = jnp.zeros((1, LANES), f32)
        for blk in range(nblk):
            z = f_ref[blk * CUM_BLK:(blk + 1) * CUM_BLK, :] + b_ref[...]
            logf = jnp.minimum(z, 0.0) - jnp.log1p(jnp.exp(-jnp.abs(z)))
            cs = jnp.dot(tri, logf, precision=lax.Precision.HIGHEST, preferred_element_type=f32) + carry
            col_ref[blk * CUM_BLK:(blk + 1) * CUM_BLK, :] = cs
            carry = cs[CUM_BLK - 1:CUM_BLK, :]

    return pl.pallas_call(
        body, grid=(1,), in_specs=[pl.BlockSpec((SEQ, LANES), lambda i: (0, C_F // LANES)),
                                   pl.BlockSpec((1, LANES), lambda i: (0, 0))],
        out_specs=pl.BlockSpec((SEQ, LANES), lambda i: (0, 0)),
        out_shape=SDS((SEQ, LANES), f32), name="fox_gate_fwd",
        compiler_params=_params(("arbitrary",)),
    )(proj, b_pad)


def _fox_gate_bwd(dF_row, proj, b_pad):
    nblk = SEQ // CUM_BLK

    def body(d_ref, f_ref, b_ref, df_ref, db_ref, col_ref):
        r = lax.broadcasted_iota(jnp.int32, (CUM_BLK, CUM_BLK), 0)
        c = lax.broadcasted_iota(jnp.int32, (CUM_BLK, CUM_BLK), 1)
        tri = (r <= c).astype(f32)
        lane = lax.broadcasted_iota(jnp.int32, (CUM_BLK, LANES), 1)
        col_ref[...] = d_ref[...].T
        carry = jnp.zeros((1, LANES), f32)
        total = jnp.zeros((1, LANES), f32)
        for blk in reversed(range(nblk)):
            rows = slice(blk * CUM_BLK, (blk + 1) * CUM_BLK)
            cs = jnp.dot(tri, col_ref[rows, :], precision=lax.Precision.HIGHEST, preferred_element_type=f32) + carry
            carry = cs[0:1, :]
            z = f_ref[rows, :] + b_ref[...]
            df = jnp.where(lane < N_FOX_HEADS, cs * _sigmoid(-z), 0.0)
            df_ref[rows, :] = df.astype(df_ref.dtype)
            total = total + _colsum(df)
        db_ref[...] = total

    return pl.pallas_call(
        body, grid=(1,), in_specs=[pl.BlockSpec((LANES, SEQ), lambda i: (0, 0)),
                                   pl.BlockSpec((SEQ, LANES), lambda i: (0, C_F // LANES)),
                                   pl.BlockSpec((1, LANES), lambda i: (0, 0))],
        out_specs=[pl.BlockSpec((SEQ, LANES), lambda i: (0, 0)), pl.BlockSpec((1, LANES), lambda i: (0, 0))],
        out_shape=(SDS((SEQ, LANES), bf16), SDS((1, LANES), f32)), name="fox_gate_bwd",
        scratch_shapes=[pltpu.VMEM((SEQ, LANES), f32)],
        compiler_params=_params(("arbitrary",)),
    )(dF_row, proj, b_pad)


def _nt(a, b):
    return lax.dot_general(a, b, (((1,), (1,)), ((), ())), preferred_element_type=f32)


def _tn(a, b):
    return lax.dot_general(a, b, (((0,), (0,)), ((), ())), preferred_element_type=f32)


def _fox_prep(proj, f_col):
    def fn(t, v):
        q, k, vv, fc = t
        lane = lax.broadcasted_iota(jnp.int32, (q.shape[0], LANES), 1)
        qs, ks = [], []
        for h in range(N_FOX_HEADS):
            pair, pos = divmod(h, 2)
            own = (lane >= pos * HEAD_DIM) & (lane < (pos + 1) * HEAD_DIM)
            base = (1 - pos) * HEAD_DIM
            f = fc[:, h:h + 1]
            hi = f.astype(bf16).astype(f32)
            mid = (f - hi).astype(bf16).astype(f32)
            lo = (f - hi) - mid
            one = jnp.ones_like(f)
            qa = jnp.where(own, q[:, pair * LANES:(pair + 1) * LANES] * ATT_SCALE, 0.0)
            ka = k[:, pair * LANES:(pair + 1) * LANES]
            for idx, (qv, kv) in enumerate([(hi, one), (mid, one), (lo, one), (one, -hi), (one, -mid), (one, -lo)]):
                sel = lane == base + idx
                qa = jnp.where(sel, qv, qa)
                ka = jnp.where(sel, kv, ka)
            qs.append(qa)
            ks.append(ka)
        return [jnp.concatenate(qs, axis=1), jnp.concatenate(ks, axis=1), vv], []
    w = N_FOX_HEADS * LANES
    return _rowwise(fn, "fox_prep", [(proj, FOX_W, C_QA // FOX_W), (proj, FOX_W, C_KA // FOX_W),
                                     (proj, FOX_W, C_VA // FOX_W), (f_col, LANES, 0)], [],
                    [(w, bf16), (w, bf16), (FOX_W, bf16)])


def _fox_fwd(q_aug, k_aug, v):
    blk = FOX_BLK
    npair = FOX_W // LANES

    def body(q_ref, k_ref, v_ref, o_ref, max_ref, sum_ref, s_scr):
        i = pl.program_id(1)
        tri = lax.broadcasted_iota(jnp.int32, (blk, blk), 0) >= lax.broadcasted_iota(jnp.int32, (blk, blk), 1)
        qh = [q_ref[:, h * LANES:(h + 1) * LANES] for h in range(2)]

        def logits(c, masked):
            off = pl.multiple_of(c * blk, blk)
            tops = []
            for h in range(2):
                s = _nt(qh[h], k_ref[pl.ds(off, blk), h * LANES:(h + 1) * LANES])
                if masked:
                    s = jnp.where(tri, s, NEG)
                s_scr[h, :, pl.ds(off, blk)] = s
                tops.append(_fold_lanes(s, jnp.maximum))
            return tops

        def pass_a(c, m):
            return tuple(jnp.maximum(a, b) for a, b in zip(m, logits(c, False)))

        m = lax.fori_loop(0, i, pass_a, tuple(jnp.full((blk, LANES), NEG, f32) for _ in range(2)))
        mx = [jnp.max(jnp.maximum(a, b), axis=1, keepdims=True) for a, b in zip(m, logits(i, True))]

        def pass_b(c, carry):
            off = pl.multiple_of(c * blk, blk)
            vv = v_ref[pl.ds(off, blk), :]
            new = []
            for h in range(2):
                l, acc = carry[h]
                p = jnp.exp(s_scr[h, :, pl.ds(off, blk)] - mx[h]).astype(bf16)
                new.append((l + _fold_lanes(p.astype(f32), jnp.add), acc + jnp.dot(p, vv, preferred_element_type=f32)))
            return tuple(new)

        zero = jnp.zeros((blk, LANES), f32)
        (l_a, acc_a), (l_b, acc_b) = lax.fori_loop(0, i + 1, pass_b, ((zero, zero), (zero, zero)))
        l_a = jnp.sum(l_a, axis=1, keepdims=True)
        l_b = jnp.sum(l_b, axis=1, keepdims=True)
        first = lax.broadcasted_iota(jnp.int32, (blk, LANES), 1) < HEAD_DIM
        o_ref[...] = jnp.where(first, acc_a / l_a, acc_b / l_b)
        max_ref[0] = jnp.where(first, mx[0], mx[1])
        sum_ref[0] = jnp.where(first, l_a, l_b)

    return pl.pallas_call(
        body, grid=(npair, SEQ // blk),
        in_specs=[pl.BlockSpec((blk, 2 * LANES), lambda p, i: (i, p)),
                  pl.BlockSpec((SEQ, 2 * LANES), lambda p, i: (0, p)),
                  pl.BlockSpec((SEQ, LANES), lambda p, i: (0, p))],
        out_specs=[pl.BlockSpec((blk, LANES), lambda p, i: (i, p))]
        + [pl.BlockSpec((1, blk, LANES), lambda p, i: (p, i, 0))] * 2,
        out_shape=(SDS((SEQ, FOX_W), f32),) + (SDS((npair, SEQ, LANES), f32),) * 2, name="fox_fwd",
        scratch_shapes=[pltpu.VMEM((2, blk, SEQ), f32)],
        compiler_params=_params(("parallel", "arbitrary")),
    )(q_aug, k_aug, v)


def _fox_bwd(q_aug, k_aug, v, do, o, row_max, row_sum, after):
    blk = FOX_BLK
    npair = FOX_W // LANES
    nblk = SEQ // blk

    def body(q_ref, k_ref, v_ref, do_ref, o_ref, max_ref, sum_ref, after_ref, dq_ref, dk_ref, dv_ref, df_ref, dq_acc,
             delta_ref, inv_ref):
        inv_ref[...] = 1.0 / sum_ref[0]
        lane_s = lax.broadcasted_iota(jnp.int32, (SEQ, LANES), 1)
        prod = do_ref[...].astype(bf16).astype(f32) * o_ref[...]
        d_a = jnp.sum(jnp.where(lane_s < HEAD_DIM, prod, 0.0), axis=1, keepdims=True)
        d_b = jnp.sum(jnp.where(lane_s >= HEAD_DIM, prod, 0.0), axis=1, keepdims=True)
        delta_ref[...] = jnp.where(lane_s < HEAD_DIM, d_a, d_b)
        dq_acc[...] = jnp.zeros_like(dq_acc)
        df_ref[...] = jnp.zeros_like(df_ref)
        lane = lax.broadcasted_iota(jnp.int32, (blk, LANES), 1)
        own = [lane < HEAD_DIM, lane >= HEAD_DIM]
        tri = lax.broadcasted_iota(jnp.int32, (blk, blk), 0) >= lax.broadcasted_iota(jnp.int32, (blk, blk), 1)

        def q_slab(qoff, h):
            return q_ref[pl.ds(qoff, blk), h * LANES:(h + 1) * LANES]

        def probs(qoff, h, k_h, masked):
            s = _nt(q_slab(qoff, h), k_h)
            if masked:
                s = jnp.where(tri, s, NEG)
            col = slice(h * HEAD_DIM, h * HEAD_DIM + 1)
            weights = jnp.exp(s - max_ref[0, pl.ds(qoff, blk), col]).astype(bf16).astype(f32)
            return weights * inv_ref[pl.ds(qoff, blk), col]

        def k_slabs(koff):
            return [k_ref[pl.ds(koff, blk), h * LANES:(h + 1) * LANES] for h in range(2)]

        def kv_step(kj, _):
            koff = pl.multiple_of(kj * blk, blk)
            k_aug = k_slabs(koff)
            k_own = [jnp.where(own[h], k_aug[h], jnp.zeros_like(k_aug[h])) for h in range(2)]
            vv = v_ref[pl.ds(koff, blk), :]
            v_own = [jnp.where(own[h], vv, jnp.zeros_like(vv)) for h in range(2)]

            def q_tile(qi, carry, masked):
                qoff = pl.multiple_of(qi * blk, blk)
                dd = do_ref[pl.ds(qoff, blk), :].astype(bf16)
                new, dq_add = [], None
                for h in range(2):
                    dk_h, dv_h, dcol = carry[h]
                    p = probs(qoff, h, k_aug[h], masked)
                    dl = p * (_nt(dd, v_own[h]) - delta_ref[pl.ds(qoff, blk), h * HEAD_DIM:h * HEAD_DIM + 1])
                    dlb = dl.astype(bf16)
                    part = jnp.dot(dlb, k_own[h], preferred_element_type=f32)
                    dq_add = part if dq_add is None else dq_add + part
                    new.append((dk_h + _tn(dlb, q_slab(qoff, h)), dv_h + _tn(p.astype(bf16), dd),
                                dcol + _colsum(dl)))
                dq_acc[pl.ds(qoff, blk), :] += dq_add * ATT_SCALE
                return tuple(new)

            zero = (jnp.zeros((blk, LANES), f32), jnp.zeros((blk, LANES), f32), jnp.zeros((1, blk), f32))
            carry = q_tile(kj, (zero, zero), True)
            (dk_a, dv_a, dcol_a), (dk_b, dv_b, dcol_b) = lax.fori_loop(
                kj + 1, nblk, lambda qi, cr: q_tile(qi, cr, False), carry)
            dk_ref[pl.ds(koff, blk), :] = jnp.where(own[0], dk_a, dk_b).astype(dk_ref.dtype)
            dv_ref[pl.ds(koff, blk), :] = jnp.where(own[0], dv_a, dv_b).astype(dv_ref.dtype)
            df_ref[0, 0:1, pl.ds(koff, blk)] = -dcol_a
            df_ref[0, 1:2, pl.ds(koff, blk)] = -dcol_b
            return 0

        lax.fori_loop(0, nblk, kv_step, 0)
        dq_ref[...] = dq_acc[...].astype(dq_ref.dtype)

    pair_aug = pl.BlockSpec((SEQ, 2 * LANES), lambda p: (0, p))
    slab = pl.BlockSpec((SEQ, LANES), lambda p: (0, p))
    per_pair = pl.BlockSpec((1, SEQ, LANES), lambda p: (p, 0, 0))
    rows = pl.BlockSpec((1, 8, SEQ), lambda p: (p, 0, 0))
    return pl.pallas_call(
        body, grid=(npair,),
        in_specs=[pair_aug, pair_aug, slab, slab, slab, per_pair, per_pair, pl.BlockSpec(memory_space=pl.ANY)],
        out_specs=[slab, slab, slab, rows],
        out_shape=(SDS((SEQ, FOX_W), bf16),) * 3 + (SDS((npair, 8, SEQ), f32),), name="fox_bwd",
        scratch_shapes=[pltpu.VMEM((SEQ, LANES), f32)] * 3,
        compiler_params=_params(("parallel",)),
    )(q_aug, k_aug, v, do, o, row_max, row_sum, after)


DIL_BLK = 128
DILATIONS = (1, 4, 16)
N_GROUPS = len(DILATIONS)
DIL_PAIRS = DIL_OUT_W // LANES


def _dil_blocks(d):
    r1 = lax.broadcasted_iota(jnp.int32, (2 * DIL_BLK, DIL_BLK), 0) & (DIL_BLK - 1)
    c1 = lax.broadcasted_iota(jnp.int32, (2 * DIL_BLK, DIL_BLK), 1)
    r2 = lax.broadcasted_iota(jnp.int32, (2 * DIL_BLK, 2 * DIL_BLK), 0) & (DIL_BLK - 1)
    c2 = lax.broadcasted_iota(jnp.int32, (2 * DIL_BLK, 2 * DIL_BLK), 1)
    band = ((c2 < DIL_BLK) & (c2 >= r2)) | ((c2 >= DIL_BLK) & (c2 - DIL_BLK <= r2))
    out = []
    for r in range(d):
        for b in range(SEQ // d // DIL_BLK):
            rows = pl.ds(r + d * DIL_BLK * b, DIL_BLK, stride=d)
            if b == 0:
                out.append((rows, rows, r1 >= c1))
            else:
                out.append((rows, pl.ds(r + d * DIL_BLK * (b - 1), 2 * DIL_BLK, stride=d), band))
    return out


def _stack_heads(t, first):
    zero = jnp.zeros_like(t)
    return jnp.concatenate([jnp.where(first, t, zero), jnp.where(first, zero, t)], axis=0)


def _dil_fwd(q, k, v, g):
    def body(q_ref, k_ref, v_ref, o_ref, lse_ref):
        first = lax.broadcasted_iota(jnp.int32, (DIL_BLK, LANES), 1) < HEAD_DIM
        for rows, krows, mask in _dil_blocks(DILATIONS[g]):
            qv, kk, vv = q_ref[rows, :].astype(bf16), k_ref[krows, :].astype(bf16), v_ref[krows, :].astype(bf16)
            s = jnp.where(mask, _nt(_stack_heads(qv, first), kk), NEG)
            m = jnp.max(s, axis=1, keepdims=True)
            p = jnp.exp(s - m)
            l = jnp.sum(p, axis=1, keepdims=True)
            out = jnp.dot(p.astype(bf16), vv, preferred_element_type=f32) / l
            lse = m + jnp.log(l)
            o_ref[rows, :] = jnp.where(first, out[:DIL_BLK], out[DIL_BLK:])
            lse_ref[rows, :] = jnp.where(first, lse[:DIL_BLK], lse[DIL_BLK:])

    grouped = pl.BlockSpec((SEQ, LANES), lambda p: (0, DIL_PAIRS * g + p))
    own = pl.BlockSpec((SEQ, LANES), lambda p: (0, p))
    shape = SDS((SEQ, DIL_OUT_W), f32)
    return pl.pallas_call(
        body, grid=(DIL_PAIRS,), in_specs=[grouped] * 3, out_specs=[own] * 2, out_shape=(shape, shape),
        name=f"dil_fwd_{DILATIONS[g]}", compiler_params=_params(("parallel",)),
    )(q, k, v)


def _dil_bwd(q, k, v, do, lse, delta, g):
    def body(q_ref, k_ref, v_ref, do_ref, lse_ref, dl_ref, dq_ref, dk_ref, dv_ref):
        first = lax.broadcasted_iota(jnp.int32, (DIL_BLK, LANES), 1) < HEAD_DIM
        dk_ref[...] = jnp.zeros_like(dk_ref)
        dv_ref[...] = jnp.zeros_like(dv_ref)
        for rows, krows, mask in _dil_blocks(DILATIONS[g]):
            qv, kk, vv = q_ref[rows, :].astype(bf16), k_ref[krows, :].astype(bf16), v_ref[krows, :].astype(bf16)
            lsev, delv = lse_ref[rows, :], dl_ref[rows, :]
            q2 = _stack_heads(qv, first)
            do2 = _stack_heads(do_ref[rows, :].astype(bf16), first)
            per_head = lambda t: jnp.concatenate([t[:, 0:1], t[:, HEAD_DIM:HEAD_DIM + 1]], axis=0)
            p = jnp.exp(jnp.where(mask, _nt(q2, kk), NEG) - per_head(lsev))
            dl = (p * (_nt(do2, vv) - per_head(delv))).astype(bf16)
            dq = jnp.dot(dl, kk, preferred_element_type=f32)
            dq_ref[rows, :] = jnp.where(first, dq[:DIL_BLK], dq[DIL_BLK:]) * ATT_SCALE
            dk_ref[krows, :] += _tn(dl, q2)
            dv_ref[krows, :] += _tn(p.astype(bf16), do2)

    grouped = pl.BlockSpec((SEQ, LANES), lambda p: (0, DIL_PAIRS * g + p))
    own = pl.BlockSpec((SEQ, LANES), lambda p: (0, p))
    shape = SDS((SEQ, DIL_OUT_W), f32)
    return pl.pallas_call(
        body, grid=(DIL_PAIRS,), in_specs=[grouped] * 3 + [own] * 3, out_specs=[own] * 3,
        out_shape=(shape, shape, shape), name=f"dil_bwd_{DILATIONS[g]}", compiler_params=_params(("parallel",)),
    )(q, k, v, do, lse, delta)


def _position():
    return lax.axis_index("x"), lax.axis_index("y"), lax.axis_index("c")


def _all_gather(block, name):
    def body(x_ref, out_ref, send_sems, recv_sems, local_sem):
        x, y, c = _position()
        me, sibling = (x, y, c), (x, y, 1 - c)
        chips = [(1 - x, y), (x, 1 - y), (1 - x, 1 - y)]

        def slot(px, py, pc):
            return out_ref.at[4 * px + 2 * py + pc]

        def copy(k, blk, to, src=None):
            return pltpu.make_async_remote_copy(
                src_ref=slot(*blk) if src is None else src, dst_ref=slot(*blk),
                send_sem=send_sems.at[k], recv_sem=recv_sems.at[k], device_id=to, device_id_type=MESH)

        mine = pltpu.make_async_copy(x_ref, slot(*me), local_sem)
        mine.start()
        first = [copy(0, me, sibling, src=x_ref)]
        first += [copy(1 + j, me, (*chip, c), src=x_ref) for j, chip in enumerate(chips)]
        for cp in first:
            cp.start()
        passed = [copy(4 + j, (*chip, c), sibling) for j, chip in enumerate(chips)]
        for j, chip in enumerate(chips):
            copy(1 + j, (*chip, c), me).wait_recv()
            passed[j].start()
        copy(0, sibling, me).wait_recv()
        for j, chip in enumerate(chips):
            copy(4 + j, (*chip, 1 - c), me).wait_recv()
        for cp in first + passed:
            cp.wait_send()
        mine.wait()

    return pl.pallas_call(
        body, out_shape=SDS((N_DEV,) + block.shape, block.dtype),
        in_specs=[pl.BlockSpec(memory_space=pl.ANY)], out_specs=pl.BlockSpec(memory_space=pl.ANY),
        scratch_shapes=[pltpu.SemaphoreType.DMA((7,)), pltpu.SemaphoreType.DMA((7,)), pltpu.SemaphoreType.DMA],
        name=name,
    )(block)


HBM_SPEC = pl.BlockSpec(memory_space=pltpu.HBM)
SEM_SPEC = pl.BlockSpec(memory_space=pltpu.SEMAPHORE)
SPLIT_COPY = pltpu.CompilerParams(has_side_effects=pltpu.SideEffectType.DATAFLOW_SIDE_EFFECTING)


def _in_hbm(t):
    return pltpu.with_memory_space_constraint(t, pltpu.HBM)


def _pair_copies(g_refs, land_refs, send_sems, recv_sems):
    x, y, c = _position()
    return [pltpu.make_async_remote_copy(
        src_ref=g.at[2 * k + (1 - c)], dst_ref=land.at[k], send_sem=send_sems.at[4 * a + k],
        recv_sem=recv_sems.at[4 * a + k], device_id=(x, y, 1 - c), device_id_type=MESH)
        for a, (g, land) in enumerate(zip(g_refs, land_refs, strict=True)) for k in range(4)]


def _chip_copies(t_refs, land_refs, send_sems, recv_sems):
    x, y, c = _position()
    chips = [(1 - x, y), (x, 1 - y), (1 - x, 1 - y)]
    return [pltpu.make_async_remote_copy(
        src_ref=t.at[2 * px + py], dst_ref=land.at[j], send_sem=send_sems.at[3 * a + j],
        recv_sem=recv_sems.at[3 * a + j], device_id=(px, py, c), device_id_type=MESH)
        for a, (t, land) in enumerate(zip(t_refs, land_refs, strict=True)) for j, (px, py) in enumerate(chips)]


_ROUNDS = {"pair": (_pair_copies, 4), "chip": (_chip_copies, 3)}


def _exchange_start(kind, ts, name):
    copies, slots = _ROUNDS[kind]
    n = len(ts)
    lands = [_in_hbm(lax.empty((slots,) + t.shape[1:], t.dtype)) for t in ts]

    def body(*refs):
        for cp in copies(refs[:n], refs[n:2 * n], refs[2 * n], refs[2 * n + 1]):
            cp.start()
        refs[-1][...] = jnp.zeros_like(refs[-1])

    sems = pltpu.SemaphoreType.DMA((slots * n,))
    res = pl.pallas_call(
        body, name=name, in_specs=[HBM_SPEC] * (2 * n),
        out_shape=(sems, sems, *[pltpu.HBM(t.shape, t.dtype) for t in (*ts, *lands)], SDS((8, LANES), f32)),
        out_specs=(SEM_SPEC, SEM_SPEC, *[HBM_SPEC] * (2 * n), pl.BlockSpec(memory_space=pltpu.VMEM)),
        input_output_aliases={i: 2 + i for i in range(2 * n)}, compiler_params=SPLIT_COPY,
    )(*[_in_hbm(t) for t in ts], *lands)
    return res[:-1], res[-1]


def _exchange_wait(kind, state, after, name):
    copies, _ = _ROUNDS[kind]
    send_sems, recv_sems, *arrays = state
    n = len(arrays) // 2

    def body(*refs):
        for cp in copies(refs[:n], refs[n:2 * n], refs[2 * n], refs[2 * n + 1]):
            cp.wait_send()
            cp.wait_recv()

    res = pl.pallas_call(
        body, name=name, in_specs=[HBM_SPEC] * (2 * n) + [SEM_SPEC, SEM_SPEC, pl.BlockSpec(memory_space=pl.ANY)],
        out_shape=[pltpu.HBM(t.shape, t.dtype) for t in arrays], out_specs=[HBM_SPEC] * (2 * n),
        input_output_aliases={i: i for i in range(2 * n)}, compiler_params=SPLIT_COPY,
    )(*arrays, send_sems, recv_sems, after)
    return res[:n], res[n:]


def _gather_copies(x_refs, out_refs, send_sems, recv_sems):
    x, y, c = _position()
    peers = [(x, y, 1 - c), (1 - x, y, c), (x, 1 - y, c), (1 - x, 1 - y, c)]
    sends, arrivals = [], []
    for a, (x_ref, out_ref) in enumerate(zip(x_refs, out_refs, strict=True)):
        for k, (px, py, pc) in enumerate(peers):
            sems = dict(send_sem=send_sems.at[4 * a + k], recv_sem=recv_sems.at[4 * a + k],
                        device_id=(px, py, pc), device_id_type=MESH)
            sends.append(pltpu.make_async_remote_copy(src_ref=x_ref, dst_ref=out_ref.at[4 * x + 2 * y + c], **sems))
            arrivals.append(pltpu.make_async_remote_copy(src_ref=x_ref, dst_ref=out_ref.at[4 * px + 2 * py + pc],
                                                         **sems))
    return sends, arrivals


def _gather_start(blocks, after, name):
    n = len(blocks)
    outs = [_in_hbm(lax.empty((N_DEV,) + b.shape, b.dtype)) for b in blocks]

    def body(*refs):
        sends, _ = _gather_copies(refs[:n], refs[n:2 * n], refs[2 * n + 1], refs[2 * n + 2])
        for cp in sends:
            cp.start()
        refs[-1][...] = jnp.zeros_like(refs[-1])

    sems = pltpu.SemaphoreType.DMA((4 * n,))
    res = pl.pallas_call(
        body, name=name, in_specs=[HBM_SPEC] * (2 * n) + [pl.BlockSpec(memory_space=pl.ANY)],
        out_shape=(sems, sems, *[pltpu.HBM(t.shape, t.dtype) for t in (*blocks, *outs)], SDS((8, LANES), f32)),
        out_specs=(SEM_SPEC, SEM_SPEC, *[HBM_SPEC] * (2 * n), pl.BlockSpec(memory_space=pltpu.VMEM)),
        input_output_aliases={i: 2 + i for i in range(2 * n)}, compiler_params=SPLIT_COPY,
    )(*[_in_hbm(b) for b in blocks], *outs, after)
    return res[:-1], res[-1]


def _gather_wait(state, after, name):
    send_sems, recv_sems, *arrays = state
    n = len(arrays) // 2

    def body(*refs):
        sends, arrivals = _gather_copies(refs[:n], refs[n:2 * n], refs[2 * n], refs[2 * n + 1])
        for cp in sends:
            cp.wait_send()
        for cp in arrivals:
            cp.wait_recv()

    res = pl.pallas_call(
        body, name=name, in_specs=[HBM_SPEC] * (2 * n) + [SEM_SPEC, SEM_SPEC, pl.BlockSpec(memory_space=pl.ANY)],
        out_shape=[pltpu.HBM(t.shape, t.dtype) for t in arrays], out_specs=[HBM_SPEC] * (2 * n),
        input_output_aliases={i: i for i in range(2 * n)}, compiler_params=SPLIT_COPY,
    )(*arrays, send_sems, recv_sems, after)
    return res[:n], res[n:]


def _gather_finish(partial, name):
    n = len(partial)

    def body(*refs):
        in_refs, out_refs = refs[:n], refs[n:2 * n]
        send_sems, recv_sems = refs[2 * n:]
        x, y, c = _position()
        chips = [(1 - x, y), (x, 1 - y), (1 - x, 1 - y)]
        copies = []
        for a in range(n):
            for j, (px, py) in enumerate(chips):
                cp = pltpu.make_async_remote_copy(
                    src_ref=in_refs[a].at[4 * px + 2 * py + c], dst_ref=out_refs[a].at[4 * px + 2 * py + c],
                    send_sem=send_sems.at[a, j], recv_sem=recv_sems.at[a, j], device_id=(x, y, 1 - c),
                    device_id_type=MESH)
                cp.start()
                copies.append(cp)
        for a in range(n):
            for j, (px, py) in enumerate(chips):
                pltpu.make_async_remote_copy(
                    src_ref=in_refs[a].at[4 * px + 2 * py + (1 - c)], dst_ref=out_refs[a].at[4 * px + 2 * py + (1 - c)],
                    send_sem=send_sems.at[a, j], recv_sem=recv_sems.at[a, j], device_id=(x, y, 1 - c),
                    device_id_type=MESH).wait_recv()
        for cp in copies:
            cp.wait_send()

    hbm = pl.BlockSpec(memory_space=pl.ANY)
    return pl.pallas_call(
        body, out_shape=[SDS(p.shape, p.dtype) for p in partial], in_specs=[hbm] * n, out_specs=[hbm] * n,
        input_output_aliases={a: a for a in range(n)},
        scratch_shapes=[pltpu.SemaphoreType.DMA((n, 3)), pltpu.SemaphoreType.DMA((n, 3))],
        name=name,
    )(*partial)


def _row_tile(rows):
    return 512 if rows % 512 == 0 and rows > 512 else rows


def _pair_add(g, r1, core, name):
    def body(c_ref, g_ref, r_ref, o_ref):
        o_ref[...] = (g_ref[...].astype(f32) + r_ref[...].astype(f32)).astype(o_ref.dtype)

    rows, cols = g.shape[1:]
    tile = _row_tile(rows)
    blk = (1, tile, cols)
    return pl.pallas_call(
        body, out_shape=SDS((4, rows, cols), g.dtype), name=name,
        grid_spec=pltpu.PrefetchScalarGridSpec(
            num_scalar_prefetch=1, grid=(4, rows // tile),
            in_specs=[pl.BlockSpec(blk, lambda k, i, c_ref: (2 * k + c_ref[0], i, 0)),
                      pl.BlockSpec(blk, lambda k, i, c_ref: (k, i, 0))],
            out_specs=pl.BlockSpec(blk, lambda k, i, c_ref: (k, i, 0))),
        compiler_params=_params(("parallel", "arbitrary")),
    )(core, g, r1)


def _chip_add(t, r2, chip, name):
    def body(c_ref, t_ref, r_ref, o_ref):
        o_ref[...] = ((t_ref[0].astype(f32) + r_ref[0].astype(f32)) + r_ref[1].astype(f32)) + r_ref[2].astype(f32)

    rows, cols = t.shape[1:]
    tile = _row_tile(rows)
    return pl.pallas_call(
        body, out_shape=SDS((rows, cols), f32), name=name,
        grid_spec=pltpu.PrefetchScalarGridSpec(
            num_scalar_prefetch=1, grid=(rows // tile,),
            in_specs=[pl.BlockSpec((1, tile, cols), lambda i, c_ref: (c_ref[0], i, 0)),
                      pl.BlockSpec((3, tile, cols), lambda i, c_ref: (0, i, 0))],
            out_specs=pl.BlockSpec((tile, cols), lambda i, c_ref: (i, 0))),
        compiler_params=_params(("arbitrary",)),
    )(chip, t, r2)


def _pad_to(t, axis, size):
    pads = [(0, 0)] * t.ndim
    pads[axis] = (0, size - t.shape[axis])
    return jnp.pad(t, pads)


_REF_COLS = {"qa": (0, FOX_W), "ka": (FOX_W, FOX_W), "va": (2 * FOX_W, FOX_W), "f": (3 * FOX_W, N_FOX_HEADS)}
_REF_COLS.update({n: (3 * FOX_W + N_FOX_HEADS + i * DIL_W, DIL_W) for i, n in enumerate(("qb", "kb", "vb"))})
_REF_COLS.update({n: (3 * FOX_W + N_FOX_HEADS + 3 * DIL_W + i * D, D) for i, n in enumerate(("ga", "gb"))})
_REF_ORDER = ("qa", "ka", "va", "f", "qb", "kb", "vb", "ga", "gb")


def _place_cols(sources, src_of, out_cols, name, row_block=512):
    arrays = [s[0] if isinstance(s, tuple) else s for s in sources]
    widths = [a.shape[-1] for a in arrays]
    rows = arrays[0].shape[-2]
    plan = []
    for t in range(out_cols // LANES):
        segs, c, end = [], t * LANES, (t + 1) * LANES
        while c < end:
            s = src_of(c)
            if s is None:
                c += 1
                continue
            n = 1
            while c + n < end and src_of(c + n) == (s[0], s[1] + n):
                n += 1
            segs.append((s[0], s[1], c - t * LANES, n))
            c += n
        plan.append(segs)

    def body(*refs):
        o_ref = refs[-1]
        for t, segs in enumerate(plan):
            acc = None
            for si, c0, o0, n in segs:
                a0 = c0 // LANES * LANES
                wide = min(2 * LANES, widths[si] - a0)
                win = refs[si][0, :, a0:a0 + wide] if isinstance(sources[si], tuple) else refs[si][:, a0:a0 + wide]
                r = lax.broadcasted_iota(jnp.int32, (wide, LANES), 0)
                c = lax.broadcasted_iota(jnp.int32, (wide, LANES), 1)
                pick = ((r - (c0 - a0) == c - o0) & (c >= o0) & (c < o0 + n)).astype(bf16)
                part = jnp.dot(win.astype(bf16), pick, preferred_element_type=f32)
                acc = part if acc is None else acc + part
            tile = jnp.zeros((row_block, LANES), f32) if acc is None else acc
            o_ref[:, t * LANES:(t + 1) * LANES] = tile.astype(o_ref.dtype)

    def spec(s):
        if isinstance(s, tuple):
            j = s[1]
            return pl.BlockSpec((1, row_block, s[0].shape[-1]), lambda i: (j, i, 0))
        return pl.BlockSpec((row_block, s.shape[-1]), lambda i: (i, 0))

    return pl.pallas_call(
        body, grid=(rows // row_block,), in_specs=[spec(s) for s in sources],
        out_specs=pl.BlockSpec((row_block, out_cols), lambda i: (i, 0)), out_shape=SDS((rows, out_cols), bf16),
        name=name, compiler_params=_params(("parallel",)),
    )(*arrays)


def _ref_piece(r):
    for name in _REF_ORDER:
        lo, width = _REF_COLS[name]
        if lo <= r < lo + width:
            return name, r - lo
    raise ValueError(r)


def _shard_pad_cols(pieces):
    names = [n for n in _REF_ORDER if n != "vb"]
    sources = [pieces[n] for n in names] + list(pieces["vb"])

    def src_of(c):
        j, i = divmod(c, W_IN_PAD)
        if i >= W_IN_SH:
            return None
        name, col = _ref_piece(j * W_IN_SH + i)
        if name == "vb":
            return len(names) + col // DIL_OUT_W, col % DIL_OUT_W
        return names.index(name), col

    return _place_cols(sources, src_of, N_DEV * W_IN_PAD, "place_dproj")


_SLABS = {"ga": C_GA, "gb": C_GB, "qb": C_QB, "kb": C_KB, "vb": C_VB, "qa": C_QA, "ka": C_KA, "va": C_VA, "f": C_F}


def _slab_w_in(stack):
    def src_of(c):
        for name, start in _SLABS.items():
            lo, width = _REF_COLS[name]
            if start <= c < start + width:
                return divmod(lo + c - start, W_IN_SH)
        return None

    return _place_cols([(stack, j) for j in range(N_DEV)], src_of, PROJ_W, "place_w_in")


def kernel(x, c, w_ada, b_ada, g_mix, w_in, b_fgate, w_br_a, w_br_b, w_out, g_ffn, w_ffn_gate, w_ffn_up, w_ffn_down, g_final, loss_target, m_w_ada, m_b_ada, m_g_mix, m_w_in, m_b_fgate, m_w_br_a, m_w_br_b, m_w_out, m_g_ffn, m_w_ffn_gate, m_w_ffn_up, m_w_ffn_down, m_g_final, v_w_ada, v_b_ada, v_g_mix, v_w_in, v_b_fgate, v_w_br_a, v_w_br_b, v_w_out, v_g_ffn, v_w_ffn_gate, v_w_ffn_up, v_w_ffn_down, v_g_final):
    px, py, pc = _position()
    dev = 4 * px + 2 * py + pc
    x2d, tgt = x[0], loss_target[0]

    c_all = _all_gather(c, "gather_c").reshape(N_DEV, D)
    ada_cols = w_ada.shape[2]
    b_shard = lax.dynamic_slice(b_ada, (0, dev * ada_cols), (1, ada_cols))
    mod_shard = _ada_fwd(c_all, w_ada[0], b_shard)
    mod_all = _all_gather(mod_shard, "gather_mod")
    modv = lax.dynamic_index_in_dim(mod_all, dev, axis=1, keepdims=False).reshape(6, D)
    h1 = _pre1(x2d, modv, g_mix)

    w_in_s = _all_gather(_pad_to(w_in[0], 1, W_IN_PAD).astype(bf16), "gather_w_in")
    gate_up = jnp.concatenate([_pad_to(w_ffn_gate[0], 1, FF_PAD), _pad_to(w_ffn_up[0], 1, FF_PAD)], axis=1)
    later = [w_br_a[0], w_br_b[0], w_out[0], gate_up, _pad_to(w_ffn_down[0], 0, FF_PAD)]
    later_state, later_token = _gather_start([t.astype(bf16) for t in later], w_in_s, "gather_rest_start")
    w_in_p = _slab_w_in(w_in_s)

    proj = _matmul(h1, w_in_p, name="mm_proj", tm=SEQ, tn=896, tk=D, after=later_token)
    b_pad = jnp.pad(b_fgate, ((0, 0), (0, LANES - N_FOX_HEADS)))
    q_aug, k_aug, va = _fox_prep(proj, _fox_gate_fwd(proj, b_pad))
    ya_h, max_a, sum_a = _fox_fwd(q_aug, k_aug, va)

    tables = _rope_tables()
    qb_r, kb_r, vb = _rope_fwd(proj, tables)
    by_group = [_dil_fwd(qb_r, kb_r, vb, grp) for grp in range(N_GROUPS)]
    yb_h, lse_b = _dil_combine([o for o, _ in by_group], [l for _, l in by_group])

    mine, arrived = _gather_wait(later_state, yb_h, "gather_rest_wait")
    w_a_s, w_b_s, w_o_s, w_gu_s, w_d_s = [
        lax.dynamic_update_slice(stack, block[None], (dev, 0, 0))
        for stack, block in zip(_gather_finish(arrived, "gather_rest_finish"), mine, strict=True)]
    w_o = w_o_s.reshape(D, D)
    w_d = w_d_s.reshape(FF_HID, D)
    ya = _matmul_stack(ya_h, w_a_s, name="mm_br_a")
    yb = _matmul_stack(yb_h, w_b_s, name="mm_br_b")

    merged = _merge_fwd(ya, yb, proj)
    mix = _matmul(merged, w_o, name="mm_out", tm=SEQ, tn=512, tk=D)
    x1, h2 = _post1(x2d, mix, modv, g_ffn)
    act, au = _ffn_in(h2, w_gu_s)
    ff = _matmul(act, w_d, name="mm_ffn_down", tm=SEQ // 2, tn=512, tk=FF_HID)

    dx2, dff, dg_final, dga_f, loss_lanes = _final(x1, ff, tgt, modv, g_final.reshape(1, D))
    dau = _ffn_bwd_in(dff, w_d_s, au)

    core = pc.astype(jnp.int32).reshape(1)
    chip = (2 * px + py).astype(jnp.int32).reshape(1)

    def pair_done(state, after, tags, name):
        mine, theirs = _exchange_wait("pair", state, after, "pair_wait_" + name)
        sums = [_pair_add(g, r, core, "pair_add_" + t) for g, r, t in zip(mine, theirs, tags)]
        return _exchange_start("chip", sums, "chip_start_" + name)

    def from_chips(state, after, tags, name):
        sums, got = _exchange_wait("chip", state, after, "chip_wait_" + name)
        return [_chip_add(p, r, chip, "chip_add_" + t) for p, r, t in zip(sums, got, tags)]

    g_gu = _matmul(h2, dau, ta=True, by_shard=True, out_dtype=bf16, name="mm_g_ffn_in", tm=D, tn=2 * FF_PAD, tk=SEQ)
    g_d = _matmul(act, dff, ta=True, out_dtype=bf16, name="mm_g_down", tm=FF_HID // 2, tn=512, tk=SEQ)
    ffn_tags = ["gu", "down"]
    ffn_pair, ffn_pair_token = _exchange_start("pair", [g_gu, g_d.reshape(N_DEV, FF_PAD, D)], "pair_start_ffn")

    dx1, dmix, dsh_f, dsc_f, dg_ffn, dga_m = _mid_bwd(dau, w_gu_s, ffn_pair_token, x1, dx2, mix, modv, g_ffn)
    ffn_state, ffn_token = pair_done(ffn_pair, dx1, ffn_tags, "ffn")
    dmerged = _matmul(dmix, w_o, tb=True, name="mm_d_merged", tm=SEQ, tn=512, tk=D, after=ffn_token)
    dya, dyb, dga, dgb = _merge_bwd(dmerged, ya, yb, proj)
    dya_h = _matmul_stack(dya, w_a_s, tb=True, name="mm_d_ya")
    dyb_h = _matmul_stack(dyb, w_b_s, tb=True, name="mm_d_yb")

    g_o = _matmul(merged, dmix, ta=True, out_dtype=bf16, name="mm_g_out", tm=D, tn=512, tk=SEQ)
    g_a = _matmul_stack(ya_h, dya, ta=True, out_dtype=bf16, name="mm_g_br_a")
    g_b = _matmul_stack(yb_h, dyb, ta=True, out_dtype=bf16, name="mm_g_br_b")
    rows_a, rows_b = FOX_W * W_BR_SH // D, DIL_OUT_W * W_BR_SH // D
    g_small = jnp.concatenate([g_a.reshape(N_DEV, rows_a, D), g_b.reshape(N_DEV, rows_b, D),
                               g_o.reshape(N_DEV, W_BR_SH, D)], axis=1)
    small_pair, small_pair_token = _exchange_start("pair", [g_small], "pair_start_small")

    dqa, dka, dva, dF = _fox_bwd(q_aug, k_aug, va, dya_h, ya_h, max_a, sum_a, small_pair_token)
    dF_row = jnp.pad(dF[:, :2, :].reshape(N_FOX_HEADS, SEQ), ((0, LANES - N_FOX_HEADS), (0, 0)))
    df, db_fgate = _fox_gate_bwd(dF_row, proj, b_pad)
    small_state, small_token = pair_done(small_pair, df, ["small"], "small")

    delta_b = _dil_delta(dyb_h, yb_h)
    dil_grads = [_dil_bwd(qb_r, kb_r, vb, dyb_h, lse_b, delta_b, grp) for grp in range(N_GROUPS)]
    dqb, dkb = _rope_bwd([t[0] for t in dil_grads], [t[1] for t in dil_grads], tables)

    dproj = _shard_pad_cols({"qa": dqa, "ka": dka, "va": dva, "f": df, "qb": dqb, "kb": dkb,
                             "vb": [t[2] for t in dil_grads], "ga": dga, "gb": dgb})
    g_in = _matmul(h1, dproj, ta=True, by_shard=True, out_dtype=bf16, name="mm_g_in", tm=D, tn=W_IN_PAD, tk=SEQ,
                   after=small_token)
    mix_tags = ["in"]
    mix_pair, mix_pair_token = _exchange_start("pair", [g_in], "pair_start_mixer")

    grad_x, dsh_m, dsc_m, dg_mix = _first_bwd(dproj, w_in_s, mix_pair_token, x2d, dx1, modv, g_mix)

    pad_lane = lambda t: jnp.pad(t, ((0, 0), (0, D - t.shape[1])))
    small = jnp.concatenate([dsh_m, dsc_m, dga_m, dsh_f, dsc_f, dga_f, dg_mix, dg_ffn, dg_final,
                             pad_lane(db_fgate), loss_lanes, jnp.zeros((SMALL_ROWS - 11, D), f32)], axis=0)
    small_all = _all_gather(small, "gather_small")
    mix_state, mix_token = pair_done(mix_pair, small_all, mix_tags, "mixer")

    small_sum, loss_row = _small_reduce(small_all, mix_token)
    dmod_all = small_all[:, :6, :].reshape(N_DEV, 6 * D)
    g_w_ada = _ada_bwd(c_all, lax.dynamic_slice(dmod_all, (0, dev * ada_cols), (N_DEV, ada_cols)))
    s_gu, s_d = from_chips(ffn_state, small_sum, ffn_tags, "ffn")
    s_small, = from_chips(small_state, small_sum, ["small"], "small")

    loss = loss_row[0, 0]
    g = {
        "w_ada": g_w_ada[None], "b_ada": small_sum[0:6].reshape(1, 6 * D), "g_mix": small_sum[6:7],
        "b_fgate": small_sum[9:10, :N_FOX_HEADS], "g_ffn": small_sum[7:8], "w_ffn_gate": s_gu[None, :, :W_FF_SH],
        "w_ffn_up": s_gu[None, :, FF_PAD:FF_PAD + W_FF_SH], "w_ffn_down": s_d[None, :W_FF_SH],
        "g_final": small_sum[8], "w_br_a": s_small[:rows_a].reshape(1, FOX_W, W_BR_SH),
        "w_br_b": s_small[rows_a:rows_a + rows_b].reshape(1, DIL_OUT_W, W_BR_SH), "w_out": s_small[None, rows_a + rows_b:],
    }
    w = {"w_ada": w_ada, "b_ada": b_ada, "g_mix": g_mix, "w_in": w_in, "b_fgate": b_fgate, "w_br_a": w_br_a,
         "w_br_b": w_br_b, "w_out": w_out, "g_ffn": g_ffn, "w_ffn_gate": w_ffn_gate, "w_ffn_up": w_ffn_up,
         "w_ffn_down": w_ffn_down, "g_final": g_final}
    m = {"w_ada": m_w_ada, "b_ada": m_b_ada, "g_mix": m_g_mix, "w_in": m_w_in, "b_fgate": m_b_fgate,
         "w_br_a": m_w_br_a, "w_br_b": m_w_br_b, "w_out": m_w_out, "g_ffn": m_g_ffn, "w_ffn_gate": m_w_ffn_gate,
         "w_ffn_up": m_w_ffn_up, "w_ffn_down": m_w_ffn_down, "g_final": m_g_final}
    v = {"w_ada": v_w_ada, "b_ada": v_b_ada, "g_mix": v_g_mix, "w_in": v_w_in, "b_fgate": v_b_fgate,
         "w_br_a": v_w_br_a, "w_br_b": v_w_br_b, "w_out": v_w_out, "g_ffn": v_g_ffn, "w_ffn_gate": v_w_ffn_gate,
         "w_ffn_up": v_w_ffn_up, "w_ffn_down": v_w_ffn_down, "g_final": v_g_final}
    names = list(w)
    delta, new_m, new_v = {}, {}, {}

    transposed = ("w_in", "w_ffn_gate", "w_ffn_up")

    def update(n):
        shape = w[n].shape
        if n in transposed:
            g_t = g[n][0].T
            dl, mn, vn = _adamw(w[n][0].T, g_t, m[n][0].T, v[n][0].T, "adamw_" + n)
            g[n], delta[n], new_m[n], new_v[n] = g_t.T[None], dl.T[None], mn.T[None], vn.T[None]
            return
        two_d = (lambda t: t.reshape(shape[-2:])) if len(shape) == 3 else (lambda t: t)
        dl, mn, vn = _adamw(two_d(w[n]), two_d(g[n]), two_d(m[n]), two_d(v[n]), "adamw_" + n)
        delta[n], new_m[n], new_v[n] = dl.reshape(shape), mn.reshape(shape), vn.reshape(shape)

    for n in list(g):
        update(n)
    done = sum(delta[n].reshape(-1)[:N_FOX_HEADS] for n in g)
    s_in, = from_chips(mix_state, done, mix_tags, "mixer")
    g["w_in"] = s_in[None, :, :W_IN_SH]
    update("w_in")

    return (loss, grad_x[None], *[g[n] for n in names], *[delta[n] for n in names],
            *[new_m[n] for n in names], *[new_v[n] for n in names])
```

```python
import functools

import jax
import jax.numpy as jnp
from jax import lax
from jax.experimental import pallas as pl
from jax.experimental.pallas import tpu as pltpu

f32 = jnp.float32
bf16 = jnp.bfloat16
SDS = jax.ShapeDtypeStruct
MESH = pl.DeviceIdType.MESH

N_DEV = 8
D = 1024
SEQ = 2048
HEAD_DIM = 64
N_FOX_HEADS = 8
FOX_W = 512
DIL_W = 768
DIL_OUT_W = 256
ROT_DIM = 16
ROPE_THETA = 500000.0
D_FF = 2816
IN_COLS = 5896
EPS = 1e-6
NEG = -1e30
ATT_SCALE = HEAD_DIM ** -0.5

ADAM_LR = 0.001
ADAM_B1 = 0.9
ADAM_B2 = 0.999
ADAM_EPS = 1e-08
ADAM_WD = 0.01
ADAM_STEP = 10

C_GA, C_GB, C_QB, C_KB, C_VB, C_QA, C_KA, C_VA, C_F = 0, 1024, 2304, 3072, 3840, 4608, 5120, 5632, 6144
PROJ_W = 6272
LANES = 128
VMEM_LIMIT = 52 * 1024 * 1024

W_IN_SH, W_IN_PAD = IN_COLS // N_DEV, 768
W_BR_SH = D // N_DEV
W_FF_SH, FF_PAD = D_FF // N_DEV, 384
FF_HID = N_DEV * FF_PAD
SMALL_ROWS = 16


def _params(sem=None):
    if sem is None:
        return pltpu.CompilerParams(vmem_limit_bytes=VMEM_LIMIT)
    return pltpu.CompilerParams(dimension_semantics=sem, vmem_limit_bytes=VMEM_LIMIT)


def _rowwise(fn, name, tiled, vecs, outs, reds=(), tile=256):
    nt, nv, no = len(tiled), len(vecs), len(outs)
    rows = tiled[0][0].shape[0]
    assert rows % tile == 0

    def body(*refs):
        tin = [r[...] for r in refs[:nt]]
        vin = [r[...] for r in refs[nt:nt + nv]]
        orefs = refs[nt + nv:nt + nv + no]
        rrefs = refs[nt + nv + no:]
        touts, routs = fn(tin, vin)
        for r, t in zip(orefs, touts, strict=True):
            r[...] = t.astype(r.dtype)
        if rrefs:
            @pl.when(pl.program_id(0) == 0)
            def _():
                for r in rrefs:
                    r[...] = jnp.zeros_like(r)
            for r, t in zip(rrefs, routs, strict=True):
                r[...] += t

    def col_map(cb):
        return lambda i: (i, cb)

    def whole_map(nd):
        return lambda i: (0,) * nd

    in_specs = [pl.BlockSpec((tile, w), col_map(cb)) for (_, w, cb) in tiled]
    in_specs += [pl.BlockSpec(v.shape, whole_map(v.ndim)) for v in vecs]
    out_specs = [pl.BlockSpec((tile, w), lambda i: (i, 0)) for (w, _) in outs]
    out_specs += [pl.BlockSpec((1, w), lambda i: (0, 0)) for w in reds]
    out_shape = [SDS((rows, w), dt) for (w, dt) in outs] + [SDS((1, w), f32) for w in reds]
    res = pl.pallas_call(
        body, grid=(rows // tile,), in_specs=in_specs, out_specs=out_specs, out_shape=out_shape, name=name,
        compiler_params=_params(("arbitrary",)),
    )(*[t[0] for t in tiled], *vecs)
    return res


def _matmul(a, b, *, ta=False, tb=False, out_dtype=f32, name, tm, tn, tk, by_shard=False, after=None):
    m, k = (a.shape[1], a.shape[0]) if ta else a.shape
    if by_shard and not ta:
        n, kb = (b.shape[1], N_DEV * b.shape[2]) if tb else (N_DEV * b.shape[2], b.shape[1])
        assert (tk if tb else tn) == b.shape[2]
    else:
        n, kb = (b.shape[0], b.shape[1]) if tb else (b.shape[1], b.shape[0])
    assert kb == k and m % tm == 0 and n % tn == 0 and k % tk == 0
    nk = k // tk
    dims = (((0 if ta else 1,), (1 if tb else 0,)), ((), ()))
    b_stacked = by_shard and not ta
    o_stacked = by_shard and ta

    def body(a_ref, b_ref, *rest):
        o_ref, *acc = rest[1:] if after is not None else rest
        bv = b_ref[0] if b_stacked else b_ref[...]
        p = lax.dot_general(a_ref[...].astype(bf16), bv.astype(bf16), dims, preferred_element_type=f32)

        def put(val):
            if o_stacked:
                o_ref[0] = val.astype(o_ref.dtype)
            else:
                o_ref[...] = val.astype(o_ref.dtype)

        if nk == 1:
            put(p)
        else:
            acc_ref, = acc
            kk = pl.program_id(2)

            @pl.when(kk == 0)
            def _():
                acc_ref[...] = p

            @pl.when(kk > 0)
            def _():
                acc_ref[...] += p

            @pl.when(kk == nk - 1)
            def _():
                put(acc_ref[...])

    a_spec = pl.BlockSpec((tk, tm), lambda i, j, kk: (kk, i)) if ta else pl.BlockSpec((tm, tk), lambda i, j, kk: (i, kk))
    if b_stacked and tb:
        b_spec = pl.BlockSpec((1, tn, tk), lambda i, j, kk: (kk, j, 0))
    elif b_stacked:
        b_spec = pl.BlockSpec((1, tk, tn), lambda i, j, kk: (j, kk, 0))
    elif tb:
        b_spec = pl.BlockSpec((tn, tk), lambda i, j, kk: (j, kk))
    else:
        b_spec = pl.BlockSpec((tk, tn), lambda i, j, kk: (kk, j))
    if o_stacked:
        assert tn == n // N_DEV
        out_spec = pl.BlockSpec((1, tm, tn), lambda i, j, kk: (j, i, 0))
        out_shape = SDS((N_DEV, m, tn), out_dtype)
    else:
        out_spec = pl.BlockSpec((tm, tn), lambda i, j, kk: (i, j))
        out_shape = SDS((m, n), out_dtype)
    extra_specs, extra = ([pl.BlockSpec(memory_space=pl.ANY)], [after]) if after is not None else ([], [])
    return pl.pallas_call(
        body, grid=(m // tm, n // tn, nk), in_specs=[a_spec, b_spec] + extra_specs, out_specs=out_spec,
        out_shape=out_shape, name=name, scratch_shapes=[pltpu.VMEM((tm, tn), f32)] if nk > 1 else [],
        compiler_params=_params(("parallel", "parallel", "arbitrary")),
    )(a, b, *extra)


def _matmul_stack(a, b, *, ta=False, tb=False, out_dtype=f32, name):
    def lanes(ref):
        return jnp.concatenate([ref[j] for j in range(N_DEV)], axis=1).astype(bf16)

    if ta:
        w = b.shape[1] // N_DEV

        def body(a_ref, b_ref, o_ref):
            p = _tn(a_ref[...].astype(bf16), b_ref[...].astype(bf16))
            for j in range(N_DEV):
                o_ref[j] = p[:, j * w:(j + 1) * w].astype(o_ref.dtype)

        return pl.pallas_call(body, out_shape=SDS((N_DEV, a.shape[1], w), out_dtype), name=name,
                              compiler_params=_params())(a, b)

    m, half = a.shape[0], a.shape[0] // 2
    n = b.shape[1] if tb else N_DEV * b.shape[2]

    def body(a_ref, b_ref, o_ref):
        av = a_ref[...].astype(bf16)
        o_ref[...] = (_nt(av, lanes(b_ref)) if tb else jnp.dot(av, lanes(b_ref), preferred_element_type=f32)
                      ).astype(o_ref.dtype)

    return pl.pallas_call(
        body, grid=(2,), in_specs=[pl.BlockSpec((half, a.shape[1]), lambda i: (i, 0)),
                                   pl.BlockSpec(b.shape, lambda i: (0, 0, 0))],
        out_specs=pl.BlockSpec((half, n), lambda i: (i, 0)), out_shape=SDS((m, n), out_dtype), name=name,
        compiler_params=_params(("parallel",)),
    )(a, b)


def _matmul_rows(form, a, b, after, fn, tiled, vecs, outs, reds, *, name, tm=512):
    m = a.shape[0]
    assert m % tm == 0
    tiled = [t if isinstance(t, tuple) else (t, t.shape[1], 0) for t in tiled]
    nt, nv, no = len(tiled), len(vecs), len(outs)

    def body(a_ref, b_ref, after_ref, *refs):
        if form == "nt_stack":
            w = b.shape[2]
            acc = _nt(a_ref[:, 0:w], b_ref[0])
            for j in range(1, N_DEV):
                acc = acc + _nt(a_ref[:, j * w:(j + 1) * w], b_ref[j])
        elif form == "nt":
            acc = _nt(a_ref[...], b_ref[...])
        else:
            acc = jnp.dot(a_ref[...], b_ref[...], preferred_element_type=f32)
        orefs, rrefs = refs[nt + nv:nt + nv + no], refs[nt + nv + no:]
        touts, routs = fn([acc] + [r[...] for r in refs[:nt]], [r[...] for r in refs[nt:nt + nv]])
        for r, t in zip(orefs, touts, strict=True):
            r[...] = t.astype(r.dtype)

        @pl.when(pl.program_id(0) == 0)
        def _():
            for r in rrefs:
                r[...] = jnp.zeros_like(r)
        for r, t in zip(rrefs, routs, strict=True):
            r[...] += t

    def whole_map(nd):
        return lambda i: (0,) * nd

    def rows(width, cb=0):
        return pl.BlockSpec((tm, width), lambda i: (i, cb))

    return pl.pallas_call(
        body, grid=(m // tm,),
        in_specs=[rows(a.shape[1]), pl.BlockSpec(b.shape, whole_map(b.ndim), pipeline_mode=pl.Buffered(1)),
                  pl.BlockSpec(memory_space=pl.ANY)]
        + [rows(width, cb) for _, width, cb in tiled] + [pl.BlockSpec(v.shape, whole_map(v.ndim)) for v in vecs],
        out_specs=[rows(width) for width, _ in outs] + [pl.BlockSpec((1, width), lambda i: (0, 0)) for width in reds],
        out_shape=[SDS((m, width), dt) for width, dt in outs] + [SDS((1, width), f32) for width in reds], name=name,
        compiler_params=_params(("arbitrary",)),
    )(a, b, after, *[t[0] for t in tiled], *vecs)


def _rms(x):
    r = lax.rsqrt(jnp.mean(x * x, axis=-1, keepdims=True) + EPS)
    return r, x * r


def _rms_bwd(r, xn, dxn):
    return r * (dxn - xn * jnp.mean(dxn * xn, axis=-1, keepdims=True))


def _colsum(t):
    return jnp.sum(t, axis=0, keepdims=True)


def _sigmoid(x):
    return 1.0 / (1.0 + jnp.exp(-x))


def _modulated_norm(x, g, shift, scale):
    _, xn = _rms(x)
    return (xn * g) * (1.0 + scale) + shift


def _pre1(x, modv, g_mix):
    def fn(t, v):
        (xt,), (mv, g) = t, v
        return [_modulated_norm(xt, g, mv[0:1], mv[1:2])], []
    return _rowwise(fn, "pre1", [(x, D, 0)], [modv, g_mix], [(D, bf16)])[0]


def _post1(merged, w_o, x, modv, g_ffn):
    def fn(t, v):
        (mt, xt), (mv, g) = t, v
        x1 = xt + mv[2:3] * mt
        return [mt, x1, _modulated_norm(x1, g, mv[3:4], mv[4:5])], []
    return _matmul_rows("nn", merged, w_o, x, fn, [x], [modv, g_ffn], [(D, f32), (D, f32), (D, bf16)], [],
                        name="post1")


def _ffn_in(h, w_stack):
    def body(h_ref, w_ref, act_ref, au_ref):
        p = jnp.dot(h_ref[...], w_ref[0], preferred_element_type=f32)
        a, u = p[:, :FF_PAD], p[:, FF_PAD:]
        act_ref[...] = (a * _sigmoid(a) * u).astype(act_ref.dtype)
        au_ref[...] = p.astype(au_ref.dtype)

    return pl.pallas_call(
        body, grid=(N_DEV,),
        in_specs=[pl.BlockSpec((SEQ, D), lambda j: (0, 0)), pl.BlockSpec((1, D, 2 * FF_PAD), lambda j: (j, 0, 0))],
        out_specs=[pl.BlockSpec((SEQ, FF_PAD), lambda j: (0, j)), pl.BlockSpec((SEQ, 2 * FF_PAD), lambda j: (0, j))],
        out_shape=(SDS((SEQ, FF_HID), bf16), SDS((SEQ, 2 * FF_HID), bf16)), name="ffn_in",
        compiler_params=_params(("parallel",)),
    )(h, w_stack)


def _ffn_bwd_in(dff, w_down_stack, au):
    def body(d_ref, w_ref, au_ref, o_ref):
        dact = _nt(d_ref[...], w_ref[0])
        p = au_ref[...].astype(f32)
        a, u = p[:, :FF_PAD], p[:, FF_PAD:]
        sg = _sigmoid(a)
        o_ref[...] = jnp.concatenate([dact * u * (sg * (1.0 + a * (1.0 - sg))), dact * (a * sg)],
                                     axis=1).astype(o_ref.dtype)

    return pl.pallas_call(
        body, grid=(N_DEV,),
        in_specs=[pl.BlockSpec((SEQ, D), lambda j: (0, 0)), pl.BlockSpec((1, FF_PAD, D), lambda j: (j, 0, 0)),
                  pl.BlockSpec((SEQ, 2 * FF_PAD), lambda j: (0, j))],
        out_specs=pl.BlockSpec((SEQ, 2 * FF_PAD), lambda j: (0, j)),
        out_shape=SDS((SEQ, 2 * FF_HID), bf16), name="ffn_bwd_in", compiler_params=_params(("parallel",)),
    )(dff, w_down_stack, au)


def _final(act, w_down, x1, target, modv, g_final):
    def fn(t, v):
        (fft, x1t, tgt), (mv, g) = t, v
        x2 = x1t + mv[5:6] * fft
        r, xn = _rms(x2)
        err = xn * g - tgt
        dy = err * (1.0 / D)
        dx2 = _rms_bwd(r, xn, dy * g)
        return [dx2, dx2 * mv[5:6]], [_colsum(dy * xn), _colsum(dx2 * fft), _colsum(err * err) * (0.5 / D)]
    return _matmul_rows("nn", act, w_down, x1, fn, [x1, target], [modv, g_final], [(D, f32), (D, bf16)], [D, D, D],
                        name="final")


def _mid_bwd(dau, w_stack, after, x1, dx2, mix, modv, g_ffn):
    def fn(t, v):
        (dh, x1t, dx2t, mt), (mv, g) = t, v
        r, xn = _rms(x1t)
        dn = dh * (1.0 + mv[4:5])
        dx1 = dx2t + _rms_bwd(r, xn, dn * g)
        return [dx1, dx1 * mv[2:3]], [_colsum(dh), _colsum(dh * (xn * g)), _colsum(dn * xn), _colsum(dx1 * mt)]
    return _matmul_rows("nt_stack", dau, w_stack, after, fn, [x1, dx2, mix], [modv, g_ffn], [(D, f32), (D, bf16)],
                        [D, D, D, D], name="mid_bwd")


def _first_bwd(dproj, w_stack, after, x, dx1, modv, g_mix):
    def fn(t, v):
        (dh, xt, dx1t), (mv, g) = t, v
        r, xn = _rms(xt)
        dn = dh * (1.0 + mv[1:2])
        return [dx1t + _rms_bwd(r, xn, dn * g)], [_colsum(dh), _colsum(dh * (xn * g)), _colsum(dn * xn)]
    return _matmul_rows("nt_stack", dproj, w_stack, after, fn, [x, dx1], [modv, g_mix], [(D, f32)], [D, D, D],
                        name="first_bwd")


def _merge_fwd(ya, yb, proj):
    def fn(t, v):
        ya_t, yb_t, ga, gb = t
        return [_sigmoid(ga) * ya_t + _sigmoid(gb) * yb_t], []
    return _rowwise(fn, "merge_fwd", [(ya, D, 0), (yb, D, 0), (proj, D, C_GA // D), (proj, D, C_GB // D)], [],
                    [(D, bf16)])[0]


def _merge_bwd(dmix, w_o, after, ya, yb, proj):
    def fn(t, v):
        dm, ya_t, yb_t, ga, gb = t
        sa, sb = _sigmoid(ga), _sigmoid(gb)
        return [dm * sa, dm * sb, dm * ya_t * (sa * (1.0 - sa)), dm * yb_t * (sb * (1.0 - sb))], []
    return _matmul_rows("nt", dmix, w_o, after, fn, [ya, yb, (proj, D, C_GA // D), (proj, D, C_GB // D)], [],
                        [(D, bf16), (D, bf16), (D, bf16), (D, bf16)], [], name="merge_bwd")


def _rope_tables():
    half = ROT_DIM // 2
    pos = jnp.arange(SEQ, dtype=f32)
    inv_freq = ROPE_THETA ** (-jnp.arange(0, ROT_DIM, 2, dtype=f32) / ROT_DIM)
    ang = pos[:, None] * inv_freq[None, :]
    cos, sin = jnp.cos(ang), jnp.sin(ang)
    pad = jnp.zeros((SEQ, HEAD_DIM - ROT_DIM), f32)
    zero = jnp.zeros((SEQ, half), f32)
    c_head = jnp.concatenate([cos, cos, pad + 1.0], axis=1)
    lo_head = jnp.concatenate([-sin, zero, pad], axis=1)
    hi_head = jnp.concatenate([zero, sin, pad], axis=1)
    return tuple(jnp.concatenate([t, t], axis=1) for t in (c_head, lo_head, hi_head))


def _over_heads(tables):
    return [jnp.tile(t, (1, DIL_W // LANES)) for t in tables]


def _rope_fwd(proj, tables):
    half = ROT_DIM // 2

    def fn(t, v):
        q, k, vv = t[:3]
        c, lo, hi = _over_heads(t[3:])
        rot = lambda z: z * c + pltpu.roll(z, DIL_W - half, 1) * lo + pltpu.roll(z, half, 1) * hi
        return [rot(q) * ATT_SCALE, rot(k), vv], []
    return _rowwise(fn, "rope_fwd", [(proj, DIL_W, C_QB // DIL_W), (proj, DIL_W, C_KB // DIL_W),
                                     (proj, DIL_W, C_VB // DIL_W)] + [(tb, LANES, 0) for tb in tables], [],
                    [(DIL_W, f32)] * 3)


def _rope_bwd(dqs, dks, tables):
    half = ROT_DIM // 2

    def fn(t, v):
        dq_t, dk_t = jnp.concatenate(t[:N_GROUPS], axis=1), jnp.concatenate(t[N_GROUPS:2 * N_GROUPS], axis=1)
        c, lo, hi = _over_heads(t[2 * N_GROUPS:])
        rot_t = lambda z: z * c + pltpu.roll(z * lo, half, 1) + pltpu.roll(z * hi, DIL_W - half, 1)
        return [rot_t(dq_t), rot_t(dk_t)], []
    return _rowwise(fn, "rope_bwd", [(a, DIL_OUT_W, 0) for a in (*dqs, *dks)] + [(tb, LANES, 0) for tb in tables],
                    [], [(DIL_W, bf16), (DIL_W, bf16)])


def _head_bcast_sum(d):
    lane = lax.broadcasted_iota(jnp.int32, d.shape, 1)
    out = jnp.zeros_like(d)
    for h in range(d.shape[1] // HEAD_DIM):
        sel = (lane >= h * HEAD_DIM) & (lane < (h + 1) * HEAD_DIM)
        out = jnp.where(sel, jnp.sum(jnp.where(sel, d, 0.0), axis=1, keepdims=True), out)
    return out


def _dil_combine(outs, lses):
    def fn(t, v):
        o0, o1, o2, l0, l1, l2 = t
        m = jnp.maximum(jnp.maximum(l0, l1), l2)
        w0, w1, w2 = jnp.exp(l0 - m), jnp.exp(l1 - m), jnp.exp(l2 - m)
        tot = w0 + w1 + w2
        return [(w0 * o0 + w1 * o1 + w2 * o2) / tot, m + jnp.log(tot)], []
    w = DIL_OUT_W
    return _rowwise(fn, "dil_combine", [(t, w, 0) for t in (*outs, *lses)], [], [(w, f32), (w, f32)])


def _dil_delta(dyb_h, yb_h):
    def fn(t, v):
        return [_head_bcast_sum(t[0] * t[1])], []
    return _rowwise(fn, "dil_delta", [(dyb_h, DIL_OUT_W, 0), (yb_h, DIL_OUT_W, 0)], [], [(DIL_OUT_W, f32)])[0]


def _adamw_math(wt, gt, mt, vt):
    mn = ADAM_B1 * mt + (1.0 - ADAM_B1) * gt
    vn = ADAM_B2 * vt + (1.0 - ADAM_B2) * (gt * gt)
    m_hat = mn / (1.0 - ADAM_B1 ** ADAM_STEP)
    v_hat = vn / (1.0 - ADAM_B2 ** ADAM_STEP)
    return -ADAM_LR * (m_hat / (jnp.sqrt(v_hat) + ADAM_EPS) + ADAM_WD * wt), mn, vn


def _adamw(w, g, m, v, name):
    shape = w.shape
    if w.ndim == 1:
        w, g, m, v = (t.reshape(1, -1) for t in (w, g, m, v))
    rows, cols = w.shape
    if rows % 8 and rows > 8:
        return _adamw_by_cols(w, g, m, v, name)
    tile = 256 if rows % 256 == 0 and rows > 512 else rows

    def fn(t, _):
        return list(_adamw_math(*t)), []
    delta, mn, vn = _rowwise(fn, name, [(w, cols, 0), (g, cols, 0), (m, cols, 0), (v, cols, 0)], [],
                             [(cols, f32)] * 3, tile=tile)
    return delta.reshape(shape), mn.reshape(shape), vn.reshape(shape)


def _adamw_by_cols(w, g, m, v, name, tile=256):
    rows, cols = w.shape

    def body(w_ref, g_ref, m_ref, v_ref, d_ref, mn_ref, vn_ref):
        d_ref[...], mn_ref[...], vn_ref[...] = _adamw_math(w_ref[...], g_ref[...], m_ref[...], v_ref[...])

    spec = pl.BlockSpec((rows, tile), lambda j: (0, j))
    return pl.pallas_call(body, grid=(cols // tile,), in_specs=[spec] * 4, out_specs=[spec] * 3,
                          out_shape=[SDS((rows, cols), f32)] * 3, name=name,
                          compiler_params=_params(("parallel",)))(w, g, m, v)


def _ada_fwd(c_all, w_shard, b_shard):
    def body(c_ref, w_ref, b_ref, o_ref):
        cv = c_ref[...]
        sc = (cv * _sigmoid(cv)).astype(bf16)
        o_ref[...] = jnp.dot(sc, w_ref[...].astype(bf16), preferred_element_type=f32) + b_ref[...]
    return pl.pallas_call(body, out_shape=SDS((N_DEV, w_shard.shape[1]), f32), name="ada_fwd",
                          compiler_params=_params())(c_all, w_shard, b_shard)


def _ada_bwd(c_all, dmod_cols):
    def body(c_ref, d_ref, o_ref):
        cv = c_ref[...]
        sc = cv * _sigmoid(cv)
        o_ref[...] = lax.dot_general(sc, d_ref[...], (((0,), (0,)), ((), ())), precision=lax.Precision.HIGHEST,
                                     preferred_element_type=f32)
    return pl.pallas_call(body, out_shape=SDS((D, dmod_cols.shape[1]), f32), name="ada_bwd",
                          compiler_params=_params())(c_all, dmod_cols)


def _small_reduce(gathered, after):
    def body(g_ref, after_ref, o_ref, loss_ref):
        acc = g_ref[0]
        for d in range(1, N_DEV):
            acc = acc + g_ref[d]
        o_ref[...] = acc
        loss_ref[...] = jnp.zeros((1, LANES), f32) + jnp.sum(acc[10:11, :])
    return pl.pallas_call(body, out_shape=(SDS((SMALL_ROWS, D), f32), SDS((1, LANES), f32)), name="small_reduce",
                          in_specs=[pl.BlockSpec(memory_space=pltpu.VMEM), pl.BlockSpec(memory_space=pl.ANY)],
                          compiler_params=_params())(gathered, after)


FOX_BLK = 512
CUM_BLK = 128


def _fold_lanes(t, op):
    out = t[:, :LANES]
    for j in range(1, t.shape[1] // LANES):
        out = op(out, t[:, j * LANES:(j + 1) * LANES])
    return out


def _fox_gate_fwd(proj, b_pad):
    nblk = SEQ // CUM_BLK

    def body(f_ref, b_ref, col_ref):
        r = lax.broadcasted_iota(jnp.int32, (CUM_BLK, CUM_BLK), 0)
        c = lax.broadcasted_iota(jnp.int32, (CUM_BLK, CUM_BLK), 1)
        tri = (r >= c).astype(f32)
        carry = jnp.zeros((1, LANES), f32)
        for blk in range(nblk):
            z = f_ref[blk * CUM_BLK:(blk + 1) * CUM_BLK, :] + b_ref[...]
            logf = jnp.minimum(z, 0.0) - jnp.log1p(jnp.exp(-jnp.abs(z)))
            cs = jnp.dot(tri, logf, precision=lax.Precision.HIGHEST, preferred_element_type=f32) + carry
            col_ref[blk * CUM_BLK:(blk + 1) * CUM_BLK, :] = cs
            carry = cs[CUM_BLK - 1:CUM_BLK, :]

    return pl.pallas_call(
        body, grid=(1,), in_specs=[pl.BlockSpec((SEQ, LANES), lambda i: (0, C_F // LANES)),
                                   pl.BlockSpec((1, LANES), lambda i: (0, 0))],
        out_specs=pl.BlockSpec((SEQ, LANES), lambda i: (0, 0)),
        out_shape=SDS((SEQ, LANES), f32), name="fox_gate_fwd",
        compiler_params=_params(("arbitrary",)),
    )(proj, b_pad)


def _fox_gate_bwd(dF_row, proj, b_pad):
    nblk = SEQ // CUM_BLK

    def body(d_ref, f_ref, b_ref, df_ref, db_ref, col_ref):
        r = lax.broadcasted_iota(jnp.int32, (CUM_BLK, CUM_BLK), 0)
        c = lax.broadcasted_iota(jnp.int32, (CUM_BLK, CUM_BLK), 1)
        tri = (r <= c).astype(f32)
        lane = lax.broadcasted_iota(jnp.int32, (CUM_BLK, LANES), 1)
        col_ref[...] = d_ref[...].T
        carry = jnp.zeros((1, LANES), f32)
        total = jnp.zeros((1, LANES), f32)
        for blk in reversed(range(nblk)):
            rows = slice(blk * CUM_BLK, (blk + 1) * CUM_BLK)
            cs = jnp.dot(tri, col_ref[rows, :], precision=lax.Precision.HIGHEST, preferred_element_type=f32) + carry
            carry = cs[0:1, :]
            z = f_ref[rows, :] + b_ref[...]
            df = jnp.where(lane < N_FOX_HEADS, cs * _sigmoid(-z), 0.0)
            df_ref[rows, :] = df.astype(df_ref.dtype)
            total = total + _colsum(df)
        db_ref[...] = total

    return pl.pallas_call(
        body, grid=(1,), in_specs=[pl.BlockSpec((LANES, SEQ), lambda i: (0, 0)),
                                   pl.BlockSpec((SEQ, LANES), lambda i: (0, C_F // LANES)),
                                   pl.BlockSpec((1, LANES), lambda i: (0, 0))],
        out_specs=[pl.BlockSpec((SEQ, LANES), lambda i: (0, 0)), pl.BlockSpec((1, LANES), lambda i: (0, 0))],
        out_shape=(SDS((SEQ, LANES), bf16), SDS((1, LANES), f32)), name="fox_gate_bwd",
        scratch_shapes=[pltpu.VMEM((SEQ, LANES), f32)],
        compiler_params=_params(("arbitrary",)),
    )(dF_row, proj, b_pad)


def _nt(a, b):
    return lax.dot_general(a, b, (((1,), (1,)), ((), ())), preferred_element_type=f32)


def _tn(a, b):
    return lax.dot_general(a, b, (((0,), (0,)), ((), ())), preferred_element_type=f32)


def _fox_prep(proj, f_col):
    def fn(t, v):
        q, k, vv, fc = t
        lane = lax.broadcasted_iota(jnp.int32, (q.shape[0], LANES), 1)
        qs, ks = [], []
        for h in range(N_FOX_HEADS):
            pair, pos = divmod(h, 2)
            own = (lane >= pos * HEAD_DIM) & (lane < (pos + 1) * HEAD_DIM)
            base = (1 - pos) * HEAD_DIM
            f = fc[:, h:h + 1]
            hi = f.astype(bf16).astype(f32)
            mid = (f - hi).astype(bf16).astype(f32)
            lo = (f - hi) - mid
            one = jnp.ones_like(f)
            qa = jnp.where(own, q[:, pair * LANES:(pair + 1) * LANES] * ATT_SCALE, 0.0)
            ka = k[:, pair * LANES:(pair + 1) * LANES]
            for idx, (qv, kv) in enumerate([(hi, one), (mid, one), (lo, one), (one, -hi), (one, -mid), (one, -lo)]):
                sel = lane == base + idx
                qa = jnp.where(sel, qv, qa)
                ka = jnp.where(sel, kv, ka)
            qs.append(qa)
            ks.append(ka)
        return [jnp.concatenate(qs, axis=1), jnp.concatenate(ks, axis=1), vv], []
    w = N_FOX_HEADS * LANES
    return _rowwise(fn, "fox_prep", [(proj, FOX_W, C_QA // FOX_W), (proj, FOX_W, C_KA // FOX_W),
                                     (proj, FOX_W, C_VA // FOX_W), (f_col, LANES, 0)], [],
                    [(w, bf16), (w, bf16), (FOX_W, bf16)])


def _fox_fwd(q_aug, k_aug, v):
    blk = FOX_BLK
    npair = FOX_W // LANES

    def body(q_ref, k_ref, v_ref, o_ref, max_ref, sum_ref, s_scr):
        i = pl.program_id(1)
        tri = lax.broadcasted_iota(jnp.int32, (blk, blk), 0) >= lax.broadcasted_iota(jnp.int32, (blk, blk), 1)
        qh = [q_ref[:, h * LANES:(h + 1) * LANES] for h in range(2)]

        def logits(c, masked):
            off = pl.multiple_of(c * blk, blk)
            tops = []
            for h in range(2):
                s = _nt(qh[h], k_ref[pl.ds(off, blk), h * LANES:(h + 1) * LANES])
                if masked:
                    s = jnp.where(tri, s, NEG)
                s_scr[h, :, pl.ds(off, blk)] = s
                tops.append(_fold_lanes(s, jnp.maximum))
            return tops

        def pass_a(c, m):
            return tuple(jnp.maximum(a, b) for a, b in zip(m, logits(c, False)))

        m = lax.fori_loop(0, i, pass_a, tuple(jnp.full((blk, LANES), NEG, f32) for _ in range(2)))
        mx = [jnp.max(jnp.maximum(a, b), axis=1, keepdims=True) for a, b in zip(m, logits(i, True))]

        def pass_b(c, carry):
            off = pl.multiple_of(c * blk, blk)
            vv = v_ref[pl.ds(off, blk), :]
            new = []
            for h in range(2):
                l, acc = carry[h]
                p = jnp.exp(s_scr[h, :, pl.ds(off, blk)] - mx[h]).astype(bf16)
                new.append((l + _fold_lanes(p.astype(f32), jnp.add), acc + jnp.dot(p, vv, preferred_element_type=f32)))
            return tuple(new)

        zero = jnp.zeros((blk, LANES), f32)
        (l_a, acc_a), (l_b, acc_b) = lax.fori_loop(0, i + 1, pass_b, ((zero, zero), (zero, zero)))
        l_a = jnp.sum(l_a, axis=1, keepdims=True)
        l_b = jnp.sum(l_b, axis=1, keepdims=True)
        first = lax.broadcasted_iota(jnp.int32, (blk, LANES), 1) < HEAD_DIM
        o_ref[...] = jnp.where(first, acc_a / l_a, acc_b / l_b)
        max_ref[0] = jnp.where(first, mx[0], mx[1])
        sum_ref[0] = jnp.where(first, l_a, l_b)

    return pl.pallas_call(
        body, grid=(npair, SEQ // blk),
        in_specs=[pl.BlockSpec((blk, 2 * LANES), lambda p, i: (i, p)),
                  pl.BlockSpec((SEQ, 2 * LANES), lambda p, i: (0, p)),
                  pl.BlockSpec((SEQ, LANES), lambda p, i: (0, p))],
        out_specs=[pl.BlockSpec((blk, LANES), lambda p, i: (i, p))]
        + [pl.BlockSpec((1, blk, LANES), lambda p, i: (p, i, 0))] * 2,
        out_shape=(SDS((SEQ, FOX_W), f32),) + (SDS((npair, SEQ, LANES), f32),) * 2, name="fox_fwd",
        scratch_shapes=[pltpu.VMEM((2, blk, SEQ), f32)],
        compiler_params=_params(("parallel", "arbitrary")),
    )(q_aug, k_aug, v)


def _fox_bwd(q_aug, k_aug, v, do, o, row_max, row_sum, after):
    blk = FOX_BLK
    npair = FOX_W // LANES
    nblk = SEQ // blk

    def body(q_ref, k_ref, v_ref, do_ref, o_ref, max_ref, sum_ref, after_ref, dq_ref, dk_ref, dv_ref, df_ref, dq_acc,
             delta_ref, inv_ref):
        inv_ref[...] = 1.0 / sum_ref[0]
        lane_s = lax.broadcasted_iota(jnp.int32, (SEQ, LANES), 1)
        prod = do_ref[...].astype(bf16).astype(f32) * o_ref[...]
        d_a = jnp.sum(jnp.where(lane_s < HEAD_DIM, prod, 0.0), axis=1, keepdims=True)
        d_b = jnp.sum(jnp.where(lane_s >= HEAD_DIM, prod, 0.0), axis=1, keepdims=True)
        delta_ref[...] = jnp.where(lane_s < HEAD_DIM, d_a, d_b)
        dq_acc[...] = jnp.zeros_like(dq_acc)
        df_ref[...] = jnp.zeros_like(df_ref)
        lane = lax.broadcasted_iota(jnp.int32, (blk, LANES), 1)
        own = [lane < HEAD_DIM, lane >= HEAD_DIM]
        tri = lax.broadcasted_iota(jnp.int32, (blk, blk), 0) >= lax.broadcasted_iota(jnp.int32, (blk, blk), 1)

        def q_slab(qoff, h):
            return q_ref[pl.ds(qoff, blk), h * LANES:(h + 1) * LANES]

        def probs(qoff, h, k_h, masked):
            s = _nt(q_slab(qoff, h), k_h)
            if masked:
                s = jnp.where(tri, s, NEG)
            col = slice(h * HEAD_DIM, h * HEAD_DIM + 1)
            weights = jnp.exp(s - max_ref[0, pl.ds(qoff, blk), col]).astype(bf16).astype(f32)
            return weights * inv_ref[pl.ds(qoff, blk), col]

        def k_slabs(koff):
            return [k_ref[pl.ds(koff, blk), h * LANES:(h + 1) * LANES] for h in range(2)]

        def kv_step(kj, _):
            koff = pl.multiple_of(kj * blk, blk)
            k_aug = k_slabs(koff)
            k_own = [jnp.where(own[h], k_aug[h], jnp.zeros_like(k_aug[h])) for h in range(2)]
            vv = v_ref[pl.ds(koff, blk), :]
            v_own = [jnp.where(own[h], vv, jnp.zeros_like(vv)) for h in range(2)]

            def q_tile(qi, carry, masked):
                qoff = pl.multiple_of(qi * blk, blk)
                dd = do_ref[pl.ds(qoff, blk), :].astype(bf16)
                new, dq_add = [], None
                for h in range(2):
                    dk_h, dv_h, dcol = carry[h]
                    p = probs(qoff, h, k_aug[h], masked)
                    dl = p * (_nt(dd, v_own[h]) - delta_ref[pl.ds(qoff, blk), h * HEAD_DIM:h * HEAD_DIM + 1])
                    dlb = dl.astype(bf16)
                    part = jnp.dot(dlb, k_own[h], preferred_element_type=f32)
                    dq_add = part if dq_add is None else dq_add + part
                    new.append((dk_h + _tn(dlb, q_slab(qoff, h)), dv_h + _tn(p.astype(bf16), dd),
                                dcol + _colsum(dl)))
                dq_acc[pl.ds(qoff, blk), :] += dq_add * ATT_SCALE
                return tuple(new)

            zero = (jnp.zeros((blk, LANES), f32), jnp.zeros((blk, LANES), f32), jnp.zeros((1, blk), f32))
            carry = q_tile(kj, (zero, zero), True)
            (dk_a, dv_a, dcol_a), (dk_b, dv_b, dcol_b) = lax.fori_loop(
                kj + 1, nblk, lambda qi, cr: q_tile(qi, cr, False), carry)
            dk_ref[pl.ds(koff, blk), :] = jnp.where(own[0], dk_a, dk_b).astype(dk_ref.dtype)
            dv_ref[pl.ds(koff, blk), :] = jnp.where(own[0], dv_a, dv_b).astype(dv_ref.dtype)
            df_ref[0, 0:1, pl.ds(koff, blk)] = -dcol_a
            df_ref[0, 1:2, pl.ds(koff, blk)] = -dcol_b
            return 0

        lax.fori_loop(0, nblk, kv_step, 0)
        dq_ref[...] = dq_acc[...].astype(dq_ref.dtype)

    pair_aug = pl.BlockSpec((SEQ, 2 * LANES), lambda p: (0, p))
    slab = pl.BlockSpec((SEQ, LANES), lambda p: (0, p))
    per_pair = pl.BlockSpec((1, SEQ, LANES), lambda p: (p, 0, 0))
    rows = pl.BlockSpec((1, 8, SEQ), lambda p: (p, 0, 0))
    return pl.pallas_call(
        body, grid=(npair,),
        in_specs=[pair_aug, pair_aug, slab, slab, slab, per_pair, per_pair, pl.BlockSpec(memory_space=pl.ANY)],
        out_specs=[slab, slab, slab, rows],
        out_shape=(SDS((SEQ, FOX_W), bf16),) * 3 + (SDS((npair, 8, SEQ), f32),), name="fox_bwd",
        scratch_shapes=[pltpu.VMEM((SEQ, LANES), f32)] * 3,
        compiler_params=_params(("parallel",)),
    )(q_aug, k_aug, v, do, o, row_max, row_sum, after)


DIL_BLK = 128
DILATIONS = (1, 4, 16)
N_GROUPS = len(DILATIONS)
DIL_PAIRS = DIL_OUT_W // LANES


def _dil_blocks(d):
    r1 = lax.broadcasted_iota(jnp.int32, (2 * DIL_BLK, DIL_BLK), 0) & (DIL_BLK - 1)
    c1 = lax.broadcasted_iota(jnp.int32, (2 * DIL_BLK, DIL_BLK), 1)
    r2 = lax.broadcasted_iota(jnp.int32, (2 * DIL_BLK, 2 * DIL_BLK), 0) & (DIL_BLK - 1)
    c2 = lax.broadcasted_iota(jnp.int32, (2 * DIL_BLK, 2 * DIL_BLK), 1)
    band = ((c2 < DIL_BLK) & (c2 >= r2)) | ((c2 >= DIL_BLK) & (c2 - DIL_BLK <= r2))
    out = []
    for r in range(d):
        for b in range(SEQ // d // DIL_BLK):
            rows = pl.ds(r + d * DIL_BLK * b, DIL_BLK, stride=d)
            if b == 0:
                out.append((rows, rows, r1 >= c1))
            else:
                out.append((rows, pl.ds(r + d * DIL_BLK * (b - 1), 2 * DIL_BLK, stride=d), band))
    return out


def _stack_heads(t, first):
    zero = jnp.zeros_like(t)
    return jnp.concatenate([jnp.where(first, t, zero), jnp.where(first, zero, t)], axis=0)


def _dil_fwd(q, k, v, g):
    def body(q_ref, k_ref, v_ref, o_ref, lse_ref):
        first = lax.broadcasted_iota(jnp.int32, (DIL_BLK, LANES), 1) < HEAD_DIM
        for rows, krows, mask in _dil_blocks(DILATIONS[g]):
            qv, kk, vv = q_ref[rows, :].astype(bf16), k_ref[krows, :].astype(bf16), v_ref[krows, :].astype(bf16)
            s = jnp.where(mask, _nt(_stack_heads(qv, first), kk), NEG)
            m = jnp.max(s, axis=1, keepdims=True)
            p = jnp.exp(s - m)
            l = jnp.sum(p, axis=1, keepdims=True)
            out = jnp.dot(p.astype(bf16), vv, preferred_element_type=f32) / l
            lse = m + jnp.log(l)
            o_ref[rows, :] = jnp.where(first, out[:DIL_BLK], out[DIL_BLK:])
            lse_ref[rows, :] = jnp.where(first, lse[:DIL_BLK], lse[DIL_BLK:])

    grouped = pl.BlockSpec((SEQ, LANES), lambda p: (0, DIL_PAIRS * g + p))
    own = pl.BlockSpec((SEQ, LANES), lambda p: (0, p))
    shape = SDS((SEQ, DIL_OUT_W), f32)
    return pl.pallas_call(
        body, grid=(DIL_PAIRS,), in_specs=[grouped] * 3, out_specs=[own] * 2, out_shape=(shape, shape),
        name=f"dil_fwd_{DILATIONS[g]}", compiler_params=_params(("parallel",)),
    )(q, k, v)


def _dil_bwd(q, k, v, do, lse, delta, g):
    def body(q_ref, k_ref, v_ref, do_ref, lse_ref, dl_ref, dq_ref, dk_ref, dv_ref):
        first = lax.broadcasted_iota(jnp.int32, (DIL_BLK, LANES), 1) < HEAD_DIM
        dk_ref[...] = jnp.zeros_like(dk_ref)
        dv_ref[...] = jnp.zeros_like(dv_ref)
        for rows, krows, mask in _dil_blocks(DILATIONS[g]):
            qv, kk, vv = q_ref[rows, :].astype(bf16), k_ref[krows, :].astype(bf16), v_ref[krows, :].astype(bf16)
            lsev, delv = lse_ref[rows, :], dl_ref[rows, :]
            q2 = _stack_heads(qv, first)
            do2 = _stack_heads(do_ref[rows, :].astype(bf16), first)
            per_head = lambda t: jnp.concatenate([t[:, 0:1], t[:, HEAD_DIM:HEAD_DIM + 1]], axis=0)
            p = jnp.exp(jnp.where(mask, _nt(q2, kk), NEG) - per_head(lsev))
            dl = (p * (_nt(do2, vv) - per_head(delv))).astype(bf16)
            dq = jnp.dot(dl, kk, preferred_element_type=f32)
            dq_ref[rows, :] = jnp.where(first, dq[:DIL_BLK], dq[DIL_BLK:]) * ATT_SCALE
            dk_ref[krows, :] += _tn(dl, q2)
            dv_ref[krows, :] += _tn(p.astype(bf16), do2)

    grouped = pl.BlockSpec((SEQ, LANES), lambda p: (0, DIL_PAIRS * g + p))
    own = pl.BlockSpec((SEQ, LANES), lambda p: (0, p))
    shape = SDS((SEQ, DIL_OUT_W), f32)
    return pl.pallas_call(
        body, grid=(DIL_PAIRS,), in_specs=[grouped] * 3 + [own] * 3, out_specs=[own] * 3,
        out_shape=(shape, shape, shape), name=f"dil_bwd_{DILATIONS[g]}", compiler_params=_params(("parallel",)),
    )(q, k, v, do, lse, delta)


def _position():
    return lax.axis_index("x"), lax.axis_index("y"), lax.axis_index("c")


def _all_gather(block, name):
    def body(x_ref, out_ref, send_sems, recv_sems, local_sem):
        x, y, c = _position()
        me, sibling = (x, y, c), (x, y, 1 - c)
        chips = [(1 - x, y), (x, 1 - y), (1 - x, 1 - y)]

        def slot(px, py, pc):
            return out_ref.at[4 * px + 2 * py + pc]

        def copy(k, blk, to, src=None):
            return pltpu.make_async_remote_copy(
                src_ref=slot(*blk) if src is None else src, dst_ref=slot(*blk),
                send_sem=send_sems.at[k], recv_sem=recv_sems.at[k], device_id=to, device_id_type=MESH)

        mine = pltpu.make_async_copy(x_ref, slot(*me), local_sem)
        mine.start()
        first = [copy(0, me, sibling, src=x_ref)]
        first += [copy(1 + j, me, (*chip, c), src=x_ref) for j, chip in enumerate(chips)]
        for cp in first:
            cp.start()
        passed = [copy(4 + j, (*chip, c), sibling) for j, chip in enumerate(chips)]
        for j, chip in enumerate(chips):
            copy(1 + j, (*chip, c), me).wait_recv()
            passed[j].start()
        copy(0, sibling, me).wait_recv()
        for j, chip in enumerate(chips):
            copy(4 + j, (*chip, 1 - c), me).wait_recv()
        for cp in first + passed:
            cp.wait_send()
        mine.wait()

    return pl.pallas_call(
        body, out_shape=SDS((N_DEV,) + block.shape, block.dtype),
        in_specs=[pl.BlockSpec(memory_space=pl.ANY)], out_specs=pl.BlockSpec(memory_space=pl.ANY),
        scratch_shapes=[pltpu.SemaphoreType.DMA((7,)), pltpu.SemaphoreType.DMA((7,)), pltpu.SemaphoreType.DMA],
        name=name,
    )(block)


HBM_SPEC = pl.BlockSpec(memory_space=pltpu.HBM)
SEM_SPEC = pl.BlockSpec(memory_space=pltpu.SEMAPHORE)
SPLIT_COPY = pltpu.CompilerParams(has_side_effects=pltpu.SideEffectType.DATAFLOW_SIDE_EFFECTING)


def _in_hbm(t):
    return pltpu.with_memory_space_constraint(t, pltpu.HBM)


def _pair_copies(g_refs, land_refs, send_sems, recv_sems):
    x, y, c = _position()
    return [pltpu.make_async_remote_copy(
        src_ref=g.at[2 * k + (1 - c)], dst_ref=land.at[k], send_sem=send_sems.at[4 * a + k],
        recv_sem=recv_sems.at[4 * a + k], device_id=(x, y, 1 - c), device_id_type=MESH)
        for a, (g, land) in enumerate(zip(g_refs, land_refs, strict=True)) for k in range(4)]


def _chip_copies(t_refs, land_refs, send_sems, recv_sems):
    x, y, c = _position()
    chips = [(1 - x, y), (x, 1 - y), (1 - x, 1 - y)]
    return [pltpu.make_async_remote_copy(
        src_ref=t.at[2 * px + py], dst_ref=land.at[j], send_sem=send_sems.at[3 * a + j],
        recv_sem=recv_sems.at[3 * a + j], device_id=(px, py, c), device_id_type=MESH)
        for a, (t, land) in enumerate(zip(t_refs, land_refs, strict=True)) for j, (px, py) in enumerate(chips)]


_ROUNDS = {"pair": (_pair_copies, 4), "chip": (_chip_copies, 3)}


def _exchange_start(kind, ts, name):
    copies, slots = _ROUNDS[kind]
    n = len(ts)
    lands = [_in_hbm(lax.empty((slots,) + t.shape[1:], t.dtype)) for t in ts]

    def body(*refs):
        for cp in copies(refs[:n], refs[n:2 * n], refs[2 * n], refs[2 * n + 1]):
            cp.start()
        refs[-1][...] = jnp.zeros_like(refs[-1])

    sems = pltpu.SemaphoreType.DMA((slots * n,))
    res = pl.pallas_call(
        body, name=name, in_specs=[HBM_SPEC] * (2 * n),
        out_shape=(sems, sems, *[pltpu.HBM(t.shape, t.dtype) for t in (*ts, *lands)], SDS((8, LANES), f32)),
        out_specs=(SEM_SPEC, SEM_SPEC, *[HBM_SPEC] * (2 * n), pl.BlockSpec(memory_space=pltpu.VMEM)),
        input_output_aliases={i: 2 + i for i in range(2 * n)}, compiler_params=SPLIT_COPY,
    )(*[_in_hbm(t) for t in ts], *lands)
    return res[:-1], res[-1]


def _exchange_wait(kind, state, after, name):
    copies, _ = _ROUNDS[kind]
    send_sems, recv_sems, *arrays = state
    n = len(arrays) // 2

    def body(*refs):
        for cp in copies(refs[:n], refs[n:2 * n], refs[2 * n], refs[2 * n + 1]):
            cp.wait_send()
            cp.wait_recv()

    res = pl.pallas_call(
        body, name=name, in_specs=[HBM_SPEC] * (2 * n) + [SEM_SPEC, SEM_SPEC, pl.BlockSpec(memory_space=pl.ANY)],
        out_shape=[pltpu.HBM(t.shape, t.dtype) for t in arrays], out_specs=[HBM_SPEC] * (2 * n),
        input_output_aliases={i: i for i in range(2 * n)}, compiler_params=SPLIT_COPY,
    )(*arrays, send_sems, recv_sems, after)
    return res[:n], res[n:]


def _gather_copies(x_refs, out_refs, send_sems, recv_sems):
    x, y, c = _position()
    peers = [(x, y, 1 - c), (1 - x, y, c), (x, 1 - y, c), (1 - x, 1 - y, c)]
    sends, arrivals = [], []
    for a, (x_ref, out_ref) in enumerate(zip(x_refs, out_refs, strict=True)):
        for k, (px, py, pc) in enumerate(peers):
            sems = dict(send_sem=send_sems.at[4 * a + k], recv_sem=recv_sems.at[4 * a + k],
                        device_id=(px, py, pc), device_id_type=MESH)
            sends.append(pltpu.make_async_remote_copy(src_ref=x_ref, dst_ref=out_ref.at[4 * x + 2 * y + c], **sems))
            arrivals.append(pltpu.make_async_remote_copy(src_ref=x_ref, dst_ref=out_ref.at[4 * px + 2 * py + pc],
                                                         **sems))
    return sends, arrivals


def _gather_start(blocks, after, name):
    n = len(blocks)
    outs = [_in_hbm(lax.empty((N_DEV,) + b.shape, b.dtype)) for b in blocks]

    def body(*refs):
        sends, _ = _gather_copies(refs[:n], refs[n:2 * n], refs[2 * n + 1], refs[2 * n + 2])
        for cp in sends:
            cp.start()
        refs[-1][...] = jnp.zeros_like(refs[-1])

    sems = pltpu.SemaphoreType.DMA((4 * n,))
    res = pl.pallas_call(
        body, name=name, in_specs=[HBM_SPEC] * (2 * n) + [pl.BlockSpec(memory_space=pl.ANY)],
        out_shape=(sems, sems, *[pltpu.HBM(t.shape, t.dtype) for t in (*blocks, *outs)], SDS((8, LANES), f32)),
        out_specs=(SEM_SPEC, SEM_SPEC, *[HBM_SPEC] * (2 * n), pl.BlockSpec(memory_space=pltpu.VMEM)),
        input_output_aliases={i: 2 + i for i in range(2 * n)}, compiler_params=SPLIT_COPY,
    )(*[_in_hbm(b) for b in blocks], *outs, after)
    return res[:-1], res[-1]


def _gather_wait(state, after, name):
    send_sems, recv_sems, *arrays = state
    n = len(arrays) // 2

    def body(*refs):
        sends, arrivals = _gather_copies(refs[:n], refs[n:2 * n], refs[2 * n], refs[2 * n + 1])
        for cp in sends:
            cp.wait_send()
        for cp in arrivals:
            cp.wait_recv()

    res = pl.pallas_call(
        body, name=name, in_specs=[HBM_SPEC] * (2 * n) + [SEM_SPEC, SEM_SPEC, pl.BlockSpec(memory_space=pl.ANY)],
        out_shape=[pltpu.HBM(t.shape, t.dtype) for t in arrays], out_specs=[HBM_SPEC] * (2 * n),
        input_output_aliases={i: i for i in range(2 * n)}, compiler_params=SPLIT_COPY,
    )(*arrays, send_sems, recv_sems, after)
    return res[:n], res[n:]


def _gather_finish(partial, name):
    n = len(partial)

    def body(*refs):
        in_refs, out_refs = refs[:n], refs[n:2 * n]
        send_sems, recv_sems = refs[2 * n:]
        x, y, c = _position()
        chips = [(1 - x, y), (x, 1 - y), (1 - x, 1 - y)]
        copies = []
        for a in range(n):
            for j, (px, py) in enumerate(chips):
                cp = pltpu.make_async_remote_copy(
                    src_ref=in_refs[a].at[4 * px + 2 * py + c], dst_ref=out_refs[a].at[4 * px + 2 * py + c],
                    send_sem=send_sems.at[a, j], recv_sem=recv_sems.at[a, j], device_id=(x, y, 1 - c),
                    device_id_type=MESH)
                cp.start()
                copies.append(cp)
        for a in range(n):
            for j, (px, py) in enumerate(chips):
                pltpu.make_async_remote_copy(
                    src_ref=in_refs[a].at[4 * px + 2 * py + (1 - c)], dst_ref=out_refs[a].at[4 * px + 2 * py + (1 - c)],
                    send_sem=send_sems.at[a, j], recv_sem=recv_sems.at[a, j], device_id=(x, y, 1 - c),
                    device_id_type=MESH).wait_recv()
        for cp in copies:
            cp.wait_send()

    hbm = pl.BlockSpec(memory_space=pl.ANY)
    return pl.pallas_call(
        body, out_shape=[SDS(p.shape, p.dtype) for p in partial], in_specs=[hbm] * n, out_specs=[hbm] * n,
        input_output_aliases={a: a for a in range(n)},
        scratch_shapes=[pltpu.SemaphoreType.DMA((n, 3)), pltpu.SemaphoreType.DMA((n, 3))],
        name=name,
    )(*partial)


def _row_tile(rows):
    return 512 if rows % 512 == 0 and rows > 512 else rows


def _pair_add(g, r1, core, name):
    def body(c_ref, g_ref, r_ref, o_ref):
        o_ref[...] = (g_ref[...].astype(f32) + r_ref[...].astype(f32)).astype(o_ref.dtype)

    rows, cols = g.shape[1:]
    tile = _row_tile(rows)
    blk = (1, tile, cols)
    return pl.pallas_call(
        body, out_shape=SDS((4, rows, cols), g.dtype), name=name,
        grid_spec=pltpu.PrefetchScalarGridSpec(
            num_scalar_prefetch=1, grid=(4, rows // tile),
            in_specs=[pl.BlockSpec(blk, lambda k, i, c_ref: (2 * k + c_ref[0], i, 0)),
                      pl.BlockSpec(blk, lambda k, i, c_ref: (k, i, 0))],
            out_specs=pl.BlockSpec(blk, lambda k, i, c_ref: (k, i, 0))),
        compiler_params=_params(("parallel", "arbitrary")),
    )(core, g, r1)


def _chip_add(t, r2, chip, name):
    def body(c_ref, t_ref, r_ref, o_ref):
        o_ref[...] = ((t_ref[0].astype(f32) + r_ref[0].astype(f32)) + r_ref[1].astype(f32)) + r_ref[2].astype(f32)

    rows, cols = t.shape[1:]
    tile = _row_tile(rows)
    return pl.pallas_call(
        body, out_shape=SDS((rows, cols), f32), name=name,
        grid_spec=pltpu.PrefetchScalarGridSpec(
            num_scalar_prefetch=1, grid=(rows // tile,),
            in_specs=[pl.BlockSpec((1, tile, cols), lambda i, c_ref: (c_ref[0], i, 0)),
                      pl.BlockSpec((3, tile, cols), lambda i, c_ref: (0, i, 0))],
            out_specs=pl.BlockSpec((tile, cols), lambda i, c_ref: (i, 0))),
        compiler_params=_params(("arbitrary",)),
    )(chip, t, r2)


def _pad_to(t, axis, size):
    pads = [(0, 0)] * t.ndim
    pads[axis] = (0, size - t.shape[axis])
    return jnp.pad(t, pads)


_REF_COLS = {"qa": (0, FOX_W), "ka": (FOX_W, FOX_W), "va": (2 * FOX_W, FOX_W), "f": (3 * FOX_W, N_FOX_HEADS)}
_REF_COLS.update({n: (3 * FOX_W + N_FOX_HEADS + i * DIL_W, DIL_W) for i, n in enumerate(("qb", "kb", "vb"))})
_REF_COLS.update({n: (3 * FOX_W + N_FOX_HEADS + 3 * DIL_W + i * D, D) for i, n in enumerate(("ga", "gb"))})
_REF_ORDER = ("qa", "ka", "va", "f", "qb", "kb", "vb", "ga", "gb")


def _place_cols(sources, src_of, out_cols, name, row_block=512):
    arrays = [s[0] if isinstance(s, tuple) else s for s in sources]
    widths = [a.shape[-1] for a in arrays]
    rows = arrays[0].shape[-2]
    plan = []
    for t in range(out_cols // LANES):
        segs, c, end = [], t * LANES, (t + 1) * LANES
        while c < end:
            s = src_of(c)
            if s is None:
                c += 1
                continue
            n = 1
            while c + n < end and src_of(c + n) == (s[0], s[1] + n):
                n += 1
            segs.append((s[0], s[1], c - t * LANES, n))
            c += n
        plan.append(segs)

    def body(*refs):
        o_ref = refs[-1]
        for t, segs in enumerate(plan):
            acc = None
            for si, c0, o0, n in segs:
                a0 = c0 // LANES * LANES
                wide = min(2 * LANES, widths[si] - a0)
                win = refs[si][0, :, a0:a0 + wide] if isinstance(sources[si], tuple) else refs[si][:, a0:a0 + wide]
                r = lax.broadcasted_iota(jnp.int32, (wide, LANES), 0)
                c = lax.broadcasted_iota(jnp.int32, (wide, LANES), 1)
                pick = ((r - (c0 - a0) == c - o0) & (c >= o0) & (c < o0 + n)).astype(bf16)
                part = jnp.dot(win.astype(bf16), pick, preferred_element_type=f32)
                acc = part if acc is None else acc + part
            tile = jnp.zeros((row_block, LANES), f32) if acc is None else acc
            o_ref[:, t * LANES:(t + 1) * LANES] = tile.astype(o_ref.dtype)

    def spec(s):
        if isinstance(s, tuple):
            j = s[1]
            return pl.BlockSpec((1, row_block, s[0].shape[-1]), lambda i: (j, i, 0))
        return pl.BlockSpec((row_block, s.shape[-1]), lambda i: (i, 0))

    return pl.pallas_call(
        body, grid=(rows // row_block,), in_specs=[spec(s) for s in sources],
        out_specs=pl.BlockSpec((row_block, out_cols), lambda i: (i, 0)), out_shape=SDS((rows, out_cols), bf16),
        name=name, compiler_params=_params(("parallel",)),
    )(*arrays)


def _ref_piece(r):
    for name in _REF_ORDER:
        lo, width = _REF_COLS[name]
        if lo <= r < lo + width:
            return name, r - lo
    raise ValueError(r)


def _shard_pad_cols(pieces):
    names = [n for n in _REF_ORDER if n != "vb"]
    sources = [pieces[n] for n in names] + list(pieces["vb"])

    def src_of(c):
        j, i = divmod(c, W_IN_PAD)
        if i >= W_IN_SH:
            return None
        name, col = _ref_piece(j * W_IN_SH + i)
        if name == "vb":
            return len(names) + col // DIL_OUT_W, col % DIL_OUT_W
        return names.index(name), col

    return _place_cols(sources, src_of, N_DEV * W_IN_PAD, "place_dproj")


_SLABS = {"ga": C_GA, "gb": C_GB, "qb": C_QB, "kb": C_KB, "vb": C_VB, "qa": C_QA, "ka": C_KA, "va": C_VA, "f": C_F}


def _slab_w_in(stack):
    def src_of(c):
        for name, start in _SLABS.items():
            lo, width = _REF_COLS[name]
            if start <= c < start + width:
                return divmod(lo + c - start, W_IN_SH)
        return None

    return _place_cols([(stack, j) for j in range(N_DEV)], src_of, PROJ_W, "place_w_in")


def kernel(x, c, w_ada, b_ada, g_mix, w_in, b_fgate, w_br_a, w_br_b, w_out, g_ffn, w_ffn_gate, w_ffn_up, w_ffn_down, g_final, loss_target, m_w_ada, m_b_ada, m_g_mix, m_w_in, m_b_fgate, m_w_br_a, m_w_br_b, m_w_out, m_g_ffn, m_w_ffn_gate, m_w_ffn_up, m_w_ffn_down, m_g_final, v_w_ada, v_b_ada, v_g_mix, v_w_in, v_b_fgate, v_w_br_a, v_w_br_b, v_w_out, v_g_ffn, v_w_ffn_gate, v_w_ffn_up, v_w_ffn_down, v_g_final):
    px, py, pc = _position()
    dev = 4 * px + 2 * py + pc
    x2d, tgt = x[0], loss_target[0]

    c_all = _all_gather(c, "gather_c").reshape(N_DEV, D)
    ada_cols = w_ada.shape[2]
    b_shard = lax.dynamic_slice(b_ada, (0, dev * ada_cols), (1, ada_cols))
    mod_shard = _ada_fwd(c_all, w_ada[0], b_shard)
    mod_all = _all_gather(mod_shard, "gather_mod")
    modv = lax.dynamic_index_in_dim(mod_all, dev, axis=1, keepdims=False).reshape(6, D)
    h1 = _pre1(x2d, modv, g_mix)

    w_in_s = _all_gather(_pad_to(w_in[0], 1, W_IN_PAD).astype(bf16), "gather_w_in")
    gate_up = jnp.concatenate([_pad_to(w_ffn_gate[0], 1, FF_PAD), _pad_to(w_ffn_up[0], 1, FF_PAD)], axis=1)
    later = [w_br_a[0], w_br_b[0], w_out[0], gate_up, _pad_to(w_ffn_down[0], 0, FF_PAD)]
    later_state, later_token = _gather_start([t.astype(bf16) for t in later], w_in_s, "gather_rest_start")
    w_in_p = _slab_w_in(w_in_s)

    proj = _matmul(h1, w_in_p, name="mm_proj", tm=SEQ, tn=896, tk=D, after=later_token)
    b_pad = jnp.pad(b_fgate, ((0, 0), (0, LANES - N_FOX_HEADS)))
    q_aug, k_aug, va = _fox_prep(proj, _fox_gate_fwd(proj, b_pad))
    ya_h, max_a, sum_a = _fox_fwd(q_aug, k_aug, va)

    tables = _rope_tables()
    qb_r, kb_r, vb = _rope_fwd(proj, tables)
    by_group = [_dil_fwd(qb_r, kb_r, vb, grp) for grp in range(N_GROUPS)]
    yb_h, lse_b = _dil_combine([o for o, _ in by_group], [l for _, l in by_group])

    mine, arrived = _gather_wait(later_state, yb_h, "gather_rest_wait")
    w_a_s, w_b_s, w_o_s, w_gu_s, w_d_s = [
        lax.dynamic_update_slice(stack, block[None], (dev, 0, 0))
        for stack, block in zip(_gather_finish(arrived, "gather_rest_finish"), mine, strict=True)]
    w_o = w_o_s.reshape(D, D)
    w_d = w_d_s.reshape(FF_HID, D)
    ya = _matmul_stack(ya_h, w_a_s, name="mm_br_a")
    yb = _matmul_stack(yb_h, w_b_s, name="mm_br_b")

    merged = _merge_fwd(ya, yb, proj)
    mix, x1, h2 = _post1(merged, w_o, x2d, modv, g_ffn)
    act, au = _ffn_in(h2, w_gu_s)

    dx2, dff, dg_final, dga_f, loss_lanes = _final(act, w_d, x1, tgt, modv, g_final.reshape(1, D))
    dau = _ffn_bwd_in(dff, w_d_s, au)

    core = pc.astype(jnp.int32).reshape(1)
    chip = (2 * px + py).astype(jnp.int32).reshape(1)

    def pair_done(state, after, tags, name):
        mine, theirs = _exchange_wait("pair", state, after, "pair_wait_" + name)
        sums = [_pair_add(g, r, core, "pair_add_" + t) for g, r, t in zip(mine, theirs, tags)]
        return _exchange_start("chip", sums, "chip_start_" + name)

    def from_chips(state, after, tags, name):
        sums, got = _exchange_wait("chip", state, after, "chip_wait_" + name)
        return [_chip_add(p, r, chip, "chip_add_" + t) for p, r, t in zip(sums, got, tags)]

    g_gu = _matmul(h2, dau, ta=True, by_shard=True, out_dtype=bf16, name="mm_g_ffn_in", tm=D, tn=2 * FF_PAD, tk=SEQ)
    g_d = _matmul(act, dff, ta=True, out_dtype=bf16, name="mm_g_down", tm=FF_HID // 2, tn=512, tk=SEQ)
    ffn_tags = ["gu", "down"]
    ffn_pair, ffn_pair_token = _exchange_start("pair", [g_gu, g_d.reshape(N_DEV, FF_PAD, D)], "pair_start_ffn")

    dx1, dmix, dsh_f, dsc_f, dg_ffn, dga_m = _mid_bwd(dau, w_gu_s, ffn_pair_token, x1, dx2, mix, modv, g_ffn)
    ffn_state, ffn_token = pair_done(ffn_pair, dx1, ffn_tags, "ffn")
    dya, dyb, dga, dgb = _merge_bwd(dmix, w_o, ffn_token, ya, yb, proj)
    dya_h = _matmul_stack(dya, w_a_s, tb=True, name="mm_d_ya")
    dyb_h = _matmul_stack(dyb, w_b_s, tb=True, name="mm_d_yb")

    g_o = _matmul(merged, dmix, ta=True, out_dtype=bf16, name="mm_g_out", tm=D, tn=512, tk=SEQ)
    g_a = _matmul_stack(ya_h, dya, ta=True, out_dtype=bf16, name="mm_g_br_a")
    g_b = _matmul_stack(yb_h, dyb, ta=True, out_dtype=bf16, name="mm_g_br_b")
    rows_a, rows_b = FOX_W * W_BR_SH // D, DIL_OUT_W * W_BR_SH // D
    g_small = jnp.concatenate([g_a.reshape(N_DEV, rows_a, D), g_b.reshape(N_DEV, rows_b, D),
                               g_o.reshape(N_DEV, W_BR_SH, D)], axis=1)
    small_pair, small_pair_token = _exchange_start("pair", [g_small], "pair_start_small")

    dqa, dka, dva, dF = _fox_bwd(q_aug, k_aug, va, dya_h, ya_h, max_a, sum_a, small_pair_token)
    dF_row = jnp.pad(dF[:, :2, :].reshape(N_FOX_HEADS, SEQ), ((0, LANES - N_FOX_HEADS), (0, 0)))
    df, db_fgate = _fox_gate_bwd(dF_row, proj, b_pad)
    small_state, small_token = pair_done(small_pair, df, ["small"], "small")

    delta_b = _dil_delta(dyb_h, yb_h)
    dil_grads = [_dil_bwd(qb_r, kb_r, vb, dyb_h, lse_b, delta_b, grp) for grp in range(N_GROUPS)]
    dqb, dkb = _rope_bwd([t[0] for t in dil_grads], [t[1] for t in dil_grads], tables)

    dproj = _shard_pad_cols({"qa": dqa, "ka": dka, "va": dva, "f": df, "qb": dqb, "kb": dkb,
                             "vb": [t[2] for t in dil_grads], "ga": dga, "gb": dgb})
    g_in = _matmul(h1, dproj, ta=True, by_shard=True, out_dtype=bf16, name="mm_g_in", tm=D, tn=W_IN_PAD, tk=SEQ,
                   after=small_token)
    mix_tags = ["in"]
    mix_pair, mix_pair_token = _exchange_start("pair", [g_in], "pair_start_mixer")

    grad_x, dsh_m, dsc_m, dg_mix = _first_bwd(dproj, w_in_s, mix_pair_token, x2d, dx1, modv, g_mix)

    pad_lane = lambda t: jnp.pad(t, ((0, 0), (0, D - t.shape[1])))
    small = jnp.concatenate([dsh_m, dsc_m, dga_m, dsh_f, dsc_f, dga_f, dg_mix, dg_ffn, dg_final,
                             pad_lane(db_fgate), loss_lanes, jnp.zeros((SMALL_ROWS - 11, D), f32)], axis=0)
    small_all = _all_gather(small, "gather_small")
    mix_state, mix_token = pair_done(mix_pair, small_all, mix_tags, "mixer")

    small_sum, loss_row = _small_reduce(small_all, mix_token)
    dmod_all = small_all[:, :6, :].reshape(N_DEV, 6 * D)
    g_w_ada = _ada_bwd(c_all, lax.dynamic_slice(dmod_all, (0, dev * ada_cols), (N_DEV, ada_cols)))
    s_gu, s_d = from_chips(ffn_state, small_sum, ffn_tags, "ffn")
    s_small, = from_chips(small_state, small_sum, ["small"], "small")

    loss = loss_row[0, 0]
    g = {
        "w_ada": g_w_ada[None], "b_ada": small_sum[0:6].reshape(1, 6 * D), "g_mix": small_sum[6:7],
        "b_fgate": small_sum[9:10, :N_FOX_HEADS], "g_ffn": small_sum[7:8], "w_ffn_gate": s_gu[None, :, :W_FF_SH],
        "w_ffn_up": s_gu[None, :, FF_PAD:FF_PAD + W_FF_SH], "w_ffn_down": s_d[None, :W_FF_SH],
        "g_final": small_sum[8], "w_br_a": s_small[:rows_a].reshape(1, FOX_W, W_BR_SH),
        "w_br_b": s_small[rows_a:rows_a + rows_b].reshape(1, DIL_OUT_W, W_BR_SH), "w_out": s_small[None, rows_a + rows_b:],
    }
    w = {"w_ada": w_ada, "b_ada": b_ada, "g_mix": g_mix, "w_in": w_in, "b_fgate": b_fgate, "w_br_a": w_br_a,
         "w_br_b": w_br_b, "w_out": w_out, "g_ffn": g_ffn, "w_ffn_gate": w_ffn_gate, "w_ffn_up": w_ffn_up,
         "w_ffn_down": w_ffn_down, "g_final": g_final}
    m = {"w_ada": m_w_ada, "b_ada": m_b_ada, "g_mix": m_g_mix, "w_in": m_w_in, "b_fgate": m_b_fgate,
         "w_br_a": m_w_br_a, "w_br_b": m_w_br_b, "w_out": m_w_out, "g_ffn": m_g_ffn, "w_ffn_gate": m_w_ffn_gate,
         "w_ffn_up": m_w_ffn_up, "w_ffn_down": m_w_ffn_down, "g_final": m_g_final}
    v = {"w_ada": v_w_ada, "b_ada": v_b_ada, "g_mix": v_g_mix, "w_in": v_w_in, "b_fgate": v_b_fgate,
         "w_br_a": v_w_br_a, "w_br_b": v_w_br_b, "w_out": v_w_out, "g_ffn": v_g_ffn, "w_ffn_gate": v_w_ffn_gate,
         "w_ffn_up": v_w_ffn_up, "w_ffn_down": v_w_ffn_down, "g_final": v_g_final}
    names = list(w)
    delta, new_m, new_v = {}, {}, {}

    transposed = ("w_in", "w_ffn_gate", "w_ffn_up")

    def update(n):
        shape = w[n].shape
        if n in transposed:
            g_t = g[n][0].T
            dl, mn, vn = _adamw(w[n][0].T, g_t, m[n][0].T, v[n][0].T, "adamw_" + n)
            g[n], delta[n], new_m[n], new_v[n] = g_t.T[None], dl.T[None], mn.T[None], vn.T[None]
            return
        two_d = (lambda t: t.reshape(shape[-2:])) if len(shape) == 3 else (lambda t: t)
        dl, mn, vn = _adamw(two_d(w[n]), two_d(g[n]), two_d(m[n]), two_d(v[n]), "adamw_" + n)
        delta[n], new_m[n], new_v[n] = dl.reshape(shape), mn.reshape(shape), vn.reshape(shape)

    for n in list(g):
        update(n)
    done = sum(delta[n].reshape(-1)[:N_FOX_HEADS] for n in g)
    s_in, = from_chips(mix_state, done, mix_tags, "mixer")
    g["w_in"] = s_in[None, :, :W_IN_SH]
    update("w_in")

    return (loss, grad_x[None], *[g[n] for n in names], *[delta[n] for n in names],
            *[new_m[n] for n in names], *[new_v[n] for n in names])
```

```python
import functools

import jax
import jax.numpy as jnp
from jax import lax
from jax.experimental import pallas as pl
from jax.experimental.pallas import tpu as pltpu

f32 = jnp.float32
bf16 = jnp.bfloat16
SDS = jax.ShapeDtypeStruct
MESH = pl.DeviceIdType.MESH

N_DEV = 8
D = 1024
SEQ = 2048
HEAD_DIM = 64
N_FOX_HEADS = 8
FOX_W = 512
DIL_W = 768
DIL_OUT_W = 256
ROT_DIM = 16
ROPE_THETA = 500000.0
D_FF = 2816
IN_COLS = 5896
EPS = 1e-6
NEG = -1e30
ATT_SCALE = HEAD_DIM ** -0.5

ADAM_LR = 0.001
ADAM_B1 = 0.9
ADAM_B2 = 0.999
ADAM_EPS = 1e-08
ADAM_WD = 0.01
ADAM_STEP = 10

C_GA, C_GB, C_QB, C_KB, C_VB, C_QA, C_KA, C_VA, C_F = 0, 1024, 2304, 3072, 3840, 4608, 5120, 5632, 6144
PROJ_W = 6272
LANES = 128
VMEM_LIMIT = 52 * 1024 * 1024

W_IN_SH, W_IN_PAD = IN_COLS // N_DEV, 768
W_BR_SH = D // N_DEV
W_FF_SH, FF_PAD = D_FF // N_DEV, 384
FF_HID = N_DEV * FF_PAD
SMALL_ROWS = 16


def _params(sem=None):
    if sem is None:
        return pltpu.CompilerParams(vmem_limit_bytes=VMEM_LIMIT)
    return pltpu.CompilerParams(dimension_semantics=sem, vmem_limit_bytes=VMEM_LIMIT)


def _rowwise(fn, name, tiled, vecs, outs, reds=(), tile=256):
    nt, nv, no = len(tiled), len(vecs), len(outs)
    rows = tiled[0][0].shape[0]
    assert rows % tile == 0

    def body(*refs):
        tin = [r[...] for r in refs[:nt]]
        vin = [r[...] for r in refs[nt:nt + nv]]
        orefs = refs[nt + nv:nt + nv + no]
        rrefs = refs[nt + nv + no:]
        touts, routs = fn(tin, vin)
        for r, t in zip(orefs, touts, strict=True):
            r[...] = t.astype(r.dtype)
        if rrefs:
            @pl.when(pl.program_id(0) == 0)
            def _():
                for r in rrefs:
                    r[...] = jnp.zeros_like(r)
            for r, t in zip(rrefs, routs, strict=True):
                r[...] += t

    def col_map(cb):
        return lambda i: (i, cb)

    def whole_map(nd):
        return lambda i: (0,) * nd

    in_specs = [pl.BlockSpec((tile, w), col_map(cb)) for (_, w, cb) in tiled]
    in_specs += [pl.BlockSpec(v.shape, whole_map(v.ndim)) for v in vecs]
    out_specs = [pl.BlockSpec((tile, w), lambda i: (i, 0)) for (w, _) in outs]
    out_specs += [pl.BlockSpec((1, w), lambda i: (0, 0)) for w in reds]
    out_shape = [SDS((rows, w), dt) for (w, dt) in outs] + [SDS((1, w), f32) for w in reds]
    res = pl.pallas_call(
        body, grid=(rows // tile,), in_specs=in_specs, out_specs=out_specs, out_shape=out_shape, name=name,
        compiler_params=_params(("arbitrary",)),
    )(*[t[0] for t in tiled], *vecs)
    return res


def _matmul(a, b, *, ta=False, tb=False, out_dtype=f32, name, tm, tn, tk, by_shard=False, after=None):
    m, k = (a.shape[1], a.shape[0]) if ta else a.shape
    if by_shard and not ta:
        n, kb = (b.shape[1], N_DEV * b.shape[2]) if tb else (N_DEV * b.shape[2], b.shape[1])
        assert (tk if tb else tn) == b.shape[2]
    else:
        n, kb = (b.shape[0], b.shape[1]) if tb else (b.shape[1], b.shape[0])
    assert kb == k and m % tm == 0 and n % tn == 0 and k % tk == 0
    nk = k // tk
    dims = (((0 if ta else 1,), (1 if tb else 0,)), ((), ()))
    b_stacked = by_shard and not ta
    o_stacked = by_shard and ta

    def body(a_ref, b_ref, *rest):
        o_ref, *acc = rest[1:] if after is not None else rest
        bv = b_ref[0] if b_stacked else b_ref[...]
        p = lax.dot_general(a_ref[...].astype(bf16), bv.astype(bf16), dims, preferred_element_type=f32)

        def put(val):
            if o_stacked:
                o_ref[0] = val.astype(o_ref.dtype)
            else:
                o_ref[...] = val.astype(o_ref.dtype)

        if nk == 1:
            put(p)
        else:
            acc_ref, = acc
            kk = pl.program_id(2)

            @pl.when(kk == 0)
            def _():
                acc_ref[...] = p

            @pl.when(kk > 0)
            def _():
                acc_ref[...] += p

            @pl.when(kk == nk - 1)
            def _():
                put(acc_ref[...])

    a_spec = pl.BlockSpec((tk, tm), lambda i, j, kk: (kk, i)) if ta else pl.BlockSpec((tm, tk), lambda i, j, kk: (i, kk))
    if b_stacked and tb:
        b_spec = pl.BlockSpec((1, tn, tk), lambda i, j, kk: (kk, j, 0))
    elif b_stacked:
        b_spec = pl.BlockSpec((1, tk, tn), lambda i, j, kk: (j, kk, 0))
    elif tb:
        b_spec = pl.BlockSpec((tn, tk), lambda i, j, kk: (j, kk))
    else:
        b_spec = pl.BlockSpec((tk, tn), lambda i, j, kk: (kk, j))
    if o_stacked:
        assert tn == n // N_DEV
        out_spec = pl.BlockSpec((1, tm, tn), lambda i, j, kk: (j, i, 0))
        out_shape = SDS((N_DEV, m, tn), out_dtype)
    else:
        out_spec = pl.BlockSpec((tm, tn), lambda i, j, kk: (i, j))
        out_shape = SDS((m, n), out_dtype)
    extra_specs, extra = ([pl.BlockSpec(memory_space=pl.ANY)], [after]) if after is not None else ([], [])
    return pl.pallas_call(
        body, grid=(m // tm, n // tn, nk), in_specs=[a_spec, b_spec] + extra_specs, out_specs=out_spec,
        out_shape=out_shape, name=name, scratch_shapes=[pltpu.VMEM((tm, tn), f32)] if nk > 1 else [],
        compiler_params=_params(("parallel", "parallel", "arbitrary")),
    )(a, b, *extra)


def _matmul_stack(a, b, *, ta=False, tb=False, out_dtype=f32, name):
    def lanes(ref):
        return jnp.concatenate([ref[j] for j in range(N_DEV)], axis=1).astype(bf16)

    if ta:
        w = b.shape[1] // N_DEV

        def body(a_ref, b_ref, o_ref):
            p = _tn(a_ref[...].astype(bf16), b_ref[...].astype(bf16))
            for j in range(N_DEV):
                o_ref[j] = p[:, j * w:(j + 1) * w].astype(o_ref.dtype)

        return pl.pallas_call(body, out_shape=SDS((N_DEV, a.shape[1], w), out_dtype), name=name,
                              compiler_params=_params())(a, b)

    m, half = a.shape[0], a.shape[0] // 2
    n = b.shape[1] if tb else N_DEV * b.shape[2]

    def body(a_ref, b_ref, o_ref):
        av = a_ref[...].astype(bf16)
        o_ref[...] = (_nt(av, lanes(b_ref)) if tb else jnp.dot(av, lanes(b_ref), preferred_element_type=f32)
                      ).astype(o_ref.dtype)

    return pl.pallas_call(
        body, grid=(2,), in_specs=[pl.BlockSpec((half, a.shape[1]), lambda i: (i, 0)),
                                   pl.BlockSpec(b.shape, lambda i: (0, 0, 0))],
        out_specs=pl.BlockSpec((half, n), lambda i: (i, 0)), out_shape=SDS((m, n), out_dtype), name=name,
        compiler_params=_params(("parallel",)),
    )(a, b)


def _matmul_rows(form, a, b, after, fn, tiled, vecs, outs, reds, *, name, tm=512):
    m = a.shape[0]
    assert m % tm == 0
    tiled = [t if isinstance(t, tuple) else (t, t.shape[1], 0) for t in tiled]
    nt, nv, no = len(tiled), len(vecs), len(outs)

    def body(a_ref, b_ref, after_ref, *refs):
        if form == "nt_stack":
            w = b.shape[2]
            acc = _nt(a_ref[:, 0:w], b_ref[0])
            for j in range(1, N_DEV):
                acc = acc + _nt(a_ref[:, j * w:(j + 1) * w], b_ref[j])
        elif form == "nt":
            acc = _nt(a_ref[...], b_ref[...])
        else:
            acc = jnp.dot(a_ref[...], b_ref[...], preferred_element_type=f32)
        orefs, rrefs = refs[nt + nv:nt + nv + no], refs[nt + nv + no:]
        touts, routs = fn([acc] + [r[...] for r in refs[:nt]], [r[...] for r in refs[nt:nt + nv]])
        for r, t in zip(orefs, touts, strict=True):
            r[...] = t.astype(r.dtype)

        @pl.when(pl.program_id(0) == 0)
        def _():
            for r in rrefs:
                r[...] = jnp.zeros_like(r)
        for r, t in zip(rrefs, routs, strict=True):
            r[...] += t

    def whole_map(nd):
        return lambda i: (0,) * nd

    def rows(width, cb=0):
        return pl.BlockSpec((tm, width), lambda i: (i, cb))

    return pl.pallas_call(
        body, grid=(m // tm,),
        in_specs=[rows(a.shape[1]), pl.BlockSpec(b.shape, whole_map(b.ndim), pipeline_mode=pl.Buffered(1)),
                  pl.BlockSpec(memory_space=pl.ANY)]
        + [rows(width, cb) for _, width, cb in tiled] + [pl.BlockSpec(v.shape, whole_map(v.ndim)) for v in vecs],
        out_specs=[rows(width) for width, _ in outs] + [pl.BlockSpec((1, width), lambda i: (0, 0)) for width in reds],
        out_shape=[SDS((m, width), dt) for width, dt in outs] + [SDS((1, width), f32) for width in reds], name=name,
        compiler_params=_params(("arbitrary",)),
    )(a, b, after, *[t[0] for t in tiled], *vecs)


def _rms(x):
    r = lax.rsqrt(jnp.mean(x * x, axis=-1, keepdims=True) + EPS)
    return r, x * r


def _rms_bwd(r, xn, dxn):
    return r * (dxn - xn * jnp.mean(dxn * xn, axis=-1, keepdims=True))


def _colsum(t):
    return jnp.sum(t, axis=0, keepdims=True)


def _sigmoid(x):
    return 1.0 / (1.0 + jnp.exp(-x))


def _modulated_norm(x, g, shift, scale):
    _, xn = _rms(x)
    return (xn * g) * (1.0 + scale) + shift


def _pre1(x, modv, g_mix):
    def fn(t, v):
        (xt,), (mv, g) = t, v
        return [_modulated_norm(xt, g, mv[0:1], mv[1:2])], []
    return _rowwise(fn, "pre1", [(x, D, 0)], [modv, g_mix], [(D, bf16)])[0]


def _post1(merged, w_o, x, modv, g_ffn):
    def fn(t, v):
        (mt, xt), (mv, g) = t, v
        x1 = xt + mv[2:3] * mt
        return [mt, x1, _modulated_norm(x1, g, mv[3:4], mv[4:5])], []
    return _matmul_rows("nn", merged, w_o, x, fn, [x], [modv, g_ffn], [(D, f32), (D, f32), (D, bf16)], [],
                        name="post1")


def _ffn_in(h, w_stack):
    def body(h_ref, w_ref, act_ref, au_ref):
        p = jnp.dot(h_ref[...], w_ref[0], preferred_element_type=f32)
        a, u = p[:, :FF_PAD], p[:, FF_PAD:]
        act_ref[...] = (a * _sigmoid(a) * u).astype(act_ref.dtype)
        au_ref[...] = p.astype(au_ref.dtype)

    return pl.pallas_call(
        body, grid=(N_DEV,),
        in_specs=[pl.BlockSpec((SEQ, D), lambda j: (0, 0)), pl.BlockSpec((1, D, 2 * FF_PAD), lambda j: (j, 0, 0))],
        out_specs=[pl.BlockSpec((SEQ, FF_PAD), lambda j: (0, j)), pl.BlockSpec((SEQ, 2 * FF_PAD), lambda j: (0, j))],
        out_shape=(SDS((SEQ, FF_HID), bf16), SDS((SEQ, 2 * FF_HID), bf16)), name="ffn_in",
        compiler_params=_params(("parallel",)),
    )(h, w_stack)


def _ffn_bwd_in(dff, w_down_stack, au):
    def body(d_ref, w_ref, au_ref, o_ref):
        dact = _nt(d_ref[...], w_ref[0])
        p = au_ref[...].astype(f32)
        a, u = p[:, :FF_PAD], p[:, FF_PAD:]
        sg = _sigmoid(a)
        o_ref[...] = jnp.concatenate([dact * u * (sg * (1.0 + a * (1.0 - sg))), dact * (a * sg)],
                                     axis=1).astype(o_ref.dtype)

    return pl.pallas_call(
        body, grid=(N_DEV,),
        in_specs=[pl.BlockSpec((SEQ, D), lambda j: (0, 0)), pl.BlockSpec((1, FF_PAD, D), lambda j: (j, 0, 0)),
                  pl.BlockSpec((SEQ, 2 * FF_PAD), lambda j: (0, j))],
        out_specs=pl.BlockSpec((SEQ, 2 * FF_PAD), lambda j: (0, j)),
        out_shape=SDS((SEQ, 2 * FF_HID), bf16), name="ffn_bwd_in", compiler_params=_params(("parallel",)),
    )(dff, w_down_stack, au)


def _final(act, w_down, x1, target, modv, g_final):
    def fn(t, v):
        (fft, x1t, tgt), (mv, g) = t, v
        x2 = x1t + mv[5:6] * fft
        r, xn = _rms(x2)
        err = xn * g - tgt
        dy = err * (1.0 / D)
        dx2 = _rms_bwd(r, xn, dy * g)
        return [dx2, dx2 * mv[5:6]], [_colsum(dy * xn), _colsum(dx2 * fft), _colsum(err * err) * (0.5 / D)]
    return _matmul_rows("nn", act, w_down, x1, fn, [x1, target], [modv, g_final], [(D, f32), (D, bf16)], [D, D, D],
                        name="final")


def _mid_bwd(dau, w_stack, after, x1, dx2, mix, modv, g_ffn):
    def fn(t, v):
        (dh, x1t, dx2t, mt), (mv, g) = t, v
        r, xn = _rms(x1t)
        dn = dh * (1.0 + mv[4:5])
        dx1 = dx2t + _rms_bwd(r, xn, dn * g)
        return [dx1, dx1 * mv[2:3]], [_colsum(dh), _colsum(dh * (xn * g)), _colsum(dn * xn), _colsum(dx1 * mt)]
    return _matmul_rows("nt_stack", dau, w_stack, after, fn, [x1, dx2, mix], [modv, g_ffn], [(D, f32), (D, bf16)],
                        [D, D, D, D], name="mid_bwd")


def _first_bwd(dproj, w_stack, after, x, dx1, modv, g_mix):
    def fn(t, v):
        (dh, xt, dx1t), (mv, g) = t, v
        r, xn = _rms(xt)
        dn = dh * (1.0 + mv[1:2])
        return [dx1t + _rms_bwd(r, xn, dn * g)], [_colsum(dh), _colsum(dh * (xn * g)), _colsum(dn * xn)]
    return _matmul_rows("nt_stack", dproj, w_stack, after, fn, [x, dx1], [modv, g_mix], [(D, f32)], [D, D, D],
                        name="first_bwd")


def _merge_fwd(ya, yb, proj):
    def fn(t, v):
        ya_t, yb_t, ga, gb = t
        return [_sigmoid(ga) * ya_t + _sigmoid(gb) * yb_t], []
    return _rowwise(fn, "merge_fwd", [(ya, D, 0), (yb, D, 0), (proj, D, C_GA // D), (proj, D, C_GB // D)], [],
                    [(D, bf16)])[0]


def _merge_bwd(dmix, w_o, after, ya, yb, proj):
    def fn(t, v):
        dm, ya_t, yb_t, ga, gb = t
        sa, sb = _sigmoid(ga), _sigmoid(gb)
        return [dm * sa, dm * sb, dm * ya_t * (sa * (1.0 - sa)), dm * yb_t * (sb * (1.0 - sb))], []
    return _matmul_rows("nt", dmix, w_o, after, fn, [ya, yb, (proj, D, C_GA // D), (proj, D, C_GB // D)], [],
                        [(D, bf16), (D, bf16), (D, bf16), (D, bf16)], [], name="merge_bwd")


def _rope_tables():
    half = ROT_DIM // 2
    pos = jnp.arange(SEQ, dtype=f32)
    inv_freq = ROPE_THETA ** (-jnp.arange(0, ROT_DIM, 2, dtype=f32) / ROT_DIM)
    ang = pos[:, None] * inv_freq[None, :]
    cos, sin = jnp.cos(ang), jnp.sin(ang)
    pad = jnp.zeros((SEQ, HEAD_DIM - ROT_DIM), f32)
    zero = jnp.zeros((SEQ, half), f32)
    c_head = jnp.concatenate([cos, cos, pad + 1.0], axis=1)
    lo_head = jnp.concatenate([-sin, zero, pad], axis=1)
    hi_head = jnp.concatenate([zero, sin, pad], axis=1)
    return tuple(jnp.concatenate([t, t], axis=1) for t in (c_head, lo_head, hi_head))


def _over_heads(tables):
    return [jnp.tile(t, (1, DIL_W // LANES)) for t in tables]


def _rope_fwd(proj, tables):
    half = ROT_DIM // 2

    def fn(t, v):
        q, k, vv = t[:3]
        c, lo, hi = _over_heads(t[3:])
        rot = lambda z: z * c + pltpu.roll(z, DIL_W - half, 1) * lo + pltpu.roll(z, half, 1) * hi
        return [rot(q) * ATT_SCALE, rot(k), vv], []
    return _rowwise(fn, "rope_fwd", [(proj, DIL_W, C_QB // DIL_W), (proj, DIL_W, C_KB // DIL_W),
                                     (proj, DIL_W, C_VB // DIL_W)] + [(tb, LANES, 0) for tb in tables], [],
                    [(DIL_W, f32)] * 3)


def _rope_bwd(dqs, dks, tables):
    half = ROT_DIM // 2

    def fn(t, v):
        dq_t, dk_t = jnp.concatenate(t[:N_GROUPS], axis=1), jnp.concatenate(t[N_GROUPS:2 * N_GROUPS], axis=1)
        c, lo, hi = _over_heads(t[2 * N_GROUPS:])
        rot_t = lambda z: z * c + pltpu.roll(z * lo, half, 1) + pltpu.roll(z * hi, DIL_W - half, 1)
        return [rot_t(dq_t), rot_t(dk_t)], []
    return _rowwise(fn, "rope_bwd", [(a, DIL_OUT_W, 0) for a in (*dqs, *dks)] + [(tb, LANES, 0) for tb in tables],
                    [], [(DIL_W, bf16), (DIL_W, bf16)])


def _head_bcast_sum(d):
    lane = lax.broadcasted_iota(jnp.int32, d.shape, 1)
    out = jnp.zeros_like(d)
    for h in range(d.shape[1] // HEAD_DIM):
        sel = (lane >= h * HEAD_DIM) & (lane < (h + 1) * HEAD_DIM)
        out = jnp.where(sel, jnp.sum(jnp.where(sel, d, 0.0), axis=1, keepdims=True), out)
    return out


def _dil_combine(outs, lses):
    def fn(t, v):
        o0, o1, o2, l0, l1, l2 = t
        m = jnp.maximum(jnp.maximum(l0, l1), l2)
        w0, w1, w2 = jnp.exp(l0 - m), jnp.exp(l1 - m), jnp.exp(l2 - m)
        tot = w0 + w1 + w2
        return [(w0 * o0 + w1 * o1 + w2 * o2) / tot, m + jnp.log(tot)], []
    w = DIL_OUT_W
    return _rowwise(fn, "dil_combine", [(t, w, 0) for t in (*outs, *lses)], [], [(w, f32), (w, f32)])


def _dil_delta(dyb_h, yb_h):
    def fn(t, v):
        return [_head_bcast_sum(t[0] * t[1])], []
    return _rowwise(fn, "dil_delta", [(dyb_h, DIL_OUT_W, 0), (yb_h, DIL_OUT_W, 0)], [], [(DIL_OUT_W, f32)])[0]


def _adamw_math(wt, gt, mt, vt):
    mn = ADAM_B1 * mt + (1.0 - ADAM_B1) * gt
    vn = ADAM_B2 * vt + (1.0 - ADAM_B2) * (gt * gt)
    m_hat = mn / (1.0 - ADAM_B1 ** ADAM_STEP)
    v_hat = vn / (1.0 - ADAM_B2 ** ADAM_STEP)
    return -ADAM_LR * (m_hat / (jnp.sqrt(v_hat) + ADAM_EPS) + ADAM_WD * wt), mn, vn


def _adamw(w, g, m, v, name):
    shape = w.shape
    if w.ndim == 1:
        w, g, m, v = (t.reshape(1, -1) for t in (w, g, m, v))
    rows, cols = w.shape
    if rows % 8 and rows > 8:
        return _adamw_by_cols(w, g, m, v, name)
    tile = 256 if rows % 256 == 0 and rows > 512 else rows

    def fn(t, _):
        return list(_adamw_math(*t)), []
    delta, mn, vn = _rowwise(fn, name, [(w, cols, 0), (g, cols, 0), (m, cols, 0), (v, cols, 0)], [],
                             [(cols, f32)] * 3, tile=tile)
    return delta.reshape(shape), mn.reshape(shape), vn.reshape(shape)


def _adamw_by_cols(w, g, m, v, name, tile=256):
    rows, cols = w.shape

    def body(w_ref, g_ref, m_ref, v_ref, d_ref, mn_ref, vn_ref):
        d_ref[...], mn_ref[...], vn_ref[...] = _adamw_math(w_ref[...], g_ref[...], m_ref[...], v_ref[...])

    spec = pl.BlockSpec((rows, tile), lambda j: (0, j))
    return pl.pallas_call(body, grid=(cols // tile,), in_specs=[spec] * 4, out_specs=[spec] * 3,
                          out_shape=[SDS((rows, cols), f32)] * 3, name=name,
                          compiler_params=_params(("parallel",)))(w, g, m, v)


def _ada_fwd(c_all, w_shard, b_shard):
    def body(c_ref, w_ref, b_ref, o_ref):
        cv = c_ref[...]
        sc = (cv * _sigmoid(cv)).astype(bf16)
        o_ref[...] = jnp.dot(sc, w_ref[...].astype(bf16), preferred_element_type=f32) + b_ref[...]
    return pl.pallas_call(body, out_shape=SDS((N_DEV, w_shard.shape[1]), f32), name="ada_fwd",
                          compiler_params=_params())(c_all, w_shard, b_shard)


def _ada_bwd(c_all, dmod_cols):
    def body(c_ref, d_ref, o_ref):
        cv = c_ref[...]
        sc = cv * _sigmoid(cv)
        o_ref[...] = lax.dot_general(sc, d_ref[...], (((0,), (0,)), ((), ())), precision=lax.Precision.HIGHEST,
                                     preferred_element_type=f32)
    return pl.pallas_call(body, out_shape=SDS((D, dmod_cols.shape[1]), f32), name="ada_bwd",
                          compiler_params=_params())(c_all, dmod_cols)


def _small_reduce(gathered, after):
    def body(g_ref, after_ref, o_ref, loss_ref):
        acc = g_ref[0]
        for d in range(1, N_DEV):
            acc = acc + g_ref[d]
        o_ref[...] = acc
        loss_ref[...] = jnp.zeros((1, LANES), f32) + jnp.sum(acc[10:11, :])
    return pl.pallas_call(body, out_shape=(SDS((SMALL_ROWS, D), f32), SDS((1, LANES), f32)), name="small_reduce",
                          in_specs=[pl.BlockSpec(memory_space=pltpu.VMEM), pl.BlockSpec(memory_space=pl.ANY)],
                          compiler_params=_params())(gathered, after)


FOX_BLK = 512
CUM_BLK = 128


def _fold_lanes(t, op):
    out = t[:, :LANES]
    for j in range(1, t.shape[1] // LANES):
        out = op(out, t[:, j * LANES:(j + 1) * LANES])
    return out


def _fox_gate_fwd(proj, b_pad):
    nblk = SEQ // CUM_BLK

    def body(f_ref, b_ref, col_ref):
        r = lax.broadcasted_iota(jnp.int32, (CUM_BLK, CUM_BLK), 0)
        c = lax.broadcasted_iota(jnp.int32, (CUM_BLK, CUM_BLK), 1)
        tri = (r >= c).astype(f32)
        carry = jnp.zeros((1, LANES), f32)
        for blk in range(nblk):
            z = f_ref[blk * CUM_BLK:(blk + 1) * CUM_BLK, :] + b_ref[...]
            logf = jnp.minimum(z, 0.0) - jnp.log1p(jnp.exp(-jnp.abs(z)))
            cs = jnp.dot(tri, logf, precision=lax.Precision.HIGHEST, preferred_element_type=f32) + carry
            col_ref[blk * CUM_BLK:(blk + 1) * CUM_BLK, :] = cs
            carry = cs[CUM_BLK - 1:CUM_BLK, :]

    return pl.pallas_call(
        body, grid=(1,), in_specs=[pl.BlockSpec((SEQ, LANES), lambda i: (0, C_F // LANES)),
                                   pl.BlockSpec((1, LANES), lambda i: (0, 0))],
        out_specs=pl.BlockSpec((SEQ, LANES), lambda i: (0, 0)),
        out_shape=SDS((SEQ, LANES), f32), name="fox_gate_fwd",
        compiler_params=_params(("arbitrary",)),
    )(proj, b_pad)


def _fox_gate_bwd(dF_row, proj, b_pad):
    nblk = SEQ // CUM_BLK

    def body(d_ref, f_ref, b_ref, df_ref, db_ref, col_ref):
        r = lax.broadcasted_iota(jnp.int32, (CUM_BLK, CUM_BLK), 0)
        c = lax.broadcasted_iota(jnp.int32, (CUM_BLK, CUM_BLK), 1)
        tri = (r <= c).astype(f32)
        lane = lax.broadcasted_iota(jnp.int32, (CUM_BLK, LANES), 1)
        col_ref[...] = d_ref[...].T
        carry = jnp.zeros((1, LANES), f32)
        total = jnp.zeros((1, LANES), f32)
        for blk in reversed(range(nblk)):
            rows = slice(blk * CUM_BLK, (blk + 1) * CUM_BLK)
            cs = jnp.dot(tri, col_ref[rows, :], precision=lax.Precision.HIGHEST, preferred_element_type=f32) + carry
            carry = cs[0:1, :]
            z = f_ref[rows, :] + b_ref[...]
            df = jnp.where(lane < N_FOX_HEADS, cs * _sigmoid(-z), 0.0)
            df_ref[rows, :] = df.astype(df_ref.dtype)
            total = total + _colsum(df)
        db_ref[...] = total

    return pl.pallas_call(
        body, grid=(1,), in_specs=[pl.BlockSpec((LANES, SEQ), lambda i: (0, 0)),
                                   pl.BlockSpec((SEQ, LANES), lambda i: (0, C_F // LANES)),
                                   pl.BlockSpec((1, LANES), lambda i: (0, 0))],
        out_specs=[pl.BlockSpec((SEQ, LANES), lambda i: (0, 0)), pl.BlockSpec((1, LANES), lambda i: (0, 0))],
        out_shape=(SDS((SEQ, LANES), bf16), SDS((1, LANES), f32)), name="fox_gate_bwd",
        scratch_shapes=[pltpu.VMEM((SEQ, LANES), f32)],
        compiler_params=_params(("arbitrary",)),
    )(dF_row, proj, b_pad)


def _nt(a, b):
    return lax.dot_general(a, b, (((1,), (1,)), ((), ())), preferred_element_type=f32)


def _tn(a, b):
    return lax.dot_general(a, b, (((0,), (0,)), ((), ())), preferred_element_type=f32)


def _fox_prep(proj, f_col):
    def fn(t, v):
        q, k, vv, fc = t
        lane = lax.broadcasted_iota(jnp.int32, (q.shape[0], LANES), 1)
        qs, ks = [], []
        for h in range(N_FOX_HEADS):
            pair, pos = divmod(h, 2)
            own = (lane >= pos * HEAD_DIM) & (lane < (pos + 1) * HEAD_DIM)
            base = (1 - pos) * HEAD_DIM
            f = fc[:, h:h + 1]
            hi = f.astype(bf16).astype(f32)
            mid = (f - hi).astype(bf16).astype(f32)
            lo = (f - hi) - mid
            one = jnp.ones_like(f)
            qa = jnp.where(own, q[:, pair * LANES:(pair + 1) * LANES] * ATT_SCALE, 0.0)
            ka = k[:, pair * LANES:(pair + 1) * LANES]
            for idx, (qv, kv) in enumerate([(hi, one), (mid, one), (lo, one), (one, -hi), (one, -mid), (one, -lo)]):
                sel = lane == base + idx
                qa = jnp.where(sel, qv, qa)
                ka = jnp.where(sel, kv, ka)
            qs.append(qa)
            ks.append(ka)
        return [jnp.concatenate(qs, axis=1), jnp.concatenate(ks, axis=1), vv], []
    w = N_FOX_HEADS * LANES
    return _rowwise(fn, "fox_prep", [(proj, FOX_W, C_QA // FOX_W), (proj, FOX_W, C_KA // FOX_W),
                                     (proj, FOX_W, C_VA // FOX_W), (f_col, LANES, 0)], [],
                    [(w, bf16), (w, bf16), (FOX_W, bf16)])


def _fox_fwd(q_aug, k_aug, v):
    blk = FOX_BLK
    npair = FOX_W // LANES

    def body(q_ref, k_ref, v_ref, o_ref, max_ref, sum_ref, s_scr):
        i = pl.program_id(1)
        tri = lax.broadcasted_iota(jnp.int32, (blk, blk), 0) >= lax.broadcasted_iota(jnp.int32, (blk, blk), 1)
        qh = [q_ref[:, h * LANES:(h + 1) * LANES] for h in range(2)]

        def logits(c, masked):
            off = pl.multiple_of(c * blk, blk)
            tops = []
            for h in range(2):
                s = _nt(qh[h], k_ref[pl.ds(off, blk), h * LANES:(h + 1) * LANES])
                if masked:
                    s = jnp.where(tri, s, NEG)
                s_scr[h, :, pl.ds(off, blk)] = s
                tops.append(_fold_lanes(s, jnp.maximum))
            return tops

        def pass_a(c, m):
            return tuple(jnp.maximum(a, b) for a, b in zip(m, logits(c, False)))

        m = lax.fori_loop(0, i, pass_a, tuple(jnp.full((blk, LANES), NEG, f32) for _ in range(2)))
        mx = [jnp.max(jnp.maximum(a, b), axis=1, keepdims=True) for a, b in zip(m, logits(i, True))]

        def pass_b(c, carry):
            off = pl.multiple_of(c * blk, blk)
            vv = v_ref[pl.ds(off, blk), :]
            new = []
            for h in range(2):
                l, acc = carry[h]
                p = jnp.exp(s_scr[h, :, pl.ds(off, blk)] - mx[h]).astype(bf16)
                new.append((l + _fold_lanes(p.astype(f32), jnp.add), acc + jnp.dot(p, vv, preferred_element_type=f32)))
            return tuple(new)

        zero = jnp.zeros((blk, LANES), f32)
        (l_a, acc_a), (l_b, acc_b) = lax.fori_loop(0, i + 1, pass_b, ((zero, zero), (zero, zero)))
        l_a = jnp.sum(l_a, axis=1, keepdims=True)
        l_b = jnp.sum(l_b, axis=1, keepdims=True)
        first = lax.broadcasted_iota(jnp.int32, (blk, LANES), 1) < HEAD_DIM
        o_ref[...] = jnp.where(first, acc_a / l_a, acc_b / l_b)
        max_ref[0] = jnp.where(first, mx[0], mx[1])
        sum_ref[0] = jnp.where(first, l_a, l_b)

    return pl.pallas_call(
        body, grid=(npair, SEQ // blk),
        in_specs=[pl.BlockSpec((blk, 2 * LANES), lambda p, i: (i, p)),
                  pl.BlockSpec((SEQ, 2 * LANES), lambda p, i: (0, p)),
                  pl.BlockSpec((SEQ, LANES), lambda p, i: (0, p))],
        out_specs=[pl.BlockSpec((blk, LANES), lambda p, i: (i, p))]
        + [pl.BlockSpec((1, blk, LANES), lambda p, i: (p, i, 0))] * 2,
        out_shape=(SDS((SEQ, FOX_W), f32),) + (SDS((npair, SEQ, LANES), f32),) * 2, name="fox_fwd",
        scratch_shapes=[pltpu.VMEM((2, blk, SEQ), f32)],
        compiler_params=_params(("parallel", "arbitrary")),
    )(q_aug, k_aug, v)


def _fox_bwd(q_aug, k_aug, v, do, o, row_max, row_sum, after):
    blk = FOX_BLK
    npair = FOX_W // LANES
    nblk = SEQ // blk

    def body(q_ref, k_ref, v_ref, do_ref, o_ref, max_ref, sum_ref, after_ref, dq_ref, dk_ref, dv_ref, df_ref, dq_acc,
             delta_ref, inv_ref):
        inv_ref[...] = 1.0 / sum_ref[0]
        lane_s = lax.broadcasted_iota(jnp.int32, (SEQ, LANES), 1)
        prod = do_ref[...].astype(bf16).astype(f32) * o_ref[...]
        d_a = jnp.sum(jnp.where(lane_s < HEAD_DIM, prod, 0.0), axis=1, keepdims=True)
        d_b = jnp.sum(jnp.where(lane_s >= HEAD_DIM, prod, 0.0), axis=1, keepdims=True)
        delta_ref[...] = jnp.where(lane_s < HEAD_DIM, d_a, d_b)
        dq_acc[...] = jnp.zeros_like(dq_acc)
        df_ref[...] = jnp.zeros_like(df_ref)
        lane = lax.broadcasted_iota(jnp.int32, (blk, LANES), 1)
        own = [lane < HEAD_DIM, lane >= HEAD_DIM]
        tri = lax.broadcasted_iota(jnp.int32, (blk, blk), 0) >= lax.broadcasted_iota(jnp.int32, (blk, blk), 1)

        def q_slab(qoff, h):
            return q_ref[pl.ds(qoff, blk), h * LANES:(h + 1) * LANES]

        def probs(qoff, h, k_h, masked):
            s = _nt(q_slab(qoff, h), k_h)
            if masked:
                s = jnp.where(tri, s, NEG)
            col = slice(h * HEAD_DIM, h * HEAD_DIM + 1)
            weights = jnp.exp(s - max_ref[0, pl.ds(qoff, blk), col]).astype(bf16).astype(f32)
            return weights * inv_ref[pl.ds(qoff, blk), col]

        def k_slabs(koff):
            return [k_ref[pl.ds(koff, blk), h * LANES:(h + 1) * LANES] for h in range(2)]

        def kv_step(kj, _):
            koff = pl.multiple_of(kj * blk, blk)
            k_aug = k_slabs(koff)
            k_own = [jnp.where(own[h], k_aug[h], jnp.zeros_like(k_aug[h])) for h in range(2)]
            vv = v_ref[pl.ds(koff, blk), :]
            v_own = [jnp.where(own[h], vv, jnp.zeros_like(vv)) for h in range(2)]

            def q_tile(qi, carry, masked):
                qoff = pl.multiple_of(qi * blk, blk)
                dd = do_ref[pl.ds(qoff, blk), :].astype(bf16)
                new, dq_add = [], None
                for h in range(2):
                    dk_h, dv_h, dcol = carry[h]
                    p = probs(qoff, h, k_aug[h], masked)
                    dl = p * (_nt(dd, v_own[h]) - delta_ref[pl.ds(qoff, blk), h * HEAD_DIM:h * HEAD_DIM + 1])
                    dlb = dl.astype(bf16)
                    part = jnp.dot(dlb, k_own[h], preferred_element_type=f32)
                    dq_add = part if dq_add is None else dq_add + part
                    new.append((dk_h + _tn(dlb, q_slab(qoff, h)), dv_h + _tn(p.astype(bf16), dd),
                                dcol + _colsum(dl)))
                dq_acc[pl.ds(qoff, blk), :] += dq_add * ATT_SCALE
                return tuple(new)

            zero = (jnp.zeros((blk, LANES), f32), jnp.zeros((blk, LANES), f32), jnp.zeros((1, blk), f32))
            carry = q_tile(kj, (zero, zero), True)
            (dk_a, dv_a, dcol_a), (dk_b, dv_b, dcol_b) = lax.fori_loop(
                kj + 1, nblk, lambda qi, cr: q_tile(qi, cr, False), carry)
            dk_ref[pl.ds(koff, blk), :] = jnp.where(own[0], dk_a, dk_b).astype(dk_ref.dtype)
            dv_ref[pl.ds(koff, blk), :] = jnp.where(own[0], dv_a, dv_b).astype(dv_ref.dtype)
            df_ref[0, 0:1, pl.ds(koff, blk)] = -dcol_a
            df_ref[0, 1:2, pl.ds(koff, blk)] = -dcol_b
            return 0

        lax.fori_loop(0, nblk, kv_step, 0)
        dq_ref[...] = dq_acc[...].astype(dq_ref.dtype)

    pair_aug = pl.BlockSpec((SEQ, 2 * LANES), lambda p: (0, p))
    slab = pl.BlockSpec((SEQ, LANES), lambda p: (0, p))
    per_pair = pl.BlockSpec((1, SEQ, LANES), lambda p: (p, 0, 0))
    rows = pl.BlockSpec((1, 8, SEQ), lambda p: (p, 0, 0))
    return pl.pallas_call(
        body, grid=(npair,),
        in_specs=[pair_aug, pair_aug, slab, slab, slab, per_pair, per_pair, pl.BlockSpec(memory_space=pl.ANY)],
        out_specs=[slab, slab, slab, rows],
        out_shape=(SDS((SEQ, FOX_W), bf16),) * 3 + (SDS((npair, 8, SEQ), f32),), name="fox_bwd",
        scratch_shapes=[pltpu.VMEM((SEQ, LANES), f32)] * 3,
        compiler_params=_params(("parallel",)),
    )(q_aug, k_aug, v, do, o, row_max, row_sum, after)


DIL_BLK = 128
DILATIONS = (1, 4, 16)
N_GROUPS = len(DILATIONS)
DIL_PAIRS = DIL_OUT_W // LANES


def _dil_blocks(d):
    r1 = lax.broadcasted_iota(jnp.int32, (2 * DIL_BLK, DIL_BLK), 0) & (DIL_BLK - 1)
    c1 = lax.broadcasted_iota(jnp.int32, (2 * DIL_BLK, DIL_BLK), 1)
    r2 = lax.broadcasted_iota(jnp.int32, (2 * DIL_BLK, 2 * DIL_BLK), 0) & (DIL_BLK - 1)
    c2 = lax.broadcasted_iota(jnp.int32, (2 * DIL_BLK, 2 * DIL_BLK), 1)
    band = ((c2 < DIL_BLK) & (c2 >= r2)) | ((c2 >= DIL_BLK) & (c2 - DIL_BLK <= r2))
    out = []
    for r in range(d):
        for b in range(SEQ // d // DIL_BLK):
            rows = pl.ds(r + d * DIL_BLK * b, DIL_BLK, stride=d)
            if b == 0:
                out.append((rows, rows, r1 >= c1))
            else:
                out.append((rows, pl.ds(r + d * DIL_BLK * (b - 1), 2 * DIL_BLK, stride=d), band))
    return out


def _stack_heads(t, first):
    zero = jnp.zeros_like(t)
    return jnp.concatenate([jnp.where(first, t, zero), jnp.where(first, zero, t)], axis=0)


def _dil_fwd(q, k, v, g):
    def body(q_ref, k_ref, v_ref, o_ref, lse_ref):
        first = lax.broadcasted_iota(jnp.int32, (DIL_BLK, LANES), 1) < HEAD_DIM
        for rows, krows, mask in _dil_blocks(DILATIONS[g]):
            qv, kk, vv = q_ref[rows, :].astype(bf16), k_ref[krows, :].astype(bf16), v_ref[krows, :].astype(bf16)
            s = jnp.where(mask, _nt(_stack_heads(qv, first), kk), NEG)
            m = jnp.max(s, axis=1, keepdims=True)
            p = jnp.exp(s - m)
            l = jnp.sum(p, axis=1, keepdims=True)
            out = jnp.dot(p.astype(bf16), vv, preferred_element_type=f32) / l
            lse = m + jnp.log(l)
            o_ref[rows, :] = jnp.where(first, out[:DIL_BLK], out[DIL_BLK:])
            lse_ref[rows, :] = jnp.where(first, lse[:DIL_BLK], lse[DIL_BLK:])

    grouped = pl.BlockSpec((SEQ, LANES), lambda p: (0, DIL_PAIRS * g + p))
    own = pl.BlockSpec((SEQ, LANES), lambda p: (0, p))
    shape = SDS((SEQ, DIL_OUT_W), f32)
    return pl.pallas_call(
        body, grid=(DIL_PAIRS,), in_specs=[grouped] * 3, out_specs=[own] * 2, out_shape=(shape, shape),
        name=f"dil_fwd_{DILATIONS[g]}", compiler_params=_params(("parallel",)),
    )(q, k, v)


def _dil_bwd(q, k, v, do, lse, delta, g):
    def body(q_ref, k_ref, v_ref, do_ref, lse_ref, dl_ref, dq_ref, dk_ref, dv_ref):
        first = lax.broadcasted_iota(jnp.int32, (DIL_BLK, LANES), 1) < HEAD_DIM
        dk_ref[...] = jnp.zeros_like(dk_ref)
        dv_ref[...] = jnp.zeros_like(dv_ref)
        for rows, krows, mask in _dil_blocks(DILATIONS[g]):
            qv, kk, vv = q_ref[rows, :].astype(bf16), k_ref[krows, :].astype(bf16), v_ref[krows, :].astype(bf16)
            lsev, delv = lse_ref[rows, :], dl_ref[rows, :]
            q2 = _stack_heads(qv, first)
            do2 = _stack_heads(do_ref[rows, :].astype(bf16), first)
            per_head = lambda t: jnp.concatenate([t[:, 0:1], t[:, HEAD_DIM:HEAD_DIM + 1]], axis=0)
            p = jnp.exp(jnp.where(mask, _nt(q2, kk), NEG) - per_head(lsev))
            dl = (p * (_nt(do2, vv) - per_head(delv))).astype(bf16)
            dq = jnp.dot(dl, kk, preferred_element_type=f32)
            dq_ref[rows, :] = jnp.where(first, dq[:DIL_BLK], dq[DIL_BLK:]) * ATT_SCALE
            dk_ref[krows, :] += _tn(dl, q2)
            dv_ref[krows, :] += _tn(p.astype(bf16), do2)

    grouped = pl.BlockSpec((SEQ, LANES), lambda p: (0, DIL_PAIRS * g + p))
    own = pl.BlockSpec((SEQ, LANES), lambda p: (0, p))
    shape = SDS((SEQ, DIL_OUT_W), f32)
    return pl.pallas_call(
        body, grid=(DIL_PAIRS,), in_specs=[grouped] * 3 + [own] * 3, out_specs=[own] * 3,
        out_shape=(shape, shape, shape), name=f"dil_bwd_{DILATIONS[g]}", compiler_params=_params(("parallel",)),
    )(q, k, v, do, lse, delta)


def _position():
    return lax.axis_index("x"), lax.axis_index("y"), lax.axis_index("c")


def _all_gather(block, name):
    def body(x_ref, out_ref, send_sems, recv_sems, local_sem):
        x, y, c = _position()
        me, sibling = (x, y, c), (x, y, 1 - c)
        chips = [(1 - x, y), (x, 1 - y), (1 - x, 1 - y)]

        def slot(px, py, pc):
            return out_ref.at[4 * px + 2 * py + pc]

        def copy(k, blk, to, src=None):
            return pltpu.make_async_remote_copy(
                src_ref=slot(*blk) if src is None else src, dst_ref=slot(*blk),
                send_sem=send_sems.at[k], recv_sem=recv_sems.at[k], device_id=to, device_id_type=MESH)

        mine = pltpu.make_async_copy(x_ref, slot(*me), local_sem)
        mine.start()
        first = [copy(0, me, sibling, src=x_ref)]
        first += [copy(1 + j, me, (*chip, c), src=x_ref) for j, chip in enumerate(chips)]
        for cp in first:
            cp.start()
        passed = [copy(4 + j, (*chip, c), sibling) for j, chip in enumerate(chips)]
        for j, chip in enumerate(chips):
            copy(1 + j, (*chip, c), me).wait_recv()
            passed[j].start()
        copy(0, sibling, me).wait_recv()
        for j, chip in enumerate(chips):
            copy(4 + j, (*chip, 1 - c), me).wait_recv()
        for cp in first + passed:
            cp.wait_send()
        mine.wait()

    return pl.pallas_call(
        body, out_shape=SDS((N_DEV,) + block.shape, block.dtype),
        in_specs=[pl.BlockSpec(memory_space=pl.ANY)], out_specs=pl.BlockSpec(memory_space=pl.ANY),
        scratch_shapes=[pltpu.SemaphoreType.DMA((7,)), pltpu.SemaphoreType.DMA((7,)), pltpu.SemaphoreType.DMA],
        name=name,
    )(block)


HBM_SPEC = pl.BlockSpec(memory_space=pltpu.HBM)
SEM_SPEC = pl.BlockSpec(memory_space=pltpu.SEMAPHORE)
SPLIT_COPY = pltpu.CompilerParams(has_side_effects=pltpu.SideEffectType.DATAFLOW_SIDE_EFFECTING)


def _in_hbm(t):
    return pltpu.with_memory_space_constraint(t, pltpu.HBM)


def _pair_copies(g_refs, land_refs, send_sems, recv_sems):
    x, y, c = _position()
    return [pltpu.make_async_remote_copy(
        src_ref=g.at[2 * k + (1 - c)], dst_ref=land.at[k], send_sem=send_sems.at[4 * a + k],
        recv_sem=recv_sems.at[4 * a + k], device_id=(x, y, 1 - c), device_id_type=MESH)
        for a, (g, land) in enumerate(zip(g_refs, land_refs, strict=True)) for k in range(4)]


def _chip_copies(t_refs, land_refs, send_sems, recv_sems):
    x, y, c = _position()
    chips = [(1 - x, y), (x, 1 - y), (1 - x, 1 - y)]
    return [pltpu.make_async_remote_copy(
        src_ref=t.at[2 * px + py], dst_ref=land.at[j], send_sem=send_sems.at[3 * a + j],
        recv_sem=recv_sems.at[3 * a + j], device_id=(px, py, c), device_id_type=MESH)
        for a, (t, land) in enumerate(zip(t_refs, land_refs, strict=True)) for j, (px, py) in enumerate(chips)]


_ROUNDS = {"pair": (_pair_copies, 4), "chip": (_chip_copies, 3)}


def _exchange_start(kind, ts, name):
    copies, slots = _ROUNDS[kind]
    n = len(ts)
    lands = [_in_hbm(lax.empty((slots,) + t.shape[1:], t.dtype)) for t in ts]

    def body(*refs):
        for cp in copies(refs[:n], refs[n:2 * n], refs[2 * n], refs[2 * n + 1]):
            cp.start()
        refs[-1][...] = jnp.zeros_like(refs[-1])

    sems = pltpu.SemaphoreType.DMA((slots * n,))
    res = pl.pallas_call(
        body, name=name, in_specs=[HBM_SPEC] * (2 * n),
        out_shape=(sems, sems, *[pltpu.HBM(t.shape, t.dtype) for t in (*ts, *lands)], SDS((8, LANES), f32)),
        out_specs=(SEM_SPEC, SEM_SPEC, *[HBM_SPEC] * (2 * n), pl.BlockSpec(memory_space=pltpu.VMEM)),
        input_output_aliases={i: 2 + i for i in range(2 * n)}, compiler_params=SPLIT_COPY,
    )(*[_in_hbm(t) for t in ts], *lands)
    return res[:-1], res[-1]


def _exchange_wait(kind, state, after, name):
    copies, _ = _ROUNDS[kind]
    send_sems, recv_sems, *arrays = state
    n = len(arrays) // 2

    def body(*refs):
        for cp in copies(refs[:n], refs[n:2 * n], refs[2 * n], refs[2 * n + 1]):
            cp.wait_send()
            cp.wait_recv()

    res = pl.pallas_call(
        body, name=name, in_specs=[HBM_SPEC] * (2 * n) + [SEM_SPEC, SEM_SPEC, pl.BlockSpec(memory_space=pl.ANY)],
        out_shape=[pltpu.HBM(t.shape, t.dtype) for t in arrays], out_specs=[HBM_SPEC] * (2 * n),
        input_output_aliases={i: i for i in range(2 * n)}, compiler_params=SPLIT_COPY,
    )(*arrays, send_sems, recv_sems, after)
    return res[:n], res[n:]


def _gather_copies(x_refs, out_refs, send_sems, recv_sems):
    x, y, c = _position()
    peers = [(x, y, 1 - c), (1 - x, y, c), (x, 1 - y, c), (1 - x, 1 - y, c)]
    sends, arrivals = [], []
    for a, (x_ref, out_ref) in enumerate(zip(x_refs, out_refs, strict=True)):
        for k, (px, py, pc) in enumerate(peers):
            sems = dict(send_sem=send_sems.at[4 * a + k], recv_sem=recv_sems.at[4 * a + k],
                        device_id=(px, py, pc), device_id_type=MESH)
            sends.append(pltpu.make_async_remote_copy(src_ref=x_ref, dst_ref=out_ref.at[4 * x + 2 * y + c], **sems))
            arrivals.append(pltpu.make_async_remote_copy(src_ref=x_ref, dst_ref=out_ref.at[4 * px + 2 * py + pc],
                                                         **sems))
    return sends, arrivals


def _gather_start(blocks, after, name):
    n = len(blocks)
    outs = [_in_hbm(lax.empty((N_DEV,) + b.shape, b.dtype)) for b in blocks]

    def body(*refs):
        sends, _ = _gather_copies(refs[:n], refs[n:2 * n], refs[2 * n + 1], refs[2 * n + 2])
        for cp in sends:
            cp.start()
        refs[-1][...] = jnp.zeros_like(refs[-1])

    sems = pltpu.SemaphoreType.DMA((4 * n,))
    res = pl.pallas_call(
        body, name=name, in_specs=[HBM_SPEC] * (2 * n) + [pl.BlockSpec(memory_space=pl.ANY)],
        out_shape=(sems, sems, *[pltpu.HBM(t.shape, t.dtype) for t in (*blocks, *outs)], SDS((8, LANES), f32)),
        out_specs=(SEM_SPEC, SEM_SPEC, *[HBM_SPEC] * (2 * n), pl.BlockSpec(memory_space=pltpu.VMEM)),
        input_output_aliases={i: 2 + i for i in range(2 * n)}, compiler_params=SPLIT_COPY,
    )(*[_in_hbm(b) for b in blocks], *outs, after)
    return res[:-1], res[-1]


def _gather_wait(state, after, name):
    send_sems, recv_sems, *arrays = state
    n = len(arrays) // 2

    def body(*refs):
        sends, arrivals = _gather_copies(refs[:n], refs[n:2 * n], refs[2 * n], refs[2 * n + 1])
        for cp in sends:
            cp.wait_send()
        for cp in arrivals:
            cp.wait_recv()

    res = pl.pallas_call(
        body, name=name, in_specs=[HBM_SPEC] * (2 * n) + [SEM_SPEC, SEM_SPEC, pl.BlockSpec(memory_space=pl.ANY)],
        out_shape=[pltpu.HBM(t.shape, t.dtype) for t in arrays], out_specs=[HBM_SPEC] * (2 * n),
        input_output_aliases={i: i for i in range(2 * n)}, compiler_params=SPLIT_COPY,
    )(*arrays, send_sems, recv_sems, after)
    return res[:n], res[n:]


def _gather_finish(partial, name):
    n = len(partial)

    def body(*refs):
        in_refs, out_refs = refs[:n], refs[n:2 * n]
        send_sems, recv_sems = refs[2 * n:]
        x, y, c = _position()
        chips = [(1 - x, y), (x, 1 - y), (1 - x, 1 - y)]
        copies = []
        for a in range(n):
            for j, (px, py) in enumerate(chips):
                cp = pltpu.make_async_remote_copy(
                    src_ref=in_refs[a].at[4 * px + 2 * py + c], dst_ref=out_refs[a].at[4 * px + 2 * py + c],
                    send_sem=send_sems.at[a, j], recv_sem=recv_sems.at[a, j], device_id=(x, y, 1 - c),
                    device_id_type=MESH)
                cp.start()
                copies.append(cp)
        for a in range(n):
            for j, (px, py) in enumerate(chips):
                pltpu.make_async_remote_copy(
                    src_ref=in_refs[a].at[4 * px + 2 * py + (1 - c)], dst_ref=out_refs[a].at[4 * px + 2 * py + (1 - c)],
                    send_sem=send_sems.at[a, j], recv_sem=recv_sems.at[a, j], device_id=(x, y, 1 - c),
                    device_id_type=MESH).wait_recv()
        for cp in copies:
            cp.wait_send()

    hbm = pl.BlockSpec(memory_space=pl.ANY)
    return pl.pallas_call(
        body, out_shape=[SDS(p.shape, p.dtype) for p in partial], in_specs=[hbm] * n, out_specs=[hbm] * n,
        input_output_aliases={a: a for a in range(n)},
        scratch_shapes=[pltpu.SemaphoreType.DMA((n, 3)), pltpu.SemaphoreType.DMA((n, 3))],
        name=name,
    )(*partial)


def _row_tile(rows):
    return 512 if rows % 512 == 0 and rows > 512 else rows


def _pair_add(g, r1, core, name):
    def body(c_ref, g_ref, r_ref, o_ref):
        o_ref[...] = (g_ref[...].astype(f32) + r_ref[...].astype(f32)).astype(o_ref.dtype)

    rows, cols = g.shape[1:]
    tile = _row_tile(rows)
    blk = (1, tile, cols)
    return pl.pallas_call(
        body, out_shape=SDS((4, rows, cols), g.dtype), name=name,
        grid_spec=pltpu.PrefetchScalarGridSpec(
            num_scalar_prefetch=1, grid=(4, rows // tile),
            in_specs=[pl.BlockSpec(blk, lambda k, i, c_ref: (2 * k + c_ref[0], i, 0)),
                      pl.BlockSpec(blk, lambda k, i, c_ref: (k, i, 0))],
            out_specs=pl.BlockSpec(blk, lambda k, i, c_ref: (k, i, 0))),
        compiler_params=_params(("parallel", "arbitrary")),
    )(core, g, r1)


def _chip_add(t, r2, chip, name, transposed=False):
    def body(c_ref, t_ref, r_ref, o_ref):
        s = ((t_ref[0].astype(f32) + r_ref[0].astype(f32)) + r_ref[1].astype(f32)) + r_ref[2].astype(f32)
        o_ref[...] = s.T if transposed else s

    rows, cols = t.shape[1:]
    tile = _row_tile(rows)
    out_spec = pl.BlockSpec((cols, tile), lambda i, c_ref: (0, i)) if transposed else pl.BlockSpec(
        (tile, cols), lambda i, c_ref: (i, 0))
    return pl.pallas_call(
        body, out_shape=SDS((cols, rows) if transposed else (rows, cols), f32), name=name,
        grid_spec=pltpu.PrefetchScalarGridSpec(
            num_scalar_prefetch=1, grid=(rows // tile,),
            in_specs=[pl.BlockSpec((1, tile, cols), lambda i, c_ref: (c_ref[0], i, 0)),
                      pl.BlockSpec((3, tile, cols), lambda i, c_ref: (0, i, 0))],
            out_specs=out_spec),
        compiler_params=_params(("arbitrary",)),
    )(chip, t, r2)


def _pad_to(t, axis, size):
    pads = [(0, 0)] * t.ndim
    pads[axis] = (0, size - t.shape[axis])
    return jnp.pad(t, pads)


_REF_COLS = {"qa": (0, FOX_W), "ka": (FOX_W, FOX_W), "va": (2 * FOX_W, FOX_W), "f": (3 * FOX_W, N_FOX_HEADS)}
_REF_COLS.update({n: (3 * FOX_W + N_FOX_HEADS + i * DIL_W, DIL_W) for i, n in enumerate(("qb", "kb", "vb"))})
_REF_COLS.update({n: (3 * FOX_W + N_FOX_HEADS + 3 * DIL_W + i * D, D) for i, n in enumerate(("ga", "gb"))})
_REF_ORDER = ("qa", "ka", "va", "f", "qb", "kb", "vb", "ga", "gb")


def _place_cols(sources, src_of, out_cols, name, row_block=512):
    arrays = [s[0] if isinstance(s, tuple) else s for s in sources]
    widths = [a.shape[-1] for a in arrays]
    rows = arrays[0].shape[-2]
    plan = []
    for t in range(out_cols // LANES):
        segs, c, end = [], t * LANES, (t + 1) * LANES
        while c < end:
            s = src_of(c)
            if s is None:
                c += 1
                continue
            n = 1
            while c + n < end and src_of(c + n) == (s[0], s[1] + n):
                n += 1
            segs.append((s[0], s[1], c - t * LANES, n))
            c += n
        plan.append(segs)

    def body(*refs):
        o_ref = refs[-1]
        for t, segs in enumerate(plan):
            acc = None
            for si, c0, o0, n in segs:
                a0 = c0 // LANES * LANES
                wide = min(2 * LANES, widths[si] - a0)
                win = refs[si][0, :, a0:a0 + wide] if isinstance(sources[si], tuple) else refs[si][:, a0:a0 + wide]
                r = lax.broadcasted_iota(jnp.int32, (wide, LANES), 0)
                c = lax.broadcasted_iota(jnp.int32, (wide, LANES), 1)
                pick = ((r - (c0 - a0) == c - o0) & (c >= o0) & (c < o0 + n)).astype(bf16)
                part = jnp.dot(win.astype(bf16), pick, preferred_element_type=f32)
                acc = part if acc is None else acc + part
            tile = jnp.zeros((row_block, LANES), f32) if acc is None else acc
            o_ref[:, t * LANES:(t + 1) * LANES] = tile.astype(o_ref.dtype)

    def spec(s):
        if isinstance(s, tuple):
            j = s[1]
            return pl.BlockSpec((1, row_block, s[0].shape[-1]), lambda i: (j, i, 0))
        return pl.BlockSpec((row_block, s.shape[-1]), lambda i: (i, 0))

    return pl.pallas_call(
        body, grid=(rows // row_block,), in_specs=[spec(s) for s in sources],
        out_specs=pl.BlockSpec((row_block, out_cols), lambda i: (i, 0)), out_shape=SDS((rows, out_cols), bf16),
        name=name, compiler_params=_params(("parallel",)),
    )(*arrays)


def _ref_piece(r):
    for name in _REF_ORDER:
        lo, width = _REF_COLS[name]
        if lo <= r < lo + width:
            return name, r - lo
    raise ValueError(r)


def _shard_pad_cols(pieces):
    names = [n for n in _REF_ORDER if n != "vb"]
    sources = [pieces[n] for n in names] + list(pieces["vb"])

    def src_of(c):
        j, i = divmod(c, W_IN_PAD)
        if i >= W_IN_SH:
            return None
        name, col = _ref_piece(j * W_IN_SH + i)
        if name == "vb":
            return len(names) + col // DIL_OUT_W, col % DIL_OUT_W
        return names.index(name), col

    return _place_cols(sources, src_of, N_DEV * W_IN_PAD, "place_dproj")


_SLABS = {"ga": C_GA, "gb": C_GB, "qb": C_QB, "kb": C_KB, "vb": C_VB, "qa": C_QA, "ka": C_KA, "va": C_VA, "f": C_F}


def _slab_w_in(stack):
    def src_of(c):
        for name, start in _SLABS.items():
            lo, width = _REF_COLS[name]
            if start <= c < start + width:
                return divmod(lo + c - start, W_IN_SH)
        return None

    return _place_cols([(stack, j) for j in range(N_DEV)], src_of, PROJ_W, "place_w_in")


def kernel(x, c, w_ada, b_ada, g_mix, w_in, b_fgate, w_br_a, w_br_b, w_out, g_ffn, w_ffn_gate, w_ffn_up, w_ffn_down, g_final, loss_target, m_w_ada, m_b_ada, m_g_mix, m_w_in, m_b_fgate, m_w_br_a, m_w_br_b, m_w_out, m_g_ffn, m_w_ffn_gate, m_w_ffn_up, m_w_ffn_down, m_g_final, v_w_ada, v_b_ada, v_g_mix, v_w_in, v_b_fgate, v_w_br_a, v_w_br_b, v_w_out, v_g_ffn, v_w_ffn_gate, v_w_ffn_up, v_w_ffn_down, v_g_final):
    px, py, pc = _position()
    dev = 4 * px + 2 * py + pc
    x2d, tgt = x[0], loss_target[0]

    c_all = _all_gather(c, "gather_c").reshape(N_DEV, D)
    ada_cols = w_ada.shape[2]
    b_shard = lax.dynamic_slice(b_ada, (0, dev * ada_cols), (1, ada_cols))
    mod_shard = _ada_fwd(c_all, w_ada[0], b_shard)
    mod_all = _all_gather(mod_shard, "gather_mod")
    modv = lax.dynamic_index_in_dim(mod_all, dev, axis=1, keepdims=False).reshape(6, D)
    h1 = _pre1(x2d, modv, g_mix)

    w_in_s = _all_gather(_pad_to(w_in[0], 1, W_IN_PAD).astype(bf16), "gather_w_in")
    gate_up = jnp.concatenate([_pad_to(w_ffn_gate[0], 1, FF_PAD), _pad_to(w_ffn_up[0], 1, FF_PAD)], axis=1)
    later = [w_br_a[0], w_br_b[0], w_out[0], gate_up, _pad_to(w_ffn_down[0], 0, FF_PAD)]
    later_state, later_token = _gather_start([t.astype(bf16) for t in later], w_in_s, "gather_rest_start")
    w_in_p = _slab_w_in(w_in_s)

    proj = _matmul(h1, w_in_p, name="mm_proj", tm=SEQ, tn=896, tk=D, after=later_token)
    b_pad = jnp.pad(b_fgate, ((0, 0), (0, LANES - N_FOX_HEADS)))
    q_aug, k_aug, va = _fox_prep(proj, _fox_gate_fwd(proj, b_pad))
    ya_h, max_a, sum_a = _fox_fwd(q_aug, k_aug, va)

    tables = _rope_tables()
    qb_r, kb_r, vb = _rope_fwd(proj, tables)
    by_group = [_dil_fwd(qb_r, kb_r, vb, grp) for grp in range(N_GROUPS)]
    yb_h, lse_b = _dil_combine([o for o, _ in by_group], [l for _, l in by_group])

    mine, arrived = _gather_wait(later_state, yb_h, "gather_rest_wait")
    w_a_s, w_b_s, w_o_s, w_gu_s, w_d_s = [
        lax.dynamic_update_slice(stack, block[None], (dev, 0, 0))
        for stack, block in zip(_gather_finish(arrived, "gather_rest_finish"), mine, strict=True)]
    w_o = w_o_s.reshape(D, D)
    w_d = w_d_s.reshape(FF_HID, D)
    ya = _matmul_stack(ya_h, w_a_s, name="mm_br_a")
    yb = _matmul_stack(yb_h, w_b_s, name="mm_br_b")

    merged = _merge_fwd(ya, yb, proj)
    mix, x1, h2 = _post1(merged, w_o, x2d, modv, g_ffn)
    act, au = _ffn_in(h2, w_gu_s)

    dx2, dff, dg_final, dga_f, loss_lanes = _final(act, w_d, x1, tgt, modv, g_final.reshape(1, D))
    dau = _ffn_bwd_in(dff, w_d_s, au)

    core = pc.astype(jnp.int32).reshape(1)
    chip = (2 * px + py).astype(jnp.int32).reshape(1)

    def pair_done(state, after, tags, name):
        mine, theirs = _exchange_wait("pair", state, after, "pair_wait_" + name)
        sums = [_pair_add(g, r, core, "pair_add_" + t) for g, r, t in zip(mine, theirs, tags)]
        return _exchange_start("chip", sums, "chip_start_" + name)

    def from_chips(state, after, tags, name, transposed=None):
        sums, got = _exchange_wait("chip", state, after, "chip_wait_" + name)
        flips = transposed or [False] * len(tags)
        return [_chip_add(p, r, chip, "chip_add_" + t, f) for p, r, t, f in zip(sums, got, tags, flips)]

    g_gu = _matmul(h2, dau, ta=True, by_shard=True, out_dtype=bf16, name="mm_g_ffn_in", tm=D, tn=2 * FF_PAD, tk=SEQ)
    g_d = _matmul(act, dff, ta=True, out_dtype=bf16, name="mm_g_down", tm=FF_HID // 2, tn=512, tk=SEQ)
    ffn_tags = ["gu", "down"]
    ffn_pair, ffn_pair_token = _exchange_start("pair", [g_gu, g_d.reshape(N_DEV, FF_PAD, D)], "pair_start_ffn")

    dx1, dmix, dsh_f, dsc_f, dg_ffn, dga_m = _mid_bwd(dau, w_gu_s, ffn_pair_token, x1, dx2, mix, modv, g_ffn)
    ffn_state, ffn_token = pair_done(ffn_pair, dx1, ffn_tags, "ffn")
    dya, dyb, dga, dgb = _merge_bwd(dmix, w_o, ffn_token, ya, yb, proj)
    dya_h = _matmul_stack(dya, w_a_s, tb=True, name="mm_d_ya")
    dyb_h = _matmul_stack(dyb, w_b_s, tb=True, name="mm_d_yb")

    g_o = _matmul(merged, dmix, ta=True, out_dtype=bf16, name="mm_g_out", tm=D, tn=512, tk=SEQ)
    g_a = _matmul_stack(ya_h, dya, ta=True, out_dtype=bf16, name="mm_g_br_a")
    g_b = _matmul_stack(yb_h, dyb, ta=True, out_dtype=bf16, name="mm_g_br_b")
    rows_a, rows_b = FOX_W * W_BR_SH // D, DIL_OUT_W * W_BR_SH // D
    g_small = jnp.concatenate([g_a.reshape(N_DEV, rows_a, D), g_b.reshape(N_DEV, rows_b, D),
                               g_o.reshape(N_DEV, W_BR_SH, D)], axis=1)
    small_pair, small_pair_token = _exchange_start("pair", [g_small], "pair_start_small")

    dqa, dka, dva, dF = _fox_bwd(q_aug, k_aug, va, dya_h, ya_h, max_a, sum_a, small_pair_token)
    dF_row = jnp.pad(dF[:, :2, :].reshape(N_FOX_HEADS, SEQ), ((0, LANES - N_FOX_HEADS), (0, 0)))
    df, db_fgate = _fox_gate_bwd(dF_row, proj, b_pad)
    small_state, small_token = pair_done(small_pair, df, ["small"], "small")

    delta_b = _dil_delta(dyb_h, yb_h)
    dil_grads = [_dil_bwd(qb_r, kb_r, vb, dyb_h, lse_b, delta_b, grp) for grp in range(N_GROUPS)]
    dqb, dkb = _rope_bwd([t[0] for t in dil_grads], [t[1] for t in dil_grads], tables)

    dproj = _shard_pad_cols({"qa": dqa, "ka": dka, "va": dva, "f": df, "qb": dqb, "kb": dkb,
                             "vb": [t[2] for t in dil_grads], "ga": dga, "gb": dgb})
    g_in = _matmul(h1, dproj, ta=True, by_shard=True, out_dtype=bf16, name="mm_g_in", tm=D, tn=W_IN_PAD, tk=SEQ,
                   after=small_token)
    mix_tags = ["in"]
    mix_pair, mix_pair_token = _exchange_start("pair", [g_in], "pair_start_mixer")

    grad_x, dsh_m, dsc_m, dg_mix = _first_bwd(dproj, w_in_s, mix_pair_token, x2d, dx1, modv, g_mix)

    pad_lane = lambda t: jnp.pad(t, ((0, 0), (0, D - t.shape[1])))
    small = jnp.concatenate([dsh_m, dsc_m, dga_m, dsh_f, dsc_f, dga_f, dg_mix, dg_ffn, dg_final,
                             pad_lane(db_fgate), loss_lanes, jnp.zeros((SMALL_ROWS - 11, D), f32)], axis=0)
    small_all = _all_gather(small, "gather_small")
    mix_state, mix_token = pair_done(mix_pair, small_all, mix_tags, "mixer")

    small_sum, loss_row = _small_reduce(small_all, mix_token)
    dmod_all = small_all[:, :6, :].reshape(N_DEV, 6 * D)
    g_w_ada = _ada_bwd(c_all, lax.dynamic_slice(dmod_all, (0, dev * ada_cols), (N_DEV, ada_cols)))
    s_gu_t, s_d = from_chips(ffn_state, small_sum, ffn_tags, "ffn", [True, False])
    s_small, = from_chips(small_state, small_sum, ["small"], "small")

    loss = loss_row[0, 0]
    g = {
        "w_ada": g_w_ada[None], "b_ada": small_sum[0:6].reshape(1, 6 * D), "g_mix": small_sum[6:7],
        "b_fgate": small_sum[9:10, :N_FOX_HEADS], "g_ffn": small_sum[7:8], "w_ffn_gate": s_gu_t[:W_FF_SH],
        "w_ffn_up": s_gu_t[FF_PAD:FF_PAD + W_FF_SH], "w_ffn_down": s_d[None, :W_FF_SH],
        "g_final": small_sum[8], "w_br_a": s_small[:rows_a].reshape(1, FOX_W, W_BR_SH),
        "w_br_b": s_small[rows_a:rows_a + rows_b].reshape(1, DIL_OUT_W, W_BR_SH), "w_out": s_small[None, rows_a + rows_b:],
    }
    w = {"w_ada": w_ada, "b_ada": b_ada, "g_mix": g_mix, "w_in": w_in, "b_fgate": b_fgate, "w_br_a": w_br_a,
         "w_br_b": w_br_b, "w_out": w_out, "g_ffn": g_ffn, "w_ffn_gate": w_ffn_gate, "w_ffn_up": w_ffn_up,
         "w_ffn_down": w_ffn_down, "g_final": g_final}
    m = {"w_ada": m_w_ada, "b_ada": m_b_ada, "g_mix": m_g_mix, "w_in": m_w_in, "b_fgate": m_b_fgate,
         "w_br_a": m_w_br_a, "w_br_b": m_w_br_b, "w_out": m_w_out, "g_ffn": m_g_ffn, "w_ffn_gate": m_w_ffn_gate,
         "w_ffn_up": m_w_ffn_up, "w_ffn_down": m_w_ffn_down, "g_final": m_g_final}
    v = {"w_ada": v_w_ada, "b_ada": v_b_ada, "g_mix": v_g_mix, "w_in": v_w_in, "b_fgate": v_b_fgate,
         "w_br_a": v_w_br_a, "w_br_b": v_w_br_b, "w_out": v_w_out, "g_ffn": v_g_ffn, "w_ffn_gate": v_w_ffn_gate,
         "w_ffn_up": v_w_ffn_up, "w_ffn_down": v_w_ffn_down, "g_final": v_g_final}
    names = list(w)
    delta, new_m, new_v = {}, {}, {}

    transposed = ("w_in", "w_ffn_gate", "w_ffn_up")

    def update(n):
        shape = w[n].shape
        if n in transposed:
            g_t = g[n]
            dl, mn, vn = _adamw(w[n][0].T, g_t, m[n][0].T, v[n][0].T, "adamw_" + n)
            g[n], delta[n], new_m[n], new_v[n] = g_t.T[None], dl.T[None], mn.T[None], vn.T[None]
            return
        two_d = (lambda t: t.reshape(shape[-2:])) if len(shape) == 3 else (lambda t: t)
        dl, mn, vn = _adamw(two_d(w[n]), two_d(g[n]), two_d(m[n]), two_d(v[n]), "adamw_" + n)
        delta[n], new_m[n], new_v[n] = dl.reshape(shape), mn.reshape(shape), vn.reshape(shape)

    for n in list(g):
        update(n)
    done = sum(delta[n].reshape(-1)[:N_FOX_HEADS] for n in g)
    s_in_t, = from_chips(mix_state, done, mix_tags, "mixer", [True])
    g["w_in"] = s_in_t[:W_IN_SH]
    update("w_in")

    return (loss, grad_x[None], *[g[n] for n in names], *[delta[n] for n in names],
            *[new_m[n] for n in names], *[new_v[n] for n in names])
```

```python
import functools

import jax
import jax.numpy as jnp
from jax import lax
from jax.experimental import pallas as pl
from jax.experimental.pallas import tpu as pltpu

f32 = jnp.float32
bf16 = jnp.bfloat16
SDS = jax.ShapeDtypeStruct
MESH = pl.DeviceIdType.MESH

N_DEV = 8
D = 1024
SEQ = 2048
HEAD_DIM = 64
N_FOX_HEADS = 8
FOX_W = 512
DIL_W = 768
DIL_OUT_W = 256
ROT_DIM = 16
ROPE_THETA = 500000.0
D_FF = 2816
IN_COLS = 5896
EPS = 1e-6
NEG = -1e30
ATT_SCALE = HEAD_DIM ** -0.5

ADAM_LR = 0.001
ADAM_B1 = 0.9
ADAM_B2 = 0.999
ADAM_EPS = 1e-08
ADAM_WD = 0.01
ADAM_STEP = 10

C_GA, C_GB, C_QB, C_KB, C_VB, C_QA, C_KA, C_VA, C_F = 0, 1024, 2304, 3072, 3840, 4608, 5120, 5632, 6144
PROJ_W = 6272
LANES = 128
VMEM_LIMIT = 52 * 1024 * 1024

W_IN_SH, W_IN_PAD = IN_COLS // N_DEV, 768
W_BR_SH = D // N_DEV
W_FF_SH, FF_PAD = D_FF // N_DEV, 384
FF_HID = N_DEV * FF_PAD
SMALL_ROWS = 16


def _params(sem=None):
    if sem is None:
        return pltpu.CompilerParams(vmem_limit_bytes=VMEM_LIMIT)
    return pltpu.CompilerParams(dimension_semantics=sem, vmem_limit_bytes=VMEM_LIMIT)


def _rowwise(fn, name, tiled, vecs, outs, reds=(), tile=256):
    nt, nv, no = len(tiled), len(vecs), len(outs)
    rows = tiled[0][0].shape[0]
    assert rows % tile == 0

    def body(*refs):
        tin = [r[...] for r in refs[:nt]]
        vin = [r[...] for r in refs[nt:nt + nv]]
        orefs = refs[nt + nv:nt + nv + no]
        rrefs = refs[nt + nv + no:]
        touts, routs = fn(tin, vin)
        for r, t in zip(orefs, touts, strict=True):
            r[...] = t.astype(r.dtype)
        if rrefs:
            @pl.when(pl.program_id(0) == 0)
            def _():
                for r in rrefs:
                    r[...] = jnp.zeros_like(r)
            for r, t in zip(rrefs, routs, strict=True):
                r[...] += t

    def col_map(cb):
        return lambda i: (i, cb)

    def whole_map(nd):
        return lambda i: (0,) * nd

    in_specs = [pl.BlockSpec((tile, w), col_map(cb)) for (_, w, cb) in tiled]
    in_specs += [pl.BlockSpec(v.shape, whole_map(v.ndim)) for v in vecs]
    out_specs = [pl.BlockSpec((tile, w), lambda i: (i, 0)) for (w, _) in outs]
    out_specs += [pl.BlockSpec((1, w), lambda i: (0, 0)) for w in reds]
    out_shape = [SDS((rows, w), dt) for (w, dt) in outs] + [SDS((1, w), f32) for w in reds]
    res = pl.pallas_call(
        body, grid=(rows // tile,), in_specs=in_specs, out_specs=out_specs, out_shape=out_shape, name=name,
        compiler_params=_params(("arbitrary",)),
    )(*[t[0] for t in tiled], *vecs)
    return res


def _matmul(a, b, *, ta=False, tb=False, out_dtype=f32, name, tm, tn, tk, by_shard=False, after=None):
    m, k = (a.shape[1], a.shape[0]) if ta else a.shape
    if by_shard and not ta:
        n, kb = (b.shape[1], N_DEV * b.shape[2]) if tb else (N_DEV * b.shape[2], b.shape[1])
        assert (tk if tb else tn) == b.shape[2]
    else:
        n, kb = (b.shape[0], b.shape[1]) if tb else (b.shape[1], b.shape[0])
    assert kb == k and m % tm == 0 and n % tn == 0 and k % tk == 0
    nk = k // tk
    dims = (((0 if ta else 1,), (1 if tb else 0,)), ((), ()))
    b_stacked = by_shard and not ta
    o_stacked = by_shard and ta

    def body(a_ref, b_ref, *rest):
        o_ref, *acc = rest[1:] if after is not None else rest
        bv = b_ref[0] if b_stacked else b_ref[...]
        p = lax.dot_general(a_ref[...].astype(bf16), bv.astype(bf16), dims, preferred_element_type=f32)

        def put(val):
            if o_stacked:
                o_ref[0] = val.astype(o_ref.dtype)
            else:
                o_ref[...] = val.astype(o_ref.dtype)

        if nk == 1:
            put(p)
        else:
            acc_ref, = acc
            kk = pl.program_id(2)

            @pl.when(kk == 0)
            def _():
                acc_ref[...] = p

            @pl.when(kk > 0)
            def _():
                acc_ref[...] += p

            @pl.when(kk == nk - 1)
            def _():
                put(acc_ref[...])

    a_spec = pl.BlockSpec((tk, tm), lambda i, j, kk: (kk, i)) if ta else pl.BlockSpec((tm, tk), lambda i, j, kk: (i, kk))
    if b_stacked and tb:
        b_spec = pl.BlockSpec((1, tn, tk), lambda i, j, kk: (kk, j, 0))
    elif b_stacked:
        b_spec = pl.BlockSpec((1, tk, tn), lambda i, j, kk: (j, kk, 0))
    elif tb:
        b_spec = pl.BlockSpec((tn, tk), lambda i, j, kk: (j, kk))
    else:
        b_spec = pl.BlockSpec((tk, tn), lambda i, j, kk: (kk, j))
    if o_stacked:
        assert tn == n // N_DEV
        out_spec = pl.BlockSpec((1, tm, tn), lambda i, j, kk: (j, i, 0))
        out_shape = SDS((N_DEV, m, tn), out_dtype)
    else:
        out_spec = pl.BlockSpec((tm, tn), lambda i, j, kk: (i, j))
        out_shape = SDS((m, n), out_dtype)
    extra_specs, extra = ([pl.BlockSpec(memory_space=pl.ANY)], [after]) if after is not None else ([], [])
    return pl.pallas_call(
        body, grid=(m // tm, n // tn, nk), in_specs=[a_spec, b_spec] + extra_specs, out_specs=out_spec,
        out_shape=out_shape, name=name, scratch_shapes=[pltpu.VMEM((tm, tn), f32)] if nk > 1 else [],
        compiler_params=_params(("parallel", "parallel", "arbitrary")),
    )(a, b, *extra)


def _matmul_stack(a, b, *, ta=False, tb=False, out_dtype=f32, name):
    def lanes(ref):
        return jnp.concatenate([ref[j] for j in range(N_DEV)], axis=1).astype(bf16)

    if ta:
        w = b.shape[1] // N_DEV

        def body(a_ref, b_ref, o_ref):
            p = _tn(a_ref[...].astype(bf16), b_ref[...].astype(bf16))
            for j in range(N_DEV):
                o_ref[j] = p[:, j * w:(j + 1) * w].astype(o_ref.dtype)

        return pl.pallas_call(body, out_shape=SDS((N_DEV, a.shape[1], w), out_dtype), name=name,
                              compiler_params=_params())(a, b)

    m, half = a.shape[0], a.shape[0] // 2
    n = b.shape[1] if tb else N_DEV * b.shape[2]

    def body(a_ref, b_ref, o_ref):
        av = a_ref[...].astype(bf16)
        o_ref[...] = (_nt(av, lanes(b_ref)) if tb else jnp.dot(av, lanes(b_ref), preferred_element_type=f32)
                      ).astype(o_ref.dtype)

    return pl.pallas_call(
        body, grid=(2,), in_specs=[pl.BlockSpec((half, a.shape[1]), lambda i: (i, 0)),
                                   pl.BlockSpec(b.shape, lambda i: (0, 0, 0))],
        out_specs=pl.BlockSpec((half, n), lambda i: (i, 0)), out_shape=SDS((m, n), out_dtype), name=name,
        compiler_params=_params(("parallel",)),
    )(a, b)


def _matmul_rows(form, a, b, after, fn, tiled, vecs, outs, reds, *, name, tm=512):
    m = a.shape[0]
    assert m % tm == 0
    tiled = [t if isinstance(t, tuple) else (t, t.shape[1], 0) for t in tiled]
    nt, nv, no = len(tiled), len(vecs), len(outs)

    def body(a_ref, b_ref, after_ref, *refs):
        if form == "nt_stack":
            w = b.shape[2]
            acc = _nt(a_ref[:, 0:w], b_ref[0])
            for j in range(1, N_DEV):
                acc = acc + _nt(a_ref[:, j * w:(j + 1) * w], b_ref[j])
        elif form == "nt":
            acc = _nt(a_ref[...], b_ref[...])
        else:
            acc = jnp.dot(a_ref[...], b_ref[...], preferred_element_type=f32)
        orefs, rrefs = refs[nt + nv:nt + nv + no], refs[nt + nv + no:]
        touts, routs = fn([acc] + [r[...] for r in refs[:nt]], [r[...] for r in refs[nt:nt + nv]])
        for r, t in zip(orefs, touts, strict=True):
            r[...] = t.astype(r.dtype)

        @pl.when(pl.program_id(0) == 0)
        def _():
            for r in rrefs:
                r[...] = jnp.zeros_like(r)
        for r, t in zip(rrefs, routs, strict=True):
            r[...] += t

    def whole_map(nd):
        return lambda i: (0,) * nd

    def rows(width, cb=0):
        return pl.BlockSpec((tm, width), lambda i: (i, cb))

    return pl.pallas_call(
        body, grid=(m // tm,),
        in_specs=[rows(a.shape[1]), pl.BlockSpec(b.shape, whole_map(b.ndim), pipeline_mode=pl.Buffered(1)),
                  pl.BlockSpec(memory_space=pl.ANY)]
        + [rows(width, cb) for _, width, cb in tiled] + [pl.BlockSpec(v.shape, whole_map(v.ndim)) for v in vecs],
        out_specs=[rows(width) for width, _ in outs] + [pl.BlockSpec((1, width), lambda i: (0, 0)) for width in reds],
        out_shape=[SDS((m, width), dt) for width, dt in outs] + [SDS((1, width), f32) for width in reds], name=name,
        compiler_params=_params(("arbitrary",)),
    )(a, b, after, *[t[0] for t in tiled], *vecs)


def _rms(x):
    r = lax.rsqrt(jnp.mean(x * x, axis=-1, keepdims=True) + EPS)
    return r, x * r


def _rms_bwd(r, xn, dxn):
    return r * (dxn - xn * jnp.mean(dxn * xn, axis=-1, keepdims=True))


def _colsum(t):
    return jnp.sum(t, axis=0, keepdims=True)


def _sigmoid(x):
    return 0.5 * jnp.tanh(0.5 * x) + 0.5


def _modulated_norm(x, g, shift, scale):
    _, xn = _rms(x)
    return (xn * g) * (1.0 + scale) + shift


def _pre1(x, modv, g_mix):
    def fn(t, v):
        (xt,), (mv, g) = t, v
        return [_modulated_norm(xt, g, mv[0:1], mv[1:2])], []
    return _rowwise(fn, "pre1", [(x, D, 0)], [modv, g_mix], [(D, bf16)])[0]


def _post1(merged, w_o, x, modv, g_ffn):
    def fn(t, v):
        (mt, xt), (mv, g) = t, v
        x1 = xt + mv[2:3] * mt
        return [mt, x1, _modulated_norm(x1, g, mv[3:4], mv[4:5])], []
    return _matmul_rows("nn", merged, w_o, x, fn, [x], [modv, g_ffn], [(D, f32), (D, f32), (D, bf16)], [],
                        name="post1")


def _ffn_in(h, w_stack):
    def body(h_ref, w_ref, act_ref, au_ref):
        p = jnp.dot(h_ref[...], w_ref[0], preferred_element_type=f32)
        a, u = p[:, :FF_PAD], p[:, FF_PAD:]
        act_ref[...] = (a * _sigmoid(a) * u).astype(act_ref.dtype)
        au_ref[...] = p.astype(au_ref.dtype)

    return pl.pallas_call(
        body, grid=(N_DEV,),
        in_specs=[pl.BlockSpec((SEQ, D), lambda j: (0, 0)), pl.BlockSpec((1, D, 2 * FF_PAD), lambda j: (j, 0, 0))],
        out_specs=[pl.BlockSpec((SEQ, FF_PAD), lambda j: (0, j)), pl.BlockSpec((SEQ, 2 * FF_PAD), lambda j: (0, j))],
        out_shape=(SDS((SEQ, FF_HID), bf16), SDS((SEQ, 2 * FF_HID), bf16)), name="ffn_in",
        compiler_params=_params(("parallel",)),
    )(h, w_stack)


def _ffn_bwd_in(dff, w_down_stack, au):
    def body(d_ref, w_ref, au_ref, o_ref):
        dact = _nt(d_ref[...], w_ref[0])
        p = au_ref[...].astype(f32)
        a, u = p[:, :FF_PAD], p[:, FF_PAD:]
        sg = _sigmoid(a)
        o_ref[...] = jnp.concatenate([dact * u * (sg * (1.0 + a * (1.0 - sg))), dact * (a * sg)],
                                     axis=1).astype(o_ref.dtype)

    return pl.pallas_call(
        body, grid=(N_DEV,),
        in_specs=[pl.BlockSpec((SEQ, D), lambda j: (0, 0)), pl.BlockSpec((1, FF_PAD, D), lambda j: (j, 0, 0)),
                  pl.BlockSpec((SEQ, 2 * FF_PAD), lambda j: (0, j))],
        out_specs=pl.BlockSpec((SEQ, 2 * FF_PAD), lambda j: (0, j)),
        out_shape=SDS((SEQ, 2 * FF_HID), bf16), name="ffn_bwd_in", compiler_params=_params(("parallel",)),
    )(dff, w_down_stack, au)


def _final(act, w_down, x1, target, modv, g_final):
    def fn(t, v):
        (fft, x1t, tgt), (mv, g) = t, v
        x2 = x1t + mv[5:6] * fft
        r, xn = _rms(x2)
        err = xn * g - tgt
        dy = err * (1.0 / D)
        dx2 = _rms_bwd(r, xn, dy * g)
        return [dx2, dx2 * mv[5:6]], [_colsum(dy * xn), _colsum(dx2 * fft), _colsum(err * err) * (0.5 / D)]
    return _matmul_rows("nn", act, w_down, x1, fn, [x1, target], [modv, g_final], [(D, f32), (D, bf16)], [D, D, D],
                        name="final")


def _mid_bwd(dau, w_stack, after, x1, dx2, mix, modv, g_ffn):
    def fn(t, v):
        (dh, x1t, dx2t, mt), (mv, g) = t, v
        r, xn = _rms(x1t)
        dn = dh * (1.0 + mv[4:5])
        dx1 = dx2t + _rms_bwd(r, xn, dn * g)
        return [dx1, dx1 * mv[2:3]], [_colsum(dh), _colsum(dh * (xn * g)), _colsum(dn * xn), _colsum(dx1 * mt)]
    return _matmul_rows("nt_stack", dau, w_stack, after, fn, [x1, dx2, mix], [modv, g_ffn], [(D, f32), (D, bf16)],
                        [D, D, D, D], name="mid_bwd")


def _first_bwd(dproj, w_stack, after, x, dx1, modv, g_mix):
    def fn(t, v):
        (dh, xt, dx1t), (mv, g) = t, v
        r, xn = _rms(xt)
        dn = dh * (1.0 + mv[1:2])
        return [dx1t + _rms_bwd(r, xn, dn * g)], [_colsum(dh), _colsum(dh * (xn * g)), _colsum(dn * xn)]
    return _matmul_rows("nt_stack", dproj, w_stack, after, fn, [x, dx1], [modv, g_mix], [(D, f32)], [D, D, D],
                        name="first_bwd")


def _merge_fwd(ya, yb, proj):
    def fn(t, v):
        ya_t, yb_t, ga, gb = t
        return [_sigmoid(ga) * ya_t + _sigmoid(gb) * yb_t], []
    return _rowwise(fn, "merge_fwd", [(ya, D, 0), (yb, D, 0), (proj, D, C_GA // D), (proj, D, C_GB // D)], [],
                    [(D, bf16)])[0]


def _merge_bwd(dmix, w_o, after, ya, yb, proj):
    def fn(t, v):
        dm, ya_t, yb_t, ga, gb = t
        sa, sb = _sigmoid(ga), _sigmoid(gb)
        return [dm * sa, dm * sb, dm * ya_t * (sa * (1.0 - sa)), dm * yb_t * (sb * (1.0 - sb))], []
    return _matmul_rows("nt", dmix, w_o, after, fn, [ya, yb, (proj, D, C_GA // D), (proj, D, C_GB // D)], [],
                        [(D, bf16), (D, bf16), (D, bf16), (D, bf16)], [], name="merge_bwd")


def _rope_tables():
    half = ROT_DIM // 2
    pos = jnp.arange(SEQ, dtype=f32)
    inv_freq = ROPE_THETA ** (-jnp.arange(0, ROT_DIM, 2, dtype=f32) / ROT_DIM)
    ang = pos[:, None] * inv_freq[None, :]
    cos, sin = jnp.cos(ang), jnp.sin(ang)
    pad = jnp.zeros((SEQ, HEAD_DIM - ROT_DIM), f32)
    zero = jnp.zeros((SEQ, half), f32)
    c_head = jnp.concatenate([cos, cos, pad + 1.0], axis=1)
    lo_head = jnp.concatenate([-sin, zero, pad], axis=1)
    hi_head = jnp.concatenate([zero, sin, pad], axis=1)
    return tuple(jnp.concatenate([t, t], axis=1) for t in (c_head, lo_head, hi_head))


def _over_heads(tables):
    return [jnp.tile(t, (1, DIL_W // LANES)) for t in tables]


def _rope_fwd(proj, tables):
    half = ROT_DIM // 2

    def fn(t, v):
        q, k = t[:2]
        c, lo, hi = _over_heads(t[2:])
        rot = lambda z: z * c + pltpu.roll(z, DIL_W - half, 1) * lo + pltpu.roll(z, half, 1) * hi
        return [rot(q) * ATT_SCALE, rot(k)], []
    return _rowwise(fn, "rope_fwd", [(proj, DIL_W, C_QB // DIL_W), (proj, DIL_W, C_KB // DIL_W)]
                    + [(tb, LANES, 0) for tb in tables], [], [(DIL_W, f32)] * 2)


def _rope_bwd(dqs, dks, tables):
    half = ROT_DIM // 2

    def fn(t, v):
        dq_t, dk_t = jnp.concatenate(t[:N_GROUPS], axis=1), jnp.concatenate(t[N_GROUPS:2 * N_GROUPS], axis=1)
        c, lo, hi = _over_heads(t[2 * N_GROUPS:])
        rot_t = lambda z: z * c + pltpu.roll(z * lo, half, 1) + pltpu.roll(z * hi, DIL_W - half, 1)
        return [rot_t(dq_t), rot_t(dk_t)], []
    return _rowwise(fn, "rope_bwd", [(a, DIL_OUT_W, 0) for a in (*dqs, *dks)] + [(tb, LANES, 0) for tb in tables],
                    [], [(DIL_W, bf16), (DIL_W, bf16)])


def _head_bcast_sum(d):
    lane = lax.broadcasted_iota(jnp.int32, d.shape, 1)
    out = jnp.zeros_like(d)
    for h in range(d.shape[1] // HEAD_DIM):
        sel = (lane >= h * HEAD_DIM) & (lane < (h + 1) * HEAD_DIM)
        out = jnp.where(sel, jnp.sum(jnp.where(sel, d, 0.0), axis=1, keepdims=True), out)
    return out


def _dil_combine(outs, lses):
    def fn(t, v):
        o0, o1, o2, l0, l1, l2 = t
        m = jnp.maximum(jnp.maximum(l0, l1), l2)
        w0, w1, w2 = jnp.exp(l0 - m), jnp.exp(l1 - m), jnp.exp(l2 - m)
        tot = w0 + w1 + w2
        return [(w0 * o0 + w1 * o1 + w2 * o2) / tot, m + jnp.log(tot)], []
    w = DIL_OUT_W
    return _rowwise(fn, "dil_combine", [(t, w, 0) for t in (*outs, *lses)], [], [(w, f32), (w, f32)])


def _dil_delta(dyb_h, yb_h):
    def fn(t, v):
        return [_head_bcast_sum(t[0] * t[1])], []
    return _rowwise(fn, "dil_delta", [(dyb_h, DIL_OUT_W, 0), (yb_h, DIL_OUT_W, 0)], [], [(DIL_OUT_W, f32)])[0]


def _adamw_math(wt, gt, mt, vt):
    mn = ADAM_B1 * mt + (1.0 - ADAM_B1) * gt
    vn = ADAM_B2 * vt + (1.0 - ADAM_B2) * (gt * gt)
    m_hat = mn / (1.0 - ADAM_B1 ** ADAM_STEP)
    v_hat = vn / (1.0 - ADAM_B2 ** ADAM_STEP)
    return -ADAM_LR * (m_hat / (jnp.sqrt(v_hat) + ADAM_EPS) + ADAM_WD * wt), mn, vn


def _adamw(w, g, m, v, name):
    shape = w.shape
    if w.ndim == 1:
        w, g, m, v = (t.reshape(1, -1) for t in (w, g, m, v))
    rows, cols = w.shape
    if rows % 8 and rows > 8:
        return _adamw_by_cols(w, g, m, v, name)
    tile = 256 if rows % 256 == 0 and rows > 512 else rows

    def fn(t, _):
        return list(_adamw_math(*t)), []
    delta, mn, vn = _rowwise(fn, name, [(w, cols, 0), (g, cols, 0), (m, cols, 0), (v, cols, 0)], [],
                             [(cols, f32)] * 3, tile=tile)
    return delta.reshape(shape), mn.reshape(shape), vn.reshape(shape)


def _adamw_by_cols(w, g, m, v, name, tile=256):
    rows, cols = w.shape

    def body(w_ref, g_ref, m_ref, v_ref, d_ref, mn_ref, vn_ref):
        d_ref[...], mn_ref[...], vn_ref[...] = _adamw_math(w_ref[...], g_ref[...], m_ref[...], v_ref[...])

    spec = pl.BlockSpec((rows, tile), lambda j: (0, j))
    return pl.pallas_call(body, grid=(cols // tile,), in_specs=[spec] * 4, out_specs=[spec] * 3,
                          out_shape=[SDS((rows, cols), f32)] * 3, name=name,
                          compiler_params=_params(("parallel",)))(w, g, m, v)


def _ada_fwd(c_all, w_shard, b_shard):
    def body(c_ref, w_ref, b_ref, o_ref):
        cv = c_ref[...]
        sc = (cv * _sigmoid(cv)).astype(bf16)
        o_ref[...] = jnp.dot(sc, w_ref[...].astype(bf16), preferred_element_type=f32) + b_ref[...]
    return pl.pallas_call(body, out_shape=SDS((N_DEV, w_shard.shape[1]), f32), name="ada_fwd",
                          compiler_params=_params())(c_all, w_shard, b_shard)


def _ada_bwd(c_all, dmod_cols):
    def body(c_ref, d_ref, o_ref):
        cv = c_ref[...]
        sc = cv * _sigmoid(cv)
        o_ref[...] = lax.dot_general(sc, d_ref[...], (((0,), (0,)), ((), ())), precision=lax.Precision.HIGHEST,
                                     preferred_element_type=f32)
    return pl.pallas_call(body, out_shape=SDS((D, dmod_cols.shape[1]), f32), name="ada_bwd",
                          compiler_params=_params())(c_all, dmod_cols)


def _small_reduce(gathered, after):
    def body(g_ref, after_ref, o_ref, loss_ref):
        acc = g_ref[0]
        for d in range(1, N_DEV):
            acc = acc + g_ref[d]
        o_ref[...] = acc
        loss_ref[...] = jnp.zeros((1, LANES), f32) + jnp.sum(acc[10:11, :])
    return pl.pallas_call(body, out_shape=(SDS((SMALL_ROWS, D), f32), SDS((1, LANES), f32)), name="small_reduce",
                          in_specs=[pl.BlockSpec(memory_space=pltpu.VMEM), pl.BlockSpec(memory_space=pl.ANY)],
                          compiler_params=_params())(gathered, after)


FOX_BLK = 512
CUM_BLK = 128


def _fold_lanes(t, op):
    out = t[:, :LANES]
    for j in range(1, t.shape[1] // LANES):
        out = op(out, t[:, j * LANES:(j + 1) * LANES])
    return out


def _fox_gate_fwd(proj, b_pad):
    nblk = SEQ // CUM_BLK

    def body(f_ref, b_ref, col_ref):
        r = lax.broadcasted_iota(jnp.int32, (CUM_BLK, CUM_BLK), 0)
        c = lax.broadcasted_iota(jnp.int32, (CUM_BLK, CUM_BLK), 1)
        tri = (r >= c).astype(f32)
        carry = jnp.zeros((1, LANES), f32)
        for blk in range(nblk):
            z = f_ref[blk * CUM_BLK:(blk + 1) * CUM_BLK, :] + b_ref[...]
            logf = jnp.minimum(z, 0.0) - jnp.log1p(jnp.exp(-jnp.abs(z)))
            cs = jnp.dot(tri, logf, precision=lax.Precision.HIGHEST, preferred_element_type=f32) + carry
            col_ref[blk * CUM_BLK:(blk + 1) * CUM_BLK, :] = cs
            carry = cs[CUM_BLK - 1:CUM_BLK, :]

    return pl.pallas_call(
        body, grid=(1,), in_specs=[pl.BlockSpec((SEQ, LANES), lambda i: (0, C_F // LANES)),
                                   pl.BlockSpec((1, LANES), lambda i: (0, 0))],
        out_specs=pl.BlockSpec((SEQ, LANES), lambda i: (0, 0)),
        out_shape=SDS((SEQ, LANES), f32), name="fox_gate_fwd",
        compiler_params=_params(("arbitrary",)),
    )(proj, b_pad)


def _fox_gate_bwd(dF_row, proj, b_pad):
    nblk = SEQ // CUM_BLK

    def body(d_ref, f_ref, b_ref, df_ref, db_ref, col_ref):
        r = lax.broadcasted_iota(jnp.int32, (CUM_BLK, CUM_BLK), 0)
        c = lax.broadcasted_iota(jnp.int32, (CUM_BLK, CUM_BLK), 1)
        tri = (r <= c).astype(f32)
        lane = lax.broadcasted_iota(jnp.int32, (CUM_BLK, LANES), 1)
        col_ref[...] = d_ref[...].T
        carry = jnp.zeros((1, LANES), f32)
        total = jnp.zeros((1, LANES), f32)
        for blk in reversed(range(nblk)):
            rows = slice(blk * CUM_BLK, (blk + 1) * CUM_BLK)
            cs = jnp.dot(tri, col_ref[rows, :], precision=lax.Precision.HIGHEST, preferred_element_type=f32) + carry
            carry = cs[0:1, :]
            z = f_ref[rows, :] + b_ref[...]
            df = jnp.where(lane < N_FOX_HEADS, cs * _sigmoid(-z), 0.0)
            df_ref[rows, :] = df.astype(df_ref.dtype)
            total = total + _colsum(df)
        db_ref[...] = total

    return pl.pallas_call(
        body, grid=(1,), in_specs=[pl.BlockSpec((LANES, SEQ), lambda i: (0, 0)),
                                   pl.BlockSpec((SEQ, LANES), lambda i: (0, C_F // LANES)),
                                   pl.BlockSpec((1, LANES), lambda i: (0, 0))],
        out_specs=[pl.BlockSpec((SEQ, LANES), lambda i: (0, 0)), pl.BlockSpec((1, LANES), lambda i: (0, 0))],
        out_shape=(SDS((SEQ, LANES), bf16), SDS((1, LANES), f32)), name="fox_gate_bwd",
        scratch_shapes=[pltpu.VMEM((SEQ, LANES), f32)],
        compiler_params=_params(("arbitrary",)),
    )(dF_row, proj, b_pad)


def _nt(a, b):
    return lax.dot_general(a, b, (((1,), (1,)), ((), ())), preferred_element_type=f32)


def _tn(a, b):
    return lax.dot_general(a, b, (((0,), (0,)), ((), ())), preferred_element_type=f32)


def _fox_prep(proj, f_col):
    def fn(t, v):
        q, k, vv, fc = t
        lane = lax.broadcasted_iota(jnp.int32, (q.shape[0], LANES), 1)
        qs, ks = [], []
        for h in range(N_FOX_HEADS):
            pair, pos = divmod(h, 2)
            own = (lane >= pos * HEAD_DIM) & (lane < (pos + 1) * HEAD_DIM)
            base = (1 - pos) * HEAD_DIM
            f = fc[:, h:h + 1]
            hi = f.astype(bf16).astype(f32)
            mid = (f - hi).astype(bf16).astype(f32)
            lo = (f - hi) - mid
            one = jnp.ones_like(f)
            qa = jnp.where(own, q[:, pair * LANES:(pair + 1) * LANES] * ATT_SCALE, 0.0)
            ka = k[:, pair * LANES:(pair + 1) * LANES]
            for idx, (qv, kv) in enumerate([(hi, one), (mid, one), (lo, one), (one, -hi), (one, -mid), (one, -lo)]):
                sel = lane == base + idx
                qa = jnp.where(sel, qv, qa)
                ka = jnp.where(sel, kv, ka)
            qs.append(qa)
            ks.append(ka)
        return [jnp.concatenate(qs, axis=1), jnp.concatenate(ks, axis=1), vv], []
    w = N_FOX_HEADS * LANES
    return _rowwise(fn, "fox_prep", [(proj, FOX_W, C_QA // FOX_W), (proj, FOX_W, C_KA // FOX_W),
                                     (proj, FOX_W, C_VA // FOX_W), (f_col, LANES, 0)], [],
                    [(w, bf16), (w, bf16), (FOX_W, bf16)])


def _fox_fwd(q_aug, k_aug, v):
    blk = FOX_BLK
    npair = FOX_W // LANES

    def body(q_ref, k_ref, v_ref, o_ref, max_ref, sum_ref, s_scr):
        i = pl.program_id(1)
        tri = lax.broadcasted_iota(jnp.int32, (blk, blk), 0) >= lax.broadcasted_iota(jnp.int32, (blk, blk), 1)
        qh = [q_ref[:, h * LANES:(h + 1) * LANES] for h in range(2)]

        def logits(c, masked):
            off = pl.multiple_of(c * blk, blk)
            tops = []
            for h in range(2):
                s = _nt(qh[h], k_ref[pl.ds(off, blk), h * LANES:(h + 1) * LANES])
                if masked:
                    s = jnp.where(tri, s, NEG)
                s_scr[h, :, pl.ds(off, blk)] = s
                tops.append(_fold_lanes(s, jnp.maximum))
            return tops

        def pass_a(c, m):
            return tuple(jnp.maximum(a, b) for a, b in zip(m, logits(c, False)))

        m = lax.fori_loop(0, i, pass_a, tuple(jnp.full((blk, LANES), NEG, f32) for _ in range(2)))
        mx = [jnp.max(jnp.maximum(a, b), axis=1, keepdims=True) for a, b in zip(m, logits(i, True))]

        def pass_b(c, carry):
            off = pl.multiple_of(c * blk, blk)
            vv = v_ref[pl.ds(off, blk), :]
            new = []
            for h in range(2):
                l, acc = carry[h]
                p = jnp.exp(s_scr[h, :, pl.ds(off, blk)] - mx[h]).astype(bf16)
                new.append((l + _fold_lanes(p.astype(f32), jnp.add), acc + jnp.dot(p, vv, preferred_element_type=f32)))
            return tuple(new)

        zero = jnp.zeros((blk, LANES), f32)
        (l_a, acc_a), (l_b, acc_b) = lax.fori_loop(0, i + 1, pass_b, ((zero, zero), (zero, zero)))
        l_a = jnp.sum(l_a, axis=1, keepdims=True)
        l_b = jnp.sum(l_b, axis=1, keepdims=True)
        first = lax.broadcasted_iota(jnp.int32, (blk, LANES), 1) < HEAD_DIM
        o_ref[...] = jnp.where(first, acc_a / l_a, acc_b / l_b)
        max_ref[0] = jnp.where(first, mx[0], mx[1])
        sum_ref[0] = jnp.where(first, l_a, l_b)

    return pl.pallas_call(
        body, grid=(npair, SEQ // blk),
        in_specs=[pl.BlockSpec((blk, 2 * LANES), lambda p, i: (i, p)),
                  pl.BlockSpec((SEQ, 2 * LANES), lambda p, i: (0, p)),
                  pl.BlockSpec((SEQ, LANES), lambda p, i: (0, p))],
        out_specs=[pl.BlockSpec((blk, LANES), lambda p, i: (i, p))]
        + [pl.BlockSpec((1, blk, LANES), lambda p, i: (p, i, 0))] * 2,
        out_shape=(SDS((SEQ, FOX_W), f32),) + (SDS((npair, SEQ, LANES), f32),) * 2, name="fox_fwd",
        scratch_shapes=[pltpu.VMEM((2, blk, SEQ), f32)],
        compiler_params=_params(("parallel", "arbitrary")),
    )(q_aug, k_aug, v)


def _fox_bwd(q_aug, k_aug, v, do, o, row_max, row_sum, after):
    blk = FOX_BLK
    npair = FOX_W // LANES
    nblk = SEQ // blk

    def body(q_ref, k_ref, v_ref, do_ref, o_ref, max_ref, sum_ref, after_ref, dq_ref, dk_ref, dv_ref, df_ref, dq_acc,
             delta_ref, inv_ref):
        inv_ref[...] = 1.0 / sum_ref[0]
        lane_s = lax.broadcasted_iota(jnp.int32, (SEQ, LANES), 1)
        prod = do_ref[...].astype(bf16).astype(f32) * o_ref[...]
        d_a = jnp.sum(jnp.where(lane_s < HEAD_DIM, prod, 0.0), axis=1, keepdims=True)
        d_b = jnp.sum(jnp.where(lane_s >= HEAD_DIM, prod, 0.0), axis=1, keepdims=True)
        delta_ref[...] = jnp.where(lane_s < HEAD_DIM, d_a, d_b)
        dq_acc[...] = jnp.zeros_like(dq_acc)
        df_ref[...] = jnp.zeros_like(df_ref)
        lane = lax.broadcasted_iota(jnp.int32, (blk, LANES), 1)
        own = [lane < HEAD_DIM, lane >= HEAD_DIM]
        tri = lax.broadcasted_iota(jnp.int32, (blk, blk), 0) >= lax.broadcasted_iota(jnp.int32, (blk, blk), 1)

        def q_slab(qoff, h):
            return q_ref[pl.ds(qoff, blk), h * LANES:(h + 1) * LANES]

        def probs(qoff, h, k_h, masked):
            s = _nt(q_slab(qoff, h), k_h)
            if masked:
                s = jnp.where(tri, s, NEG)
            col = slice(h * HEAD_DIM, h * HEAD_DIM + 1)
            weights = jnp.exp(s - max_ref[0, pl.ds(qoff, blk), col]).astype(bf16).astype(f32)
            return weights * inv_ref[pl.ds(qoff, blk), col]

        def k_slabs(koff):
            return [k_ref[pl.ds(koff, blk), h * LANES:(h + 1) * LANES] for h in range(2)]

        def kv_step(kj, _):
            koff = pl.multiple_of(kj * blk, blk)
            k_aug = k_slabs(koff)
            k_own = [jnp.where(own[h], k_aug[h], jnp.zeros_like(k_aug[h])) for h in range(2)]
            vv = v_ref[pl.ds(koff, blk), :]
            v_own = [jnp.where(own[h], vv, jnp.zeros_like(vv)) for h in range(2)]

            def q_tile(qi, carry, masked):
                qoff = pl.multiple_of(qi * blk, blk)
                dd = do_ref[pl.ds(qoff, blk), :].astype(bf16)
                new, dq_add = [], None
                for h in range(2):
                    dk_h, dv_h, dcol = carry[h]
                    p = probs(qoff, h, k_aug[h], masked)
                    dl = p * (_nt(dd, v_own[h]) - delta_ref[pl.ds(qoff, blk), h * HEAD_DIM:h * HEAD_DIM + 1])
                    dlb = dl.astype(bf16)
                    part = jnp.dot(dlb, k_own[h], preferred_element_type=f32)
                    dq_add = part if dq_add is None else dq_add + part
                    new.append((dk_h + _tn(dlb, q_slab(qoff, h)), dv_h + _tn(p.astype(bf16), dd),
                                dcol + _colsum(dl)))
                dq_acc[pl.ds(qoff, blk), :] += dq_add * ATT_SCALE
                return tuple(new)

            zero = (jnp.zeros((blk, LANES), f32), jnp.zeros((blk, LANES), f32), jnp.zeros((1, blk), f32))
            carry = q_tile(kj, (zero, zero), True)
            (dk_a, dv_a, dcol_a), (dk_b, dv_b, dcol_b) = lax.fori_loop(
                kj + 1, nblk, lambda qi, cr: q_tile(qi, cr, False), carry)
            dk_ref[pl.ds(koff, blk), :] = jnp.where(own[0], dk_a, dk_b).astype(dk_ref.dtype)
            dv_ref[pl.ds(koff, blk), :] = jnp.where(own[0], dv_a, dv_b).astype(dv_ref.dtype)
            df_ref[0, 0:1, pl.ds(koff, blk)] = -dcol_a
            df_ref[0, 1:2, pl.ds(koff, blk)] = -dcol_b
            return 0

        lax.fori_loop(0, nblk, kv_step, 0)
        dq_ref[...] = dq_acc[...].astype(dq_ref.dtype)

    pair_aug = pl.BlockSpec((SEQ, 2 * LANES), lambda p: (0, p))
    slab = pl.BlockSpec((SEQ, LANES), lambda p: (0, p))
    per_pair = pl.BlockSpec((1, SEQ, LANES), lambda p: (p, 0, 0))
    rows = pl.BlockSpec((1, 8, SEQ), lambda p: (p, 0, 0))
    return pl.pallas_call(
        body, grid=(npair,),
        in_specs=[pair_aug, pair_aug, slab, slab, slab, per_pair, per_pair, pl.BlockSpec(memory_space=pl.ANY)],
        out_specs=[slab, slab, slab, rows],
        out_shape=(SDS((SEQ, FOX_W), bf16),) * 3 + (SDS((npair, 8, SEQ), f32),), name="fox_bwd",
        scratch_shapes=[pltpu.VMEM((SEQ, LANES), f32)] * 3,
        compiler_params=_params(("parallel",)),
    )(q_aug, k_aug, v, do, o, row_max, row_sum, after)


DIL_BLK = 128
DILATIONS = (1, 4, 16)
N_GROUPS = len(DILATIONS)
DIL_PAIRS = DIL_OUT_W // LANES


def _dil_blocks(d):
    r1 = lax.broadcasted_iota(jnp.int32, (2 * DIL_BLK, DIL_BLK), 0) & (DIL_BLK - 1)
    c1 = lax.broadcasted_iota(jnp.int32, (2 * DIL_BLK, DIL_BLK), 1)
    r2 = lax.broadcasted_iota(jnp.int32, (2 * DIL_BLK, 2 * DIL_BLK), 0) & (DIL_BLK - 1)
    c2 = lax.broadcasted_iota(jnp.int32, (2 * DIL_BLK, 2 * DIL_BLK), 1)
    band = ((c2 < DIL_BLK) & (c2 >= r2)) | ((c2 >= DIL_BLK) & (c2 - DIL_BLK <= r2))
    out = []
    for r in range(d):
        for b in range(SEQ // d // DIL_BLK):
            rows = pl.ds(r + d * DIL_BLK * b, DIL_BLK, stride=d)
            if b == 0:
                out.append((rows, rows, r1 >= c1))
            else:
                out.append((rows, pl.ds(r + d * DIL_BLK * (b - 1), 2 * DIL_BLK, stride=d), band))
    return out


def _dil_v_spec(g):
    return pl.BlockSpec((SEQ, LANES), lambda p: (0, C_VB // LANES + DIL_PAIRS * g + p))


def _stack_heads(t, first):
    zero = jnp.zeros_like(t)
    return jnp.concatenate([jnp.where(first, t, zero), jnp.where(first, zero, t)], axis=0)


def _dil_fwd(q, k, v, g):
    def body(q_ref, k_ref, v_ref, o_ref, lse_ref):
        first = lax.broadcasted_iota(jnp.int32, (DIL_BLK, LANES), 1) < HEAD_DIM
        for rows, krows, mask in _dil_blocks(DILATIONS[g]):
            qv, kk, vv = q_ref[rows, :].astype(bf16), k_ref[krows, :].astype(bf16), v_ref[krows, :].astype(bf16)
            s = jnp.where(mask, _nt(_stack_heads(qv, first), kk), NEG)
            m = jnp.max(s, axis=1, keepdims=True)
            p = jnp.exp(s - m)
            l = jnp.sum(p, axis=1, keepdims=True)
            out = jnp.dot(p.astype(bf16), vv, preferred_element_type=f32) / l
            lse = m + jnp.log(l)
            o_ref[rows, :] = jnp.where(first, out[:DIL_BLK], out[DIL_BLK:])
            lse_ref[rows, :] = jnp.where(first, lse[:DIL_BLK], lse[DIL_BLK:])

    grouped = pl.BlockSpec((SEQ, LANES), lambda p: (0, DIL_PAIRS * g + p))
    own = pl.BlockSpec((SEQ, LANES), lambda p: (0, p))
    shape = SDS((SEQ, DIL_OUT_W), f32)
    return pl.pallas_call(
        body, grid=(DIL_PAIRS,), in_specs=[grouped, grouped, _dil_v_spec(g)], out_specs=[own] * 2,
        out_shape=(shape, shape),
        name=f"dil_fwd_{DILATIONS[g]}", compiler_params=_params(("parallel",)),
    )(q, k, v)


def _dil_bwd(q, k, v, do, lse, delta, g):
    def body(q_ref, k_ref, v_ref, do_ref, lse_ref, dl_ref, dq_ref, dk_ref, dv_ref):
        first = lax.broadcasted_iota(jnp.int32, (DIL_BLK, LANES), 1) < HEAD_DIM
        dk_ref[...] = jnp.zeros_like(dk_ref)
        dv_ref[...] = jnp.zeros_like(dv_ref)
        for rows, krows, mask in _dil_blocks(DILATIONS[g]):
            qv, kk, vv = q_ref[rows, :].astype(bf16), k_ref[krows, :].astype(bf16), v_ref[krows, :].astype(bf16)
            lsev, delv = lse_ref[rows, :], dl_ref[rows, :]
            q2 = _stack_heads(qv, first)
            do2 = _stack_heads(do_ref[rows, :].astype(bf16), first)
            per_head = lambda t: jnp.concatenate([t[:, 0:1], t[:, HEAD_DIM:HEAD_DIM + 1]], axis=0)
            p = jnp.exp(jnp.where(mask, _nt(q2, kk), NEG) - per_head(lsev))
            dl = (p * (_nt(do2, vv) - per_head(delv))).astype(bf16)
            dq = jnp.dot(dl, kk, preferred_element_type=f32)
            dq_ref[rows, :] = jnp.where(first, dq[:DIL_BLK], dq[DIL_BLK:]) * ATT_SCALE
            dk_ref[krows, :] += _tn(dl, q2)
            dv_ref[krows, :] += _tn(p.astype(bf16), do2)

    grouped = pl.BlockSpec((SEQ, LANES), lambda p: (0, DIL_PAIRS * g + p))
    own = pl.BlockSpec((SEQ, LANES), lambda p: (0, p))
    shape = SDS((SEQ, DIL_OUT_W), f32)
    return pl.pallas_call(
        body, grid=(DIL_PAIRS,), in_specs=[grouped, grouped, _dil_v_spec(g)] + [own] * 3, out_specs=[own] * 3,
        out_shape=(shape, shape, shape), name=f"dil_bwd_{DILATIONS[g]}", compiler_params=_params(("parallel",)),
    )(q, k, v, do, lse, delta)


def _position():
    return lax.axis_index("x"), lax.axis_index("y"), lax.axis_index("c")


def _all_gather(block, name):
    def body(x_ref, out_ref, send_sems, recv_sems, local_sem):
        x, y, c = _position()
        me, sibling = (x, y, c), (x, y, 1 - c)
        chips = [(1 - x, y), (x, 1 - y), (1 - x, 1 - y)]

        def slot(px, py, pc):
            return out_ref.at[4 * px + 2 * py + pc]

        def copy(k, blk, to, src=None):
            return pltpu.make_async_remote_copy(
                src_ref=slot(*blk) if src is None else src, dst_ref=slot(*blk),
                send_sem=send_sems.at[k], recv_sem=recv_sems.at[k], device_id=to, device_id_type=MESH)

        mine = pltpu.make_async_copy(x_ref, slot(*me), local_sem)
        mine.start()
        first = [copy(0, me, sibling, src=x_ref)]
        first += [copy(1 + j, me, (*chip, c), src=x_ref) for j, chip in enumerate(chips)]
        for cp in first:
            cp.start()
        passed = [copy(4 + j, (*chip, c), sibling) for j, chip in enumerate(chips)]
        for j, chip in enumerate(chips):
            copy(1 + j, (*chip, c), me).wait_recv()
            passed[j].start()
        copy(0, sibling, me).wait_recv()
        for j, chip in enumerate(chips):
            copy(4 + j, (*chip, 1 - c), me).wait_recv()
        for cp in first + passed:
            cp.wait_send()
        mine.wait()

    return pl.pallas_call(
        body, out_shape=SDS((N_DEV,) + block.shape, block.dtype),
        in_specs=[pl.BlockSpec(memory_space=pl.ANY)], out_specs=pl.BlockSpec(memory_space=pl.ANY),
        scratch_shapes=[pltpu.SemaphoreType.DMA((7,)), pltpu.SemaphoreType.DMA((7,)), pltpu.SemaphoreType.DMA],
        name=name,
    )(block)


HBM_SPEC = pl.BlockSpec(memory_space=pltpu.HBM)
SEM_SPEC = pl.BlockSpec(memory_space=pltpu.SEMAPHORE)
SPLIT_COPY = pltpu.CompilerParams(has_side_effects=pltpu.SideEffectType.DATAFLOW_SIDE_EFFECTING)


def _in_hbm(t):
    return pltpu.with_memory_space_constraint(t, pltpu.HBM)


def _pair_copies(g_refs, land_refs, send_sems, recv_sems):
    x, y, c = _position()
    return [pltpu.make_async_remote_copy(
        src_ref=g.at[2 * k + (1 - c)], dst_ref=land.at[k], send_sem=send_sems.at[4 * a + k],
        recv_sem=recv_sems.at[4 * a + k], device_id=(x, y, 1 - c), device_id_type=MESH)
        for a, (g, land) in enumerate(zip(g_refs, land_refs, strict=True)) for k in range(4)]


def _chip_copies(t_refs, land_refs, send_sems, recv_sems):
    x, y, c = _position()
    chips = [(1 - x, y), (x, 1 - y), (1 - x, 1 - y)]
    return [pltpu.make_async_remote_copy(
        src_ref=t.at[2 * px + py], dst_ref=land.at[j], send_sem=send_sems.at[3 * a + j],
        recv_sem=recv_sems.at[3 * a + j], device_id=(px, py, c), device_id_type=MESH)
        for a, (t, land) in enumerate(zip(t_refs, land_refs, strict=True)) for j, (px, py) in enumerate(chips)]


_ROUNDS = {"pair": (_pair_copies, 4), "chip": (_chip_copies, 3)}


def _exchange_start(kind, ts, name):
    copies, slots = _ROUNDS[kind]
    n = len(ts)
    lands = [_in_hbm(lax.empty((slots,) + t.shape[1:], t.dtype)) for t in ts]

    def body(*refs):
        for cp in copies(refs[:n], refs[n:2 * n], refs[2 * n], refs[2 * n + 1]):
            cp.start()
        refs[-1][...] = jnp.zeros_like(refs[-1])

    sems = pltpu.SemaphoreType.DMA((slots * n,))
    res = pl.pallas_call(
        body, name=name, in_specs=[HBM_SPEC] * (2 * n),
        out_shape=(sems, sems, *[pltpu.HBM(t.shape, t.dtype) for t in (*ts, *lands)], SDS((8, LANES), f32)),
        out_specs=(SEM_SPEC, SEM_SPEC, *[HBM_SPEC] * (2 * n), pl.BlockSpec(memory_space=pltpu.VMEM)),
        input_output_aliases={i: 2 + i for i in range(2 * n)}, compiler_params=SPLIT_COPY,
    )(*[_in_hbm(t) for t in ts], *lands)
    return res[:-1], res[-1]


def _exchange_wait(kind, state, after, name):
    copies, _ = _ROUNDS[kind]
    send_sems, recv_sems, *arrays = state
    n = len(arrays) // 2

    def body(*refs):
        for cp in copies(refs[:n], refs[n:2 * n], refs[2 * n], refs[2 * n + 1]):
            cp.wait_send()
            cp.wait_recv()

    res = pl.pallas_call(
        body, name=name, in_specs=[HBM_SPEC] * (2 * n) + [SEM_SPEC, SEM_SPEC, pl.BlockSpec(memory_space=pl.ANY)],
        out_shape=[pltpu.HBM(t.shape, t.dtype) for t in arrays], out_specs=[HBM_SPEC] * (2 * n),
        input_output_aliases={i: i for i in range(2 * n)}, compiler_params=SPLIT_COPY,
    )(*arrays, send_sems, recv_sems, after)
    return res[:n], res[n:]


def _gather_copies(x_refs, out_refs, send_sems, recv_sems):
    x, y, c = _position()
    peers = [(x, y, 1 - c), (1 - x, y, c), (x, 1 - y, c), (1 - x, 1 - y, c)]
    sends, arrivals = [], []
    for a, (x_ref, out_ref) in enumerate(zip(x_refs, out_refs, strict=True)):
        for k, (px, py, pc) in enumerate(peers):
            sems = dict(send_sem=send_sems.at[4 * a + k], recv_sem=recv_sems.at[4 * a + k],
                        device_id=(px, py, pc), device_id_type=MESH)
            sends.append(pltpu.make_async_remote_copy(src_ref=x_ref, dst_ref=out_ref.at[4 * x + 2 * y + c], **sems))
            arrivals.append(pltpu.make_async_remote_copy(src_ref=x_ref, dst_ref=out_ref.at[4 * px + 2 * py + pc],
                                                         **sems))
    return sends, arrivals


def _gather_start(blocks, after, name):
    n = len(blocks)
    outs = [_in_hbm(lax.empty((N_DEV,) + b.shape, b.dtype)) for b in blocks]

    def body(*refs):
        sends, _ = _gather_copies(refs[:n], refs[n:2 * n], refs[2 * n + 1], refs[2 * n + 2])
        for cp in sends:
            cp.start()
        refs[-1][...] = jnp.zeros_like(refs[-1])

    sems = pltpu.SemaphoreType.DMA((4 * n,))
    res = pl.pallas_call(
        body, name=name, in_specs=[HBM_SPEC] * (2 * n) + [pl.BlockSpec(memory_space=pl.ANY)],
        out_shape=(sems, sems, *[pltpu.HBM(t.shape, t.dtype) for t in (*blocks, *outs)], SDS((8, LANES), f32)),
        out_specs=(SEM_SPEC, SEM_SPEC, *[HBM_SPEC] * (2 * n), pl.BlockSpec(memory_space=pltpu.VMEM)),
        input_output_aliases={i: 2 + i for i in range(2 * n)}, compiler_params=SPLIT_COPY,
    )(*[_in_hbm(b) for b in blocks], *outs, after)
    return res[:-1], res[-1]


def _gather_wait(state, after, name):
    send_sems, recv_sems, *arrays = state
    n = len(arrays) // 2

    def body(*refs):
        sends, arrivals = _gather_copies(refs[:n], refs[n:2 * n], refs[2 * n], refs[2 * n + 1])
        for cp in sends:
            cp.wait_send()
        for cp in arrivals:
            cp.wait_recv()

    res = pl.pallas_call(
        body, name=name, in_specs=[HBM_SPEC] * (2 * n) + [SEM_SPEC, SEM_SPEC, pl.BlockSpec(memory_space=pl.ANY)],
        out_shape=[pltpu.HBM(t.shape, t.dtype) for t in arrays], out_specs=[HBM_SPEC] * (2 * n),
        input_output_aliases={i: i for i in range(2 * n)}, compiler_params=SPLIT_COPY,
    )(*arrays, send_sems, recv_sems, after)
    return res[:n], res[n:]


def _gather_finish(partial, name):
    n = len(partial)

    def body(*refs):
        in_refs, out_refs = refs[:n], refs[n:2 * n]
        send_sems, recv_sems = refs[2 * n:]
        x, y, c = _position()
        chips = [(1 - x, y), (x, 1 - y), (1 - x, 1 - y)]
        copies = []
        for a in range(n):
            for j, (px, py) in enumerate(chips):
                cp = pltpu.make_async_remote_copy(
                    src_ref=in_refs[a].at[4 * px + 2 * py + c], dst_ref=out_refs[a].at[4 * px + 2 * py + c],
                    send_sem=send_sems.at[a, j], recv_sem=recv_sems.at[a, j], device_id=(x, y, 1 - c),
                    device_id_type=MESH)
                cp.start()
                copies.append(cp)
        for a in range(n):
            for j, (px, py) in enumerate(chips):
                pltpu.make_async_remote_copy(
                    src_ref=in_refs[a].at[4 * px + 2 * py + (1 - c)], dst_ref=out_refs[a].at[4 * px + 2 * py + (1 - c)],
                    send_sem=send_sems.at[a, j], recv_sem=recv_sems.at[a, j], device_id=(x, y, 1 - c),
                    device_id_type=MESH).wait_recv()
        for cp in copies:
            cp.wait_send()

    hbm = pl.BlockSpec(memory_space=pl.ANY)
    return pl.pallas_call(
        body, out_shape=[SDS(p.shape, p.dtype) for p in partial], in_specs=[hbm] * n, out_specs=[hbm] * n,
        input_output_aliases={a: a for a in range(n)},
        scratch_shapes=[pltpu.SemaphoreType.DMA((n, 3)), pltpu.SemaphoreType.DMA((n, 3))],
        name=name,
    )(*partial)


def _row_tile(rows):
    return 512 if rows % 512 == 0 and rows > 512 else rows


def _pair_add(g, r1, core, name):
    def body(c_ref, g_ref, r_ref, o_ref):
        o_ref[...] = (g_ref[...].astype(f32) + r_ref[...].astype(f32)).astype(o_ref.dtype)

    rows, cols = g.shape[1:]
    tile = _row_tile(rows)
    blk = (1, tile, cols)
    return pl.pallas_call(
        body, out_shape=SDS((4, rows, cols), g.dtype), name=name,
        grid_spec=pltpu.PrefetchScalarGridSpec(
            num_scalar_prefetch=1, grid=(4, rows // tile),
            in_specs=[pl.BlockSpec(blk, lambda k, i, c_ref: (2 * k + c_ref[0], i, 0)),
                      pl.BlockSpec(blk, lambda k, i, c_ref: (k, i, 0))],
            out_specs=pl.BlockSpec(blk, lambda k, i, c_ref: (k, i, 0))),
        compiler_params=_params(("parallel", "arbitrary")),
    )(core, g, r1)


def _chip_add(t, r2, chip, name, transposed=False):
    def body(c_ref, t_ref, r_ref, o_ref):
        s = ((t_ref[0].astype(f32) + r_ref[0].astype(f32)) + r_ref[1].astype(f32)) + r_ref[2].astype(f32)
        o_ref[...] = s.T if transposed else s

    rows, cols = t.shape[1:]
    tile = _row_tile(rows)
    out_spec = pl.BlockSpec((cols, tile), lambda i, c_ref: (0, i)) if transposed else pl.BlockSpec(
        (tile, cols), lambda i, c_ref: (i, 0))
    return pl.pallas_call(
        body, out_shape=SDS((cols, rows) if transposed else (rows, cols), f32), name=name,
        grid_spec=pltpu.PrefetchScalarGridSpec(
            num_scalar_prefetch=1, grid=(rows // tile,),
            in_specs=[pl.BlockSpec((1, tile, cols), lambda i, c_ref: (c_ref[0], i, 0)),
                      pl.BlockSpec((3, tile, cols), lambda i, c_ref: (0, i, 0))],
            out_specs=out_spec),
        compiler_params=_params(("arbitrary",)),
    )(chip, t, r2)


def _pad_to(t, axis, size):
    pads = [(0, 0)] * t.ndim
    pads[axis] = (0, size - t.shape[axis])
    return jnp.pad(t, pads)


_REF_COLS = {"qa": (0, FOX_W), "ka": (FOX_W, FOX_W), "va": (2 * FOX_W, FOX_W), "f": (3 * FOX_W, N_FOX_HEADS)}
_REF_COLS.update({n: (3 * FOX_W + N_FOX_HEADS + i * DIL_W, DIL_W) for i, n in enumerate(("qb", "kb", "vb"))})
_REF_COLS.update({n: (3 * FOX_W + N_FOX_HEADS + 3 * DIL_W + i * D, D) for i, n in enumerate(("ga", "gb"))})
_REF_ORDER = ("qa", "ka", "va", "f", "qb", "kb", "vb", "ga", "gb")


def _place_cols(sources, src_of, out_cols, name, row_block=512):
    arrays = [s[0] if isinstance(s, tuple) else s for s in sources]
    widths = [a.shape[-1] for a in arrays]
    rows = arrays[0].shape[-2]
    plan = []
    for t in range(out_cols // LANES):
        segs, c, end = [], t * LANES, (t + 1) * LANES
        while c < end:
            s = src_of(c)
            if s is None:
                c += 1
                continue
            n = 1
            while c + n < end and src_of(c + n) == (s[0], s[1] + n):
                n += 1
            segs.append((s[0], s[1], c - t * LANES, n))
            c += n
        plan.append(segs)

    def body(*refs):
        o_ref = refs[-1]
        for t, segs in enumerate(plan):
            acc = None
            for si, c0, o0, n in segs:
                a0 = c0 // LANES * LANES
                wide = min(2 * LANES, widths[si] - a0)
                win = refs[si][0, :, a0:a0 + wide] if isinstance(sources[si], tuple) else refs[si][:, a0:a0 + wide]
                r = lax.broadcasted_iota(jnp.int32, (wide, LANES), 0)
                c = lax.broadcasted_iota(jnp.int32, (wide, LANES), 1)
                pick = ((r - (c0 - a0) == c - o0) & (c >= o0) & (c < o0 + n)).astype(bf16)
                part = jnp.dot(win.astype(bf16), pick, preferred_element_type=f32)
                acc = part if acc is None else acc + part
            tile = jnp.zeros((row_block, LANES), f32) if acc is None else acc
            o_ref[:, t * LANES:(t + 1) * LANES] = tile.astype(o_ref.dtype)

    def spec(s):
        if isinstance(s, tuple):
            j = s[1]
            return pl.BlockSpec((1, row_block, s[0].shape[-1]), lambda i: (j, i, 0))
        return pl.BlockSpec((row_block, s.shape[-1]), lambda i: (i, 0))

    return pl.pallas_call(
        body, grid=(rows // row_block,), in_specs=[spec(s) for s in sources],
        out_specs=pl.BlockSpec((row_block, out_cols), lambda i: (i, 0)), out_shape=SDS((rows, out_cols), bf16),
        name=name, compiler_params=_params(("parallel",)),
    )(*arrays)


def _ref_piece(r):
    for name in _REF_ORDER:
        lo, width = _REF_COLS[name]
        if lo <= r < lo + width:
            return name, r - lo
    raise ValueError(r)


def _shard_pad_cols(pieces):
    names = [n for n in _REF_ORDER if n != "vb"]
    sources = [pieces[n] for n in names] + list(pieces["vb"])

    def src_of(c):
        j, i = divmod(c, W_IN_PAD)
        if i >= W_IN_SH:
            return None
        name, col = _ref_piece(j * W_IN_SH + i)
        if name == "vb":
            return len(names) + col // DIL_OUT_W, col % DIL_OUT_W
        return names.index(name), col

    return _place_cols(sources, src_of, N_DEV * W_IN_PAD, "place_dproj")


_SLABS = {"ga": C_GA, "gb": C_GB, "qb": C_QB, "kb": C_KB, "vb": C_VB, "qa": C_QA, "ka": C_KA, "va": C_VA, "f": C_F}


def _slab_w_in(stack):
    def src_of(c):
        for name, start in _SLABS.items():
            lo, width = _REF_COLS[name]
            if start <= c < start + width:
                return divmod(lo + c - start, W_IN_SH)
        return None

    return _place_cols([(stack, j) for j in range(N_DEV)], src_of, PROJ_W, "place_w_in")


def kernel(x, c, w_ada, b_ada, g_mix, w_in, b_fgate, w_br_a, w_br_b, w_out, g_ffn, w_ffn_gate, w_ffn_up, w_ffn_down, g_final, loss_target, m_w_ada, m_b_ada, m_g_mix, m_w_in, m_b_fgate, m_w_br_a, m_w_br_b, m_w_out, m_g_ffn, m_w_ffn_gate, m_w_ffn_up, m_w_ffn_down, m_g_final, v_w_ada, v_b_ada, v_g_mix, v_w_in, v_b_fgate, v_w_br_a, v_w_br_b, v_w_out, v_g_ffn, v_w_ffn_gate, v_w_ffn_up, v_w_ffn_down, v_g_final):
    px, py, pc = _position()
    dev = 4 * px + 2 * py + pc
    x2d, tgt = x[0], loss_target[0]

    c_all = _all_gather(c, "gather_c").reshape(N_DEV, D)
    ada_cols = w_ada.shape[2]
    b_shard = lax.dynamic_slice(b_ada, (0, dev * ada_cols), (1, ada_cols))
    mod_shard = _ada_fwd(c_all, w_ada[0], b_shard)
    mod_all = _all_gather(mod_shard, "gather_mod")
    modv = lax.dynamic_index_in_dim(mod_all, dev, axis=1, keepdims=False).reshape(6, D)
    h1 = _pre1(x2d, modv, g_mix)

    w_in_s = _all_gather(_pad_to(w_in[0], 1, W_IN_PAD).astype(bf16), "gather_w_in")
    gate_up = jnp.concatenate([_pad_to(w_ffn_gate[0], 1, FF_PAD), _pad_to(w_ffn_up[0], 1, FF_PAD)], axis=1)
    later = [w_br_a[0], w_br_b[0], w_out[0], gate_up, _pad_to(w_ffn_down[0], 0, FF_PAD)]
    later_state, later_token = _gather_start([t.astype(bf16) for t in later], w_in_s, "gather_rest_start")
    w_in_p = _slab_w_in(w_in_s)

    proj = _matmul(h1, w_in_p, name="mm_proj", tm=SEQ, tn=896, tk=D, after=later_token)
    b_pad = jnp.pad(b_fgate, ((0, 0), (0, LANES - N_FOX_HEADS)))
    q_aug, k_aug, va = _fox_prep(proj, _fox_gate_fwd(proj, b_pad))
    ya_h, max_a, sum_a = _fox_fwd(q_aug, k_aug, va)

    tables = _rope_tables()
    qb_r, kb_r = _rope_fwd(proj, tables)
    by_group = [_dil_fwd(qb_r, kb_r, proj, grp) for grp in range(N_GROUPS)]
    yb_h, lse_b = _dil_combine([o for o, _ in by_group], [l for _, l in by_group])

    mine, arrived = _gather_wait(later_state, yb_h, "gather_rest_wait")
    w_a_s, w_b_s, w_o_s, w_gu_s, w_d_s = [
        lax.dynamic_update_slice(stack, block[None], (dev, 0, 0))
        for stack, block in zip(_gather_finish(arrived, "gather_rest_finish"), mine, strict=True)]
    w_o = w_o_s.reshape(D, D)
    w_d = w_d_s.reshape(FF_HID, D)
    ya = _matmul_stack(ya_h, w_a_s, name="mm_br_a")
    yb = _matmul_stack(yb_h, w_b_s, name="mm_br_b")

    merged = _merge_fwd(ya, yb, proj)
    mix, x1, h2 = _post1(merged, w_o, x2d, modv, g_ffn)
    act, au = _ffn_in(h2, w_gu_s)

    dx2, dff, dg_final, dga_f, loss_lanes = _final(act, w_d, x1, tgt, modv, g_final.reshape(1, D))
    dau = _ffn_bwd_in(dff, w_d_s, au)

    core = pc.astype(jnp.int32).reshape(1)
    chip = (2 * px + py).astype(jnp.int32).reshape(1)

    def pair_done(state, after, tags, name):
        mine, theirs = _exchange_wait("pair", state, after, "pair_wait_" + name)
        sums = [_pair_add(g, r, core, "pair_add_" + t) for g, r, t in zip(mine, theirs, tags)]
        return _exchange_start("chip", sums, "chip_start_" + name)

    def from_chips(state, after, tags, name, transposed=None):
        sums, got = _exchange_wait("chip", state, after, "chip_wait_" + name)
        flips = transposed or [False] * len(tags)
        return [_chip_add(p, r, chip, "chip_add_" + t, f) for p, r, t, f in zip(sums, got, tags, flips)]

    g_gu = _matmul(h2, dau, ta=True, by_shard=True, out_dtype=bf16, name="mm_g_ffn_in", tm=D, tn=2 * FF_PAD, tk=SEQ)
    g_d = _matmul(act, dff, ta=True, out_dtype=bf16, name="mm_g_down", tm=FF_HID // 2, tn=512, tk=SEQ)
    ffn_tags = ["gu", "down"]
    ffn_pair, ffn_pair_token = _exchange_start("pair", [g_gu, g_d.reshape(N_DEV, FF_PAD, D)], "pair_start_ffn")

    dx1, dmix, dsh_f, dsc_f, dg_ffn, dga_m = _mid_bwd(dau, w_gu_s, ffn_pair_token, x1, dx2, mix, modv, g_ffn)
    ffn_state, ffn_token = pair_done(ffn_pair, dx1, ffn_tags, "ffn")
    dya, dyb, dga, dgb = _merge_bwd(dmix, w_o, ffn_token, ya, yb, proj)
    dya_h = _matmul_stack(dya, w_a_s, tb=True, name="mm_d_ya")
    dyb_h = _matmul_stack(dyb, w_b_s, tb=True, name="mm_d_yb")

    g_o = _matmul(merged, dmix, ta=True, out_dtype=bf16, name="mm_g_out", tm=D, tn=512, tk=SEQ)
    g_a = _matmul_stack(ya_h, dya, ta=True, out_dtype=bf16, name="mm_g_br_a")
    g_b = _matmul_stack(yb_h, dyb, ta=True, out_dtype=bf16, name="mm_g_br_b")
    rows_a, rows_b = FOX_W * W_BR_SH // D, DIL_OUT_W * W_BR_SH // D
    g_small = jnp.concatenate([g_a.reshape(N_DEV, rows_a, D), g_b.reshape(N_DEV, rows_b, D),
                               g_o.reshape(N_DEV, W_BR_SH, D)], axis=1)
    small_pair, small_pair_token = _exchange_start("pair", [g_small], "pair_start_small")

    dqa, dka, dva, dF = _fox_bwd(q_aug, k_aug, va, dya_h, ya_h, max_a, sum_a, small_pair_token)
    dF_row = jnp.pad(dF[:, :2, :].reshape(N_FOX_HEADS, SEQ), ((0, LANES - N_FOX_HEADS), (0, 0)))
    df, db_fgate = _fox_gate_bwd(dF_row, proj, b_pad)
    small_state, small_token = pair_done(small_pair, df, ["small"], "small")

    delta_b = _dil_delta(dyb_h, yb_h)
    dil_grads = [_dil_bwd(qb_r, kb_r, proj, dyb_h, lse_b, delta_b, grp) for grp in range(N_GROUPS)]
    dqb, dkb = _rope_bwd([t[0] for t in dil_grads], [t[1] for t in dil_grads], tables)

    dproj = _shard_pad_cols({"qa": dqa, "ka": dka, "va": dva, "f": df, "qb": dqb, "kb": dkb,
                             "vb": [t[2] for t in dil_grads], "ga": dga, "gb": dgb})
    g_in = _matmul(h1, dproj, ta=True, by_shard=True, out_dtype=bf16, name="mm_g_in", tm=D, tn=W_IN_PAD, tk=SEQ,
                   after=small_token)
    mix_tags = ["in"]
    mix_pair, mix_pair_token = _exchange_start("pair", [g_in], "pair_start_mixer")

    grad_x, dsh_m, dsc_m, dg_mix = _first_bwd(dproj, w_in_s, mix_pair_token, x2d, dx1, modv, g_mix)

    pad_lane = lambda t: jnp.pad(t, ((0, 0), (0, D - t.shape[1])))
    small = jnp.concatenate([dsh_m, dsc_m, dga_m, dsh_f, dsc_f, dga_f, dg_mix, dg_ffn, dg_final,
                             pad_lane(db_fgate), loss_lanes, jnp.zeros((SMALL_ROWS - 11, D), f32)], axis=0)
    small_all = _all_gather(small, "gather_small")
    mix_state, mix_token = pair_done(mix_pair, small_all, mix_tags, "mixer")

    small_sum, loss_row = _small_reduce(small_all, mix_token)
    dmod_all = small_all[:, :6, :].reshape(N_DEV, 6 * D)
    g_w_ada = _ada_bwd(c_all, lax.dynamic_slice(dmod_all, (0, dev * ada_cols), (N_DEV, ada_cols)))
    s_gu_t, s_d = from_chips(ffn_state, small_sum, ffn_tags, "ffn", [True, False])
    s_small, = from_chips(small_state, small_sum, ["small"], "small")

    loss = loss_row[0, 0]
    g = {
        "w_ada": g_w_ada[None], "b_ada": small_sum[0:6].reshape(1, 6 * D), "g_mix": small_sum[6:7],
        "b_fgate": small_sum[9:10, :N_FOX_HEADS], "g_ffn": small_sum[7:8], "w_ffn_gate": s_gu_t[:W_FF_SH],
        "w_ffn_up": s_gu_t[FF_PAD:FF_PAD + W_FF_SH], "w_ffn_down": s_d[None, :W_FF_SH],
        "g_final": small_sum[8], "w_br_a": s_small[:rows_a].reshape(1, FOX_W, W_BR_SH),
        "w_br_b": s_small[rows_a:rows_a + rows_b].reshape(1, DIL_OUT_W, W_BR_SH), "w_out": s_small[None, rows_a + rows_b:],
    }
    w = {"w_ada": w_ada, "b_ada": b_ada, "g_mix": g_mix, "w_in": w_in, "b_fgate": b_fgate, "w_br_a": w_br_a,
         "w_br_b": w_br_b, "w_out": w_out, "g_ffn": g_ffn, "w_ffn_gate": w_ffn_gate, "w_ffn_up": w_ffn_up,
         "w_ffn_down": w_ffn_down, "g_final": g_final}
    m = {"w_ada": m_w_ada, "b_ada": m_b_ada, "g_mix": m_g_mix, "w_in": m_w_in, "b_fgate": m_b_fgate,
         "w_br_a": m_w_br_a, "w_br_b": m_w_br_b, "w_out": m_w_out, "g_ffn": m_g_ffn, "w_ffn_gate": m_w_ffn_gate,
         "w_ffn_up": m_w_ffn_up, "w_ffn_down": m_w_ffn_down, "g_final": m_g_final}
    v = {"w_ada": v_w_ada, "b_ada": v_b_ada, "g_mix": v_g_mix, "w_in": v_w_in, "b_fgate": v_b_fgate,
         "w_br_a": v_w_br_a, "w_br_b": v_w_br_b, "w_out": v_w_out, "g_ffn": v_g_ffn, "w_ffn_gate": v_w_ffn_gate,
         "w_ffn_up": v_w_ffn_up, "w_ffn_down": v_w_ffn_down, "g_final": v_g_final}
    names = list(w)
    delta, new_m, new_v = {}, {}, {}

    transposed = ("w_in", "w_ffn_gate", "w_ffn_up")

    def update(n):
        shape = w[n].shape
        if n in transposed:
            g_t = g[n]
            dl, mn, vn = _adamw(w[n][0].T, g_t, m[n][0].T, v[n][0].T, "adamw_" + n)
            g[n], delta[n], new_m[n], new_v[n] = g_t.T[None], dl.T[None], mn.T[None], vn.T[None]
            return
        two_d = (lambda t: t.reshape(shape[-2:])) if len(shape) == 3 else (lambda t: t)
        dl, mn, vn = _adamw(two_d(w[n]), two_d(g[n]), two_d(m[n]), two_d(v[n]), "adamw_" + n)
        delta[n], new_m[n], new_v[n] = dl.reshape(shape), mn.reshape(shape), vn.reshape(shape)

    for n in list(g):
        update(n)
    done = sum(delta[n].reshape(-1)[:N_FOX_HEADS] for n in g)
    s_in_t, = from_chips(mix_state, done, mix_tags, "mixer", [True])
    g["w_in"] = s_in_t[:W_IN_SH]
    update("w_in")

    return (loss, grad_x[None], *[g[n] for n in names], *[delta[n] for n in names],
            *[new_m[n] for n in names], *[new_v[n] for n in names])
```

```python
import functools

import jax
import jax.numpy as jnp
import numpy as np
from jax import lax
from jax.experimental import pallas as pl
from jax.experimental.pallas import tpu as pltpu

f32 = jnp.float32
bf16 = jnp.bfloat16
SDS = jax.ShapeDtypeStruct
MESH = pl.DeviceIdType.MESH

N_DEV = 8
D = 1024
SEQ = 2048
HEAD_DIM = 64
N_FOX_HEADS = 8
FOX_W = 512
DIL_W = 768
DIL_OUT_W = 256
ROT_DIM = 16
ROPE_THETA = 500000.0
D_FF = 2816
IN_COLS = 5896
EPS = 1e-6
NEG = -1e30
ATT_SCALE = HEAD_DIM ** -0.5

ADAM_LR = 0.001
ADAM_B1 = 0.9
ADAM_B2 = 0.999
ADAM_EPS = 1e-08
ADAM_WD = 0.01
ADAM_STEP = 10

C_GA, C_GB, C_QB, C_KB, C_VB, C_QA, C_KA, C_VA, C_F = 0, 1024, 2304, 3072, 3840, 4608, 5120, 5632, 6144
PROJ_W = 6272
LANES = 128
VMEM_LIMIT = 52 * 1024 * 1024

W_IN_SH, W_IN_PAD = IN_COLS // N_DEV, 768
W_BR_SH = D // N_DEV
W_FF_SH, FF_PAD = D_FF // N_DEV, 384
FF_HID = N_DEV * FF_PAD
SMALL_ROWS = 16


def _params(sem=None):
    if sem is None:
        return pltpu.CompilerParams(vmem_limit_bytes=VMEM_LIMIT)
    return pltpu.CompilerParams(dimension_semantics=sem, vmem_limit_bytes=VMEM_LIMIT)


def _rowwise(fn, name, tiled, vecs, outs, reds=(), tile=256):
    nt, nv, no = len(tiled), len(vecs), len(outs)
    rows = tiled[0][0].shape[0]
    assert rows % tile == 0

    def body(*refs):
        tin = [r[...] for r in refs[:nt]]
        vin = [r[...] for r in refs[nt:nt + nv]]
        orefs = refs[nt + nv:nt + nv + no]
        rrefs = refs[nt + nv + no:]
        touts, routs = fn(tin, vin)
        for r, t in zip(orefs, touts, strict=True):
            r[...] = t.astype(r.dtype)
        if rrefs:
            @pl.when(pl.program_id(0) == 0)
            def _():
                for r in rrefs:
                    r[...] = jnp.zeros_like(r)
            for r, t in zip(rrefs, routs, strict=True):
                r[...] += t

    def col_map(cb):
        return lambda i: (i, cb)

    def whole_map(nd):
        return lambda i: (0,) * nd

    in_specs = [pl.BlockSpec((tile, w), col_map(cb)) for (_, w, cb) in tiled]
    in_specs += [pl.BlockSpec(v.shape, whole_map(v.ndim)) for v in vecs]
    out_specs = [pl.BlockSpec((tile, w), lambda i: (i, 0)) for (w, _) in outs]
    out_specs += [pl.BlockSpec((1, w), lambda i: (0, 0)) for w in reds]
    out_shape = [SDS((rows, w), dt) for (w, dt) in outs] + [SDS((1, w), f32) for w in reds]
    res = pl.pallas_call(
        body, grid=(rows // tile,), in_specs=in_specs, out_specs=out_specs, out_shape=out_shape, name=name,
        compiler_params=_params(("arbitrary",)),
    )(*[t[0] for t in tiled], *vecs)
    return res


def _matmul(a, b, *, ta=False, tb=False, out_dtype=f32, name, tm, tn, tk, by_shard=False, after=None):
    m, k = (a.shape[1], a.shape[0]) if ta else a.shape
    if by_shard and not ta:
        n, kb = (b.shape[1], N_DEV * b.shape[2]) if tb else (N_DEV * b.shape[2], b.shape[1])
        assert (tk if tb else tn) == b.shape[2]
    else:
        n, kb = (b.shape[0], b.shape[1]) if tb else (b.shape[1], b.shape[0])
    assert kb == k and m % tm == 0 and n % tn == 0 and k % tk == 0
    nk = k // tk
    dims = (((0 if ta else 1,), (1 if tb else 0,)), ((), ()))
    b_stacked = by_shard and not ta
    o_stacked = by_shard and ta

    def body(a_ref, b_ref, *rest):
        o_ref, *acc = rest[1:] if after is not None else rest
        bv = b_ref[0] if b_stacked else b_ref[...]
        p = lax.dot_general(a_ref[...].astype(bf16), bv.astype(bf16), dims, preferred_element_type=f32)

        def put(val):
            if o_stacked:
                o_ref[0] = val.astype(o_ref.dtype)
            else:
                o_ref[...] = val.astype(o_ref.dtype)

        if nk == 1:
            put(p)
        else:
            acc_ref, = acc
            kk = pl.program_id(2)

            @pl.when(kk == 0)
            def _():
                acc_ref[...] = p

            @pl.when(kk > 0)
            def _():
                acc_ref[...] += p

            @pl.when(kk == nk - 1)
            def _():
                put(acc_ref[...])

    a_spec = pl.BlockSpec((tk, tm), lambda i, j, kk: (kk, i)) if ta else pl.BlockSpec((tm, tk), lambda i, j, kk: (i, kk))
    if b_stacked and tb:
        b_spec = pl.BlockSpec((1, tn, tk), lambda i, j, kk: (kk, j, 0))
    elif b_stacked:
        b_spec = pl.BlockSpec((1, tk, tn), lambda i, j, kk: (j, kk, 0))
    elif tb:
        b_spec = pl.BlockSpec((tn, tk), lambda i, j, kk: (j, kk))
    else:
        b_spec = pl.BlockSpec((tk, tn), lambda i, j, kk: (kk, j))
    if o_stacked:
        assert tn == n // N_DEV
        out_spec = pl.BlockSpec((1, tm, tn), lambda i, j, kk: (j, i, 0))
        out_shape = SDS((N_DEV, m, tn), out_dtype)
    else:
        out_spec = pl.BlockSpec((tm, tn), lambda i, j, kk: (i, j))
        out_shape = SDS((m, n), out_dtype)
    extra_specs, extra = ([pl.BlockSpec(memory_space=pl.ANY)], [after]) if after is not None else ([], [])
    return pl.pallas_call(
        body, grid=(m // tm, n // tn, nk), in_specs=[a_spec, b_spec] + extra_specs, out_specs=out_spec,
        out_shape=out_shape, name=name, scratch_shapes=[pltpu.VMEM((tm, tn), f32)] if nk > 1 else [],
        compiler_params=_params(("parallel", "parallel", "arbitrary")),
    )(a, b, *extra)


def _matmul_stack(a, b, *, ta=False, tb=False, out_dtype=f32, name):
    def lanes(ref):
        return jnp.concatenate([ref[j] for j in range(N_DEV)], axis=1).astype(bf16)

    if ta:
        w = b.shape[1] // N_DEV

        def body(a_ref, b_ref, o_ref):
            p = _tn(a_ref[...].astype(bf16), b_ref[...].astype(bf16))
            for j in range(N_DEV):
                o_ref[j] = p[:, j * w:(j + 1) * w].astype(o_ref.dtype)

        return pl.pallas_call(body, out_shape=SDS((N_DEV, a.shape[1], w), out_dtype), name=name,
                              compiler_params=_params())(a, b)

    m, half = a.shape[0], a.shape[0] // 2
    n = b.shape[1] if tb else N_DEV * b.shape[2]

    def body(a_ref, b_ref, o_ref):
        av = a_ref[...].astype(bf16)
        o_ref[...] = (_nt(av, lanes(b_ref)) if tb else jnp.dot(av, lanes(b_ref), preferred_element_type=f32)
                      ).astype(o_ref.dtype)

    return pl.pallas_call(
        body, grid=(2,), in_specs=[pl.BlockSpec((half, a.shape[1]), lambda i: (i, 0)),
                                   pl.BlockSpec(b.shape, lambda i: (0, 0, 0))],
        out_specs=pl.BlockSpec((half, n), lambda i: (i, 0)), out_shape=SDS((m, n), out_dtype), name=name,
        compiler_params=_params(("parallel",)),
    )(a, b)


def _matmul_rows(form, a, b, after, fn, tiled, vecs, outs, reds, *, name, tm=512):
    norm = lambda ts: [t if isinstance(t, tuple) else (t, t.shape[1], 0) for t in ts]
    make, sources = a if isinstance(a, tuple) else (None, [a])
    sources, tiled = norm(sources), norm(tiled)
    m, k = sources[0][0].shape[0], (b.shape[0] if form == "nn" else b.shape[-1] * (N_DEV if form == "nt_stack" else 1))
    assert m % tm == 0
    ns, nt, nv, no = len(sources), len(tiled), len(vecs), len(outs)

    def body(*refs):
        src_refs, b_ref, refs = refs[:ns], refs[ns], refs[ns + 2:]
        if make is None:
            lhs = lambda lo, hi: src_refs[0][:, lo:hi]
        else:
            made = make([r[...] for r in src_refs]).astype(bf16)
            lhs = lambda lo, hi: made[:, lo:hi]
        if form == "nt_stack":
            w = b.shape[2]
            acc = _nt(lhs(0, w), b_ref[0])
            for j in range(1, N_DEV):
                acc = acc + _nt(lhs(j * w, (j + 1) * w), b_ref[j])
        elif form == "nt":
            acc = _nt(lhs(0, k), b_ref[...])
        else:
            acc = jnp.dot(lhs(0, k), b_ref[...], preferred_element_type=f32)
        if make is not None:
            refs[nt + nv][...] = made
            refs = refs[:nt + nv] + refs[nt + nv + 1:]
        orefs, rrefs = refs[nt + nv:nt + nv + no], refs[nt + nv + no:]
        touts, routs = fn([acc] + [r[...] for r in refs[:nt]], [r[...] for r in refs[nt:nt + nv]])
        for r, t in zip(orefs, touts, strict=True):
            r[...] = t.astype(r.dtype)

        @pl.when(pl.program_id(0) == 0)
        def _():
            for r in rrefs:
                r[...] = jnp.zeros_like(r)
        for r, t in zip(rrefs, routs, strict=True):
            r[...] += t

    def whole_map(nd):
        return lambda i: (0,) * nd

    def rows(width, cb=0):
        return pl.BlockSpec((tm, width), lambda i: (i, cb))

    made_out = [(k, bf16)] if make is not None else []
    return pl.pallas_call(
        body, grid=(m // tm,),
        in_specs=[rows(width, cb) for _, width, cb in sources]
        + [pl.BlockSpec(b.shape, whole_map(b.ndim), pipeline_mode=pl.Buffered(1)), pl.BlockSpec(memory_space=pl.ANY)]
        + [rows(width, cb) for _, width, cb in tiled] + [pl.BlockSpec(v.shape, whole_map(v.ndim)) for v in vecs],
        out_specs=[rows(width) for width, _ in made_out + list(outs)]
        + [pl.BlockSpec((1, width), lambda i: (0, 0)) for width in reds],
        out_shape=[SDS((m, width), dt) for width, dt in made_out + list(outs)]
        + [SDS((1, width), f32) for width in reds], name=name,
        compiler_params=_params(("arbitrary",)),
    )(*[t[0] for t in sources], b, after, *[t[0] for t in tiled], *vecs)


def _rms(x):
    r = lax.rsqrt(jnp.mean(x * x, axis=-1, keepdims=True) + EPS)
    return r, x * r


def _rms_bwd(r, xn, dxn):
    return r * (dxn - xn * jnp.mean(dxn * xn, axis=-1, keepdims=True))


def _colsum(t):
    return jnp.sum(t, axis=0, keepdims=True)


def _sigmoid(x):
    return 0.5 * jnp.tanh(0.5 * x) + 0.5


def _modulated_norm(x, g, shift, scale):
    _, xn = _rms(x)
    return (xn * g) * (1.0 + scale) + shift


def _pre1(x, modv, g_mix):
    def fn(t, v):
        (xt,), (mv, g) = t, v
        return [_modulated_norm(xt, g, mv[0:1], mv[1:2])], []
    return _rowwise(fn, "pre1", [(x, D, 0)], [modv, g_mix], [(D, bf16)])[0]


def _post1(ya, yb, proj, w_o, x, modv, g_ffn):
    def merge(t):
        ya_t, yb_t, ga, gb = t
        return _sigmoid(ga) * ya_t + _sigmoid(gb) * yb_t

    def fn(t, v):
        (mt, xt), (mv, g) = t, v
        x1 = xt + mv[2:3] * mt
        return [mt, x1, _modulated_norm(x1, g, mv[3:4], mv[4:5])], []
    return _matmul_rows("nn", (merge, [ya, yb, (proj, D, C_GA // D), (proj, D, C_GB // D)]), w_o, x, fn, [x],
                        [modv, g_ffn], [(D, f32), (D, f32), (D, bf16)], [], name="post1")


def _ffn_in(h, w_stack):
    def body(h_ref, w_ref, act_ref, au_ref):
        p = jnp.dot(h_ref[...], w_ref[0], preferred_element_type=f32)
        a, u = p[:, :FF_PAD], p[:, FF_PAD:]
        act_ref[...] = (a * _sigmoid(a) * u).astype(act_ref.dtype)
        au_ref[...] = p.astype(au_ref.dtype)

    return pl.pallas_call(
        body, grid=(N_DEV,),
        in_specs=[pl.BlockSpec((SEQ, D), lambda j: (0, 0)), pl.BlockSpec((1, D, 2 * FF_PAD), lambda j: (j, 0, 0))],
        out_specs=[pl.BlockSpec((SEQ, FF_PAD), lambda j: (0, j)), pl.BlockSpec((SEQ, 2 * FF_PAD), lambda j: (0, j))],
        out_shape=(SDS((SEQ, FF_HID), bf16), SDS((SEQ, 2 * FF_HID), bf16)), name="ffn_in",
        compiler_params=_params(("parallel",)),
    )(h, w_stack)


def _ffn_bwd_in(dff, w_down_stack, au):
    def body(d_ref, w_ref, au_ref, o_ref):
        dact = _nt(d_ref[...], w_ref[0])
        p = au_ref[...].astype(f32)
        a, u = p[:, :FF_PAD], p[:, FF_PAD:]
        sg = _sigmoid(a)
        o_ref[...] = jnp.concatenate([dact * u * (sg * (1.0 + a * (1.0 - sg))), dact * (a * sg)],
                                     axis=1).astype(o_ref.dtype)

    return pl.pallas_call(
        body, grid=(N_DEV,),
        in_specs=[pl.BlockSpec((SEQ, D), lambda j: (0, 0)), pl.BlockSpec((1, FF_PAD, D), lambda j: (j, 0, 0)),
                  pl.BlockSpec((SEQ, 2 * FF_PAD), lambda j: (0, j))],
        out_specs=pl.BlockSpec((SEQ, 2 * FF_PAD), lambda j: (0, j)),
        out_shape=SDS((SEQ, 2 * FF_HID), bf16), name="ffn_bwd_in", compiler_params=_params(("parallel",)),
    )(dff, w_down_stack, au)


def _final(act, w_down, x1, target, modv, g_final):
    def fn(t, v):
        (fft, x1t, tgt), (mv, g) = t, v
        x2 = x1t + mv[5:6] * fft
        r, xn = _rms(x2)
        err = xn * g - tgt
        dy = err * (1.0 / D)
        dx2 = _rms_bwd(r, xn, dy * g)
        return [dx2, dx2 * mv[5:6]], [_colsum(dy * xn), _colsum(dx2 * fft), _colsum(err * err) * (0.5 / D)]
    return _matmul_rows("nn", act, w_down, x1, fn, [x1, target], [modv, g_final], [(D, f32), (D, bf16)], [D, D, D],
                        name="final")


def _mid_bwd(dau, w_stack, after, x1, dx2, mix, modv, g_ffn):
    def fn(t, v):
        (dh, x1t, dx2t, mt), (mv, g) = t, v
        r, xn = _rms(x1t)
        dn = dh * (1.0 + mv[4:5])
        dx1 = dx2t + _rms_bwd(r, xn, dn * g)
        return [dx1, dx1 * mv[2:3]], [_colsum(dh), _colsum(dh * (xn * g)), _colsum(dn * xn), _colsum(dx1 * mt)]
    return _matmul_rows("nt_stack", dau, w_stack, after, fn, [x1, dx2, mix], [modv, g_ffn], [(D, f32), (D, bf16)],
                        [D, D, D, D], name="mid_bwd")


def _first_bwd(dproj, w_stack, after, x, dx1, modv, g_mix):
    def fn(t, v):
        (dh, xt, dx1t), (mv, g) = t, v
        r, xn = _rms(xt)
        dn = dh * (1.0 + mv[1:2])
        return [dx1t + _rms_bwd(r, xn, dn * g)], [_colsum(dh), _colsum(dh * (xn * g)), _colsum(dn * xn)]
    return _matmul_rows("nt_stack", dproj, w_stack, after, fn, [x, dx1], [modv, g_mix], [(D, f32)], [D, D, D],
                        name="first_bwd")


def _merge_bwd(dmix, w_o, after, ya, yb, proj):
    def fn(t, v):
        dm, ya_t, yb_t, ga, gb = t
        sa, sb = _sigmoid(ga), _sigmoid(gb)
        return [dm * sa, dm * sb, dm * ya_t * (sa * (1.0 - sa)), dm * yb_t * (sb * (1.0 - sb))], []
    return _matmul_rows("nt", dmix, w_o, after, fn, [ya, yb, (proj, D, C_GA // D), (proj, D, C_GB // D)], [],
                        [(D, bf16), (D, bf16), (D, bf16), (D, bf16)], [], name="merge_bwd")


def _rope_tables():
    half = ROT_DIM // 2
    pos = np.arange(SEQ, dtype=np.float32)
    inv_freq = np.float32(ROPE_THETA) ** (-np.arange(0, ROT_DIM, 2, dtype=np.float32) / np.float32(ROT_DIM))
    ang = pos[:, None] * inv_freq[None, :].astype(np.float32)
    cos, sin = np.cos(ang).astype(np.float32), np.sin(ang).astype(np.float32)
    pad = np.zeros((SEQ, HEAD_DIM - ROT_DIM), np.float32)
    zero = np.zeros((SEQ, half), np.float32)
    c_head = np.concatenate([cos, cos, pad + 1.0], axis=1)
    lo_head = np.concatenate([-sin, zero, pad], axis=1)
    hi_head = np.concatenate([zero, sin, pad], axis=1)
    return tuple(jnp.asarray(np.concatenate([t, t], axis=1)) for t in (c_head, lo_head, hi_head))


def _over_heads(tables):
    return [jnp.tile(t, (1, DIL_W // LANES)) for t in tables]


def _rope_fwd(proj, tables):
    half = ROT_DIM // 2

    def fn(t, v):
        q, k = t[:2]
        c, lo, hi = _over_heads(t[2:])
        rot = lambda z: z * c + pltpu.roll(z, DIL_W - half, 1) * lo + pltpu.roll(z, half, 1) * hi
        return [rot(q) * ATT_SCALE, rot(k)], []
    return _rowwise(fn, "rope_fwd", [(proj, DIL_W, C_QB // DIL_W), (proj, DIL_W, C_KB // DIL_W)]
                    + [(tb, LANES, 0) for tb in tables], [], [(DIL_W, f32)] * 2)


def _rope_bwd(dqs, dks, tables):
    half = ROT_DIM // 2

    def fn(t, v):
        dq_t, dk_t = jnp.concatenate(t[:N_GROUPS], axis=1), jnp.concatenate(t[N_GROUPS:2 * N_GROUPS], axis=1)
        c, lo, hi = _over_heads(t[2 * N_GROUPS:])
        rot_t = lambda z: z * c + pltpu.roll(z * lo, half, 1) + pltpu.roll(z * hi, DIL_W - half, 1)
        return [rot_t(dq_t), rot_t(dk_t)], []
    return _rowwise(fn, "rope_bwd", [(a, DIL_OUT_W, 0) for a in (*dqs, *dks)] + [(tb, LANES, 0) for tb in tables],
                    [], [(DIL_W, bf16), (DIL_W, bf16)])


def _head_bcast_sum(d):
    lane = lax.broadcasted_iota(jnp.int32, d.shape, 1)
    out = jnp.zeros_like(d)
    for h in range(d.shape[1] // HEAD_DIM):
        sel = (lane >= h * HEAD_DIM) & (lane < (h + 1) * HEAD_DIM)
        out = jnp.where(sel, jnp.sum(jnp.where(sel, d, 0.0), axis=1, keepdims=True), out)
    return out


def _dil_combine(outs, lses):
    def fn(t, v):
        o0, o1, o2, l0, l1, l2 = t
        m = jnp.maximum(jnp.maximum(l0, l1), l2)
        w0, w1, w2 = jnp.exp(l0 - m), jnp.exp(l1 - m), jnp.exp(l2 - m)
        tot = w0 + w1 + w2
        return [(w0 * o0 + w1 * o1 + w2 * o2) / tot, m + jnp.log(tot)], []
    w = DIL_OUT_W
    return _rowwise(fn, "dil_combine", [(t, w, 0) for t in (*outs, *lses)], [], [(w, f32), (w, f32)])


def _dil_delta(dyb_h, yb_h):
    def fn(t, v):
        return [_head_bcast_sum(t[0] * t[1])], []
    return _rowwise(fn, "dil_delta", [(dyb_h, DIL_OUT_W, 0), (yb_h, DIL_OUT_W, 0)], [], [(DIL_OUT_W, f32)])[0]


def _adamw_math(wt, gt, mt, vt):
    mn = ADAM_B1 * mt + (1.0 - ADAM_B1) * gt
    vn = ADAM_B2 * vt + (1.0 - ADAM_B2) * (gt * gt)
    m_hat = mn / (1.0 - ADAM_B1 ** ADAM_STEP)
    v_hat = vn / (1.0 - ADAM_B2 ** ADAM_STEP)
    return -ADAM_LR * (m_hat / (jnp.sqrt(v_hat) + ADAM_EPS) + ADAM_WD * wt), mn, vn


def _adamw(w, g, m, v, name):
    shape = w.shape
    if w.ndim == 1:
        w, g, m, v = (t.reshape(1, -1) for t in (w, g, m, v))
    rows, cols = w.shape
    if rows % 8 and rows > 8:
        return _adamw_by_cols(w, g, m, v, name)
    tile = 256 if rows % 256 == 0 and rows > 512 else rows

    def fn(t, _):
        return list(_adamw_math(*t)), []
    delta, mn, vn = _rowwise(fn, name, [(w, cols, 0), (g, cols, 0), (m, cols, 0), (v, cols, 0)], [],
                             [(cols, f32)] * 3, tile=tile)
    return delta.reshape(shape), mn.reshape(shape), vn.reshape(shape)


def _adamw_by_cols(w, g, m, v, name, tile=256):
    rows, cols = w.shape

    def body(w_ref, g_ref, m_ref, v_ref, d_ref, mn_ref, vn_ref):
        d_ref[...], mn_ref[...], vn_ref[...] = _adamw_math(w_ref[...], g_ref[...], m_ref[...], v_ref[...])

    spec = pl.BlockSpec((rows, tile), lambda j: (0, j))
    return pl.pallas_call(body, grid=(cols // tile,), in_specs=[spec] * 4, out_specs=[spec] * 3,
                          out_shape=[SDS((rows, cols), f32)] * 3, name=name,
                          compiler_params=_params(("parallel",)))(w, g, m, v)


def _ada_fwd(c_all, w_shard, b_shard):
    def body(c_ref, w_ref, b_ref, o_ref):
        cv = c_ref[...]
        sc = (cv * _sigmoid(cv)).astype(bf16)
        o_ref[...] = jnp.dot(sc, w_ref[...].astype(bf16), preferred_element_type=f32) + b_ref[...]
    return pl.pallas_call(body, out_shape=SDS((N_DEV, w_shard.shape[1]), f32), name="ada_fwd",
                          compiler_params=_params())(c_all, w_shard, b_shard)


def _ada_bwd(c_all, dmod_cols):
    def body(c_ref, d_ref, o_ref):
        cv = c_ref[...]
        sc = cv * _sigmoid(cv)
        o_ref[...] = lax.dot_general(sc, d_ref[...], (((0,), (0,)), ((), ())), precision=lax.Precision.HIGHEST,
                                     preferred_element_type=f32)
    return pl.pallas_call(body, out_shape=SDS((D, dmod_cols.shape[1]), f32), name="ada_bwd",
                          compiler_params=_params())(c_all, dmod_cols)


def _small_reduce(gathered, after):
    def body(g_ref, after_ref, o_ref, loss_ref):
        acc = g_ref[0]
        for d in range(1, N_DEV):
            acc = acc + g_ref[d]
        o_ref[...] = acc
        loss_ref[...] = jnp.zeros((1, LANES), f32) + jnp.sum(acc[10:11, :])
    return pl.pallas_call(body, out_shape=(SDS((SMALL_ROWS, D), f32), SDS((1, LANES), f32)), name="small_reduce",
                          in_specs=[pl.BlockSpec(memory_space=pltpu.VMEM), pl.BlockSpec(memory_space=pl.ANY)],
                          compiler_params=_params())(gathered, after)


FOX_BLK = 512
CUM_BLK = 128


def _fold_lanes(t, op):
    out = t[:, :LANES]
    for j in range(1, t.shape[1] // LANES):
        out = op(out, t[:, j * LANES:(j + 1) * LANES])
    return out


def _fox_gate_fwd(proj, b_pad):
    nblk = SEQ // CUM_BLK

    def body(f_ref, b_ref, col_ref):
        r = lax.broadcasted_iota(jnp.int32, (CUM_BLK, CUM_BLK), 0)
        c = lax.broadcasted_iota(jnp.int32, (CUM_BLK, CUM_BLK), 1)
        tri = (r >= c).astype(f32)
        carry = jnp.zeros((1, LANES), f32)
        for blk in range(nblk):
            z = f_ref[blk * CUM_BLK:(blk + 1) * CUM_BLK, :] + b_ref[...]
            logf = jnp.minimum(z, 0.0) - jnp.log1p(jnp.exp(-jnp.abs(z)))
            cs = jnp.dot(tri, logf, precision=lax.Precision.HIGHEST, preferred_element_type=f32) + carry
            col_ref[blk * CUM_BLK:(blk + 1) * CUM_BLK, :] = cs
            carry = cs[CUM_BLK - 1:CUM_BLK, :]

    return pl.pallas_call(
        body, grid=(1,), in_specs=[pl.BlockSpec((SEQ, LANES), lambda i: (0, C_F // LANES)),
                                   pl.BlockSpec((1, LANES), lambda i: (0, 0))],
        out_specs=pl.BlockSpec((SEQ, LANES), lambda i: (0, 0)),
        out_shape=SDS((SEQ, LANES), f32), name="fox_gate_fwd",
        compiler_params=_params(("arbitrary",)),
    )(proj, b_pad)


def _fox_gate_bwd(dF_row, proj, b_pad):
    nblk = SEQ // CUM_BLK

    def body(d_ref, f_ref, b_ref, df_ref, db_ref, col_ref):
        r = lax.broadcasted_iota(jnp.int32, (CUM_BLK, CUM_BLK), 0)
        c = lax.broadcasted_iota(jnp.int32, (CUM_BLK, CUM_BLK), 1)
        tri = (r <= c).astype(f32)
        lane = lax.broadcasted_iota(jnp.int32, (CUM_BLK, LANES), 1)
        col_ref[...] = d_ref[...].T
        carry = jnp.zeros((1, LANES), f32)
        total = jnp.zeros((1, LANES), f32)
        for blk in reversed(range(nblk)):
            rows = slice(blk * CUM_BLK, (blk + 1) * CUM_BLK)
            cs = jnp.dot(tri, col_ref[rows, :], precision=lax.Precision.HIGHEST, preferred_element_type=f32) + carry
            carry = cs[0:1, :]
            z = f_ref[rows, :] + b_ref[...]
            df = jnp.where(lane < N_FOX_HEADS, cs * _sigmoid(-z), 0.0)
            df_ref[rows, :] = df.astype(df_ref.dtype)
            total = total + _colsum(df)
        db_ref[...] = total

    return pl.pallas_call(
        body, grid=(1,), in_specs=[pl.BlockSpec((LANES, SEQ), lambda i: (0, 0)),
                                   pl.BlockSpec((SEQ, LANES), lambda i: (0, C_F // LANES)),
                                   pl.BlockSpec((1, LANES), lambda i: (0, 0))],
        out_specs=[pl.BlockSpec((SEQ, LANES), lambda i: (0, 0)), pl.BlockSpec((1, LANES), lambda i: (0, 0))],
        out_shape=(SDS((SEQ, LANES), bf16), SDS((1, LANES), f32)), name="fox_gate_bwd",
        scratch_shapes=[pltpu.VMEM((SEQ, LANES), f32)],
        compiler_params=_params(("arbitrary",)),
    )(dF_row, proj, b_pad)


def _nt(a, b):
    return lax.dot_general(a, b, (((1,), (1,)), ((), ())), preferred_element_type=f32)


def _tn(a, b):
    return lax.dot_general(a, b, (((0,), (0,)), ((), ())), preferred_element_type=f32)


def _fox_prep(proj, f_col):
    def fn(t, v):
        q, k, vv, fc = t
        lane = lax.broadcasted_iota(jnp.int32, (q.shape[0], LANES), 1)
        qs, ks = [], []
        for h in range(N_FOX_HEADS):
            pair, pos = divmod(h, 2)
            own = (lane >= pos * HEAD_DIM) & (lane < (pos + 1) * HEAD_DIM)
            base = (1 - pos) * HEAD_DIM
            f = fc[:, h:h + 1]
            hi = f.astype(bf16).astype(f32)
            mid = (f - hi).astype(bf16).astype(f32)
            lo = (f - hi) - mid
            one = jnp.ones_like(f)
            qa = jnp.where(own, q[:, pair * LANES:(pair + 1) * LANES] * ATT_SCALE, 0.0)
            ka = k[:, pair * LANES:(pair + 1) * LANES]
            for idx, (qv, kv) in enumerate([(hi, one), (mid, one), (lo, one), (one, -hi), (one, -mid), (one, -lo)]):
                sel = lane == base + idx
                qa = jnp.where(sel, qv, qa)
                ka = jnp.where(sel, kv, ka)
            qs.append(qa)
            ks.append(ka)
        return [jnp.concatenate(qs, axis=1), jnp.concatenate(ks, axis=1), vv], []
    w = N_FOX_HEADS * LANES
    return _rowwise(fn, "fox_prep", [(proj, FOX_W, C_QA // FOX_W), (proj, FOX_W, C_KA // FOX_W),
                                     (proj, FOX_W, C_VA // FOX_W), (f_col, LANES, 0)], [],
                    [(w, bf16), (w, bf16), (FOX_W, bf16)])


def _fox_fwd(q_aug, k_aug, v):
    blk = FOX_BLK
    npair = FOX_W // LANES

    def body(q_ref, k_ref, v_ref, o_ref, max_ref, sum_ref, s_scr):
        i = pl.program_id(1)
        tri = lax.broadcasted_iota(jnp.int32, (blk, blk), 0) >= lax.broadcasted_iota(jnp.int32, (blk, blk), 1)
        qh = [q_ref[:, h * LANES:(h + 1) * LANES] for h in range(2)]

        def logits(c, masked):
            off = pl.multiple_of(c * blk, blk)
            tops = []
            for h in range(2):
                s = _nt(qh[h], k_ref[pl.ds(off, blk), h * LANES:(h + 1) * LANES])
                if masked:
                    s = jnp.where(tri, s, NEG)
                s_scr[h, :, pl.ds(off, blk)] = s
                tops.append(_fold_lanes(s, jnp.maximum))
            return tops

        def pass_a(c, m):
            return tuple(jnp.maximum(a, b) for a, b in zip(m, logits(c, False)))

        m = lax.fori_loop(0, i, pass_a, tuple(jnp.full((blk, LANES), NEG, f32) for _ in range(2)))
        mx = [jnp.max(jnp.maximum(a, b), axis=1, keepdims=True) for a, b in zip(m, logits(i, True))]

        def pass_b(c, carry):
            off = pl.multiple_of(c * blk, blk)
            vv = v_ref[pl.ds(off, blk), :]
            new = []
            for h in range(2):
                l, acc = carry[h]
                p = jnp.exp(s_scr[h, :, pl.ds(off, blk)] - mx[h]).astype(bf16)
                new.append((l + _fold_lanes(p.astype(f32), jnp.add), acc + jnp.dot(p, vv, preferred_element_type=f32)))
            return tuple(new)

        zero = jnp.zeros((blk, LANES), f32)
        (l_a, acc_a), (l_b, acc_b) = lax.fori_loop(0, i + 1, pass_b, ((zero, zero), (zero, zero)))
        l_a = jnp.sum(l_a, axis=1, keepdims=True)
        l_b = jnp.sum(l_b, axis=1, keepdims=True)
        first = lax.broadcasted_iota(jnp.int32, (blk, LANES), 1) < HEAD_DIM
        o_ref[...] = jnp.where(first, acc_a / l_a, acc_b / l_b)
        max_ref[0] = jnp.where(first, mx[0], mx[1])
        sum_ref[0] = jnp.where(first, l_a, l_b)

    return pl.pallas_call(
        body, grid=(npair, SEQ // blk),
        in_specs=[pl.BlockSpec((blk, 2 * LANES), lambda p, i: (i, p)),
                  pl.BlockSpec((SEQ, 2 * LANES), lambda p, i: (0, p)),
                  pl.BlockSpec((SEQ, LANES), lambda p, i: (0, p))],
        out_specs=[pl.BlockSpec((blk, LANES), lambda p, i: (i, p))]
        + [pl.BlockSpec((1, blk, LANES), lambda p, i: (p, i, 0))] * 2,
        out_shape=(SDS((SEQ, FOX_W), f32),) + (SDS((npair, SEQ, LANES), f32),) * 2, name="fox_fwd",
        scratch_shapes=[pltpu.VMEM((2, blk, SEQ), f32)],
        compiler_params=_params(("parallel", "arbitrary")),
    )(q_aug, k_aug, v)


def _fox_bwd(q_aug, k_aug, v, do, o, row_max, row_sum, after):
    blk = FOX_BLK
    npair = FOX_W // LANES
    nblk = SEQ // blk

    def body(q_ref, k_ref, v_ref, do_ref, o_ref, max_ref, sum_ref, after_ref, dq_ref, dk_ref, dv_ref, df_ref, dq_acc,
             delta_ref, inv_ref):
        inv_ref[...] = 1.0 / sum_ref[0]
        lane_s = lax.broadcasted_iota(jnp.int32, (SEQ, LANES), 1)
        prod = do_ref[...].astype(bf16).astype(f32) * o_ref[...]
        d_a = jnp.sum(jnp.where(lane_s < HEAD_DIM, prod, 0.0), axis=1, keepdims=True)
        d_b = jnp.sum(jnp.where(lane_s >= HEAD_DIM, prod, 0.0), axis=1, keepdims=True)
        delta_ref[...] = jnp.where(lane_s < HEAD_DIM, d_a, d_b)
        dq_acc[...] = jnp.zeros_like(dq_acc)
        df_ref[...] = jnp.zeros_like(df_ref)
        lane = lax.broadcasted_iota(jnp.int32, (blk, LANES), 1)
        own = [lane < HEAD_DIM, lane >= HEAD_DIM]
        tri = lax.broadcasted_iota(jnp.int32, (blk, blk), 0) >= lax.broadcasted_iota(jnp.int32, (blk, blk), 1)

        def q_slab(qoff, h):
            return q_ref[pl.ds(qoff, blk), h * LANES:(h + 1) * LANES]

        def probs(qoff, h, k_h, masked):
            s = _nt(q_slab(qoff, h), k_h)
            if masked:
                s = jnp.where(tri, s, NEG)
            col = slice(h * HEAD_DIM, h * HEAD_DIM + 1)
            weights = jnp.exp(s - max_ref[0, pl.ds(qoff, blk), col]).astype(bf16).astype(f32)
            return weights * inv_ref[pl.ds(qoff, blk), col]

        def k_slabs(koff):
            return [k_ref[pl.ds(koff, blk), h * LANES:(h + 1) * LANES] for h in range(2)]

        def kv_step(kj, _):
            koff = pl.multiple_of(kj * blk, blk)
            k_aug = k_slabs(koff)
            k_own = [jnp.where(own[h], k_aug[h], jnp.zeros_like(k_aug[h])) for h in range(2)]
            vv = v_ref[pl.ds(koff, blk), :]
            v_own = [jnp.where(own[h], vv, jnp.zeros_like(vv)) for h in range(2)]

            def q_tile(qi, carry, masked):
                qoff = pl.multiple_of(qi * blk, blk)
                dd = do_ref[pl.ds(qoff, blk), :].astype(bf16)
                new, dq_add = [], None
                for h in range(2):
                    dk_h, dv_h, dcol = carry[h]
                    p = probs(qoff, h, k_aug[h], masked)
                    dl = p * (_nt(dd, v_own[h]) - delta_ref[pl.ds(qoff, blk), h * HEAD_DIM:h * HEAD_DIM + 1])
                    dlb = dl.astype(bf16)
                    part = jnp.dot(dlb, k_own[h], preferred_element_type=f32)
                    dq_add = part if dq_add is None else dq_add + part
                    new.append((dk_h + _tn(dlb, q_slab(qoff, h)), dv_h + _tn(p.astype(bf16), dd),
                                dcol + _colsum(dl)))
                dq_acc[pl.ds(qoff, blk), :] += dq_add * ATT_SCALE
                return tuple(new)

            zero = (jnp.zeros((blk, LANES), f32), jnp.zeros((blk, LANES), f32), jnp.zeros((1, blk), f32))
            carry = q_tile(kj, (zero, zero), True)
            (dk_a, dv_a, dcol_a), (dk_b, dv_b, dcol_b) = lax.fori_loop(
                kj + 1, nblk, lambda qi, cr: q_tile(qi, cr, False), carry)
            dk_ref[pl.ds(koff, blk), :] = jnp.where(own[0], dk_a, dk_b).astype(dk_ref.dtype)
            dv_ref[pl.ds(koff, blk), :] = jnp.where(own[0], dv_a, dv_b).astype(dv_ref.dtype)
            df_ref[0, 0:1, pl.ds(koff, blk)] = -dcol_a
            df_ref[0, 1:2, pl.ds(koff, blk)] = -dcol_b
            return 0

        lax.fori_loop(0, nblk, kv_step, 0)
        dq_ref[...] = dq_acc[...].astype(dq_ref.dtype)

    pair_aug = pl.BlockSpec((SEQ, 2 * LANES), lambda p: (0, p))
    slab = pl.BlockSpec((SEQ, LANES), lambda p: (0, p))
    per_pair = pl.BlockSpec((1, SEQ, LANES), lambda p: (p, 0, 0))
    rows = pl.BlockSpec((1, 8, SEQ), lambda p: (p, 0, 0))
    return pl.pallas_call(
        body, grid=(npair,),
        in_specs=[pair_aug, pair_aug, slab, slab, slab, per_pair, per_pair, pl.BlockSpec(memory_space=pl.ANY)],
        out_specs=[slab, slab, slab, rows],
        out_shape=(SDS((SEQ, FOX_W), bf16),) * 3 + (SDS((npair, 8, SEQ), f32),), name="fox_bwd",
        scratch_shapes=[pltpu.VMEM((SEQ, LANES), f32)] * 3,
        compiler_params=_params(("parallel",)),
    )(q_aug, k_aug, v, do, o, row_max, row_sum, after)


DIL_BLK = 128
DILATIONS = (1, 4, 16)
N_GROUPS = len(DILATIONS)
DIL_PAIRS = DIL_OUT_W // LANES


def _dil_blocks(d):
    r1 = lax.broadcasted_iota(jnp.int32, (2 * DIL_BLK, DIL_BLK), 0) & (DIL_BLK - 1)
    c1 = lax.broadcasted_iota(jnp.int32, (2 * DIL_BLK, DIL_BLK), 1)
    r2 = lax.broadcasted_iota(jnp.int32, (2 * DIL_BLK, 2 * DIL_BLK), 0) & (DIL_BLK - 1)
    c2 = lax.broadcasted_iota(jnp.int32, (2 * DIL_BLK, 2 * DIL_BLK), 1)
    band = ((c2 < DIL_BLK) & (c2 >= r2)) | ((c2 >= DIL_BLK) & (c2 - DIL_BLK <= r2))
    out = []
    for r in range(d):
        for b in range(SEQ // d // DIL_BLK):
            rows = pl.ds(r + d * DIL_BLK * b, DIL_BLK, stride=d)
            if b == 0:
                out.append((rows, rows, r1 >= c1))
            else:
                out.append((rows, pl.ds(r + d * DIL_BLK * (b - 1), 2 * DIL_BLK, stride=d), band))
    return out


def _dil_v_spec(g):
    return pl.BlockSpec((SEQ, LANES), lambda p: (0, C_VB // LANES + DIL_PAIRS * g + p))


def _stack_heads(t, first):
    zero = jnp.zeros_like(t)
    return jnp.concatenate([jnp.where(first, t, zero), jnp.where(first, zero, t)], axis=0)


def _dil_fwd(q, k, v, g):
    def body(q_ref, k_ref, v_ref, o_ref, lse_ref):
        first = lax.broadcasted_iota(jnp.int32, (DIL_BLK, LANES), 1) < HEAD_DIM
        for rows, krows, mask in _dil_blocks(DILATIONS[g]):
            qv, kk, vv = q_ref[rows, :].astype(bf16), k_ref[krows, :].astype(bf16), v_ref[krows, :].astype(bf16)
            s = jnp.where(mask, _nt(_stack_heads(qv, first), kk), NEG)
            m = jnp.max(s, axis=1, keepdims=True)
            p = jnp.exp(s - m)
            l = jnp.sum(p, axis=1, keepdims=True)
            out = jnp.dot(p.astype(bf16), vv, preferred_element_type=f32) / l
            lse = m + jnp.log(l)
            o_ref[rows, :] = jnp.where(first, out[:DIL_BLK], out[DIL_BLK:])
            lse_ref[rows, :] = jnp.where(first, lse[:DIL_BLK], lse[DIL_BLK:])

    grouped = pl.BlockSpec((SEQ, LANES), lambda p: (0, DIL_PAIRS * g + p))
    own = pl.BlockSpec((SEQ, LANES), lambda p: (0, p))
    shape = SDS((SEQ, DIL_OUT_W), f32)
    return pl.pallas_call(
        body, grid=(DIL_PAIRS,), in_specs=[grouped, grouped, _dil_v_spec(g)], out_specs=[own] * 2,
        out_shape=(shape, shape),
        name=f"dil_fwd_{DILATIONS[g]}", compiler_params=_params(("parallel",)),
    )(q, k, v)


def _dil_bwd(q, k, v, do, lse, delta, g):
    def body(q_ref, k_ref, v_ref, do_ref, lse_ref, dl_ref, dq_ref, dk_ref, dv_ref):
        first = lax.broadcasted_iota(jnp.int32, (DIL_BLK, LANES), 1) < HEAD_DIM
        dk_ref[...] = jnp.zeros_like(dk_ref)
        dv_ref[...] = jnp.zeros_like(dv_ref)
        for rows, krows, mask in _dil_blocks(DILATIONS[g]):
            qv, kk, vv = q_ref[rows, :].astype(bf16), k_ref[krows, :].astype(bf16), v_ref[krows, :].astype(bf16)
            lsev, delv = lse_ref[rows, :], dl_ref[rows, :]
            q2 = _stack_heads(qv, first)
            do2 = _stack_heads(do_ref[rows, :].astype(bf16), first)
            per_head = lambda t: jnp.concatenate([t[:, 0:1], t[:, HEAD_DIM:HEAD_DIM + 1]], axis=0)
            p = jnp.exp(jnp.where(mask, _nt(q2, kk), NEG) - per_head(lsev))
            dl = (p * (_nt(do2, vv) - per_head(delv))).astype(bf16)
            dq = jnp.dot(dl, kk, preferred_element_type=f32)
            dq_ref[rows, :] = jnp.where(first, dq[:DIL_BLK], dq[DIL_BLK:]) * ATT_SCALE
            dk_ref[krows, :] += _tn(dl, q2)
            dv_ref[krows, :] += _tn(p.astype(bf16), do2)

    grouped = pl.BlockSpec((SEQ, LANES), lambda p: (0, DIL_PAIRS * g + p))
    own = pl.BlockSpec((SEQ, LANES), lambda p: (0, p))
    shape = SDS((SEQ, DIL_OUT_W), f32)
    return pl.pallas_call(
        body, grid=(DIL_PAIRS,), in_specs=[grouped, grouped, _dil_v_spec(g)] + [own] * 3, out_specs=[own] * 3,
        out_shape=(shape, shape, shape), name=f"dil_bwd_{DILATIONS[g]}", compiler_params=_params(("parallel",)),
    )(q, k, v, do, lse, delta)


def _position():
    return lax.axis_index("x"), lax.axis_index("y"), lax.axis_index("c")


def _all_gather(block, name):
    def body(x_ref, out_ref, send_sems, recv_sems, local_sem):
        x, y, c = _position()
        me, sibling = (x, y, c), (x, y, 1 - c)
        chips = [(1 - x, y), (x, 1 - y), (1 - x, 1 - y)]

        def slot(px, py, pc):
            return out_ref.at[4 * px + 2 * py + pc]

        def copy(k, blk, to, src=None):
            return pltpu.make_async_remote_copy(
                src_ref=slot(*blk) if src is None else src, dst_ref=slot(*blk),
                send_sem=send_sems.at[k], recv_sem=recv_sems.at[k], device_id=to, device_id_type=MESH)

        mine = pltpu.make_async_copy(x_ref, slot(*me), local_sem)
        mine.start()
        first = [copy(0, me, sibling, src=x_ref)]
        first += [copy(1 + j, me, (*chip, c), src=x_ref) for j, chip in enumerate(chips)]
        for cp in first:
            cp.start()
        passed = [copy(4 + j, (*chip, c), sibling) for j, chip in enumerate(chips)]
        for j, chip in enumerate(chips):
            copy(1 + j, (*chip, c), me).wait_recv()
            passed[j].start()
        copy(0, sibling, me).wait_recv()
        for j, chip in enumerate(chips):
            copy(4 + j, (*chip, 1 - c), me).wait_recv()
        for cp in first + passed:
            cp.wait_send()
        mine.wait()

    return pl.pallas_call(
        body, out_shape=SDS((N_DEV,) + block.shape, block.dtype),
        in_specs=[pl.BlockSpec(memory_space=pl.ANY)], out_specs=pl.BlockSpec(memory_space=pl.ANY),
        scratch_shapes=[pltpu.SemaphoreType.DMA((7,)), pltpu.SemaphoreType.DMA((7,)), pltpu.SemaphoreType.DMA],
        name=name,
    )(block)


HBM_SPEC = pl.BlockSpec(memory_space=pltpu.HBM)
SEM_SPEC = pl.BlockSpec(memory_space=pltpu.SEMAPHORE)
SPLIT_COPY = pltpu.CompilerParams(has_side_effects=pltpu.SideEffectType.DATAFLOW_SIDE_EFFECTING)


def _in_hbm(t):
    return pltpu.with_memory_space_constraint(t, pltpu.HBM)


def _pair_copies(g_refs, land_refs, send_sems, recv_sems):
    x, y, c = _position()
    return [pltpu.make_async_remote_copy(
        src_ref=g.at[2 * k + (1 - c)], dst_ref=land.at[k], send_sem=send_sems.at[4 * a + k],
        recv_sem=recv_sems.at[4 * a + k], device_id=(x, y, 1 - c), device_id_type=MESH)
        for a, (g, land) in enumerate(zip(g_refs, land_refs, strict=True)) for k in range(4)]


def _chip_copies(t_refs, land_refs, send_sems, recv_sems):
    x, y, c = _position()
    chips = [(1 - x, y), (x, 1 - y), (1 - x, 1 - y)]
    return [pltpu.make_async_remote_copy(
        src_ref=t.at[2 * px + py], dst_ref=land.at[j], send_sem=send_sems.at[3 * a + j],
        recv_sem=recv_sems.at[3 * a + j], device_id=(px, py, c), device_id_type=MESH)
        for a, (t, land) in enumerate(zip(t_refs, land_refs, strict=True)) for j, (px, py) in enumerate(chips)]


_ROUNDS = {"pair": (_pair_copies, 4), "chip": (_chip_copies, 3)}


def _exchange_start(kind, ts, name):
    copies, slots = _ROUNDS[kind]
    n = len(ts)
    lands = [_in_hbm(lax.empty((slots,) + t.shape[1:], t.dtype)) for t in ts]

    def body(*refs):
        for cp in copies(refs[:n], refs[n:2 * n], refs[2 * n], refs[2 * n + 1]):
            cp.start()
        refs[-1][...] = jnp.zeros_like(refs[-1])

    sems = pltpu.SemaphoreType.DMA((slots * n,))
    res = pl.pallas_call(
        body, name=name, in_specs=[HBM_SPEC] * (2 * n),
        out_shape=(sems, sems, *[pltpu.HBM(t.shape, t.dtype) for t in (*ts, *lands)], SDS((8, LANES), f32)),
        out_specs=(SEM_SPEC, SEM_SPEC, *[HBM_SPEC] * (2 * n), pl.BlockSpec(memory_space=pltpu.VMEM)),
        input_output_aliases={i: 2 + i for i in range(2 * n)}, compiler_params=SPLIT_COPY,
    )(*[_in_hbm(t) for t in ts], *lands)
    return res[:-1], res[-1]


def _exchange_wait(kind, state, after, name):
    copies, _ = _ROUNDS[kind]
    send_sems, recv_sems, *arrays = state
    n = len(arrays) // 2

    def body(*refs):
        for cp in copies(refs[:n], refs[n:2 * n], refs[2 * n], refs[2 * n + 1]):
            cp.wait_send()
            cp.wait_recv()

    res = pl.pallas_call(
        body, name=name, in_specs=[HBM_SPEC] * (2 * n) + [SEM_SPEC, SEM_SPEC, pl.BlockSpec(memory_space=pl.ANY)],
        out_shape=[pltpu.HBM(t.shape, t.dtype) for t in arrays], out_specs=[HBM_SPEC] * (2 * n),
        input_output_aliases={i: i for i in range(2 * n)}, compiler_params=SPLIT_COPY,
    )(*arrays, send_sems, recv_sems, after)
    return res[:n], res[n:]


def _gather_copies(x_refs, out_refs, send_sems, recv_sems):
    x, y, c = _position()
    peers = [(x, y, 1 - c), (1 - x, y, c), (x, 1 - y, c), (1 - x, 1 - y, c)]
    sends, arrivals = [], []
    for a, (x_ref, out_ref) in enumerate(zip(x_refs, out_refs, strict=True)):
        for k, (px, py, pc) in enumerate(peers):
            sems = dict(send_sem=send_sems.at[4 * a + k], recv_sem=recv_sems.at[4 * a + k],
                        device_id=(px, py, pc), device_id_type=MESH)
            sends.append(pltpu.make_async_remote_copy(src_ref=x_ref, dst_ref=out_ref.at[4 * x + 2 * y + c], **sems))
            arrivals.append(pltpu.make_async_remote_copy(src_ref=x_ref, dst_ref=out_ref.at[4 * px + 2 * py + pc],
                                                         **sems))
    return sends, arrivals


def _gather_start(blocks, after, name):
    n = len(blocks)
    outs = [_in_hbm(lax.empty((N_DEV,) + b.shape, b.dtype)) for b in blocks]

    def body(*refs):
        sends, _ = _gather_copies(refs[:n], refs[n:2 * n], refs[2 * n + 1], refs[2 * n + 2])
        for cp in sends:
            cp.start()
        refs[-1][...] = jnp.zeros_like(refs[-1])

    sems = pltpu.SemaphoreType.DMA((4 * n,))
    res = pl.pallas_call(
        body, name=name, in_specs=[HBM_SPEC] * (2 * n) + [pl.BlockSpec(memory_space=pl.ANY)],
        out_shape=(sems, sems, *[pltpu.HBM(t.shape, t.dtype) for t in (*blocks, *outs)], SDS((8, LANES), f32)),
        out_specs=(SEM_SPEC, SEM_SPEC, *[HBM_SPEC] * (2 * n), pl.BlockSpec(memory_space=pltpu.VMEM)),
        input_output_aliases={i: 2 + i for i in range(2 * n)}, compiler_params=SPLIT_COPY,
    )(*[_in_hbm(b) for b in blocks], *outs, after)
    return res[:-1], res[-1]


def _gather_wait(state, after, name):
    send_sems, recv_sems, *arrays = state
    n = len(arrays) // 2

    def body(*refs):
        sends, arrivals = _gather_copies(refs[:n], refs[n:2 * n], refs[2 * n], refs[2 * n + 1])
        for cp in sends:
            cp.wait_send()
        for cp in arrivals:
            cp.wait_recv()

    res = pl.pallas_call(
        body, name=name, in_specs=[HBM_SPEC] * (2 * n) + [SEM_SPEC, SEM_SPEC, pl.BlockSpec(memory_space=pl.ANY)],
        out_shape=[pltpu.HBM(t.shape, t.dtype) for t in arrays], out_specs=[HBM_SPEC] * (2 * n),
        input_output_aliases={i: i for i in range(2 * n)}, compiler_params=SPLIT_COPY,
    )(*arrays, send_sems, recv_sems, after)
    return res[:n], res[n:]


def _gather_finish(partial, name):
    n = len(partial)

    def body(*refs):
        in_refs, out_refs = refs[:n], refs[n:2 * n]
        send_sems, recv_sems = refs[2 * n:]
        x, y, c = _position()
        chips = [(1 - x, y), (x, 1 - y), (1 - x, 1 - y)]
        copies = []
        for a in range(n):
            for j, (px, py) in enumerate(chips):
                cp = pltpu.make_async_remote_copy(
                    src_ref=in_refs[a].at[4 * px + 2 * py + c], dst_ref=out_refs[a].at[4 * px + 2 * py + c],
                    send_sem=send_sems.at[a, j], recv_sem=recv_sems.at[a, j], device_id=(x, y, 1 - c),
                    device_id_type=MESH)
                cp.start()
                copies.append(cp)
        for a in range(n):
            for j, (px, py) in enumerate(chips):
                pltpu.make_async_remote_copy(
                    src_ref=in_refs[a].at[4 * px + 2 * py + (1 - c)], dst_ref=out_refs[a].at[4 * px + 2 * py + (1 - c)],
                    send_sem=send_sems.at[a, j], recv_sem=recv_sems.at[a, j], device_id=(x, y, 1 - c),
                    device_id_type=MESH).wait_recv()
        for cp in copies:
            cp.wait_send()

    hbm = pl.BlockSpec(memory_space=pl.ANY)
    return pl.pallas_call(
        body, out_shape=[SDS(p.shape, p.dtype) for p in partial], in_specs=[hbm] * n, out_specs=[hbm] * n,
        input_output_aliases={a: a for a in range(n)},
        scratch_shapes=[pltpu.SemaphoreType.DMA((n, 3)), pltpu.SemaphoreType.DMA((n, 3))],
        name=name,
    )(*partial)


def _row_tile(rows):
    return 512 if rows % 512 == 0 and rows > 512 else rows


def _pair_add(g, r1, core, name):
    def body(c_ref, g_ref, r_ref, o_ref):
        o_ref[...] = (g_ref[...].astype(f32) + r_ref[...].astype(f32)).astype(o_ref.dtype)

    rows, cols = g.shape[1:]
    tile = _row_tile(rows)
    blk = (1, tile, cols)
    return pl.pallas_call(
        body, out_shape=SDS((4, rows, cols), g.dtype), name=name,
        grid_spec=pltpu.PrefetchScalarGridSpec(
            num_scalar_prefetch=1, grid=(4, rows // tile),
            in_specs=[pl.BlockSpec(blk, lambda k, i, c_ref: (2 * k + c_ref[0], i, 0)),
                      pl.BlockSpec(blk, lambda k, i, c_ref: (k, i, 0))],
            out_specs=pl.BlockSpec(blk, lambda k, i, c_ref: (k, i, 0))),
        compiler_params=_params(("parallel", "arbitrary")),
    )(core, g, r1)


def _chip_add(t, r2, chip, name, transposed=False):
    def body(c_ref, t_ref, r_ref, o_ref):
        s = ((t_ref[0].astype(f32) + r_ref[0].astype(f32)) + r_ref[1].astype(f32)) + r_ref[2].astype(f32)
        o_ref[...] = s.T if transposed else s

    rows, cols = t.shape[1:]
    tile = _row_tile(rows)
    out_spec = pl.BlockSpec((cols, tile), lambda i, c_ref: (0, i)) if transposed else pl.BlockSpec(
        (tile, cols), lambda i, c_ref: (i, 0))
    return pl.pallas_call(
        body, out_shape=SDS((cols, rows) if transposed else (rows, cols), f32), name=name,
        grid_spec=pltpu.PrefetchScalarGridSpec(
            num_scalar_prefetch=1, grid=(rows // tile,),
            in_specs=[pl.BlockSpec((1, tile, cols), lambda i, c_ref: (c_ref[0], i, 0)),
                      pl.BlockSpec((3, tile, cols), lambda i, c_ref: (0, i, 0))],
            out_specs=out_spec),
        compiler_params=_params(("arbitrary",)),
    )(chip, t, r2)


def _pad_to(t, axis, size):
    pads = [(0, 0)] * t.ndim
    pads[axis] = (0, size - t.shape[axis])
    return jnp.pad(t, pads)


_REF_COLS = {"qa": (0, FOX_W), "ka": (FOX_W, FOX_W), "va": (2 * FOX_W, FOX_W), "f": (3 * FOX_W, N_FOX_HEADS)}
_REF_COLS.update({n: (3 * FOX_W + N_FOX_HEADS + i * DIL_W, DIL_W) for i, n in enumerate(("qb", "kb", "vb"))})
_REF_COLS.update({n: (3 * FOX_W + N_FOX_HEADS + 3 * DIL_W + i * D, D) for i, n in enumerate(("ga", "gb"))})
_REF_ORDER = ("qa", "ka", "va", "f", "qb", "kb", "vb", "ga", "gb")


def _place_cols(sources, src_of, out_cols, name, row_block=512):
    arrays = [s[0] if isinstance(s, tuple) else s for s in sources]
    widths = [a.shape[-1] for a in arrays]
    rows = arrays[0].shape[-2]
    plan = []
    for t in range(out_cols // LANES):
        segs, c, end = [], t * LANES, (t + 1) * LANES
        while c < end:
            s = src_of(c)
            if s is None:
                c += 1
                continue
            n = 1
            while c + n < end and src_of(c + n) == (s[0], s[1] + n):
                n += 1
            segs.append((s[0], s[1], c - t * LANES, n))
            c += n
        plan.append(segs)

    def body(*refs):
        o_ref = refs[-1]
        for t, segs in enumerate(plan):
            acc = None
            for si, c0, o0, n in segs:
                a0 = c0 // LANES * LANES
                wide = min(2 * LANES, widths[si] - a0)
                win = refs[si][0, :, a0:a0 + wide] if isinstance(sources[si], tuple) else refs[si][:, a0:a0 + wide]
                r = lax.broadcasted_iota(jnp.int32, (wide, LANES), 0)
                c = lax.broadcasted_iota(jnp.int32, (wide, LANES), 1)
                pick = ((r - (c0 - a0) == c - o0) & (c >= o0) & (c < o0 + n)).astype(bf16)
                part = jnp.dot(win.astype(bf16), pick, preferred_element_type=f32)
                acc = part if acc is None else acc + part
            tile = jnp.zeros((row_block, LANES), f32) if acc is None else acc
            o_ref[:, t * LANES:(t + 1) * LANES] = tile.astype(o_ref.dtype)

    def spec(s):
        if isinstance(s, tuple):
            j = s[1]
            return pl.BlockSpec((1, row_block, s[0].shape[-1]), lambda i: (j, i, 0))
        return pl.BlockSpec((row_block, s.shape[-1]), lambda i: (i, 0))

    return pl.pallas_call(
        body, grid=(rows // row_block,), in_specs=[spec(s) for s in sources],
        out_specs=pl.BlockSpec((row_block, out_cols), lambda i: (i, 0)), out_shape=SDS((rows, out_cols), bf16),
        name=name, compiler_params=_params(("parallel",)),
    )(*arrays)


def _ref_piece(r):
    for name in _REF_ORDER:
        lo, width = _REF_COLS[name]
        if lo <= r < lo + width:
            return name, r - lo
    raise ValueError(r)


def _shard_pad_cols(pieces):
    names = [n for n in _REF_ORDER if n != "vb"]
    sources = [pieces[n] for n in names] + list(pieces["vb"])

    def src_of(c):
        j, i = divmod(c, W_IN_PAD)
        if i >= W_IN_SH:
            return None
        name, col = _ref_piece(j * W_IN_SH + i)
        if name == "vb":
            return len(names) + col // DIL_OUT_W, col % DIL_OUT_W
        return names.index(name), col

    return _place_cols(sources, src_of, N_DEV * W_IN_PAD, "place_dproj")


_SLABS = {"ga": C_GA, "gb": C_GB, "qb": C_QB, "kb": C_KB, "vb": C_VB, "qa": C_QA, "ka": C_KA, "va": C_VA, "f": C_F}


def _slab_w_in(stack):
    def src_of(c):
        for name, start in _SLABS.items():
            lo, width = _REF_COLS[name]
            if start <= c < start + width:
                return divmod(lo + c - start, W_IN_SH)
        return None

    return _place_cols([(stack, j) for j in range(N_DEV)], src_of, PROJ_W, "place_w_in")


def kernel(x, c, w_ada, b_ada, g_mix, w_in, b_fgate, w_br_a, w_br_b, w_out, g_ffn, w_ffn_gate, w_ffn_up, w_ffn_down, g_final, loss_target, m_w_ada, m_b_ada, m_g_mix, m_w_in, m_b_fgate, m_w_br_a, m_w_br_b, m_w_out, m_g_ffn, m_w_ffn_gate, m_w_ffn_up, m_w_ffn_down, m_g_final, v_w_ada, v_b_ada, v_g_mix, v_w_in, v_b_fgate, v_w_br_a, v_w_br_b, v_w_out, v_g_ffn, v_w_ffn_gate, v_w_ffn_up, v_w_ffn_down, v_g_final):
    px, py, pc = _position()
    dev = 4 * px + 2 * py + pc
    x2d, tgt = x[0], loss_target[0]

    c_all = _all_gather(c, "gather_c").reshape(N_DEV, D)
    ada_cols = w_ada.shape[2]
    b_shard = lax.dynamic_slice(b_ada, (0, dev * ada_cols), (1, ada_cols))
    mod_shard = _ada_fwd(c_all, w_ada[0], b_shard)
    mod_all = _all_gather(mod_shard, "gather_mod")
    modv = lax.dynamic_index_in_dim(mod_all, dev, axis=1, keepdims=False).reshape(6, D)
    h1 = _pre1(x2d, modv, g_mix)

    w_in_s = _all_gather(_pad_to(w_in[0], 1, W_IN_PAD).astype(bf16), "gather_w_in")
    gate_up = jnp.concatenate([_pad_to(w_ffn_gate[0], 1, FF_PAD), _pad_to(w_ffn_up[0], 1, FF_PAD)], axis=1)
    later = [w_br_a[0], w_br_b[0], w_out[0], gate_up, _pad_to(w_ffn_down[0], 0, FF_PAD)]
    later_state, later_token = _gather_start([t.astype(bf16) for t in later], w_in_s, "gather_rest_start")
    w_in_p = _slab_w_in(w_in_s)

    proj = _matmul(h1, w_in_p, name="mm_proj", tm=SEQ, tn=896, tk=D, after=later_token)
    b_pad = jnp.pad(b_fgate, ((0, 0), (0, LANES - N_FOX_HEADS)))
    q_aug, k_aug, va = _fox_prep(proj, _fox_gate_fwd(proj, b_pad))
    ya_h, max_a, sum_a = _fox_fwd(q_aug, k_aug, va)

    tables = _rope_tables()
    qb_r, kb_r = _rope_fwd(proj, tables)
    by_group = [_dil_fwd(qb_r, kb_r, proj, grp) for grp in range(N_GROUPS)]
    yb_h, lse_b = _dil_combine([o for o, _ in by_group], [l for _, l in by_group])

    mine, arrived = _gather_wait(later_state, yb_h, "gather_rest_wait")
    w_a_s, w_b_s, w_o_s, w_gu_s, w_d_s = [
        lax.dynamic_update_slice(stack, block[None], (dev, 0, 0))
        for stack, block in zip(_gather_finish(arrived, "gather_rest_finish"), mine, strict=True)]
    w_o = w_o_s.reshape(D, D)
    w_d = w_d_s.reshape(FF_HID, D)
    ya = _matmul_stack(ya_h, w_a_s, name="mm_br_a")
    yb = _matmul_stack(yb_h, w_b_s, name="mm_br_b")

    merged, mix, x1, h2 = _post1(ya, yb, proj, w_o, x2d, modv, g_ffn)
    act, au = _ffn_in(h2, w_gu_s)

    dx2, dff, dg_final, dga_f, loss_lanes = _final(act, w_d, x1, tgt, modv, g_final.reshape(1, D))
    dau = _ffn_bwd_in(dff, w_d_s, au)

    core = pc.astype(jnp.int32).reshape(1)
    chip = (2 * px + py).astype(jnp.int32).reshape(1)

    def pair_done(state, after, tags, name):
        mine, theirs = _exchange_wait("pair", state, after, "pair_wait_" + name)
        sums = [_pair_add(g, r, core, "pair_add_" + t) for g, r, t in zip(mine, theirs, tags)]
        return _exchange_start("chip", sums, "chip_start_" + name)

    def from_chips(state, after, tags, name, transposed=None):
        sums, got = _exchange_wait("chip", state, after, "chip_wait_" + name)
        flips = transposed or [False] * len(tags)
        return [_chip_add(p, r, chip, "chip_add_" + t, f) for p, r, t, f in zip(sums, got, tags, flips)]

    g_gu = _matmul(h2, dau, ta=True, by_shard=True, out_dtype=bf16, name="mm_g_ffn_in", tm=D, tn=2 * FF_PAD, tk=SEQ)
    g_d = _matmul(act, dff, ta=True, out_dtype=bf16, name="mm_g_down", tm=FF_HID // 2, tn=512, tk=SEQ)
    ffn_tags = ["gu", "down"]
    ffn_pair, ffn_pair_token = _exchange_start("pair", [g_gu, g_d.reshape(N_DEV, FF_PAD, D)], "pair_start_ffn")

    dx1, dmix, dsh_f, dsc_f, dg_ffn, dga_m = _mid_bwd(dau, w_gu_s, ffn_pair_token, x1, dx2, mix, modv, g_ffn)
    ffn_state, ffn_token = pair_done(ffn_pair, dx1, ffn_tags, "ffn")
    dya, dyb, dga, dgb = _merge_bwd(dmix, w_o, ffn_token, ya, yb, proj)
    dya_h = _matmul_stack(dya, w_a_s, tb=True, name="mm_d_ya")
    dyb_h = _matmul_stack(dyb, w_b_s, tb=True, name="mm_d_yb")

    g_o = _matmul(merged, dmix, ta=True, out_dtype=bf16, name="mm_g_out", tm=D, tn=512, tk=SEQ)
    g_a = _matmul_stack(ya_h, dya, ta=True, out_dtype=bf16, name="mm_g_br_a")
    g_b = _matmul_stack(yb_h, dyb, ta=True, out_dtype=bf16, name="mm_g_br_b")
    rows_a, rows_b = FOX_W * W_BR_SH // D, DIL_OUT_W * W_BR_SH // D
    g_small = jnp.concatenate([g_a.reshape(N_DEV, rows_a, D), g_b.reshape(N_DEV, rows_b, D),
                               g_o.reshape(N_DEV, W_BR_SH, D)], axis=1)
    small_pair, small_pair_token = _exchange_start("pair", [g_small], "pair_start_small")

    dqa, dka, dva, dF = _fox_bwd(q_aug, k_aug, va, dya_h, ya_h, max_a, sum_a, small_pair_token)
    dF_row = jnp.pad(dF[:, :2, :].reshape(N_FOX_HEADS, SEQ), ((0, LANES - N_FOX_HEADS), (0, 0)))
    df, db_fgate = _fox_gate_bwd(dF_row, proj, b_pad)
    small_state, small_token = pair_done(small_pair, df, ["small"], "small")

    delta_b = _dil_delta(dyb_h, yb_h)
    dil_grads = [_dil_bwd(qb_r, kb_r, proj, dyb_h, lse_b, delta_b, grp) for grp in range(N_GROUPS)]
    dqb, dkb = _rope_bwd([t[0] for t in dil_grads], [t[1] for t in dil_grads], tables)

    dproj = _shard_pad_cols({"qa": dqa, "ka": dka, "va": dva, "f": df, "qb": dqb, "kb": dkb,
                             "vb": [t[2] for t in dil_grads], "ga": dga, "gb": dgb})
    g_in = _matmul(h1, dproj, ta=True, by_shard=True, out_dtype=bf16, name="mm_g_in", tm=D, tn=W_IN_PAD, tk=SEQ,
                   after=small_token)
    mix_tags = ["in"]
    mix_pair, mix_pair_token = _exchange_start("pair", [g_in], "pair_start_mixer")

    grad_x, dsh_m, dsc_m, dg_mix = _first_bwd(dproj, w_in_s, mix_pair_token, x2d, dx1, modv, g_mix)

    pad_lane = lambda t: jnp.pad(t, ((0, 0), (0, D - t.shape[1])))
    small = jnp.concatenate([dsh_m, dsc_m, dga_m, dsh_f, dsc_f, dga_f, dg_mix, dg_ffn, dg_final,
                             pad_lane(db_fgate), loss_lanes, jnp.zeros((SMALL_ROWS - 11, D), f32)], axis=0)
    small_all = _all_gather(small, "gather_small")
    mix_state, mix_token = pair_done(mix_pair, small_all, mix_tags, "mixer")

    small_sum, loss_row = _small_reduce(small_all, mix_token)
    dmod_all = small_all[:, :6, :].reshape(N_DEV, 6 * D)
    g_w_ada = _ada_bwd(c_all, lax.dynamic_slice(dmod_all, (0, dev * ada_cols), (N_DEV, ada_cols)))
    s_gu_t, s_d = from_chips(ffn_state, small_sum, ffn_tags, "ffn", [True, False])
    s_small, = from_chips(small_state, small_sum, ["small"], "small")

    loss = loss_row[0, 0]
    g = {
        "w_ada": g_w_ada[None], "b_ada": small_sum[0:6].reshape(1, 6 * D), "g_mix": small_sum[6:7],
        "b_fgate": small_sum[9:10, :N_FOX_HEADS], "g_ffn": small_sum[7:8], "w_ffn_gate": s_gu_t[:W_FF_SH],
        "w_ffn_up": s_gu_t[FF_PAD:FF_PAD + W_FF_SH], "w_ffn_down": s_d[None, :W_FF_SH],
        "g_final": small_sum[8], "w_br_a": s_small[:rows_a].reshape(1, FOX_W, W_BR_SH),
        "w_br_b": s_small[rows_a:rows_a + rows_b].reshape(1, DIL_OUT_W, W_BR_SH), "w_out": s_small[None, rows_a + rows_b:],
    }
    w = {"w_ada": w_ada, "b_ada": b_ada, "g_mix": g_mix, "w_in": w_in, "b_fgate": b_fgate, "w_br_a": w_br_a,
         "w_br_b": w_br_b, "w_out": w_out, "g_ffn": g_ffn, "w_ffn_gate": w_ffn_gate, "w_ffn_up": w_ffn_up,
         "w_ffn_down": w_ffn_down, "g_final": g_final}
    m = {"w_ada": m_w_ada, "b_ada": m_b_ada, "g_mix": m_g_mix, "w_in": m_w_in, "b_fgate": m_b_fgate,
         "w_br_a": m_w_br_a, "w_br_b": m_w_br_b, "w_out": m_w_out, "g_ffn": m_g_ffn, "w_ffn_gate": m_w_ffn_gate,
         "w_ffn_up": m_w_ffn_up, "w_ffn_down": m_w_ffn_down, "g_final": m_g_final}
    v = {"w_ada": v_w_ada, "b_ada": v_b_ada, "g_mix": v_g_mix, "w_in": v_w_in, "b_fgate": v_b_fgate,
         "w_br_a": v_w_br_a, "w_br_b": v_w_br_b, "w_out": v_w_out, "g_ffn": v_g_ffn, "w_ffn_gate": v_w_ffn_gate,
         "w_ffn_up": v_w_ffn_up, "w_ffn_down": v_w_ffn_down, "g_final": v_g_final}
    names = list(w)
    delta, new_m, new_v = {}, {}, {}

    transposed = ("w_in", "w_ffn_gate", "w_ffn_up")

    def update(n):
        shape = w[n].shape
        if n in transposed:
            g_t = g[n]
            dl, mn, vn = _adamw(w[n][0].T, g_t, m[n][0].T, v[n][0].T, "adamw_" + n)
            g[n], delta[n], new_m[n], new_v[n] = g_t.T[None], dl.T[None], mn.T[None], vn.T[None]
            return
        two_d = (lambda t: t.reshape(shape[-2:])) if len(shape) == 3 else (lambda t: t)
        dl, mn, vn = _adamw(two_d(w[n]), two_d(g[n]), two_d(m[n]), two_d(v[n]), "adamw_" + n)
        delta[n], new_m[n], new_v[n] = dl.reshape(shape), mn.reshape(shape), vn.reshape(shape)

    for n in list(g):
        update(n)
    done = sum(delta[n].reshape(-1)[:N_FOX_HEADS] for n in g)
    s_in_t, = from_chips(mix_state, done, mix_tags, "mixer", [True])
    g["w_in"] = s_in_t[:W_IN_SH]
    update("w_in")

    return (loss, grad_x[None], *[g[n] for n in names], *[delta[n] for n in names],
            *[new_m[n] for n in names], *[new_v[n] for n in names])
```

```python
import functools

import jax
import jax.numpy as jnp
import numpy as np
from jax import lax
from jax.experimental import pallas as pl
from jax.experimental.pallas import tpu as pltpu

f32 = jnp.float32
bf16 = jnp.bfloat16
SDS = jax.ShapeDtypeStruct
MESH = pl.DeviceIdType.MESH

N_DEV = 8
D = 1024
SEQ = 2048
HEAD_DIM = 64
N_FOX_HEADS = 8
FOX_W = 512
DIL_W = 768
DIL_OUT_W = 256
ROT_DIM = 16
ROPE_THETA = 500000.0
D_FF = 2816
IN_COLS = 5896
EPS = 1e-6
NEG = -1e30
ATT_SCALE = HEAD_DIM ** -0.5

ADAM_LR = 0.001
ADAM_B1 = 0.9
ADAM_B2 = 0.999
ADAM_EPS = 1e-08
ADAM_WD = 0.01
ADAM_STEP = 10

C_GA, C_GB, C_QB, C_KB, C_VB, C_QA, C_KA, C_VA, C_F = 0, 1024, 2304, 3072, 3840, 4608, 5120, 5632, 6144
PROJ_W = 6272
LANES = 128
VMEM_LIMIT = 52 * 1024 * 1024

W_IN_SH, W_IN_PAD = IN_COLS // N_DEV, 768
W_BR_SH = D // N_DEV
W_FF_SH, FF_PAD = D_FF // N_DEV, 384
FF_HID = N_DEV * FF_PAD
SMALL_ROWS = 16


def _params(sem=None):
    if sem is None:
        return pltpu.CompilerParams(vmem_limit_bytes=VMEM_LIMIT)
    return pltpu.CompilerParams(dimension_semantics=sem, vmem_limit_bytes=VMEM_LIMIT)


def _rowwise(fn, name, tiled, vecs, outs, reds=(), tile=256):
    nt, nv, no = len(tiled), len(vecs), len(outs)
    rows = tiled[0][0].shape[0]
    assert rows % tile == 0

    def body(*refs):
        tin = [r[...] for r in refs[:nt]]
        vin = [r[...] for r in refs[nt:nt + nv]]
        orefs = refs[nt + nv:nt + nv + no]
        rrefs = refs[nt + nv + no:]
        touts, routs = fn(tin, vin)
        for r, t in zip(orefs, touts, strict=True):
            r[...] = t.astype(r.dtype)
        if rrefs:
            @pl.when(pl.program_id(0) == 0)
            def _():
                for r in rrefs:
                    r[...] = jnp.zeros_like(r)
            for r, t in zip(rrefs, routs, strict=True):
                r[...] += t

    def col_map(cb):
        return lambda i: (i, cb)

    def whole_map(nd):
        return lambda i: (0,) * nd

    in_specs = [pl.BlockSpec((tile, w), col_map(cb)) for (_, w, cb) in tiled]
    in_specs += [pl.BlockSpec(v.shape, whole_map(v.ndim)) for v in vecs]
    out_specs = [pl.BlockSpec((tile, w), lambda i: (i, 0)) for (w, _) in outs]
    out_specs += [pl.BlockSpec((1, w), lambda i: (0, 0)) for w in reds]
    out_shape = [SDS((rows, w), dt) for (w, dt) in outs] + [SDS((1, w), f32) for w in reds]
    res = pl.pallas_call(
        body, grid=(rows // tile,), in_specs=in_specs, out_specs=out_specs, out_shape=out_shape, name=name,
        compiler_params=_params(("arbitrary",)),
    )(*[t[0] for t in tiled], *vecs)
    return res


def _matmul(a, b, *, ta=False, tb=False, out_dtype=f32, name, tm, tn, tk, by_shard=False, after=None):
    m, k = (a.shape[1], a.shape[0]) if ta else a.shape
    if by_shard and not ta:
        n, kb = (b.shape[1], N_DEV * b.shape[2]) if tb else (N_DEV * b.shape[2], b.shape[1])
        assert (tk if tb else tn) == b.shape[2]
    else:
        n, kb = (b.shape[0], b.shape[1]) if tb else (b.shape[1], b.shape[0])
    assert kb == k and m % tm == 0 and n % tn == 0 and k % tk == 0
    nk = k // tk
    dims = (((0 if ta else 1,), (1 if tb else 0,)), ((), ()))
    b_stacked = by_shard and not ta
    o_stacked = by_shard and ta

    def body(a_ref, b_ref, *rest):
        o_ref, *acc = rest[1:] if after is not None else rest
        bv = b_ref[0] if b_stacked else b_ref[...]
        p = lax.dot_general(a_ref[...].astype(bf16), bv.astype(bf16), dims, preferred_element_type=f32)

        def put(val):
            if o_stacked:
                o_ref[0] = val.astype(o_ref.dtype)
            else:
                o_ref[...] = val.astype(o_ref.dtype)

        if nk == 1:
            put(p)
        else:
            acc_ref, = acc
            kk = pl.program_id(2)

            @pl.when(kk == 0)
            def _():
                acc_ref[...] = p

            @pl.when(kk > 0)
            def _():
                acc_ref[...] += p

            @pl.when(kk == nk - 1)
            def _():
                put(acc_ref[...])

    a_spec = pl.BlockSpec((tk, tm), lambda i, j, kk: (kk, i)) if ta else pl.BlockSpec((tm, tk), lambda i, j, kk: (i, kk))
    if b_stacked and tb:
        b_spec = pl.BlockSpec((1, tn, tk), lambda i, j, kk: (kk, j, 0))
    elif b_stacked:
        b_spec = pl.BlockSpec((1, tk, tn), lambda i, j, kk: (j, kk, 0))
    elif tb:
        b_spec = pl.BlockSpec((tn, tk), lambda i, j, kk: (j, kk))
    else:
        b_spec = pl.BlockSpec((tk, tn), lambda i, j, kk: (kk, j))
    if o_stacked:
        assert tn == n // N_DEV
        out_spec = pl.BlockSpec((1, tm, tn), lambda i, j, kk: (j, i, 0))
        out_shape = SDS((N_DEV, m, tn), out_dtype)
    else:
        out_spec = pl.BlockSpec((tm, tn), lambda i, j, kk: (i, j))
        out_shape = SDS((m, n), out_dtype)
    extra_specs, extra = ([pl.BlockSpec(memory_space=pl.ANY)], [after]) if after is not None else ([], [])
    return pl.pallas_call(
        body, grid=(m // tm, n // tn, nk), in_specs=[a_spec, b_spec] + extra_specs, out_specs=out_spec,
        out_shape=out_shape, name=name, scratch_shapes=[pltpu.VMEM((tm, tn), f32)] if nk > 1 else [],
        compiler_params=_params(("parallel", "parallel", "arbitrary")),
    )(a, b, *extra)


def _matmul_stack(a, b, *, ta=False, tb=False, out_dtype=f32, name):
    def lanes(ref):
        return jnp.concatenate([ref[j] for j in range(N_DEV)], axis=1).astype(bf16)

    if ta:
        w = b.shape[1] // N_DEV

        def body(a_ref, b_ref, o_ref):
            p = _tn(a_ref[...].astype(bf16), b_ref[...].astype(bf16))
            for j in range(N_DEV):
                o_ref[j] = p[:, j * w:(j + 1) * w].astype(o_ref.dtype)

        return pl.pallas_call(body, out_shape=SDS((N_DEV, a.shape[1], w), out_dtype), name=name,
                              compiler_params=_params())(a, b)

    m, half = a.shape[0], a.shape[0] // 2
    n = b.shape[1] if tb else N_DEV * b.shape[2]

    def body(a_ref, b_ref, o_ref):
        av = a_ref[...].astype(bf16)
        o_ref[...] = (_nt(av, lanes(b_ref)) if tb else jnp.dot(av, lanes(b_ref), preferred_element_type=f32)
                      ).astype(o_ref.dtype)

    return pl.pallas_call(
        body, grid=(2,), in_specs=[pl.BlockSpec((half, a.shape[1]), lambda i: (i, 0)),
                                   pl.BlockSpec(b.shape, lambda i: (0, 0, 0))],
        out_specs=pl.BlockSpec((half, n), lambda i: (i, 0)), out_shape=SDS((m, n), out_dtype), name=name,
        compiler_params=_params(("parallel",)),
    )(a, b)


def _matmul_rows(form, a, b, after, fn, tiled, vecs, outs, reds, *, name, tm=512):
    norm = lambda ts: [t if isinstance(t, tuple) else (t, t.shape[1], 0) for t in ts]
    make, sources = a if isinstance(a, tuple) else (None, [a])
    sources, tiled = norm(sources), norm(tiled)
    m, k = sources[0][0].shape[0], (b.shape[0] if form == "nn" else b.shape[-1] * (N_DEV if form == "nt_stack" else 1))
    assert m % tm == 0
    ns, nt, nv, no = len(sources), len(tiled), len(vecs), len(outs)

    def body(*refs):
        src_refs, b_ref, refs = refs[:ns], refs[ns], refs[ns + 2:]
        if make is None:
            lhs = lambda lo, hi: src_refs[0][:, lo:hi]
        else:
            made = make([r[...] for r in src_refs]).astype(bf16)
            lhs = lambda lo, hi: made[:, lo:hi]
        if form == "nt_stack":
            w = b.shape[2]
            acc = _nt(lhs(0, w), b_ref[0])
            for j in range(1, N_DEV):
                acc = acc + _nt(lhs(j * w, (j + 1) * w), b_ref[j])
        elif form == "nt":
            acc = _nt(lhs(0, k), b_ref[...])
        else:
            acc = jnp.dot(lhs(0, k), b_ref[...], preferred_element_type=f32)
        if make is not None:
            refs[nt + nv][...] = made
            refs = refs[:nt + nv] + refs[nt + nv + 1:]
        orefs, rrefs = refs[nt + nv:nt + nv + no], refs[nt + nv + no:]
        touts, routs = fn([acc] + [r[...] for r in refs[:nt]], [r[...] for r in refs[nt:nt + nv]])
        for r, t in zip(orefs, touts, strict=True):
            r[...] = t.astype(r.dtype)

        @pl.when(pl.program_id(0) == 0)
        def _():
            for r in rrefs:
                r[...] = jnp.zeros_like(r)
        for r, t in zip(rrefs, routs, strict=True):
            r[...] += t

    def whole_map(nd):
        return lambda i: (0,) * nd

    def rows(width, cb=0):
        return pl.BlockSpec((tm, width), lambda i: (i, cb))

    made_out = [(k, bf16)] if make is not None else []
    return pl.pallas_call(
        body, grid=(m // tm,),
        in_specs=[rows(width, cb) for _, width, cb in sources]
        + [pl.BlockSpec(b.shape, whole_map(b.ndim), pipeline_mode=pl.Buffered(1)), pl.BlockSpec(memory_space=pl.ANY)]
        + [rows(width, cb) for _, width, cb in tiled] + [pl.BlockSpec(v.shape, whole_map(v.ndim)) for v in vecs],
        out_specs=[rows(width) for width, _ in made_out + list(outs)]
        + [pl.BlockSpec((1, width), lambda i: (0, 0)) for width in reds],
        out_shape=[SDS((m, width), dt) for width, dt in made_out + list(outs)]
        + [SDS((1, width), f32) for width in reds], name=name,
        compiler_params=_params(("arbitrary",)),
    )(*[t[0] for t in sources], b, after, *[t[0] for t in tiled], *vecs)


def _rms(x):
    r = lax.rsqrt(jnp.mean(x * x, axis=-1, keepdims=True) + EPS)
    return r, x * r


def _rms_bwd(r, xn, dxn):
    return r * (dxn - xn * jnp.mean(dxn * xn, axis=-1, keepdims=True))


def _colsum(t):
    return jnp.sum(t, axis=0, keepdims=True)


def _sigmoid(x):
    return 0.5 * jnp.tanh(0.5 * x) + 0.5


def _modulated_norm(x, g, shift, scale):
    _, xn = _rms(x)
    return (xn * g) * (1.0 + scale) + shift


def _pre1(x, modv, g_mix):
    def fn(t, v):
        (xt,), (mv, g) = t, v
        return [_modulated_norm(xt, g, mv[0:1], mv[1:2])], []
    return _rowwise(fn, "pre1", [(x, D, 0)], [modv, g_mix], [(D, bf16)])[0]


def _post1(ya, yb, proj, w_o, x, modv, g_ffn):
    def merge(t):
        ya_t, yb_t, ga, gb = t
        return _sigmoid(ga) * ya_t + _sigmoid(gb) * yb_t

    def fn(t, v):
        (mt, xt), (mv, g) = t, v
        x1 = xt + mv[2:3] * mt
        return [mt, x1, _modulated_norm(x1, g, mv[3:4], mv[4:5])], []
    return _matmul_rows("nn", (merge, [ya, yb, (proj, D, C_GA // D), (proj, D, C_GB // D)]), w_o, x, fn, [x],
                        [modv, g_ffn], [(D, f32), (D, f32), (D, bf16)], [], name="post1")


def _ffn_in(h, w_stack):
    def body(h_ref, w_ref, act_ref, au_ref):
        p = jnp.dot(h_ref[...], w_ref[0], preferred_element_type=f32)
        a, u = p[:, :FF_PAD], p[:, FF_PAD:]
        act_ref[...] = (a * _sigmoid(a) * u).astype(act_ref.dtype)
        au_ref[...] = p.astype(au_ref.dtype)

    return pl.pallas_call(
        body, grid=(N_DEV,),
        in_specs=[pl.BlockSpec((SEQ, D), lambda j: (0, 0)), pl.BlockSpec((1, D, 2 * FF_PAD), lambda j: (j, 0, 0))],
        out_specs=[pl.BlockSpec((SEQ, FF_PAD), lambda j: (0, j)), pl.BlockSpec((SEQ, 2 * FF_PAD), lambda j: (0, j))],
        out_shape=(SDS((SEQ, FF_HID), bf16), SDS((SEQ, 2 * FF_HID), bf16)), name="ffn_in",
        compiler_params=_params(("parallel",)),
    )(h, w_stack)


def _ffn_bwd_in(dff, w_down_stack, au):
    def body(d_ref, w_ref, au_ref, o_ref):
        dact = _nt(d_ref[...], w_ref[0])
        p = au_ref[...].astype(f32)
        a, u = p[:, :FF_PAD], p[:, FF_PAD:]
        sg = _sigmoid(a)
        o_ref[...] = jnp.concatenate([dact * u * (sg * (1.0 + a * (1.0 - sg))), dact * (a * sg)],
                                     axis=1).astype(o_ref.dtype)

    return pl.pallas_call(
        body, grid=(N_DEV,),
        in_specs=[pl.BlockSpec((SEQ, D), lambda j: (0, 0)), pl.BlockSpec((1, FF_PAD, D), lambda j: (j, 0, 0)),
                  pl.BlockSpec((SEQ, 2 * FF_PAD), lambda j: (0, j))],
        out_specs=pl.BlockSpec((SEQ, 2 * FF_PAD), lambda j: (0, j)),
        out_shape=SDS((SEQ, 2 * FF_HID), bf16), name="ffn_bwd_in", compiler_params=_params(("parallel",)),
    )(dff, w_down_stack, au)


def _final(act, w_down, x1, target, modv, g_final):
    def fn(t, v):
        (fft, x1t, tgt), (mv, g) = t, v
        x2 = x1t + mv[5:6] * fft
        r, xn = _rms(x2)
        err = xn * g - tgt
        dy = err * (1.0 / D)
        dx2 = _rms_bwd(r, xn, dy * g)
        return [dx2, dx2 * mv[5:6]], [_colsum(dy * xn), _colsum(dx2 * fft), _colsum(err * err) * (0.5 / D)]
    return _matmul_rows("nn", act, w_down, x1, fn, [x1, target], [modv, g_final], [(D, f32), (D, bf16)], [D, D, D],
                        name="final")


def _mid_bwd(dau, w_stack, after, x1, dx2, mix, modv, g_ffn):
    def fn(t, v):
        (dh, x1t, dx2t, mt), (mv, g) = t, v
        r, xn = _rms(x1t)
        dn = dh * (1.0 + mv[4:5])
        dx1 = dx2t + _rms_bwd(r, xn, dn * g)
        return [dx1, dx1 * mv[2:3]], [_colsum(dh), _colsum(dh * (xn * g)), _colsum(dn * xn), _colsum(dx1 * mt)]
    return _matmul_rows("nt_stack", dau, w_stack, after, fn, [x1, dx2, mix], [modv, g_ffn], [(D, f32), (D, bf16)],
                        [D, D, D, D], name="mid_bwd")


def _first_bwd(dproj, w_stack, after, x, dx1, modv, g_mix):
    def fn(t, v):
        (dh, xt, dx1t), (mv, g) = t, v
        r, xn = _rms(xt)
        dn = dh * (1.0 + mv[1:2])
        return [dx1t + _rms_bwd(r, xn, dn * g)], [_colsum(dh), _colsum(dh * (xn * g)), _colsum(dn * xn)]
    return _matmul_rows("nt_stack", dproj, w_stack, after, fn, [x, dx1], [modv, g_mix], [(D, f32)], [D, D, D],
                        name="first_bwd")


def _merge_bwd(dmix, w_o, after, ya, yb, proj):
    def fn(t, v):
        dm, ya_t, yb_t, ga, gb = t
        sa, sb = _sigmoid(ga), _sigmoid(gb)
        return [dm * sa, dm * sb, dm * ya_t * (sa * (1.0 - sa)), dm * yb_t * (sb * (1.0 - sb))], []
    return _matmul_rows("nt", dmix, w_o, after, fn, [ya, yb, (proj, D, C_GA // D), (proj, D, C_GB // D)], [],
                        [(D, bf16), (D, bf16), (D, bf16), (D, bf16)], [], name="merge_bwd")


def _rope_tables():
    half = ROT_DIM // 2
    pos = np.arange(SEQ, dtype=np.float32)
    inv_freq = np.float32(ROPE_THETA) ** (-np.arange(0, ROT_DIM, 2, dtype=np.float32) / np.float32(ROT_DIM))
    ang = pos[:, None] * inv_freq[None, :].astype(np.float32)
    cos, sin = np.cos(ang).astype(np.float32), np.sin(ang).astype(np.float32)
    pad = np.zeros((SEQ, HEAD_DIM - ROT_DIM), np.float32)
    zero = np.zeros((SEQ, half), np.float32)
    c_head = np.concatenate([cos, cos, pad + 1.0], axis=1)
    lo_head = np.concatenate([-sin, zero, pad], axis=1)
    hi_head = np.concatenate([zero, sin, pad], axis=1)
    return tuple(jnp.asarray(np.concatenate([t, t], axis=1)) for t in (c_head, lo_head, hi_head))


def _over_heads(tables):
    return [jnp.tile(t, (1, DIL_W // LANES)) for t in tables]


def _rope_fwd(proj, tables):
    half = ROT_DIM // 2

    def fn(t, v):
        q, k = t[:2]
        c, lo, hi = _over_heads(t[2:])
        rot = lambda z: z * c + pltpu.roll(z, DIL_W - half, 1) * lo + pltpu.roll(z, half, 1) * hi
        return [rot(q) * ATT_SCALE, rot(k)], []
    return _rowwise(fn, "rope_fwd", [(proj, DIL_W, C_QB // DIL_W), (proj, DIL_W, C_KB // DIL_W)]
                    + [(tb, LANES, 0) for tb in tables], [], [(DIL_W, f32)] * 2)


def _rope_bwd(dqs, dks, tables):
    half = ROT_DIM // 2

    def fn(t, v):
        dq_t, dk_t = jnp.concatenate(t[:N_GROUPS], axis=1), jnp.concatenate(t[N_GROUPS:2 * N_GROUPS], axis=1)
        c, lo, hi = _over_heads(t[2 * N_GROUPS:])
        rot_t = lambda z: z * c + pltpu.roll(z * lo, half, 1) + pltpu.roll(z * hi, DIL_W - half, 1)
        return [rot_t(dq_t), rot_t(dk_t)], []
    return _rowwise(fn, "rope_bwd", [(a, DIL_OUT_W, 0) for a in (*dqs, *dks)] + [(tb, LANES, 0) for tb in tables],
                    [], [(DIL_W, bf16), (DIL_W, bf16)])


def _head_bcast_sum(d):
    lane = lax.broadcasted_iota(jnp.int32, d.shape, 1)
    out = jnp.zeros_like(d)
    for h in range(d.shape[1] // HEAD_DIM):
        sel = (lane >= h * HEAD_DIM) & (lane < (h + 1) * HEAD_DIM)
        out = jnp.where(sel, jnp.sum(jnp.where(sel, d, 0.0), axis=1, keepdims=True), out)
    return out


def _dil_combine(outs, lses):
    def fn(t, v):
        o0, o1, o2, l0, l1, l2 = t
        m = jnp.maximum(jnp.maximum(l0, l1), l2)
        w0, w1, w2 = jnp.exp(l0 - m), jnp.exp(l1 - m), jnp.exp(l2 - m)
        tot = w0 + w1 + w2
        return [(w0 * o0 + w1 * o1 + w2 * o2) / tot, m + jnp.log(tot)], []
    w = DIL_OUT_W
    return _rowwise(fn, "dil_combine", [(t, w, 0) for t in (*outs, *lses)], [], [(w, f32), (w, f32)])


def _dil_delta(dyb_h, yb_h):
    def fn(t, v):
        return [_head_bcast_sum(t[0] * t[1])], []
    return _rowwise(fn, "dil_delta", [(dyb_h, DIL_OUT_W, 0), (yb_h, DIL_OUT_W, 0)], [], [(DIL_OUT_W, f32)])[0]


def _adamw_math(wt, gt, mt, vt):
    mn = ADAM_B1 * mt + (1.0 - ADAM_B1) * gt
    vn = ADAM_B2 * vt + (1.0 - ADAM_B2) * (gt * gt)
    m_hat = mn / (1.0 - ADAM_B1 ** ADAM_STEP)
    v_hat = vn / (1.0 - ADAM_B2 ** ADAM_STEP)
    return -ADAM_LR * (m_hat / (jnp.sqrt(v_hat) + ADAM_EPS) + ADAM_WD * wt), mn, vn


def _adamw(w, g, m, v, name):
    shape = w.shape
    if w.ndim == 1:
        w, g, m, v = (t.reshape(1, -1) for t in (w, g, m, v))
    rows, cols = w.shape
    if rows % 8 and rows > 8:
        return _adamw_by_cols(w, g, m, v, name)
    tile = 256 if rows % 256 == 0 and rows > 512 else rows

    def fn(t, _):
        return list(_adamw_math(*t)), []
    delta, mn, vn = _rowwise(fn, name, [(w, cols, 0), (g, cols, 0), (m, cols, 0), (v, cols, 0)], [],
                             [(cols, f32)] * 3, tile=tile)
    return delta.reshape(shape), mn.reshape(shape), vn.reshape(shape)


def _adamw_by_cols(w, g, m, v, name, tile=256):
    rows, cols = w.shape

    def body(w_ref, g_ref, m_ref, v_ref, d_ref, mn_ref, vn_ref):
        d_ref[...], mn_ref[...], vn_ref[...] = _adamw_math(w_ref[...], g_ref[...], m_ref[...], v_ref[...])

    spec = pl.BlockSpec((rows, tile), lambda j: (0, j))
    return pl.pallas_call(body, grid=(cols // tile,), in_specs=[spec] * 4, out_specs=[spec] * 3,
                          out_shape=[SDS((rows, cols), f32)] * 3, name=name,
                          compiler_params=_params(("parallel",)))(w, g, m, v)


def _ada_fwd(c_all, w_shard, b_shard):
    def body(c_ref, w_ref, b_ref, o_ref):
        cv = c_ref[...]
        sc = (cv * _sigmoid(cv)).astype(bf16)
        o_ref[...] = jnp.dot(sc, w_ref[...].astype(bf16), preferred_element_type=f32) + b_ref[...]
    return pl.pallas_call(body, out_shape=SDS((N_DEV, w_shard.shape[1]), f32), name="ada_fwd",
                          compiler_params=_params())(c_all, w_shard, b_shard)


def _ada_bwd(c_all, dmod_cols):
    def body(c_ref, d_ref, o_ref):
        cv = c_ref[...]
        sc = cv * _sigmoid(cv)
        o_ref[...] = lax.dot_general(sc, d_ref[...], (((0,), (0,)), ((), ())), precision=lax.Precision.HIGHEST,
                                     preferred_element_type=f32)
    return pl.pallas_call(body, out_shape=SDS((D, dmod_cols.shape[1]), f32), name="ada_bwd",
                          compiler_params=_params())(c_all, dmod_cols)


def _small_reduce(gathered, after):
    def body(g_ref, after_ref, o_ref, loss_ref):
        acc = g_ref[0]
        for d in range(1, N_DEV):
            acc = acc + g_ref[d]
        o_ref[...] = acc
        loss_ref[...] = jnp.zeros((1, LANES), f32) + jnp.sum(acc[10:11, :])
    return pl.pallas_call(body, out_shape=(SDS((SMALL_ROWS, D), f32), SDS((1, LANES), f32)), name="small_reduce",
                          in_specs=[pl.BlockSpec(memory_space=pltpu.VMEM), pl.BlockSpec(memory_space=pl.ANY)],
                          compiler_params=_params())(gathered, after)


FOX_BLK = 512
CUM_BLK = 128


def _fold_lanes(t, op):
    out = t[:, :LANES]
    for j in range(1, t.shape[1] // LANES):
        out = op(out, t[:, j * LANES:(j + 1) * LANES])
    return out


def _fox_gate_fwd(proj, b_pad):
    nblk = SEQ // CUM_BLK

    def body(f_ref, b_ref, col_ref):
        r = lax.broadcasted_iota(jnp.int32, (CUM_BLK, CUM_BLK), 0)
        c = lax.broadcasted_iota(jnp.int32, (CUM_BLK, CUM_BLK), 1)
        tri = (r >= c).astype(f32)
        carry = jnp.zeros((1, LANES), f32)
        for blk in range(nblk):
            z = f_ref[blk * CUM_BLK:(blk + 1) * CUM_BLK, :] + b_ref[...]
            logf = jnp.minimum(z, 0.0) - jnp.log1p(jnp.exp(-jnp.abs(z)))
            cs = jnp.dot(tri, logf, precision=lax.Precision.HIGHEST, preferred_element_type=f32) + carry
            col_ref[blk * CUM_BLK:(blk + 1) * CUM_BLK, :] = cs
            carry = cs[CUM_BLK - 1:CUM_BLK, :]

    return pl.pallas_call(
        body, grid=(1,), in_specs=[pl.BlockSpec((SEQ, LANES), lambda i: (0, C_F // LANES)),
                                   pl.BlockSpec((1, LANES), lambda i: (0, 0))],
        out_specs=pl.BlockSpec((SEQ, LANES), lambda i: (0, 0)),
        out_shape=SDS((SEQ, LANES), f32), name="fox_gate_fwd",
        compiler_params=_params(("arbitrary",)),
    )(proj, b_pad)


def _fox_gate_bwd(dF_row, proj, b_pad):
    nblk = SEQ // CUM_BLK

    def body(d_ref, f_ref, b_ref, df_ref, db_ref, col_ref):
        r = lax.broadcasted_iota(jnp.int32, (CUM_BLK, CUM_BLK), 0)
        c = lax.broadcasted_iota(jnp.int32, (CUM_BLK, CUM_BLK), 1)
        tri = (r <= c).astype(f32)
        lane = lax.broadcasted_iota(jnp.int32, (CUM_BLK, LANES), 1)
        col_ref[...] = d_ref[...].T
        carry = jnp.zeros((1, LANES), f32)
        total = jnp.zeros((1, LANES), f32)
        for blk in reversed(range(nblk)):
            rows = slice(blk * CUM_BLK, (blk + 1) * CUM_BLK)
            cs = jnp.dot(tri, col_ref[rows, :], precision=lax.Precision.HIGHEST, preferred_element_type=f32) + carry
            carry = cs[0:1, :]
            z = f_ref[rows, :] + b_ref[...]
            df = jnp.where(lane < N_FOX_HEADS, cs * _sigmoid(-z), 0.0)
            df_ref[rows, :] = df.astype(df_ref.dtype)
            total = total + _colsum(df)
        db_ref[...] = total

    return pl.pallas_call(
        body, grid=(1,), in_specs=[pl.BlockSpec((LANES, SEQ), lambda i: (0, 0)),
                                   pl.BlockSpec((SEQ, LANES), lambda i: (0, C_F // LANES)),
                                   pl.BlockSpec((1, LANES), lambda i: (0, 0))],
        out_specs=[pl.BlockSpec((SEQ, LANES), lambda i: (0, 0)), pl.BlockSpec((1, LANES), lambda i: (0, 0))],
        out_shape=(SDS((SEQ, LANES), bf16), SDS((1, LANES), f32)), name="fox_gate_bwd",
        scratch_shapes=[pltpu.VMEM((SEQ, LANES), f32)],
        compiler_params=_params(("arbitrary",)),
    )(dF_row, proj, b_pad)


def _nt(a, b):
    return lax.dot_general(a, b, (((1,), (1,)), ((), ())), preferred_element_type=f32)


def _tn(a, b):
    return lax.dot_general(a, b, (((0,), (0,)), ((), ())), preferred_element_type=f32)


def _fox_prep(proj, f_col):
    def fn(t, v):
        q, k, vv, fc = t
        lane = lax.broadcasted_iota(jnp.int32, (q.shape[0], LANES), 1)
        qs, ks = [], []
        for h in range(N_FOX_HEADS):
            pair, pos = divmod(h, 2)
            own = (lane >= pos * HEAD_DIM) & (lane < (pos + 1) * HEAD_DIM)
            base = (1 - pos) * HEAD_DIM
            f = fc[:, h:h + 1]
            hi = f.astype(bf16).astype(f32)
            mid = (f - hi).astype(bf16).astype(f32)
            lo = (f - hi) - mid
            one = jnp.ones_like(f)
            qa = jnp.where(own, q[:, pair * LANES:(pair + 1) * LANES] * ATT_SCALE, 0.0)
            ka = k[:, pair * LANES:(pair + 1) * LANES]
            for idx, (qv, kv) in enumerate([(hi, one), (mid, one), (lo, one), (one, -hi), (one, -mid), (one, -lo)]):
                sel = lane == base + idx
                qa = jnp.where(sel, qv, qa)
                ka = jnp.where(sel, kv, ka)
            qs.append(qa)
            ks.append(ka)
        return [jnp.concatenate(qs, axis=1), jnp.concatenate(ks, axis=1), vv], []
    w = N_FOX_HEADS * LANES
    return _rowwise(fn, "fox_prep", [(proj, FOX_W, C_QA // FOX_W), (proj, FOX_W, C_KA // FOX_W),
                                     (proj, FOX_W, C_VA // FOX_W), (f_col, LANES, 0)], [],
                    [(w, bf16), (w, bf16), (FOX_W, bf16)])


def _fox_fwd(q_aug, k_aug, v):
    blk = FOX_BLK
    npair = FOX_W // LANES

    def body(q_ref, k_ref, v_ref, o_ref, max_ref, sum_ref, s_scr):
        i = pl.program_id(1)
        tri = lax.broadcasted_iota(jnp.int32, (blk, blk), 0) >= lax.broadcasted_iota(jnp.int32, (blk, blk), 1)
        qh = [q_ref[:, h * LANES:(h + 1) * LANES] for h in range(2)]

        def logits(c, masked):
            off = pl.multiple_of(c * blk, blk)
            tops = []
            for h in range(2):
                s = _nt(qh[h], k_ref[pl.ds(off, blk), h * LANES:(h + 1) * LANES])
                if masked:
                    s = jnp.where(tri, s, NEG)
                s_scr[h, :, pl.ds(off, blk)] = s
                tops.append(_fold_lanes(s, jnp.maximum))
            return tops

        def pass_a(c, m):
            return tuple(jnp.maximum(a, b) for a, b in zip(m, logits(c, False)))

        m = lax.fori_loop(0, i, pass_a, tuple(jnp.full((blk, LANES), NEG, f32) for _ in range(2)))
        mx = [jnp.max(jnp.maximum(a, b), axis=1, keepdims=True) for a, b in zip(m, logits(i, True))]

        def pass_b(c, carry):
            off = pl.multiple_of(c * blk, blk)
            vv = v_ref[pl.ds(off, blk), :]
            new = []
            for h in range(2):
                l, acc = carry[h]
                p = jnp.exp(s_scr[h, :, pl.ds(off, blk)] - mx[h]).astype(bf16)
                new.append((l + _fold_lanes(p.astype(f32), jnp.add), acc + jnp.dot(p, vv, preferred_element_type=f32)))
            return tuple(new)

        zero = jnp.zeros((blk, LANES), f32)
        (l_a, acc_a), (l_b, acc_b) = lax.fori_loop(0, i + 1, pass_b, ((zero, zero), (zero, zero)))
        l_a = jnp.sum(l_a, axis=1, keepdims=True)
        l_b = jnp.sum(l_b, axis=1, keepdims=True)
        first = lax.broadcasted_iota(jnp.int32, (blk, LANES), 1) < HEAD_DIM
        o_ref[...] = jnp.where(first, acc_a / l_a, acc_b / l_b)
        max_ref[0] = jnp.where(first, mx[0], mx[1])
        sum_ref[0] = jnp.where(first, l_a, l_b)

    return pl.pallas_call(
        body, grid=(npair, SEQ // blk),
        in_specs=[pl.BlockSpec((blk, 2 * LANES), lambda p, i: (i, p)),
                  pl.BlockSpec((SEQ, 2 * LANES), lambda p, i: (0, p)),
                  pl.BlockSpec((SEQ, LANES), lambda p, i: (0, p))],
        out_specs=[pl.BlockSpec((blk, LANES), lambda p, i: (i, p))]
        + [pl.BlockSpec((1, blk, LANES), lambda p, i: (p, i, 0))] * 2,
        out_shape=(SDS((SEQ, FOX_W), f32),) + (SDS((npair, SEQ, LANES), f32),) * 2, name="fox_fwd",
        scratch_shapes=[pltpu.VMEM((2, blk, SEQ), f32)],
        compiler_params=_params(("parallel", "arbitrary")),
    )(q_aug, k_aug, v)


def _fox_bwd(q_aug, k_aug, v, do, o, row_max, row_sum, after):
    blk = FOX_BLK
    npair = FOX_W // LANES
    nblk = SEQ // blk

    def body(q_ref, k_ref, v_ref, do_ref, o_ref, max_ref, sum_ref, after_ref, dq_ref, dk_ref, dv_ref, df_ref, dq_acc,
             delta_ref, inv_ref):
        inv_ref[...] = 1.0 / sum_ref[0]
        lane_s = lax.broadcasted_iota(jnp.int32, (SEQ, LANES), 1)
        prod = do_ref[...].astype(bf16).astype(f32) * o_ref[...]
        d_a = jnp.sum(jnp.where(lane_s < HEAD_DIM, prod, 0.0), axis=1, keepdims=True)
        d_b = jnp.sum(jnp.where(lane_s >= HEAD_DIM, prod, 0.0), axis=1, keepdims=True)
        delta_ref[...] = jnp.where(lane_s < HEAD_DIM, d_a, d_b)
        dq_acc[...] = jnp.zeros_like(dq_acc)
        df_ref[...] = jnp.zeros_like(df_ref)
        lane = lax.broadcasted_iota(jnp.int32, (blk, LANES), 1)
        own = [lane < HEAD_DIM, lane >= HEAD_DIM]
        tri = lax.broadcasted_iota(jnp.int32, (blk, blk), 0) >= lax.broadcasted_iota(jnp.int32, (blk, blk), 1)

        def q_slab(qoff, h):
            return q_ref[pl.ds(qoff, blk), h * LANES:(h + 1) * LANES]

        def probs(qoff, h, k_h, masked):
            s = _nt(q_slab(qoff, h), k_h)
            if masked:
                s = jnp.where(tri, s, NEG)
            col = slice(h * HEAD_DIM, h * HEAD_DIM + 1)
            weights = jnp.exp(s - max_ref[0, pl.ds(qoff, blk), col]).astype(bf16).astype(f32)
            return weights * inv_ref[pl.ds(qoff, blk), col]

        def k_slabs(koff):
            return [k_ref[pl.ds(koff, blk), h * LANES:(h + 1) * LANES] for h in range(2)]

        def kv_step(kj, _):
            koff = pl.multiple_of(kj * blk, blk)
            k_aug = k_slabs(koff)
            k_own = [jnp.where(own[h], k_aug[h], jnp.zeros_like(k_aug[h])) for h in range(2)]
            vv = v_ref[pl.ds(koff, blk), :]
            v_own = [jnp.where(own[h], vv, jnp.zeros_like(vv)) for h in range(2)]

            def q_tile(qi, carry, masked):
                qoff = pl.multiple_of(qi * blk, blk)
                dd = do_ref[pl.ds(qoff, blk), :].astype(bf16)
                new, dq_add = [], None
                for h in range(2):
                    dk_h, dv_h, dcol = carry[h]
                    p = probs(qoff, h, k_aug[h], masked)
                    dl = p * (_nt(dd, v_own[h]) - delta_ref[pl.ds(qoff, blk), h * HEAD_DIM:h * HEAD_DIM + 1])
                    dlb = dl.astype(bf16)
                    part = jnp.dot(dlb, k_own[h], preferred_element_type=f32)
                    dq_add = part if dq_add is None else dq_add + part
                    new.append((dk_h + _tn(dlb, q_slab(qoff, h)), dv_h + _tn(p.astype(bf16), dd),
                                dcol + _colsum(dl)))
                dq_acc[pl.ds(qoff, blk), :] += dq_add * ATT_SCALE
                return tuple(new)

            zero = (jnp.zeros((blk, LANES), f32), jnp.zeros((blk, LANES), f32), jnp.zeros((1, blk), f32))
            carry = q_tile(kj, (zero, zero), True)
            (dk_a, dv_a, dcol_a), (dk_b, dv_b, dcol_b) = lax.fori_loop(
                kj + 1, nblk, lambda qi, cr: q_tile(qi, cr, False), carry)
            dk_ref[pl.ds(koff, blk), :] = jnp.where(own[0], dk_a, dk_b).astype(dk_ref.dtype)
            dv_ref[pl.ds(koff, blk), :] = jnp.where(own[0], dv_a, dv_b).astype(dv_ref.dtype)
            df_ref[0, 0:1, pl.ds(koff, blk)] = -dcol_a
            df_ref[0, 1:2, pl.ds(koff, blk)] = -dcol_b
            return 0

        lax.fori_loop(0, nblk, kv_step, 0)
        dq_ref[...] = dq_acc[...].astype(dq_ref.dtype)

    pair_aug = pl.BlockSpec((SEQ, 2 * LANES), lambda p: (0, p))
    slab = pl.BlockSpec((SEQ, LANES), lambda p: (0, p))
    per_pair = pl.BlockSpec((1, SEQ, LANES), lambda p: (p, 0, 0))
    rows = pl.BlockSpec((1, 8, SEQ), lambda p: (p, 0, 0))
    return pl.pallas_call(
        body, grid=(npair,),
        in_specs=[pair_aug, pair_aug, slab, slab, slab, per_pair, per_pair, pl.BlockSpec(memory_space=pl.ANY)],
        out_specs=[slab, slab, slab, rows],
        out_shape=(SDS((SEQ, FOX_W), bf16),) * 3 + (SDS((npair, 8, SEQ), f32),), name="fox_bwd",
        scratch_shapes=[pltpu.VMEM((SEQ, LANES), f32)] * 3,
        compiler_params=_params(("parallel",)),
    )(q_aug, k_aug, v, do, o, row_max, row_sum, after)


DIL_BLK = 128
DILATIONS = (1, 4, 16)
N_GROUPS = len(DILATIONS)
DIL_PAIRS = DIL_OUT_W // LANES


def _dil_blocks(d):
    r1 = lax.broadcasted_iota(jnp.int32, (2 * DIL_BLK, DIL_BLK), 0) & (DIL_BLK - 1)
    c1 = lax.broadcasted_iota(jnp.int32, (2 * DIL_BLK, DIL_BLK), 1)
    r2 = lax.broadcasted_iota(jnp.int32, (2 * DIL_BLK, 2 * DIL_BLK), 0) & (DIL_BLK - 1)
    c2 = lax.broadcasted_iota(jnp.int32, (2 * DIL_BLK, 2 * DIL_BLK), 1)
    band = ((c2 < DIL_BLK) & (c2 >= r2)) | ((c2 >= DIL_BLK) & (c2 - DIL_BLK <= r2))
    out = []
    for r in range(d):
        for b in range(SEQ // d // DIL_BLK):
            rows = pl.ds(r + d * DIL_BLK * b, DIL_BLK, stride=d)
            if b == 0:
                out.append((rows, rows, r1 >= c1))
            else:
                out.append((rows, pl.ds(r + d * DIL_BLK * (b - 1), 2 * DIL_BLK, stride=d), band))
    return out


def _dil_v_spec(g):
    return pl.BlockSpec((SEQ, LANES), lambda p: (0, C_VB // LANES + DIL_PAIRS * g + p))


def _stack_heads(t, first):
    zero = jnp.zeros_like(t)
    return jnp.concatenate([jnp.where(first, t, zero), jnp.where(first, zero, t)], axis=0)


def _dil_fwd(q, k, v, g):
    def body(q_ref, k_ref, v_ref, o_ref, lse_ref):
        first = lax.broadcasted_iota(jnp.int32, (DIL_BLK, LANES), 1) < HEAD_DIM
        for rows, krows, mask in _dil_blocks(DILATIONS[g]):
            qv, kk, vv = q_ref[rows, :].astype(bf16), k_ref[krows, :].astype(bf16), v_ref[krows, :].astype(bf16)
            s = jnp.where(mask, _nt(_stack_heads(qv, first), kk), NEG)
            m = jnp.max(s, axis=1, keepdims=True)
            p = jnp.exp(s - m)
            l = jnp.sum(p, axis=1, keepdims=True)
            out = jnp.dot(p.astype(bf16), vv, preferred_element_type=f32) / l
            lse = m + jnp.log(l)
            o_ref[rows, :] = jnp.where(first, out[:DIL_BLK], out[DIL_BLK:])
            lse_ref[rows, :] = jnp.where(first, lse[:DIL_BLK], lse[DIL_BLK:])

    grouped = pl.BlockSpec((SEQ, LANES), lambda p: (0, DIL_PAIRS * g + p))
    own = pl.BlockSpec((SEQ, LANES), lambda p: (0, p))
    shape = SDS((SEQ, DIL_OUT_W), f32)
    return pl.pallas_call(
        body, grid=(DIL_PAIRS,), in_specs=[grouped, grouped, _dil_v_spec(g)], out_specs=[own] * 2,
        out_shape=(shape, shape),
        name=f"dil_fwd_{DILATIONS[g]}", compiler_params=_params(("parallel",)),
    )(q, k, v)


def _dil_bwd(q, k, v, do, lse, delta, g):
    def body(q_ref, k_ref, v_ref, do_ref, lse_ref, dl_ref, dq_ref, dk_ref, dv_ref):
        first = lax.broadcasted_iota(jnp.int32, (DIL_BLK, LANES), 1) < HEAD_DIM
        dk_ref[...] = jnp.zeros_like(dk_ref)
        dv_ref[...] = jnp.zeros_like(dv_ref)
        for rows, krows, mask in _dil_blocks(DILATIONS[g]):
            qv, kk, vv = q_ref[rows, :].astype(bf16), k_ref[krows, :].astype(bf16), v_ref[krows, :].astype(bf16)
            lsev, delv = lse_ref[rows, :], dl_ref[rows, :]
            q2 = _stack_heads(qv, first)
            do2 = _stack_heads(do_ref[rows, :].astype(bf16), first)
            per_head = lambda t: jnp.concatenate([t[:, 0:1], t[:, HEAD_DIM:HEAD_DIM + 1]], axis=0)
            p = jnp.exp(jnp.where(mask, _nt(q2, kk), NEG) - per_head(lsev))
            dl = (p * (_nt(do2, vv) - per_head(delv))).astype(bf16)
            dq = jnp.dot(dl, kk, preferred_element_type=f32)
            dq_ref[rows, :] = jnp.where(first, dq[:DIL_BLK], dq[DIL_BLK:]) * ATT_SCALE
            dk_ref[krows, :] += _tn(dl, q2)
            dv_ref[krows, :] += _tn(p.astype(bf16), do2)

    grouped = pl.BlockSpec((SEQ, LANES), lambda p: (0, DIL_PAIRS * g + p))
    own = pl.BlockSpec((SEQ, LANES), lambda p: (0, p))
    shape = SDS((SEQ, DIL_OUT_W), f32)
    return pl.pallas_call(
        body, grid=(DIL_PAIRS,), in_specs=[grouped, grouped, _dil_v_spec(g)] + [own] * 3, out_specs=[own] * 3,
        out_shape=(shape, shape, shape), name=f"dil_bwd_{DILATIONS[g]}", compiler_params=_params(("parallel",)),
    )(q, k, v, do, lse, delta)


def _position():
    return lax.axis_index("x"), lax.axis_index("y"), lax.axis_index("c")


def _all_gather(block, name):
    def body(x_ref, out_ref, send_sems, recv_sems, local_sem):
        x, y, c = _position()
        me, sibling = (x, y, c), (x, y, 1 - c)
        chips = [(1 - x, y), (x, 1 - y), (1 - x, 1 - y)]

        def slot(px, py, pc):
            return out_ref.at[4 * px + 2 * py + pc]

        def copy(k, blk, to, src=None):
            return pltpu.make_async_remote_copy(
                src_ref=slot(*blk) if src is None else src, dst_ref=slot(*blk),
                send_sem=send_sems.at[k], recv_sem=recv_sems.at[k], device_id=to, device_id_type=MESH)

        mine = pltpu.make_async_copy(x_ref, slot(*me), local_sem)
        mine.start()
        first = [copy(0, me, sibling, src=x_ref)]
        first += [copy(1 + j, me, (*chip, c), src=x_ref) for j, chip in enumerate(chips)]
        for cp in first:
            cp.start()
        passed = [copy(4 + j, (*chip, c), sibling) for j, chip in enumerate(chips)]
        for j, chip in enumerate(chips):
            copy(1 + j, (*chip, c), me).wait_recv()
            passed[j].start()
        copy(0, sibling, me).wait_recv()
        for j, chip in enumerate(chips):
            copy(4 + j, (*chip, 1 - c), me).wait_recv()
        for cp in first + passed:
            cp.wait_send()
        mine.wait()

    return pl.pallas_call(
        body, out_shape=SDS((N_DEV,) + block.shape, block.dtype),
        in_specs=[pl.BlockSpec(memory_space=pl.ANY)], out_specs=pl.BlockSpec(memory_space=pl.ANY),
        scratch_shapes=[pltpu.SemaphoreType.DMA((7,)), pltpu.SemaphoreType.DMA((7,)), pltpu.SemaphoreType.DMA],
        name=name,
    )(block)


HBM_SPEC = pl.BlockSpec(memory_space=pltpu.HBM)
SEM_SPEC = pl.BlockSpec(memory_space=pltpu.SEMAPHORE)
SPLIT_COPY = pltpu.CompilerParams(has_side_effects=pltpu.SideEffectType.DATAFLOW_SIDE_EFFECTING)


def _in_hbm(t):
    return pltpu.with_memory_space_constraint(t, pltpu.HBM)


def _pair_copies(g_refs, land_refs, send_sems, recv_sems):
    x, y, c = _position()
    return [pltpu.make_async_remote_copy(
        src_ref=g.at[2 * k + (1 - c)], dst_ref=land.at[k], send_sem=send_sems.at[4 * a + k],
        recv_sem=recv_sems.at[4 * a + k], device_id=(x, y, 1 - c), device_id_type=MESH)
        for a, (g, land) in enumerate(zip(g_refs, land_refs, strict=True)) for k in range(4)]


def _chip_copies(t_refs, land_refs, send_sems, recv_sems):
    x, y, c = _position()
    chips = [(1 - x, y), (x, 1 - y), (1 - x, 1 - y)]
    return [pltpu.make_async_remote_copy(
        src_ref=t.at[2 * px + py], dst_ref=land.at[j], send_sem=send_sems.at[3 * a + j],
        recv_sem=recv_sems.at[3 * a + j], device_id=(px, py, c), device_id_type=MESH)
        for a, (t, land) in enumerate(zip(t_refs, land_refs, strict=True)) for j, (px, py) in enumerate(chips)]


_ROUNDS = {"pair": (_pair_copies, 4), "chip": (_chip_copies, 3)}


def _exchange_start(kind, ts, name):
    copies, slots = _ROUNDS[kind]
    n = len(ts)
    lands = [_in_hbm(lax.empty((slots,) + t.shape[1:], t.dtype)) for t in ts]

    def body(*refs):
        for cp in copies(refs[:n], refs[n:2 * n], refs[2 * n], refs[2 * n + 1]):
            cp.start()
        refs[-1][...] = jnp.zeros_like(refs[-1])

    sems = pltpu.SemaphoreType.DMA((slots * n,))
    res = pl.pallas_call(
        body, name=name, in_specs=[HBM_SPEC] * (2 * n),
        out_shape=(sems, sems, *[pltpu.HBM(t.shape, t.dtype) for t in (*ts, *lands)], SDS((8, LANES), f32)),
        out_specs=(SEM_SPEC, SEM_SPEC, *[HBM_SPEC] * (2 * n), pl.BlockSpec(memory_space=pltpu.VMEM)),
        input_output_aliases={i: 2 + i for i in range(2 * n)}, compiler_params=SPLIT_COPY,
    )(*[_in_hbm(t) for t in ts], *lands)
    return res[:-1], res[-1]


def _exchange_wait(kind, state, after, name):
    copies, _ = _ROUNDS[kind]
    send_sems, recv_sems, *arrays = state
    n = len(arrays) // 2

    def body(*refs):
        for cp in copies(refs[:n], refs[n:2 * n], refs[2 * n], refs[2 * n + 1]):
            cp.wait_send()
            cp.wait_recv()

    res = pl.pallas_call(
        body, name=name, in_specs=[HBM_SPEC] * (2 * n) + [SEM_SPEC, SEM_SPEC, pl.BlockSpec(memory_space=pl.ANY)],
        out_shape=[pltpu.HBM(t.shape, t.dtype) for t in arrays], out_specs=[HBM_SPEC] * (2 * n),
        input_output_aliases={i: i for i in range(2 * n)}, compiler_params=SPLIT_COPY,
    )(*arrays, send_sems, recv_sems, after)
    return res[:n], res[n:]


def _gather_copies(x_refs, out_refs, send_sems, recv_sems):
    x, y, c = _position()
    peers = [(x, y, 1 - c), (1 - x, y, c), (x, 1 - y, c), (1 - x, 1 - y, c)]
    sends, arrivals = [], []
    for a, (x_ref, out_ref) in enumerate(zip(x_refs, out_refs, strict=True)):
        for k, (px, py, pc) in enumerate(peers):
            sems = dict(send_sem=send_sems.at[4 * a + k], recv_sem=recv_sems.at[4 * a + k],
                        device_id=(px, py, pc), device_id_type=MESH)
            sends.append(pltpu.make_async_remote_copy(src_ref=x_ref, dst_ref=out_ref.at[4 * x + 2 * y + c], **sems))
            arrivals.append(pltpu.make_async_remote_copy(src_ref=x_ref, dst_ref=out_ref.at[4 * px + 2 * py + pc],
                                                         **sems))
    return sends, arrivals


def _gather_start(blocks, after, name):
    n = len(blocks)
    outs = [_in_hbm(lax.empty((N_DEV,) + b.shape, b.dtype)) for b in blocks]

    def body(*refs):
        sends, _ = _gather_copies(refs[:n], refs[n:2 * n], refs[2 * n + 1], refs[2 * n + 2])
        for cp in sends:
            cp.start()
        refs[-1][...] = jnp.zeros_like(refs[-1])

    sems = pltpu.SemaphoreType.DMA((4 * n,))
    res = pl.pallas_call(
        body, name=name, in_specs=[HBM_SPEC] * (2 * n) + [pl.BlockSpec(memory_space=pl.ANY)],
        out_shape=(sems, sems, *[pltpu.HBM(t.shape, t.dtype) for t in (*blocks, *outs)], SDS((8, LANES), f32)),
        out_specs=(SEM_SPEC, SEM_SPEC, *[HBM_SPEC] * (2 * n), pl.BlockSpec(memory_space=pltpu.VMEM)),
        input_output_aliases={i: 2 + i for i in range(2 * n)}, compiler_params=SPLIT_COPY,
    )(*[_in_hbm(b) for b in blocks], *outs, after)
    return res[:-1], res[-1]


def _gather_wait(state, after, name):
    send_sems, recv_sems, *arrays = state
    n = len(arrays) // 2

    def body(*refs):
        sends, arrivals = _gather_copies(refs[:n], refs[n:2 * n], refs[2 * n], refs[2 * n + 1])
        for cp in sends:
            cp.wait_send()
        for cp in arrivals:
            cp.wait_recv()

    res = pl.pallas_call(
        body, name=name, in_specs=[HBM_SPEC] * (2 * n) + [SEM_SPEC, SEM_SPEC, pl.BlockSpec(memory_space=pl.ANY)],
        out_shape=[pltpu.HBM(t.shape, t.dtype) for t in arrays], out_specs=[HBM_SPEC] * (2 * n),
        input_output_aliases={i: i for i in range(2 * n)}, compiler_params=SPLIT_COPY,
    )(*arrays, send_sems, recv_sems, after)
    return res[:n], res[n:]


def _gather_finish(partial, name):
    n = len(partial)

    def body(*refs):
        in_refs, out_refs = refs[:n], refs[n:2 * n]
        send_sems, recv_sems = refs[2 * n:]
        x, y, c = _position()
        chips = [(1 - x, y), (x, 1 - y), (1 - x, 1 - y)]
        copies = []
        for a in range(n):
            for j, (px, py) in enumerate(chips):
                cp = pltpu.make_async_remote_copy(
                    src_ref=in_refs[a].at[4 * px + 2 * py + c], dst_ref=out_refs[a].at[4 * px + 2 * py + c],
                    send_sem=send_sems.at[a, j], recv_sem=recv_sems.at[a, j], device_id=(x, y, 1 - c),
                    device_id_type=MESH)
                cp.start()
                copies.append(cp)
        for a in range(n):
            for j, (px, py) in enumerate(chips):
                pltpu.make_async_remote_copy(
                    src_ref=in_refs[a].at[4 * px + 2 * py + (1 - c)], dst_ref=out_refs[a].at[4 * px + 2 * py + (1 - c)],
                    send_sem=send_sems.at[a, j], recv_sem=recv_sems.at[a, j], device_id=(x, y, 1 - c),
                    device_id_type=MESH).wait_recv()
        for cp in copies:
            cp.wait_send()

    hbm = pl.BlockSpec(memory_space=pl.ANY)
    return pl.pallas_call(
        body, out_shape=[SDS(p.shape, p.dtype) for p in partial], in_specs=[hbm] * n, out_specs=[hbm] * n,
        input_output_aliases={a: a for a in range(n)},
        scratch_shapes=[pltpu.SemaphoreType.DMA((n, 3)), pltpu.SemaphoreType.DMA((n, 3))],
        name=name,
    )(*partial)


def _row_tile(rows):
    return 512 if rows % 512 == 0 and rows > 512 else rows


def _pair_add(g, r1, core, name):
    def body(c_ref, g_ref, r_ref, o_ref):
        o_ref[...] = (g_ref[...].astype(f32) + r_ref[...].astype(f32)).astype(o_ref.dtype)

    rows, cols = g.shape[1:]
    tile = _row_tile(rows)
    blk = (1, tile, cols)
    return pl.pallas_call(
        body, out_shape=SDS((4, rows, cols), g.dtype), name=name,
        grid_spec=pltpu.PrefetchScalarGridSpec(
            num_scalar_prefetch=1, grid=(4, rows // tile),
            in_specs=[pl.BlockSpec(blk, lambda k, i, c_ref: (2 * k + c_ref[0], i, 0)),
                      pl.BlockSpec(blk, lambda k, i, c_ref: (k, i, 0))],
            out_specs=pl.BlockSpec(blk, lambda k, i, c_ref: (k, i, 0))),
        compiler_params=_params(("parallel", "arbitrary")),
    )(core, g, r1)


def _chip_add(t, r2, chip, name, transposed=False):
    def body(c_ref, t_ref, r_ref, o_ref):
        s = ((t_ref[0].astype(f32) + r_ref[0].astype(f32)) + r_ref[1].astype(f32)) + r_ref[2].astype(f32)
        o_ref[...] = s.T if transposed else s

    rows, cols = t.shape[1:]
    tile = _row_tile(rows)
    out_spec = pl.BlockSpec((cols, tile), lambda i, c_ref: (0, i)) if transposed else pl.BlockSpec(
        (tile, cols), lambda i, c_ref: (i, 0))
    return pl.pallas_call(
        body, out_shape=SDS((cols, rows) if transposed else (rows, cols), f32), name=name,
        grid_spec=pltpu.PrefetchScalarGridSpec(
            num_scalar_prefetch=1, grid=(rows // tile,),
            in_specs=[pl.BlockSpec((1, tile, cols), lambda i, c_ref: (c_ref[0], i, 0)),
                      pl.BlockSpec((3, tile, cols), lambda i, c_ref: (0, i, 0))],
            out_specs=out_spec),
        compiler_params=_params(("arbitrary",)),
    )(chip, t, r2)


def _pad_to(t, axis, size):
    pads = [(0, 0)] * t.ndim
    pads[axis] = (0, size - t.shape[axis])
    return jnp.pad(t, pads)


_REF_COLS = {"qa": (0, FOX_W), "ka": (FOX_W, FOX_W), "va": (2 * FOX_W, FOX_W), "f": (3 * FOX_W, N_FOX_HEADS)}
_REF_COLS.update({n: (3 * FOX_W + N_FOX_HEADS + i * DIL_W, DIL_W) for i, n in enumerate(("qb", "kb", "vb"))})
_REF_COLS.update({n: (3 * FOX_W + N_FOX_HEADS + 3 * DIL_W + i * D, D) for i, n in enumerate(("ga", "gb"))})
_REF_ORDER = ("qa", "ka", "va", "f", "qb", "kb", "vb", "ga", "gb")


def _place_cols(sources, src_of, out_cols, name, row_block=512):
    arrays = [s[0] if isinstance(s, tuple) else s for s in sources]
    widths = [a.shape[-1] for a in arrays]
    rows = arrays[0].shape[-2]
    plan = []
    for t in range(out_cols // LANES):
        segs, c, end = [], t * LANES, (t + 1) * LANES
        while c < end:
            s = src_of(c)
            if s is None:
                c += 1
                continue
            n = 1
            while c + n < end and src_of(c + n) == (s[0], s[1] + n):
                n += 1
            segs.append((s[0], s[1], c - t * LANES, n))
            c += n
        plan.append(segs)

    def body(*refs):
        o_ref = refs[-1]
        for t, segs in enumerate(plan):
            acc = None
            for si, c0, o0, n in segs:
                a0 = c0 // LANES * LANES
                wide = min(2 * LANES, widths[si] - a0)
                win = refs[si][0, :, a0:a0 + wide] if isinstance(sources[si], tuple) else refs[si][:, a0:a0 + wide]
                r = lax.broadcasted_iota(jnp.int32, (wide, LANES), 0)
                c = lax.broadcasted_iota(jnp.int32, (wide, LANES), 1)
                pick = ((r - (c0 - a0) == c - o0) & (c >= o0) & (c < o0 + n)).astype(bf16)
                part = jnp.dot(win.astype(bf16), pick, preferred_element_type=f32)
                acc = part if acc is None else acc + part
            tile = jnp.zeros((row_block, LANES), f32) if acc is None else acc
            o_ref[:, t * LANES:(t + 1) * LANES] = tile.astype(o_ref.dtype)

    def spec(s):
        if isinstance(s, tuple):
            j = s[1]
            return pl.BlockSpec((1, row_block, s[0].shape[-1]), lambda i: (j, i, 0))
        return pl.BlockSpec((row_block, s.shape[-1]), lambda i: (i, 0))

    return pl.pallas_call(
        body, grid=(rows // row_block,), in_specs=[spec(s) for s in sources],
        out_specs=pl.BlockSpec((row_block, out_cols), lambda i: (i, 0)), out_shape=SDS((rows, out_cols), bf16),
        name=name, compiler_params=_params(("parallel",)),
    )(*arrays)


def _ref_piece(r):
    for name in _REF_ORDER:
        lo, width = _REF_COLS[name]
        if lo <= r < lo + width:
            return name, r - lo
    raise ValueError(r)


def _shard_pad_cols(pieces):
    names = [n for n in _REF_ORDER if n != "vb"]
    sources = [pieces[n] for n in names] + list(pieces["vb"])

    def src_of(c):
        j, i = divmod(c, W_IN_PAD)
        if i >= W_IN_SH:
            return None
        name, col = _ref_piece(j * W_IN_SH + i)
        if name == "vb":
            return len(names) + col // DIL_OUT_W, col % DIL_OUT_W
        return names.index(name), col

    return _place_cols(sources, src_of, N_DEV * W_IN_PAD, "place_dproj")


_SLABS = {"ga": C_GA, "gb": C_GB, "qb": C_QB, "kb": C_KB, "vb": C_VB, "qa": C_QA, "ka": C_KA, "va": C_VA, "f": C_F}


def _slab_w_in(stack):
    def src_of(c):
        for name, start in _SLABS.items():
            lo, width = _REF_COLS[name]
            if start <= c < start + width:
                return divmod(lo + c - start, W_IN_SH)
        return None

    return _place_cols([(stack, j) for j in range(N_DEV)], src_of, PROJ_W, "place_w_in")


def kernel(x, c, w_ada, b_ada, g_mix, w_in, b_fgate, w_br_a, w_br_b, w_out, g_ffn, w_ffn_gate, w_ffn_up, w_ffn_down, g_final, loss_target, m_w_ada, m_b_ada, m_g_mix, m_w_in, m_b_fgate, m_w_br_a, m_w_br_b, m_w_out, m_g_ffn, m_w_ffn_gate, m_w_ffn_up, m_w_ffn_down, m_g_final, v_w_ada, v_b_ada, v_g_mix, v_w_in, v_b_fgate, v_w_br_a, v_w_br_b, v_w_out, v_g_ffn, v_w_ffn_gate, v_w_ffn_up, v_w_ffn_down, v_g_final):
    px, py, pc = _position()
    dev = 4 * px + 2 * py + pc
    x2d, tgt = x[0], loss_target[0]

    c_all = _all_gather(c, "gather_c").reshape(N_DEV, D)
    ada_cols = w_ada.shape[2]
    b_shard = lax.dynamic_slice(b_ada, (0, dev * ada_cols), (1, ada_cols))
    mod_shard = _ada_fwd(c_all, w_ada[0], b_shard)
    mod_all = _all_gather(mod_shard, "gather_mod")
    modv = lax.dynamic_index_in_dim(mod_all, dev, axis=1, keepdims=False).reshape(6, D)
    h1 = _pre1(x2d, modv, g_mix)

    w_in_s = _all_gather(_pad_to(w_in[0], 1, W_IN_PAD).astype(bf16), "gather_w_in")
    gate_up = jnp.concatenate([_pad_to(w_ffn_gate[0], 1, FF_PAD), _pad_to(w_ffn_up[0], 1, FF_PAD)], axis=1)
    later = [w_br_a[0], w_br_b[0], w_out[0], gate_up, _pad_to(w_ffn_down[0], 0, FF_PAD)]
    later_state, later_token = _gather_start([t.astype(bf16) for t in later], w_in_s, "gather_rest_start")
    w_in_p = _slab_w_in(w_in_s)

    proj = _matmul(h1, w_in_p, name="mm_proj", tm=SEQ, tn=896, tk=D, after=later_token)
    b_pad = jnp.pad(b_fgate, ((0, 0), (0, LANES - N_FOX_HEADS)))
    q_aug, k_aug, va = _fox_prep(proj, _fox_gate_fwd(proj, b_pad))
    ya_h, max_a, sum_a = _fox_fwd(q_aug, k_aug, va)

    tables = _rope_tables()
    qb_r, kb_r = _rope_fwd(proj, tables)
    by_group = [_dil_fwd(qb_r, kb_r, proj, grp) for grp in range(N_GROUPS)]
    yb_h, lse_b = _dil_combine([o for o, _ in by_group], [l for _, l in by_group])

    both_done = ya_h[:8, :LANES] + yb_h[:8, :LANES]
    mine, arrived = _gather_wait(later_state, both_done, "gather_rest_wait")
    w_a_s, w_b_s, w_o_s, w_gu_s, w_d_s = [
        lax.dynamic_update_slice(stack, block[None], (dev, 0, 0))
        for stack, block in zip(_gather_finish(arrived, "gather_rest_finish"), mine, strict=True)]
    w_o = w_o_s.reshape(D, D)
    w_d = w_d_s.reshape(FF_HID, D)
    ya = _matmul_stack(ya_h, w_a_s, name="mm_br_a")
    yb = _matmul_stack(yb_h, w_b_s, name="mm_br_b")

    merged, mix, x1, h2 = _post1(ya, yb, proj, w_o, x2d, modv, g_ffn)
    act, au = _ffn_in(h2, w_gu_s)

    dx2, dff, dg_final, dga_f, loss_lanes = _final(act, w_d, x1, tgt, modv, g_final.reshape(1, D))
    dau = _ffn_bwd_in(dff, w_d_s, au)

    core = pc.astype(jnp.int32).reshape(1)
    chip = (2 * px + py).astype(jnp.int32).reshape(1)

    def pair_done(state, after, tags, name):
        mine, theirs = _exchange_wait("pair", state, after, "pair_wait_" + name)
        sums = [_pair_add(g, r, core, "pair_add_" + t) for g, r, t in zip(mine, theirs, tags)]
        return _exchange_start("chip", sums, "chip_start_" + name)

    def from_chips(state, after, tags, name, transposed=None):
        sums, got = _exchange_wait("chip", state, after, "chip_wait_" + name)
        flips = transposed or [False] * len(tags)
        return [_chip_add(p, r, chip, "chip_add_" + t, f) for p, r, t, f in zip(sums, got, tags, flips)]

    g_gu = _matmul(h2, dau, ta=True, by_shard=True, out_dtype=bf16, name="mm_g_ffn_in", tm=D, tn=2 * FF_PAD, tk=SEQ)
    g_d = _matmul(act, dff, ta=True, out_dtype=bf16, name="mm_g_down", tm=FF_HID // 2, tn=512, tk=SEQ)
    ffn_tags = ["gu", "down"]
    ffn_pair, ffn_pair_token = _exchange_start("pair", [g_gu, g_d.reshape(N_DEV, FF_PAD, D)], "pair_start_ffn")

    dx1, dmix, dsh_f, dsc_f, dg_ffn, dga_m = _mid_bwd(dau, w_gu_s, ffn_pair_token, x1, dx2, mix, modv, g_ffn)
    ffn_state, ffn_token = pair_done(ffn_pair, dx1, ffn_tags, "ffn")
    dya, dyb, dga, dgb = _merge_bwd(dmix, w_o, ffn_token, ya, yb, proj)
    dya_h = _matmul_stack(dya, w_a_s, tb=True, name="mm_d_ya")
    dyb_h = _matmul_stack(dyb, w_b_s, tb=True, name="mm_d_yb")

    g_o = _matmul(merged, dmix, ta=True, out_dtype=bf16, name="mm_g_out", tm=D, tn=512, tk=SEQ)
    g_a = _matmul_stack(ya_h, dya, ta=True, out_dtype=bf16, name="mm_g_br_a")
    g_b = _matmul_stack(yb_h, dyb, ta=True, out_dtype=bf16, name="mm_g_br_b")
    rows_a, rows_b = FOX_W * W_BR_SH // D, DIL_OUT_W * W_BR_SH // D
    g_small = jnp.concatenate([g_a.reshape(N_DEV, rows_a, D), g_b.reshape(N_DEV, rows_b, D),
                               g_o.reshape(N_DEV, W_BR_SH, D)], axis=1)
    small_pair, small_pair_token = _exchange_start("pair", [g_small], "pair_start_small")

    dqa, dka, dva, dF = _fox_bwd(q_aug, k_aug, va, dya_h, ya_h, max_a, sum_a, small_pair_token)
    dF_row = jnp.pad(dF[:, :2, :].reshape(N_FOX_HEADS, SEQ), ((0, LANES - N_FOX_HEADS), (0, 0)))
    df, db_fgate = _fox_gate_bwd(dF_row, proj, b_pad)
    small_state, small_token = pair_done(small_pair, df, ["small"], "small")

    delta_b = _dil_delta(dyb_h, yb_h)
    dil_grads = [_dil_bwd(qb_r, kb_r, proj, dyb_h, lse_b, delta_b, grp) for grp in range(N_GROUPS)]
    dqb, dkb = _rope_bwd([t[0] for t in dil_grads], [t[1] for t in dil_grads], tables)

    dproj = _shard_pad_cols({"qa": dqa, "ka": dka, "va": dva, "f": df, "qb": dqb, "kb": dkb,
                             "vb": [t[2] for t in dil_grads], "ga": dga, "gb": dgb})
    g_in = _matmul(h1, dproj, ta=True, by_shard=True, out_dtype=bf16, name="mm_g_in", tm=D, tn=W_IN_PAD, tk=SEQ,
                   after=small_token)
    mix_tags = ["in"]
    mix_pair, mix_pair_token = _exchange_start("pair", [g_in], "pair_start_mixer")

    grad_x, dsh_m, dsc_m, dg_mix = _first_bwd(dproj, w_in_s, mix_pair_token, x2d, dx1, modv, g_mix)

    pad_lane = lambda t: jnp.pad(t, ((0, 0), (0, D - t.shape[1])))
    small = jnp.concatenate([dsh_m, dsc_m, dga_m, dsh_f, dsc_f, dga_f, dg_mix, dg_ffn, dg_final,
                             pad_lane(db_fgate), loss_lanes, jnp.zeros((SMALL_ROWS - 11, D), f32)], axis=0)
    small_all = _all_gather(small, "gather_small")
    mix_state, mix_token = pair_done(mix_pair, small_all, mix_tags, "mixer")

    small_sum, loss_row = _small_reduce(small_all, mix_token)
    dmod_all = small_all[:, :6, :].reshape(N_DEV, 6 * D)
    g_w_ada = _ada_bwd(c_all, lax.dynamic_slice(dmod_all, (0, dev * ada_cols), (N_DEV, ada_cols)))
    s_gu_t, s_d = from_chips(ffn_state, small_sum, ffn_tags, "ffn", [True, False])
    s_small, = from_chips(small_state, small_sum, ["small"], "small")

    loss = loss_row[0, 0]
    g = {
        "w_ada": g_w_ada[None], "b_ada": small_sum[0:6].reshape(1, 6 * D), "g_mix": small_sum[6:7],
        "b_fgate": small_sum[9:10, :N_FOX_HEADS], "g_ffn": small_sum[7:8], "w_ffn_gate": s_gu_t[:W_FF_SH],
        "w_ffn_up": s_gu_t[FF_PAD:FF_PAD + W_FF_SH], "w_ffn_down": s_d[None, :W_FF_SH],
        "g_final": small_sum[8], "w_br_a": s_small[:rows_a].reshape(1, FOX_W, W_BR_SH),
        "w_br_b": s_small[rows_a:rows_a + rows_b].reshape(1, DIL_OUT_W, W_BR_SH), "w_out": s_small[None, rows_a + rows_b:],
    }
    w = {"w_ada": w_ada, "b_ada": b_ada, "g_mix": g_mix, "w_in": w_in, "b_fgate": b_fgate, "w_br_a": w_br_a,
         "w_br_b": w_br_b, "w_out": w_out, "g_ffn": g_ffn, "w_ffn_gate": w_ffn_gate, "w_ffn_up": w_ffn_up,
         "w_ffn_down": w_ffn_down, "g_final": g_final}
    m = {"w_ada": m_w_ada, "b_ada": m_b_ada, "g_mix": m_g_mix, "w_in": m_w_in, "b_fgate": m_b_fgate,
         "w_br_a": m_w_br_a, "w_br_b": m_w_br_b, "w_out": m_w_out, "g_ffn": m_g_ffn, "w_ffn_gate": m_w_ffn_gate,
         "w_ffn_up": m_w_ffn_up, "w_ffn_down": m_w_ffn_down, "g_final": m_g_final}
    v = {"w_ada": v_w_ada, "b_ada": v_b_ada, "g_mix": v_g_mix, "w_in": v_w_in, "b_fgate": v_b_fgate,
         "w_br_a": v_w_br_a, "w_br_b": v_w_br_b, "w_out": v_w_out, "g_ffn": v_g_ffn, "w_ffn_gate": v_w_ffn_gate,
         "w_ffn_up": v_w_ffn_up, "w_ffn_down": v_w_ffn_down, "g_final": v_g_final}
    names = list(w)
    delta, new_m, new_v = {}, {}, {}

    transposed = ("w_in", "w_ffn_gate", "w_ffn_up")

    def update(n):
        shape = w[n].shape
        if n in transposed:
            g_t = g[n]
            dl, mn, vn = _adamw(w[n][0].T, g_t, m[n][0].T, v[n][0].T, "adamw_" + n)
            g[n], delta[n], new_m[n], new_v[n] = g_t.T[None], dl.T[None], mn.T[None], vn.T[None]
            return
        two_d = (lambda t: t.reshape(shape[-2:])) if len(shape) == 3 else (lambda t: t)
        dl, mn, vn = _adamw(two_d(w[n]), two_d(g[n]), two_d(m[n]), two_d(v[n]), "adamw_" + n)
        delta[n], new_m[n], new_v[n] = dl.reshape(shape), mn.reshape(shape), vn.reshape(shape)

    for n in list(g):
        update(n)
    done = sum(delta[n].reshape(-1)[:N_FOX_HEADS] for n in g)
    s_in_t, = from_chips(mix_state, done, mix_tags, "mixer", [True])
    g["w_in"] = s_in_t[:W_IN_SH]
    update("w_in")

    return (loss, grad_x[None], *[g[n] for n in names], *[delta[n] for n in names],
            *[new_m[n] for n in names], *[new_v[n] for n in names])
```

```python
import functools

import jax
import jax.numpy as jnp
import numpy as np
from jax import lax
from jax.experimental import pallas as pl
from jax.experimental.pallas import tpu as pltpu

f32 = jnp.float32
bf16 = jnp.bfloat16
SDS = jax.ShapeDtypeStruct
MESH = pl.DeviceIdType.MESH

N_DEV = 8
D = 1024
SEQ = 2048
HEAD_DIM = 64
N_FOX_HEADS = 8
FOX_W = 512
DIL_W = 768
DIL_OUT_W = 256
ROT_DIM = 16
ROPE_THETA = 500000.0
D_FF = 2816
IN_COLS = 5896
EPS = 1e-6
NEG = -1e30
ATT_SCALE = HEAD_DIM ** -0.5

ADAM_LR = 0.001
ADAM_B1 = 0.9
ADAM_B2 = 0.999
ADAM_EPS = 1e-08
ADAM_WD = 0.01
ADAM_STEP = 10

C_GA, C_GB, C_QB, C_KB, C_VB, C_QA, C_KA, C_VA, C_F = 0, 1024, 2304, 3072, 3840, 4608, 5120, 5632, 6144
PROJ_W = 6272
LANES = 128
VMEM_LIMIT = 52 * 1024 * 1024

W_IN_SH, W_IN_PAD = IN_COLS // N_DEV, 768
W_BR_SH = D // N_DEV
W_FF_SH, FF_PAD = D_FF // N_DEV, 384
FF_HID = N_DEV * FF_PAD
SMALL_ROWS = 16


def _params(sem=None):
    if sem is None:
        return pltpu.CompilerParams(vmem_limit_bytes=VMEM_LIMIT)
    return pltpu.CompilerParams(dimension_semantics=sem, vmem_limit_bytes=VMEM_LIMIT)


def _rowwise(fn, name, tiled, vecs, outs, reds=(), tile=256):
    nt, nv, no = len(tiled), len(vecs), len(outs)
    rows = tiled[0][0].shape[0]
    assert rows % tile == 0

    def body(*refs):
        tin = [r[...] for r in refs[:nt]]
        vin = [r[...] for r in refs[nt:nt + nv]]
        orefs = refs[nt + nv:nt + nv + no]
        rrefs = refs[nt + nv + no:]
        touts, routs = fn(tin, vin)
        for r, t in zip(orefs, touts, strict=True):
            r[...] = t.astype(r.dtype)
        if rrefs:
            @pl.when(pl.program_id(0) == 0)
            def _():
                for r in rrefs:
                    r[...] = jnp.zeros_like(r)
            for r, t in zip(rrefs, routs, strict=True):
                r[...] += t

    def col_map(cb):
        return lambda i: (i, cb)

    def whole_map(nd):
        return lambda i: (0,) * nd

    in_specs = [pl.BlockSpec((tile, w), col_map(cb)) for (_, w, cb) in tiled]
    in_specs += [pl.BlockSpec(v.shape, whole_map(v.ndim)) for v in vecs]
    out_specs = [pl.BlockSpec((tile, w), lambda i: (i, 0)) for (w, _) in outs]
    out_specs += [pl.BlockSpec((1, w), lambda i: (0, 0)) for w in reds]
    out_shape = [SDS((rows, w), dt) for (w, dt) in outs] + [SDS((1, w), f32) for w in reds]
    res = pl.pallas_call(
        body, grid=(rows // tile,), in_specs=in_specs, out_specs=out_specs, out_shape=out_shape, name=name,
        compiler_params=_params(("arbitrary",)),
    )(*[t[0] for t in tiled], *vecs)
    return res


def _matmul(a, b, *, ta=False, tb=False, out_dtype=f32, name, tm, tn, tk, by_shard=False, after=None):
    m, k = (a.shape[1], a.shape[0]) if ta else a.shape
    if by_shard and not ta:
        n, kb = (b.shape[1], N_DEV * b.shape[2]) if tb else (N_DEV * b.shape[2], b.shape[1])
        assert (tk if tb else tn) == b.shape[2]
    else:
        n, kb = (b.shape[0], b.shape[1]) if tb else (b.shape[1], b.shape[0])
    assert kb == k and m % tm == 0 and n % tn == 0 and k % tk == 0
    nk = k // tk
    dims = (((0 if ta else 1,), (1 if tb else 0,)), ((), ()))
    b_stacked = by_shard and not ta
    o_stacked = by_shard and ta

    def body(a_ref, b_ref, *rest):
        o_ref, *acc = rest[1:] if after is not None else rest
        bv = b_ref[0] if b_stacked else b_ref[...]
        p = lax.dot_general(a_ref[...].astype(bf16), bv.astype(bf16), dims, preferred_element_type=f32)

        def put(val):
            if o_stacked:
                o_ref[0] = val.astype(o_ref.dtype)
            else:
                o_ref[...] = val.astype(o_ref.dtype)

        if nk == 1:
            put(p)
        else:
            acc_ref, = acc
            kk = pl.program_id(2)

            @pl.when(kk == 0)
            def _():
                acc_ref[...] = p

            @pl.when(kk > 0)
            def _():
                acc_ref[...] += p

            @pl.when(kk == nk - 1)
            def _():
                put(acc_ref[...])

    a_spec = pl.BlockSpec((tk, tm), lambda i, j, kk: (kk, i)) if ta else pl.BlockSpec((tm, tk), lambda i, j, kk: (i, kk))
    if b_stacked and tb:
        b_spec = pl.BlockSpec((1, tn, tk), lambda i, j, kk: (kk, j, 0))
    elif b_stacked:
        b_spec = pl.BlockSpec((1, tk, tn), lambda i, j, kk: (j, kk, 0))
    elif tb:
        b_spec = pl.BlockSpec((tn, tk), lambda i, j, kk: (j, kk))
    else:
        b_spec = pl.BlockSpec((tk, tn), lambda i, j, kk: (kk, j))
    if o_stacked:
        assert tn == n // N_DEV
        out_spec = pl.BlockSpec((1, tm, tn), lambda i, j, kk: (j, i, 0))
        out_shape = SDS((N_DEV, m, tn), out_dtype)
    else:
        out_spec = pl.BlockSpec((tm, tn), lambda i, j, kk: (i, j))
        out_shape = SDS((m, n), out_dtype)
    extra_specs, extra = ([pl.BlockSpec(memory_space=pl.ANY)], [after]) if after is not None else ([], [])
    return pl.pallas_call(
        body, grid=(m // tm, n // tn, nk), in_specs=[a_spec, b_spec] + extra_specs, out_specs=out_spec,
        out_shape=out_shape, name=name, scratch_shapes=[pltpu.VMEM((tm, tn), f32)] if nk > 1 else [],
        compiler_params=_params(("parallel", "parallel", "arbitrary")),
    )(a, b, *extra)


def _matmul_stack(a, b, *, ta=False, tb=False, out_dtype=f32, name):
    def lanes(ref):
        return jnp.concatenate([ref[j] for j in range(N_DEV)], axis=1).astype(bf16)

    if ta:
        w = b.shape[1] // N_DEV

        def body(a_ref, b_ref, o_ref):
            p = _tn(a_ref[...].astype(bf16), b_ref[...].astype(bf16))
            for j in range(N_DEV):
                o_ref[j] = p[:, j * w:(j + 1) * w].astype(o_ref.dtype)

        return pl.pallas_call(body, out_shape=SDS((N_DEV, a.shape[1], w), out_dtype), name=name,
                              compiler_params=_params())(a, b)

    m, half = a.shape[0], a.shape[0] // 2
    n = b.shape[1] if tb else N_DEV * b.shape[2]

    def body(a_ref, b_ref, o_ref):
        av = a_ref[...].astype(bf16)
        o_ref[...] = (_nt(av, lanes(b_ref)) if tb else jnp.dot(av, lanes(b_ref), preferred_element_type=f32)
                      ).astype(o_ref.dtype)

    return pl.pallas_call(
        body, grid=(2,), in_specs=[pl.BlockSpec((half, a.shape[1]), lambda i: (i, 0)),
                                   pl.BlockSpec(b.shape, lambda i: (0, 0, 0))],
        out_specs=pl.BlockSpec((half, n), lambda i: (i, 0)), out_shape=SDS((m, n), out_dtype), name=name,
        compiler_params=_params(("parallel",)),
    )(a, b)


def _matmul_rows(form, a, b, after, fn, tiled, vecs, outs, reds, *, name, tm=512):
    norm = lambda ts: [t if isinstance(t, tuple) else (t, t.shape[1], 0) for t in ts]
    make, sources = a if isinstance(a, tuple) else (None, [a])
    sources, tiled = norm(sources), norm(tiled)
    m, k = sources[0][0].shape[0], (b.shape[0] if form == "nn" else b.shape[-1] * (N_DEV if form == "nt_stack" else 1))
    assert m % tm == 0
    ns, nt, nv, no = len(sources), len(tiled), len(vecs), len(outs)

    def body(*refs):
        src_refs, b_ref, refs = refs[:ns], refs[ns], refs[ns + 2:]
        if make is None:
            lhs = lambda lo, hi: src_refs[0][:, lo:hi]
        else:
            made = make([r[...] for r in src_refs]).astype(bf16)
            lhs = lambda lo, hi: made[:, lo:hi]
        if form == "nt_stack":
            w = b.shape[2]
            acc = _nt(lhs(0, w), b_ref[0])
            for j in range(1, N_DEV):
                acc = acc + _nt(lhs(j * w, (j + 1) * w), b_ref[j])
        elif form == "nt":
            acc = _nt(lhs(0, k), b_ref[...])
        else:
            acc = jnp.dot(lhs(0, k), b_ref[...], preferred_element_type=f32)
        if make is not None:
            refs[nt + nv][...] = made
            refs = refs[:nt + nv] + refs[nt + nv + 1:]
        orefs, rrefs = refs[nt + nv:nt + nv + no], refs[nt + nv + no:]
        touts, routs = fn([acc] + [r[...] for r in refs[:nt]], [r[...] for r in refs[nt:nt + nv]])
        for r, t in zip(orefs, touts, strict=True):
            r[...] = t.astype(r.dtype)

        @pl.when(pl.program_id(0) == 0)
        def _():
            for r in rrefs:
                r[...] = jnp.zeros_like(r)
        for r, t in zip(rrefs, routs, strict=True):
            r[...] += t

    def whole_map(nd):
        return lambda i: (0,) * nd

    def rows(width, cb=0):
        return pl.BlockSpec((tm, width), lambda i: (i, cb))

    made_out = [(k, bf16)] if make is not None else []
    return pl.pallas_call(
        body, grid=(m // tm,),
        in_specs=[rows(width, cb) for _, width, cb in sources]
        + [pl.BlockSpec(b.shape, whole_map(b.ndim), pipeline_mode=pl.Buffered(1)), pl.BlockSpec(memory_space=pl.ANY)]
        + [rows(width, cb) for _, width, cb in tiled] + [pl.BlockSpec(v.shape, whole_map(v.ndim)) for v in vecs],
        out_specs=[rows(width) for width, _ in made_out + list(outs)]
        + [pl.BlockSpec((1, width), lambda i: (0, 0)) for width in reds],
        out_shape=[SDS((m, width), dt) for width, dt in made_out + list(outs)]
        + [SDS((1, width), f32) for width in reds], name=name,
        compiler_params=_params(("arbitrary",)),
    )(*[t[0] for t in sources], b, after, *[t[0] for t in tiled], *vecs)


def _rms(x):
    r = lax.rsqrt(jnp.mean(x * x, axis=-1, keepdims=True) + EPS)
    return r, x * r


def _rms_bwd(r, xn, dxn):
    return r * (dxn - xn * jnp.mean(dxn * xn, axis=-1, keepdims=True))


def _colsum(t):
    return jnp.sum(t, axis=0, keepdims=True)


def _sigmoid(x):
    return 0.5 * jnp.tanh(0.5 * x) + 0.5


def _modulated_norm(x, g, shift, scale):
    _, xn = _rms(x)
    return (xn * g) * (1.0 + scale) + shift


def _pre1(x, modv, g_mix):
    def fn(t, v):
        (xt,), (mv, g) = t, v
        return [_modulated_norm(xt, g, mv[0:1], mv[1:2])], []
    return _rowwise(fn, "pre1", [(x, D, 0)], [modv, g_mix], [(D, bf16)])[0]


def _post1(ya, yb, proj, w_o, x, modv, g_ffn, after):
    def merge(t):
        ya_t, yb_t, ga, gb = t
        return _sigmoid(ga) * ya_t + _sigmoid(gb) * yb_t

    def fn(t, v):
        (mt, xt), (mv, g) = t, v
        x1 = xt + mv[2:3] * mt
        return [mt, x1, _modulated_norm(x1, g, mv[3:4], mv[4:5])], []
    return _matmul_rows("nn", (merge, [ya, yb, (proj, D, C_GA // D), (proj, D, C_GB // D)]), w_o, after, fn, [x],
                        [modv, g_ffn], [(D, f32), (D, f32), (D, bf16)], [], name="post1")


def _ffn_in(h, w_stack):
    def body(h_ref, w_ref, act_ref, au_ref):
        p = jnp.dot(h_ref[...], w_ref[0], preferred_element_type=f32)
        a, u = p[:, :FF_PAD], p[:, FF_PAD:]
        act_ref[...] = (a * _sigmoid(a) * u).astype(act_ref.dtype)
        au_ref[...] = p.astype(au_ref.dtype)

    return pl.pallas_call(
        body, grid=(N_DEV,),
        in_specs=[pl.BlockSpec((SEQ, D), lambda j: (0, 0)), pl.BlockSpec((1, D, 2 * FF_PAD), lambda j: (j, 0, 0))],
        out_specs=[pl.BlockSpec((SEQ, FF_PAD), lambda j: (0, j)), pl.BlockSpec((SEQ, 2 * FF_PAD), lambda j: (0, j))],
        out_shape=(SDS((SEQ, FF_HID), bf16), SDS((SEQ, 2 * FF_HID), bf16)), name="ffn_in",
        compiler_params=_params(("parallel",)),
    )(h, w_stack)


def _ffn_bwd_in(dff, w_down_stack, au):
    def body(d_ref, w_ref, au_ref, o_ref):
        dact = _nt(d_ref[...], w_ref[0])
        p = au_ref[...].astype(f32)
        a, u = p[:, :FF_PAD], p[:, FF_PAD:]
        sg = _sigmoid(a)
        o_ref[...] = jnp.concatenate([dact * u * (sg * (1.0 + a * (1.0 - sg))), dact * (a * sg)],
                                     axis=1).astype(o_ref.dtype)

    return pl.pallas_call(
        body, grid=(N_DEV,),
        in_specs=[pl.BlockSpec((SEQ, D), lambda j: (0, 0)), pl.BlockSpec((1, FF_PAD, D), lambda j: (j, 0, 0)),
                  pl.BlockSpec((SEQ, 2 * FF_PAD), lambda j: (0, j))],
        out_specs=pl.BlockSpec((SEQ, 2 * FF_PAD), lambda j: (0, j)),
        out_shape=SDS((SEQ, 2 * FF_HID), bf16), name="ffn_bwd_in", compiler_params=_params(("parallel",)),
    )(dff, w_down_stack, au)


def _final(act, w_down, x1, target, modv, g_final):
    def fn(t, v):
        (fft, x1t, tgt), (mv, g) = t, v
        x2 = x1t + mv[5:6] * fft
        r, xn = _rms(x2)
        err = xn * g - tgt
        dy = err * (1.0 / D)
        dx2 = _rms_bwd(r, xn, dy * g)
        return [dx2, dx2 * mv[5:6]], [_colsum(dy * xn), _colsum(dx2 * fft), _colsum(err * err) * (0.5 / D)]
    return _matmul_rows("nn", act, w_down, x1, fn, [x1, target], [modv, g_final], [(D, f32), (D, bf16)], [D, D, D],
                        name="final")


def _mid_bwd(dau, w_stack, after, x1, dx2, mix, modv, g_ffn):
    def fn(t, v):
        (dh, x1t, dx2t, mt), (mv, g) = t, v
        r, xn = _rms(x1t)
        dn = dh * (1.0 + mv[4:5])
        dx1 = dx2t + _rms_bwd(r, xn, dn * g)
        return [dx1, dx1 * mv[2:3]], [_colsum(dh), _colsum(dh * (xn * g)), _colsum(dn * xn), _colsum(dx1 * mt)]
    return _matmul_rows("nt_stack", dau, w_stack, after, fn, [x1, dx2, mix], [modv, g_ffn], [(D, f32), (D, bf16)],
                        [D, D, D, D], name="mid_bwd")


def _first_bwd(dproj, w_stack, after, x, dx1, modv, g_mix):
    def fn(t, v):
        (dh, xt, dx1t), (mv, g) = t, v
        r, xn = _rms(xt)
        dn = dh * (1.0 + mv[1:2])
        return [dx1t + _rms_bwd(r, xn, dn * g)], [_colsum(dh), _colsum(dh * (xn * g)), _colsum(dn * xn)]
    return _matmul_rows("nt_stack", dproj, w_stack, after, fn, [x, dx1], [modv, g_mix], [(D, f32)], [D, D, D],
                        name="first_bwd")


def _merge_bwd(dmix, w_o, after, ya, yb, proj):
    def fn(t, v):
        dm, ya_t, yb_t, ga, gb = t
        sa, sb = _sigmoid(ga), _sigmoid(gb)
        return [dm * sa, dm * sb, dm * ya_t * (sa * (1.0 - sa)), dm * yb_t * (sb * (1.0 - sb))], []
    return _matmul_rows("nt", dmix, w_o, after, fn, [ya, yb, (proj, D, C_GA // D), (proj, D, C_GB // D)], [],
                        [(D, bf16), (D, bf16), (D, bf16), (D, bf16)], [], name="merge_bwd")


def _rope_tables():
    half = ROT_DIM // 2
    pos = np.arange(SEQ, dtype=np.float32)
    inv_freq = np.float32(ROPE_THETA) ** (-np.arange(0, ROT_DIM, 2, dtype=np.float32) / np.float32(ROT_DIM))
    ang = pos[:, None] * inv_freq[None, :].astype(np.float32)
    cos, sin = np.cos(ang).astype(np.float32), np.sin(ang).astype(np.float32)
    pad = np.zeros((SEQ, HEAD_DIM - ROT_DIM), np.float32)
    zero = np.zeros((SEQ, half), np.float32)
    c_head = np.concatenate([cos, cos, pad + 1.0], axis=1)
    lo_head = np.concatenate([-sin, zero, pad], axis=1)
    hi_head = np.concatenate([zero, sin, pad], axis=1)
    return tuple(jnp.asarray(np.concatenate([t, t], axis=1)) for t in (c_head, lo_head, hi_head))


def _over_heads(tables):
    return [jnp.tile(t, (1, DIL_W // LANES)) for t in tables]


def _rope_fwd(proj, tables):
    half = ROT_DIM // 2

    def fn(t, v):
        q, k = t[:2]
        c, lo, hi = _over_heads(t[2:])
        rot = lambda z: z * c + pltpu.roll(z, DIL_W - half, 1) * lo + pltpu.roll(z, half, 1) * hi
        return [rot(q) * ATT_SCALE, rot(k)], []
    return _rowwise(fn, "rope_fwd", [(proj, DIL_W, C_QB // DIL_W), (proj, DIL_W, C_KB // DIL_W)]
                    + [(tb, LANES, 0) for tb in tables], [], [(DIL_W, f32)] * 2)


def _rope_bwd(dqs, dks, tables):
    half = ROT_DIM // 2

    def fn(t, v):
        dq_t, dk_t = jnp.concatenate(t[:N_GROUPS], axis=1), jnp.concatenate(t[N_GROUPS:2 * N_GROUPS], axis=1)
        c, lo, hi = _over_heads(t[2 * N_GROUPS:])
        rot_t = lambda z: z * c + pltpu.roll(z * lo, half, 1) + pltpu.roll(z * hi, DIL_W - half, 1)
        return [rot_t(dq_t), rot_t(dk_t)], []
    return _rowwise(fn, "rope_bwd", [(a, DIL_OUT_W, 0) for a in (*dqs, *dks)] + [(tb, LANES, 0) for tb in tables],
                    [], [(DIL_W, bf16), (DIL_W, bf16)])


def _head_bcast_sum(d):
    lane = lax.broadcasted_iota(jnp.int32, d.shape, 1)
    out = jnp.zeros_like(d)
    for h in range(d.shape[1] // HEAD_DIM):
        sel = (lane >= h * HEAD_DIM) & (lane < (h + 1) * HEAD_DIM)
        out = jnp.where(sel, jnp.sum(jnp.where(sel, d, 0.0), axis=1, keepdims=True), out)
    return out


def _dil_combine(outs, lses):
    def fn(t, v):
        o0, o1, o2, l0, l1, l2 = t
        m = jnp.maximum(jnp.maximum(l0, l1), l2)
        w0, w1, w2 = jnp.exp(l0 - m), jnp.exp(l1 - m), jnp.exp(l2 - m)
        tot = w0 + w1 + w2
        return [(w0 * o0 + w1 * o1 + w2 * o2) / tot, m + jnp.log(tot)], []
    w = DIL_OUT_W
    return _rowwise(fn, "dil_combine", [(t, w, 0) for t in (*outs, *lses)], [], [(w, f32), (w, f32)])


def _dil_delta(dyb_h, yb_h):
    def fn(t, v):
        return [_head_bcast_sum(t[0] * t[1])], []
    return _rowwise(fn, "dil_delta", [(dyb_h, DIL_OUT_W, 0), (yb_h, DIL_OUT_W, 0)], [], [(DIL_OUT_W, f32)])[0]


def _adamw_math(wt, gt, mt, vt):
    mn = ADAM_B1 * mt + (1.0 - ADAM_B1) * gt
    vn = ADAM_B2 * vt + (1.0 - ADAM_B2) * (gt * gt)
    m_hat = mn / (1.0 - ADAM_B1 ** ADAM_STEP)
    v_hat = vn / (1.0 - ADAM_B2 ** ADAM_STEP)
    return -ADAM_LR * (m_hat / (jnp.sqrt(v_hat) + ADAM_EPS) + ADAM_WD * wt), mn, vn


def _adamw(w, g, m, v, name):
    shape = w.shape
    if w.ndim == 1:
        w, g, m, v = (t.reshape(1, -1) for t in (w, g, m, v))
    rows, cols = w.shape
    if rows % 8 and rows > 8:
        return _adamw_by_cols(w, g, m, v, name)
    tile = 256 if rows % 256 == 0 and rows > 512 else rows

    def fn(t, _):
        return list(_adamw_math(*t)), []
    delta, mn, vn = _rowwise(fn, name, [(w, cols, 0), (g, cols, 0), (m, cols, 0), (v, cols, 0)], [],
                             [(cols, f32)] * 3, tile=tile)
    return delta.reshape(shape), mn.reshape(shape), vn.reshape(shape)


def _adamw_by_cols(w, g, m, v, name, tile=256):
    rows, cols = w.shape

    def body(w_ref, g_ref, m_ref, v_ref, d_ref, mn_ref, vn_ref):
        d_ref[...], mn_ref[...], vn_ref[...] = _adamw_math(w_ref[...], g_ref[...], m_ref[...], v_ref[...])

    spec = pl.BlockSpec((rows, tile), lambda j: (0, j))
    return pl.pallas_call(body, grid=(cols // tile,), in_specs=[spec] * 4, out_specs=[spec] * 3,
                          out_shape=[SDS((rows, cols), f32)] * 3, name=name,
                          compiler_params=_params(("parallel",)))(w, g, m, v)


def _ada_fwd(c_all, w_shard, b_shard):
    def body(c_ref, w_ref, b_ref, o_ref):
        cv = c_ref[...]
        sc = (cv * _sigmoid(cv)).astype(bf16)
        o_ref[...] = jnp.dot(sc, w_ref[...].astype(bf16), preferred_element_type=f32) + b_ref[...]
    return pl.pallas_call(body, out_shape=SDS((N_DEV, w_shard.shape[1]), f32), name="ada_fwd",
                          compiler_params=_params())(c_all, w_shard, b_shard)


def _ada_bwd(c_all, dmod_cols):
    def body(c_ref, d_ref, o_ref):
        cv = c_ref[...]
        sc = cv * _sigmoid(cv)
        o_ref[...] = lax.dot_general(sc, d_ref[...], (((0,), (0,)), ((), ())), precision=lax.Precision.HIGHEST,
                                     preferred_element_type=f32)
    return pl.pallas_call(body, out_shape=SDS((D, dmod_cols.shape[1]), f32), name="ada_bwd",
                          compiler_params=_params())(c_all, dmod_cols)


def _small_reduce(gathered, after):
    def body(g_ref, after_ref, o_ref, loss_ref):
        acc = g_ref[0]
        for d in range(1, N_DEV):
            acc = acc + g_ref[d]
        o_ref[...] = acc
        loss_ref[...] = jnp.zeros((1, LANES), f32) + jnp.sum(acc[10:11, :])
    return pl.pallas_call(body, out_shape=(SDS((SMALL_ROWS, D), f32), SDS((1, LANES), f32)), name="small_reduce",
                          in_specs=[pl.BlockSpec(memory_space=pltpu.VMEM), pl.BlockSpec(memory_space=pl.ANY)],
                          compiler_params=_params())(gathered, after)


FOX_BLK = 512
CUM_BLK = 128


def _fold_lanes(t, op):
    out = t[:, :LANES]
    for j in range(1, t.shape[1] // LANES):
        out = op(out, t[:, j * LANES:(j + 1) * LANES])
    return out


def _fox_gate_fwd(proj, b_pad):
    nblk = SEQ // CUM_BLK

    def body(f_ref, b_ref, col_ref):
        r = lax.broadcasted_iota(jnp.int32, (CUM_BLK, CUM_BLK), 0)
        c = lax.broadcasted_iota(jnp.int32, (CUM_BLK, CUM_BLK), 1)
        tri = (r >= c).astype(f32)
        carry = jnp.zeros((1, LANES), f32)
        for blk in range(nblk):
            z = f_ref[blk * CUM_BLK:(blk + 1) * CUM_BLK, :] + b_ref[...]
            logf = jnp.minimum(z, 0.0) - jnp.log1p(jnp.exp(-jnp.abs(z)))
            cs = jnp.dot(tri, logf, precision=lax.Precision.HIGHEST, preferred_element_type=f32) + carry
            col_ref[blk * CUM_BLK:(blk + 1) * CUM_BLK, :] = cs
            carry = cs[CUM_BLK - 1:CUM_BLK, :]

    return pl.pallas_call(
        body, grid=(1,), in_specs=[pl.BlockSpec((SEQ, LANES), lambda i: (0, C_F // LANES)),
                                   pl.BlockSpec((1, LANES), lambda i: (0, 0))],
        out_specs=pl.BlockSpec((SEQ, LANES), lambda i: (0, 0)),
        out_shape=SDS((SEQ, LANES), f32), name="fox_gate_fwd",
        compiler_params=_params(("arbitrary",)),
    )(proj, b_pad)


def _fox_gate_bwd(dF_row, proj, b_pad):
    nblk = SEQ // CUM_BLK

    def body(d_ref, f_ref, b_ref, df_ref, db_ref, col_ref):
        r = lax.broadcasted_iota(jnp.int32, (CUM_BLK, CUM_BLK), 0)
        c = lax.broadcasted_iota(jnp.int32, (CUM_BLK, CUM_BLK), 1)
        tri = (r <= c).astype(f32)
        lane = lax.broadcasted_iota(jnp.int32, (CUM_BLK, LANES), 1)
        col_ref[...] = d_ref[...].T
        carry = jnp.zeros((1, LANES), f32)
        total = jnp.zeros((1, LANES), f32)
        for blk in reversed(range(nblk)):
            rows = slice(blk * CUM_BLK, (blk + 1) * CUM_BLK)
            cs = jnp.dot(tri, col_ref[rows, :], precision=lax.Precision.HIGHEST, preferred_element_type=f32) + carry
            carry = cs[0:1, :]
            z = f_ref[rows, :] + b_ref[...]
            df = jnp.where(lane < N_FOX_HEADS, cs * _sigmoid(-z), 0.0)
            df_ref[rows, :] = df.astype(df_ref.dtype)
            total = total + _colsum(df)
        db_ref[...] = total

    return pl.pallas_call(
        body, grid=(1,), in_specs=[pl.BlockSpec((LANES, SEQ), lambda i: (0, 0)),
                                   pl.BlockSpec((SEQ, LANES), lambda i: (0, C_F // LANES)),
                                   pl.BlockSpec((1, LANES), lambda i: (0, 0))],
        out_specs=[pl.BlockSpec((SEQ, LANES), lambda i: (0, 0)), pl.BlockSpec((1, LANES), lambda i: (0, 0))],
        out_shape=(SDS((SEQ, LANES), bf16), SDS((1, LANES), f32)), name="fox_gate_bwd",
        scratch_shapes=[pltpu.VMEM((SEQ, LANES), f32)],
        compiler_params=_params(("arbitrary",)),
    )(dF_row, proj, b_pad)


def _nt(a, b):
    return lax.dot_general(a, b, (((1,), (1,)), ((), ())), preferred_element_type=f32)


def _tn(a, b):
    return lax.dot_general(a, b, (((0,), (0,)), ((), ())), preferred_element_type=f32)


def _fox_prep(proj, f_col):
    def fn(t, v):
        q, k, vv, fc = t
        lane = lax.broadcasted_iota(jnp.int32, (q.shape[0], LANES), 1)
        qs, ks = [], []
        for h in range(N_FOX_HEADS):
            pair, pos = divmod(h, 2)
            own = (lane >= pos * HEAD_DIM) & (lane < (pos + 1) * HEAD_DIM)
            base = (1 - pos) * HEAD_DIM
            f = fc[:, h:h + 1]
            hi = f.astype(bf16).astype(f32)
            mid = (f - hi).astype(bf16).astype(f32)
            lo = (f - hi) - mid
            one = jnp.ones_like(f)
            qa = jnp.where(own, q[:, pair * LANES:(pair + 1) * LANES] * ATT_SCALE, 0.0)
            ka = k[:, pair * LANES:(pair + 1) * LANES]
            for idx, (qv, kv) in enumerate([(hi, one), (mid, one), (lo, one), (one, -hi), (one, -mid), (one, -lo)]):
                sel = lane == base + idx
                qa = jnp.where(sel, qv, qa)
                ka = jnp.where(sel, kv, ka)
            qs.append(qa)
            ks.append(ka)
        return [jnp.concatenate(qs, axis=1), jnp.concatenate(ks, axis=1), vv], []
    w = N_FOX_HEADS * LANES
    return _rowwise(fn, "fox_prep", [(proj, FOX_W, C_QA // FOX_W), (proj, FOX_W, C_KA // FOX_W),
                                     (proj, FOX_W, C_VA // FOX_W), (f_col, LANES, 0)], [],
                    [(w, bf16), (w, bf16), (FOX_W, bf16)])


def _fox_fwd(q_aug, k_aug, v):
    blk = FOX_BLK
    npair = FOX_W // LANES

    def body(q_ref, k_ref, v_ref, o_ref, max_ref, sum_ref, s_scr):
        i = pl.program_id(1)
        tri = lax.broadcasted_iota(jnp.int32, (blk, blk), 0) >= lax.broadcasted_iota(jnp.int32, (blk, blk), 1)
        qh = [q_ref[:, h * LANES:(h + 1) * LANES] for h in range(2)]

        def logits(c, masked):
            off = pl.multiple_of(c * blk, blk)
            tops = []
            for h in range(2):
                s = _nt(qh[h], k_ref[pl.ds(off, blk), h * LANES:(h + 1) * LANES])
                if masked:
                    s = jnp.where(tri, s, NEG)
                s_scr[h, :, pl.ds(off, blk)] = s
                tops.append(_fold_lanes(s, jnp.maximum))
            return tops

        def pass_a(c, m):
            return tuple(jnp.maximum(a, b) for a, b in zip(m, logits(c, False)))

        m = lax.fori_loop(0, i, pass_a, tuple(jnp.full((blk, LANES), NEG, f32) for _ in range(2)))
        mx = [jnp.max(jnp.maximum(a, b), axis=1, keepdims=True) for a, b in zip(m, logits(i, True))]

        def pass_b(c, carry):
            off = pl.multiple_of(c * blk, blk)
            vv = v_ref[pl.ds(off, blk), :]
            new = []
            for h in range(2):
                l, acc = carry[h]
                p = jnp.exp(s_scr[h, :, pl.ds(off, blk)] - mx[h]).astype(bf16)
                new.append((l + _fold_lanes(p.astype(f32), jnp.add), acc + jnp.dot(p, vv, preferred_element_type=f32)))
            return tuple(new)

        zero = jnp.zeros((blk, LANES), f32)
        (l_a, acc_a), (l_b, acc_b) = lax.fori_loop(0, i + 1, pass_b, ((zero, zero), (zero, zero)))
        l_a = jnp.sum(l_a, axis=1, keepdims=True)
        l_b = jnp.sum(l_b, axis=1, keepdims=True)
        first = lax.broadcasted_iota(jnp.int32, (blk, LANES), 1) < HEAD_DIM
        o_ref[...] = jnp.where(first, acc_a / l_a, acc_b / l_b)
        max_ref[0] = jnp.where(first, mx[0], mx[1])
        sum_ref[0] = jnp.where(first, l_a, l_b)

    return pl.pallas_call(
        body, grid=(npair, SEQ // blk),
        in_specs=[pl.BlockSpec((blk, 2 * LANES), lambda p, i: (i, p)),
                  pl.BlockSpec((SEQ, 2 * LANES), lambda p, i: (0, p)),
                  pl.BlockSpec((SEQ, LANES), lambda p, i: (0, p))],
        out_specs=[pl.BlockSpec((blk, LANES), lambda p, i: (i, p))]
        + [pl.BlockSpec((1, blk, LANES), lambda p, i: (p, i, 0))] * 2,
        out_shape=(SDS((SEQ, FOX_W), f32),) + (SDS((npair, SEQ, LANES), f32),) * 2, name="fox_fwd",
        scratch_shapes=[pltpu.VMEM((2, blk, SEQ), f32)],
        compiler_params=_params(("parallel", "arbitrary")),
    )(q_aug, k_aug, v)


def _fox_bwd(q_aug, k_aug, v, do, o, row_max, row_sum, after):
    blk = FOX_BLK
    npair = FOX_W // LANES
    nblk = SEQ // blk

    def body(q_ref, k_ref, v_ref, do_ref, o_ref, max_ref, sum_ref, after_ref, dq_ref, dk_ref, dv_ref, df_ref, dq_acc,
             delta_ref, inv_ref):
        inv_ref[...] = 1.0 / sum_ref[0]
        lane_s = lax.broadcasted_iota(jnp.int32, (SEQ, LANES), 1)
        prod = do_ref[...].astype(bf16).astype(f32) * o_ref[...]
        d_a = jnp.sum(jnp.where(lane_s < HEAD_DIM, prod, 0.0), axis=1, keepdims=True)
        d_b = jnp.sum(jnp.where(lane_s >= HEAD_DIM, prod, 0.0), axis=1, keepdims=True)
        delta_ref[...] = jnp.where(lane_s < HEAD_DIM, d_a, d_b)
        dq_acc[...] = jnp.zeros_like(dq_acc)
        df_ref[...] = jnp.zeros_like(df_ref)
        lane = lax.broadcasted_iota(jnp.int32, (blk, LANES), 1)
        own = [lane < HEAD_DIM, lane >= HEAD_DIM]
        tri = lax.broadcasted_iota(jnp.int32, (blk, blk), 0) >= lax.broadcasted_iota(jnp.int32, (blk, blk), 1)

        def q_slab(qoff, h):
            return q_ref[pl.ds(qoff, blk), h * LANES:(h + 1) * LANES]

        def probs(qoff, h, k_h, masked):
            s = _nt(q_slab(qoff, h), k_h)
            if masked:
                s = jnp.where(tri, s, NEG)
            col = slice(h * HEAD_DIM, h * HEAD_DIM + 1)
            weights = jnp.exp(s - max_ref[0, pl.ds(qoff, blk), col]).astype(bf16).astype(f32)
            return weights * inv_ref[pl.ds(qoff, blk), col]

        def k_slabs(koff):
            return [k_ref[pl.ds(koff, blk), h * LANES:(h + 1) * LANES] for h in range(2)]

        def kv_step(kj, _):
            koff = pl.multiple_of(kj * blk, blk)
            k_aug = k_slabs(koff)
            k_own = [jnp.where(own[h], k_aug[h], jnp.zeros_like(k_aug[h])) for h in range(2)]
            vv = v_ref[pl.ds(koff, blk), :]
            v_own = [jnp.where(own[h], vv, jnp.zeros_like(vv)) for h in range(2)]

            def q_tile(qi, carry, masked):
                qoff = pl.multiple_of(qi * blk, blk)
                dd = do_ref[pl.ds(qoff, blk), :].astype(bf16)
                new, dq_add = [], None
                for h in range(2):
                    dk_h, dv_h, dcol = carry[h]
                    p = probs(qoff, h, k_aug[h], masked)
                    dl = p * (_nt(dd, v_own[h]) - delta_ref[pl.ds(qoff, blk), h * HEAD_DIM:h * HEAD_DIM + 1])
                    dlb = dl.astype(bf16)
                    part = jnp.dot(dlb, k_own[h], preferred_element_type=f32)
                    dq_add = part if dq_add is None else dq_add + part
                    new.append((dk_h + _tn(dlb, q_slab(qoff, h)), dv_h + _tn(p.astype(bf16), dd),
                                dcol + _colsum(dl)))
                dq_acc[pl.ds(qoff, blk), :] += dq_add * ATT_SCALE
                return tuple(new)

            zero = (jnp.zeros((blk, LANES), f32), jnp.zeros((blk, LANES), f32), jnp.zeros((1, blk), f32))
            carry = q_tile(kj, (zero, zero), True)
            (dk_a, dv_a, dcol_a), (dk_b, dv_b, dcol_b) = lax.fori_loop(
                kj + 1, nblk, lambda qi, cr: q_tile(qi, cr, False), carry)
            dk_ref[pl.ds(koff, blk), :] = jnp.where(own[0], dk_a, dk_b).astype(dk_ref.dtype)
            dv_ref[pl.ds(koff, blk), :] = jnp.where(own[0], dv_a, dv_b).astype(dv_ref.dtype)
            df_ref[0, 0:1, pl.ds(koff, blk)] = -dcol_a
            df_ref[0, 1:2, pl.ds(koff, blk)] = -dcol_b
            return 0

        lax.fori_loop(0, nblk, kv_step, 0)
        dq_ref[...] = dq_acc[...].astype(dq_ref.dtype)

    pair_aug = pl.BlockSpec((SEQ, 2 * LANES), lambda p: (0, p))
    slab = pl.BlockSpec((SEQ, LANES), lambda p: (0, p))
    per_pair = pl.BlockSpec((1, SEQ, LANES), lambda p: (p, 0, 0))
    rows = pl.BlockSpec((1, 8, SEQ), lambda p: (p, 0, 0))
    return pl.pallas_call(
        body, grid=(npair,),
        in_specs=[pair_aug, pair_aug, slab, slab, slab, per_pair, per_pair, pl.BlockSpec(memory_space=pl.ANY)],
        out_specs=[slab, slab, slab, rows],
        out_shape=(SDS((SEQ, FOX_W), bf16),) * 3 + (SDS((npair, 8, SEQ), f32),), name="fox_bwd",
        scratch_shapes=[pltpu.VMEM((SEQ, LANES), f32)] * 3,
        compiler_params=_params(("parallel",)),
    )(q_aug, k_aug, v, do, o, row_max, row_sum, after)


DIL_BLK = 128
DILATIONS = (1, 4, 16)
N_GROUPS = len(DILATIONS)
DIL_PAIRS = DIL_OUT_W // LANES


def _dil_blocks(d):
    r1 = lax.broadcasted_iota(jnp.int32, (2 * DIL_BLK, DIL_BLK), 0) & (DIL_BLK - 1)
    c1 = lax.broadcasted_iota(jnp.int32, (2 * DIL_BLK, DIL_BLK), 1)
    r2 = lax.broadcasted_iota(jnp.int32, (2 * DIL_BLK, 2 * DIL_BLK), 0) & (DIL_BLK - 1)
    c2 = lax.broadcasted_iota(jnp.int32, (2 * DIL_BLK, 2 * DIL_BLK), 1)
    band = ((c2 < DIL_BLK) & (c2 >= r2)) | ((c2 >= DIL_BLK) & (c2 - DIL_BLK <= r2))
    out = []
    for r in range(d):
        for b in range(SEQ // d // DIL_BLK):
            rows = pl.ds(r + d * DIL_BLK * b, DIL_BLK, stride=d)
            if b == 0:
                out.append((rows, rows, r1 >= c1))
            else:
                out.append((rows, pl.ds(r + d * DIL_BLK * (b - 1), 2 * DIL_BLK, stride=d), band))
    return out


def _dil_v_spec(g):
    return pl.BlockSpec((SEQ, LANES), lambda p: (0, C_VB // LANES + DIL_PAIRS * g + p))


def _stack_heads(t, first):
    zero = jnp.zeros_like(t)
    return jnp.concatenate([jnp.where(first, t, zero), jnp.where(first, zero, t)], axis=0)


def _dil_fwd(q, k, v, g):
    def body(q_ref, k_ref, v_ref, o_ref, lse_ref):
        first = lax.broadcasted_iota(jnp.int32, (DIL_BLK, LANES), 1) < HEAD_DIM
        for rows, krows, mask in _dil_blocks(DILATIONS[g]):
            qv, kk, vv = q_ref[rows, :].astype(bf16), k_ref[krows, :].astype(bf16), v_ref[krows, :].astype(bf16)
            s = jnp.where(mask, _nt(_stack_heads(qv, first), kk), NEG)
            m = jnp.max(s, axis=1, keepdims=True)
            p = jnp.exp(s - m)
            l = jnp.sum(p, axis=1, keepdims=True)
            out = jnp.dot(p.astype(bf16), vv, preferred_element_type=f32) / l
            lse = m + jnp.log(l)
            o_ref[rows, :] = jnp.where(first, out[:DIL_BLK], out[DIL_BLK:])
            lse_ref[rows, :] = jnp.where(first, lse[:DIL_BLK], lse[DIL_BLK:])

    grouped = pl.BlockSpec((SEQ, LANES), lambda p: (0, DIL_PAIRS * g + p))
    own = pl.BlockSpec((SEQ, LANES), lambda p: (0, p))
    shape = SDS((SEQ, DIL_OUT_W), f32)
    return pl.pallas_call(
        body, grid=(DIL_PAIRS,), in_specs=[grouped, grouped, _dil_v_spec(g)], out_specs=[own] * 2,
        out_shape=(shape, shape),
        name=f"dil_fwd_{DILATIONS[g]}", compiler_params=_params(("parallel",)),
    )(q, k, v)


def _dil_bwd(q, k, v, do, lse, delta, g):
    def body(q_ref, k_ref, v_ref, do_ref, lse_ref, dl_ref, dq_ref, dk_ref, dv_ref):
        first = lax.broadcasted_iota(jnp.int32, (DIL_BLK, LANES), 1) < HEAD_DIM
        dk_ref[...] = jnp.zeros_like(dk_ref)
        dv_ref[...] = jnp.zeros_like(dv_ref)
        for rows, krows, mask in _dil_blocks(DILATIONS[g]):
            qv, kk, vv = q_ref[rows, :].astype(bf16), k_ref[krows, :].astype(bf16), v_ref[krows, :].astype(bf16)
            lsev, delv = lse_ref[rows, :], dl_ref[rows, :]
            q2 = _stack_heads(qv, first)
            do2 = _stack_heads(do_ref[rows, :].astype(bf16), first)
            per_head = lambda t: jnp.concatenate([t[:, 0:1], t[:, HEAD_DIM:HEAD_DIM + 1]], axis=0)
            p = jnp.exp(jnp.where(mask, _nt(q2, kk), NEG) - per_head(lsev))
            dl = (p * (_nt(do2, vv) - per_head(delv))).astype(bf16)
            dq = jnp.dot(dl, kk, preferred_element_type=f32)
            dq_ref[rows, :] = jnp.where(first, dq[:DIL_BLK], dq[DIL_BLK:]) * ATT_SCALE
            dk_ref[krows, :] += _tn(dl, q2)
            dv_ref[krows, :] += _tn(p.astype(bf16), do2)

    grouped = pl.BlockSpec((SEQ, LANES), lambda p: (0, DIL_PAIRS * g + p))
    own = pl.BlockSpec((SEQ, LANES), lambda p: (0, p))
    shape = SDS((SEQ, DIL_OUT_W), f32)
    return pl.pallas_call(
        body, grid=(DIL_PAIRS,), in_specs=[grouped, grouped, _dil_v_spec(g)] + [own] * 3, out_specs=[own] * 3,
        out_shape=(shape, shape, shape), name=f"dil_bwd_{DILATIONS[g]}", compiler_params=_params(("parallel",)),
    )(q, k, v, do, lse, delta)


def _position():
    return lax.axis_index("x"), lax.axis_index("y"), lax.axis_index("c")


def _all_gather(block, name):
    def body(x_ref, out_ref, send_sems, recv_sems, local_sem):
        x, y, c = _position()
        me, sibling = (x, y, c), (x, y, 1 - c)
        chips = [(1 - x, y), (x, 1 - y), (1 - x, 1 - y)]

        def slot(px, py, pc):
            return out_ref.at[4 * px + 2 * py + pc]

        def copy(k, blk, to, src=None):
            return pltpu.make_async_remote_copy(
                src_ref=slot(*blk) if src is None else src, dst_ref=slot(*blk),
                send_sem=send_sems.at[k], recv_sem=recv_sems.at[k], device_id=to, device_id_type=MESH)

        mine = pltpu.make_async_copy(x_ref, slot(*me), local_sem)
        mine.start()
        first = [copy(0, me, sibling, src=x_ref)]
        first += [copy(1 + j, me, (*chip, c), src=x_ref) for j, chip in enumerate(chips)]
        for cp in first:
            cp.start()
        passed = [copy(4 + j, (*chip, c), sibling) for j, chip in enumerate(chips)]
        for j, chip in enumerate(chips):
            copy(1 + j, (*chip, c), me).wait_recv()
            passed[j].start()
        copy(0, sibling, me).wait_recv()
        for j, chip in enumerate(chips):
            copy(4 + j, (*chip, 1 - c), me).wait_recv()
        for cp in first + passed:
            cp.wait_send()
        mine.wait()

    return pl.pallas_call(
        body, out_shape=SDS((N_DEV,) + block.shape, block.dtype),
        in_specs=[pl.BlockSpec(memory_space=pl.ANY)], out_specs=pl.BlockSpec(memory_space=pl.ANY),
        scratch_shapes=[pltpu.SemaphoreType.DMA((7,)), pltpu.SemaphoreType.DMA((7,)), pltpu.SemaphoreType.DMA],
        name=name,
    )(block)


HBM_SPEC = pl.BlockSpec(memory_space=pltpu.HBM)
SEM_SPEC = pl.BlockSpec(memory_space=pltpu.SEMAPHORE)
SPLIT_COPY = pltpu.CompilerParams(has_side_effects=pltpu.SideEffectType.DATAFLOW_SIDE_EFFECTING)


def _in_hbm(t):
    return pltpu.with_memory_space_constraint(t, pltpu.HBM)


def _pair_copies(g_refs, land_refs, send_sems, recv_sems):
    x, y, c = _position()
    return [pltpu.make_async_remote_copy(
        src_ref=g.at[2 * k + (1 - c)], dst_ref=land.at[k], send_sem=send_sems.at[4 * a + k],
        recv_sem=recv_sems.at[4 * a + k], device_id=(x, y, 1 - c), device_id_type=MESH)
        for a, (g, land) in enumerate(zip(g_refs, land_refs, strict=True)) for k in range(4)]


def _chip_copies(t_refs, land_refs, send_sems, recv_sems):
    x, y, c = _position()
    chips = [(1 - x, y), (x, 1 - y), (1 - x, 1 - y)]
    return [pltpu.make_async_remote_copy(
        src_ref=t.at[2 * px + py], dst_ref=land.at[j], send_sem=send_sems.at[3 * a + j],
        recv_sem=recv_sems.at[3 * a + j], device_id=(px, py, c), device_id_type=MESH)
        for a, (t, land) in enumerate(zip(t_refs, land_refs, strict=True)) for j, (px, py) in enumerate(chips)]


_ROUNDS = {"pair": (_pair_copies, 4), "chip": (_chip_copies, 3)}


def _exchange_start(kind, ts, name):
    copies, slots = _ROUNDS[kind]
    n = len(ts)
    lands = [_in_hbm(lax.empty((slots,) + t.shape[1:], t.dtype)) for t in ts]

    def body(*refs):
        for cp in copies(refs[:n], refs[n:2 * n], refs[2 * n], refs[2 * n + 1]):
            cp.start()
        refs[-1][...] = jnp.zeros_like(refs[-1])

    sems = pltpu.SemaphoreType.DMA((slots * n,))
    res = pl.pallas_call(
        body, name=name, in_specs=[HBM_SPEC] * (2 * n),
        out_shape=(sems, sems, *[pltpu.HBM(t.shape, t.dtype) for t in (*ts, *lands)], SDS((8, LANES), f32)),
        out_specs=(SEM_SPEC, SEM_SPEC, *[HBM_SPEC] * (2 * n), pl.BlockSpec(memory_space=pltpu.VMEM)),
        input_output_aliases={i: 2 + i for i in range(2 * n)}, compiler_params=SPLIT_COPY,
    )(*[_in_hbm(t) for t in ts], *lands)
    return res[:-1], res[-1]


def _exchange_wait(kind, state, after, name):
    copies, _ = _ROUNDS[kind]
    send_sems, recv_sems, *arrays = state
    n = len(arrays) // 2

    def body(*refs):
        for cp in copies(refs[:n], refs[n:2 * n], refs[2 * n], refs[2 * n + 1]):
            cp.wait_send()
            cp.wait_recv()

    res = pl.pallas_call(
        body, name=name, in_specs=[HBM_SPEC] * (2 * n) + [SEM_SPEC, SEM_SPEC, pl.BlockSpec(memory_space=pl.ANY)],
        out_shape=[pltpu.HBM(t.shape, t.dtype) for t in arrays], out_specs=[HBM_SPEC] * (2 * n),
        input_output_aliases={i: i for i in range(2 * n)}, compiler_params=SPLIT_COPY,
    )(*arrays, send_sems, recv_sems, after)
    return res[:n], res[n:]


def _gather_copies(x_refs, out_refs, send_sems, recv_sems):
    x, y, c = _position()
    peers = [(x, y, 1 - c), (1 - x, y, c), (x, 1 - y, c), (1 - x, 1 - y, c)]
    sends, arrivals = [], []
    for a, (x_ref, out_ref) in enumerate(zip(x_refs, out_refs, strict=True)):
        for k, (px, py, pc) in enumerate(peers):
            sems = dict(send_sem=send_sems.at[4 * a + k], recv_sem=recv_sems.at[4 * a + k],
                        device_id=(px, py, pc), device_id_type=MESH)
            sends.append(pltpu.make_async_remote_copy(src_ref=x_ref, dst_ref=out_ref.at[4 * x + 2 * y + c], **sems))
            arrivals.append(pltpu.make_async_remote_copy(src_ref=x_ref, dst_ref=out_ref.at[4 * px + 2 * py + pc],
                                                         **sems))
    return sends, arrivals


def _gather_start(blocks, after, name):
    n = len(blocks)
    outs = [_in_hbm(lax.empty((N_DEV,) + b.shape, b.dtype)) for b in blocks]

    def body(*refs):
        sends, _ = _gather_copies(refs[:n], refs[n:2 * n], refs[2 * n + 1], refs[2 * n + 2])
        for cp in sends:
            cp.start()
        refs[-1][...] = jnp.zeros_like(refs[-1])

    sems = pltpu.SemaphoreType.DMA((4 * n,))
    res = pl.pallas_call(
        body, name=name, in_specs=[HBM_SPEC] * (2 * n) + [pl.BlockSpec(memory_space=pl.ANY)],
        out_shape=(sems, sems, *[pltpu.HBM(t.shape, t.dtype) for t in (*blocks, *outs)], SDS((8, LANES), f32)),
        out_specs=(SEM_SPEC, SEM_SPEC, *[HBM_SPEC] * (2 * n), pl.BlockSpec(memory_space=pltpu.VMEM)),
        input_output_aliases={i: 2 + i for i in range(2 * n)}, compiler_params=SPLIT_COPY,
    )(*[_in_hbm(b) for b in blocks], *outs, after)
    return res[:-1], res[-1]


def _gather_wait(state, after, name):
    send_sems, recv_sems, *arrays = state
    n = len(arrays) // 2

    def body(*refs):
        sends, arrivals = _gather_copies(refs[:n], refs[n:2 * n], refs[2 * n], refs[2 * n + 1])
        for cp in sends:
            cp.wait_send()
        for cp in arrivals:
            cp.wait_recv()

    res = pl.pallas_call(
        body, name=name, in_specs=[HBM_SPEC] * (2 * n) + [SEM_SPEC, SEM_SPEC, pl.BlockSpec(memory_space=pl.ANY)],
        out_shape=[pltpu.HBM(t.shape, t.dtype) for t in arrays], out_specs=[HBM_SPEC] * (2 * n),
        input_output_aliases={i: i for i in range(2 * n)}, compiler_params=SPLIT_COPY,
    )(*arrays, send_sems, recv_sems, after)
    return res[:n], res[n:]


def _gather_finish(partial, name):
    n = len(partial)

    def body(*refs):
        in_refs, out_refs = refs[:n], refs[n:2 * n]
        send_sems, recv_sems = refs[2 * n:]
        x, y, c = _position()
        chips = [(1 - x, y), (x, 1 - y), (1 - x, 1 - y)]
        copies = []
        for a in range(n):
            for j, (px, py) in enumerate(chips):
                cp = pltpu.make_async_remote_copy(
                    src_ref=in_refs[a].at[4 * px + 2 * py + c], dst_ref=out_refs[a].at[4 * px + 2 * py + c],
                    send_sem=send_sems.at[a, j], recv_sem=recv_sems.at[a, j], device_id=(x, y, 1 - c),
                    device_id_type=MESH)
                cp.start()
                copies.append(cp)
        for a in range(n):
            for j, (px, py) in enumerate(chips):
                pltpu.make_async_remote_copy(
                    src_ref=in_refs[a].at[4 * px + 2 * py + (1 - c)], dst_ref=out_refs[a].at[4 * px + 2 * py + (1 - c)],
                    send_sem=send_sems.at[a, j], recv_sem=recv_sems.at[a, j], device_id=(x, y, 1 - c),
                    device_id_type=MESH).wait_recv()
        for cp in copies:
            cp.wait_send()

    hbm = pl.BlockSpec(memory_space=pl.ANY)
    return pl.pallas_call(
        body, out_shape=[SDS(p.shape, p.dtype) for p in partial], in_specs=[hbm] * n, out_specs=[hbm] * n,
        input_output_aliases={a: a for a in range(n)},
        scratch_shapes=[pltpu.SemaphoreType.DMA((n, 3)), pltpu.SemaphoreType.DMA((n, 3))],
        name=name,
    )(*partial)


def _forward_copies(refs, send_sems, recv_sems):
    x, y, c = _position()
    sends, arrivals = [], []
    for a, ref in enumerate(refs):
        for j, (px, py) in enumerate([(1 - x, y), (x, 1 - y), (1 - x, 1 - y)]):
            sems = dict(send_sem=send_sems.at[3 * a + j], recv_sem=recv_sems.at[3 * a + j],
                        device_id=(x, y, 1 - c), device_id_type=MESH)
            mine, theirs = ref.at[4 * px + 2 * py + c], ref.at[4 * px + 2 * py + (1 - c)]
            sends.append(pltpu.make_async_remote_copy(src_ref=mine, dst_ref=mine, **sems))
            arrivals.append(pltpu.make_async_remote_copy(src_ref=theirs, dst_ref=theirs, **sems))
    return sends, arrivals


def _forward_start(arrays, after, name):
    n = len(arrays)

    def body(*refs):
        sends, _ = _forward_copies(refs[:n], refs[n + 1], refs[n + 2])
        for cp in sends:
            cp.start()
        refs[-1][...] = jnp.zeros_like(refs[-1])

    sems = pltpu.SemaphoreType.DMA((3 * n,))
    res = pl.pallas_call(
        body, name=name, in_specs=[HBM_SPEC] * n + [pl.BlockSpec(memory_space=pl.ANY)],
        out_shape=(sems, sems, *[pltpu.HBM(t.shape, t.dtype) for t in arrays], SDS((8, LANES), f32)),
        out_specs=(SEM_SPEC, SEM_SPEC, *[HBM_SPEC] * n, pl.BlockSpec(memory_space=pltpu.VMEM)),
        input_output_aliases={i: 2 + i for i in range(n)}, compiler_params=SPLIT_COPY,
    )(*[_in_hbm(t) for t in arrays], after)
    return res[:-1], res[-1]


def _forward_wait(state, after, name):
    send_sems, recv_sems, *arrays = state
    n = len(arrays)

    def body(*refs):
        sends, arrivals = _forward_copies(refs[:n], refs[n], refs[n + 1])
        for cp in sends:
            cp.wait_send()
        for cp in arrivals:
            cp.wait_recv()

    return pl.pallas_call(
        body, name=name, in_specs=[HBM_SPEC] * n + [SEM_SPEC, SEM_SPEC, pl.BlockSpec(memory_space=pl.ANY)],
        out_shape=[pltpu.HBM(t.shape, t.dtype) for t in arrays], out_specs=[HBM_SPEC] * n,
        input_output_aliases={i: i for i in range(n)}, compiler_params=SPLIT_COPY,
    )(*arrays, send_sems, recv_sems, after)


def _row_tile(rows):
    return 512 if rows % 512 == 0 and rows > 512 else rows


def _pair_add(g, r1, core, name):
    def body(c_ref, g_ref, r_ref, o_ref):
        o_ref[...] = (g_ref[...].astype(f32) + r_ref[...].astype(f32)).astype(o_ref.dtype)

    rows, cols = g.shape[1:]
    tile = _row_tile(rows)
    blk = (1, tile, cols)
    return pl.pallas_call(
        body, out_shape=SDS((4, rows, cols), g.dtype), name=name,
        grid_spec=pltpu.PrefetchScalarGridSpec(
            num_scalar_prefetch=1, grid=(4, rows // tile),
            in_specs=[pl.BlockSpec(blk, lambda k, i, c_ref: (2 * k + c_ref[0], i, 0)),
                      pl.BlockSpec(blk, lambda k, i, c_ref: (k, i, 0))],
            out_specs=pl.BlockSpec(blk, lambda k, i, c_ref: (k, i, 0))),
        compiler_params=_params(("parallel", "arbitrary")),
    )(core, g, r1)


def _chip_add(t, r2, chip, name, transposed=False):
    def body(c_ref, t_ref, r_ref, o_ref):
        s = ((t_ref[0].astype(f32) + r_ref[0].astype(f32)) + r_ref[1].astype(f32)) + r_ref[2].astype(f32)
        o_ref[...] = s.T if transposed else s

    rows, cols = t.shape[1:]
    tile = _row_tile(rows)
    out_spec = pl.BlockSpec((cols, tile), lambda i, c_ref: (0, i)) if transposed else pl.BlockSpec(
        (tile, cols), lambda i, c_ref: (i, 0))
    return pl.pallas_call(
        body, out_shape=SDS((cols, rows) if transposed else (rows, cols), f32), name=name,
        grid_spec=pltpu.PrefetchScalarGridSpec(
            num_scalar_prefetch=1, grid=(rows // tile,),
            in_specs=[pl.BlockSpec((1, tile, cols), lambda i, c_ref: (c_ref[0], i, 0)),
                      pl.BlockSpec((3, tile, cols), lambda i, c_ref: (0, i, 0))],
            out_specs=out_spec),
        compiler_params=_params(("arbitrary",)),
    )(chip, t, r2)


def _pad_to(t, axis, size):
    pads = [(0, 0)] * t.ndim
    pads[axis] = (0, size - t.shape[axis])
    return jnp.pad(t, pads)


_REF_COLS = {"qa": (0, FOX_W), "ka": (FOX_W, FOX_W), "va": (2 * FOX_W, FOX_W), "f": (3 * FOX_W, N_FOX_HEADS)}
_REF_COLS.update({n: (3 * FOX_W + N_FOX_HEADS + i * DIL_W, DIL_W) for i, n in enumerate(("qb", "kb", "vb"))})
_REF_COLS.update({n: (3 * FOX_W + N_FOX_HEADS + 3 * DIL_W + i * D, D) for i, n in enumerate(("ga", "gb"))})
_REF_ORDER = ("qa", "ka", "va", "f", "qb", "kb", "vb", "ga", "gb")


def _place_cols(sources, src_of, out_cols, name, row_block=512):
    arrays = [s[0] if isinstance(s, tuple) else s for s in sources]
    widths = [a.shape[-1] for a in arrays]
    rows = arrays[0].shape[-2]
    plan = []
    for t in range(out_cols // LANES):
        segs, c, end = [], t * LANES, (t + 1) * LANES
        while c < end:
            s = src_of(c)
            if s is None:
                c += 1
                continue
            n = 1
            while c + n < end and src_of(c + n) == (s[0], s[1] + n):
                n += 1
            segs.append((s[0], s[1], c - t * LANES, n))
            c += n
        plan.append(segs)

    def body(*refs):
        o_ref = refs[-1]
        for t, segs in enumerate(plan):
            acc = None
            for si, c0, o0, n in segs:
                a0 = c0 // LANES * LANES
                wide = min(2 * LANES, widths[si] - a0)
                win = refs[si][0, :, a0:a0 + wide] if isinstance(sources[si], tuple) else refs[si][:, a0:a0 + wide]
                r = lax.broadcasted_iota(jnp.int32, (wide, LANES), 0)
                c = lax.broadcasted_iota(jnp.int32, (wide, LANES), 1)
                pick = ((r - (c0 - a0) == c - o0) & (c >= o0) & (c < o0 + n)).astype(bf16)
                part = jnp.dot(win.astype(bf16), pick, preferred_element_type=f32)
                acc = part if acc is None else acc + part
            tile = jnp.zeros((row_block, LANES), f32) if acc is None else acc
            o_ref[:, t * LANES:(t + 1) * LANES] = tile.astype(o_ref.dtype)

    def spec(s):
        if isinstance(s, tuple):
            j = s[1]
            return pl.BlockSpec((1, row_block, s[0].shape[-1]), lambda i: (j, i, 0))
        return pl.BlockSpec((row_block, s.shape[-1]), lambda i: (i, 0))

    return pl.pallas_call(
        body, grid=(rows // row_block,), in_specs=[spec(s) for s in sources],
        out_specs=pl.BlockSpec((row_block, out_cols), lambda i: (i, 0)), out_shape=SDS((rows, out_cols), bf16),
        name=name, compiler_params=_params(("parallel",)),
    )(*arrays)


def _ref_piece(r):
    for name in _REF_ORDER:
        lo, width = _REF_COLS[name]
        if lo <= r < lo + width:
            return name, r - lo
    raise ValueError(r)


def _shard_pad_cols(pieces):
    names = [n for n in _REF_ORDER if n != "vb"]
    sources = [pieces[n] for n in names] + list(pieces["vb"])

    def src_of(c):
        j, i = divmod(c, W_IN_PAD)
        if i >= W_IN_SH:
            return None
        name, col = _ref_piece(j * W_IN_SH + i)
        if name == "vb":
            return len(names) + col // DIL_OUT_W, col % DIL_OUT_W
        return names.index(name), col

    return _place_cols(sources, src_of, N_DEV * W_IN_PAD, "place_dproj")


_SLABS = {"ga": C_GA, "gb": C_GB, "qb": C_QB, "kb": C_KB, "vb": C_VB, "qa": C_QA, "ka": C_KA, "va": C_VA, "f": C_F}


def _slab_w_in(stack):
    def src_of(c):
        for name, start in _SLABS.items():
            lo, width = _REF_COLS[name]
            if start <= c < start + width:
                return divmod(lo + c - start, W_IN_SH)
        return None

    return _place_cols([(stack, j) for j in range(N_DEV)], src_of, PROJ_W, "place_w_in")


def kernel(x, c, w_ada, b_ada, g_mix, w_in, b_fgate, w_br_a, w_br_b, w_out, g_ffn, w_ffn_gate, w_ffn_up, w_ffn_down, g_final, loss_target, m_w_ada, m_b_ada, m_g_mix, m_w_in, m_b_fgate, m_w_br_a, m_w_br_b, m_w_out, m_g_ffn, m_w_ffn_gate, m_w_ffn_up, m_w_ffn_down, m_g_final, v_w_ada, v_b_ada, v_g_mix, v_w_in, v_b_fgate, v_w_br_a, v_w_br_b, v_w_out, v_g_ffn, v_w_ffn_gate, v_w_ffn_up, v_w_ffn_down, v_g_final):
    px, py, pc = _position()
    dev = 4 * px + 2 * py + pc
    x2d, tgt = x[0], loss_target[0]

    c_all = _all_gather(c, "gather_c").reshape(N_DEV, D)
    ada_cols = w_ada.shape[2]
    b_shard = lax.dynamic_slice(b_ada, (0, dev * ada_cols), (1, ada_cols))
    mod_shard = _ada_fwd(c_all, w_ada[0], b_shard)
    mod_all = _all_gather(mod_shard, "gather_mod")
    modv = lax.dynamic_index_in_dim(mod_all, dev, axis=1, keepdims=False).reshape(6, D)
    h1 = _pre1(x2d, modv, g_mix)

    w_in_s = _all_gather(_pad_to(w_in[0], 1, W_IN_PAD).astype(bf16), "gather_w_in")
    gate_up = jnp.concatenate([_pad_to(w_ffn_gate[0], 1, FF_PAD), _pad_to(w_ffn_up[0], 1, FF_PAD)], axis=1)
    later = [w_br_a[0], w_br_b[0], w_out[0], gate_up, _pad_to(w_ffn_down[0], 0, FF_PAD)]
    later_state, later_token = _gather_start([t.astype(bf16) for t in later], w_in_s, "gather_rest_start")
    w_in_p = _slab_w_in(w_in_s)

    proj = _matmul(h1, w_in_p, name="mm_proj", tm=SEQ, tn=896, tk=D, after=later_token)
    b_pad = jnp.pad(b_fgate, ((0, 0), (0, LANES - N_FOX_HEADS)))
    q_aug, k_aug, va = _fox_prep(proj, _fox_gate_fwd(proj, b_pad))
    ya_h, max_a, sum_a = _fox_fwd(q_aug, k_aug, va)

    tables = _rope_tables()
    qb_r, kb_r = _rope_fwd(proj, tables)
    by_group = [_dil_fwd(qb_r, kb_r, proj, grp) for grp in range(N_GROUPS)]
    yb_h, lse_b = _dil_combine([o for o, _ in by_group], [l for _, l in by_group])

    both_done = ya_h[:8, :LANES] + yb_h[:8, :LANES]
    mine, arrived = _gather_wait(later_state, both_done, "gather_rest_wait")
    own_block = lambda stack, block: lax.dynamic_update_slice(stack, block[None], (dev, 0, 0))
    small_stacks = _gather_finish(arrived[:3], "gather_rest_finish")
    ffn_forward, ffn_forward_token = _forward_start(arrived[3:], small_stacks[0], "gather_ffn_forward_start")
    w_a_s, w_b_s, w_o_s = map(own_block, small_stacks, mine[:3])
    w_o = w_o_s.reshape(D, D)
    ya = _matmul_stack(ya_h, w_a_s, name="mm_br_a")
    yb = _matmul_stack(yb_h, w_b_s, name="mm_br_b")

    merged, mix, x1, h2 = _post1(ya, yb, proj, w_o, x2d, modv, g_ffn, ffn_forward_token)
    w_gu_s, w_d_s = map(own_block, _forward_wait(ffn_forward, h2, "gather_ffn_forward_wait"), mine[3:])
    w_d = w_d_s.reshape(FF_HID, D)
    act, au = _ffn_in(h2, w_gu_s)

    dx2, dff, dg_final, dga_f, loss_lanes = _final(act, w_d, x1, tgt, modv, g_final.reshape(1, D))
    dau = _ffn_bwd_in(dff, w_d_s, au)

    core = pc.astype(jnp.int32).reshape(1)
    chip = (2 * px + py).astype(jnp.int32).reshape(1)

    def pair_done(state, after, tags, name):
        mine, theirs = _exchange_wait("pair", state, after, "pair_wait_" + name)
        sums = [_pair_add(g, r, core, "pair_add_" + t) for g, r, t in zip(mine, theirs, tags)]
        return _exchange_start("chip", sums, "chip_start_" + name)

    def from_chips(state, after, tags, name, transposed=None):
        sums, got = _exchange_wait("chip", state, after, "chip_wait_" + name)
        flips = transposed or [False] * len(tags)
        return [_chip_add(p, r, chip, "chip_add_" + t, f) for p, r, t, f in zip(sums, got, tags, flips)]

    g_gu = _matmul(h2, dau, ta=True, by_shard=True, out_dtype=bf16, name="mm_g_ffn_in", tm=D, tn=2 * FF_PAD, tk=SEQ)
    g_d = _matmul(act, dff, ta=True, out_dtype=bf16, name="mm_g_down", tm=FF_HID // 2, tn=512, tk=SEQ)
    ffn_tags = ["gu", "down"]
    ffn_pair, ffn_pair_token = _exchange_start("pair", [g_gu, g_d.reshape(N_DEV, FF_PAD, D)], "pair_start_ffn")

    dx1, dmix, dsh_f, dsc_f, dg_ffn, dga_m = _mid_bwd(dau, w_gu_s, ffn_pair_token, x1, dx2, mix, modv, g_ffn)
    ffn_state, ffn_token = pair_done(ffn_pair, dx1, ffn_tags, "ffn")
    dya, dyb, dga, dgb = _merge_bwd(dmix, w_o, ffn_token, ya, yb, proj)
    dya_h = _matmul_stack(dya, w_a_s, tb=True, name="mm_d_ya")
    dyb_h = _matmul_stack(dyb, w_b_s, tb=True, name="mm_d_yb")

    g_o = _matmul(merged, dmix, ta=True, out_dtype=bf16, name="mm_g_out", tm=D, tn=512, tk=SEQ)
    g_a = _matmul_stack(ya_h, dya, ta=True, out_dtype=bf16, name="mm_g_br_a")
    g_b = _matmul_stack(yb_h, dyb, ta=True, out_dtype=bf16, name="mm_g_br_b")
    rows_a, rows_b = FOX_W * W_BR_SH // D, DIL_OUT_W * W_BR_SH // D
    g_small = jnp.concatenate([g_a.reshape(N_DEV, rows_a, D), g_b.reshape(N_DEV, rows_b, D),
                               g_o.reshape(N_DEV, W_BR_SH, D)], axis=1)
    small_pair, small_pair_token = _exchange_start("pair", [g_small], "pair_start_small")

    dqa, dka, dva, dF = _fox_bwd(q_aug, k_aug, va, dya_h, ya_h, max_a, sum_a, small_pair_token)
    dF_row = jnp.pad(dF[:, :2, :].reshape(N_FOX_HEADS, SEQ), ((0, LANES - N_FOX_HEADS), (0, 0)))
    df, db_fgate = _fox_gate_bwd(dF_row, proj, b_pad)
    small_state, small_token = pair_done(small_pair, df, ["small"], "small")

    delta_b = _dil_delta(dyb_h, yb_h)
    dil_grads = [_dil_bwd(qb_r, kb_r, proj, dyb_h, lse_b, delta_b, grp) for grp in range(N_GROUPS)]
    dqb, dkb = _rope_bwd([t[0] for t in dil_grads], [t[1] for t in dil_grads], tables)

    dproj = _shard_pad_cols({"qa": dqa, "ka": dka, "va": dva, "f": df, "qb": dqb, "kb": dkb,
                             "vb": [t[2] for t in dil_grads], "ga": dga, "gb": dgb})
    g_in = _matmul(h1, dproj, ta=True, by_shard=True, out_dtype=bf16, name="mm_g_in", tm=D, tn=W_IN_PAD, tk=SEQ,
                   after=small_token)
    mix_tags = ["in"]
    mix_pair, mix_pair_token = _exchange_start("pair", [g_in], "pair_start_mixer")

    grad_x, dsh_m, dsc_m, dg_mix = _first_bwd(dproj, w_in_s, mix_pair_token, x2d, dx1, modv, g_mix)

    pad_lane = lambda t: jnp.pad(t, ((0, 0), (0, D - t.shape[1])))
    small = jnp.concatenate([dsh_m, dsc_m, dga_m, dsh_f, dsc_f, dga_f, dg_mix, dg_ffn, dg_final,
                             pad_lane(db_fgate), loss_lanes, jnp.zeros((SMALL_ROWS - 11, D), f32)], axis=0)
    small_all = _all_gather(small, "gather_small")
    mix_state, mix_token = pair_done(mix_pair, small_all, mix_tags, "mixer")

    small_sum, loss_row = _small_reduce(small_all, mix_token)
    dmod_all = small_all[:, :6, :].reshape(N_DEV, 6 * D)
    g_w_ada = _ada_bwd(c_all, lax.dynamic_slice(dmod_all, (0, dev * ada_cols), (N_DEV, ada_cols)))
    s_gu_t, s_d = from_chips(ffn_state, small_sum, ffn_tags, "ffn", [True, False])
    s_small, = from_chips(small_state, small_sum, ["small"], "small")

    loss = loss_row[0, 0]
    g = {
        "w_ada": g_w_ada[None], "b_ada": small_sum[0:6].reshape(1, 6 * D), "g_mix": small_sum[6:7],
        "b_fgate": small_sum[9:10, :N_FOX_HEADS], "g_ffn": small_sum[7:8], "w_ffn_gate": s_gu_t[:W_FF_SH],
        "w_ffn_up": s_gu_t[FF_PAD:FF_PAD + W_FF_SH], "w_ffn_down": s_d[None, :W_FF_SH],
        "g_final": small_sum[8], "w_br_a": s_small[:rows_a].reshape(1, FOX_W, W_BR_SH),
        "w_br_b": s_small[rows_a:rows_a + rows_b].reshape(1, DIL_OUT_W, W_BR_SH), "w_out": s_small[None, rows_a + rows_b:],
    }
    w = {"w_ada": w_ada, "b_ada": b_ada, "g_mix": g_mix, "w_in": w_in, "b_fgate": b_fgate, "w_br_a": w_br_a,
         "w_br_b": w_br_b, "w_out": w_out, "g_ffn": g_ffn, "w_ffn_gate": w_ffn_gate, "w_ffn_up": w_ffn_up,
         "w_ffn_down": w_ffn_down, "g_final": g_final}
    m = {"w_ada": m_w_ada, "b_ada": m_b_ada, "g_mix": m_g_mix, "w_in": m_w_in, "b_fgate": m_b_fgate,
         "w_br_a": m_w_br_a, "w_br_b": m_w_br_b, "w_out": m_w_out, "g_ffn": m_g_ffn, "w_ffn_gate": m_w_ffn_gate,
         "w_ffn_up": m_w_ffn_up, "w_ffn_down": m_w_ffn_down, "g_final": m_g_final}
    v = {"w_ada": v_w_ada, "b_ada": v_b_ada, "g_mix": v_g_mix, "w_in": v_w_in, "b_fgate": v_b_fgate,
         "w_br_a": v_w_br_a, "w_br_b": v_w_br_b, "w_out": v_w_out, "g_ffn": v_g_ffn, "w_ffn_gate": v_w_ffn_gate,
         "w_ffn_up": v_w_ffn_up, "w_ffn_down": v_w_ffn_down, "g_final": v_g_final}
    names = list(w)
    delta, new_m, new_v = {}, {}, {}

    transposed = ("w_in", "w_ffn_gate", "w_ffn_up")

    def update(n):
        shape = w[n].shape
        if n in transposed:
            g_t = g[n]
            dl, mn, vn = _adamw(w[n][0].T, g_t, m[n][0].T, v[n][0].T, "adamw_" + n)
            g[n], delta[n], new_m[n], new_v[n] = g_t.T[None], dl.T[None], mn.T[None], vn.T[None]
            return
        two_d = (lambda t: t.reshape(shape[-2:])) if len(shape) == 3 else (lambda t: t)
        dl, mn, vn = _adamw(two_d(w[n]), two_d(g[n]), two_d(m[n]), two_d(v[n]), "adamw_" + n)
        delta[n], new_m[n], new_v[n] = dl.reshape(shape), mn.reshape(shape), vn.reshape(shape)

    for n in list(g):
        update(n)
    done = sum(delta[n].reshape(-1)[:N_FOX_HEADS] for n in g)
    s_in_t, = from_chips(mix_state, done, mix_tags, "mixer", [True])
    g["w_in"] = s_in_t[:W_IN_SH]
    update("w_in")

    return (loss, grad_x[None], *[g[n] for n in names], *[delta[n] for n in names],
            *[new_m[n] for n in names], *[new_v[n] for n in names])
```

```python
import jax
import jax.numpy as jnp
import numpy as np
from jax import lax
from jax.experimental import pallas as pl
from jax.experimental.pallas import tpu as pltpu

f32 = jnp.float32
bf16 = jnp.bfloat16
SDS = jax.ShapeDtypeStruct
MESH = pl.DeviceIdType.MESH

N_DEV = 8
D = 1024
SEQ = 2048
HEAD_DIM = 64
N_FOX_HEADS = 8
FOX_W = 512
DIL_W = 768
DIL_OUT_W = 256
ROT_DIM = 16
ROPE_THETA = 500000.0
D_FF = 2816
IN_COLS = 5896
EPS = 1e-6
NEG = -1e30
ATT_SCALE = HEAD_DIM ** -0.5

ADAM_LR = 0.001
ADAM_B1 = 0.9
ADAM_B2 = 0.999
ADAM_EPS = 1e-08
ADAM_WD = 0.01
ADAM_STEP = 10

C_GA, C_GB, C_QB, C_KB, C_VB, C_QA, C_KA, C_VA, C_F = 0, 1024, 2304, 3072, 3840, 4608, 5120, 5632, 6144
PROJ_W = 6272
LANES = 128
VMEM_LIMIT = 52 * 1024 * 1024

W_IN_SH, W_IN_PAD = IN_COLS // N_DEV, 768
W_BR_SH = D // N_DEV
W_FF_SH, FF_PAD = D_FF // N_DEV, 384
FF_HID = N_DEV * FF_PAD
SMALL_ROWS = 16


def _params(sem=None):
    if sem is None:
        return pltpu.CompilerParams(vmem_limit_bytes=VMEM_LIMIT)
    return pltpu.CompilerParams(dimension_semantics=sem, vmem_limit_bytes=VMEM_LIMIT)


def _rowwise(fn, name, tiled, vecs, outs, reds=(), tile=256):
    nt, nv, no = len(tiled), len(vecs), len(outs)
    rows = tiled[0][0].shape[0]
    assert rows % tile == 0

    def body(*refs):
        tin = [r[...] for r in refs[:nt]]
        vin = [r[...] for r in refs[nt:nt + nv]]
        orefs = refs[nt + nv:nt + nv + no]
        rrefs = refs[nt + nv + no:]
        touts, routs = fn(tin, vin)
        for r, t in zip(orefs, touts, strict=True):
            r[...] = t.astype(r.dtype)
        if rrefs:
            @pl.when(pl.program_id(0) == 0)
            def _():
                for r in rrefs:
                    r[...] = jnp.zeros_like(r)
            for r, t in zip(rrefs, routs, strict=True):
                r[...] += t

    def col_map(cb):
        return lambda i: (i, cb)

    def whole_map(nd):
        return lambda i: (0,) * nd

    in_specs = [pl.BlockSpec((tile, w), col_map(cb)) for (_, w, cb) in tiled]
    in_specs += [pl.BlockSpec(v.shape, whole_map(v.ndim)) for v in vecs]
    out_specs = [pl.BlockSpec((tile, w), lambda i: (i, 0)) for (w, _) in outs]
    out_specs += [pl.BlockSpec((1, w), lambda i: (0, 0)) for w in reds]
    out_shape = [SDS((rows, w), dt) for (w, dt) in outs] + [SDS((1, w), f32) for w in reds]
    res = pl.pallas_call(
        body, grid=(rows // tile,), in_specs=in_specs, out_specs=out_specs, out_shape=out_shape, name=name,
        compiler_params=_params(("arbitrary",)),
    )(*[t[0] for t in tiled], *vecs)
    return res


def _matmul(a, b, *, ta=False, out_dtype=f32, name, tm, tn, by_shard=False, after=None):
    (m, k), n = ((a.shape[1], a.shape[0]) if ta else a.shape), b.shape[1]
    assert b.shape[0] == k and m % tm == 0 and n % tn == 0 and (ta or not by_shard)
    dims = (((0 if ta else 1,), (0,)), ((), ()))

    def body(a_ref, b_ref, *rest):
        p = lax.dot_general(a_ref[...].astype(bf16), b_ref[...].astype(bf16), dims, preferred_element_type=f32)
        o_ref = rest[-1]
        if by_shard:
            o_ref[0] = p.astype(o_ref.dtype)
        else:
            o_ref[...] = p.astype(o_ref.dtype)

    a_spec = pl.BlockSpec((k, tm), lambda i, j: (0, i)) if ta else pl.BlockSpec((tm, k), lambda i, j: (i, 0))
    if by_shard:
        assert tn == n // N_DEV
        out_spec, out_shape = pl.BlockSpec((1, tm, tn), lambda i, j: (j, i, 0)), SDS((N_DEV, m, tn), out_dtype)
    else:
        out_spec, out_shape = pl.BlockSpec((tm, tn), lambda i, j: (i, j)), SDS((m, n), out_dtype)
    extra_specs, extra = ([pl.BlockSpec(memory_space=pl.ANY)], [after]) if after is not None else ([], [])
    return pl.pallas_call(
        body, grid=(m // tm, n // tn), in_specs=[a_spec, pl.BlockSpec((k, tn), lambda i, j: (0, j))] + extra_specs,
        out_specs=out_spec, out_shape=out_shape, name=name, compiler_params=_params(("parallel", "parallel")),
    )(a, b, *extra)


def _matmul_stack(a, b, *, ta=False, tb=False, out_dtype=f32, name):
    def lanes(ref):
        return jnp.concatenate([ref[j] for j in range(N_DEV)], axis=1).astype(bf16)

    if ta:
        w = b.shape[1] // N_DEV

        def body(a_ref, b_ref, o_ref):
            p = _tn(a_ref[...].astype(bf16), b_ref[...].astype(bf16))
            for j in range(N_DEV):
                o_ref[j] = p[:, j * w:(j + 1) * w].astype(o_ref.dtype)

        return pl.pallas_call(body, out_shape=SDS((N_DEV, a.shape[1], w), out_dtype), name=name,
                              compiler_params=_params())(a, b)

    m, half = a.shape[0], a.shape[0] // 2
    n = b.shape[1] if tb else N_DEV * b.shape[2]

    def body(a_ref, b_ref, o_ref):
        av = a_ref[...].astype(bf16)
        o_ref[...] = (_nt(av, lanes(b_ref)) if tb else jnp.dot(av, lanes(b_ref), preferred_element_type=f32)
                      ).astype(o_ref.dtype)

    return pl.pallas_call(
        body, grid=(2,), in_specs=[pl.BlockSpec((half, a.shape[1]), lambda i: (i, 0)),
                                   pl.BlockSpec(b.shape, lambda i: (0, 0, 0))],
        out_specs=pl.BlockSpec((half, n), lambda i: (i, 0)), out_shape=SDS((m, n), out_dtype), name=name,
        compiler_params=_params(("parallel",)),
    )(a, b)


def _matmul_rows(form, a, b, after, fn, tiled, vecs, outs, reds, *, name, tm=512):
    norm = lambda ts: [t if isinstance(t, tuple) else (t, t.shape[1], 0) for t in ts]
    make, sources = a if isinstance(a, tuple) else (None, [a])
    sources, tiled = norm(sources), norm(tiled)
    m, k = sources[0][0].shape[0], (b.shape[0] if form == "nn" else b.shape[-1] * (N_DEV if form == "nt_stack" else 1))
    assert m % tm == 0
    ns, nt, nv, no = len(sources), len(tiled), len(vecs), len(outs)

    def body(*refs):
        src_refs, b_ref, refs = refs[:ns], refs[ns], refs[ns + 2:]
        if make is None:
            lhs = lambda lo, hi: src_refs[0][:, lo:hi]
        else:
            made = make([r[...] for r in src_refs]).astype(bf16)
            lhs = lambda lo, hi: made[:, lo:hi]
        if form == "nt_stack":
            w = b.shape[2]
            acc = _nt(lhs(0, w), b_ref[0])
            for j in range(1, N_DEV):
                acc = acc + _nt(lhs(j * w, (j + 1) * w), b_ref[j])
        elif form == "nt":
            acc = _nt(lhs(0, k), b_ref[...])
        else:
            acc = jnp.dot(lhs(0, k), b_ref[...], preferred_element_type=f32)
        if make is not None:
            refs[nt + nv][...] = made
            refs = refs[:nt + nv] + refs[nt + nv + 1:]
        orefs, rrefs = refs[nt + nv:nt + nv + no], refs[nt + nv + no:]
        touts, routs = fn([acc] + [r[...] for r in refs[:nt]], [r[...] for r in refs[nt:nt + nv]])
        for r, t in zip(orefs, touts, strict=True):
            r[...] = t.astype(r.dtype)

        @pl.when(pl.program_id(0) == 0)
        def _():
            for r in rrefs:
                r[...] = jnp.zeros_like(r)
        for r, t in zip(rrefs, routs, strict=True):
            r[...] += t

    def whole_map(nd):
        return lambda i: (0,) * nd

    def rows(width, cb=0):
        return pl.BlockSpec((tm, width), lambda i: (i, cb))

    made_out = [(k, bf16)] if make is not None else []
    return pl.pallas_call(
        body, grid=(m // tm,),
        in_specs=[rows(width, cb) for _, width, cb in sources]
        + [pl.BlockSpec(b.shape, whole_map(b.ndim), pipeline_mode=pl.Buffered(1)), pl.BlockSpec(memory_space=pl.ANY)]
        + [rows(width, cb) for _, width, cb in tiled] + [pl.BlockSpec(v.shape, whole_map(v.ndim)) for v in vecs],
        out_specs=[rows(width) for width, _ in made_out + list(outs)]
        + [pl.BlockSpec((1, width), lambda i: (0, 0)) for width in reds],
        out_shape=[SDS((m, width), dt) for width, dt in made_out + list(outs)]
        + [SDS((1, width), f32) for width in reds], name=name,
        compiler_params=_params(("arbitrary",)),
    )(*[t[0] for t in sources], b, after, *[t[0] for t in tiled], *vecs)


def _rms(x):
    r = lax.rsqrt(jnp.mean(x * x, axis=-1, keepdims=True) + EPS)
    return r, x * r


def _rms_bwd(r, xn, dxn):
    return r * (dxn - xn * jnp.mean(dxn * xn, axis=-1, keepdims=True))


def _colsum(t):
    return jnp.sum(t, axis=0, keepdims=True)


def _sigmoid(x):
    return 0.5 * jnp.tanh(0.5 * x) + 0.5


def _modulated_norm(x, g, shift, scale):
    _, xn = _rms(x)
    return (xn * g) * (1.0 + scale) + shift


def _pre1(x, modv, g_mix):
    def fn(t, v):
        (xt,), (mv, g) = t, v
        return [_modulated_norm(xt, g, mv[0:1], mv[1:2])], []
    return _rowwise(fn, "pre1", [(x, D, 0)], [modv, g_mix], [(D, bf16)])[0]


def _post1(ya, yb, proj, w_o, x, modv, g_ffn):
    def merge(t):
        ya_t, yb_t, ga, gb = t
        return _sigmoid(ga) * ya_t + _sigmoid(gb) * yb_t

    def fn(t, v):
        (mt, xt), (mv, g) = t, v
        x1 = xt + mv[2:3] * mt
        return [mt, x1, _modulated_norm(x1, g, mv[3:4], mv[4:5])], []
    return _matmul_rows("nn", (merge, [ya, yb, (proj, D, C_GA // D), (proj, D, C_GB // D)]), w_o, x, fn, [x],
                        [modv, g_ffn], [(D, f32), (D, f32), (D, bf16)], [], name="post1")


def _ffn_in(h, w_stack):
    def body(h_ref, w_ref, act_ref, au_ref):
        p = jnp.dot(h_ref[...], w_ref[0], preferred_element_type=f32)
        a, u = p[:, :FF_PAD], p[:, FF_PAD:]
        act_ref[...] = (a * _sigmoid(a) * u).astype(act_ref.dtype)
        au_ref[...] = p.astype(au_ref.dtype)

    return pl.pallas_call(
        body, grid=(N_DEV,),
        in_specs=[pl.BlockSpec((SEQ, D), lambda j: (0, 0)), pl.BlockSpec((1, D, 2 * FF_PAD), lambda j: (j, 0, 0))],
        out_specs=[pl.BlockSpec((SEQ, FF_PAD), lambda j: (0, j)), pl.BlockSpec((SEQ, 2 * FF_PAD), lambda j: (0, j))],
        out_shape=(SDS((SEQ, FF_HID), bf16), SDS((SEQ, 2 * FF_HID), bf16)), name="ffn_in",
        compiler_params=_params(("parallel",)),
    )(h, w_stack)


def _ffn_bwd_in(dff, w_down_stack, au):
    def body(d_ref, w_ref, au_ref, o_ref):
        dact = _nt(d_ref[...], w_ref[0])
        p = au_ref[...].astype(f32)
        a, u = p[:, :FF_PAD], p[:, FF_PAD:]
        sg = _sigmoid(a)
        o_ref[...] = jnp.concatenate([dact * u * (sg * (1.0 + a * (1.0 - sg))), dact * (a * sg)],
                                     axis=1).astype(o_ref.dtype)

    return pl.pallas_call(
        body, grid=(N_DEV,),
        in_specs=[pl.BlockSpec((SEQ, D), lambda j: (0, 0)), pl.BlockSpec((1, FF_PAD, D), lambda j: (j, 0, 0)),
                  pl.BlockSpec((SEQ, 2 * FF_PAD), lambda j: (0, j))],
        out_specs=pl.BlockSpec((SEQ, 2 * FF_PAD), lambda j: (0, j)),
        out_shape=SDS((SEQ, 2 * FF_HID), bf16), name="ffn_bwd_in", compiler_params=_params(("parallel",)),
    )(dff, w_down_stack, au)


def _final(act, w_down, x1, target, modv, g_final):
    def fn(t, v):
        (fft, x1t, tgt), (mv, g) = t, v
        x2 = x1t + mv[5:6] * fft
        r, xn = _rms(x2)
        err = xn * g - tgt
        dy = err * (1.0 / D)
        dx2 = _rms_bwd(r, xn, dy * g)
        return [dx2, dx2 * mv[5:6]], [_colsum(dy * xn), _colsum(dx2 * fft), _colsum(err * err) * (0.5 / D)]
    return _matmul_rows("nn", act, w_down, x1, fn, [x1, target], [modv, g_final], [(D, f32), (D, bf16)], [D, D, D],
                        name="final")


def _mid_bwd(dau, w_stack, after, x1, dx2, mix, modv, g_ffn):
    def fn(t, v):
        (dh, x1t, dx2t, mt), (mv, g) = t, v
        r, xn = _rms(x1t)
        dn = dh * (1.0 + mv[4:5])
        dx1 = dx2t + _rms_bwd(r, xn, dn * g)
        return [dx1, dx1 * mv[2:3]], [_colsum(dh), _colsum(dh * (xn * g)), _colsum(dn * xn), _colsum(dx1 * mt)]
    return _matmul_rows("nt_stack", dau, w_stack, after, fn, [x1, dx2, mix], [modv, g_ffn], [(D, f32), (D, bf16)],
                        [D, D, D, D], name="mid_bwd")


def _first_bwd(dproj, w_stack, after, x, dx1, modv, g_mix):
    def fn(t, v):
        (dh, xt, dx1t), (mv, g) = t, v
        r, xn = _rms(xt)
        dn = dh * (1.0 + mv[1:2])
        return [dx1t + _rms_bwd(r, xn, dn * g)], [_colsum(dh), _colsum(dh * (xn * g)), _colsum(dn * xn)]
    return _matmul_rows("nt_stack", dproj, w_stack, after, fn, [x, dx1], [modv, g_mix], [(D, f32)], [D, D, D],
                        name="first_bwd")


def _merge_bwd(dmix, w_o, after, ya, yb, proj):
    def fn(t, v):
        dm, ya_t, yb_t, ga, gb = t
        sa, sb = _sigmoid(ga), _sigmoid(gb)
        return [dm * sa, dm * sb, dm * ya_t * (sa * (1.0 - sa)), dm * yb_t * (sb * (1.0 - sb))], []
    return _matmul_rows("nt", dmix, w_o, after, fn, [ya, yb, (proj, D, C_GA // D), (proj, D, C_GB // D)], [],
                        [(D, bf16), (D, bf16), (D, bf16), (D, bf16)], [], name="merge_bwd")


def _rope_tables():
    half = ROT_DIM // 2
    pos = np.arange(SEQ, dtype=np.float32)
    inv_freq = np.float32(ROPE_THETA) ** (-np.arange(0, ROT_DIM, 2, dtype=np.float32) / np.float32(ROT_DIM))
    ang = pos[:, None] * inv_freq[None, :].astype(np.float32)
    cos, sin = np.cos(ang).astype(np.float32), np.sin(ang).astype(np.float32)
    pad = np.zeros((SEQ, HEAD_DIM - ROT_DIM), np.float32)
    zero = np.zeros((SEQ, half), np.float32)
    c_head = np.concatenate([cos, cos, pad + 1.0], axis=1)
    lo_head = np.concatenate([-sin, zero, pad], axis=1)
    hi_head = np.concatenate([zero, sin, pad], axis=1)
    return tuple(jnp.asarray(np.concatenate([t, t], axis=1)) for t in (c_head, lo_head, hi_head))


def _over_heads(tables):
    return [jnp.tile(t, (1, DIL_W // LANES)) for t in tables]


def _rope_fwd(proj, tables):
    half = ROT_DIM // 2

    def fn(t, v):
        q, k = t[:2]
        c, lo, hi = _over_heads(t[2:])
        rot = lambda z: z * c + pltpu.roll(z, DIL_W - half, 1) * lo + pltpu.roll(z, half, 1) * hi
        return [rot(q) * ATT_SCALE, rot(k)], []
    return _rowwise(fn, "rope_fwd", [(proj, DIL_W, C_QB // DIL_W), (proj, DIL_W, C_KB // DIL_W)]
                    + [(tb, LANES, 0) for tb in tables], [], [(DIL_W, f32)] * 2)


def _rope_bwd(dqs, dks, tables):
    half = ROT_DIM // 2

    def fn(t, v):
        dq_t, dk_t = jnp.concatenate(t[:N_GROUPS], axis=1), jnp.concatenate(t[N_GROUPS:2 * N_GROUPS], axis=1)
        c, lo, hi = _over_heads(t[2 * N_GROUPS:])
        rot_t = lambda z: z * c + pltpu.roll(z * lo, half, 1) + pltpu.roll(z * hi, DIL_W - half, 1)
        return [rot_t(dq_t), rot_t(dk_t)], []
    return _rowwise(fn, "rope_bwd", [(a, DIL_OUT_W, 0) for a in (*dqs, *dks)] + [(tb, LANES, 0) for tb in tables],
                    [], [(DIL_W, bf16), (DIL_W, bf16)])


def _head_bcast_sum(d):
    lane = lax.broadcasted_iota(jnp.int32, d.shape, 1)
    out = jnp.zeros_like(d)
    for h in range(d.shape[1] // HEAD_DIM):
        sel = (lane >= h * HEAD_DIM) & (lane < (h + 1) * HEAD_DIM)
        out = jnp.where(sel, jnp.sum(jnp.where(sel, d, 0.0), axis=1, keepdims=True), out)
    return out


def _dil_combine(outs, lses):
    def fn(t, v):
        o0, o1, o2, l0, l1, l2 = t
        m = jnp.maximum(jnp.maximum(l0, l1), l2)
        w0, w1, w2 = jnp.exp(l0 - m), jnp.exp(l1 - m), jnp.exp(l2 - m)
        tot = w0 + w1 + w2
        return [(w0 * o0 + w1 * o1 + w2 * o2) / tot, m + jnp.log(tot)], []
    w = DIL_OUT_W
    return _rowwise(fn, "dil_combine", [(t, w, 0) for t in (*outs, *lses)], [], [(w, f32), (w, f32)])


def _dil_delta(dyb_h, yb_h):
    def fn(t, v):
        return [_head_bcast_sum(t[0] * t[1])], []
    return _rowwise(fn, "dil_delta", [(dyb_h, DIL_OUT_W, 0), (yb_h, DIL_OUT_W, 0)], [], [(DIL_OUT_W, f32)])[0]


def _adamw_math(wt, gt, mt, vt):
    mn = ADAM_B1 * mt + (1.0 - ADAM_B1) * gt
    vn = ADAM_B2 * vt + (1.0 - ADAM_B2) * (gt * gt)
    m_hat = mn / (1.0 - ADAM_B1 ** ADAM_STEP)
    v_hat = vn / (1.0 - ADAM_B2 ** ADAM_STEP)
    return -ADAM_LR * (m_hat / (jnp.sqrt(v_hat) + ADAM_EPS) + ADAM_WD * wt), mn, vn


def _adamw(w, g, m, v, name):
    shape = w.shape
    if w.ndim == 1:
        w, g, m, v = (t.reshape(1, -1) for t in (w, g, m, v))
    rows, cols = w.shape
    if rows % 8 and rows > 8:
        return _adamw_by_cols(w, g, m, v, name)
    tile = 256 if rows % 256 == 0 and rows > 512 else rows

    def fn(t, _):
        return list(_adamw_math(*t)), []
    delta, mn, vn = _rowwise(fn, name, [(w, cols, 0), (g, cols, 0), (m, cols, 0), (v, cols, 0)], [],
                             [(cols, f32)] * 3, tile=tile)
    return delta.reshape(shape), mn.reshape(shape), vn.reshape(shape)


def _adamw_by_cols(w, g, m, v, name, tile=256):
    rows, cols = w.shape

    def body(w_ref, g_ref, m_ref, v_ref, d_ref, mn_ref, vn_ref):
        d_ref[...], mn_ref[...], vn_ref[...] = _adamw_math(w_ref[...], g_ref[...], m_ref[...], v_ref[...])

    spec = pl.BlockSpec((rows, tile), lambda j: (0, j))
    return pl.pallas_call(body, grid=(cols // tile,), in_specs=[spec] * 4, out_specs=[spec] * 3,
                          out_shape=[SDS((rows, cols), f32)] * 3, name=name,
                          compiler_params=_params(("parallel",)))(w, g, m, v)


def _ada_fwd(c_all, w_shard, b_shard):
    def body(c_ref, w_ref, b_ref, o_ref):
        cv = c_ref[...]
        sc = (cv * _sigmoid(cv)).astype(bf16)
        o_ref[...] = jnp.dot(sc, w_ref[...].astype(bf16), preferred_element_type=f32) + b_ref[...]
    return pl.pallas_call(body, out_shape=SDS((N_DEV, w_shard.shape[1]), f32), name="ada_fwd",
                          compiler_params=_params())(c_all, w_shard, b_shard)


def _ada_bwd(c_all, dmod_cols):
    def body(c_ref, d_ref, o_ref):
        cv = c_ref[...]
        sc = cv * _sigmoid(cv)
        o_ref[...] = lax.dot_general(sc, d_ref[...], (((0,), (0,)), ((), ())), precision=lax.Precision.HIGHEST,
                                     preferred_element_type=f32)
    return pl.pallas_call(body, out_shape=SDS((D, dmod_cols.shape[1]), f32), name="ada_bwd",
                          compiler_params=_params())(c_all, dmod_cols)


def _small_reduce(gathered, after):
    def body(g_ref, after_ref, o_ref, loss_ref):
        acc = g_ref[0]
        for d in range(1, N_DEV):
            acc = acc + g_ref[d]
        o_ref[...] = acc
        loss_ref[...] = jnp.zeros((1, LANES), f32) + jnp.sum(acc[10:11, :])
    return pl.pallas_call(body, out_shape=(SDS((SMALL_ROWS, D), f32), SDS((1, LANES), f32)), name="small_reduce",
                          in_specs=[pl.BlockSpec(memory_space=pltpu.VMEM), pl.BlockSpec(memory_space=pl.ANY)],
                          compiler_params=_params())(gathered, after)


FOX_BLK = 512
CUM_BLK = 128


def _fold_lanes(t, op):
    out = t[:, :LANES]
    for j in range(1, t.shape[1] // LANES):
        out = op(out, t[:, j * LANES:(j + 1) * LANES])
    return out


def _fox_gate_fwd(proj, b_pad):
    nblk = SEQ // CUM_BLK

    def body(f_ref, b_ref, col_ref):
        r = lax.broadcasted_iota(jnp.int32, (CUM_BLK, CUM_BLK), 0)
        c = lax.broadcasted_iota(jnp.int32, (CUM_BLK, CUM_BLK), 1)
        tri = (r >= c).astype(f32)
        carry = jnp.zeros((1, LANES), f32)
        for blk in range(nblk):
            z = f_ref[blk * CUM_BLK:(blk + 1) * CUM_BLK, :] + b_ref[...]
            logf = jnp.minimum(z, 0.0) - jnp.log1p(jnp.exp(-jnp.abs(z)))
            cs = jnp.dot(tri, logf, precision=lax.Precision.HIGHEST, preferred_element_type=f32) + carry
            col_ref[blk * CUM_BLK:(blk + 1) * CUM_BLK, :] = cs
            carry = cs[CUM_BLK - 1:CUM_BLK, :]

    return pl.pallas_call(
        body, grid=(1,), in_specs=[pl.BlockSpec((SEQ, LANES), lambda i: (0, C_F // LANES)),
                                   pl.BlockSpec((1, LANES), lambda i: (0, 0))],
        out_specs=pl.BlockSpec((SEQ, LANES), lambda i: (0, 0)),
        out_shape=SDS((SEQ, LANES), f32), name="fox_gate_fwd",
        compiler_params=_params(("arbitrary",)),
    )(proj, b_pad)


def _fox_gate_bwd(dF_row, proj, b_pad):
    nblk = SEQ // CUM_BLK

    def body(d_ref, f_ref, b_ref, df_ref, db_ref, col_ref):
        r = lax.broadcasted_iota(jnp.int32, (CUM_BLK, CUM_BLK), 0)
        c = lax.broadcasted_iota(jnp.int32, (CUM_BLK, CUM_BLK), 1)
        tri = (r <= c).astype(f32)
        lane = lax.broadcasted_iota(jnp.int32, (CUM_BLK, LANES), 1)
        col_ref[...] = d_ref[...].T
        carry = jnp.zeros((1, LANES), f32)
        total = jnp.zeros((1, LANES), f32)
        for blk in reversed(range(nblk)):
            rows = slice(blk * CUM_BLK, (blk + 1) * CUM_BLK)
            cs = jnp.dot(tri, col_ref[rows, :], precision=lax.Precision.HIGHEST, preferred_element_type=f32) + carry
            carry = cs[0:1, :]
            z = f_ref[rows, :] + b_ref[...]
            df = jnp.where(lane < N_FOX_HEADS, cs * _sigmoid(-z), 0.0)
            df_ref[rows, :] = df.astype(df_ref.dtype)
            total = total + _colsum(df)
        db_ref[...] = total

    return pl.pallas_call(
        body, grid=(1,), in_specs=[pl.BlockSpec((LANES, SEQ), lambda i: (0, 0)),
                                   pl.BlockSpec((SEQ, LANES), lambda i: (0, C_F // LANES)),
                                   pl.BlockSpec((1, LANES), lambda i: (0, 0))],
        out_specs=[pl.BlockSpec((SEQ, LANES), lambda i: (0, 0)), pl.BlockSpec((1, LANES), lambda i: (0, 0))],
        out_shape=(SDS((SEQ, LANES), bf16), SDS((1, LANES), f32)), name="fox_gate_bwd",
        scratch_shapes=[pltpu.VMEM((SEQ, LANES), f32)],
        compiler_params=_params(("arbitrary",)),
    )(dF_row, proj, b_pad)


def _nt(a, b):
    return lax.dot_general(a, b, (((1,), (1,)), ((), ())), preferred_element_type=f32)


def _tn(a, b):
    return lax.dot_general(a, b, (((0,), (0,)), ((), ())), preferred_element_type=f32)


def _fox_prep(proj, f_col):
    def fn(t, v):
        q, k, vv, fc = t
        lane = lax.broadcasted_iota(jnp.int32, (q.shape[0], LANES), 1)
        qs, ks = [], []
        for h in range(N_FOX_HEADS):
            pair, pos = divmod(h, 2)
            own = (lane >= pos * HEAD_DIM) & (lane < (pos + 1) * HEAD_DIM)
            base = (1 - pos) * HEAD_DIM
            f = fc[:, h:h + 1]
            hi = f.astype(bf16).astype(f32)
            mid = (f - hi).astype(bf16).astype(f32)
            lo = (f - hi) - mid
            one = jnp.ones_like(f)
            qa = jnp.where(own, q[:, pair * LANES:(pair + 1) * LANES] * ATT_SCALE, 0.0)
            ka = k[:, pair * LANES:(pair + 1) * LANES]
            for idx, (qv, kv) in enumerate([(hi, one), (mid, one), (lo, one), (one, -hi), (one, -mid), (one, -lo)]):
                sel = lane == base + idx
                qa = jnp.where(sel, qv, qa)
                ka = jnp.where(sel, kv, ka)
            qs.append(qa)
            ks.append(ka)
        return [jnp.concatenate(qs, axis=1), jnp.concatenate(ks, axis=1), vv], []
    w = N_FOX_HEADS * LANES
    return _rowwise(fn, "fox_prep", [(proj, FOX_W, C_QA // FOX_W), (proj, FOX_W, C_KA // FOX_W),
                                     (proj, FOX_W, C_VA // FOX_W), (f_col, LANES, 0)], [],
                    [(w, bf16), (w, bf16), (FOX_W, bf16)])


def _fox_fwd(q_aug, k_aug, v):
    blk = FOX_BLK
    npair = FOX_W // LANES

    def body(q_ref, k_ref, v_ref, o_ref, max_ref, sum_ref, s_scr):
        i = pl.program_id(1)
        tri = lax.broadcasted_iota(jnp.int32, (blk, blk), 0) >= lax.broadcasted_iota(jnp.int32, (blk, blk), 1)
        qh = [q_ref[:, h * LANES:(h + 1) * LANES] for h in range(2)]

        def logits(c, masked):
            off = pl.multiple_of(c * blk, blk)
            tops = []
            for h in range(2):
                s = _nt(qh[h], k_ref[pl.ds(off, blk), h * LANES:(h + 1) * LANES])
                if masked:
                    s = jnp.where(tri, s, NEG)
                s_scr[h, :, pl.ds(off, blk)] = s
                tops.append(_fold_lanes(s, jnp.maximum))
            return tops

        def pass_a(c, m):
            return tuple(jnp.maximum(a, b) for a, b in zip(m, logits(c, False)))

        m = lax.fori_loop(0, i, pass_a, tuple(jnp.full((blk, LANES), NEG, f32) for _ in range(2)))
        mx = [jnp.max(jnp.maximum(a, b), axis=1, keepdims=True) for a, b in zip(m, logits(i, True))]

        def pass_b(c, carry):
            off = pl.multiple_of(c * blk, blk)
            vv = v_ref[pl.ds(off, blk), :]
            new = []
            for h in range(2):
                l, acc = carry[h]
                p = jnp.exp(s_scr[h, :, pl.ds(off, blk)] - mx[h]).astype(bf16)
                new.append((l + _fold_lanes(p.astype(f32), jnp.add), acc + jnp.dot(p, vv, preferred_element_type=f32)))
            return tuple(new)

        zero = jnp.zeros((blk, LANES), f32)
        (l_a, acc_a), (l_b, acc_b) = lax.fori_loop(0, i + 1, pass_b, ((zero, zero), (zero, zero)))
        l_a = jnp.sum(l_a, axis=1, keepdims=True)
        l_b = jnp.sum(l_b, axis=1, keepdims=True)
        first = lax.broadcasted_iota(jnp.int32, (blk, LANES), 1) < HEAD_DIM
        o_ref[...] = jnp.where(first, acc_a / l_a, acc_b / l_b)
        max_ref[0] = jnp.where(first, mx[0], mx[1])
        sum_ref[0] = jnp.where(first, l_a, l_b)

    return pl.pallas_call(
        body, grid=(npair, SEQ // blk),
        in_specs=[pl.BlockSpec((blk, 2 * LANES), lambda p, i: (i, p)),
                  pl.BlockSpec((SEQ, 2 * LANES), lambda p, i: (0, p)),
                  pl.BlockSpec((SEQ, LANES), lambda p, i: (0, p))],
        out_specs=[pl.BlockSpec((blk, LANES), lambda p, i: (i, p))]
        + [pl.BlockSpec((1, blk, LANES), lambda p, i: (p, i, 0))] * 2,
        out_shape=(SDS((SEQ, FOX_W), f32),) + (SDS((npair, SEQ, LANES), f32),) * 2, name="fox_fwd",
        scratch_shapes=[pltpu.VMEM((2, blk, SEQ), f32)],
        compiler_params=_params(("parallel", "arbitrary")),
    )(q_aug, k_aug, v)


def _fox_bwd(q_aug, k_aug, v, do, o, row_max, row_sum, after):
    blk = FOX_BLK
    npair = FOX_W // LANES
    nblk = SEQ // blk

    def body(q_ref, k_ref, v_ref, do_ref, o_ref, max_ref, sum_ref, after_ref, dq_ref, dk_ref, dv_ref, df_ref, dq_acc,
             delta_ref, inv_ref):
        inv_ref[...] = 1.0 / sum_ref[0]
        lane_s = lax.broadcasted_iota(jnp.int32, (SEQ, LANES), 1)
        prod = do_ref[...].astype(bf16).astype(f32) * o_ref[...]
        d_a = jnp.sum(jnp.where(lane_s < HEAD_DIM, prod, 0.0), axis=1, keepdims=True)
        d_b = jnp.sum(jnp.where(lane_s >= HEAD_DIM, prod, 0.0), axis=1, keepdims=True)
        delta_ref[...] = jnp.where(lane_s < HEAD_DIM, d_a, d_b)
        dq_acc[...] = jnp.zeros_like(dq_acc)
        df_ref[...] = jnp.zeros_like(df_ref)
        lane = lax.broadcasted_iota(jnp.int32, (blk, LANES), 1)
        own = [lane < HEAD_DIM, lane >= HEAD_DIM]
        tri = lax.broadcasted_iota(jnp.int32, (blk, blk), 0) >= lax.broadcasted_iota(jnp.int32, (blk, blk), 1)

        def q_slab(qoff, h):
            return q_ref[pl.ds(qoff, blk), h * LANES:(h + 1) * LANES]

        def probs(qoff, h, k_h, masked):
            s = _nt(q_slab(qoff, h), k_h)
            if masked:
                s = jnp.where(tri, s, NEG)
            col = slice(h * HEAD_DIM, h * HEAD_DIM + 1)
            weights = jnp.exp(s - max_ref[0, pl.ds(qoff, blk), col]).astype(bf16).astype(f32)
            return weights * inv_ref[pl.ds(qoff, blk), col]

        def k_slabs(koff):
            return [k_ref[pl.ds(koff, blk), h * LANES:(h + 1) * LANES] for h in range(2)]

        def kv_step(kj, _):
            koff = pl.multiple_of(kj * blk, blk)
            k_aug = k_slabs(koff)
            k_own = [jnp.where(own[h], k_aug[h], jnp.zeros_like(k_aug[h])) for h in range(2)]
            vv = v_ref[pl.ds(koff, blk), :]
            v_own = [jnp.where(own[h], vv, jnp.zeros_like(vv)) for h in range(2)]

            def q_tile(qi, carry, masked):
                qoff = pl.multiple_of(qi * blk, blk)
                dd = do_ref[pl.ds(qoff, blk), :].astype(bf16)
                new, dq_add = [], None
                for h in range(2):
                    dk_h, dv_h, dcol = carry[h]
                    p = probs(qoff, h, k_aug[h], masked)
                    dl = p * (_nt(dd, v_own[h]) - delta_ref[pl.ds(qoff, blk), h * HEAD_DIM:h * HEAD_DIM + 1])
                    dlb = dl.astype(bf16)
                    part = jnp.dot(dlb, k_own[h], preferred_element_type=f32)
                    dq_add = part if dq_add is None else dq_add + part
                    new.append((dk_h + _tn(dlb, q_slab(qoff, h)), dv_h + _tn(p.astype(bf16), dd),
                                dcol + _colsum(dl)))
                dq_acc[pl.ds(qoff, blk), :] += dq_add * ATT_SCALE
                return tuple(new)

            zero = (jnp.zeros((blk, LANES), f32), jnp.zeros((blk, LANES), f32), jnp.zeros((1, blk), f32))
            carry = q_tile(kj, (zero, zero), True)
            (dk_a, dv_a, dcol_a), (dk_b, dv_b, dcol_b) = lax.fori_loop(
                kj + 1, nblk, lambda qi, cr: q_tile(qi, cr, False), carry)
            dk_ref[pl.ds(koff, blk), :] = jnp.where(own[0], dk_a, dk_b).astype(dk_ref.dtype)
            dv_ref[pl.ds(koff, blk), :] = jnp.where(own[0], dv_a, dv_b).astype(dv_ref.dtype)
            df_ref[0, 0:1, pl.ds(koff, blk)] = -dcol_a
            df_ref[0, 1:2, pl.ds(koff, blk)] = -dcol_b
            return 0

        lax.fori_loop(0, nblk, kv_step, 0)
        dq_ref[...] = dq_acc[...].astype(dq_ref.dtype)

    pair_aug = pl.BlockSpec((SEQ, 2 * LANES), lambda p: (0, p))
    slab = pl.BlockSpec((SEQ, LANES), lambda p: (0, p))
    per_pair = pl.BlockSpec((1, SEQ, LANES), lambda p: (p, 0, 0))
    rows = pl.BlockSpec((1, 8, SEQ), lambda p: (p, 0, 0))
    return pl.pallas_call(
        body, grid=(npair,),
        in_specs=[pair_aug, pair_aug, slab, slab, slab, per_pair, per_pair, pl.BlockSpec(memory_space=pl.ANY)],
        out_specs=[slab, slab, slab, rows],
        out_shape=(SDS((SEQ, FOX_W), bf16),) * 3 + (SDS((npair, 8, SEQ), f32),), name="fox_bwd",
        scratch_shapes=[pltpu.VMEM((SEQ, LANES), f32)] * 3,
        compiler_params=_params(("parallel",)),
    )(q_aug, k_aug, v, do, o, row_max, row_sum, after)


DIL_BLK = 128
DILATIONS = (1, 4, 16)
N_GROUPS = len(DILATIONS)
DIL_PAIRS = DIL_OUT_W // LANES


def _dil_blocks(d):
    r1 = lax.broadcasted_iota(jnp.int32, (2 * DIL_BLK, DIL_BLK), 0) & (DIL_BLK - 1)
    c1 = lax.broadcasted_iota(jnp.int32, (2 * DIL_BLK, DIL_BLK), 1)
    r2 = lax.broadcasted_iota(jnp.int32, (2 * DIL_BLK, 2 * DIL_BLK), 0) & (DIL_BLK - 1)
    c2 = lax.broadcasted_iota(jnp.int32, (2 * DIL_BLK, 2 * DIL_BLK), 1)
    band = ((c2 < DIL_BLK) & (c2 >= r2)) | ((c2 >= DIL_BLK) & (c2 - DIL_BLK <= r2))
    out = []
    for r in range(d):
        for b in range(SEQ // d // DIL_BLK):
            rows = pl.ds(r + d * DIL_BLK * b, DIL_BLK, stride=d)
            if b == 0:
                out.append((rows, rows, r1 >= c1))
            else:
                out.append((rows, pl.ds(r + d * DIL_BLK * (b - 1), 2 * DIL_BLK, stride=d), band))
    return out


def _dil_v_spec(g):
    return pl.BlockSpec((SEQ, LANES), lambda p: (0, C_VB // LANES + DIL_PAIRS * g + p))


def _stack_heads(t, first):
    zero = jnp.zeros_like(t)
    return jnp.concatenate([jnp.where(first, t, zero), jnp.where(first, zero, t)], axis=0)


def _dil_fwd(q, k, v, g):
    def body(q_ref, k_ref, v_ref, o_ref, lse_ref):
        first = lax.broadcasted_iota(jnp.int32, (DIL_BLK, LANES), 1) < HEAD_DIM
        for rows, krows, mask in _dil_blocks(DILATIONS[g]):
            qv, kk, vv = q_ref[rows, :].astype(bf16), k_ref[krows, :].astype(bf16), v_ref[krows, :].astype(bf16)
            s = jnp.where(mask, _nt(_stack_heads(qv, first), kk), NEG)
            m = jnp.max(s, axis=1, keepdims=True)
            p = jnp.exp(s - m)
            l = jnp.sum(p, axis=1, keepdims=True)
            out = jnp.dot(p.astype(bf16), vv, preferred_element_type=f32) / l
            lse = m + jnp.log(l)
            o_ref[rows, :] = jnp.where(first, out[:DIL_BLK], out[DIL_BLK:])
            lse_ref[rows, :] = jnp.where(first, lse[:DIL_BLK], lse[DIL_BLK:])

    grouped = pl.BlockSpec((SEQ, LANES), lambda p: (0, DIL_PAIRS * g + p))
    own = pl.BlockSpec((SEQ, LANES), lambda p: (0, p))
    shape = SDS((SEQ, DIL_OUT_W), f32)
    return pl.pallas_call(
        body, grid=(DIL_PAIRS,), in_specs=[grouped, grouped, _dil_v_spec(g)], out_specs=[own] * 2,
        out_shape=(shape, shape),
        name=f"dil_fwd_{DILATIONS[g]}", compiler_params=_params(("parallel",)),
    )(q, k, v)


def _dil_bwd(q, k, v, do, lse, delta, g):
    def body(q_ref, k_ref, v_ref, do_ref, lse_ref, dl_ref, dq_ref, dk_ref, dv_ref):
        first = lax.broadcasted_iota(jnp.int32, (DIL_BLK, LANES), 1) < HEAD_DIM
        dk_ref[...] = jnp.zeros_like(dk_ref)
        dv_ref[...] = jnp.zeros_like(dv_ref)
        for rows, krows, mask in _dil_blocks(DILATIONS[g]):
            qv, kk, vv = q_ref[rows, :].astype(bf16), k_ref[krows, :].astype(bf16), v_ref[krows, :].astype(bf16)
            lsev, delv = lse_ref[rows, :], dl_ref[rows, :]
            q2 = _stack_heads(qv, first)
            do2 = _stack_heads(do_ref[rows, :].astype(bf16), first)
            per_head = lambda t: jnp.concatenate([t[:, 0:1], t[:, HEAD_DIM:HEAD_DIM + 1]], axis=0)
            p = jnp.exp(jnp.where(mask, _nt(q2, kk), NEG) - per_head(lsev))
            dl = (p * (_nt(do2, vv) - per_head(delv))).astype(bf16)
            dq = jnp.dot(dl, kk, preferred_element_type=f32)
            dq_ref[rows, :] = jnp.where(first, dq[:DIL_BLK], dq[DIL_BLK:]) * ATT_SCALE
            dk_ref[krows, :] += _tn(dl, q2)
            dv_ref[krows, :] += _tn(p.astype(bf16), do2)

    grouped = pl.BlockSpec((SEQ, LANES), lambda p: (0, DIL_PAIRS * g + p))
    own = pl.BlockSpec((SEQ, LANES), lambda p: (0, p))
    shape = SDS((SEQ, DIL_OUT_W), f32)
    return pl.pallas_call(
        body, grid=(DIL_PAIRS,), in_specs=[grouped, grouped, _dil_v_spec(g)] + [own] * 3, out_specs=[own] * 3,
        out_shape=(shape, shape, shape), name=f"dil_bwd_{DILATIONS[g]}", compiler_params=_params(("parallel",)),
    )(q, k, v, do, lse, delta)


def _position():
    return lax.axis_index("x"), lax.axis_index("y"), lax.axis_index("c")


def _all_gather(block, name):
    def body(x_ref, out_ref, send_sems, recv_sems, local_sem):
        x, y, c = _position()
        me, sibling = (x, y, c), (x, y, 1 - c)
        chips = [(1 - x, y), (x, 1 - y), (1 - x, 1 - y)]

        def slot(px, py, pc):
            return out_ref.at[4 * px + 2 * py + pc]

        def copy(k, blk, to, src=None):
            return pltpu.make_async_remote_copy(
                src_ref=slot(*blk) if src is None else src, dst_ref=slot(*blk),
                send_sem=send_sems.at[k], recv_sem=recv_sems.at[k], device_id=to, device_id_type=MESH)

        mine = pltpu.make_async_copy(x_ref, slot(*me), local_sem)
        mine.start()
        first = [copy(0, me, sibling, src=x_ref)]
        first += [copy(1 + j, me, (*chip, c), src=x_ref) for j, chip in enumerate(chips)]
        for cp in first:
            cp.start()
        passed = [copy(4 + j, (*chip, c), sibling) for j, chip in enumerate(chips)]
        for j, chip in enumerate(chips):
            copy(1 + j, (*chip, c), me).wait_recv()
            passed[j].start()
        copy(0, sibling, me).wait_recv()
        for j, chip in enumerate(chips):
            copy(4 + j, (*chip, 1 - c), me).wait_recv()
        for cp in first + passed:
            cp.wait_send()
        mine.wait()

    return pl.pallas_call(
        body, out_shape=SDS((N_DEV,) + block.shape, block.dtype),
        in_specs=[pl.BlockSpec(memory_space=pl.ANY)], out_specs=pl.BlockSpec(memory_space=pl.ANY),
        scratch_shapes=[pltpu.SemaphoreType.DMA((7,)), pltpu.SemaphoreType.DMA((7,)), pltpu.SemaphoreType.DMA],
        name=name,
    )(block)


HBM_SPEC = pl.BlockSpec(memory_space=pltpu.HBM)
SEM_SPEC = pl.BlockSpec(memory_space=pltpu.SEMAPHORE)
SPLIT_COPY = pltpu.CompilerParams(has_side_effects=pltpu.SideEffectType.DATAFLOW_SIDE_EFFECTING)


def _in_hbm(t):
    return pltpu.with_memory_space_constraint(t, pltpu.HBM)


def _pair_copies(g_refs, land_refs, send_sems, recv_sems):
    x, y, c = _position()
    return [pltpu.make_async_remote_copy(
        src_ref=g.at[2 * k + (1 - c)], dst_ref=land.at[k], send_sem=send_sems.at[4 * a + k],
        recv_sem=recv_sems.at[4 * a + k], device_id=(x, y, 1 - c), device_id_type=MESH)
        for a, (g, land) in enumerate(zip(g_refs, land_refs, strict=True)) for k in range(4)]


def _chip_copies(t_refs, land_refs, send_sems, recv_sems):
    x, y, c = _position()
    chips = [(1 - x, y), (x, 1 - y), (1 - x, 1 - y)]
    return [pltpu.make_async_remote_copy(
        src_ref=t.at[2 * px + py], dst_ref=land.at[j], send_sem=send_sems.at[3 * a + j],
        recv_sem=recv_sems.at[3 * a + j], device_id=(px, py, c), device_id_type=MESH)
        for a, (t, land) in enumerate(zip(t_refs, land_refs, strict=True)) for j, (px, py) in enumerate(chips)]


_ROUNDS = {"pair": (_pair_copies, 4), "chip": (_chip_copies, 3)}


def _exchange_start(kind, ts, name):
    copies, slots = _ROUNDS[kind]
    n = len(ts)
    lands = [_in_hbm(lax.empty((slots,) + t.shape[1:], t.dtype)) for t in ts]

    def body(*refs):
        for cp in copies(refs[:n], refs[n:2 * n], refs[2 * n], refs[2 * n + 1]):
            cp.start()
        refs[-1][...] = jnp.zeros_like(refs[-1])

    sems = pltpu.SemaphoreType.DMA((slots * n,))
    res = pl.pallas_call(
        body, name=name, in_specs=[HBM_SPEC] * (2 * n),
        out_shape=(sems, sems, *[pltpu.HBM(t.shape, t.dtype) for t in (*ts, *lands)], SDS((8, LANES), f32)),
        out_specs=(SEM_SPEC, SEM_SPEC, *[HBM_SPEC] * (2 * n), pl.BlockSpec(memory_space=pltpu.VMEM)),
        input_output_aliases={i: 2 + i for i in range(2 * n)}, compiler_params=SPLIT_COPY,
    )(*[_in_hbm(t) for t in ts], *lands)
    return res[:-1], res[-1]


def _exchange_wait(kind, state, after, name):
    copies, _ = _ROUNDS[kind]
    send_sems, recv_sems, *arrays = state
    n = len(arrays) // 2

    def body(*refs):
        for cp in copies(refs[:n], refs[n:2 * n], refs[2 * n], refs[2 * n + 1]):
            cp.wait_send()
            cp.wait_recv()

    res = pl.pallas_call(
        body, name=name, in_specs=[HBM_SPEC] * (2 * n) + [SEM_SPEC, SEM_SPEC, pl.BlockSpec(memory_space=pl.ANY)],
        out_shape=[pltpu.HBM(t.shape, t.dtype) for t in arrays], out_specs=[HBM_SPEC] * (2 * n),
        input_output_aliases={i: i for i in range(2 * n)}, compiler_params=SPLIT_COPY,
    )(*arrays, send_sems, recv_sems, after)
    return res[:n], res[n:]


def _gather_copies(x_refs, out_refs, send_sems, recv_sems):
    x, y, c = _position()
    peers = [(x, y, 1 - c), (1 - x, y, c), (x, 1 - y, c), (1 - x, 1 - y, c)]
    sends, arrivals = [], []
    for a, (x_ref, out_ref) in enumerate(zip(x_refs, out_refs, strict=True)):
        for k, (px, py, pc) in enumerate(peers):
            sems = dict(send_sem=send_sems.at[4 * a + k], recv_sem=recv_sems.at[4 * a + k],
                        device_id=(px, py, pc), device_id_type=MESH)
            sends.append(pltpu.make_async_remote_copy(src_ref=x_ref, dst_ref=out_ref.at[4 * x + 2 * y + c], **sems))
            arrivals.append(pltpu.make_async_remote_copy(src_ref=x_ref, dst_ref=out_ref.at[4 * px + 2 * py + pc],
                                                         **sems))
    return sends, arrivals


def _gather_start(blocks, after, name):
    n = len(blocks)
    outs = [_in_hbm(lax.empty((N_DEV,) + b.shape, b.dtype)) for b in blocks]

    def body(*refs):
        sends, _ = _gather_copies(refs[:n], refs[n:2 * n], refs[2 * n + 1], refs[2 * n + 2])
        for cp in sends:
            cp.start()
        refs[-1][...] = jnp.zeros_like(refs[-1])

    sems = pltpu.SemaphoreType.DMA((4 * n,))
    res = pl.pallas_call(
        body, name=name, in_specs=[HBM_SPEC] * (2 * n) + [pl.BlockSpec(memory_space=pl.ANY)],
        out_shape=(sems, sems, *[pltpu.HBM(t.shape, t.dtype) for t in (*blocks, *outs)], SDS((8, LANES), f32)),
        out_specs=(SEM_SPEC, SEM_SPEC, *[HBM_SPEC] * (2 * n), pl.BlockSpec(memory_space=pltpu.VMEM)),
        input_output_aliases={i: 2 + i for i in range(2 * n)}, compiler_params=SPLIT_COPY,
    )(*[_in_hbm(b) for b in blocks], *outs, after)
    return res[:-1], res[-1]


def _gather_wait(state, after, name):
    send_sems, recv_sems, *arrays = state
    n = len(arrays) // 2

    def body(*refs):
        sends, arrivals = _gather_copies(refs[:n], refs[n:2 * n], refs[2 * n], refs[2 * n + 1])
        for cp in sends:
            cp.wait_send()
        for cp in arrivals:
            cp.wait_recv()

    res = pl.pallas_call(
        body, name=name, in_specs=[HBM_SPEC] * (2 * n) + [SEM_SPEC, SEM_SPEC, pl.BlockSpec(memory_space=pl.ANY)],
        out_shape=[pltpu.HBM(t.shape, t.dtype) for t in arrays], out_specs=[HBM_SPEC] * (2 * n),
        input_output_aliases={i: i for i in range(2 * n)}, compiler_params=SPLIT_COPY,
    )(*arrays, send_sems, recv_sems, after)
    return res[:n], res[n:]


def _gather_finish(partial, name):
    n = len(partial)

    def body(*refs):
        in_refs, out_refs = refs[:n], refs[n:2 * n]
        send_sems, recv_sems = refs[2 * n:]
        x, y, c = _position()
        chips = [(1 - x, y), (x, 1 - y), (1 - x, 1 - y)]
        copies = []
        for a in range(n):
            for j, (px, py) in enumerate(chips):
                cp = pltpu.make_async_remote_copy(
                    src_ref=in_refs[a].at[4 * px + 2 * py + c], dst_ref=out_refs[a].at[4 * px + 2 * py + c],
                    send_sem=send_sems.at[a, j], recv_sem=recv_sems.at[a, j], device_id=(x, y, 1 - c),
                    device_id_type=MESH)
                cp.start()
                copies.append(cp)
        for a in range(n):
            for j, (px, py) in enumerate(chips):
                pltpu.make_async_remote_copy(
                    src_ref=in_refs[a].at[4 * px + 2 * py + (1 - c)], dst_ref=out_refs[a].at[4 * px + 2 * py + (1 - c)],
                    send_sem=send_sems.at[a, j], recv_sem=recv_sems.at[a, j], device_id=(x, y, 1 - c),
                    device_id_type=MESH).wait_recv()
        for cp in copies:
            cp.wait_send()

    hbm = pl.BlockSpec(memory_space=pl.ANY)
    return pl.pallas_call(
        body, out_shape=[SDS(p.shape, p.dtype) for p in partial], in_specs=[hbm] * n, out_specs=[hbm] * n,
        input_output_aliases={a: a for a in range(n)},
        scratch_shapes=[pltpu.SemaphoreType.DMA((n, 3)), pltpu.SemaphoreType.DMA((n, 3))],
        name=name,
    )(*partial)


def _row_tile(rows):
    return 512 if rows % 512 == 0 and rows > 512 else rows


def _pair_add(g, r1, core, name):
    def body(c_ref, g_ref, r_ref, o_ref):
        o_ref[...] = (g_ref[...].astype(f32) + r_ref[...].astype(f32)).astype(o_ref.dtype)

    rows, cols = g.shape[1:]
    tile = _row_tile(rows)
    blk = (1, tile, cols)
    return pl.pallas_call(
        body, out_shape=SDS((4, rows, cols), g.dtype), name=name,
        grid_spec=pltpu.PrefetchScalarGridSpec(
            num_scalar_prefetch=1, grid=(4, rows // tile),
            in_specs=[pl.BlockSpec(blk, lambda k, i, c_ref: (2 * k + c_ref[0], i, 0)),
                      pl.BlockSpec(blk, lambda k, i, c_ref: (k, i, 0))],
            out_specs=pl.BlockSpec(blk, lambda k, i, c_ref: (k, i, 0))),
        compiler_params=_params(("parallel", "arbitrary")),
    )(core, g, r1)


def _chip_add(t, r2, chip, name, transposed=False):
    def body(c_ref, t_ref, r_ref, o_ref):
        s = ((t_ref[0].astype(f32) + r_ref[0].astype(f32)) + r_ref[1].astype(f32)) + r_ref[2].astype(f32)
        o_ref[...] = s.T if transposed else s

    rows, cols = t.shape[1:]
    tile = _row_tile(rows)
    out_spec = pl.BlockSpec((cols, tile), lambda i, c_ref: (0, i)) if transposed else pl.BlockSpec(
        (tile, cols), lambda i, c_ref: (i, 0))
    return pl.pallas_call(
        body, out_shape=SDS((cols, rows) if transposed else (rows, cols), f32), name=name,
        grid_spec=pltpu.PrefetchScalarGridSpec(
            num_scalar_prefetch=1, grid=(rows // tile,),
            in_specs=[pl.BlockSpec((1, tile, cols), lambda i, c_ref: (c_ref[0], i, 0)),
                      pl.BlockSpec((3, tile, cols), lambda i, c_ref: (0, i, 0))],
            out_specs=out_spec),
        compiler_params=_params(("arbitrary",)),
    )(chip, t, r2)


def _pad_to(t, axis, size):
    pads = [(0, 0)] * t.ndim
    pads[axis] = (0, size - t.shape[axis])
    return jnp.pad(t, pads)


_REF_COLS = {"qa": (0, FOX_W), "ka": (FOX_W, FOX_W), "va": (2 * FOX_W, FOX_W), "f": (3 * FOX_W, N_FOX_HEADS)}
_REF_COLS.update({n: (3 * FOX_W + N_FOX_HEADS + i * DIL_W, DIL_W) for i, n in enumerate(("qb", "kb", "vb"))})
_REF_COLS.update({n: (3 * FOX_W + N_FOX_HEADS + 3 * DIL_W + i * D, D) for i, n in enumerate(("ga", "gb"))})
_REF_ORDER = ("qa", "ka", "va", "f", "qb", "kb", "vb", "ga", "gb")


def _place_cols(sources, src_of, out_cols, name, row_block=512):
    arrays = [s[0] if isinstance(s, tuple) else s for s in sources]
    widths = [a.shape[-1] for a in arrays]
    rows = arrays[0].shape[-2]
    plan = []
    for t in range(out_cols // LANES):
        segs, c, end = [], t * LANES, (t + 1) * LANES
        while c < end:
            s = src_of(c)
            if s is None:
                c += 1
                continue
            n = 1
            while c + n < end and src_of(c + n) == (s[0], s[1] + n):
                n += 1
            segs.append((s[0], s[1], c - t * LANES, n))
            c += n
        plan.append(segs)

    def body(*refs):
        o_ref = refs[-1]
        for t, segs in enumerate(plan):
            acc = None
            for si, c0, o0, n in segs:
                a0 = c0 // LANES * LANES
                wide = min(2 * LANES, widths[si] - a0)
                win = refs[si][0, :, a0:a0 + wide] if isinstance(sources[si], tuple) else refs[si][:, a0:a0 + wide]
                r = lax.broadcasted_iota(jnp.int32, (wide, LANES), 0)
                c = lax.broadcasted_iota(jnp.int32, (wide, LANES), 1)
                pick = ((r - (c0 - a0) == c - o0) & (c >= o0) & (c < o0 + n)).astype(bf16)
                part = jnp.dot(win.astype(bf16), pick, preferred_element_type=f32)
                acc = part if acc is None else acc + part
            tile = jnp.zeros((row_block, LANES), f32) if acc is None else acc
            o_ref[:, t * LANES:(t + 1) * LANES] = tile.astype(o_ref.dtype)

    def spec(s):
        if isinstance(s, tuple):
            j = s[1]
            return pl.BlockSpec((1, row_block, s[0].shape[-1]), lambda i: (j, i, 0))
        return pl.BlockSpec((row_block, s.shape[-1]), lambda i: (i, 0))

    return pl.pallas_call(
        body, grid=(rows // row_block,), in_specs=[spec(s) for s in sources],
        out_specs=pl.BlockSpec((row_block, out_cols), lambda i: (i, 0)), out_shape=SDS((rows, out_cols), bf16),
        name=name, compiler_params=_params(("parallel",)),
    )(*arrays)


def _ref_piece(r):
    for name in _REF_ORDER:
        lo, width = _REF_COLS[name]
        if lo <= r < lo + width:
            return name, r - lo
    raise ValueError(r)


def _shard_pad_cols(pieces):
    names = [n for n in _REF_ORDER if n != "vb"]
    sources = [pieces[n] for n in names] + list(pieces["vb"])

    def src_of(c):
        j, i = divmod(c, W_IN_PAD)
        if i >= W_IN_SH:
            return None
        name, col = _ref_piece(j * W_IN_SH + i)
        if name == "vb":
            return len(names) + col // DIL_OUT_W, col % DIL_OUT_W
        return names.index(name), col

    return _place_cols(sources, src_of, N_DEV * W_IN_PAD, "place_dproj")


_SLABS = {"ga": C_GA, "gb": C_GB, "qb": C_QB, "kb": C_KB, "vb": C_VB, "qa": C_QA, "ka": C_KA, "va": C_VA, "f": C_F}


def _slab_w_in(stack):
    def src_of(c):
        for name, start in _SLABS.items():
            lo, width = _REF_COLS[name]
            if start <= c < start + width:
                return divmod(lo + c - start, W_IN_SH)
        return None

    return _place_cols([(stack, j) for j in range(N_DEV)], src_of, PROJ_W, "place_w_in")


def kernel(x, c, w_ada, b_ada, g_mix, w_in, b_fgate, w_br_a, w_br_b, w_out, g_ffn, w_ffn_gate, w_ffn_up, w_ffn_down, g_final, loss_target, m_w_ada, m_b_ada, m_g_mix, m_w_in, m_b_fgate, m_w_br_a, m_w_br_b, m_w_out, m_g_ffn, m_w_ffn_gate, m_w_ffn_up, m_w_ffn_down, m_g_final, v_w_ada, v_b_ada, v_g_mix, v_w_in, v_b_fgate, v_w_br_a, v_w_br_b, v_w_out, v_g_ffn, v_w_ffn_gate, v_w_ffn_up, v_w_ffn_down, v_g_final):
    px, py, pc = _position()
    dev = 4 * px + 2 * py + pc
    x2d, tgt = x[0], loss_target[0]

    c_all = _all_gather(c, "gather_c").reshape(N_DEV, D)
    ada_cols = w_ada.shape[2]
    b_shard = lax.dynamic_slice(b_ada, (0, dev * ada_cols), (1, ada_cols))
    mod_shard = _ada_fwd(c_all, w_ada[0], b_shard)
    mod_all = _all_gather(mod_shard, "gather_mod")
    modv = lax.dynamic_index_in_dim(mod_all, dev, axis=1, keepdims=False).reshape(6, D)
    h1 = _pre1(x2d, modv, g_mix)

    w_in_s = _all_gather(_pad_to(w_in[0], 1, W_IN_PAD).astype(bf16), "gather_w_in")
    gate_up = jnp.concatenate([_pad_to(w_ffn_gate[0], 1, FF_PAD), _pad_to(w_ffn_up[0], 1, FF_PAD)], axis=1)
    later = [w_br_a[0], w_br_b[0], w_out[0], gate_up, _pad_to(w_ffn_down[0], 0, FF_PAD)]
    later_state, later_token = _gather_start([t.astype(bf16) for t in later], w_in_s, "gather_rest_start")
    w_in_p = _slab_w_in(w_in_s)

    proj = _matmul(h1, w_in_p, name="mm_proj", tm=SEQ, tn=896, after=later_token)
    b_pad = jnp.pad(b_fgate, ((0, 0), (0, LANES - N_FOX_HEADS)))
    q_aug, k_aug, va = _fox_prep(proj, _fox_gate_fwd(proj, b_pad))
    ya_h, max_a, sum_a = _fox_fwd(q_aug, k_aug, va)

    tables = _rope_tables()
    qb_r, kb_r = _rope_fwd(proj, tables)
    by_group = [_dil_fwd(qb_r, kb_r, proj, grp) for grp in range(N_GROUPS)]
    yb_h, lse_b = _dil_combine([o for o, _ in by_group], [l for _, l in by_group])

    both_done = ya_h[:8, :LANES] + yb_h[:8, :LANES]
    mine, arrived = _gather_wait(later_state, both_done, "gather_rest_wait")
    w_a_s, w_b_s, w_o_s, w_gu_s, w_d_s = [
        lax.dynamic_update_slice(stack, block[None], (dev, 0, 0))
        for stack, block in zip(_gather_finish(arrived, "gather_rest_finish"), mine, strict=True)]
    w_o = w_o_s.reshape(D, D)
    w_d = w_d_s.reshape(FF_HID, D)
    ya = _matmul_stack(ya_h, w_a_s, name="mm_br_a")
    yb = _matmul_stack(yb_h, w_b_s, name="mm_br_b")

    merged, mix, x1, h2 = _post1(ya, yb, proj, w_o, x2d, modv, g_ffn)
    act, au = _ffn_in(h2, w_gu_s)

    dx2, dff, dg_final, dga_f, loss_lanes = _final(act, w_d, x1, tgt, modv, g_final.reshape(1, D))
    dau = _ffn_bwd_in(dff, w_d_s, au)

    core = pc.astype(jnp.int32).reshape(1)
    chip = (2 * px + py).astype(jnp.int32).reshape(1)

    def pair_done(state, after, tags, name):
        mine, theirs = _exchange_wait("pair", state, after, "pair_wait_" + name)
        sums = [_pair_add(g, r, core, "pair_add_" + t) for g, r, t in zip(mine, theirs, tags)]
        return _exchange_start("chip", sums, "chip_start_" + name)

    def from_chips(state, after, tags, name, transposed=None):
        sums, got = _exchange_wait("chip", state, after, "chip_wait_" + name)
        flips = transposed or [False] * len(tags)
        return [_chip_add(p, r, chip, "chip_add_" + t, f) for p, r, t, f in zip(sums, got, tags, flips)]

    g_gu = _matmul(h2, dau, ta=True, by_shard=True, out_dtype=bf16, name="mm_g_ffn_in", tm=D, tn=2 * FF_PAD)
    g_d = _matmul(act, dff, ta=True, out_dtype=bf16, name="mm_g_down", tm=FF_HID // 2, tn=512)
    ffn_tags = ["gu", "down"]
    ffn_pair, ffn_pair_token = _exchange_start("pair", [g_gu, g_d.reshape(N_DEV, FF_PAD, D)], "pair_start_ffn")

    dx1, dmix, dsh_f, dsc_f, dg_ffn, dga_m = _mid_bwd(dau, w_gu_s, ffn_pair_token, x1, dx2, mix, modv, g_ffn)
    ffn_state, ffn_token = pair_done(ffn_pair, dx1, ffn_tags, "ffn")
    dya, dyb, dga, dgb = _merge_bwd(dmix, w_o, ffn_token, ya, yb, proj)
    dya_h = _matmul_stack(dya, w_a_s, tb=True, name="mm_d_ya")
    dyb_h = _matmul_stack(dyb, w_b_s, tb=True, name="mm_d_yb")

    g_o = _matmul(merged, dmix, ta=True, out_dtype=bf16, name="mm_g_out", tm=D, tn=512)
    g_a = _matmul_stack(ya_h, dya, ta=True, out_dtype=bf16, name="mm_g_br_a")
    g_b = _matmul_stack(yb_h, dyb, ta=True, out_dtype=bf16, name="mm_g_br_b")
    rows_a, rows_b = FOX_W * W_BR_SH // D, DIL_OUT_W * W_BR_SH // D
    g_small = jnp.concatenate([g_a.reshape(N_DEV, rows_a, D), g_b.reshape(N_DEV, rows_b, D),
                               g_o.reshape(N_DEV, W_BR_SH, D)], axis=1)
    small_pair, small_pair_token = _exchange_start("pair", [g_small], "pair_start_small")

    dqa, dka, dva, dF = _fox_bwd(q_aug, k_aug, va, dya_h, ya_h, max_a, sum_a, small_pair_token)
    dF_row = jnp.pad(dF[:, :2, :].reshape(N_FOX_HEADS, SEQ), ((0, LANES - N_FOX_HEADS), (0, 0)))
    df, db_fgate = _fox_gate_bwd(dF_row, proj, b_pad)
    small_state, small_token = pair_done(small_pair, df, ["small"], "small")

    delta_b = _dil_delta(dyb_h, yb_h)
    dil_grads = [_dil_bwd(qb_r, kb_r, proj, dyb_h, lse_b, delta_b, grp) for grp in range(N_GROUPS)]
    dqb, dkb = _rope_bwd([t[0] for t in dil_grads], [t[1] for t in dil_grads], tables)

    dproj = _shard_pad_cols({"qa": dqa, "ka": dka, "va": dva, "f": df, "qb": dqb, "kb": dkb,
                             "vb": [t[2] for t in dil_grads], "ga": dga, "gb": dgb})
    g_in = _matmul(h1, dproj, ta=True, by_shard=True, out_dtype=bf16, name="mm_g_in", tm=D, tn=W_IN_PAD,
                   after=small_token)
    mix_tags = ["in"]
    mix_pair, mix_pair_token = _exchange_start("pair", [g_in], "pair_start_mixer")

    grad_x, dsh_m, dsc_m, dg_mix = _first_bwd(dproj, w_in_s, mix_pair_token, x2d, dx1, modv, g_mix)

    pad_lane = lambda t: jnp.pad(t, ((0, 0), (0, D - t.shape[1])))
    small = jnp.concatenate([dsh_m, dsc_m, dga_m, dsh_f, dsc_f, dga_f, dg_mix, dg_ffn, dg_final,
                             pad_lane(db_fgate), loss_lanes, jnp.zeros((SMALL_ROWS - 11, D), f32)], axis=0)
    small_all = _all_gather(small, "gather_small")
    mix_state, mix_token = pair_done(mix_pair, small_all, mix_tags, "mixer")

    small_sum, loss_row = _small_reduce(small_all, mix_token)
    dmod_all = small_all[:, :6, :].reshape(N_DEV, 6 * D)
    g_w_ada = _ada_bwd(c_all, lax.dynamic_slice(dmod_all, (0, dev * ada_cols), (N_DEV, ada_cols)))
    s_gu_t, s_d = from_chips(ffn_state, small_sum, ffn_tags, "ffn", [True, False])
    s_small, = from_chips(small_state, small_sum, ["small"], "small")

    loss = loss_row[0, 0]
    g = {
        "w_ada": g_w_ada[None], "b_ada": small_sum[0:6].reshape(1, 6 * D), "g_mix": small_sum[6:7],
        "b_fgate": small_sum[9:10, :N_FOX_HEADS], "g_ffn": small_sum[7:8], "w_ffn_gate": s_gu_t[:W_FF_SH],
        "w_ffn_up": s_gu_t[FF_PAD:FF_PAD + W_FF_SH], "w_ffn_down": s_d[None, :W_FF_SH],
        "g_final": small_sum[8], "w_br_a": s_small[:rows_a].reshape(1, FOX_W, W_BR_SH),
        "w_br_b": s_small[rows_a:rows_a + rows_b].reshape(1, DIL_OUT_W, W_BR_SH), "w_out": s_small[None, rows_a + rows_b:],
    }
    w = {"w_ada": w_ada, "b_ada": b_ada, "g_mix": g_mix, "w_in": w_in, "b_fgate": b_fgate, "w_br_a": w_br_a,
         "w_br_b": w_br_b, "w_out": w_out, "g_ffn": g_ffn, "w_ffn_gate": w_ffn_gate, "w_ffn_up": w_ffn_up,
         "w_ffn_down": w_ffn_down, "g_final": g_final}
    m = {"w_ada": m_w_ada, "b_ada": m_b_ada, "g_mix": m_g_mix, "w_in": m_w_in, "b_fgate": m_b_fgate,
         "w_br_a": m_w_br_a, "w_br_b": m_w_br_b, "w_out": m_w_out, "g_ffn": m_g_ffn, "w_ffn_gate": m_w_ffn_gate,
         "w_ffn_up": m_w_ffn_up, "w_ffn_down": m_w_ffn_down, "g_final": m_g_final}
    v = {"w_ada": v_w_ada, "b_ada": v_b_ada, "g_mix": v_g_mix, "w_in": v_w_in, "b_fgate": v_b_fgate,
         "w_br_a": v_w_br_a, "w_br_b": v_w_br_b, "w_out": v_w_out, "g_ffn": v_g_ffn, "w_ffn_gate": v_w_ffn_gate,
         "w_ffn_up": v_w_ffn_up, "w_ffn_down": v_w_ffn_down, "g_final": v_g_final}
    names = list(w)
    delta, new_m, new_v = {}, {}, {}

    transposed = ("w_in", "w_ffn_gate", "w_ffn_up")

    def update(n):
        shape = w[n].shape
        if n in transposed:
            g_t = g[n]
            dl, mn, vn = _adamw(w[n][0].T, g_t, m[n][0].T, v[n][0].T, "adamw_" + n)
            g[n], delta[n], new_m[n], new_v[n] = g_t.T[None], dl.T[None], mn.T[None], vn.T[None]
            return
        two_d = (lambda t: t.reshape(shape[-2:])) if len(shape) == 3 else (lambda t: t)
        dl, mn, vn = _adamw(two_d(w[n]), two_d(g[n]), two_d(m[n]), two_d(v[n]), "adamw_" + n)
        delta[n], new_m[n], new_v[n] = dl.reshape(shape), mn.reshape(shape), vn.reshape(shape)

    for n in list(g):
        update(n)
    done = sum(delta[n].reshape(-1)[:N_FOX_HEADS] for n in g)
    s_in_t, = from_chips(mix_state, done, mix_tags, "mixer", [True])
    g["w_in"] = s_in_t[:W_IN_SH]
    update("w_in")

    return (loss, grad_x[None], *[g[n] for n in names], *[delta[n] for n in names],
            *[new_m[n] for n in names], *[new_v[n] for n in names])
```

```python
import jax
import jax.numpy as jnp
import numpy as np
from jax import lax
from jax.experimental import pallas as pl
from jax.experimental.pallas import tpu as pltpu

f32 = jnp.float32
bf16 = jnp.bfloat16
SDS = jax.ShapeDtypeStruct
MESH = pl.DeviceIdType.MESH

N_DEV = 8
D = 1024
SEQ = 2048
HEAD_DIM = 64
N_FOX_HEADS = 8
FOX_W = 512
DIL_W = 768
DIL_OUT_W = 256
ROT_DIM = 16
ROPE_THETA = 500000.0
D_FF = 2816
IN_COLS = 5896
EPS = 1e-6
NEG = -1e30
ATT_SCALE = HEAD_DIM ** -0.5

ADAM_LR = 0.001
ADAM_B1 = 0.9
ADAM_B2 = 0.999
ADAM_EPS = 1e-08
ADAM_WD = 0.01
ADAM_STEP = 10

C_GA, C_GB, C_QB, C_KB, C_VB, C_QA, C_KA, C_VA, C_F = 0, 1024, 2304, 3072, 3840, 4608, 5120, 5632, 6144
PROJ_W = 6272
LANES = 128
VMEM_LIMIT = 52 * 1024 * 1024

W_IN_SH, W_IN_PAD = IN_COLS // N_DEV, 768
W_BR_SH = D // N_DEV
W_FF_SH, FF_PAD = D_FF // N_DEV, 384
FF_HID = N_DEV * FF_PAD
SMALL_ROWS = 16


def _params(sem=None):
    if sem is None:
        return pltpu.CompilerParams(vmem_limit_bytes=VMEM_LIMIT)
    return pltpu.CompilerParams(dimension_semantics=sem, vmem_limit_bytes=VMEM_LIMIT)


def _rowwise(fn, name, tiled, vecs, outs, reds=(), tile=256):
    nt, nv, no = len(tiled), len(vecs), len(outs)
    rows = tiled[0][0].shape[0]
    assert rows % tile == 0

    def body(*refs):
        tin = [r[...] for r in refs[:nt]]
        vin = [r[...] for r in refs[nt:nt + nv]]
        orefs = refs[nt + nv:nt + nv + no]
        rrefs = refs[nt + nv + no:]
        touts, routs = fn(tin, vin)
        for r, t in zip(orefs, touts, strict=True):
            r[...] = t.astype(r.dtype)
        if rrefs:
            @pl.when(pl.program_id(0) == 0)
            def _():
                for r in rrefs:
                    r[...] = jnp.zeros_like(r)
            for r, t in zip(rrefs, routs, strict=True):
                r[...] += t

    def col_map(cb):
        return lambda i: (i, cb)

    def whole_map(nd):
        return lambda i: (0,) * nd

    in_specs = [pl.BlockSpec((tile, w), col_map(cb)) for (_, w, cb) in tiled]
    in_specs += [pl.BlockSpec(v.shape, whole_map(v.ndim)) for v in vecs]
    out_specs = [pl.BlockSpec((tile, w), lambda i: (i, 0)) for (w, _) in outs]
    out_specs += [pl.BlockSpec((1, w), lambda i: (0, 0)) for w in reds]
    out_shape = [SDS((rows, w), dt) for (w, dt) in outs] + [SDS((1, w), f32) for w in reds]
    res = pl.pallas_call(
        body, grid=(rows // tile,), in_specs=in_specs, out_specs=out_specs, out_shape=out_shape, name=name,
        compiler_params=_params(("arbitrary",)),
    )(*[t[0] for t in tiled], *vecs)
    return res


def _matmul(a, b, *, ta=False, out_dtype=f32, name, tm, tn, by_shard=False, after=None):
    (m, k), n = ((a.shape[1], a.shape[0]) if ta else a.shape), b.shape[1]
    assert b.shape[0] == k and m % tm == 0 and n % tn == 0 and (ta or not by_shard)
    dims = (((0 if ta else 1,), (0,)), ((), ()))

    def body(a_ref, b_ref, *rest):
        p = lax.dot_general(a_ref[...].astype(bf16), b_ref[...].astype(bf16), dims, preferred_element_type=f32)
        o_ref = rest[-1]
        if by_shard:
            o_ref[0] = p.astype(o_ref.dtype)
        else:
            o_ref[...] = p.astype(o_ref.dtype)

    a_spec = pl.BlockSpec((k, tm), lambda i, j: (0, i)) if ta else pl.BlockSpec((tm, k), lambda i, j: (i, 0))
    if by_shard:
        assert tn == n // N_DEV
        out_spec, out_shape = pl.BlockSpec((1, tm, tn), lambda i, j: (j, i, 0)), SDS((N_DEV, m, tn), out_dtype)
    else:
        out_spec, out_shape = pl.BlockSpec((tm, tn), lambda i, j: (i, j)), SDS((m, n), out_dtype)
    extra_specs, extra = ([pl.BlockSpec(memory_space=pl.ANY)], [after]) if after is not None else ([], [])
    return pl.pallas_call(
        body, grid=(m // tm, n // tn), in_specs=[a_spec, pl.BlockSpec((k, tn), lambda i, j: (0, j))] + extra_specs,
        out_specs=out_spec, out_shape=out_shape, name=name, compiler_params=_params(("parallel", "parallel")),
    )(a, b, *extra)


def _matmul_stack(a, b, *, ta=False, tb=False, out_dtype=f32, name):
    def lanes(ref):
        return jnp.concatenate([ref[j] for j in range(N_DEV)], axis=1).astype(bf16)

    if ta:
        w = b.shape[1] // N_DEV

        def body(a_ref, b_ref, o_ref):
            p = _tn(a_ref[...].astype(bf16), b_ref[...].astype(bf16))
            for j in range(N_DEV):
                o_ref[j] = p[:, j * w:(j + 1) * w].astype(o_ref.dtype)

        return pl.pallas_call(body, out_shape=SDS((N_DEV, a.shape[1], w), out_dtype), name=name,
                              compiler_params=_params())(a, b)

    m, half = a.shape[0], a.shape[0] // 2
    n = b.shape[1] if tb else N_DEV * b.shape[2]

    def body(a_ref, b_ref, o_ref):
        av = a_ref[...].astype(bf16)
        o_ref[...] = (_nt(av, lanes(b_ref)) if tb else jnp.dot(av, lanes(b_ref), preferred_element_type=f32)
                      ).astype(o_ref.dtype)

    return pl.pallas_call(
        body, grid=(2,), in_specs=[pl.BlockSpec((half, a.shape[1]), lambda i: (i, 0)),
                                   pl.BlockSpec(b.shape, lambda i: (0, 0, 0))],
        out_specs=pl.BlockSpec((half, n), lambda i: (i, 0)), out_shape=SDS((m, n), out_dtype), name=name,
        compiler_params=_params(("parallel",)),
    )(a, b)


def _matmul_rows(form, a, b, after, fn, tiled, vecs, outs, reds, *, name, tm=512):
    norm = lambda ts: [t if isinstance(t, tuple) else (t, t.shape[1], 0) for t in ts]
    make, sources = a if isinstance(a, tuple) else (None, [a])
    sources, tiled = norm(sources), norm(tiled)
    m, k = sources[0][0].shape[0], (b.shape[0] if form == "nn" else b.shape[-1] * (N_DEV if form == "nt_stack" else 1))
    assert m % tm == 0
    ns, nt, nv, no = len(sources), len(tiled), len(vecs), len(outs)

    def body(*refs):
        src_refs, b_ref, refs = refs[:ns], refs[ns], refs[ns + 2:]
        if make is None:
            lhs = lambda lo, hi: src_refs[0][:, lo:hi]
        else:
            made = make([r[...] for r in src_refs]).astype(bf16)
            lhs = lambda lo, hi: made[:, lo:hi]
        if form == "nt_stack":
            w = b.shape[2]
            acc = _nt(lhs(0, w), b_ref[0])
            for j in range(1, N_DEV):
                acc = acc + _nt(lhs(j * w, (j + 1) * w), b_ref[j])
        elif form == "nt":
            acc = _nt(lhs(0, k), b_ref[...])
        else:
            acc = jnp.dot(lhs(0, k), b_ref[...], preferred_element_type=f32)
        if make is not None:
            refs[nt + nv][...] = made
            refs = refs[:nt + nv] + refs[nt + nv + 1:]
        orefs, rrefs = refs[nt + nv:nt + nv + no], refs[nt + nv + no:]
        touts, routs = fn([acc] + [r[...] for r in refs[:nt]], [r[...] for r in refs[nt:nt + nv]])
        for r, t in zip(orefs, touts, strict=True):
            r[...] = t.astype(r.dtype)

        @pl.when(pl.program_id(0) == 0)
        def _():
            for r in rrefs:
                r[...] = jnp.zeros_like(r)
        for r, t in zip(rrefs, routs, strict=True):
            r[...] += t

    def whole_map(nd):
        return lambda i: (0,) * nd

    def rows(width, cb=0):
        return pl.BlockSpec((tm, width), lambda i: (i, cb))

    made_out = [(k, bf16)] if make is not None else []
    return pl.pallas_call(
        body, grid=(m // tm,),
        in_specs=[rows(width, cb) for _, width, cb in sources]
        + [pl.BlockSpec(b.shape, whole_map(b.ndim), pipeline_mode=pl.Buffered(1)), pl.BlockSpec(memory_space=pl.ANY)]
        + [rows(width, cb) for _, width, cb in tiled] + [pl.BlockSpec(v.shape, whole_map(v.ndim)) for v in vecs],
        out_specs=[rows(width) for width, _ in made_out + list(outs)]
        + [pl.BlockSpec((1, width), lambda i: (0, 0)) for width in reds],
        out_shape=[SDS((m, width), dt) for width, dt in made_out + list(outs)]
        + [SDS((1, width), f32) for width in reds], name=name,
        compiler_params=_params(("arbitrary",)),
    )(*[t[0] for t in sources], b, after, *[t[0] for t in tiled], *vecs)


def _rms(x):
    r = lax.rsqrt(jnp.mean(x * x, axis=-1, keepdims=True) + EPS)
    return r, x * r


def _rms_bwd(r, xn, dxn):
    return r * (dxn - xn * jnp.mean(dxn * xn, axis=-1, keepdims=True))


def _colsum(t):
    return jnp.sum(t, axis=0, keepdims=True)


def _sigmoid(x):
    return 0.5 * jnp.tanh(0.5 * x) + 0.5


def _modulated_norm(x, g, shift, scale):
    _, xn = _rms(x)
    return (xn * g) * (1.0 + scale) + shift


def _pre1(x, modv, g_mix):
    def fn(t, v):
        (xt,), (mv, g) = t, v
        return [_modulated_norm(xt, g, mv[0:1], mv[1:2])], []
    return _rowwise(fn, "pre1", [(x, D, 0)], [modv, g_mix], [(D, bf16)])[0]


def _post1(ya, yb, proj, w_o, x, modv, g_ffn):
    def merge(t):
        ya_t, yb_t, ga, gb = t
        return _sigmoid(ga) * ya_t + _sigmoid(gb) * yb_t

    def fn(t, v):
        (mt, xt), (mv, g) = t, v
        x1 = xt + mv[2:3] * mt
        return [mt, x1, _modulated_norm(x1, g, mv[3:4], mv[4:5])], []
    return _matmul_rows("nn", (merge, [ya, yb, (proj, D, C_GA // D), (proj, D, C_GB // D)]), w_o, x, fn, [x],
                        [modv, g_ffn], [(D, f32), (D, f32), (D, bf16)], [], name="post1")


def _ffn_in(h, w_stack):
    def body(h_ref, w_ref, act_ref, au_ref):
        p = jnp.dot(h_ref[...], w_ref[0], preferred_element_type=f32)
        a, u = p[:, :FF_PAD], p[:, FF_PAD:]
        act_ref[...] = (a * _sigmoid(a) * u).astype(act_ref.dtype)
        au_ref[...] = p.astype(au_ref.dtype)

    return pl.pallas_call(
        body, grid=(N_DEV,),
        in_specs=[pl.BlockSpec((SEQ, D), lambda j: (0, 0)), pl.BlockSpec((1, D, 2 * FF_PAD), lambda j: (j, 0, 0))],
        out_specs=[pl.BlockSpec((SEQ, FF_PAD), lambda j: (0, j)), pl.BlockSpec((SEQ, 2 * FF_PAD), lambda j: (0, j))],
        out_shape=(SDS((SEQ, FF_HID), bf16), SDS((SEQ, 2 * FF_HID), bf16)), name="ffn_in",
        compiler_params=_params(("parallel",)),
    )(h, w_stack)


def _ffn_bwd_in(dff, w_down_stack, au):
    def body(d_ref, w_ref, au_ref, o_ref):
        dact = _nt(d_ref[...], w_ref[0])
        p = au_ref[...].astype(f32)
        a, u = p[:, :FF_PAD], p[:, FF_PAD:]
        sg = _sigmoid(a)
        o_ref[...] = jnp.concatenate([dact * u * (sg * (1.0 + a * (1.0 - sg))), dact * (a * sg)],
                                     axis=1).astype(o_ref.dtype)

    return pl.pallas_call(
        body, grid=(N_DEV,),
        in_specs=[pl.BlockSpec((SEQ, D), lambda j: (0, 0)), pl.BlockSpec((1, FF_PAD, D), lambda j: (j, 0, 0)),
                  pl.BlockSpec((SEQ, 2 * FF_PAD), lambda j: (0, j))],
        out_specs=pl.BlockSpec((SEQ, 2 * FF_PAD), lambda j: (0, j)),
        out_shape=SDS((SEQ, 2 * FF_HID), bf16), name="ffn_bwd_in", compiler_params=_params(("parallel",)),
    )(dff, w_down_stack, au)


def _final(act, w_down, x1, target, modv, g_final):
    def fn(t, v):
        (fft, x1t, tgt), (mv, g) = t, v
        x2 = x1t + mv[5:6] * fft
        r, xn = _rms(x2)
        err = xn * g - tgt
        dy = err * (1.0 / D)
        dx2 = _rms_bwd(r, xn, dy * g)
        return [dx2, dx2 * mv[5:6]], [_colsum(dy * xn), _colsum(dx2 * fft), _colsum(err * err) * (0.5 / D)]
    return _matmul_rows("nn", act, w_down, x1, fn, [x1, target], [modv, g_final], [(D, f32), (D, bf16)], [D, D, D],
                        name="final")


def _mid_bwd(dau, w_stack, after, x1, dx2, mix, modv, g_ffn):
    def fn(t, v):
        (dh, x1t, dx2t, mt), (mv, g) = t, v
        r, xn = _rms(x1t)
        dn = dh * (1.0 + mv[4:5])
        dx1 = dx2t + _rms_bwd(r, xn, dn * g)
        return [dx1, dx1 * mv[2:3]], [_colsum(dh), _colsum(dh * (xn * g)), _colsum(dn * xn), _colsum(dx1 * mt)]
    return _matmul_rows("nt_stack", dau, w_stack, after, fn, [x1, dx2, mix], [modv, g_ffn], [(D, f32), (D, bf16)],
                        [D, D, D, D], name="mid_bwd")


def _first_bwd(dproj, w_stack, after, x, dx1, modv, g_mix):
    def fn(t, v):
        (dh, xt, dx1t), (mv, g) = t, v
        r, xn = _rms(xt)
        dn = dh * (1.0 + mv[1:2])
        return [dx1t + _rms_bwd(r, xn, dn * g)], [_colsum(dh), _colsum(dh * (xn * g)), _colsum(dn * xn)]
    return _matmul_rows("nt_stack", dproj, w_stack, after, fn, [x, dx1], [modv, g_mix], [(D, f32)], [D, D, D],
                        name="first_bwd")


def _merge_bwd(dmix, w_o, after, ya, yb, proj):
    def fn(t, v):
        dm, ya_t, yb_t, ga, gb = t
        sa, sb = _sigmoid(ga), _sigmoid(gb)
        return [dm * sa, dm * sb, dm * ya_t * (sa * (1.0 - sa)), dm * yb_t * (sb * (1.0 - sb))], []
    return _matmul_rows("nt", dmix, w_o, after, fn, [ya, yb, (proj, D, C_GA // D), (proj, D, C_GB // D)], [],
                        [(D, bf16), (D, bf16), (D, bf16), (D, bf16)], [], name="merge_bwd")


def _rope_tables():
    half = ROT_DIM // 2
    pos = np.arange(SEQ, dtype=np.float32)
    inv_freq = np.float32(ROPE_THETA) ** (-np.arange(0, ROT_DIM, 2, dtype=np.float32) / np.float32(ROT_DIM))
    ang = pos[:, None] * inv_freq[None, :].astype(np.float32)
    cos, sin = np.cos(ang).astype(np.float32), np.sin(ang).astype(np.float32)
    pad = np.zeros((SEQ, HEAD_DIM - ROT_DIM), np.float32)
    zero = np.zeros((SEQ, half), np.float32)
    c_head = np.concatenate([cos, cos, pad + 1.0], axis=1)
    lo_head = np.concatenate([-sin, zero, pad], axis=1)
    hi_head = np.concatenate([zero, sin, pad], axis=1)
    return tuple(jnp.asarray(np.concatenate([t, t], axis=1)) for t in (c_head, lo_head, hi_head))


def _over_heads(tables):
    return [jnp.tile(t, (1, DIL_W // LANES)) for t in tables]


def _rope_fwd(proj, tables):
    half = ROT_DIM // 2

    def fn(t, v):
        q, k = t[:2]
        c, lo, hi = _over_heads(t[2:])
        rot = lambda z: z * c + pltpu.roll(z, DIL_W - half, 1) * lo + pltpu.roll(z, half, 1) * hi
        return [rot(q) * ATT_SCALE, rot(k)], []
    return _rowwise(fn, "rope_fwd", [(proj, DIL_W, C_QB // DIL_W), (proj, DIL_W, C_KB // DIL_W)]
                    + [(tb, LANES, 0) for tb in tables], [], [(DIL_W, f32)] * 2)


def _rope_bwd(dqs, dks, tables):
    half = ROT_DIM // 2

    def fn(t, v):
        dq_t, dk_t = jnp.concatenate(t[:N_GROUPS], axis=1), jnp.concatenate(t[N_GROUPS:2 * N_GROUPS], axis=1)
        c, lo, hi = _over_heads(t[2 * N_GROUPS:])
        rot_t = lambda z: z * c + pltpu.roll(z * lo, half, 1) + pltpu.roll(z * hi, DIL_W - half, 1)
        return [rot_t(dq_t), rot_t(dk_t)], []
    return _rowwise(fn, "rope_bwd", [(a, DIL_OUT_W, 0) for a in (*dqs, *dks)] + [(tb, LANES, 0) for tb in tables],
                    [], [(DIL_W, bf16), (DIL_W, bf16)])


def _dil_combine(outs, lses):
    def fn(t, v):
        o0, o1, o2, l0, l1, l2 = t
        m = jnp.maximum(jnp.maximum(l0, l1), l2)
        w0, w1, w2 = jnp.exp(l0 - m), jnp.exp(l1 - m), jnp.exp(l2 - m)
        tot = w0 + w1 + w2
        return [(w0 * o0 + w1 * o1 + w2 * o2) / tot, m + jnp.log(tot)], []
    w = DIL_OUT_W
    return _rowwise(fn, "dil_combine", [(t, w, 0) for t in (*outs, *lses)], [], [(w, f32), (w, f32)])


def _adamw_math(wt, gt, mt, vt):
    mn = ADAM_B1 * mt + (1.0 - ADAM_B1) * gt
    vn = ADAM_B2 * vt + (1.0 - ADAM_B2) * (gt * gt)
    m_hat = mn / (1.0 - ADAM_B1 ** ADAM_STEP)
    v_hat = vn / (1.0 - ADAM_B2 ** ADAM_STEP)
    return -ADAM_LR * (m_hat / (jnp.sqrt(v_hat) + ADAM_EPS) + ADAM_WD * wt), mn, vn


def _adamw(w, g, m, v, name):
    shape = w.shape
    if w.ndim == 1:
        w, g, m, v = (t.reshape(1, -1) for t in (w, g, m, v))
    rows, cols = w.shape
    if rows % 8 and rows > 8:
        return _adamw_by_cols(w, g, m, v, name)
    tile = 256 if rows % 256 == 0 and rows > 512 else rows

    def fn(t, _):
        return list(_adamw_math(*t)), []
    delta, mn, vn = _rowwise(fn, name, [(w, cols, 0), (g, cols, 0), (m, cols, 0), (v, cols, 0)], [],
                             [(cols, f32)] * 3, tile=tile)
    return delta.reshape(shape), mn.reshape(shape), vn.reshape(shape)


def _adamw_by_cols(w, g, m, v, name, tile=256):
    rows, cols = w.shape

    def body(w_ref, g_ref, m_ref, v_ref, d_ref, mn_ref, vn_ref):
        d_ref[...], mn_ref[...], vn_ref[...] = _adamw_math(w_ref[...], g_ref[...], m_ref[...], v_ref[...])

    spec = pl.BlockSpec((rows, tile), lambda j: (0, j))
    return pl.pallas_call(body, grid=(cols // tile,), in_specs=[spec] * 4, out_specs=[spec] * 3,
                          out_shape=[SDS((rows, cols), f32)] * 3, name=name,
                          compiler_params=_params(("parallel",)))(w, g, m, v)


def _ada_fwd(c_all, w_shard, b_shard):
    def body(c_ref, w_ref, b_ref, o_ref):
        cv = c_ref[...]
        sc = (cv * _sigmoid(cv)).astype(bf16)
        o_ref[...] = jnp.dot(sc, w_ref[...].astype(bf16), preferred_element_type=f32) + b_ref[...]
    return pl.pallas_call(body, out_shape=SDS((N_DEV, w_shard.shape[1]), f32), name="ada_fwd",
                          compiler_params=_params())(c_all, w_shard, b_shard)


def _ada_bwd(c_all, dmod_cols):
    def body(c_ref, d_ref, o_ref):
        cv = c_ref[...]
        sc = cv * _sigmoid(cv)
        o_ref[...] = lax.dot_general(sc, d_ref[...], (((0,), (0,)), ((), ())), precision=lax.Precision.HIGHEST,
                                     preferred_element_type=f32)
    return pl.pallas_call(body, out_shape=SDS((D, dmod_cols.shape[1]), f32), name="ada_bwd",
                          compiler_params=_params())(c_all, dmod_cols)


def _small_reduce(gathered, after):
    def body(g_ref, after_ref, o_ref, loss_ref):
        acc = g_ref[0]
        for d in range(1, N_DEV):
            acc = acc + g_ref[d]
        o_ref[...] = acc
        loss_ref[...] = jnp.zeros((1, LANES), f32) + jnp.sum(acc[10:11, :])
    return pl.pallas_call(body, out_shape=(SDS((SMALL_ROWS, D), f32), SDS((1, LANES), f32)), name="small_reduce",
                          in_specs=[pl.BlockSpec(memory_space=pltpu.VMEM), pl.BlockSpec(memory_space=pl.ANY)],
                          compiler_params=_params())(gathered, after)


FOX_BLK = 512
CUM_BLK = 128


def _fold_lanes(t, op):
    out = t[:, :LANES]
    for j in range(1, t.shape[1] // LANES):
        out = op(out, t[:, j * LANES:(j + 1) * LANES])
    return out


def _fox_gate_fwd(proj, b_pad):
    nblk = SEQ // CUM_BLK

    def body(f_ref, b_ref, col_ref):
        r = lax.broadcasted_iota(jnp.int32, (CUM_BLK, CUM_BLK), 0)
        c = lax.broadcasted_iota(jnp.int32, (CUM_BLK, CUM_BLK), 1)
        tri = (r >= c).astype(f32)
        carry = jnp.zeros((1, LANES), f32)
        for blk in range(nblk):
            z = f_ref[blk * CUM_BLK:(blk + 1) * CUM_BLK, :] + b_ref[...]
            logf = jnp.minimum(z, 0.0) - jnp.log1p(jnp.exp(-jnp.abs(z)))
            cs = jnp.dot(tri, logf, precision=lax.Precision.HIGHEST, preferred_element_type=f32) + carry
            col_ref[blk * CUM_BLK:(blk + 1) * CUM_BLK, :] = cs
            carry = cs[CUM_BLK - 1:CUM_BLK, :]

    return pl.pallas_call(
        body, grid=(1,), in_specs=[pl.BlockSpec((SEQ, LANES), lambda i: (0, C_F // LANES)),
                                   pl.BlockSpec((1, LANES), lambda i: (0, 0))],
        out_specs=pl.BlockSpec((SEQ, LANES), lambda i: (0, 0)),
        out_shape=SDS((SEQ, LANES), f32), name="fox_gate_fwd",
        compiler_params=_params(("arbitrary",)),
    )(proj, b_pad)


def _fox_gate_bwd(dF_row, proj, b_pad):
    nblk = SEQ // CUM_BLK

    def body(d_ref, f_ref, b_ref, df_ref, db_ref, col_ref):
        r = lax.broadcasted_iota(jnp.int32, (CUM_BLK, CUM_BLK), 0)
        c = lax.broadcasted_iota(jnp.int32, (CUM_BLK, CUM_BLK), 1)
        tri = (r <= c).astype(f32)
        lane = lax.broadcasted_iota(jnp.int32, (CUM_BLK, LANES), 1)
        col_ref[...] = d_ref[...].T
        carry = jnp.zeros((1, LANES), f32)
        total = jnp.zeros((1, LANES), f32)
        for blk in reversed(range(nblk)):
            rows = slice(blk * CUM_BLK, (blk + 1) * CUM_BLK)
            cs = jnp.dot(tri, col_ref[rows, :], precision=lax.Precision.HIGHEST, preferred_element_type=f32) + carry
            carry = cs[0:1, :]
            z = f_ref[rows, :] + b_ref[...]
            df = jnp.where(lane < N_FOX_HEADS, cs * _sigmoid(-z), 0.0)
            df_ref[rows, :] = df.astype(df_ref.dtype)
            total = total + _colsum(df)
        db_ref[...] = total

    return pl.pallas_call(
        body, grid=(1,), in_specs=[pl.BlockSpec((LANES, SEQ), lambda i: (0, 0)),
                                   pl.BlockSpec((SEQ, LANES), lambda i: (0, C_F // LANES)),
                                   pl.BlockSpec((1, LANES), lambda i: (0, 0))],
        out_specs=[pl.BlockSpec((SEQ, LANES), lambda i: (0, 0)), pl.BlockSpec((1, LANES), lambda i: (0, 0))],
        out_shape=(SDS((SEQ, LANES), bf16), SDS((1, LANES), f32)), name="fox_gate_bwd",
        scratch_shapes=[pltpu.VMEM((SEQ, LANES), f32)],
        compiler_params=_params(("arbitrary",)),
    )(dF_row, proj, b_pad)


def _nt(a, b):
    return lax.dot_general(a, b, (((1,), (1,)), ((), ())), preferred_element_type=f32)


def _tn(a, b):
    return lax.dot_general(a, b, (((0,), (0,)), ((), ())), preferred_element_type=f32)


def _fox_prep(proj, f_col):
    def fn(t, v):
        q, k, vv, fc = t
        lane = lax.broadcasted_iota(jnp.int32, (q.shape[0], LANES), 1)
        qs, ks = [], []
        for h in range(N_FOX_HEADS):
            pair, pos = divmod(h, 2)
            own = (lane >= pos * HEAD_DIM) & (lane < (pos + 1) * HEAD_DIM)
            base = (1 - pos) * HEAD_DIM
            f = fc[:, h:h + 1]
            hi = f.astype(bf16).astype(f32)
            mid = (f - hi).astype(bf16).astype(f32)
            lo = (f - hi) - mid
            one = jnp.ones_like(f)
            qa = jnp.where(own, q[:, pair * LANES:(pair + 1) * LANES] * ATT_SCALE, 0.0)
            ka = k[:, pair * LANES:(pair + 1) * LANES]
            for idx, (qv, kv) in enumerate([(hi, one), (mid, one), (lo, one), (one, -hi), (one, -mid), (one, -lo)]):
                sel = lane == base + idx
                qa = jnp.where(sel, qv, qa)
                ka = jnp.where(sel, kv, ka)
            qs.append(qa)
            ks.append(ka)
        return [jnp.concatenate(qs, axis=1), jnp.concatenate(ks, axis=1), vv], []
    w = N_FOX_HEADS * LANES
    return _rowwise(fn, "fox_prep", [(proj, FOX_W, C_QA // FOX_W), (proj, FOX_W, C_KA // FOX_W),
                                     (proj, FOX_W, C_VA // FOX_W), (f_col, LANES, 0)], [],
                    [(w, bf16), (w, bf16), (FOX_W, bf16)])


def _fox_fwd(q_aug, k_aug, v):
    blk = FOX_BLK
    npair = FOX_W // LANES

    def body(q_ref, k_ref, v_ref, o_ref, max_ref, sum_ref, s_scr):
        i = pl.program_id(1)
        tri = lax.broadcasted_iota(jnp.int32, (blk, blk), 0) >= lax.broadcasted_iota(jnp.int32, (blk, blk), 1)
        qh = [q_ref[:, h * LANES:(h + 1) * LANES] for h in range(2)]

        def logits(c, masked):
            off = pl.multiple_of(c * blk, blk)
            tops = []
            for h in range(2):
                s = _nt(qh[h], k_ref[pl.ds(off, blk), h * LANES:(h + 1) * LANES])
                if masked:
                    s = jnp.where(tri, s, NEG)
                s_scr[h, :, pl.ds(off, blk)] = s
                tops.append(_fold_lanes(s, jnp.maximum))
            return tops

        def pass_a(c, m):
            return tuple(jnp.maximum(a, b) for a, b in zip(m, logits(c, False)))

        m = lax.fori_loop(0, i, pass_a, tuple(jnp.full((blk, LANES), NEG, f32) for _ in range(2)))
        mx = [jnp.max(jnp.maximum(a, b), axis=1, keepdims=True) for a, b in zip(m, logits(i, True))]

        def pass_b(c, carry):
            off = pl.multiple_of(c * blk, blk)
            vv = v_ref[pl.ds(off, blk), :]
            new = []
            for h in range(2):
                l, acc = carry[h]
                p = jnp.exp(s_scr[h, :, pl.ds(off, blk)] - mx[h]).astype(bf16)
                new.append((l + _fold_lanes(p.astype(f32), jnp.add), acc + jnp.dot(p, vv, preferred_element_type=f32)))
            return tuple(new)

        zero = jnp.zeros((blk, LANES), f32)
        (l_a, acc_a), (l_b, acc_b) = lax.fori_loop(0, i + 1, pass_b, ((zero, zero), (zero, zero)))
        l_a = jnp.sum(l_a, axis=1, keepdims=True)
        l_b = jnp.sum(l_b, axis=1, keepdims=True)
        first = lax.broadcasted_iota(jnp.int32, (blk, LANES), 1) < HEAD_DIM
        o_ref[...] = jnp.where(first, acc_a / l_a, acc_b / l_b)
        max_ref[0] = jnp.where(first, mx[0], mx[1])
        sum_ref[0] = jnp.where(first, l_a, l_b)

    return pl.pallas_call(
        body, grid=(npair, SEQ // blk),
        in_specs=[pl.BlockSpec((blk, 2 * LANES), lambda p, i: (i, p)),
                  pl.BlockSpec((SEQ, 2 * LANES), lambda p, i: (0, p)),
                  pl.BlockSpec((SEQ, LANES), lambda p, i: (0, p))],
        out_specs=[pl.BlockSpec((blk, LANES), lambda p, i: (i, p))]
        + [pl.BlockSpec((1, blk, LANES), lambda p, i: (p, i, 0))] * 2,
        out_shape=(SDS((SEQ, FOX_W), f32),) + (SDS((npair, SEQ, LANES), f32),) * 2, name="fox_fwd",
        scratch_shapes=[pltpu.VMEM((2, blk, SEQ), f32)],
        compiler_params=_params(("parallel", "arbitrary")),
    )(q_aug, k_aug, v)


def _fox_bwd(q_aug, k_aug, v, do, o, row_max, row_sum, after):
    blk = FOX_BLK
    npair = FOX_W // LANES
    nblk = SEQ // blk

    def body(q_ref, k_ref, v_ref, do_ref, o_ref, max_ref, sum_ref, after_ref, dq_ref, dk_ref, dv_ref, df_ref, dq_acc,
             delta_ref, inv_ref):
        inv_ref[...] = 1.0 / sum_ref[0]
        lane_s = lax.broadcasted_iota(jnp.int32, (SEQ, LANES), 1)
        prod = do_ref[...].astype(bf16).astype(f32) * o_ref[...]
        d_a = jnp.sum(jnp.where(lane_s < HEAD_DIM, prod, 0.0), axis=1, keepdims=True)
        d_b = jnp.sum(jnp.where(lane_s >= HEAD_DIM, prod, 0.0), axis=1, keepdims=True)
        delta_ref[...] = jnp.where(lane_s < HEAD_DIM, d_a, d_b)
        dq_acc[...] = jnp.zeros_like(dq_acc)
        df_ref[...] = jnp.zeros_like(df_ref)
        lane = lax.broadcasted_iota(jnp.int32, (blk, LANES), 1)
        own = [lane < HEAD_DIM, lane >= HEAD_DIM]
        tri = lax.broadcasted_iota(jnp.int32, (blk, blk), 0) >= lax.broadcasted_iota(jnp.int32, (blk, blk), 1)

        def q_slab(qoff, h):
            return q_ref[pl.ds(qoff, blk), h * LANES:(h + 1) * LANES]

        def probs(qoff, h, k_h, masked):
            s = _nt(q_slab(qoff, h), k_h)
            if masked:
                s = jnp.where(tri, s, NEG)
            col = slice(h * HEAD_DIM, h * HEAD_DIM + 1)
            weights = jnp.exp(s - max_ref[0, pl.ds(qoff, blk), col]).astype(bf16).astype(f32)
            return weights * inv_ref[pl.ds(qoff, blk), col]

        def k_slabs(koff):
            return [k_ref[pl.ds(koff, blk), h * LANES:(h + 1) * LANES] for h in range(2)]

        def kv_step(kj, _):
            koff = pl.multiple_of(kj * blk, blk)
            k_aug = k_slabs(koff)
            k_own = [jnp.where(own[h], k_aug[h], jnp.zeros_like(k_aug[h])) for h in range(2)]
            vv = v_ref[pl.ds(koff, blk), :]
            v_own = [jnp.where(own[h], vv, jnp.zeros_like(vv)) for h in range(2)]

            def q_tile(qi, carry, masked):
                qoff = pl.multiple_of(qi * blk, blk)
                dd = do_ref[pl.ds(qoff, blk), :].astype(bf16)
                new, dq_add = [], None
                for h in range(2):
                    dk_h, dv_h, dcol = carry[h]
                    p = probs(qoff, h, k_aug[h], masked)
                    dl = p * (_nt(dd, v_own[h]) - delta_ref[pl.ds(qoff, blk), h * HEAD_DIM:h * HEAD_DIM + 1])
                    dlb = dl.astype(bf16)
                    part = jnp.dot(dlb, k_own[h], preferred_element_type=f32)
                    dq_add = part if dq_add is None else dq_add + part
                    new.append((dk_h + _tn(dlb, q_slab(qoff, h)), dv_h + _tn(p.astype(bf16), dd),
                                dcol + _colsum(dl)))
                dq_acc[pl.ds(qoff, blk), :] += dq_add * ATT_SCALE
                return tuple(new)

            zero = (jnp.zeros((blk, LANES), f32), jnp.zeros((blk, LANES), f32), jnp.zeros((1, blk), f32))
            carry = q_tile(kj, (zero, zero), True)
            (dk_a, dv_a, dcol_a), (dk_b, dv_b, dcol_b) = lax.fori_loop(
                kj + 1, nblk, lambda qi, cr: q_tile(qi, cr, False), carry)
            dk_ref[pl.ds(koff, blk), :] = jnp.where(own[0], dk_a, dk_b).astype(dk_ref.dtype)
            dv_ref[pl.ds(koff, blk), :] = jnp.where(own[0], dv_a, dv_b).astype(dv_ref.dtype)
            df_ref[0, 0:1, pl.ds(koff, blk)] = -dcol_a
            df_ref[0, 1:2, pl.ds(koff, blk)] = -dcol_b
            return 0

        lax.fori_loop(0, nblk, kv_step, 0)
        dq_ref[...] = dq_acc[...].astype(dq_ref.dtype)

    pair_aug = pl.BlockSpec((SEQ, 2 * LANES), lambda p: (0, p))
    slab = pl.BlockSpec((SEQ, LANES), lambda p: (0, p))
    per_pair = pl.BlockSpec((1, SEQ, LANES), lambda p: (p, 0, 0))
    rows = pl.BlockSpec((1, 8, SEQ), lambda p: (p, 0, 0))
    return pl.pallas_call(
        body, grid=(npair,),
        in_specs=[pair_aug, pair_aug, slab, slab, slab, per_pair, per_pair, pl.BlockSpec(memory_space=pl.ANY)],
        out_specs=[slab, slab, slab, rows],
        out_shape=(SDS((SEQ, FOX_W), bf16),) * 3 + (SDS((npair, 8, SEQ), f32),), name="fox_bwd",
        scratch_shapes=[pltpu.VMEM((SEQ, LANES), f32)] * 3,
        compiler_params=_params(("parallel",)),
    )(q_aug, k_aug, v, do, o, row_max, row_sum, after)


DIL_BLK = 128
DILATIONS = (1, 4, 16)
N_GROUPS = len(DILATIONS)
DIL_PAIRS = DIL_OUT_W // LANES


def _dil_blocks(d):
    r1 = lax.broadcasted_iota(jnp.int32, (2 * DIL_BLK, DIL_BLK), 0) & (DIL_BLK - 1)
    c1 = lax.broadcasted_iota(jnp.int32, (2 * DIL_BLK, DIL_BLK), 1)
    r2 = lax.broadcasted_iota(jnp.int32, (2 * DIL_BLK, 2 * DIL_BLK), 0) & (DIL_BLK - 1)
    c2 = lax.broadcasted_iota(jnp.int32, (2 * DIL_BLK, 2 * DIL_BLK), 1)
    band = ((c2 < DIL_BLK) & (c2 >= r2)) | ((c2 >= DIL_BLK) & (c2 - DIL_BLK <= r2))
    out = []
    for r in range(d):
        for b in range(SEQ // d // DIL_BLK):
            rows = pl.ds(r + d * DIL_BLK * b, DIL_BLK, stride=d)
            if b == 0:
                out.append((rows, rows, r1 >= c1))
            else:
                out.append((rows, pl.ds(r + d * DIL_BLK * (b - 1), 2 * DIL_BLK, stride=d), band))
    return out


def _dil_v_spec(g):
    return pl.BlockSpec((SEQ, LANES), lambda p: (0, C_VB // LANES + DIL_PAIRS * g + p))


def _stack_heads(t, first):
    zero = jnp.zeros_like(t)
    return jnp.concatenate([jnp.where(first, t, zero), jnp.where(first, zero, t)], axis=0)


def _dil_fwd(q, k, v, g):
    def body(q_ref, k_ref, v_ref, o_ref, lse_ref):
        first = lax.broadcasted_iota(jnp.int32, (DIL_BLK, LANES), 1) < HEAD_DIM
        for rows, krows, mask in _dil_blocks(DILATIONS[g]):
            qv, kk, vv = q_ref[rows, :].astype(bf16), k_ref[krows, :].astype(bf16), v_ref[krows, :].astype(bf16)
            s = jnp.where(mask, _nt(_stack_heads(qv, first), kk), NEG)
            m = jnp.max(s, axis=1, keepdims=True)
            p = jnp.exp(s - m)
            l = jnp.sum(p, axis=1, keepdims=True)
            out = jnp.dot(p.astype(bf16), vv, preferred_element_type=f32) / l
            lse = m + jnp.log(l)
            o_ref[rows, :] = jnp.where(first, out[:DIL_BLK], out[DIL_BLK:])
            lse_ref[rows, :] = jnp.where(first, lse[:DIL_BLK], lse[DIL_BLK:])

    grouped = pl.BlockSpec((SEQ, LANES), lambda p: (0, DIL_PAIRS * g + p))
    own = pl.BlockSpec((SEQ, LANES), lambda p: (0, p))
    shape = SDS((SEQ, DIL_OUT_W), f32)
    return pl.pallas_call(
        body, grid=(DIL_PAIRS,), in_specs=[grouped, grouped, _dil_v_spec(g)], out_specs=[own] * 2,
        out_shape=(shape, shape),
        name=f"dil_fwd_{DILATIONS[g]}", compiler_params=_params(("parallel",)),
    )(q, k, v)


def _dil_bwd(q, k, v, do, out, lse, g):
    def body(q_ref, k_ref, v_ref, do_ref, out_ref, lse_ref, dq_ref, dk_ref, dv_ref):
        first = lax.broadcasted_iota(jnp.int32, (DIL_BLK, LANES), 1) < HEAD_DIM
        dk_ref[...] = jnp.zeros_like(dk_ref)
        dv_ref[...] = jnp.zeros_like(dv_ref)
        for rows, krows, mask in _dil_blocks(DILATIONS[g]):
            qv, kk, vv = q_ref[rows, :].astype(bf16), k_ref[krows, :].astype(bf16), v_ref[krows, :].astype(bf16)
            dov, lsev = do_ref[rows, :], lse_ref[rows, :]
            prod = dov * out_ref[rows, :]
            delta = jnp.concatenate([jnp.sum(jnp.where(first, prod, 0.0), axis=1, keepdims=True),
                                     jnp.sum(jnp.where(first, 0.0, prod), axis=1, keepdims=True)], axis=0)
            q2 = _stack_heads(qv, first)
            do2 = _stack_heads(dov.astype(bf16), first)
            per_head = lambda t: jnp.concatenate([t[:, 0:1], t[:, HEAD_DIM:HEAD_DIM + 1]], axis=0)
            p = jnp.exp(jnp.where(mask, _nt(q2, kk), NEG) - per_head(lsev))
            dl = (p * (_nt(do2, vv) - delta)).astype(bf16)
            dq = jnp.dot(dl, kk, preferred_element_type=f32)
            dq_ref[rows, :] = jnp.where(first, dq[:DIL_BLK], dq[DIL_BLK:]) * ATT_SCALE
            dk_ref[krows, :] += _tn(dl, q2)
            dv_ref[krows, :] += _tn(p.astype(bf16), do2)

    grouped = pl.BlockSpec((SEQ, LANES), lambda p: (0, DIL_PAIRS * g + p))
    own = pl.BlockSpec((SEQ, LANES), lambda p: (0, p))
    shape = SDS((SEQ, DIL_OUT_W), f32)
    return pl.pallas_call(
        body, grid=(DIL_PAIRS,), in_specs=[grouped, grouped, _dil_v_spec(g)] + [own] * 3, out_specs=[own] * 3,
        out_shape=(shape, shape, shape), name=f"dil_bwd_{DILATIONS[g]}", compiler_params=_params(("parallel",)),
    )(q, k, v, do, out, lse)


def _position():
    return lax.axis_index("x"), lax.axis_index("y"), lax.axis_index("c")


def _all_gather(block, name):
    def body(x_ref, out_ref, send_sems, recv_sems, local_sem):
        x, y, c = _position()
        me, sibling = (x, y, c), (x, y, 1 - c)
        chips = [(1 - x, y), (x, 1 - y), (1 - x, 1 - y)]

        def slot(px, py, pc):
            return out_ref.at[4 * px + 2 * py + pc]

        def copy(k, blk, to, src=None):
            return pltpu.make_async_remote_copy(
                src_ref=slot(*blk) if src is None else src, dst_ref=slot(*blk),
                send_sem=send_sems.at[k], recv_sem=recv_sems.at[k], device_id=to, device_id_type=MESH)

        mine = pltpu.make_async_copy(x_ref, slot(*me), local_sem)
        mine.start()
        first = [copy(0, me, sibling, src=x_ref)]
        first += [copy(1 + j, me, (*chip, c), src=x_ref) for j, chip in enumerate(chips)]
        for cp in first:
            cp.start()
        passed = [copy(4 + j, (*chip, c), sibling) for j, chip in enumerate(chips)]
        for j, chip in enumerate(chips):
            copy(1 + j, (*chip, c), me).wait_recv()
            passed[j].start()
        copy(0, sibling, me).wait_recv()
        for j, chip in enumerate(chips):
            copy(4 + j, (*chip, 1 - c), me).wait_recv()
        for cp in first + passed:
            cp.wait_send()
        mine.wait()

    return pl.pallas_call(
        body, out_shape=SDS((N_DEV,) + block.shape, block.dtype),
        in_specs=[pl.BlockSpec(memory_space=pl.ANY)], out_specs=pl.BlockSpec(memory_space=pl.ANY),
        scratch_shapes=[pltpu.SemaphoreType.DMA((7,)), pltpu.SemaphoreType.DMA((7,)), pltpu.SemaphoreType.DMA],
        name=name,
    )(block)


HBM_SPEC = pl.BlockSpec(memory_space=pltpu.HBM)
SEM_SPEC = pl.BlockSpec(memory_space=pltpu.SEMAPHORE)
SPLIT_COPY = pltpu.CompilerParams(has_side_effects=pltpu.SideEffectType.DATAFLOW_SIDE_EFFECTING)


def _in_hbm(t):
    return pltpu.with_memory_space_constraint(t, pltpu.HBM)


def _pair_copies(g_refs, land_refs, send_sems, recv_sems):
    x, y, c = _position()
    return [pltpu.make_async_remote_copy(
        src_ref=g.at[2 * k + (1 - c)], dst_ref=land.at[k], send_sem=send_sems.at[4 * a + k],
        recv_sem=recv_sems.at[4 * a + k], device_id=(x, y, 1 - c), device_id_type=MESH)
        for a, (g, land) in enumerate(zip(g_refs, land_refs, strict=True)) for k in range(4)]


def _chip_copies(t_refs, land_refs, send_sems, recv_sems):
    x, y, c = _position()
    chips = [(1 - x, y), (x, 1 - y), (1 - x, 1 - y)]
    return [pltpu.make_async_remote_copy(
        src_ref=t.at[2 * px + py], dst_ref=land.at[j], send_sem=send_sems.at[3 * a + j],
        recv_sem=recv_sems.at[3 * a + j], device_id=(px, py, c), device_id_type=MESH)
        for a, (t, land) in enumerate(zip(t_refs, land_refs, strict=True)) for j, (px, py) in enumerate(chips)]


_ROUNDS = {"pair": (_pair_copies, 4), "chip": (_chip_copies, 3)}


def _exchange_start(kind, ts, name):
    copies, slots = _ROUNDS[kind]
    n = len(ts)
    lands = [_in_hbm(lax.empty((slots,) + t.shape[1:], t.dtype)) for t in ts]

    def body(*refs):
        for cp in copies(refs[:n], refs[n:2 * n], refs[2 * n], refs[2 * n + 1]):
            cp.start()
        refs[-1][...] = jnp.zeros_like(refs[-1])

    sems = pltpu.SemaphoreType.DMA((slots * n,))
    res = pl.pallas_call(
        body, name=name, in_specs=[HBM_SPEC] * (2 * n),
        out_shape=(sems, sems, *[pltpu.HBM(t.shape, t.dtype) for t in (*ts, *lands)], SDS((8, LANES), f32)),
        out_specs=(SEM_SPEC, SEM_SPEC, *[HBM_SPEC] * (2 * n), pl.BlockSpec(memory_space=pltpu.VMEM)),
        input_output_aliases={i: 2 + i for i in range(2 * n)}, compiler_params=SPLIT_COPY,
    )(*[_in_hbm(t) for t in ts], *lands)
    return res[:-1], res[-1]


def _exchange_wait(kind, state, after, name):
    copies, _ = _ROUNDS[kind]
    send_sems, recv_sems, *arrays = state
    n = len(arrays) // 2

    def body(*refs):
        for cp in copies(refs[:n], refs[n:2 * n], refs[2 * n], refs[2 * n + 1]):
            cp.wait_send()
            cp.wait_recv()

    res = pl.pallas_call(
        body, name=name, in_specs=[HBM_SPEC] * (2 * n) + [SEM_SPEC, SEM_SPEC, pl.BlockSpec(memory_space=pl.ANY)],
        out_shape=[pltpu.HBM(t.shape, t.dtype) for t in arrays], out_specs=[HBM_SPEC] * (2 * n),
        input_output_aliases={i: i for i in range(2 * n)}, compiler_params=SPLIT_COPY,
    )(*arrays, send_sems, recv_sems, after)
    return res[:n], res[n:]


def _gather_copies(x_refs, out_refs, send_sems, recv_sems):
    x, y, c = _position()
    peers = [(x, y, 1 - c), (1 - x, y, c), (x, 1 - y, c), (1 - x, 1 - y, c)]
    sends, arrivals = [], []
    for a, (x_ref, out_ref) in enumerate(zip(x_refs, out_refs, strict=True)):
        for k, (px, py, pc) in enumerate(peers):
            sems = dict(send_sem=send_sems.at[4 * a + k], recv_sem=recv_sems.at[4 * a + k],
                        device_id=(px, py, pc), device_id_type=MESH)
            sends.append(pltpu.make_async_remote_copy(src_ref=x_ref, dst_ref=out_ref.at[4 * x + 2 * y + c], **sems))
            arrivals.append(pltpu.make_async_remote_copy(src_ref=x_ref, dst_ref=out_ref.at[4 * px + 2 * py + pc],
                                                         **sems))
    return sends, arrivals


def _gather_start(blocks, after, name):
    n = len(blocks)
    outs = [_in_hbm(lax.empty((N_DEV,) + b.shape, b.dtype)) for b in blocks]

    def body(*refs):
        sends, _ = _gather_copies(refs[:n], refs[n:2 * n], refs[2 * n + 1], refs[2 * n + 2])
        for cp in sends:
            cp.start()
        refs[-1][...] = jnp.zeros_like(refs[-1])

    sems = pltpu.SemaphoreType.DMA((4 * n,))
    res = pl.pallas_call(
        body, name=name, in_specs=[HBM_SPEC] * (2 * n) + [pl.BlockSpec(memory_space=pl.ANY)],
        out_shape=(sems, sems, *[pltpu.HBM(t.shape, t.dtype) for t in (*blocks, *outs)], SDS((8, LANES), f32)),
        out_specs=(SEM_SPEC, SEM_SPEC, *[HBM_SPEC] * (2 * n), pl.BlockSpec(memory_space=pltpu.VMEM)),
        input_output_aliases={i: 2 + i for i in range(2 * n)}, compiler_params=SPLIT_COPY,
    )(*[_in_hbm(b) for b in blocks], *outs, after)
    return res[:-1], res[-1]


def _gather_wait(state, after, name):
    send_sems, recv_sems, *arrays = state
    n = len(arrays) // 2

    def body(*refs):
        sends, arrivals = _gather_copies(refs[:n], refs[n:2 * n], refs[2 * n], refs[2 * n + 1])
        for cp in sends:
            cp.wait_send()
        for cp in arrivals:
            cp.wait_recv()

    res = pl.pallas_call(
        body, name=name, in_specs=[HBM_SPEC] * (2 * n) + [SEM_SPEC, SEM_SPEC, pl.BlockSpec(memory_space=pl.ANY)],
        out_shape=[pltpu.HBM(t.shape, t.dtype) for t in arrays], out_specs=[HBM_SPEC] * (2 * n),
        input_output_aliases={i: i for i in range(2 * n)}, compiler_params=SPLIT_COPY,
    )(*arrays, send_sems, recv_sems, after)
    return res[:n], res[n:]


def _gather_finish(partial, name):
    n = len(partial)

    def body(*refs):
        in_refs, out_refs = refs[:n], refs[n:2 * n]
        send_sems, recv_sems = refs[2 * n:]
        x, y, c = _position()
        chips = [(1 - x, y), (x, 1 - y), (1 - x, 1 - y)]
        copies = []
        for a in range(n):
            for j, (px, py) in enumerate(chips):
                cp = pltpu.make_async_remote_copy(
                    src_ref=in_refs[a].at[4 * px + 2 * py + c], dst_ref=out_refs[a].at[4 * px + 2 * py + c],
                    send_sem=send_sems.at[a, j], recv_sem=recv_sems.at[a, j], device_id=(x, y, 1 - c),
                    device_id_type=MESH)
                cp.start()
                copies.append(cp)
        for a in range(n):
            for j, (px, py) in enumerate(chips):
                pltpu.make_async_remote_copy(
                    src_ref=in_refs[a].at[4 * px + 2 * py + (1 - c)], dst_ref=out_refs[a].at[4 * px + 2 * py + (1 - c)],
                    send_sem=send_sems.at[a, j], recv_sem=recv_sems.at[a, j], device_id=(x, y, 1 - c),
                    device_id_type=MESH).wait_recv()
        for cp in copies:
            cp.wait_send()

    hbm = pl.BlockSpec(memory_space=pl.ANY)
    return pl.pallas_call(
        body, out_shape=[SDS(p.shape, p.dtype) for p in partial], in_specs=[hbm] * n, out_specs=[hbm] * n,
        input_output_aliases={a: a for a in range(n)},
        scratch_shapes=[pltpu.SemaphoreType.DMA((n, 3)), pltpu.SemaphoreType.DMA((n, 3))],
        name=name,
    )(*partial)


def _row_tile(rows):
    return 512 if rows % 512 == 0 and rows > 512 else rows


def _pair_add(g, r1, core, name):
    def body(c_ref, g_ref, r_ref, o_ref):
        o_ref[...] = (g_ref[...].astype(f32) + r_ref[...].astype(f32)).astype(o_ref.dtype)

    rows, cols = g.shape[1:]
    tile = _row_tile(rows)
    blk = (1, tile, cols)
    return pl.pallas_call(
        body, out_shape=SDS((4, rows, cols), g.dtype), name=name,
        grid_spec=pltpu.PrefetchScalarGridSpec(
            num_scalar_prefetch=1, grid=(4, rows // tile),
            in_specs=[pl.BlockSpec(blk, lambda k, i, c_ref: (2 * k + c_ref[0], i, 0)),
                      pl.BlockSpec(blk, lambda k, i, c_ref: (k, i, 0))],
            out_specs=pl.BlockSpec(blk, lambda k, i, c_ref: (k, i, 0))),
        compiler_params=_params(("parallel", "arbitrary")),
    )(core, g, r1)


def _chip_add(t, r2, chip, name, transposed=False):
    def body(c_ref, t_ref, r_ref, o_ref):
        s = ((t_ref[0].astype(f32) + r_ref[0].astype(f32)) + r_ref[1].astype(f32)) + r_ref[2].astype(f32)
        o_ref[...] = s.T if transposed else s

    rows, cols = t.shape[1:]
    tile = _row_tile(rows)
    out_spec = pl.BlockSpec((cols, tile), lambda i, c_ref: (0, i)) if transposed else pl.BlockSpec(
        (tile, cols), lambda i, c_ref: (i, 0))
    return pl.pallas_call(
        body, out_shape=SDS((cols, rows) if transposed else (rows, cols), f32), name=name,
        grid_spec=pltpu.PrefetchScalarGridSpec(
            num_scalar_prefetch=1, grid=(rows // tile,),
            in_specs=[pl.BlockSpec((1, tile, cols), lambda i, c_ref: (c_ref[0], i, 0)),
                      pl.BlockSpec((3, tile, cols), lambda i, c_ref: (0, i, 0))],
            out_specs=out_spec),
        compiler_params=_params(("arbitrary",)),
    )(chip, t, r2)


def _pad_to(t, axis, size):
    pads = [(0, 0)] * t.ndim
    pads[axis] = (0, size - t.shape[axis])
    return jnp.pad(t, pads)


_REF_COLS = {"qa": (0, FOX_W), "ka": (FOX_W, FOX_W), "va": (2 * FOX_W, FOX_W), "f": (3 * FOX_W, N_FOX_HEADS)}
_REF_COLS.update({n: (3 * FOX_W + N_FOX_HEADS + i * DIL_W, DIL_W) for i, n in enumerate(("qb", "kb", "vb"))})
_REF_COLS.update({n: (3 * FOX_W + N_FOX_HEADS + 3 * DIL_W + i * D, D) for i, n in enumerate(("ga", "gb"))})
_REF_ORDER = ("qa", "ka", "va", "f", "qb", "kb", "vb", "ga", "gb")


def _place_cols(sources, src_of, out_cols, name, row_block=512):
    arrays = [s[0] if isinstance(s, tuple) else s for s in sources]
    widths = [a.shape[-1] for a in arrays]
    rows = arrays[0].shape[-2]
    plan = []
    for t in range(out_cols // LANES):
        segs, c, end = [], t * LANES, (t + 1) * LANES
        while c < end:
            s = src_of(c)
            if s is None:
                c += 1
                continue
            n = 1
            while c + n < end and src_of(c + n) == (s[0], s[1] + n):
                n += 1
            segs.append((s[0], s[1], c - t * LANES, n))
            c += n
        plan.append(segs)

    def body(*refs):
        o_ref = refs[-1]
        for t, segs in enumerate(plan):
            acc = None
            for si, c0, o0, n in segs:
                a0 = c0 // LANES * LANES
                wide = min(2 * LANES, widths[si] - a0)
                win = refs[si][0, :, a0:a0 + wide] if isinstance(sources[si], tuple) else refs[si][:, a0:a0 + wide]
                r = lax.broadcasted_iota(jnp.int32, (wide, LANES), 0)
                c = lax.broadcasted_iota(jnp.int32, (wide, LANES), 1)
                pick = ((r - (c0 - a0) == c - o0) & (c >= o0) & (c < o0 + n)).astype(bf16)
                part = jnp.dot(win.astype(bf16), pick, preferred_element_type=f32)
                acc = part if acc is None else acc + part
            tile = jnp.zeros((row_block, LANES), f32) if acc is None else acc
            o_ref[:, t * LANES:(t + 1) * LANES] = tile.astype(o_ref.dtype)

    def spec(s):
        if isinstance(s, tuple):
            j = s[1]
            return pl.BlockSpec((1, row_block, s[0].shape[-1]), lambda i: (j, i, 0))
        return pl.BlockSpec((row_block, s.shape[-1]), lambda i: (i, 0))

    return pl.pallas_call(
        body, grid=(rows // row_block,), in_specs=[spec(s) for s in sources],
        out_specs=pl.BlockSpec((row_block, out_cols), lambda i: (i, 0)), out_shape=SDS((rows, out_cols), bf16),
        name=name, compiler_params=_params(("parallel",)),
    )(*arrays)


def _ref_piece(r):
    for name in _REF_ORDER:
        lo, width = _REF_COLS[name]
        if lo <= r < lo + width:
            return name, r - lo
    raise ValueError(r)


def _shard_pad_cols(pieces):
    names = [n for n in _REF_ORDER if n != "vb"]
    sources = [pieces[n] for n in names] + list(pieces["vb"])

    def src_of(c):
        j, i = divmod(c, W_IN_PAD)
        if i >= W_IN_SH:
            return None
        name, col = _ref_piece(j * W_IN_SH + i)
        if name == "vb":
            return len(names) + col // DIL_OUT_W, col % DIL_OUT_W
        return names.index(name), col

    return _place_cols(sources, src_of, N_DEV * W_IN_PAD, "place_dproj")


_SLABS = {"ga": C_GA, "gb": C_GB, "qb": C_QB, "kb": C_KB, "vb": C_VB, "qa": C_QA, "ka": C_KA, "va": C_VA, "f": C_F}


def _slab_w_in(stack):
    def src_of(c):
        for name, start in _SLABS.items():
            lo, width = _REF_COLS[name]
            if start <= c < start + width:
                return divmod(lo + c - start, W_IN_SH)
        return None

    return _place_cols([(stack, j) for j in range(N_DEV)], src_of, PROJ_W, "place_w_in")


def kernel(x, c, w_ada, b_ada, g_mix, w_in, b_fgate, w_br_a, w_br_b, w_out, g_ffn, w_ffn_gate, w_ffn_up, w_ffn_down, g_final, loss_target, m_w_ada, m_b_ada, m_g_mix, m_w_in, m_b_fgate, m_w_br_a, m_w_br_b, m_w_out, m_g_ffn, m_w_ffn_gate, m_w_ffn_up, m_w_ffn_down, m_g_final, v_w_ada, v_b_ada, v_g_mix, v_w_in, v_b_fgate, v_w_br_a, v_w_br_b, v_w_out, v_g_ffn, v_w_ffn_gate, v_w_ffn_up, v_w_ffn_down, v_g_final):
    px, py, pc = _position()
    dev = 4 * px + 2 * py + pc
    x2d, tgt = x[0], loss_target[0]

    c_all = _all_gather(c, "gather_c").reshape(N_DEV, D)
    ada_cols = w_ada.shape[2]
    b_shard = lax.dynamic_slice(b_ada, (0, dev * ada_cols), (1, ada_cols))
    mod_shard = _ada_fwd(c_all, w_ada[0], b_shard)
    mod_all = _all_gather(mod_shard, "gather_mod")
    modv = lax.dynamic_index_in_dim(mod_all, dev, axis=1, keepdims=False).reshape(6, D)
    h1 = _pre1(x2d, modv, g_mix)

    w_in_s = _all_gather(_pad_to(w_in[0], 1, W_IN_PAD).astype(bf16), "gather_w_in")
    gate_up = jnp.concatenate([_pad_to(w_ffn_gate[0], 1, FF_PAD), _pad_to(w_ffn_up[0], 1, FF_PAD)], axis=1)
    later = [w_br_a[0], w_br_b[0], w_out[0], gate_up, _pad_to(w_ffn_down[0], 0, FF_PAD)]
    later_state, later_token = _gather_start([t.astype(bf16) for t in later], w_in_s, "gather_rest_start")
    w_in_p = _slab_w_in(w_in_s)

    proj = _matmul(h1, w_in_p, name="mm_proj", tm=SEQ, tn=896, after=later_token)
    b_pad = jnp.pad(b_fgate, ((0, 0), (0, LANES - N_FOX_HEADS)))
    q_aug, k_aug, va = _fox_prep(proj, _fox_gate_fwd(proj, b_pad))
    ya_h, max_a, sum_a = _fox_fwd(q_aug, k_aug, va)

    tables = _rope_tables()
    qb_r, kb_r = _rope_fwd(proj, tables)
    by_group = [_dil_fwd(qb_r, kb_r, proj, grp) for grp in range(N_GROUPS)]
    yb_h, lse_b = _dil_combine([o for o, _ in by_group], [l for _, l in by_group])

    both_done = ya_h[:8, :LANES] + yb_h[:8, :LANES]
    mine, arrived = _gather_wait(later_state, both_done, "gather_rest_wait")
    w_a_s, w_b_s, w_o_s, w_gu_s, w_d_s = [
        lax.dynamic_update_slice(stack, block[None], (dev, 0, 0))
        for stack, block in zip(_gather_finish(arrived, "gather_rest_finish"), mine, strict=True)]
    w_o = w_o_s.reshape(D, D)
    w_d = w_d_s.reshape(FF_HID, D)
    ya = _matmul_stack(ya_h, w_a_s, name="mm_br_a")
    yb = _matmul_stack(yb_h, w_b_s, name="mm_br_b")

    merged, mix, x1, h2 = _post1(ya, yb, proj, w_o, x2d, modv, g_ffn)
    act, au = _ffn_in(h2, w_gu_s)

    dx2, dff, dg_final, dga_f, loss_lanes = _final(act, w_d, x1, tgt, modv, g_final.reshape(1, D))
    dau = _ffn_bwd_in(dff, w_d_s, au)

    core = pc.astype(jnp.int32).reshape(1)
    chip = (2 * px + py).astype(jnp.int32).reshape(1)

    def pair_done(state, after, tags, name):
        mine, theirs = _exchange_wait("pair", state, after, "pair_wait_" + name)
        sums = [_pair_add(g, r, core, "pair_add_" + t) for g, r, t in zip(mine, theirs, tags)]
        return _exchange_start("chip", sums, "chip_start_" + name)

    def from_chips(state, after, tags, name, transposed=None):
        sums, got = _exchange_wait("chip", state, after, "chip_wait_" + name)
        flips = transposed or [False] * len(tags)
        return [_chip_add(p, r, chip, "chip_add_" + t, f) for p, r, t, f in zip(sums, got, tags, flips)]

    g_gu = _matmul(h2, dau, ta=True, by_shard=True, out_dtype=bf16, name="mm_g_ffn_in", tm=D, tn=2 * FF_PAD)
    g_d = _matmul(act, dff, ta=True, out_dtype=bf16, name="mm_g_down", tm=FF_HID // 2, tn=512)
    ffn_tags = ["gu", "down"]
    ffn_pair, ffn_pair_token = _exchange_start("pair", [g_gu, g_d.reshape(N_DEV, FF_PAD, D)], "pair_start_ffn")

    dx1, dmix, dsh_f, dsc_f, dg_ffn, dga_m = _mid_bwd(dau, w_gu_s, ffn_pair_token, x1, dx2, mix, modv, g_ffn)
    ffn_state, ffn_token = pair_done(ffn_pair, dx1, ffn_tags, "ffn")
    dya, dyb, dga, dgb = _merge_bwd(dmix, w_o, ffn_token, ya, yb, proj)
    dya_h = _matmul_stack(dya, w_a_s, tb=True, name="mm_d_ya")
    dyb_h = _matmul_stack(dyb, w_b_s, tb=True, name="mm_d_yb")

    g_o = _matmul(merged, dmix, ta=True, out_dtype=bf16, name="mm_g_out", tm=D, tn=512)
    g_a = _matmul_stack(ya_h, dya, ta=True, out_dtype=bf16, name="mm_g_br_a")
    g_b = _matmul_stack(yb_h, dyb, ta=True, out_dtype=bf16, name="mm_g_br_b")
    rows_a, rows_b = FOX_W * W_BR_SH // D, DIL_OUT_W * W_BR_SH // D
    g_small = jnp.concatenate([g_a.reshape(N_DEV, rows_a, D), g_b.reshape(N_DEV, rows_b, D),
                               g_o.reshape(N_DEV, W_BR_SH, D)], axis=1)
    small_pair, small_pair_token = _exchange_start("pair", [g_small], "pair_start_small")

    dqa, dka, dva, dF = _fox_bwd(q_aug, k_aug, va, dya_h, ya_h, max_a, sum_a, small_pair_token)
    dF_row = jnp.pad(dF[:, :2, :].reshape(N_FOX_HEADS, SEQ), ((0, LANES - N_FOX_HEADS), (0, 0)))
    df, db_fgate = _fox_gate_bwd(dF_row, proj, b_pad)
    small_state, small_token = pair_done(small_pair, df, ["small"], "small")

    dil_grads = [_dil_bwd(qb_r, kb_r, proj, dyb_h, yb_h, lse_b, grp) for grp in range(N_GROUPS)]
    dqb, dkb = _rope_bwd([t[0] for t in dil_grads], [t[1] for t in dil_grads], tables)

    dproj = _shard_pad_cols({"qa": dqa, "ka": dka, "va": dva, "f": df, "qb": dqb, "kb": dkb,
                             "vb": [t[2] for t in dil_grads], "ga": dga, "gb": dgb})
    g_in = _matmul(h1, dproj, ta=True, by_shard=True, out_dtype=bf16, name="mm_g_in", tm=D, tn=W_IN_PAD,
                   after=small_token)
    mix_tags = ["in"]
    mix_pair, mix_pair_token = _exchange_start("pair", [g_in], "pair_start_mixer")

    grad_x, dsh_m, dsc_m, dg_mix = _first_bwd(dproj, w_in_s, mix_pair_token, x2d, dx1, modv, g_mix)

    pad_lane = lambda t: jnp.pad(t, ((0, 0), (0, D - t.shape[1])))
    small = jnp.concatenate([dsh_m, dsc_m, dga_m, dsh_f, dsc_f, dga_f, dg_mix, dg_ffn, dg_final,
                             pad_lane(db_fgate), loss_lanes, jnp.zeros((SMALL_ROWS - 11, D), f32)], axis=0)
    small_all = _all_gather(small, "gather_small")
    mix_state, mix_token = pair_done(mix_pair, small_all, mix_tags, "mixer")

    small_sum, loss_row = _small_reduce(small_all, mix_token)
    dmod_all = small_all[:, :6, :].reshape(N_DEV, 6 * D)
    g_w_ada = _ada_bwd(c_all, lax.dynamic_slice(dmod_all, (0, dev * ada_cols), (N_DEV, ada_cols)))
    s_gu_t, s_d = from_chips(ffn_state, small_sum, ffn_tags, "ffn", [True, False])
    s_small, = from_chips(small_state, small_sum, ["small"], "small")

    loss = loss_row[0, 0]
    g = {
        "w_ada": g_w_ada[None], "b_ada": small_sum[0:6].reshape(1, 6 * D), "g_mix": small_sum[6:7],
        "b_fgate": small_sum[9:10, :N_FOX_HEADS], "g_ffn": small_sum[7:8], "w_ffn_gate": s_gu_t[:W_FF_SH],
        "w_ffn_up": s_gu_t[FF_PAD:FF_PAD + W_FF_SH], "w_ffn_down": s_d[None, :W_FF_SH],
        "g_final": small_sum[8], "w_br_a": s_small[:rows_a].reshape(1, FOX_W, W_BR_SH),
        "w_br_b": s_small[rows_a:rows_a + rows_b].reshape(1, DIL_OUT_W, W_BR_SH), "w_out": s_small[None, rows_a + rows_b:],
    }
    w = {"w_ada": w_ada, "b_ada": b_ada, "g_mix": g_mix, "w_in": w_in, "b_fgate": b_fgate, "w_br_a": w_br_a,
         "w_br_b": w_br_b, "w_out": w_out, "g_ffn": g_ffn, "w_ffn_gate": w_ffn_gate, "w_ffn_up": w_ffn_up,
         "w_ffn_down": w_ffn_down, "g_final": g_final}
    m = {"w_ada": m_w_ada, "b_ada": m_b_ada, "g_mix": m_g_mix, "w_in": m_w_in, "b_fgate": m_b_fgate,
         "w_br_a": m_w_br_a, "w_br_b": m_w_br_b, "w_out": m_w_out, "g_ffn": m_g_ffn, "w_ffn_gate": m_w_ffn_gate,
         "w_ffn_up": m_w_ffn_up, "w_ffn_down": m_w_ffn_down, "g_final": m_g_final}
    v = {"w_ada": v_w_ada, "b_ada": v_b_ada, "g_mix": v_g_mix, "w_in": v_w_in, "b_fgate": v_b_fgate,
         "w_br_a": v_w_br_a, "w_br_b": v_w_br_b, "w_out": v_w_out, "g_ffn": v_g_ffn, "w_ffn_gate": v_w_ffn_gate,
         "w_ffn_up": v_w_ffn_up, "w_ffn_down": v_w_ffn_down, "g_final": v_g_final}
    names = list(w)
    delta, new_m, new_v = {}, {}, {}

    transposed = ("w_in", "w_ffn_gate", "w_ffn_up")

    def update(n):
        shape = w[n].shape
        if n in transposed:
            g_t = g[n]
            dl, mn, vn = _adamw(w[n][0].T, g_t, m[n][0].T, v[n][0].T, "adamw_" + n)
            g[n], delta[n], new_m[n], new_v[n] = g_t.T[None], dl.T[None], mn.T[None], vn.T[None]
            return
        two_d = (lambda t: t.reshape(shape[-2:])) if len(shape) == 3 else (lambda t: t)
        dl, mn, vn = _adamw(two_d(w[n]), two_d(g[n]), two_d(m[n]), two_d(v[n]), "adamw_" + n)
        delta[n], new_m[n], new_v[n] = dl.reshape(shape), mn.reshape(shape), vn.reshape(shape)

    for n in list(g):
        update(n)
    done = sum(delta[n].reshape(-1)[:N_FOX_HEADS] for n in g)
    s_in_t, = from_chips(mix_state, done, mix_tags, "mixer", [True])
    g["w_in"] = s_in_t[:W_IN_SH]
    update("w_in")

    return (loss, grad_x[None], *[g[n] for n in names], *[delta[n] for n in names],
            *[new_m[n] for n in names], *[new_v[n] for n in names])
```

```python
import jax
import jax.numpy as jnp
import numpy as np
from jax import lax
from jax.experimental import pallas as pl
from jax.experimental.pallas import tpu as pltpu

f32 = jnp.float32
bf16 = jnp.bfloat16
SDS = jax.ShapeDtypeStruct
MESH = pl.DeviceIdType.MESH

N_DEV = 8
D = 1024
SEQ = 2048
HEAD_DIM = 64
N_FOX_HEADS = 8
FOX_W = 512
DIL_W = 768
DIL_OUT_W = 256
ROT_DIM = 16
ROPE_THETA = 500000.0
D_FF = 2816
IN_COLS = 5896
EPS = 1e-6
NEG = -1e30
ATT_SCALE = HEAD_DIM ** -0.5

ADAM_LR = 0.001
ADAM_B1 = 0.9
ADAM_B2 = 0.999
ADAM_EPS = 1e-08
ADAM_WD = 0.01
ADAM_STEP = 10

C_GA, C_GB, C_QB, C_KB, C_VB, C_QA, C_KA, C_VA, C_F = 0, 1024, 2304, 3072, 3840, 4608, 5120, 5632, 6144
PROJ_W = 6272
LANES = 128
VMEM_LIMIT = 52 * 1024 * 1024

W_IN_SH, W_IN_PAD = IN_COLS // N_DEV, 768
W_BR_SH = D // N_DEV
W_FF_SH, FF_PAD = D_FF // N_DEV, 384
FF_HID = N_DEV * FF_PAD
SMALL_ROWS = 16


def _params(sem=None):
    if sem is None:
        return pltpu.CompilerParams(vmem_limit_bytes=VMEM_LIMIT)
    return pltpu.CompilerParams(dimension_semantics=sem, vmem_limit_bytes=VMEM_LIMIT)


def _rowwise(fn, name, tiled, vecs, outs, reds=(), tile=256):
    nt, nv, no = len(tiled), len(vecs), len(outs)
    rows = tiled[0][0].shape[0]
    assert rows % tile == 0

    def body(*refs):
        tin = [r[...] for r in refs[:nt]]
        vin = [r[...] for r in refs[nt:nt + nv]]
        orefs = refs[nt + nv:nt + nv + no]
        rrefs = refs[nt + nv + no:]
        touts, routs = fn(tin, vin)
        for r, t in zip(orefs, touts, strict=True):
            r[...] = t.astype(r.dtype)
        if rrefs:
            @pl.when(pl.program_id(0) == 0)
            def _():
                for r in rrefs:
                    r[...] = jnp.zeros_like(r)
            for r, t in zip(rrefs, routs, strict=True):
                r[...] += t

    def col_map(cb):
        return lambda i: (i, cb)

    def whole_map(nd):
        return lambda i: (0,) * nd

    in_specs = [pl.BlockSpec((tile, w), col_map(cb)) for (_, w, cb) in tiled]
    in_specs += [pl.BlockSpec(v.shape, whole_map(v.ndim)) for v in vecs]
    out_specs = [pl.BlockSpec((tile, w), lambda i: (i, 0)) for (w, _) in outs]
    out_specs += [pl.BlockSpec((1, w), lambda i: (0, 0)) for w in reds]
    out_shape = [SDS((rows, w), dt) for (w, dt) in outs] + [SDS((1, w), f32) for w in reds]
    res = pl.pallas_call(
        body, grid=(rows // tile,), in_specs=in_specs, out_specs=out_specs, out_shape=out_shape, name=name,
        compiler_params=_params(("arbitrary",)),
    )(*[t[0] for t in tiled], *vecs)
    return res


def _matmul(a, b, *, ta=False, out_dtype=f32, name, tm, tn, by_shard=False, after=None):
    (m, k), n = ((a.shape[1], a.shape[0]) if ta else a.shape), b.shape[1]
    assert b.shape[0] == k and m % tm == 0 and n % tn == 0 and (ta or not by_shard)
    dims = (((0 if ta else 1,), (0,)), ((), ()))

    def body(a_ref, b_ref, *rest):
        p = lax.dot_general(a_ref[...].astype(bf16), b_ref[...].astype(bf16), dims, preferred_element_type=f32)
        o_ref = rest[-1]
        if by_shard:
            o_ref[0] = p.astype(o_ref.dtype)
        else:
            o_ref[...] = p.astype(o_ref.dtype)

    a_spec = pl.BlockSpec((k, tm), lambda i, j: (0, i)) if ta else pl.BlockSpec((tm, k), lambda i, j: (i, 0))
    if by_shard:
        assert tn == n // N_DEV
        out_spec, out_shape = pl.BlockSpec((1, tm, tn), lambda i, j: (j, i, 0)), SDS((N_DEV, m, tn), out_dtype)
    else:
        out_spec, out_shape = pl.BlockSpec((tm, tn), lambda i, j: (i, j)), SDS((m, n), out_dtype)
    extra_specs, extra = ([pl.BlockSpec(memory_space=pl.ANY)], [after]) if after is not None else ([], [])
    return pl.pallas_call(
        body, grid=(m // tm, n // tn), in_specs=[a_spec, pl.BlockSpec((k, tn), lambda i, j: (0, j))] + extra_specs,
        out_specs=out_spec, out_shape=out_shape, name=name, compiler_params=_params(("parallel", "parallel")),
    )(a, b, *extra)


def _matmul_stack(a, b, *, ta=False, tb=False, out_dtype=f32, name):
    def lanes(ref):
        return jnp.concatenate([ref[j] for j in range(N_DEV)], axis=1).astype(bf16)

    if ta:
        w = b.shape[1] // N_DEV

        def body(a_ref, b_ref, o_ref):
            p = _tn(a_ref[...].astype(bf16), b_ref[...].astype(bf16))
            for j in range(N_DEV):
                o_ref[j] = p[:, j * w:(j + 1) * w].astype(o_ref.dtype)

        return pl.pallas_call(body, out_shape=SDS((N_DEV, a.shape[1], w), out_dtype), name=name,
                              compiler_params=_params())(a, b)

    m, half = a.shape[0], a.shape[0] // 2
    n = b.shape[1] if tb else N_DEV * b.shape[2]

    def body(a_ref, b_ref, o_ref):
        av = a_ref[...].astype(bf16)
        o_ref[...] = (_nt(av, lanes(b_ref)) if tb else jnp.dot(av, lanes(b_ref), preferred_element_type=f32)
                      ).astype(o_ref.dtype)

    return pl.pallas_call(
        body, grid=(2,), in_specs=[pl.BlockSpec((half, a.shape[1]), lambda i: (i, 0)),
                                   pl.BlockSpec(b.shape, lambda i: (0, 0, 0))],
        out_specs=pl.BlockSpec((half, n), lambda i: (i, 0)), out_shape=SDS((m, n), out_dtype), name=name,
        compiler_params=_params(("parallel",)),
    )(a, b)


def _matmul_rows(form, a, b, after, fn, tiled, vecs, outs, reds, *, name, tm=512):
    norm = lambda ts: [t if isinstance(t, tuple) else (t, t.shape[1], 0) for t in ts]
    make, sources = a if isinstance(a, tuple) else (None, [a])
    sources, tiled = norm(sources), norm(tiled)
    m, k = sources[0][0].shape[0], (b.shape[0] if form == "nn" else b.shape[-1] * (N_DEV if form == "nt_stack" else 1))
    assert m % tm == 0
    ns, nt, nv, no = len(sources), len(tiled), len(vecs), len(outs)

    def body(*refs):
        src_refs, b_ref, refs = refs[:ns], refs[ns], refs[ns + 2:]
        if make is None:
            lhs = lambda lo, hi: src_refs[0][:, lo:hi]
        else:
            made = make([r[...] for r in src_refs]).astype(bf16)
            lhs = lambda lo, hi: made[:, lo:hi]
        if form == "nt_stack":
            w = b.shape[2]
            acc = _nt(lhs(0, w), b_ref[0])
            for j in range(1, N_DEV):
                acc = acc + _nt(lhs(j * w, (j + 1) * w), b_ref[j])
        elif form == "nt":
            acc = _nt(lhs(0, k), b_ref[...])
        else:
            acc = jnp.dot(lhs(0, k), b_ref[...], preferred_element_type=f32)
        if make is not None:
            refs[nt + nv][...] = made
            refs = refs[:nt + nv] + refs[nt + nv + 1:]
        orefs, rrefs = refs[nt + nv:nt + nv + no], refs[nt + nv + no:]
        touts, routs = fn([acc] + [r[...] for r in refs[:nt]], [r[...] for r in refs[nt:nt + nv]])
        for r, t in zip(orefs, touts, strict=True):
            r[...] = t.astype(r.dtype)

        @pl.when(pl.program_id(0) == 0)
        def _():
            for r in rrefs:
                r[...] = jnp.zeros_like(r)
        for r, t in zip(rrefs, routs, strict=True):
            r[...] += t

    def whole_map(nd):
        return lambda i: (0,) * nd

    def rows(width, cb=0):
        return pl.BlockSpec((tm, width), lambda i: (i, cb))

    made_out = [(k, bf16)] if make is not None else []
    return pl.pallas_call(
        body, grid=(m // tm,),
        in_specs=[rows(width, cb) for _, width, cb in sources]
        + [pl.BlockSpec(b.shape, whole_map(b.ndim), pipeline_mode=pl.Buffered(1)), pl.BlockSpec(memory_space=pl.ANY)]
        + [rows(width, cb) for _, width, cb in tiled] + [pl.BlockSpec(v.shape, whole_map(v.ndim)) for v in vecs],
        out_specs=[rows(width) for width, _ in made_out + list(outs)]
        + [pl.BlockSpec((1, width), lambda i: (0, 0)) for width in reds],
        out_shape=[SDS((m, width), dt) for width, dt in made_out + list(outs)]
        + [SDS((1, width), f32) for width in reds], name=name,
        compiler_params=_params(("arbitrary",)),
    )(*[t[0] for t in sources], b, after, *[t[0] for t in tiled], *vecs)


def _rms(x):
    r = lax.rsqrt(jnp.mean(x * x, axis=-1, keepdims=True) + EPS)
    return r, x * r


def _rms_bwd(r, xn, dxn):
    return r * (dxn - xn * jnp.mean(dxn * xn, axis=-1, keepdims=True))


def _colsum(t):
    return jnp.sum(t, axis=0, keepdims=True)


def _sigmoid(x):
    return 0.5 * jnp.tanh(0.5 * x) + 0.5


def _modulated_norm(x, g, shift, scale):
    _, xn = _rms(x)
    return (xn * g) * (1.0 + scale) + shift


def _pre1(x, modv, g_mix):
    def fn(t, v):
        (xt,), (mv, g) = t, v
        return [_modulated_norm(xt, g, mv[0:1], mv[1:2])], []
    return _rowwise(fn, "pre1", [(x, D, 0)], [modv, g_mix], [(D, bf16)])[0]


def _post1(ya, yb, proj, w_o, x, modv, g_ffn):
    def merge(t):
        ya_t, yb_t, ga, gb = t
        return _sigmoid(ga) * ya_t + _sigmoid(gb) * yb_t

    def fn(t, v):
        (mt, xt), (mv, g) = t, v
        x1 = xt + mv[2:3] * mt
        return [mt, x1, _modulated_norm(x1, g, mv[3:4], mv[4:5])], []
    return _matmul_rows("nn", (merge, [ya, yb, (proj, D, C_GA // D), (proj, D, C_GB // D)]), w_o, x, fn, [x],
                        [modv, g_ffn], [(D, f32), (D, f32), (D, bf16)], [], name="post1")


def _ffn_in(h, w_stack):
    def body(h_ref, w_ref, act_ref, au_ref):
        p = jnp.dot(h_ref[...], w_ref[0], preferred_element_type=f32)
        a, u = p[:, :FF_PAD], p[:, FF_PAD:]
        act_ref[...] = (a * _sigmoid(a) * u).astype(act_ref.dtype)
        au_ref[...] = p.astype(au_ref.dtype)

    return pl.pallas_call(
        body, grid=(N_DEV,),
        in_specs=[pl.BlockSpec((SEQ, D), lambda j: (0, 0)), pl.BlockSpec((1, D, 2 * FF_PAD), lambda j: (j, 0, 0))],
        out_specs=[pl.BlockSpec((SEQ, FF_PAD), lambda j: (0, j)), pl.BlockSpec((SEQ, 2 * FF_PAD), lambda j: (0, j))],
        out_shape=(SDS((SEQ, FF_HID), bf16), SDS((SEQ, 2 * FF_HID), bf16)), name="ffn_in",
        compiler_params=_params(("parallel",)),
    )(h, w_stack)


def _ffn_bwd_in(dff, w_down_stack, au):
    def body(d_ref, w_ref, au_ref, o_ref):
        dact = _nt(d_ref[...], w_ref[0])
        p = au_ref[...].astype(f32)
        a, u = p[:, :FF_PAD], p[:, FF_PAD:]
        sg = _sigmoid(a)
        o_ref[...] = jnp.concatenate([dact * u * (sg * (1.0 + a * (1.0 - sg))), dact * (a * sg)],
                                     axis=1).astype(o_ref.dtype)

    return pl.pallas_call(
        body, grid=(N_DEV,),
        in_specs=[pl.BlockSpec((SEQ, D), lambda j: (0, 0)), pl.BlockSpec((1, FF_PAD, D), lambda j: (j, 0, 0)),
                  pl.BlockSpec((SEQ, 2 * FF_PAD), lambda j: (0, j))],
        out_specs=pl.BlockSpec((SEQ, 2 * FF_PAD), lambda j: (0, j)),
        out_shape=SDS((SEQ, 2 * FF_HID), bf16), name="ffn_bwd_in", compiler_params=_params(("parallel",)),
    )(dff, w_down_stack, au)


def _final(act, w_down, x1, target, modv, g_final):
    def fn(t, v):
        (fft, x1t, tgt), (mv, g) = t, v
        x2 = x1t + mv[5:6] * fft
        r, xn = _rms(x2)
        err = xn * g - tgt
        dy = err * (1.0 / D)
        dx2 = _rms_bwd(r, xn, dy * g)
        return [dx2, dx2 * mv[5:6]], [_colsum(dy * xn), _colsum(dx2 * fft), _colsum(err * err) * (0.5 / D)]
    return _matmul_rows("nn", act, w_down, x1, fn, [x1, target], [modv, g_final], [(D, f32), (D, bf16)], [D, D, D],
                        name="final")


def _mid_bwd(dau, w_stack, after, x1, dx2, mix, modv, g_ffn):
    def fn(t, v):
        (dh, x1t, dx2t, mt), (mv, g) = t, v
        r, xn = _rms(x1t)
        dn = dh * (1.0 + mv[4:5])
        dx1 = dx2t + _rms_bwd(r, xn, dn * g)
        return [dx1, dx1 * mv[2:3]], [_colsum(dh), _colsum(dh * (xn * g)), _colsum(dn * xn), _colsum(dx1 * mt)]
    return _matmul_rows("nt_stack", dau, w_stack, after, fn, [x1, dx2, mix], [modv, g_ffn], [(D, f32), (D, bf16)],
                        [D, D, D, D], name="mid_bwd")


def _first_bwd(dproj, w_stack, after, x, dx1, modv, g_mix):
    def fn(t, v):
        (dh, xt, dx1t), (mv, g) = t, v
        r, xn = _rms(xt)
        dn = dh * (1.0 + mv[1:2])
        return [dx1t + _rms_bwd(r, xn, dn * g)], [_colsum(dh), _colsum(dh * (xn * g)), _colsum(dn * xn)]
    return _matmul_rows("nt_stack", dproj, w_stack, after, fn, [x, dx1], [modv, g_mix], [(D, f32)], [D, D, D],
                        name="first_bwd")


def _merge_bwd(dmix, w_o, after, ya, yb, proj):
    def fn(t, v):
        dm, ya_t, yb_t, ga, gb = t
        sa, sb = _sigmoid(ga), _sigmoid(gb)
        return [dm * sa, dm * sb, dm * ya_t * (sa * (1.0 - sa)), dm * yb_t * (sb * (1.0 - sb))], []
    return _matmul_rows("nt", dmix, w_o, after, fn, [ya, yb, (proj, D, C_GA // D), (proj, D, C_GB // D)], [],
                        [(D, bf16), (D, bf16), (D, bf16), (D, bf16)], [], name="merge_bwd")


def _rope_tables():
    half = ROT_DIM // 2
    pos = np.arange(SEQ, dtype=np.float32)
    inv_freq = np.float32(ROPE_THETA) ** (-np.arange(0, ROT_DIM, 2, dtype=np.float32) / np.float32(ROT_DIM))
    ang = pos[:, None] * inv_freq[None, :].astype(np.float32)
    cos, sin = np.cos(ang).astype(np.float32), np.sin(ang).astype(np.float32)
    pad = np.zeros((SEQ, HEAD_DIM - ROT_DIM), np.float32)
    zero = np.zeros((SEQ, half), np.float32)
    c_head = np.concatenate([cos, cos, pad + 1.0], axis=1)
    lo_head = np.concatenate([-sin, zero, pad], axis=1)
    hi_head = np.concatenate([zero, sin, pad], axis=1)
    return tuple(jnp.asarray(np.concatenate([t, t], axis=1)) for t in (c_head, lo_head, hi_head))


def _over_heads(tables):
    return [jnp.tile(t, (1, DIL_W // LANES)) for t in tables]


def _rope_fwd(proj, tables):
    half = ROT_DIM // 2

    def fn(t, v):
        q, k = t[:2]
        c, lo, hi = _over_heads(t[2:])
        rot = lambda z: z * c + pltpu.roll(z, DIL_W - half, 1) * lo + pltpu.roll(z, half, 1) * hi
        return [rot(q) * ATT_SCALE, rot(k)], []
    return _rowwise(fn, "rope_fwd", [(proj, DIL_W, C_QB // DIL_W), (proj, DIL_W, C_KB // DIL_W)]
                    + [(tb, LANES, 0) for tb in tables], [], [(DIL_W, f32)] * 2)


def _rope_bwd(dqs, dks, tables):
    half = ROT_DIM // 2

    def fn(t, v):
        dq_t, dk_t = jnp.concatenate(t[:N_GROUPS], axis=1), jnp.concatenate(t[N_GROUPS:2 * N_GROUPS], axis=1)
        c, lo, hi = _over_heads(t[2 * N_GROUPS:])
        rot_t = lambda z: z * c + pltpu.roll(z * lo, half, 1) + pltpu.roll(z * hi, DIL_W - half, 1)
        return [rot_t(dq_t), rot_t(dk_t)], []
    return _rowwise(fn, "rope_bwd", [(a, DIL_OUT_W, 0) for a in (*dqs, *dks)] + [(tb, LANES, 0) for tb in tables],
                    [], [(DIL_W, bf16), (DIL_W, bf16)])


def _head_bcast_sum(d):
    lane = lax.broadcasted_iota(jnp.int32, d.shape, 1)
    out = jnp.zeros_like(d)
    for h in range(d.shape[1] // HEAD_DIM):
        sel = (lane >= h * HEAD_DIM) & (lane < (h + 1) * HEAD_DIM)
        out = jnp.where(sel, jnp.sum(jnp.where(sel, d, 0.0), axis=1, keepdims=True), out)
    return out


def _dil_combine(outs, lses):
    def fn(t, v):
        o0, o1, o2, l0, l1, l2 = t
        m = jnp.maximum(jnp.maximum(l0, l1), l2)
        w0, w1, w2 = jnp.exp(l0 - m), jnp.exp(l1 - m), jnp.exp(l2 - m)
        tot = w0 + w1 + w2
        return [(w0 * o0 + w1 * o1 + w2 * o2) / tot, m + jnp.log(tot)], []
    w = DIL_OUT_W
    return _rowwise(fn, "dil_combine", [(t, w, 0) for t in (*outs, *lses)], [], [(w, f32), (w, f32)])


def _dil_delta(dyb_h, yb_h):
    def fn(t, v):
        return [_head_bcast_sum(t[0] * t[1])], []
    return _rowwise(fn, "dil_delta", [(dyb_h, DIL_OUT_W, 0), (yb_h, DIL_OUT_W, 0)], [], [(DIL_OUT_W, f32)])[0]


def _adamw_math(wt, gt, mt, vt):
    mn = ADAM_B1 * mt + (1.0 - ADAM_B1) * gt
    vn = ADAM_B2 * vt + (1.0 - ADAM_B2) * (gt * gt)
    m_hat = mn / (1.0 - ADAM_B1 ** ADAM_STEP)
    v_hat = vn / (1.0 - ADAM_B2 ** ADAM_STEP)
    return -ADAM_LR * (m_hat / (jnp.sqrt(v_hat) + ADAM_EPS) + ADAM_WD * wt), mn, vn


def _adamw(w, g, m, v, name):
    shape = w.shape
    if w.ndim == 1:
        w, g, m, v = (t.reshape(1, -1) for t in (w, g, m, v))
    rows, cols = w.shape
    tile = 256 if rows % 256 == 0 and rows > 512 else rows

    def fn(t, _):
        return list(_adamw_math(*t)), []
    delta, mn, vn = _rowwise(fn, name, [(w, cols, 0), (g, cols, 0), (m, cols, 0), (v, cols, 0)], [],
                             [(cols, f32)] * 3, tile=tile)
    return delta.reshape(shape), mn.reshape(shape), vn.reshape(shape)


def _adamw_by_planes(w, g, m, v, name, most=128):
    planes, _, width = w.shape
    tile = max(t for t in range(1, most + 1) if planes % t == 0)

    def body(w_ref, g_ref, m_ref, v_ref, d_ref, mn_ref, vn_ref):
        d_ref[...], mn_ref[...], vn_ref[...] = _adamw_math(w_ref[...], g_ref[...], m_ref[...], v_ref[...])

    spec = pl.BlockSpec((tile, 1, width), lambda i: (i, 0, 0))
    return pl.pallas_call(body, grid=(planes // tile,), in_specs=[spec] * 4, out_specs=[spec] * 3,
                          out_shape=[SDS(w.shape, f32)] * 3, name=name,
                          compiler_params=_params(("parallel",)))(w, g, m, v)


def _ada_fwd(c_all, w_shard, b_shard):
    def body(c_ref, w_ref, b_ref, o_ref):
        cv = c_ref[...]
        sc = (cv * _sigmoid(cv)).astype(bf16)
        o_ref[...] = jnp.dot(sc, w_ref[...].astype(bf16), preferred_element_type=f32) + b_ref[...]
    return pl.pallas_call(body, out_shape=SDS((N_DEV, w_shard.shape[1]), f32), name="ada_fwd",
                          compiler_params=_params())(c_all, w_shard, b_shard)


def _ada_bwd(c_all, dmod_cols):
    def body(c_ref, d_ref, o_ref):
        cv = c_ref[...]
        sc = cv * _sigmoid(cv)
        o_ref[...] = lax.dot_general(sc, d_ref[...], (((0,), (0,)), ((), ())), precision=lax.Precision.HIGHEST,
                                     preferred_element_type=f32)
    return pl.pallas_call(body, out_shape=SDS((D, dmod_cols.shape[1]), f32), name="ada_bwd",
                          compiler_params=_params())(c_all, dmod_cols)


def _small_reduce(gathered, after):
    def body(g_ref, after_ref, o_ref, loss_ref):
        acc = g_ref[0]
        for d in range(1, N_DEV):
            acc = acc + g_ref[d]
        o_ref[...] = acc
        loss_ref[...] = jnp.zeros((1, LANES), f32) + jnp.sum(acc[10:11, :])
    return pl.pallas_call(body, out_shape=(SDS((SMALL_ROWS, D), f32), SDS((1, LANES), f32)), name="small_reduce",
                          in_specs=[pl.BlockSpec(memory_space=pltpu.VMEM), pl.BlockSpec(memory_space=pl.ANY)],
                          compiler_params=_params())(gathered, after)


FOX_BLK = 512
CUM_BLK = 128


def _fold_lanes(t, op):
    out = t[:, :LANES]
    for j in range(1, t.shape[1] // LANES):
        out = op(out, t[:, j * LANES:(j + 1) * LANES])
    return out


def _fox_gate_fwd(proj, b_pad):
    nblk = SEQ // CUM_BLK

    def body(f_ref, b_ref, col_ref):
        r = lax.broadcasted_iota(jnp.int32, (CUM_BLK, CUM_BLK), 0)
        c = lax.broadcasted_iota(jnp.int32, (CUM_BLK, CUM_BLK), 1)
        tri = (r >= c).astype(f32)
        carry = jnp.zeros((1, LANES), f32)
        for blk in range(nblk):
            z = f_ref[blk * CUM_BLK:(blk + 1) * CUM_BLK, :] + b_ref[...]
            logf = jnp.minimum(z, 0.0) - jnp.log1p(jnp.exp(-jnp.abs(z)))
            cs = jnp.dot(tri, logf, precision=lax.Precision.HIGHEST, preferred_element_type=f32) + carry
            col_ref[blk * CUM_BLK:(blk + 1) * CUM_BLK, :] = cs
            carry = cs[CUM_BLK - 1:CUM_BLK, :]

    return pl.pallas_call(
        body, grid=(1,), in_specs=[pl.BlockSpec((SEQ, LANES), lambda i: (0, C_F // LANES)),
                                   pl.BlockSpec((1, LANES), lambda i: (0, 0))],
        out_specs=pl.BlockSpec((SEQ, LANES), lambda i: (0, 0)),
        out_shape=SDS((SEQ, LANES), f32), name="fox_gate_fwd",
        compiler_params=_params(("arbitrary",)),
    )(proj, b_pad)


def _fox_gate_bwd(dF_row, proj, b_pad):
    nblk = SEQ // CUM_BLK

    def body(d_ref, f_ref, b_ref, df_ref, db_ref, col_ref):
        r = lax.broadcasted_iota(jnp.int32, (CUM_BLK, CUM_BLK), 0)
        c = lax.broadcasted_iota(jnp.int32, (CUM_BLK, CUM_BLK), 1)
        tri = (r <= c).astype(f32)
        lane = lax.broadcasted_iota(jnp.int32, (CUM_BLK, LANES), 1)
        col_ref[...] = d_ref[...].T
        carry = jnp.zeros((1, LANES), f32)
        total = jnp.zeros((1, LANES), f32)
        for blk in reversed(range(nblk)):
            rows = slice(blk * CUM_BLK, (blk + 1) * CUM_BLK)
            cs = jnp.dot(tri, col_ref[rows, :], precision=lax.Precision.HIGHEST, preferred_element_type=f32) + carry
            carry = cs[0:1, :]
            z = f_ref[rows, :] + b_ref[...]
            df = jnp.where(lane < N_FOX_HEADS, cs * _sigmoid(-z), 0.0)
            df_ref[rows, :] = df.astype(df_ref.dtype)
            total = total + _colsum(df)
        db_ref[...] = total

    return pl.pallas_call(
        body, grid=(1,), in_specs=[pl.BlockSpec((LANES, SEQ), lambda i: (0, 0)),
                                   pl.BlockSpec((SEQ, LANES), lambda i: (0, C_F // LANES)),
                                   pl.BlockSpec((1, LANES), lambda i: (0, 0))],
        out_specs=[pl.BlockSpec((SEQ, LANES), lambda i: (0, 0)), pl.BlockSpec((1, LANES), lambda i: (0, 0))],
        out_shape=(SDS((SEQ, LANES), bf16), SDS((1, LANES), f32)), name="fox_gate_bwd",
        scratch_shapes=[pltpu.VMEM((SEQ, LANES), f32)],
        compiler_params=_params(("arbitrary",)),
    )(dF_row, proj, b_pad)


def _nt(a, b):
    return lax.dot_general(a, b, (((1,), (1,)), ((), ())), preferred_element_type=f32)


def _tn(a, b):
    return lax.dot_general(a, b, (((0,), (0,)), ((), ())), preferred_element_type=f32)


def _fox_prep(proj, f_col):
    def fn(t, v):
        q, k, vv, fc = t
        lane = lax.broadcasted_iota(jnp.int32, (q.shape[0], LANES), 1)
        qs, ks = [], []
        for h in range(N_FOX_HEADS):
            pair, pos = divmod(h, 2)
            own = (lane >= pos * HEAD_DIM) & (lane < (pos + 1) * HEAD_DIM)
            base = (1 - pos) * HEAD_DIM
            f = fc[:, h:h + 1]
            hi = f.astype(bf16).astype(f32)
            mid = (f - hi).astype(bf16).astype(f32)
            lo = (f - hi) - mid
            one = jnp.ones_like(f)
            qa = jnp.where(own, q[:, pair * LANES:(pair + 1) * LANES] * ATT_SCALE, 0.0)
            ka = k[:, pair * LANES:(pair + 1) * LANES]
            for idx, (qv, kv) in enumerate([(hi, one), (mid, one), (lo, one), (one, -hi), (one, -mid), (one, -lo)]):
                sel = lane == base + idx
                qa = jnp.where(sel, qv, qa)
                ka = jnp.where(sel, kv, ka)
            qs.append(qa)
            ks.append(ka)
        return [jnp.concatenate(qs, axis=1), jnp.concatenate(ks, axis=1), vv], []
    w = N_FOX_HEADS * LANES
    return _rowwise(fn, "fox_prep", [(proj, FOX_W, C_QA // FOX_W), (proj, FOX_W, C_KA // FOX_W),
                                     (proj, FOX_W, C_VA // FOX_W), (f_col, LANES, 0)], [],
                    [(w, bf16), (w, bf16), (FOX_W, bf16)])


def _fox_fwd(q_aug, k_aug, v):
    blk = FOX_BLK
    npair = FOX_W // LANES

    def body(q_ref, k_ref, v_ref, o_ref, max_ref, sum_ref, s_scr):
        i = pl.program_id(1)
        tri = lax.broadcasted_iota(jnp.int32, (blk, blk), 0) >= lax.broadcasted_iota(jnp.int32, (blk, blk), 1)
        qh = [q_ref[:, h * LANES:(h + 1) * LANES] for h in range(2)]

        def logits(c, masked):
            off = pl.multiple_of(c * blk, blk)
            tops = []
            for h in range(2):
                s = _nt(qh[h], k_ref[pl.ds(off, blk), h * LANES:(h + 1) * LANES])
                if masked:
                    s = jnp.where(tri, s, NEG)
                s_scr[h, :, pl.ds(off, blk)] = s
                tops.append(_fold_lanes(s, jnp.maximum))
            return tops

        def pass_a(c, m):
            return tuple(jnp.maximum(a, b) for a, b in zip(m, logits(c, False)))

        m = lax.fori_loop(0, i, pass_a, tuple(jnp.full((blk, LANES), NEG, f32) for _ in range(2)))
        mx = [jnp.max(jnp.maximum(a, b), axis=1, keepdims=True) for a, b in zip(m, logits(i, True))]

        def pass_b(c, carry):
            off = pl.multiple_of(c * blk, blk)
            vv = v_ref[pl.ds(off, blk), :]
            new = []
            for h in range(2):
                l, acc = carry[h]
                p = jnp.exp(s_scr[h, :, pl.ds(off, blk)] - mx[h]).astype(bf16)
                new.append((l + _fold_lanes(p.astype(f32), jnp.add), acc + jnp.dot(p, vv, preferred_element_type=f32)))
            return tuple(new)

        zero = jnp.zeros((blk, LANES), f32)
        (l_a, acc_a), (l_b, acc_b) = lax.fori_loop(0, i + 1, pass_b, ((zero, zero), (zero, zero)))
        l_a = jnp.sum(l_a, axis=1, keepdims=True)
        l_b = jnp.sum(l_b, axis=1, keepdims=True)
        first = lax.broadcasted_iota(jnp.int32, (blk, LANES), 1) < HEAD_DIM
        o_ref[...] = jnp.where(first, acc_a / l_a, acc_b / l_b)
        max_ref[0] = jnp.where(first, mx[0], mx[1])
        sum_ref[0] = jnp.where(first, l_a, l_b)

    return pl.pallas_call(
        body, grid=(npair, SEQ // blk),
        in_specs=[pl.BlockSpec((blk, 2 * LANES), lambda p, i: (i, p)),
                  pl.BlockSpec((SEQ, 2 * LANES), lambda p, i: (0, p)),
                  pl.BlockSpec((SEQ, LANES), lambda p, i: (0, p))],
        out_specs=[pl.BlockSpec((blk, LANES), lambda p, i: (i, p))]
        + [pl.BlockSpec((1, blk, LANES), lambda p, i: (p, i, 0))] * 2,
        out_shape=(SDS((SEQ, FOX_W), f32),) + (SDS((npair, SEQ, LANES), f32),) * 2, name="fox_fwd",
        scratch_shapes=[pltpu.VMEM((2, blk, SEQ), f32)],
        compiler_params=_params(("parallel", "arbitrary")),
    )(q_aug, k_aug, v)


def _fox_bwd(q_aug, k_aug, v, do, o, row_max, row_sum, after):
    blk = FOX_BLK
    npair = FOX_W // LANES
    nblk = SEQ // blk

    def body(q_ref, k_ref, v_ref, do_ref, o_ref, max_ref, sum_ref, after_ref, dq_ref, dk_ref, dv_ref, df_ref, dq_acc,
             delta_ref, inv_ref):
        inv_ref[...] = 1.0 / sum_ref[0]
        lane_s = lax.broadcasted_iota(jnp.int32, (SEQ, LANES), 1)
        prod = do_ref[...].astype(bf16).astype(f32) * o_ref[...]
        d_a = jnp.sum(jnp.where(lane_s < HEAD_DIM, prod, 0.0), axis=1, keepdims=True)
        d_b = jnp.sum(jnp.where(lane_s >= HEAD_DIM, prod, 0.0), axis=1, keepdims=True)
        delta_ref[...] = jnp.where(lane_s < HEAD_DIM, d_a, d_b)
        dq_acc[...] = jnp.zeros_like(dq_acc)
        df_ref[...] = jnp.zeros_like(df_ref)
        lane = lax.broadcasted_iota(jnp.int32, (blk, LANES), 1)
        own = [lane < HEAD_DIM, lane >= HEAD_DIM]
        tri = lax.broadcasted_iota(jnp.int32, (blk, blk), 0) >= lax.broadcasted_iota(jnp.int32, (blk, blk), 1)

        def q_slab(qoff, h):
            return q_ref[pl.ds(qoff, blk), h * LANES:(h + 1) * LANES]

        def probs(qoff, h, k_h, masked):
            s = _nt(q_slab(qoff, h), k_h)
            if masked:
                s = jnp.where(tri, s, NEG)
            col = slice(h * HEAD_DIM, h * HEAD_DIM + 1)
            weights = jnp.exp(s - max_ref[0, pl.ds(qoff, blk), col]).astype(bf16).astype(f32)
            return weights * inv_ref[pl.ds(qoff, blk), col]

        def k_slabs(koff):
            return [k_ref[pl.ds(koff, blk), h * LANES:(h + 1) * LANES] for h in range(2)]

        def kv_step(kj, _):
            koff = pl.multiple_of(kj * blk, blk)
            k_aug = k_slabs(koff)
            k_own = [jnp.where(own[h], k_aug[h], jnp.zeros_like(k_aug[h])) for h in range(2)]
            vv = v_ref[pl.ds(koff, blk), :]
            v_own = [jnp.where(own[h], vv, jnp.zeros_like(vv)) for h in range(2)]

            def q_tile(qi, carry, masked):
                qoff = pl.multiple_of(qi * blk, blk)
                dd = do_ref[pl.ds(qoff, blk), :].astype(bf16)
                new, dq_add = [], None
                for h in range(2):
                    dk_h, dv_h, dcol = carry[h]
                    p = probs(qoff, h, k_aug[h], masked)
                    dl = p * (_nt(dd, v_own[h]) - delta_ref[pl.ds(qoff, blk), h * HEAD_DIM:h * HEAD_DIM + 1])
                    dlb = dl.astype(bf16)
                    part = jnp.dot(dlb, k_own[h], preferred_element_type=f32)
                    dq_add = part if dq_add is None else dq_add + part
                    new.append((dk_h + _tn(dlb, q_slab(qoff, h)), dv_h + _tn(p.astype(bf16), dd),
                                dcol + _colsum(dl)))
                dq_acc[pl.ds(qoff, blk), :] += dq_add * ATT_SCALE
                return tuple(new)

            zero = (jnp.zeros((blk, LANES), f32), jnp.zeros((blk, LANES), f32), jnp.zeros((1, blk), f32))
            carry = q_tile(kj, (zero, zero), True)
            (dk_a, dv_a, dcol_a), (dk_b, dv_b, dcol_b) = lax.fori_loop(
                kj + 1, nblk, lambda qi, cr: q_tile(qi, cr, False), carry)
            dk_ref[pl.ds(koff, blk), :] = jnp.where(own[0], dk_a, dk_b).astype(dk_ref.dtype)
            dv_ref[pl.ds(koff, blk), :] = jnp.where(own[0], dv_a, dv_b).astype(dv_ref.dtype)
            df_ref[0, 0:1, pl.ds(koff, blk)] = -dcol_a
            df_ref[0, 1:2, pl.ds(koff, blk)] = -dcol_b
            return 0

        lax.fori_loop(0, nblk, kv_step, 0)
        dq_ref[...] = dq_acc[...].astype(dq_ref.dtype)

    pair_aug = pl.BlockSpec((SEQ, 2 * LANES), lambda p: (0, p))
    slab = pl.BlockSpec((SEQ, LANES), lambda p: (0, p))
    per_pair = pl.BlockSpec((1, SEQ, LANES), lambda p: (p, 0, 0))
    rows = pl.BlockSpec((1, 8, SEQ), lambda p: (p, 0, 0))
    return pl.pallas_call(
        body, grid=(npair,),
        in_specs=[pair_aug, pair_aug, slab, slab, slab, per_pair, per_pair, pl.BlockSpec(memory_space=pl.ANY)],
        out_specs=[slab, slab, slab, rows],
        out_shape=(SDS((SEQ, FOX_W), bf16),) * 3 + (SDS((npair, 8, SEQ), f32),), name="fox_bwd",
        scratch_shapes=[pltpu.VMEM((SEQ, LANES), f32)] * 3,
        compiler_params=_params(("parallel",)),
    )(q_aug, k_aug, v, do, o, row_max, row_sum, after)


DIL_BLK = 128
DILATIONS = (1, 4, 16)
N_GROUPS = len(DILATIONS)
DIL_PAIRS = DIL_OUT_W // LANES


def _dil_blocks(d):
    r1 = lax.broadcasted_iota(jnp.int32, (2 * DIL_BLK, DIL_BLK), 0) & (DIL_BLK - 1)
    c1 = lax.broadcasted_iota(jnp.int32, (2 * DIL_BLK, DIL_BLK), 1)
    r2 = lax.broadcasted_iota(jnp.int32, (2 * DIL_BLK, 2 * DIL_BLK), 0) & (DIL_BLK - 1)
    c2 = lax.broadcasted_iota(jnp.int32, (2 * DIL_BLK, 2 * DIL_BLK), 1)
    band = ((c2 < DIL_BLK) & (c2 >= r2)) | ((c2 >= DIL_BLK) & (c2 - DIL_BLK <= r2))
    out = []
    for r in range(d):
        for b in range(SEQ // d // DIL_BLK):
            rows = pl.ds(r + d * DIL_BLK * b, DIL_BLK, stride=d)
            if b == 0:
                out.append((rows, rows, r1 >= c1))
            else:
                out.append((rows, pl.ds(r + d * DIL_BLK * (b - 1), 2 * DIL_BLK, stride=d), band))
    return out


def _dil_v_spec(g):
    return pl.BlockSpec((SEQ, LANES), lambda p: (0, C_VB // LANES + DIL_PAIRS * g + p))


def _stack_heads(t, first):
    zero = jnp.zeros_like(t)
    return jnp.concatenate([jnp.where(first, t, zero), jnp.where(first, zero, t)], axis=0)


def _dil_fwd(q, k, v, g):
    def body(q_ref, k_ref, v_ref, o_ref, lse_ref):
        first = lax.broadcasted_iota(jnp.int32, (DIL_BLK, LANES), 1) < HEAD_DIM
        for rows, krows, mask in _dil_blocks(DILATIONS[g]):
            qv, kk, vv = q_ref[rows, :].astype(bf16), k_ref[krows, :].astype(bf16), v_ref[krows, :].astype(bf16)
            s = jnp.where(mask, _nt(_stack_heads(qv, first), kk), NEG)
            m = jnp.max(s, axis=1, keepdims=True)
            p = jnp.exp(s - m)
            l = jnp.sum(p, axis=1, keepdims=True)
            out = jnp.dot(p.astype(bf16), vv, preferred_element_type=f32) / l
            lse = m + jnp.log(l)
            o_ref[rows, :] = jnp.where(first, out[:DIL_BLK], out[DIL_BLK:])
            lse_ref[rows, :] = jnp.where(first, lse[:DIL_BLK], lse[DIL_BLK:])

    grouped = pl.BlockSpec((SEQ, LANES), lambda p: (0, DIL_PAIRS * g + p))
    own = pl.BlockSpec((SEQ, LANES), lambda p: (0, p))
    shape = SDS((SEQ, DIL_OUT_W), f32)
    return pl.pallas_call(
        body, grid=(DIL_PAIRS,), in_specs=[grouped, grouped, _dil_v_spec(g)], out_specs=[own] * 2,
        out_shape=(shape, shape),
        name=f"dil_fwd_{DILATIONS[g]}", compiler_params=_params(("parallel",)),
    )(q, k, v)


def _dil_bwd(q, k, v, do, lse, delta, g):
    def body(q_ref, k_ref, v_ref, do_ref, lse_ref, dl_ref, dq_ref, dk_ref, dv_ref):
        first = lax.broadcasted_iota(jnp.int32, (DIL_BLK, LANES), 1) < HEAD_DIM
        dk_ref[...] = jnp.zeros_like(dk_ref)
        dv_ref[...] = jnp.zeros_like(dv_ref)
        for rows, krows, mask in _dil_blocks(DILATIONS[g]):
            qv, kk, vv = q_ref[rows, :].astype(bf16), k_ref[krows, :].astype(bf16), v_ref[krows, :].astype(bf16)
            lsev, delv = lse_ref[rows, :], dl_ref[rows, :]
            q2 = _stack_heads(qv, first)
            do2 = _stack_heads(do_ref[rows, :].astype(bf16), first)
            per_head = lambda t: jnp.concatenate([t[:, 0:1], t[:, HEAD_DIM:HEAD_DIM + 1]], axis=0)
            p = jnp.exp(jnp.where(mask, _nt(q2, kk), NEG) - per_head(lsev))
            dl = (p * (_nt(do2, vv) - per_head(delv))).astype(bf16)
            dq = jnp.dot(dl, kk, preferred_element_type=f32)
            dq_ref[rows, :] = jnp.where(first, dq[:DIL_BLK], dq[DIL_BLK:]) * ATT_SCALE
            dk_ref[krows, :] += _tn(dl, q2)
            dv_ref[krows, :] += _tn(p.astype(bf16), do2)

    grouped = pl.BlockSpec((SEQ, LANES), lambda p: (0, DIL_PAIRS * g + p))
    own = pl.BlockSpec((SEQ, LANES), lambda p: (0, p))
    shape = SDS((SEQ, DIL_OUT_W), f32)
    return pl.pallas_call(
        body, grid=(DIL_PAIRS,), in_specs=[grouped, grouped, _dil_v_spec(g)] + [own] * 3, out_specs=[own] * 3,
        out_shape=(shape, shape, shape), name=f"dil_bwd_{DILATIONS[g]}", compiler_params=_params(("parallel",)),
    )(q, k, v, do, lse, delta)


def _position():
    return lax.axis_index("x"), lax.axis_index("y"), lax.axis_index("c")


def _all_gather(block, name):
    def body(x_ref, out_ref, send_sems, recv_sems, local_sem):
        x, y, c = _position()
        me, sibling = (x, y, c), (x, y, 1 - c)
        chips = [(1 - x, y), (x, 1 - y), (1 - x, 1 - y)]

        def slot(px, py, pc):
            return out_ref.at[4 * px + 2 * py + pc]

        def copy(k, blk, to, src=None):
            return pltpu.make_async_remote_copy(
                src_ref=slot(*blk) if src is None else src, dst_ref=slot(*blk),
                send_sem=send_sems.at[k], recv_sem=recv_sems.at[k], device_id=to, device_id_type=MESH)

        mine = pltpu.make_async_copy(x_ref, slot(*me), local_sem)
        mine.start()
        first = [copy(0, me, sibling, src=x_ref)]
        first += [copy(1 + j, me, (*chip, c), src=x_ref) for j, chip in enumerate(chips)]
        for cp in first:
            cp.start()
        passed = [copy(4 + j, (*chip, c), sibling) for j, chip in enumerate(chips)]
        for j, chip in enumerate(chips):
            copy(1 + j, (*chip, c), me).wait_recv()
            passed[j].start()
        copy(0, sibling, me).wait_recv()
        for j, chip in enumerate(chips):
            copy(4 + j, (*chip, 1 - c), me).wait_recv()
        for cp in first + passed:
            cp.wait_send()
        mine.wait()

    return pl.pallas_call(
        body, out_shape=SDS((N_DEV,) + block.shape, block.dtype),
        in_specs=[pl.BlockSpec(memory_space=pl.ANY)], out_specs=pl.BlockSpec(memory_space=pl.ANY),
        scratch_shapes=[pltpu.SemaphoreType.DMA((7,)), pltpu.SemaphoreType.DMA((7,)), pltpu.SemaphoreType.DMA],
        name=name,
    )(block)


HBM_SPEC = pl.BlockSpec(memory_space=pltpu.HBM)
SEM_SPEC = pl.BlockSpec(memory_space=pltpu.SEMAPHORE)
SPLIT_COPY = pltpu.CompilerParams(has_side_effects=pltpu.SideEffectType.DATAFLOW_SIDE_EFFECTING)


def _in_hbm(t):
    return pltpu.with_memory_space_constraint(t, pltpu.HBM)


def _pair_copies(g_refs, land_refs, send_sems, recv_sems):
    x, y, c = _position()
    return [pltpu.make_async_remote_copy(
        src_ref=g.at[2 * k + (1 - c)], dst_ref=land.at[k], send_sem=send_sems.at[4 * a + k],
        recv_sem=recv_sems.at[4 * a + k], device_id=(x, y, 1 - c), device_id_type=MESH)
        for a, (g, land) in enumerate(zip(g_refs, land_refs, strict=True)) for k in range(4)]


def _chip_copies(t_refs, land_refs, send_sems, recv_sems):
    x, y, c = _position()
    chips = [(1 - x, y), (x, 1 - y), (1 - x, 1 - y)]
    return [pltpu.make_async_remote_copy(
        src_ref=t.at[2 * px + py], dst_ref=land.at[j], send_sem=send_sems.at[3 * a + j],
        recv_sem=recv_sems.at[3 * a + j], device_id=(px, py, c), device_id_type=MESH)
        for a, (t, land) in enumerate(zip(t_refs, land_refs, strict=True)) for j, (px, py) in enumerate(chips)]


_ROUNDS = {"pair": (_pair_copies, 4), "chip": (_chip_copies, 3)}


def _exchange_start(kind, ts, name):
    copies, slots = _ROUNDS[kind]
    n = len(ts)
    lands = [_in_hbm(lax.empty((slots,) + t.shape[1:], t.dtype)) for t in ts]

    def body(*refs):
        for cp in copies(refs[:n], refs[n:2 * n], refs[2 * n], refs[2 * n + 1]):
            cp.start()
        refs[-1][...] = jnp.zeros_like(refs[-1])

    sems = pltpu.SemaphoreType.DMA((slots * n,))
    res = pl.pallas_call(
        body, name=name, in_specs=[HBM_SPEC] * (2 * n),
        out_shape=(sems, sems, *[pltpu.HBM(t.shape, t.dtype) for t in (*ts, *lands)], SDS((8, LANES), f32)),
        out_specs=(SEM_SPEC, SEM_SPEC, *[HBM_SPEC] * (2 * n), pl.BlockSpec(memory_space=pltpu.VMEM)),
        input_output_aliases={i: 2 + i for i in range(2 * n)}, compiler_params=SPLIT_COPY,
    )(*[_in_hbm(t) for t in ts], *lands)
    return res[:-1], res[-1]


def _exchange_wait(kind, state, after, name):
    copies, _ = _ROUNDS[kind]
    send_sems, recv_sems, *arrays = state
    n = len(arrays) // 2

    def body(*refs):
        for cp in copies(refs[:n], refs[n:2 * n], refs[2 * n], refs[2 * n + 1]):
            cp.wait_send()
            cp.wait_recv()

    res = pl.pallas_call(
        body, name=name, in_specs=[HBM_SPEC] * (2 * n) + [SEM_SPEC, SEM_SPEC, pl.BlockSpec(memory_space=pl.ANY)],
        out_shape=[pltpu.HBM(t.shape, t.dtype) for t in arrays], out_specs=[HBM_SPEC] * (2 * n),
        input_output_aliases={i: i for i in range(2 * n)}, compiler_params=SPLIT_COPY,
    )(*arrays, send_sems, recv_sems, after)
    return res[:n], res[n:]


def _gather_copies(x_refs, out_refs, send_sems, recv_sems):
    x, y, c = _position()
    peers = [(x, y, 1 - c), (1 - x, y, c), (x, 1 - y, c), (1 - x, 1 - y, c)]
    sends, arrivals = [], []
    for a, (x_ref, out_ref) in enumerate(zip(x_refs, out_refs, strict=True)):
        for k, (px, py, pc) in enumerate(peers):
            sems = dict(send_sem=send_sems.at[4 * a + k], recv_sem=recv_sems.at[4 * a + k],
                        device_id=(px, py, pc), device_id_type=MESH)
            sends.append(pltpu.make_async_remote_copy(src_ref=x_ref, dst_ref=out_ref.at[4 * x + 2 * y + c], **sems))
            arrivals.append(pltpu.make_async_remote_copy(src_ref=x_ref, dst_ref=out_ref.at[4 * px + 2 * py + pc],
                                                         **sems))
    return sends, arrivals


def _gather_start(blocks, after, name):
    n = len(blocks)
    outs = [_in_hbm(lax.empty((N_DEV,) + b.shape, b.dtype)) for b in blocks]

    def body(*refs):
        sends, _ = _gather_copies(refs[:n], refs[n:2 * n], refs[2 * n + 1], refs[2 * n + 2])
        for cp in sends:
            cp.start()
        refs[-1][...] = jnp.zeros_like(refs[-1])

    sems = pltpu.SemaphoreType.DMA((4 * n,))
    res = pl.pallas_call(
        body, name=name, in_specs=[HBM_SPEC] * (2 * n) + [pl.BlockSpec(memory_space=pl.ANY)],
        out_shape=(sems, sems, *[pltpu.HBM(t.shape, t.dtype) for t in (*blocks, *outs)], SDS((8, LANES), f32)),
        out_specs=(SEM_SPEC, SEM_SPEC, *[HBM_SPEC] * (2 * n), pl.BlockSpec(memory_space=pltpu.VMEM)),
        input_output_aliases={i: 2 + i for i in range(2 * n)}, compiler_params=SPLIT_COPY,
    )(*[_in_hbm(b) for b in blocks], *outs, after)
    return res[:-1], res[-1]


def _gather_wait(state, after, name):
    send_sems, recv_sems, *arrays = state
    n = len(arrays) // 2

    def body(*refs):
        sends, arrivals = _gather_copies(refs[:n], refs[n:2 * n], refs[2 * n], refs[2 * n + 1])
        for cp in sends:
            cp.wait_send()
        for cp in arrivals:
            cp.wait_recv()

    res = pl.pallas_call(
        body, name=name, in_specs=[HBM_SPEC] * (2 * n) + [SEM_SPEC, SEM_SPEC, pl.BlockSpec(memory_space=pl.ANY)],
        out_shape=[pltpu.HBM(t.shape, t.dtype) for t in arrays], out_specs=[HBM_SPEC] * (2 * n),
        input_output_aliases={i: i for i in range(2 * n)}, compiler_params=SPLIT_COPY,
    )(*arrays, send_sems, recv_sems, after)
    return res[:n], res[n:]


def _gather_finish(partial, name):
    n = len(partial)

    def body(*refs):
        in_refs, out_refs = refs[:n], refs[n:2 * n]
        send_sems, recv_sems = refs[2 * n:]
        x, y, c = _position()
        chips = [(1 - x, y), (x, 1 - y), (1 - x, 1 - y)]
        copies = []
        for a in range(n):
            for j, (px, py) in enumerate(chips):
                cp = pltpu.make_async_remote_copy(
                    src_ref=in_refs[a].at[4 * px + 2 * py + c], dst_ref=out_refs[a].at[4 * px + 2 * py + c],
                    send_sem=send_sems.at[a, j], recv_sem=recv_sems.at[a, j], device_id=(x, y, 1 - c),
                    device_id_type=MESH)
                cp.start()
                copies.append(cp)
        for a in range(n):
            for j, (px, py) in enumerate(chips):
                pltpu.make_async_remote_copy(
                    src_ref=in_refs[a].at[4 * px + 2 * py + (1 - c)], dst_ref=out_refs[a].at[4 * px + 2 * py + (1 - c)],
                    send_sem=send_sems.at[a, j], recv_sem=recv_sems.at[a, j], device_id=(x, y, 1 - c),
                    device_id_type=MESH).wait_recv()
        for cp in copies:
            cp.wait_send()

    hbm = pl.BlockSpec(memory_space=pl.ANY)
    return pl.pallas_call(
        body, out_shape=[SDS(p.shape, p.dtype) for p in partial], in_specs=[hbm] * n, out_specs=[hbm] * n,
        input_output_aliases={a: a for a in range(n)},
        scratch_shapes=[pltpu.SemaphoreType.DMA((n, 3)), pltpu.SemaphoreType.DMA((n, 3))],
        name=name,
    )(*partial)


def _row_tile(rows):
    return 512 if rows % 512 == 0 and rows > 512 else rows


def _pair_add(g, r1, core, name):
    def body(c_ref, g_ref, r_ref, o_ref):
        o_ref[...] = (g_ref[...].astype(f32) + r_ref[...].astype(f32)).astype(o_ref.dtype)

    rows, cols = g.shape[1:]
    tile = _row_tile(rows)
    blk = (1, tile, cols)
    return pl.pallas_call(
        body, out_shape=SDS((4, rows, cols), g.dtype), name=name,
        grid_spec=pltpu.PrefetchScalarGridSpec(
            num_scalar_prefetch=1, grid=(4, rows // tile),
            in_specs=[pl.BlockSpec(blk, lambda k, i, c_ref: (2 * k + c_ref[0], i, 0)),
                      pl.BlockSpec(blk, lambda k, i, c_ref: (k, i, 0))],
            out_specs=pl.BlockSpec(blk, lambda k, i, c_ref: (k, i, 0))),
        compiler_params=_params(("parallel", "arbitrary")),
    )(core, g, r1)


def _chip_add(t, r2, chip, name, transposed=False):
    def body(c_ref, t_ref, r_ref, o_ref):
        s = ((t_ref[0].astype(f32) + r_ref[0].astype(f32)) + r_ref[1].astype(f32)) + r_ref[2].astype(f32)
        o_ref[...] = s.T if transposed else s

    rows, cols = t.shape[1:]
    tile = _row_tile(rows)
    out_spec = pl.BlockSpec((cols, tile), lambda i, c_ref: (0, i)) if transposed else pl.BlockSpec(
        (tile, cols), lambda i, c_ref: (i, 0))
    return pl.pallas_call(
        body, out_shape=SDS((cols, rows) if transposed else (rows, cols), f32), name=name,
        grid_spec=pltpu.PrefetchScalarGridSpec(
            num_scalar_prefetch=1, grid=(rows // tile,),
            in_specs=[pl.BlockSpec((1, tile, cols), lambda i, c_ref: (c_ref[0], i, 0)),
                      pl.BlockSpec((3, tile, cols), lambda i, c_ref: (0, i, 0))],
            out_specs=out_spec),
        compiler_params=_params(("arbitrary",)),
    )(chip, t, r2)


def _pad_to(t, axis, size):
    pads = [(0, 0)] * t.ndim
    pads[axis] = (0, size - t.shape[axis])
    return jnp.pad(t, pads)


_REF_COLS = {"qa": (0, FOX_W), "ka": (FOX_W, FOX_W), "va": (2 * FOX_W, FOX_W), "f": (3 * FOX_W, N_FOX_HEADS)}
_REF_COLS.update({n: (3 * FOX_W + N_FOX_HEADS + i * DIL_W, DIL_W) for i, n in enumerate(("qb", "kb", "vb"))})
_REF_COLS.update({n: (3 * FOX_W + N_FOX_HEADS + 3 * DIL_W + i * D, D) for i, n in enumerate(("ga", "gb"))})
_REF_ORDER = ("qa", "ka", "va", "f", "qb", "kb", "vb", "ga", "gb")


def _place_cols(sources, src_of, out_cols, name, row_block=512):
    arrays = [s[0] if isinstance(s, tuple) else s for s in sources]
    widths = [a.shape[-1] for a in arrays]
    rows = arrays[0].shape[-2]
    plan = []
    for t in range(out_cols // LANES):
        segs, c, end = [], t * LANES, (t + 1) * LANES
        while c < end:
            s = src_of(c)
            if s is None:
                c += 1
                continue
            n = 1
            while c + n < end and src_of(c + n) == (s[0], s[1] + n):
                n += 1
            segs.append((s[0], s[1], c - t * LANES, n))
            c += n
        plan.append(segs)

    def body(*refs):
        o_ref = refs[-1]
        for t, segs in enumerate(plan):
            acc = None
            for si, c0, o0, n in segs:
                a0 = c0 // LANES * LANES
                wide = min(2 * LANES, widths[si] - a0)
                win = refs[si][0, :, a0:a0 + wide] if isinstance(sources[si], tuple) else refs[si][:, a0:a0 + wide]
                r = lax.broadcasted_iota(jnp.int32, (wide, LANES), 0)
                c = lax.broadcasted_iota(jnp.int32, (wide, LANES), 1)
                pick = ((r - (c0 - a0) == c - o0) & (c >= o0) & (c < o0 + n)).astype(bf16)
                part = jnp.dot(win.astype(bf16), pick, preferred_element_type=f32)
                acc = part if acc is None else acc + part
            tile = jnp.zeros((row_block, LANES), f32) if acc is None else acc
            o_ref[:, t * LANES:(t + 1) * LANES] = tile.astype(o_ref.dtype)

    def spec(s):
        if isinstance(s, tuple):
            j = s[1]
            return pl.BlockSpec((1, row_block, s[0].shape[-1]), lambda i: (j, i, 0))
        return pl.BlockSpec((row_block, s.shape[-1]), lambda i: (i, 0))

    return pl.pallas_call(
        body, grid=(rows // row_block,), in_specs=[spec(s) for s in sources],
        out_specs=pl.BlockSpec((row_block, out_cols), lambda i: (i, 0)), out_shape=SDS((rows, out_cols), bf16),
        name=name, compiler_params=_params(("parallel",)),
    )(*arrays)


def _ref_piece(r):
    for name in _REF_ORDER:
        lo, width = _REF_COLS[name]
        if lo <= r < lo + width:
            return name, r - lo
    raise ValueError(r)


def _shard_pad_cols(pieces):
    names = [n for n in _REF_ORDER if n != "vb"]
    sources = [pieces[n] for n in names] + list(pieces["vb"])

    def src_of(c):
        j, i = divmod(c, W_IN_PAD)
        if i >= W_IN_SH:
            return None
        name, col = _ref_piece(j * W_IN_SH + i)
        if name == "vb":
            return len(names) + col // DIL_OUT_W, col % DIL_OUT_W
        return names.index(name), col

    return _place_cols(sources, src_of, N_DEV * W_IN_PAD, "place_dproj")


_SLABS = {"ga": C_GA, "gb": C_GB, "qb": C_QB, "kb": C_KB, "vb": C_VB, "qa": C_QA, "ka": C_KA, "va": C_VA, "f": C_F}


def _slab_w_in(stack):
    def src_of(c):
        for name, start in _SLABS.items():
            lo, width = _REF_COLS[name]
            if start <= c < start + width:
                return divmod(lo + c - start, W_IN_SH)
        return None

    return _place_cols([(stack, j) for j in range(N_DEV)], src_of, PROJ_W, "place_w_in")


def kernel(x, c, w_ada, b_ada, g_mix, w_in, b_fgate, w_br_a, w_br_b, w_out, g_ffn, w_ffn_gate, w_ffn_up, w_ffn_down, g_final, loss_target, m_w_ada, m_b_ada, m_g_mix, m_w_in, m_b_fgate, m_w_br_a, m_w_br_b, m_w_out, m_g_ffn, m_w_ffn_gate, m_w_ffn_up, m_w_ffn_down, m_g_final, v_w_ada, v_b_ada, v_g_mix, v_w_in, v_b_fgate, v_w_br_a, v_w_br_b, v_w_out, v_g_ffn, v_w_ffn_gate, v_w_ffn_up, v_w_ffn_down, v_g_final):
    px, py, pc = _position()
    dev = 4 * px + 2 * py + pc
    x2d, tgt = x[0], loss_target[0]

    c_all = _all_gather(c, "gather_c").reshape(N_DEV, D)
    ada_cols = w_ada.shape[2]
    b_shard = lax.dynamic_slice(b_ada, (0, dev * ada_cols), (1, ada_cols))
    mod_shard = _ada_fwd(c_all, w_ada[0], b_shard)
    mod_all = _all_gather(mod_shard, "gather_mod")
    modv = lax.dynamic_index_in_dim(mod_all, dev, axis=1, keepdims=False).reshape(6, D)
    h1 = _pre1(x2d, modv, g_mix)

    w_in_s = _all_gather(_pad_to(w_in[0], 1, W_IN_PAD).astype(bf16), "gather_w_in")
    gate_up = jnp.concatenate([_pad_to(w_ffn_gate[0], 1, FF_PAD), _pad_to(w_ffn_up[0], 1, FF_PAD)], axis=1)
    later = [w_br_a[0], w_br_b[0], w_out[0], gate_up, _pad_to(w_ffn_down[0], 0, FF_PAD)]
    later_state, later_token = _gather_start([t.astype(bf16) for t in later], w_in_s, "gather_rest_start")
    w_in_p = _slab_w_in(w_in_s)

    proj = _matmul(h1, w_in_p, name="mm_proj", tm=SEQ, tn=896, after=later_token)
    b_pad = jnp.pad(b_fgate, ((0, 0), (0, LANES - N_FOX_HEADS)))
    q_aug, k_aug, va = _fox_prep(proj, _fox_gate_fwd(proj, b_pad))
    ya_h, max_a, sum_a = _fox_fwd(q_aug, k_aug, va)

    tables = _rope_tables()
    qb_r, kb_r = _rope_fwd(proj, tables)
    by_group = [_dil_fwd(qb_r, kb_r, proj, grp) for grp in range(N_GROUPS)]
    yb_h, lse_b = _dil_combine([o for o, _ in by_group], [l for _, l in by_group])

    both_done = ya_h[:8, :LANES] + yb_h[:8, :LANES]
    mine, arrived = _gather_wait(later_state, both_done, "gather_rest_wait")
    w_a_s, w_b_s, w_o_s, w_gu_s, w_d_s = [
        lax.dynamic_update_slice(stack, block[None], (dev, 0, 0))
        for stack, block in zip(_gather_finish(arrived, "gather_rest_finish"), mine, strict=True)]
    w_o = w_o_s.reshape(D, D)
    w_d = w_d_s.reshape(FF_HID, D)
    ya = _matmul_stack(ya_h, w_a_s, name="mm_br_a")
    yb = _matmul_stack(yb_h, w_b_s, name="mm_br_b")

    merged, mix, x1, h2 = _post1(ya, yb, proj, w_o, x2d, modv, g_ffn)
    act, au = _ffn_in(h2, w_gu_s)

    dx2, dff, dg_final, dga_f, loss_lanes = _final(act, w_d, x1, tgt, modv, g_final.reshape(1, D))
    dau = _ffn_bwd_in(dff, w_d_s, au)

    core = pc.astype(jnp.int32).reshape(1)
    chip = (2 * px + py).astype(jnp.int32).reshape(1)

    def pair_done(state, after, tags, name):
        mine, theirs = _exchange_wait("pair", state, after, "pair_wait_" + name)
        sums = [_pair_add(g, r, core, "pair_add_" + t) for g, r, t in zip(mine, theirs, tags)]
        return _exchange_start("chip", sums, "chip_start_" + name)

    def from_chips(state, after, tags, name, transposed=None):
        sums, got = _exchange_wait("chip", state, after, "chip_wait_" + name)
        flips = transposed or [False] * len(tags)
        return [_chip_add(p, r, chip, "chip_add_" + t, f) for p, r, t, f in zip(sums, got, tags, flips)]

    g_gu = _matmul(h2, dau, ta=True, by_shard=True, out_dtype=bf16, name="mm_g_ffn_in", tm=D, tn=2 * FF_PAD)
    g_d = _matmul(act, dff, ta=True, out_dtype=bf16, name="mm_g_down", tm=FF_HID // 2, tn=512)
    ffn_tags = ["gu", "down"]
    ffn_pair, ffn_pair_token = _exchange_start("pair", [g_gu, g_d.reshape(N_DEV, FF_PAD, D)], "pair_start_ffn")

    dx1, dmix, dsh_f, dsc_f, dg_ffn, dga_m = _mid_bwd(dau, w_gu_s, ffn_pair_token, x1, dx2, mix, modv, g_ffn)
    ffn_state, ffn_token = pair_done(ffn_pair, dx1, ffn_tags, "ffn")
    dya, dyb, dga, dgb = _merge_bwd(dmix, w_o, ffn_token, ya, yb, proj)
    dya_h = _matmul_stack(dya, w_a_s, tb=True, name="mm_d_ya")
    dyb_h = _matmul_stack(dyb, w_b_s, tb=True, name="mm_d_yb")

    g_o = _matmul(merged, dmix, ta=True, out_dtype=bf16, name="mm_g_out", tm=D, tn=512)
    g_a = _matmul_stack(ya_h, dya, ta=True, out_dtype=bf16, name="mm_g_br_a")
    g_b = _matmul_stack(yb_h, dyb, ta=True, out_dtype=bf16, name="mm_g_br_b")
    rows_a, rows_b = FOX_W * W_BR_SH // D, DIL_OUT_W * W_BR_SH // D
    g_small = jnp.concatenate([g_a.reshape(N_DEV, rows_a, D), g_b.reshape(N_DEV, rows_b, D),
                               g_o.reshape(N_DEV, W_BR_SH, D)], axis=1)
    small_pair, small_pair_token = _exchange_start("pair", [g_small], "pair_start_small")

    dqa, dka, dva, dF = _fox_bwd(q_aug, k_aug, va, dya_h, ya_h, max_a, sum_a, small_pair_token)
    dF_row = jnp.pad(dF[:, :2, :].reshape(N_FOX_HEADS, SEQ), ((0, LANES - N_FOX_HEADS), (0, 0)))
    df, db_fgate = _fox_gate_bwd(dF_row, proj, b_pad)
    small_state, small_token = pair_done(small_pair, df, ["small"], "small")

    delta_b = _dil_delta(dyb_h, yb_h)
    dil_grads = [_dil_bwd(qb_r, kb_r, proj, dyb_h, lse_b, delta_b, grp) for grp in range(N_GROUPS)]
    dqb, dkb = _rope_bwd([t[0] for t in dil_grads], [t[1] for t in dil_grads], tables)

    dproj = _shard_pad_cols({"qa": dqa, "ka": dka, "va": dva, "f": df, "qb": dqb, "kb": dkb,
                             "vb": [t[2] for t in dil_grads], "ga": dga, "gb": dgb})
    g_in = _matmul(h1, dproj, ta=True, by_shard=True, out_dtype=bf16, name="mm_g_in", tm=D, tn=W_IN_PAD,
                   after=small_token)
    mix_tags = ["in"]
    mix_pair, mix_pair_token = _exchange_start("pair", [g_in], "pair_start_mixer")

    grad_x, dsh_m, dsc_m, dg_mix = _first_bwd(dproj, w_in_s, mix_pair_token, x2d, dx1, modv, g_mix)

    pad_lane = lambda t: jnp.pad(t, ((0, 0), (0, D - t.shape[1])))
    small = jnp.concatenate([dsh_m, dsc_m, dga_m, dsh_f, dsc_f, dga_f, dg_mix, dg_ffn, dg_final,
                             pad_lane(db_fgate), loss_lanes, jnp.zeros((SMALL_ROWS - 11, D), f32)], axis=0)
    small_all = _all_gather(small, "gather_small")
    mix_state, mix_token = pair_done(mix_pair, small_all, mix_tags, "mixer")

    small_sum, loss_row = _small_reduce(small_all, mix_token)
    dmod_all = small_all[:, :6, :].reshape(N_DEV, 6 * D)
    g_w_ada = _ada_bwd(c_all, lax.dynamic_slice(dmod_all, (0, dev * ada_cols), (N_DEV, ada_cols)))
    s_gu_t, s_d = from_chips(ffn_state, small_sum, ffn_tags, "ffn", [True, False])
    s_small, = from_chips(small_state, small_sum, ["small"], "small")

    loss = loss_row[0, 0]
    g = {
        "w_ada": g_w_ada[None], "b_ada": small_sum[0:6].reshape(1, 6 * D), "g_mix": small_sum[6:7],
        "b_fgate": small_sum[9:10, :N_FOX_HEADS], "g_ffn": small_sum[7:8], "w_ffn_gate": s_gu_t[:W_FF_SH],
        "w_ffn_up": s_gu_t[FF_PAD:FF_PAD + W_FF_SH], "w_ffn_down": s_d[None, :W_FF_SH],
        "g_final": small_sum[8], "w_br_a": s_small[:rows_a].reshape(1, FOX_W, W_BR_SH),
        "w_br_b": s_small[rows_a:rows_a + rows_b].reshape(1, DIL_OUT_W, W_BR_SH), "w_out": s_small[None, rows_a + rows_b:],
    }
    w = {"w_ada": w_ada, "b_ada": b_ada, "g_mix": g_mix, "w_in": w_in, "b_fgate": b_fgate, "w_br_a": w_br_a,
         "w_br_b": w_br_b, "w_out": w_out, "g_ffn": g_ffn, "w_ffn_gate": w_ffn_gate, "w_ffn_up": w_ffn_up,
         "w_ffn_down": w_ffn_down, "g_final": g_final}
    m = {"w_ada": m_w_ada, "b_ada": m_b_ada, "g_mix": m_g_mix, "w_in": m_w_in, "b_fgate": m_b_fgate,
         "w_br_a": m_w_br_a, "w_br_b": m_w_br_b, "w_out": m_w_out, "g_ffn": m_g_ffn, "w_ffn_gate": m_w_ffn_gate,
         "w_ffn_up": m_w_ffn_up, "w_ffn_down": m_w_ffn_down, "g_final": m_g_final}
    v = {"w_ada": v_w_ada, "b_ada": v_b_ada, "g_mix": v_g_mix, "w_in": v_w_in, "b_fgate": v_b_fgate,
         "w_br_a": v_w_br_a, "w_br_b": v_w_br_b, "w_out": v_w_out, "g_ffn": v_g_ffn, "w_ffn_gate": v_w_ffn_gate,
         "w_ffn_up": v_w_ffn_up, "w_ffn_down": v_w_ffn_down, "g_final": v_g_final}
    names = list(w)
    delta, new_m, new_v = {}, {}, {}

    transposed = ("w_in", "w_ffn_gate", "w_ffn_up")
    by_column = lambda t: jnp.transpose(t, (2, 0, 1))
    by_row = lambda t: jnp.transpose(t, (1, 2, 0))

    def update(n):
        shape = w[n].shape
        if n in transposed:
            g[n] = g[n].T[None]
            dl, mn, vn = _adamw_by_planes(by_column(w[n]), by_column(g[n]), by_column(m[n]), by_column(v[n]),
                                          "adamw_" + n)
            delta[n], new_m[n], new_v[n] = by_row(dl), by_row(mn), by_row(vn)
            return
        two_d = (lambda t: t.reshape(shape[-2:])) if len(shape) == 3 else (lambda t: t)
        dl, mn, vn = _adamw(two_d(w[n]), two_d(g[n]), two_d(m[n]), two_d(v[n]), "adamw_" + n)
        delta[n], new_m[n], new_v[n] = dl.reshape(shape), mn.reshape(shape), vn.reshape(shape)

    for n in list(g):
        update(n)
    done = sum(delta[n].reshape(-1)[:N_FOX_HEADS] for n in g)
    s_in_t, = from_chips(mix_state, done, mix_tags, "mixer", [True])
    g["w_in"] = s_in_t[:W_IN_SH]
    update("w_in")

    return (loss, grad_x[None], *[g[n] for n in names], *[delta[n] for n in names],
            *[new_m[n] for n in names], *[new_v[n] for n in names])
```

```python
import jax
import jax.numpy as jnp
import numpy as np
from jax import lax
from jax.experimental import pallas as pl
from jax.experimental.pallas import tpu as pltpu

f32 = jnp.float32
bf16 = jnp.bfloat16
SDS = jax.ShapeDtypeStruct
MESH = pl.DeviceIdType.MESH

N_DEV = 8
D = 1024
SEQ = 2048
HEAD_DIM = 64
N_FOX_HEADS = 8
FOX_W = 512
DIL_W = 768
DIL_OUT_W = 256
ROT_DIM = 16
ROPE_THETA = 500000.0
D_FF = 2816
IN_COLS = 5896
EPS = 1e-6
NEG = -1e30
ATT_SCALE = HEAD_DIM ** -0.5

ADAM_LR = 0.001
ADAM_B1 = 0.9
ADAM_B2 = 0.999
ADAM_EPS = 1e-08
ADAM_WD = 0.01
ADAM_STEP = 10

C_GA, C_GB, C_QB, C_KB, C_VB, C_QA, C_KA, C_VA, C_F = 0, 1024, 2304, 3072, 3840, 4608, 5120, 5632, 6144
PROJ_W = 6272
LANES = 128
VMEM_LIMIT = 52 * 1024 * 1024

W_IN_SH, W_IN_PAD = IN_COLS // N_DEV, 768
W_BR_SH = D // N_DEV
W_FF_SH, FF_PAD = D_FF // N_DEV, 384
FF_HID = N_DEV * FF_PAD
SMALL_ROWS = 16


def _params(sem=None):
    if sem is None:
        return pltpu.CompilerParams(vmem_limit_bytes=VMEM_LIMIT)
    return pltpu.CompilerParams(dimension_semantics=sem, vmem_limit_bytes=VMEM_LIMIT)


def _rowwise(fn, name, tiled, vecs, outs, reds=(), tile=256):
    nt, nv, no = len(tiled), len(vecs), len(outs)
    rows = tiled[0][0].shape[0]
    assert rows % tile == 0

    def body(*refs):
        tin = [r[...] for r in refs[:nt]]
        vin = [r[...] for r in refs[nt:nt + nv]]
        orefs = refs[nt + nv:nt + nv + no]
        rrefs = refs[nt + nv + no:]
        touts, routs = fn(tin, vin)
        for r, t in zip(orefs, touts, strict=True):
            r[...] = t.astype(r.dtype)
        if rrefs:
            @pl.when(pl.program_id(0) == 0)
            def _():
                for r in rrefs:
                    r[...] = jnp.zeros_like(r)
            for r, t in zip(rrefs, routs, strict=True):
                r[...] += t

    def col_map(cb):
        return lambda i: (i, cb)

    def whole_map(nd):
        return lambda i: (0,) * nd

    in_specs = [pl.BlockSpec((tile, w), col_map(cb)) for (_, w, cb) in tiled]
    in_specs += [pl.BlockSpec(v.shape, whole_map(v.ndim)) for v in vecs]
    out_specs = [pl.BlockSpec((tile, w), lambda i: (i, 0)) for (w, _) in outs]
    out_specs += [pl.BlockSpec((1, w), lambda i: (0, 0)) for w in reds]
    out_shape = [SDS((rows, w), dt) for (w, dt) in outs] + [SDS((1, w), f32) for w in reds]
    res = pl.pallas_call(
        body, grid=(rows // tile,), in_specs=in_specs, out_specs=out_specs, out_shape=out_shape, name=name,
        compiler_params=_params(("arbitrary",)),
    )(*[t[0] for t in tiled], *vecs)
    return res


def _matmul(a, b, *, ta=False, out_dtype=f32, name, tm, tn, by_shard=False, after=None):
    (m, k), n = ((a.shape[1], a.shape[0]) if ta else a.shape), b.shape[1]
    assert b.shape[0] == k and m % tm == 0 and n % tn == 0 and (ta or not by_shard)
    dims = (((0 if ta else 1,), (0,)), ((), ()))

    def body(a_ref, b_ref, *rest):
        p = lax.dot_general(a_ref[...].astype(bf16), b_ref[...].astype(bf16), dims, preferred_element_type=f32)
        o_ref = rest[-1]
        if by_shard:
            o_ref[0] = p.astype(o_ref.dtype)
        else:
            o_ref[...] = p.astype(o_ref.dtype)

    a_spec = pl.BlockSpec((k, tm), lambda i, j: (0, i)) if ta else pl.BlockSpec((tm, k), lambda i, j: (i, 0))
    if by_shard:
        assert tn == n // N_DEV
        out_spec, out_shape = pl.BlockSpec((1, tm, tn), lambda i, j: (j, i, 0)), SDS((N_DEV, m, tn), out_dtype)
    else:
        out_spec, out_shape = pl.BlockSpec((tm, tn), lambda i, j: (i, j)), SDS((m, n), out_dtype)
    extra_specs, extra = ([pl.BlockSpec(memory_space=pl.ANY)], [after]) if after is not None else ([], [])
    return pl.pallas_call(
        body, grid=(m // tm, n // tn), in_specs=[a_spec, pl.BlockSpec((k, tn), lambda i, j: (0, j))] + extra_specs,
        out_specs=out_spec, out_shape=out_shape, name=name, compiler_params=_params(("parallel", "parallel")),
    )(a, b, *extra)


def _matmul_stack(a, b, *, ta=False, tb=False, out_dtype=f32, name):
    def lanes(ref):
        return jnp.concatenate([ref[j] for j in range(N_DEV)], axis=1).astype(bf16)

    if ta:
        w = b.shape[1] // N_DEV

        def body(a_ref, b_ref, o_ref):
            p = _tn(a_ref[...].astype(bf16), b_ref[...].astype(bf16))
            for j in range(N_DEV):
                o_ref[j] = p[:, j * w:(j + 1) * w].astype(o_ref.dtype)

        return pl.pallas_call(body, out_shape=SDS((N_DEV, a.shape[1], w), out_dtype), name=name,
                              compiler_params=_params())(a, b)

    m, half = a.shape[0], a.shape[0] // 2
    n = b.shape[1] if tb else N_DEV * b.shape[2]

    def body(a_ref, b_ref, o_ref):
        av = a_ref[...].astype(bf16)
        o_ref[...] = (_nt(av, lanes(b_ref)) if tb else jnp.dot(av, lanes(b_ref), preferred_element_type=f32)
                      ).astype(o_ref.dtype)

    return pl.pallas_call(
        body, grid=(2,), in_specs=[pl.BlockSpec((half, a.shape[1]), lambda i: (i, 0)),
                                   pl.BlockSpec(b.shape, lambda i: (0, 0, 0))],
        out_specs=pl.BlockSpec((half, n), lambda i: (i, 0)), out_shape=SDS((m, n), out_dtype), name=name,
        compiler_params=_params(("parallel",)),
    )(a, b)


def _matmul_rows(form, a, b, after, fn, tiled, vecs, outs, reds, *, name, tm=512):
    norm = lambda ts: [t if isinstance(t, tuple) else (t, t.shape[1], 0) for t in ts]
    make, sources = a if isinstance(a, tuple) else (None, [a])
    sources, tiled = norm(sources), norm(tiled)
    m, k = sources[0][0].shape[0], (b.shape[0] if form == "nn" else b.shape[-1] * (N_DEV if form == "nt_stack" else 1))
    assert m % tm == 0
    ns, nt, nv, no = len(sources), len(tiled), len(vecs), len(outs)

    def body(*refs):
        src_refs, b_ref, refs = refs[:ns], refs[ns], refs[ns + 2:]
        if make is None:
            lhs = lambda lo, hi: src_refs[0][:, lo:hi]
        else:
            made = make([r[...] for r in src_refs]).astype(bf16)
            lhs = lambda lo, hi: made[:, lo:hi]
        if form == "nt_stack":
            w = b.shape[2]
            acc = _nt(lhs(0, w), b_ref[0])
            for j in range(1, N_DEV):
                acc = acc + _nt(lhs(j * w, (j + 1) * w), b_ref[j])
        elif form == "nt":
            acc = _nt(lhs(0, k), b_ref[...])
        else:
            acc = jnp.dot(lhs(0, k), b_ref[...], preferred_element_type=f32)
        if make is not None:
            refs[nt + nv][...] = made
            refs = refs[:nt + nv] + refs[nt + nv + 1:]
        orefs, rrefs = refs[nt + nv:nt + nv + no], refs[nt + nv + no:]
        touts, routs = fn([acc] + [r[...] for r in refs[:nt]], [r[...] for r in refs[nt:nt + nv]])
        for r, t in zip(orefs, touts, strict=True):
            r[...] = t.astype(r.dtype)

        @pl.when(pl.program_id(0) == 0)
        def _():
            for r in rrefs:
                r[...] = jnp.zeros_like(r)
        for r, t in zip(rrefs, routs, strict=True):
            r[...] += t

    def whole_map(nd):
        return lambda i: (0,) * nd

    def rows(width, cb=0):
        return pl.BlockSpec((tm, width), lambda i: (i, cb))

    made_out = [(k, bf16)] if make is not None else []
    return pl.pallas_call(
        body, grid=(m // tm,),
        in_specs=[rows(width, cb) for _, width, cb in sources]
        + [pl.BlockSpec(b.shape, whole_map(b.ndim), pipeline_mode=pl.Buffered(1)), pl.BlockSpec(memory_space=pl.ANY)]
        + [rows(width, cb) for _, width, cb in tiled] + [pl.BlockSpec(v.shape, whole_map(v.ndim)) for v in vecs],
        out_specs=[rows(width) for width, _ in made_out + list(outs)]
        + [pl.BlockSpec((1, width), lambda i: (0, 0)) for width in reds],
        out_shape=[SDS((m, width), dt) for width, dt in made_out + list(outs)]
        + [SDS((1, width), f32) for width in reds], name=name,
        compiler_params=_params(("arbitrary",)),
    )(*[t[0] for t in sources], b, after, *[t[0] for t in tiled], *vecs)


def _rms(x):
    r = lax.rsqrt(jnp.mean(x * x, axis=-1, keepdims=True) + EPS)
    return r, x * r


def _rms_bwd(r, xn, dxn):
    return r * (dxn - xn * jnp.mean(dxn * xn, axis=-1, keepdims=True))


def _colsum(t):
    return jnp.sum(t, axis=0, keepdims=True)


def _sigmoid(x):
    return 0.5 * jnp.tanh(0.5 * x) + 0.5


def _modulated_norm(x, g, shift, scale):
    _, xn = _rms(x)
    return (xn * g) * (1.0 + scale) + shift


def _pre1(x, modv, g_mix):
    def fn(t, v):
        (xt,), (mv, g) = t, v
        return [_modulated_norm(xt, g, mv[0:1], mv[1:2])], []
    return _rowwise(fn, "pre1", [(x, D, 0)], [modv, g_mix], [(D, bf16)])[0]


def _post1(ya, yb, proj, w_o, x, modv, g_ffn):
    def merge(t):
        ya_t, yb_t, ga, gb = t
        return _sigmoid(ga) * ya_t + _sigmoid(gb) * yb_t

    def fn(t, v):
        (mt, xt), (mv, g) = t, v
        x1 = xt + mv[2:3] * mt
        return [mt, x1, _modulated_norm(x1, g, mv[3:4], mv[4:5])], []
    return _matmul_rows("nn", (merge, [ya, yb, (proj, D, C_GA // D), (proj, D, C_GB // D)]), w_o, x, fn, [x],
                        [modv, g_ffn], [(D, f32), (D, f32), (D, bf16)], [], name="post1")


def _ffn_in(h, w_stack):
    def body(h_ref, w_ref, act_ref, au_ref):
        p = jnp.dot(h_ref[...], w_ref[0], preferred_element_type=f32)
        a, u = p[:, :FF_PAD], p[:, FF_PAD:]
        act_ref[...] = (a * _sigmoid(a) * u).astype(act_ref.dtype)
        au_ref[...] = p.astype(au_ref.dtype)

    return pl.pallas_call(
        body, grid=(N_DEV,),
        in_specs=[pl.BlockSpec((SEQ, D), lambda j: (0, 0)), pl.BlockSpec((1, D, 2 * FF_PAD), lambda j: (j, 0, 0))],
        out_specs=[pl.BlockSpec((SEQ, FF_PAD), lambda j: (0, j)), pl.BlockSpec((SEQ, 2 * FF_PAD), lambda j: (0, j))],
        out_shape=(SDS((SEQ, FF_HID), bf16), SDS((SEQ, 2 * FF_HID), bf16)), name="ffn_in",
        compiler_params=_params(("parallel",)),
    )(h, w_stack)


def _ffn_bwd_in(dff, w_down_stack, au):
    def body(d_ref, w_ref, au_ref, o_ref):
        dact = _nt(d_ref[...], w_ref[0])
        p = au_ref[...].astype(f32)
        a, u = p[:, :FF_PAD], p[:, FF_PAD:]
        sg = _sigmoid(a)
        o_ref[...] = jnp.concatenate([dact * u * (sg * (1.0 + a * (1.0 - sg))), dact * (a * sg)],
                                     axis=1).astype(o_ref.dtype)

    return pl.pallas_call(
        body, grid=(N_DEV,),
        in_specs=[pl.BlockSpec((SEQ, D), lambda j: (0, 0)), pl.BlockSpec((1, FF_PAD, D), lambda j: (j, 0, 0)),
                  pl.BlockSpec((SEQ, 2 * FF_PAD), lambda j: (0, j))],
        out_specs=pl.BlockSpec((SEQ, 2 * FF_PAD), lambda j: (0, j)),
        out_shape=SDS((SEQ, 2 * FF_HID), bf16), name="ffn_bwd_in", compiler_params=_params(("parallel",)),
    )(dff, w_down_stack, au)


def _final(act, w_down, x1, target, modv, g_final):
    def fn(t, v):
        (fft, x1t, tgt), (mv, g) = t, v
        x2 = x1t + mv[5:6] * fft
        r, xn = _rms(x2)
        err = xn * g - tgt
        dy = err * (1.0 / D)
        dx2 = _rms_bwd(r, xn, dy * g)
        return [dx2, dx2 * mv[5:6]], [_colsum(dy * xn), _colsum(dx2 * fft), _colsum(err * err) * (0.5 / D)]
    return _matmul_rows("nn", act, w_down, x1, fn, [x1, target], [modv, g_final], [(D, f32), (D, bf16)], [D, D, D],
                        name="final")


def _mid_bwd(dau, w_stack, after, x1, dx2, mix, modv, g_ffn):
    def fn(t, v):
        (dh, x1t, dx2t, mt), (mv, g) = t, v
        r, xn = _rms(x1t)
        dn = dh * (1.0 + mv[4:5])
        dx1 = dx2t + _rms_bwd(r, xn, dn * g)
        return [dx1, dx1 * mv[2:3]], [_colsum(dh), _colsum(dh * (xn * g)), _colsum(dn * xn), _colsum(dx1 * mt)]
    return _matmul_rows("nt_stack", dau, w_stack, after, fn, [x1, dx2, mix], [modv, g_ffn], [(D, f32), (D, bf16)],
                        [D, D, D, D], name="mid_bwd")


def _first_bwd(dproj, w_stack, after, x, dx1, modv, g_mix):
    def fn(t, v):
        (dh, xt, dx1t), (mv, g) = t, v
        r, xn = _rms(xt)
        dn = dh * (1.0 + mv[1:2])
        return [dx1t + _rms_bwd(r, xn, dn * g)], [_colsum(dh), _colsum(dh * (xn * g)), _colsum(dn * xn)]
    return _matmul_rows("nt_stack", dproj, w_stack, after, fn, [x, dx1], [modv, g_mix], [(D, f32)], [D, D, D],
                        name="first_bwd")


def _merge_bwd(dmix, w_o, after, ya, yb, proj):
    def fn(t, v):
        dm, ya_t, yb_t, ga, gb = t
        sa, sb = _sigmoid(ga), _sigmoid(gb)
        return [dm * sa, dm * sb, dm * ya_t * (sa * (1.0 - sa)), dm * yb_t * (sb * (1.0 - sb))], []
    return _matmul_rows("nt", dmix, w_o, after, fn, [ya, yb, (proj, D, C_GA // D), (proj, D, C_GB // D)], [],
                        [(D, bf16), (D, bf16), (D, bf16), (D, bf16)], [], name="merge_bwd")


def _rope_tables():
    half = ROT_DIM // 2
    pos = np.arange(SEQ, dtype=np.float32)
    inv_freq = np.float32(ROPE_THETA) ** (-np.arange(0, ROT_DIM, 2, dtype=np.float32) / np.float32(ROT_DIM))
    ang = pos[:, None] * inv_freq[None, :].astype(np.float32)
    cos, sin = np.cos(ang).astype(np.float32), np.sin(ang).astype(np.float32)
    pad = np.zeros((SEQ, HEAD_DIM - ROT_DIM), np.float32)
    zero = np.zeros((SEQ, half), np.float32)
    c_head = np.concatenate([cos, cos, pad + 1.0], axis=1)
    lo_head = np.concatenate([-sin, zero, pad], axis=1)
    hi_head = np.concatenate([zero, sin, pad], axis=1)
    return tuple(jnp.asarray(np.concatenate([t, t], axis=1)) for t in (c_head, lo_head, hi_head))


def _over_heads(tables):
    return [jnp.tile(t, (1, DIL_W // LANES)) for t in tables]


def _rope_fwd(proj, tables):
    half = ROT_DIM // 2

    def fn(t, v):
        q, k = t[:2]
        c, lo, hi = _over_heads(t[2:])
        rot = lambda z: z * c + pltpu.roll(z, DIL_W - half, 1) * lo + pltpu.roll(z, half, 1) * hi
        return [rot(q) * ATT_SCALE, rot(k)], []
    return _rowwise(fn, "rope_fwd", [(proj, DIL_W, C_QB // DIL_W), (proj, DIL_W, C_KB // DIL_W)]
                    + [(tb, LANES, 0) for tb in tables], [], [(DIL_W, f32)] * 2)


def _rope_bwd(dqs, dks, tables):
    half = ROT_DIM // 2

    def fn(t, v):
        dq_t, dk_t = jnp.concatenate(t[:N_GROUPS], axis=1), jnp.concatenate(t[N_GROUPS:2 * N_GROUPS], axis=1)
        c, lo, hi = _over_heads(t[2 * N_GROUPS:])
        rot_t = lambda z: z * c + pltpu.roll(z * lo, half, 1) + pltpu.roll(z * hi, DIL_W - half, 1)
        return [rot_t(dq_t), rot_t(dk_t)], []
    return _rowwise(fn, "rope_bwd", [(a, DIL_OUT_W, 0) for a in (*dqs, *dks)] + [(tb, LANES, 0) for tb in tables],
                    [], [(DIL_W, bf16), (DIL_W, bf16)])


def _head_bcast_sum(d):
    lane = lax.broadcasted_iota(jnp.int32, d.shape, 1)
    out = jnp.zeros_like(d)
    for h in range(d.shape[1] // HEAD_DIM):
        sel = (lane >= h * HEAD_DIM) & (lane < (h + 1) * HEAD_DIM)
        out = jnp.where(sel, jnp.sum(jnp.where(sel, d, 0.0), axis=1, keepdims=True), out)
    return out


def _dil_combine(outs, lses):
    def fn(t, v):
        o0, o1, o2, l0, l1, l2 = t
        m = jnp.maximum(jnp.maximum(l0, l1), l2)
        w0, w1, w2 = jnp.exp(l0 - m), jnp.exp(l1 - m), jnp.exp(l2 - m)
        tot = w0 + w1 + w2
        return [(w0 * o0 + w1 * o1 + w2 * o2) / tot, m + jnp.log(tot)], []
    w = DIL_OUT_W
    return _rowwise(fn, "dil_combine", [(t, w, 0) for t in (*outs, *lses)], [], [(w, f32), (w, f32)])


def _dil_delta(dyb_h, yb_h):
    def fn(t, v):
        return [_head_bcast_sum(t[0] * t[1])], []
    return _rowwise(fn, "dil_delta", [(dyb_h, DIL_OUT_W, 0), (yb_h, DIL_OUT_W, 0)], [], [(DIL_OUT_W, f32)])[0]


def _adamw_math(wt, gt, mt, vt):
    mn = ADAM_B1 * mt + (1.0 - ADAM_B1) * gt
    vn = ADAM_B2 * vt + (1.0 - ADAM_B2) * (gt * gt)
    m_hat = mn / (1.0 - ADAM_B1 ** ADAM_STEP)
    v_hat = vn / (1.0 - ADAM_B2 ** ADAM_STEP)
    return -ADAM_LR * (m_hat / (jnp.sqrt(v_hat) + ADAM_EPS) + ADAM_WD * wt), mn, vn


def _adamw(w, g, m, v, name):
    shape = w.shape
    if w.ndim == 1:
        w, g, m, v = (t.reshape(1, -1) for t in (w, g, m, v))
    rows, cols = w.shape
    tile = 256 if rows % 256 == 0 and rows > 512 else rows

    def fn(t, _):
        return list(_adamw_math(*t)), []
    delta, mn, vn = _rowwise(fn, name, [(w, cols, 0), (g, cols, 0), (m, cols, 0), (v, cols, 0)], [],
                             [(cols, f32)] * 3, tile=tile)
    return delta.reshape(shape), mn.reshape(shape), vn.reshape(shape)


def _adamw_by_planes(w, g, m, v, name, most=128):
    planes, _, width = w.shape
    tile = max(t for t in range(1, most + 1) if planes % t == 0)

    def body(w_ref, g_ref, m_ref, v_ref, d_ref, mn_ref, vn_ref):
        d_ref[...], mn_ref[...], vn_ref[...] = _adamw_math(w_ref[...], g_ref[...], m_ref[...], v_ref[...])

    spec = pl.BlockSpec((tile, 1, width), lambda i: (i, 0, 0))
    return pl.pallas_call(body, grid=(planes // tile,), in_specs=[spec] * 4, out_specs=[spec] * 3,
                          out_shape=[SDS(w.shape, f32)] * 3, name=name,
                          compiler_params=_params(("parallel",)))(w, g, m, v)


def _ada_fwd(c_all, w_shard, b_shard):
    def body(c_ref, w_ref, b_ref, o_ref):
        cv = c_ref[...]
        sc = (cv * _sigmoid(cv)).astype(bf16)
        o_ref[...] = jnp.dot(sc, w_ref[...].astype(bf16), preferred_element_type=f32) + b_ref[...]
    return pl.pallas_call(body, out_shape=SDS((N_DEV, w_shard.shape[1]), f32), name="ada_fwd",
                          compiler_params=_params())(c_all, w_shard, b_shard)


def _ada_bwd(c_all, dmod_cols):
    def body(c_ref, d_ref, o_ref):
        cv = c_ref[...]
        sc = cv * _sigmoid(cv)
        o_ref[...] = lax.dot_general(sc, d_ref[...], (((0,), (0,)), ((), ())), precision=lax.Precision.HIGHEST,
                                     preferred_element_type=f32)
    return pl.pallas_call(body, out_shape=SDS((D, dmod_cols.shape[1]), f32), name="ada_bwd",
                          compiler_params=_params())(c_all, dmod_cols)


def _small_reduce(gathered, after):
    def body(g_ref, after_ref, o_ref, loss_ref):
        acc = g_ref[0]
        for d in range(1, N_DEV):
            acc = acc + g_ref[d]
        o_ref[...] = acc
        loss_ref[...] = jnp.zeros((1, LANES), f32) + jnp.sum(acc[10:11, :])
    return pl.pallas_call(body, out_shape=(SDS((SMALL_ROWS, D), f32), SDS((1, LANES), f32)), name="small_reduce",
                          in_specs=[pl.BlockSpec(memory_space=pltpu.VMEM), pl.BlockSpec(memory_space=pl.ANY)],
                          compiler_params=_params())(gathered, after)


FOX_BLK = 512
CUM_BLK = 128


def _fold_lanes(t, op):
    out = t[:, :LANES]
    for j in range(1, t.shape[1] // LANES):
        out = op(out, t[:, j * LANES:(j + 1) * LANES])
    return out


def _fox_gate_fwd(proj, b_pad):
    nblk = SEQ // CUM_BLK

    def body(f_ref, b_ref, col_ref):
        r = lax.broadcasted_iota(jnp.int32, (CUM_BLK, CUM_BLK), 0)
        c = lax.broadcasted_iota(jnp.int32, (CUM_BLK, CUM_BLK), 1)
        tri = (r >= c).astype(f32)
        carry = jnp.zeros((1, LANES), f32)
        for blk in range(nblk):
            z = f_ref[blk * CUM_BLK:(blk + 1) * CUM_BLK, :] + b_ref[...]
            logf = jnp.minimum(z, 0.0) - jnp.log1p(jnp.exp(-jnp.abs(z)))
            cs = jnp.dot(tri, logf, precision=lax.Precision.HIGHEST, preferred_element_type=f32) + carry
            col_ref[blk * CUM_BLK:(blk + 1) * CUM_BLK, :] = cs
            carry = cs[CUM_BLK - 1:CUM_BLK, :]

    return pl.pallas_call(
        body, grid=(1,), in_specs=[pl.BlockSpec((SEQ, LANES), lambda i: (0, C_F // LANES)),
                                   pl.BlockSpec((1, LANES), lambda i: (0, 0))],
        out_specs=pl.BlockSpec((SEQ, LANES), lambda i: (0, 0)),
        out_shape=SDS((SEQ, LANES), f32), name="fox_gate_fwd",
        compiler_params=_params(("arbitrary",)),
    )(proj, b_pad)


def _fox_gate_bwd(dF_row, proj, b_pad):
    nblk = SEQ // CUM_BLK

    def body(d_ref, f_ref, b_ref, df_ref, db_ref, col_ref):
        r = lax.broadcasted_iota(jnp.int32, (CUM_BLK, CUM_BLK), 0)
        c = lax.broadcasted_iota(jnp.int32, (CUM_BLK, CUM_BLK), 1)
        tri = (r <= c).astype(f32)
        lane = lax.broadcasted_iota(jnp.int32, (CUM_BLK, LANES), 1)
        col_ref[...] = d_ref[...].T
        carry = jnp.zeros((1, LANES), f32)
        total = jnp.zeros((1, LANES), f32)
        for blk in reversed(range(nblk)):
            rows = slice(blk * CUM_BLK, (blk + 1) * CUM_BLK)
            cs = jnp.dot(tri, col_ref[rows, :], precision=lax.Precision.HIGHEST, preferred_element_type=f32) + carry
            carry = cs[0:1, :]
            z = f_ref[rows, :] + b_ref[...]
            df = jnp.where(lane < N_FOX_HEADS, cs * _sigmoid(-z), 0.0)
            df_ref[rows, :] = df.astype(df_ref.dtype)
            total = total + _colsum(df)
        db_ref[...] = total

    return pl.pallas_call(
        body, grid=(1,), in_specs=[pl.BlockSpec((LANES, SEQ), lambda i: (0, 0)),
                                   pl.BlockSpec((SEQ, LANES), lambda i: (0, C_F // LANES)),
                                   pl.BlockSpec((1, LANES), lambda i: (0, 0))],
        out_specs=[pl.BlockSpec((SEQ, LANES), lambda i: (0, 0)), pl.BlockSpec((1, LANES), lambda i: (0, 0))],
        out_shape=(SDS((SEQ, LANES), bf16), SDS((1, LANES), f32)), name="fox_gate_bwd",
        scratch_shapes=[pltpu.VMEM((SEQ, LANES), f32)],
        compiler_params=_params(("arbitrary",)),
    )(dF_row, proj, b_pad)


def _nt(a, b):
    return lax.dot_general(a, b, (((1,), (1,)), ((), ())), preferred_element_type=f32)


def _tn(a, b):
    return lax.dot_general(a, b, (((0,), (0,)), ((), ())), preferred_element_type=f32)


def _fox_prep(proj, f_col):
    def fn(t, v):
        q, k, vv, fc = t
        lane = lax.broadcasted_iota(jnp.int32, (q.shape[0], LANES), 1)
        qs, ks = [], []
        for h in range(N_FOX_HEADS):
            pair, pos = divmod(h, 2)
            own = (lane >= pos * HEAD_DIM) & (lane < (pos + 1) * HEAD_DIM)
            base = (1 - pos) * HEAD_DIM
            f = fc[:, h:h + 1]
            hi = f.astype(bf16).astype(f32)
            mid = (f - hi).astype(bf16).astype(f32)
            lo = (f - hi) - mid
            one = jnp.ones_like(f)
            qa = jnp.where(own, q[:, pair * LANES:(pair + 1) * LANES] * ATT_SCALE, 0.0)
            ka = k[:, pair * LANES:(pair + 1) * LANES]
            for idx, (qv, kv) in enumerate([(hi, one), (mid, one), (lo, one), (one, -hi), (one, -mid), (one, -lo)]):
                sel = lane == base + idx
                qa = jnp.where(sel, qv, qa)
                ka = jnp.where(sel, kv, ka)
            qs.append(qa)
            ks.append(ka)
        return [jnp.concatenate(qs, axis=1), jnp.concatenate(ks, axis=1), vv], []
    w = N_FOX_HEADS * LANES
    return _rowwise(fn, "fox_prep", [(proj, FOX_W, C_QA // FOX_W), (proj, FOX_W, C_KA // FOX_W),
                                     (proj, FOX_W, C_VA // FOX_W), (f_col, LANES, 0)], [],
                    [(w, bf16), (w, bf16), (FOX_W, bf16)])


def _fox_fwd(q_aug, k_aug, v):
    blk = FOX_BLK
    npair = FOX_W // LANES

    def body(q_ref, k_ref, v_ref, o_ref, max_ref, sum_ref, s_scr):
        i = pl.program_id(1)
        tri = lax.broadcasted_iota(jnp.int32, (blk, blk), 0) >= lax.broadcasted_iota(jnp.int32, (blk, blk), 1)
        qh = [q_ref[:, h * LANES:(h + 1) * LANES] for h in range(2)]

        def logits(c, masked):
            off = pl.multiple_of(c * blk, blk)
            tops = []
            for h in range(2):
                s = _nt(qh[h], k_ref[pl.ds(off, blk), h * LANES:(h + 1) * LANES])
                if masked:
                    s = jnp.where(tri, s, NEG)
                s_scr[h, :, pl.ds(off, blk)] = s
                tops.append(_fold_lanes(s, jnp.maximum))
            return tops

        def pass_a(c, m):
            return tuple(jnp.maximum(a, b) for a, b in zip(m, logits(c, False)))

        m = lax.fori_loop(0, i, pass_a, tuple(jnp.full((blk, LANES), NEG, f32) for _ in range(2)))
        mx = [jnp.max(jnp.maximum(a, b), axis=1, keepdims=True) for a, b in zip(m, logits(i, True))]

        def pass_b(c, carry):
            off = pl.multiple_of(c * blk, blk)
            vv = v_ref[pl.ds(off, blk), :]
            new = []
            for h in range(2):
                l, acc = carry[h]
                p = jnp.exp(s_scr[h, :, pl.ds(off, blk)] - mx[h]).astype(bf16)
                new.append((l + _fold_lanes(p.astype(f32), jnp.add), acc + jnp.dot(p, vv, preferred_element_type=f32)))
            return tuple(new)

        zero = jnp.zeros((blk, LANES), f32)
        (l_a, acc_a), (l_b, acc_b) = lax.fori_loop(0, i + 1, pass_b, ((zero, zero), (zero, zero)))
        l_a = jnp.sum(l_a, axis=1, keepdims=True)
        l_b = jnp.sum(l_b, axis=1, keepdims=True)
        first = lax.broadcasted_iota(jnp.int32, (blk, LANES), 1) < HEAD_DIM
        o_ref[...] = jnp.where(first, acc_a / l_a, acc_b / l_b)
        max_ref[0] = jnp.where(first, mx[0], mx[1])
        sum_ref[0] = jnp.where(first, l_a, l_b)

    return pl.pallas_call(
        body, grid=(npair, SEQ // blk),
        in_specs=[pl.BlockSpec((blk, 2 * LANES), lambda p, i: (i, p)),
                  pl.BlockSpec((SEQ, 2 * LANES), lambda p, i: (0, p)),
                  pl.BlockSpec((SEQ, LANES), lambda p, i: (0, p))],
        out_specs=[pl.BlockSpec((blk, LANES), lambda p, i: (i, p))]
        + [pl.BlockSpec((1, blk, LANES), lambda p, i: (p, i, 0))] * 2,
        out_shape=(SDS((SEQ, FOX_W), f32),) + (SDS((npair, SEQ, LANES), f32),) * 2, name="fox_fwd",
        scratch_shapes=[pltpu.VMEM((2, blk, SEQ), f32)],
        compiler_params=_params(("parallel", "arbitrary")),
    )(q_aug, k_aug, v)


def _fox_bwd(q_aug, k_aug, v, do, o, row_max, row_sum, after):
    blk = FOX_BLK
    npair = FOX_W // LANES
    nblk = SEQ // blk

    def body(q_ref, k_ref, v_ref, do_ref, o_ref, max_ref, sum_ref, after_ref, dq_ref, dk_ref, dv_ref, df_ref, dq_acc,
             delta_ref, inv_ref):
        inv_ref[...] = 1.0 / sum_ref[0]
        lane_s = lax.broadcasted_iota(jnp.int32, (SEQ, LANES), 1)
        prod = do_ref[...].astype(bf16).astype(f32) * o_ref[...]
        d_a = jnp.sum(jnp.where(lane_s < HEAD_DIM, prod, 0.0), axis=1, keepdims=True)
        d_b = jnp.sum(jnp.where(lane_s >= HEAD_DIM, prod, 0.0), axis=1, keepdims=True)
        delta_ref[...] = jnp.where(lane_s < HEAD_DIM, d_a, d_b)
        dq_acc[...] = jnp.zeros_like(dq_acc)
        df_ref[...] = jnp.zeros_like(df_ref)
        lane = lax.broadcasted_iota(jnp.int32, (blk, LANES), 1)
        own = [lane < HEAD_DIM, lane >= HEAD_DIM]
        tri = lax.broadcasted_iota(jnp.int32, (blk, blk), 0) >= lax.broadcasted_iota(jnp.int32, (blk, blk), 1)

        def q_slab(qoff, h):
            return q_ref[pl.ds(qoff, blk), h * LANES:(h + 1) * LANES]

        def probs(qoff, h, k_h, masked):
            s = _nt(q_slab(qoff, h), k_h)
            if masked:
                s = jnp.where(tri, s, NEG)
            col = slice(h * HEAD_DIM, h * HEAD_DIM + 1)
            weights = jnp.exp(s - max_ref[0, pl.ds(qoff, blk), col]).astype(bf16).astype(f32)
            return weights * inv_ref[pl.ds(qoff, blk), col]

        def k_slabs(koff):
            return [k_ref[pl.ds(koff, blk), h * LANES:(h + 1) * LANES] for h in range(2)]

        def kv_step(kj, _):
            koff = pl.multiple_of(kj * blk, blk)
            k_aug = k_slabs(koff)
            k_own = [jnp.where(own[h], k_aug[h], jnp.zeros_like(k_aug[h])) for h in range(2)]
            vv = v_ref[pl.ds(koff, blk), :]
            v_own = [jnp.where(own[h], vv, jnp.zeros_like(vv)) for h in range(2)]

            def q_tile(qi, carry, masked):
                qoff = pl.multiple_of(qi * blk, blk)
                dd = do_ref[pl.ds(qoff, blk), :].astype(bf16)
                new, dq_add = [], None
                for h in range(2):
                    dk_h, dv_h, dcol = carry[h]
                    p = probs(qoff, h, k_aug[h], masked)
                    dl = p * (_nt(dd, v_own[h]) - delta_ref[pl.ds(qoff, blk), h * HEAD_DIM:h * HEAD_DIM + 1])
                    dlb = dl.astype(bf16)
                    part = jnp.dot(dlb, k_own[h], preferred_element_type=f32)
                    dq_add = part if dq_add is None else dq_add + part
                    new.append((dk_h + _tn(dlb, q_slab(qoff, h)), dv_h + _tn(p.astype(bf16), dd),
                                dcol + _colsum(dl)))
                dq_acc[pl.ds(qoff, blk), :] += dq_add * ATT_SCALE
                return tuple(new)

            zero = (jnp.zeros((blk, LANES), f32), jnp.zeros((blk, LANES), f32), jnp.zeros((1, blk), f32))
            carry = q_tile(kj, (zero, zero), True)
            (dk_a, dv_a, dcol_a), (dk_b, dv_b, dcol_b) = lax.fori_loop(
                kj + 1, nblk, lambda qi, cr: q_tile(qi, cr, False), carry)
            dk_ref[pl.ds(koff, blk), :] = jnp.where(own[0], dk_a, dk_b).astype(dk_ref.dtype)
            dv_ref[pl.ds(koff, blk), :] = jnp.where(own[0], dv_a, dv_b).astype(dv_ref.dtype)
            df_ref[0, 0:1, pl.ds(koff, blk)] = -dcol_a
            df_ref[0, 1:2, pl.ds(koff, blk)] = -dcol_b
            return 0

        lax.fori_loop(0, nblk, kv_step, 0)
        dq_ref[...] = dq_acc[...].astype(dq_ref.dtype)

    pair_aug = pl.BlockSpec((SEQ, 2 * LANES), lambda p: (0, p))
    slab = pl.BlockSpec((SEQ, LANES), lambda p: (0, p))
    per_pair = pl.BlockSpec((1, SEQ, LANES), lambda p: (p, 0, 0))
    rows = pl.BlockSpec((1, 8, SEQ), lambda p: (p, 0, 0))
    return pl.pallas_call(
        body, grid=(npair,),
        in_specs=[pair_aug, pair_aug, slab, slab, slab, per_pair, per_pair, pl.BlockSpec(memory_space=pl.ANY)],
        out_specs=[slab, slab, slab, rows],
        out_shape=(SDS((SEQ, FOX_W), bf16),) * 3 + (SDS((npair, 8, SEQ), f32),), name="fox_bwd",
        scratch_shapes=[pltpu.VMEM((SEQ, LANES), f32)] * 3,
        compiler_params=_params(("parallel",)),
    )(q_aug, k_aug, v, do, o, row_max, row_sum, after)


DIL_BLK = 128
DILATIONS = (1, 4, 16)
N_GROUPS = len(DILATIONS)
DIL_PAIRS = DIL_OUT_W // LANES


def _dil_blocks(d):
    r1 = lax.broadcasted_iota(jnp.int32, (2 * DIL_BLK, DIL_BLK), 0) & (DIL_BLK - 1)
    c1 = lax.broadcasted_iota(jnp.int32, (2 * DIL_BLK, DIL_BLK), 1)
    r2 = lax.broadcasted_iota(jnp.int32, (2 * DIL_BLK, 2 * DIL_BLK), 0) & (DIL_BLK - 1)
    c2 = lax.broadcasted_iota(jnp.int32, (2 * DIL_BLK, 2 * DIL_BLK), 1)
    band = ((c2 < DIL_BLK) & (c2 >= r2)) | ((c2 >= DIL_BLK) & (c2 - DIL_BLK <= r2))
    out = []
    for r in range(d):
        for b in range(SEQ // d // DIL_BLK):
            rows = pl.ds(r + d * DIL_BLK * b, DIL_BLK, stride=d)
            if b == 0:
                out.append((rows, rows, r1 >= c1))
            else:
                out.append((rows, pl.ds(r + d * DIL_BLK * (b - 1), 2 * DIL_BLK, stride=d), band))
    return out


def _dil_v_spec(g):
    return pl.BlockSpec((SEQ, LANES), lambda p: (0, C_VB // LANES + DIL_PAIRS * g + p))


def _stack_heads(t, first):
    zero = jnp.zeros_like(t)
    return jnp.concatenate([jnp.where(first, t, zero), jnp.where(first, zero, t)], axis=0)


def _dil_fwd(q, k, v, g):
    def body(q_ref, k_ref, v_ref, o_ref, lse_ref):
        first = lax.broadcasted_iota(jnp.int32, (DIL_BLK, LANES), 1) < HEAD_DIM
        for rows, krows, mask in _dil_blocks(DILATIONS[g]):
            qv, kk, vv = q_ref[rows, :].astype(bf16), k_ref[krows, :].astype(bf16), v_ref[krows, :].astype(bf16)
            s = jnp.where(mask, _nt(_stack_heads(qv, first), kk), NEG)
            m = jnp.max(s, axis=1, keepdims=True)
            p = jnp.exp(s - m)
            l = jnp.sum(p, axis=1, keepdims=True)
            out = jnp.dot(p.astype(bf16), vv, preferred_element_type=f32) / l
            lse = m + jnp.log(l)
            o_ref[rows, :] = jnp.where(first, out[:DIL_BLK], out[DIL_BLK:])
            lse_ref[rows, :] = jnp.where(first, lse[:DIL_BLK], lse[DIL_BLK:])

    grouped = pl.BlockSpec((SEQ, LANES), lambda p: (0, DIL_PAIRS * g + p))
    own = pl.BlockSpec((SEQ, LANES), lambda p: (0, p))
    shape = SDS((SEQ, DIL_OUT_W), f32)
    return pl.pallas_call(
        body, grid=(DIL_PAIRS,), in_specs=[grouped, grouped, _dil_v_spec(g)], out_specs=[own] * 2,
        out_shape=(shape, shape),
        name=f"dil_fwd_{DILATIONS[g]}", compiler_params=_params(("parallel",)),
    )(q, k, v)


def _dil_bwd(q, k, v, do, lse, delta, g):
    def body(q_ref, k_ref, v_ref, do_ref, lse_ref, dl_ref, dq_ref, dk_ref, dv_ref):
        first = lax.broadcasted_iota(jnp.int32, (DIL_BLK, LANES), 1) < HEAD_DIM
        dk_ref[...] = jnp.zeros_like(dk_ref)
        dv_ref[...] = jnp.zeros_like(dv_ref)
        for rows, krows, mask in _dil_blocks(DILATIONS[g]):
            qv, kk, vv = q_ref[rows, :].astype(bf16), k_ref[krows, :].astype(bf16), v_ref[krows, :].astype(bf16)
            lsev, delv = lse_ref[rows, :], dl_ref[rows, :]
            q2 = _stack_heads(qv, first)
            do2 = _stack_heads(do_ref[rows, :].astype(bf16), first)
            per_head = lambda t: jnp.concatenate([t[:, 0:1], t[:, HEAD_DIM:HEAD_DIM + 1]], axis=0)
            p = jnp.exp(jnp.where(mask, _nt(q2, kk), NEG) - per_head(lsev))
            dl = (p * (_nt(do2, vv) - per_head(delv))).astype(bf16)
            dq = jnp.dot(dl, kk, preferred_element_type=f32)
            dq_ref[rows, :] = jnp.where(first, dq[:DIL_BLK], dq[DIL_BLK:]) * ATT_SCALE
            dk_ref[krows, :] += _tn(dl, q2)
            dv_ref[krows, :] += _tn(p.astype(bf16), do2)

    grouped = pl.BlockSpec((SEQ, LANES), lambda p: (0, DIL_PAIRS * g + p))
    own = pl.BlockSpec((SEQ, LANES), lambda p: (0, p))
    shape = SDS((SEQ, DIL_OUT_W), f32)
    return pl.pallas_call(
        body, grid=(DIL_PAIRS,), in_specs=[grouped, grouped, _dil_v_spec(g)] + [own] * 3, out_specs=[own] * 3,
        out_shape=(shape, shape, shape), name=f"dil_bwd_{DILATIONS[g]}", compiler_params=_params(("parallel",)),
    )(q, k, v, do, lse, delta)


def _position():
    return lax.axis_index("x"), lax.axis_index("y"), lax.axis_index("c")


def _all_gather(block, name):
    def body(x_ref, out_ref, send_sems, recv_sems, local_sem):
        x, y, c = _position()
        me, sibling = (x, y, c), (x, y, 1 - c)
        chips = [(1 - x, y), (x, 1 - y), (1 - x, 1 - y)]

        def slot(px, py, pc):
            return out_ref.at[4 * px + 2 * py + pc]

        def copy(k, blk, to, src=None):
            return pltpu.make_async_remote_copy(
                src_ref=slot(*blk) if src is None else src, dst_ref=slot(*blk),
                send_sem=send_sems.at[k], recv_sem=recv_sems.at[k], device_id=to, device_id_type=MESH)

        mine = pltpu.make_async_copy(x_ref, slot(*me), local_sem)
        mine.start()
        first = [copy(0, me, sibling, src=x_ref)]
        first += [copy(1 + j, me, (*chip, c), src=x_ref) for j, chip in enumerate(chips)]
        for cp in first:
            cp.start()
        passed = [copy(4 + j, (*chip, c), sibling) for j, chip in enumerate(chips)]
        for j, chip in enumerate(chips):
            copy(1 + j, (*chip, c), me).wait_recv()
            passed[j].start()
        copy(0, sibling, me).wait_recv()
        for j, chip in enumerate(chips):
            copy(4 + j, (*chip, 1 - c), me).wait_recv()
        for cp in first + passed:
            cp.wait_send()
        mine.wait()

    return pl.pallas_call(
        body, out_shape=SDS((N_DEV,) + block.shape, block.dtype),
        in_specs=[pl.BlockSpec(memory_space=pl.ANY)], out_specs=pl.BlockSpec(memory_space=pl.ANY),
        scratch_shapes=[pltpu.SemaphoreType.DMA((7,)), pltpu.SemaphoreType.DMA((7,)), pltpu.SemaphoreType.DMA],
        name=name,
    )(block)


HBM_SPEC = pl.BlockSpec(memory_space=pltpu.HBM)
SEM_SPEC = pl.BlockSpec(memory_space=pltpu.SEMAPHORE)
SPLIT_COPY = pltpu.CompilerParams(has_side_effects=pltpu.SideEffectType.DATAFLOW_SIDE_EFFECTING)


def _in_hbm(t):
    return pltpu.with_memory_space_constraint(t, pltpu.HBM)


def _pair_copies(g_refs, land_refs, send_sems, recv_sems):
    x, y, c = _position()
    return [pltpu.make_async_remote_copy(
        src_ref=g.at[2 * k + (1 - c)], dst_ref=land.at[k], send_sem=send_sems.at[4 * a + k],
        recv_sem=recv_sems.at[4 * a + k], device_id=(x, y, 1 - c), device_id_type=MESH)
        for a, (g, land) in enumerate(zip(g_refs, land_refs, strict=True)) for k in range(4)]


def _chip_copies(t_refs, land_refs, send_sems, recv_sems):
    x, y, c = _position()
    chips = [(1 - x, y), (x, 1 - y), (1 - x, 1 - y)]
    return [pltpu.make_async_remote_copy(
        src_ref=t.at[2 * px + py], dst_ref=land.at[j], send_sem=send_sems.at[3 * a + j],
        recv_sem=recv_sems.at[3 * a + j], device_id=(px, py, c), device_id_type=MESH)
        for a, (t, land) in enumerate(zip(t_refs, land_refs, strict=True)) for j, (px, py) in enumerate(chips)]


_ROUNDS = {"pair": (_pair_copies, 4), "chip": (_chip_copies, 3)}


def _exchange_start(kind, ts, name):
    copies, slots = _ROUNDS[kind]
    n = len(ts)
    lands = [_in_hbm(lax.empty((slots,) + t.shape[1:], t.dtype)) for t in ts]

    def body(*refs):
        for cp in copies(refs[:n], refs[n:2 * n], refs[2 * n], refs[2 * n + 1]):
            cp.start()
        refs[-1][...] = jnp.zeros_like(refs[-1])

    sems = pltpu.SemaphoreType.DMA((slots * n,))
    res = pl.pallas_call(
        body, name=name, in_specs=[HBM_SPEC] * (2 * n),
        out_shape=(sems, sems, *[pltpu.HBM(t.shape, t.dtype) for t in (*ts, *lands)], SDS((8, LANES), f32)),
        out_specs=(SEM_SPEC, SEM_SPEC, *[HBM_SPEC] * (2 * n), pl.BlockSpec(memory_space=pltpu.VMEM)),
        input_output_aliases={i: 2 + i for i in range(2 * n)}, compiler_params=SPLIT_COPY,
    )(*[_in_hbm(t) for t in ts], *lands)
    return res[:-1], res[-1]


def _exchange_wait(kind, state, after, name):
    copies, _ = _ROUNDS[kind]
    send_sems, recv_sems, *arrays = state
    n = len(arrays) // 2

    def body(*refs):
        for cp in copies(refs[:n], refs[n:2 * n], refs[2 * n], refs[2 * n + 1]):
            cp.wait_send()
            cp.wait_recv()

    res = pl.pallas_call(
        body, name=name, in_specs=[HBM_SPEC] * (2 * n) + [SEM_SPEC, SEM_SPEC, pl.BlockSpec(memory_space=pl.ANY)],
        out_shape=[pltpu.HBM(t.shape, t.dtype) for t in arrays], out_specs=[HBM_SPEC] * (2 * n),
        input_output_aliases={i: i for i in range(2 * n)}, compiler_params=SPLIT_COPY,
    )(*arrays, send_sems, recv_sems, after)
    return res[:n], res[n:]


def _gather_copies(x_refs, out_refs, send_sems, recv_sems):
    x, y, c = _position()
    peers = [(x, y, 1 - c), (1 - x, y, c), (x, 1 - y, c), (1 - x, 1 - y, c)]
    sends, arrivals = [], []
    for a, (x_ref, out_ref) in enumerate(zip(x_refs, out_refs, strict=True)):
        for k, (px, py, pc) in enumerate(peers):
            sems = dict(send_sem=send_sems.at[4 * a + k], recv_sem=recv_sems.at[4 * a + k],
                        device_id=(px, py, pc), device_id_type=MESH)
            sends.append(pltpu.make_async_remote_copy(src_ref=x_ref, dst_ref=out_ref.at[4 * x + 2 * y + c], **sems))
            arrivals.append(pltpu.make_async_remote_copy(src_ref=x_ref, dst_ref=out_ref.at[4 * px + 2 * py + pc],
                                                         **sems))
    return sends, arrivals


def _gather_start(blocks, after, name):
    n = len(blocks)
    outs = [_in_hbm(lax.empty((N_DEV,) + b.shape, b.dtype)) for b in blocks]

    def body(*refs):
        sends, _ = _gather_copies(refs[:n], refs[n:2 * n], refs[2 * n + 1], refs[2 * n + 2])
        for cp in sends:
            cp.start()
        refs[-1][...] = jnp.zeros_like(refs[-1])

    sems = pltpu.SemaphoreType.DMA((4 * n,))
    res = pl.pallas_call(
        body, name=name, in_specs=[HBM_SPEC] * (2 * n) + [pl.BlockSpec(memory_space=pl.ANY)],
        out_shape=(sems, sems, *[pltpu.HBM(t.shape, t.dtype) for t in (*blocks, *outs)], SDS((8, LANES), f32)),
        out_specs=(SEM_SPEC, SEM_SPEC, *[HBM_SPEC] * (2 * n), pl.BlockSpec(memory_space=pltpu.VMEM)),
        input_output_aliases={i: 2 + i for i in range(2 * n)}, compiler_params=SPLIT_COPY,
    )(*[_in_hbm(b) for b in blocks], *outs, after)
    return res[:-1], res[-1]


def _gather_wait(state, after, name):
    send_sems, recv_sems, *arrays = state
    n = len(arrays) // 2

    def body(*refs):
        sends, arrivals = _gather_copies(refs[:n], refs[n:2 * n], refs[2 * n], refs[2 * n + 1])
        for cp in sends:
            cp.wait_send()
        for cp in arrivals:
            cp.wait_recv()

    res = pl.pallas_call(
        body, name=name, in_specs=[HBM_SPEC] * (2 * n) + [SEM_SPEC, SEM_SPEC, pl.BlockSpec(memory_space=pl.ANY)],
        out_shape=[pltpu.HBM(t.shape, t.dtype) for t in arrays], out_specs=[HBM_SPEC] * (2 * n),
        input_output_aliases={i: i for i in range(2 * n)}, compiler_params=SPLIT_COPY,
    )(*arrays, send_sems, recv_sems, after)
    return res[:n], res[n:]


def _gather_finish(partial, name):
    n = len(partial)

    def body(*refs):
        in_refs, out_refs = refs[:n], refs[n:2 * n]
        send_sems, recv_sems = refs[2 * n:]
        x, y, c = _position()
        chips = [(1 - x, y), (x, 1 - y), (1 - x, 1 - y)]
        copies = []
        for a in range(n):
            for j, (px, py) in enumerate(chips):
                cp = pltpu.make_async_remote_copy(
                    src_ref=in_refs[a].at[4 * px + 2 * py + c], dst_ref=out_refs[a].at[4 * px + 2 * py + c],
                    send_sem=send_sems.at[a, j], recv_sem=recv_sems.at[a, j], device_id=(x, y, 1 - c),
                    device_id_type=MESH)
                cp.start()
                copies.append(cp)
        for a in range(n):
            for j, (px, py) in enumerate(chips):
                pltpu.make_async_remote_copy(
                    src_ref=in_refs[a].at[4 * px + 2 * py + (1 - c)], dst_ref=out_refs[a].at[4 * px + 2 * py + (1 - c)],
                    send_sem=send_sems.at[a, j], recv_sem=recv_sems.at[a, j], device_id=(x, y, 1 - c),
                    device_id_type=MESH).wait_recv()
        for cp in copies:
            cp.wait_send()

    hbm = pl.BlockSpec(memory_space=pl.ANY)
    return pl.pallas_call(
        body, out_shape=[SDS(p.shape, p.dtype) for p in partial], in_specs=[hbm] * n, out_specs=[hbm] * n,
        input_output_aliases={a: a for a in range(n)},
        scratch_shapes=[pltpu.SemaphoreType.DMA((n, 3)), pltpu.SemaphoreType.DMA((n, 3))],
        name=name,
    )(*partial)


def _row_tile(rows):
    return 512 if rows % 512 == 0 and rows > 512 else rows


def _pair_add(g, r1, core, name):
    def body(c_ref, g_ref, r_ref, o_ref):
        o_ref[...] = (g_ref[...].astype(f32) + r_ref[...].astype(f32)).astype(o_ref.dtype)

    rows, cols = g.shape[1:]
    tile = _row_tile(rows)
    blk = (1, tile, cols)
    return pl.pallas_call(
        body, out_shape=SDS((4, rows, cols), g.dtype), name=name,
        grid_spec=pltpu.PrefetchScalarGridSpec(
            num_scalar_prefetch=1, grid=(4, rows // tile),
            in_specs=[pl.BlockSpec(blk, lambda k, i, c_ref: (2 * k + c_ref[0], i, 0)),
                      pl.BlockSpec(blk, lambda k, i, c_ref: (k, i, 0))],
            out_specs=pl.BlockSpec(blk, lambda k, i, c_ref: (k, i, 0))),
        compiler_params=_params(("parallel", "arbitrary")),
    )(core, g, r1)


def _chip_add(t, r2, chip, name, transposed=False):
    def body(c_ref, t_ref, r_ref, o_ref):
        s = ((t_ref[0].astype(f32) + r_ref[0].astype(f32)) + r_ref[1].astype(f32)) + r_ref[2].astype(f32)
        o_ref[...] = s.T if transposed else s

    rows, cols = t.shape[1:]
    tile = _row_tile(rows)
    out_spec = pl.BlockSpec((cols, tile), lambda i, c_ref: (0, i)) if transposed else pl.BlockSpec(
        (tile, cols), lambda i, c_ref: (i, 0))
    return pl.pallas_call(
        body, out_shape=SDS((cols, rows) if transposed else (rows, cols), f32), name=name,
        grid_spec=pltpu.PrefetchScalarGridSpec(
            num_scalar_prefetch=1, grid=(rows // tile,),
            in_specs=[pl.BlockSpec((1, tile, cols), lambda i, c_ref: (c_ref[0], i, 0)),
                      pl.BlockSpec((3, tile, cols), lambda i, c_ref: (0, i, 0))],
            out_specs=out_spec),
        compiler_params=_params(("arbitrary",)),
    )(chip, t, r2)


def _pad_to(t, axis, size):
    pads = [(0, 0)] * t.ndim
    pads[axis] = (0, size - t.shape[axis])
    return jnp.pad(t, pads)


_REF_COLS = {"qa": (0, FOX_W), "ka": (FOX_W, FOX_W), "va": (2 * FOX_W, FOX_W), "f": (3 * FOX_W, N_FOX_HEADS)}
_REF_COLS.update({n: (3 * FOX_W + N_FOX_HEADS + i * DIL_W, DIL_W) for i, n in enumerate(("qb", "kb", "vb"))})
_REF_COLS.update({n: (3 * FOX_W + N_FOX_HEADS + 3 * DIL_W + i * D, D) for i, n in enumerate(("ga", "gb"))})
_REF_ORDER = ("qa", "ka", "va", "f", "qb", "kb", "vb", "ga", "gb")


def _place_cols(sources, src_of, out_cols, name, row_block=512):
    arrays = [s[0] if isinstance(s, tuple) else s for s in sources]
    widths = [a.shape[-1] for a in arrays]
    rows = arrays[0].shape[-2]
    plan = []
    for t in range(out_cols // LANES):
        segs, c, end = [], t * LANES, (t + 1) * LANES
        while c < end:
            s = src_of(c)
            if s is None:
                c += 1
                continue
            n = 1
            while c + n < end and src_of(c + n) == (s[0], s[1] + n):
                n += 1
            segs.append((s[0], s[1], c - t * LANES, n))
            c += n
        plan.append(segs)

    def body(*refs):
        o_ref = refs[-1]
        for t, segs in enumerate(plan):
            acc = None
            for si, c0, o0, n in segs:
                a0 = c0 // LANES * LANES
                wide = min(2 * LANES, widths[si] - a0)
                win = refs[si][0, :, a0:a0 + wide] if isinstance(sources[si], tuple) else refs[si][:, a0:a0 + wide]
                r = lax.broadcasted_iota(jnp.int32, (wide, LANES), 0)
                c = lax.broadcasted_iota(jnp.int32, (wide, LANES), 1)
                pick = ((r - (c0 - a0) == c - o0) & (c >= o0) & (c < o0 + n)).astype(bf16)
                part = jnp.dot(win.astype(bf16), pick, preferred_element_type=f32)
                acc = part if acc is None else acc + part
            tile = jnp.zeros((row_block, LANES), f32) if acc is None else acc
            o_ref[:, t * LANES:(t + 1) * LANES] = tile.astype(o_ref.dtype)

    def spec(s):
        if isinstance(s, tuple):
            j = s[1]
            return pl.BlockSpec((1, row_block, s[0].shape[-1]), lambda i: (j, i, 0))
        return pl.BlockSpec((row_block, s.shape[-1]), lambda i: (i, 0))

    return pl.pallas_call(
        body, grid=(rows // row_block,), in_specs=[spec(s) for s in sources],
        out_specs=pl.BlockSpec((row_block, out_cols), lambda i: (i, 0)), out_shape=SDS((rows, out_cols), bf16),
        name=name, compiler_params=_params(("parallel",)),
    )(*arrays)


def _ref_piece(r):
    for name in _REF_ORDER:
        lo, width = _REF_COLS[name]
        if lo <= r < lo + width:
            return name, r - lo
    raise ValueError(r)


def _shard_pad_cols(pieces):
    names = [n for n in _REF_ORDER if n != "vb"]
    sources = [pieces[n] for n in names] + list(pieces["vb"])

    def src_of(c):
        j, i = divmod(c, W_IN_PAD)
        if i >= W_IN_SH:
            return None
        name, col = _ref_piece(j * W_IN_SH + i)
        if name == "vb":
            return len(names) + col // DIL_OUT_W, col % DIL_OUT_W
        return names.index(name), col

    return _place_cols(sources, src_of, N_DEV * W_IN_PAD, "place_dproj")


_SLABS = {"ga": C_GA, "gb": C_GB, "qb": C_QB, "kb": C_KB, "vb": C_VB, "qa": C_QA, "ka": C_KA, "va": C_VA, "f": C_F}


def _slab_w_in(stack):
    def src_of(c):
        for name, start in _SLABS.items():
            lo, width = _REF_COLS[name]
            if start <= c < start + width:
                return divmod(lo + c - start, W_IN_SH)
        return None

    return _place_cols([(stack, j) for j in range(N_DEV)], src_of, PROJ_W, "place_w_in")


def kernel(x, c, w_ada, b_ada, g_mix, w_in, b_fgate, w_br_a, w_br_b, w_out, g_ffn, w_ffn_gate, w_ffn_up, w_ffn_down, g_final, loss_target, m_w_ada, m_b_ada, m_g_mix, m_w_in, m_b_fgate, m_w_br_a, m_w_br_b, m_w_out, m_g_ffn, m_w_ffn_gate, m_w_ffn_up, m_w_ffn_down, m_g_final, v_w_ada, v_b_ada, v_g_mix, v_w_in, v_b_fgate, v_w_br_a, v_w_br_b, v_w_out, v_g_ffn, v_w_ffn_gate, v_w_ffn_up, v_w_ffn_down, v_g_final):
    px, py, pc = _position()
    dev = 4 * px + 2 * py + pc
    x2d, tgt = x[0], loss_target[0]

    c_all = _all_gather(c, "gather_c").reshape(N_DEV, D)
    ada_cols = w_ada.shape[2]
    b_shard = lax.dynamic_slice(b_ada, (0, dev * ada_cols), (1, ada_cols))
    mod_shard = _ada_fwd(c_all, w_ada[0], b_shard)
    mod_all = _all_gather(mod_shard, "gather_mod")
    modv = lax.dynamic_index_in_dim(mod_all, dev, axis=1, keepdims=False).reshape(6, D)
    h1 = _pre1(x2d, modv, g_mix)

    w_in_s = _all_gather(_pad_to(w_in[0], 1, W_IN_PAD).astype(bf16), "gather_w_in")
    gate_up = jnp.concatenate([_pad_to(w_ffn_gate[0], 1, FF_PAD), _pad_to(w_ffn_up[0], 1, FF_PAD)], axis=1)
    later = [w_br_a[0], w_br_b[0], w_out[0], gate_up, _pad_to(w_ffn_down[0], 0, FF_PAD)]
    later_state, later_token = _gather_start([t.astype(bf16) for t in later], w_in_s, "gather_rest_start")
    w_in_p = _slab_w_in(w_in_s)

    proj = _matmul(h1, w_in_p, name="mm_proj", tm=SEQ, tn=896, after=later_token)
    b_pad = jnp.pad(b_fgate, ((0, 0), (0, LANES - N_FOX_HEADS)))
    q_aug, k_aug, va = _fox_prep(proj, _fox_gate_fwd(proj, b_pad))
    ya_h, max_a, sum_a = _fox_fwd(q_aug, k_aug, va)

    tables = _rope_tables()
    qb_r, kb_r = _rope_fwd(proj, tables)
    by_group = [_dil_fwd(qb_r, kb_r, proj, grp) for grp in range(N_GROUPS)]
    yb_h, lse_b = _dil_combine([o for o, _ in by_group], [l for _, l in by_group])

    both_done = ya_h[:8, :LANES] + yb_h[:8, :LANES]
    mine, arrived = _gather_wait(later_state, both_done, "gather_rest_wait")
    w_a_s, w_b_s, w_o_s, w_gu_s, w_d_s = [
        lax.dynamic_update_slice(stack, block[None], (dev, 0, 0))
        for stack, block in zip(_gather_finish(arrived, "gather_rest_finish"), mine, strict=True)]
    w_o = w_o_s.reshape(D, D)
    w_d = w_d_s.reshape(FF_HID, D)
    ya = _matmul_stack(ya_h, w_a_s, name="mm_br_a")
    yb = _matmul_stack(yb_h, w_b_s, name="mm_br_b")

    merged, mix, x1, h2 = _post1(ya, yb, proj, w_o, x2d, modv, g_ffn)
    act, au = _ffn_in(h2, w_gu_s)

    dx2, dff, dg_final, dga_f, loss_lanes = _final(act, w_d, x1, tgt, modv, g_final.reshape(1, D))
    dau = _ffn_bwd_in(dff, w_d_s, au)

    core = pc.astype(jnp.int32).reshape(1)
    chip = (2 * px + py).astype(jnp.int32).reshape(1)

    def pair_done(state, after, tags, name):
        mine, theirs = _exchange_wait("pair", state, after, "pair_wait_" + name)
        sums = [_pair_add(g, r, core, "pair_add_" + t) for g, r, t in zip(mine, theirs, tags)]
        return _exchange_start("chip", sums, "chip_start_" + name)

    def from_chips(state, after, tags, name, transposed=None):
        sums, got = _exchange_wait("chip", state, after, "chip_wait_" + name)
        flips = transposed or [False] * len(tags)
        return [_chip_add(p, r, chip, "chip_add_" + t, f) for p, r, t, f in zip(sums, got, tags, flips)]

    g_gu = _matmul(h2, dau, ta=True, by_shard=True, out_dtype=bf16, name="mm_g_ffn_in", tm=D, tn=2 * FF_PAD)
    g_d = _matmul(act, dff, ta=True, out_dtype=bf16, name="mm_g_down", tm=FF_HID // 2, tn=512)
    ffn_tags = ["gu", "down"]
    ffn_pair, ffn_pair_token = _exchange_start("pair", [g_gu, g_d.reshape(N_DEV, FF_PAD, D)], "pair_start_ffn")

    dx1, dmix, dsh_f, dsc_f, dg_ffn, dga_m = _mid_bwd(dau, w_gu_s, ffn_pair_token, x1, dx2, mix, modv, g_ffn)
    ffn_state, ffn_token = pair_done(ffn_pair, dx1, ffn_tags, "ffn")
    dya, dyb, dga, dgb = _merge_bwd(dmix, w_o, ffn_token, ya, yb, proj)
    dya_h = _matmul_stack(dya, w_a_s, tb=True, name="mm_d_ya")
    dyb_h = _matmul_stack(dyb, w_b_s, tb=True, name="mm_d_yb")

    g_o = _matmul(merged, dmix, ta=True, out_dtype=bf16, name="mm_g_out", tm=D, tn=512)
    g_a = _matmul_stack(ya_h, dya, ta=True, out_dtype=bf16, name="mm_g_br_a")
    g_b = _matmul_stack(yb_h, dyb, ta=True, out_dtype=bf16, name="mm_g_br_b")
    rows_a, rows_b = FOX_W * W_BR_SH // D, DIL_OUT_W * W_BR_SH // D
    g_small = jnp.concatenate([g_a.reshape(N_DEV, rows_a, D), g_b.reshape(N_DEV, rows_b, D),
                               g_o.reshape(N_DEV, W_BR_SH, D)], axis=1)
    small_pair, small_pair_token = _exchange_start("pair", [g_small], "pair_start_small")

    dqa, dka, dva, dF = _fox_bwd(q_aug, k_aug, va, dya_h, ya_h, max_a, sum_a, small_pair_token)
    dF_row = jnp.pad(dF[:, :2, :].reshape(N_FOX_HEADS, SEQ), ((0, LANES - N_FOX_HEADS), (0, 0)))
    df, db_fgate = _fox_gate_bwd(dF_row, proj, b_pad)
    small_state, small_token = pair_done(small_pair, df, ["small"], "small")

    delta_b = _dil_delta(dyb_h, yb_h)
    dil_grads = [_dil_bwd(qb_r, kb_r, proj, dyb_h, lse_b, delta_b, grp) for grp in range(N_GROUPS)]
    dqb, dkb = _rope_bwd([t[0] for t in dil_grads], [t[1] for t in dil_grads], tables)

    dproj = _shard_pad_cols({"qa": dqa, "ka": dka, "va": dva, "f": df, "qb": dqb, "kb": dkb,
                             "vb": [t[2] for t in dil_grads], "ga": dga, "gb": dgb})
    g_in = _matmul(h1, dproj, ta=True, by_shard=True, out_dtype=bf16, name="mm_g_in", tm=D, tn=W_IN_PAD,
                   after=small_token)
    mix_tags = ["in"]
    mix_pair, mix_pair_token = _exchange_start("pair", [g_in], "pair_start_mixer")

    grad_x, dsh_m, dsc_m, dg_mix = _first_bwd(dproj, w_in_s, mix_pair_token, x2d, dx1, modv, g_mix)

    pad_lane = lambda t: jnp.pad(t, ((0, 0), (0, D - t.shape[1])))
    small = jnp.concatenate([dsh_m, dsc_m, dga_m, dsh_f, dsc_f, dga_f, dg_mix, dg_ffn, dg_final,
                             pad_lane(db_fgate), loss_lanes, jnp.zeros((SMALL_ROWS - 11, D), f32)], axis=0)
    small_all = _all_gather(small, "gather_small")
    mix_state, mix_token = pair_done(mix_pair, small_all, mix_tags, "mixer")

    small_sum, loss_row = _small_reduce(small_all, mix_token)
    dmod_all = small_all[:, :6, :].reshape(N_DEV, 6 * D)
    g_w_ada = _ada_bwd(c_all, lax.dynamic_slice(dmod_all, (0, dev * ada_cols), (N_DEV, ada_cols)))
    s_gu_t, s_d = from_chips(ffn_state, small_sum, ffn_tags, "ffn", [True, False])
    s_small, = from_chips(small_state, small_sum, ["small"], "small")

    loss = loss_row[0, 0]
    g = {
        "w_ada": g_w_ada[None], "b_ada": small_sum[0:6].reshape(1, 6 * D), "g_mix": small_sum[6:7],
        "b_fgate": small_sum[9:10, :N_FOX_HEADS], "g_ffn": small_sum[7:8], "w_ffn_gate": s_gu_t[:W_FF_SH],
        "w_ffn_up": s_gu_t[FF_PAD:FF_PAD + W_FF_SH], "w_ffn_down": s_d[None, :W_FF_SH],
        "g_final": small_sum[8], "w_br_a": s_small[:rows_a].reshape(1, FOX_W, W_BR_SH),
        "w_br_b": s_small[rows_a:rows_a + rows_b].reshape(1, DIL_OUT_W, W_BR_SH), "w_out": s_small[None, rows_a + rows_b:],
    }
    w = {"w_ada": w_ada, "b_ada": b_ada, "g_mix": g_mix, "w_in": w_in, "b_fgate": b_fgate, "w_br_a": w_br_a,
         "w_br_b": w_br_b, "w_out": w_out, "g_ffn": g_ffn, "w_ffn_gate": w_ffn_gate, "w_ffn_up": w_ffn_up,
         "w_ffn_down": w_ffn_down, "g_final": g_final}
    m = {"w_ada": m_w_ada, "b_ada": m_b_ada, "g_mix": m_g_mix, "w_in": m_w_in, "b_fgate": m_b_fgate,
         "w_br_a": m_w_br_a, "w_br_b": m_w_br_b, "w_out": m_w_out, "g_ffn": m_g_ffn, "w_ffn_gate": m_w_ffn_gate,
         "w_ffn_up": m_w_ffn_up, "w_ffn_down": m_w_ffn_down, "g_final": m_g_final}
    v = {"w_ada": v_w_ada, "b_ada": v_b_ada, "g_mix": v_g_mix, "w_in": v_w_in, "b_fgate": v_b_fgate,
         "w_br_a": v_w_br_a, "w_br_b": v_w_br_b, "w_out": v_w_out, "g_ffn": v_g_ffn, "w_ffn_gate": v_w_ffn_gate,
         "w_ffn_up": v_w_ffn_up, "w_ffn_down": v_w_ffn_down, "g_final": v_g_final}
    names = list(w)
    delta, new_m, new_v = {}, {}, {}

    transposed = ("w_ffn_gate", "w_ffn_up")
    by_column = lambda t: jnp.transpose(t, (2, 0, 1))
    by_row = lambda t: jnp.transpose(t, (1, 2, 0))

    def update(n):
        shape = w[n].shape
        if n == "w_in":
            g3 = g[n][:, None, :]
            dl, mn, vn = _adamw_by_planes(by_column(w[n]), g3, by_column(m[n]), by_column(v[n]), "adamw_" + n)
            g[n], delta[n], new_m[n], new_v[n] = by_row(g3), by_row(dl), by_row(mn), by_row(vn)
            return
        if n in transposed:
            g_t = g[n]
            dl, mn, vn = _adamw(w[n][0].T, g_t, m[n][0].T, v[n][0].T, "adamw_" + n)
            g[n], delta[n], new_m[n], new_v[n] = g_t.T[None], dl.T[None], mn.T[None], vn.T[None]
            return
        two_d = (lambda t: t.reshape(shape[-2:])) if len(shape) == 3 else (lambda t: t)
        dl, mn, vn = _adamw(two_d(w[n]), two_d(g[n]), two_d(m[n]), two_d(v[n]), "adamw_" + n)
        delta[n], new_m[n], new_v[n] = dl.reshape(shape), mn.reshape(shape), vn.reshape(shape)

    for n in list(g):
        update(n)
    done = sum(delta[n][(0,) * (delta[n].ndim - 1)][:N_FOX_HEADS] for n in g)
    s_in_t, = from_chips(mix_state, done, mix_tags, "mixer", [True])
    g["w_in"] = s_in_t[:W_IN_SH]
    update("w_in")

    return (loss, grad_x[None], *[g[n] for n in names], *[delta[n] for n in names],
            *[new_m[n] for n in names], *[new_v[n] for n in names])
```

```python
import jax
import jax.numpy as jnp
import numpy as np
from jax import lax
from jax.experimental import pallas as pl
from jax.experimental.pallas import tpu as pltpu

f32 = jnp.float32
bf16 = jnp.bfloat16
SDS = jax.ShapeDtypeStruct
MESH = pl.DeviceIdType.MESH

N_DEV = 8
D = 1024
SEQ = 2048
HEAD_DIM = 64
N_FOX_HEADS = 8
FOX_W = 512
DIL_W = 768
DIL_OUT_W = 256
ROT_DIM = 16
ROPE_THETA = 500000.0
D_FF = 2816
IN_COLS = 5896
EPS = 1e-6
NEG = -1e30
ATT_SCALE = HEAD_DIM ** -0.5

ADAM_LR = 0.001
ADAM_B1 = 0.9
ADAM_B2 = 0.999
ADAM_EPS = 1e-08
ADAM_WD = 0.01
ADAM_STEP = 10

C_GA, C_GB, C_QB, C_KB, C_VB, C_QA, C_KA, C_VA, C_F = 0, 1024, 2304, 3072, 3840, 4608, 5120, 5632, 6144
PROJ_W = 6272
LANES = 128
VMEM_LIMIT = 52 * 1024 * 1024

W_IN_SH, W_IN_PAD = IN_COLS // N_DEV, 768
W_BR_SH = D // N_DEV
W_FF_SH, FF_PAD = D_FF // N_DEV, 384
FF_HID = N_DEV * FF_PAD
SMALL_ROWS = 16


def _params(sem=None):
    if sem is None:
        return pltpu.CompilerParams(vmem_limit_bytes=VMEM_LIMIT)
    return pltpu.CompilerParams(dimension_semantics=sem, vmem_limit_bytes=VMEM_LIMIT)


def _rowwise(fn, name, tiled, vecs, outs, reds=(), tile=256):
    nt, nv, no = len(tiled), len(vecs), len(outs)
    rows = tiled[0][0].shape[0]
    assert rows % tile == 0

    def body(*refs):
        tin = [r[...] for r in refs[:nt]]
        vin = [r[...] for r in refs[nt:nt + nv]]
        orefs = refs[nt + nv:nt + nv + no]
        rrefs = refs[nt + nv + no:]
        touts, routs = fn(tin, vin)
        for r, t in zip(orefs, touts, strict=True):
            r[...] = t.astype(r.dtype)
        if rrefs:
            @pl.when(pl.program_id(0) == 0)
            def _():
                for r in rrefs:
                    r[...] = jnp.zeros_like(r)
            for r, t in zip(rrefs, routs, strict=True):
                r[...] += t

    def col_map(cb):
        return lambda i: (i, cb)

    def whole_map(nd):
        return lambda i: (0,) * nd

    in_specs = [pl.BlockSpec((tile, w), col_map(cb)) for (_, w, cb) in tiled]
    in_specs += [pl.BlockSpec(v.shape, whole_map(v.ndim)) for v in vecs]
    out_specs = [pl.BlockSpec((tile, w), lambda i: (i, 0)) for (w, _) in outs]
    out_specs += [pl.BlockSpec((1, w), lambda i: (0, 0)) for w in reds]
    out_shape = [SDS((rows, w), dt) for (w, dt) in outs] + [SDS((1, w), f32) for w in reds]
    res = pl.pallas_call(
        body, grid=(rows // tile,), in_specs=in_specs, out_specs=out_specs, out_shape=out_shape, name=name,
        compiler_params=_params(("arbitrary",)),
    )(*[t[0] for t in tiled], *vecs)
    return res


def _matmul(a, b, *, ta=False, out_dtype=f32, name, tm, tn, by_shard=False, after=None):
    (m, k), n = ((a.shape[1], a.shape[0]) if ta else a.shape), b.shape[1]
    assert b.shape[0] == k and m % tm == 0 and n % tn == 0 and (ta or not by_shard)
    dims = (((0 if ta else 1,), (0,)), ((), ()))

    def body(a_ref, b_ref, *rest):
        p = lax.dot_general(a_ref[...].astype(bf16), b_ref[...].astype(bf16), dims, preferred_element_type=f32)
        o_ref = rest[-1]
        if by_shard:
            o_ref[0] = p.astype(o_ref.dtype)
        else:
            o_ref[...] = p.astype(o_ref.dtype)

    a_spec = pl.BlockSpec((k, tm), lambda i, j: (0, i)) if ta else pl.BlockSpec((tm, k), lambda i, j: (i, 0))
    if by_shard:
        assert tn == n // N_DEV
        out_spec, out_shape = pl.BlockSpec((1, tm, tn), lambda i, j: (j, i, 0)), SDS((N_DEV, m, tn), out_dtype)
    else:
        out_spec, out_shape = pl.BlockSpec((tm, tn), lambda i, j: (i, j)), SDS((m, n), out_dtype)
    extra_specs, extra = ([pl.BlockSpec(memory_space=pl.ANY)], [after]) if after is not None else ([], [])
    return pl.pallas_call(
        body, grid=(m // tm, n // tn), in_specs=[a_spec, pl.BlockSpec((k, tn), lambda i, j: (0, j))] + extra_specs,
        out_specs=out_spec, out_shape=out_shape, name=name, compiler_params=_params(("parallel", "parallel")),
    )(a, b, *extra)


def _matmul_stack(a, b, *, ta=False, tb=False, out_dtype=f32, name):
    def lanes(ref):
        return jnp.concatenate([ref[j] for j in range(N_DEV)], axis=1).astype(bf16)

    if ta:
        w = b.shape[1] // N_DEV

        def body(a_ref, b_ref, o_ref):
            p = _tn(a_ref[...].astype(bf16), b_ref[...].astype(bf16))
            for j in range(N_DEV):
                o_ref[j] = p[:, j * w:(j + 1) * w].astype(o_ref.dtype)

        return pl.pallas_call(body, out_shape=SDS((N_DEV, a.shape[1], w), out_dtype), name=name,
                              compiler_params=_params())(a, b)

    m, half = a.shape[0], a.shape[0] // 2
    n = b.shape[1] if tb else N_DEV * b.shape[2]

    def body(a_ref, b_ref, o_ref):
        av = a_ref[...].astype(bf16)
        o_ref[...] = (_nt(av, lanes(b_ref)) if tb else jnp.dot(av, lanes(b_ref), preferred_element_type=f32)
                      ).astype(o_ref.dtype)

    return pl.pallas_call(
        body, grid=(2,), in_specs=[pl.BlockSpec((half, a.shape[1]), lambda i: (i, 0)),
                                   pl.BlockSpec(b.shape, lambda i: (0, 0, 0))],
        out_specs=pl.BlockSpec((half, n), lambda i: (i, 0)), out_shape=SDS((m, n), out_dtype), name=name,
        compiler_params=_params(("parallel",)),
    )(a, b)


def _matmul_rows(form, a, b, after, fn, tiled, vecs, outs, reds, *, name, tm=512):
    norm = lambda ts: [t if isinstance(t, tuple) else (t, t.shape[1], 0) for t in ts]
    make, sources = a if isinstance(a, tuple) else (None, [a])
    sources, tiled = norm(sources), norm(tiled)
    m, k = sources[0][0].shape[0], (b.shape[0] if form == "nn" else b.shape[-1] * (N_DEV if form == "nt_stack" else 1))
    assert m % tm == 0
    ns, nt, nv, no = len(sources), len(tiled), len(vecs), len(outs)

    def body(*refs):
        src_refs, b_ref, refs = refs[:ns], refs[ns], refs[ns + 2:]
        if make is None:
            lhs = lambda lo, hi: src_refs[0][:, lo:hi]
        else:
            made = make([r[...] for r in src_refs]).astype(bf16)
            lhs = lambda lo, hi: made[:, lo:hi]
        if form == "nt_stack":
            w = b.shape[2]
            acc = _nt(lhs(0, w), b_ref[0])
            for j in range(1, N_DEV):
                acc = acc + _nt(lhs(j * w, (j + 1) * w), b_ref[j])
        elif form == "nt":
            acc = _nt(lhs(0, k), b_ref[...])
        else:
            acc = jnp.dot(lhs(0, k), b_ref[...], preferred_element_type=f32)
        if make is not None:
            refs[nt + nv][...] = made
            refs = refs[:nt + nv] + refs[nt + nv + 1:]
        orefs, rrefs = refs[nt + nv:nt + nv + no], refs[nt + nv + no:]
        touts, routs = fn([acc] + [r[...] for r in refs[:nt]], [r[...] for r in refs[nt:nt + nv]])
        for r, t in zip(orefs, touts, strict=True):
            r[...] = t.astype(r.dtype)

        @pl.when(pl.program_id(0) == 0)
        def _():
            for r in rrefs:
                r[...] = jnp.zeros_like(r)
        for r, t in zip(rrefs, routs, strict=True):
            r[...] += t

    def whole_map(nd):
        return lambda i: (0,) * nd

    def rows(width, cb=0):
        return pl.BlockSpec((tm, width), lambda i: (i, cb))

    made_out = [(k, bf16)] if make is not None else []
    return pl.pallas_call(
        body, grid=(m // tm,),
        in_specs=[rows(width, cb) for _, width, cb in sources]
        + [pl.BlockSpec(b.shape, whole_map(b.ndim), pipeline_mode=pl.Buffered(1)), pl.BlockSpec(memory_space=pl.ANY)]
        + [rows(width, cb) for _, width, cb in tiled] + [pl.BlockSpec(v.shape, whole_map(v.ndim)) for v in vecs],
        out_specs=[rows(width) for width, _ in made_out + list(outs)]
        + [pl.BlockSpec((1, width), lambda i: (0, 0)) for width in reds],
        out_shape=[SDS((m, width), dt) for width, dt in made_out + list(outs)]
        + [SDS((1, width), f32) for width in reds], name=name,
        compiler_params=_params(("arbitrary",)),
    )(*[t[0] for t in sources], b, after, *[t[0] for t in tiled], *vecs)


def _rms(x):
    r = lax.rsqrt(jnp.mean(x * x, axis=-1, keepdims=True) + EPS)
    return r, x * r


def _rms_bwd(r, xn, dxn):
    return r * (dxn - xn * jnp.mean(dxn * xn, axis=-1, keepdims=True))


def _colsum(t):
    return jnp.sum(t, axis=0, keepdims=True)


def _sigmoid(x):
    return 0.5 * jnp.tanh(0.5 * x) + 0.5


def _modulated_norm(x, g, shift, scale):
    _, xn = _rms(x)
    return (xn * g) * (1.0 + scale) + shift


def _pre1(x, modv, g_mix):
    def fn(t, v):
        (xt,), (mv, g) = t, v
        return [_modulated_norm(xt, g, mv[0:1], mv[1:2])], []
    return _rowwise(fn, "pre1", [(x, D, 0)], [modv, g_mix], [(D, bf16)])[0]


def _post1(ya, yb, proj, w_o, x, modv, g_ffn):
    def merge(t):
        ya_t, yb_t, ga, gb = t
        return _sigmoid(ga) * ya_t + _sigmoid(gb) * yb_t

    def fn(t, v):
        (mt, xt), (mv, g) = t, v
        x1 = xt + mv[2:3] * mt
        return [mt, x1, _modulated_norm(x1, g, mv[3:4], mv[4:5])], []
    return _matmul_rows("nn", (merge, [ya, yb, (proj, D, C_GA // D), (proj, D, C_GB // D)]), w_o, x, fn, [x],
                        [modv, g_ffn], [(D, f32), (D, f32), (D, bf16)], [], name="post1")


def _ffn_in(h, w_stack):
    def body(h_ref, w_ref, act_ref, au_ref):
        p = jnp.dot(h_ref[...], w_ref[0], preferred_element_type=f32)
        a, u = p[:, :FF_PAD], p[:, FF_PAD:]
        act_ref[...] = (a * _sigmoid(a) * u).astype(act_ref.dtype)
        au_ref[...] = p.astype(au_ref.dtype)

    return pl.pallas_call(
        body, grid=(N_DEV,),
        in_specs=[pl.BlockSpec((SEQ, D), lambda j: (0, 0)), pl.BlockSpec((1, D, 2 * FF_PAD), lambda j: (j, 0, 0))],
        out_specs=[pl.BlockSpec((SEQ, FF_PAD), lambda j: (0, j)), pl.BlockSpec((SEQ, 2 * FF_PAD), lambda j: (0, j))],
        out_shape=(SDS((SEQ, FF_HID), bf16), SDS((SEQ, 2 * FF_HID), bf16)), name="ffn_in",
        compiler_params=_params(("parallel",)),
    )(h, w_stack)


def _ffn_bwd_in(dff, w_down_stack, au):
    def body(d_ref, w_ref, au_ref, o_ref):
        dact = _nt(d_ref[...], w_ref[0])
        p = au_ref[...].astype(f32)
        a, u = p[:, :FF_PAD], p[:, FF_PAD:]
        sg = _sigmoid(a)
        o_ref[...] = jnp.concatenate([dact * u * (sg * (1.0 + a * (1.0 - sg))), dact * (a * sg)],
                                     axis=1).astype(o_ref.dtype)

    return pl.pallas_call(
        body, grid=(N_DEV,),
        in_specs=[pl.BlockSpec((SEQ, D), lambda j: (0, 0)), pl.BlockSpec((1, FF_PAD, D), lambda j: (j, 0, 0)),
                  pl.BlockSpec((SEQ, 2 * FF_PAD), lambda j: (0, j))],
        out_specs=pl.BlockSpec((SEQ, 2 * FF_PAD), lambda j: (0, j)),
        out_shape=SDS((SEQ, 2 * FF_HID), bf16), name="ffn_bwd_in", compiler_params=_params(("parallel",)),
    )(dff, w_down_stack, au)


def _final(act, w_down, x1, target, modv, g_final):
    def fn(t, v):
        (fft, x1t, tgt), (mv, g) = t, v
        x2 = x1t + mv[5:6] * fft
        r, xn = _rms(x2)
        err = xn * g - tgt
        dy = err * (1.0 / D)
        dx2 = _rms_bwd(r, xn, dy * g)
        return [dx2, dx2 * mv[5:6]], [_colsum(dy * xn), _colsum(dx2 * fft), _colsum(err * err) * (0.5 / D)]
    return _matmul_rows("nn", act, w_down, x1, fn, [x1, target], [modv, g_final], [(D, f32), (D, bf16)], [D, D, D],
                        name="final")


def _mid_bwd(dau, w_stack, after, x1, dx2, mix, modv, g_ffn):
    def fn(t, v):
        (dh, x1t, dx2t, mt), (mv, g) = t, v
        r, xn = _rms(x1t)
        dn = dh * (1.0 + mv[4:5])
        dx1 = dx2t + _rms_bwd(r, xn, dn * g)
        return [dx1, dx1 * mv[2:3]], [_colsum(dh), _colsum(dh * (xn * g)), _colsum(dn * xn), _colsum(dx1 * mt)]
    return _matmul_rows("nt_stack", dau, w_stack, after, fn, [x1, dx2, mix], [modv, g_ffn], [(D, f32), (D, bf16)],
                        [D, D, D, D], name="mid_bwd")


def _first_bwd(dproj, w_stack, after, x, dx1, modv, g_mix):
    def fn(t, v):
        (dh, xt, dx1t), (mv, g) = t, v
        r, xn = _rms(xt)
        dn = dh * (1.0 + mv[1:2])
        return [dx1t + _rms_bwd(r, xn, dn * g)], [_colsum(dh), _colsum(dh * (xn * g)), _colsum(dn * xn)]
    return _matmul_rows("nt_stack", dproj, w_stack, after, fn, [x, dx1], [modv, g_mix], [(D, f32)], [D, D, D],
                        name="first_bwd")


def _merge_bwd(dmix, w_o, after, ya, yb, proj):
    def fn(t, v):
        dm, ya_t, yb_t, ga, gb = t
        sa, sb = _sigmoid(ga), _sigmoid(gb)
        return [dm * sa, dm * sb, dm * ya_t * (sa * (1.0 - sa)), dm * yb_t * (sb * (1.0 - sb))], []
    return _matmul_rows("nt", dmix, w_o, after, fn, [ya, yb, (proj, D, C_GA // D), (proj, D, C_GB // D)], [],
                        [(D, bf16), (D, bf16), (D, bf16), (D, bf16)], [], name="merge_bwd")


def _rope_tables():
    half = ROT_DIM // 2
    pos = np.arange(SEQ, dtype=np.float32)
    inv_freq = np.float32(ROPE_THETA) ** (-np.arange(0, ROT_DIM, 2, dtype=np.float32) / np.float32(ROT_DIM))
    ang = pos[:, None] * inv_freq[None, :].astype(np.float32)
    cos, sin = np.cos(ang).astype(np.float32), np.sin(ang).astype(np.float32)
    pad = np.zeros((SEQ, HEAD_DIM - ROT_DIM), np.float32)
    zero = np.zeros((SEQ, half), np.float32)
    c_head = np.concatenate([cos, cos, pad + 1.0], axis=1)
    lo_head = np.concatenate([-sin, zero, pad], axis=1)
    hi_head = np.concatenate([zero, sin, pad], axis=1)
    return tuple(jnp.asarray(np.concatenate([t, t], axis=1)) for t in (c_head, lo_head, hi_head))


def _over_heads(tables):
    return [jnp.tile(t, (1, DIL_W // LANES)) for t in tables]


def _rope_fwd(proj, tables):
    half = ROT_DIM // 2

    def fn(t, v):
        q, k = t[:2]
        c, lo, hi = _over_heads(t[2:])
        rot = lambda z: z * c + pltpu.roll(z, DIL_W - half, 1) * lo + pltpu.roll(z, half, 1) * hi
        return [rot(q) * ATT_SCALE, rot(k)], []
    return _rowwise(fn, "rope_fwd", [(proj, DIL_W, C_QB // DIL_W), (proj, DIL_W, C_KB // DIL_W)]
                    + [(tb, LANES, 0) for tb in tables], [], [(DIL_W, f32)] * 2)


def _rope_bwd(dqs, dks, tables):
    half = ROT_DIM // 2

    def fn(t, v):
        dq_t, dk_t = jnp.concatenate(t[:N_GROUPS], axis=1), jnp.concatenate(t[N_GROUPS:2 * N_GROUPS], axis=1)
        c, lo, hi = _over_heads(t[2 * N_GROUPS:])
        rot_t = lambda z: z * c + pltpu.roll(z * lo, half, 1) + pltpu.roll(z * hi, DIL_W - half, 1)
        return [rot_t(dq_t), rot_t(dk_t)], []
    return _rowwise(fn, "rope_bwd", [(a, DIL_OUT_W, 0) for a in (*dqs, *dks)] + [(tb, LANES, 0) for tb in tables],
                    [], [(DIL_W, bf16), (DIL_W, bf16)])


def _head_bcast_sum(d):
    lane = lax.broadcasted_iota(jnp.int32, d.shape, 1)
    out = jnp.zeros_like(d)
    for h in range(d.shape[1] // HEAD_DIM):
        sel = (lane >= h * HEAD_DIM) & (lane < (h + 1) * HEAD_DIM)
        out = jnp.where(sel, jnp.sum(jnp.where(sel, d, 0.0), axis=1, keepdims=True), out)
    return out


def _dil_combine(outs, lses):
    def fn(t, v):
        o0, o1, o2, l0, l1, l2 = t
        m = jnp.maximum(jnp.maximum(l0, l1), l2)
        w0, w1, w2 = jnp.exp(l0 - m), jnp.exp(l1 - m), jnp.exp(l2 - m)
        tot = w0 + w1 + w2
        return [(w0 * o0 + w1 * o1 + w2 * o2) / tot, m + jnp.log(tot)], []
    w = DIL_OUT_W
    return _rowwise(fn, "dil_combine", [(t, w, 0) for t in (*outs, *lses)], [], [(w, f32), (w, f32)])


def _dil_delta(dyb_h, yb_h):
    def fn(t, v):
        return [_head_bcast_sum(t[0] * t[1])], []
    return _rowwise(fn, "dil_delta", [(dyb_h, DIL_OUT_W, 0), (yb_h, DIL_OUT_W, 0)], [], [(DIL_OUT_W, f32)])[0]


def _adamw_math(wt, gt, mt, vt):
    mn = ADAM_B1 * mt + (1.0 - ADAM_B1) * gt
    vn = ADAM_B2 * vt + (1.0 - ADAM_B2) * (gt * gt)
    m_hat = mn / (1.0 - ADAM_B1 ** ADAM_STEP)
    v_hat = vn / (1.0 - ADAM_B2 ** ADAM_STEP)
    return -ADAM_LR * (m_hat / (jnp.sqrt(v_hat) + ADAM_EPS) + ADAM_WD * wt), mn, vn


def _adamw(w, g, m, v, name):
    shape = w.shape
    if w.ndim == 1:
        w, g, m, v = (t.reshape(1, -1) for t in (w, g, m, v))
    rows, cols = w.shape
    tile = 256 if rows % 256 == 0 and rows > 512 else rows

    def fn(t, _):
        return list(_adamw_math(*t)), []
    delta, mn, vn = _rowwise(fn, name, [(w, cols, 0), (g, cols, 0), (m, cols, 0), (v, cols, 0)], [],
                             [(cols, f32)] * 3, tile=tile)
    return delta.reshape(shape), mn.reshape(shape), vn.reshape(shape)


def _adamw_by_planes(w, g, m, v, name, most=128):
    planes, _, width = w.shape
    tile = max(t for t in range(1, most + 1) if planes % t == 0)

    def body(w_ref, g_ref, m_ref, v_ref, d_ref, mn_ref, vn_ref):
        d_ref[...], mn_ref[...], vn_ref[...] = _adamw_math(w_ref[...], g_ref[...], m_ref[...], v_ref[...])

    spec = pl.BlockSpec((tile, 1, width), lambda i: (i, 0, 0))
    return pl.pallas_call(body, grid=(planes // tile,), in_specs=[spec] * 4, out_specs=[spec] * 3,
                          out_shape=[SDS(w.shape, f32)] * 3, name=name,
                          compiler_params=_params(("parallel",)))(w, g, m, v)


def _ada_fwd(c_all, w_shard, b_shard):
    def body(c_ref, w_ref, b_ref, o_ref):
        cv = c_ref[...]
        sc = (cv * _sigmoid(cv)).astype(bf16)
        o_ref[...] = jnp.dot(sc, w_ref[...].astype(bf16), preferred_element_type=f32) + b_ref[...]
    return pl.pallas_call(body, out_shape=SDS((N_DEV, w_shard.shape[1]), f32), name="ada_fwd",
                          compiler_params=_params())(c_all, w_shard, b_shard)


def _ada_bwd(c_all, dmod_cols):
    def body(c_ref, d_ref, o_ref):
        cv = c_ref[...]
        sc = cv * _sigmoid(cv)
        o_ref[...] = lax.dot_general(sc, d_ref[...], (((0,), (0,)), ((), ())), precision=lax.Precision.HIGHEST,
                                     preferred_element_type=f32)
    return pl.pallas_call(body, out_shape=SDS((D, dmod_cols.shape[1]), f32), name="ada_bwd",
                          compiler_params=_params())(c_all, dmod_cols)


def _small_reduce(gathered, after):
    def body(g_ref, after_ref, o_ref, loss_ref):
        acc = g_ref[0]
        for d in range(1, N_DEV):
            acc = acc + g_ref[d]
        o_ref[...] = acc
        loss_ref[...] = jnp.zeros((1, LANES), f32) + jnp.sum(acc[10:11, :])
    return pl.pallas_call(body, out_shape=(SDS((SMALL_ROWS, D), f32), SDS((1, LANES), f32)), name="small_reduce",
                          in_specs=[pl.BlockSpec(memory_space=pltpu.VMEM), pl.BlockSpec(memory_space=pl.ANY)],
                          compiler_params=_params())(gathered, after)


FOX_BLK = 512
CUM_BLK = 128


def _fold_lanes(t, op):
    out = t[:, :LANES]
    for j in range(1, t.shape[1] // LANES):
        out = op(out, t[:, j * LANES:(j + 1) * LANES])
    return out


def _fox_gate_fwd(proj, b_pad):
    nblk = SEQ // CUM_BLK

    def body(f_ref, b_ref, col_ref):
        r = lax.broadcasted_iota(jnp.int32, (CUM_BLK, CUM_BLK), 0)
        c = lax.broadcasted_iota(jnp.int32, (CUM_BLK, CUM_BLK), 1)
        tri = (r >= c).astype(f32)
        carry = jnp.zeros((1, LANES), f32)
        for blk in range(nblk):
            z = f_ref[blk * CUM_BLK:(blk + 1) * CUM_BLK, :] + b_ref[...]
            logf = jnp.minimum(z, 0.0) - jnp.log1p(jnp.exp(-jnp.abs(z)))
            cs = jnp.dot(tri, logf, precision=lax.Precision.HIGHEST, preferred_element_type=f32) + carry
            col_ref[blk * CUM_BLK:(blk + 1) * CUM_BLK, :] = cs
            carry = cs[CUM_BLK - 1:CUM_BLK, :]

    return pl.pallas_call(
        body, grid=(1,), in_specs=[pl.BlockSpec((SEQ, LANES), lambda i: (0, C_F // LANES)),
                                   pl.BlockSpec((1, LANES), lambda i: (0, 0))],
        out_specs=pl.BlockSpec((SEQ, LANES), lambda i: (0, 0)),
        out_shape=SDS((SEQ, LANES), f32), name="fox_gate_fwd",
        compiler_params=_params(("arbitrary",)),
    )(proj, b_pad)


def _fox_gate_bwd(dF_row, proj, b_pad):
    nblk = SEQ // CUM_BLK

    def body(d_ref, f_ref, b_ref, df_ref, db_ref, col_ref):
        r = lax.broadcasted_iota(jnp.int32, (CUM_BLK, CUM_BLK), 0)
        c = lax.broadcasted_iota(jnp.int32, (CUM_BLK, CUM_BLK), 1)
        tri = (r <= c).astype(f32)
        lane = lax.broadcasted_iota(jnp.int32, (CUM_BLK, LANES), 1)
        col_ref[...] = d_ref[...].T
        carry = jnp.zeros((1, LANES), f32)
        total = jnp.zeros((1, LANES), f32)
        for blk in reversed(range(nblk)):
            rows = slice(blk * CUM_BLK, (blk + 1) * CUM_BLK)
            cs = jnp.dot(tri, col_ref[rows, :], precision=lax.Precision.HIGHEST, preferred_element_type=f32) + carry
            carry = cs[0:1, :]
            z = f_ref[rows, :] + b_ref[...]
            df = jnp.where(lane < N_FOX_HEADS, cs * _sigmoid(-z), 0.0)
            df_ref[rows, :] = df.astype(df_ref.dtype)
            total = total + _colsum(df)
        db_ref[...] = total

    return pl.pallas_call(
        body, grid=(1,), in_specs=[pl.BlockSpec((LANES, SEQ), lambda i: (0, 0)),
                                   pl.BlockSpec((SEQ, LANES), lambda i: (0, C_F // LANES)),
                                   pl.BlockSpec((1, LANES), lambda i: (0, 0))],
        out_specs=[pl.BlockSpec((SEQ, LANES), lambda i: (0, 0)), pl.BlockSpec((1, LANES), lambda i: (0, 0))],
        out_shape=(SDS((SEQ, LANES), bf16), SDS((1, LANES), f32)), name="fox_gate_bwd",
        scratch_shapes=[pltpu.VMEM((SEQ, LANES), f32)],
        compiler_params=_params(("arbitrary",)),
    )(dF_row, proj, b_pad)


def _nt(a, b):
    return lax.dot_general(a, b, (((1,), (1,)), ((), ())), preferred_element_type=f32)


def _tn(a, b):
    return lax.dot_general(a, b, (((0,), (0,)), ((), ())), preferred_element_type=f32)


def _fox_prep(proj, f_col):
    def fn(t, v):
        q, k, vv, fc = t
        lane = lax.broadcasted_iota(jnp.int32, (q.shape[0], LANES), 1)
        qs, ks = [], []
        for h in range(N_FOX_HEADS):
            pair, pos = divmod(h, 2)
            own = (lane >= pos * HEAD_DIM) & (lane < (pos + 1) * HEAD_DIM)
            base = (1 - pos) * HEAD_DIM
            f = fc[:, h:h + 1]
            hi = f.astype(bf16).astype(f32)
            mid = (f - hi).astype(bf16).astype(f32)
            lo = (f - hi) - mid
            one = jnp.ones_like(f)
            qa = jnp.where(own, q[:, pair * LANES:(pair + 1) * LANES] * ATT_SCALE, 0.0)
            ka = k[:, pair * LANES:(pair + 1) * LANES]
            for idx, (qv, kv) in enumerate([(hi, one), (mid, one), (lo, one), (one, -hi), (one, -mid), (one, -lo)]):
                sel = lane == base + idx
                qa = jnp.where(sel, qv, qa)
                ka = jnp.where(sel, kv, ka)
            qs.append(qa)
            ks.append(ka)
        return [jnp.concatenate(qs, axis=1), jnp.concatenate(ks, axis=1), vv], []
    w = N_FOX_HEADS * LANES
    return _rowwise(fn, "fox_prep", [(proj, FOX_W, C_QA // FOX_W), (proj, FOX_W, C_KA // FOX_W),
                                     (proj, FOX_W, C_VA // FOX_W), (f_col, LANES, 0)], [],
                    [(w, bf16), (w, bf16), (FOX_W, bf16)])


def _fox_fwd(q_aug, k_aug, v):
    blk = FOX_BLK
    npair = FOX_W // LANES

    def body(q_ref, k_ref, v_ref, o_ref, max_ref, sum_ref, s_scr):
        i = pl.program_id(1)
        tri = lax.broadcasted_iota(jnp.int32, (blk, blk), 0) >= lax.broadcasted_iota(jnp.int32, (blk, blk), 1)
        qh = [q_ref[:, h * LANES:(h + 1) * LANES] for h in range(2)]

        def logits(c, masked):
            off = pl.multiple_of(c * blk, blk)
            tops = []
            for h in range(2):
                s = _nt(qh[h], k_ref[pl.ds(off, blk), h * LANES:(h + 1) * LANES])
                if masked:
                    s = jnp.where(tri, s, NEG)
                s_scr[h, :, pl.ds(off, blk)] = s
                tops.append(_fold_lanes(s, jnp.maximum))
            return tops

        def pass_a(c, m):
            return tuple(jnp.maximum(a, b) for a, b in zip(m, logits(c, False)))

        m = lax.fori_loop(0, i, pass_a, tuple(jnp.full((blk, LANES), NEG, f32) for _ in range(2)))
        mx = [jnp.max(jnp.maximum(a, b), axis=1, keepdims=True) for a, b in zip(m, logits(i, True))]

        def pass_b(c, carry):
            off = pl.multiple_of(c * blk, blk)
            vv = v_ref[pl.ds(off, blk), :]
            new = []
            for h in range(2):
                l, acc = carry[h]
                p = jnp.exp(s_scr[h, :, pl.ds(off, blk)] - mx[h]).astype(bf16)
                new.append((l + _fold_lanes(p.astype(f32), jnp.add), acc + jnp.dot(p, vv, preferred_element_type=f32)))
            return tuple(new)

        zero = jnp.zeros((blk, LANES), f32)
        (l_a, acc_a), (l_b, acc_b) = lax.fori_loop(0, i + 1, pass_b, ((zero, zero), (zero, zero)))
        l_a = jnp.sum(l_a, axis=1, keepdims=True)
        l_b = jnp.sum(l_b, axis=1, keepdims=True)
        first = lax.broadcasted_iota(jnp.int32, (blk, LANES), 1) < HEAD_DIM
        o_ref[...] = jnp.where(first, acc_a / l_a, acc_b / l_b)
        max_ref[0] = jnp.where(first, mx[0], mx[1])
        sum_ref[0] = jnp.where(first, l_a, l_b)

    return pl.pallas_call(
        body, grid=(npair, SEQ // blk),
        in_specs=[pl.BlockSpec((blk, 2 * LANES), lambda p, i: (i, p)),
                  pl.BlockSpec((SEQ, 2 * LANES), lambda p, i: (0, p)),
                  pl.BlockSpec((SEQ, LANES), lambda p, i: (0, p))],
        out_specs=[pl.BlockSpec((blk, LANES), lambda p, i: (i, p))]
        + [pl.BlockSpec((1, blk, LANES), lambda p, i: (p, i, 0))] * 2,
        out_shape=(SDS((SEQ, FOX_W), f32),) + (SDS((npair, SEQ, LANES), f32),) * 2, name="fox_fwd",
        scratch_shapes=[pltpu.VMEM((2, blk, SEQ), f32)],
        compiler_params=_params(("parallel", "arbitrary")),
    )(q_aug, k_aug, v)


def _fox_bwd(q_aug, k_aug, v, do, o, row_max, row_sum, after):
    blk = FOX_BLK
    npair = FOX_W // LANES
    nblk = SEQ // blk

    def body(q_ref, k_ref, v_ref, do_ref, o_ref, max_ref, sum_ref, after_ref, dq_ref, dk_ref, dv_ref, df_ref, dq_acc,
             delta_ref, inv_ref):
        inv_ref[...] = 1.0 / sum_ref[0]
        lane_s = lax.broadcasted_iota(jnp.int32, (SEQ, LANES), 1)
        prod = do_ref[...].astype(bf16).astype(f32) * o_ref[...]
        d_a = jnp.sum(jnp.where(lane_s < HEAD_DIM, prod, 0.0), axis=1, keepdims=True)
        d_b = jnp.sum(jnp.where(lane_s >= HEAD_DIM, prod, 0.0), axis=1, keepdims=True)
        delta_ref[...] = jnp.where(lane_s < HEAD_DIM, d_a, d_b)
        dq_acc[...] = jnp.zeros_like(dq_acc)
        df_ref[...] = jnp.zeros_like(df_ref)
        lane = lax.broadcasted_iota(jnp.int32, (blk, LANES), 1)
        own = [lane < HEAD_DIM, lane >= HEAD_DIM]
        tri = lax.broadcasted_iota(jnp.int32, (blk, blk), 0) >= lax.broadcasted_iota(jnp.int32, (blk, blk), 1)

        def q_slab(qoff, h):
            return q_ref[pl.ds(qoff, blk), h * LANES:(h + 1) * LANES]

        def probs(qoff, h, k_h, masked):
            s = _nt(q_slab(qoff, h), k_h)
            if masked:
                s = jnp.where(tri, s, NEG)
            col = slice(h * HEAD_DIM, h * HEAD_DIM + 1)
            weights = jnp.exp(s - max_ref[0, pl.ds(qoff, blk), col]).astype(bf16).astype(f32)
            return weights * inv_ref[pl.ds(qoff, blk), col]

        def k_slabs(koff):
            return [k_ref[pl.ds(koff, blk), h * LANES:(h + 1) * LANES] for h in range(2)]

        def kv_step(kj, _):
            koff = pl.multiple_of(kj * blk, blk)
            k_aug = k_slabs(koff)
            k_own = [jnp.where(own[h], k_aug[h], jnp.zeros_like(k_aug[h])) for h in range(2)]
            vv = v_ref[pl.ds(koff, blk), :]
            v_own = [jnp.where(own[h], vv, jnp.zeros_like(vv)) for h in range(2)]

            def q_tile(qi, carry, masked):
                qoff = pl.multiple_of(qi * blk, blk)
                dd = do_ref[pl.ds(qoff, blk), :].astype(bf16)
                new, dq_add = [], None
                for h in range(2):
                    dk_h, dv_h, dcol = carry[h]
                    p = probs(qoff, h, k_aug[h], masked)
                    dl = p * (_nt(dd, v_own[h]) - delta_ref[pl.ds(qoff, blk), h * HEAD_DIM:h * HEAD_DIM + 1])
                    dlb = dl.astype(bf16)
                    part = jnp.dot(dlb, k_own[h], preferred_element_type=f32)
                    dq_add = part if dq_add is None else dq_add + part
                    new.append((dk_h + _tn(dlb, q_slab(qoff, h)), dv_h + _tn(p.astype(bf16), dd),
                                dcol + _colsum(dl)))
                dq_acc[pl.ds(qoff, blk), :] += dq_add * ATT_SCALE
                return tuple(new)

            zero = (jnp.zeros((blk, LANES), f32), jnp.zeros((blk, LANES), f32), jnp.zeros((1, blk), f32))
            carry = q_tile(kj, (zero, zero), True)
            (dk_a, dv_a, dcol_a), (dk_b, dv_b, dcol_b) = lax.fori_loop(
                kj + 1, nblk, lambda qi, cr: q_tile(qi, cr, False), carry)
            dk_ref[pl.ds(koff, blk), :] = jnp.where(own[0], dk_a, dk_b).astype(dk_ref.dtype)
            dv_ref[pl.ds(koff, blk), :] = jnp.where(own[0], dv_a, dv_b).astype(dv_ref.dtype)
            df_ref[0, 0:1, pl.ds(koff, blk)] = -dcol_a
            df_ref[0, 1:2, pl.ds(koff, blk)] = -dcol_b
            return 0

        lax.fori_loop(0, nblk, kv_step, 0)
        dq_ref[...] = dq_acc[...].astype(dq_ref.dtype)

    pair_aug = pl.BlockSpec((SEQ, 2 * LANES), lambda p: (0, p))
    slab = pl.BlockSpec((SEQ, LANES), lambda p: (0, p))
    per_pair = pl.BlockSpec((1, SEQ, LANES), lambda p: (p, 0, 0))
    rows = pl.BlockSpec((1, 8, SEQ), lambda p: (p, 0, 0))
    return pl.pallas_call(
        body, grid=(npair,),
        in_specs=[pair_aug, pair_aug, slab, slab, slab, per_pair, per_pair, pl.BlockSpec(memory_space=pl.ANY)],
        out_specs=[slab, slab, slab, rows],
        out_shape=(SDS((SEQ, FOX_W), bf16),) * 3 + (SDS((npair, 8, SEQ), f32),), name="fox_bwd",
        scratch_shapes=[pltpu.VMEM((SEQ, LANES), f32)] * 3,
        compiler_params=_params(("parallel",)),
    )(q_aug, k_aug, v, do, o, row_max, row_sum, after)


DIL_BLK = 128
DILATIONS = (1, 4, 16)
N_GROUPS = len(DILATIONS)
DIL_PAIRS = DIL_OUT_W // LANES


def _dil_blocks(d):
    r1 = lax.broadcasted_iota(jnp.int32, (2 * DIL_BLK, DIL_BLK), 0) & (DIL_BLK - 1)
    c1 = lax.broadcasted_iota(jnp.int32, (2 * DIL_BLK, DIL_BLK), 1)
    r2 = lax.broadcasted_iota(jnp.int32, (2 * DIL_BLK, 2 * DIL_BLK), 0) & (DIL_BLK - 1)
    c2 = lax.broadcasted_iota(jnp.int32, (2 * DIL_BLK, 2 * DIL_BLK), 1)
    band = ((c2 < DIL_BLK) & (c2 >= r2)) | ((c2 >= DIL_BLK) & (c2 - DIL_BLK <= r2))
    out = []
    for r in range(d):
        for b in range(SEQ // d // DIL_BLK):
            rows = pl.ds(r + d * DIL_BLK * b, DIL_BLK, stride=d)
            if b == 0:
                out.append((rows, rows, r1 >= c1))
            else:
                out.append((rows, pl.ds(r + d * DIL_BLK * (b - 1), 2 * DIL_BLK, stride=d), band))
    return out


def _dil_v_spec(g):
    return pl.BlockSpec((SEQ, LANES), lambda p: (0, C_VB // LANES + DIL_PAIRS * g + p))


def _stack_heads(t, first):
    zero = jnp.zeros_like(t)
    return jnp.concatenate([jnp.where(first, t, zero), jnp.where(first, zero, t)], axis=0)


def _dil_fwd(q, k, v, g):
    def body(q_ref, k_ref, v_ref, o_ref, lse_ref):
        first = lax.broadcasted_iota(jnp.int32, (DIL_BLK, LANES), 1) < HEAD_DIM
        for rows, krows, mask in _dil_blocks(DILATIONS[g]):
            qv, kk, vv = q_ref[rows, :].astype(bf16), k_ref[krows, :].astype(bf16), v_ref[krows, :].astype(bf16)
            s = jnp.where(mask, _nt(_stack_heads(qv, first), kk), NEG)
            m = jnp.max(s, axis=1, keepdims=True)
            p = jnp.exp(s - m)
            l = jnp.sum(p, axis=1, keepdims=True)
            out = jnp.dot(p.astype(bf16), vv, preferred_element_type=f32) / l
            lse = m + jnp.log(l)
            o_ref[rows, :] = jnp.where(first, out[:DIL_BLK], out[DIL_BLK:])
            lse_ref[rows, :] = jnp.where(first, lse[:DIL_BLK], lse[DIL_BLK:])

    grouped = pl.BlockSpec((SEQ, LANES), lambda p: (0, DIL_PAIRS * g + p))
    own = pl.BlockSpec((SEQ, LANES), lambda p: (0, p))
    shape = SDS((SEQ, DIL_OUT_W), f32)
    return pl.pallas_call(
        body, grid=(DIL_PAIRS,), in_specs=[grouped, grouped, _dil_v_spec(g)], out_specs=[own] * 2,
        out_shape=(shape, shape),
        name=f"dil_fwd_{DILATIONS[g]}", compiler_params=_params(("parallel",)),
    )(q, k, v)


def _dil_bwd(q, k, v, do, lse, delta, g):
    def body(q_ref, k_ref, v_ref, do_ref, lse_ref, dl_ref, dq_ref, dk_ref, dv_ref):
        first = lax.broadcasted_iota(jnp.int32, (DIL_BLK, LANES), 1) < HEAD_DIM
        dk_ref[...] = jnp.zeros_like(dk_ref)
        dv_ref[...] = jnp.zeros_like(dv_ref)
        for rows, krows, mask in _dil_blocks(DILATIONS[g]):
            qv, kk, vv = q_ref[rows, :].astype(bf16), k_ref[krows, :].astype(bf16), v_ref[krows, :].astype(bf16)
            lsev, delv = lse_ref[rows, :], dl_ref[rows, :]
            q2 = _stack_heads(qv, first)
            do2 = _stack_heads(do_ref[rows, :].astype(bf16), first)
            per_head = lambda t: jnp.concatenate([t[:, 0:1], t[:, HEAD_DIM:HEAD_DIM + 1]], axis=0)
            p = jnp.exp(jnp.where(mask, _nt(q2, kk), NEG) - per_head(lsev))
            dl = (p * (_nt(do2, vv) - per_head(delv))).astype(bf16)
            dq = jnp.dot(dl, kk, preferred_element_type=f32)
            dq_ref[rows, :] = jnp.where(first, dq[:DIL_BLK], dq[DIL_BLK:]) * ATT_SCALE
            dk_ref[krows, :] += _tn(dl, q2)
            dv_ref[krows, :] += _tn(p.astype(bf16), do2)

    grouped = pl.BlockSpec((SEQ, LANES), lambda p: (0, DIL_PAIRS * g + p))
    own = pl.BlockSpec((SEQ, LANES), lambda p: (0, p))
    shape = SDS((SEQ, DIL_OUT_W), f32)
    return pl.pallas_call(
        body, grid=(DIL_PAIRS,), in_specs=[grouped, grouped, _dil_v_spec(g)] + [own] * 3, out_specs=[own] * 3,
        out_shape=(shape, shape, shape), name=f"dil_bwd_{DILATIONS[g]}", compiler_params=_params(("parallel",)),
    )(q, k, v, do, lse, delta)


def _position():
    return lax.axis_index("x"), lax.axis_index("y"), lax.axis_index("c")


def _all_gather(block, name, after=None):
    after = [] if after is None else [after]

    def body(x_ref, *refs):
        out_ref, send_sems, recv_sems, local_sem = refs[len(after):]
        x, y, c = _position()
        me, sibling = (x, y, c), (x, y, 1 - c)
        chips = [(1 - x, y), (x, 1 - y), (1 - x, 1 - y)]

        def slot(px, py, pc):
            return out_ref.at[4 * px + 2 * py + pc]

        def copy(k, blk, to, src=None):
            return pltpu.make_async_remote_copy(
                src_ref=slot(*blk) if src is None else src, dst_ref=slot(*blk),
                send_sem=send_sems.at[k], recv_sem=recv_sems.at[k], device_id=to, device_id_type=MESH)

        mine = pltpu.make_async_copy(x_ref, slot(*me), local_sem)
        mine.start()
        first = [copy(0, me, sibling, src=x_ref)]
        first += [copy(1 + j, me, (*chip, c), src=x_ref) for j, chip in enumerate(chips)]
        for cp in first:
            cp.start()
        passed = [copy(4 + j, (*chip, c), sibling) for j, chip in enumerate(chips)]
        for j, chip in enumerate(chips):
            copy(1 + j, (*chip, c), me).wait_recv()
            passed[j].start()
        copy(0, sibling, me).wait_recv()
        for j, chip in enumerate(chips):
            copy(4 + j, (*chip, 1 - c), me).wait_recv()
        for cp in first + passed:
            cp.wait_send()
        mine.wait()

    return pl.pallas_call(
        body, out_shape=SDS((N_DEV,) + block.shape, block.dtype),
        in_specs=[pl.BlockSpec(memory_space=pl.ANY)] * (1 + len(after)), out_specs=pl.BlockSpec(memory_space=pl.ANY),
        scratch_shapes=[pltpu.SemaphoreType.DMA((7,)), pltpu.SemaphoreType.DMA((7,)), pltpu.SemaphoreType.DMA],
        name=name,
    )(block, *after)


HBM_SPEC = pl.BlockSpec(memory_space=pltpu.HBM)
SEM_SPEC = pl.BlockSpec(memory_space=pltpu.SEMAPHORE)
SPLIT_COPY = pltpu.CompilerParams(has_side_effects=pltpu.SideEffectType.DATAFLOW_SIDE_EFFECTING)


def _in_hbm(t):
    return pltpu.with_memory_space_constraint(t, pltpu.HBM)


def _pair_copies(g_refs, land_refs, send_sems, recv_sems):
    x, y, c = _position()
    return [pltpu.make_async_remote_copy(
        src_ref=g.at[2 * k + (1 - c)], dst_ref=land.at[k], send_sem=send_sems.at[4 * a + k],
        recv_sem=recv_sems.at[4 * a + k], device_id=(x, y, 1 - c), device_id_type=MESH)
        for a, (g, land) in enumerate(zip(g_refs, land_refs, strict=True)) for k in range(4)]


def _chip_copies(t_refs, land_refs, send_sems, recv_sems):
    x, y, c = _position()
    chips = [(1 - x, y), (x, 1 - y), (1 - x, 1 - y)]
    return [pltpu.make_async_remote_copy(
        src_ref=t.at[2 * px + py], dst_ref=land.at[j], send_sem=send_sems.at[3 * a + j],
        recv_sem=recv_sems.at[3 * a + j], device_id=(px, py, c), device_id_type=MESH)
        for a, (t, land) in enumerate(zip(t_refs, land_refs, strict=True)) for j, (px, py) in enumerate(chips)]


_ROUNDS = {"pair": (_pair_copies, 4), "chip": (_chip_copies, 3)}


def _exchange_start(kind, ts, name):
    copies, slots = _ROUNDS[kind]
    n = len(ts)
    lands = [_in_hbm(lax.empty((slots,) + t.shape[1:], t.dtype)) for t in ts]

    def body(*refs):
        for cp in copies(refs[:n], refs[n:2 * n], refs[2 * n], refs[2 * n + 1]):
            cp.start()
        refs[-1][...] = jnp.zeros_like(refs[-1])

    sems = pltpu.SemaphoreType.DMA((slots * n,))
    res = pl.pallas_call(
        body, name=name, in_specs=[HBM_SPEC] * (2 * n),
        out_shape=(sems, sems, *[pltpu.HBM(t.shape, t.dtype) for t in (*ts, *lands)], SDS((8, LANES), f32)),
        out_specs=(SEM_SPEC, SEM_SPEC, *[HBM_SPEC] * (2 * n), pl.BlockSpec(memory_space=pltpu.VMEM)),
        input_output_aliases={i: 2 + i for i in range(2 * n)}, compiler_params=SPLIT_COPY,
    )(*[_in_hbm(t) for t in ts], *lands)
    return res[:-1], res[-1]


def _exchange_wait(kind, state, after, name):
    copies, _ = _ROUNDS[kind]
    send_sems, recv_sems, *arrays = state
    n = len(arrays) // 2

    def body(*refs):
        for cp in copies(refs[:n], refs[n:2 * n], refs[2 * n], refs[2 * n + 1]):
            cp.wait_send()
            cp.wait_recv()

    res = pl.pallas_call(
        body, name=name, in_specs=[HBM_SPEC] * (2 * n) + [SEM_SPEC, SEM_SPEC, pl.BlockSpec(memory_space=pl.ANY)],
        out_shape=[pltpu.HBM(t.shape, t.dtype) for t in arrays], out_specs=[HBM_SPEC] * (2 * n),
        input_output_aliases={i: i for i in range(2 * n)}, compiler_params=SPLIT_COPY,
    )(*arrays, send_sems, recv_sems, after)
    return res[:n], res[n:]


def _gather_copies(x_refs, out_refs, send_sems, recv_sems):
    x, y, c = _position()
    peers = [(x, y, 1 - c), (1 - x, y, c), (x, 1 - y, c), (1 - x, 1 - y, c)]
    sends, arrivals = [], []
    for a, (x_ref, out_ref) in enumerate(zip(x_refs, out_refs, strict=True)):
        for k, (px, py, pc) in enumerate(peers):
            sems = dict(send_sem=send_sems.at[4 * a + k], recv_sem=recv_sems.at[4 * a + k],
                        device_id=(px, py, pc), device_id_type=MESH)
            sends.append(pltpu.make_async_remote_copy(src_ref=x_ref, dst_ref=out_ref.at[4 * x + 2 * y + c], **sems))
            arrivals.append(pltpu.make_async_remote_copy(src_ref=x_ref, dst_ref=out_ref.at[4 * px + 2 * py + pc],
                                                         **sems))
    return sends, arrivals


def _gather_start(blocks, after, name):
    n = len(blocks)
    outs = [_in_hbm(lax.empty((N_DEV,) + b.shape, b.dtype)) for b in blocks]

    def body(*refs):
        sends, _ = _gather_copies(refs[:n], refs[n:2 * n], refs[2 * n + 1], refs[2 * n + 2])
        for cp in sends:
            cp.start()
        refs[-1][...] = jnp.zeros_like(refs[-1])

    sems = pltpu.SemaphoreType.DMA((4 * n,))
    res = pl.pallas_call(
        body, name=name, in_specs=[HBM_SPEC] * (2 * n) + [pl.BlockSpec(memory_space=pl.ANY)],
        out_shape=(sems, sems, *[pltpu.HBM(t.shape, t.dtype) for t in (*blocks, *outs)], SDS((8, LANES), f32)),
        out_specs=(SEM_SPEC, SEM_SPEC, *[HBM_SPEC] * (2 * n), pl.BlockSpec(memory_space=pltpu.VMEM)),
        input_output_aliases={i: 2 + i for i in range(2 * n)}, compiler_params=SPLIT_COPY,
    )(*[_in_hbm(b) for b in blocks], *outs, after)
    return res[:-1], res[-1]


def _gather_wait(state, after, name):
    send_sems, recv_sems, *arrays = state
    n = len(arrays) // 2

    def body(*refs):
        sends, arrivals = _gather_copies(refs[:n], refs[n:2 * n], refs[2 * n], refs[2 * n + 1])
        for cp in sends:
            cp.wait_send()
        for cp in arrivals:
            cp.wait_recv()

    res = pl.pallas_call(
        body, name=name, in_specs=[HBM_SPEC] * (2 * n) + [SEM_SPEC, SEM_SPEC, pl.BlockSpec(memory_space=pl.ANY)],
        out_shape=[pltpu.HBM(t.shape, t.dtype) for t in arrays], out_specs=[HBM_SPEC] * (2 * n),
        input_output_aliases={i: i for i in range(2 * n)}, compiler_params=SPLIT_COPY,
    )(*arrays, send_sems, recv_sems, after)
    return res[:n], res[n:]


def _gather_finish(partial, name):
    n = len(partial)

    def body(*refs):
        in_refs, out_refs = refs[:n], refs[n:2 * n]
        send_sems, recv_sems = refs[2 * n:]
        x, y, c = _position()
        chips = [(1 - x, y), (x, 1 - y), (1 - x, 1 - y)]
        copies = []
        for a in range(n):
            for j, (px, py) in enumerate(chips):
                cp = pltpu.make_async_remote_copy(
                    src_ref=in_refs[a].at[4 * px + 2 * py + c], dst_ref=out_refs[a].at[4 * px + 2 * py + c],
                    send_sem=send_sems.at[a, j], recv_sem=recv_sems.at[a, j], device_id=(x, y, 1 - c),
                    device_id_type=MESH)
                cp.start()
                copies.append(cp)
        for a in range(n):
            for j, (px, py) in enumerate(chips):
                pltpu.make_async_remote_copy(
                    src_ref=in_refs[a].at[4 * px + 2 * py + (1 - c)], dst_ref=out_refs[a].at[4 * px + 2 * py + (1 - c)],
                    send_sem=send_sems.at[a, j], recv_sem=recv_sems.at[a, j], device_id=(x, y, 1 - c),
                    device_id_type=MESH).wait_recv()
        for cp in copies:
            cp.wait_send()

    hbm = pl.BlockSpec(memory_space=pl.ANY)
    return pl.pallas_call(
        body, out_shape=[SDS(p.shape, p.dtype) for p in partial], in_specs=[hbm] * n, out_specs=[hbm] * n,
        input_output_aliases={a: a for a in range(n)},
        scratch_shapes=[pltpu.SemaphoreType.DMA((n, 3)), pltpu.SemaphoreType.DMA((n, 3))],
        name=name,
    )(*partial)


def _row_tile(rows):
    return 512 if rows % 512 == 0 and rows > 512 else rows


def _pair_add(g, r1, core, name):
    def body(c_ref, g_ref, r_ref, o_ref):
        o_ref[...] = (g_ref[...].astype(f32) + r_ref[...].astype(f32)).astype(o_ref.dtype)

    rows, cols = g.shape[1:]
    tile = _row_tile(rows)
    blk = (1, tile, cols)
    return pl.pallas_call(
        body, out_shape=SDS((4, rows, cols), g.dtype), name=name,
        grid_spec=pltpu.PrefetchScalarGridSpec(
            num_scalar_prefetch=1, grid=(4, rows // tile),
            in_specs=[pl.BlockSpec(blk, lambda k, i, c_ref: (2 * k + c_ref[0], i, 0)),
                      pl.BlockSpec(blk, lambda k, i, c_ref: (k, i, 0))],
            out_specs=pl.BlockSpec(blk, lambda k, i, c_ref: (k, i, 0))),
        compiler_params=_params(("parallel", "arbitrary")),
    )(core, g, r1)


def _chip_add(t, r2, chip, name, transposed=False):
    def body(c_ref, t_ref, r_ref, o_ref):
        s = ((t_ref[0].astype(f32) + r_ref[0].astype(f32)) + r_ref[1].astype(f32)) + r_ref[2].astype(f32)
        o_ref[...] = s.T if transposed else s

    rows, cols = t.shape[1:]
    tile = _row_tile(rows)
    out_spec = pl.BlockSpec((cols, tile), lambda i, c_ref: (0, i)) if transposed else pl.BlockSpec(
        (tile, cols), lambda i, c_ref: (i, 0))
    return pl.pallas_call(
        body, out_shape=SDS((cols, rows) if transposed else (rows, cols), f32), name=name,
        grid_spec=pltpu.PrefetchScalarGridSpec(
            num_scalar_prefetch=1, grid=(rows // tile,),
            in_specs=[pl.BlockSpec((1, tile, cols), lambda i, c_ref: (c_ref[0], i, 0)),
                      pl.BlockSpec((3, tile, cols), lambda i, c_ref: (0, i, 0))],
            out_specs=out_spec),
        compiler_params=_params(("arbitrary",)),
    )(chip, t, r2)


def _pad_to(t, axis, size):
    pads = [(0, 0)] * t.ndim
    pads[axis] = (0, size - t.shape[axis])
    return jnp.pad(t, pads)


_REF_COLS = {"qa": (0, FOX_W), "ka": (FOX_W, FOX_W), "va": (2 * FOX_W, FOX_W), "f": (3 * FOX_W, N_FOX_HEADS)}
_REF_COLS.update({n: (3 * FOX_W + N_FOX_HEADS + i * DIL_W, DIL_W) for i, n in enumerate(("qb", "kb", "vb"))})
_REF_COLS.update({n: (3 * FOX_W + N_FOX_HEADS + 3 * DIL_W + i * D, D) for i, n in enumerate(("ga", "gb"))})
_REF_ORDER = ("qa", "ka", "va", "f", "qb", "kb", "vb", "ga", "gb")


def _place_cols(sources, src_of, out_cols, name, row_block=512):
    arrays = [s[0] if isinstance(s, tuple) else s for s in sources]
    widths = [a.shape[-1] for a in arrays]
    rows = arrays[0].shape[-2]
    plan = []
    for t in range(out_cols // LANES):
        segs, c, end = [], t * LANES, (t + 1) * LANES
        while c < end:
            s = src_of(c)
            if s is None:
                c += 1
                continue
            n = 1
            while c + n < end and src_of(c + n) == (s[0], s[1] + n):
                n += 1
            segs.append((s[0], s[1], c - t * LANES, n))
            c += n
        plan.append(segs)

    def body(*refs):
        o_ref = refs[-1]
        for t, segs in enumerate(plan):
            acc = None
            for si, c0, o0, n in segs:
                a0 = c0 // LANES * LANES
                wide = min(2 * LANES, widths[si] - a0)
                win = refs[si][0, :, a0:a0 + wide] if isinstance(sources[si], tuple) else refs[si][:, a0:a0 + wide]
                r = lax.broadcasted_iota(jnp.int32, (wide, LANES), 0)
                c = lax.broadcasted_iota(jnp.int32, (wide, LANES), 1)
                pick = ((r - (c0 - a0) == c - o0) & (c >= o0) & (c < o0 + n)).astype(bf16)
                part = jnp.dot(win.astype(bf16), pick, preferred_element_type=f32)
                acc = part if acc is None else acc + part
            tile = jnp.zeros((row_block, LANES), f32) if acc is None else acc
            o_ref[:, t * LANES:(t + 1) * LANES] = tile.astype(o_ref.dtype)

    def spec(s):
        if isinstance(s, tuple):
            j = s[1]
            return pl.BlockSpec((1, row_block, s[0].shape[-1]), lambda i: (j, i, 0))
        return pl.BlockSpec((row_block, s.shape[-1]), lambda i: (i, 0))

    return pl.pallas_call(
        body, grid=(rows // row_block,), in_specs=[spec(s) for s in sources],
        out_specs=pl.BlockSpec((row_block, out_cols), lambda i: (i, 0)), out_shape=SDS((rows, out_cols), bf16),
        name=name, compiler_params=_params(("parallel",)),
    )(*arrays)


def _ref_piece(r):
    for name in _REF_ORDER:
        lo, width = _REF_COLS[name]
        if lo <= r < lo + width:
            return name, r - lo
    raise ValueError(r)


def _shard_pad_cols(pieces):
    names = [n for n in _REF_ORDER if n != "vb"]
    sources = [pieces[n] for n in names] + list(pieces["vb"])

    def src_of(c):
        j, i = divmod(c, W_IN_PAD)
        if i >= W_IN_SH:
            return None
        name, col = _ref_piece(j * W_IN_SH + i)
        if name == "vb":
            return len(names) + col // DIL_OUT_W, col % DIL_OUT_W
        return names.index(name), col

    return _place_cols(sources, src_of, N_DEV * W_IN_PAD, "place_dproj")


_SLABS = {"ga": C_GA, "gb": C_GB, "qb": C_QB, "kb": C_KB, "vb": C_VB, "qa": C_QA, "ka": C_KA, "va": C_VA, "f": C_F}


def _slab_w_in(stack):
    def src_of(c):
        for name, start in _SLABS.items():
            lo, width = _REF_COLS[name]
            if start <= c < start + width:
                return divmod(lo + c - start, W_IN_SH)
        return None

    return _place_cols([(stack, j) for j in range(N_DEV)], src_of, PROJ_W, "place_w_in")


def kernel(x, c, w_ada, b_ada, g_mix, w_in, b_fgate, w_br_a, w_br_b, w_out, g_ffn, w_ffn_gate, w_ffn_up, w_ffn_down, g_final, loss_target, m_w_ada, m_b_ada, m_g_mix, m_w_in, m_b_fgate, m_w_br_a, m_w_br_b, m_w_out, m_g_ffn, m_w_ffn_gate, m_w_ffn_up, m_w_ffn_down, m_g_final, v_w_ada, v_b_ada, v_g_mix, v_w_in, v_b_fgate, v_w_br_a, v_w_br_b, v_w_out, v_g_ffn, v_w_ffn_gate, v_w_ffn_up, v_w_ffn_down, v_g_final):
    px, py, pc = _position()
    dev = 4 * px + 2 * py + pc
    x2d, tgt = x[0], loss_target[0]

    c_all = _all_gather(c, "gather_c").reshape(N_DEV, D)
    ada_cols = w_ada.shape[2]
    b_shard = lax.dynamic_slice(b_ada, (0, dev * ada_cols), (1, ada_cols))
    mod_shard = _ada_fwd(c_all, w_ada[0], b_shard)
    mod_all = _all_gather(mod_shard, "gather_mod")
    modv = lax.dynamic_index_in_dim(mod_all, dev, axis=1, keepdims=False).reshape(6, D)
    h1 = _pre1(x2d, modv, g_mix)

    w_in_s = _all_gather(_pad_to(w_in[0], 1, W_IN_PAD).astype(bf16), "gather_w_in")
    gate_up = jnp.concatenate([_pad_to(w_ffn_gate[0], 1, FF_PAD), _pad_to(w_ffn_up[0], 1, FF_PAD)], axis=1)
    later = [w_br_a[0], w_br_b[0], w_out[0], gate_up, _pad_to(w_ffn_down[0], 0, FF_PAD)]
    later_state, later_token = _gather_start([t.astype(bf16) for t in later], w_in_s, "gather_rest_start")
    w_in_p = _slab_w_in(w_in_s)

    proj = _matmul(h1, w_in_p, name="mm_proj", tm=SEQ, tn=896, after=later_token)
    b_pad = jnp.pad(b_fgate, ((0, 0), (0, LANES - N_FOX_HEADS)))
    q_aug, k_aug, va = _fox_prep(proj, _fox_gate_fwd(proj, b_pad))
    ya_h, max_a, sum_a = _fox_fwd(q_aug, k_aug, va)

    tables = _rope_tables()
    qb_r, kb_r = _rope_fwd(proj, tables)
    by_group = [_dil_fwd(qb_r, kb_r, proj, grp) for grp in range(N_GROUPS)]
    yb_h, lse_b = _dil_combine([o for o, _ in by_group], [l for _, l in by_group])

    both_done = ya_h[:8, :LANES] + yb_h[:8, :LANES]
    mine, arrived = _gather_wait(later_state, both_done, "gather_rest_wait")
    w_a_s, w_b_s, w_o_s, w_gu_s, w_d_s = [
        lax.dynamic_update_slice(stack, block[None], (dev, 0, 0))
        for stack, block in zip(_gather_finish(arrived, "gather_rest_finish"), mine, strict=True)]
    w_o = w_o_s.reshape(D, D)
    w_d = w_d_s.reshape(FF_HID, D)
    ya = _matmul_stack(ya_h, w_a_s, name="mm_br_a")
    yb = _matmul_stack(yb_h, w_b_s, name="mm_br_b")

    merged, mix, x1, h2 = _post1(ya, yb, proj, w_o, x2d, modv, g_ffn)
    act, au = _ffn_in(h2, w_gu_s)

    dx2, dff, dg_final, dga_f, loss_lanes = _final(act, w_d, x1, tgt, modv, g_final.reshape(1, D))
    dau = _ffn_bwd_in(dff, w_d_s, au)

    core = pc.astype(jnp.int32).reshape(1)
    chip = (2 * px + py).astype(jnp.int32).reshape(1)

    def pair_done(state, after, tags, name):
        mine, theirs = _exchange_wait("pair", state, after, "pair_wait_" + name)
        sums = [_pair_add(g, r, core, "pair_add_" + t) for g, r, t in zip(mine, theirs, tags)]
        return _exchange_start("chip", sums, "chip_start_" + name)

    def from_chips(state, after, tags, name, transposed=None):
        sums, got = _exchange_wait("chip", state, after, "chip_wait_" + name)
        flips = transposed or [False] * len(tags)
        return [_chip_add(p, r, chip, "chip_add_" + t, f) for p, r, t, f in zip(sums, got, tags, flips)]

    g_gu = _matmul(h2, dau, ta=True, by_shard=True, out_dtype=bf16, name="mm_g_ffn_in", tm=D, tn=2 * FF_PAD)
    g_d = _matmul(act, dff, ta=True, out_dtype=bf16, name="mm_g_down", tm=FF_HID // 2, tn=512)
    ffn_tags = ["gu", "down"]
    ffn_pair, ffn_pair_token = _exchange_start("pair", [g_gu, g_d.reshape(N_DEV, FF_PAD, D)], "pair_start_ffn")

    dx1, dmix, dsh_f, dsc_f, dg_ffn, dga_m = _mid_bwd(dau, w_gu_s, ffn_pair_token, x1, dx2, mix, modv, g_ffn)
    ffn_state, ffn_token = pair_done(ffn_pair, dx1, ffn_tags, "ffn")
    dya, dyb, dga, dgb = _merge_bwd(dmix, w_o, ffn_token, ya, yb, proj)
    dya_h = _matmul_stack(dya, w_a_s, tb=True, name="mm_d_ya")
    dyb_h = _matmul_stack(dyb, w_b_s, tb=True, name="mm_d_yb")

    g_o = _matmul(merged, dmix, ta=True, out_dtype=bf16, name="mm_g_out", tm=D, tn=512)
    g_a = _matmul_stack(ya_h, dya, ta=True, out_dtype=bf16, name="mm_g_br_a")
    g_b = _matmul_stack(yb_h, dyb, ta=True, out_dtype=bf16, name="mm_g_br_b")
    rows_a, rows_b = FOX_W * W_BR_SH // D, DIL_OUT_W * W_BR_SH // D
    g_small = jnp.concatenate([g_a.reshape(N_DEV, rows_a, D), g_b.reshape(N_DEV, rows_b, D),
                               g_o.reshape(N_DEV, W_BR_SH, D)], axis=1)
    small_pair, small_pair_token = _exchange_start("pair", [g_small], "pair_start_small")

    dqa, dka, dva, dF = _fox_bwd(q_aug, k_aug, va, dya_h, ya_h, max_a, sum_a, small_pair_token)
    dF_row = jnp.pad(dF[:, :2, :].reshape(N_FOX_HEADS, SEQ), ((0, LANES - N_FOX_HEADS), (0, 0)))
    df, db_fgate = _fox_gate_bwd(dF_row, proj, b_pad)
    small_state, small_token = pair_done(small_pair, df, ["small"], "small")

    delta_b = _dil_delta(dyb_h, yb_h)
    dil_grads = [_dil_bwd(qb_r, kb_r, proj, dyb_h, lse_b, delta_b, grp) for grp in range(N_GROUPS)]
    dqb, dkb = _rope_bwd([t[0] for t in dil_grads], [t[1] for t in dil_grads], tables)

    dproj = _shard_pad_cols({"qa": dqa, "ka": dka, "va": dva, "f": df, "qb": dqb, "kb": dkb,
                             "vb": [t[2] for t in dil_grads], "ga": dga, "gb": dgb})
    g_in = _matmul(h1, dproj, ta=True, by_shard=True, out_dtype=bf16, name="mm_g_in", tm=D, tn=W_IN_PAD,
                   after=small_token)
    mix_tags = ["in"]
    mix_pair, mix_pair_token = _exchange_start("pair", [g_in], "pair_start_mixer")
    mix_state, mix_token = pair_done(mix_pair, mix_pair_token, mix_tags, "mixer")

    grad_x, dsh_m, dsc_m, dg_mix = _first_bwd(dproj, w_in_s, mix_token, x2d, dx1, modv, g_mix)

    w = {"w_ada": w_ada, "b_ada": b_ada, "g_mix": g_mix, "w_in": w_in, "b_fgate": b_fgate, "w_br_a": w_br_a,
         "w_br_b": w_br_b, "w_out": w_out, "g_ffn": g_ffn, "w_ffn_gate": w_ffn_gate, "w_ffn_up": w_ffn_up,
         "w_ffn_down": w_ffn_down, "g_final": g_final}
    m = {"w_ada": m_w_ada, "b_ada": m_b_ada, "g_mix": m_g_mix, "w_in": m_w_in, "b_fgate": m_b_fgate,
         "w_br_a": m_w_br_a, "w_br_b": m_w_br_b, "w_out": m_w_out, "g_ffn": m_g_ffn, "w_ffn_gate": m_w_ffn_gate,
         "w_ffn_up": m_w_ffn_up, "w_ffn_down": m_w_ffn_down, "g_final": m_g_final}
    v = {"w_ada": v_w_ada, "b_ada": v_b_ada, "g_mix": v_g_mix, "w_in": v_w_in, "b_fgate": v_b_fgate,
         "w_br_a": v_w_br_a, "w_br_b": v_w_br_b, "w_out": v_w_out, "g_ffn": v_g_ffn, "w_ffn_gate": v_w_ffn_gate,
         "w_ffn_up": v_w_ffn_up, "w_ffn_down": v_w_ffn_down, "g_final": v_g_final}
    names = list(w)
    g, delta, new_m, new_v = {}, {}, {}, {}

    transposed = ("w_ffn_gate", "w_ffn_up")
    by_column = lambda t: jnp.transpose(t, (2, 0, 1))
    by_row = lambda t: jnp.transpose(t, (1, 2, 0))

    def update(n):
        shape = w[n].shape
        if n == "w_in":
            g3 = g[n][:, None, :]
            dl, mn, vn = _adamw_by_planes(by_column(w[n]), g3, by_column(m[n]), by_column(v[n]), "adamw_" + n)
            g[n], delta[n], new_m[n], new_v[n] = by_row(g3), by_row(dl), by_row(mn), by_row(vn)
            return
        if n in transposed:
            g_t = g[n]
            dl, mn, vn = _adamw(w[n][0].T, g_t, m[n][0].T, v[n][0].T, "adamw_" + n)
            g[n], delta[n], new_m[n], new_v[n] = g_t.T[None], dl.T[None], mn.T[None], vn.T[None]
            return
        two_d = (lambda t: t.reshape(shape[-2:])) if len(shape) == 3 else (lambda t: t)
        dl, mn, vn = _adamw(two_d(w[n]), two_d(g[n]), two_d(m[n]), two_d(v[n]), "adamw_" + n)
        delta[n], new_m[n], new_v[n] = dl.reshape(shape), mn.reshape(shape), vn.reshape(shape)

    def update_all(grads):
        g.update(grads)
        for n in grads:
            update(n)
        return sum(delta[n][(0,) * (delta[n].ndim - 1)][:N_FOX_HEADS] for n in grads)

    s_gu_t, s_d = from_chips(ffn_state, grad_x, ffn_tags, "ffn", [True, False])
    s_small, = from_chips(small_state, grad_x, ["small"], "small")
    sharded_done = update_all({
        "w_ffn_gate": s_gu_t[:W_FF_SH], "w_ffn_up": s_gu_t[FF_PAD:FF_PAD + W_FF_SH], "w_ffn_down": s_d[None, :W_FF_SH],
        "w_br_a": s_small[:rows_a].reshape(1, FOX_W, W_BR_SH),
        "w_br_b": s_small[rows_a:rows_a + rows_b].reshape(1, DIL_OUT_W, W_BR_SH), "w_out": s_small[None, rows_a + rows_b:],
    })

    pad_lane = lambda t: jnp.pad(t, ((0, 0), (0, D - t.shape[1])))
    small = jnp.concatenate([dsh_m, dsc_m, dga_m, dsh_f, dsc_f, dga_f, dg_mix, dg_ffn, dg_final,
                             pad_lane(db_fgate), loss_lanes, jnp.zeros((SMALL_ROWS - 11, D), f32)], axis=0)
    small_all = _all_gather(small, "gather_small", after=sharded_done)
    small_sum, loss_row = _small_reduce(small_all, mix_token)
    loss = loss_row[0, 0]
    dmod_all = small_all[:, :6, :].reshape(N_DEV, 6 * D)
    g_w_ada = _ada_bwd(c_all, lax.dynamic_slice(dmod_all, (0, dev * ada_cols), (N_DEV, ada_cols)))
    done = update_all({
        "w_ada": g_w_ada[None], "b_ada": small_sum[0:6].reshape(1, 6 * D), "g_mix": small_sum[6:7],
        "b_fgate": small_sum[9:10, :N_FOX_HEADS], "g_ffn": small_sum[7:8], "g_final": small_sum[8],
    })
    s_in_t, = from_chips(mix_state, done, mix_tags, "mixer", [True])
    g["w_in"] = s_in_t[:W_IN_SH]
    update("w_in")

    return (loss, grad_x[None], *[g[n] for n in names], *[delta[n] for n in names],
            *[new_m[n] for n in names], *[new_v[n] for n in names])
```

```python
import jax
import jax.numpy as jnp
import numpy as np
from jax import lax
from jax.experimental import pallas as pl
from jax.experimental.pallas import tpu as pltpu

f32 = jnp.float32
bf16 = jnp.bfloat16
SDS = jax.ShapeDtypeStruct
MESH = pl.DeviceIdType.MESH

N_DEV = 8
D = 1024
SEQ = 2048
HEAD_DIM = 64
N_FOX_HEADS = 8
FOX_W = 512
DIL_W = 768
DIL_OUT_W = 256
ROT_DIM = 16
ROPE_THETA = 500000.0
D_FF = 2816
IN_COLS = 5896
EPS = 1e-6
NEG = -1e30
ATT_SCALE = HEAD_DIM ** -0.5

ADAM_LR = 0.001
ADAM_B1 = 0.9
ADAM_B2 = 0.999
ADAM_EPS = 1e-08
ADAM_WD = 0.01
ADAM_STEP = 10

C_GA, C_GB, C_QB, C_KB, C_VB, C_QA, C_KA, C_VA, C_F = 0, 1024, 2304, 3072, 3840, 4608, 5120, 5632, 6144
PROJ_W = 6272
LANES = 128
VMEM_LIMIT = 52 * 1024 * 1024

W_IN_SH, W_IN_PAD = IN_COLS // N_DEV, 768
W_BR_SH = D // N_DEV
W_FF_SH, FF_PAD = D_FF // N_DEV, 384
FF_HID = N_DEV * FF_PAD
SMALL_ROWS = 16


def _params(sem=None):
    if sem is None:
        return pltpu.CompilerParams(vmem_limit_bytes=VMEM_LIMIT)
    return pltpu.CompilerParams(dimension_semantics=sem, vmem_limit_bytes=VMEM_LIMIT)


def _rowwise(fn, name, tiled, vecs, outs, reds=(), tile=256):
    nt, nv, no = len(tiled), len(vecs), len(outs)
    rows = tiled[0][0].shape[0]
    assert rows % tile == 0

    def body(*refs):
        tin = [r[...] for r in refs[:nt]]
        vin = [r[...] for r in refs[nt:nt + nv]]
        orefs = refs[nt + nv:nt + nv + no]
        rrefs = refs[nt + nv + no:]
        touts, routs = fn(tin, vin)
        for r, t in zip(orefs, touts, strict=True):
            r[...] = t.astype(r.dtype)
        if rrefs:
            @pl.when(pl.program_id(0) == 0)
            def _():
                for r in rrefs:
                    r[...] = jnp.zeros_like(r)
            for r, t in zip(rrefs, routs, strict=True):
                r[...] += t

    def col_map(cb):
        return lambda i: (i, cb)

    def whole_map(nd):
        return lambda i: (0,) * nd

    in_specs = [pl.BlockSpec((tile, w), col_map(cb)) for (_, w, cb) in tiled]
    in_specs += [pl.BlockSpec(v.shape, whole_map(v.ndim)) for v in vecs]
    out_specs = [pl.BlockSpec((tile, w), lambda i: (i, 0)) for (w, _) in outs]
    out_specs += [pl.BlockSpec((1, w), lambda i: (0, 0)) for w in reds]
    out_shape = [SDS((rows, w), dt) for (w, dt) in outs] + [SDS((1, w), f32) for w in reds]
    res = pl.pallas_call(
        body, grid=(rows // tile,), in_specs=in_specs, out_specs=out_specs, out_shape=out_shape, name=name,
        compiler_params=_params(("arbitrary",)),
    )(*[t[0] for t in tiled], *vecs)
    return res


def _matmul(a, b, *, ta=False, out_dtype=f32, name, tm, tn, by_shard=False, after=None):
    (m, k), n = ((a.shape[1], a.shape[0]) if ta else a.shape), b.shape[1]
    assert b.shape[0] == k and m % tm == 0 and n % tn == 0 and (ta or not by_shard)
    dims = (((0 if ta else 1,), (0,)), ((), ()))

    def body(a_ref, b_ref, *rest):
        p = lax.dot_general(a_ref[...].astype(bf16), b_ref[...].astype(bf16), dims, preferred_element_type=f32)
        o_ref = rest[-1]
        if by_shard:
            o_ref[0] = p.astype(o_ref.dtype)
        else:
            o_ref[...] = p.astype(o_ref.dtype)

    a_spec = pl.BlockSpec((k, tm), lambda i, j: (0, i)) if ta else pl.BlockSpec((tm, k), lambda i, j: (i, 0))
    if by_shard:
        assert tn == n // N_DEV
        out_spec, out_shape = pl.BlockSpec((1, tm, tn), lambda i, j: (j, i, 0)), SDS((N_DEV, m, tn), out_dtype)
    else:
        out_spec, out_shape = pl.BlockSpec((tm, tn), lambda i, j: (i, j)), SDS((m, n), out_dtype)
    extra_specs, extra = ([pl.BlockSpec(memory_space=pl.ANY)], [after]) if after is not None else ([], [])
    return pl.pallas_call(
        body, grid=(m // tm, n // tn), in_specs=[a_spec, pl.BlockSpec((k, tn), lambda i, j: (0, j))] + extra_specs,
        out_specs=out_spec, out_shape=out_shape, name=name, compiler_params=_params(("parallel", "parallel")),
    )(a, b, *extra)


def _matmul_stack(a, b, *, ta=False, tb=False, out_dtype=f32, name):
    def lanes(ref):
        return jnp.concatenate([ref[j] for j in range(N_DEV)], axis=1).astype(bf16)

    if ta:
        w = b.shape[1] // N_DEV

        def body(a_ref, b_ref, o_ref):
            p = _tn(a_ref[...].astype(bf16), b_ref[...].astype(bf16))
            for j in range(N_DEV):
                o_ref[j] = p[:, j * w:(j + 1) * w].astype(o_ref.dtype)

        return pl.pallas_call(body, out_shape=SDS((N_DEV, a.shape[1], w), out_dtype), name=name,
                              compiler_params=_params())(a, b)

    m, half = a.shape[0], a.shape[0] // 2
    n = b.shape[1] if tb else N_DEV * b.shape[2]

    def body(a_ref, b_ref, o_ref):
        av = a_ref[...].astype(bf16)
        o_ref[...] = (_nt(av, lanes(b_ref)) if tb else jnp.dot(av, lanes(b_ref), preferred_element_type=f32)
                      ).astype(o_ref.dtype)

    return pl.pallas_call(
        body, grid=(2,), in_specs=[pl.BlockSpec((half, a.shape[1]), lambda i: (i, 0)),
                                   pl.BlockSpec(b.shape, lambda i: (0, 0, 0))],
        out_specs=pl.BlockSpec((half, n), lambda i: (i, 0)), out_shape=SDS((m, n), out_dtype), name=name,
        compiler_params=_params(("parallel",)),
    )(a, b)


def _matmul_rows(form, a, b, after, fn, tiled, vecs, outs, reds, *, name, tm=512):
    norm = lambda ts: [t if isinstance(t, tuple) else (t, t.shape[1], 0) for t in ts]
    make, sources = a if isinstance(a, tuple) else (None, [a])
    sources, tiled = norm(sources), norm(tiled)
    m, k = sources[0][0].shape[0], (b.shape[0] if form == "nn" else b.shape[-1] * (N_DEV if form == "nt_stack" else 1))
    assert m % tm == 0
    ns, nt, nv, no = len(sources), len(tiled), len(vecs), len(outs)

    def body(*refs):
        src_refs, b_ref, refs = refs[:ns], refs[ns], refs[ns + 2:]
        if make is None:
            lhs = lambda lo, hi: src_refs[0][:, lo:hi]
        else:
            made = make([r[...] for r in src_refs]).astype(bf16)
            lhs = lambda lo, hi: made[:, lo:hi]
        if form == "nt_stack":
            w = b.shape[2]
            acc = _nt(lhs(0, w), b_ref[0])
            for j in range(1, N_DEV):
                acc = acc + _nt(lhs(j * w, (j + 1) * w), b_ref[j])
        elif form == "nt":
            acc = _nt(lhs(0, k), b_ref[...])
        else:
            acc = jnp.dot(lhs(0, k), b_ref[...], preferred_element_type=f32)
        if make is not None:
            refs[nt + nv][...] = made
            refs = refs[:nt + nv] + refs[nt + nv + 1:]
        orefs, rrefs = refs[nt + nv:nt + nv + no], refs[nt + nv + no:]
        touts, routs = fn([acc] + [r[...] for r in refs[:nt]], [r[...] for r in refs[nt:nt + nv]])
        for r, t in zip(orefs, touts, strict=True):
            r[...] = t.astype(r.dtype)

        @pl.when(pl.program_id(0) == 0)
        def _():
            for r in rrefs:
                r[...] = jnp.zeros_like(r)
        for r, t in zip(rrefs, routs, strict=True):
            r[...] += t

    def whole_map(nd):
        return lambda i: (0,) * nd

    def rows(width, cb=0):
        return pl.BlockSpec((tm, width), lambda i: (i, cb))

    made_out = [(k, bf16)] if make is not None else []
    return pl.pallas_call(
        body, grid=(m // tm,),
        in_specs=[rows(width, cb) for _, width, cb in sources]
        + [pl.BlockSpec(b.shape, whole_map(b.ndim), pipeline_mode=pl.Buffered(1)), pl.BlockSpec(memory_space=pl.ANY)]
        + [rows(width, cb) for _, width, cb in tiled] + [pl.BlockSpec(v.shape, whole_map(v.ndim)) for v in vecs],
        out_specs=[rows(width) for width, _ in made_out + list(outs)]
        + [pl.BlockSpec((1, width), lambda i: (0, 0)) for width in reds],
        out_shape=[SDS((m, width), dt) for width, dt in made_out + list(outs)]
        + [SDS((1, width), f32) for width in reds], name=name,
        compiler_params=_params(("arbitrary",)),
    )(*[t[0] for t in sources], b, after, *[t[0] for t in tiled], *vecs)


def _rms(x):
    r = lax.rsqrt(jnp.mean(x * x, axis=-1, keepdims=True) + EPS)
    return r, x * r


def _rms_bwd(r, xn, dxn):
    return r * (dxn - xn * jnp.mean(dxn * xn, axis=-1, keepdims=True))


def _colsum(t):
    return jnp.sum(t, axis=0, keepdims=True)


def _sigmoid(x):
    return 0.5 * jnp.tanh(0.5 * x) + 0.5


def _modulated_norm(x, g, shift, scale):
    _, xn = _rms(x)
    return (xn * g) * (1.0 + scale) + shift


def _pre1(x, modv, g_mix):
    def fn(t, v):
        (xt,), (mv, g) = t, v
        return [_modulated_norm(xt, g, mv[0:1], mv[1:2])], []
    return _rowwise(fn, "pre1", [(x, D, 0)], [modv, g_mix], [(D, bf16)])[0]


def _post1(ya, yb, proj, w_o, x, modv, g_ffn):
    def merge(t):
        ya_t, yb_t, ga, gb = t
        return _sigmoid(ga) * ya_t + _sigmoid(gb) * yb_t

    def fn(t, v):
        (mt, xt), (mv, g) = t, v
        x1 = xt + mv[2:3] * mt
        return [mt, x1, _modulated_norm(x1, g, mv[3:4], mv[4:5])], []
    return _matmul_rows("nn", (merge, [ya, yb, (proj, D, C_GA // D), (proj, D, C_GB // D)]), w_o, x, fn, [x],
                        [modv, g_ffn], [(D, f32), (D, f32), (D, bf16)], [], name="post1")


def _ffn_in(h, w_stack):
    def body(h_ref, w_ref, act_ref, au_ref):
        p = jnp.dot(h_ref[...], w_ref[0], preferred_element_type=f32)
        a, u = p[:, :FF_PAD], p[:, FF_PAD:]
        act_ref[...] = (a * _sigmoid(a) * u).astype(act_ref.dtype)
        au_ref[...] = p.astype(au_ref.dtype)

    return pl.pallas_call(
        body, grid=(N_DEV,),
        in_specs=[pl.BlockSpec((SEQ, D), lambda j: (0, 0)), pl.BlockSpec((1, D, 2 * FF_PAD), lambda j: (j, 0, 0))],
        out_specs=[pl.BlockSpec((SEQ, FF_PAD), lambda j: (0, j)), pl.BlockSpec((SEQ, 2 * FF_PAD), lambda j: (0, j))],
        out_shape=(SDS((SEQ, FF_HID), bf16), SDS((SEQ, 2 * FF_HID), bf16)), name="ffn_in",
        compiler_params=_params(("parallel",)),
    )(h, w_stack)


def _ffn_bwd_in(dff, w_down_stack, au):
    def body(d_ref, w_ref, au_ref, o_ref):
        dact = _nt(d_ref[...], w_ref[0])
        p = au_ref[...].astype(f32)
        a, u = p[:, :FF_PAD], p[:, FF_PAD:]
        sg = _sigmoid(a)
        o_ref[...] = jnp.concatenate([dact * u * (sg * (1.0 + a * (1.0 - sg))), dact * (a * sg)],
                                     axis=1).astype(o_ref.dtype)

    return pl.pallas_call(
        body, grid=(N_DEV,),
        in_specs=[pl.BlockSpec((SEQ, D), lambda j: (0, 0)), pl.BlockSpec((1, FF_PAD, D), lambda j: (j, 0, 0)),
                  pl.BlockSpec((SEQ, 2 * FF_PAD), lambda j: (0, j))],
        out_specs=pl.BlockSpec((SEQ, 2 * FF_PAD), lambda j: (0, j)),
        out_shape=SDS((SEQ, 2 * FF_HID), bf16), name="ffn_bwd_in", compiler_params=_params(("parallel",)),
    )(dff, w_down_stack, au)


def _final(act, w_down, x1, target, modv, g_final):
    def fn(t, v):
        (fft, x1t, tgt), (mv, g) = t, v
        x2 = x1t + mv[5:6] * fft
        r, xn = _rms(x2)
        err = xn * g - tgt
        dy = err * (1.0 / D)
        dx2 = _rms_bwd(r, xn, dy * g)
        return [dx2, dx2 * mv[5:6]], [_colsum(dy * xn), _colsum(dx2 * fft), _colsum(err * err) * (0.5 / D)]
    return _matmul_rows("nn", act, w_down, x1, fn, [x1, target], [modv, g_final], [(D, f32), (D, bf16)], [D, D, D],
                        name="final")


def _mid_bwd(dau, w_stack, after, x1, dx2, mix, modv, g_ffn):
    def fn(t, v):
        (dh, x1t, dx2t, mt), (mv, g) = t, v
        r, xn = _rms(x1t)
        dn = dh * (1.0 + mv[4:5])
        dx1 = dx2t + _rms_bwd(r, xn, dn * g)
        return [dx1, dx1 * mv[2:3]], [_colsum(dh), _colsum(dh * (xn * g)), _colsum(dn * xn), _colsum(dx1 * mt)]
    return _matmul_rows("nt_stack", dau, w_stack, after, fn, [x1, dx2, mix], [modv, g_ffn], [(D, f32), (D, bf16)],
                        [D, D, D, D], name="mid_bwd")


def _first_bwd(dproj, w_stack, after, x, dx1, modv, g_mix):
    def fn(t, v):
        (dh, xt, dx1t), (mv, g) = t, v
        r, xn = _rms(xt)
        dn = dh * (1.0 + mv[1:2])
        return [dx1t + _rms_bwd(r, xn, dn * g)], [_colsum(dh), _colsum(dh * (xn * g)), _colsum(dn * xn)]
    return _matmul_rows("nt_stack", dproj, w_stack, after, fn, [x, dx1], [modv, g_mix], [(D, f32)], [D, D, D],
                        name="first_bwd")


def _merge_bwd(dmix, w_o, after, ya, yb, proj):
    def fn(t, v):
        dm, ya_t, yb_t, ga, gb = t
        sa, sb = _sigmoid(ga), _sigmoid(gb)
        return [dm * sa, dm * sb, dm * ya_t * (sa * (1.0 - sa)), dm * yb_t * (sb * (1.0 - sb))], []
    return _matmul_rows("nt", dmix, w_o, after, fn, [ya, yb, (proj, D, C_GA // D), (proj, D, C_GB // D)], [],
                        [(D, bf16), (D, bf16), (D, bf16), (D, bf16)], [], name="merge_bwd")


def _rope_tables():
    half = ROT_DIM // 2
    pos = np.arange(SEQ, dtype=np.float32)
    inv_freq = np.float32(ROPE_THETA) ** (-np.arange(0, ROT_DIM, 2, dtype=np.float32) / np.float32(ROT_DIM))
    ang = pos[:, None] * inv_freq[None, :].astype(np.float32)
    cos, sin = np.cos(ang).astype(np.float32), np.sin(ang).astype(np.float32)
    pad = np.zeros((SEQ, HEAD_DIM - ROT_DIM), np.float32)
    zero = np.zeros((SEQ, half), np.float32)
    c_head = np.concatenate([cos, cos, pad + 1.0], axis=1)
    lo_head = np.concatenate([-sin, zero, pad], axis=1)
    hi_head = np.concatenate([zero, sin, pad], axis=1)
    return tuple(jnp.asarray(np.concatenate([t, t], axis=1)) for t in (c_head, lo_head, hi_head))


def _over_heads(tables):
    return [jnp.tile(t, (1, DIL_W // LANES)) for t in tables]


def _rope_fwd(proj, tables):
    half = ROT_DIM // 2

    def fn(t, v):
        q, k = t[:2]
        c, lo, hi = _over_heads(t[2:])
        rot = lambda z: z * c + pltpu.roll(z, DIL_W - half, 1) * lo + pltpu.roll(z, half, 1) * hi
        return [rot(q) * ATT_SCALE, rot(k)], []
    return _rowwise(fn, "rope_fwd", [(proj, DIL_W, C_QB // DIL_W), (proj, DIL_W, C_KB // DIL_W)]
                    + [(tb, LANES, 0) for tb in tables], [], [(DIL_W, f32)] * 2)


def _rope_bwd(dqs, dks, tables):
    half = ROT_DIM // 2

    def fn(t, v):
        dq_t, dk_t = jnp.concatenate(t[:N_GROUPS], axis=1), jnp.concatenate(t[N_GROUPS:2 * N_GROUPS], axis=1)
        c, lo, hi = _over_heads(t[2 * N_GROUPS:])
        rot_t = lambda z: z * c + pltpu.roll(z * lo, half, 1) + pltpu.roll(z * hi, DIL_W - half, 1)
        return [rot_t(dq_t), rot_t(dk_t)], []
    return _rowwise(fn, "rope_bwd", [(a, DIL_OUT_W, 0) for a in (*dqs, *dks)] + [(tb, LANES, 0) for tb in tables],
                    [], [(DIL_W, bf16), (DIL_W, bf16)])


def _head_bcast_sum(d):
    lane = lax.broadcasted_iota(jnp.int32, d.shape, 1)
    out = jnp.zeros_like(d)
    for h in range(d.shape[1] // HEAD_DIM):
        sel = (lane >= h * HEAD_DIM) & (lane < (h + 1) * HEAD_DIM)
        out = jnp.where(sel, jnp.sum(jnp.where(sel, d, 0.0), axis=1, keepdims=True), out)
    return out


def _dil_combine(outs, lses):
    def fn(t, v):
        o0, o1, o2, l0, l1, l2 = t
        m = jnp.maximum(jnp.maximum(l0, l1), l2)
        w0, w1, w2 = jnp.exp(l0 - m), jnp.exp(l1 - m), jnp.exp(l2 - m)
        tot = w0 + w1 + w2
        return [(w0 * o0 + w1 * o1 + w2 * o2) / tot, m + jnp.log(tot)], []
    w = DIL_OUT_W
    return _rowwise(fn, "dil_combine", [(t, w, 0) for t in (*outs, *lses)], [], [(w, f32), (w, f32)])


def _dil_delta(dyb_h, yb_h):
    def fn(t, v):
        return [_head_bcast_sum(t[0] * t[1])], []
    return _rowwise(fn, "dil_delta", [(dyb_h, DIL_OUT_W, 0), (yb_h, DIL_OUT_W, 0)], [], [(DIL_OUT_W, f32)])[0]


def _adamw_math(wt, gt, mt, vt):
    mn = ADAM_B1 * mt + (1.0 - ADAM_B1) * gt
    vn = ADAM_B2 * vt + (1.0 - ADAM_B2) * (gt * gt)
    m_hat = mn / (1.0 - ADAM_B1 ** ADAM_STEP)
    v_hat = vn / (1.0 - ADAM_B2 ** ADAM_STEP)
    return -ADAM_LR * (m_hat / (jnp.sqrt(v_hat) + ADAM_EPS) + ADAM_WD * wt), mn, vn


def _adamw(w, g, m, v, name):
    shape = w.shape
    if w.ndim == 1:
        w, g, m, v = (t.reshape(1, -1) for t in (w, g, m, v))
    rows, cols = w.shape
    tile = 256 if rows % 256 == 0 and rows > 512 else rows

    def fn(t, _):
        return list(_adamw_math(*t)), []
    delta, mn, vn = _rowwise(fn, name, [(w, cols, 0), (g, cols, 0), (m, cols, 0), (v, cols, 0)], [],
                             [(cols, f32)] * 3, tile=tile)
    return delta.reshape(shape), mn.reshape(shape), vn.reshape(shape)


def _adamw_by_planes(w, g, m, v, name, most=128):
    planes, _, width = w.shape
    tile = max(t for t in range(1, most + 1) if planes % t == 0)

    def body(w_ref, g_ref, m_ref, v_ref, d_ref, mn_ref, vn_ref):
        d_ref[...], mn_ref[...], vn_ref[...] = _adamw_math(w_ref[...], g_ref[...], m_ref[...], v_ref[...])

    spec = pl.BlockSpec((tile, 1, width), lambda i: (i, 0, 0))
    return pl.pallas_call(body, grid=(planes // tile,), in_specs=[spec] * 4, out_specs=[spec] * 3,
                          out_shape=[SDS(w.shape, f32)] * 3, name=name,
                          compiler_params=_params(("parallel",)))(w, g, m, v)


def _ada_fwd(c_all, w_shard, b_shard):
    def body(c_ref, w_ref, b_ref, o_ref):
        cv = c_ref[...]
        sc = (cv * _sigmoid(cv)).astype(bf16)
        o_ref[...] = jnp.dot(sc, w_ref[...].astype(bf16), preferred_element_type=f32) + b_ref[...]
    return pl.pallas_call(body, out_shape=SDS((N_DEV, w_shard.shape[1]), f32), name="ada_fwd",
                          compiler_params=_params())(c_all, w_shard, b_shard)


def _ada_bwd(c_all, dmod_cols):
    def body(c_ref, d_ref, o_ref):
        cv = c_ref[...]
        sc = cv * _sigmoid(cv)
        o_ref[...] = lax.dot_general(sc, d_ref[...], (((0,), (0,)), ((), ())), precision=lax.Precision.HIGHEST,
                                     preferred_element_type=f32)
    return pl.pallas_call(body, out_shape=SDS((D, dmod_cols.shape[1]), f32), name="ada_bwd",
                          compiler_params=_params())(c_all, dmod_cols)


def _small_reduce(gathered, after):
    def body(g_ref, after_ref, o_ref, loss_ref):
        acc = g_ref[0]
        for d in range(1, N_DEV):
            acc = acc + g_ref[d]
        o_ref[...] = acc
        loss_ref[...] = jnp.zeros((1, LANES), f32) + jnp.sum(acc[10:11, :])
    return pl.pallas_call(body, out_shape=(SDS((SMALL_ROWS, D), f32), SDS((1, LANES), f32)), name="small_reduce",
                          in_specs=[pl.BlockSpec(memory_space=pltpu.VMEM), pl.BlockSpec(memory_space=pl.ANY)],
                          compiler_params=_params())(gathered, after)


FOX_BLK = 512
CUM_BLK = 128


def _fold_lanes(t, op):
    out = t[:, :LANES]
    for j in range(1, t.shape[1] // LANES):
        out = op(out, t[:, j * LANES:(j + 1) * LANES])
    return out


def _fox_gate_fwd(proj, b_pad):
    nblk = SEQ // CUM_BLK

    def body(f_ref, b_ref, col_ref):
        r = lax.broadcasted_iota(jnp.int32, (CUM_BLK, CUM_BLK), 0)
        c = lax.broadcasted_iota(jnp.int32, (CUM_BLK, CUM_BLK), 1)
        tri = (r >= c).astype(f32)
        carry = jnp.zeros((1, LANES), f32)
        for blk in range(nblk):
            z = f_ref[blk * CUM_BLK:(blk + 1) * CUM_BLK, :] + b_ref[...]
            logf = jnp.minimum(z, 0.0) - jnp.log1p(jnp.exp(-jnp.abs(z)))
            cs = jnp.dot(tri, logf, precision=lax.Precision.HIGHEST, preferred_element_type=f32) + carry
            col_ref[blk * CUM_BLK:(blk + 1) * CUM_BLK, :] = cs
            carry = cs[CUM_BLK - 1:CUM_BLK, :]

    return pl.pallas_call(
        body, grid=(1,), in_specs=[pl.BlockSpec((SEQ, LANES), lambda i: (0, C_F // LANES)),
                                   pl.BlockSpec((1, LANES), lambda i: (0, 0))],
        out_specs=pl.BlockSpec((SEQ, LANES), lambda i: (0, 0)),
        out_shape=SDS((SEQ, LANES), f32), name="fox_gate_fwd",
        compiler_params=_params(("arbitrary",)),
    )(proj, b_pad)


def _fox_gate_bwd(dF_row, proj, b_pad):
    nblk = SEQ // CUM_BLK

    def body(d_ref, f_ref, b_ref, df_ref, db_ref, col_ref):
        r = lax.broadcasted_iota(jnp.int32, (CUM_BLK, CUM_BLK), 0)
        c = lax.broadcasted_iota(jnp.int32, (CUM_BLK, CUM_BLK), 1)
        tri = (r <= c).astype(f32)
        lane = lax.broadcasted_iota(jnp.int32, (CUM_BLK, LANES), 1)
        col_ref[...] = d_ref[...].T
        carry = jnp.zeros((1, LANES), f32)
        total = jnp.zeros((1, LANES), f32)
        for blk in reversed(range(nblk)):
            rows = slice(blk * CUM_BLK, (blk + 1) * CUM_BLK)
            cs = jnp.dot(tri, col_ref[rows, :], precision=lax.Precision.HIGHEST, preferred_element_type=f32) + carry
            carry = cs[0:1, :]
            z = f_ref[rows, :] + b_ref[...]
            df = jnp.where(lane < N_FOX_HEADS, cs * _sigmoid(-z), 0.0)
            df_ref[rows, :] = df.astype(df_ref.dtype)
            total = total + _colsum(df)
        db_ref[...] = total

    return pl.pallas_call(
        body, grid=(1,), in_specs=[pl.BlockSpec((LANES, SEQ), lambda i: (0, 0)),
                                   pl.BlockSpec((SEQ, LANES), lambda i: (0, C_F // LANES)),
                                   pl.BlockSpec((1, LANES), lambda i: (0, 0))],
        out_specs=[pl.BlockSpec((SEQ, LANES), lambda i: (0, 0)), pl.BlockSpec((1, LANES), lambda i: (0, 0))],
        out_shape=(SDS((SEQ, LANES), bf16), SDS((1, LANES), f32)), name="fox_gate_bwd",
        scratch_shapes=[pltpu.VMEM((SEQ, LANES), f32)],
        compiler_params=_params(("arbitrary",)),
    )(dF_row, proj, b_pad)


def _nt(a, b):
    return lax.dot_general(a, b, (((1,), (1,)), ((), ())), preferred_element_type=f32)


def _tn(a, b):
    return lax.dot_general(a, b, (((0,), (0,)), ((), ())), preferred_element_type=f32)


def _fox_prep(proj, f_col):
    def fn(t, v):
        q, k, vv, fc = t
        lane = lax.broadcasted_iota(jnp.int32, (q.shape[0], LANES), 1)
        qs, ks = [], []
        for h in range(N_FOX_HEADS):
            pair, pos = divmod(h, 2)
            own = (lane >= pos * HEAD_DIM) & (lane < (pos + 1) * HEAD_DIM)
            base = (1 - pos) * HEAD_DIM
            f = fc[:, h:h + 1]
            hi = f.astype(bf16).astype(f32)
            mid = (f - hi).astype(bf16).astype(f32)
            lo = (f - hi) - mid
            one = jnp.ones_like(f)
            qa = jnp.where(own, q[:, pair * LANES:(pair + 1) * LANES] * ATT_SCALE, 0.0)
            ka = k[:, pair * LANES:(pair + 1) * LANES]
            for idx, (qv, kv) in enumerate([(hi, one), (mid, one), (lo, one), (one, -hi), (one, -mid), (one, -lo)]):
                sel = lane == base + idx
                qa = jnp.where(sel, qv, qa)
                ka = jnp.where(sel, kv, ka)
            qs.append(qa)
            ks.append(ka)
        return [jnp.concatenate(qs, axis=1), jnp.concatenate(ks, axis=1), vv], []
    w = N_FOX_HEADS * LANES
    return _rowwise(fn, "fox_prep", [(proj, FOX_W, C_QA // FOX_W), (proj, FOX_W, C_KA // FOX_W),
                                     (proj, FOX_W, C_VA // FOX_W), (f_col, LANES, 0)], [],
                    [(w, bf16), (w, bf16), (FOX_W, bf16)])


def _fox_fwd(q_aug, k_aug, v):
    blk = FOX_BLK
    npair = FOX_W // LANES

    def body(q_ref, k_ref, v_ref, o_ref, max_ref, sum_ref, s_scr):
        i = pl.program_id(1)
        tri = lax.broadcasted_iota(jnp.int32, (blk, blk), 0) >= lax.broadcasted_iota(jnp.int32, (blk, blk), 1)
        qh = [q_ref[:, h * LANES:(h + 1) * LANES] for h in range(2)]

        def logits(c, masked):
            off = pl.multiple_of(c * blk, blk)
            tops = []
            for h in range(2):
                s = _nt(qh[h], k_ref[pl.ds(off, blk), h * LANES:(h + 1) * LANES])
                if masked:
                    s = jnp.where(tri, s, NEG)
                s_scr[h, :, pl.ds(off, blk)] = s
                tops.append(_fold_lanes(s, jnp.maximum))
            return tops

        def pass_a(c, m):
            return tuple(jnp.maximum(a, b) for a, b in zip(m, logits(c, False)))

        m = lax.fori_loop(0, i, pass_a, tuple(jnp.full((blk, LANES), NEG, f32) for _ in range(2)))
        mx = [jnp.max(jnp.maximum(a, b), axis=1, keepdims=True) for a, b in zip(m, logits(i, True))]

        def pass_b(c, carry):
            off = pl.multiple_of(c * blk, blk)
            vv = v_ref[pl.ds(off, blk), :]
            new = []
            for h in range(2):
                l, acc = carry[h]
                p = jnp.exp(s_scr[h, :, pl.ds(off, blk)] - mx[h]).astype(bf16)
                new.append((l + _fold_lanes(p.astype(f32), jnp.add), acc + jnp.dot(p, vv, preferred_element_type=f32)))
            return tuple(new)

        zero = jnp.zeros((blk, LANES), f32)
        (l_a, acc_a), (l_b, acc_b) = lax.fori_loop(0, i + 1, pass_b, ((zero, zero), (zero, zero)))
        l_a = jnp.sum(l_a, axis=1, keepdims=True)
        l_b = jnp.sum(l_b, axis=1, keepdims=True)
        first = lax.broadcasted_iota(jnp.int32, (blk, LANES), 1) < HEAD_DIM
        o_ref[...] = jnp.where(first, acc_a / l_a, acc_b / l_b)
        max_ref[0] = jnp.where(first, mx[0], mx[1])
        sum_ref[0] = jnp.where(first, l_a, l_b)

    return pl.pallas_call(
        body, grid=(npair, SEQ // blk),
        in_specs=[pl.BlockSpec((blk, 2 * LANES), lambda p, i: (i, p)),
                  pl.BlockSpec((SEQ, 2 * LANES), lambda p, i: (0, p)),
                  pl.BlockSpec((SEQ, LANES), lambda p, i: (0, p))],
        out_specs=[pl.BlockSpec((blk, LANES), lambda p, i: (i, p))]
        + [pl.BlockSpec((1, blk, LANES), lambda p, i: (p, i, 0))] * 2,
        out_shape=(SDS((SEQ, FOX_W), f32),) + (SDS((npair, SEQ, LANES), f32),) * 2, name="fox_fwd",
        scratch_shapes=[pltpu.VMEM((2, blk, SEQ), f32)],
        compiler_params=_params(("parallel", "arbitrary")),
    )(q_aug, k_aug, v)


def _fox_bwd(q_aug, k_aug, v, do, o, row_max, row_sum, after):
    blk = FOX_BLK
    npair = FOX_W // LANES
    nblk = SEQ // blk

    def body(q_ref, k_ref, v_ref, do_ref, o_ref, max_ref, sum_ref, after_ref, dq_ref, dk_ref, dv_ref, df_ref, dq_acc,
             delta_ref, inv_ref):
        inv_ref[...] = 1.0 / sum_ref[0]
        lane_s = lax.broadcasted_iota(jnp.int32, (SEQ, LANES), 1)
        prod = do_ref[...].astype(bf16).astype(f32) * o_ref[...]
        d_a = jnp.sum(jnp.where(lane_s < HEAD_DIM, prod, 0.0), axis=1, keepdims=True)
        d_b = jnp.sum(jnp.where(lane_s >= HEAD_DIM, prod, 0.0), axis=1, keepdims=True)
        delta_ref[...] = jnp.where(lane_s < HEAD_DIM, d_a, d_b)
        dq_acc[...] = jnp.zeros_like(dq_acc)
        df_ref[...] = jnp.zeros_like(df_ref)
        lane = lax.broadcasted_iota(jnp.int32, (blk, LANES), 1)
        own = [lane < HEAD_DIM, lane >= HEAD_DIM]
        tri = lax.broadcasted_iota(jnp.int32, (blk, blk), 0) >= lax.broadcasted_iota(jnp.int32, (blk, blk), 1)

        def q_slab(qoff, h):
            return q_ref[pl.ds(qoff, blk), h * LANES:(h + 1) * LANES]

        def probs(qoff, h, k_h, masked):
            s = _nt(q_slab(qoff, h), k_h)
            if masked:
                s = jnp.where(tri, s, NEG)
            col = slice(h * HEAD_DIM, h * HEAD_DIM + 1)
            weights = jnp.exp(s - max_ref[0, pl.ds(qoff, blk), col]).astype(bf16).astype(f32)
            return weights * inv_ref[pl.ds(qoff, blk), col]

        def k_slabs(koff):
            return [k_ref[pl.ds(koff, blk), h * LANES:(h + 1) * LANES] for h in range(2)]

        def kv_step(kj, _):
            koff = pl.multiple_of(kj * blk, blk)
            k_aug = k_slabs(koff)
            k_own = [jnp.where(own[h], k_aug[h], jnp.zeros_like(k_aug[h])) for h in range(2)]
            vv = v_ref[pl.ds(koff, blk), :]
            v_own = [jnp.where(own[h], vv, jnp.zeros_like(vv)) for h in range(2)]

            def q_tile(qi, carry, masked):
                qoff = pl.multiple_of(qi * blk, blk)
                dd = do_ref[pl.ds(qoff, blk), :].astype(bf16)
                new, dq_add = [], None
                for h in range(2):
                    dk_h, dv_h, dcol = carry[h]
                    p = probs(qoff, h, k_aug[h], masked)
                    dl = p * (_nt(dd, v_own[h]) - delta_ref[pl.ds(qoff, blk), h * HEAD_DIM:h * HEAD_DIM + 1])
                    dlb = dl.astype(bf16)
                    part = jnp.dot(dlb, k_own[h], preferred_element_type=f32)
                    dq_add = part if dq_add is None else dq_add + part
                    new.append((dk_h + _tn(dlb, q_slab(qoff, h)), dv_h + _tn(p.astype(bf16), dd),
                                dcol + _colsum(dl)))
                dq_acc[pl.ds(qoff, blk), :] += dq_add * ATT_SCALE
                return tuple(new)

            zero = (jnp.zeros((blk, LANES), f32), jnp.zeros((blk, LANES), f32), jnp.zeros((1, blk), f32))
            carry = q_tile(kj, (zero, zero), True)
            (dk_a, dv_a, dcol_a), (dk_b, dv_b, dcol_b) = lax.fori_loop(
                kj + 1, nblk, lambda qi, cr: q_tile(qi, cr, False), carry)
            dk_ref[pl.ds(koff, blk), :] = jnp.where(own[0], dk_a, dk_b).astype(dk_ref.dtype)
            dv_ref[pl.ds(koff, blk), :] = jnp.where(own[0], dv_a, dv_b).astype(dv_ref.dtype)
            df_ref[0, 0:1, pl.ds(koff, blk)] = -dcol_a
            df_ref[0, 1:2, pl.ds(koff, blk)] = -dcol_b
            return 0

        lax.fori_loop(0, nblk, kv_step, 0)
        dq_ref[...] = dq_acc[...].astype(dq_ref.dtype)

    pair_aug = pl.BlockSpec((SEQ, 2 * LANES), lambda p: (0, p))
    slab = pl.BlockSpec((SEQ, LANES), lambda p: (0, p))
    per_pair = pl.BlockSpec((1, SEQ, LANES), lambda p: (p, 0, 0))
    rows = pl.BlockSpec((1, 8, SEQ), lambda p: (p, 0, 0))
    return pl.pallas_call(
        body, grid=(npair,),
        in_specs=[pair_aug, pair_aug, slab, slab, slab, per_pair, per_pair, pl.BlockSpec(memory_space=pl.ANY)],
        out_specs=[slab, slab, slab, rows],
        out_shape=(SDS((SEQ, FOX_W), bf16),) * 3 + (SDS((npair, 8, SEQ), f32),), name="fox_bwd",
        scratch_shapes=[pltpu.VMEM((SEQ, LANES), f32)] * 3,
        compiler_params=_params(("parallel",)),
    )(q_aug, k_aug, v, do, o, row_max, row_sum, after)


DIL_BLK = 128
DILATIONS = (1, 4, 16)
N_GROUPS = len(DILATIONS)
DIL_PAIRS = DIL_OUT_W // LANES


def _dil_blocks(d):
    r1 = lax.broadcasted_iota(jnp.int32, (2 * DIL_BLK, DIL_BLK), 0) & (DIL_BLK - 1)
    c1 = lax.broadcasted_iota(jnp.int32, (2 * DIL_BLK, DIL_BLK), 1)
    r2 = lax.broadcasted_iota(jnp.int32, (2 * DIL_BLK, 2 * DIL_BLK), 0) & (DIL_BLK - 1)
    c2 = lax.broadcasted_iota(jnp.int32, (2 * DIL_BLK, 2 * DIL_BLK), 1)
    band = ((c2 < DIL_BLK) & (c2 >= r2)) | ((c2 >= DIL_BLK) & (c2 - DIL_BLK <= r2))
    out = []
    for r in range(d):
        for b in range(SEQ // d // DIL_BLK):
            rows = pl.ds(r + d * DIL_BLK * b, DIL_BLK, stride=d)
            if b == 0:
                out.append((rows, rows, r1 >= c1))
            else:
                out.append((rows, pl.ds(r + d * DIL_BLK * (b - 1), 2 * DIL_BLK, stride=d), band))
    return out


def _dil_v_spec(g):
    return pl.BlockSpec((SEQ, LANES), lambda p: (0, C_VB // LANES + DIL_PAIRS * g + p))


def _stack_heads(t, first):
    zero = jnp.zeros_like(t)
    return jnp.concatenate([jnp.where(first, t, zero), jnp.where(first, zero, t)], axis=0)


def _dil_fwd(q, k, v, g):
    def body(q_ref, k_ref, v_ref, o_ref, lse_ref):
        first = lax.broadcasted_iota(jnp.int32, (DIL_BLK, LANES), 1) < HEAD_DIM
        for rows, krows, mask in _dil_blocks(DILATIONS[g]):
            qv, kk, vv = q_ref[rows, :].astype(bf16), k_ref[krows, :].astype(bf16), v_ref[krows, :].astype(bf16)
            s = jnp.where(mask, _nt(_stack_heads(qv, first), kk), NEG)
            m = jnp.max(s, axis=1, keepdims=True)
            p = jnp.exp(s - m)
            l = jnp.sum(p, axis=1, keepdims=True)
            out = jnp.dot(p.astype(bf16), vv, preferred_element_type=f32) / l
            lse = m + jnp.log(l)
            o_ref[rows, :] = jnp.where(first, out[:DIL_BLK], out[DIL_BLK:])
            lse_ref[rows, :] = jnp.where(first, lse[:DIL_BLK], lse[DIL_BLK:])

    grouped = pl.BlockSpec((SEQ, LANES), lambda p: (0, DIL_PAIRS * g + p))
    own = pl.BlockSpec((SEQ, LANES), lambda p: (0, p))
    shape = SDS((SEQ, DIL_OUT_W), f32)
    return pl.pallas_call(
        body, grid=(DIL_PAIRS,), in_specs=[grouped, grouped, _dil_v_spec(g)], out_specs=[own] * 2,
        out_shape=(shape, shape),
        name=f"dil_fwd_{DILATIONS[g]}", compiler_params=_params(("parallel",)),
    )(q, k, v)


def _dil_bwd(q, k, v, do, lse, delta, g):
    def body(q_ref, k_ref, v_ref, do_ref, lse_ref, dl_ref, dq_ref, dk_ref, dv_ref):
        first = lax.broadcasted_iota(jnp.int32, (DIL_BLK, LANES), 1) < HEAD_DIM
        dk_ref[...] = jnp.zeros_like(dk_ref)
        dv_ref[...] = jnp.zeros_like(dv_ref)
        for rows, krows, mask in _dil_blocks(DILATIONS[g]):
            qv, kk, vv = q_ref[rows, :].astype(bf16), k_ref[krows, :].astype(bf16), v_ref[krows, :].astype(bf16)
            lsev, delv = lse_ref[rows, :], dl_ref[rows, :]
            q2 = _stack_heads(qv, first)
            do2 = _stack_heads(do_ref[rows, :].astype(bf16), first)
            per_head = lambda t: jnp.concatenate([t[:, 0:1], t[:, HEAD_DIM:HEAD_DIM + 1]], axis=0)
            p = jnp.exp(jnp.where(mask, _nt(q2, kk), NEG) - per_head(lsev))
            dl = (p * (_nt(do2, vv) - per_head(delv))).astype(bf16)
            dq = jnp.dot(dl, kk, preferred_element_type=f32)
            dq_ref[rows, :] = jnp.where(first, dq[:DIL_BLK], dq[DIL_BLK:]) * ATT_SCALE
            dk_ref[krows, :] += _tn(dl, q2)
            dv_ref[krows, :] += _tn(p.astype(bf16), do2)

    grouped = pl.BlockSpec((SEQ, LANES), lambda p: (0, DIL_PAIRS * g + p))
    own = pl.BlockSpec((SEQ, LANES), lambda p: (0, p))
    shape = SDS((SEQ, DIL_OUT_W), f32)
    return pl.pallas_call(
        body, grid=(DIL_PAIRS,), in_specs=[grouped, grouped, _dil_v_spec(g)] + [own] * 3, out_specs=[own] * 3,
        out_shape=(shape, shape, shape), name=f"dil_bwd_{DILATIONS[g]}", compiler_params=_params(("parallel",)),
    )(q, k, v, do, lse, delta)


def _position():
    return lax.axis_index("x"), lax.axis_index("y"), lax.axis_index("c")


def _all_gather(block, name, after=None):
    after = [] if after is None else [after]

    def body(x_ref, *refs):
        out_ref, send_sems, recv_sems, local_sem = refs[len(after):]
        x, y, c = _position()
        me, sibling = (x, y, c), (x, y, 1 - c)
        chips = [(1 - x, y), (x, 1 - y), (1 - x, 1 - y)]

        def slot(px, py, pc):
            return out_ref.at[4 * px + 2 * py + pc]

        def copy(k, blk, to, src=None):
            return pltpu.make_async_remote_copy(
                src_ref=slot(*blk) if src is None else src, dst_ref=slot(*blk),
                send_sem=send_sems.at[k], recv_sem=recv_sems.at[k], device_id=to, device_id_type=MESH)

        mine = pltpu.make_async_copy(x_ref, slot(*me), local_sem)
        mine.start()
        first = [copy(0, me, sibling, src=x_ref)]
        first += [copy(1 + j, me, (*chip, c), src=x_ref) for j, chip in enumerate(chips)]
        for cp in first:
            cp.start()
        passed = [copy(4 + j, (*chip, c), sibling) for j, chip in enumerate(chips)]
        for j, chip in enumerate(chips):
            copy(1 + j, (*chip, c), me).wait_recv()
            passed[j].start()
        copy(0, sibling, me).wait_recv()
        for j, chip in enumerate(chips):
            copy(4 + j, (*chip, 1 - c), me).wait_recv()
        for cp in first + passed:
            cp.wait_send()
        mine.wait()

    return pl.pallas_call(
        body, out_shape=SDS((N_DEV,) + block.shape, block.dtype),
        in_specs=[pl.BlockSpec(memory_space=pl.ANY)] * (1 + len(after)), out_specs=pl.BlockSpec(memory_space=pl.ANY),
        scratch_shapes=[pltpu.SemaphoreType.DMA((7,)), pltpu.SemaphoreType.DMA((7,)), pltpu.SemaphoreType.DMA],
        name=name,
    )(block, *after)


HBM_SPEC = pl.BlockSpec(memory_space=pltpu.HBM)
SEM_SPEC = pl.BlockSpec(memory_space=pltpu.SEMAPHORE)
SPLIT_COPY = pltpu.CompilerParams(has_side_effects=pltpu.SideEffectType.DATAFLOW_SIDE_EFFECTING)


def _in_hbm(t):
    return pltpu.with_memory_space_constraint(t, pltpu.HBM)


def _pair_copies(g_refs, land_refs, send_sems, recv_sems):
    x, y, c = _position()
    return [pltpu.make_async_remote_copy(
        src_ref=g.at[2 * k + (1 - c)], dst_ref=land.at[k], send_sem=send_sems.at[4 * a + k],
        recv_sem=recv_sems.at[4 * a + k], device_id=(x, y, 1 - c), device_id_type=MESH)
        for a, (g, land) in enumerate(zip(g_refs, land_refs, strict=True)) for k in range(4)]


def _chip_copies(t_refs, land_refs, send_sems, recv_sems):
    x, y, c = _position()
    chips = [(1 - x, y), (x, 1 - y), (1 - x, 1 - y)]
    return [pltpu.make_async_remote_copy(
        src_ref=t.at[2 * px + py], dst_ref=land.at[j], send_sem=send_sems.at[3 * a + j],
        recv_sem=recv_sems.at[3 * a + j], device_id=(px, py, c), device_id_type=MESH)
        for a, (t, land) in enumerate(zip(t_refs, land_refs, strict=True)) for j, (px, py) in enumerate(chips)]


_ROUNDS = {"pair": (_pair_copies, 4), "chip": (_chip_copies, 3)}


def _exchange_start(kind, ts, name):
    copies, slots = _ROUNDS[kind]
    n = len(ts)
    lands = [_in_hbm(lax.empty((slots,) + t.shape[1:], t.dtype)) for t in ts]

    def body(*refs):
        for cp in copies(refs[:n], refs[n:2 * n], refs[2 * n], refs[2 * n + 1]):
            cp.start()
        refs[-1][...] = jnp.zeros_like(refs[-1])

    sems = pltpu.SemaphoreType.DMA((slots * n,))
    res = pl.pallas_call(
        body, name=name, in_specs=[HBM_SPEC] * (2 * n),
        out_shape=(sems, sems, *[pltpu.HBM(t.shape, t.dtype) for t in (*ts, *lands)], SDS((8, LANES), f32)),
        out_specs=(SEM_SPEC, SEM_SPEC, *[HBM_SPEC] * (2 * n), pl.BlockSpec(memory_space=pltpu.VMEM)),
        input_output_aliases={i: 2 + i for i in range(2 * n)}, compiler_params=SPLIT_COPY,
    )(*[_in_hbm(t) for t in ts], *lands)
    return res[:-1], res[-1]


def _exchange_wait(kind, state, after, name):
    copies, _ = _ROUNDS[kind]
    send_sems, recv_sems, *arrays = state
    n = len(arrays) // 2

    def body(*refs):
        for cp in copies(refs[:n], refs[n:2 * n], refs[2 * n], refs[2 * n + 1]):
            cp.wait_send()
            cp.wait_recv()

    res = pl.pallas_call(
        body, name=name, in_specs=[HBM_SPEC] * (2 * n) + [SEM_SPEC, SEM_SPEC, pl.BlockSpec(memory_space=pl.ANY)],
        out_shape=[pltpu.HBM(t.shape, t.dtype) for t in arrays], out_specs=[HBM_SPEC] * (2 * n),
        input_output_aliases={i: i for i in range(2 * n)}, compiler_params=SPLIT_COPY,
    )(*arrays, send_sems, recv_sems, after)
    return res[:n], res[n:]


def _gather_copies(x_refs, out_refs, send_sems, recv_sems):
    x, y, c = _position()
    peers = [(x, y, 1 - c), (1 - x, y, c), (x, 1 - y, c), (1 - x, 1 - y, c)]
    sends, arrivals = [], []
    for a, (x_ref, out_ref) in enumerate(zip(x_refs, out_refs, strict=True)):
        for k, (px, py, pc) in enumerate(peers):
            sems = dict(send_sem=send_sems.at[4 * a + k], recv_sem=recv_sems.at[4 * a + k],
                        device_id=(px, py, pc), device_id_type=MESH)
            sends.append(pltpu.make_async_remote_copy(src_ref=x_ref, dst_ref=out_ref.at[4 * x + 2 * y + c], **sems))
            arrivals.append(pltpu.make_async_remote_copy(src_ref=x_ref, dst_ref=out_ref.at[4 * px + 2 * py + pc],
                                                         **sems))
    return sends, arrivals


def _gather_start(blocks, after, name):
    n = len(blocks)
    outs = [_in_hbm(lax.empty((N_DEV,) + b.shape, b.dtype)) for b in blocks]

    def body(*refs):
        sends, _ = _gather_copies(refs[:n], refs[n:2 * n], refs[2 * n + 1], refs[2 * n + 2])
        for cp in sends:
            cp.start()
        refs[-1][...] = jnp.zeros_like(refs[-1])

    sems = pltpu.SemaphoreType.DMA((4 * n,))
    res = pl.pallas_call(
        body, name=name, in_specs=[HBM_SPEC] * (2 * n) + [pl.BlockSpec(memory_space=pl.ANY)],
        out_shape=(sems, sems, *[pltpu.HBM(t.shape, t.dtype) for t in (*blocks, *outs)], SDS((8, LANES), f32)),
        out_specs=(SEM_SPEC, SEM_SPEC, *[HBM_SPEC] * (2 * n), pl.BlockSpec(memory_space=pltpu.VMEM)),
        input_output_aliases={i: 2 + i for i in range(2 * n)}, compiler_params=SPLIT_COPY,
    )(*[_in_hbm(b) for b in blocks], *outs, after)
    return res[:-1], res[-1]


def _gather_wait(state, after, name):
    send_sems, recv_sems, *arrays = state
    n = len(arrays) // 2

    def body(*refs):
        sends, arrivals = _gather_copies(refs[:n], refs[n:2 * n], refs[2 * n], refs[2 * n + 1])
        for cp in sends:
            cp.wait_send()
        for cp in arrivals:
            cp.wait_recv()

    res = pl.pallas_call(
        body, name=name, in_specs=[HBM_SPEC] * (2 * n) + [SEM_SPEC, SEM_SPEC, pl.BlockSpec(memory_space=pl.ANY)],
        out_shape=[pltpu.HBM(t.shape, t.dtype) for t in arrays], out_specs=[HBM_SPEC] * (2 * n),
        input_output_aliases={i: i for i in range(2 * n)}, compiler_params=SPLIT_COPY,
    )(*arrays, send_sems, recv_sems, after)
    return res[:n], res[n:]


def _gather_finish(partial, name):
    n = len(partial)

    def body(*refs):
        in_refs, out_refs = refs[:n], refs[n:2 * n]
        send_sems, recv_sems = refs[2 * n:]
        x, y, c = _position()
        chips = [(1 - x, y), (x, 1 - y), (1 - x, 1 - y)]
        copies = []
        for a in range(n):
            for j, (px, py) in enumerate(chips):
                cp = pltpu.make_async_remote_copy(
                    src_ref=in_refs[a].at[4 * px + 2 * py + c], dst_ref=out_refs[a].at[4 * px + 2 * py + c],
                    send_sem=send_sems.at[a, j], recv_sem=recv_sems.at[a, j], device_id=(x, y, 1 - c),
                    device_id_type=MESH)
                cp.start()
                copies.append(cp)
        for a in range(n):
            for j, (px, py) in enumerate(chips):
                pltpu.make_async_remote_copy(
                    src_ref=in_refs[a].at[4 * px + 2 * py + (1 - c)], dst_ref=out_refs[a].at[4 * px + 2 * py + (1 - c)],
                    send_sem=send_sems.at[a, j], recv_sem=recv_sems.at[a, j], device_id=(x, y, 1 - c),
                    device_id_type=MESH).wait_recv()
        for cp in copies:
            cp.wait_send()

    hbm = pl.BlockSpec(memory_space=pl.ANY)
    return pl.pallas_call(
        body, out_shape=[SDS(p.shape, p.dtype) for p in partial], in_specs=[hbm] * n, out_specs=[hbm] * n,
        input_output_aliases={a: a for a in range(n)},
        scratch_shapes=[pltpu.SemaphoreType.DMA((n, 3)), pltpu.SemaphoreType.DMA((n, 3))],
        name=name,
    )(*partial)


def _row_tile(rows):
    return 512 if rows % 512 == 0 and rows > 512 else rows


def _pair_add(g, r1, core, name):
    def body(c_ref, g_ref, r_ref, o_ref):
        o_ref[...] = (g_ref[...].astype(f32) + r_ref[...].astype(f32)).astype(o_ref.dtype)

    rows, cols = g.shape[1:]
    tile = _row_tile(rows)
    blk = (1, tile, cols)
    return pl.pallas_call(
        body, out_shape=SDS((4, rows, cols), g.dtype), name=name,
        grid_spec=pltpu.PrefetchScalarGridSpec(
            num_scalar_prefetch=1, grid=(4, rows // tile),
            in_specs=[pl.BlockSpec(blk, lambda k, i, c_ref: (2 * k + c_ref[0], i, 0)),
                      pl.BlockSpec(blk, lambda k, i, c_ref: (k, i, 0))],
            out_specs=pl.BlockSpec(blk, lambda k, i, c_ref: (k, i, 0))),
        compiler_params=_params(("parallel", "arbitrary")),
    )(core, g, r1)


def _chip_add(t, r2, chip, name, transposed=False, planes=None):
    def body(c_ref, t_ref, r_ref, o_ref):
        s = ((t_ref[0].astype(f32) + r_ref[0].astype(f32)) + r_ref[1].astype(f32)) + r_ref[2].astype(f32)
        if planes:
            o_ref[...] = s.T[:planes][:, None, :]
        else:
            o_ref[...] = s.T if transposed else s

    rows, cols = t.shape[1:]
    tile = _row_tile(rows)
    if planes:
        out_shape, out_spec = (planes, 1, rows), pl.BlockSpec((planes, 1, tile), lambda i, c_ref: (0, 0, i))
    elif transposed:
        out_shape, out_spec = (cols, rows), pl.BlockSpec((cols, tile), lambda i, c_ref: (0, i))
    else:
        out_shape, out_spec = (rows, cols), pl.BlockSpec((tile, cols), lambda i, c_ref: (i, 0))
    return pl.pallas_call(
        body, out_shape=SDS(out_shape, f32), name=name,
        grid_spec=pltpu.PrefetchScalarGridSpec(
            num_scalar_prefetch=1, grid=(rows // tile,),
            in_specs=[pl.BlockSpec((1, tile, cols), lambda i, c_ref: (c_ref[0], i, 0)),
                      pl.BlockSpec((3, tile, cols), lambda i, c_ref: (0, i, 0))],
            out_specs=out_spec),
        compiler_params=_params(("arbitrary",)),
    )(chip, t, r2)


def _pad_to(t, axis, size):
    pads = [(0, 0)] * t.ndim
    pads[axis] = (0, size - t.shape[axis])
    return jnp.pad(t, pads)


_REF_COLS = {"qa": (0, FOX_W), "ka": (FOX_W, FOX_W), "va": (2 * FOX_W, FOX_W), "f": (3 * FOX_W, N_FOX_HEADS)}
_REF_COLS.update({n: (3 * FOX_W + N_FOX_HEADS + i * DIL_W, DIL_W) for i, n in enumerate(("qb", "kb", "vb"))})
_REF_COLS.update({n: (3 * FOX_W + N_FOX_HEADS + 3 * DIL_W + i * D, D) for i, n in enumerate(("ga", "gb"))})
_REF_ORDER = ("qa", "ka", "va", "f", "qb", "kb", "vb", "ga", "gb")


def _place_cols(sources, src_of, out_cols, name, row_block=512):
    arrays = [s[0] if isinstance(s, tuple) else s for s in sources]
    widths = [a.shape[-1] for a in arrays]
    rows = arrays[0].shape[-2]
    plan = []
    for t in range(out_cols // LANES):
        segs, c, end = [], t * LANES, (t + 1) * LANES
        while c < end:
            s = src_of(c)
            if s is None:
                c += 1
                continue
            n = 1
            while c + n < end and src_of(c + n) == (s[0], s[1] + n):
                n += 1
            segs.append((s[0], s[1], c - t * LANES, n))
            c += n
        plan.append(segs)

    def body(*refs):
        o_ref = refs[-1]
        for t, segs in enumerate(plan):
            acc = None
            for si, c0, o0, n in segs:
                a0 = c0 // LANES * LANES
                wide = min(2 * LANES, widths[si] - a0)
                win = refs[si][0, :, a0:a0 + wide] if isinstance(sources[si], tuple) else refs[si][:, a0:a0 + wide]
                r = lax.broadcasted_iota(jnp.int32, (wide, LANES), 0)
                c = lax.broadcasted_iota(jnp.int32, (wide, LANES), 1)
                pick = ((r - (c0 - a0) == c - o0) & (c >= o0) & (c < o0 + n)).astype(bf16)
                part = jnp.dot(win.astype(bf16), pick, preferred_element_type=f32)
                acc = part if acc is None else acc + part
            tile = jnp.zeros((row_block, LANES), f32) if acc is None else acc
            o_ref[:, t * LANES:(t + 1) * LANES] = tile.astype(o_ref.dtype)

    def spec(s):
        if isinstance(s, tuple):
            j = s[1]
            return pl.BlockSpec((1, row_block, s[0].shape[-1]), lambda i: (j, i, 0))
        return pl.BlockSpec((row_block, s.shape[-1]), lambda i: (i, 0))

    return pl.pallas_call(
        body, grid=(rows // row_block,), in_specs=[spec(s) for s in sources],
        out_specs=pl.BlockSpec((row_block, out_cols), lambda i: (i, 0)), out_shape=SDS((rows, out_cols), bf16),
        name=name, compiler_params=_params(("parallel",)),
    )(*arrays)


def _ref_piece(r):
    for name in _REF_ORDER:
        lo, width = _REF_COLS[name]
        if lo <= r < lo + width:
            return name, r - lo
    raise ValueError(r)


def _shard_pad_cols(pieces):
    names = [n for n in _REF_ORDER if n != "vb"]
    sources = [pieces[n] for n in names] + list(pieces["vb"])

    def src_of(c):
        j, i = divmod(c, W_IN_PAD)
        if i >= W_IN_SH:
            return None
        name, col = _ref_piece(j * W_IN_SH + i)
        if name == "vb":
            return len(names) + col // DIL_OUT_W, col % DIL_OUT_W
        return names.index(name), col

    return _place_cols(sources, src_of, N_DEV * W_IN_PAD, "place_dproj")


_SLABS = {"ga": C_GA, "gb": C_GB, "qb": C_QB, "kb": C_KB, "vb": C_VB, "qa": C_QA, "ka": C_KA, "va": C_VA, "f": C_F}


def _slab_w_in(stack):
    def src_of(c):
        for name, start in _SLABS.items():
            lo, width = _REF_COLS[name]
            if start <= c < start + width:
                return divmod(lo + c - start, W_IN_SH)
        return None

    return _place_cols([(stack, j) for j in range(N_DEV)], src_of, PROJ_W, "place_w_in")


def kernel(x, c, w_ada, b_ada, g_mix, w_in, b_fgate, w_br_a, w_br_b, w_out, g_ffn, w_ffn_gate, w_ffn_up, w_ffn_down, g_final, loss_target, m_w_ada, m_b_ada, m_g_mix, m_w_in, m_b_fgate, m_w_br_a, m_w_br_b, m_w_out, m_g_ffn, m_w_ffn_gate, m_w_ffn_up, m_w_ffn_down, m_g_final, v_w_ada, v_b_ada, v_g_mix, v_w_in, v_b_fgate, v_w_br_a, v_w_br_b, v_w_out, v_g_ffn, v_w_ffn_gate, v_w_ffn_up, v_w_ffn_down, v_g_final):
    px, py, pc = _position()
    dev = 4 * px + 2 * py + pc
    x2d, tgt = x[0], loss_target[0]

    c_all = _all_gather(c, "gather_c").reshape(N_DEV, D)
    ada_cols = w_ada.shape[2]
    b_shard = lax.dynamic_slice(b_ada, (0, dev * ada_cols), (1, ada_cols))
    mod_shard = _ada_fwd(c_all, w_ada[0], b_shard)
    mod_all = _all_gather(mod_shard, "gather_mod")
    modv = lax.dynamic_index_in_dim(mod_all, dev, axis=1, keepdims=False).reshape(6, D)
    h1 = _pre1(x2d, modv, g_mix)

    w_in_s = _all_gather(_pad_to(w_in[0], 1, W_IN_PAD).astype(bf16), "gather_w_in")
    gate_up = jnp.concatenate([_pad_to(w_ffn_gate[0], 1, FF_PAD), _pad_to(w_ffn_up[0], 1, FF_PAD)], axis=1)
    later = [w_br_a[0], w_br_b[0], w_out[0], gate_up, _pad_to(w_ffn_down[0], 0, FF_PAD)]
    later_state, later_token = _gather_start([t.astype(bf16) for t in later], w_in_s, "gather_rest_start")
    w_in_p = _slab_w_in(w_in_s)

    proj = _matmul(h1, w_in_p, name="mm_proj", tm=SEQ, tn=896, after=later_token)
    b_pad = jnp.pad(b_fgate, ((0, 0), (0, LANES - N_FOX_HEADS)))
    q_aug, k_aug, va = _fox_prep(proj, _fox_gate_fwd(proj, b_pad))
    ya_h, max_a, sum_a = _fox_fwd(q_aug, k_aug, va)

    tables = _rope_tables()
    qb_r, kb_r = _rope_fwd(proj, tables)
    by_group = [_dil_fwd(qb_r, kb_r, proj, grp) for grp in range(N_GROUPS)]
    yb_h, lse_b = _dil_combine([o for o, _ in by_group], [l for _, l in by_group])

    both_done = ya_h[:8, :LANES] + yb_h[:8, :LANES]
    mine, arrived = _gather_wait(later_state, both_done, "gather_rest_wait")
    w_a_s, w_b_s, w_o_s, w_gu_s, w_d_s = [
        lax.dynamic_update_slice(stack, block[None], (dev, 0, 0))
        for stack, block in zip(_gather_finish(arrived, "gather_rest_finish"), mine, strict=True)]
    w_o = w_o_s.reshape(D, D)
    w_d = w_d_s.reshape(FF_HID, D)
    ya = _matmul_stack(ya_h, w_a_s, name="mm_br_a")
    yb = _matmul_stack(yb_h, w_b_s, name="mm_br_b")

    merged, mix, x1, h2 = _post1(ya, yb, proj, w_o, x2d, modv, g_ffn)
    act, au = _ffn_in(h2, w_gu_s)

    dx2, dff, dg_final, dga_f, loss_lanes = _final(act, w_d, x1, tgt, modv, g_final.reshape(1, D))
    dau = _ffn_bwd_in(dff, w_d_s, au)

    core = pc.astype(jnp.int32).reshape(1)
    chip = (2 * px + py).astype(jnp.int32).reshape(1)

    def pair_done(state, after, tags, name):
        mine, theirs = _exchange_wait("pair", state, after, "pair_wait_" + name)
        sums = [_pair_add(g, r, core, "pair_add_" + t) for g, r, t in zip(mine, theirs, tags)]
        return _exchange_start("chip", sums, "chip_start_" + name)

    def from_chips(state, after, tags, name, transposed=None):
        sums, got = _exchange_wait("chip", state, after, "chip_wait_" + name)
        flips = transposed or [False] * len(tags)
        return [_chip_add(p, r, chip, "chip_add_" + t, f) for p, r, t, f in zip(sums, got, tags, flips)]

    g_gu = _matmul(h2, dau, ta=True, by_shard=True, out_dtype=bf16, name="mm_g_ffn_in", tm=D, tn=2 * FF_PAD)
    g_d = _matmul(act, dff, ta=True, out_dtype=bf16, name="mm_g_down", tm=FF_HID // 2, tn=512)
    ffn_tags = ["gu", "down"]
    ffn_pair, ffn_pair_token = _exchange_start("pair", [g_gu, g_d.reshape(N_DEV, FF_PAD, D)], "pair_start_ffn")

    dx1, dmix, dsh_f, dsc_f, dg_ffn, dga_m = _mid_bwd(dau, w_gu_s, ffn_pair_token, x1, dx2, mix, modv, g_ffn)
    ffn_state, ffn_token = pair_done(ffn_pair, dx1, ffn_tags, "ffn")
    dya, dyb, dga, dgb = _merge_bwd(dmix, w_o, ffn_token, ya, yb, proj)
    dya_h = _matmul_stack(dya, w_a_s, tb=True, name="mm_d_ya")
    dyb_h = _matmul_stack(dyb, w_b_s, tb=True, name="mm_d_yb")

    g_o = _matmul(merged, dmix, ta=True, out_dtype=bf16, name="mm_g_out", tm=D, tn=512)
    g_a = _matmul_stack(ya_h, dya, ta=True, out_dtype=bf16, name="mm_g_br_a")
    g_b = _matmul_stack(yb_h, dyb, ta=True, out_dtype=bf16, name="mm_g_br_b")
    rows_a, rows_b = FOX_W * W_BR_SH // D, DIL_OUT_W * W_BR_SH // D
    g_small = jnp.concatenate([g_a.reshape(N_DEV, rows_a, D), g_b.reshape(N_DEV, rows_b, D),
                               g_o.reshape(N_DEV, W_BR_SH, D)], axis=1)
    small_pair, small_pair_token = _exchange_start("pair", [g_small], "pair_start_small")

    dqa, dka, dva, dF = _fox_bwd(q_aug, k_aug, va, dya_h, ya_h, max_a, sum_a, small_pair_token)
    dF_row = jnp.pad(dF[:, :2, :].reshape(N_FOX_HEADS, SEQ), ((0, LANES - N_FOX_HEADS), (0, 0)))
    df, db_fgate = _fox_gate_bwd(dF_row, proj, b_pad)
    small_state, small_token = pair_done(small_pair, df, ["small"], "small")

    delta_b = _dil_delta(dyb_h, yb_h)
    dil_grads = [_dil_bwd(qb_r, kb_r, proj, dyb_h, lse_b, delta_b, grp) for grp in range(N_GROUPS)]
    dqb, dkb = _rope_bwd([t[0] for t in dil_grads], [t[1] for t in dil_grads], tables)

    dproj = _shard_pad_cols({"qa": dqa, "ka": dka, "va": dva, "f": df, "qb": dqb, "kb": dkb,
                             "vb": [t[2] for t in dil_grads], "ga": dga, "gb": dgb})
    g_in = _matmul(h1, dproj, ta=True, by_shard=True, out_dtype=bf16, name="mm_g_in", tm=D, tn=W_IN_PAD,
                   after=small_token)
    mix_tags = ["in"]
    mix_pair, mix_pair_token = _exchange_start("pair", [g_in], "pair_start_mixer")
    mix_state, mix_token = pair_done(mix_pair, mix_pair_token, mix_tags, "mixer")

    grad_x, dsh_m, dsc_m, dg_mix = _first_bwd(dproj, w_in_s, mix_token, x2d, dx1, modv, g_mix)

    w = {"w_ada": w_ada, "b_ada": b_ada, "g_mix": g_mix, "w_in": w_in, "b_fgate": b_fgate, "w_br_a": w_br_a,
         "w_br_b": w_br_b, "w_out": w_out, "g_ffn": g_ffn, "w_ffn_gate": w_ffn_gate, "w_ffn_up": w_ffn_up,
         "w_ffn_down": w_ffn_down, "g_final": g_final}
    m = {"w_ada": m_w_ada, "b_ada": m_b_ada, "g_mix": m_g_mix, "w_in": m_w_in, "b_fgate": m_b_fgate,
         "w_br_a": m_w_br_a, "w_br_b": m_w_br_b, "w_out": m_w_out, "g_ffn": m_g_ffn, "w_ffn_gate": m_w_ffn_gate,
         "w_ffn_up": m_w_ffn_up, "w_ffn_down": m_w_ffn_down, "g_final": m_g_final}
    v = {"w_ada": v_w_ada, "b_ada": v_b_ada, "g_mix": v_g_mix, "w_in": v_w_in, "b_fgate": v_b_fgate,
         "w_br_a": v_w_br_a, "w_br_b": v_w_br_b, "w_out": v_w_out, "g_ffn": v_g_ffn, "w_ffn_gate": v_w_ffn_gate,
         "w_ffn_up": v_w_ffn_up, "w_ffn_down": v_w_ffn_down, "g_final": v_g_final}
    names = list(w)
    g, delta, new_m, new_v = {}, {}, {}, {}

    transposed = ("w_ffn_gate", "w_ffn_up")
    by_column = lambda t: jnp.transpose(t, (2, 0, 1))
    by_row = lambda t: jnp.transpose(t, (1, 2, 0))

    def update(n):
        shape = w[n].shape
        if n == "w_in":
            g3 = g[n]
            dl, mn, vn = _adamw_by_planes(by_column(w[n]), g3, by_column(m[n]), by_column(v[n]), "adamw_" + n)
            g[n], delta[n], new_m[n], new_v[n] = by_row(g3), by_row(dl), by_row(mn), by_row(vn)
            return
        if n in transposed:
            g_t = g[n]
            dl, mn, vn = _adamw(w[n][0].T, g_t, m[n][0].T, v[n][0].T, "adamw_" + n)
            g[n], delta[n], new_m[n], new_v[n] = g_t.T[None], dl.T[None], mn.T[None], vn.T[None]
            return
        two_d = (lambda t: t.reshape(shape[-2:])) if len(shape) == 3 else (lambda t: t)
        dl, mn, vn = _adamw(two_d(w[n]), two_d(g[n]), two_d(m[n]), two_d(v[n]), "adamw_" + n)
        delta[n], new_m[n], new_v[n] = dl.reshape(shape), mn.reshape(shape), vn.reshape(shape)

    def update_all(grads):
        g.update(grads)
        for n in grads:
            update(n)
        return sum(delta[n][(0,) * (delta[n].ndim - 1)][:N_FOX_HEADS] for n in grads)

    s_gu_t, s_d = from_chips(ffn_state, grad_x, ffn_tags, "ffn", [True, False])
    s_small, = from_chips(small_state, grad_x, ["small"], "small")
    sharded_done = update_all({
        "w_ffn_gate": s_gu_t[:W_FF_SH], "w_ffn_up": s_gu_t[FF_PAD:FF_PAD + W_FF_SH], "w_ffn_down": s_d[None, :W_FF_SH],
        "w_br_a": s_small[:rows_a].reshape(1, FOX_W, W_BR_SH),
        "w_br_b": s_small[rows_a:rows_a + rows_b].reshape(1, DIL_OUT_W, W_BR_SH), "w_out": s_small[None, rows_a + rows_b:],
    })

    pad_lane = lambda t: jnp.pad(t, ((0, 0), (0, D - t.shape[1])))
    small = jnp.concatenate([dsh_m, dsc_m, dga_m, dsh_f, dsc_f, dga_f, dg_mix, dg_ffn, dg_final,
                             pad_lane(db_fgate), loss_lanes, jnp.zeros((SMALL_ROWS - 11, D), f32)], axis=0)
    small_all = _all_gather(small, "gather_small", after=sharded_done)
    small_sum, loss_row = _small_reduce(small_all, mix_token)
    loss = loss_row[0, 0]
    dmod_all = small_all[:, :6, :].reshape(N_DEV, 6 * D)
    g_w_ada = _ada_bwd(c_all, lax.dynamic_slice(dmod_all, (0, dev * ada_cols), (N_DEV, ada_cols)))
    done = update_all({
        "w_ada": g_w_ada[None], "b_ada": small_sum[0:6].reshape(1, 6 * D), "g_mix": small_sum[6:7],
        "b_fgate": small_sum[9:10, :N_FOX_HEADS], "g_ffn": small_sum[7:8], "g_final": small_sum[8],
    })
    mix_sums, mix_got = _exchange_wait("chip", mix_state, done, "chip_wait_mixer")
    g["w_in"] = _chip_add(mix_sums[0], mix_got[0], chip, "chip_add_in", planes=W_IN_SH)
    update("w_in")

    return (loss, grad_x[None], *[g[n] for n in names], *[delta[n] for n in names],
            *[new_m[n] for n in names], *[new_v[n] for n in names])
```

```python
import jax
import jax.numpy as jnp
import numpy as np
from jax import lax
from jax.experimental import pallas as pl
from jax.experimental.pallas import tpu as pltpu

f32 = jnp.float32
bf16 = jnp.bfloat16
SDS = jax.ShapeDtypeStruct
MESH = pl.DeviceIdType.MESH

N_DEV = 8
D = 1024
SEQ = 2048
HEAD_DIM = 64
N_FOX_HEADS = 8
FOX_W = 512
DIL_W = 768
DIL_OUT_W = 256
ROT_DIM = 16
ROPE_THETA = 500000.0
D_FF = 2816
IN_COLS = 5896
EPS = 1e-6
NEG = -1e30
ATT_SCALE = HEAD_DIM ** -0.5

ADAM_LR = 0.001
ADAM_B1 = 0.9
ADAM_B2 = 0.999
ADAM_EPS = 1e-08
ADAM_WD = 0.01
ADAM_STEP = 10

C_GA, C_GB, C_QB, C_KB, C_VB, C_QA, C_KA, C_VA, C_F = 0, 1024, 2304, 3072, 3840, 4608, 5120, 5632, 6144
PROJ_W = 6272
LANES = 128
VMEM_LIMIT = 52 * 1024 * 1024

W_IN_SH, W_IN_PAD = IN_COLS // N_DEV, 768
W_BR_SH = D // N_DEV
W_FF_SH, FF_PAD = D_FF // N_DEV, 384
FF_HID = N_DEV * FF_PAD
SMALL_ROWS = 16


def _params(sem=None):
    if sem is None:
        return pltpu.CompilerParams(vmem_limit_bytes=VMEM_LIMIT)
    return pltpu.CompilerParams(dimension_semantics=sem, vmem_limit_bytes=VMEM_LIMIT)


def _rowwise(fn, name, tiled, vecs, outs, reds=(), tile=256):
    nt, nv, no = len(tiled), len(vecs), len(outs)
    rows = tiled[0][0].shape[0]
    assert rows % tile == 0

    def body(*refs):
        tin = [r[...] for r in refs[:nt]]
        vin = [r[...] for r in refs[nt:nt + nv]]
        orefs = refs[nt + nv:nt + nv + no]
        rrefs = refs[nt + nv + no:]
        touts, routs = fn(tin, vin)
        for r, t in zip(orefs, touts, strict=True):
            r[...] = t.astype(r.dtype)
        if rrefs:
            @pl.when(pl.program_id(0) == 0)
            def _():
                for r in rrefs:
                    r[...] = jnp.zeros_like(r)
            for r, t in zip(rrefs, routs, strict=True):
                r[...] += t

    def col_map(cb):
        return lambda i: (i, cb)

    def whole_map(nd):
        return lambda i: (0,) * nd

    in_specs = [pl.BlockSpec((tile, w), col_map(cb)) for (_, w, cb) in tiled]
    in_specs += [pl.BlockSpec(v.shape, whole_map(v.ndim)) for v in vecs]
    out_specs = [pl.BlockSpec((tile, w), lambda i: (i, 0)) for (w, _) in outs]
    out_specs += [pl.BlockSpec((1, w), lambda i: (0, 0)) for w in reds]
    out_shape = [SDS((rows, w), dt) for (w, dt) in outs] + [SDS((1, w), f32) for w in reds]
    res = pl.pallas_call(
        body, grid=(rows // tile,), in_specs=in_specs, out_specs=out_specs, out_shape=out_shape, name=name,
        compiler_params=_params(("arbitrary",)),
    )(*[t[0] for t in tiled], *vecs)
    return res


def _matmul(a, b, *, ta=False, out_dtype=f32, name, tm, tn, by_shard=False, after=None):
    (m, k), n = ((a.shape[1], a.shape[0]) if ta else a.shape), b.shape[1]
    assert b.shape[0] == k and m % tm == 0 and n % tn == 0 and (ta or not by_shard)
    dims = (((0 if ta else 1,), (0,)), ((), ()))

    def body(a_ref, b_ref, *rest):
        p = lax.dot_general(a_ref[...].astype(bf16), b_ref[...].astype(bf16), dims, preferred_element_type=f32)
        o_ref = rest[-1]
        if by_shard:
            o_ref[0] = p.astype(o_ref.dtype)
        else:
            o_ref[...] = p.astype(o_ref.dtype)

    a_spec = pl.BlockSpec((k, tm), lambda i, j: (0, i)) if ta else pl.BlockSpec((tm, k), lambda i, j: (i, 0))
    if by_shard:
        assert tn == n // N_DEV
        out_spec, out_shape = pl.BlockSpec((1, tm, tn), lambda i, j: (j, i, 0)), SDS((N_DEV, m, tn), out_dtype)
    else:
        out_spec, out_shape = pl.BlockSpec((tm, tn), lambda i, j: (i, j)), SDS((m, n), out_dtype)
    extra_specs, extra = ([pl.BlockSpec(memory_space=pl.ANY)], [after]) if after is not None else ([], [])
    return pl.pallas_call(
        body, grid=(m // tm, n // tn), in_specs=[a_spec, pl.BlockSpec((k, tn), lambda i, j: (0, j))] + extra_specs,
        out_specs=out_spec, out_shape=out_shape, name=name, compiler_params=_params(("parallel", "parallel")),
    )(a, b, *extra)


def _matmul_stack(a, b, *, ta=False, tb=False, out_dtype=f32, name):
    def lanes(ref):
        return jnp.concatenate([ref[j] for j in range(N_DEV)], axis=1).astype(bf16)

    if ta:
        w = b.shape[1] // N_DEV

        def body(a_ref, b_ref, o_ref):
            p = _tn(a_ref[...].astype(bf16), b_ref[...].astype(bf16))
            for j in range(N_DEV):
                o_ref[j] = p[:, j * w:(j + 1) * w].astype(o_ref.dtype)

        return pl.pallas_call(body, out_shape=SDS((N_DEV, a.shape[1], w), out_dtype), name=name,
                              compiler_params=_params())(a, b)

    m, half = a.shape[0], a.shape[0] // 2
    n = b.shape[1] if tb else N_DEV * b.shape[2]

    def body(a_ref, b_ref, o_ref):
        av = a_ref[...].astype(bf16)
        o_ref[...] = (_nt(av, lanes(b_ref)) if tb else jnp.dot(av, lanes(b_ref), preferred_element_type=f32)
                      ).astype(o_ref.dtype)

    return pl.pallas_call(
        body, grid=(2,), in_specs=[pl.BlockSpec((half, a.shape[1]), lambda i: (i, 0)),
                                   pl.BlockSpec(b.shape, lambda i: (0, 0, 0))],
        out_specs=pl.BlockSpec((half, n), lambda i: (i, 0)), out_shape=SDS((m, n), out_dtype), name=name,
        compiler_params=_params(("parallel",)),
    )(a, b)


def _matmul_rows(form, a, b, after, fn, tiled, vecs, outs, reds, *, name, tm=512):
    norm = lambda ts: [t if isinstance(t, tuple) else (t, t.shape[1], 0) for t in ts]
    make, sources = a if isinstance(a, tuple) else (None, [a])
    sources, tiled = norm(sources), norm(tiled)
    m, k = sources[0][0].shape[0], (b.shape[0] if form == "nn" else b.shape[-1] * (N_DEV if form == "nt_stack" else 1))
    assert m % tm == 0
    ns, nt, nv, no = len(sources), len(tiled), len(vecs), len(outs)

    def body(*refs):
        src_refs, b_ref, refs = refs[:ns], refs[ns], refs[ns + 2:]
        if make is None:
            lhs = lambda lo, hi: src_refs[0][:, lo:hi]
        else:
            made = make([r[...] for r in src_refs]).astype(bf16)
            lhs = lambda lo, hi: made[:, lo:hi]
        if form == "nt_stack":
            w = b.shape[2]
            acc = _nt(lhs(0, w), b_ref[0])
            for j in range(1, N_DEV):
                acc = acc + _nt(lhs(j * w, (j + 1) * w), b_ref[j])
        elif form == "nt":
            acc = _nt(lhs(0, k), b_ref[...])
        else:
            acc = jnp.dot(lhs(0, k), b_ref[...], preferred_element_type=f32)
        if make is not None:
            refs[nt + nv][...] = made
            refs = refs[:nt + nv] + refs[nt + nv + 1:]
        orefs, rrefs = refs[nt + nv:nt + nv + no], refs[nt + nv + no:]
        touts, routs = fn([acc] + [r[...] for r in refs[:nt]], [r[...] for r in refs[nt:nt + nv]])
        for r, t in zip(orefs, touts, strict=True):
            r[...] = t.astype(r.dtype)

        @pl.when(pl.program_id(0) == 0)
        def _():
            for r in rrefs:
                r[...] = jnp.zeros_like(r)
        for r, t in zip(rrefs, routs, strict=True):
            r[...] += t

    def whole_map(nd):
        return lambda i: (0,) * nd

    def rows(width, cb=0):
        return pl.BlockSpec((tm, width), lambda i: (i, cb))

    made_out = [(k, bf16)] if make is not None else []
    return pl.pallas_call(
        body, grid=(m // tm,),
        in_specs=[rows(width, cb) for _, width, cb in sources]
        + [pl.BlockSpec(b.shape, whole_map(b.ndim), pipeline_mode=pl.Buffered(1)), pl.BlockSpec(memory_space=pl.ANY)]
        + [rows(width, cb) for _, width, cb in tiled] + [pl.BlockSpec(v.shape, whole_map(v.ndim)) for v in vecs],
        out_specs=[rows(width) for width, _ in made_out + list(outs)]
        + [pl.BlockSpec((1, width), lambda i: (0, 0)) for width in reds],
        out_shape=[SDS((m, width), dt) for width, dt in made_out + list(outs)]
        + [SDS((1, width), f32) for width in reds], name=name,
        compiler_params=_params(("arbitrary",)),
    )(*[t[0] for t in sources], b, after, *[t[0] for t in tiled], *vecs)


def _rms(x):
    r = lax.rsqrt(jnp.mean(x * x, axis=-1, keepdims=True) + EPS)
    return r, x * r


def _rms_bwd(r, xn, dxn):
    return r * (dxn - xn * jnp.mean(dxn * xn, axis=-1, keepdims=True))


def _colsum(t):
    return jnp.sum(t, axis=0, keepdims=True)


def _sigmoid(x):
    return 0.5 * jnp.tanh(0.5 * x) + 0.5


def _modulated_norm(x, g, shift, scale):
    _, xn = _rms(x)
    return (xn * g) * (1.0 + scale) + shift


def _pre1(x, modv, g_mix):
    def fn(t, v):
        (xt,), (mv, g) = t, v
        return [_modulated_norm(xt, g, mv[0:1], mv[1:2])], []
    return _rowwise(fn, "pre1", [(x, D, 0)], [modv, g_mix], [(D, bf16)])[0]


def _post1(ya, yb, proj, w_o, x, modv, g_ffn):
    def merge(t):
        ya_t, yb_t, ga, gb = t
        return _sigmoid(ga) * ya_t + _sigmoid(gb) * yb_t

    def fn(t, v):
        (mt, xt), (mv, g) = t, v
        x1 = xt + mv[2:3] * mt
        return [mt, x1, _modulated_norm(x1, g, mv[3:4], mv[4:5])], []
    return _matmul_rows("nn", (merge, [ya, yb, (proj, D, C_GA // D), (proj, D, C_GB // D)]), w_o, x, fn, [x],
                        [modv, g_ffn], [(D, f32), (D, f32), (D, bf16)], [], name="post1")


def _ffn_in(h, w_stack):
    def body(h_ref, w_ref, act_ref, au_ref):
        p = jnp.dot(h_ref[...], w_ref[0], preferred_element_type=f32)
        a, u = p[:, :FF_PAD], p[:, FF_PAD:]
        act_ref[...] = (a * _sigmoid(a) * u).astype(act_ref.dtype)
        au_ref[...] = p.astype(au_ref.dtype)

    return pl.pallas_call(
        body, grid=(N_DEV,),
        in_specs=[pl.BlockSpec((SEQ, D), lambda j: (0, 0)), pl.BlockSpec((1, D, 2 * FF_PAD), lambda j: (j, 0, 0))],
        out_specs=[pl.BlockSpec((SEQ, FF_PAD), lambda j: (0, j)), pl.BlockSpec((SEQ, 2 * FF_PAD), lambda j: (0, j))],
        out_shape=(SDS((SEQ, FF_HID), bf16), SDS((SEQ, 2 * FF_HID), bf16)), name="ffn_in",
        compiler_params=_params(("parallel",)),
    )(h, w_stack)


def _ffn_bwd_in(dff, w_down_stack, au):
    def body(d_ref, w_ref, au_ref, o_ref):
        dact = _nt(d_ref[...], w_ref[0])
        p = au_ref[...].astype(f32)
        a, u = p[:, :FF_PAD], p[:, FF_PAD:]
        sg = _sigmoid(a)
        o_ref[...] = jnp.concatenate([dact * u * (sg * (1.0 + a * (1.0 - sg))), dact * (a * sg)],
                                     axis=1).astype(o_ref.dtype)

    return pl.pallas_call(
        body, grid=(N_DEV,),
        in_specs=[pl.BlockSpec((SEQ, D), lambda j: (0, 0)), pl.BlockSpec((1, FF_PAD, D), lambda j: (j, 0, 0)),
                  pl.BlockSpec((SEQ, 2 * FF_PAD), lambda j: (0, j))],
        out_specs=pl.BlockSpec((SEQ, 2 * FF_PAD), lambda j: (0, j)),
        out_shape=SDS((SEQ, 2 * FF_HID), bf16), name="ffn_bwd_in", compiler_params=_params(("parallel",)),
    )(dff, w_down_stack, au)


def _final(act, w_down, x1, target, modv, g_final):
    def fn(t, v):
        (fft, x1t, tgt), (mv, g) = t, v
        x2 = x1t + mv[5:6] * fft
        r, xn = _rms(x2)
        err = xn * g - tgt
        dy = err * (1.0 / D)
        dx2 = _rms_bwd(r, xn, dy * g)
        return [dx2, dx2 * mv[5:6]], [_colsum(dy * xn), _colsum(dx2 * fft), _colsum(err * err) * (0.5 / D)]
    return _matmul_rows("nn", act, w_down, x1, fn, [x1, target], [modv, g_final], [(D, f32), (D, bf16)], [D, D, D],
                        name="final")


def _mid_bwd(dau, w_stack, after, x1, dx2, mix, modv, g_ffn):
    def fn(t, v):
        (dh, x1t, dx2t, mt), (mv, g) = t, v
        r, xn = _rms(x1t)
        dn = dh * (1.0 + mv[4:5])
        dx1 = dx2t + _rms_bwd(r, xn, dn * g)
        return [dx1, dx1 * mv[2:3]], [_colsum(dh), _colsum(dh * (xn * g)), _colsum(dn * xn), _colsum(dx1 * mt)]
    return _matmul_rows("nt_stack", dau, w_stack, after, fn, [x1, dx2, mix], [modv, g_ffn], [(D, f32), (D, bf16)],
                        [D, D, D, D], name="mid_bwd")


def _first_bwd(dproj, w_stack, after, x, dx1, modv, g_mix):
    def fn(t, v):
        (dh, xt, dx1t), (mv, g) = t, v
        r, xn = _rms(xt)
        dn = dh * (1.0 + mv[1:2])
        return [dx1t + _rms_bwd(r, xn, dn * g)], [_colsum(dh), _colsum(dh * (xn * g)), _colsum(dn * xn)]
    return _matmul_rows("nt_stack", dproj, w_stack, after, fn, [x, dx1], [modv, g_mix], [(D, f32)], [D, D, D],
                        name="first_bwd")


def _merge_bwd(dmix, w_o, after, ya, yb, proj):
    def fn(t, v):
        dm, ya_t, yb_t, ga, gb = t
        sa, sb = _sigmoid(ga), _sigmoid(gb)
        return [dm * sa, dm * sb, dm * ya_t * (sa * (1.0 - sa)), dm * yb_t * (sb * (1.0 - sb))], []
    return _matmul_rows("nt", dmix, w_o, after, fn, [ya, yb, (proj, D, C_GA // D), (proj, D, C_GB // D)], [],
                        [(D, bf16), (D, bf16), (D, bf16), (D, bf16)], [], name="merge_bwd")


def _rope_tables():
    half = ROT_DIM // 2
    pos = np.arange(SEQ, dtype=np.float32)
    inv_freq = np.float32(ROPE_THETA) ** (-np.arange(0, ROT_DIM, 2, dtype=np.float32) / np.float32(ROT_DIM))
    ang = pos[:, None] * inv_freq[None, :].astype(np.float32)
    cos, sin = np.cos(ang).astype(np.float32), np.sin(ang).astype(np.float32)
    pad = np.zeros((SEQ, HEAD_DIM - ROT_DIM), np.float32)
    zero = np.zeros((SEQ, half), np.float32)
    c_head = np.concatenate([cos, cos, pad + 1.0], axis=1)
    lo_head = np.concatenate([-sin, zero, pad], axis=1)
    hi_head = np.concatenate([zero, sin, pad], axis=1)
    return tuple(jnp.asarray(np.concatenate([t, t], axis=1)) for t in (c_head, lo_head, hi_head))


def _over_heads(tables):
    return [jnp.tile(t, (1, DIL_W // LANES)) for t in tables]


def _rope_fwd(proj, tables):
    half = ROT_DIM // 2

    def fn(t, v):
        q, k = t[:2]
        c, lo, hi = _over_heads(t[2:])
        rot = lambda z: z * c + pltpu.roll(z, DIL_W - half, 1) * lo + pltpu.roll(z, half, 1) * hi
        return [rot(q) * ATT_SCALE, rot(k)], []
    return _rowwise(fn, "rope_fwd", [(proj, DIL_W, C_QB // DIL_W), (proj, DIL_W, C_KB // DIL_W)]
                    + [(tb, LANES, 0) for tb in tables], [], [(DIL_W, f32)] * 2)


def _rope_bwd(dqs, dks, tables):
    half = ROT_DIM // 2

    def fn(t, v):
        dq_t, dk_t = jnp.concatenate(t[:N_GROUPS], axis=1), jnp.concatenate(t[N_GROUPS:2 * N_GROUPS], axis=1)
        c, lo, hi = _over_heads(t[2 * N_GROUPS:])
        rot_t = lambda z: z * c + pltpu.roll(z * lo, half, 1) + pltpu.roll(z * hi, DIL_W - half, 1)
        return [rot_t(dq_t), rot_t(dk_t)], []
    return _rowwise(fn, "rope_bwd", [(a, DIL_OUT_W, 0) for a in (*dqs, *dks)] + [(tb, LANES, 0) for tb in tables],
                    [], [(DIL_W, bf16), (DIL_W, bf16)])


def _head_bcast_sum(d):
    lane = lax.broadcasted_iota(jnp.int32, d.shape, 1)
    out = jnp.zeros_like(d)
    for h in range(d.shape[1] // HEAD_DIM):
        sel = (lane >= h * HEAD_DIM) & (lane < (h + 1) * HEAD_DIM)
        out = jnp.where(sel, jnp.sum(jnp.where(sel, d, 0.0), axis=1, keepdims=True), out)
    return out


def _dil_combine(outs, lses):
    def fn(t, v):
        o0, o1, o2, l0, l1, l2 = t
        m = jnp.maximum(jnp.maximum(l0, l1), l2)
        w0, w1, w2 = jnp.exp(l0 - m), jnp.exp(l1 - m), jnp.exp(l2 - m)
        tot = w0 + w1 + w2
        return [(w0 * o0 + w1 * o1 + w2 * o2) / tot, m + jnp.log(tot)], []
    w = DIL_OUT_W
    return _rowwise(fn, "dil_combine", [(t, w, 0) for t in (*outs, *lses)], [], [(w, f32), (w, f32)])


def _dil_delta(dyb_h, yb_h):
    def fn(t, v):
        return [_head_bcast_sum(t[0] * t[1])], []
    return _rowwise(fn, "dil_delta", [(dyb_h, DIL_OUT_W, 0), (yb_h, DIL_OUT_W, 0)], [], [(DIL_OUT_W, f32)])[0]


def _adamw_math(wt, gt, mt, vt):
    mn = ADAM_B1 * mt + (1.0 - ADAM_B1) * gt
    vn = ADAM_B2 * vt + (1.0 - ADAM_B2) * (gt * gt)
    m_hat = mn / (1.0 - ADAM_B1 ** ADAM_STEP)
    v_hat = vn / (1.0 - ADAM_B2 ** ADAM_STEP)
    return -ADAM_LR * (m_hat / (jnp.sqrt(v_hat) + ADAM_EPS) + ADAM_WD * wt), mn, vn


def _adamw(w, g, m, v, name):
    shape = w.shape
    if w.ndim == 1:
        w, g, m, v = (t.reshape(1, -1) for t in (w, g, m, v))
    rows, cols = w.shape
    tile = 256 if rows % 256 == 0 and rows > 512 else rows

    def fn(t, _):
        return list(_adamw_math(*t)), []
    delta, mn, vn = _rowwise(fn, name, [(w, cols, 0), (g, cols, 0), (m, cols, 0), (v, cols, 0)], [],
                             [(cols, f32)] * 3, tile=tile)
    return delta.reshape(shape), mn.reshape(shape), vn.reshape(shape)


def _adamw_by_planes(w, g, m, v, name, most=128):
    planes, _, width = w.shape
    tile = max(t for t in range(1, most + 1) if planes % t == 0)

    def body(w_ref, g_ref, m_ref, v_ref, d_ref, mn_ref, vn_ref):
        d_ref[...], mn_ref[...], vn_ref[...] = _adamw_math(w_ref[...], g_ref[...], m_ref[...], v_ref[...])

    spec = pl.BlockSpec((tile, 1, width), lambda i: (i, 0, 0))
    return pl.pallas_call(body, grid=(planes // tile,), in_specs=[spec] * 4, out_specs=[spec] * 3,
                          out_shape=[SDS(w.shape, f32)] * 3, name=name,
                          compiler_params=_params(("parallel",)))(w, g, m, v)


def _ada_fwd(c_all, w_shard, b_shard):
    def body(c_ref, w_ref, b_ref, o_ref):
        cv = c_ref[...]
        sc = (cv * _sigmoid(cv)).astype(bf16)
        o_ref[...] = jnp.dot(sc, w_ref[...].astype(bf16), preferred_element_type=f32) + b_ref[...]
    return pl.pallas_call(body, out_shape=SDS((N_DEV, w_shard.shape[1]), f32), name="ada_fwd",
                          compiler_params=_params())(c_all, w_shard, b_shard)


def _ada_bwd(c_all, dmod_cols):
    def body(c_ref, d_ref, o_ref):
        cv = c_ref[...]
        sc = cv * _sigmoid(cv)
        o_ref[...] = lax.dot_general(sc, d_ref[...], (((0,), (0,)), ((), ())), precision=lax.Precision.HIGHEST,
                                     preferred_element_type=f32)
    return pl.pallas_call(body, out_shape=SDS((D, dmod_cols.shape[1]), f32), name="ada_bwd",
                          compiler_params=_params())(c_all, dmod_cols)


def _small_reduce(gathered, after):
    def body(g_ref, after_ref, o_ref, loss_ref):
        acc = g_ref[0]
        for d in range(1, N_DEV):
            acc = acc + g_ref[d]
        o_ref[...] = acc
        loss_ref[...] = jnp.zeros((1, LANES), f32) + jnp.sum(acc[10:11, :])
    return pl.pallas_call(body, out_shape=(SDS((SMALL_ROWS, D), f32), SDS((1, LANES), f32)), name="small_reduce",
                          in_specs=[pl.BlockSpec(memory_space=pltpu.VMEM), pl.BlockSpec(memory_space=pl.ANY)],
                          compiler_params=_params())(gathered, after)


FOX_BLK = 512
CUM_BLK = 128


def _fold_lanes(t, op):
    out = t[:, :LANES]
    for j in range(1, t.shape[1] // LANES):
        out = op(out, t[:, j * LANES:(j + 1) * LANES])
    return out


def _fox_gate_fwd(proj, b_pad):
    nblk = SEQ // CUM_BLK

    def body(f_ref, b_ref, col_ref):
        r = lax.broadcasted_iota(jnp.int32, (CUM_BLK, CUM_BLK), 0)
        c = lax.broadcasted_iota(jnp.int32, (CUM_BLK, CUM_BLK), 1)
        tri = (r >= c).astype(f32)
        carry = jnp.zeros((1, LANES), f32)
        for blk in range(nblk):
            z = f_ref[blk * CUM_BLK:(blk + 1) * CUM_BLK, :] + b_ref[...]
            logf = jnp.minimum(z, 0.0) - jnp.log1p(jnp.exp(-jnp.abs(z)))
            cs = jnp.dot(tri, logf, precision=lax.Precision.HIGHEST, preferred_element_type=f32) + carry
            col_ref[blk * CUM_BLK:(blk + 1) * CUM_BLK, :] = cs
            carry = cs[CUM_BLK - 1:CUM_BLK, :]

    return pl.pallas_call(
        body, grid=(1,), in_specs=[pl.BlockSpec((SEQ, LANES), lambda i: (0, C_F // LANES)),
                                   pl.BlockSpec((1, LANES), lambda i: (0, 0))],
        out_specs=pl.BlockSpec((SEQ, LANES), lambda i: (0, 0)),
        out_shape=SDS((SEQ, LANES), f32), name="fox_gate_fwd",
        compiler_params=_params(("arbitrary",)),
    )(proj, b_pad)


def _fox_gate_bwd(dF_row, proj, b_pad):
    nblk = SEQ // CUM_BLK

    def body(d_ref, f_ref, b_ref, df_ref, db_ref, col_ref):
        r = lax.broadcasted_iota(jnp.int32, (CUM_BLK, CUM_BLK), 0)
        c = lax.broadcasted_iota(jnp.int32, (CUM_BLK, CUM_BLK), 1)
        tri = (r <= c).astype(f32)
        lane = lax.broadcasted_iota(jnp.int32, (CUM_BLK, LANES), 1)
        col_ref[...] = d_ref[...].T
        carry = jnp.zeros((1, LANES), f32)
        total = jnp.zeros((1, LANES), f32)
        for blk in reversed(range(nblk)):
            rows = slice(blk * CUM_BLK, (blk + 1) * CUM_BLK)
            cs = jnp.dot(tri, col_ref[rows, :], precision=lax.Precision.HIGHEST, preferred_element_type=f32) + carry
            carry = cs[0:1, :]
            z = f_ref[rows, :] + b_ref[...]
            df = jnp.where(lane < N_FOX_HEADS, cs * _sigmoid(-z), 0.0)
            df_ref[rows, :] = df.astype(df_ref.dtype)
            total = total + _colsum(df)
        db_ref[...] = total

    return pl.pallas_call(
        body, grid=(1,), in_specs=[pl.BlockSpec((LANES, SEQ), lambda i: (0, 0)),
                                   pl.BlockSpec((SEQ, LANES), lambda i: (0, C_F // LANES)),
                                   pl.BlockSpec((1, LANES), lambda i: (0, 0))],
        out_specs=[pl.BlockSpec((SEQ, LANES), lambda i: (0, 0)), pl.BlockSpec((1, LANES), lambda i: (0, 0))],
        out_shape=(SDS((SEQ, LANES), bf16), SDS((1, LANES), f32)), name="fox_gate_bwd",
        scratch_shapes=[pltpu.VMEM((SEQ, LANES), f32)],
        compiler_params=_params(("arbitrary",)),
    )(dF_row, proj, b_pad)


def _nt(a, b):
    return lax.dot_general(a, b, (((1,), (1,)), ((), ())), preferred_element_type=f32)


def _tn(a, b):
    return lax.dot_general(a, b, (((0,), (0,)), ((), ())), preferred_element_type=f32)


def _fox_prep(proj, f_col):
    def fn(t, v):
        q, k, vv, fc = t
        lane = lax.broadcasted_iota(jnp.int32, (q.shape[0], LANES), 1)
        qs, ks = [], []
        for h in range(N_FOX_HEADS):
            pair, pos = divmod(h, 2)
            own = (lane >= pos * HEAD_DIM) & (lane < (pos + 1) * HEAD_DIM)
            base = (1 - pos) * HEAD_DIM
            f = fc[:, h:h + 1]
            hi = f.astype(bf16).astype(f32)
            mid = (f - hi).astype(bf16).astype(f32)
            lo = (f - hi) - mid
            one = jnp.ones_like(f)
            qa = jnp.where(own, q[:, pair * LANES:(pair + 1) * LANES] * ATT_SCALE, 0.0)
            ka = k[:, pair * LANES:(pair + 1) * LANES]
            for idx, (qv, kv) in enumerate([(hi, one), (mid, one), (lo, one), (one, -hi), (one, -mid), (one, -lo)]):
                sel = lane == base + idx
                qa = jnp.where(sel, qv, qa)
                ka = jnp.where(sel, kv, ka)
            qs.append(qa)
            ks.append(ka)
        return [jnp.concatenate(qs, axis=1), jnp.concatenate(ks, axis=1), vv], []
    w = N_FOX_HEADS * LANES
    return _rowwise(fn, "fox_prep", [(proj, FOX_W, C_QA // FOX_W), (proj, FOX_W, C_KA // FOX_W),
                                     (proj, FOX_W, C_VA // FOX_W), (f_col, LANES, 0)], [],
                    [(w, bf16), (w, bf16), (FOX_W, bf16)])


def _fox_fwd(q_aug, k_aug, v):
    blk = FOX_BLK
    npair = FOX_W // LANES

    def body(q_ref, k_ref, v_ref, o_ref, max_ref, sum_ref, s_scr):
        i = pl.program_id(1)
        tri = lax.broadcasted_iota(jnp.int32, (blk, blk), 0) >= lax.broadcasted_iota(jnp.int32, (blk, blk), 1)
        qh = [q_ref[:, h * LANES:(h + 1) * LANES] for h in range(2)]

        def logits(c, masked):
            off = pl.multiple_of(c * blk, blk)
            tops = []
            for h in range(2):
                s = _nt(qh[h], k_ref[pl.ds(off, blk), h * LANES:(h + 1) * LANES])
                if masked:
                    s = jnp.where(tri, s, NEG)
                s_scr[h, :, pl.ds(off, blk)] = s
                tops.append(_fold_lanes(s, jnp.maximum))
            return tops

        def pass_a(c, m):
            return tuple(jnp.maximum(a, b) for a, b in zip(m, logits(c, False)))

        m = lax.fori_loop(0, i, pass_a, tuple(jnp.full((blk, LANES), NEG, f32) for _ in range(2)))
        mx = [jnp.max(jnp.maximum(a, b), axis=1, keepdims=True) for a, b in zip(m, logits(i, True))]

        def pass_b(c, carry):
            off = pl.multiple_of(c * blk, blk)
            vv = v_ref[pl.ds(off, blk), :]
            new = []
            for h in range(2):
                l, acc = carry[h]
                p = jnp.exp(s_scr[h, :, pl.ds(off, blk)] - mx[h]).astype(bf16)
                new.append((l + _fold_lanes(p.astype(f32), jnp.add), acc + jnp.dot(p, vv, preferred_element_type=f32)))
            return tuple(new)

        zero = jnp.zeros((blk, LANES), f32)
        (l_a, acc_a), (l_b, acc_b) = lax.fori_loop(0, i + 1, pass_b, ((zero, zero), (zero, zero)))
        l_a = jnp.sum(l_a, axis=1, keepdims=True)
        l_b = jnp.sum(l_b, axis=1, keepdims=True)
        first = lax.broadcasted_iota(jnp.int32, (blk, LANES), 1) < HEAD_DIM
        o_ref[...] = jnp.where(first, acc_a / l_a, acc_b / l_b)
        max_ref[0] = jnp.where(first, mx[0], mx[1])
        sum_ref[0] = jnp.where(first, l_a, l_b)

    return pl.pallas_call(
        body, grid=(npair, SEQ // blk),
        in_specs=[pl.BlockSpec((blk, 2 * LANES), lambda p, i: (i, p)),
                  pl.BlockSpec((SEQ, 2 * LANES), lambda p, i: (0, p)),
                  pl.BlockSpec((SEQ, LANES), lambda p, i: (0, p))],
        out_specs=[pl.BlockSpec((blk, LANES), lambda p, i: (i, p))]
        + [pl.BlockSpec((1, blk, LANES), lambda p, i: (p, i, 0))] * 2,
        out_shape=(SDS((SEQ, FOX_W), f32),) + (SDS((npair, SEQ, LANES), f32),) * 2, name="fox_fwd",
        scratch_shapes=[pltpu.VMEM((2, blk, SEQ), f32)],
        compiler_params=_params(("parallel", "arbitrary")),
    )(q_aug, k_aug, v)


def _fox_bwd(q_aug, k_aug, v, do, o, row_max, row_sum, after):
    blk = FOX_BLK
    npair = FOX_W // LANES
    nblk = SEQ // blk

    def body(q_ref, k_ref, v_ref, do_ref, o_ref, max_ref, sum_ref, after_ref, dq_ref, dk_ref, dv_ref, df_ref, dq_acc,
             delta_ref, inv_ref):
        inv_ref[...] = 1.0 / sum_ref[0]
        lane_s = lax.broadcasted_iota(jnp.int32, (SEQ, LANES), 1)
        prod = do_ref[...].astype(bf16).astype(f32) * o_ref[...]
        d_a = jnp.sum(jnp.where(lane_s < HEAD_DIM, prod, 0.0), axis=1, keepdims=True)
        d_b = jnp.sum(jnp.where(lane_s >= HEAD_DIM, prod, 0.0), axis=1, keepdims=True)
        delta_ref[...] = jnp.where(lane_s < HEAD_DIM, d_a, d_b)
        dq_acc[...] = jnp.zeros_like(dq_acc)
        df_ref[...] = jnp.zeros_like(df_ref)
        lane = lax.broadcasted_iota(jnp.int32, (blk, LANES), 1)
        own = [lane < HEAD_DIM, lane >= HEAD_DIM]
        tri = lax.broadcasted_iota(jnp.int32, (blk, blk), 0) >= lax.broadcasted_iota(jnp.int32, (blk, blk), 1)

        def q_slab(qoff, h):
            return q_ref[pl.ds(qoff, blk), h * LANES:(h + 1) * LANES]

        def probs(qoff, h, k_h, masked):
            s = _nt(q_slab(qoff, h), k_h)
            if masked:
                s = jnp.where(tri, s, NEG)
            col = slice(h * HEAD_DIM, h * HEAD_DIM + 1)
            weights = jnp.exp(s - max_ref[0, pl.ds(qoff, blk), col]).astype(bf16).astype(f32)
            return weights * inv_ref[pl.ds(qoff, blk), col]

        def k_slabs(koff):
            return [k_ref[pl.ds(koff, blk), h * LANES:(h + 1) * LANES] for h in range(2)]

        def kv_step(kj, _):
            koff = pl.multiple_of(kj * blk, blk)
            k_aug = k_slabs(koff)
            k_own = [jnp.where(own[h], k_aug[h], jnp.zeros_like(k_aug[h])) for h in range(2)]
            vv = v_ref[pl.ds(koff, blk), :]
            v_own = [jnp.where(own[h], vv, jnp.zeros_like(vv)) for h in range(2)]

            def q_tile(qi, carry, masked):
                qoff = pl.multiple_of(qi * blk, blk)
                dd = do_ref[pl.ds(qoff, blk), :].astype(bf16)
                new, dq_add = [], None
                for h in range(2):
                    dk_h, dv_h, dcol = carry[h]
                    p = probs(qoff, h, k_aug[h], masked)
                    dl = p * (_nt(dd, v_own[h]) - delta_ref[pl.ds(qoff, blk), h * HEAD_DIM:h * HEAD_DIM + 1])
                    dlb = dl.astype(bf16)
                    part = jnp.dot(dlb, k_own[h], preferred_element_type=f32)
                    dq_add = part if dq_add is None else dq_add + part
                    new.append((dk_h + _tn(dlb, q_slab(qoff, h)), dv_h + _tn(p.astype(bf16), dd),
                                dcol + _colsum(dl)))
                dq_acc[pl.ds(qoff, blk), :] += dq_add * ATT_SCALE
                return tuple(new)

            zero = (jnp.zeros((blk, LANES), f32), jnp.zeros((blk, LANES), f32), jnp.zeros((1, blk), f32))
            carry = q_tile(kj, (zero, zero), True)
            (dk_a, dv_a, dcol_a), (dk_b, dv_b, dcol_b) = lax.fori_loop(
                kj + 1, nblk, lambda qi, cr: q_tile(qi, cr, False), carry)
            dk_ref[pl.ds(koff, blk), :] = jnp.where(own[0], dk_a, dk_b).astype(dk_ref.dtype)
            dv_ref[pl.ds(koff, blk), :] = jnp.where(own[0], dv_a, dv_b).astype(dv_ref.dtype)
            df_ref[0, 0:1, pl.ds(koff, blk)] = -dcol_a
            df_ref[0, 1:2, pl.ds(koff, blk)] = -dcol_b
            return 0

        lax.fori_loop(0, nblk, kv_step, 0)
        dq_ref[...] = dq_acc[...].astype(dq_ref.dtype)

    pair_aug = pl.BlockSpec((SEQ, 2 * LANES), lambda p: (0, p))
    slab = pl.BlockSpec((SEQ, LANES), lambda p: (0, p))
    per_pair = pl.BlockSpec((1, SEQ, LANES), lambda p: (p, 0, 0))
    rows = pl.BlockSpec((1, 8, SEQ), lambda p: (p, 0, 0))
    return pl.pallas_call(
        body, grid=(npair,),
        in_specs=[pair_aug, pair_aug, slab, slab, slab, per_pair, per_pair, pl.BlockSpec(memory_space=pl.ANY)],
        out_specs=[slab, slab, slab, rows],
        out_shape=(SDS((SEQ, FOX_W), bf16),) * 3 + (SDS((npair, 8, SEQ), f32),), name="fox_bwd",
        scratch_shapes=[pltpu.VMEM((SEQ, LANES), f32)] * 3,
        compiler_params=_params(("parallel",)),
    )(q_aug, k_aug, v, do, o, row_max, row_sum, after)


DIL_BLK = 128
DILATIONS = (1, 4, 16)
N_GROUPS = len(DILATIONS)
DIL_PAIRS = DIL_OUT_W // LANES


def _dil_blocks(d):
    r1 = lax.broadcasted_iota(jnp.int32, (2 * DIL_BLK, DIL_BLK), 0) & (DIL_BLK - 1)
    c1 = lax.broadcasted_iota(jnp.int32, (2 * DIL_BLK, DIL_BLK), 1)
    r2 = lax.broadcasted_iota(jnp.int32, (2 * DIL_BLK, 2 * DIL_BLK), 0) & (DIL_BLK - 1)
    c2 = lax.broadcasted_iota(jnp.int32, (2 * DIL_BLK, 2 * DIL_BLK), 1)
    band = ((c2 < DIL_BLK) & (c2 >= r2)) | ((c2 >= DIL_BLK) & (c2 - DIL_BLK <= r2))
    out = []
    for r in range(d):
        for b in range(SEQ // d // DIL_BLK):
            rows = pl.ds(r + d * DIL_BLK * b, DIL_BLK, stride=d)
            if b == 0:
                out.append((rows, rows, r1 >= c1))
            else:
                out.append((rows, pl.ds(r + d * DIL_BLK * (b - 1), 2 * DIL_BLK, stride=d), band))
    return out


def _dil_v_spec(g):
    return pl.BlockSpec((SEQ, LANES), lambda p: (0, C_VB // LANES + DIL_PAIRS * g + p))


def _stack_heads(t, first):
    zero = jnp.zeros_like(t)
    return jnp.concatenate([jnp.where(first, t, zero), jnp.where(first, zero, t)], axis=0)


def _dil_fwd(q, k, v, g):
    def body(q_ref, k_ref, v_ref, o_ref, lse_ref):
        first = lax.broadcasted_iota(jnp.int32, (DIL_BLK, LANES), 1) < HEAD_DIM
        for rows, krows, mask in _dil_blocks(DILATIONS[g]):
            qv, kk, vv = q_ref[rows, :].astype(bf16), k_ref[krows, :].astype(bf16), v_ref[krows, :].astype(bf16)
            s = jnp.where(mask, _nt(_stack_heads(qv, first), kk), NEG)
            m = jnp.max(s, axis=1, keepdims=True)
            p = jnp.exp(s - m)
            l = jnp.sum(p, axis=1, keepdims=True)
            out = jnp.dot(p.astype(bf16), vv, preferred_element_type=f32) / l
            lse = m + jnp.log(l)
            o_ref[rows, :] = jnp.where(first, out[:DIL_BLK], out[DIL_BLK:])
            lse_ref[rows, :] = jnp.where(first, lse[:DIL_BLK], lse[DIL_BLK:])

    grouped = pl.BlockSpec((SEQ, LANES), lambda p: (0, DIL_PAIRS * g + p))
    own = pl.BlockSpec((SEQ, LANES), lambda p: (0, p))
    shape = SDS((SEQ, DIL_OUT_W), f32)
    return pl.pallas_call(
        body, grid=(DIL_PAIRS,), in_specs=[grouped, grouped, _dil_v_spec(g)], out_specs=[own] * 2,
        out_shape=(shape, shape),
        name=f"dil_fwd_{DILATIONS[g]}", compiler_params=_params(("parallel",)),
    )(q, k, v)


def _dil_bwd(q, k, v, do, lse, delta, g):
    def body(q_ref, k_ref, v_ref, do_ref, lse_ref, dl_ref, dq_ref, dk_ref, dv_ref):
        first = lax.broadcasted_iota(jnp.int32, (DIL_BLK, LANES), 1) < HEAD_DIM
        dk_ref[...] = jnp.zeros_like(dk_ref)
        dv_ref[...] = jnp.zeros_like(dv_ref)
        for rows, krows, mask in _dil_blocks(DILATIONS[g]):
            qv, kk, vv = q_ref[rows, :].astype(bf16), k_ref[krows, :].astype(bf16), v_ref[krows, :].astype(bf16)
            lsev, delv = lse_ref[rows, :], dl_ref[rows, :]
            q2 = _stack_heads(qv, first)
            do2 = _stack_heads(do_ref[rows, :].astype(bf16), first)
            per_head = lambda t: jnp.concatenate([t[:, 0:1], t[:, HEAD_DIM:HEAD_DIM + 1]], axis=0)
            p = jnp.exp(jnp.where(mask, _nt(q2, kk), NEG) - per_head(lsev))
            dl = (p * (_nt(do2, vv) - per_head(delv))).astype(bf16)
            dq = jnp.dot(dl, kk, preferred_element_type=f32)
            dq_ref[rows, :] = jnp.where(first, dq[:DIL_BLK], dq[DIL_BLK:]) * ATT_SCALE
            dk_ref[krows, :] += _tn(dl, q2)
            dv_ref[krows, :] += _tn(p.astype(bf16), do2)

    grouped = pl.BlockSpec((SEQ, LANES), lambda p: (0, DIL_PAIRS * g + p))
    own = pl.BlockSpec((SEQ, LANES), lambda p: (0, p))
    shape = SDS((SEQ, DIL_OUT_W), f32)
    return pl.pallas_call(
        body, grid=(DIL_PAIRS,), in_specs=[grouped, grouped, _dil_v_spec(g)] + [own] * 3, out_specs=[own] * 3,
        out_shape=(shape, shape, shape), name=f"dil_bwd_{DILATIONS[g]}", compiler_params=_params(("parallel",)),
    )(q, k, v, do, lse, delta)


def _position():
    return lax.axis_index("x"), lax.axis_index("y"), lax.axis_index("c")


def _all_gather(block, name, after=None):
    after = [] if after is None else [after]

    def body(x_ref, *refs):
        out_ref, send_sems, recv_sems, local_sem = refs[len(after):]
        x, y, c = _position()
        me, sibling = (x, y, c), (x, y, 1 - c)
        chips = [(1 - x, y), (x, 1 - y), (1 - x, 1 - y)]

        def slot(px, py, pc):
            return out_ref.at[4 * px + 2 * py + pc]

        def copy(k, blk, to, src=None):
            return pltpu.make_async_remote_copy(
                src_ref=slot(*blk) if src is None else src, dst_ref=slot(*blk),
                send_sem=send_sems.at[k], recv_sem=recv_sems.at[k], device_id=to, device_id_type=MESH)

        mine = pltpu.make_async_copy(x_ref, slot(*me), local_sem)
        mine.start()
        first = [copy(0, me, sibling, src=x_ref)]
        first += [copy(1 + j, me, (*chip, c), src=x_ref) for j, chip in enumerate(chips)]
        for cp in first:
            cp.start()
        passed = [copy(4 + j, (*chip, c), sibling) for j, chip in enumerate(chips)]
        for j, chip in enumerate(chips):
            copy(1 + j, (*chip, c), me).wait_recv()
            passed[j].start()
        copy(0, sibling, me).wait_recv()
        for j, chip in enumerate(chips):
            copy(4 + j, (*chip, 1 - c), me).wait_recv()
        for cp in first + passed:
            cp.wait_send()
        mine.wait()

    return pl.pallas_call(
        body, out_shape=SDS((N_DEV,) + block.shape, block.dtype),
        in_specs=[pl.BlockSpec(memory_space=pl.ANY)] * (1 + len(after)), out_specs=pl.BlockSpec(memory_space=pl.ANY),
        scratch_shapes=[pltpu.SemaphoreType.DMA((7,)), pltpu.SemaphoreType.DMA((7,)), pltpu.SemaphoreType.DMA],
        name=name,
    )(block, *after)


HBM_SPEC = pl.BlockSpec(memory_space=pltpu.HBM)
SEM_SPEC = pl.BlockSpec(memory_space=pltpu.SEMAPHORE)
SPLIT_COPY = pltpu.CompilerParams(has_side_effects=pltpu.SideEffectType.DATAFLOW_SIDE_EFFECTING)


def _in_hbm(t):
    return pltpu.with_memory_space_constraint(t, pltpu.HBM)


def _pair_copies(g_refs, land_refs, send_sems, recv_sems):
    x, y, c = _position()
    return [pltpu.make_async_remote_copy(
        src_ref=g.at[2 * k + (1 - c)], dst_ref=land.at[k], send_sem=send_sems.at[4 * a + k],
        recv_sem=recv_sems.at[4 * a + k], device_id=(x, y, 1 - c), device_id_type=MESH)
        for a, (g, land) in enumerate(zip(g_refs, land_refs, strict=True)) for k in range(4)]


def _chip_copies(t_refs, land_refs, send_sems, recv_sems):
    x, y, c = _position()
    chips = [(1 - x, y), (x, 1 - y), (1 - x, 1 - y)]
    return [pltpu.make_async_remote_copy(
        src_ref=t.at[2 * px + py], dst_ref=land.at[j], send_sem=send_sems.at[3 * a + j],
        recv_sem=recv_sems.at[3 * a + j], device_id=(px, py, c), device_id_type=MESH)
        for a, (t, land) in enumerate(zip(t_refs, land_refs, strict=True)) for j, (px, py) in enumerate(chips)]


_ROUNDS = {"pair": (_pair_copies, 4), "chip": (_chip_copies, 3)}


def _exchange_start(kind, ts, name):
    copies, slots = _ROUNDS[kind]
    n = len(ts)
    lands = [_in_hbm(lax.empty((slots,) + t.shape[1:], t.dtype)) for t in ts]

    def body(*refs):
        for cp in copies(refs[:n], refs[n:2 * n], refs[2 * n], refs[2 * n + 1]):
            cp.start()
        refs[-1][...] = jnp.zeros_like(refs[-1])

    sems = pltpu.SemaphoreType.DMA((slots * n,))
    res = pl.pallas_call(
        body, name=name, in_specs=[HBM_SPEC] * (2 * n),
        out_shape=(sems, sems, *[pltpu.HBM(t.shape, t.dtype) for t in (*ts, *lands)], SDS((8, LANES), f32)),
        out_specs=(SEM_SPEC, SEM_SPEC, *[HBM_SPEC] * (2 * n), pl.BlockSpec(memory_space=pltpu.VMEM)),
        input_output_aliases={i: 2 + i for i in range(2 * n)}, compiler_params=SPLIT_COPY,
    )(*[_in_hbm(t) for t in ts], *lands)
    return res[:-1], res[-1]


def _exchange_wait(kind, state, after, name):
    copies, _ = _ROUNDS[kind]
    send_sems, recv_sems, *arrays = state
    n = len(arrays) // 2

    def body(*refs):
        for cp in copies(refs[:n], refs[n:2 * n], refs[2 * n], refs[2 * n + 1]):
            cp.wait_send()
            cp.wait_recv()

    res = pl.pallas_call(
        body, name=name, in_specs=[HBM_SPEC] * (2 * n) + [SEM_SPEC, SEM_SPEC, pl.BlockSpec(memory_space=pl.ANY)],
        out_shape=[pltpu.HBM(t.shape, t.dtype) for t in arrays], out_specs=[HBM_SPEC] * (2 * n),
        input_output_aliases={i: i for i in range(2 * n)}, compiler_params=SPLIT_COPY,
    )(*arrays, send_sems, recv_sems, after)
    return res[:n], res[n:]


def _gather_copies(x_refs, out_refs, send_sems, recv_sems):
    x, y, c = _position()
    peers = [(x, y, 1 - c), (1 - x, y, c), (x, 1 - y, c), (1 - x, 1 - y, c)]
    sends, arrivals = [], []
    for a, (x_ref, out_ref) in enumerate(zip(x_refs, out_refs, strict=True)):
        for k, (px, py, pc) in enumerate(peers):
            sems = dict(send_sem=send_sems.at[4 * a + k], recv_sem=recv_sems.at[4 * a + k],
                        device_id=(px, py, pc), device_id_type=MESH)
            sends.append(pltpu.make_async_remote_copy(src_ref=x_ref, dst_ref=out_ref.at[4 * x + 2 * y + c], **sems))
            arrivals.append(pltpu.make_async_remote_copy(src_ref=x_ref, dst_ref=out_ref.at[4 * px + 2 * py + pc],
                                                         **sems))
    return sends, arrivals


def _gather_start(blocks, after, name):
    n = len(blocks)
    outs = [_in_hbm(lax.empty((N_DEV,) + b.shape, b.dtype)) for b in blocks]

    def body(*refs):
        sends, _ = _gather_copies(refs[:n], refs[n:2 * n], refs[2 * n + 1], refs[2 * n + 2])
        for cp in sends:
            cp.start()
        refs[-1][...] = jnp.zeros_like(refs[-1])

    sems = pltpu.SemaphoreType.DMA((4 * n,))
    res = pl.pallas_call(
        body, name=name, in_specs=[HBM_SPEC] * (2 * n) + [pl.BlockSpec(memory_space=pl.ANY)],
        out_shape=(sems, sems, *[pltpu.HBM(t.shape, t.dtype) for t in (*blocks, *outs)], SDS((8, LANES), f32)),
        out_specs=(SEM_SPEC, SEM_SPEC, *[HBM_SPEC] * (2 * n), pl.BlockSpec(memory_space=pltpu.VMEM)),
        input_output_aliases={i: 2 + i for i in range(2 * n)}, compiler_params=SPLIT_COPY,
    )(*[_in_hbm(b) for b in blocks], *outs, after)
    return res[:-1], res[-1]


def _gather_wait(state, after, name):
    send_sems, recv_sems, *arrays = state
    n = len(arrays) // 2

    def body(*refs):
        sends, arrivals = _gather_copies(refs[:n], refs[n:2 * n], refs[2 * n], refs[2 * n + 1])
        for cp in sends:
            cp.wait_send()
        for cp in arrivals:
            cp.wait_recv()

    res = pl.pallas_call(
        body, name=name, in_specs=[HBM_SPEC] * (2 * n) + [SEM_SPEC, SEM_SPEC, pl.BlockSpec(memory_space=pl.ANY)],
        out_shape=[pltpu.HBM(t.shape, t.dtype) for t in arrays], out_specs=[HBM_SPEC] * (2 * n),
        input_output_aliases={i: i for i in range(2 * n)}, compiler_params=SPLIT_COPY,
    )(*arrays, send_sems, recv_sems, after)
    return res[:n], res[n:]


def _gather_finish(partial, name):
    n = len(partial)

    def body(*refs):
        in_refs, out_refs = refs[:n], refs[n:2 * n]
        send_sems, recv_sems = refs[2 * n:]
        x, y, c = _position()
        chips = [(1 - x, y), (x, 1 - y), (1 - x, 1 - y)]
        copies = []
        for a in range(n):
            for j, (px, py) in enumerate(chips):
                cp = pltpu.make_async_remote_copy(
                    src_ref=in_refs[a].at[4 * px + 2 * py + c], dst_ref=out_refs[a].at[4 * px + 2 * py + c],
                    send_sem=send_sems.at[a, j], recv_sem=recv_sems.at[a, j], device_id=(x, y, 1 - c),
                    device_id_type=MESH)
                cp.start()
                copies.append(cp)
        for a in range(n):
            for j, (px, py) in enumerate(chips):
                pltpu.make_async_remote_copy(
                    src_ref=in_refs[a].at[4 * px + 2 * py + (1 - c)], dst_ref=out_refs[a].at[4 * px + 2 * py + (1 - c)],
                    send_sem=send_sems.at[a, j], recv_sem=recv_sems.at[a, j], device_id=(x, y, 1 - c),
                    device_id_type=MESH).wait_recv()
        for cp in copies:
            cp.wait_send()

    hbm = pl.BlockSpec(memory_space=pl.ANY)
    return pl.pallas_call(
        body, out_shape=[SDS(p.shape, p.dtype) for p in partial], in_specs=[hbm] * n, out_specs=[hbm] * n,
        input_output_aliases={a: a for a in range(n)},
        scratch_shapes=[pltpu.SemaphoreType.DMA((n, 3)), pltpu.SemaphoreType.DMA((n, 3))],
        name=name,
    )(*partial)


def _row_tile(rows):
    return 512 if rows % 512 == 0 and rows > 512 else rows


def _pair_add(g, r1, core, name):
    def body(c_ref, g_ref, r_ref, o_ref):
        o_ref[...] = (g_ref[...].astype(f32) + r_ref[...].astype(f32)).astype(o_ref.dtype)

    rows, cols = g.shape[1:]
    tile = _row_tile(rows)
    blk = (1, tile, cols)
    return pl.pallas_call(
        body, out_shape=SDS((4, rows, cols), g.dtype), name=name,
        grid_spec=pltpu.PrefetchScalarGridSpec(
            num_scalar_prefetch=1, grid=(4, rows // tile),
            in_specs=[pl.BlockSpec(blk, lambda k, i, c_ref: (2 * k + c_ref[0], i, 0)),
                      pl.BlockSpec(blk, lambda k, i, c_ref: (k, i, 0))],
            out_specs=pl.BlockSpec(blk, lambda k, i, c_ref: (k, i, 0))),
        compiler_params=_params(("parallel", "arbitrary")),
    )(core, g, r1)


def _chip_add(t, r2, chip, name, transposed=False, planes=None, pieces=None):
    def body(c_ref, t_ref, r_ref, *o_refs):
        s = ((t_ref[0].astype(f32) + r_ref[0].astype(f32)) + r_ref[1].astype(f32)) + r_ref[2].astype(f32)
        if planes:
            o_refs[0][...] = s.T[:planes][:, None, :]
        elif pieces:
            for o_ref, (start, size) in zip(o_refs, pieces, strict=True):
                o_ref[...] = s.T[start:start + size]
        else:
            o_refs[0][...] = s.T if transposed else s

    rows, cols = t.shape[1:]
    tile = _row_tile(rows)
    if planes:
        out_shape, out_spec = SDS((planes, 1, rows), f32), pl.BlockSpec((planes, 1, tile), lambda i, c_ref: (0, 0, i))
    elif pieces:
        out_shape = [SDS((size, rows), f32) for _, size in pieces]
        out_spec = [pl.BlockSpec((size, tile), lambda i, c_ref: (0, i)) for _, size in pieces]
    elif transposed:
        out_shape, out_spec = SDS((cols, rows), f32), pl.BlockSpec((cols, tile), lambda i, c_ref: (0, i))
    else:
        out_shape, out_spec = SDS((rows, cols), f32), pl.BlockSpec((tile, cols), lambda i, c_ref: (i, 0))
    return pl.pallas_call(
        body, out_shape=out_shape, name=name,
        grid_spec=pltpu.PrefetchScalarGridSpec(
            num_scalar_prefetch=1, grid=(rows // tile,),
            in_specs=[pl.BlockSpec((1, tile, cols), lambda i, c_ref: (c_ref[0], i, 0)),
                      pl.BlockSpec((3, tile, cols), lambda i, c_ref: (0, i, 0))],
            out_specs=out_spec),
        compiler_params=_params(("arbitrary",)),
    )(chip, t, r2)


def _pad_to(t, axis, size):
    pads = [(0, 0)] * t.ndim
    pads[axis] = (0, size - t.shape[axis])
    return jnp.pad(t, pads)


_REF_COLS = {"qa": (0, FOX_W), "ka": (FOX_W, FOX_W), "va": (2 * FOX_W, FOX_W), "f": (3 * FOX_W, N_FOX_HEADS)}
_REF_COLS.update({n: (3 * FOX_W + N_FOX_HEADS + i * DIL_W, DIL_W) for i, n in enumerate(("qb", "kb", "vb"))})
_REF_COLS.update({n: (3 * FOX_W + N_FOX_HEADS + 3 * DIL_W + i * D, D) for i, n in enumerate(("ga", "gb"))})
_REF_ORDER = ("qa", "ka", "va", "f", "qb", "kb", "vb", "ga", "gb")


def _place_cols(sources, src_of, out_cols, name, row_block=512):
    arrays = [s[0] if isinstance(s, tuple) else s for s in sources]
    widths = [a.shape[-1] for a in arrays]
    rows = arrays[0].shape[-2]
    plan = []
    for t in range(out_cols // LANES):
        segs, c, end = [], t * LANES, (t + 1) * LANES
        while c < end:
            s = src_of(c)
            if s is None:
                c += 1
                continue
            n = 1
            while c + n < end and src_of(c + n) == (s[0], s[1] + n):
                n += 1
            segs.append((s[0], s[1], c - t * LANES, n))
            c += n
        plan.append(segs)

    def body(*refs):
        o_ref = refs[-1]
        for t, segs in enumerate(plan):
            acc = None
            for si, c0, o0, n in segs:
                a0 = c0 // LANES * LANES
                wide = min(2 * LANES, widths[si] - a0)
                win = refs[si][0, :, a0:a0 + wide] if isinstance(sources[si], tuple) else refs[si][:, a0:a0 + wide]
                r = lax.broadcasted_iota(jnp.int32, (wide, LANES), 0)
                c = lax.broadcasted_iota(jnp.int32, (wide, LANES), 1)
                pick = ((r - (c0 - a0) == c - o0) & (c >= o0) & (c < o0 + n)).astype(bf16)
                part = jnp.dot(win.astype(bf16), pick, preferred_element_type=f32)
                acc = part if acc is None else acc + part
            tile = jnp.zeros((row_block, LANES), f32) if acc is None else acc
            o_ref[:, t * LANES:(t + 1) * LANES] = tile.astype(o_ref.dtype)

    def spec(s):
        if isinstance(s, tuple):
            j = s[1]
            return pl.BlockSpec((1, row_block, s[0].shape[-1]), lambda i: (j, i, 0))
        return pl.BlockSpec((row_block, s.shape[-1]), lambda i: (i, 0))

    return pl.pallas_call(
        body, grid=(rows // row_block,), in_specs=[spec(s) for s in sources],
        out_specs=pl.BlockSpec((row_block, out_cols), lambda i: (i, 0)), out_shape=SDS((rows, out_cols), bf16),
        name=name, compiler_params=_params(("parallel",)),
    )(*arrays)


def _ref_piece(r):
    for name in _REF_ORDER:
        lo, width = _REF_COLS[name]
        if lo <= r < lo + width:
            return name, r - lo
    raise ValueError(r)


def _shard_pad_cols(pieces):
    names = [n for n in _REF_ORDER if n != "vb"]
    sources = [pieces[n] for n in names] + list(pieces["vb"])

    def src_of(c):
        j, i = divmod(c, W_IN_PAD)
        if i >= W_IN_SH:
            return None
        name, col = _ref_piece(j * W_IN_SH + i)
        if name == "vb":
            return len(names) + col // DIL_OUT_W, col % DIL_OUT_W
        return names.index(name), col

    return _place_cols(sources, src_of, N_DEV * W_IN_PAD, "place_dproj")


_SLABS = {"ga": C_GA, "gb": C_GB, "qb": C_QB, "kb": C_KB, "vb": C_VB, "qa": C_QA, "ka": C_KA, "va": C_VA, "f": C_F}


def _slab_w_in(stack):
    def src_of(c):
        for name, start in _SLABS.items():
            lo, width = _REF_COLS[name]
            if start <= c < start + width:
                return divmod(lo + c - start, W_IN_SH)
        return None

    return _place_cols([(stack, j) for j in range(N_DEV)], src_of, PROJ_W, "place_w_in")


def kernel(x, c, w_ada, b_ada, g_mix, w_in, b_fgate, w_br_a, w_br_b, w_out, g_ffn, w_ffn_gate, w_ffn_up, w_ffn_down, g_final, loss_target, m_w_ada, m_b_ada, m_g_mix, m_w_in, m_b_fgate, m_w_br_a, m_w_br_b, m_w_out, m_g_ffn, m_w_ffn_gate, m_w_ffn_up, m_w_ffn_down, m_g_final, v_w_ada, v_b_ada, v_g_mix, v_w_in, v_b_fgate, v_w_br_a, v_w_br_b, v_w_out, v_g_ffn, v_w_ffn_gate, v_w_ffn_up, v_w_ffn_down, v_g_final):
    px, py, pc = _position()
    dev = 4 * px + 2 * py + pc
    x2d, tgt = x[0], loss_target[0]

    c_all = _all_gather(c, "gather_c").reshape(N_DEV, D)
    ada_cols = w_ada.shape[2]
    b_shard = lax.dynamic_slice(b_ada, (0, dev * ada_cols), (1, ada_cols))
    mod_shard = _ada_fwd(c_all, w_ada[0], b_shard)
    mod_all = _all_gather(mod_shard, "gather_mod")
    modv = lax.dynamic_index_in_dim(mod_all, dev, axis=1, keepdims=False).reshape(6, D)
    h1 = _pre1(x2d, modv, g_mix)

    w_in_s = _all_gather(_pad_to(w_in[0], 1, W_IN_PAD).astype(bf16), "gather_w_in")
    gate_up = jnp.concatenate([_pad_to(w_ffn_gate[0], 1, FF_PAD), _pad_to(w_ffn_up[0], 1, FF_PAD)], axis=1)
    later = [w_br_a[0], w_br_b[0], w_out[0], gate_up, _pad_to(w_ffn_down[0], 0, FF_PAD)]
    later_state, later_token = _gather_start([t.astype(bf16) for t in later], w_in_s, "gather_rest_start")
    w_in_p = _slab_w_in(w_in_s)

    proj = _matmul(h1, w_in_p, name="mm_proj", tm=SEQ, tn=896, after=later_token)
    b_pad = jnp.pad(b_fgate, ((0, 0), (0, LANES - N_FOX_HEADS)))
    q_aug, k_aug, va = _fox_prep(proj, _fox_gate_fwd(proj, b_pad))
    ya_h, max_a, sum_a = _fox_fwd(q_aug, k_aug, va)

    tables = _rope_tables()
    qb_r, kb_r = _rope_fwd(proj, tables)
    by_group = [_dil_fwd(qb_r, kb_r, proj, grp) for grp in range(N_GROUPS)]
    yb_h, lse_b = _dil_combine([o for o, _ in by_group], [l for _, l in by_group])

    both_done = ya_h[:8, :LANES] + yb_h[:8, :LANES]
    mine, arrived = _gather_wait(later_state, both_done, "gather_rest_wait")
    w_a_s, w_b_s, w_o_s, w_gu_s, w_d_s = [
        lax.dynamic_update_slice(stack, block[None], (dev, 0, 0))
        for stack, block in zip(_gather_finish(arrived, "gather_rest_finish"), mine, strict=True)]
    w_o = w_o_s.reshape(D, D)
    w_d = w_d_s.reshape(FF_HID, D)
    ya = _matmul_stack(ya_h, w_a_s, name="mm_br_a")
    yb = _matmul_stack(yb_h, w_b_s, name="mm_br_b")

    merged, mix, x1, h2 = _post1(ya, yb, proj, w_o, x2d, modv, g_ffn)
    act, au = _ffn_in(h2, w_gu_s)

    dx2, dff, dg_final, dga_f, loss_lanes = _final(act, w_d, x1, tgt, modv, g_final.reshape(1, D))
    dau = _ffn_bwd_in(dff, w_d_s, au)

    core = pc.astype(jnp.int32).reshape(1)
    chip = (2 * px + py).astype(jnp.int32).reshape(1)

    def pair_done(state, after, tags, name):
        mine, theirs = _exchange_wait("pair", state, after, "pair_wait_" + name)
        sums = [_pair_add(g, r, core, "pair_add_" + t) for g, r, t in zip(mine, theirs, tags)]
        return _exchange_start("chip", sums, "chip_start_" + name)

    def from_chips(state, after, tags, name, forms=None):
        sums, got = _exchange_wait("chip", state, after, "chip_wait_" + name)
        forms = forms or [{}] * len(tags)
        return [_chip_add(p, r, chip, "chip_add_" + t, **f) for p, r, t, f in zip(sums, got, tags, forms)]

    g_gu = _matmul(h2, dau, ta=True, by_shard=True, out_dtype=bf16, name="mm_g_ffn_in", tm=D, tn=2 * FF_PAD)
    g_d = _matmul(act, dff, ta=True, out_dtype=bf16, name="mm_g_down", tm=FF_HID // 2, tn=512)
    ffn_tags = ["gu", "down"]
    ffn_pair, ffn_pair_token = _exchange_start("pair", [g_gu, g_d.reshape(N_DEV, FF_PAD, D)], "pair_start_ffn")

    dx1, dmix, dsh_f, dsc_f, dg_ffn, dga_m = _mid_bwd(dau, w_gu_s, ffn_pair_token, x1, dx2, mix, modv, g_ffn)
    ffn_state, ffn_token = pair_done(ffn_pair, dx1, ffn_tags, "ffn")
    dya, dyb, dga, dgb = _merge_bwd(dmix, w_o, ffn_token, ya, yb, proj)
    dya_h = _matmul_stack(dya, w_a_s, tb=True, name="mm_d_ya")
    dyb_h = _matmul_stack(dyb, w_b_s, tb=True, name="mm_d_yb")

    g_o = _matmul(merged, dmix, ta=True, out_dtype=bf16, name="mm_g_out", tm=D, tn=512)
    g_a = _matmul_stack(ya_h, dya, ta=True, out_dtype=bf16, name="mm_g_br_a")
    g_b = _matmul_stack(yb_h, dyb, ta=True, out_dtype=bf16, name="mm_g_br_b")
    rows_a, rows_b = FOX_W * W_BR_SH // D, DIL_OUT_W * W_BR_SH // D
    g_small = jnp.concatenate([g_a.reshape(N_DEV, rows_a, D), g_b.reshape(N_DEV, rows_b, D),
                               g_o.reshape(N_DEV, W_BR_SH, D)], axis=1)
    small_pair, small_pair_token = _exchange_start("pair", [g_small], "pair_start_small")

    dqa, dka, dva, dF = _fox_bwd(q_aug, k_aug, va, dya_h, ya_h, max_a, sum_a, small_pair_token)
    dF_row = jnp.pad(dF[:, :2, :].reshape(N_FOX_HEADS, SEQ), ((0, LANES - N_FOX_HEADS), (0, 0)))
    df, db_fgate = _fox_gate_bwd(dF_row, proj, b_pad)
    small_state, small_token = pair_done(small_pair, df, ["small"], "small")

    delta_b = _dil_delta(dyb_h, yb_h)
    dil_grads = [_dil_bwd(qb_r, kb_r, proj, dyb_h, lse_b, delta_b, grp) for grp in range(N_GROUPS)]
    dqb, dkb = _rope_bwd([t[0] for t in dil_grads], [t[1] for t in dil_grads], tables)

    dproj = _shard_pad_cols({"qa": dqa, "ka": dka, "va": dva, "f": df, "qb": dqb, "kb": dkb,
                             "vb": [t[2] for t in dil_grads], "ga": dga, "gb": dgb})
    g_in = _matmul(h1, dproj, ta=True, by_shard=True, out_dtype=bf16, name="mm_g_in", tm=D, tn=W_IN_PAD,
                   after=small_token)
    mix_tags = ["in"]
    mix_pair, mix_pair_token = _exchange_start("pair", [g_in], "pair_start_mixer")

    w = {"w_ada": w_ada, "b_ada": b_ada, "g_mix": g_mix, "w_in": w_in, "b_fgate": b_fgate, "w_br_a": w_br_a,
         "w_br_b": w_br_b, "w_out": w_out, "g_ffn": g_ffn, "w_ffn_gate": w_ffn_gate, "w_ffn_up": w_ffn_up,
         "w_ffn_down": w_ffn_down, "g_final": g_final}
    m = {"w_ada": m_w_ada, "b_ada": m_b_ada, "g_mix": m_g_mix, "w_in": m_w_in, "b_fgate": m_b_fgate,
         "w_br_a": m_w_br_a, "w_br_b": m_w_br_b, "w_out": m_w_out, "g_ffn": m_g_ffn, "w_ffn_gate": m_w_ffn_gate,
         "w_ffn_up": m_w_ffn_up, "w_ffn_down": m_w_ffn_down, "g_final": m_g_final}
    v = {"w_ada": v_w_ada, "b_ada": v_b_ada, "g_mix": v_g_mix, "w_in": v_w_in, "b_fgate": v_b_fgate,
         "w_br_a": v_w_br_a, "w_br_b": v_w_br_b, "w_out": v_w_out, "g_ffn": v_g_ffn, "w_ffn_gate": v_w_ffn_gate,
         "w_ffn_up": v_w_ffn_up, "w_ffn_down": v_w_ffn_down, "g_final": v_g_final}
    names = list(w)
    g, delta, new_m, new_v = {}, {}, {}, {}

    transposed = ("w_ffn_gate", "w_ffn_up")
    by_column = lambda t: jnp.transpose(t, (2, 0, 1))
    by_row = lambda t: jnp.transpose(t, (1, 2, 0))

    def update(n):
        shape = w[n].shape
        if n == "w_in":
            g3 = g[n]
            dl, mn, vn = _adamw_by_planes(by_column(w[n]), g3, by_column(m[n]), by_column(v[n]), "adamw_" + n)
            g[n], delta[n], new_m[n], new_v[n] = by_row(g3), by_row(dl), by_row(mn), by_row(vn)
            return
        if n in transposed:
            g_t = g[n]
            dl, mn, vn = _adamw(w[n][0].T, g_t, m[n][0].T, v[n][0].T, "adamw_" + n)
            g[n], delta[n], new_m[n], new_v[n] = g_t.T[None], dl.T[None], mn.T[None], vn.T[None]
            return
        two_d = (lambda t: t.reshape(shape[-2:])) if len(shape) == 3 else (lambda t: t)
        dl, mn, vn = _adamw(two_d(w[n]), two_d(g[n]), two_d(m[n]), two_d(v[n]), "adamw_" + n)
        delta[n], new_m[n], new_v[n] = dl.reshape(shape), mn.reshape(shape), vn.reshape(shape)

    def update_all(grads):
        g.update(grads)
        for n in grads:
            update(n)
        return sum(delta[n][(0,) * (delta[n].ndim - 1)][:N_FOX_HEADS] for n in grads)

    (g_gate_t, g_up_t), s_d = from_chips(ffn_state, mix_pair_token, ffn_tags, "ffn",
                                         [dict(pieces=[(0, W_FF_SH), (FF_PAD, W_FF_SH)]), {}])
    gate_up_done = update_all({"w_ffn_gate": g_gate_t, "w_ffn_up": g_up_t})
    mix_state, mix_token = pair_done(mix_pair, gate_up_done, mix_tags, "mixer")

    grad_x, dsh_m, dsc_m, dg_mix = _first_bwd(dproj, w_in_s, mix_token, x2d, dx1, modv, g_mix)

    s_small, = from_chips(small_state, grad_x, ["small"], "small")
    sharded_done = update_all({
        "w_ffn_down": s_d[None, :W_FF_SH],
        "w_br_a": s_small[:rows_a].reshape(1, FOX_W, W_BR_SH),
        "w_br_b": s_small[rows_a:rows_a + rows_b].reshape(1, DIL_OUT_W, W_BR_SH), "w_out": s_small[None, rows_a + rows_b:],
    })

    pad_lane = lambda t: jnp.pad(t, ((0, 0), (0, D - t.shape[1])))
    small = jnp.concatenate([dsh_m, dsc_m, dga_m, dsh_f, dsc_f, dga_f, dg_mix, dg_ffn, dg_final,
                             pad_lane(db_fgate), loss_lanes, jnp.zeros((SMALL_ROWS - 11, D), f32)], axis=0)
    small_all = _all_gather(small, "gather_small", after=sharded_done)
    small_sum, loss_row = _small_reduce(small_all, mix_token)
    loss = loss_row[0, 0]
    dmod_all = small_all[:, :6, :].reshape(N_DEV, 6 * D)
    g_w_ada = _ada_bwd(c_all, lax.dynamic_slice(dmod_all, (0, dev * ada_cols), (N_DEV, ada_cols)))
    done = update_all({
        "w_ada": g_w_ada[None], "b_ada": small_sum[0:6].reshape(1, 6 * D), "g_mix": small_sum[6:7],
        "b_fgate": small_sum[9:10, :N_FOX_HEADS], "g_ffn": small_sum[7:8], "g_final": small_sum[8],
    })
    mix_sums, mix_got = _exchange_wait("chip", mix_state, done, "chip_wait_mixer")
    g["w_in"] = _chip_add(mix_sums[0], mix_got[0], chip, "chip_add_in", planes=W_IN_SH)
    update("w_in")

    return (loss, grad_x[None], *[g[n] for n in names], *[delta[n] for n in names],
            *[new_m[n] for n in names], *[new_v[n] for n in names])
```

```python
import jax
import jax.numpy as jnp
import numpy as np
from jax import lax
from jax.experimental import pallas as pl
from jax.experimental.pallas import tpu as pltpu

f32 = jnp.float32
bf16 = jnp.bfloat16
SDS = jax.ShapeDtypeStruct
MESH = pl.DeviceIdType.MESH

N_DEV = 8
D = 1024
SEQ = 2048
HEAD_DIM = 64
N_FOX_HEADS = 8
FOX_W = 512
DIL_W = 768
DIL_OUT_W = 256
ROT_DIM = 16
ROPE_THETA = 500000.0
D_FF = 2816
IN_COLS = 5896
EPS = 1e-6
NEG = -1e30
ATT_SCALE = HEAD_DIM ** -0.5

ADAM_LR = 0.001
ADAM_B1 = 0.9
ADAM_B2 = 0.999
ADAM_EPS = 1e-08
ADAM_WD = 0.01
ADAM_STEP = 10

C_GA, C_GB, C_QB, C_KB, C_VB, C_QA, C_KA, C_VA, C_F = 0, 1024, 2304, 3072, 3840, 4608, 5120, 5632, 6144
PROJ_W = 6272
LANES = 128
VMEM_LIMIT = 52 * 1024 * 1024

W_IN_SH, W_IN_PAD = IN_COLS // N_DEV, 768
W_BR_SH = D // N_DEV
W_FF_SH, FF_PAD = D_FF // N_DEV, 384
FF_HID = N_DEV * FF_PAD
SMALL_ROWS = 16


def _params(sem=None):
    if sem is None:
        return pltpu.CompilerParams(vmem_limit_bytes=VMEM_LIMIT)
    return pltpu.CompilerParams(dimension_semantics=sem, vmem_limit_bytes=VMEM_LIMIT)


def _rowwise(fn, name, tiled, vecs, outs, reds=(), tile=256):
    nt, nv, no = len(tiled), len(vecs), len(outs)
    rows = tiled[0][0].shape[0]
    assert rows % tile == 0

    def body(*refs):
        tin = [r[...] for r in refs[:nt]]
        vin = [r[...] for r in refs[nt:nt + nv]]
        orefs = refs[nt + nv:nt + nv + no]
        rrefs = refs[nt + nv + no:]
        touts, routs = fn(tin, vin)
        for r, t in zip(orefs, touts, strict=True):
            r[...] = t.astype(r.dtype)
        if rrefs:
            @pl.when(pl.program_id(0) == 0)
            def _():
                for r in rrefs:
                    r[...] = jnp.zeros_like(r)
            for r, t in zip(rrefs, routs, strict=True):
                r[...] += t

    def col_map(cb):
        return lambda i: (i, cb)

    def whole_map(nd):
        return lambda i: (0,) * nd

    in_specs = [pl.BlockSpec((tile, w), col_map(cb)) for (_, w, cb) in tiled]
    in_specs += [pl.BlockSpec(v.shape, whole_map(v.ndim)) for v in vecs]
    out_specs = [pl.BlockSpec((tile, w), lambda i: (i, 0)) for (w, _) in outs]
    out_specs += [pl.BlockSpec((1, w), lambda i: (0, 0)) for w in reds]
    out_shape = [SDS((rows, w), dt) for (w, dt) in outs] + [SDS((1, w), f32) for w in reds]
    res = pl.pallas_call(
        body, grid=(rows // tile,), in_specs=in_specs, out_specs=out_specs, out_shape=out_shape, name=name,
        compiler_params=_params(("arbitrary",)),
    )(*[t[0] for t in tiled], *vecs)
    return res


def _matmul(a, b, *, ta=False, out_dtype=f32, name, tm, tn, by_shard=False, after=None):
    (m, k), n = ((a.shape[1], a.shape[0]) if ta else a.shape), b.shape[1]
    assert b.shape[0] == k and m % tm == 0 and n % tn == 0 and (ta or not by_shard)
    dims = (((0 if ta else 1,), (0,)), ((), ()))

    def body(a_ref, b_ref, *rest):
        p = lax.dot_general(a_ref[...].astype(bf16), b_ref[...].astype(bf16), dims, preferred_element_type=f32)
        o_ref = rest[-1]
        if by_shard:
            o_ref[0] = p.astype(o_ref.dtype)
        else:
            o_ref[...] = p.astype(o_ref.dtype)

    a_spec = pl.BlockSpec((k, tm), lambda i, j: (0, i)) if ta else pl.BlockSpec((tm, k), lambda i, j: (i, 0))
    if by_shard:
        assert tn == n // N_DEV
        out_spec, out_shape = pl.BlockSpec((1, tm, tn), lambda i, j: (j, i, 0)), SDS((N_DEV, m, tn), out_dtype)
    else:
        out_spec, out_shape = pl.BlockSpec((tm, tn), lambda i, j: (i, j)), SDS((m, n), out_dtype)
    extra_specs, extra = ([pl.BlockSpec(memory_space=pl.ANY)], [after]) if after is not None else ([], [])
    return pl.pallas_call(
        body, grid=(m // tm, n // tn), in_specs=[a_spec, pl.BlockSpec((k, tn), lambda i, j: (0, j))] + extra_specs,
        out_specs=out_spec, out_shape=out_shape, name=name, compiler_params=_params(("parallel", "parallel")),
    )(a, b, *extra)


def _matmul_stack(a, b, *, ta=False, tb=False, out_dtype=f32, name):
    def lanes(ref):
        return jnp.concatenate([ref[j] for j in range(N_DEV)], axis=1).astype(bf16)

    if ta:
        w = b.shape[1] // N_DEV

        def body(a_ref, b_ref, o_ref):
            p = _tn(a_ref[...].astype(bf16), b_ref[...].astype(bf16))
            for j in range(N_DEV):
                o_ref[j] = p[:, j * w:(j + 1) * w].astype(o_ref.dtype)

        return pl.pallas_call(body, out_shape=SDS((N_DEV, a.shape[1], w), out_dtype), name=name,
                              compiler_params=_params())(a, b)

    m, half = a.shape[0], a.shape[0] // 2
    n = b.shape[1] if tb else N_DEV * b.shape[2]

    def body(a_ref, b_ref, o_ref):
        av = a_ref[...].astype(bf16)
        o_ref[...] = (_nt(av, lanes(b_ref)) if tb else jnp.dot(av, lanes(b_ref), preferred_element_type=f32)
                      ).astype(o_ref.dtype)

    return pl.pallas_call(
        body, grid=(2,), in_specs=[pl.BlockSpec((half, a.shape[1]), lambda i: (i, 0)),
                                   pl.BlockSpec(b.shape, lambda i: (0, 0, 0))],
        out_specs=pl.BlockSpec((half, n), lambda i: (i, 0)), out_shape=SDS((m, n), out_dtype), name=name,
        compiler_params=_params(("parallel",)),
    )(a, b)


def _matmul_rows(form, a, b, after, fn, tiled, vecs, outs, reds, *, name, tm=512):
    norm = lambda ts: [t if isinstance(t, tuple) else (t, t.shape[1], 0) for t in ts]
    make, sources = a if isinstance(a, tuple) else (None, [a])
    sources, tiled = norm(sources), norm(tiled)
    m, k = sources[0][0].shape[0], (b.shape[0] if form == "nn" else b.shape[-1] * (N_DEV if form == "nt_stack" else 1))
    assert m % tm == 0
    ns, nt, nv, no = len(sources), len(tiled), len(vecs), len(outs)

    def body(*refs):
        src_refs, b_ref, refs = refs[:ns], refs[ns], refs[ns + 2:]
        if make is None:
            lhs = lambda lo, hi: src_refs[0][:, lo:hi]
        else:
            made = make([r[...] for r in src_refs]).astype(bf16)
            lhs = lambda lo, hi: made[:, lo:hi]
        if form == "nt_stack":
            w = b.shape[2]
            acc = _nt(lhs(0, w), b_ref[0])
            for j in range(1, N_DEV):
                acc = acc + _nt(lhs(j * w, (j + 1) * w), b_ref[j])
        elif form == "nt":
            acc = _nt(lhs(0, k), b_ref[...])
        else:
            acc = jnp.dot(lhs(0, k), b_ref[...], preferred_element_type=f32)
        if make is not None:
            refs[nt + nv][...] = made
            refs = refs[:nt + nv] + refs[nt + nv + 1:]
        orefs, rrefs = refs[nt + nv:nt + nv + no], refs[nt + nv + no:]
        touts, routs = fn([acc] + [r[...] for r in refs[:nt]], [r[...] for r in refs[nt:nt + nv]])
        for r, t in zip(orefs, touts, strict=True):
            r[...] = t.astype(r.dtype)

        @pl.when(pl.program_id(0) == 0)
        def _():
            for r in rrefs:
                r[...] = jnp.zeros_like(r)
        for r, t in zip(rrefs, routs, strict=True):
            r[...] += t

    def whole_map(nd):
        return lambda i: (0,) * nd

    def rows(width, cb=0):
        return pl.BlockSpec((tm, width), lambda i: (i, cb))

    made_out = [(k, bf16)] if make is not None else []
    return pl.pallas_call(
        body, grid=(m // tm,),
        in_specs=[rows(width, cb) for _, width, cb in sources]
        + [pl.BlockSpec(b.shape, whole_map(b.ndim), pipeline_mode=pl.Buffered(1)), pl.BlockSpec(memory_space=pl.ANY)]
        + [rows(width, cb) for _, width, cb in tiled] + [pl.BlockSpec(v.shape, whole_map(v.ndim)) for v in vecs],
        out_specs=[rows(width) for width, _ in made_out + list(outs)]
        + [pl.BlockSpec((1, width), lambda i: (0, 0)) for width in reds],
        out_shape=[SDS((m, width), dt) for width, dt in made_out + list(outs)]
        + [SDS((1, width), f32) for width in reds], name=name,
        compiler_params=_params(("arbitrary",)),
    )(*[t[0] for t in sources], b, after, *[t[0] for t in tiled], *vecs)


def _rms(x):
    r = lax.rsqrt(jnp.mean(x * x, axis=-1, keepdims=True) + EPS)
    return r, x * r


def _rms_bwd(r, xn, dxn):
    return r * (dxn - xn * jnp.mean(dxn * xn, axis=-1, keepdims=True))


def _colsum(t):
    return jnp.sum(t, axis=0, keepdims=True)


def _sigmoid(x):
    return 0.5 * jnp.tanh(0.5 * x) + 0.5


def _modulated_norm(x, g, shift, scale):
    _, xn = _rms(x)
    return (xn * g) * (1.0 + scale) + shift


def _pre1(x, modv, g_mix):
    def fn(t, v):
        (xt,), (mv, g) = t, v
        return [_modulated_norm(xt, g, mv[0:1], mv[1:2])], []
    return _rowwise(fn, "pre1", [(x, D, 0)], [modv, g_mix], [(D, bf16)])[0]


def _post1(ya, yb, proj, w_o, x, modv, g_ffn):
    def merge(t):
        ya_t, yb_t, ga, gb = t
        return _sigmoid(ga) * ya_t + _sigmoid(gb) * yb_t

    def fn(t, v):
        (mt, xt), (mv, g) = t, v
        x1 = xt + mv[2:3] * mt
        return [mt, x1, _modulated_norm(x1, g, mv[3:4], mv[4:5])], []
    return _matmul_rows("nn", (merge, [ya, yb, (proj, D, C_GA // D), (proj, D, C_GB // D)]), w_o, x, fn, [x],
                        [modv, g_ffn], [(D, f32), (D, f32), (D, bf16)], [], name="post1")


def _ffn_in(h, w_stack):
    def body(h_ref, w_ref, act_ref, au_ref):
        p = jnp.dot(h_ref[...], w_ref[0], preferred_element_type=f32)
        a, u = p[:, :FF_PAD], p[:, FF_PAD:]
        act_ref[...] = (a * _sigmoid(a) * u).astype(act_ref.dtype)
        au_ref[...] = p.astype(au_ref.dtype)

    return pl.pallas_call(
        body, grid=(N_DEV,),
        in_specs=[pl.BlockSpec((SEQ, D), lambda j: (0, 0)), pl.BlockSpec((1, D, 2 * FF_PAD), lambda j: (j, 0, 0))],
        out_specs=[pl.BlockSpec((SEQ, FF_PAD), lambda j: (0, j)), pl.BlockSpec((SEQ, 2 * FF_PAD), lambda j: (0, j))],
        out_shape=(SDS((SEQ, FF_HID), bf16), SDS((SEQ, 2 * FF_HID), bf16)), name="ffn_in",
        compiler_params=_params(("parallel",)),
    )(h, w_stack)


def _ffn_bwd_in(dff, w_down_stack, au):
    def body(d_ref, w_ref, au_ref, o_ref):
        dact = _nt(d_ref[...], w_ref[0])
        p = au_ref[...].astype(f32)
        a, u = p[:, :FF_PAD], p[:, FF_PAD:]
        sg = _sigmoid(a)
        o_ref[...] = jnp.concatenate([dact * u * (sg * (1.0 + a * (1.0 - sg))), dact * (a * sg)],
                                     axis=1).astype(o_ref.dtype)

    return pl.pallas_call(
        body, grid=(N_DEV,),
        in_specs=[pl.BlockSpec((SEQ, D), lambda j: (0, 0)), pl.BlockSpec((1, FF_PAD, D), lambda j: (j, 0, 0)),
                  pl.BlockSpec((SEQ, 2 * FF_PAD), lambda j: (0, j))],
        out_specs=pl.BlockSpec((SEQ, 2 * FF_PAD), lambda j: (0, j)),
        out_shape=SDS((SEQ, 2 * FF_HID), bf16), name="ffn_bwd_in", compiler_params=_params(("parallel",)),
    )(dff, w_down_stack, au)


def _final(act, w_down, x1, target, modv, g_final):
    def fn(t, v):
        (fft, x1t, tgt), (mv, g) = t, v
        x2 = x1t + mv[5:6] * fft
        r, xn = _rms(x2)
        err = xn * g - tgt
        dy = err * (1.0 / D)
        dx2 = _rms_bwd(r, xn, dy * g)
        return [dx2, dx2 * mv[5:6]], [_colsum(dy * xn), _colsum(dx2 * fft), _colsum(err * err) * (0.5 / D)]
    return _matmul_rows("nn", act, w_down, x1, fn, [x1, target], [modv, g_final], [(D, f32), (D, bf16)], [D, D, D],
                        name="final")


def _mid_bwd(dau, w_stack, after, x1, dx2, mix, modv, g_ffn):
    def fn(t, v):
        (dh, x1t, dx2t, mt), (mv, g) = t, v
        r, xn = _rms(x1t)
        dn = dh * (1.0 + mv[4:5])
        dx1 = dx2t + _rms_bwd(r, xn, dn * g)
        return [dx1, dx1 * mv[2:3]], [_colsum(dh), _colsum(dh * (xn * g)), _colsum(dn * xn), _colsum(dx1 * mt)]
    return _matmul_rows("nt_stack", dau, w_stack, after, fn, [x1, dx2, mix], [modv, g_ffn], [(D, f32), (D, bf16)],
                        [D, D, D, D], name="mid_bwd")


def _first_bwd(dproj, w_stack, after, x, dx1, modv, g_mix):
    def fn(t, v):
        (dh, xt, dx1t), (mv, g) = t, v
        r, xn = _rms(xt)
        dn = dh * (1.0 + mv[1:2])
        return [dx1t + _rms_bwd(r, xn, dn * g)], [_colsum(dh), _colsum(dh * (xn * g)), _colsum(dn * xn)]
    return _matmul_rows("nt_stack", dproj, w_stack, after, fn, [x, dx1], [modv, g_mix], [(D, f32)], [D, D, D],
                        name="first_bwd")


def _merge_bwd(dmix, w_o, after, ya, yb, proj):
    def fn(t, v):
        dm, ya_t, yb_t, ga, gb = t
        sa, sb = _sigmoid(ga), _sigmoid(gb)
        return [dm * sa, dm * sb, dm * ya_t * (sa * (1.0 - sa)), dm * yb_t * (sb * (1.0 - sb))], []
    return _matmul_rows("nt", dmix, w_o, after, fn, [ya, yb, (proj, D, C_GA // D), (proj, D, C_GB // D)], [],
                        [(D, bf16), (D, bf16), (D, bf16), (D, bf16)], [], name="merge_bwd")


def _rope_tables():
    half = ROT_DIM // 2
    pos = np.arange(SEQ, dtype=np.float32)
    inv_freq = np.float32(ROPE_THETA) ** (-np.arange(0, ROT_DIM, 2, dtype=np.float32) / np.float32(ROT_DIM))
    ang = pos[:, None] * inv_freq[None, :].astype(np.float32)
    cos, sin = np.cos(ang).astype(np.float32), np.sin(ang).astype(np.float32)
    pad = np.zeros((SEQ, HEAD_DIM - ROT_DIM), np.float32)
    zero = np.zeros((SEQ, half), np.float32)
    c_head = np.concatenate([cos, cos, pad + 1.0], axis=1)
    lo_head = np.concatenate([-sin, zero, pad], axis=1)
    hi_head = np.concatenate([zero, sin, pad], axis=1)
    return tuple(jnp.asarray(np.concatenate([t, t], axis=1)) for t in (c_head, lo_head, hi_head))


def _over_heads(tables):
    return [jnp.tile(t, (1, DIL_W // LANES)) for t in tables]


def _rope_fwd(proj, tables):
    half = ROT_DIM // 2

    def fn(t, v):
        q, k = t[:2]
        c, lo, hi = _over_heads(t[2:])
        rot = lambda z: z * c + pltpu.roll(z, DIL_W - half, 1) * lo + pltpu.roll(z, half, 1) * hi
        return [rot(q) * ATT_SCALE, rot(k)], []
    return _rowwise(fn, "rope_fwd", [(proj, DIL_W, C_QB // DIL_W), (proj, DIL_W, C_KB // DIL_W)]
                    + [(tb, LANES, 0) for tb in tables], [], [(DIL_W, f32)] * 2)


def _rope_bwd(dqs, dks, tables):
    half = ROT_DIM // 2

    def fn(t, v):
        dq_t, dk_t = jnp.concatenate(t[:N_GROUPS], axis=1), jnp.concatenate(t[N_GROUPS:2 * N_GROUPS], axis=1)
        c, lo, hi = _over_heads(t[2 * N_GROUPS:])
        rot_t = lambda z: z * c + pltpu.roll(z * lo, half, 1) + pltpu.roll(z * hi, DIL_W - half, 1)
        return [rot_t(dq_t), rot_t(dk_t)], []
    return _rowwise(fn, "rope_bwd", [(a, DIL_OUT_W, 0) for a in (*dqs, *dks)] + [(tb, LANES, 0) for tb in tables],
                    [], [(DIL_W, bf16), (DIL_W, bf16)])


def _head_bcast_sum(d):
    lane = lax.broadcasted_iota(jnp.int32, d.shape, 1)
    out = jnp.zeros_like(d)
    for h in range(d.shape[1] // HEAD_DIM):
        sel = (lane >= h * HEAD_DIM) & (lane < (h + 1) * HEAD_DIM)
        out = jnp.where(sel, jnp.sum(jnp.where(sel, d, 0.0), axis=1, keepdims=True), out)
    return out


def _dil_combine(outs, lses):
    def fn(t, v):
        o0, o1, o2, l0, l1, l2 = t
        m = jnp.maximum(jnp.maximum(l0, l1), l2)
        w0, w1, w2 = jnp.exp(l0 - m), jnp.exp(l1 - m), jnp.exp(l2 - m)
        tot = w0 + w1 + w2
        return [(w0 * o0 + w1 * o1 + w2 * o2) / tot, m + jnp.log(tot)], []
    w = DIL_OUT_W
    return _rowwise(fn, "dil_combine", [(t, w, 0) for t in (*outs, *lses)], [], [(w, f32), (w, f32)])


def _dil_delta(dyb_h, yb_h):
    def fn(t, v):
        return [_head_bcast_sum(t[0] * t[1])], []
    return _rowwise(fn, "dil_delta", [(dyb_h, DIL_OUT_W, 0), (yb_h, DIL_OUT_W, 0)], [], [(DIL_OUT_W, f32)])[0]


def _adamw_math(wt, gt, mt, vt):
    mn = ADAM_B1 * mt + (1.0 - ADAM_B1) * gt
    vn = ADAM_B2 * vt + (1.0 - ADAM_B2) * (gt * gt)
    m_hat = mn / (1.0 - ADAM_B1 ** ADAM_STEP)
    v_hat = vn / (1.0 - ADAM_B2 ** ADAM_STEP)
    return -ADAM_LR * (m_hat / (jnp.sqrt(v_hat) + ADAM_EPS) + ADAM_WD * wt), mn, vn


def _adamw(w, g, m, v, name):
    shape = w.shape
    if w.ndim == 1:
        w, g, m, v = (t.reshape(1, -1) for t in (w, g, m, v))
    rows, cols = w.shape
    tile = 256 if rows % 256 == 0 and rows > 512 else rows

    def fn(t, _):
        return list(_adamw_math(*t)), []
    delta, mn, vn = _rowwise(fn, name, [(w, cols, 0), (g, cols, 0), (m, cols, 0), (v, cols, 0)], [],
                             [(cols, f32)] * 3, tile=tile)
    return delta.reshape(shape), mn.reshape(shape), vn.reshape(shape)


def _adamw_by_planes(w, g, m, v, name, most=128):
    planes, _, width = w.shape
    tile = max(t for t in range(1, most + 1) if planes % t == 0)

    def body(w_ref, g_ref, m_ref, v_ref, d_ref, mn_ref, vn_ref):
        d_ref[...], mn_ref[...], vn_ref[...] = _adamw_math(w_ref[...], g_ref[...], m_ref[...], v_ref[...])

    spec = pl.BlockSpec((tile, 1, width), lambda i: (i, 0, 0))
    return pl.pallas_call(body, grid=(planes // tile,), in_specs=[spec] * 4, out_specs=[spec] * 3,
                          out_shape=[SDS(w.shape, f32)] * 3, name=name,
                          compiler_params=_params(("parallel",)))(w, g, m, v)


def _ada_fwd(c_all, w_shard, b_shard):
    def body(c_ref, w_ref, b_ref, o_ref):
        cv = c_ref[...]
        sc = (cv * _sigmoid(cv)).astype(bf16)
        o_ref[...] = jnp.dot(sc, w_ref[...].astype(bf16), preferred_element_type=f32) + b_ref[...]
    return pl.pallas_call(body, out_shape=SDS((N_DEV, w_shard.shape[1]), f32), name="ada_fwd",
                          compiler_params=_params())(c_all, w_shard, b_shard)


def _ada_bwd(c_all, dmod_cols):
    def body(c_ref, d_ref, o_ref):
        cv = c_ref[...]
        sc = cv * _sigmoid(cv)
        o_ref[...] = lax.dot_general(sc, d_ref[...], (((0,), (0,)), ((), ())), precision=lax.Precision.HIGHEST,
                                     preferred_element_type=f32)
    return pl.pallas_call(body, out_shape=SDS((D, dmod_cols.shape[1]), f32), name="ada_bwd",
                          compiler_params=_params())(c_all, dmod_cols)


def _small_reduce(gathered, after):
    def body(g_ref, after_ref, o_ref, loss_ref):
        acc = g_ref[0]
        for d in range(1, N_DEV):
            acc = acc + g_ref[d]
        o_ref[...] = acc
        loss_ref[...] = jnp.zeros((1, LANES), f32) + jnp.sum(acc[10:11, :])
    return pl.pallas_call(body, out_shape=(SDS((SMALL_ROWS, D), f32), SDS((1, LANES), f32)), name="small_reduce",
                          in_specs=[pl.BlockSpec(memory_space=pltpu.VMEM), pl.BlockSpec(memory_space=pl.ANY)],
                          compiler_params=_params())(gathered, after)


FOX_BLK = 512
CUM_BLK = 128


def _fold_lanes(t, op):
    out = t[:, :LANES]
    for j in range(1, t.shape[1] // LANES):
        out = op(out, t[:, j * LANES:(j + 1) * LANES])
    return out


def _fox_gate_fwd(proj, b_pad):
    nblk = SEQ // CUM_BLK

    def body(f_ref, b_ref, col_ref):
        r = lax.broadcasted_iota(jnp.int32, (CUM_BLK, CUM_BLK), 0)
        c = lax.broadcasted_iota(jnp.int32, (CUM_BLK, CUM_BLK), 1)
        tri = (r >= c).astype(f32)
        carry = jnp.zeros((1, LANES), f32)
        for blk in range(nblk):
            z = f_ref[blk * CUM_BLK:(blk + 1) * CUM_BLK, :] + b_ref[...]
            logf = jnp.minimum(z, 0.0) - jnp.log1p(jnp.exp(-jnp.abs(z)))
            cs = jnp.dot(tri, logf, precision=lax.Precision.HIGHEST, preferred_element_type=f32) + carry
            col_ref[blk * CUM_BLK:(blk + 1) * CUM_BLK, :] = cs
            carry = cs[CUM_BLK - 1:CUM_BLK, :]

    return pl.pallas_call(
        body, grid=(1,), in_specs=[pl.BlockSpec((SEQ, LANES), lambda i: (0, C_F // LANES)),
                                   pl.BlockSpec((1, LANES), lambda i: (0, 0))],
        out_specs=pl.BlockSpec((SEQ, LANES), lambda i: (0, 0)),
        out_shape=SDS((SEQ, LANES), f32), name="fox_gate_fwd",
        compiler_params=_params(("arbitrary",)),
    )(proj, b_pad)


def _fox_gate_bwd(dF_row, proj, b_pad):
    nblk = SEQ // CUM_BLK

    def body(d_ref, f_ref, b_ref, df_ref, db_ref, col_ref):
        r = lax.broadcasted_iota(jnp.int32, (CUM_BLK, CUM_BLK), 0)
        c = lax.broadcasted_iota(jnp.int32, (CUM_BLK, CUM_BLK), 1)
        tri = (r <= c).astype(f32)
        lane = lax.broadcasted_iota(jnp.int32, (CUM_BLK, LANES), 1)
        col_ref[...] = d_ref[...].T
        carry = jnp.zeros((1, LANES), f32)
        total = jnp.zeros((1, LANES), f32)
        for blk in reversed(range(nblk)):
            rows = slice(blk * CUM_BLK, (blk + 1) * CUM_BLK)
            cs = jnp.dot(tri, col_ref[rows, :], precision=lax.Precision.HIGHEST, preferred_element_type=f32) + carry
            carry = cs[0:1, :]
            z = f_ref[rows, :] + b_ref[...]
            df = jnp.where(lane < N_FOX_HEADS, cs * _sigmoid(-z), 0.0)
            df_ref[rows, :] = df.astype(df_ref.dtype)
            total = total + _colsum(df)
        db_ref[...] = total

    return pl.pallas_call(
        body, grid=(1,), in_specs=[pl.BlockSpec((LANES, SEQ), lambda i: (0, 0)),
                                   pl.BlockSpec((SEQ, LANES), lambda i: (0, C_F // LANES)),
                                   pl.BlockSpec((1, LANES), lambda i: (0, 0))],
        out_specs=[pl.BlockSpec((SEQ, LANES), lambda i: (0, 0)), pl.BlockSpec((1, LANES), lambda i: (0, 0))],
        out_shape=(SDS((SEQ, LANES), bf16), SDS((1, LANES), f32)), name="fox_gate_bwd",
        scratch_shapes=[pltpu.VMEM((SEQ, LANES), f32)],
        compiler_params=_params(("arbitrary",)),
    )(dF_row, proj, b_pad)


def _nt(a, b):
    return lax.dot_general(a, b, (((1,), (1,)), ((), ())), preferred_element_type=f32)


def _tn(a, b):
    return lax.dot_general(a, b, (((0,), (0,)), ((), ())), preferred_element_type=f32)


def _fox_prep(proj, f_col):
    def fn(t, v):
        q, k, vv, fc = t
        lane = lax.broadcasted_iota(jnp.int32, (q.shape[0], LANES), 1)
        qs, ks = [], []
        for h in range(N_FOX_HEADS):
            pair, pos = divmod(h, 2)
            own = (lane >= pos * HEAD_DIM) & (lane < (pos + 1) * HEAD_DIM)
            base = (1 - pos) * HEAD_DIM
            f = fc[:, h:h + 1]
            hi = f.astype(bf16).astype(f32)
            mid = (f - hi).astype(bf16).astype(f32)
            lo = (f - hi) - mid
            one = jnp.ones_like(f)
            qa = jnp.where(own, q[:, pair * LANES:(pair + 1) * LANES] * ATT_SCALE, 0.0)
            ka = k[:, pair * LANES:(pair + 1) * LANES]
            for idx, (qv, kv) in enumerate([(hi, one), (mid, one), (lo, one), (one, -hi), (one, -mid), (one, -lo)]):
                sel = lane == base + idx
                qa = jnp.where(sel, qv, qa)
                ka = jnp.where(sel, kv, ka)
            qs.append(qa)
            ks.append(ka)
        return [jnp.concatenate(qs, axis=1), jnp.concatenate(ks, axis=1), vv], []
    w = N_FOX_HEADS * LANES
    return _rowwise(fn, "fox_prep", [(proj, FOX_W, C_QA // FOX_W), (proj, FOX_W, C_KA // FOX_W),
                                     (proj, FOX_W, C_VA // FOX_W), (f_col, LANES, 0)], [],
                    [(w, bf16), (w, bf16), (FOX_W, bf16)])


def _fox_fwd(q_aug, k_aug, v):
    blk = FOX_BLK
    npair = FOX_W // LANES

    def body(q_ref, k_ref, v_ref, o_ref, max_ref, sum_ref, s_scr):
        i = pl.program_id(1)
        tri = lax.broadcasted_iota(jnp.int32, (blk, blk), 0) >= lax.broadcasted_iota(jnp.int32, (blk, blk), 1)
        qh = [q_ref[:, h * LANES:(h + 1) * LANES] for h in range(2)]

        def logits(c, masked):
            off = pl.multiple_of(c * blk, blk)
            tops = []
            for h in range(2):
                s = _nt(qh[h], k_ref[pl.ds(off, blk), h * LANES:(h + 1) * LANES])
                if masked:
                    s = jnp.where(tri, s, NEG)
                s_scr[h, :, pl.ds(off, blk)] = s
                tops.append(_fold_lanes(s, jnp.maximum))
            return tops

        def pass_a(c, m):
            return tuple(jnp.maximum(a, b) for a, b in zip(m, logits(c, False)))

        m = lax.fori_loop(0, i, pass_a, tuple(jnp.full((blk, LANES), NEG, f32) for _ in range(2)))
        mx = [jnp.max(jnp.maximum(a, b), axis=1, keepdims=True) for a, b in zip(m, logits(i, True))]

        def pass_b(c, carry):
            off = pl.multiple_of(c * blk, blk)
            vv = v_ref[pl.ds(off, blk), :]
            new = []
            for h in range(2):
                l, acc = carry[h]
                p = jnp.exp(s_scr[h, :, pl.ds(off, blk)] - mx[h]).astype(bf16)
                new.append((l + _fold_lanes(p.astype(f32), jnp.add), acc + jnp.dot(p, vv, preferred_element_type=f32)))
            return tuple(new)

        zero = jnp.zeros((blk, LANES), f32)
        (l_a, acc_a), (l_b, acc_b) = lax.fori_loop(0, i + 1, pass_b, ((zero, zero), (zero, zero)))
        l_a = jnp.sum(l_a, axis=1, keepdims=True)
        l_b = jnp.sum(l_b, axis=1, keepdims=True)
        first = lax.broadcasted_iota(jnp.int32, (blk, LANES), 1) < HEAD_DIM
        o_ref[...] = jnp.where(first, acc_a / l_a, acc_b / l_b)
        max_ref[0] = jnp.where(first, mx[0], mx[1])
        sum_ref[0] = jnp.where(first, l_a, l_b)

    return pl.pallas_call(
        body, grid=(npair, SEQ // blk),
        in_specs=[pl.BlockSpec((blk, 2 * LANES), lambda p, i: (i, p)),
                  pl.BlockSpec((SEQ, 2 * LANES), lambda p, i: (0, p)),
                  pl.BlockSpec((SEQ, LANES), lambda p, i: (0, p))],
        out_specs=[pl.BlockSpec((blk, LANES), lambda p, i: (i, p))]
        + [pl.BlockSpec((1, blk, LANES), lambda p, i: (p, i, 0))] * 2,
        out_shape=(SDS((SEQ, FOX_W), f32),) + (SDS((npair, SEQ, LANES), f32),) * 2, name="fox_fwd",
        scratch_shapes=[pltpu.VMEM((2, blk, SEQ), f32)],
        compiler_params=_params(("parallel", "arbitrary")),
    )(q_aug, k_aug, v)


def _fox_bwd(q_aug, k_aug, v, do, o, row_max, row_sum, after):
    blk = FOX_BLK
    npair = FOX_W // LANES
    nblk = SEQ // blk

    def body(q_ref, k_ref, v_ref, do_ref, o_ref, max_ref, sum_ref, after_ref, dq_ref, dk_ref, dv_ref, df_ref, dq_acc,
             delta_ref, inv_ref):
        inv_ref[...] = 1.0 / sum_ref[0]
        lane_s = lax.broadcasted_iota(jnp.int32, (SEQ, LANES), 1)
        prod = do_ref[...].astype(bf16).astype(f32) * o_ref[...]
        d_a = jnp.sum(jnp.where(lane_s < HEAD_DIM, prod, 0.0), axis=1, keepdims=True)
        d_b = jnp.sum(jnp.where(lane_s >= HEAD_DIM, prod, 0.0), axis=1, keepdims=True)
        delta_ref[...] = jnp.where(lane_s < HEAD_DIM, d_a, d_b)
        dq_acc[...] = jnp.zeros_like(dq_acc)
        df_ref[...] = jnp.zeros_like(df_ref)
        lane = lax.broadcasted_iota(jnp.int32, (blk, LANES), 1)
        own = [lane < HEAD_DIM, lane >= HEAD_DIM]
        tri = lax.broadcasted_iota(jnp.int32, (blk, blk), 0) >= lax.broadcasted_iota(jnp.int32, (blk, blk), 1)

        def q_slab(qoff, h):
            return q_ref[pl.ds(qoff, blk), h * LANES:(h + 1) * LANES]

        def probs(qoff, h, k_h, masked):
            s = _nt(q_slab(qoff, h), k_h)
            if masked:
                s = jnp.where(tri, s, NEG)
            col = slice(h * HEAD_DIM, h * HEAD_DIM + 1)
            weights = jnp.exp(s - max_ref[0, pl.ds(qoff, blk), col]).astype(bf16).astype(f32)
            return weights * inv_ref[pl.ds(qoff, blk), col]

        def k_slabs(koff):
            return [k_ref[pl.ds(koff, blk), h * LANES:(h + 1) * LANES] for h in range(2)]

        def kv_step(kj, _):
            koff = pl.multiple_of(kj * blk, blk)
            k_aug = k_slabs(koff)
            k_own = [jnp.where(own[h], k_aug[h], jnp.zeros_like(k_aug[h])) for h in range(2)]
            vv = v_ref[pl.ds(koff, blk), :]
            v_own = [jnp.where(own[h], vv, jnp.zeros_like(vv)) for h in range(2)]

            def q_tile(qi, carry, masked):
                qoff = pl.multiple_of(qi * blk, blk)
                dd = do_ref[pl.ds(qoff, blk), :].astype(bf16)
                new, dq_add = [], None
                for h in range(2):
                    dk_h, dv_h, dcol = carry[h]
                    p = probs(qoff, h, k_aug[h], masked)
                    dl = p * (_nt(dd, v_own[h]) - delta_ref[pl.ds(qoff, blk), h * HEAD_DIM:h * HEAD_DIM + 1])
                    dlb = dl.astype(bf16)
                    part = jnp.dot(dlb, k_own[h], preferred_element_type=f32)
                    dq_add = part if dq_add is None else dq_add + part
                    new.append((dk_h + _tn(dlb, q_slab(qoff, h)), dv_h + _tn(p.astype(bf16), dd),
                                dcol + _colsum(dl)))
                dq_acc[pl.ds(qoff, blk), :] += dq_add * ATT_SCALE
                return tuple(new)

            zero = (jnp.zeros((blk, LANES), f32), jnp.zeros((blk, LANES), f32), jnp.zeros((1, blk), f32))
            carry = q_tile(kj, (zero, zero), True)
            (dk_a, dv_a, dcol_a), (dk_b, dv_b, dcol_b) = lax.fori_loop(
                kj + 1, nblk, lambda qi, cr: q_tile(qi, cr, False), carry)
            dk_ref[pl.ds(koff, blk), :] = jnp.where(own[0], dk_a, dk_b).astype(dk_ref.dtype)
            dv_ref[pl.ds(koff, blk), :] = jnp.where(own[0], dv_a, dv_b).astype(dv_ref.dtype)
            df_ref[0, 0:1, pl.ds(koff, blk)] = -dcol_a
            df_ref[0, 1:2, pl.ds(koff, blk)] = -dcol_b
            return 0

        lax.fori_loop(0, nblk, kv_step, 0)
        dq_ref[...] = dq_acc[...].astype(dq_ref.dtype)

    pair_aug = pl.BlockSpec((SEQ, 2 * LANES), lambda p: (0, p))
    slab = pl.BlockSpec((SEQ, LANES), lambda p: (0, p))
    per_pair = pl.BlockSpec((1, SEQ, LANES), lambda p: (p, 0, 0))
    rows = pl.BlockSpec((1, 8, SEQ), lambda p: (p, 0, 0))
    return pl.pallas_call(
        body, grid=(npair,),
        in_specs=[pair_aug, pair_aug, slab, slab, slab, per_pair, per_pair, pl.BlockSpec(memory_space=pl.ANY)],
        out_specs=[slab, slab, slab, rows],
        out_shape=(SDS((SEQ, FOX_W), bf16),) * 3 + (SDS((npair, 8, SEQ), f32),), name="fox_bwd",
        scratch_shapes=[pltpu.VMEM((SEQ, LANES), f32)] * 3,
        compiler_params=_params(("parallel",)),
    )(q_aug, k_aug, v, do, o, row_max, row_sum, after)


DIL_BLK = 128
DILATIONS = (1, 4, 16)
N_GROUPS = len(DILATIONS)
DIL_PAIRS = DIL_OUT_W // LANES


def _dil_blocks(d):
    r1 = lax.broadcasted_iota(jnp.int32, (2 * DIL_BLK, DIL_BLK), 0) & (DIL_BLK - 1)
    c1 = lax.broadcasted_iota(jnp.int32, (2 * DIL_BLK, DIL_BLK), 1)
    r2 = lax.broadcasted_iota(jnp.int32, (2 * DIL_BLK, 2 * DIL_BLK), 0) & (DIL_BLK - 1)
    c2 = lax.broadcasted_iota(jnp.int32, (2 * DIL_BLK, 2 * DIL_BLK), 1)
    band = ((c2 < DIL_BLK) & (c2 >= r2)) | ((c2 >= DIL_BLK) & (c2 - DIL_BLK <= r2))
    out = []
    for r in range(d):
        for b in range(SEQ // d // DIL_BLK):
            rows = pl.ds(r + d * DIL_BLK * b, DIL_BLK, stride=d)
            if b == 0:
                out.append((rows, rows, r1 >= c1))
            else:
                out.append((rows, pl.ds(r + d * DIL_BLK * (b - 1), 2 * DIL_BLK, stride=d), band))
    return out


def _dil_v_spec(g):
    return pl.BlockSpec((SEQ, LANES), lambda p: (0, C_VB // LANES + DIL_PAIRS * g + p))


def _stack_heads(t, first):
    zero = jnp.zeros_like(t)
    return jnp.concatenate([jnp.where(first, t, zero), jnp.where(first, zero, t)], axis=0)


def _dil_fwd(q, k, v, g):
    def body(q_ref, k_ref, v_ref, o_ref, lse_ref):
        first = lax.broadcasted_iota(jnp.int32, (DIL_BLK, LANES), 1) < HEAD_DIM
        for rows, krows, mask in _dil_blocks(DILATIONS[g]):
            qv, kk, vv = q_ref[rows, :].astype(bf16), k_ref[krows, :].astype(bf16), v_ref[krows, :].astype(bf16)
            s = jnp.where(mask, _nt(_stack_heads(qv, first), kk), NEG)
            m = jnp.max(s, axis=1, keepdims=True)
            p = jnp.exp(s - m)
            l = jnp.sum(p, axis=1, keepdims=True)
            out = jnp.dot(p.astype(bf16), vv, preferred_element_type=f32) / l
            lse = m + jnp.log(l)
            o_ref[rows, :] = jnp.where(first, out[:DIL_BLK], out[DIL_BLK:])
            lse_ref[rows, :] = jnp.where(first, lse[:DIL_BLK], lse[DIL_BLK:])

    grouped = pl.BlockSpec((SEQ, LANES), lambda p: (0, DIL_PAIRS * g + p))
    own = pl.BlockSpec((SEQ, LANES), lambda p: (0, p))
    shape = SDS((SEQ, DIL_OUT_W), f32)
    return pl.pallas_call(
        body, grid=(DIL_PAIRS,), in_specs=[grouped, grouped, _dil_v_spec(g)], out_specs=[own] * 2,
        out_shape=(shape, shape),
        name=f"dil_fwd_{DILATIONS[g]}", compiler_params=_params(("parallel",)),
    )(q, k, v)


def _dil_bwd(q, k, v, do, lse, delta, g):
    def body(q_ref, k_ref, v_ref, do_ref, lse_ref, dl_ref, dq_ref, dk_ref, dv_ref):
        first = lax.broadcasted_iota(jnp.int32, (DIL_BLK, LANES), 1) < HEAD_DIM
        dk_ref[...] = jnp.zeros_like(dk_ref)
        dv_ref[...] = jnp.zeros_like(dv_ref)
        for rows, krows, mask in _dil_blocks(DILATIONS[g]):
            qv, kk, vv = q_ref[rows, :].astype(bf16), k_ref[krows, :].astype(bf16), v_ref[krows, :].astype(bf16)
            lsev, delv = lse_ref[rows, :], dl_ref[rows, :]
            q2 = _stack_heads(qv, first)
            do2 = _stack_heads(do_ref[rows, :].astype(bf16), first)
            per_head = lambda t: jnp.concatenate([t[:, 0:1], t[:, HEAD_DIM:HEAD_DIM + 1]], axis=0)
            p = jnp.exp(jnp.where(mask, _nt(q2, kk), NEG) - per_head(lsev))
            dl = (p * (_nt(do2, vv) - per_head(delv))).astype(bf16)
            dq = jnp.dot(dl, kk, preferred_element_type=f32)
            dq_ref[rows, :] = jnp.where(first, dq[:DIL_BLK], dq[DIL_BLK:]) * ATT_SCALE
            dk_ref[krows, :] += _tn(dl, q2)
            dv_ref[krows, :] += _tn(p.astype(bf16), do2)

    grouped = pl.BlockSpec((SEQ, LANES), lambda p: (0, DIL_PAIRS * g + p))
    own = pl.BlockSpec((SEQ, LANES), lambda p: (0, p))
    shape = SDS((SEQ, DIL_OUT_W), f32)
    return pl.pallas_call(
        body, grid=(DIL_PAIRS,), in_specs=[grouped, grouped, _dil_v_spec(g)] + [own] * 3, out_specs=[own] * 3,
        out_shape=(shape, shape, shape), name=f"dil_bwd_{DILATIONS[g]}", compiler_params=_params(("parallel",)),
    )(q, k, v, do, lse, delta)


def _position():
    return lax.axis_index("x"), lax.axis_index("y"), lax.axis_index("c")


def _all_gather(block, name, after=None):
    after = [] if after is None else [after]

    def body(x_ref, *refs):
        out_ref, send_sems, recv_sems, local_sem = refs[len(after):]
        x, y, c = _position()
        me, sibling = (x, y, c), (x, y, 1 - c)
        chips = [(1 - x, y), (x, 1 - y), (1 - x, 1 - y)]

        def slot(px, py, pc):
            return out_ref.at[4 * px + 2 * py + pc]

        def copy(k, blk, to, src=None):
            return pltpu.make_async_remote_copy(
                src_ref=slot(*blk) if src is None else src, dst_ref=slot(*blk),
                send_sem=send_sems.at[k], recv_sem=recv_sems.at[k], device_id=to, device_id_type=MESH)

        mine = pltpu.make_async_copy(x_ref, slot(*me), local_sem)
        mine.start()
        first = [copy(0, me, sibling, src=x_ref)]
        first += [copy(1 + j, me, (*chip, c), src=x_ref) for j, chip in enumerate(chips)]
        for cp in first:
            cp.start()
        passed = [copy(4 + j, (*chip, c), sibling) for j, chip in enumerate(chips)]
        for j, chip in enumerate(chips):
            copy(1 + j, (*chip, c), me).wait_recv()
            passed[j].start()
        copy(0, sibling, me).wait_recv()
        for j, chip in enumerate(chips):
            copy(4 + j, (*chip, 1 - c), me).wait_recv()
        for cp in first + passed:
            cp.wait_send()
        mine.wait()

    return pl.pallas_call(
        body, out_shape=SDS((N_DEV,) + block.shape, block.dtype),
        in_specs=[pl.BlockSpec(memory_space=pl.ANY)] * (1 + len(after)), out_specs=pl.BlockSpec(memory_space=pl.ANY),
        scratch_shapes=[pltpu.SemaphoreType.DMA((7,)), pltpu.SemaphoreType.DMA((7,)), pltpu.SemaphoreType.DMA],
        name=name,
    )(block, *after)


HBM_SPEC = pl.BlockSpec(memory_space=pltpu.HBM)
SEM_SPEC = pl.BlockSpec(memory_space=pltpu.SEMAPHORE)
SPLIT_COPY = pltpu.CompilerParams(has_side_effects=pltpu.SideEffectType.DATAFLOW_SIDE_EFFECTING)


def _in_hbm(t):
    return pltpu.with_memory_space_constraint(t, pltpu.HBM)


def _pair_copies(g_refs, land_refs, send_sems, recv_sems):
    x, y, c = _position()
    return [pltpu.make_async_remote_copy(
        src_ref=g.at[2 * k + (1 - c)], dst_ref=land.at[k], send_sem=send_sems.at[4 * a + k],
        recv_sem=recv_sems.at[4 * a + k], device_id=(x, y, 1 - c), device_id_type=MESH)
        for a, (g, land) in enumerate(zip(g_refs, land_refs, strict=True)) for k in range(4)]


def _chip_copies(t_refs, land_refs, send_sems, recv_sems):
    x, y, c = _position()
    chips = [(1 - x, y), (x, 1 - y), (1 - x, 1 - y)]
    return [pltpu.make_async_remote_copy(
        src_ref=t.at[2 * px + py], dst_ref=land.at[j], send_sem=send_sems.at[3 * a + j],
        recv_sem=recv_sems.at[3 * a + j], device_id=(px, py, c), device_id_type=MESH)
        for a, (t, land) in enumerate(zip(t_refs, land_refs, strict=True)) for j, (px, py) in enumerate(chips)]


_ROUNDS = {"pair": (_pair_copies, 4), "chip": (_chip_copies, 3)}


def _exchange_start(kind, ts, name):
    copies, slots = _ROUNDS[kind]
    n = len(ts)
    lands = [_in_hbm(lax.empty((slots,) + t.shape[1:], t.dtype)) for t in ts]

    def body(*refs):
        for cp in copies(refs[:n], refs[n:2 * n], refs[2 * n], refs[2 * n + 1]):
            cp.start()
        refs[-1][...] = jnp.zeros_like(refs[-1])

    sems = pltpu.SemaphoreType.DMA((slots * n,))
    res = pl.pallas_call(
        body, name=name, in_specs=[HBM_SPEC] * (2 * n),
        out_shape=(sems, sems, *[pltpu.HBM(t.shape, t.dtype) for t in (*ts, *lands)], SDS((8, LANES), f32)),
        out_specs=(SEM_SPEC, SEM_SPEC, *[HBM_SPEC] * (2 * n), pl.BlockSpec(memory_space=pltpu.VMEM)),
        input_output_aliases={i: 2 + i for i in range(2 * n)}, compiler_params=SPLIT_COPY,
    )(*[_in_hbm(t) for t in ts], *lands)
    return res[:-1], res[-1]


def _exchange_wait(kind, state, after, name):
    copies, _ = _ROUNDS[kind]
    send_sems, recv_sems, *arrays = state
    n = len(arrays) // 2

    def body(*refs):
        for cp in copies(refs[:n], refs[n:2 * n], refs[2 * n], refs[2 * n + 1]):
            cp.wait_send()
            cp.wait_recv()

    res = pl.pallas_call(
        body, name=name, in_specs=[HBM_SPEC] * (2 * n) + [SEM_SPEC, SEM_SPEC, pl.BlockSpec(memory_space=pl.ANY)],
        out_shape=[pltpu.HBM(t.shape, t.dtype) for t in arrays], out_specs=[HBM_SPEC] * (2 * n),
        input_output_aliases={i: i for i in range(2 * n)}, compiler_params=SPLIT_COPY,
    )(*arrays, send_sems, recv_sems, after)
    return res[:n], res[n:]


def _gather_copies(x_refs, out_refs, send_sems, recv_sems):
    x, y, c = _position()
    peers = [(x, y, 1 - c), (1 - x, y, c), (x, 1 - y, c), (1 - x, 1 - y, c)]
    sends, arrivals = [], []
    for a, (x_ref, out_ref) in enumerate(zip(x_refs, out_refs, strict=True)):
        for k, (px, py, pc) in enumerate(peers):
            sems = dict(send_sem=send_sems.at[4 * a + k], recv_sem=recv_sems.at[4 * a + k],
                        device_id=(px, py, pc), device_id_type=MESH)
            sends.append(pltpu.make_async_remote_copy(src_ref=x_ref, dst_ref=out_ref.at[4 * x + 2 * y + c], **sems))
            arrivals.append(pltpu.make_async_remote_copy(src_ref=x_ref, dst_ref=out_ref.at[4 * px + 2 * py + pc],
                                                         **sems))
    return sends, arrivals


def _gather_start(blocks, after, name):
    n = len(blocks)
    outs = [_in_hbm(lax.empty((N_DEV,) + b.shape, b.dtype)) for b in blocks]

    def body(*refs):
        sends, _ = _gather_copies(refs[:n], refs[n:2 * n], refs[2 * n + 1], refs[2 * n + 2])
        for cp in sends:
            cp.start()
        refs[-1][...] = jnp.zeros_like(refs[-1])

    sems = pltpu.SemaphoreType.DMA((4 * n,))
    res = pl.pallas_call(
        body, name=name, in_specs=[HBM_SPEC] * (2 * n) + [pl.BlockSpec(memory_space=pl.ANY)],
        out_shape=(sems, sems, *[pltpu.HBM(t.shape, t.dtype) for t in (*blocks, *outs)], SDS((8, LANES), f32)),
        out_specs=(SEM_SPEC, SEM_SPEC, *[HBM_SPEC] * (2 * n), pl.BlockSpec(memory_space=pltpu.VMEM)),
        input_output_aliases={i: 2 + i for i in range(2 * n)}, compiler_params=SPLIT_COPY,
    )(*[_in_hbm(b) for b in blocks], *outs, after)
    return res[:-1], res[-1]


def _gather_wait(state, after, name):
    send_sems, recv_sems, *arrays = state
    n = len(arrays) // 2

    def body(*refs):
        sends, arrivals = _gather_copies(refs[:n], refs[n:2 * n], refs[2 * n], refs[2 * n + 1])
        for cp in sends:
            cp.wait_send()
        for cp in arrivals:
            cp.wait_recv()

    res = pl.pallas_call(
        body, name=name, in_specs=[HBM_SPEC] * (2 * n) + [SEM_SPEC, SEM_SPEC, pl.BlockSpec(memory_space=pl.ANY)],
        out_shape=[pltpu.HBM(t.shape, t.dtype) for t in arrays], out_specs=[HBM_SPEC] * (2 * n),
        input_output_aliases={i: i for i in range(2 * n)}, compiler_params=SPLIT_COPY,
    )(*arrays, send_sems, recv_sems, after)
    return res[:n], res[n:]


def _gather_finish(partial, name):
    n = len(partial)

    def body(*refs):
        in_refs, out_refs = refs[:n], refs[n:2 * n]
        send_sems, recv_sems = refs[2 * n:]
        x, y, c = _position()
        chips = [(1 - x, y), (x, 1 - y), (1 - x, 1 - y)]
        copies = []
        for a in range(n):
            for j, (px, py) in enumerate(chips):
                cp = pltpu.make_async_remote_copy(
                    src_ref=in_refs[a].at[4 * px + 2 * py + c], dst_ref=out_refs[a].at[4 * px + 2 * py + c],
                    send_sem=send_sems.at[a, j], recv_sem=recv_sems.at[a, j], device_id=(x, y, 1 - c),
                    device_id_type=MESH)
                cp.start()
                copies.append(cp)
        for a in range(n):
            for j, (px, py) in enumerate(chips):
                pltpu.make_async_remote_copy(
                    src_ref=in_refs[a].at[4 * px + 2 * py + (1 - c)], dst_ref=out_refs[a].at[4 * px + 2 * py + (1 - c)],
                    send_sem=send_sems.at[a, j], recv_sem=recv_sems.at[a, j], device_id=(x, y, 1 - c),
                    device_id_type=MESH).wait_recv()
        for cp in copies:
            cp.wait_send()

    hbm = pl.BlockSpec(memory_space=pl.ANY)
    return pl.pallas_call(
        body, out_shape=[SDS(p.shape, p.dtype) for p in partial], in_specs=[hbm] * n, out_specs=[hbm] * n,
        input_output_aliases={a: a for a in range(n)},
        scratch_shapes=[pltpu.SemaphoreType.DMA((n, 3)), pltpu.SemaphoreType.DMA((n, 3))],
        name=name,
    )(*partial)


def _row_tile(rows):
    return 512 if rows % 512 == 0 and rows > 512 else rows


def _pair_add(g, r1, core, name):
    def body(c_ref, g_ref, r_ref, o_ref):
        o_ref[...] = (g_ref[...].astype(f32) + r_ref[...].astype(f32)).astype(o_ref.dtype)

    rows, cols = g.shape[1:]
    tile = _row_tile(rows)
    blk = (1, tile, cols)
    return pl.pallas_call(
        body, out_shape=SDS((4, rows, cols), g.dtype), name=name,
        grid_spec=pltpu.PrefetchScalarGridSpec(
            num_scalar_prefetch=1, grid=(4, rows // tile),
            in_specs=[pl.BlockSpec(blk, lambda k, i, c_ref: (2 * k + c_ref[0], i, 0)),
                      pl.BlockSpec(blk, lambda k, i, c_ref: (k, i, 0))],
            out_specs=pl.BlockSpec(blk, lambda k, i, c_ref: (k, i, 0))),
        compiler_params=_params(("parallel", "arbitrary")),
    )(core, g, r1)


def _chip_add(t, r2, chip, name, planes=None, pieces=None, after=None):
    after = [] if after is None else [after]

    def body(c_ref, t_ref, r_ref, *refs):
        o_refs = refs[len(after):]
        s = ((t_ref[0].astype(f32) + r_ref[0].astype(f32)) + r_ref[1].astype(f32)) + r_ref[2].astype(f32)
        if planes:
            o_refs[0][...] = s.T[:planes][:, None, :]
        elif pieces:
            for o_ref, (start, size) in zip(o_refs, pieces, strict=True):
                o_ref[...] = s.T[start:start + size]
        else:
            o_refs[0][...] = s

    rows, cols = t.shape[1:]
    tile = _row_tile(rows)
    if planes:
        out_shape, out_spec = SDS((planes, 1, rows), f32), pl.BlockSpec((planes, 1, tile), lambda i, c_ref: (0, 0, i))
    elif pieces:
        out_shape = [SDS((size, rows), f32) for _, size in pieces]
        out_spec = [pl.BlockSpec((size, tile), lambda i, c_ref: (0, i)) for _, size in pieces]
    else:
        out_shape, out_spec = SDS((rows, cols), f32), pl.BlockSpec((tile, cols), lambda i, c_ref: (i, 0))
    return pl.pallas_call(
        body, out_shape=out_shape, name=name,
        grid_spec=pltpu.PrefetchScalarGridSpec(
            num_scalar_prefetch=1, grid=(rows // tile,),
            in_specs=[pl.BlockSpec((1, tile, cols), lambda i, c_ref: (c_ref[0], i, 0)),
                      pl.BlockSpec((3, tile, cols), lambda i, c_ref: (0, i, 0))]
            + [pl.BlockSpec(memory_space=pl.ANY)] * len(after),
            out_specs=out_spec),
        compiler_params=_params(("arbitrary",)),
    )(chip, t, r2, *after)


def _pad_to(t, axis, size):
    pads = [(0, 0)] * t.ndim
    pads[axis] = (0, size - t.shape[axis])
    return jnp.pad(t, pads)


_REF_COLS = {"qa": (0, FOX_W), "ka": (FOX_W, FOX_W), "va": (2 * FOX_W, FOX_W), "f": (3 * FOX_W, N_FOX_HEADS)}
_REF_COLS.update({n: (3 * FOX_W + N_FOX_HEADS + i * DIL_W, DIL_W) for i, n in enumerate(("qb", "kb", "vb"))})
_REF_COLS.update({n: (3 * FOX_W + N_FOX_HEADS + 3 * DIL_W + i * D, D) for i, n in enumerate(("ga", "gb"))})
_REF_ORDER = ("qa", "ka", "va", "f", "qb", "kb", "vb", "ga", "gb")


def _place_cols(sources, src_of, out_cols, name, row_block=512):
    arrays = [s[0] if isinstance(s, tuple) else s for s in sources]
    widths = [a.shape[-1] for a in arrays]
    rows = arrays[0].shape[-2]
    plan = []
    for t in range(out_cols // LANES):
        segs, c, end = [], t * LANES, (t + 1) * LANES
        while c < end:
            s = src_of(c)
            if s is None:
                c += 1
                continue
            n = 1
            while c + n < end and src_of(c + n) == (s[0], s[1] + n):
                n += 1
            segs.append((s[0], s[1], c - t * LANES, n))
            c += n
        plan.append(segs)

    def body(*refs):
        o_ref = refs[-1]
        for t, segs in enumerate(plan):
            acc = None
            for si, c0, o0, n in segs:
                a0 = c0 // LANES * LANES
                wide = min(2 * LANES, widths[si] - a0)
                win = refs[si][0, :, a0:a0 + wide] if isinstance(sources[si], tuple) else refs[si][:, a0:a0 + wide]
                r = lax.broadcasted_iota(jnp.int32, (wide, LANES), 0)
                c = lax.broadcasted_iota(jnp.int32, (wide, LANES), 1)
                pick = ((r - (c0 - a0) == c - o0) & (c >= o0) & (c < o0 + n)).astype(bf16)
                part = jnp.dot(win.astype(bf16), pick, preferred_element_type=f32)
                acc = part if acc is None else acc + part
            tile = jnp.zeros((row_block, LANES), f32) if acc is None else acc
            o_ref[:, t * LANES:(t + 1) * LANES] = tile.astype(o_ref.dtype)

    def spec(s):
        if isinstance(s, tuple):
            j = s[1]
            return pl.BlockSpec((1, row_block, s[0].shape[-1]), lambda i: (j, i, 0))
        return pl.BlockSpec((row_block, s.shape[-1]), lambda i: (i, 0))

    return pl.pallas_call(
        body, grid=(rows // row_block,), in_specs=[spec(s) for s in sources],
        out_specs=pl.BlockSpec((row_block, out_cols), lambda i: (i, 0)), out_shape=SDS((rows, out_cols), bf16),
        name=name, compiler_params=_params(("parallel",)),
    )(*arrays)


def _ref_piece(r):
    for name in _REF_ORDER:
        lo, width = _REF_COLS[name]
        if lo <= r < lo + width:
            return name, r - lo
    raise ValueError(r)


def _shard_pad_cols(pieces):
    names = [n for n in _REF_ORDER if n != "vb"]
    sources = [pieces[n] for n in names] + list(pieces["vb"])

    def src_of(c):
        j, i = divmod(c, W_IN_PAD)
        if i >= W_IN_SH:
            return None
        name, col = _ref_piece(j * W_IN_SH + i)
        if name == "vb":
            return len(names) + col // DIL_OUT_W, col % DIL_OUT_W
        return names.index(name), col

    return _place_cols(sources, src_of, N_DEV * W_IN_PAD, "place_dproj")


_SLABS = {"ga": C_GA, "gb": C_GB, "qb": C_QB, "kb": C_KB, "vb": C_VB, "qa": C_QA, "ka": C_KA, "va": C_VA, "f": C_F}


def _slab_w_in(stack):
    def src_of(c):
        for name, start in _SLABS.items():
            lo, width = _REF_COLS[name]
            if start <= c < start + width:
                return divmod(lo + c - start, W_IN_SH)
        return None

    return _place_cols([(stack, j) for j in range(N_DEV)], src_of, PROJ_W, "place_w_in")


def kernel(x, c, w_ada, b_ada, g_mix, w_in, b_fgate, w_br_a, w_br_b, w_out, g_ffn, w_ffn_gate, w_ffn_up, w_ffn_down, g_final, loss_target, m_w_ada, m_b_ada, m_g_mix, m_w_in, m_b_fgate, m_w_br_a, m_w_br_b, m_w_out, m_g_ffn, m_w_ffn_gate, m_w_ffn_up, m_w_ffn_down, m_g_final, v_w_ada, v_b_ada, v_g_mix, v_w_in, v_b_fgate, v_w_br_a, v_w_br_b, v_w_out, v_g_ffn, v_w_ffn_gate, v_w_ffn_up, v_w_ffn_down, v_g_final):
    px, py, pc = _position()
    dev = 4 * px + 2 * py + pc
    x2d, tgt = x[0], loss_target[0]

    c_all = _all_gather(c, "gather_c").reshape(N_DEV, D)
    ada_cols = w_ada.shape[2]
    b_shard = lax.dynamic_slice(b_ada, (0, dev * ada_cols), (1, ada_cols))
    mod_shard = _ada_fwd(c_all, w_ada[0], b_shard)
    mod_all = _all_gather(mod_shard, "gather_mod")
    modv = lax.dynamic_index_in_dim(mod_all, dev, axis=1, keepdims=False).reshape(6, D)
    h1 = _pre1(x2d, modv, g_mix)

    w_in_s = _all_gather(_pad_to(w_in[0], 1, W_IN_PAD).astype(bf16), "gather_w_in")
    gate_up = jnp.concatenate([_pad_to(w_ffn_gate[0], 1, FF_PAD), _pad_to(w_ffn_up[0], 1, FF_PAD)], axis=1)
    later = [w_br_a[0], w_br_b[0], w_out[0], gate_up, _pad_to(w_ffn_down[0], 0, FF_PAD)]
    later_state, later_token = _gather_start([t.astype(bf16) for t in later], w_in_s, "gather_rest_start")
    w_in_p = _slab_w_in(w_in_s)

    proj = _matmul(h1, w_in_p, name="mm_proj", tm=SEQ, tn=896, after=later_token)
    b_pad = jnp.pad(b_fgate, ((0, 0), (0, LANES - N_FOX_HEADS)))
    q_aug, k_aug, va = _fox_prep(proj, _fox_gate_fwd(proj, b_pad))
    ya_h, max_a, sum_a = _fox_fwd(q_aug, k_aug, va)

    tables = _rope_tables()
    qb_r, kb_r = _rope_fwd(proj, tables)
    by_group = [_dil_fwd(qb_r, kb_r, proj, grp) for grp in range(N_GROUPS)]
    yb_h, lse_b = _dil_combine([o for o, _ in by_group], [l for _, l in by_group])

    both_done = ya_h[:8, :LANES] + yb_h[:8, :LANES]
    mine, arrived = _gather_wait(later_state, both_done, "gather_rest_wait")
    w_a_s, w_b_s, w_o_s, w_gu_s, w_d_s = [
        lax.dynamic_update_slice(stack, block[None], (dev, 0, 0))
        for stack, block in zip(_gather_finish(arrived, "gather_rest_finish"), mine, strict=True)]
    w_o = w_o_s.reshape(D, D)
    w_d = w_d_s.reshape(FF_HID, D)
    ya = _matmul_stack(ya_h, w_a_s, name="mm_br_a")
    yb = _matmul_stack(yb_h, w_b_s, name="mm_br_b")

    merged, mix, x1, h2 = _post1(ya, yb, proj, w_o, x2d, modv, g_ffn)
    act, au = _ffn_in(h2, w_gu_s)

    dx2, dff, dg_final, dga_f, loss_lanes = _final(act, w_d, x1, tgt, modv, g_final.reshape(1, D))
    dau = _ffn_bwd_in(dff, w_d_s, au)

    core = pc.astype(jnp.int32).reshape(1)
    chip = (2 * px + py).astype(jnp.int32).reshape(1)

    def pair_done(state, after, tags, name):
        mine, theirs = _exchange_wait("pair", state, after, "pair_wait_" + name)
        sums = [_pair_add(g, r, core, "pair_add_" + t) for g, r, t in zip(mine, theirs, tags)]
        return _exchange_start("chip", sums, "chip_start_" + name)

    def from_chips(state, after, tags, name):
        sums, got = _exchange_wait("chip", state, after, "chip_wait_" + name)
        return [_chip_add(p, r, chip, "chip_add_" + t) for p, r, t in zip(sums, got, tags)]

    g_gu = _matmul(h2, dau, ta=True, by_shard=True, out_dtype=bf16, name="mm_g_ffn_in", tm=D, tn=2 * FF_PAD)
    g_d = _matmul(act, dff, ta=True, out_dtype=bf16, name="mm_g_down", tm=FF_HID // 2, tn=512)
    ffn_tags = ["gu", "down"]
    ffn_pair, ffn_pair_token = _exchange_start("pair", [g_gu, g_d.reshape(N_DEV, FF_PAD, D)], "pair_start_ffn")

    dx1, dmix, dsh_f, dsc_f, dg_ffn, dga_m = _mid_bwd(dau, w_gu_s, ffn_pair_token, x1, dx2, mix, modv, g_ffn)
    ffn_state, ffn_token = pair_done(ffn_pair, dx1, ffn_tags, "ffn")
    dya, dyb, dga, dgb = _merge_bwd(dmix, w_o, ffn_token, ya, yb, proj)
    dya_h = _matmul_stack(dya, w_a_s, tb=True, name="mm_d_ya")
    dyb_h = _matmul_stack(dyb, w_b_s, tb=True, name="mm_d_yb")

    g_o = _matmul(merged, dmix, ta=True, out_dtype=bf16, name="mm_g_out", tm=D, tn=512)
    g_a = _matmul_stack(ya_h, dya, ta=True, out_dtype=bf16, name="mm_g_br_a")
    g_b = _matmul_stack(yb_h, dyb, ta=True, out_dtype=bf16, name="mm_g_br_b")
    rows_a, rows_b = FOX_W * W_BR_SH // D, DIL_OUT_W * W_BR_SH // D
    g_small = jnp.concatenate([g_a.reshape(N_DEV, rows_a, D), g_b.reshape(N_DEV, rows_b, D),
                               g_o.reshape(N_DEV, W_BR_SH, D)], axis=1)
    small_pair, small_pair_token = _exchange_start("pair", [g_small], "pair_start_small")

    dqa, dka, dva, dF = _fox_bwd(q_aug, k_aug, va, dya_h, ya_h, max_a, sum_a, small_pair_token)
    dF_row = jnp.pad(dF[:, :2, :].reshape(N_FOX_HEADS, SEQ), ((0, LANES - N_FOX_HEADS), (0, 0)))
    df, db_fgate = _fox_gate_bwd(dF_row, proj, b_pad)
    small_state, small_token = pair_done(small_pair, df, ["small"], "small")

    delta_b = _dil_delta(dyb_h, yb_h)
    dil_grads = [_dil_bwd(qb_r, kb_r, proj, dyb_h, lse_b, delta_b, grp) for grp in range(N_GROUPS)]
    dqb, dkb = _rope_bwd([t[0] for t in dil_grads], [t[1] for t in dil_grads], tables)

    dproj = _shard_pad_cols({"qa": dqa, "ka": dka, "va": dva, "f": df, "qb": dqb, "kb": dkb,
                             "vb": [t[2] for t in dil_grads], "ga": dga, "gb": dgb})
    g_in = _matmul(h1, dproj, ta=True, by_shard=True, out_dtype=bf16, name="mm_g_in", tm=D, tn=W_IN_PAD,
                   after=small_token)
    mix_tags = ["in"]
    mix_pair, mix_pair_token = _exchange_start("pair", [g_in], "pair_start_mixer")

    w = {"w_ada": w_ada, "b_ada": b_ada, "g_mix": g_mix, "w_in": w_in, "b_fgate": b_fgate, "w_br_a": w_br_a,
         "w_br_b": w_br_b, "w_out": w_out, "g_ffn": g_ffn, "w_ffn_gate": w_ffn_gate, "w_ffn_up": w_ffn_up,
         "w_ffn_down": w_ffn_down, "g_final": g_final}
    m = {"w_ada": m_w_ada, "b_ada": m_b_ada, "g_mix": m_g_mix, "w_in": m_w_in, "b_fgate": m_b_fgate,
         "w_br_a": m_w_br_a, "w_br_b": m_w_br_b, "w_out": m_w_out, "g_ffn": m_g_ffn, "w_ffn_gate": m_w_ffn_gate,
         "w_ffn_up": m_w_ffn_up, "w_ffn_down": m_w_ffn_down, "g_final": m_g_final}
    v = {"w_ada": v_w_ada, "b_ada": v_b_ada, "g_mix": v_g_mix, "w_in": v_w_in, "b_fgate": v_b_fgate,
         "w_br_a": v_w_br_a, "w_br_b": v_w_br_b, "w_out": v_w_out, "g_ffn": v_g_ffn, "w_ffn_gate": v_w_ffn_gate,
         "w_ffn_up": v_w_ffn_up, "w_ffn_down": v_w_ffn_down, "g_final": v_g_final}
    names = list(w)
    g, delta, new_m, new_v = {}, {}, {}, {}

    transposed = ("w_ffn_gate", "w_ffn_up")
    by_column = lambda t: jnp.transpose(t, (2, 0, 1))
    by_row = lambda t: jnp.transpose(t, (1, 2, 0))

    def update(n):
        shape = w[n].shape
        if n == "w_in":
            g3 = g[n]
            dl, mn, vn = _adamw_by_planes(by_column(w[n]), g3, by_column(m[n]), by_column(v[n]), "adamw_" + n)
            g[n], delta[n], new_m[n], new_v[n] = by_row(g3), by_row(dl), by_row(mn), by_row(vn)
            return
        if n in transposed:
            g_t = g[n]
            dl, mn, vn = _adamw(w[n][0].T, g_t, m[n][0].T, v[n][0].T, "adamw_" + n)
            g[n], delta[n], new_m[n], new_v[n] = g_t.T[None], dl.T[None], mn.T[None], vn.T[None]
            return
        two_d = (lambda t: t.reshape(shape[-2:])) if len(shape) == 3 else (lambda t: t)
        dl, mn, vn = _adamw(two_d(w[n]), two_d(g[n]), two_d(m[n]), two_d(v[n]), "adamw_" + n)
        delta[n], new_m[n], new_v[n] = dl.reshape(shape), mn.reshape(shape), vn.reshape(shape)

    def update_all(grads):
        g.update(grads)
        for n in grads:
            update(n)
        return sum(delta[n][(0,) * (delta[n].ndim - 1)][:N_FOX_HEADS] for n in grads)

    ffn_sums, ffn_got = _exchange_wait("chip", ffn_state, mix_pair_token, "chip_wait_ffn")
    g_gate_t, g_up_t = _chip_add(ffn_sums[0], ffn_got[0], chip, "chip_add_gu",
                                 pieces=[(0, W_FF_SH), (FF_PAD, W_FF_SH)])
    gate_up_done = update_all({"w_ffn_gate": g_gate_t, "w_ffn_up": g_up_t})
    mix_state, mix_token = pair_done(mix_pair, gate_up_done, mix_tags, "mixer")

    grad_x, dsh_m, dsc_m, dg_mix = _first_bwd(dproj, w_in_s, mix_token, x2d, dx1, modv, g_mix)

    s_d = _chip_add(ffn_sums[1], ffn_got[1], chip, "chip_add_down", after=grad_x)
    s_small, = from_chips(small_state, grad_x, ["small"], "small")
    sharded_done = update_all({
        "w_ffn_down": s_d[None, :W_FF_SH],
        "w_br_a": s_small[:rows_a].reshape(1, FOX_W, W_BR_SH),
        "w_br_b": s_small[rows_a:rows_a + rows_b].reshape(1, DIL_OUT_W, W_BR_SH), "w_out": s_small[None, rows_a + rows_b:],
    })

    pad_lane = lambda t: jnp.pad(t, ((0, 0), (0, D - t.shape[1])))
    small = jnp.concatenate([dsh_m, dsc_m, dga_m, dsh_f, dsc_f, dga_f, dg_mix, dg_ffn, dg_final,
                             pad_lane(db_fgate), loss_lanes, jnp.zeros((SMALL_ROWS - 11, D), f32)], axis=0)
    small_all = _all_gather(small, "gather_small", after=sharded_done)
    small_sum, loss_row = _small_reduce(small_all, mix_token)
    loss = loss_row[0, 0]
    dmod_all = small_all[:, :6, :].reshape(N_DEV, 6 * D)
    g_w_ada = _ada_bwd(c_all, lax.dynamic_slice(dmod_all, (0, dev * ada_cols), (N_DEV, ada_cols)))
    done = update_all({
        "w_ada": g_w_ada[None], "b_ada": small_sum[0:6].reshape(1, 6 * D), "g_mix": small_sum[6:7],
        "b_fgate": small_sum[9:10, :N_FOX_HEADS], "g_ffn": small_sum[7:8], "g_final": small_sum[8],
    })
    mix_sums, mix_got = _exchange_wait("chip", mix_state, done, "chip_wait_mixer")
    g["w_in"] = _chip_add(mix_sums[0], mix_got[0], chip, "chip_add_in", planes=W_IN_SH)
    update("w_in")

    return (loss, grad_x[None], *[g[n] for n in names], *[delta[n] for n in names],
            *[new_m[n] for n in names], *[new_v[n] for n in names])
```

```python
import jax
import jax.numpy as jnp
import numpy as np
from jax import lax
from jax.experimental import pallas as pl
from jax.experimental.pallas import tpu as pltpu

f32 = jnp.float32
bf16 = jnp.bfloat16
SDS = jax.ShapeDtypeStruct
MESH = pl.DeviceIdType.MESH

N_DEV = 8
D = 1024
SEQ = 2048
HEAD_DIM = 64
N_FOX_HEADS = 8
FOX_W = 512
DIL_W = 768
DIL_OUT_W = 256
ROT_DIM = 16
ROPE_THETA = 500000.0
D_FF = 2816
IN_COLS = 5896
EPS = 1e-6
NEG = -1e30
ATT_SCALE = HEAD_DIM ** -0.5

ADAM_LR = 0.001
ADAM_B1 = 0.9
ADAM_B2 = 0.999
ADAM_EPS = 1e-08
ADAM_WD = 0.01
ADAM_STEP = 10

C_GA, C_GB, C_QB, C_KB, C_VB, C_QA, C_KA, C_VA, C_F = 0, 1024, 2304, 3072, 3840, 4608, 5120, 5632, 6144
PROJ_W = 6272
LANES = 128
VMEM_LIMIT = 52 * 1024 * 1024

W_IN_SH, W_IN_PAD = IN_COLS // N_DEV, 768
W_BR_SH = D // N_DEV
W_FF_SH, FF_PAD = D_FF // N_DEV, 384
FF_HID = N_DEV * FF_PAD
SMALL_ROWS = 16


def _params(sem=None):
    if sem is None:
        return pltpu.CompilerParams(vmem_limit_bytes=VMEM_LIMIT)
    return pltpu.CompilerParams(dimension_semantics=sem, vmem_limit_bytes=VMEM_LIMIT)


def _rowwise(fn, name, tiled, vecs, outs, reds=(), tile=256):
    nt, nv, no = len(tiled), len(vecs), len(outs)
    rows = tiled[0][0].shape[0]
    assert rows % tile == 0

    def body(*refs):
        tin = [r[...] for r in refs[:nt]]
        vin = [r[...] for r in refs[nt:nt + nv]]
        orefs = refs[nt + nv:nt + nv + no]
        rrefs = refs[nt + nv + no:]
        touts, routs = fn(tin, vin)
        for r, t in zip(orefs, touts, strict=True):
            r[...] = t.astype(r.dtype)
        if rrefs:
            @pl.when(pl.program_id(0) == 0)
            def _():
                for r in rrefs:
                    r[...] = jnp.zeros_like(r)
            for r, t in zip(rrefs, routs, strict=True):
                r[...] += t

    def col_map(cb):
        return lambda i: (i, cb)

    def whole_map(nd):
        return lambda i: (0,) * nd

    in_specs = [pl.BlockSpec((tile, w), col_map(cb)) for (_, w, cb) in tiled]
    in_specs += [pl.BlockSpec(v.shape, whole_map(v.ndim)) for v in vecs]
    out_specs = [pl.BlockSpec((tile, w), lambda i: (i, 0)) for (w, _) in outs]
    out_specs += [pl.BlockSpec((1, w), lambda i: (0, 0)) for w in reds]
    out_shape = [SDS((rows, w), dt) for (w, dt) in outs] + [SDS((1, w), f32) for w in reds]
    res = pl.pallas_call(
        body, grid=(rows // tile,), in_specs=in_specs, out_specs=out_specs, out_shape=out_shape, name=name,
        compiler_params=_params(("arbitrary",)),
    )(*[t[0] for t in tiled], *vecs)
    return res


def _matmul(a, b, *, ta=False, out_dtype=f32, name, tm, tn, by_shard=False, after=None):
    (m, k), n = ((a.shape[1], a.shape[0]) if ta else a.shape), b.shape[1]
    assert b.shape[0] == k and m % tm == 0 and n % tn == 0 and (ta or not by_shard)
    dims = (((0 if ta else 1,), (0,)), ((), ()))

    def body(a_ref, b_ref, *rest):
        p = lax.dot_general(a_ref[...].astype(bf16), b_ref[...].astype(bf16), dims, preferred_element_type=f32)
        o_ref = rest[-1]
        if by_shard:
            o_ref[0] = p.astype(o_ref.dtype)
        else:
            o_ref[...] = p.astype(o_ref.dtype)

    a_spec = pl.BlockSpec((k, tm), lambda i, j: (0, i)) if ta else pl.BlockSpec((tm, k), lambda i, j: (i, 0))
    if by_shard:
        assert tn == n // N_DEV
        out_spec, out_shape = pl.BlockSpec((1, tm, tn), lambda i, j: (j, i, 0)), SDS((N_DEV, m, tn), out_dtype)
    else:
        out_spec, out_shape = pl.BlockSpec((tm, tn), lambda i, j: (i, j)), SDS((m, n), out_dtype)
    extra_specs, extra = ([pl.BlockSpec(memory_space=pl.ANY)], [after]) if after is not None else ([], [])
    return pl.pallas_call(
        body, grid=(m // tm, n // tn), in_specs=[a_spec, pl.BlockSpec((k, tn), lambda i, j: (0, j))] + extra_specs,
        out_specs=out_spec, out_shape=out_shape, name=name, compiler_params=_params(("parallel", "parallel")),
    )(a, b, *extra)


def _matmul_stack(a, b, *, ta=False, tb=False, out_dtype=f32, name):
    def lanes(ref):
        return jnp.concatenate([ref[j] for j in range(N_DEV)], axis=1).astype(bf16)

    if ta:
        w = b.shape[1] // N_DEV

        def body(a_ref, b_ref, o_ref):
            p = _tn(a_ref[...].astype(bf16), b_ref[...].astype(bf16))
            for j in range(N_DEV):
                o_ref[j] = p[:, j * w:(j + 1) * w].astype(o_ref.dtype)

        return pl.pallas_call(body, out_shape=SDS((N_DEV, a.shape[1], w), out_dtype), name=name,
                              compiler_params=_params())(a, b)

    m, half = a.shape[0], a.shape[0] // 2
    n = b.shape[1] if tb else N_DEV * b.shape[2]

    def body(a_ref, b_ref, o_ref):
        av = a_ref[...].astype(bf16)
        o_ref[...] = (_nt(av, lanes(b_ref)) if tb else jnp.dot(av, lanes(b_ref), preferred_element_type=f32)
                      ).astype(o_ref.dtype)

    return pl.pallas_call(
        body, grid=(2,), in_specs=[pl.BlockSpec((half, a.shape[1]), lambda i: (i, 0)),
                                   pl.BlockSpec(b.shape, lambda i: (0, 0, 0))],
        out_specs=pl.BlockSpec((half, n), lambda i: (i, 0)), out_shape=SDS((m, n), out_dtype), name=name,
        compiler_params=_params(("parallel",)),
    )(a, b)


def _matmul_rows(form, a, b, after, fn, tiled, vecs, outs, reds, *, name, tm=512):
    norm = lambda ts: [t if isinstance(t, tuple) else (t, t.shape[1], 0) for t in ts]
    make, sources = a if isinstance(a, tuple) else (None, [a])
    sources, tiled = norm(sources), norm(tiled)
    m, k = sources[0][0].shape[0], (b.shape[0] if form == "nn" else b.shape[-1] * (N_DEV if form == "nt_stack" else 1))
    assert m % tm == 0
    ns, nt, nv, no = len(sources), len(tiled), len(vecs), len(outs)

    def body(*refs):
        src_refs, b_ref, refs = refs[:ns], refs[ns], refs[ns + 2:]
        if make is None:
            lhs = lambda lo, hi: src_refs[0][:, lo:hi]
        else:
            made = make([r[...] for r in src_refs]).astype(bf16)
            lhs = lambda lo, hi: made[:, lo:hi]
        if form == "nt_stack":
            w = b.shape[2]
            acc = _nt(lhs(0, w), b_ref[0])
            for j in range(1, N_DEV):
                acc = acc + _nt(lhs(j * w, (j + 1) * w), b_ref[j])
        elif form == "nt":
            acc = _nt(lhs(0, k), b_ref[...])
        else:
            acc = jnp.dot(lhs(0, k), b_ref[...], preferred_element_type=f32)
        if make is not None:
            refs[nt + nv][...] = made
            refs = refs[:nt + nv] + refs[nt + nv + 1:]
        orefs, rrefs = refs[nt + nv:nt + nv + no], refs[nt + nv + no:]
        touts, routs = fn([acc] + [r[...] for r in refs[:nt]], [r[...] for r in refs[nt:nt + nv]])
        for r, t in zip(orefs, touts, strict=True):
            r[...] = t.astype(r.dtype)

        @pl.when(pl.program_id(0) == 0)
        def _():
            for r in rrefs:
                r[...] = jnp.zeros_like(r)
        for r, t in zip(rrefs, routs, strict=True):
            r[...] += t

    def whole_map(nd):
        return lambda i: (0,) * nd

    def rows(width, cb=0):
        return pl.BlockSpec((tm, width), lambda i: (i, cb))

    made_out = [(k, bf16)] if make is not None else []
    return pl.pallas_call(
        body, grid=(m // tm,),
        in_specs=[rows(width, cb) for _, width, cb in sources]
        + [pl.BlockSpec(b.shape, whole_map(b.ndim), pipeline_mode=pl.Buffered(1)), pl.BlockSpec(memory_space=pl.ANY)]
        + [rows(width, cb) for _, width, cb in tiled] + [pl.BlockSpec(v.shape, whole_map(v.ndim)) for v in vecs],
        out_specs=[rows(width) for width, _ in made_out + list(outs)]
        + [pl.BlockSpec((1, width), lambda i: (0, 0)) for width in reds],
        out_shape=[SDS((m, width), dt) for width, dt in made_out + list(outs)]
        + [SDS((1, width), f32) for width in reds], name=name,
        compiler_params=_params(("arbitrary",)),
    )(*[t[0] for t in sources], b, after, *[t[0] for t in tiled], *vecs)


def _rms(x):
    r = lax.rsqrt(jnp.mean(x * x, axis=-1, keepdims=True) + EPS)
    return r, x * r


def _rms_bwd(r, xn, dxn):
    return r * (dxn - xn * jnp.mean(dxn * xn, axis=-1, keepdims=True))


def _colsum(t):
    return jnp.sum(t, axis=0, keepdims=True)


def _sigmoid(x):
    return 0.5 * jnp.tanh(0.5 * x) + 0.5


def _modulated_norm(x, g, shift, scale):
    _, xn = _rms(x)
    return (xn * g) * (1.0 + scale) + shift


def _pre1(x, modv, g_mix):
    def fn(t, v):
        (xt,), (mv, g) = t, v
        return [_modulated_norm(xt, g, mv[0:1], mv[1:2])], []
    return _rowwise(fn, "pre1", [(x, D, 0)], [modv, g_mix], [(D, bf16)])[0]


def _post1(ya, yb, proj, w_o, x, modv, g_ffn):
    def merge(t):
        ya_t, yb_t, ga, gb = t
        return _sigmoid(ga) * ya_t + _sigmoid(gb) * yb_t

    def fn(t, v):
        (mt, xt), (mv, g) = t, v
        x1 = xt + mv[2:3] * mt
        return [mt, x1, _modulated_norm(x1, g, mv[3:4], mv[4:5])], []
    return _matmul_rows("nn", (merge, [ya, yb, (proj, D, C_GA // D), (proj, D, C_GB // D)]), w_o, x, fn, [x],
                        [modv, g_ffn], [(D, f32), (D, f32), (D, bf16)], [], name="post1")


def _ffn_in(h, w_stack):
    def body(h_ref, w_ref, act_ref, au_ref):
        p = jnp.dot(h_ref[...], w_ref[0], preferred_element_type=f32)
        a, u = p[:, :FF_PAD], p[:, FF_PAD:]
        act_ref[...] = (a * _sigmoid(a) * u).astype(act_ref.dtype)
        au_ref[...] = p.astype(au_ref.dtype)

    return pl.pallas_call(
        body, grid=(N_DEV,),
        in_specs=[pl.BlockSpec((SEQ, D), lambda j: (0, 0)), pl.BlockSpec((1, D, 2 * FF_PAD), lambda j: (j, 0, 0))],
        out_specs=[pl.BlockSpec((SEQ, FF_PAD), lambda j: (0, j)), pl.BlockSpec((SEQ, 2 * FF_PAD), lambda j: (0, j))],
        out_shape=(SDS((SEQ, FF_HID), bf16), SDS((SEQ, 2 * FF_HID), bf16)), name="ffn_in",
        compiler_params=_params(("parallel",)),
    )(h, w_stack)


def _ffn_bwd_in(dff, w_down_stack, au):
    def body(d_ref, w_ref, au_ref, o_ref):
        dact = _nt(d_ref[...], w_ref[0])
        p = au_ref[...].astype(f32)
        a, u = p[:, :FF_PAD], p[:, FF_PAD:]
        sg = _sigmoid(a)
        o_ref[...] = jnp.concatenate([dact * u * (sg * (1.0 + a * (1.0 - sg))), dact * (a * sg)],
                                     axis=1).astype(o_ref.dtype)

    return pl.pallas_call(
        body, grid=(N_DEV,),
        in_specs=[pl.BlockSpec((SEQ, D), lambda j: (0, 0)), pl.BlockSpec((1, FF_PAD, D), lambda j: (j, 0, 0)),
                  pl.BlockSpec((SEQ, 2 * FF_PAD), lambda j: (0, j))],
        out_specs=pl.BlockSpec((SEQ, 2 * FF_PAD), lambda j: (0, j)),
        out_shape=SDS((SEQ, 2 * FF_HID), bf16), name="ffn_bwd_in", compiler_params=_params(("parallel",)),
    )(dff, w_down_stack, au)


def _final(act, w_down, x1, target, modv, g_final):
    def fn(t, v):
        (fft, x1t, tgt), (mv, g) = t, v
        x2 = x1t + mv[5:6] * fft
        r, xn = _rms(x2)
        err = xn * g - tgt
        dy = err * (1.0 / D)
        dx2 = _rms_bwd(r, xn, dy * g)
        return [dx2, dx2 * mv[5:6]], [_colsum(dy * xn), _colsum(dx2 * fft), _colsum(err * err) * (0.5 / D)]
    return _matmul_rows("nn", act, w_down, x1, fn, [x1, target], [modv, g_final], [(D, f32), (D, bf16)], [D, D, D],
                        name="final")


def _mid_bwd(dau, w_stack, after, x1, dx2, mix, modv, g_ffn):
    def fn(t, v):
        (dh, x1t, dx2t, mt), (mv, g) = t, v
        r, xn = _rms(x1t)
        dn = dh * (1.0 + mv[4:5])
        dx1 = dx2t + _rms_bwd(r, xn, dn * g)
        return [dx1, dx1 * mv[2:3]], [_colsum(dh), _colsum(dh * (xn * g)), _colsum(dn * xn), _colsum(dx1 * mt)]
    return _matmul_rows("nt_stack", dau, w_stack, after, fn, [x1, dx2, mix], [modv, g_ffn], [(D, f32), (D, bf16)],
                        [D, D, D, D], name="mid_bwd")


def _first_bwd(dproj, w_stack, after, x, dx1, modv, g_mix):
    def fn(t, v):
        (dh, xt, dx1t), (mv, g) = t, v
        r, xn = _rms(xt)
        dn = dh * (1.0 + mv[1:2])
        return [dx1t + _rms_bwd(r, xn, dn * g)], [_colsum(dh), _colsum(dh * (xn * g)), _colsum(dn * xn)]
    return _matmul_rows("nt_stack", dproj, w_stack, after, fn, [x, dx1], [modv, g_mix], [(D, f32)], [D, D, D],
                        name="first_bwd")


def _merge_bwd(dmix, w_o, after, ya, yb, proj):
    def fn(t, v):
        dm, ya_t, yb_t, ga, gb = t
        sa, sb = _sigmoid(ga), _sigmoid(gb)
        return [dm * sa, dm * sb, dm * ya_t * (sa * (1.0 - sa)), dm * yb_t * (sb * (1.0 - sb))], []
    return _matmul_rows("nt", dmix, w_o, after, fn, [ya, yb, (proj, D, C_GA // D), (proj, D, C_GB // D)], [],
                        [(D, bf16), (D, bf16), (D, bf16), (D, bf16)], [], name="merge_bwd")


def _rope_tables():
    half = ROT_DIM // 2
    pos = np.arange(SEQ, dtype=np.float32)
    inv_freq = np.float32(ROPE_THETA) ** (-np.arange(0, ROT_DIM, 2, dtype=np.float32) / np.float32(ROT_DIM))
    ang = pos[:, None] * inv_freq[None, :].astype(np.float32)
    cos, sin = np.cos(ang).astype(np.float32), np.sin(ang).astype(np.float32)
    pad = np.zeros((SEQ, HEAD_DIM - ROT_DIM), np.float32)
    zero = np.zeros((SEQ, half), np.float32)
    c_head = np.concatenate([cos, cos, pad + 1.0], axis=1)
    lo_head = np.concatenate([-sin, zero, pad], axis=1)
    hi_head = np.concatenate([zero, sin, pad], axis=1)
    return tuple(jnp.asarray(np.concatenate([t, t], axis=1)) for t in (c_head, lo_head, hi_head))


def _over_heads(tables):
    return [jnp.tile(t, (1, DIL_W // LANES)) for t in tables]


def _rope_fwd(proj, tables):
    half = ROT_DIM // 2

    def fn(t, v):
        q, k = t[:2]
        c, lo, hi = _over_heads(t[2:])
        rot = lambda z: z * c + pltpu.roll(z, DIL_W - half, 1) * lo + pltpu.roll(z, half, 1) * hi
        return [rot(q) * ATT_SCALE, rot(k)], []
    return _rowwise(fn, "rope_fwd", [(proj, DIL_W, C_QB // DIL_W), (proj, DIL_W, C_KB // DIL_W)]
                    + [(tb, LANES, 0) for tb in tables], [], [(DIL_W, f32)] * 2)


def _rope_bwd(dqs, dks, tables):
    half = ROT_DIM // 2

    def fn(t, v):
        dq_t, dk_t = jnp.concatenate(t[:N_GROUPS], axis=1), jnp.concatenate(t[N_GROUPS:2 * N_GROUPS], axis=1)
        c, lo, hi = _over_heads(t[2 * N_GROUPS:])
        rot_t = lambda z: z * c + pltpu.roll(z * lo, half, 1) + pltpu.roll(z * hi, DIL_W - half, 1)
        return [rot_t(dq_t), rot_t(dk_t)], []
    return _rowwise(fn, "rope_bwd", [(a, DIL_OUT_W, 0) for a in (*dqs, *dks)] + [(tb, LANES, 0) for tb in tables],
                    [], [(DIL_W, bf16), (DIL_W, bf16)])


def _head_bcast_sum(d):
    lane = lax.broadcasted_iota(jnp.int32, d.shape, 1)
    out = jnp.zeros_like(d)
    for h in range(d.shape[1] // HEAD_DIM):
        sel = (lane >= h * HEAD_DIM) & (lane < (h + 1) * HEAD_DIM)
        out = jnp.where(sel, jnp.sum(jnp.where(sel, d, 0.0), axis=1, keepdims=True), out)
    return out


def _dil_combine(outs, lses):
    def fn(t, v):
        o0, o1, o2, l0, l1, l2 = t
        m = jnp.maximum(jnp.maximum(l0, l1), l2)
        w0, w1, w2 = jnp.exp(l0 - m), jnp.exp(l1 - m), jnp.exp(l2 - m)
        tot = w0 + w1 + w2
        return [(w0 * o0 + w1 * o1 + w2 * o2) / tot, m + jnp.log(tot)], []
    w = DIL_OUT_W
    return _rowwise(fn, "dil_combine", [(t, w, 0) for t in (*outs, *lses)], [], [(w, f32), (w, f32)])


def _dil_delta(dyb_h, yb_h):
    def fn(t, v):
        return [_head_bcast_sum(t[0] * t[1])], []
    return _rowwise(fn, "dil_delta", [(dyb_h, DIL_OUT_W, 0), (yb_h, DIL_OUT_W, 0)], [], [(DIL_OUT_W, f32)])[0]


def _adamw_math(wt, gt, mt, vt):
    mn = ADAM_B1 * mt + (1.0 - ADAM_B1) * gt
    vn = ADAM_B2 * vt + (1.0 - ADAM_B2) * (gt * gt)
    m_hat = mn / (1.0 - ADAM_B1 ** ADAM_STEP)
    v_hat = vn / (1.0 - ADAM_B2 ** ADAM_STEP)
    return -ADAM_LR * (m_hat / (jnp.sqrt(v_hat) + ADAM_EPS) + ADAM_WD * wt), mn, vn


def _adamw(w, g, m, v, name):
    shape = w.shape
    if w.ndim == 1:
        w, g, m, v = (t.reshape(1, -1) for t in (w, g, m, v))
    rows, cols = w.shape
    tile = 256 if rows % 256 == 0 and rows > 512 else rows

    def fn(t, _):
        return list(_adamw_math(*t)), []
    delta, mn, vn = _rowwise(fn, name, [(w, cols, 0), (g, cols, 0), (m, cols, 0), (v, cols, 0)], [],
                             [(cols, f32)] * 3, tile=tile)
    return delta.reshape(shape), mn.reshape(shape), vn.reshape(shape)


def _adamw_by_planes(w, g, m, v, name, chunks=4):
    planes, _, width = w.shape
    bounds = [planes * k // chunks for k in range(chunks + 1)]
    spans = list(zip(bounds[:-1], bounds[1:]))

    def body(w_ref, g_ref, m_ref, v_ref, d_ref, mn_ref, vn_ref, *scratch):
        load_sems, store_sems = scratch[-2:]
        loads, stores, bufs = [], [], []
        for k, (lo, hi) in enumerate(spans):
            rows = pl.ds(lo, hi - lo)
            wb, gb, mb, vb = bufs_k = scratch[4 * k:4 * k + 4]
            bufs.append(bufs_k)
            loads.append([pltpu.make_async_copy(src.at[rows], buf, load_sems.at[4 * k + j])
                          for j, (src, buf) in enumerate(zip((w_ref, g_ref, m_ref, v_ref), bufs_k))])
            stores.append([pltpu.make_async_copy(buf, dst.at[rows], store_sems.at[3 * k + j])
                           for j, (buf, dst) in enumerate(zip((wb, mb, vb), (d_ref, mn_ref, vn_ref)))])
        for cp in sum(loads, []):
            cp.start()
        for k in range(chunks):
            for cp in loads[k]:
                cp.wait()
            wb, gb, mb, vb = bufs[k]
            wb[...], mb[...], vb[...] = _adamw_math(wb[...], gb[...], mb[...], vb[...])
            for cp in stores[k]:
                cp.start()
        for cp in sum(stores, []):
            cp.wait()

    hbm = pl.BlockSpec(memory_space=pl.ANY)
    scratch = [pltpu.VMEM((hi - lo, 1, width), f32) for lo, hi in spans for _ in range(4)]
    scratch += [pltpu.SemaphoreType.DMA((4 * chunks,)), pltpu.SemaphoreType.DMA((3 * chunks,))]
    return pl.pallas_call(body, in_specs=[hbm] * 4, out_specs=[hbm] * 3, out_shape=[SDS(w.shape, f32)] * 3,
                          scratch_shapes=scratch, name=name, compiler_params=_params())(w, g, m, v)


def _ada_fwd(c_all, w_shard, b_shard):
    def body(c_ref, w_ref, b_ref, o_ref):
        cv = c_ref[...]
        sc = (cv * _sigmoid(cv)).astype(bf16)
        o_ref[...] = jnp.dot(sc, w_ref[...].astype(bf16), preferred_element_type=f32) + b_ref[...]
    return pl.pallas_call(body, out_shape=SDS((N_DEV, w_shard.shape[1]), f32), name="ada_fwd",
                          compiler_params=_params())(c_all, w_shard, b_shard)


def _ada_bwd(c_all, dmod_cols):
    def body(c_ref, d_ref, o_ref):
        cv = c_ref[...]
        sc = cv * _sigmoid(cv)
        o_ref[...] = lax.dot_general(sc, d_ref[...], (((0,), (0,)), ((), ())), precision=lax.Precision.HIGHEST,
                                     preferred_element_type=f32)
    return pl.pallas_call(body, out_shape=SDS((D, dmod_cols.shape[1]), f32), name="ada_bwd",
                          compiler_params=_params())(c_all, dmod_cols)


def _small_reduce(gathered, after):
    def body(g_ref, after_ref, o_ref, loss_ref):
        acc = g_ref[0]
        for d in range(1, N_DEV):
            acc = acc + g_ref[d]
        o_ref[...] = acc
        loss_ref[...] = jnp.zeros((1, LANES), f32) + jnp.sum(acc[10:11, :])
    return pl.pallas_call(body, out_shape=(SDS((SMALL_ROWS, D), f32), SDS((1, LANES), f32)), name="small_reduce",
                          in_specs=[pl.BlockSpec(memory_space=pltpu.VMEM), pl.BlockSpec(memory_space=pl.ANY)],
                          compiler_params=_params())(gathered, after)


FOX_BLK = 512
CUM_BLK = 128


def _fold_lanes(t, op):
    out = t[:, :LANES]
    for j in range(1, t.shape[1] // LANES):
        out = op(out, t[:, j * LANES:(j + 1) * LANES])
    return out


def _fox_gate_fwd(proj, b_pad):
    nblk = SEQ // CUM_BLK

    def body(f_ref, b_ref, col_ref):
        r = lax.broadcasted_iota(jnp.int32, (CUM_BLK, CUM_BLK), 0)
        c = lax.broadcasted_iota(jnp.int32, (CUM_BLK, CUM_BLK), 1)
        tri = (r >= c).astype(f32)
        carry = jnp.zeros((1, LANES), f32)
        for blk in range(nblk):
            z = f_ref[blk * CUM_BLK:(blk + 1) * CUM_BLK, :] + b_ref[...]
            logf = jnp.minimum(z, 0.0) - jnp.log1p(jnp.exp(-jnp.abs(z)))
            cs = jnp.dot(tri, logf, precision=lax.Precision.HIGHEST, preferred_element_type=f32) + carry
            col_ref[blk * CUM_BLK:(blk + 1) * CUM_BLK, :] = cs
            carry = cs[CUM_BLK - 1:CUM_BLK, :]

    return pl.pallas_call(
        body, grid=(1,), in_specs=[pl.BlockSpec((SEQ, LANES), lambda i: (0, C_F // LANES)),
                                   pl.BlockSpec((1, LANES), lambda i: (0, 0))],
        out_specs=pl.BlockSpec((SEQ, LANES), lambda i: (0, 0)),
        out_shape=SDS((SEQ, LANES), f32), name="fox_gate_fwd",
        compiler_params=_params(("arbitrary",)),
    )(proj, b_pad)


def _fox_gate_bwd(dF_row, proj, b_pad):
    nblk = SEQ // CUM_BLK

    def body(d_ref, f_ref, b_ref, df_ref, db_ref, col_ref):
        r = lax.broadcasted_iota(jnp.int32, (CUM_BLK, CUM_BLK), 0)
        c = lax.broadcasted_iota(jnp.int32, (CUM_BLK, CUM_BLK), 1)
        tri = (r <= c).astype(f32)
        lane = lax.broadcasted_iota(jnp.int32, (CUM_BLK, LANES), 1)
        col_ref[...] = d_ref[...].T
        carry = jnp.zeros((1, LANES), f32)
        total = jnp.zeros((1, LANES), f32)
        for blk in reversed(range(nblk)):
            rows = slice(blk * CUM_BLK, (blk + 1) * CUM_BLK)
            cs = jnp.dot(tri, col_ref[rows, :], precision=lax.Precision.HIGHEST, preferred_element_type=f32) + carry
            carry = cs[0:1, :]
            z = f_ref[rows, :] + b_ref[...]
            df = jnp.where(lane < N_FOX_HEADS, cs * _sigmoid(-z), 0.0)
            df_ref[rows, :] = df.astype(df_ref.dtype)
            total = total + _colsum(df)
        db_ref[...] = total

    return pl.pallas_call(
        body, grid=(1,), in_specs=[pl.BlockSpec((LANES, SEQ), lambda i: (0, 0)),
                                   pl.BlockSpec((SEQ, LANES), lambda i: (0, C_F // LANES)),
                                   pl.BlockSpec((1, LANES), lambda i: (0, 0))],
        out_specs=[pl.BlockSpec((SEQ, LANES), lambda i: (0, 0)), pl.BlockSpec((1, LANES), lambda i: (0, 0))],
        out_shape=(SDS((SEQ, LANES), bf16), SDS((1, LANES), f32)), name="fox_gate_bwd",
        scratch_shapes=[pltpu.VMEM((SEQ, LANES), f32)],
        compiler_params=_params(("arbitrary",)),
    )(dF_row, proj, b_pad)


def _nt(a, b):
    return lax.dot_general(a, b, (((1,), (1,)), ((), ())), preferred_element_type=f32)


def _tn(a, b):
    return lax.dot_general(a, b, (((0,), (0,)), ((), ())), preferred_element_type=f32)


def _fox_prep(proj, f_col):
    def fn(t, v):
        q, k, vv, fc = t
        lane = lax.broadcasted_iota(jnp.int32, (q.shape[0], LANES), 1)
        qs, ks = [], []
        for h in range(N_FOX_HEADS):
            pair, pos = divmod(h, 2)
            own = (lane >= pos * HEAD_DIM) & (lane < (pos + 1) * HEAD_DIM)
            base = (1 - pos) * HEAD_DIM
            f = fc[:, h:h + 1]
            hi = f.astype(bf16).astype(f32)
            mid = (f - hi).astype(bf16).astype(f32)
            lo = (f - hi) - mid
            one = jnp.ones_like(f)
            qa = jnp.where(own, q[:, pair * LANES:(pair + 1) * LANES] * ATT_SCALE, 0.0)
            ka = k[:, pair * LANES:(pair + 1) * LANES]
            for idx, (qv, kv) in enumerate([(hi, one), (mid, one), (lo, one), (one, -hi), (one, -mid), (one, -lo)]):
                sel = lane == base + idx
                qa = jnp.where(sel, qv, qa)
                ka = jnp.where(sel, kv, ka)
            qs.append(qa)
            ks.append(ka)
        return [jnp.concatenate(qs, axis=1), jnp.concatenate(ks, axis=1), vv], []
    w = N_FOX_HEADS * LANES
    return _rowwise(fn, "fox_prep", [(proj, FOX_W, C_QA // FOX_W), (proj, FOX_W, C_KA // FOX_W),
                                     (proj, FOX_W, C_VA // FOX_W), (f_col, LANES, 0)], [],
                    [(w, bf16), (w, bf16), (FOX_W, bf16)])


def _fox_fwd(q_aug, k_aug, v):
    blk = FOX_BLK
    npair = FOX_W // LANES

    def body(q_ref, k_ref, v_ref, o_ref, max_ref, sum_ref, s_scr):
        i = pl.program_id(1)
        tri = lax.broadcasted_iota(jnp.int32, (blk, blk), 0) >= lax.broadcasted_iota(jnp.int32, (blk, blk), 1)
        qh = [q_ref[:, h * LANES:(h + 1) * LANES] for h in range(2)]

        def logits(c, masked):
            off = pl.multiple_of(c * blk, blk)
            tops = []
            for h in range(2):
                s = _nt(qh[h], k_ref[pl.ds(off, blk), h * LANES:(h + 1) * LANES])
                if masked:
                    s = jnp.where(tri, s, NEG)
                s_scr[h, :, pl.ds(off, blk)] = s
                tops.append(_fold_lanes(s, jnp.maximum))
            return tops

        def pass_a(c, m):
            return tuple(jnp.maximum(a, b) for a, b in zip(m, logits(c, False)))

        m = lax.fori_loop(0, i, pass_a, tuple(jnp.full((blk, LANES), NEG, f32) for _ in range(2)))
        mx = [jnp.max(jnp.maximum(a, b), axis=1, keepdims=True) for a, b in zip(m, logits(i, True))]

        def pass_b(c, carry):
            off = pl.multiple_of(c * blk, blk)
            vv = v_ref[pl.ds(off, blk), :]
            new = []
            for h in range(2):
                l, acc = carry[h]
                p = jnp.exp(s_scr[h, :, pl.ds(off, blk)] - mx[h]).astype(bf16)
                new.append((l + _fold_lanes(p.astype(f32), jnp.add), acc + jnp.dot(p, vv, preferred_element_type=f32)))
            return tuple(new)

        zero = jnp.zeros((blk, LANES), f32)
        (l_a, acc_a), (l_b, acc_b) = lax.fori_loop(0, i + 1, pass_b, ((zero, zero), (zero, zero)))
        l_a = jnp.sum(l_a, axis=1, keepdims=True)
        l_b = jnp.sum(l_b, axis=1, keepdims=True)
        first = lax.broadcasted_iota(jnp.int32, (blk, LANES), 1) < HEAD_DIM
        o_ref[...] = jnp.where(first, acc_a / l_a, acc_b / l_b)
        max_ref[0] = jnp.where(first, mx[0], mx[1])
        sum_ref[0] = jnp.where(first, l_a, l_b)

    return pl.pallas_call(
        body, grid=(npair, SEQ // blk),
        in_specs=[pl.BlockSpec((blk, 2 * LANES), lambda p, i: (i, p)),
                  pl.BlockSpec((SEQ, 2 * LANES), lambda p, i: (0, p)),
                  pl.BlockSpec((SEQ, LANES), lambda p, i: (0, p))],
        out_specs=[pl.BlockSpec((blk, LANES), lambda p, i: (i, p))]
        + [pl.BlockSpec((1, blk, LANES), lambda p, i: (p, i, 0))] * 2,
        out_shape=(SDS((SEQ, FOX_W), f32),) + (SDS((npair, SEQ, LANES), f32),) * 2, name="fox_fwd",
        scratch_shapes=[pltpu.VMEM((2, blk, SEQ), f32)],
        compiler_params=_params(("parallel", "arbitrary")),
    )(q_aug, k_aug, v)


def _fox_bwd(q_aug, k_aug, v, do, o, row_max, row_sum, after):
    blk = FOX_BLK
    npair = FOX_W // LANES
    nblk = SEQ // blk

    def body(q_ref, k_ref, v_ref, do_ref, o_ref, max_ref, sum_ref, after_ref, dq_ref, dk_ref, dv_ref, df_ref, dq_acc,
             delta_ref, inv_ref):
        inv_ref[...] = 1.0 / sum_ref[0]
        lane_s = lax.broadcasted_iota(jnp.int32, (SEQ, LANES), 1)
        prod = do_ref[...].astype(bf16).astype(f32) * o_ref[...]
        d_a = jnp.sum(jnp.where(lane_s < HEAD_DIM, prod, 0.0), axis=1, keepdims=True)
        d_b = jnp.sum(jnp.where(lane_s >= HEAD_DIM, prod, 0.0), axis=1, keepdims=True)
        delta_ref[...] = jnp.where(lane_s < HEAD_DIM, d_a, d_b)
        dq_acc[...] = jnp.zeros_like(dq_acc)
        df_ref[...] = jnp.zeros_like(df_ref)
        lane = lax.broadcasted_iota(jnp.int32, (blk, LANES), 1)
        own = [lane < HEAD_DIM, lane >= HEAD_DIM]
        tri = lax.broadcasted_iota(jnp.int32, (blk, blk), 0) >= lax.broadcasted_iota(jnp.int32, (blk, blk), 1)

        def q_slab(qoff, h):
            return q_ref[pl.ds(qoff, blk), h * LANES:(h + 1) * LANES]

        def probs(qoff, h, k_h, masked):
            s = _nt(q_slab(qoff, h), k_h)
            if masked:
                s = jnp.where(tri, s, NEG)
            col = slice(h * HEAD_DIM, h * HEAD_DIM + 1)
            weights = jnp.exp(s - max_ref[0, pl.ds(qoff, blk), col]).astype(bf16).astype(f32)
            return weights * inv_ref[pl.ds(qoff, blk), col]

        def k_slabs(koff):
            return [k_ref[pl.ds(koff, blk), h * LANES:(h + 1) * LANES] for h in range(2)]

        def kv_step(kj, _):
            koff = pl.multiple_of(kj * blk, blk)
            k_aug = k_slabs(koff)
            k_own = [jnp.where(own[h], k_aug[h], jnp.zeros_like(k_aug[h])) for h in range(2)]
            vv = v_ref[pl.ds(koff, blk), :]
            v_own = [jnp.where(own[h], vv, jnp.zeros_like(vv)) for h in range(2)]

            def q_tile(qi, carry, masked):
                qoff = pl.multiple_of(qi * blk, blk)
                dd = do_ref[pl.ds(qoff, blk), :].astype(bf16)
                new, dq_add = [], None
                for h in range(2):
                    dk_h, dv_h, dcol = carry[h]
                    p = probs(qoff, h, k_aug[h], masked)
                    dl = p * (_nt(dd, v_own[h]) - delta_ref[pl.ds(qoff, blk), h * HEAD_DIM:h * HEAD_DIM + 1])
                    dlb = dl.astype(bf16)
                    part = jnp.dot(dlb, k_own[h], preferred_element_type=f32)
                    dq_add = part if dq_add is None else dq_add + part
                    new.append((dk_h + _tn(dlb, q_slab(qoff, h)), dv_h + _tn(p.astype(bf16), dd),
                                dcol + _colsum(dl)))
                dq_acc[pl.ds(qoff, blk), :] += dq_add * ATT_SCALE
                return tuple(new)

            zero = (jnp.zeros((blk, LANES), f32), jnp.zeros((blk, LANES), f32), jnp.zeros((1, blk), f32))
            carry = q_tile(kj, (zero, zero), True)
            (dk_a, dv_a, dcol_a), (dk_b, dv_b, dcol_b) = lax.fori_loop(
                kj + 1, nblk, lambda qi, cr: q_tile(qi, cr, False), carry)
            dk_ref[pl.ds(koff, blk), :] = jnp.where(own[0], dk_a, dk_b).astype(dk_ref.dtype)
            dv_ref[pl.ds(koff, blk), :] = jnp.where(own[0], dv_a, dv_b).astype(dv_ref.dtype)
            df_ref[0, 0:1, pl.ds(koff, blk)] = -dcol_a
            df_ref[0, 1:2, pl.ds(koff, blk)] = -dcol_b
            return 0

        lax.fori_loop(0, nblk, kv_step, 0)
        dq_ref[...] = dq_acc[...].astype(dq_ref.dtype)

    pair_aug = pl.BlockSpec((SEQ, 2 * LANES), lambda p: (0, p))
    slab = pl.BlockSpec((SEQ, LANES), lambda p: (0, p))
    per_pair = pl.BlockSpec((1, SEQ, LANES), lambda p: (p, 0, 0))
    rows = pl.BlockSpec((1, 8, SEQ), lambda p: (p, 0, 0))
    return pl.pallas_call(
        body, grid=(npair,),
        in_specs=[pair_aug, pair_aug, slab, slab, slab, per_pair, per_pair, pl.BlockSpec(memory_space=pl.ANY)],
        out_specs=[slab, slab, slab, rows],
        out_shape=(SDS((SEQ, FOX_W), bf16),) * 3 + (SDS((npair, 8, SEQ), f32),), name="fox_bwd",
        scratch_shapes=[pltpu.VMEM((SEQ, LANES), f32)] * 3,
        compiler_params=_params(("parallel",)),
    )(q_aug, k_aug, v, do, o, row_max, row_sum, after)


DIL_BLK = 128
DILATIONS = (1, 4, 16)
N_GROUPS = len(DILATIONS)
DIL_PAIRS = DIL_OUT_W // LANES


def _dil_blocks(d):
    r1 = lax.broadcasted_iota(jnp.int32, (2 * DIL_BLK, DIL_BLK), 0) & (DIL_BLK - 1)
    c1 = lax.broadcasted_iota(jnp.int32, (2 * DIL_BLK, DIL_BLK), 1)
    r2 = lax.broadcasted_iota(jnp.int32, (2 * DIL_BLK, 2 * DIL_BLK), 0) & (DIL_BLK - 1)
    c2 = lax.broadcasted_iota(jnp.int32, (2 * DIL_BLK, 2 * DIL_BLK), 1)
    band = ((c2 < DIL_BLK) & (c2 >= r2)) | ((c2 >= DIL_BLK) & (c2 - DIL_BLK <= r2))
    out = []
    for r in range(d):
        for b in range(SEQ // d // DIL_BLK):
            rows = pl.ds(r + d * DIL_BLK * b, DIL_BLK, stride=d)
            if b == 0:
                out.append((rows, rows, r1 >= c1))
            else:
                out.append((rows, pl.ds(r + d * DIL_BLK * (b - 1), 2 * DIL_BLK, stride=d), band))
    return out


def _dil_v_spec(g):
    return pl.BlockSpec((SEQ, LANES), lambda p: (0, C_VB // LANES + DIL_PAIRS * g + p))


def _stack_heads(t, first):
    zero = jnp.zeros_like(t)
    return jnp.concatenate([jnp.where(first, t, zero), jnp.where(first, zero, t)], axis=0)


def _dil_fwd(q, k, v, g):
    def body(q_ref, k_ref, v_ref, o_ref, lse_ref):
        first = lax.broadcasted_iota(jnp.int32, (DIL_BLK, LANES), 1) < HEAD_DIM
        for rows, krows, mask in _dil_blocks(DILATIONS[g]):
            qv, kk, vv = q_ref[rows, :].astype(bf16), k_ref[krows, :].astype(bf16), v_ref[krows, :].astype(bf16)
            s = jnp.where(mask, _nt(_stack_heads(qv, first), kk), NEG)
            m = jnp.max(s, axis=1, keepdims=True)
            p = jnp.exp(s - m)
            l = jnp.sum(p, axis=1, keepdims=True)
            out = jnp.dot(p.astype(bf16), vv, preferred_element_type=f32) / l
            lse = m + jnp.log(l)
            o_ref[rows, :] = jnp.where(first, out[:DIL_BLK], out[DIL_BLK:])
            lse_ref[rows, :] = jnp.where(first, lse[:DIL_BLK], lse[DIL_BLK:])

    grouped = pl.BlockSpec((SEQ, LANES), lambda p: (0, DIL_PAIRS * g + p))
    own = pl.BlockSpec((SEQ, LANES), lambda p: (0, p))
    shape = SDS((SEQ, DIL_OUT_W), f32)
    return pl.pallas_call(
        body, grid=(DIL_PAIRS,), in_specs=[grouped, grouped, _dil_v_spec(g)], out_specs=[own] * 2,
        out_shape=(shape, shape),
        name=f"dil_fwd_{DILATIONS[g]}", compiler_params=_params(("parallel",)),
    )(q, k, v)


def _dil_bwd(q, k, v, do, lse, delta, g):
    def body(q_ref, k_ref, v_ref, do_ref, lse_ref, dl_ref, dq_ref, dk_ref, dv_ref):
        first = lax.broadcasted_iota(jnp.int32, (DIL_BLK, LANES), 1) < HEAD_DIM
        dk_ref[...] = jnp.zeros_like(dk_ref)
        dv_ref[...] = jnp.zeros_like(dv_ref)
        for rows, krows, mask in _dil_blocks(DILATIONS[g]):
            qv, kk, vv = q_ref[rows, :].astype(bf16), k_ref[krows, :].astype(bf16), v_ref[krows, :].astype(bf16)
            lsev, delv = lse_ref[rows, :], dl_ref[rows, :]
            q2 = _stack_heads(qv, first)
            do2 = _stack_heads(do_ref[rows, :].astype(bf16), first)
            per_head = lambda t: jnp.concatenate([t[:, 0:1], t[:, HEAD_DIM:HEAD_DIM + 1]], axis=0)
            p = jnp.exp(jnp.where(mask, _nt(q2, kk), NEG) - per_head(lsev))
            dl = (p * (_nt(do2, vv) - per_head(delv))).astype(bf16)
            dq = jnp.dot(dl, kk, preferred_element_type=f32)
            dq_ref[rows, :] = jnp.where(first, dq[:DIL_BLK], dq[DIL_BLK:]) * ATT_SCALE
            dk_ref[krows, :] += _tn(dl, q2)
            dv_ref[krows, :] += _tn(p.astype(bf16), do2)

    grouped = pl.BlockSpec((SEQ, LANES), lambda p: (0, DIL_PAIRS * g + p))
    own = pl.BlockSpec((SEQ, LANES), lambda p: (0, p))
    shape = SDS((SEQ, DIL_OUT_W), f32)
    return pl.pallas_call(
        body, grid=(DIL_PAIRS,), in_specs=[grouped, grouped, _dil_v_spec(g)] + [own] * 3, out_specs=[own] * 3,
        out_shape=(shape, shape, shape), name=f"dil_bwd_{DILATIONS[g]}", compiler_params=_params(("parallel",)),
    )(q, k, v, do, lse, delta)


def _position():
    return lax.axis_index("x"), lax.axis_index("y"), lax.axis_index("c")


def _all_gather(block, name, after=None):
    after = [] if after is None else [after]

    def body(x_ref, *refs):
        out_ref, send_sems, recv_sems, local_sem = refs[len(after):]
        x, y, c = _position()
        me, sibling = (x, y, c), (x, y, 1 - c)
        chips = [(1 - x, y), (x, 1 - y), (1 - x, 1 - y)]

        def slot(px, py, pc):
            return out_ref.at[4 * px + 2 * py + pc]

        def copy(k, blk, to, src=None):
            return pltpu.make_async_remote_copy(
                src_ref=slot(*blk) if src is None else src, dst_ref=slot(*blk),
                send_sem=send_sems.at[k], recv_sem=recv_sems.at[k], device_id=to, device_id_type=MESH)

        mine = pltpu.make_async_copy(x_ref, slot(*me), local_sem)
        mine.start()
        first = [copy(0, me, sibling, src=x_ref)]
        first += [copy(1 + j, me, (*chip, c), src=x_ref) for j, chip in enumerate(chips)]
        for cp in first:
            cp.start()
        passed = [copy(4 + j, (*chip, c), sibling) for j, chip in enumerate(chips)]
        for j, chip in enumerate(chips):
            copy(1 + j, (*chip, c), me).wait_recv()
            passed[j].start()
        copy(0, sibling, me).wait_recv()
        for j, chip in enumerate(chips):
            copy(4 + j, (*chip, 1 - c), me).wait_recv()
        for cp in first + passed:
            cp.wait_send()
        mine.wait()

    return pl.pallas_call(
        body, out_shape=SDS((N_DEV,) + block.shape, block.dtype),
        in_specs=[pl.BlockSpec(memory_space=pl.ANY)] * (1 + len(after)), out_specs=pl.BlockSpec(memory_space=pl.ANY),
        scratch_shapes=[pltpu.SemaphoreType.DMA((7,)), pltpu.SemaphoreType.DMA((7,)), pltpu.SemaphoreType.DMA],
        name=name,
    )(block, *after)


HBM_SPEC = pl.BlockSpec(memory_space=pltpu.HBM)
SEM_SPEC = pl.BlockSpec(memory_space=pltpu.SEMAPHORE)
SPLIT_COPY = pltpu.CompilerParams(has_side_effects=pltpu.SideEffectType.DATAFLOW_SIDE_EFFECTING)


def _in_hbm(t):
    return pltpu.with_memory_space_constraint(t, pltpu.HBM)


def _pair_copies(g_refs, land_refs, send_sems, recv_sems):
    x, y, c = _position()
    return [pltpu.make_async_remote_copy(
        src_ref=g.at[2 * k + (1 - c)], dst_ref=land.at[k], send_sem=send_sems.at[4 * a + k],
        recv_sem=recv_sems.at[4 * a + k], device_id=(x, y, 1 - c), device_id_type=MESH)
        for a, (g, land) in enumerate(zip(g_refs, land_refs, strict=True)) for k in range(4)]


def _chip_copies(t_refs, land_refs, send_sems, recv_sems):
    x, y, c = _position()
    chips = [(1 - x, y), (x, 1 - y), (1 - x, 1 - y)]
    return [pltpu.make_async_remote_copy(
        src_ref=t.at[2 * px + py], dst_ref=land.at[j], send_sem=send_sems.at[3 * a + j],
        recv_sem=recv_sems.at[3 * a + j], device_id=(px, py, c), device_id_type=MESH)
        for a, (t, land) in enumerate(zip(t_refs, land_refs, strict=True)) for j, (px, py) in enumerate(chips)]


_ROUNDS = {"pair": (_pair_copies, 4), "chip": (_chip_copies, 3)}


def _exchange_start(kind, ts, name):
    copies, slots = _ROUNDS[kind]
    n = len(ts)
    lands = [_in_hbm(lax.empty((slots,) + t.shape[1:], t.dtype)) for t in ts]

    def body(*refs):
        for cp in copies(refs[:n], refs[n:2 * n], refs[2 * n], refs[2 * n + 1]):
            cp.start()
        refs[-1][...] = jnp.zeros_like(refs[-1])

    sems = pltpu.SemaphoreType.DMA((slots * n,))
    res = pl.pallas_call(
        body, name=name, in_specs=[HBM_SPEC] * (2 * n),
        out_shape=(sems, sems, *[pltpu.HBM(t.shape, t.dtype) for t in (*ts, *lands)], SDS((8, LANES), f32)),
        out_specs=(SEM_SPEC, SEM_SPEC, *[HBM_SPEC] * (2 * n), pl.BlockSpec(memory_space=pltpu.VMEM)),
        input_output_aliases={i: 2 + i for i in range(2 * n)}, compiler_params=SPLIT_COPY,
    )(*[_in_hbm(t) for t in ts], *lands)
    return res[:-1], res[-1]


def _exchange_wait(kind, state, after, name):
    copies, _ = _ROUNDS[kind]
    send_sems, recv_sems, *arrays = state
    n = len(arrays) // 2

    def body(*refs):
        for cp in copies(refs[:n], refs[n:2 * n], refs[2 * n], refs[2 * n + 1]):
            cp.wait_send()
            cp.wait_recv()

    res = pl.pallas_call(
        body, name=name, in_specs=[HBM_SPEC] * (2 * n) + [SEM_SPEC, SEM_SPEC, pl.BlockSpec(memory_space=pl.ANY)],
        out_shape=[pltpu.HBM(t.shape, t.dtype) for t in arrays], out_specs=[HBM_SPEC] * (2 * n),
        input_output_aliases={i: i for i in range(2 * n)}, compiler_params=SPLIT_COPY,
    )(*arrays, send_sems, recv_sems, after)
    return res[:n], res[n:]


def _gather_copies(x_refs, out_refs, send_sems, recv_sems):
    x, y, c = _position()
    peers = [(x, y, 1 - c), (1 - x, y, c), (x, 1 - y, c), (1 - x, 1 - y, c)]
    sends, arrivals = [], []
    for a, (x_ref, out_ref) in enumerate(zip(x_refs, out_refs, strict=True)):
        for k, (px, py, pc) in enumerate(peers):
            sems = dict(send_sem=send_sems.at[4 * a + k], recv_sem=recv_sems.at[4 * a + k],
                        device_id=(px, py, pc), device_id_type=MESH)
            sends.append(pltpu.make_async_remote_copy(src_ref=x_ref, dst_ref=out_ref.at[4 * x + 2 * y + c], **sems))
            arrivals.append(pltpu.make_async_remote_copy(src_ref=x_ref, dst_ref=out_ref.at[4 * px + 2 * py + pc],
                                                         **sems))
    return sends, arrivals


def _gather_start(blocks, after, name):
    n = len(blocks)
    outs = [_in_hbm(lax.empty((N_DEV,) + b.shape, b.dtype)) for b in blocks]

    def body(*refs):
        sends, _ = _gather_copies(refs[:n], refs[n:2 * n], refs[2 * n + 1], refs[2 * n + 2])
        for cp in sends:
            cp.start()
        refs[-1][...] = jnp.zeros_like(refs[-1])

    sems = pltpu.SemaphoreType.DMA((4 * n,))
    res = pl.pallas_call(
        body, name=name, in_specs=[HBM_SPEC] * (2 * n) + [pl.BlockSpec(memory_space=pl.ANY)],
        out_shape=(sems, sems, *[pltpu.HBM(t.shape, t.dtype) for t in (*blocks, *outs)], SDS((8, LANES), f32)),
        out_specs=(SEM_SPEC, SEM_SPEC, *[HBM_SPEC] * (2 * n), pl.BlockSpec(memory_space=pltpu.VMEM)),
        input_output_aliases={i: 2 + i for i in range(2 * n)}, compiler_params=SPLIT_COPY,
    )(*[_in_hbm(b) for b in blocks], *outs, after)
    return res[:-1], res[-1]


def _gather_wait(state, after, name):
    send_sems, recv_sems, *arrays = state
    n = len(arrays) // 2

    def body(*refs):
        sends, arrivals = _gather_copies(refs[:n], refs[n:2 * n], refs[2 * n], refs[2 * n + 1])
        for cp in sends:
            cp.wait_send()
        for cp in arrivals:
            cp.wait_recv()

    res = pl.pallas_call(
        body, name=name, in_specs=[HBM_SPEC] * (2 * n) + [SEM_SPEC, SEM_SPEC, pl.BlockSpec(memory_space=pl.ANY)],
        out_shape=[pltpu.HBM(t.shape, t.dtype) for t in arrays], out_specs=[HBM_SPEC] * (2 * n),
        input_output_aliases={i: i for i in range(2 * n)}, compiler_params=SPLIT_COPY,
    )(*arrays, send_sems, recv_sems, after)
    return res[:n], res[n:]


def _gather_finish(partial, name):
    n = len(partial)

    def body(*refs):
        in_refs, out_refs = refs[:n], refs[n:2 * n]
        send_sems, recv_sems = refs[2 * n:]
        x, y, c = _position()
        chips = [(1 - x, y), (x, 1 - y), (1 - x, 1 - y)]
        copies = []
        for a in range(n):
            for j, (px, py) in enumerate(chips):
                cp = pltpu.make_async_remote_copy(
                    src_ref=in_refs[a].at[4 * px + 2 * py + c], dst_ref=out_refs[a].at[4 * px + 2 * py + c],
                    send_sem=send_sems.at[a, j], recv_sem=recv_sems.at[a, j], device_id=(x, y, 1 - c),
                    device_id_type=MESH)
                cp.start()
                copies.append(cp)
        for a in range(n):
            for j, (px, py) in enumerate(chips):
                pltpu.make_async_remote_copy(
                    src_ref=in_refs[a].at[4 * px + 2 * py + (1 - c)], dst_ref=out_refs[a].at[4 * px + 2 * py + (1 - c)],
                    send_sem=send_sems.at[a, j], recv_sem=recv_sems.at[a, j], device_id=(x, y, 1 - c),
                    device_id_type=MESH).wait_recv()
        for cp in copies:
            cp.wait_send()

    hbm = pl.BlockSpec(memory_space=pl.ANY)
    return pl.pallas_call(
        body, out_shape=[SDS(p.shape, p.dtype) for p in partial], in_specs=[hbm] * n, out_specs=[hbm] * n,
        input_output_aliases={a: a for a in range(n)},
        scratch_shapes=[pltpu.SemaphoreType.DMA((n, 3)), pltpu.SemaphoreType.DMA((n, 3))],
        name=name,
    )(*partial)


def _row_tile(rows):
    return 512 if rows % 512 == 0 and rows > 512 else rows


def _pair_add(g, r1, core, name):
    def body(c_ref, g_ref, r_ref, o_ref):
        o_ref[...] = (g_ref[...].astype(f32) + r_ref[...].astype(f32)).astype(o_ref.dtype)

    rows, cols = g.shape[1:]
    tile = _row_tile(rows)
    blk = (1, tile, cols)
    return pl.pallas_call(
        body, out_shape=SDS((4, rows, cols), g.dtype), name=name,
        grid_spec=pltpu.PrefetchScalarGridSpec(
            num_scalar_prefetch=1, grid=(4, rows // tile),
            in_specs=[pl.BlockSpec(blk, lambda k, i, c_ref: (2 * k + c_ref[0], i, 0)),
                      pl.BlockSpec(blk, lambda k, i, c_ref: (k, i, 0))],
            out_specs=pl.BlockSpec(blk, lambda k, i, c_ref: (k, i, 0))),
        compiler_params=_params(("parallel", "arbitrary")),
    )(core, g, r1)


def _chip_add(t, r2, chip, name, planes=None, pieces=None, after=None):
    after = [] if after is None else [after]

    def body(c_ref, t_ref, r_ref, *refs):
        o_refs = refs[len(after):]
        s = ((t_ref[0].astype(f32) + r_ref[0].astype(f32)) + r_ref[1].astype(f32)) + r_ref[2].astype(f32)
        if planes:
            o_refs[0][...] = s.T[:planes][:, None, :]
        elif pieces:
            for o_ref, (start, size) in zip(o_refs, pieces, strict=True):
                o_ref[...] = s.T[start:start + size]
        else:
            o_refs[0][...] = s

    rows, cols = t.shape[1:]
    tile = _row_tile(rows)
    if planes:
        out_shape, out_spec = SDS((planes, 1, rows), f32), pl.BlockSpec((planes, 1, tile), lambda i, c_ref: (0, 0, i))
    elif pieces:
        out_shape = [SDS((size, rows), f32) for _, size in pieces]
        out_spec = [pl.BlockSpec((size, tile), lambda i, c_ref: (0, i)) for _, size in pieces]
    else:
        out_shape, out_spec = SDS((rows, cols), f32), pl.BlockSpec((tile, cols), lambda i, c_ref: (i, 0))
    return pl.pallas_call(
        body, out_shape=out_shape, name=name,
        grid_spec=pltpu.PrefetchScalarGridSpec(
            num_scalar_prefetch=1, grid=(rows // tile,),
            in_specs=[pl.BlockSpec((1, tile, cols), lambda i, c_ref: (c_ref[0], i, 0)),
                      pl.BlockSpec((3, tile, cols), lambda i, c_ref: (0, i, 0))]
            + [pl.BlockSpec(memory_space=pl.ANY)] * len(after),
            out_specs=out_spec),
        compiler_params=_params(("arbitrary",)),
    )(chip, t, r2, *after)


def _pad_to(t, axis, size):
    pads = [(0, 0)] * t.ndim
    pads[axis] = (0, size - t.shape[axis])
    return jnp.pad(t, pads)


_REF_COLS = {"qa": (0, FOX_W), "ka": (FOX_W, FOX_W), "va": (2 * FOX_W, FOX_W), "f": (3 * FOX_W, N_FOX_HEADS)}
_REF_COLS.update({n: (3 * FOX_W + N_FOX_HEADS + i * DIL_W, DIL_W) for i, n in enumerate(("qb", "kb", "vb"))})
_REF_COLS.update({n: (3 * FOX_W + N_FOX_HEADS + 3 * DIL_W + i * D, D) for i, n in enumerate(("ga", "gb"))})
_REF_ORDER = ("qa", "ka", "va", "f", "qb", "kb", "vb", "ga", "gb")


def _place_cols(sources, src_of, out_cols, name, row_block=512):
    arrays = [s[0] if isinstance(s, tuple) else s for s in sources]
    widths = [a.shape[-1] for a in arrays]
    rows = arrays[0].shape[-2]
    plan = []
    for t in range(out_cols // LANES):
        segs, c, end = [], t * LANES, (t + 1) * LANES
        while c < end:
            s = src_of(c)
            if s is None:
                c += 1
                continue
            n = 1
            while c + n < end and src_of(c + n) == (s[0], s[1] + n):
                n += 1
            segs.append((s[0], s[1], c - t * LANES, n))
            c += n
        plan.append(segs)

    def body(*refs):
        o_ref = refs[-1]
        for t, segs in enumerate(plan):
            acc = None
            for si, c0, o0, n in segs:
                a0 = c0 // LANES * LANES
                wide = min(2 * LANES, widths[si] - a0)
                win = refs[si][0, :, a0:a0 + wide] if isinstance(sources[si], tuple) else refs[si][:, a0:a0 + wide]
                r = lax.broadcasted_iota(jnp.int32, (wide, LANES), 0)
                c = lax.broadcasted_iota(jnp.int32, (wide, LANES), 1)
                pick = ((r - (c0 - a0) == c - o0) & (c >= o0) & (c < o0 + n)).astype(bf16)
                part = jnp.dot(win.astype(bf16), pick, preferred_element_type=f32)
                acc = part if acc is None else acc + part
            tile = jnp.zeros((row_block, LANES), f32) if acc is None else acc
            o_ref[:, t * LANES:(t + 1) * LANES] = tile.astype(o_ref.dtype)

    def spec(s):
        if isinstance(s, tuple):
            j = s[1]
            return pl.BlockSpec((1, row_block, s[0].shape[-1]), lambda i: (j, i, 0))
        return pl.BlockSpec((row_block, s.shape[-1]), lambda i: (i, 0))

    return pl.pallas_call(
        body, grid=(rows // row_block,), in_specs=[spec(s) for s in sources],
        out_specs=pl.BlockSpec((row_block, out_cols), lambda i: (i, 0)), out_shape=SDS((rows, out_cols), bf16),
        name=name, compiler_params=_params(("parallel",)),
    )(*arrays)


def _ref_piece(r):
    for name in _REF_ORDER:
        lo, width = _REF_COLS[name]
        if lo <= r < lo + width:
            return name, r - lo
    raise ValueError(r)


def _shard_pad_cols(pieces):
    names = [n for n in _REF_ORDER if n != "vb"]
    sources = [pieces[n] for n in names] + list(pieces["vb"])

    def src_of(c):
        j, i = divmod(c, W_IN_PAD)
        if i >= W_IN_SH:
            return None
        name, col = _ref_piece(j * W_IN_SH + i)
        if name == "vb":
            return len(names) + col // DIL_OUT_W, col % DIL_OUT_W
        return names.index(name), col

    return _place_cols(sources, src_of, N_DEV * W_IN_PAD, "place_dproj")


_SLABS = {"ga": C_GA, "gb": C_GB, "qb": C_QB, "kb": C_KB, "vb": C_VB, "qa": C_QA, "ka": C_KA, "va": C_VA, "f": C_F}


def _slab_w_in(stack):
    def src_of(c):
        for name, start in _SLABS.items():
            lo, width = _REF_COLS[name]
            if start <= c < start + width:
                return divmod(lo + c - start, W_IN_SH)
        return None

    return _place_cols([(stack, j) for j in range(N_DEV)], src_of, PROJ_W, "place_w_in")


def kernel(x, c, w_ada, b_ada, g_mix, w_in, b_fgate, w_br_a, w_br_b, w_out, g_ffn, w_ffn_gate, w_ffn_up, w_ffn_down, g_final, loss_target, m_w_ada, m_b_ada, m_g_mix, m_w_in, m_b_fgate, m_w_br_a, m_w_br_b, m_w_out, m_g_ffn, m_w_ffn_gate, m_w_ffn_up, m_w_ffn_down, m_g_final, v_w_ada, v_b_ada, v_g_mix, v_w_in, v_b_fgate, v_w_br_a, v_w_br_b, v_w_out, v_g_ffn, v_w_ffn_gate, v_w_ffn_up, v_w_ffn_down, v_g_final):
    px, py, pc = _position()
    dev = 4 * px + 2 * py + pc
    x2d, tgt = x[0], loss_target[0]

    c_all = _all_gather(c, "gather_c").reshape(N_DEV, D)
    ada_cols = w_ada.shape[2]
    b_shard = lax.dynamic_slice(b_ada, (0, dev * ada_cols), (1, ada_cols))
    mod_shard = _ada_fwd(c_all, w_ada[0], b_shard)
    mod_all = _all_gather(mod_shard, "gather_mod")
    modv = lax.dynamic_index_in_dim(mod_all, dev, axis=1, keepdims=False).reshape(6, D)
    h1 = _pre1(x2d, modv, g_mix)

    w_in_s = _all_gather(_pad_to(w_in[0], 1, W_IN_PAD).astype(bf16), "gather_w_in")
    gate_up = jnp.concatenate([_pad_to(w_ffn_gate[0], 1, FF_PAD), _pad_to(w_ffn_up[0], 1, FF_PAD)], axis=1)
    later = [w_br_a[0], w_br_b[0], w_out[0], gate_up, _pad_to(w_ffn_down[0], 0, FF_PAD)]
    later_state, later_token = _gather_start([t.astype(bf16) for t in later], w_in_s, "gather_rest_start")
    w_in_p = _slab_w_in(w_in_s)

    proj = _matmul(h1, w_in_p, name="mm_proj", tm=SEQ, tn=896, after=later_token)
    b_pad = jnp.pad(b_fgate, ((0, 0), (0, LANES - N_FOX_HEADS)))
    q_aug, k_aug, va = _fox_prep(proj, _fox_gate_fwd(proj, b_pad))
    ya_h, max_a, sum_a = _fox_fwd(q_aug, k_aug, va)

    tables = _rope_tables()
    qb_r, kb_r = _rope_fwd(proj, tables)
    by_group = [_dil_fwd(qb_r, kb_r, proj, grp) for grp in range(N_GROUPS)]
    yb_h, lse_b = _dil_combine([o for o, _ in by_group], [l for _, l in by_group])

    both_done = ya_h[:8, :LANES] + yb_h[:8, :LANES]
    mine, arrived = _gather_wait(later_state, both_done, "gather_rest_wait")
    w_a_s, w_b_s, w_o_s, w_gu_s, w_d_s = [
        lax.dynamic_update_slice(stack, block[None], (dev, 0, 0))
        for stack, block in zip(_gather_finish(arrived, "gather_rest_finish"), mine, strict=True)]
    w_o = w_o_s.reshape(D, D)
    w_d = w_d_s.reshape(FF_HID, D)
    ya = _matmul_stack(ya_h, w_a_s, name="mm_br_a")
    yb = _matmul_stack(yb_h, w_b_s, name="mm_br_b")

    merged, mix, x1, h2 = _post1(ya, yb, proj, w_o, x2d, modv, g_ffn)
    act, au = _ffn_in(h2, w_gu_s)

    dx2, dff, dg_final, dga_f, loss_lanes = _final(act, w_d, x1, tgt, modv, g_final.reshape(1, D))
    dau = _ffn_bwd_in(dff, w_d_s, au)

    core = pc.astype(jnp.int32).reshape(1)
    chip = (2 * px + py).astype(jnp.int32).reshape(1)

    def pair_done(state, after, tags, name):
        mine, theirs = _exchange_wait("pair", state, after, "pair_wait_" + name)
        sums = [_pair_add(g, r, core, "pair_add_" + t) for g, r, t in zip(mine, theirs, tags)]
        return _exchange_start("chip", sums, "chip_start_" + name)

    def from_chips(state, after, tags, name):
        sums, got = _exchange_wait("chip", state, after, "chip_wait_" + name)
        return [_chip_add(p, r, chip, "chip_add_" + t) for p, r, t in zip(sums, got, tags)]

    g_gu = _matmul(h2, dau, ta=True, by_shard=True, out_dtype=bf16, name="mm_g_ffn_in", tm=D, tn=2 * FF_PAD)
    g_d = _matmul(act, dff, ta=True, out_dtype=bf16, name="mm_g_down", tm=FF_HID // 2, tn=512)
    ffn_tags = ["gu", "down"]
    ffn_pair, ffn_pair_token = _exchange_start("pair", [g_gu, g_d.reshape(N_DEV, FF_PAD, D)], "pair_start_ffn")

    dx1, dmix, dsh_f, dsc_f, dg_ffn, dga_m = _mid_bwd(dau, w_gu_s, ffn_pair_token, x1, dx2, mix, modv, g_ffn)
    ffn_state, ffn_token = pair_done(ffn_pair, dx1, ffn_tags, "ffn")
    dya, dyb, dga, dgb = _merge_bwd(dmix, w_o, ffn_token, ya, yb, proj)
    dya_h = _matmul_stack(dya, w_a_s, tb=True, name="mm_d_ya")
    dyb_h = _matmul_stack(dyb, w_b_s, tb=True, name="mm_d_yb")

    g_o = _matmul(merged, dmix, ta=True, out_dtype=bf16, name="mm_g_out", tm=D, tn=512)
    g_a = _matmul_stack(ya_h, dya, ta=True, out_dtype=bf16, name="mm_g_br_a")
    g_b = _matmul_stack(yb_h, dyb, ta=True, out_dtype=bf16, name="mm_g_br_b")
    rows_a, rows_b = FOX_W * W_BR_SH // D, DIL_OUT_W * W_BR_SH // D
    g_small = jnp.concatenate([g_a.reshape(N_DEV, rows_a, D), g_b.reshape(N_DEV, rows_b, D),
                               g_o.reshape(N_DEV, W_BR_SH, D)], axis=1)
    small_pair, small_pair_token = _exchange_start("pair", [g_small], "pair_start_small")

    dqa, dka, dva, dF = _fox_bwd(q_aug, k_aug, va, dya_h, ya_h, max_a, sum_a, small_pair_token)
    dF_row = jnp.pad(dF[:, :2, :].reshape(N_FOX_HEADS, SEQ), ((0, LANES - N_FOX_HEADS), (0, 0)))
    df, db_fgate = _fox_gate_bwd(dF_row, proj, b_pad)
    small_state, small_token = pair_done(small_pair, df, ["small"], "small")

    delta_b = _dil_delta(dyb_h, yb_h)
    dil_grads = [_dil_bwd(qb_r, kb_r, proj, dyb_h, lse_b, delta_b, grp) for grp in range(N_GROUPS)]
    dqb, dkb = _rope_bwd([t[0] for t in dil_grads], [t[1] for t in dil_grads], tables)

    dproj = _shard_pad_cols({"qa": dqa, "ka": dka, "va": dva, "f": df, "qb": dqb, "kb": dkb,
                             "vb": [t[2] for t in dil_grads], "ga": dga, "gb": dgb})
    g_in = _matmul(h1, dproj, ta=True, by_shard=True, out_dtype=bf16, name="mm_g_in", tm=D, tn=W_IN_PAD,
                   after=small_token)
    mix_tags = ["in"]
    mix_pair, mix_pair_token = _exchange_start("pair", [g_in], "pair_start_mixer")

    w = {"w_ada": w_ada, "b_ada": b_ada, "g_mix": g_mix, "w_in": w_in, "b_fgate": b_fgate, "w_br_a": w_br_a,
         "w_br_b": w_br_b, "w_out": w_out, "g_ffn": g_ffn, "w_ffn_gate": w_ffn_gate, "w_ffn_up": w_ffn_up,
         "w_ffn_down": w_ffn_down, "g_final": g_final}
    m = {"w_ada": m_w_ada, "b_ada": m_b_ada, "g_mix": m_g_mix, "w_in": m_w_in, "b_fgate": m_b_fgate,
         "w_br_a": m_w_br_a, "w_br_b": m_w_br_b, "w_out": m_w_out, "g_ffn": m_g_ffn, "w_ffn_gate": m_w_ffn_gate,
         "w_ffn_up": m_w_ffn_up, "w_ffn_down": m_w_ffn_down, "g_final": m_g_final}
    v = {"w_ada": v_w_ada, "b_ada": v_b_ada, "g_mix": v_g_mix, "w_in": v_w_in, "b_fgate": v_b_fgate,
         "w_br_a": v_w_br_a, "w_br_b": v_w_br_b, "w_out": v_w_out, "g_ffn": v_g_ffn, "w_ffn_gate": v_w_ffn_gate,
         "w_ffn_up": v_w_ffn_up, "w_ffn_down": v_w_ffn_down, "g_final": v_g_final}
    names = list(w)
    g, delta, new_m, new_v = {}, {}, {}, {}

    transposed = ("w_ffn_gate", "w_ffn_up")
    by_column = lambda t: jnp.transpose(t, (2, 0, 1))
    by_row = lambda t: jnp.transpose(t, (1, 2, 0))

    def update(n):
        shape = w[n].shape
        if n == "w_in":
            g3 = g[n]
            dl, mn, vn = _adamw_by_planes(by_column(w[n]), g3, by_column(m[n]), by_column(v[n]), "adamw_" + n)
            g[n], delta[n], new_m[n], new_v[n] = by_row(g3), by_row(dl), by_row(mn), by_row(vn)
            return
        if n in transposed:
            g_t = g[n]
            dl, mn, vn = _adamw(w[n][0].T, g_t, m[n][0].T, v[n][0].T, "adamw_" + n)
            g[n], delta[n], new_m[n], new_v[n] = g_t.T[None], dl.T[None], mn.T[None], vn.T[None]
            return
        two_d = (lambda t: t.reshape(shape[-2:])) if len(shape) == 3 else (lambda t: t)
        dl, mn, vn = _adamw(two_d(w[n]), two_d(g[n]), two_d(m[n]), two_d(v[n]), "adamw_" + n)
        delta[n], new_m[n], new_v[n] = dl.reshape(shape), mn.reshape(shape), vn.reshape(shape)

    def update_all(grads):
        g.update(grads)
        for n in grads:
            update(n)
        return sum(delta[n][(0,) * (delta[n].ndim - 1)][:N_FOX_HEADS] for n in grads)

    ffn_sums, ffn_got = _exchange_wait("chip", ffn_state, mix_pair_token, "chip_wait_ffn")
    g_gate_t, g_up_t = _chip_add(ffn_sums[0], ffn_got[0], chip, "chip_add_gu",
                                 pieces=[(0, W_FF_SH), (FF_PAD, W_FF_SH)])
    gate_up_done = update_all({"w_ffn_gate": g_gate_t, "w_ffn_up": g_up_t})
    mix_state, mix_token = pair_done(mix_pair, gate_up_done, mix_tags, "mixer")

    grad_x, dsh_m, dsc_m, dg_mix = _first_bwd(dproj, w_in_s, mix_token, x2d, dx1, modv, g_mix)

    s_d = _chip_add(ffn_sums[1], ffn_got[1], chip, "chip_add_down", after=grad_x)
    s_small, = from_chips(small_state, grad_x, ["small"], "small")
    sharded_done = update_all({
        "w_ffn_down": s_d[None, :W_FF_SH],
        "w_br_a": s_small[:rows_a].reshape(1, FOX_W, W_BR_SH),
        "w_br_b": s_small[rows_a:rows_a + rows_b].reshape(1, DIL_OUT_W, W_BR_SH), "w_out": s_small[None, rows_a + rows_b:],
    })

    pad_lane = lambda t: jnp.pad(t, ((0, 0), (0, D - t.shape[1])))
    small = jnp.concatenate([dsh_m, dsc_m, dga_m, dsh_f, dsc_f, dga_f, dg_mix, dg_ffn, dg_final,
                             pad_lane(db_fgate), loss_lanes, jnp.zeros((SMALL_ROWS - 11, D), f32)], axis=0)
    small_all = _all_gather(small, "gather_small", after=sharded_done)
    small_sum, loss_row = _small_reduce(small_all, mix_token)
    loss = loss_row[0, 0]
    dmod_all = small_all[:, :6, :].reshape(N_DEV, 6 * D)
    g_w_ada = _ada_bwd(c_all, lax.dynamic_slice(dmod_all, (0, dev * ada_cols), (N_DEV, ada_cols)))
    done = update_all({
        "w_ada": g_w_ada[None], "b_ada": small_sum[0:6].reshape(1, 6 * D), "g_mix": small_sum[6:7],
        "b_fgate": small_sum[9:10, :N_FOX_HEADS], "g_ffn": small_sum[7:8], "g_final": small_sum[8],
    })
    mix_sums, mix_got = _exchange_wait("chip", mix_state, done, "chip_wait_mixer")
    g["w_in"] = _chip_add(mix_sums[0], mix_got[0], chip, "chip_add_in", planes=W_IN_SH)
    update("w_in")

    return (loss, grad_x[None], *[g[n] for n in names], *[delta[n] for n in names],
            *[new_m[n] for n in names], *[new_v[n] for n in names])
```

```python
import jax
import jax.numpy as jnp
import numpy as np
from jax import lax
from jax.experimental import pallas as pl
from jax.experimental.pallas import tpu as pltpu

f32 = jnp.float32
bf16 = jnp.bfloat16
SDS = jax.ShapeDtypeStruct
MESH = pl.DeviceIdType.MESH

N_DEV = 8
D = 1024
SEQ = 2048
HEAD_DIM = 64
N_FOX_HEADS = 8
FOX_W = 512
DIL_W = 768
DIL_OUT_W = 256
ROT_DIM = 16
ROPE_THETA = 500000.0
D_FF = 2816
IN_COLS = 5896
EPS = 1e-6
NEG = -1e30
ATT_SCALE = HEAD_DIM ** -0.5

ADAM_LR = 0.001
ADAM_B1 = 0.9
ADAM_B2 = 0.999
ADAM_EPS = 1e-08
ADAM_WD = 0.01
ADAM_STEP = 10

C_GA, C_GB, C_QB, C_KB, C_VB, C_QA, C_KA, C_VA, C_F = 0, 1024, 2304, 3072, 3840, 4608, 5120, 5632, 6144
PROJ_W = 6272
LANES = 128
SUBLANES = 8
BIG_UPDATE = 256 * 1024
VMEM_LIMIT = 52 * 1024 * 1024

W_IN_SH, W_IN_PAD = IN_COLS // N_DEV, 768
W_BR_SH = D // N_DEV
W_FF_SH, FF_PAD = D_FF // N_DEV, 384
FF_HID = N_DEV * FF_PAD
SMALL_ROWS = 16


def _params(sem=None):
    if sem is None:
        return pltpu.CompilerParams(vmem_limit_bytes=VMEM_LIMIT)
    return pltpu.CompilerParams(dimension_semantics=sem, vmem_limit_bytes=VMEM_LIMIT)


def _rowwise(fn, name, tiled, vecs, outs, reds=(), tile=256):
    nt, nv, no = len(tiled), len(vecs), len(outs)
    rows = tiled[0][0].shape[0]
    assert rows % tile == 0

    def body(*refs):
        tin = [r[...] for r in refs[:nt]]
        vin = [r[...] for r in refs[nt:nt + nv]]
        orefs = refs[nt + nv:nt + nv + no]
        rrefs = refs[nt + nv + no:]
        touts, routs = fn(tin, vin)
        for r, t in zip(orefs, touts, strict=True):
            r[...] = t.astype(r.dtype)
        if rrefs:
            @pl.when(pl.program_id(0) == 0)
            def _():
                for r in rrefs:
                    r[...] = jnp.zeros_like(r)
            for r, t in zip(rrefs, routs, strict=True):
                r[...] += t

    def col_map(cb):
        return lambda i: (i, cb)

    def whole_map(nd):
        return lambda i: (0,) * nd

    in_specs = [pl.BlockSpec((tile, w), col_map(cb)) for (_, w, cb) in tiled]
    in_specs += [pl.BlockSpec(v.shape, whole_map(v.ndim)) for v in vecs]
    out_specs = [pl.BlockSpec((tile, w), lambda i: (i, 0)) for (w, _) in outs]
    out_specs += [pl.BlockSpec((1, w), lambda i: (0, 0)) for w in reds]
    out_shape = [SDS((rows, w), dt) for (w, dt) in outs] + [SDS((1, w), f32) for w in reds]
    res = pl.pallas_call(
        body, grid=(rows // tile,), in_specs=in_specs, out_specs=out_specs, out_shape=out_shape, name=name,
        compiler_params=_params(("arbitrary",)),
    )(*[t[0] for t in tiled], *vecs)
    return res


def _matmul(a, b, *, ta=False, out_dtype=f32, name, tm, tn, by_shard=False, after=None):
    (m, k), n = ((a.shape[1], a.shape[0]) if ta else a.shape), b.shape[1]
    assert b.shape[0] == k and m % tm == 0 and n % tn == 0 and (ta or not by_shard)
    dims = (((0 if ta else 1,), (0,)), ((), ()))

    def body(a_ref, b_ref, *rest):
        p = lax.dot_general(a_ref[...].astype(bf16), b_ref[...].astype(bf16), dims, preferred_element_type=f32)
        o_ref = rest[-1]
        if by_shard:
            o_ref[0] = p.astype(o_ref.dtype)
        else:
            o_ref[...] = p.astype(o_ref.dtype)

    a_spec = pl.BlockSpec((k, tm), lambda i, j: (0, i)) if ta else pl.BlockSpec((tm, k), lambda i, j: (i, 0))
    if by_shard:
        assert tn == n // N_DEV
        out_spec, out_shape = pl.BlockSpec((1, tm, tn), lambda i, j: (j, i, 0)), SDS((N_DEV, m, tn), out_dtype)
    else:
        out_spec, out_shape = pl.BlockSpec((tm, tn), lambda i, j: (i, j)), SDS((m, n), out_dtype)
    extra_specs, extra = ([pl.BlockSpec(memory_space=pl.ANY)], [after]) if after is not None else ([], [])
    return pl.pallas_call(
        body, grid=(m // tm, n // tn), in_specs=[a_spec, pl.BlockSpec((k, tn), lambda i, j: (0, j))] + extra_specs,
        out_specs=out_spec, out_shape=out_shape, name=name, compiler_params=_params(("parallel", "parallel")),
    )(a, b, *extra)


def _matmul_stack(a, b, *, ta=False, tb=False, out_dtype=f32, name):
    def lanes(ref):
        return jnp.concatenate([ref[j] for j in range(N_DEV)], axis=1).astype(bf16)

    if ta:
        w = b.shape[1] // N_DEV

        def body(a_ref, b_ref, o_ref):
            p = _tn(a_ref[...].astype(bf16), b_ref[...].astype(bf16))
            for j in range(N_DEV):
                o_ref[j] = p[:, j * w:(j + 1) * w].astype(o_ref.dtype)

        return pl.pallas_call(body, out_shape=SDS((N_DEV, a.shape[1], w), out_dtype), name=name,
                              compiler_params=_params())(a, b)

    m, half = a.shape[0], a.shape[0] // 2
    n = b.shape[1] if tb else N_DEV * b.shape[2]

    def body(a_ref, b_ref, o_ref):
        av = a_ref[...].astype(bf16)
        o_ref[...] = (_nt(av, lanes(b_ref)) if tb else jnp.dot(av, lanes(b_ref), preferred_element_type=f32)
                      ).astype(o_ref.dtype)

    return pl.pallas_call(
        body, grid=(2,), in_specs=[pl.BlockSpec((half, a.shape[1]), lambda i: (i, 0)),
                                   pl.BlockSpec(b.shape, lambda i: (0, 0, 0))],
        out_specs=pl.BlockSpec((half, n), lambda i: (i, 0)), out_shape=SDS((m, n), out_dtype), name=name,
        compiler_params=_params(("parallel",)),
    )(a, b)


def _matmul_rows(form, a, b, after, fn, tiled, vecs, outs, reds, *, name, tm=512):
    norm = lambda ts: [t if isinstance(t, tuple) else (t, t.shape[1], 0) for t in ts]
    make, sources = a if isinstance(a, tuple) else (None, [a])
    sources, tiled = norm(sources), norm(tiled)
    m, k = sources[0][0].shape[0], (b.shape[0] if form == "nn" else b.shape[-1] * (N_DEV if form == "nt_stack" else 1))
    assert m % tm == 0
    ns, nt, nv, no = len(sources), len(tiled), len(vecs), len(outs)

    def body(*refs):
        src_refs, b_ref, refs = refs[:ns], refs[ns], refs[ns + 2:]
        if make is None:
            lhs = lambda lo, hi: src_refs[0][:, lo:hi]
        else:
            made = make([r[...] for r in src_refs]).astype(bf16)
            lhs = lambda lo, hi: made[:, lo:hi]
        if form == "nt_stack":
            w = b.shape[2]
            acc = _nt(lhs(0, w), b_ref[0])
            for j in range(1, N_DEV):
                acc = acc + _nt(lhs(j * w, (j + 1) * w), b_ref[j])
        elif form == "nt":
            acc = _nt(lhs(0, k), b_ref[...])
        else:
            acc = jnp.dot(lhs(0, k), b_ref[...], preferred_element_type=f32)
        if make is not None:
            refs[nt + nv][...] = made
            refs = refs[:nt + nv] + refs[nt + nv + 1:]
        orefs, rrefs = refs[nt + nv:nt + nv + no], refs[nt + nv + no:]
        touts, routs = fn([acc] + [r[...] for r in refs[:nt]], [r[...] for r in refs[nt:nt + nv]])
        for r, t in zip(orefs, touts, strict=True):
            r[...] = t.astype(r.dtype)

        @pl.when(pl.program_id(0) == 0)
        def _():
            for r in rrefs:
                r[...] = jnp.zeros_like(r)
        for r, t in zip(rrefs, routs, strict=True):
            r[...] += t

    def whole_map(nd):
        return lambda i: (0,) * nd

    def rows(width, cb=0):
        return pl.BlockSpec((tm, width), lambda i: (i, cb))

    made_out = [(k, bf16)] if make is not None else []
    return pl.pallas_call(
        body, grid=(m // tm,),
        in_specs=[rows(width, cb) for _, width, cb in sources]
        + [pl.BlockSpec(b.shape, whole_map(b.ndim), pipeline_mode=pl.Buffered(1)), pl.BlockSpec(memory_space=pl.ANY)]
        + [rows(width, cb) for _, width, cb in tiled] + [pl.BlockSpec(v.shape, whole_map(v.ndim)) for v in vecs],
        out_specs=[rows(width) for width, _ in made_out + list(outs)]
        + [pl.BlockSpec((1, width), lambda i: (0, 0)) for width in reds],
        out_shape=[SDS((m, width), dt) for width, dt in made_out + list(outs)]
        + [SDS((1, width), f32) for width in reds], name=name,
        compiler_params=_params(("arbitrary",)),
    )(*[t[0] for t in sources], b, after, *[t[0] for t in tiled], *vecs)


def _rms(x):
    r = lax.rsqrt(jnp.mean(x * x, axis=-1, keepdims=True) + EPS)
    return r, x * r


def _rms_bwd(r, xn, dxn):
    return r * (dxn - xn * jnp.mean(dxn * xn, axis=-1, keepdims=True))


def _colsum(t):
    return jnp.sum(t, axis=0, keepdims=True)


def _sigmoid(x):
    return 0.5 * jnp.tanh(0.5 * x) + 0.5


def _modulated_norm(x, g, shift, scale):
    _, xn = _rms(x)
    return (xn * g) * (1.0 + scale) + shift


def _pre1(x, modv, g_mix):
    def fn(t, v):
        (xt,), (mv, g) = t, v
        return [_modulated_norm(xt, g, mv[0:1], mv[1:2])], []
    return _rowwise(fn, "pre1", [(x, D, 0)], [modv, g_mix], [(D, bf16)])[0]


def _post1(ya, yb, proj, w_o, x, modv, g_ffn):
    def merge(t):
        ya_t, yb_t, ga, gb = t
        return _sigmoid(ga) * ya_t + _sigmoid(gb) * yb_t

    def fn(t, v):
        (mt, xt), (mv, g) = t, v
        x1 = xt + mv[2:3] * mt
        return [mt, x1, _modulated_norm(x1, g, mv[3:4], mv[4:5])], []
    return _matmul_rows("nn", (merge, [ya, yb, (proj, D, C_GA // D), (proj, D, C_GB // D)]), w_o, x, fn, [x],
                        [modv, g_ffn], [(D, f32), (D, f32), (D, bf16)], [], name="post1")


def _ffn_in(h, w_stack):
    def body(h_ref, w_ref, act_ref, au_ref):
        p = jnp.dot(h_ref[...], w_ref[0], preferred_element_type=f32)
        a, u = p[:, :FF_PAD], p[:, FF_PAD:]
        act_ref[...] = (a * _sigmoid(a) * u).astype(act_ref.dtype)
        au_ref[...] = p.astype(au_ref.dtype)

    return pl.pallas_call(
        body, grid=(N_DEV,),
        in_specs=[pl.BlockSpec((SEQ, D), lambda j: (0, 0)), pl.BlockSpec((1, D, 2 * FF_PAD), lambda j: (j, 0, 0))],
        out_specs=[pl.BlockSpec((SEQ, FF_PAD), lambda j: (0, j)), pl.BlockSpec((SEQ, 2 * FF_PAD), lambda j: (0, j))],
        out_shape=(SDS((SEQ, FF_HID), bf16), SDS((SEQ, 2 * FF_HID), bf16)), name="ffn_in",
        compiler_params=_params(("parallel",)),
    )(h, w_stack)


def _ffn_bwd_in(dff, w_down_stack, au):
    def body(d_ref, w_ref, au_ref, o_ref):
        dact = _nt(d_ref[...], w_ref[0])
        p = au_ref[...].astype(f32)
        a, u = p[:, :FF_PAD], p[:, FF_PAD:]
        sg = _sigmoid(a)
        o_ref[...] = jnp.concatenate([dact * u * (sg * (1.0 + a * (1.0 - sg))), dact * (a * sg)],
                                     axis=1).astype(o_ref.dtype)

    return pl.pallas_call(
        body, grid=(N_DEV,),
        in_specs=[pl.BlockSpec((SEQ, D), lambda j: (0, 0)), pl.BlockSpec((1, FF_PAD, D), lambda j: (j, 0, 0)),
                  pl.BlockSpec((SEQ, 2 * FF_PAD), lambda j: (0, j))],
        out_specs=pl.BlockSpec((SEQ, 2 * FF_PAD), lambda j: (0, j)),
        out_shape=SDS((SEQ, 2 * FF_HID), bf16), name="ffn_bwd_in", compiler_params=_params(("parallel",)),
    )(dff, w_down_stack, au)


def _final(act, w_down, x1, target, modv, g_final):
    def fn(t, v):
        (fft, x1t, tgt), (mv, g) = t, v
        x2 = x1t + mv[5:6] * fft
        r, xn = _rms(x2)
        err = xn * g - tgt
        dy = err * (1.0 / D)
        dx2 = _rms_bwd(r, xn, dy * g)
        return [dx2, dx2 * mv[5:6]], [_colsum(dy * xn), _colsum(dx2 * fft), _colsum(err * err) * (0.5 / D)]
    return _matmul_rows("nn", act, w_down, x1, fn, [x1, target], [modv, g_final], [(D, f32), (D, bf16)], [D, D, D],
                        name="final")


def _mid_bwd(dau, w_stack, after, x1, dx2, mix, modv, g_ffn):
    def fn(t, v):
        (dh, x1t, dx2t, mt), (mv, g) = t, v
        r, xn = _rms(x1t)
        dn = dh * (1.0 + mv[4:5])
        dx1 = dx2t + _rms_bwd(r, xn, dn * g)
        return [dx1, dx1 * mv[2:3]], [_colsum(dh), _colsum(dh * (xn * g)), _colsum(dn * xn), _colsum(dx1 * mt)]
    return _matmul_rows("nt_stack", dau, w_stack, after, fn, [x1, dx2, mix], [modv, g_ffn], [(D, f32), (D, bf16)],
                        [D, D, D, D], name="mid_bwd")


def _first_bwd(dproj, w_stack, after, x, dx1, modv, g_mix):
    def fn(t, v):
        (dh, xt, dx1t), (mv, g) = t, v
        r, xn = _rms(xt)
        dn = dh * (1.0 + mv[1:2])
        return [dx1t + _rms_bwd(r, xn, dn * g)], [_colsum(dh), _colsum(dh * (xn * g)), _colsum(dn * xn)]
    return _matmul_rows("nt_stack", dproj, w_stack, after, fn, [x, dx1], [modv, g_mix], [(D, f32)], [D, D, D],
                        name="first_bwd")


def _merge_bwd(dmix, w_o, after, ya, yb, proj):
    def fn(t, v):
        dm, ya_t, yb_t, ga, gb = t
        sa, sb = _sigmoid(ga), _sigmoid(gb)
        return [dm * sa, dm * sb, dm * ya_t * (sa * (1.0 - sa)), dm * yb_t * (sb * (1.0 - sb))], []
    return _matmul_rows("nt", dmix, w_o, after, fn, [ya, yb, (proj, D, C_GA // D), (proj, D, C_GB // D)], [],
                        [(D, bf16), (D, bf16), (D, bf16), (D, bf16)], [], name="merge_bwd")


def _rope_tables():
    half = ROT_DIM // 2
    pos = np.arange(SEQ, dtype=np.float32)
    inv_freq = np.float32(ROPE_THETA) ** (-np.arange(0, ROT_DIM, 2, dtype=np.float32) / np.float32(ROT_DIM))
    ang = pos[:, None] * inv_freq[None, :].astype(np.float32)
    cos, sin = np.cos(ang).astype(np.float32), np.sin(ang).astype(np.float32)
    pad = np.zeros((SEQ, HEAD_DIM - ROT_DIM), np.float32)
    zero = np.zeros((SEQ, half), np.float32)
    c_head = np.concatenate([cos, cos, pad + 1.0], axis=1)
    lo_head = np.concatenate([-sin, zero, pad], axis=1)
    hi_head = np.concatenate([zero, sin, pad], axis=1)
    return tuple(jnp.asarray(np.concatenate([t, t], axis=1)) for t in (c_head, lo_head, hi_head))


def _over_heads(tables):
    return [jnp.tile(t, (1, DIL_W // LANES)) for t in tables]


def _rope_fwd(proj, tables):
    half = ROT_DIM // 2

    def fn(t, v):
        q, k = t[:2]
        c, lo, hi = _over_heads(t[2:])
        rot = lambda z: z * c + pltpu.roll(z, DIL_W - half, 1) * lo + pltpu.roll(z, half, 1) * hi
        return [rot(q) * ATT_SCALE, rot(k)], []
    return _rowwise(fn, "rope_fwd", [(proj, DIL_W, C_QB // DIL_W), (proj, DIL_W, C_KB // DIL_W)]
                    + [(tb, LANES, 0) for tb in tables], [], [(DIL_W, f32)] * 2)


def _rope_bwd(dqs, dks, tables):
    half = ROT_DIM // 2

    def fn(t, v):
        dq_t, dk_t = jnp.concatenate(t[:N_GROUPS], axis=1), jnp.concatenate(t[N_GROUPS:2 * N_GROUPS], axis=1)
        c, lo, hi = _over_heads(t[2 * N_GROUPS:])
        rot_t = lambda z: z * c + pltpu.roll(z * lo, half, 1) + pltpu.roll(z * hi, DIL_W - half, 1)
        return [rot_t(dq_t), rot_t(dk_t)], []
    return _rowwise(fn, "rope_bwd", [(a, DIL_OUT_W, 0) for a in (*dqs, *dks)] + [(tb, LANES, 0) for tb in tables],
                    [], [(DIL_W, bf16), (DIL_W, bf16)])


def _head_bcast_sum(d):
    lane = lax.broadcasted_iota(jnp.int32, d.shape, 1)
    out = jnp.zeros_like(d)
    for h in range(d.shape[1] // HEAD_DIM):
        sel = (lane >= h * HEAD_DIM) & (lane < (h + 1) * HEAD_DIM)
        out = jnp.where(sel, jnp.sum(jnp.where(sel, d, 0.0), axis=1, keepdims=True), out)
    return out


def _dil_combine(outs, lses):
    def fn(t, v):
        o0, o1, o2, l0, l1, l2 = t
        m = jnp.maximum(jnp.maximum(l0, l1), l2)
        w0, w1, w2 = jnp.exp(l0 - m), jnp.exp(l1 - m), jnp.exp(l2 - m)
        tot = w0 + w1 + w2
        return [(w0 * o0 + w1 * o1 + w2 * o2) / tot, m + jnp.log(tot)], []
    w = DIL_OUT_W
    return _rowwise(fn, "dil_combine", [(t, w, 0) for t in (*outs, *lses)], [], [(w, f32), (w, f32)])


def _dil_delta(dyb_h, yb_h):
    def fn(t, v):
        return [_head_bcast_sum(t[0] * t[1])], []
    return _rowwise(fn, "dil_delta", [(dyb_h, DIL_OUT_W, 0), (yb_h, DIL_OUT_W, 0)], [], [(DIL_OUT_W, f32)])[0]


def _adamw_math(wt, gt, mt, vt):
    mn = ADAM_B1 * mt + (1.0 - ADAM_B1) * gt
    vn = ADAM_B2 * vt + (1.0 - ADAM_B2) * (gt * gt)
    m_hat = mn / (1.0 - ADAM_B1 ** ADAM_STEP)
    v_hat = vn / (1.0 - ADAM_B2 ** ADAM_STEP)
    return -ADAM_LR * (m_hat / (jnp.sqrt(v_hat) + ADAM_EPS) + ADAM_WD * wt), mn, vn


def _adamw(w, g, m, v, name):
    shape = w.shape
    if w.ndim == 1:
        w, g, m, v = (t.reshape(1, -1) for t in (w, g, m, v))
    rows, cols = w.shape
    if rows % (4 * SUBLANES) == 0 and rows * cols >= BIG_UPDATE:
        return _adamw_by_planes(w, g, m, v, name)

    def fn(t, _):
        return list(_adamw_math(*t)), []
    delta, mn, vn = _rowwise(fn, name, [(w, cols, 0), (g, cols, 0), (m, cols, 0), (v, cols, 0)], [],
                             [(cols, f32)] * 3, tile=rows)
    return delta.reshape(shape), mn.reshape(shape), vn.reshape(shape)


def _adamw_by_planes(w, g, m, v, name, chunks=4):
    planes = w.shape[0]
    bounds = [planes * k // chunks for k in range(chunks + 1)]
    spans = list(zip(bounds[:-1], bounds[1:]))

    def body(w_ref, g_ref, m_ref, v_ref, d_ref, mn_ref, vn_ref, *scratch):
        load_sems, store_sems = scratch[-2:]
        loads, stores, bufs = [], [], []
        for k, (lo, hi) in enumerate(spans):
            rows = pl.ds(lo, hi - lo)
            wb, gb, mb, vb = bufs_k = scratch[4 * k:4 * k + 4]
            bufs.append(bufs_k)
            loads.append([pltpu.make_async_copy(src.at[rows], buf, load_sems.at[4 * k + j])
                          for j, (src, buf) in enumerate(zip((w_ref, g_ref, m_ref, v_ref), bufs_k))])
            stores.append([pltpu.make_async_copy(buf, dst.at[rows], store_sems.at[3 * k + j])
                           for j, (buf, dst) in enumerate(zip((wb, mb, vb), (d_ref, mn_ref, vn_ref)))])
        for cp in sum(loads, []):
            cp.start()
        for k in range(chunks):
            for cp in loads[k]:
                cp.wait()
            wb, gb, mb, vb = bufs[k]
            wb[...], mb[...], vb[...] = _adamw_math(wb[...], gb[...], mb[...], vb[...])
            for cp in stores[k]:
                cp.start()
        for cp in sum(stores, []):
            cp.wait()

    hbm = pl.BlockSpec(memory_space=pl.ANY)
    scratch = [pltpu.VMEM((hi - lo,) + w.shape[1:], f32) for lo, hi in spans for _ in range(4)]
    scratch += [pltpu.SemaphoreType.DMA((4 * chunks,)), pltpu.SemaphoreType.DMA((3 * chunks,))]
    return pl.pallas_call(body, in_specs=[hbm] * 4, out_specs=[hbm] * 3, out_shape=[SDS(w.shape, f32)] * 3,
                          scratch_shapes=scratch, name=name, compiler_params=_params())(w, g, m, v)


def _ada_fwd(c_all, w_shard, b_shard):
    def body(c_ref, w_ref, b_ref, o_ref):
        cv = c_ref[...]
        sc = (cv * _sigmoid(cv)).astype(bf16)
        o_ref[...] = jnp.dot(sc, w_ref[...].astype(bf16), preferred_element_type=f32) + b_ref[...]
    return pl.pallas_call(body, out_shape=SDS((N_DEV, w_shard.shape[1]), f32), name="ada_fwd",
                          compiler_params=_params())(c_all, w_shard, b_shard)


def _ada_bwd(c_all, dmod_cols):
    def body(c_ref, d_ref, o_ref):
        cv = c_ref[...]
        sc = cv * _sigmoid(cv)
        o_ref[...] = lax.dot_general(sc, d_ref[...], (((0,), (0,)), ((), ())), precision=lax.Precision.HIGHEST,
                                     preferred_element_type=f32)
    return pl.pallas_call(body, out_shape=SDS((D, dmod_cols.shape[1]), f32), name="ada_bwd",
                          compiler_params=_params())(c_all, dmod_cols)


def _small_reduce(gathered, after):
    def body(g_ref, after_ref, o_ref, loss_ref):
        acc = g_ref[0]
        for d in range(1, N_DEV):
            acc = acc + g_ref[d]
        o_ref[...] = acc
        loss_ref[...] = jnp.zeros((1, LANES), f32) + jnp.sum(acc[10:11, :])
    return pl.pallas_call(body, out_shape=(SDS((SMALL_ROWS, D), f32), SDS((1, LANES), f32)), name="small_reduce",
                          in_specs=[pl.BlockSpec(memory_space=pltpu.VMEM), pl.BlockSpec(memory_space=pl.ANY)],
                          compiler_params=_params())(gathered, after)


FOX_BLK = 512
CUM_BLK = 128


def _fold_lanes(t, op):
    out = t[:, :LANES]
    for j in range(1, t.shape[1] // LANES):
        out = op(out, t[:, j * LANES:(j + 1) * LANES])
    return out


def _fox_gate_fwd(proj, b_pad):
    nblk = SEQ // CUM_BLK

    def body(f_ref, b_ref, col_ref):
        r = lax.broadcasted_iota(jnp.int32, (CUM_BLK, CUM_BLK), 0)
        c = lax.broadcasted_iota(jnp.int32, (CUM_BLK, CUM_BLK), 1)
        tri = (r >= c).astype(f32)
        carry = jnp.zeros((1, LANES), f32)
        for blk in range(nblk):
            z = f_ref[blk * CUM_BLK:(blk + 1) * CUM_BLK, :] + b_ref[...]
            logf = jnp.minimum(z, 0.0) - jnp.log1p(jnp.exp(-jnp.abs(z)))
            cs = jnp.dot(tri, logf, precision=lax.Precision.HIGHEST, preferred_element_type=f32) + carry
            col_ref[blk * CUM_BLK:(blk + 1) * CUM_BLK, :] = cs
            carry = cs[CUM_BLK - 1:CUM_BLK, :]

    return pl.pallas_call(
        body, grid=(1,), in_specs=[pl.BlockSpec((SEQ, LANES), lambda i: (0, C_F // LANES)),
                                   pl.BlockSpec((1, LANES), lambda i: (0, 0))],
        out_specs=pl.BlockSpec((SEQ, LANES), lambda i: (0, 0)),
        out_shape=SDS((SEQ, LANES), f32), name="fox_gate_fwd",
        compiler_params=_params(("arbitrary",)),
    )(proj, b_pad)


def _fox_gate_bwd(dF_row, proj, b_pad):
    nblk = SEQ // CUM_BLK

    def body(d_ref, f_ref, b_ref, df_ref, db_ref, col_ref):
        r = lax.broadcasted_iota(jnp.int32, (CUM_BLK, CUM_BLK), 0)
        c = lax.broadcasted_iota(jnp.int32, (CUM_BLK, CUM_BLK), 1)
        tri = (r <= c).astype(f32)
        lane = lax.broadcasted_iota(jnp.int32, (CUM_BLK, LANES), 1)
        col_ref[...] = d_ref[...].T
        carry = jnp.zeros((1, LANES), f32)
        total = jnp.zeros((1, LANES), f32)
        for blk in reversed(range(nblk)):
            rows = slice(blk * CUM_BLK, (blk + 1) * CUM_BLK)
            cs = jnp.dot(tri, col_ref[rows, :], precision=lax.Precision.HIGHEST, preferred_element_type=f32) + carry
            carry = cs[0:1, :]
            z = f_ref[rows, :] + b_ref[...]
            df = jnp.where(lane < N_FOX_HEADS, cs * _sigmoid(-z), 0.0)
            df_ref[rows, :] = df.astype(df_ref.dtype)
            total = total + _colsum(df)
        db_ref[...] = total

    return pl.pallas_call(
        body, grid=(1,), in_specs=[pl.BlockSpec((LANES, SEQ), lambda i: (0, 0)),
                                   pl.BlockSpec((SEQ, LANES), lambda i: (0, C_F // LANES)),
                                   pl.BlockSpec((1, LANES), lambda i: (0, 0))],
        out_specs=[pl.BlockSpec((SEQ, LANES), lambda i: (0, 0)), pl.BlockSpec((1, LANES), lambda i: (0, 0))],
        out_shape=(SDS((SEQ, LANES), bf16), SDS((1, LANES), f32)), name="fox_gate_bwd",
        scratch_shapes=[pltpu.VMEM((SEQ, LANES), f32)],
        compiler_params=_params(("arbitrary",)),
    )(dF_row, proj, b_pad)


def _nt(a, b):
    return lax.dot_general(a, b, (((1,), (1,)), ((), ())), preferred_element_type=f32)


def _tn(a, b):
    return lax.dot_general(a, b, (((0,), (0,)), ((), ())), preferred_element_type=f32)


def _fox_prep(proj, f_col):
    def fn(t, v):
        q, k, vv, fc = t
        lane = lax.broadcasted_iota(jnp.int32, (q.shape[0], LANES), 1)
        qs, ks = [], []
        for h in range(N_FOX_HEADS):
            pair, pos = divmod(h, 2)
            own = (lane >= pos * HEAD_DIM) & (lane < (pos + 1) * HEAD_DIM)
            base = (1 - pos) * HEAD_DIM
            f = fc[:, h:h + 1]
            hi = f.astype(bf16).astype(f32)
            mid = (f - hi).astype(bf16).astype(f32)
            lo = (f - hi) - mid
            one = jnp.ones_like(f)
            qa = jnp.where(own, q[:, pair * LANES:(pair + 1) * LANES] * ATT_SCALE, 0.0)
            ka = k[:, pair * LANES:(pair + 1) * LANES]
            for idx, (qv, kv) in enumerate([(hi, one), (mid, one), (lo, one), (one, -hi), (one, -mid), (one, -lo)]):
                sel = lane == base + idx
                qa = jnp.where(sel, qv, qa)
                ka = jnp.where(sel, kv, ka)
            qs.append(qa)
            ks.append(ka)
        return [jnp.concatenate(qs, axis=1), jnp.concatenate(ks, axis=1), vv], []
    w = N_FOX_HEADS * LANES
    return _rowwise(fn, "fox_prep", [(proj, FOX_W, C_QA // FOX_W), (proj, FOX_W, C_KA // FOX_W),
                                     (proj, FOX_W, C_VA // FOX_W), (f_col, LANES, 0)], [],
                    [(w, bf16), (w, bf16), (FOX_W, bf16)])


def _fox_fwd(q_aug, k_aug, v):
    blk = FOX_BLK
    npair = FOX_W // LANES

    def body(q_ref, k_ref, v_ref, o_ref, max_ref, sum_ref, s_scr):
        i = pl.program_id(1)
        tri = lax.broadcasted_iota(jnp.int32, (blk, blk), 0) >= lax.broadcasted_iota(jnp.int32, (blk, blk), 1)
        qh = [q_ref[:, h * LANES:(h + 1) * LANES] for h in range(2)]

        def logits(c, masked):
            off = pl.multiple_of(c * blk, blk)
            tops = []
            for h in range(2):
                s = _nt(qh[h], k_ref[pl.ds(off, blk), h * LANES:(h + 1) * LANES])
                if masked:
                    s = jnp.where(tri, s, NEG)
                s_scr[h, :, pl.ds(off, blk)] = s
                tops.append(_fold_lanes(s, jnp.maximum))
            return tops

        def pass_a(c, m):
            return tuple(jnp.maximum(a, b) for a, b in zip(m, logits(c, False)))

        m = lax.fori_loop(0, i, pass_a, tuple(jnp.full((blk, LANES), NEG, f32) for _ in range(2)))
        mx = [jnp.max(jnp.maximum(a, b), axis=1, keepdims=True) for a, b in zip(m, logits(i, True))]

        def pass_b(c, carry):
            off = pl.multiple_of(c * blk, blk)
            vv = v_ref[pl.ds(off, blk), :]
            new = []
            for h in range(2):
                l, acc = carry[h]
                p = jnp.exp(s_scr[h, :, pl.ds(off, blk)] - mx[h]).astype(bf16)
                new.append((l + _fold_lanes(p.astype(f32), jnp.add), acc + jnp.dot(p, vv, preferred_element_type=f32)))
            return tuple(new)

        zero = jnp.zeros((blk, LANES), f32)
        (l_a, acc_a), (l_b, acc_b) = lax.fori_loop(0, i + 1, pass_b, ((zero, zero), (zero, zero)))
        l_a = jnp.sum(l_a, axis=1, keepdims=True)
        l_b = jnp.sum(l_b, axis=1, keepdims=True)
        first = lax.broadcasted_iota(jnp.int32, (blk, LANES), 1) < HEAD_DIM
        o_ref[...] = jnp.where(first, acc_a / l_a, acc_b / l_b)
        max_ref[0] = jnp.where(first, mx[0], mx[1])
        sum_ref[0] = jnp.where(first, l_a, l_b)

    return pl.pallas_call(
        body, grid=(npair, SEQ // blk),
        in_specs=[pl.BlockSpec((blk, 2 * LANES), lambda p, i: (i, p)),
                  pl.BlockSpec((SEQ, 2 * LANES), lambda p, i: (0, p)),
                  pl.BlockSpec((SEQ, LANES), lambda p, i: (0, p))],
        out_specs=[pl.BlockSpec((blk, LANES), lambda p, i: (i, p))]
        + [pl.BlockSpec((1, blk, LANES), lambda p, i: (p, i, 0))] * 2,
        out_shape=(SDS((SEQ, FOX_W), f32),) + (SDS((npair, SEQ, LANES), f32),) * 2, name="fox_fwd",
        scratch_shapes=[pltpu.VMEM((2, blk, SEQ), f32)],
        compiler_params=_params(("parallel", "arbitrary")),
    )(q_aug, k_aug, v)


def _fox_bwd(q_aug, k_aug, v, do, o, row_max, row_sum, after):
    blk = FOX_BLK
    npair = FOX_W // LANES
    nblk = SEQ // blk

    def body(q_ref, k_ref, v_ref, do_ref, o_ref, max_ref, sum_ref, after_ref, dq_ref, dk_ref, dv_ref, df_ref, dq_acc,
             delta_ref, inv_ref):
        inv_ref[...] = 1.0 / sum_ref[0]
        lane_s = lax.broadcasted_iota(jnp.int32, (SEQ, LANES), 1)
        prod = do_ref[...].astype(bf16).astype(f32) * o_ref[...]
        d_a = jnp.sum(jnp.where(lane_s < HEAD_DIM, prod, 0.0), axis=1, keepdims=True)
        d_b = jnp.sum(jnp.where(lane_s >= HEAD_DIM, prod, 0.0), axis=1, keepdims=True)
        delta_ref[...] = jnp.where(lane_s < HEAD_DIM, d_a, d_b)
        dq_acc[...] = jnp.zeros_like(dq_acc)
        df_ref[...] = jnp.zeros_like(df_ref)
        lane = lax.broadcasted_iota(jnp.int32, (blk, LANES), 1)
        own = [lane < HEAD_DIM, lane >= HEAD_DIM]
        tri = lax.broadcasted_iota(jnp.int32, (blk, blk), 0) >= lax.broadcasted_iota(jnp.int32, (blk, blk), 1)

        def q_slab(qoff, h):
            return q_ref[pl.ds(qoff, blk), h * LANES:(h + 1) * LANES]

        def probs(qoff, h, k_h, masked):
            s = _nt(q_slab(qoff, h), k_h)
            if masked:
                s = jnp.where(tri, s, NEG)
            col = slice(h * HEAD_DIM, h * HEAD_DIM + 1)
            weights = jnp.exp(s - max_ref[0, pl.ds(qoff, blk), col]).astype(bf16).astype(f32)
            return weights * inv_ref[pl.ds(qoff, blk), col]

        def k_slabs(koff):
            return [k_ref[pl.ds(koff, blk), h * LANES:(h + 1) * LANES] for h in range(2)]

        def kv_step(kj, _):
            koff = pl.multiple_of(kj * blk, blk)
            k_aug = k_slabs(koff)
            k_own = [jnp.where(own[h], k_aug[h], jnp.zeros_like(k_aug[h])) for h in range(2)]
            vv = v_ref[pl.ds(koff, blk), :]
            v_own = [jnp.where(own[h], vv, jnp.zeros_like(vv)) for h in range(2)]

            def q_tile(qi, carry, masked):
                qoff = pl.multiple_of(qi * blk, blk)
                dd = do_ref[pl.ds(qoff, blk), :].astype(bf16)
                new, dq_add = [], None
                for h in range(2):
                    dk_h, dv_h, dcol = carry[h]
                    p = probs(qoff, h, k_aug[h], masked)
                    dl = p * (_nt(dd, v_own[h]) - delta_ref[pl.ds(qoff, blk), h * HEAD_DIM:h * HEAD_DIM + 1])
                    dlb = dl.astype(bf16)
                    part = jnp.dot(dlb, k_own[h], preferred_element_type=f32)
                    dq_add = part if dq_add is None else dq_add + part
                    new.append((dk_h + _tn(dlb, q_slab(qoff, h)), dv_h + _tn(p.astype(bf16), dd),
                                dcol + _colsum(dl)))
                dq_acc[pl.ds(qoff, blk), :] += dq_add * ATT_SCALE
                return tuple(new)

            zero = (jnp.zeros((blk, LANES), f32), jnp.zeros((blk, LANES), f32), jnp.zeros((1, blk), f32))
            carry = q_tile(kj, (zero, zero), True)
            (dk_a, dv_a, dcol_a), (dk_b, dv_b, dcol_b) = lax.fori_loop(
                kj + 1, nblk, lambda qi, cr: q_tile(qi, cr, False), carry)
            dk_ref[pl.ds(koff, blk), :] = jnp.where(own[0], dk_a, dk_b).astype(dk_ref.dtype)
            dv_ref[pl.ds(koff, blk), :] = jnp.where(own[0], dv_a, dv_b).astype(dv_ref.dtype)
            df_ref[0, 0:1, pl.ds(koff, blk)] = -dcol_a
            df_ref[0, 1:2, pl.ds(koff, blk)] = -dcol_b
            return 0

        lax.fori_loop(0, nblk, kv_step, 0)
        dq_ref[...] = dq_acc[...].astype(dq_ref.dtype)

    pair_aug = pl.BlockSpec((SEQ, 2 * LANES), lambda p: (0, p))
    slab = pl.BlockSpec((SEQ, LANES), lambda p: (0, p))
    per_pair = pl.BlockSpec((1, SEQ, LANES), lambda p: (p, 0, 0))
    rows = pl.BlockSpec((1, 8, SEQ), lambda p: (p, 0, 0))
    return pl.pallas_call(
        body, grid=(npair,),
        in_specs=[pair_aug, pair_aug, slab, slab, slab, per_pair, per_pair, pl.BlockSpec(memory_space=pl.ANY)],
        out_specs=[slab, slab, slab, rows],
        out_shape=(SDS((SEQ, FOX_W), bf16),) * 3 + (SDS((npair, 8, SEQ), f32),), name="fox_bwd",
        scratch_shapes=[pltpu.VMEM((SEQ, LANES), f32)] * 3,
        compiler_params=_params(("parallel",)),
    )(q_aug, k_aug, v, do, o, row_max, row_sum, after)


DIL_BLK = 128
DILATIONS = (1, 4, 16)
N_GROUPS = len(DILATIONS)
DIL_PAIRS = DIL_OUT_W // LANES


def _dil_blocks(d):
    r1 = lax.broadcasted_iota(jnp.int32, (2 * DIL_BLK, DIL_BLK), 0) & (DIL_BLK - 1)
    c1 = lax.broadcasted_iota(jnp.int32, (2 * DIL_BLK, DIL_BLK), 1)
    r2 = lax.broadcasted_iota(jnp.int32, (2 * DIL_BLK, 2 * DIL_BLK), 0) & (DIL_BLK - 1)
    c2 = lax.broadcasted_iota(jnp.int32, (2 * DIL_BLK, 2 * DIL_BLK), 1)
    band = ((c2 < DIL_BLK) & (c2 >= r2)) | ((c2 >= DIL_BLK) & (c2 - DIL_BLK <= r2))
    out = []
    for r in range(d):
        for b in range(SEQ // d // DIL_BLK):
            rows = pl.ds(r + d * DIL_BLK * b, DIL_BLK, stride=d)
            if b == 0:
                out.append((rows, rows, r1 >= c1))
            else:
                out.append((rows, pl.ds(r + d * DIL_BLK * (b - 1), 2 * DIL_BLK, stride=d), band))
    return out


def _dil_v_spec(g):
    return pl.BlockSpec((SEQ, LANES), lambda p: (0, C_VB // LANES + DIL_PAIRS * g + p))


def _stack_heads(t, first):
    zero = jnp.zeros_like(t)
    return jnp.concatenate([jnp.where(first, t, zero), jnp.where(first, zero, t)], axis=0)


def _dil_fwd(q, k, v, g):
    def body(q_ref, k_ref, v_ref, o_ref, lse_ref):
        first = lax.broadcasted_iota(jnp.int32, (DIL_BLK, LANES), 1) < HEAD_DIM
        for rows, krows, mask in _dil_blocks(DILATIONS[g]):
            qv, kk, vv = q_ref[rows, :].astype(bf16), k_ref[krows, :].astype(bf16), v_ref[krows, :].astype(bf16)
            s = jnp.where(mask, _nt(_stack_heads(qv, first), kk), NEG)
            m = jnp.max(s, axis=1, keepdims=True)
            p = jnp.exp(s - m)
            l = jnp.sum(p, axis=1, keepdims=True)
            out = jnp.dot(p.astype(bf16), vv, preferred_element_type=f32) / l
            lse = m + jnp.log(l)
            o_ref[rows, :] = jnp.where(first, out[:DIL_BLK], out[DIL_BLK:])
            lse_ref[rows, :] = jnp.where(first, lse[:DIL_BLK], lse[DIL_BLK:])

    grouped = pl.BlockSpec((SEQ, LANES), lambda p: (0, DIL_PAIRS * g + p))
    own = pl.BlockSpec((SEQ, LANES), lambda p: (0, p))
    shape = SDS((SEQ, DIL_OUT_W), f32)
    return pl.pallas_call(
        body, grid=(DIL_PAIRS,), in_specs=[grouped, grouped, _dil_v_spec(g)], out_specs=[own] * 2,
        out_shape=(shape, shape),
        name=f"dil_fwd_{DILATIONS[g]}", compiler_params=_params(("parallel",)),
    )(q, k, v)


def _dil_bwd(q, k, v, do, lse, delta, g):
    def body(q_ref, k_ref, v_ref, do_ref, lse_ref, dl_ref, dq_ref, dk_ref, dv_ref):
        first = lax.broadcasted_iota(jnp.int32, (DIL_BLK, LANES), 1) < HEAD_DIM
        dk_ref[...] = jnp.zeros_like(dk_ref)
        dv_ref[...] = jnp.zeros_like(dv_ref)
        for rows, krows, mask in _dil_blocks(DILATIONS[g]):
            qv, kk, vv = q_ref[rows, :].astype(bf16), k_ref[krows, :].astype(bf16), v_ref[krows, :].astype(bf16)
            lsev, delv = lse_ref[rows, :], dl_ref[rows, :]
            q2 = _stack_heads(qv, first)
            do2 = _stack_heads(do_ref[rows, :].astype(bf16), first)
            per_head = lambda t: jnp.concatenate([t[:, 0:1], t[:, HEAD_DIM:HEAD_DIM + 1]], axis=0)
            p = jnp.exp(jnp.where(mask, _nt(q2, kk), NEG) - per_head(lsev))
            dl = (p * (_nt(do2, vv) - per_head(delv))).astype(bf16)
            dq = jnp.dot(dl, kk, preferred_element_type=f32)
            dq_ref[rows, :] = jnp.where(first, dq[:DIL_BLK], dq[DIL_BLK:]) * ATT_SCALE
            dk_ref[krows, :] += _tn(dl, q2)
            dv_ref[krows, :] += _tn(p.astype(bf16), do2)

    grouped = pl.BlockSpec((SEQ, LANES), lambda p: (0, DIL_PAIRS * g + p))
    own = pl.BlockSpec((SEQ, LANES), lambda p: (0, p))
    shape = SDS((SEQ, DIL_OUT_W), f32)
    return pl.pallas_call(
        body, grid=(DIL_PAIRS,), in_specs=[grouped, grouped, _dil_v_spec(g)] + [own] * 3, out_specs=[own] * 3,
        out_shape=(shape, shape, shape), name=f"dil_bwd_{DILATIONS[g]}", compiler_params=_params(("parallel",)),
    )(q, k, v, do, lse, delta)


def _position():
    return lax.axis_index("x"), lax.axis_index("y"), lax.axis_index("c")


def _all_gather(block, name, after=None):
    after = [] if after is None else [after]

    def body(x_ref, *refs):
        out_ref, send_sems, recv_sems, local_sem = refs[len(after):]
        x, y, c = _position()
        me, sibling = (x, y, c), (x, y, 1 - c)
        chips = [(1 - x, y), (x, 1 - y), (1 - x, 1 - y)]

        def slot(px, py, pc):
            return out_ref.at[4 * px + 2 * py + pc]

        def copy(k, blk, to, src=None):
            return pltpu.make_async_remote_copy(
                src_ref=slot(*blk) if src is None else src, dst_ref=slot(*blk),
                send_sem=send_sems.at[k], recv_sem=recv_sems.at[k], device_id=to, device_id_type=MESH)

        mine = pltpu.make_async_copy(x_ref, slot(*me), local_sem)
        mine.start()
        first = [copy(0, me, sibling, src=x_ref)]
        first += [copy(1 + j, me, (*chip, c), src=x_ref) for j, chip in enumerate(chips)]
        for cp in first:
            cp.start()
        passed = [copy(4 + j, (*chip, c), sibling) for j, chip in enumerate(chips)]
        for j, chip in enumerate(chips):
            copy(1 + j, (*chip, c), me).wait_recv()
            passed[j].start()
        copy(0, sibling, me).wait_recv()
        for j, chip in enumerate(chips):
            copy(4 + j, (*chip, 1 - c), me).wait_recv()
        for cp in first + passed:
            cp.wait_send()
        mine.wait()

    return pl.pallas_call(
        body, out_shape=SDS((N_DEV,) + block.shape, block.dtype),
        in_specs=[pl.BlockSpec(memory_space=pl.ANY)] * (1 + len(after)), out_specs=pl.BlockSpec(memory_space=pl.ANY),
        scratch_shapes=[pltpu.SemaphoreType.DMA((7,)), pltpu.SemaphoreType.DMA((7,)), pltpu.SemaphoreType.DMA],
        name=name,
    )(block, *after)


HBM_SPEC = pl.BlockSpec(memory_space=pltpu.HBM)
SEM_SPEC = pl.BlockSpec(memory_space=pltpu.SEMAPHORE)
SPLIT_COPY = pltpu.CompilerParams(has_side_effects=pltpu.SideEffectType.DATAFLOW_SIDE_EFFECTING)


def _in_hbm(t):
    return pltpu.with_memory_space_constraint(t, pltpu.HBM)


def _pair_copies(g_refs, land_refs, send_sems, recv_sems):
    x, y, c = _position()
    return [pltpu.make_async_remote_copy(
        src_ref=g.at[2 * k + (1 - c)], dst_ref=land.at[k], send_sem=send_sems.at[4 * a + k],
        recv_sem=recv_sems.at[4 * a + k], device_id=(x, y, 1 - c), device_id_type=MESH)
        for a, (g, land) in enumerate(zip(g_refs, land_refs, strict=True)) for k in range(4)]


def _chip_copies(t_refs, land_refs, send_sems, recv_sems):
    x, y, c = _position()
    chips = [(1 - x, y), (x, 1 - y), (1 - x, 1 - y)]
    return [pltpu.make_async_remote_copy(
        src_ref=t.at[2 * px + py], dst_ref=land.at[j], send_sem=send_sems.at[3 * a + j],
        recv_sem=recv_sems.at[3 * a + j], device_id=(px, py, c), device_id_type=MESH)
        for a, (t, land) in enumerate(zip(t_refs, land_refs, strict=True)) for j, (px, py) in enumerate(chips)]


_ROUNDS = {"pair": (_pair_copies, 4), "chip": (_chip_copies, 3)}


def _exchange_start(kind, ts, name):
    copies, slots = _ROUNDS[kind]
    n = len(ts)
    lands = [_in_hbm(lax.empty((slots,) + t.shape[1:], t.dtype)) for t in ts]

    def body(*refs):
        for cp in copies(refs[:n], refs[n:2 * n], refs[2 * n], refs[2 * n + 1]):
            cp.start()
        refs[-1][...] = jnp.zeros_like(refs[-1])

    sems = pltpu.SemaphoreType.DMA((slots * n,))
    res = pl.pallas_call(
        body, name=name, in_specs=[HBM_SPEC] * (2 * n),
        out_shape=(sems, sems, *[pltpu.HBM(t.shape, t.dtype) for t in (*ts, *lands)], SDS((8, LANES), f32)),
        out_specs=(SEM_SPEC, SEM_SPEC, *[HBM_SPEC] * (2 * n), pl.BlockSpec(memory_space=pltpu.VMEM)),
        input_output_aliases={i: 2 + i for i in range(2 * n)}, compiler_params=SPLIT_COPY,
    )(*[_in_hbm(t) for t in ts], *lands)
    return res[:-1], res[-1]


def _exchange_wait(kind, state, after, name):
    copies, _ = _ROUNDS[kind]
    send_sems, recv_sems, *arrays = state
    n = len(arrays) // 2

    def body(*refs):
        for cp in copies(refs[:n], refs[n:2 * n], refs[2 * n], refs[2 * n + 1]):
            cp.wait_send()
            cp.wait_recv()

    res = pl.pallas_call(
        body, name=name, in_specs=[HBM_SPEC] * (2 * n) + [SEM_SPEC, SEM_SPEC, pl.BlockSpec(memory_space=pl.ANY)],
        out_shape=[pltpu.HBM(t.shape, t.dtype) for t in arrays], out_specs=[HBM_SPEC] * (2 * n),
        input_output_aliases={i: i for i in range(2 * n)}, compiler_params=SPLIT_COPY,
    )(*arrays, send_sems, recv_sems, after)
    return res[:n], res[n:]


def _gather_copies(x_refs, out_refs, send_sems, recv_sems):
    x, y, c = _position()
    peers = [(x, y, 1 - c), (1 - x, y, c), (x, 1 - y, c), (1 - x, 1 - y, c)]
    sends, arrivals = [], []
    for a, (x_ref, out_ref) in enumerate(zip(x_refs, out_refs, strict=True)):
        for k, (px, py, pc) in enumerate(peers):
            sems = dict(send_sem=send_sems.at[4 * a + k], recv_sem=recv_sems.at[4 * a + k],
                        device_id=(px, py, pc), device_id_type=MESH)
            sends.append(pltpu.make_async_remote_copy(src_ref=x_ref, dst_ref=out_ref.at[4 * x + 2 * y + c], **sems))
            arrivals.append(pltpu.make_async_remote_copy(src_ref=x_ref, dst_ref=out_ref.at[4 * px + 2 * py + pc],
                                                         **sems))
    return sends, arrivals


def _gather_start(blocks, after, name):
    n = len(blocks)
    outs = [_in_hbm(lax.empty((N_DEV,) + b.shape, b.dtype)) for b in blocks]

    def body(*refs):
        sends, _ = _gather_copies(refs[:n], refs[n:2 * n], refs[2 * n + 1], refs[2 * n + 2])
        for cp in sends:
            cp.start()
        refs[-1][...] = jnp.zeros_like(refs[-1])

    sems = pltpu.SemaphoreType.DMA((4 * n,))
    res = pl.pallas_call(
        body, name=name, in_specs=[HBM_SPEC] * (2 * n) + [pl.BlockSpec(memory_space=pl.ANY)],
        out_shape=(sems, sems, *[pltpu.HBM(t.shape, t.dtype) for t in (*blocks, *outs)], SDS((8, LANES), f32)),
        out_specs=(SEM_SPEC, SEM_SPEC, *[HBM_SPEC] * (2 * n), pl.BlockSpec(memory_space=pltpu.VMEM)),
        input_output_aliases={i: 2 + i for i in range(2 * n)}, compiler_params=SPLIT_COPY,
    )(*[_in_hbm(b) for b in blocks], *outs, after)
    return res[:-1], res[-1]


def _gather_wait(state, after, name):
    send_sems, recv_sems, *arrays = state
    n = len(arrays) // 2

    def body(*refs):
        sends, arrivals = _gather_copies(refs[:n], refs[n:2 * n], refs[2 * n], refs[2 * n + 1])
        for cp in sends:
            cp.wait_send()
        for cp in arrivals:
            cp.wait_recv()

    res = pl.pallas_call(
        body, name=name, in_specs=[HBM_SPEC] * (2 * n) + [SEM_SPEC, SEM_SPEC, pl.BlockSpec(memory_space=pl.ANY)],
        out_shape=[pltpu.HBM(t.shape, t.dtype) for t in arrays], out_specs=[HBM_SPEC] * (2 * n),
        input_output_aliases={i: i for i in range(2 * n)}, compiler_params=SPLIT_COPY,
    )(*arrays, send_sems, recv_sems, after)
    return res[:n], res[n:]


def _gather_finish(partial, name):
    n = len(partial)

    def body(*refs):
        in_refs, out_refs = refs[:n], refs[n:2 * n]
        send_sems, recv_sems = refs[2 * n:]
        x, y, c = _position()
        chips = [(1 - x, y), (x, 1 - y), (1 - x, 1 - y)]
        copies = []
        for a in range(n):
            for j, (px, py) in enumerate(chips):
                cp = pltpu.make_async_remote_copy(
                    src_ref=in_refs[a].at[4 * px + 2 * py + c], dst_ref=out_refs[a].at[4 * px + 2 * py + c],
                    send_sem=send_sems.at[a, j], recv_sem=recv_sems.at[a, j], device_id=(x, y, 1 - c),
                    device_id_type=MESH)
                cp.start()
                copies.append(cp)
        for a in range(n):
            for j, (px, py) in enumerate(chips):
                pltpu.make_async_remote_copy(
                    src_ref=in_refs[a].at[4 * px + 2 * py + (1 - c)], dst_ref=out_refs[a].at[4 * px + 2 * py + (1 - c)],
                    send_sem=send_sems.at[a, j], recv_sem=recv_sems.at[a, j], device_id=(x, y, 1 - c),
                    device_id_type=MESH).wait_recv()
        for cp in copies:
            cp.wait_send()

    hbm = pl.BlockSpec(memory_space=pl.ANY)
    return pl.pallas_call(
        body, out_shape=[SDS(p.shape, p.dtype) for p in partial], in_specs=[hbm] * n, out_specs=[hbm] * n,
        input_output_aliases={a: a for a in range(n)},
        scratch_shapes=[pltpu.SemaphoreType.DMA((n, 3)), pltpu.SemaphoreType.DMA((n, 3))],
        name=name,
    )(*partial)


def _row_tile(rows):
    return 512 if rows % 512 == 0 and rows > 512 else rows


def _pair_add(g, r1, core, name):
    def body(c_ref, g_ref, r_ref, o_ref):
        o_ref[...] = (g_ref[...].astype(f32) + r_ref[...].astype(f32)).astype(o_ref.dtype)

    rows, cols = g.shape[1:]
    tile = _row_tile(rows)
    blk = (1, tile, cols)
    return pl.pallas_call(
        body, out_shape=SDS((4, rows, cols), g.dtype), name=name,
        grid_spec=pltpu.PrefetchScalarGridSpec(
            num_scalar_prefetch=1, grid=(4, rows // tile),
            in_specs=[pl.BlockSpec(blk, lambda k, i, c_ref: (2 * k + c_ref[0], i, 0)),
                      pl.BlockSpec(blk, lambda k, i, c_ref: (k, i, 0))],
            out_specs=pl.BlockSpec(blk, lambda k, i, c_ref: (k, i, 0))),
        compiler_params=_params(("parallel", "arbitrary")),
    )(core, g, r1)


def _chip_add(t, r2, chip, name, planes=None, pieces=None, after=None):
    after = [] if after is None else [after]

    def body(c_ref, t_ref, r_ref, *refs):
        o_refs = refs[len(after):]
        s = ((t_ref[0].astype(f32) + r_ref[0].astype(f32)) + r_ref[1].astype(f32)) + r_ref[2].astype(f32)
        if planes:
            o_refs[0][...] = s.T[:planes][:, None, :]
        elif pieces:
            for o_ref, (start, size) in zip(o_refs, pieces, strict=True):
                o_ref[...] = s.T[start:start + size]
        else:
            o_refs[0][...] = s

    rows, cols = t.shape[1:]
    tile = _row_tile(rows)
    if planes:
        out_shape, out_spec = SDS((planes, 1, rows), f32), pl.BlockSpec((planes, 1, tile), lambda i, c_ref: (0, 0, i))
    elif pieces:
        out_shape = [SDS((size, rows), f32) for _, size in pieces]
        out_spec = [pl.BlockSpec((size, tile), lambda i, c_ref: (0, i)) for _, size in pieces]
    else:
        out_shape, out_spec = SDS((rows, cols), f32), pl.BlockSpec((tile, cols), lambda i, c_ref: (i, 0))
    return pl.pallas_call(
        body, out_shape=out_shape, name=name,
        grid_spec=pltpu.PrefetchScalarGridSpec(
            num_scalar_prefetch=1, grid=(rows // tile,),
            in_specs=[pl.BlockSpec((1, tile, cols), lambda i, c_ref: (c_ref[0], i, 0)),
                      pl.BlockSpec((3, tile, cols), lambda i, c_ref: (0, i, 0))]
            + [pl.BlockSpec(memory_space=pl.ANY)] * len(after),
            out_specs=out_spec),
        compiler_params=_params(("arbitrary",)),
    )(chip, t, r2, *after)


def _pad_to(t, axis, size):
    pads = [(0, 0)] * t.ndim
    pads[axis] = (0, size - t.shape[axis])
    return jnp.pad(t, pads)


_REF_COLS = {"qa": (0, FOX_W), "ka": (FOX_W, FOX_W), "va": (2 * FOX_W, FOX_W), "f": (3 * FOX_W, N_FOX_HEADS)}
_REF_COLS.update({n: (3 * FOX_W + N_FOX_HEADS + i * DIL_W, DIL_W) for i, n in enumerate(("qb", "kb", "vb"))})
_REF_COLS.update({n: (3 * FOX_W + N_FOX_HEADS + 3 * DIL_W + i * D, D) for i, n in enumerate(("ga", "gb"))})
_REF_ORDER = ("qa", "ka", "va", "f", "qb", "kb", "vb", "ga", "gb")


def _place_cols(sources, src_of, out_cols, name, row_block=512):
    arrays = [s[0] if isinstance(s, tuple) else s for s in sources]
    widths = [a.shape[-1] for a in arrays]
    rows = arrays[0].shape[-2]
    plan = []
    for t in range(out_cols // LANES):
        segs, c, end = [], t * LANES, (t + 1) * LANES
        while c < end:
            s = src_of(c)
            if s is None:
                c += 1
                continue
            n = 1
            while c + n < end and src_of(c + n) == (s[0], s[1] + n):
                n += 1
            segs.append((s[0], s[1], c - t * LANES, n))
            c += n
        plan.append(segs)

    def body(*refs):
        o_ref = refs[-1]
        for t, segs in enumerate(plan):
            acc = None
            for si, c0, o0, n in segs:
                a0 = c0 // LANES * LANES
                wide = min(2 * LANES, widths[si] - a0)
                win = refs[si][0, :, a0:a0 + wide] if isinstance(sources[si], tuple) else refs[si][:, a0:a0 + wide]
                r = lax.broadcasted_iota(jnp.int32, (wide, LANES), 0)
                c = lax.broadcasted_iota(jnp.int32, (wide, LANES), 1)
                pick = ((r - (c0 - a0) == c - o0) & (c >= o0) & (c < o0 + n)).astype(bf16)
                part = jnp.dot(win.astype(bf16), pick, preferred_element_type=f32)
                acc = part if acc is None else acc + part
            tile = jnp.zeros((row_block, LANES), f32) if acc is None else acc
            o_ref[:, t * LANES:(t + 1) * LANES] = tile.astype(o_ref.dtype)

    def spec(s):
        if isinstance(s, tuple):
            j = s[1]
            return pl.BlockSpec((1, row_block, s[0].shape[-1]), lambda i: (j, i, 0))
        return pl.BlockSpec((row_block, s.shape[-1]), lambda i: (i, 0))

    return pl.pallas_call(
        body, grid=(rows // row_block,), in_specs=[spec(s) for s in sources],
        out_specs=pl.BlockSpec((row_block, out_cols), lambda i: (i, 0)), out_shape=SDS((rows, out_cols), bf16),
        name=name, compiler_params=_params(("parallel",)),
    )(*arrays)


def _ref_piece(r):
    for name in _REF_ORDER:
        lo, width = _REF_COLS[name]
        if lo <= r < lo + width:
            return name, r - lo
    raise ValueError(r)


def _shard_pad_cols(pieces):
    names = [n for n in _REF_ORDER if n != "vb"]
    sources = [pieces[n] for n in names] + list(pieces["vb"])

    def src_of(c):
        j, i = divmod(c, W_IN_PAD)
        if i >= W_IN_SH:
            return None
        name, col = _ref_piece(j * W_IN_SH + i)
        if name == "vb":
            return len(names) + col // DIL_OUT_W, col % DIL_OUT_W
        return names.index(name), col

    return _place_cols(sources, src_of, N_DEV * W_IN_PAD, "place_dproj")


_SLABS = {"ga": C_GA, "gb": C_GB, "qb": C_QB, "kb": C_KB, "vb": C_VB, "qa": C_QA, "ka": C_KA, "va": C_VA, "f": C_F}


def _slab_w_in(stack):
    def src_of(c):
        for name, start in _SLABS.items():
            lo, width = _REF_COLS[name]
            if start <= c < start + width:
                return divmod(lo + c - start, W_IN_SH)
        return None

    return _place_cols([(stack, j) for j in range(N_DEV)], src_of, PROJ_W, "place_w_in")


def kernel(x, c, w_ada, b_ada, g_mix, w_in, b_fgate, w_br_a, w_br_b, w_out, g_ffn, w_ffn_gate, w_ffn_up, w_ffn_down, g_final, loss_target, m_w_ada, m_b_ada, m_g_mix, m_w_in, m_b_fgate, m_w_br_a, m_w_br_b, m_w_out, m_g_ffn, m_w_ffn_gate, m_w_ffn_up, m_w_ffn_down, m_g_final, v_w_ada, v_b_ada, v_g_mix, v_w_in, v_b_fgate, v_w_br_a, v_w_br_b, v_w_out, v_g_ffn, v_w_ffn_gate, v_w_ffn_up, v_w_ffn_down, v_g_final):
    px, py, pc = _position()
    dev = 4 * px + 2 * py + pc
    x2d, tgt = x[0], loss_target[0]

    c_all = _all_gather(c, "gather_c").reshape(N_DEV, D)
    ada_cols = w_ada.shape[2]
    b_shard = lax.dynamic_slice(b_ada, (0, dev * ada_cols), (1, ada_cols))
    mod_shard = _ada_fwd(c_all, w_ada[0], b_shard)
    mod_all = _all_gather(mod_shard, "gather_mod")
    modv = lax.dynamic_index_in_dim(mod_all, dev, axis=1, keepdims=False).reshape(6, D)
    h1 = _pre1(x2d, modv, g_mix)

    w_in_s = _all_gather(_pad_to(w_in[0], 1, W_IN_PAD).astype(bf16), "gather_w_in")
    gate_up = jnp.concatenate([_pad_to(w_ffn_gate[0], 1, FF_PAD), _pad_to(w_ffn_up[0], 1, FF_PAD)], axis=1)
    later = [w_br_a[0], w_br_b[0], w_out[0], gate_up, _pad_to(w_ffn_down[0], 0, FF_PAD)]
    later_state, later_token = _gather_start([t.astype(bf16) for t in later], w_in_s, "gather_rest_start")
    w_in_p = _slab_w_in(w_in_s)

    proj = _matmul(h1, w_in_p, name="mm_proj", tm=SEQ, tn=896, after=later_token)
    b_pad = jnp.pad(b_fgate, ((0, 0), (0, LANES - N_FOX_HEADS)))
    q_aug, k_aug, va = _fox_prep(proj, _fox_gate_fwd(proj, b_pad))
    ya_h, max_a, sum_a = _fox_fwd(q_aug, k_aug, va)

    tables = _rope_tables()
    qb_r, kb_r = _rope_fwd(proj, tables)
    by_group = [_dil_fwd(qb_r, kb_r, proj, grp) for grp in range(N_GROUPS)]
    yb_h, lse_b = _dil_combine([o for o, _ in by_group], [l for _, l in by_group])

    both_done = ya_h[:8, :LANES] + yb_h[:8, :LANES]
    mine, arrived = _gather_wait(later_state, both_done, "gather_rest_wait")
    w_a_s, w_b_s, w_o_s, w_gu_s, w_d_s = [
        lax.dynamic_update_slice(stack, block[None], (dev, 0, 0))
        for stack, block in zip(_gather_finish(arrived, "gather_rest_finish"), mine, strict=True)]
    w_o = w_o_s.reshape(D, D)
    w_d = w_d_s.reshape(FF_HID, D)
    ya = _matmul_stack(ya_h, w_a_s, name="mm_br_a")
    yb = _matmul_stack(yb_h, w_b_s, name="mm_br_b")

    merged, mix, x1, h2 = _post1(ya, yb, proj, w_o, x2d, modv, g_ffn)
    act, au = _ffn_in(h2, w_gu_s)

    dx2, dff, dg_final, dga_f, loss_lanes = _final(act, w_d, x1, tgt, modv, g_final.reshape(1, D))
    dau = _ffn_bwd_in(dff, w_d_s, au)

    core = pc.astype(jnp.int32).reshape(1)
    chip = (2 * px + py).astype(jnp.int32).reshape(1)

    def pair_done(state, after, tags, name):
        mine, theirs = _exchange_wait("pair", state, after, "pair_wait_" + name)
        sums = [_pair_add(g, r, core, "pair_add_" + t) for g, r, t in zip(mine, theirs, tags)]
        return _exchange_start("chip", sums, "chip_start_" + name)

    def from_chips(state, after, tags, name):
        sums, got = _exchange_wait("chip", state, after, "chip_wait_" + name)
        return [_chip_add(p, r, chip, "chip_add_" + t) for p, r, t in zip(sums, got, tags)]

    g_gu = _matmul(h2, dau, ta=True, by_shard=True, out_dtype=bf16, name="mm_g_ffn_in", tm=D, tn=2 * FF_PAD)
    g_d = _matmul(act, dff, ta=True, out_dtype=bf16, name="mm_g_down", tm=FF_HID // 2, tn=512)
    ffn_tags = ["gu", "down"]
    ffn_pair, ffn_pair_token = _exchange_start("pair", [g_gu, g_d.reshape(N_DEV, FF_PAD, D)], "pair_start_ffn")

    dx1, dmix, dsh_f, dsc_f, dg_ffn, dga_m = _mid_bwd(dau, w_gu_s, ffn_pair_token, x1, dx2, mix, modv, g_ffn)
    ffn_state, ffn_token = pair_done(ffn_pair, dx1, ffn_tags, "ffn")
    dya, dyb, dga, dgb = _merge_bwd(dmix, w_o, ffn_token, ya, yb, proj)
    dya_h = _matmul_stack(dya, w_a_s, tb=True, name="mm_d_ya")
    dyb_h = _matmul_stack(dyb, w_b_s, tb=True, name="mm_d_yb")

    g_o = _matmul(merged, dmix, ta=True, out_dtype=bf16, name="mm_g_out", tm=D, tn=512)
    g_a = _matmul_stack(ya_h, dya, ta=True, out_dtype=bf16, name="mm_g_br_a")
    g_b = _matmul_stack(yb_h, dyb, ta=True, out_dtype=bf16, name="mm_g_br_b")
    rows_a, rows_b = FOX_W * W_BR_SH // D, DIL_OUT_W * W_BR_SH // D
    g_small = jnp.concatenate([g_a.reshape(N_DEV, rows_a, D), g_b.reshape(N_DEV, rows_b, D),
                               g_o.reshape(N_DEV, W_BR_SH, D)], axis=1)
    small_pair, small_pair_token = _exchange_start("pair", [g_small], "pair_start_small")

    dqa, dka, dva, dF = _fox_bwd(q_aug, k_aug, va, dya_h, ya_h, max_a, sum_a, small_pair_token)
    dF_row = jnp.pad(dF[:, :2, :].reshape(N_FOX_HEADS, SEQ), ((0, LANES - N_FOX_HEADS), (0, 0)))
    df, db_fgate = _fox_gate_bwd(dF_row, proj, b_pad)
    small_state, small_token = pair_done(small_pair, df, ["small"], "small")

    delta_b = _dil_delta(dyb_h, yb_h)
    dil_grads = [_dil_bwd(qb_r, kb_r, proj, dyb_h, lse_b, delta_b, grp) for grp in range(N_GROUPS)]
    dqb, dkb = _rope_bwd([t[0] for t in dil_grads], [t[1] for t in dil_grads], tables)

    dproj = _shard_pad_cols({"qa": dqa, "ka": dka, "va": dva, "f": df, "qb": dqb, "kb": dkb,
                             "vb": [t[2] for t in dil_grads], "ga": dga, "gb": dgb})
    g_in = _matmul(h1, dproj, ta=True, by_shard=True, out_dtype=bf16, name="mm_g_in", tm=D, tn=W_IN_PAD,
                   after=small_token)
    mix_tags = ["in"]
    mix_pair, mix_pair_token = _exchange_start("pair", [g_in], "pair_start_mixer")

    w = {"w_ada": w_ada, "b_ada": b_ada, "g_mix": g_mix, "w_in": w_in, "b_fgate": b_fgate, "w_br_a": w_br_a,
         "w_br_b": w_br_b, "w_out": w_out, "g_ffn": g_ffn, "w_ffn_gate": w_ffn_gate, "w_ffn_up": w_ffn_up,
         "w_ffn_down": w_ffn_down, "g_final": g_final}
    m = {"w_ada": m_w_ada, "b_ada": m_b_ada, "g_mix": m_g_mix, "w_in": m_w_in, "b_fgate": m_b_fgate,
         "w_br_a": m_w_br_a, "w_br_b": m_w_br_b, "w_out": m_w_out, "g_ffn": m_g_ffn, "w_ffn_gate": m_w_ffn_gate,
         "w_ffn_up": m_w_ffn_up, "w_ffn_down": m_w_ffn_down, "g_final": m_g_final}
    v = {"w_ada": v_w_ada, "b_ada": v_b_ada, "g_mix": v_g_mix, "w_in": v_w_in, "b_fgate": v_b_fgate,
         "w_br_a": v_w_br_a, "w_br_b": v_w_br_b, "w_out": v_w_out, "g_ffn": v_g_ffn, "w_ffn_gate": v_w_ffn_gate,
         "w_ffn_up": v_w_ffn_up, "w_ffn_down": v_w_ffn_down, "g_final": v_g_final}
    names = list(w)
    g, delta, new_m, new_v = {}, {}, {}, {}

    transposed = ("w_ffn_gate", "w_ffn_up")
    by_column = lambda t: jnp.transpose(t, (2, 0, 1))
    by_row = lambda t: jnp.transpose(t, (1, 2, 0))

    def update(n):
        shape = w[n].shape
        if n == "w_in":
            g3 = g[n]
            dl, mn, vn = _adamw_by_planes(by_column(w[n]), g3, by_column(m[n]), by_column(v[n]), "adamw_" + n)
            g[n], delta[n], new_m[n], new_v[n] = by_row(g3), by_row(dl), by_row(mn), by_row(vn)
            return
        if n in transposed:
            g_t = g[n]
            dl, mn, vn = _adamw(w[n][0].T, g_t, m[n][0].T, v[n][0].T, "adamw_" + n)
            g[n], delta[n], new_m[n], new_v[n] = g_t.T[None], dl.T[None], mn.T[None], vn.T[None]
            return
        two_d = (lambda t: t.reshape(shape[-2:])) if len(shape) == 3 else (lambda t: t)
        dl, mn, vn = _adamw(two_d(w[n]), two_d(g[n]), two_d(m[n]), two_d(v[n]), "adamw_" + n)
        delta[n], new_m[n], new_v[n] = dl.reshape(shape), mn.reshape(shape), vn.reshape(shape)

    def update_all(grads):
        g.update(grads)
        for n in grads:
            update(n)
        return sum(delta[n][(0,) * (delta[n].ndim - 1)][:N_FOX_HEADS] for n in grads)

    ffn_sums, ffn_got = _exchange_wait("chip", ffn_state, mix_pair_token, "chip_wait_ffn")
    g_gate_t, g_up_t = _chip_add(ffn_sums[0], ffn_got[0], chip, "chip_add_gu",
                                 pieces=[(0, W_FF_SH), (FF_PAD, W_FF_SH)])
    gate_up_done = update_all({"w_ffn_gate": g_gate_t, "w_ffn_up": g_up_t})
    mix_state, mix_token = pair_done(mix_pair, gate_up_done, mix_tags, "mixer")

    grad_x, dsh_m, dsc_m, dg_mix = _first_bwd(dproj, w_in_s, mix_token, x2d, dx1, modv, g_mix)

    s_d = _chip_add(ffn_sums[1], ffn_got[1], chip, "chip_add_down", after=grad_x)
    s_small, = from_chips(small_state, grad_x, ["small"], "small")
    sharded_done = update_all({
        "w_ffn_down": s_d[None, :W_FF_SH],
        "w_br_a": s_small[:rows_a].reshape(1, FOX_W, W_BR_SH),
        "w_br_b": s_small[rows_a:rows_a + rows_b].reshape(1, DIL_OUT_W, W_BR_SH), "w_out": s_small[None, rows_a + rows_b:],
    })

    pad_lane = lambda t: jnp.pad(t, ((0, 0), (0, D - t.shape[1])))
    small = jnp.concatenate([dsh_m, dsc_m, dga_m, dsh_f, dsc_f, dga_f, dg_mix, dg_ffn, dg_final,
                             pad_lane(db_fgate), loss_lanes, jnp.zeros((SMALL_ROWS - 11, D), f32)], axis=0)
    small_all = _all_gather(small, "gather_small", after=sharded_done)
    small_sum, loss_row = _small_reduce(small_all, mix_token)
    loss = loss_row[0, 0]
    dmod_all = small_all[:, :6, :].reshape(N_DEV, 6 * D)
    g_w_ada = _ada_bwd(c_all, lax.dynamic_slice(dmod_all, (0, dev * ada_cols), (N_DEV, ada_cols)))
    done = update_all({
        "w_ada": g_w_ada[None], "b_ada": small_sum[0:6].reshape(1, 6 * D), "g_mix": small_sum[6:7],
        "b_fgate": small_sum[9:10, :N_FOX_HEADS], "g_ffn": small_sum[7:8], "g_final": small_sum[8],
    })
    mix_sums, mix_got = _exchange_wait("chip", mix_state, done, "chip_wait_mixer")
    g["w_in"] = _chip_add(mix_sums[0], mix_got[0], chip, "chip_add_in", planes=W_IN_SH)
    update("w_in")

    return (loss, grad_x[None], *[g[n] for n in names], *[delta[n] for n in names],
            *[new_m[n] for n in names], *[new_v[n] for n in names])
```

```python
import jax
import jax.numpy as jnp
import numpy as np
from jax import lax
from jax.experimental import pallas as pl
from jax.experimental.pallas import tpu as pltpu

f32 = jnp.float32
bf16 = jnp.bfloat16
SDS = jax.ShapeDtypeStruct
MESH = pl.DeviceIdType.MESH

N_DEV = 8
D = 1024
SEQ = 2048
HEAD_DIM = 64
N_FOX_HEADS = 8
FOX_W = 512
DIL_W = 768
DIL_OUT_W = 256
ROT_DIM = 16
ROPE_THETA = 500000.0
D_FF = 2816
IN_COLS = 5896
EPS = 1e-6
NEG = -1e30
ATT_SCALE = HEAD_DIM ** -0.5

ADAM_LR = 0.001
ADAM_B1 = 0.9
ADAM_B2 = 0.999
ADAM_EPS = 1e-08
ADAM_WD = 0.01
ADAM_STEP = 10

C_GA, C_GB, C_QB, C_KB, C_VB, C_QA, C_KA, C_VA, C_F = 0, 1024, 2304, 3072, 3840, 4608, 5120, 5632, 6144
PROJ_W = 6272
LANES = 128
SUBLANES = 8
BIG_UPDATE = 256 * 1024
VMEM_LIMIT = 52 * 1024 * 1024

W_IN_SH, W_IN_PAD = IN_COLS // N_DEV, 768
W_BR_SH = D // N_DEV
W_FF_SH, FF_PAD = D_FF // N_DEV, 384
FF_HID = N_DEV * FF_PAD
SMALL_ROWS = 16


def _params(sem=None):
    if sem is None:
        return pltpu.CompilerParams(vmem_limit_bytes=VMEM_LIMIT)
    return pltpu.CompilerParams(dimension_semantics=sem, vmem_limit_bytes=VMEM_LIMIT)


def _rowwise(fn, name, tiled, vecs, outs, reds=(), tile=256):
    nt, nv, no = len(tiled), len(vecs), len(outs)
    rows = tiled[0][0].shape[0]
    assert rows % tile == 0

    def body(*refs):
        tin = [r[...] for r in refs[:nt]]
        vin = [r[...] for r in refs[nt:nt + nv]]
        orefs = refs[nt + nv:nt + nv + no]
        rrefs = refs[nt + nv + no:]
        touts, routs = fn(tin, vin)
        for r, t in zip(orefs, touts, strict=True):
            r[...] = t.astype(r.dtype)
        if rrefs:
            @pl.when(pl.program_id(0) == 0)
            def _():
                for r in rrefs:
                    r[...] = jnp.zeros_like(r)
            for r, t in zip(rrefs, routs, strict=True):
                r[...] += t

    def col_map(cb):
        return lambda i: (i, cb)

    def whole_map(nd):
        return lambda i: (0,) * nd

    in_specs = [pl.BlockSpec((tile, w), col_map(cb)) for (_, w, cb) in tiled]
    in_specs += [pl.BlockSpec(v.shape, whole_map(v.ndim)) for v in vecs]
    out_specs = [pl.BlockSpec((tile, w), lambda i: (i, 0)) for (w, _) in outs]
    out_specs += [pl.BlockSpec((1, w), lambda i: (0, 0)) for w in reds]
    out_shape = [SDS((rows, w), dt) for (w, dt) in outs] + [SDS((1, w), f32) for w in reds]
    res = pl.pallas_call(
        body, grid=(rows // tile,), in_specs=in_specs, out_specs=out_specs, out_shape=out_shape, name=name,
        compiler_params=_params(("arbitrary",)),
    )(*[t[0] for t in tiled], *vecs)
    return res


def _matmul(a, b, *, ta=False, out_dtype=f32, name, tm, tn, by_shard=False, after=None):
    (m, k), n = ((a.shape[1], a.shape[0]) if ta else a.shape), b.shape[1]
    assert b.shape[0] == k and m % tm == 0 and n % tn == 0 and (ta or not by_shard)
    dims = (((0 if ta else 1,), (0,)), ((), ()))

    def body(a_ref, b_ref, *rest):
        p = lax.dot_general(a_ref[...].astype(bf16), b_ref[...].astype(bf16), dims, preferred_element_type=f32)
        o_ref = rest[-1]
        if by_shard:
            o_ref[0] = p.astype(o_ref.dtype)
        else:
            o_ref[...] = p.astype(o_ref.dtype)

    a_spec = pl.BlockSpec((k, tm), lambda i, j: (0, i)) if ta else pl.BlockSpec((tm, k), lambda i, j: (i, 0))
    if by_shard:
        assert tn == n // N_DEV
        out_spec, out_shape = pl.BlockSpec((1, tm, tn), lambda i, j: (j, i, 0)), SDS((N_DEV, m, tn), out_dtype)
    else:
        out_spec, out_shape = pl.BlockSpec((tm, tn), lambda i, j: (i, j)), SDS((m, n), out_dtype)
    extra_specs, extra = ([pl.BlockSpec(memory_space=pl.ANY)], [after]) if after is not None else ([], [])
    return pl.pallas_call(
        body, grid=(m // tm, n // tn), in_specs=[a_spec, pl.BlockSpec((k, tn), lambda i, j: (0, j))] + extra_specs,
        out_specs=out_spec, out_shape=out_shape, name=name, compiler_params=_params(("parallel", "parallel")),
    )(a, b, *extra)


def _matmul_stack(a, b, *, ta=False, tb=False, out_dtype=f32, name):
    def lanes(ref):
        return jnp.concatenate([ref[j] for j in range(N_DEV)], axis=1).astype(bf16)

    if ta:
        w = b.shape[1] // N_DEV

        def body(a_ref, b_ref, o_ref):
            p = _tn(a_ref[...].astype(bf16), b_ref[...].astype(bf16))
            for j in range(N_DEV):
                o_ref[j] = p[:, j * w:(j + 1) * w].astype(o_ref.dtype)

        return pl.pallas_call(body, out_shape=SDS((N_DEV, a.shape[1], w), out_dtype), name=name,
                              compiler_params=_params())(a, b)

    m, half = a.shape[0], a.shape[0] // 2
    n = b.shape[1] if tb else N_DEV * b.shape[2]

    def body(a_ref, b_ref, o_ref):
        av = a_ref[...].astype(bf16)
        o_ref[...] = (_nt(av, lanes(b_ref)) if tb else jnp.dot(av, lanes(b_ref), preferred_element_type=f32)
                      ).astype(o_ref.dtype)

    return pl.pallas_call(
        body, grid=(2,), in_specs=[pl.BlockSpec((half, a.shape[1]), lambda i: (i, 0)),
                                   pl.BlockSpec(b.shape, lambda i: (0, 0, 0))],
        out_specs=pl.BlockSpec((half, n), lambda i: (i, 0)), out_shape=SDS((m, n), out_dtype), name=name,
        compiler_params=_params(("parallel",)),
    )(a, b)


def _matmul_rows(form, a, b, after, fn, tiled, vecs, outs, reds, *, name, tm=512):
    norm = lambda ts: [t if isinstance(t, tuple) else (t, t.shape[1], 0) for t in ts]
    make, sources = a if isinstance(a, tuple) else (None, [a])
    sources, tiled = norm(sources), norm(tiled)
    m, k = sources[0][0].shape[0], (b.shape[0] if form == "nn" else b.shape[-1] * (N_DEV if form == "nt_stack" else 1))
    assert m % tm == 0
    ns, nt, nv, no = len(sources), len(tiled), len(vecs), len(outs)

    def body(*refs):
        src_refs, b_ref, refs = refs[:ns], refs[ns], refs[ns + 2:]
        if make is None:
            lhs = lambda lo, hi: src_refs[0][:, lo:hi]
        else:
            made = make([r[...] for r in src_refs]).astype(bf16)
            lhs = lambda lo, hi: made[:, lo:hi]
        if form == "nt_stack":
            w = b.shape[2]
            acc = _nt(lhs(0, w), b_ref[0])
            for j in range(1, N_DEV):
                acc = acc + _nt(lhs(j * w, (j + 1) * w), b_ref[j])
        elif form == "nt":
            acc = _nt(lhs(0, k), b_ref[...])
        else:
            acc = jnp.dot(lhs(0, k), b_ref[...], preferred_element_type=f32)
        if make is not None:
            refs[nt + nv][...] = made
            refs = refs[:nt + nv] + refs[nt + nv + 1:]
        orefs, rrefs = refs[nt + nv:nt + nv + no], refs[nt + nv + no:]
        touts, routs = fn([acc] + [r[...] for r in refs[:nt]], [r[...] for r in refs[nt:nt + nv]])
        for r, t in zip(orefs, touts, strict=True):
            r[...] = t.astype(r.dtype)

        @pl.when(pl.program_id(0) == 0)
        def _():
            for r in rrefs:
                r[...] = jnp.zeros_like(r)
        for r, t in zip(rrefs, routs, strict=True):
            r[...] += t

    def whole_map(nd):
        return lambda i: (0,) * nd

    def rows(width, cb=0):
        return pl.BlockSpec((tm, width), lambda i: (i, cb))

    made_out = [(k, bf16)] if make is not None else []
    return pl.pallas_call(
        body, grid=(m // tm,),
        in_specs=[rows(width, cb) for _, width, cb in sources]
        + [pl.BlockSpec(b.shape, whole_map(b.ndim), pipeline_mode=pl.Buffered(1)), pl.BlockSpec(memory_space=pl.ANY)]
        + [rows(width, cb) for _, width, cb in tiled] + [pl.BlockSpec(v.shape, whole_map(v.ndim)) for v in vecs],
        out_specs=[rows(width) for width, _ in made_out + list(outs)]
        + [pl.BlockSpec((1, width), lambda i: (0, 0)) for width in reds],
        out_shape=[SDS((m, width), dt) for width, dt in made_out + list(outs)]
        + [SDS((1, width), f32) for width in reds], name=name,
        compiler_params=_params(("arbitrary",)),
    )(*[t[0] for t in sources], b, after, *[t[0] for t in tiled], *vecs)


def _rms(x):
    r = lax.rsqrt(jnp.mean(x * x, axis=-1, keepdims=True) + EPS)
    return r, x * r


def _rms_bwd(r, xn, dxn):
    return r * (dxn - xn * jnp.mean(dxn * xn, axis=-1, keepdims=True))


def _colsum(t):
    return jnp.sum(t, axis=0, keepdims=True)


def _sigmoid(x):
    return 0.5 * jnp.tanh(0.5 * x) + 0.5


def _modulated_norm(x, g, shift, scale):
    _, xn = _rms(x)
    return (xn * g) * (1.0 + scale) + shift


def _pre1(x, modv, g_mix):
    def fn(t, v):
        (xt,), (mv, g) = t, v
        return [_modulated_norm(xt, g, mv[0:1], mv[1:2])], []
    return _rowwise(fn, "pre1", [(x, D, 0)], [modv, g_mix], [(D, bf16)])[0]


def _post1(ya, yb, proj, w_o, x, modv, g_ffn):
    def merge(t):
        ya_t, yb_t, ga, gb = t
        return _sigmoid(ga) * ya_t + _sigmoid(gb) * yb_t

    def fn(t, v):
        (mt, xt), (mv, g) = t, v
        x1 = xt + mv[2:3] * mt
        return [mt, x1, _modulated_norm(x1, g, mv[3:4], mv[4:5])], []
    return _matmul_rows("nn", (merge, [ya, yb, (proj, D, C_GA // D), (proj, D, C_GB // D)]), w_o, x, fn, [x],
                        [modv, g_ffn], [(D, f32), (D, f32), (D, bf16)], [], name="post1")


def _ffn_in(h, w_stack):
    def body(h_ref, w_ref, act_ref, au_ref):
        p = jnp.dot(h_ref[...], w_ref[0], preferred_element_type=f32)
        a, u = p[:, :FF_PAD], p[:, FF_PAD:]
        act_ref[...] = (a * _sigmoid(a) * u).astype(act_ref.dtype)
        au_ref[...] = p.astype(au_ref.dtype)

    return pl.pallas_call(
        body, grid=(N_DEV,),
        in_specs=[pl.BlockSpec((SEQ, D), lambda j: (0, 0)), pl.BlockSpec((1, D, 2 * FF_PAD), lambda j: (j, 0, 0))],
        out_specs=[pl.BlockSpec((SEQ, FF_PAD), lambda j: (0, j)), pl.BlockSpec((SEQ, 2 * FF_PAD), lambda j: (0, j))],
        out_shape=(SDS((SEQ, FF_HID), bf16), SDS((SEQ, 2 * FF_HID), bf16)), name="ffn_in",
        compiler_params=_params(("parallel",)),
    )(h, w_stack)


def _ffn_bwd_in(dff, w_down_stack, au):
    def body(d_ref, w_ref, au_ref, o_ref):
        dact = _nt(d_ref[...], w_ref[0])
        p = au_ref[...].astype(f32)
        a, u = p[:, :FF_PAD], p[:, FF_PAD:]
        sg = _sigmoid(a)
        o_ref[...] = jnp.concatenate([dact * u * (sg * (1.0 + a * (1.0 - sg))), dact * (a * sg)],
                                     axis=1).astype(o_ref.dtype)

    return pl.pallas_call(
        body, grid=(N_DEV,),
        in_specs=[pl.BlockSpec((SEQ, D), lambda j: (0, 0)), pl.BlockSpec((1, FF_PAD, D), lambda j: (j, 0, 0)),
                  pl.BlockSpec((SEQ, 2 * FF_PAD), lambda j: (0, j))],
        out_specs=pl.BlockSpec((SEQ, 2 * FF_PAD), lambda j: (0, j)),
        out_shape=SDS((SEQ, 2 * FF_HID), bf16), name="ffn_bwd_in", compiler_params=_params(("parallel",)),
    )(dff, w_down_stack, au)


def _final(act, w_down, x1, target, modv, g_final):
    def fn(t, v):
        (fft, x1t, tgt), (mv, g) = t, v
        x2 = x1t + mv[5:6] * fft
        r, xn = _rms(x2)
        err = xn * g - tgt
        dy = err * (1.0 / D)
        dx2 = _rms_bwd(r, xn, dy * g)
        return [dx2, dx2 * mv[5:6]], [_colsum(dy * xn), _colsum(dx2 * fft), _colsum(err * err) * (0.5 / D)]
    return _matmul_rows("nn", act, w_down, x1, fn, [x1, target], [modv, g_final], [(D, f32), (D, bf16)], [D, D, D],
                        name="final")


def _mid_bwd(dau, w_stack, after, x1, dx2, mix, modv, g_ffn):
    def fn(t, v):
        (dh, x1t, dx2t, mt), (mv, g) = t, v
        r, xn = _rms(x1t)
        dn = dh * (1.0 + mv[4:5])
        dx1 = dx2t + _rms_bwd(r, xn, dn * g)
        return [dx1, dx1 * mv[2:3]], [_colsum(dh), _colsum(dh * (xn * g)), _colsum(dn * xn), _colsum(dx1 * mt)]
    return _matmul_rows("nt_stack", dau, w_stack, after, fn, [x1, dx2, mix], [modv, g_ffn], [(D, f32), (D, bf16)],
                        [D, D, D, D], name="mid_bwd")


def _first_bwd(dproj, w_stack, after, x, dx1, modv, g_mix):
    def fn(t, v):
        (dh, xt, dx1t), (mv, g) = t, v
        r, xn = _rms(xt)
        dn = dh * (1.0 + mv[1:2])
        return [dx1t + _rms_bwd(r, xn, dn * g)], [_colsum(dh), _colsum(dh * (xn * g)), _colsum(dn * xn)]
    return _matmul_rows("nt_stack", dproj, w_stack, after, fn, [x, dx1], [modv, g_mix], [(D, f32)], [D, D, D],
                        name="first_bwd")


def _merge_bwd(dmix, w_o, after, ya, yb, proj):
    def fn(t, v):
        dm, ya_t, yb_t, ga, gb = t
        sa, sb = _sigmoid(ga), _sigmoid(gb)
        return [dm * sa, dm * sb, dm * ya_t * (sa * (1.0 - sa)), dm * yb_t * (sb * (1.0 - sb))], []
    return _matmul_rows("nt", dmix, w_o, after, fn, [ya, yb, (proj, D, C_GA // D), (proj, D, C_GB // D)], [],
                        [(D, bf16), (D, bf16), (D, bf16), (D, bf16)], [], name="merge_bwd")


def _rope_tables():
    half = ROT_DIM // 2
    pos = np.arange(SEQ, dtype=np.float32)
    inv_freq = np.float32(ROPE_THETA) ** (-np.arange(0, ROT_DIM, 2, dtype=np.float32) / np.float32(ROT_DIM))
    ang = pos[:, None] * inv_freq[None, :].astype(np.float32)
    cos, sin = np.cos(ang).astype(np.float32), np.sin(ang).astype(np.float32)
    pad = np.zeros((SEQ, HEAD_DIM - ROT_DIM), np.float32)
    zero = np.zeros((SEQ, half), np.float32)
    c_head = np.concatenate([cos, cos, pad + 1.0], axis=1)
    lo_head = np.concatenate([-sin, zero, pad], axis=1)
    hi_head = np.concatenate([zero, sin, pad], axis=1)
    return tuple(jnp.asarray(np.concatenate([t, t], axis=1)) for t in (c_head, lo_head, hi_head))


def _over_heads(tables):
    return [jnp.tile(t, (1, DIL_W // LANES)) for t in tables]


def _rope_fwd(proj, tables):
    half = ROT_DIM // 2

    def fn(t, v):
        q, k = t[:2]
        c, lo, hi = _over_heads(t[2:])
        rot = lambda z: z * c + pltpu.roll(z, DIL_W - half, 1) * lo + pltpu.roll(z, half, 1) * hi
        return [rot(q) * ATT_SCALE, rot(k)], []
    return _rowwise(fn, "rope_fwd", [(proj, DIL_W, C_QB // DIL_W), (proj, DIL_W, C_KB // DIL_W)]
                    + [(tb, LANES, 0) for tb in tables], [], [(DIL_W, f32)] * 2)


def _rope_bwd(dqs, dks, tables):
    half = ROT_DIM // 2

    def fn(t, v):
        dq_t, dk_t = jnp.concatenate(t[:N_GROUPS], axis=1), jnp.concatenate(t[N_GROUPS:2 * N_GROUPS], axis=1)
        c, lo, hi = _over_heads(t[2 * N_GROUPS:])
        rot_t = lambda z: z * c + pltpu.roll(z * lo, half, 1) + pltpu.roll(z * hi, DIL_W - half, 1)
        return [rot_t(dq_t), rot_t(dk_t)], []
    return _rowwise(fn, "rope_bwd", [(a, DIL_OUT_W, 0) for a in (*dqs, *dks)] + [(tb, LANES, 0) for tb in tables],
                    [], [(DIL_W, bf16), (DIL_W, bf16)])


def _head_bcast_sum(d):
    lane = lax.broadcasted_iota(jnp.int32, d.shape, 1)
    out = jnp.zeros_like(d)
    for h in range(d.shape[1] // HEAD_DIM):
        sel = (lane >= h * HEAD_DIM) & (lane < (h + 1) * HEAD_DIM)
        out = jnp.where(sel, jnp.sum(jnp.where(sel, d, 0.0), axis=1, keepdims=True), out)
    return out


def _dil_combine(outs, lses):
    def fn(t, v):
        o0, o1, o2, l0, l1, l2 = t
        m = jnp.maximum(jnp.maximum(l0, l1), l2)
        w0, w1, w2 = jnp.exp(l0 - m), jnp.exp(l1 - m), jnp.exp(l2 - m)
        tot = w0 + w1 + w2
        return [(w0 * o0 + w1 * o1 + w2 * o2) / tot, m + jnp.log(tot)], []
    w = DIL_OUT_W
    return _rowwise(fn, "dil_combine", [(t, w, 0) for t in (*outs, *lses)], [], [(w, f32), (w, f32)])


def _dil_delta(dyb_h, yb_h):
    def fn(t, v):
        return [_head_bcast_sum(t[0] * t[1])], []
    return _rowwise(fn, "dil_delta", [(dyb_h, DIL_OUT_W, 0), (yb_h, DIL_OUT_W, 0)], [], [(DIL_OUT_W, f32)])[0]


def _adamw_math(wt, gt, mt, vt):
    mn = ADAM_B1 * mt + (1.0 - ADAM_B1) * gt
    vn = ADAM_B2 * vt + (1.0 - ADAM_B2) * (gt * gt)
    m_hat = mn / (1.0 - ADAM_B1 ** ADAM_STEP)
    v_hat = vn / (1.0 - ADAM_B2 ** ADAM_STEP)
    return -ADAM_LR * (m_hat / (jnp.sqrt(v_hat) + ADAM_EPS) + ADAM_WD * wt), mn, vn


def _adamw(w, g, m, v, name):
    shape = w.shape
    if w.ndim == 1:
        w, g, m, v = (t.reshape(1, -1) for t in (w, g, m, v))
    rows, cols = w.shape
    if rows % (4 * SUBLANES) == 0 and rows * cols >= BIG_UPDATE:
        return _adamw_by_planes(w, g, m, v, name)

    def fn(t, _):
        return list(_adamw_math(*t)), []
    delta, mn, vn = _rowwise(fn, name, [(w, cols, 0), (g, cols, 0), (m, cols, 0), (v, cols, 0)], [],
                             [(cols, f32)] * 3, tile=rows)
    return delta.reshape(shape), mn.reshape(shape), vn.reshape(shape)


def _adamw_by_planes(w, g, m, v, name, chunks=4):
    planes = w.shape[0]
    bounds = [planes * k // chunks for k in range(chunks + 1)]
    spans = list(zip(bounds[:-1], bounds[1:]))

    def body(w_ref, g_ref, m_ref, v_ref, d_ref, mn_ref, vn_ref, *scratch):
        load_sems, store_sems = scratch[-2:]
        loads, stores, bufs = [], [], []
        for k, (lo, hi) in enumerate(spans):
            rows = pl.ds(lo, hi - lo)
            wb, gb, mb, vb = bufs_k = scratch[4 * k:4 * k + 4]
            bufs.append(bufs_k)
            loads.append([pltpu.make_async_copy(src.at[rows], buf, load_sems.at[4 * k + j])
                          for j, (src, buf) in enumerate(zip((w_ref, g_ref, m_ref, v_ref), bufs_k))])
            stores.append([pltpu.make_async_copy(buf, dst.at[rows], store_sems.at[3 * k + j])
                           for j, (buf, dst) in enumerate(zip((wb, mb, vb), (d_ref, mn_ref, vn_ref)))])
        for cp in sum(loads, []):
            cp.start()
        for k in range(chunks):
            for cp in loads[k]:
                cp.wait()
            wb, gb, mb, vb = bufs[k]
            wb[...], mb[...], vb[...] = _adamw_math(wb[...], gb[...], mb[...], vb[...])
            for cp in stores[k]:
                cp.start()
        for cp in sum(stores, []):
            cp.wait()

    hbm = pl.BlockSpec(memory_space=pl.ANY)
    scratch = [pltpu.VMEM((hi - lo,) + w.shape[1:], f32) for lo, hi in spans for _ in range(4)]
    scratch += [pltpu.SemaphoreType.DMA((4 * chunks,)), pltpu.SemaphoreType.DMA((3 * chunks,))]
    return pl.pallas_call(body, in_specs=[hbm] * 4, out_specs=[hbm] * 3, out_shape=[SDS(w.shape, f32)] * 3,
                          scratch_shapes=scratch, name=name, compiler_params=_params())(w, g, m, v)


def _ada_fwd(c_all, w_shard, b_shard):
    def body(c_ref, w_ref, b_ref, o_ref):
        cv = c_ref[...]
        sc = (cv * _sigmoid(cv)).astype(bf16)
        o_ref[...] = jnp.dot(sc, w_ref[...].astype(bf16), preferred_element_type=f32) + b_ref[...]
    return pl.pallas_call(body, out_shape=SDS((N_DEV, w_shard.shape[1]), f32), name="ada_fwd",
                          compiler_params=_params())(c_all, w_shard, b_shard)


def _ada_bwd(c_all, dmod_cols):
    def body(c_ref, d_ref, o_ref):
        cv = c_ref[...]
        sc = cv * _sigmoid(cv)
        o_ref[...] = lax.dot_general(sc, d_ref[...], (((0,), (0,)), ((), ())), precision=lax.Precision.HIGHEST,
                                     preferred_element_type=f32)
    return pl.pallas_call(body, out_shape=SDS((D, dmod_cols.shape[1]), f32), name="ada_bwd",
                          compiler_params=_params())(c_all, dmod_cols)


def _small_reduce(gathered, after):
    def body(g_ref, after_ref, o_ref, loss_ref):
        acc = g_ref[0]
        for d in range(1, N_DEV):
            acc = acc + g_ref[d]
        o_ref[...] = acc
        loss_ref[...] = jnp.zeros((1, LANES), f32) + jnp.sum(acc[10:11, :])
    return pl.pallas_call(body, out_shape=(SDS((SMALL_ROWS, D), f32), SDS((1, LANES), f32)), name="small_reduce",
                          in_specs=[pl.BlockSpec(memory_space=pltpu.VMEM), pl.BlockSpec(memory_space=pl.ANY)],
                          compiler_params=_params())(gathered, after)


FOX_BLK = 512
CUM_BLK = 128


def _fold_lanes(t, op):
    out = t[:, :LANES]
    for j in range(1, t.shape[1] // LANES):
        out = op(out, t[:, j * LANES:(j + 1) * LANES])
    return out


def _fox_gate_fwd(proj, b_pad):
    nblk = SEQ // CUM_BLK

    def body(f_ref, b_ref, col_ref):
        r = lax.broadcasted_iota(jnp.int32, (CUM_BLK, CUM_BLK), 0)
        c = lax.broadcasted_iota(jnp.int32, (CUM_BLK, CUM_BLK), 1)
        tri = (r >= c).astype(f32)
        carry = jnp.zeros((1, LANES), f32)
        for blk in range(nblk):
            z = f_ref[blk * CUM_BLK:(blk + 1) * CUM_BLK, :] + b_ref[...]
            logf = jnp.minimum(z, 0.0) - jnp.log1p(jnp.exp(-jnp.abs(z)))
            cs = jnp.dot(tri, logf, precision=lax.Precision.HIGHEST, preferred_element_type=f32) + carry
            col_ref[blk * CUM_BLK:(blk + 1) * CUM_BLK, :] = cs
            carry = cs[CUM_BLK - 1:CUM_BLK, :]

    return pl.pallas_call(
        body, grid=(1,), in_specs=[pl.BlockSpec((SEQ, LANES), lambda i: (0, C_F // LANES)),
                                   pl.BlockSpec((1, LANES), lambda i: (0, 0))],
        out_specs=pl.BlockSpec((SEQ, LANES), lambda i: (0, 0)),
        out_shape=SDS((SEQ, LANES), f32), name="fox_gate_fwd",
        compiler_params=_params(("arbitrary",)),
    )(proj, b_pad)


def _fox_gate_bwd(dF_row, proj, b_pad):
    nblk = SEQ // CUM_BLK

    def body(d_ref, f_ref, b_ref, df_ref, db_ref, col_ref):
        r = lax.broadcasted_iota(jnp.int32, (CUM_BLK, CUM_BLK), 0)
        c = lax.broadcasted_iota(jnp.int32, (CUM_BLK, CUM_BLK), 1)
        tri = (r <= c).astype(f32)
        lane = lax.broadcasted_iota(jnp.int32, (CUM_BLK, LANES), 1)
        col_ref[...] = d_ref[...].T
        carry = jnp.zeros((1, LANES), f32)
        total = jnp.zeros((1, LANES), f32)
        for blk in reversed(range(nblk)):
            rows = slice(blk * CUM_BLK, (blk + 1) * CUM_BLK)
            cs = jnp.dot(tri, col_ref[rows, :], precision=lax.Precision.HIGHEST, preferred_element_type=f32) + carry
            carry = cs[0:1, :]
            z = f_ref[rows, :] + b_ref[...]
            df = jnp.where(lane < N_FOX_HEADS, cs * _sigmoid(-z), 0.0)
            df_ref[rows, :] = df.astype(df_ref.dtype)
            total = total + _colsum(df)
        db_ref[...] = total

    return pl.pallas_call(
        body, grid=(1,), in_specs=[pl.BlockSpec((LANES, SEQ), lambda i: (0, 0)),
                                   pl.BlockSpec((SEQ, LANES), lambda i: (0, C_F // LANES)),
                                   pl.BlockSpec((1, LANES), lambda i: (0, 0))],
        out_specs=[pl.BlockSpec((SEQ, LANES), lambda i: (0, 0)), pl.BlockSpec((1, LANES), lambda i: (0, 0))],
        out_shape=(SDS((SEQ, LANES), bf16), SDS((1, LANES), f32)), name="fox_gate_bwd",
        scratch_shapes=[pltpu.VMEM((SEQ, LANES), f32)],
        compiler_params=_params(("arbitrary",)),
    )(dF_row, proj, b_pad)


def _nt(a, b):
    return lax.dot_general(a, b, (((1,), (1,)), ((), ())), preferred_element_type=f32)


def _tn(a, b):
    return lax.dot_general(a, b, (((0,), (0,)), ((), ())), preferred_element_type=f32)


def _fox_prep(proj, f_col):
    def fn(t, v):
        q, k, vv, fc = t
        lane = lax.broadcasted_iota(jnp.int32, (q.shape[0], LANES), 1)
        qs, ks = [], []
        for h in range(N_FOX_HEADS):
            pair, pos = divmod(h, 2)
            own = (lane >= pos * HEAD_DIM) & (lane < (pos + 1) * HEAD_DIM)
            base = (1 - pos) * HEAD_DIM
            f = fc[:, h:h + 1]
            hi = f.astype(bf16).astype(f32)
            mid = (f - hi).astype(bf16).astype(f32)
            lo = (f - hi) - mid
            one = jnp.ones_like(f)
            qa = jnp.where(own, q[:, pair * LANES:(pair + 1) * LANES] * ATT_SCALE, 0.0)
            ka = k[:, pair * LANES:(pair + 1) * LANES]
            for idx, (qv, kv) in enumerate([(hi, one), (mid, one), (lo, one), (one, -hi), (one, -mid), (one, -lo)]):
                sel = lane == base + idx
                qa = jnp.where(sel, qv, qa)
                ka = jnp.where(sel, kv, ka)
            qs.append(qa)
            ks.append(ka)
        return [jnp.concatenate(qs, axis=1), jnp.concatenate(ks, axis=1), vv], []
    w = N_FOX_HEADS * LANES
    return _rowwise(fn, "fox_prep", [(proj, FOX_W, C_QA // FOX_W), (proj, FOX_W, C_KA // FOX_W),
                                     (proj, FOX_W, C_VA // FOX_W), (f_col, LANES, 0)], [],
                    [(w, bf16), (w, bf16), (FOX_W, bf16)])


def _fox_fwd(q_aug, k_aug, v):
    blk = FOX_BLK
    npair = FOX_W // LANES

    def body(q_ref, k_ref, v_ref, o_ref, max_ref, sum_ref, s_scr):
        i = pl.program_id(1)
        tri = lax.broadcasted_iota(jnp.int32, (blk, blk), 0) >= lax.broadcasted_iota(jnp.int32, (blk, blk), 1)
        qh = [q_ref[:, h * LANES:(h + 1) * LANES] for h in range(2)]

        def logits(c, masked):
            off = pl.multiple_of(c * blk, blk)
            tops = []
            for h in range(2):
                s = _nt(qh[h], k_ref[pl.ds(off, blk), h * LANES:(h + 1) * LANES])
                if masked:
                    s = jnp.where(tri, s, NEG)
                s_scr[h, :, pl.ds(off, blk)] = s
                tops.append(_fold_lanes(s, jnp.maximum))
            return tops

        def pass_a(c, m):
            return tuple(jnp.maximum(a, b) for a, b in zip(m, logits(c, False)))

        m = lax.fori_loop(0, i, pass_a, tuple(jnp.full((blk, LANES), NEG, f32) for _ in range(2)))
        mx = [jnp.max(jnp.maximum(a, b), axis=1, keepdims=True) for a, b in zip(m, logits(i, True))]

        def pass_b(c, carry):
            off = pl.multiple_of(c * blk, blk)
            vv = v_ref[pl.ds(off, blk), :]
            new = []
            for h in range(2):
                l, acc = carry[h]
                p = jnp.exp(s_scr[h, :, pl.ds(off, blk)] - mx[h]).astype(bf16)
                new.append((l + _fold_lanes(p.astype(f32), jnp.add), acc + jnp.dot(p, vv, preferred_element_type=f32)))
            return tuple(new)

        zero = jnp.zeros((blk, LANES), f32)
        (l_a, acc_a), (l_b, acc_b) = lax.fori_loop(0, i + 1, pass_b, ((zero, zero), (zero, zero)))
        l_a = jnp.sum(l_a, axis=1, keepdims=True)
        l_b = jnp.sum(l_b, axis=1, keepdims=True)
        first = lax.broadcasted_iota(jnp.int32, (blk, LANES), 1) < HEAD_DIM
        o_ref[...] = jnp.where(first, acc_a / l_a, acc_b / l_b)
        max_ref[0] = jnp.where(first, mx[0], mx[1])
        sum_ref[0] = jnp.where(first, l_a, l_b)

    return pl.pallas_call(
        body, grid=(npair, SEQ // blk),
        in_specs=[pl.BlockSpec((blk, 2 * LANES), lambda p, i: (i, p)),
                  pl.BlockSpec((SEQ, 2 * LANES), lambda p, i: (0, p)),
                  pl.BlockSpec((SEQ, LANES), lambda p, i: (0, p))],
        out_specs=[pl.BlockSpec((blk, LANES), lambda p, i: (i, p))]
        + [pl.BlockSpec((1, blk, LANES), lambda p, i: (p, i, 0))] * 2,
        out_shape=(SDS((SEQ, FOX_W), f32),) + (SDS((npair, SEQ, LANES), f32),) * 2, name="fox_fwd",
        scratch_shapes=[pltpu.VMEM((2, blk, SEQ), f32)],
        compiler_params=_params(("parallel", "arbitrary")),
    )(q_aug, k_aug, v)


def _fox_bwd(q_aug, k_aug, v, do, o, row_max, row_sum, after):
    blk = FOX_BLK
    npair = FOX_W // LANES
    nblk = SEQ // blk

    def body(q_ref, k_ref, v_ref, do_ref, o_ref, max_ref, sum_ref, after_ref, dq_ref, dk_ref, dv_ref, df_ref, dq_acc,
             delta_ref, inv_ref):
        inv_ref[...] = 1.0 / sum_ref[0]
        lane_s = lax.broadcasted_iota(jnp.int32, (SEQ, LANES), 1)
        prod = do_ref[...].astype(bf16).astype(f32) * o_ref[...]
        d_a = jnp.sum(jnp.where(lane_s < HEAD_DIM, prod, 0.0), axis=1, keepdims=True)
        d_b = jnp.sum(jnp.where(lane_s >= HEAD_DIM, prod, 0.0), axis=1, keepdims=True)
        delta_ref[...] = jnp.where(lane_s < HEAD_DIM, d_a, d_b)
        dq_acc[...] = jnp.zeros_like(dq_acc)
        df_ref[...] = jnp.zeros_like(df_ref)
        lane = lax.broadcasted_iota(jnp.int32, (blk, LANES), 1)
        own = [lane < HEAD_DIM, lane >= HEAD_DIM]
        tri = lax.broadcasted_iota(jnp.int32, (blk, blk), 0) >= lax.broadcasted_iota(jnp.int32, (blk, blk), 1)

        def q_slab(qoff, h):
            return q_ref[pl.ds(qoff, blk), h * LANES:(h + 1) * LANES]

        def probs(qoff, h, k_h, masked):
            s = _nt(q_slab(qoff, h), k_h)
            if masked:
                s = jnp.where(tri, s, NEG)
            col = slice(h * HEAD_DIM, h * HEAD_DIM + 1)
            weights = jnp.exp(s - max_ref[0, pl.ds(qoff, blk), col]).astype(bf16).astype(f32)
            return weights * inv_ref[pl.ds(qoff, blk), col]

        def k_slabs(koff):
            return [k_ref[pl.ds(koff, blk), h * LANES:(h + 1) * LANES] for h in range(2)]

        def kv_step(kj, _):
            koff = pl.multiple_of(kj * blk, blk)
            k_aug = k_slabs(koff)
            k_own = [jnp.where(own[h], k_aug[h], jnp.zeros_like(k_aug[h])) for h in range(2)]
            vv = v_ref[pl.ds(koff, blk), :]
            v_own = [jnp.where(own[h], vv, jnp.zeros_like(vv)) for h in range(2)]

            def q_tile(qi, carry, masked):
                qoff = pl.multiple_of(qi * blk, blk)
                dd = do_ref[pl.ds(qoff, blk), :].astype(bf16)
                new, dq_add = [], None
                for h in range(2):
                    dk_h, dv_h, dcol = carry[h]
                    p = probs(qoff, h, k_aug[h], masked)
                    dl = p * (_nt(dd, v_own[h]) - delta_ref[pl.ds(qoff, blk), h * HEAD_DIM:h * HEAD_DIM + 1])
                    dlb = dl.astype(bf16)
                    part = jnp.dot(dlb, k_own[h], preferred_element_type=f32)
                    dq_add = part if dq_add is None else dq_add + part
                    new.append((dk_h + _tn(dlb, q_slab(qoff, h)), dv_h + _tn(p.astype(bf16), dd),
                                dcol + _colsum(dl)))
                dq_acc[pl.ds(qoff, blk), :] += dq_add * ATT_SCALE
                return tuple(new)

            zero = (jnp.zeros((blk, LANES), f32), jnp.zeros((blk, LANES), f32), jnp.zeros((1, blk), f32))
            carry = q_tile(kj, (zero, zero), True)
            (dk_a, dv_a, dcol_a), (dk_b, dv_b, dcol_b) = lax.fori_loop(
                kj + 1, nblk, lambda qi, cr: q_tile(qi, cr, False), carry)
            dk_ref[pl.ds(koff, blk), :] = jnp.where(own[0], dk_a, dk_b).astype(dk_ref.dtype)
            dv_ref[pl.ds(koff, blk), :] = jnp.where(own[0], dv_a, dv_b).astype(dv_ref.dtype)
            df_ref[0, 0:1, pl.ds(koff, blk)] = -dcol_a
            df_ref[0, 1:2, pl.ds(koff, blk)] = -dcol_b
            return 0

        lax.fori_loop(0, nblk, kv_step, 0)
        dq_ref[...] = dq_acc[...].astype(dq_ref.dtype)

    pair_aug = pl.BlockSpec((SEQ, 2 * LANES), lambda p: (0, p))
    slab = pl.BlockSpec((SEQ, LANES), lambda p: (0, p))
    per_pair = pl.BlockSpec((1, SEQ, LANES), lambda p: (p, 0, 0))
    rows = pl.BlockSpec((1, 8, SEQ), lambda p: (p, 0, 0))
    return pl.pallas_call(
        body, grid=(npair,),
        in_specs=[pair_aug, pair_aug, slab, slab, slab, per_pair, per_pair, pl.BlockSpec(memory_space=pl.ANY)],
        out_specs=[slab, slab, slab, rows],
        out_shape=(SDS((SEQ, FOX_W), bf16),) * 3 + (SDS((npair, 8, SEQ), f32),), name="fox_bwd",
        scratch_shapes=[pltpu.VMEM((SEQ, LANES), f32)] * 3,
        compiler_params=_params(("parallel",)),
    )(q_aug, k_aug, v, do, o, row_max, row_sum, after)


DIL_BLK = 128
DILATIONS = (1, 4, 16)
N_GROUPS = len(DILATIONS)
DIL_PAIRS = DIL_OUT_W // LANES


def _dil_blocks(d):
    r1 = lax.broadcasted_iota(jnp.int32, (2 * DIL_BLK, DIL_BLK), 0) & (DIL_BLK - 1)
    c1 = lax.broadcasted_iota(jnp.int32, (2 * DIL_BLK, DIL_BLK), 1)
    r2 = lax.broadcasted_iota(jnp.int32, (2 * DIL_BLK, 2 * DIL_BLK), 0) & (DIL_BLK - 1)
    c2 = lax.broadcasted_iota(jnp.int32, (2 * DIL_BLK, 2 * DIL_BLK), 1)
    band = ((c2 < DIL_BLK) & (c2 >= r2)) | ((c2 >= DIL_BLK) & (c2 - DIL_BLK <= r2))
    out = []
    for r in range(d):
        for b in range(SEQ // d // DIL_BLK):
            rows = pl.ds(r + d * DIL_BLK * b, DIL_BLK, stride=d)
            if b == 0:
                out.append((rows, rows, r1 >= c1))
            else:
                out.append((rows, pl.ds(r + d * DIL_BLK * (b - 1), 2 * DIL_BLK, stride=d), band))
    return out


def _dil_v_spec(g):
    return pl.BlockSpec((SEQ, LANES), lambda p: (0, C_VB // LANES + DIL_PAIRS * g + p))


def _stack_heads(t, first):
    zero = jnp.zeros_like(t)
    return jnp.concatenate([jnp.where(first, t, zero), jnp.where(first, zero, t)], axis=0)


def _dil_fwd(q, k, v, g):
    def body(q_ref, k_ref, v_ref, o_ref, lse_ref):
        first = lax.broadcasted_iota(jnp.int32, (DIL_BLK, LANES), 1) < HEAD_DIM
        for rows, krows, mask in _dil_blocks(DILATIONS[g]):
            qv, kk, vv = q_ref[rows, :].astype(bf16), k_ref[krows, :].astype(bf16), v_ref[krows, :].astype(bf16)
            s = jnp.where(mask, _nt(_stack_heads(qv, first), kk), NEG)
            m = jnp.max(s, axis=1, keepdims=True)
            p = jnp.exp(s - m)
            l = jnp.sum(p, axis=1, keepdims=True)
            out = jnp.dot(p.astype(bf16), vv, preferred_element_type=f32) / l
            lse = m + jnp.log(l)
            o_ref[rows, :] = jnp.where(first, out[:DIL_BLK], out[DIL_BLK:])
            lse_ref[rows, :] = jnp.where(first, lse[:DIL_BLK], lse[DIL_BLK:])

    grouped = pl.BlockSpec((SEQ, LANES), lambda p: (0, DIL_PAIRS * g + p))
    own = pl.BlockSpec((SEQ, LANES), lambda p: (0, p))
    shape = SDS((SEQ, DIL_OUT_W), f32)
    return pl.pallas_call(
        body, grid=(DIL_PAIRS,), in_specs=[grouped, grouped, _dil_v_spec(g)], out_specs=[own] * 2,
        out_shape=(shape, shape),
        name=f"dil_fwd_{DILATIONS[g]}", compiler_params=_params(("parallel",)),
    )(q, k, v)


def _dil_bwd(q, k, v, do, lse, delta, g):
    def body(q_ref, k_ref, v_ref, do_ref, lse_ref, dl_ref, dq_ref, dk_ref, dv_ref):
        first = lax.broadcasted_iota(jnp.int32, (DIL_BLK, LANES), 1) < HEAD_DIM
        dk_ref[...] = jnp.zeros_like(dk_ref)
        dv_ref[...] = jnp.zeros_like(dv_ref)
        for rows, krows, mask in _dil_blocks(DILATIONS[g]):
            qv, kk, vv = q_ref[rows, :].astype(bf16), k_ref[krows, :].astype(bf16), v_ref[krows, :].astype(bf16)
            lsev, delv = lse_ref[rows, :], dl_ref[rows, :]
            q2 = _stack_heads(qv, first)
            do2 = _stack_heads(do_ref[rows, :].astype(bf16), first)
            per_head = lambda t: jnp.concatenate([t[:, 0:1], t[:, HEAD_DIM:HEAD_DIM + 1]], axis=0)
            p = jnp.exp(jnp.where(mask, _nt(q2, kk), NEG) - per_head(lsev))
            dl = (p * (_nt(do2, vv) - per_head(delv))).astype(bf16)
            dq = jnp.dot(dl, kk, preferred_element_type=f32)
            dq_ref[rows, :] = jnp.where(first, dq[:DIL_BLK], dq[DIL_BLK:]) * ATT_SCALE
            dk_ref[krows, :] += _tn(dl, q2)
            dv_ref[krows, :] += _tn(p.astype(bf16), do2)

    grouped = pl.BlockSpec((SEQ, LANES), lambda p: (0, DIL_PAIRS * g + p))
    own = pl.BlockSpec((SEQ, LANES), lambda p: (0, p))
    shape = SDS((SEQ, DIL_OUT_W), f32)
    return pl.pallas_call(
        body, grid=(DIL_PAIRS,), in_specs=[grouped, grouped, _dil_v_spec(g)] + [own] * 3, out_specs=[own] * 3,
        out_shape=(shape, shape, shape), name=f"dil_bwd_{DILATIONS[g]}", compiler_params=_params(("parallel",)),
    )(q, k, v, do, lse, delta)


def _position():
    return lax.axis_index("x"), lax.axis_index("y"), lax.axis_index("c")


def _all_gather(block, name, after=None):
    after = [] if after is None else [after]

    def body(x_ref, *refs):
        out_ref, send_sems, recv_sems, local_sem = refs[len(after):]
        x, y, c = _position()
        me, sibling = (x, y, c), (x, y, 1 - c)
        chips = [(1 - x, y), (x, 1 - y), (1 - x, 1 - y)]

        def slot(px, py, pc):
            return out_ref.at[4 * px + 2 * py + pc]

        def copy(k, blk, to, src=None):
            return pltpu.make_async_remote_copy(
                src_ref=slot(*blk) if src is None else src, dst_ref=slot(*blk),
                send_sem=send_sems.at[k], recv_sem=recv_sems.at[k], device_id=to, device_id_type=MESH)

        mine = pltpu.make_async_copy(x_ref, slot(*me), local_sem)
        mine.start()
        first = [copy(0, me, sibling, src=x_ref)]
        first += [copy(1 + j, me, (*chip, c), src=x_ref) for j, chip in enumerate(chips)]
        for cp in first:
            cp.start()
        passed = [copy(4 + j, (*chip, c), sibling) for j, chip in enumerate(chips)]
        for j, chip in enumerate(chips):
            copy(1 + j, (*chip, c), me).wait_recv()
            passed[j].start()
        copy(0, sibling, me).wait_recv()
        for j, chip in enumerate(chips):
            copy(4 + j, (*chip, 1 - c), me).wait_recv()
        for cp in first + passed:
            cp.wait_send()
        mine.wait()

    return pl.pallas_call(
        body, out_shape=SDS((N_DEV,) + block.shape, block.dtype),
        in_specs=[pl.BlockSpec(memory_space=pl.ANY)] * (1 + len(after)), out_specs=pl.BlockSpec(memory_space=pl.ANY),
        scratch_shapes=[pltpu.SemaphoreType.DMA((7,)), pltpu.SemaphoreType.DMA((7,)), pltpu.SemaphoreType.DMA],
        name=name,
    )(block, *after)


HBM_SPEC = pl.BlockSpec(memory_space=pltpu.HBM)
SEM_SPEC = pl.BlockSpec(memory_space=pltpu.SEMAPHORE)
SPLIT_COPY = pltpu.CompilerParams(has_side_effects=pltpu.SideEffectType.DATAFLOW_SIDE_EFFECTING)


def _in_hbm(t):
    return pltpu.with_memory_space_constraint(t, pltpu.HBM)


def _pair_copies(g_refs, land_refs, send_sems, recv_sems):
    x, y, c = _position()
    return [pltpu.make_async_remote_copy(
        src_ref=g.at[2 * k + (1 - c)], dst_ref=land.at[k], send_sem=send_sems.at[4 * a + k],
        recv_sem=recv_sems.at[4 * a + k], device_id=(x, y, 1 - c), device_id_type=MESH)
        for a, (g, land) in enumerate(zip(g_refs, land_refs, strict=True)) for k in range(4)]


def _chip_copies(t_refs, land_refs, send_sems, recv_sems):
    x, y, c = _position()
    chips = [(1 - x, y), (x, 1 - y), (1 - x, 1 - y)]
    return [pltpu.make_async_remote_copy(
        src_ref=t.at[2 * px + py], dst_ref=land.at[j], send_sem=send_sems.at[3 * a + j],
        recv_sem=recv_sems.at[3 * a + j], device_id=(px, py, c), device_id_type=MESH)
        for a, (t, land) in enumerate(zip(t_refs, land_refs, strict=True)) for j, (px, py) in enumerate(chips)]


_ROUNDS = {"pair": (_pair_copies, 4), "chip": (_chip_copies, 3)}


def _exchange_start(kind, ts, name):
    copies, slots = _ROUNDS[kind]
    n = len(ts)
    lands = [_in_hbm(lax.empty((slots,) + t.shape[1:], t.dtype)) for t in ts]

    def body(*refs):
        for cp in copies(refs[:n], refs[n:2 * n], refs[2 * n], refs[2 * n + 1]):
            cp.start()
        refs[-1][...] = jnp.zeros_like(refs[-1])

    sems = pltpu.SemaphoreType.DMA((slots * n,))
    res = pl.pallas_call(
        body, name=name, in_specs=[HBM_SPEC] * (2 * n),
        out_shape=(sems, sems, *[pltpu.HBM(t.shape, t.dtype) for t in (*ts, *lands)], SDS((8, LANES), f32)),
        out_specs=(SEM_SPEC, SEM_SPEC, *[HBM_SPEC] * (2 * n), pl.BlockSpec(memory_space=pltpu.VMEM)),
        input_output_aliases={i: 2 + i for i in range(2 * n)}, compiler_params=SPLIT_COPY,
    )(*[_in_hbm(t) for t in ts], *lands)
    return res[:-1], res[-1]


def _exchange_wait(kind, state, after, name):
    copies, _ = _ROUNDS[kind]
    send_sems, recv_sems, *arrays = state
    n = len(arrays) // 2

    def body(*refs):
        for cp in copies(refs[:n], refs[n:2 * n], refs[2 * n], refs[2 * n + 1]):
            cp.wait_send()
            cp.wait_recv()

    res = pl.pallas_call(
        body, name=name, in_specs=[HBM_SPEC] * (2 * n) + [SEM_SPEC, SEM_SPEC, pl.BlockSpec(memory_space=pl.ANY)],
        out_shape=[pltpu.HBM(t.shape, t.dtype) for t in arrays], out_specs=[HBM_SPEC] * (2 * n),
        input_output_aliases={i: i for i in range(2 * n)}, compiler_params=SPLIT_COPY,
    )(*arrays, send_sems, recv_sems, after)
    return res[:n], res[n:]


def _gather_copies(x_refs, out_refs, send_sems, recv_sems):
    x, y, c = _position()
    peers = [(x, y, 1 - c), (1 - x, y, c), (x, 1 - y, c), (1 - x, 1 - y, c)]
    sends, arrivals = [], []
    for a, (x_ref, out_ref) in enumerate(zip(x_refs, out_refs, strict=True)):
        for k, (px, py, pc) in enumerate(peers):
            sems = dict(send_sem=send_sems.at[4 * a + k], recv_sem=recv_sems.at[4 * a + k],
                        device_id=(px, py, pc), device_id_type=MESH)
            sends.append(pltpu.make_async_remote_copy(src_ref=x_ref, dst_ref=out_ref.at[4 * x + 2 * y + c], **sems))
            arrivals.append(pltpu.make_async_remote_copy(src_ref=x_ref, dst_ref=out_ref.at[4 * px + 2 * py + pc],
                                                         **sems))
    return sends, arrivals


def _gather_start(blocks, after, name):
    n = len(blocks)
    outs = [_in_hbm(lax.empty((N_DEV,) + b.shape, b.dtype)) for b in blocks]

    def body(*refs):
        sends, _ = _gather_copies(refs[:n], refs[n:2 * n], refs[2 * n + 1], refs[2 * n + 2])
        for cp in sends:
            cp.start()
        refs[-1][...] = jnp.zeros_like(refs[-1])

    sems = pltpu.SemaphoreType.DMA((4 * n,))
    res = pl.pallas_call(
        body, name=name, in_specs=[HBM_SPEC] * (2 * n) + [pl.BlockSpec(memory_space=pl.ANY)],
        out_shape=(sems, sems, *[pltpu.HBM(t.shape, t.dtype) for t in (*blocks, *outs)], SDS((8, LANES), f32)),
        out_specs=(SEM_SPEC, SEM_SPEC, *[HBM_SPEC] * (2 * n), pl.BlockSpec(memory_space=pltpu.VMEM)),
        input_output_aliases={i: 2 + i for i in range(2 * n)}, compiler_params=SPLIT_COPY,
    )(*[_in_hbm(b) for b in blocks], *outs, after)
    return res[:-1], res[-1]


def _gather_wait(state, after, name):
    send_sems, recv_sems, *arrays = state
    n = len(arrays) // 2

    def body(*refs):
        sends, arrivals = _gather_copies(refs[:n], refs[n:2 * n], refs[2 * n], refs[2 * n + 1])
        for cp in sends:
            cp.wait_send()
        for cp in arrivals:
            cp.wait_recv()

    res = pl.pallas_call(
        body, name=name, in_specs=[HBM_SPEC] * (2 * n) + [SEM_SPEC, SEM_SPEC, pl.BlockSpec(memory_space=pl.ANY)],
        out_shape=[pltpu.HBM(t.shape, t.dtype) for t in arrays], out_specs=[HBM_SPEC] * (2 * n),
        input_output_aliases={i: i for i in range(2 * n)}, compiler_params=SPLIT_COPY,
    )(*arrays, send_sems, recv_sems, after)
    return res[:n], res[n:]


def _gather_finish(partial, name):
    n = len(partial)

    def body(*refs):
        in_refs, out_refs = refs[:n], refs[n:2 * n]
        send_sems, recv_sems = refs[2 * n:]
        x, y, c = _position()
        chips = [(1 - x, y), (x, 1 - y), (1 - x, 1 - y)]
        copies = []
        for a in range(n):
            for j, (px, py) in enumerate(chips):
                cp = pltpu.make_async_remote_copy(
                    src_ref=in_refs[a].at[4 * px + 2 * py + c], dst_ref=out_refs[a].at[4 * px + 2 * py + c],
                    send_sem=send_sems.at[a, j], recv_sem=recv_sems.at[a, j], device_id=(x, y, 1 - c),
                    device_id_type=MESH)
                cp.start()
                copies.append(cp)
        for a in range(n):
            for j, (px, py) in enumerate(chips):
                pltpu.make_async_remote_copy(
                    src_ref=in_refs[a].at[4 * px + 2 * py + (1 - c)], dst_ref=out_refs[a].at[4 * px + 2 * py + (1 - c)],
                    send_sem=send_sems.at[a, j], recv_sem=recv_sems.at[a, j], device_id=(x, y, 1 - c),
                    device_id_type=MESH).wait_recv()
        for cp in copies:
            cp.wait_send()

    hbm = pl.BlockSpec(memory_space=pl.ANY)
    return pl.pallas_call(
        body, out_shape=[SDS(p.shape, p.dtype) for p in partial], in_specs=[hbm] * n, out_specs=[hbm] * n,
        input_output_aliases={a: a for a in range(n)},
        scratch_shapes=[pltpu.SemaphoreType.DMA((n, 3)), pltpu.SemaphoreType.DMA((n, 3))],
        name=name,
    )(*partial)


def _row_tile(rows):
    return 512 if rows % 512 == 0 and rows > 512 else rows


def _pair_add(g, r1, core, name):
    def body(c_ref, g_ref, r_ref, o_ref, *scratch):
        load_sems, store_sems = scratch[-2:]
        loads, stores = [], []
        for k in range(4):
            gb, rb = scratch[2 * k:2 * k + 2]
            loads.append([pltpu.make_async_copy(g_ref.at[2 * k + c_ref[0]], gb, load_sems.at[2 * k]),
                          pltpu.make_async_copy(r_ref.at[k], rb, load_sems.at[2 * k + 1])])
            stores.append(pltpu.make_async_copy(gb, o_ref.at[k], store_sems.at[k]))
        for cp in sum(loads, []):
            cp.start()
        for k in range(4):
            for cp in loads[k]:
                cp.wait()
            gb, rb = scratch[2 * k:2 * k + 2]
            gb[...] = (gb[...].astype(f32) + rb[...].astype(f32)).astype(gb.dtype)
            stores[k].start()
        for cp in stores:
            cp.wait()

    hbm = pl.BlockSpec(memory_space=pl.ANY)
    return pl.pallas_call(
        body, out_shape=SDS((4,) + g.shape[1:], g.dtype), name=name,
        in_specs=[pl.BlockSpec(memory_space=pltpu.SMEM), hbm, hbm], out_specs=hbm,
        scratch_shapes=[pltpu.VMEM(g.shape[1:], g.dtype)] * 8
        + [pltpu.SemaphoreType.DMA((8,)), pltpu.SemaphoreType.DMA((4,))],
        compiler_params=_params(),
    )(core, g, r1)


def _chip_add(t, r2, chip, name, planes=None, pieces=None, after=None):
    after = [] if after is None else [after]

    def body(c_ref, t_ref, r_ref, *refs):
        o_refs = refs[len(after):]
        s = ((t_ref[0].astype(f32) + r_ref[0].astype(f32)) + r_ref[1].astype(f32)) + r_ref[2].astype(f32)
        if planes:
            o_refs[0][...] = s.T[:planes][:, None, :]
        elif pieces:
            for o_ref, (start, size) in zip(o_refs, pieces, strict=True):
                o_ref[...] = s.T[start:start + size]
        else:
            o_refs[0][...] = s

    rows, cols = t.shape[1:]
    tile = _row_tile(rows)
    if planes:
        out_shape, out_spec = SDS((planes, 1, rows), f32), pl.BlockSpec((planes, 1, tile), lambda i, c_ref: (0, 0, i))
    elif pieces:
        out_shape = [SDS((size, rows), f32) for _, size in pieces]
        out_spec = [pl.BlockSpec((size, tile), lambda i, c_ref: (0, i)) for _, size in pieces]
    else:
        out_shape, out_spec = SDS((rows, cols), f32), pl.BlockSpec((tile, cols), lambda i, c_ref: (i, 0))
    return pl.pallas_call(
        body, out_shape=out_shape, name=name,
        grid_spec=pltpu.PrefetchScalarGridSpec(
            num_scalar_prefetch=1, grid=(rows // tile,),
            in_specs=[pl.BlockSpec((1, tile, cols), lambda i, c_ref: (c_ref[0], i, 0)),
                      pl.BlockSpec((3, tile, cols), lambda i, c_ref: (0, i, 0))]
            + [pl.BlockSpec(memory_space=pl.ANY)] * len(after),
            out_specs=out_spec),
        compiler_params=_params(("arbitrary",)),
    )(chip, t, r2, *after)


def _pad_to(t, axis, size):
    pads = [(0, 0)] * t.ndim
    pads[axis] = (0, size - t.shape[axis])
    return jnp.pad(t, pads)


_REF_COLS = {"qa": (0, FOX_W), "ka": (FOX_W, FOX_W), "va": (2 * FOX_W, FOX_W), "f": (3 * FOX_W, N_FOX_HEADS)}
_REF_COLS.update({n: (3 * FOX_W + N_FOX_HEADS + i * DIL_W, DIL_W) for i, n in enumerate(("qb", "kb", "vb"))})
_REF_COLS.update({n: (3 * FOX_W + N_FOX_HEADS + 3 * DIL_W + i * D, D) for i, n in enumerate(("ga", "gb"))})
_REF_ORDER = ("qa", "ka", "va", "f", "qb", "kb", "vb", "ga", "gb")


def _place_cols(sources, src_of, out_cols, name, row_block=512):
    arrays = [s[0] if isinstance(s, tuple) else s for s in sources]
    widths = [a.shape[-1] for a in arrays]
    rows = arrays[0].shape[-2]
    plan = []
    for t in range(out_cols // LANES):
        segs, c, end = [], t * LANES, (t + 1) * LANES
        while c < end:
            s = src_of(c)
            if s is None:
                c += 1
                continue
            n = 1
            while c + n < end and src_of(c + n) == (s[0], s[1] + n):
                n += 1
            segs.append((s[0], s[1], c - t * LANES, n))
            c += n
        plan.append(segs)

    def body(*refs):
        o_ref = refs[-1]
        for t, segs in enumerate(plan):
            acc = None
            for si, c0, o0, n in segs:
                a0 = c0 // LANES * LANES
                wide = min(2 * LANES, widths[si] - a0)
                win = refs[si][0, :, a0:a0 + wide] if isinstance(sources[si], tuple) else refs[si][:, a0:a0 + wide]
                r = lax.broadcasted_iota(jnp.int32, (wide, LANES), 0)
                c = lax.broadcasted_iota(jnp.int32, (wide, LANES), 1)
                pick = ((r - (c0 - a0) == c - o0) & (c >= o0) & (c < o0 + n)).astype(bf16)
                part = jnp.dot(win.astype(bf16), pick, preferred_element_type=f32)
                acc = part if acc is None else acc + part
            tile = jnp.zeros((row_block, LANES), f32) if acc is None else acc
            o_ref[:, t * LANES:(t + 1) * LANES] = tile.astype(o_ref.dtype)

    def spec(s):
        if isinstance(s, tuple):
            j = s[1]
            return pl.BlockSpec((1, row_block, s[0].shape[-1]), lambda i: (j, i, 0))
        return pl.BlockSpec((row_block, s.shape[-1]), lambda i: (i, 0))

    return pl.pallas_call(
        body, grid=(rows // row_block,), in_specs=[spec(s) for s in sources],
        out_specs=pl.BlockSpec((row_block, out_cols), lambda i: (i, 0)), out_shape=SDS((rows, out_cols), bf16),
        name=name, compiler_params=_params(("parallel",)),
    )(*arrays)


def _ref_piece(r):
    for name in _REF_ORDER:
        lo, width = _REF_COLS[name]
        if lo <= r < lo + width:
            return name, r - lo
    raise ValueError(r)


def _shard_pad_cols(pieces):
    names = [n for n in _REF_ORDER if n != "vb"]
    sources = [pieces[n] for n in names] + list(pieces["vb"])

    def src_of(c):
        j, i = divmod(c, W_IN_PAD)
        if i >= W_IN_SH:
            return None
        name, col = _ref_piece(j * W_IN_SH + i)
        if name == "vb":
            return len(names) + col // DIL_OUT_W, col % DIL_OUT_W
        return names.index(name), col

    return _place_cols(sources, src_of, N_DEV * W_IN_PAD, "place_dproj")


_SLABS = {"ga": C_GA, "gb": C_GB, "qb": C_QB, "kb": C_KB, "vb": C_VB, "qa": C_QA, "ka": C_KA, "va": C_VA, "f": C_F}


def _slab_w_in(stack):
    def src_of(c):
        for name, start in _SLABS.items():
            lo, width = _REF_COLS[name]
            if start <= c < start + width:
                return divmod(lo + c - start, W_IN_SH)
        return None

    return _place_cols([(stack, j) for j in range(N_DEV)], src_of, PROJ_W, "place_w_in")


def kernel(x, c, w_ada, b_ada, g_mix, w_in, b_fgate, w_br_a, w_br_b, w_out, g_ffn, w_ffn_gate, w_ffn_up, w_ffn_down, g_final, loss_target, m_w_ada, m_b_ada, m_g_mix, m_w_in, m_b_fgate, m_w_br_a, m_w_br_b, m_w_out, m_g_ffn, m_w_ffn_gate, m_w_ffn_up, m_w_ffn_down, m_g_final, v_w_ada, v_b_ada, v_g_mix, v_w_in, v_b_fgate, v_w_br_a, v_w_br_b, v_w_out, v_g_ffn, v_w_ffn_gate, v_w_ffn_up, v_w_ffn_down, v_g_final):
    px, py, pc = _position()
    dev = 4 * px + 2 * py + pc
    x2d, tgt = x[0], loss_target[0]

    c_all = _all_gather(c, "gather_c").reshape(N_DEV, D)
    ada_cols = w_ada.shape[2]
    b_shard = lax.dynamic_slice(b_ada, (0, dev * ada_cols), (1, ada_cols))
    mod_shard = _ada_fwd(c_all, w_ada[0], b_shard)
    mod_all = _all_gather(mod_shard, "gather_mod")
    modv = lax.dynamic_index_in_dim(mod_all, dev, axis=1, keepdims=False).reshape(6, D)
    h1 = _pre1(x2d, modv, g_mix)

    w_in_s = _all_gather(_pad_to(w_in[0], 1, W_IN_PAD).astype(bf16), "gather_w_in")
    gate_up = jnp.concatenate([_pad_to(w_ffn_gate[0], 1, FF_PAD), _pad_to(w_ffn_up[0], 1, FF_PAD)], axis=1)
    later = [w_br_a[0], w_br_b[0], w_out[0], gate_up, _pad_to(w_ffn_down[0], 0, FF_PAD)]
    later_state, later_token = _gather_start([t.astype(bf16) for t in later], w_in_s, "gather_rest_start")
    w_in_p = _slab_w_in(w_in_s)

    proj = _matmul(h1, w_in_p, name="mm_proj", tm=SEQ, tn=896, after=later_token)
    b_pad = jnp.pad(b_fgate, ((0, 0), (0, LANES - N_FOX_HEADS)))
    q_aug, k_aug, va = _fox_prep(proj, _fox_gate_fwd(proj, b_pad))
    ya_h, max_a, sum_a = _fox_fwd(q_aug, k_aug, va)

    tables = _rope_tables()
    qb_r, kb_r = _rope_fwd(proj, tables)
    by_group = [_dil_fwd(qb_r, kb_r, proj, grp) for grp in range(N_GROUPS)]
    yb_h, lse_b = _dil_combine([o for o, _ in by_group], [l for _, l in by_group])

    both_done = ya_h[:8, :LANES] + yb_h[:8, :LANES]
    mine, arrived = _gather_wait(later_state, both_done, "gather_rest_wait")
    w_a_s, w_b_s, w_o_s, w_gu_s, w_d_s = [
        lax.dynamic_update_slice(stack, block[None], (dev, 0, 0))
        for stack, block in zip(_gather_finish(arrived, "gather_rest_finish"), mine, strict=True)]
    w_o = w_o_s.reshape(D, D)
    w_d = w_d_s.reshape(FF_HID, D)
    ya = _matmul_stack(ya_h, w_a_s, name="mm_br_a")
    yb = _matmul_stack(yb_h, w_b_s, name="mm_br_b")

    merged, mix, x1, h2 = _post1(ya, yb, proj, w_o, x2d, modv, g_ffn)
    act, au = _ffn_in(h2, w_gu_s)

    dx2, dff, dg_final, dga_f, loss_lanes = _final(act, w_d, x1, tgt, modv, g_final.reshape(1, D))
    dau = _ffn_bwd_in(dff, w_d_s, au)

    core = pc.astype(jnp.int32).reshape(1)
    chip = (2 * px + py).astype(jnp.int32).reshape(1)

    def pair_done(state, after, tags, name):
        mine, theirs = _exchange_wait("pair", state, after, "pair_wait_" + name)
        sums = [_pair_add(g, r, core, "pair_add_" + t) for g, r, t in zip(mine, theirs, tags)]
        return _exchange_start("chip", sums, "chip_start_" + name)

    def from_chips(state, after, tags, name):
        sums, got = _exchange_wait("chip", state, after, "chip_wait_" + name)
        return [_chip_add(p, r, chip, "chip_add_" + t) for p, r, t in zip(sums, got, tags)]

    g_gu = _matmul(h2, dau, ta=True, by_shard=True, out_dtype=bf16, name="mm_g_ffn_in", tm=D, tn=2 * FF_PAD)
    g_d = _matmul(act, dff, ta=True, out_dtype=bf16, name="mm_g_down", tm=FF_HID // 2, tn=512)
    ffn_tags = ["gu", "down"]
    ffn_pair, ffn_pair_token = _exchange_start("pair", [g_gu, g_d.reshape(N_DEV, FF_PAD, D)], "pair_start_ffn")

    dx1, dmix, dsh_f, dsc_f, dg_ffn, dga_m = _mid_bwd(dau, w_gu_s, ffn_pair_token, x1, dx2, mix, modv, g_ffn)
    ffn_state, ffn_token = pair_done(ffn_pair, dx1, ffn_tags, "ffn")
    dya, dyb, dga, dgb = _merge_bwd(dmix, w_o, ffn_token, ya, yb, proj)
    dya_h = _matmul_stack(dya, w_a_s, tb=True, name="mm_d_ya")
    dyb_h = _matmul_stack(dyb, w_b_s, tb=True, name="mm_d_yb")

    g_o = _matmul(merged, dmix, ta=True, out_dtype=bf16, name="mm_g_out", tm=D, tn=512)
    g_a = _matmul_stack(ya_h, dya, ta=True, out_dtype=bf16, name="mm_g_br_a")
    g_b = _matmul_stack(yb_h, dyb, ta=True, out_dtype=bf16, name="mm_g_br_b")
    rows_a, rows_b = FOX_W * W_BR_SH // D, DIL_OUT_W * W_BR_SH // D
    g_small = jnp.concatenate([g_a.reshape(N_DEV, rows_a, D), g_b.reshape(N_DEV, rows_b, D),
                               g_o.reshape(N_DEV, W_BR_SH, D)], axis=1)
    small_pair, small_pair_token = _exchange_start("pair", [g_small], "pair_start_small")

    dqa, dka, dva, dF = _fox_bwd(q_aug, k_aug, va, dya_h, ya_h, max_a, sum_a, small_pair_token)
    dF_row = jnp.pad(dF[:, :2, :].reshape(N_FOX_HEADS, SEQ), ((0, LANES - N_FOX_HEADS), (0, 0)))
    df, db_fgate = _fox_gate_bwd(dF_row, proj, b_pad)
    small_state, small_token = pair_done(small_pair, df, ["small"], "small")

    delta_b = _dil_delta(dyb_h, yb_h)
    dil_grads = [_dil_bwd(qb_r, kb_r, proj, dyb_h, lse_b, delta_b, grp) for grp in range(N_GROUPS)]
    dqb, dkb = _rope_bwd([t[0] for t in dil_grads], [t[1] for t in dil_grads], tables)

    dproj = _shard_pad_cols({"qa": dqa, "ka": dka, "va": dva, "f": df, "qb": dqb, "kb": dkb,
                             "vb": [t[2] for t in dil_grads], "ga": dga, "gb": dgb})
    g_in = _matmul(h1, dproj, ta=True, by_shard=True, out_dtype=bf16, name="mm_g_in", tm=D, tn=W_IN_PAD,
                   after=small_token)
    mix_tags = ["in"]
    mix_pair, mix_pair_token = _exchange_start("pair", [g_in], "pair_start_mixer")

    w = {"w_ada": w_ada, "b_ada": b_ada, "g_mix": g_mix, "w_in": w_in, "b_fgate": b_fgate, "w_br_a": w_br_a,
         "w_br_b": w_br_b, "w_out": w_out, "g_ffn": g_ffn, "w_ffn_gate": w_ffn_gate, "w_ffn_up": w_ffn_up,
         "w_ffn_down": w_ffn_down, "g_final": g_final}
    m = {"w_ada": m_w_ada, "b_ada": m_b_ada, "g_mix": m_g_mix, "w_in": m_w_in, "b_fgate": m_b_fgate,
         "w_br_a": m_w_br_a, "w_br_b": m_w_br_b, "w_out": m_w_out, "g_ffn": m_g_ffn, "w_ffn_gate": m_w_ffn_gate,
         "w_ffn_up": m_w_ffn_up, "w_ffn_down": m_w_ffn_down, "g_final": m_g_final}
    v = {"w_ada": v_w_ada, "b_ada": v_b_ada, "g_mix": v_g_mix, "w_in": v_w_in, "b_fgate": v_b_fgate,
         "w_br_a": v_w_br_a, "w_br_b": v_w_br_b, "w_out": v_w_out, "g_ffn": v_g_ffn, "w_ffn_gate": v_w_ffn_gate,
         "w_ffn_up": v_w_ffn_up, "w_ffn_down": v_w_ffn_down, "g_final": v_g_final}
    names = list(w)
    g, delta, new_m, new_v = {}, {}, {}, {}

    transposed = ("w_ffn_gate", "w_ffn_up")
    by_column = lambda t: jnp.transpose(t, (2, 0, 1))
    by_row = lambda t: jnp.transpose(t, (1, 2, 0))

    def update(n):
        shape = w[n].shape
        if n == "w_in":
            g3 = g[n]
            dl, mn, vn = _adamw_by_planes(by_column(w[n]), g3, by_column(m[n]), by_column(v[n]), "adamw_" + n)
            g[n], delta[n], new_m[n], new_v[n] = by_row(g3), by_row(dl), by_row(mn), by_row(vn)
            return
        if n in transposed:
            g_t = g[n]
            dl, mn, vn = _adamw(w[n][0].T, g_t, m[n][0].T, v[n][0].T, "adamw_" + n)
            g[n], delta[n], new_m[n], new_v[n] = g_t.T[None], dl.T[None], mn.T[None], vn.T[None]
            return
        two_d = (lambda t: t.reshape(shape[-2:])) if len(shape) == 3 else (lambda t: t)
        dl, mn, vn = _adamw(two_d(w[n]), two_d(g[n]), two_d(m[n]), two_d(v[n]), "adamw_" + n)
        delta[n], new_m[n], new_v[n] = dl.reshape(shape), mn.reshape(shape), vn.reshape(shape)

    def update_all(grads):
        g.update(grads)
        for n in grads:
            update(n)
        return sum(delta[n][(0,) * (delta[n].ndim - 1)][:N_FOX_HEADS] for n in grads)

    ffn_sums, ffn_got = _exchange_wait("chip", ffn_state, mix_pair_token, "chip_wait_ffn")
    g_gate_t, g_up_t = _chip_add(ffn_sums[0], ffn_got[0], chip, "chip_add_gu",
                                 pieces=[(0, W_FF_SH), (FF_PAD, W_FF_SH)])
    gate_up_done = update_all({"w_ffn_gate": g_gate_t, "w_ffn_up": g_up_t})
    mix_state, mix_token = pair_done(mix_pair, gate_up_done, mix_tags, "mixer")

    grad_x, dsh_m, dsc_m, dg_mix = _first_bwd(dproj, w_in_s, mix_token, x2d, dx1, modv, g_mix)

    s_d = _chip_add(ffn_sums[1], ffn_got[1], chip, "chip_add_down", after=grad_x)
    s_small, = from_chips(small_state, grad_x, ["small"], "small")
    sharded_done = update_all({
        "w_ffn_down": s_d[None, :W_FF_SH],
        "w_br_a": s_small[:rows_a].reshape(1, FOX_W, W_BR_SH),
        "w_br_b": s_small[rows_a:rows_a + rows_b].reshape(1, DIL_OUT_W, W_BR_SH), "w_out": s_small[None, rows_a + rows_b:],
    })

    pad_lane = lambda t: jnp.pad(t, ((0, 0), (0, D - t.shape[1])))
    small = jnp.concatenate([dsh_m, dsc_m, dga_m, dsh_f, dsc_f, dga_f, dg_mix, dg_ffn, dg_final,
                             pad_lane(db_fgate), loss_lanes, jnp.zeros((SMALL_ROWS - 11, D), f32)], axis=0)
    small_all = _all_gather(small, "gather_small", after=sharded_done)
    small_sum, loss_row = _small_reduce(small_all, mix_token)
    loss = loss_row[0, 0]
    dmod_all = small_all[:, :6, :].reshape(N_DEV, 6 * D)
    g_w_ada = _ada_bwd(c_all, lax.dynamic_slice(dmod_all, (0, dev * ada_cols), (N_DEV, ada_cols)))
    done = update_all({
        "w_ada": g_w_ada[None], "b_ada": small_sum[0:6].reshape(1, 6 * D), "g_mix": small_sum[6:7],
        "b_fgate": small_sum[9:10, :N_FOX_HEADS], "g_ffn": small_sum[7:8], "g_final": small_sum[8],
    })
    mix_sums, mix_got = _exchange_wait("chip", mix_state, done, "chip_wait_mixer")
    g["w_in"] = _chip_add(mix_sums[0], mix_got[0], chip, "chip_add_in", planes=W_IN_SH)
    update("w_in")

    return (loss, grad_x[None], *[g[n] for n in names], *[delta[n] for n in names],
            *[new_m[n] for n in names], *[new_v[n] for n in names])
```
